```python
import jax, jax.numpy as jnp
from jax import lax
import numpy as np

D_MODEL = 1024
BATCH = 8
SEQ = 4096
DEPTH = 2

MLA_HEADS = 8
QK_NOPE_DIM = 64
QK_ROPE_DIM = 32
V_HEAD_DIM = 64
Q_LORA_RANK = 256
KV_LORA_RANK = 128
QK_HEAD_DIM = QK_NOPE_DIM + QK_ROPE_DIM
D_ATTN = MLA_HEADS * V_HEAD_DIM
ROPE_THETA = 10000.0
Q_BLOCK = 128
SSD_HEADS = 8
SSD_HEAD_DIM = 64
SSD_GROUPS = 2
SSD_STATE = 128
CONV_WIDTH = 4
CHUNK = 128
D_SSD = SSD_HEADS * SSD_HEAD_DIM
D_CONV = D_SSD + 2 * SSD_GROUPS * SSD_STATE
D_MIX = D_ATTN + D_SSD
D_IN = Q_LORA_RANK + KV_LORA_RANK + QK_ROPE_DIM + D_SSD + D_CONV + SSD_HEADS
D_FF = ((-(-8 * D_MODEL // 3) + 255) // 256) * 256
N_MOD = 6
EPS = 1e-6

kernel_name = "hymba_mla_ssd_adaln_block"


def rmsnorm(x, w):
    xf = x.astype(jnp.float32)
    y = xf * lax.rsqrt(jnp.mean(xf * xf, axis=-1, keepdims=True) + EPS)
    return (y * w.astype(jnp.float32)).astype(x.dtype)


def rope(x, cos, sin):
    x1, x2 = jnp.split(x, 2, axis=-1)
    return jnp.concatenate([x1 * cos - x2 * sin, x2 * cos + x1 * sin], axis=-1)


def causal_attention(q, k, v):
    b, h, s, d = q.shape
    nb = s // Q_BLOCK
    scale = d ** -0.5
    qb = q.reshape(b, h, nb, Q_BLOCK, d).transpose(2, 0, 1, 3, 4)
    kpos = jnp.arange(s)

    def one_block(args):
        qi, i = args
        sc = jnp.einsum('bhqd,bhkd->bhqk', qi, k).astype(jnp.float32) * scale
        qpos = i * Q_BLOCK + jnp.arange(Q_BLOCK)
        mask = kpos[None, :] <= qpos[:, None]
        sc = jnp.where(mask, sc, jnp.finfo(jnp.float32).min)
        p = jax.nn.softmax(sc, axis=-1)
        return jnp.einsum('bhqk,bhkd->bhqd', p.astype(v.dtype), v)

    out = lax.map(one_block, (qb, jnp.arange(nb)))
    return out.transpose(1, 0, 3, 2, 4).reshape(b, s, h * v.shape[-1])


def ssd_chunked(x, dt, a, bm, cm):
    b, l, h, p = x.shape
    rep = h // bm.shape[2]
    nc = l // CHUNK
    xf = x.astype(jnp.float32)
    xdt = xf * dt[..., None]
    adt = dt * a
    bh = jnp.repeat(bm.astype(jnp.float32), rep, axis=2)
    ch = jnp.repeat(cm.astype(jnp.float32), rep, axis=2)
    n = bh.shape[-1]
    xc = xdt.reshape(b, nc, CHUNK, h, p)
    bc = bh.reshape(b, nc, CHUNK, h, n)
    cc = ch.reshape(b, nc, CHUNK, h, n)
    acs = jnp.cumsum(adt.reshape(b, nc, CHUNK, h), axis=2)
    seg = acs[:, :, :, None, :] - acs[:, :, None, :, :]
    tri = jnp.tril(jnp.ones((CHUNK, CHUNK), dtype=bool))[None, None, :, :, None]
    lmat = jnp.exp(jnp.where(tri, seg, -jnp.inf))
    scores = jnp.einsum('bclhn,bcshn->bclsh', cc, bc) * lmat
    y_diag = jnp.einsum('bclsh,bcshp->bclhp', scores, xc)
    decay_states = jnp.exp(acs[:, :, -1:, :] - acs)
    states = jnp.einsum('bcshn,bcsh,bcshp->bchpn', bc, decay_states, xc)
    chunk_decay = jnp.exp(acs[:, :, -1, :])

    def step(carry, inp):
        s_c, d_c = inp
        new = d_c[:, :, None, None] * carry + s_c
        return new, carry

    init = jnp.zeros((b, h, p, n), jnp.float32)
    _, prev = lax.scan(step, init, (states.transpose(1, 0, 2, 3, 4), chunk_decay.transpose(1, 0, 2)))
    prev = prev.transpose(1, 0, 2, 3, 4)
    y_off = jnp.einsum('bclhn,bchpn,bclh->bclhp', cc, prev, jnp.exp(acs))
    return (y_diag + y_off).reshape(b, l, h, p)


def hybrid_mixer(h, cos, sin, w_in, q_a_norm_w, w_q_up, kv_a_norm_w, w_kv_up,
                 q_nope_norm_w, q_pe_norm_w, k_nope_norm_w, k_pe_norm_w,
                 conv_w, conv_b, dt_bias, a_log, d_skip, ssd_norm_w, w_out):
    b, s, _ = h.shape
    proj = h @ w_in
    cuts = np.cumsum([Q_LORA_RANK, KV_LORA_RANK, QK_ROPE_DIM, D_SSD, D_CONV])
    q_a, kv_a, k_pe, z, xbc, dt_raw = jnp.split(proj, [int(i) for i in cuts], axis=-1)

    q = (rmsnorm(q_a, q_a_norm_w) @ w_q_up).reshape(b, s, MLA_HEADS, QK_HEAD_DIM)
    q_nope, q_pe = jnp.split(q, [QK_NOPE_DIM], axis=-1)
    kv = (rmsnorm(kv_a, kv_a_norm_w) @ w_kv_up).reshape(b, s, MLA_HEADS, QK_NOPE_DIM + V_HEAD_DIM)
    k_nope, v = jnp.split(kv, [QK_NOPE_DIM], axis=-1)
    q_nope = rmsnorm(q_nope, q_nope_norm_w)
    q_pe = rope(rmsnorm(q_pe, q_pe_norm_w), cos, sin)
    k_nope = rmsnorm(k_nope, k_nope_norm_w)
    k_pe = rope(rmsnorm(k_pe[:, :, None, :], k_pe_norm_w), cos, sin)
    q_full = jnp.concatenate([q_nope, q_pe], axis=-1)
    k_full = jnp.concatenate([k_nope, jnp.broadcast_to(k_pe, (b, s, MLA_HEADS, QK_ROPE_DIM))], axis=-1)
    attn_out = causal_attention(q_full.transpose(0, 2, 1, 3), k_full.transpose(0, 2, 1, 3),
                                v.transpose(0, 2, 1, 3))

    xbc = lax.conv_general_dilated(xbc, conv_w[:, None, :], window_strides=(1,),
                                   padding=[(CONV_WIDTH - 1, 0)],
                                   dimension_numbers=('NWC', 'WIO', 'NWC'),
                                   feature_group_count=D_CONV)
    xbc = jax.nn.silu(xbc + conv_b)
    xs, bm, cm = jnp.split(xbc, [D_SSD, D_SSD + SSD_GROUPS * SSD_STATE], axis=-1)
    xs = xs.reshape(b, s, SSD_HEADS, SSD_HEAD_DIM)
    bm = bm.reshape(b, s, SSD_GROUPS, SSD_STATE)
    cm = cm.reshape(b, s, SSD_GROUPS, SSD_STATE)
    dt = jax.nn.softplus(dt_raw.astype(jnp.float32) + dt_bias.astype(jnp.float32))
    a = -jnp.exp(a_log.astype(jnp.float32))
    y = ssd_chunked(xs, dt, a, bm, cm) + d_skip.astype(jnp.float32)[:, None] * xs.astype(jnp.float32)
    y = y.astype(h.dtype).reshape(b, s, D_SSD)
    yg = (y * jax.nn.silu(z)).reshape(b, s, SSD_GROUPS, D_SSD // SSD_GROUPS)
    yg = rmsnorm(yg, jnp.ones((D_SSD // SSD_GROUPS,), h.dtype)).reshape(b, s, D_SSD) * ssd_norm_w

    return jnp.concatenate([attn_out, yg], axis=-1) @ w_out


def swiglu(h, w_gate_up, w_down):
    g, u = jnp.split(h @ w_gate_up, 2, axis=-1)
    return (jax.nn.silu(g) * u) @ w_down


def _fwd_setup_inputs(seed: int = 0) -> dict:
    key = jax.random.key(seed)
    ks = jax.random.split(key, 32)
    f32 = jnp.float32

    def nrm(k, shape, scale):
        return jax.random.normal(k, shape, f32) * scale

    def gain(k, shape):
        return 1.0 + 0.05 * jax.random.normal(k, shape, f32)

    L = DEPTH
    dt0 = jnp.exp(jax.random.uniform(ks[18], (L, SSD_HEADS), f32, np.log(1e-3), np.log(1e-1)))
    dt_bias = dt0 + jnp.log(-jnp.expm1(-dt0))
    pos_off = jax.random.randint(ks[2], (BATCH, 1), 0, 1024, dtype=jnp.int32)
    return {
        "x": nrm(ks[0], (BATCH, SEQ, D_MODEL), 1.0),
        "c": nrm(ks[1], (BATCH, D_MODEL), 1.0),
        "positions": pos_off + jnp.arange(SEQ, dtype=jnp.int32)[None, :],
        "norm1_w": gain(ks[3], (L, D_MODEL)),
        "norm2_w": gain(ks[4], (L, D_MODEL)),
        "w_ada": nrm(ks[5], (L, D_MODEL, N_MOD * D_MODEL), 0.5 * D_MODEL ** -0.5),
        "b_ada": nrm(ks[6], (L, N_MOD * D_MODEL), 0.01),
        "w_in": nrm(ks[7], (L, D_MODEL, D_IN), D_MODEL ** -0.5),
        "q_a_norm_w": gain(ks[8], (L, Q_LORA_RANK)),
        "w_q_up": nrm(ks[9], (L, Q_LORA_RANK, MLA_HEADS * QK_HEAD_DIM), Q_LORA_RANK ** -0.5),
        "kv_a_norm_w": gain(ks[10], (L, KV_LORA_RANK)),
        "w_kv_up": nrm(ks[11], (L, KV_LORA_RANK, MLA_HEADS * (QK_NOPE_DIM + V_HEAD_DIM)), KV_LORA_RANK ** -0.5),
        "q_nope_norm_w": gain(ks[12], (L, QK_NOPE_DIM)),
        "q_pe_norm_w": gain(ks[13], (L, QK_ROPE_DIM)),
        "k_nope_norm_w": gain(ks[14], (L, QK_NOPE_DIM)),
        "k_pe_norm_w": gain(ks[15], (L, QK_ROPE_DIM)),
        "conv_w": nrm(ks[16], (L, CONV_WIDTH, D_CONV), CONV_WIDTH ** -0.5),
        "conv_b": nrm(ks[17], (L, D_CONV), 0.01),
        "dt_bias": dt_bias,
        "a_log": jnp.log(jax.random.uniform(ks[19], (L, SSD_HEADS), f32, 1.0, 16.0)),
        "d_skip": 1.0 + 0.1 * jax.random.normal(ks[20], (L, SSD_HEADS), f32),
        "ssd_norm_w": gain(ks[21], (L, D_SSD)),
        "w_out": nrm(ks[22], (L, D_MIX, D_MODEL), D_MIX ** -0.5),
        "w_gate_up": nrm(ks[23], (L, D_MODEL, 2 * D_FF), D_MODEL ** -0.5),
        "w_down": nrm(ks[24], (L, D_FF, D_MODEL), D_FF ** -0.5),
    }


def _fwd_reference(x, c, positions, norm1_w, norm2_w, w_ada, b_ada, w_in, q_a_norm_w, w_q_up,
              kv_a_norm_w, w_kv_up, q_nope_norm_w, q_pe_norm_w, k_nope_norm_w, k_pe_norm_w,
              conv_w, conv_b, dt_bias, a_log, d_skip, ssd_norm_w, w_out, w_gate_up, w_down):
    inv_freq = 1.0 / (ROPE_THETA ** (jnp.arange(0, QK_ROPE_DIM, 2, dtype=jnp.float32) / QK_ROPE_DIM))
    ang = positions.astype(jnp.float32)[..., None] * inv_freq
    cos = jnp.cos(ang)[:, :, None, :].astype(x.dtype)
    sin = jnp.sin(ang)[:, :, None, :].astype(x.dtype)
    c_act = jax.nn.silu(c)
    for l in range(DEPTH):
        mod = (c_act @ w_ada[l] + b_ada[l])[:, None, :]
        sh1, sc1, g1, sh2, sc2, g2 = jnp.split(mod, N_MOD, axis=-1)
        h = rmsnorm(x, norm1_w[l]) * (1.0 + sc1) + sh1
        x = x + g1 * hybrid_mixer(h, cos, sin, w_in[l], q_a_norm_w[l], w_q_up[l], kv_a_norm_w[l],
                                  w_kv_up[l], q_nope_norm_w[l], q_pe_norm_w[l], k_nope_norm_w[l],
                                  k_pe_norm_w[l], conv_w[l], conv_b[l], dt_bias[l], a_log[l],
                                  d_skip[l], ssd_norm_w[l], w_out[l])
        h = rmsnorm(x, norm2_w[l]) * (1.0 + sc2) + sh2
        x = x + g2 * swiglu(h, w_gate_up[l], w_down[l])
    return x


import jax as _jax
import jax.numpy as _jnp

TWIN_FORMAT = 'train_step'
FWD_PARAMS = ['x', 'c', 'positions', 'norm1_w', 'norm2_w', 'w_ada', 'b_ada', 'w_in', 'q_a_norm_w', 'w_q_up', 'kv_a_norm_w', 'w_kv_up', 'q_nope_norm_w', 'q_pe_norm_w', 'k_nope_norm_w', 'k_pe_norm_w', 'conv_w', 'conv_b', 'dt_bias', 'a_log', 'd_skip', 'ssd_norm_w', 'w_out', 'w_gate_up', 'w_down']
TWIN_WEIGHTS = ['norm1_w', 'norm2_w', 'w_ada', 'b_ada', 'w_in', 'q_a_norm_w', 'w_q_up', 'kv_a_norm_w', 'w_kv_up', 'q_nope_norm_w', 'q_pe_norm_w', 'k_nope_norm_w', 'k_pe_norm_w', 'conv_w', 'conv_b', 'dt_bias', 'a_log', 'd_skip', 'ssd_norm_w', 'w_out', 'w_gate_up', 'w_down']
TWIN_DIFF_INPUT = 'x'
TWIN_INPUTS = ['x', 'c', 'positions', 'norm1_w', 'norm2_w', 'w_ada', 'b_ada', 'w_in', 'q_a_norm_w', 'w_q_up', 'kv_a_norm_w', 'w_kv_up', 'q_nope_norm_w', 'q_pe_norm_w', 'k_nope_norm_w', 'k_pe_norm_w', 'conv_w', 'conv_b', 'dt_bias', 'a_log', 'd_skip', 'ssd_norm_w', 'w_out', 'w_gate_up', 'w_down', 'loss_target', 'm_norm1_w', 'm_norm2_w', 'm_w_ada', 'm_b_ada', 'm_w_in', 'm_q_a_norm_w', 'm_w_q_up', 'm_kv_a_norm_w', 'm_w_kv_up', 'm_q_nope_norm_w', 'm_q_pe_norm_w', 'm_k_nope_norm_w', 'm_k_pe_norm_w', 'm_conv_w', 'm_conv_b', 'm_dt_bias', 'm_a_log', 'm_d_skip', 'm_ssd_norm_w', 'm_w_out', 'm_w_gate_up', 'm_w_down', 'v_norm1_w', 'v_norm2_w', 'v_w_ada', 'v_b_ada', 'v_w_in', 'v_q_a_norm_w', 'v_w_q_up', 'v_kv_a_norm_w', 'v_w_kv_up', 'v_q_nope_norm_w', 'v_q_pe_norm_w', 'v_k_nope_norm_w', 'v_k_pe_norm_w', 'v_conv_w', 'v_conv_b', 'v_dt_bias', 'v_a_log', 'v_d_skip', 'v_ssd_norm_w', 'v_w_out', 'v_w_gate_up', 'v_w_down']
TWIN_OUTPUTS = ['loss', 'grad_x', 'grad_norm1_w', 'grad_norm2_w', 'grad_w_ada', 'grad_b_ada', 'grad_w_in', 'grad_q_a_norm_w', 'grad_w_q_up', 'grad_kv_a_norm_w', 'grad_w_kv_up', 'grad_q_nope_norm_w', 'grad_q_pe_norm_w', 'grad_k_nope_norm_w', 'grad_k_pe_norm_w', 'grad_conv_w', 'grad_conv_b', 'grad_dt_bias', 'grad_a_log', 'grad_d_skip', 'grad_ssd_norm_w', 'grad_w_out', 'grad_w_gate_up', 'grad_w_down', 'delta_norm1_w', 'delta_norm2_w', 'delta_w_ada', 'delta_b_ada', 'delta_w_in', 'delta_q_a_norm_w', 'delta_w_q_up', 'delta_kv_a_norm_w', 'delta_w_kv_up', 'delta_q_nope_norm_w', 'delta_q_pe_norm_w', 'delta_k_nope_norm_w', 'delta_k_pe_norm_w', 'delta_conv_w', 'delta_conv_b', 'delta_dt_bias', 'delta_a_log', 'delta_d_skip', 'delta_ssd_norm_w', 'delta_w_out', 'delta_w_gate_up', 'delta_w_down', 'new_m_norm1_w', 'new_m_norm2_w', 'new_m_w_ada', 'new_m_b_ada', 'new_m_w_in', 'new_m_q_a_norm_w', 'new_m_w_q_up', 'new_m_kv_a_norm_w', 'new_m_w_kv_up', 'new_m_q_nope_norm_w', 'new_m_q_pe_norm_w', 'new_m_k_nope_norm_w', 'new_m_k_pe_norm_w', 'new_m_conv_w', 'new_m_conv_b', 'new_m_dt_bias', 'new_m_a_log', 'new_m_d_skip', 'new_m_ssd_norm_w', 'new_m_w_out', 'new_m_w_gate_up', 'new_m_w_down', 'new_v_norm1_w', 'new_v_norm2_w', 'new_v_w_ada', 'new_v_b_ada', 'new_v_w_in', 'new_v_q_a_norm_w', 'new_v_w_q_up', 'new_v_kv_a_norm_w', 'new_v_w_kv_up', 'new_v_q_nope_norm_w', 'new_v_q_pe_norm_w', 'new_v_k_nope_norm_w', 'new_v_k_pe_norm_w', 'new_v_conv_w', 'new_v_conv_b', 'new_v_dt_bias', 'new_v_a_log', 'new_v_d_skip', 'new_v_ssd_norm_w', 'new_v_w_out', 'new_v_w_gate_up', 'new_v_w_down']
TWIN_LEAF_KINDS = {'loss': 'loss', 'grad_x': 'grad_x', 'grad_norm1_w': 'grad_w', 'grad_norm2_w': 'grad_w', 'grad_w_ada': 'grad_w', 'grad_b_ada': 'grad_w', 'grad_w_in': 'grad_w', 'grad_q_a_norm_w': 'grad_w', 'grad_w_q_up': 'grad_w', 'grad_kv_a_norm_w': 'grad_w', 'grad_w_kv_up': 'grad_w', 'grad_q_nope_norm_w': 'grad_w', 'grad_q_pe_norm_w': 'grad_w', 'grad_k_nope_norm_w': 'grad_w', 'grad_k_pe_norm_w': 'grad_w', 'grad_conv_w': 'grad_w', 'grad_conv_b': 'grad_w', 'grad_dt_bias': 'grad_w', 'grad_a_log': 'grad_w', 'grad_d_skip': 'grad_w', 'grad_ssd_norm_w': 'grad_w', 'grad_w_out': 'grad_w', 'grad_w_gate_up': 'grad_w', 'grad_w_down': 'grad_w', 'delta_norm1_w': 'delta_w', 'delta_norm2_w': 'delta_w', 'delta_w_ada': 'delta_w', 'delta_b_ada': 'delta_w', 'delta_w_in': 'delta_w', 'delta_q_a_norm_w': 'delta_w', 'delta_w_q_up': 'delta_w', 'delta_kv_a_norm_w': 'delta_w', 'delta_w_kv_up': 'delta_w', 'delta_q_nope_norm_w': 'delta_w', 'delta_q_pe_norm_w': 'delta_w', 'delta_k_nope_norm_w': 'delta_w', 'delta_k_pe_norm_w': 'delta_w', 'delta_conv_w': 'delta_w', 'delta_conv_b': 'delta_w', 'delta_dt_bias': 'delta_w', 'delta_a_log': 'delta_w', 'delta_d_skip': 'delta_w', 'delta_ssd_norm_w': 'delta_w', 'delta_w_out': 'delta_w', 'delta_w_gate_up': 'delta_w', 'delta_w_down': 'delta_w', 'new_m_norm1_w': 'new_m', 'new_m_norm2_w': 'new_m', 'new_m_w_ada': 'new_m', 'new_m_b_ada': 'new_m', 'new_m_w_in': 'new_m', 'new_m_q_a_norm_w': 'new_m', 'new_m_w_q_up': 'new_m', 'new_m_kv_a_norm_w': 'new_m', 'new_m_w_kv_up': 'new_m', 'new_m_q_nope_norm_w': 'new_m', 'new_m_q_pe_norm_w': 'new_m', 'new_m_k_nope_norm_w': 'new_m', 'new_m_k_pe_norm_w': 'new_m', 'new_m_conv_w': 'new_m', 'new_m_conv_b': 'new_m', 'new_m_dt_bias': 'new_m', 'new_m_a_log': 'new_m', 'new_m_d_skip': 'new_m', 'new_m_ssd_norm_w': 'new_m', 'new_m_w_out': 'new_m', 'new_m_w_gate_up': 'new_m', 'new_m_w_down': 'new_m', 'new_v_norm1_w': 'new_v', 'new_v_norm2_w': 'new_v', 'new_v_w_ada': 'new_v', 'new_v_b_ada': 'new_v', 'new_v_w_in': 'new_v', 'new_v_q_a_norm_w': 'new_v', 'new_v_w_q_up': 'new_v', 'new_v_kv_a_norm_w': 'new_v', 'new_v_w_kv_up': 'new_v', 'new_v_q_nope_norm_w': 'new_v', 'new_v_q_pe_norm_w': 'new_v', 'new_v_k_nope_norm_w': 'new_v', 'new_v_k_pe_norm_w': 'new_v', 'new_v_conv_w': 'new_v', 'new_v_conv_b': 'new_v', 'new_v_dt_bias': 'new_v', 'new_v_a_log': 'new_v', 'new_v_d_skip': 'new_v', 'new_v_ssd_norm_w': 'new_v', 'new_v_w_out': 'new_v', 'new_v_w_gate_up': 'new_v', 'new_v_w_down': 'new_v'}


def _forward(args):
    return _fwd_reference(*[args[k] for k in FWD_PARAMS])


def _output_shape():
    out = _jax.eval_shape(lambda: _forward(_fwd_setup_inputs(0)))
    return out.shape, out.dtype

N_MICROBATCH = 1
ADAM_LR = 0.001
ADAM_B1 = 0.9
ADAM_B2 = 0.999
ADAM_EPS = 1e-08
ADAM_WD = 0.01
ADAM_STEP = 10
PER_EXAMPLE_BATCH_AXIS = {'x': 0, 'c': 0, 'positions': 0, 'loss_target': 0}
SHARED_INPUTS = []
_WEIGHT_DTYPES = {'norm1_w': _jnp.float32, 'norm2_w': _jnp.float32, 'w_ada': _jnp.float32, 'b_ada': _jnp.float32, 'w_in': _jnp.float32, 'q_a_norm_w': _jnp.float32, 'w_q_up': _jnp.float32, 'kv_a_norm_w': _jnp.float32, 'w_kv_up': _jnp.float32, 'q_nope_norm_w': _jnp.float32, 'q_pe_norm_w': _jnp.float32, 'k_nope_norm_w': _jnp.float32, 'k_pe_norm_w': _jnp.float32, 'conv_w': _jnp.float32, 'conv_b': _jnp.float32, 'dt_bias': _jnp.float32, 'a_log': _jnp.float32, 'd_skip': _jnp.float32, 'ssd_norm_w': _jnp.float32, 'w_out': _jnp.float32, 'w_gate_up': _jnp.float32, 'w_down': _jnp.float32}
MOMENT_SCALE = {'norm1_w': 1.214139e-01, 'norm2_w': 3.260611e+00, 'w_ada': 1.028550e+00, 'b_ada': 2.110510e+00, 'w_in': 1.693156e-01, 'q_a_norm_w': 2.625349e-02, 'w_q_up': 1.426781e-02, 'kv_a_norm_w': 1.181448e+00, 'w_kv_up': 1.809480e-01, 'q_nope_norm_w': 9.877539e-02, 'q_pe_norm_w': 7.522626e-02, 'k_nope_norm_w': 1.006337e-01, 'k_pe_norm_w': 7.745381e-02, 'conv_w': 1.718720e-01, 'conv_b': 3.205594e-01, 'dt_bias': 6.602847e-01, 'a_log': 8.451592e-01, 'd_skip': 8.420568e-01, 'ssd_norm_w': 4.484713e+00, 'w_out': 2.319892e-01, 'w_gate_up': 6.301957e-02, 'w_down': 8.539794e-02}


def _to_microbatches(a, axis):
    t = _jnp.moveaxis(a, axis, 0)
    t = t.reshape((N_MICROBATCH, t.shape[0] // N_MICROBATCH) + t.shape[1:])
    return _jnp.moveaxis(t, 1, axis + 1)


def setup_inputs(seed: int = 0) -> dict:
    inp = _fwd_setup_inputs(seed)
    key = _jax.random.fold_in(_jax.random.key(seed), 7919)
    shape, _ = _output_shape()
    out = dict(inp)
    out["loss_target"] = _jax.random.normal(_jax.random.fold_in(key, 0), shape, _jnp.float32)
    for i, name in enumerate(TWIN_WEIGHTS):
        w = inp[name].astype(_jnp.float32)
        if MOMENT_SCALE is None:
            s = _jnp.sqrt(_jnp.mean(_jnp.square(w)) + 1e-30)
        else:
            s = MOMENT_SCALE[name]
        km, kv = _jax.random.split(_jax.random.fold_in(key, i + 1))
        out[name] = w
        out["m_" + name] = s * _jax.random.normal(km, w.shape, _jnp.float32)
        out["v_" + name] = (s * s) * _jax.random.uniform(kv, w.shape, _jnp.float32, 0.5, 1.5)
    if N_MICROBATCH > 1:
        for name, axis in PER_EXAMPLE_BATCH_AXIS.items():
            out[name] = _to_microbatches(out[name], axis)
    return {'x': out['x'], 'c': out['c'], 'positions': out['positions'], 'norm1_w': out['norm1_w'], 'norm2_w': out['norm2_w'], 'w_ada': out['w_ada'], 'b_ada': out['b_ada'], 'w_in': out['w_in'], 'q_a_norm_w': out['q_a_norm_w'], 'w_q_up': out['w_q_up'], 'kv_a_norm_w': out['kv_a_norm_w'], 'w_kv_up': out['w_kv_up'], 'q_nope_norm_w': out['q_nope_norm_w'], 'q_pe_norm_w': out['q_pe_norm_w'], 'k_nope_norm_w': out['k_nope_norm_w'], 'k_pe_norm_w': out['k_pe_norm_w'], 'conv_w': out['conv_w'], 'conv_b': out['conv_b'], 'dt_bias': out['dt_bias'], 'a_log': out['a_log'], 'd_skip': out['d_skip'], 'ssd_norm_w': out['ssd_norm_w'], 'w_out': out['w_out'], 'w_gate_up': out['w_gate_up'], 'w_down': out['w_down'], 'loss_target': out['loss_target'], 'm_norm1_w': out['m_norm1_w'], 'm_norm2_w': out['m_norm2_w'], 'm_w_ada': out['m_w_ada'], 'm_b_ada': out['m_b_ada'], 'm_w_in': out['m_w_in'], 'm_q_a_norm_w': out['m_q_a_norm_w'], 'm_w_q_up': out['m_w_q_up'], 'm_kv_a_norm_w': out['m_kv_a_norm_w'], 'm_w_kv_up': out['m_w_kv_up'], 'm_q_nope_norm_w': out['m_q_nope_norm_w'], 'm_q_pe_norm_w': out['m_q_pe_norm_w'], 'm_k_nope_norm_w': out['m_k_nope_norm_w'], 'm_k_pe_norm_w': out['m_k_pe_norm_w'], 'm_conv_w': out['m_conv_w'], 'm_conv_b': out['m_conv_b'], 'm_dt_bias': out['m_dt_bias'], 'm_a_log': out['m_a_log'], 'm_d_skip': out['m_d_skip'], 'm_ssd_norm_w': out['m_ssd_norm_w'], 'm_w_out': out['m_w_out'], 'm_w_gate_up': out['m_w_gate_up'], 'm_w_down': out['m_w_down'], 'v_norm1_w': out['v_norm1_w'], 'v_norm2_w': out['v_norm2_w'], 'v_w_ada': out['v_w_ada'], 'v_b_ada': out['v_b_ada'], 'v_w_in': out['v_w_in'], 'v_q_a_norm_w': out['v_q_a_norm_w'], 'v_w_q_up': out['v_w_q_up'], 'v_kv_a_norm_w': out['v_kv_a_norm_w'], 'v_w_kv_up': out['v_w_kv_up'], 'v_q_nope_norm_w': out['v_q_nope_norm_w'], 'v_q_pe_norm_w': out['v_q_pe_norm_w'], 'v_k_nope_norm_w': out['v_k_nope_norm_w'], 'v_k_pe_norm_w': out['v_k_pe_norm_w'], 'v_conv_w': out['v_conv_w'], 'v_conv_b': out['v_conv_b'], 'v_dt_bias': out['v_dt_bias'], 'v_a_log': out['v_a_log'], 'v_d_skip': out['v_d_skip'], 'v_ssd_norm_w': out['v_ssd_norm_w'], 'v_w_out': out['v_w_out'], 'v_w_gate_up': out['v_w_gate_up'], 'v_w_down': out['v_w_down']}


def _loss(weights, diff, rest, loss_target):
    with _jax.named_scope("forward"):
        args = {**rest, TWIN_DIFF_INPUT: diff, **{k: w.astype(_WEIGHT_DTYPES[k]) for k, w in weights.items()}}
        y = _forward(args)
    with _jax.named_scope("loss_head"):
        err = _jnp.square(y.astype(_jnp.float32) - loss_target)
        return 0.5 * _jnp.sum(_jnp.mean(err, axis=-1)) if err.ndim else 0.5 * err


def _adamw(w, g, m, v):
    m = ADAM_B1 * m + (1.0 - ADAM_B1) * g
    v = ADAM_B2 * v + (1.0 - ADAM_B2) * _jnp.square(g)
    m_hat = m / (1.0 - ADAM_B1 ** ADAM_STEP)
    v_hat = v / (1.0 - ADAM_B2 ** ADAM_STEP)
    delta = -ADAM_LR * (m_hat / (_jnp.sqrt(v_hat) + ADAM_EPS) + ADAM_WD * w)
    return delta, m, v


def reference(x, c, positions, norm1_w, norm2_w, w_ada, b_ada, w_in, q_a_norm_w, w_q_up, kv_a_norm_w, w_kv_up, q_nope_norm_w, q_pe_norm_w, k_nope_norm_w, k_pe_norm_w, conv_w, conv_b, dt_bias, a_log, d_skip, ssd_norm_w, w_out, w_gate_up, w_down, loss_target, m_norm1_w, m_norm2_w, m_w_ada, m_b_ada, m_w_in, m_q_a_norm_w, m_w_q_up, m_kv_a_norm_w, m_w_kv_up, m_q_nope_norm_w, m_q_pe_norm_w, m_k_nope_norm_w, m_k_pe_norm_w, m_conv_w, m_conv_b, m_dt_bias, m_a_log, m_d_skip, m_ssd_norm_w, m_w_out, m_w_gate_up, m_w_down, v_norm1_w, v_norm2_w, v_w_ada, v_b_ada, v_w_in, v_q_a_norm_w, v_w_q_up, v_kv_a_norm_w, v_w_kv_up, v_q_nope_norm_w, v_q_pe_norm_w, v_k_nope_norm_w, v_k_pe_norm_w, v_conv_w, v_conv_b, v_dt_bias, v_a_log, v_d_skip, v_ssd_norm_w, v_w_out, v_w_gate_up, v_w_down):
    given = dict(x=x, c=c, positions=positions, norm1_w=norm1_w, norm2_w=norm2_w, w_ada=w_ada, b_ada=b_ada, w_in=w_in, q_a_norm_w=q_a_norm_w, w_q_up=w_q_up, kv_a_norm_w=kv_a_norm_w, w_kv_up=w_kv_up, q_nope_norm_w=q_nope_norm_w, q_pe_norm_w=q_pe_norm_w, k_nope_norm_w=k_nope_norm_w, k_pe_norm_w=k_pe_norm_w, conv_w=conv_w, conv_b=conv_b, dt_bias=dt_bias, a_log=a_log, d_skip=d_skip, ssd_norm_w=ssd_norm_w, w_out=w_out, w_gate_up=w_gate_up, w_down=w_down, loss_target=loss_target, m_norm1_w=m_norm1_w, m_norm2_w=m_norm2_w, m_w_ada=m_w_ada, m_b_ada=m_b_ada, m_w_in=m_w_in, m_q_a_norm_w=m_q_a_norm_w, m_w_q_up=m_w_q_up, m_kv_a_norm_w=m_kv_a_norm_w, m_w_kv_up=m_w_kv_up, m_q_nope_norm_w=m_q_nope_norm_w, m_q_pe_norm_w=m_q_pe_norm_w, m_k_nope_norm_w=m_k_nope_norm_w, m_k_pe_norm_w=m_k_pe_norm_w, m_conv_w=m_conv_w, m_conv_b=m_conv_b, m_dt_bias=m_dt_bias, m_a_log=m_a_log, m_d_skip=m_d_skip, m_ssd_norm_w=m_ssd_norm_w, m_w_out=m_w_out, m_w_gate_up=m_w_gate_up, m_w_down=m_w_down, v_norm1_w=v_norm1_w, v_norm2_w=v_norm2_w, v_w_ada=v_w_ada, v_b_ada=v_b_ada, v_w_in=v_w_in, v_q_a_norm_w=v_q_a_norm_w, v_w_q_up=v_w_q_up, v_kv_a_norm_w=v_kv_a_norm_w, v_w_kv_up=v_w_kv_up, v_q_nope_norm_w=v_q_nope_norm_w, v_q_pe_norm_w=v_q_pe_norm_w, v_k_nope_norm_w=v_k_nope_norm_w, v_k_pe_norm_w=v_k_pe_norm_w, v_conv_w=v_conv_w, v_conv_b=v_conv_b, v_dt_bias=v_dt_bias, v_a_log=v_a_log, v_d_skip=v_d_skip, v_ssd_norm_w=v_ssd_norm_w, v_w_out=v_w_out, v_w_gate_up=v_w_gate_up, v_w_down=v_w_down)
    weights = {n: given[n] for n in TWIN_WEIGHTS}
    shared = {n: given[n] for n in SHARED_INPUTS}
    per_example = {n: given[n] for n in ['x', 'c', 'positions']}
    grad_fn = _jax.value_and_grad(_loss, argnums=(0, 1))

    def one_microbatch(ex, loss_target):
        ex = dict(ex)
        diff = ex.pop(TWIN_DIFF_INPUT)
        return grad_fn(weights, diff, {**shared, **ex}, loss_target)

    if N_MICROBATCH == 1:
        loss, (grad_w, grad_x) = one_microbatch(per_example, given["loss_target"])
    else:
        def body(carry, xs):
            loss_sum, grad_sum = carry
            l_k, (gw_k, gx_k) = one_microbatch(xs[0], xs[1])
            with _jax.named_scope("update"):
                return (loss_sum + l_k, _jax.tree.map(_jnp.add, grad_sum, gw_k)), gx_k

        init = (_jnp.zeros((), _jnp.float32), _jax.tree.map(_jnp.zeros_like, weights))
        (loss, grad_w), grad_x = _jax.lax.scan(body, init, (per_example, given["loss_target"]))
    with _jax.named_scope("update"):
        delta_w, new_m, new_v = {}, {}, {}
        for n in TWIN_WEIGHTS:
            delta_w[n], new_m[n], new_v[n] = _adamw(weights[n], grad_w[n], given["m_" + n], given["v_" + n])
    return (loss, grad_x, *[grad_w[n] for n in TWIN_WEIGHTS], *[delta_w[n] for n in TWIN_WEIGHTS],
            *[new_m[n] for n in TWIN_WEIGHTS], *[new_v[n] for n in TWIN_WEIGHTS])
```

```python
import functools

import jax
import jax.numpy as jnp
from jax import lax
from jax.experimental import pallas as pl
from jax.experimental.pallas import tpu as pltpu

f32 = jnp.float32
bf16 = jnp.bfloat16

N_DEV = 8
D_MODEL = 1024
N_HEADS = 8
HEAD_LANES = 128
NOPE = 64
ROPE = 32
V_DIM = 64
Q_RANK = 256
KV_RANK = 128
D_SSD = 512
D_CONV = 1024
SSD_STATE = 128
SSD_HEAD_DIM = 64
CHUNK = 128
HALO = 8
D_FF = 2816
FF_SHARD = 704
D_IN = 1960
D_PROJ = 2048
EPS = 1e-6
ATTN_SCALE = (NOPE + ROPE) ** -0.5
ROPE_THETA = 10000.0
NEG = -1e30

ADAM_LR = 0.001
ADAM_B1 = 0.9
ADAM_B2 = 0.999
ADAM_EPS = 1e-08
ADAM_WD = 0.01
ADAM_STEP = 10
ADAMW_BLOCK_BYTES = 24 << 20

MESH = pl.DeviceIdType.MESH
ANY = pl.BlockSpec(memory_space=pl.ANY)

SMALL = (("norm1_w", 1024), ("norm2_w", 1024), ("b_ada", 6144), ("q_a_norm_w", 256), ("kv_a_norm_w", 128),
         ("q_nope_norm_w", 64), ("q_pe_norm_w", 32), ("k_nope_norm_w", 64), ("k_pe_norm_w", 32),
         ("conv_b", 1024), ("dt_bias", 8), ("a_log", 8), ("d_skip", 8), ("ssd_norm_w", 512))
SMALL_ROWS = 168
BIG = ("w_in", "w_q_up", "w_kv_up", "conv_w", "w_out", "w_gate_up", "w_down")
WEIGHTS = ("norm1_w", "norm2_w", "w_ada", "b_ada", "w_in", "q_a_norm_w", "w_q_up", "kv_a_norm_w", "w_kv_up",
           "q_nope_norm_w", "q_pe_norm_w", "k_nope_norm_w", "k_pe_norm_w", "conv_w", "conv_b", "dt_bias",
           "a_log", "d_skip", "ssd_norm_w", "w_out", "w_gate_up", "w_down")


def _dot(a, b, ca, cb):
    return lax.dot_general(a.astype(bf16), b.astype(bf16), (((ca,), (cb,)), ((), ())), preferred_element_type=f32)


@jax.custom_vjp
def mm(a, b):
    return _dot(a, b, 1, 0)


def _mm_fwd(a, b):
    return _dot(a, b, 1, 0), (a, b)


def _mm_bwd(res, g):
    a, b = res
    return _dot(g, b, 1, 1).astype(a.dtype), _dot(a, g, 0, 0).astype(b.dtype)


mm.defvjp(_mm_fwd, _mm_bwd)


@jax.custom_vjp
def mm_nt(a, b):
    return _dot(a, b, 1, 1)


def _mm_nt_fwd(a, b):
    return _dot(a, b, 1, 1), (a, b)


def _mm_nt_bwd(res, g):
    a, b = res
    return _dot(g, b, 1, 0).astype(a.dtype), _dot(g, a, 0, 0).astype(b.dtype)


mm_nt.defvjp(_mm_nt_fwd, _mm_nt_bwd)


@jax.custom_vjp
def mm_tn(a, b):
    return _dot(a, b, 0, 0)


def _mm_tn_fwd(a, b):
    return _dot(a, b, 0, 0), (a, b)


def _mm_tn_bwd(res, g):
    a, b = res
    return _dot(b, g, 1, 1).astype(a.dtype), _dot(a, g, 1, 0).astype(b.dtype)


mm_tn.defvjp(_mm_tn_fwd, _mm_tn_bwd)


def _rms(x, w):
    return x * lax.rsqrt(jnp.mean(x * x, axis=-1, keepdims=True) + EPS) * w


def _const(shape):
    n = len(shape)
    return pl.BlockSpec(shape, lambda *_: (0,) * n)


def _accumulate(first, refs, vals):
    @pl.when(first)
    def _():
        for r, v in zip(refs, vals):
            r[...] = v

    @pl.when(jnp.logical_not(first))
    def _():
        for r, v in zip(refs, vals):
            r[...] += v


def _token_block(s):
    return min(512, s)


def _f_proj(x, nw, sh, sc, w):
    h = _rms(x, nw) * (1.0 + sc) + sh
    return mm(h, w)


def _f_qkv(pa, plast, cos_t, sin_t, qaw, kvaw, wq, wk, wv, qnw, knw, kpw):
    lane = lax.broadcasted_iota(jnp.int32, (1, HEAD_LANES), 1)
    m_nope = lane < NOPE
    m_pe = (lane >= NOPE) & (lane < NOPE + ROPE)
    rows = pa.shape[0]

    def rope(t):
        half = ROPE // 2
        swapped = jnp.concatenate(
            [jnp.zeros((rows, NOPE), f32), t[:, NOPE + half:NOPE + ROPE], t[:, NOPE:NOPE + half],
             jnp.zeros((rows, HEAD_LANES - NOPE - ROPE), f32)], axis=1)
        return t * cos_t + swapped * sin_t

    qa = _rms(pa[:, :Q_RANK], qaw)
    kva = _rms(pa[:, Q_RANK:Q_RANK + KV_RANK], kvaw)
    kp = jnp.where(m_pe, plast, 0.0)
    kp = kp * lax.rsqrt(jnp.sum(kp * kp, axis=-1, keepdims=True) / ROPE + EPS) * kpw
    k_rot = rope(kp)
    qs, ks, vs = [], [], []
    for h in range(N_HEADS):
        qh = mm(qa, wq[h])
        ss_n = jnp.sum(jnp.where(m_nope, qh * qh, 0.0), axis=-1, keepdims=True) / NOPE
        ss_p = jnp.sum(jnp.where(m_pe, qh * qh, 0.0), axis=-1, keepdims=True) / ROPE
        r = jnp.where(m_nope, lax.rsqrt(ss_n + EPS), lax.rsqrt(ss_p + EPS))
        qs.append(rope(qh * r * qnw) * ATTN_SCALE)
        kh = mm(kva, wk[h])
        kh = kh * lax.rsqrt(jnp.sum(kh * kh, axis=-1, keepdims=True) / NOPE + EPS) * knw
        ks.append(kh + k_rot)
        vs.append(mm(kva, wv[h]))
    return jnp.stack(qs), jnp.stack(ks), jnp.stack(vs)


def _f_ssd(xext, z, plast, prev, cw, cb, dtb, alog, dskip, snw):
    n = CHUNK
    conv = cb
    for k in range(4):
        conv = conv + cw[k:k + 1] * xext[HALO - 3 + k:HALO - 3 + k + n]
    xc = jax.nn.silu(conv)
    xs, bm, cm = xc[:, :D_SSD], xc[:, D_SSD:D_SSD + 2 * SSD_STATE], xc[:, D_SSD + 2 * SSD_STATE:]
    lane = lax.broadcasted_iota(jnp.int32, (1, 128), 1)
    dt = jax.nn.softplus(jnp.where(lane < N_HEADS, plast, 0.0) + dtb)
    adt = dt * (-jnp.exp(alog))
    row = lax.broadcasted_iota(jnp.int32, (n, n), 0)
    col = lax.broadcasted_iota(jnp.int32, (n, n), 1)
    tri = row >= col
    acs = jnp.dot(tri.astype(f32), adt, precision=lax.Precision.HIGHEST, preferred_element_type=f32)
    acs_t = acs.T
    ys, news = [], []
    for g in range(2):
        bg = bm[:, g * SSD_STATE:(g + 1) * SSD_STATE]
        cg = cm[:, g * SSD_STATE:(g + 1) * SSD_STATE]
        cb_t = mm_nt(cg, bg)
        for r in range(4):
            h = g * 4 + r
            a_col = acs[:, h:h + 1]
            a_row = acs_t[h:h + 1, :]
            decay_ls = jnp.exp(jnp.where(tri, a_col - a_row, -jnp.inf))
            xh = xs[:, h * SSD_HEAD_DIM:(h + 1) * SSD_HEAD_DIM]
            xdt = xh * dt[:, h:h + 1]
            y_diag = mm(cb_t * decay_ls, xdt)
            a_last = acs[n - 1:n, h:h + 1]
            st = mm_tn(xdt * jnp.exp(a_last - a_col), bg)
            news.append(jnp.exp(a_last) * prev[h] + st)
            y_off = mm_nt(cg, prev[h]) * jnp.exp(a_col)
            ys.append(y_diag + y_off + dskip[:, h:h + 1] * xh)
    y = jnp.concatenate(ys, axis=1)
    yg = y * jax.nn.silu(z)
    half = D_SSD // 2
    outs = []
    for g in range(2):
        t = yg[:, g * half:(g + 1) * half]
        outs.append(t * lax.rsqrt(jnp.mean(t * t, axis=-1, keepdims=True) + EPS))
    return jnp.concatenate(outs, axis=1) * snw, jnp.stack(news)


def _f_out(o, yg, g1, wo):
    cat = jnp.concatenate([o[h] for h in range(N_HEADS)] + [yg], axis=1)
    return g1 * mm(cat, wo)


def _f_mlp(x, nw, sh, sc, g2, wg, wu, wd):
    h = _rms(x, nw) * (1.0 + sc) + sh
    act = jax.nn.silu(mm(h, wg)) * mm(h, wu)
    return g2 * mm(act, wd)


def proj_fwd(x, nw, sh, sc, w):
    s = x.shape[0]
    ts = _token_block(s)

    def body(x_ref, nw_ref, sh_ref, sc_ref, w_ref, pa_ref, pz_ref, px_ref, pl_ref):
        p = _f_proj(x_ref[...], nw_ref[...], sh_ref[...], sc_ref[...], w_ref[...])
        pa_ref[...] = p[:, :384]
        pz_ref[...] = p[:, 384:896]
        px_ref[...] = p[:, 896:1920]
        pl_ref[...] = p[:, 1920:]

    vec = _const((1, D_MODEL))
    return pl.pallas_call(
        body, name="proj_fwd", grid=(s // ts,),
        in_specs=[pl.BlockSpec((ts, D_MODEL), lambda i: (i, 0)), vec, vec, vec, _const((D_MODEL, D_PROJ))],
        out_specs=[pl.BlockSpec((ts, 384), lambda i: (i, 0)), pl.BlockSpec((ts, 512), lambda i: (i, 0)),
                   pl.BlockSpec((ts, 1024), lambda i: (i, 0)), pl.BlockSpec((ts, 128), lambda i: (i, 0))],
        out_shape=[jax.ShapeDtypeStruct((s, 384), f32), jax.ShapeDtypeStruct((s, 512), f32),
                   jax.ShapeDtypeStruct((s, 1024), f32), jax.ShapeDtypeStruct((s, 128), f32)],
    )(x, nw, sh, sc, w)


def rope_tables(pos, inv):
    s = pos.shape[0]
    ts = _token_block(s)

    def body(pos_ref, inv_ref, cos_ref, sin_ref):
        ang = pos_ref[...].astype(f32) * inv_ref[...]
        lane = lax.broadcasted_iota(jnp.int32, (1, HEAD_LANES), 1)
        half = ROPE // 2
        cos_ref[...] = jnp.where(lane < NOPE, 1.0, jnp.where(lane < NOPE + ROPE, jnp.cos(ang), 0.0))
        sn = jnp.sin(ang)
        sin_ref[...] = jnp.where((lane >= NOPE) & (lane < NOPE + half), -sn,
                                 jnp.where((lane >= NOPE + half) & (lane < NOPE + ROPE), sn, 0.0))

    return pl.pallas_call(
        body, name="rope_tables", grid=(s // ts,),
        in_specs=[pl.BlockSpec((ts, 1), lambda i: (i, 0)), _const((1, HEAD_LANES))],
        out_specs=[pl.BlockSpec((ts, HEAD_LANES), lambda i: (i, 0))] * 2,
        out_shape=[jax.ShapeDtypeStruct((s, HEAD_LANES), f32)] * 2,
    )(pos, inv)


def _qkv_param_specs():
    return [_const((1, Q_RANK)), _const((1, KV_RANK)), _const((N_HEADS, Q_RANK, HEAD_LANES)),
            _const((N_HEADS, KV_RANK, HEAD_LANES)), _const((N_HEADS, KV_RANK, V_DIM)),
            _const((1, HEAD_LANES)), _const((1, HEAD_LANES)), _const((1, HEAD_LANES))]


def qkv_fwd(pa, plast, cos_t, sin_t, params):
    s = pa.shape[0]
    ts = _token_block(s)

    def body(pa_ref, pl_ref, cos_ref, sin_ref, *rest):
        prm = [r[...] for r in rest[:8]]
        q_ref, k_ref, v_ref = rest[8:]
        q, k, v = _f_qkv(pa_ref[...], pl_ref[...], cos_ref[...], sin_ref[...], *prm)
        q_ref[...] = q.astype(bf16)
        k_ref[...] = k.astype(bf16)
        v_ref[...] = v.astype(bf16)

    tok = lambda w: pl.BlockSpec((ts, w), lambda i: (i, 0))
    return pl.pallas_call(
        body, name="qkv_fwd", grid=(s // ts,),
        in_specs=[tok(384), tok(128), tok(128), tok(128)] + _qkv_param_specs(),
        out_specs=[pl.BlockSpec((N_HEADS, ts, HEAD_LANES), lambda i: (0, i, 0)),
                   pl.BlockSpec((N_HEADS, ts, HEAD_LANES), lambda i: (0, i, 0)),
                   pl.BlockSpec((N_HEADS, ts, V_DIM), lambda i: (0, i, 0))],
        out_shape=[jax.ShapeDtypeStruct((N_HEADS, s, HEAD_LANES), bf16), jax.ShapeDtypeStruct((N_HEADS, s, HEAD_LANES), bf16),
                   jax.ShapeDtypeStruct((N_HEADS, s, V_DIM), bf16)],
    )(pa, plast, cos_t, sin_t, *params)


def _causal_scores(q, k, qi, ki, tq, tk):
    s = lax.dot_general(q, k, (((1,), (1,)), ((), ())), preferred_element_type=f32)
    row = qi * tq + lax.broadcasted_iota(jnp.int32, (tq, tk), 0)
    col = ki * tk + lax.broadcasted_iota(jnp.int32, (tq, tk), 1)
    return jnp.where(row >= col, s, NEG)


def attn_fwd(q, k, v):
    s = q.shape[1]
    t = _token_block(s)
    nb = s // t

    def body(q_ref, k_ref, v_ref, o_ref, lse_ref, m_sc, l_sc, acc_sc):
        qi, ki = pl.program_id(1), pl.program_id(2)

        @pl.when(ki == 0)
        def _():
            m_sc[...] = jnp.full(m_sc.shape, NEG, f32)
            l_sc[...] = jnp.zeros(l_sc.shape, f32)
            acc_sc[...] = jnp.zeros(acc_sc.shape, f32)

        @pl.when(ki <= qi)
        def _():
            sc = _causal_scores(q_ref[...], k_ref[...], qi, ki, t, t)
            m_prev = m_sc[:, :1]
            m_new = jnp.maximum(m_prev, jnp.max(sc, axis=-1, keepdims=True))
            p = jnp.exp(sc - m_new)
            alpha = jnp.exp(m_prev - m_new)
            l_new = alpha * l_sc[:, :1] + jnp.sum(p, axis=-1, keepdims=True)
            acc_sc[...] = alpha * acc_sc[...] + jnp.dot(p.astype(bf16), v_ref[...], preferred_element_type=f32)
            m_sc[...] = jnp.broadcast_to(m_new, m_sc.shape)
            l_sc[...] = jnp.broadcast_to(l_new, l_sc.shape)

        @pl.when(ki == qi)
        def _():
            l = l_sc[:, :1]
            o_ref[...] = acc_sc[...] / l
            lse_ref[...] = jnp.broadcast_to(m_sc[:, :1] + jnp.log(l), lse_ref.shape)

    kv_idx = lambda h, qi, ki: (h, jnp.minimum(ki, qi), 0)
    return pl.pallas_call(
        body, name="attn_fwd", grid=(N_HEADS, nb, nb),
        in_specs=[pl.BlockSpec((None, t, HEAD_LANES), lambda h, qi, ki: (h, qi, 0)),
                  pl.BlockSpec((None, t, HEAD_LANES), kv_idx), pl.BlockSpec((None, t, V_DIM), kv_idx)],
        out_specs=[pl.BlockSpec((None, t, V_DIM), lambda h, qi, ki: (h, qi, 0)),
                   pl.BlockSpec((None, t, 128), lambda h, qi, ki: (h, qi, 0))],
        out_shape=[jax.ShapeDtypeStruct((N_HEADS, s, V_DIM), f32), jax.ShapeDtypeStruct((N_HEADS, s, 128), f32)],
        scratch_shapes=[pltpu.VMEM((t, 128), f32), pltpu.VMEM((t, 128), f32), pltpu.VMEM((t, V_DIM), f32)],
    )(q, k, v)


def _ssd_param_specs():
    return [_const((4, D_CONV)), _const((1, D_CONV)), _const((1, 128)), _const((1, 128)), _const((1, 128)),
            _const((1, D_SSD))]


def ssd_fwd(px, pz, plast, params):
    s = px.shape[0]
    nc = s // CHUNK

    def body(px_ref, pz_ref, pl_ref, cw_ref, cb_ref, dtb_ref, alog_ref, dskip_ref, snw_ref, yg_ref, st_ref,
             state_sc, halo_sc):
        i = pl.program_id(0)

        @pl.when(i == 0)
        def _():
            state_sc[...] = jnp.zeros(state_sc.shape, f32)
            halo_sc[...] = jnp.zeros(halo_sc.shape, f32)

        x = px_ref[...]
        prev = state_sc[...]
        st_ref[...] = prev
        xext = jnp.concatenate([halo_sc[...], x], axis=0)
        yg, new = _f_ssd(xext, pz_ref[...], pl_ref[...], prev, cw_ref[...], cb_ref[...], dtb_ref[...],
                         alog_ref[...], dskip_ref[...], snw_ref[...])
        yg_ref[...] = yg
        state_sc[...] = new
        halo_sc[...] = x[CHUNK - HALO:]

    tok = lambda w: pl.BlockSpec((CHUNK, w), lambda i: (i, 0))
    return pl.pallas_call(
        body, name="ssd_fwd", grid=(nc,),
        in_specs=[tok(D_CONV), tok(D_SSD), tok(128)] + _ssd_param_specs(),
        out_specs=[tok(D_SSD), pl.BlockSpec((None, N_HEADS, SSD_HEAD_DIM, SSD_STATE), lambda i: (i, 0, 0, 0))],
        out_shape=[jax.ShapeDtypeStruct((s, D_SSD), f32),
                   jax.ShapeDtypeStruct((nc, N_HEADS, SSD_HEAD_DIM, SSD_STATE), f32)],
        scratch_shapes=[pltpu.VMEM((N_HEADS, SSD_HEAD_DIM, SSD_STATE), f32), pltpu.VMEM((HALO, D_CONV), f32)],
    )(px, pz, plast, *params)


def out_fwd(x, o, yg, g1, wo):
    s = x.shape[0]
    ts = _token_block(s)

    def body(x_ref, o_ref, yg_ref, g1_ref, wo_ref, out_ref):
        out_ref[...] = x_ref[...] + _f_out(o_ref[...], yg_ref[...], g1_ref[...], wo_ref[...])

    return pl.pallas_call(
        body, name="out_fwd", grid=(s // ts,),
        in_specs=[pl.BlockSpec((ts, D_MODEL), lambda i: (i, 0)), pl.BlockSpec((N_HEADS, ts, V_DIM), lambda i: (0, i, 0)),
                  pl.BlockSpec((ts, D_SSD), lambda i: (i, 0)), _const((1, D_MODEL)), _const((D_MODEL, D_MODEL))],
        out_specs=pl.BlockSpec((ts, D_MODEL), lambda i: (i, 0)),
        out_shape=jax.ShapeDtypeStruct((s, D_MODEL), f32),
    )(x, o, yg, g1, wo)


def mlp_fwd(x, nw, sh, sc, g2, wgu, wd):
    s = x.shape[0]
    ts = _token_block(s)
    nj = N_DEV // 2

    def body(x_ref, nw_ref, sh_ref, sc_ref, g2_ref, wg_ref, wu_ref, wd_ref, out_ref, acc_sc):
        j = pl.program_id(1)
        part = _f_mlp(x_ref[...], nw_ref[...], sh_ref[...], sc_ref[...], g2_ref[...], wg_ref[...], wu_ref[...], wd_ref[...])

        @pl.when(j == 0)
        def _():
            acc_sc[...] = x_ref[...] + part

        @pl.when(j > 0)
        def _():
            acc_sc[...] += part

        @pl.when(j == nj - 1)
        def _():
            out_ref[...] = acc_sc[...]

    vec = _const((1, D_MODEL))
    return pl.pallas_call(
        body, name="mlp_fwd", grid=(s // ts, nj),
        in_specs=[pl.BlockSpec((ts, D_MODEL), lambda i, j: (i, 0)), vec, vec, vec, vec,
                  pl.BlockSpec((None, D_MODEL, FF_SHARD), lambda i, j: (j, 0, 0)),
                  pl.BlockSpec((None, D_MODEL, FF_SHARD), lambda i, j: (j + nj, 0, 0)),
                  pl.BlockSpec((None, FF_SHARD, D_MODEL), lambda i, j: (j, 0, 0))],
        out_specs=pl.BlockSpec((ts, D_MODEL), lambda i, j: (i, 0)),
        out_shape=jax.ShapeDtypeStruct((s, D_MODEL), f32),
        scratch_shapes=[pltpu.VMEM((ts, D_MODEL), f32)],
    )(x, nw, sh, sc, g2, wgu, wgu, wd)


def loss_fwd(y, target):
    s = y.shape[0]
    ts = _token_block(s)

    def body(y_ref, t_ref, dy_ref, loss_ref):
        d = y_ref[...] - t_ref[...]
        dy_ref[...] = d * (1.0 / D_MODEL)
        part = 0.5 * jnp.sum(jnp.sum(d * d, axis=-1, keepdims=True) * (1.0 / D_MODEL), axis=0, keepdims=True)
        _accumulate(pl.program_id(0) == 0, [loss_ref], [jnp.broadcast_to(part, (8, 128))])

    return pl.pallas_call(
        body, name="loss_fwd", grid=(s // ts,),
        in_specs=[pl.BlockSpec((ts, D_MODEL), lambda i: (i, 0))] * 2,
        out_specs=[pl.BlockSpec((ts, D_MODEL), lambda i: (i, 0)), _const((8, 128))],
        out_shape=[jax.ShapeDtypeStruct((s, D_MODEL), f32), jax.ShapeDtypeStruct((8, 128), f32)],
    )(y, target)


def mlp_bwd(x, dy, nw, sh, sc, g2, wgu, wd):
    s = x.shape[0]
    ts = min(256, s)
    nj = N_DEV // 2

    def body(x_ref, dy_ref, nw_ref, sh_ref, sc_ref, g2_ref, wg_ref, wu_ref, wd_ref,
             dx_ref, dnw_ref, dsh_ref, dsc_ref, dg2_ref, dwg_ref, dwu_ref, dwd_ref):
        j, i = pl.program_id(0), pl.program_id(1)
        _, vjp = jax.vjp(_f_mlp, x_ref[...], nw_ref[...], sh_ref[...], sc_ref[...], g2_ref[...],
                         wg_ref[...].astype(f32), wu_ref[...].astype(f32), wd_ref[...].astype(f32))
        dx, dnw, dsh, dsc, dg2, dwg, dwu, dwd = vjp(dy_ref[...])
        dx_ref[...] = dx
        _accumulate((i == 0) & (j == 0), [dnw_ref, dsh_ref, dsc_ref, dg2_ref], [dnw, dsh, dsc, dg2])
        _accumulate(i == 0, [dwg_ref, dwu_ref, dwd_ref], [dwg, dwu, dwd])

    vec = _const((1, D_MODEL))
    vshape = jax.ShapeDtypeStruct((1, D_MODEL), f32)
    wspec = lambda off: pl.BlockSpec((None, D_MODEL, FF_SHARD), lambda j, i: (j + off, 0, 0))
    dspec = pl.BlockSpec((None, FF_SHARD, D_MODEL), lambda j, i: (j, 0, 0))
    outs = pl.pallas_call(
        body, name="mlp_bwd", grid=(nj, s // ts),
        in_specs=[pl.BlockSpec((ts, D_MODEL), lambda j, i: (i, 0)), pl.BlockSpec((ts, D_MODEL), lambda j, i: (i, 0)),
                  vec, vec, vec, vec, wspec(0), wspec(nj), dspec],
        out_specs=[pl.BlockSpec((None, ts, D_MODEL), lambda j, i: (j, i, 0)), vec, vec, vec, vec,
                   wspec(0), wspec(0), dspec],
        out_shape=[jax.ShapeDtypeStruct((nj, s, D_MODEL), f32), vshape, vshape, vshape, vshape,
                   jax.ShapeDtypeStruct((nj, D_MODEL, FF_SHARD), f32), jax.ShapeDtypeStruct((nj, D_MODEL, FF_SHARD), f32),
                   jax.ShapeDtypeStruct((nj, FF_SHARD, D_MODEL), f32)],
    )(x, dy, nw, sh, sc, g2, wgu, wgu, wd)
    return outs


def out_bwd(dy, dparts, o, yg, g1, wo):
    s = dy.shape[0]
    ts = _token_block(s)
    nj = dparts.shape[0]

    def body(dy_ref, dp_ref, o_ref, yg_ref, g1_ref, wo_ref, dx_ref, do_ref, delta_ref, dyg_ref, dg1_ref, dwo_ref):
        g = dy_ref[...]
        for j in range(nj):
            g = g + dp_ref[j]
        dx_ref[...] = g
        o = o_ref[...]
        _, vjp = jax.vjp(_f_out, o, yg_ref[...], g1_ref[...], wo_ref[...].astype(f32))
        do, dyg, dg1, dwo = vjp(g)
        do_ref[...] = do
        dyg_ref[...] = dyg
        delta_ref[...] = jnp.broadcast_to(jnp.sum(do * o, axis=-1, keepdims=True), delta_ref.shape)
        _accumulate(pl.program_id(0) == 0, [dg1_ref, dwo_ref], [dg1, dwo])

    head = pl.BlockSpec((N_HEADS, ts, V_DIM), lambda i: (0, i, 0))
    return pl.pallas_call(
        body, name="out_bwd", grid=(s // ts,),
        in_specs=[pl.BlockSpec((ts, D_MODEL), lambda i: (i, 0)), pl.BlockSpec((nj, ts, D_MODEL), lambda i: (0, i, 0)),
                  head, pl.BlockSpec((ts, D_SSD), lambda i: (i, 0)), _const((1, D_MODEL)), _const((D_MODEL, D_MODEL))],
        out_specs=[pl.BlockSpec((ts, D_MODEL), lambda i: (i, 0)), head,
                   pl.BlockSpec((N_HEADS, ts, 128), lambda i: (0, i, 0)), pl.BlockSpec((ts, D_SSD), lambda i: (i, 0)),
                   _const((1, D_MODEL)), _const((D_MODEL, D_MODEL))],
        out_shape=[jax.ShapeDtypeStruct((s, D_MODEL), f32), jax.ShapeDtypeStruct((N_HEADS, s, V_DIM), f32),
                   jax.ShapeDtypeStruct((N_HEADS, s, 128), f32), jax.ShapeDtypeStruct((s, D_SSD), f32),
                   jax.ShapeDtypeStruct((1, D_MODEL), f32), jax.ShapeDtypeStruct((D_MODEL, D_MODEL), f32)],
    )(dy, dparts, o, yg, g1, wo)


def attn_bwd_dq(q, k, v, do, lse, delta):
    s = q.shape[1]
    t = _token_block(s)
    nb = s // t

    def body(q_ref, k_ref, v_ref, do_ref, lse_ref, delta_ref, dq_ref):
        qi, ki = pl.program_id(1), pl.program_id(2)

        @pl.when(ki == 0)
        def _():
            dq_ref[...] = jnp.zeros(dq_ref.shape, f32)

        @pl.when(ki <= qi)
        def _():
            k = k_ref[...]
            p = jnp.exp(_causal_scores(q_ref[...], k, qi, ki, t, t) - lse_ref[:, :1])
            dp = lax.dot_general(do_ref[...].astype(bf16), v_ref[...], (((1,), (1,)), ((), ())), preferred_element_type=f32)
            ds = p * (dp - delta_ref[:, :1])
            dq_ref[...] += jnp.dot(ds.astype(bf16), k, preferred_element_type=f32)

    qspec = lambda w: pl.BlockSpec((None, t, w), lambda h, qi, ki: (h, qi, 0))
    kspec = lambda w: pl.BlockSpec((None, t, w), lambda h, qi, ki: (h, jnp.minimum(ki, qi), 0))
    return pl.pallas_call(
        body, name="attn_bwd_dq", grid=(N_HEADS, nb, nb),
        in_specs=[qspec(HEAD_LANES), kspec(HEAD_LANES), kspec(V_DIM), qspec(V_DIM), qspec(128), qspec(128)],
        out_specs=qspec(HEAD_LANES),
        out_shape=jax.ShapeDtypeStruct((N_HEADS, s, HEAD_LANES), f32),
    )(q, k, v, do, lse, delta)


def attn_bwd_dkv(q, k, v, do, lse, delta):
    s = q.shape[1]
    t = _token_block(s)
    nb = s // t

    def body(q_ref, k_ref, v_ref, do_ref, lse_ref, delta_ref, dk_ref, dv_ref):
        ki, qi = pl.program_id(1), pl.program_id(2)

        @pl.when(qi == 0)
        def _():
            dk_ref[...] = jnp.zeros(dk_ref.shape, f32)
            dv_ref[...] = jnp.zeros(dv_ref.shape, f32)

        @pl.when(qi >= ki)
        def _():
            q = q_ref[...]
            do = do_ref[...].astype(bf16)
            p = jnp.exp(_causal_scores(q, k_ref[...], qi, ki, t, t) - lse_ref[:, :1])
            dp = lax.dot_general(do, v_ref[...], (((1,), (1,)), ((), ())), preferred_element_type=f32)
            ds = p * (dp - delta_ref[:, :1])
            dv_ref[...] += lax.dot_general(p.astype(bf16), do, (((0,), (0,)), ((), ())), preferred_element_type=f32)
            dk_ref[...] += lax.dot_general(ds.astype(bf16), q, (((0,), (0,)), ((), ())), preferred_element_type=f32)

    qspec = lambda w: pl.BlockSpec((None, t, w), lambda h, ki, qi: (h, jnp.maximum(qi, ki), 0))
    kspec = lambda w: pl.BlockSpec((None, t, w), lambda h, ki, qi: (h, ki, 0))
    return pl.pallas_call(
        body, name="attn_bwd_dkv", grid=(N_HEADS, nb, nb),
        in_specs=[qspec(HEAD_LANES), kspec(HEAD_LANES), kspec(V_DIM), qspec(V_DIM), qspec(128), qspec(128)],
        out_specs=[kspec(HEAD_LANES), kspec(V_DIM)],
        out_shape=[jax.ShapeDtypeStruct((N_HEADS, s, HEAD_LANES), f32), jax.ShapeDtypeStruct((N_HEADS, s, V_DIM), f32)],
    )(q, k, v, do, lse, delta)


def ssd_bwd(px, pz, plast, states, dyg, params):
    s = px.shape[0]
    nc = s // CHUNK
    per = CHUNK // HALO

    def body(px_ref, halo_ref, pz_ref, pl_ref, st_ref, dyg_ref, cw_ref, cb_ref, dtb_ref, alog_ref, dskip_ref, snw_ref,
             dpx_ref, dpz_ref, dpl_ref, dcw_ref, dcb_ref, ddtb_ref, dalog_ref, ddskip_ref, dsnw_ref, dstate_sc, dhalo_sc):
        t = pl.program_id(0)
        chunk = nc - 1 - t

        @pl.when(t == 0)
        def _():
            dstate_sc[...] = jnp.zeros(dstate_sc.shape, f32)
            dhalo_sc[...] = jnp.zeros(dhalo_sc.shape, f32)

        halo = jnp.where(chunk > 0, halo_ref[...], 0.0)
        xext = jnp.concatenate([halo, px_ref[...]], axis=0)
        _, vjp = jax.vjp(_f_ssd, xext, pz_ref[...], pl_ref[...], st_ref[...], cw_ref[...], cb_ref[...], dtb_ref[...],
                         alog_ref[...], dskip_ref[...], snw_ref[...])
        dxext, dz, dpl, dprev, dcw, dcb, ddtb, dalog, ddskip, dsnw = vjp((dyg_ref[...], dstate_sc[...]))
        dpx_ref[...] = dxext[HALO:]
        dpx_ref[CHUNK - HALO:, :] += dhalo_sc[...]
        dhalo_sc[...] = dxext[:HALO]
        dstate_sc[...] = dprev
        dpz_ref[...] = dz
        dpl_ref[...] = dpl
        _accumulate(t == 0, [dcw_ref, dcb_ref, ddtb_ref, dalog_ref, ddskip_ref, dsnw_ref],
                    [dcw, dcb, ddtb, dalog, ddskip, dsnw])

    rev = lambda w: pl.BlockSpec((CHUNK, w), lambda t: (nc - 1 - t, 0))
    pshapes = [jax.ShapeDtypeStruct((4, D_CONV), f32), jax.ShapeDtypeStruct((1, D_CONV), f32),
               jax.ShapeDtypeStruct((1, 128), f32), jax.ShapeDtypeStruct((1, 128), f32),
               jax.ShapeDtypeStruct((1, 128), f32), jax.ShapeDtypeStruct((1, D_SSD), f32)]
    return pl.pallas_call(
        body, name="ssd_bwd", grid=(nc,),
        in_specs=[rev(D_CONV),
                  pl.BlockSpec((HALO, D_CONV), lambda t: (jnp.maximum((nc - 1 - t) * per - 1, 0), 0)),
                  rev(D_SSD), rev(128),
                  pl.BlockSpec((None, N_HEADS, SSD_HEAD_DIM, SSD_STATE), lambda t: (nc - 1 - t, 0, 0, 0)),
                  rev(D_SSD)] + _ssd_param_specs(),
        out_specs=[rev(D_CONV), rev(D_SSD), rev(128)] + _ssd_param_specs(),
        out_shape=[jax.ShapeDtypeStruct((s, D_CONV), f32), jax.ShapeDtypeStruct((s, D_SSD), f32),
                   jax.ShapeDtypeStruct((s, 128), f32)] + pshapes,
        scratch_shapes=[pltpu.VMEM((N_HEADS, SSD_HEAD_DIM, SSD_STATE), f32), pltpu.VMEM((HALO, D_CONV), f32)],
    )(px, px, pz, plast, states, dyg, *params)


def qkv_bwd(pa, plast, cos_t, sin_t, params, dq, dk, dv):
    s = pa.shape[0]
    ts = _token_block(s)

    def body(pa_ref, pl_ref, cos_ref, sin_ref, *rest):
        prm = [r[...].astype(f32) for r in rest[:8]]
        dq_ref, dk_ref, dv_ref = rest[8:11]
        dpa_ref, dpl_ref = rest[11:13]
        dprm_refs = list(rest[13:])
        cos_t, sin_t = cos_ref[...], sin_ref[...]
        _, vjp = jax.vjp(lambda a, b, *p: _f_qkv(a, b, cos_t, sin_t, *p), pa_ref[...], pl_ref[...], *prm)
        grads = vjp((dq_ref[...], dk_ref[...], dv_ref[...]))
        dpa_ref[...] = grads[0]
        dpl_ref[...] = grads[1]
        _accumulate(pl.program_id(0) == 0, dprm_refs, list(grads[2:]))

    tok = lambda w: pl.BlockSpec((ts, w), lambda i: (i, 0))
    head = lambda w: pl.BlockSpec((N_HEADS, ts, w), lambda i: (0, i, 0))
    pshapes = [jax.ShapeDtypeStruct((1, Q_RANK), f32), jax.ShapeDtypeStruct((1, KV_RANK), f32),
               jax.ShapeDtypeStruct((N_HEADS, Q_RANK, HEAD_LANES), f32), jax.ShapeDtypeStruct((N_HEADS, KV_RANK, HEAD_LANES), f32),
               jax.ShapeDtypeStruct((N_HEADS, KV_RANK, V_DIM), f32), jax.ShapeDtypeStruct((1, HEAD_LANES), f32),
               jax.ShapeDtypeStruct((1, HEAD_LANES), f32), jax.ShapeDtypeStruct((1, HEAD_LANES), f32)]
    return pl.pallas_call(
        body, name="qkv_bwd", grid=(s // ts,),
        in_specs=[tok(384), tok(128), tok(128), tok(128)] + _qkv_param_specs()
                 + [head(HEAD_LANES), head(HEAD_LANES), head(V_DIM)],
        out_specs=[tok(384), tok(128)] + _qkv_param_specs(),
        out_shape=[jax.ShapeDtypeStruct((s, 384), f32), jax.ShapeDtypeStruct((s, 128), f32)] + pshapes,
    )(pa, plast, cos_t, sin_t, *params, dq, dk, dv)


def proj_bwd(x, nw, sh, sc, w, dpa, dpz, dpx, dpl_k, dpl_dt, dres):
    s = x.shape[0]
    ts = _token_block(s)

    def body(x_ref, nw_ref, sh_ref, sc_ref, w_ref, dpa_ref, dpz_ref, dpx_ref, dplk_ref, dpld_ref, dres_ref,
             dx_ref, dnw_ref, dsh_ref, dsc_ref, dw_ref):
        g = jnp.concatenate([dpa_ref[...], dpz_ref[...], dpx_ref[...], dplk_ref[...] + dpld_ref[...]], axis=1)
        _, vjp = jax.vjp(_f_proj, x_ref[...], nw_ref[...], sh_ref[...], sc_ref[...], w_ref[...].astype(f32))
        dx, dnw, dsh, dsc, dw = vjp(g)
        dx_ref[...] = dx + dres_ref[...]
        _accumulate(pl.program_id(0) == 0, [dnw_ref, dsh_ref, dsc_ref, dw_ref], [dnw, dsh, dsc, dw])

    vec = _const((1, D_MODEL))
    vshape = jax.ShapeDtypeStruct((1, D_MODEL), f32)
    tok = lambda w_: pl.BlockSpec((ts, w_), lambda i: (i, 0))
    return pl.pallas_call(
        body, name="proj_bwd", grid=(s // ts,),
        in_specs=[tok(D_MODEL), vec, vec, vec, _const((D_MODEL, D_PROJ)), tok(384), tok(512), tok(1024), tok(128), tok(128),
                  tok(D_MODEL)],
        out_specs=[tok(D_MODEL), vec, vec, vec, _const((D_MODEL, D_PROJ))],
        out_shape=[jax.ShapeDtypeStruct((s, D_MODEL), f32), vshape, vshape, vshape,
                   jax.ShapeDtypeStruct((D_MODEL, D_PROJ), f32)],
    )(x, nw, sh, sc, w, dpa, dpz, dpx, dpl_k, dpl_dt, dres)


def ada_fwd(c_all, w_ada, b_cols):
    def body(c_ref, w_ref, b_ref, out_ref):
        act = jax.nn.silu(c_ref[...])
        for l in range(2):
            out_ref[l] = jnp.dot(act, w_ref[l], precision=lax.Precision.HIGHEST, preferred_element_type=f32) + b_ref[l]

    return pl.pallas_call(body, name="ada_fwd", out_shape=jax.ShapeDtypeStruct((2, N_DEV, 768), f32))(c_all, w_ada, b_cols)


def ada_bwd(c_all, dmod_cols):
    def body(c_ref, d_ref, out_ref):
        out_ref[0] = lax.dot_general(jax.nn.silu(c_ref[...]), d_ref[0], (((0,), (0,)), ((), ())),
                                     precision=lax.Precision.HIGHEST, preferred_element_type=f32)

    return pl.pallas_call(
        body, name="ada_bwd", grid=(2,),
        in_specs=[_const((N_DEV, D_MODEL)), pl.BlockSpec((1, N_DEV, 768), lambda l: (l, 0, 0))],
        out_specs=pl.BlockSpec((1, D_MODEL, 768), lambda l: (l, 0, 0)),
        out_shape=jax.ShapeDtypeStruct((2, D_MODEL, 768), f32),
    )(c_all, dmod_cols)


def _adamw(w, g, m, v):
    m = ADAM_B1 * m + (1.0 - ADAM_B1) * g
    v = ADAM_B2 * v + (1.0 - ADAM_B2) * (g * g)
    m_hat = m / (1.0 - ADAM_B1 ** ADAM_STEP)
    v_hat = v / (1.0 - ADAM_B2 ** ADAM_STEP)
    delta = -ADAM_LR * (m_hat / (jnp.sqrt(v_hat) + ADAM_EPS) + ADAM_WD * w)
    return delta, m, v


def adamw(parts, w, m, v, name):
    n, nl, r, c = parts.shape
    tr = r
    lanes = -(-c // 128) * 128
    if 2 * (n + 7) * r * lanes * 4 > ADAMW_BLOCK_BYTES:
        tr = next(t for t in (256, 128, 64, 32, 16, 8) if r % t == 0)

    def body(p_ref, w_ref, m_ref, v_ref, g_ref, d_ref, nm_ref, nv_ref):
        g = p_ref[0]
        for k in range(1, n):
            g = g + p_ref[k]
        delta, nm, nv = _adamw(w_ref[...], g, m_ref[...], v_ref[...])
        g_ref[...] = g
        d_ref[...] = delta
        nm_ref[...] = nm
        nv_ref[...] = nv

    blk = pl.BlockSpec((None, tr, c), lambda l, i: (l, i, 0))
    shp = jax.ShapeDtypeStruct((nl, r, c), f32)
    return pl.pallas_call(
        body, name=name, grid=(nl, r // tr),
        in_specs=[pl.BlockSpec((n, None, tr, c), lambda l, i: (0, l, i, 0)), blk, blk, blk],
        out_specs=[blk] * 4, out_shape=[shp] * 4,
    )(parts, w, m, v)


def _my_index():
    return 4 * lax.axis_index("x") + 2 * lax.axis_index("y") + lax.axis_index("c")


def _coords(idx):
    return (idx // 4, (idx // 2) % 2, idx % 2)


def all_gather(shards, name):
    n = len(shards)

    def body(*refs):
        ins, outs = refs[:n], refs[n:2 * n]
        send_sems, recv_sems, local_sems = refs[2 * n:]
        me = _my_index()
        local = [pltpu.make_async_copy(ins[k], outs[k].at[me], local_sems.at[k]) for k in range(n)]
        for cp in local:
            cp.start()
        for p in range(1, N_DEV):
            peer = (me + p) % N_DEV
            for k in range(n):
                pltpu.make_async_remote_copy(src_ref=ins[k], dst_ref=outs[k].at[me], send_sem=send_sems.at[k],
                                             recv_sem=recv_sems.at[k], device_id=_coords(peer), device_id_type=MESH).start()
        for k in range(n):
            seven = outs[k].at[pl.ds(0, N_DEV - 1)]
            drain = pltpu.make_async_remote_copy(src_ref=seven, dst_ref=seven, send_sem=send_sems.at[k],
                                                 recv_sem=recv_sems.at[k], device_id=_coords(me), device_id_type=MESH)
            drain.wait_recv()
            drain.wait_send()
        for cp in local:
            cp.wait()

    return pl.pallas_call(
        body, name=name,
        in_specs=[ANY] * n, out_specs=[ANY] * n,
        out_shape=[jax.ShapeDtypeStruct((N_DEV,) + tuple(a.shape), a.dtype) for a in shards],
        scratch_shapes=[pltpu.SemaphoreType.DMA((n,)), pltpu.SemaphoreType.DMA((n,)), pltpu.SemaphoreType.DMA((n,))],
    )(*shards)


def reduce_scatter_parts(fulls, name):
    n = len(fulls)
    flat = [a for pair in fulls for a in pair]

    def body(*refs):
        ins, outs = refs[:2 * n], refs[2 * n:3 * n]
        send_sems, recv_sems, local_sems = refs[3 * n:]
        me = _my_index()
        local = [pltpu.make_async_copy(ins[2 * k + l].at[me], outs[k].at[me, l], local_sems.at[k, l])
                 for k in range(n) for l in range(2)]
        for cp in local:
            cp.start()
        for p in range(1, N_DEV):
            peer = (me + p) % N_DEV
            for k in range(n):
                for l in range(2):
                    pltpu.make_async_remote_copy(src_ref=ins[2 * k + l].at[peer], dst_ref=outs[k].at[me, l],
                                                 send_sem=send_sems.at[k], recv_sem=recv_sems.at[k],
                                                 device_id=_coords(peer), device_id_type=MESH).start()
        for k in range(n):
            seven = outs[k].at[pl.ds(0, N_DEV - 1)]
            drain = pltpu.make_async_remote_copy(src_ref=seven, dst_ref=seven, send_sem=send_sems.at[k],
                                                 recv_sem=recv_sems.at[k], device_id=_coords(me), device_id_type=MESH)
            drain.wait_recv()
            drain.wait_send()
        for cp in local:
            cp.wait()

    return pl.pallas_call(
        body, name=name,
        in_specs=[ANY] * (2 * n), out_specs=[ANY] * n,
        out_shape=[jax.ShapeDtypeStruct((N_DEV, 2) + tuple(pair[0].shape[1:]), pair[0].dtype) for pair in fulls],
        scratch_shapes=[pltpu.SemaphoreType.DMA((n,)), pltpu.SemaphoreType.DMA((n,)), pltpu.SemaphoreType.DMA((n, 2))],
    )(*flat)


def _pad_lanes(v, lo, total=128):
    return jnp.pad(v, (lo, total - lo - v.shape[0]))[None, :]


def _layer_weights(lw):
    w_in = lw["w_in"]
    z = jnp.zeros((D_MODEL, 1), w_in.dtype)
    w_proj = jnp.concatenate(
        [w_in[:, :384], w_in[:, 416:928], w_in[:, 928:1952], w_in[:, 1952:1960], jnp.tile(z, (1, 56)),
         w_in[:, 384:416], jnp.tile(z, (1, 32))], axis=1)
    wq = jnp.pad(lw["w_q_up"], ((0, 0), (0, 0), (0, HEAD_LANES - NOPE - ROPE)))
    wk = jnp.pad(lw["w_kv_up"][:, :, :NOPE], ((0, 0), (0, 0), (0, HEAD_LANES - NOPE)))
    wv = lw["w_kv_up"][:, :, NOPE:]
    qkv = (lw["q_a_norm_w"][None, :], lw["kv_a_norm_w"][None, :], wq, wk, wv,
           _pad_lanes(jnp.concatenate([lw["q_nope_norm_w"], lw["q_pe_norm_w"]]), 0),
           _pad_lanes(lw["k_nope_norm_w"], 0), _pad_lanes(lw["k_pe_norm_w"], NOPE))
    ssd = (lw["conv_w"], lw["conv_b"][None, :], _pad_lanes(lw["dt_bias"], 0), _pad_lanes(lw["a_log"], 0),
           _pad_lanes(lw["d_skip"], 0), lw["ssd_norm_w"][None, :])
    return dict(w_proj=w_proj, qkv=qkv, ssd=ssd, wo=lw["w_out"], wgu=lw["w_gate_up"], wd=lw["w_down"],
                n1=lw["norm1_w"][None, :], n2=lw["norm2_w"][None, :])


def layer_fwd(x, mod, kw, cos_t, sin_t):
    sh1, sc1, g1, sh2, sc2, g2 = [mod[i:i + 1] for i in range(6)]
    pa, pz, px, plast = proj_fwd(x, kw["n1"], sh1, sc1, kw["w_proj"])
    q, k, v = qkv_fwd(pa, plast, cos_t, sin_t, kw["qkv"])
    o, lse = attn_fwd(q, k, v)
    yg, states = ssd_fwd(px, pz, plast, kw["ssd"])
    x_mid = out_fwd(x, o, yg, g1, kw["wo"])
    x_out = mlp_fwd(x_mid, kw["n2"], sh2, sc2, g2, kw["wgu"], kw["wd"])
    saved = dict(x=x, pa=pa, pz=pz, px=px, plast=plast, q=q, k=k, v=v, o=o, lse=lse, yg=yg, states=states, x_mid=x_mid)
    return x_out, saved


def layer_bwd(dy, mod, kw, cos_t, sin_t, sv):
    sh1, sc1, g1, sh2, sc2, g2 = [mod[i:i + 1] for i in range(6)]
    dparts, dn2, dsh2, dsc2, dg2, dwg, dwu, dwd = mlp_bwd(sv["x_mid"], dy, kw["n2"], sh2, sc2, g2, kw["wgu"], kw["wd"])
    dmid, do, delta, dyg, dg1, dwo = out_bwd(dy, dparts, sv["o"], sv["yg"], g1, kw["wo"])
    dq = attn_bwd_dq(sv["q"], sv["k"], sv["v"], do, sv["lse"], delta)
    dk, dv = attn_bwd_dkv(sv["q"], sv["k"], sv["v"], do, sv["lse"], delta)
    dpx, dpz, dpl_dt, dcw, dcb, ddtb, dalog, ddskip, dsnw = ssd_bwd(sv["px"], sv["pz"], sv["plast"], sv["states"], dyg, kw["ssd"])
    dpa, dpl_k, dqaw, dkvaw, dwq, dwk, dwv, dqnw, dknw, dkpw = qkv_bwd(sv["pa"], sv["plast"], cos_t, sin_t, kw["qkv"], dq, dk, dv)
    dx, dn1, dsh1, dsc1, dwp = proj_bwd(sv["x"], kw["n1"], sh1, sc1, kw["w_proj"], dpa, dpz, dpx, dpl_k, dpl_dt, dmid)
    dmod = jnp.concatenate([dsh1, dsc1, dg1, dsh2, dsc2, dg2], axis=0)
    dw_in = jnp.concatenate([dwp[:, :384], dwp[:, 1984:2016], dwp[:, 384:1920], dwp[:, 1920:1928]], axis=1)
    grads = dict(
        norm1_w=dn1[0], norm2_w=dn2[0], q_a_norm_w=dqaw[0], kv_a_norm_w=dkvaw[0],
        q_nope_norm_w=dqnw[0, :NOPE], q_pe_norm_w=dqnw[0, NOPE:NOPE + ROPE], k_nope_norm_w=dknw[0, :NOPE],
        k_pe_norm_w=dkpw[0, NOPE:NOPE + ROPE], conv_b=dcb[0], dt_bias=ddtb[0, :N_HEADS], a_log=dalog[0, :N_HEADS],
        d_skip=ddskip[0, :N_HEADS], ssd_norm_w=dsnw[0],
        w_in=dw_in.reshape(D_MODEL, N_DEV, D_IN // N_DEV).transpose(1, 0, 2),
        w_q_up=dwq[:, :, :NOPE + ROPE],
        w_kv_up=jnp.concatenate([dwk[:, :, :NOPE], dwv], axis=2),
        conv_w=dcw.reshape(4, N_DEV, D_CONV // N_DEV).transpose(1, 0, 2),
        w_out=dwo.reshape(N_DEV, D_MODEL // N_DEV, D_MODEL),
        w_gate_up=jnp.concatenate([dwg, dwu], axis=0),
        w_down=dwd.reshape(N_DEV, D_FF // N_DEV, D_MODEL),
    )
    return dx, dmod, grads


def _natural_weights(gathered):
    g = gathered
    return dict(
        w_in=g["w_in"].transpose(1, 0, 2).reshape(D_MODEL, D_IN),
        w_q_up=g["w_q_up"], w_kv_up=g["w_kv_up"],
        conv_w=g["conv_w"].astype(f32).transpose(1, 0, 2).reshape(4, D_CONV),
        w_out=g["w_out"].reshape(D_MODEL, D_MODEL),
        w_gate_up=g["w_gate_up"],
        w_down=g["w_down"].reshape(N_DEV // 2, FF_SHARD, D_MODEL),
    )


def _pack_small(get):
    flat = jnp.concatenate([get(name).reshape(-1) for name, _ in SMALL])
    return jnp.pad(flat, (0, SMALL_ROWS * 128 - flat.shape[0])).reshape(SMALL_ROWS, 128)


def _unpack_small(packed):
    flat = packed.reshape(-1)
    out, off = {}, 0
    for name, size in SMALL:
        out[name] = flat[off:off + 2 * size].reshape(2, size)
        off += 2 * size
    return out


def kernel(x, c, positions, norm1_w, norm2_w, w_ada, b_ada, w_in, q_a_norm_w, w_q_up, kv_a_norm_w, w_kv_up, q_nope_norm_w, q_pe_norm_w, k_nope_norm_w, k_pe_norm_w, conv_w, conv_b, dt_bias, a_log, d_skip, ssd_norm_w, w_out, w_gate_up, w_down, loss_target, m_norm1_w, m_norm2_w, m_w_ada, m_b_ada, m_w_in, m_q_a_norm_w, m_w_q_up, m_kv_a_norm_w, m_w_kv_up, m_q_nope_norm_w, m_q_pe_norm_w, m_k_nope_norm_w, m_k_pe_norm_w, m_conv_w, m_conv_b, m_dt_bias, m_a_log, m_d_skip, m_ssd_norm_w, m_w_out, m_w_gate_up, m_w_down, v_norm1_w, v_norm2_w, v_w_ada, v_b_ada, v_w_in, v_q_a_norm_w, v_w_q_up, v_kv_a_norm_w, v_w_kv_up, v_q_nope_norm_w, v_q_pe_norm_w, v_k_nope_norm_w, v_k_pe_norm_w, v_conv_w, v_conv_b, v_dt_bias, v_a_log, v_d_skip, v_ssd_norm_w, v_w_out, v_w_gate_up, v_w_down):
    w = dict(norm1_w=norm1_w, norm2_w=norm2_w, w_ada=w_ada, b_ada=b_ada, w_in=w_in, q_a_norm_w=q_a_norm_w, w_q_up=w_q_up,
             kv_a_norm_w=kv_a_norm_w, w_kv_up=w_kv_up, q_nope_norm_w=q_nope_norm_w, q_pe_norm_w=q_pe_norm_w,
             k_nope_norm_w=k_nope_norm_w, k_pe_norm_w=k_pe_norm_w, conv_w=conv_w, conv_b=conv_b, dt_bias=dt_bias,
             a_log=a_log, d_skip=d_skip, ssd_norm_w=ssd_norm_w, w_out=w_out, w_gate_up=w_gate_up, w_down=w_down)
    m = dict(norm1_w=m_norm1_w, norm2_w=m_norm2_w, w_ada=m_w_ada, b_ada=m_b_ada, w_in=m_w_in, q_a_norm_w=m_q_a_norm_w,
             w_q_up=m_w_q_up, kv_a_norm_w=m_kv_a_norm_w, w_kv_up=m_w_kv_up, q_nope_norm_w=m_q_nope_norm_w,
             q_pe_norm_w=m_q_pe_norm_w, k_nope_norm_w=m_k_nope_norm_w, k_pe_norm_w=m_k_pe_norm_w, conv_w=m_conv_w,
             conv_b=m_conv_b, dt_bias=m_dt_bias, a_log=m_a_log, d_skip=m_d_skip, ssd_norm_w=m_ssd_norm_w, w_out=m_w_out,
             w_gate_up=m_w_gate_up, w_down=m_w_down)
    v = dict(norm1_w=v_norm1_w, norm2_w=v_norm2_w, w_ada=v_w_ada, b_ada=v_b_ada, w_in=v_w_in, q_a_norm_w=v_q_a_norm_w,
             w_q_up=v_w_q_up, kv_a_norm_w=v_kv_a_norm_w, w_kv_up=v_w_kv_up, q_nope_norm_w=v_q_nope_norm_w,
             q_pe_norm_w=v_q_pe_norm_w, k_nope_norm_w=v_k_nope_norm_w, k_pe_norm_w=v_k_pe_norm_w, conv_w=v_conv_w,
             conv_b=v_conv_b, dt_bias=v_dt_bias, a_log=v_a_log, d_skip=v_d_skip, ssd_norm_w=v_ssd_norm_w, w_out=v_w_out,
             w_gate_up=v_w_gate_up, w_down=v_w_down)
    me = _my_index()
    seq = x.shape[1]

    shards = [c] + [w[name][l] if name == "conv_w" else w[name][l].astype(bf16) for l in range(2) for name in BIG]
    gathered = all_gather(shards, "gather_weights")
    c_all = gathered[0].reshape(N_DEV, D_MODEL)
    kws = []
    for l in range(2):
        g = {name: gathered[1 + l * len(BIG) + i] for i, name in enumerate(BIG)}
        lw = _natural_weights(g)
        for name, _ in SMALL:
            if name != "b_ada":
                lw[name] = w[name][l]
        kws.append(_layer_weights(lw))

    b_cols = lax.dynamic_slice_in_dim(b_ada, me * 768, 768, axis=1)
    mod_cols = ada_fwd(c_all, w_ada, b_cols)
    (mod_all,) = all_gather([mod_cols], "gather_mod")
    mod_me = lax.dynamic_index_in_dim(mod_all, me, axis=2, keepdims=False)
    mods = [mod_me[:, l, :].reshape(6, D_MODEL) for l in range(2)]

    inv_freq = 1.0 / (ROPE_THETA ** (jnp.arange(0, ROPE, 2, dtype=f32) / ROPE))
    inv = _pad_lanes(jnp.concatenate([inv_freq, inv_freq]), NOPE)
    cos_t, sin_t = rope_tables(positions.reshape(seq, 1), inv)

    h = x[0]
    saved = []
    for l in range(2):
        h, sv = layer_fwd(h, mods[l], kws[l], cos_t, sin_t)
        saved.append(sv)
    dy, loss_part = loss_fwd(h, loss_target[0])
    loss = lax.psum(loss_part[0, 0], ("x", "y", "c"))

    grads, dmods = [None, None], [None, None]
    for l in (1, 0):
        dy, dmods[l], grads[l] = layer_bwd(dy, mods[l], kws[l], cos_t, sin_t, saved[l])
    grad_x = dy[None]

    small_part = {name: jnp.stack([grads[0][name], grads[1][name]]) for name, _ in SMALL if name != "b_ada"}
    small_part["b_ada"] = jnp.stack([dmods[0].reshape(-1), dmods[1].reshape(-1)])
    (small_all,) = all_gather([_pack_small(lambda n: small_part[n])], "gather_small_grads")
    sg, sd, sm, sv_ = adamw(small_all.reshape(N_DEV, 1, SMALL_ROWS, 128), _pack_small(lambda n: w[n])[None],
                            _pack_small(lambda n: m[n])[None], _pack_small(lambda n: v[n])[None], "adamw_small")
    res = {}
    for key, packed in (("g", sg), ("d", sd), ("m", sm), ("v", sv_)):
        for name, arr in _unpack_small(packed[0]).items():
            res[key, name] = arr

    off = 2 * (1024 + 1024)
    dmod_all = small_all.reshape(N_DEV, -1)[:, off:off + 2 * 6144].reshape(N_DEV, 2, 6144)
    dmod_cols = lax.dynamic_slice_in_dim(dmod_all, me * 768, 768, axis=2).transpose(1, 0, 2)
    g_ada = ada_bwd(c_all, dmod_cols)
    res["g", "w_ada"], res["d", "w_ada"], res["m", "w_ada"], res["v", "w_ada"] = adamw(
        g_ada[None], w_ada, m_w_ada, v_w_ada, "adamw_w_ada")

    parts = reduce_scatter_parts([(grads[0][name], grads[1][name]) for name in BIG], "scatter_grads")
    for name, part in zip(BIG, parts):
        res["g", name], res["d", name], res["m", name], res["v", name] = adamw(part, w[name], m[name], v[name], "adamw_" + name)

    return (loss, grad_x, *[res["g", n] for n in WEIGHTS], *[res["d", n] for n in WEIGHTS],
            *[res["m", n] for n in WEIGHTS], *[res["v", n] for n in WEIGHTS])
```

```python
import functools

import jax
import jax.numpy as jnp
from jax import lax
from jax.experimental import pallas as pl
from jax.experimental.pallas import tpu as pltpu

f32 = jnp.float32
bf16 = jnp.bfloat16

N_DEV = 8
D_MODEL = 1024
N_HEADS = 8
HEAD_LANES = 128
NOPE = 64
ROPE = 32
V_DIM = 64
Q_RANK = 256
KV_RANK = 128
D_SSD = 512
D_CONV = 1024
SSD_STATE = 128
SSD_HEAD_DIM = 64
CHUNK = 128
HALO = 8
D_FF = 2816
FF_SHARD = 704
D_IN = 1960
D_PROJ = 2048
EPS = 1e-6
LOG2E = 1.4426950408889634
LN2 = 0.6931471805599453
Q_SCALE = (NOPE + ROPE) ** -0.5 * LOG2E
ATTN_ROWS_FWD = 256
ATTN_ROWS_BWD = 512
ROPE_THETA = 10000.0
NEG = -1e30

ADAM_LR = 0.001
ADAM_B1 = 0.9
ADAM_B2 = 0.999
ADAM_EPS = 1e-08
ADAM_WD = 0.01
ADAM_STEP = 10
ADAMW_BLOCK_BYTES = 24 << 20

MESH = pl.DeviceIdType.MESH
ANY = pl.BlockSpec(memory_space=pl.ANY)

SMALL = (("norm1_w", 1024), ("norm2_w", 1024), ("b_ada", 6144), ("q_a_norm_w", 256), ("kv_a_norm_w", 128),
         ("q_nope_norm_w", 64), ("q_pe_norm_w", 32), ("k_nope_norm_w", 64), ("k_pe_norm_w", 32),
         ("conv_b", 1024), ("dt_bias", 8), ("a_log", 8), ("d_skip", 8), ("ssd_norm_w", 512))
SMALL_ROWS = 168
BIG = ("w_in", "w_q_up", "w_kv_up", "conv_w", "w_out", "w_gate_up", "w_down")
WEIGHTS = ("norm1_w", "norm2_w", "w_ada", "b_ada", "w_in", "q_a_norm_w", "w_q_up", "kv_a_norm_w", "w_kv_up",
           "q_nope_norm_w", "q_pe_norm_w", "k_nope_norm_w", "k_pe_norm_w", "conv_w", "conv_b", "dt_bias",
           "a_log", "d_skip", "ssd_norm_w", "w_out", "w_gate_up", "w_down")


def _dot(a, b, ca, cb):
    return lax.dot_general(a.astype(bf16), b.astype(bf16), (((ca,), (cb,)), ((), ())), preferred_element_type=f32)


@jax.custom_vjp
def mm(a, b):
    return _dot(a, b, 1, 0)


def _mm_fwd(a, b):
    return _dot(a, b, 1, 0), (a, b)


def _mm_bwd(res, g):
    a, b = res
    return _dot(g, b, 1, 1).astype(a.dtype), _dot(a, g, 0, 0).astype(b.dtype)


mm.defvjp(_mm_fwd, _mm_bwd)


@jax.custom_vjp
def mm_nt(a, b):
    return _dot(a, b, 1, 1)


def _mm_nt_fwd(a, b):
    return _dot(a, b, 1, 1), (a, b)


def _mm_nt_bwd(res, g):
    a, b = res
    return _dot(g, b, 1, 0).astype(a.dtype), _dot(g, a, 0, 0).astype(b.dtype)


mm_nt.defvjp(_mm_nt_fwd, _mm_nt_bwd)


@jax.custom_vjp
def mm_tn(a, b):
    return _dot(a, b, 0, 0)


def _mm_tn_fwd(a, b):
    return _dot(a, b, 0, 0), (a, b)


def _mm_tn_bwd(res, g):
    a, b = res
    return _dot(b, g, 1, 1).astype(a.dtype), _dot(a, g, 1, 0).astype(b.dtype)


mm_tn.defvjp(_mm_tn_fwd, _mm_tn_bwd)


def _rms(x, w):
    return x * lax.rsqrt(jnp.mean(x * x, axis=-1, keepdims=True) + EPS) * w


def _const(shape):
    n = len(shape)
    return pl.BlockSpec(shape, lambda *_: (0,) * n)


def _accumulate(first, refs, vals):
    @pl.when(first)
    def _():
        for r, v in zip(refs, vals):
            r[...] = v

    @pl.when(jnp.logical_not(first))
    def _():
        for r, v in zip(refs, vals):
            r[...] += v


def _accumulate_then_cast(first, last, accs, outs, vals):
    _accumulate(first, accs, vals)

    @pl.when(last)
    def _():
        for a, o in zip(accs, outs):
            o[...] = a[...].astype(o.dtype)


def _token_block(s):
    return min(512, s)


def _f_proj(x, nw, sh, sc, w):
    h = _rms(x, nw) * (1.0 + sc) + sh
    return mm(h, w)


def _f_qkv(pa, plast, cos_t, sin_t, qaw, kvaw, wq, wk, wv, qnw, knw, kpw):
    lane = lax.broadcasted_iota(jnp.int32, (1, HEAD_LANES), 1)
    m_nope = lane < NOPE
    m_pe = (lane >= NOPE) & (lane < NOPE + ROPE)
    rows = pa.shape[0]

    def rope(t):
        half = ROPE // 2
        swapped = jnp.concatenate(
            [jnp.zeros((rows, NOPE), f32), t[:, NOPE + half:NOPE + ROPE], t[:, NOPE:NOPE + half],
             jnp.zeros((rows, HEAD_LANES - NOPE - ROPE), f32)], axis=1)
        return t * cos_t + swapped * sin_t

    qa = _rms(pa[:, :Q_RANK], qaw)
    kva = _rms(pa[:, Q_RANK:Q_RANK + KV_RANK], kvaw)
    kp = jnp.where(m_pe, plast, 0.0)
    kp = kp * lax.rsqrt(jnp.sum(kp * kp, axis=-1, keepdims=True) / ROPE + EPS) * kpw
    k_rot = rope(kp)
    qs, ks, vs = [], [], []
    for h in range(N_HEADS):
        qh = mm(qa, wq[h])
        ss_n = jnp.sum(jnp.where(m_nope, qh * qh, 0.0), axis=-1, keepdims=True) / NOPE
        ss_p = jnp.sum(jnp.where(m_pe, qh * qh, 0.0), axis=-1, keepdims=True) / ROPE
        r = jnp.where(m_nope, lax.rsqrt(ss_n + EPS), lax.rsqrt(ss_p + EPS))
        qs.append(rope(qh * r * qnw) * Q_SCALE)
        kh = mm(kva, wk[h])
        kh = kh * lax.rsqrt(jnp.sum(kh * kh, axis=-1, keepdims=True) / NOPE + EPS) * knw
        ks.append(kh + k_rot)
        vs.append(mm(kva, wv[h]))
    return jnp.stack(qs), jnp.stack(ks), jnp.stack(vs)


def _f_ssd(xext, z, plast, prev, cw, cb, dtb, alog, dskip, snw):
    n = CHUNK
    conv = cb
    for k in range(4):
        conv = conv + cw[k:k + 1] * xext[HALO - 3 + k:HALO - 3 + k + n]
    xc = jax.nn.silu(conv)
    xs, bm, cm = xc[:, :D_SSD], xc[:, D_SSD:D_SSD + 2 * SSD_STATE], xc[:, D_SSD + 2 * SSD_STATE:]
    lane = lax.broadcasted_iota(jnp.int32, (1, 128), 1)
    dt = jax.nn.softplus(jnp.where(lane < N_HEADS, plast, 0.0) + dtb)
    adt = dt * (-jnp.exp(alog))
    row = lax.broadcasted_iota(jnp.int32, (n, n), 0)
    col = lax.broadcasted_iota(jnp.int32, (n, n), 1)
    tri = row >= col
    acs = jnp.dot(tri.astype(f32), adt, precision=lax.Precision.HIGHEST, preferred_element_type=f32)
    acs_t = acs.T
    ys, news = [], []
    for g in range(2):
        bg = bm[:, g * SSD_STATE:(g + 1) * SSD_STATE]
        cg = cm[:, g * SSD_STATE:(g + 1) * SSD_STATE]
        cb_t = mm_nt(cg, bg)
        for r in range(4):
            h = g * 4 + r
            a_col = acs[:, h:h + 1]
            a_row = acs_t[h:h + 1, :]
            decay_ls = jnp.exp(jnp.where(tri, a_col - a_row, -jnp.inf))
            xh = xs[:, h * SSD_HEAD_DIM:(h + 1) * SSD_HEAD_DIM]
            xdt = xh * dt[:, h:h + 1]
            y_diag = mm(cb_t * decay_ls, xdt)
            a_last = acs[n - 1:n, h:h + 1]
            st = mm_tn(xdt * jnp.exp(a_last - a_col), bg)
            news.append(jnp.exp(a_last) * prev[h] + st)
            y_off = mm_nt(cg, prev[h]) * jnp.exp(a_col)
            ys.append(y_diag + y_off + dskip[:, h:h + 1] * xh)
    y = jnp.concatenate(ys, axis=1)
    yg = y * jax.nn.silu(z)
    half = D_SSD // 2
    outs = []
    for g in range(2):
        t = yg[:, g * half:(g + 1) * half]
        outs.append(t * lax.rsqrt(jnp.mean(t * t, axis=-1, keepdims=True) + EPS))
    return jnp.concatenate(outs, axis=1) * snw, jnp.stack(news)


def _f_out(o, yg, g1, wo):
    cat = jnp.concatenate([o[h] for h in range(N_HEADS)] + [yg], axis=1)
    return g1 * mm(cat, wo)


def _f_mlp(x, nw, sh, sc, g2, wg, wu, wd):
    h = _rms(x, nw) * (1.0 + sc) + sh
    act = jax.nn.silu(mm(h, wg)) * mm(h, wu)
    return g2 * mm(act, wd)


def proj_fwd(x, nw, sh, sc, w):
    s = x.shape[0]
    ts = _token_block(s)

    def body(x_ref, nw_ref, sh_ref, sc_ref, w_ref, pa_ref, pz_ref, px_ref, pl_ref):
        p = _f_proj(x_ref[...], nw_ref[...], sh_ref[...], sc_ref[...], w_ref[...])
        pa_ref[...] = p[:, :384]
        pz_ref[...] = p[:, 384:896]
        px_ref[...] = p[:, 896:1920]
        pl_ref[...] = p[:, 1920:]

    vec = _const((1, D_MODEL))
    return pl.pallas_call(
        body, name="proj_fwd", grid=(s // ts,),
        in_specs=[pl.BlockSpec((ts, D_MODEL), lambda i: (i, 0)), vec, vec, vec, _const((D_MODEL, D_PROJ))],
        out_specs=[pl.BlockSpec((ts, 384), lambda i: (i, 0)), pl.BlockSpec((ts, 512), lambda i: (i, 0)),
                   pl.BlockSpec((ts, 1024), lambda i: (i, 0)), pl.BlockSpec((ts, 128), lambda i: (i, 0))],
        out_shape=[jax.ShapeDtypeStruct((s, 384), f32), jax.ShapeDtypeStruct((s, 512), f32),
                   jax.ShapeDtypeStruct((s, 1024), f32), jax.ShapeDtypeStruct((s, 128), f32)],
    )(x, nw, sh, sc, w)


def rope_tables(pos, inv):
    s = pos.shape[0]
    ts = _token_block(s)

    def body(pos_ref, inv_ref, cos_ref, sin_ref):
        ang = pos_ref[...].astype(f32) * inv_ref[...]
        lane = lax.broadcasted_iota(jnp.int32, (1, HEAD_LANES), 1)
        half = ROPE // 2
        cos_ref[...] = jnp.where(lane < NOPE, 1.0, jnp.where(lane < NOPE + ROPE, jnp.cos(ang), 0.0))
        sn = jnp.sin(ang)
        sin_ref[...] = jnp.where((lane >= NOPE) & (lane < NOPE + half), -sn,
                                 jnp.where((lane >= NOPE + half) & (lane < NOPE + ROPE), sn, 0.0))

    return pl.pallas_call(
        body, name="rope_tables", grid=(s // ts,),
        in_specs=[pl.BlockSpec((ts, 1), lambda i: (i, 0)), _const((1, HEAD_LANES))],
        out_specs=[pl.BlockSpec((ts, HEAD_LANES), lambda i: (i, 0))] * 2,
        out_shape=[jax.ShapeDtypeStruct((s, HEAD_LANES), f32)] * 2,
    )(pos, inv)


def _qkv_param_specs():
    return [_const((1, Q_RANK)), _const((1, KV_RANK)), _const((N_HEADS, Q_RANK, HEAD_LANES)),
            _const((N_HEADS, KV_RANK, HEAD_LANES)), _const((N_HEADS, KV_RANK, V_DIM)),
            _const((1, HEAD_LANES)), _const((1, HEAD_LANES)), _const((1, HEAD_LANES))]


def qkv_fwd(pa, plast, cos_t, sin_t, params):
    s = pa.shape[0]
    ts = _token_block(s)

    def body(pa_ref, pl_ref, cos_ref, sin_ref, *rest):
        prm = [r[...] for r in rest[:8]]
        q_ref, k_ref, v_ref = rest[8:]
        q, k, v = _f_qkv(pa_ref[...], pl_ref[...], cos_ref[...], sin_ref[...], *prm)
        q_ref[...] = q.astype(bf16)
        k_ref[...] = k.astype(bf16)
        v_ref[...] = v.astype(bf16)

    tok = lambda w: pl.BlockSpec((ts, w), lambda i: (i, 0))
    return pl.pallas_call(
        body, name="qkv_fwd", grid=(s // ts,),
        in_specs=[tok(384), tok(128), tok(128), tok(128)] + _qkv_param_specs(),
        out_specs=[pl.BlockSpec((N_HEADS, ts, HEAD_LANES), lambda i: (0, i, 0)),
                   pl.BlockSpec((N_HEADS, ts, HEAD_LANES), lambda i: (0, i, 0)),
                   pl.BlockSpec((N_HEADS, ts, V_DIM), lambda i: (0, i, 0))],
        out_shape=[jax.ShapeDtypeStruct((N_HEADS, s, HEAD_LANES), bf16), jax.ShapeDtypeStruct((N_HEADS, s, HEAD_LANES), bf16),
                   jax.ShapeDtypeStruct((N_HEADS, s, V_DIM), bf16)],
    )(pa, plast, cos_t, sin_t, *params)


def _scores(q, k):
    return lax.dot_general(q, k, (((1,), (1,)), ((), ())), preferred_element_type=f32)


def _tril(rows, cols, row_offset):
    row = row_offset + lax.broadcasted_iota(jnp.int32, (rows, cols), 0)
    col = lax.broadcasted_iota(jnp.int32, (rows, cols), 1)
    return row >= col


def attn_fwd(q, k, v):
    s = q.shape[1]
    t = _token_block(s)
    nb = s // t

    rb = min(ATTN_ROWS_FWD, t)

    def body(q_ref, k_ref, v_ref, o_ref, lse_ref, m_sc, l_sc, acc_sc):
        qi, ki = pl.program_id(1), pl.program_id(2)

        @pl.when(ki == 0)
        def _():
            m_sc[...] = jnp.full(m_sc.shape, NEG, f32)
            l_sc[...] = jnp.zeros(l_sc.shape, f32)
            acc_sc[...] = jnp.zeros(acc_sc.shape, f32)

        def step(diagonal):
            for r in range(t // rb):
                rows = pl.ds(r * rb, rb)
                nk = (r + 1) * rb if diagonal else t
                sc = _scores(q_ref[rows, :], k_ref[:nk, :])
                if diagonal:
                    sc = jnp.where(_tril(rb, nk, r * rb), sc, NEG)
                m_prev = m_sc[rows, :1]
                m_new = jnp.maximum(m_prev, jnp.max(sc, axis=-1, keepdims=True))
                p = jnp.exp2(sc - m_new)
                alpha = jnp.exp2(m_prev - m_new)
                l_new = alpha * l_sc[rows, :1] + jnp.sum(p, axis=-1, keepdims=True)
                acc = alpha * acc_sc[rows, :] + jnp.dot(p.astype(bf16), v_ref[:nk, :], preferred_element_type=f32)
                if diagonal:
                    o_ref[rows, :] = acc / l_new
                    lse_ref[rows, :] = jnp.broadcast_to(m_new + jnp.log2(l_new), (rb, 128))
                else:
                    acc_sc[rows, :] = acc
                    m_sc[rows, :] = jnp.broadcast_to(m_new, (rb, 128))
                    l_sc[rows, :] = jnp.broadcast_to(l_new, (rb, 128))

        @pl.when(ki < qi)
        def _():
            step(False)

        @pl.when(ki == qi)
        def _():
            step(True)

    kv_idx = lambda h, qi, ki: (h, jnp.minimum(ki, qi), 0)
    return pl.pallas_call(
        body, name="attn_fwd", grid=(N_HEADS, nb, nb),
        in_specs=[pl.BlockSpec((None, t, HEAD_LANES), lambda h, qi, ki: (h, qi, 0)),
                  pl.BlockSpec((None, t, HEAD_LANES), kv_idx), pl.BlockSpec((None, t, V_DIM), kv_idx)],
        out_specs=[pl.BlockSpec((None, t, V_DIM), lambda h, qi, ki: (h, qi, 0)),
                   pl.BlockSpec((None, t, 128), lambda h, qi, ki: (h, qi, 0))],
        out_shape=[jax.ShapeDtypeStruct((N_HEADS, s, V_DIM), f32), jax.ShapeDtypeStruct((N_HEADS, s, 128), f32)],
        scratch_shapes=[pltpu.VMEM((t, 128), f32), pltpu.VMEM((t, 128), f32), pltpu.VMEM((t, V_DIM), f32)],
    )(q, k, v)


def _ssd_param_specs():
    return [_const((4, D_CONV)), _const((1, D_CONV)), _const((1, 128)), _const((1, 128)), _const((1, 128)),
            _const((1, D_SSD))]


def ssd_fwd(px, pz, plast, params):
    s = px.shape[0]
    nc = s // CHUNK

    def body(px_ref, pz_ref, pl_ref, cw_ref, cb_ref, dtb_ref, alog_ref, dskip_ref, snw_ref, yg_ref, st_ref,
             state_sc, halo_sc):
        i = pl.program_id(0)

        @pl.when(i == 0)
        def _():
            state_sc[...] = jnp.zeros(state_sc.shape, f32)
            halo_sc[...] = jnp.zeros(halo_sc.shape, f32)

        x = px_ref[...]
        prev = state_sc[...]
        st_ref[...] = prev
        xext = jnp.concatenate([halo_sc[...], x], axis=0)
        yg, new = _f_ssd(xext, pz_ref[...], pl_ref[...], prev, cw_ref[...], cb_ref[...], dtb_ref[...],
                         alog_ref[...], dskip_ref[...], snw_ref[...])
        yg_ref[...] = yg
        state_sc[...] = new
        halo_sc[...] = x[CHUNK - HALO:]

    tok = lambda w: pl.BlockSpec((CHUNK, w), lambda i: (i, 0))
    return pl.pallas_call(
        body, name="ssd_fwd", grid=(nc,),
        in_specs=[tok(D_CONV), tok(D_SSD), tok(128)] + _ssd_param_specs(),
        out_specs=[tok(D_SSD), pl.BlockSpec((None, N_HEADS, SSD_HEAD_DIM, SSD_STATE), lambda i: (i, 0, 0, 0))],
        out_shape=[jax.ShapeDtypeStruct((s, D_SSD), f32),
                   jax.ShapeDtypeStruct((nc, N_HEADS, SSD_HEAD_DIM, SSD_STATE), f32)],
        scratch_shapes=[pltpu.VMEM((N_HEADS, SSD_HEAD_DIM, SSD_STATE), f32), pltpu.VMEM((HALO, D_CONV), f32)],
    )(px, pz, plast, *params)


def out_fwd(x, o, yg, g1, wo):
    s = x.shape[0]
    ts = _token_block(s)

    def body(x_ref, o_ref, yg_ref, g1_ref, wo_ref, out_ref):
        out_ref[...] = x_ref[...] + _f_out(o_ref[...], yg_ref[...], g1_ref[...], wo_ref[...])

    return pl.pallas_call(
        body, name="out_fwd", grid=(s // ts,),
        in_specs=[pl.BlockSpec((ts, D_MODEL), lambda i: (i, 0)), pl.BlockSpec((N_HEADS, ts, V_DIM), lambda i: (0, i, 0)),
                  pl.BlockSpec((ts, D_SSD), lambda i: (i, 0)), _const((1, D_MODEL)), _const((D_MODEL, D_MODEL))],
        out_specs=pl.BlockSpec((ts, D_MODEL), lambda i: (i, 0)),
        out_shape=jax.ShapeDtypeStruct((s, D_MODEL), f32),
    )(x, o, yg, g1, wo)


def mlp_fwd(x, nw, sh, sc, g2, wgu, wd):
    s = x.shape[0]
    ts = _token_block(s)
    nj = N_DEV // 2

    def body(x_ref, nw_ref, sh_ref, sc_ref, g2_ref, wg_ref, wu_ref, wd_ref, out_ref, acc_sc):
        j = pl.program_id(1)
        part = _f_mlp(x_ref[...], nw_ref[...], sh_ref[...], sc_ref[...], g2_ref[...], wg_ref[...], wu_ref[...], wd_ref[...])

        @pl.when(j == 0)
        def _():
            acc_sc[...] = x_ref[...] + part

        @pl.when(j > 0)
        def _():
            acc_sc[...] += part

        @pl.when(j == nj - 1)
        def _():
            out_ref[...] = acc_sc[...]

    vec = _const((1, D_MODEL))
    return pl.pallas_call(
        body, name="mlp_fwd", grid=(s // ts, nj),
        in_specs=[pl.BlockSpec((ts, D_MODEL), lambda i, j: (i, 0)), vec, vec, vec, vec,
                  pl.BlockSpec((None, D_MODEL, FF_SHARD), lambda i, j: (j, 0, 0)),
                  pl.BlockSpec((None, D_MODEL, FF_SHARD), lambda i, j: (j + nj, 0, 0)),
                  pl.BlockSpec((None, FF_SHARD, D_MODEL), lambda i, j: (j, 0, 0))],
        out_specs=pl.BlockSpec((ts, D_MODEL), lambda i, j: (i, 0)),
        out_shape=jax.ShapeDtypeStruct((s, D_MODEL), f32),
        scratch_shapes=[pltpu.VMEM((ts, D_MODEL), f32)],
    )(x, nw, sh, sc, g2, wgu, wgu, wd)


def loss_fwd(y, target):
    s = y.shape[0]
    ts = _token_block(s)

    def body(y_ref, t_ref, dy_ref, loss_ref):
        d = y_ref[...] - t_ref[...]
        dy_ref[...] = d * (1.0 / D_MODEL)
        part = 0.5 * jnp.sum(jnp.sum(d * d, axis=-1, keepdims=True) * (1.0 / D_MODEL), axis=0, keepdims=True)
        _accumulate(pl.program_id(0) == 0, [loss_ref], [jnp.broadcast_to(part, (8, 128))])

    return pl.pallas_call(
        body, name="loss_fwd", grid=(s // ts,),
        in_specs=[pl.BlockSpec((ts, D_MODEL), lambda i: (i, 0))] * 2,
        out_specs=[pl.BlockSpec((ts, D_MODEL), lambda i: (i, 0)), _const((8, 128))],
        out_shape=[jax.ShapeDtypeStruct((s, D_MODEL), f32), jax.ShapeDtypeStruct((8, 128), f32)],
    )(y, target)


def mlp_bwd(x, dy, nw, sh, sc, g2, wgu, wd):
    s = x.shape[0]
    ts = min(256, s)
    nj = N_DEV // 2
    ni = s // ts

    def body(x_ref, dy_ref, nw_ref, sh_ref, sc_ref, g2_ref, wg_ref, wu_ref, wd_ref,
             dx_ref, dnw_ref, dsh_ref, dsc_ref, dg2_ref, dwg_ref, dwu_ref, dwd_ref, ag_sc, au_sc, ad_sc):
        j, i = pl.program_id(0), pl.program_id(1)
        _, vjp = jax.vjp(_f_mlp, x_ref[...], nw_ref[...], sh_ref[...], sc_ref[...], g2_ref[...],
                         wg_ref[...].astype(f32), wu_ref[...].astype(f32), wd_ref[...].astype(f32))
        dx, dnw, dsh, dsc, dg2, dwg, dwu, dwd = vjp(dy_ref[...])
        dx_ref[...] = dx
        _accumulate((i == 0) & (j == 0), [dnw_ref, dsh_ref, dsc_ref, dg2_ref], [dnw, dsh, dsc, dg2])
        _accumulate_then_cast(i == 0, i == ni - 1, [ag_sc, au_sc, ad_sc], [dwg_ref, dwu_ref, dwd_ref], [dwg, dwu, dwd])

    vec = _const((1, D_MODEL))
    vshape = jax.ShapeDtypeStruct((1, D_MODEL), f32)
    wspec = lambda off: pl.BlockSpec((None, D_MODEL, FF_SHARD), lambda j, i: (j + off, 0, 0))
    dspec = pl.BlockSpec((None, FF_SHARD, D_MODEL), lambda j, i: (j, 0, 0))
    outs = pl.pallas_call(
        body, name="mlp_bwd", grid=(nj, ni),
        in_specs=[pl.BlockSpec((ts, D_MODEL), lambda j, i: (i, 0)), pl.BlockSpec((ts, D_MODEL), lambda j, i: (i, 0)),
                  vec, vec, vec, vec, wspec(0), wspec(nj), dspec],
        out_specs=[pl.BlockSpec((None, ts, D_MODEL), lambda j, i: (j, i, 0)), vec, vec, vec, vec,
                   wspec(0), wspec(0), dspec],
        out_shape=[jax.ShapeDtypeStruct((nj, s, D_MODEL), f32), vshape, vshape, vshape, vshape,
                   jax.ShapeDtypeStruct((nj, D_MODEL, FF_SHARD), bf16), jax.ShapeDtypeStruct((nj, D_MODEL, FF_SHARD), bf16),
                   jax.ShapeDtypeStruct((nj, FF_SHARD, D_MODEL), bf16)],
        scratch_shapes=[pltpu.VMEM((D_MODEL, FF_SHARD), f32), pltpu.VMEM((D_MODEL, FF_SHARD), f32),
                        pltpu.VMEM((FF_SHARD, D_MODEL), f32)],
    )(x, dy, nw, sh, sc, g2, wgu, wgu, wd)
    return outs


def out_bwd(dy, dparts, o, yg, g1, wo):
    s = dy.shape[0]
    ts = _token_block(s)
    nj = dparts.shape[0]

    ni = s // ts

    def body(dy_ref, dp_ref, o_ref, yg_ref, g1_ref, wo_ref, dx_ref, do_ref, delta_ref, dyg_ref, dg1_ref, dwo_ref, acc_sc):
        i = pl.program_id(0)
        g = dy_ref[...]
        for j in range(nj):
            g = g + dp_ref[j]
        dx_ref[...] = g
        o = o_ref[...]
        _, vjp = jax.vjp(_f_out, o, yg_ref[...], g1_ref[...], wo_ref[...].astype(f32))
        do, dyg, dg1, dwo = vjp(g)
        do_ref[...] = do
        dyg_ref[...] = dyg
        delta_ref[...] = jnp.broadcast_to(jnp.sum(do * o, axis=-1, keepdims=True), delta_ref.shape)
        _accumulate(i == 0, [dg1_ref], [dg1])
        _accumulate_then_cast(i == 0, i == ni - 1, [acc_sc], [dwo_ref], [dwo])

    head = pl.BlockSpec((N_HEADS, ts, V_DIM), lambda i: (0, i, 0))
    return pl.pallas_call(
        body, name="out_bwd", grid=(ni,), scratch_shapes=[pltpu.VMEM((D_MODEL, D_MODEL), f32)],
        in_specs=[pl.BlockSpec((ts, D_MODEL), lambda i: (i, 0)), pl.BlockSpec((nj, ts, D_MODEL), lambda i: (0, i, 0)),
                  head, pl.BlockSpec((ts, D_SSD), lambda i: (i, 0)), _const((1, D_MODEL)), _const((D_MODEL, D_MODEL))],
        out_specs=[pl.BlockSpec((ts, D_MODEL), lambda i: (i, 0)), head,
                   pl.BlockSpec((N_HEADS, ts, 128), lambda i: (0, i, 0)), pl.BlockSpec((ts, D_SSD), lambda i: (i, 0)),
                   _const((1, D_MODEL)), _const((D_MODEL, D_MODEL))],
        out_shape=[jax.ShapeDtypeStruct((s, D_MODEL), f32), jax.ShapeDtypeStruct((N_HEADS, s, V_DIM), f32),
                   jax.ShapeDtypeStruct((N_HEADS, s, 128), f32), jax.ShapeDtypeStruct((s, D_SSD), f32),
                   jax.ShapeDtypeStruct((1, D_MODEL), f32), jax.ShapeDtypeStruct((D_MODEL, D_MODEL), bf16)],
    )(dy, dparts, o, yg, g1, wo)


def attn_bwd_dq(q, k, v, do, lse, delta):
    s = q.shape[1]
    t = _token_block(s)
    nb = s // t

    rb = min(ATTN_ROWS_BWD, t)

    def body(q_ref, k_ref, v_ref, do_ref, lse_ref, delta_ref, dq_ref):
        qi, ki = pl.program_id(1), pl.program_id(2)

        @pl.when(ki == 0)
        def _():
            dq_ref[...] = jnp.zeros(dq_ref.shape, f32)

        def step(diagonal):
            for r in range(t // rb):
                rows = pl.ds(r * rb, rb)
                nk = (r + 1) * rb if diagonal else t
                k = k_ref[:nk, :]
                sc = _scores(q_ref[rows, :], k)
                if diagonal:
                    sc = jnp.where(_tril(rb, nk, r * rb), sc, NEG)
                p = jnp.exp2(sc - lse_ref[rows, :1])
                dp = lax.dot_general(do_ref[rows, :].astype(bf16), v_ref[:nk, :], (((1,), (1,)), ((), ())),
                                     preferred_element_type=f32)
                ds = p * (dp - delta_ref[rows, :1])
                dq = dq_ref[rows, :] + jnp.dot(ds.astype(bf16), k, preferred_element_type=f32)
                dq_ref[rows, :] = dq * LN2 if diagonal else dq

        @pl.when(ki < qi)
        def _():
            step(False)

        @pl.when(ki == qi)
        def _():
            step(True)

    qspec = lambda w: pl.BlockSpec((None, t, w), lambda h, qi, ki: (h, qi, 0))
    kspec = lambda w: pl.BlockSpec((None, t, w), lambda h, qi, ki: (h, jnp.minimum(ki, qi), 0))
    return pl.pallas_call(
        body, name="attn_bwd_dq", grid=(N_HEADS, nb, nb),
        in_specs=[qspec(HEAD_LANES), kspec(HEAD_LANES), kspec(V_DIM), qspec(V_DIM), qspec(128), qspec(128)],
        out_specs=qspec(HEAD_LANES),
        out_shape=jax.ShapeDtypeStruct((N_HEADS, s, HEAD_LANES), f32),
    )(q, k, v, do, lse, delta)


def attn_bwd_dkv(q, k, v, do, lse, delta):
    s = q.shape[1]
    t = _token_block(s)
    nb = s // t

    cb = min(ATTN_ROWS_BWD, t)

    def body(q_ref, k_ref, v_ref, do_ref, lse_ref, delta_ref, dk_ref, dv_ref):
        ki, qi = pl.program_id(1), pl.program_id(2)

        @pl.when(qi == 0)
        def _():
            dk_ref[...] = jnp.zeros(dk_ref.shape, f32)
            dv_ref[...] = jnp.zeros(dv_ref.shape, f32)

        def step(diagonal):
            for c in range(t // cb):
                keys = pl.ds(c * cb, cb)
                r0 = c * cb if diagonal else 0
                q = q_ref[r0:, :]
                do = do_ref[r0:, :].astype(bf16)
                sc = _scores(q, k_ref[keys, :])
                if diagonal:
                    sc = jnp.where(_tril(t - r0, cb, 0), sc, NEG)
                p = jnp.exp2(sc - lse_ref[r0:, :1])
                dp = lax.dot_general(do, v_ref[keys, :], (((1,), (1,)), ((), ())), preferred_element_type=f32)
                ds = p * (dp - delta_ref[r0:, :1])
                dv_ref[keys, :] += lax.dot_general(p.astype(bf16), do, (((0,), (0,)), ((), ())), preferred_element_type=f32)
                dk_ref[keys, :] += lax.dot_general(ds.astype(bf16), q, (((0,), (0,)), ((), ())), preferred_element_type=f32)

        @pl.when(qi > ki)
        def _():
            step(False)

        @pl.when(qi == ki)
        def _():
            step(True)

        @pl.when(qi == nb - 1)
        def _():
            dk_ref[...] = dk_ref[...] * LN2

    qspec = lambda w: pl.BlockSpec((None, t, w), lambda h, ki, qi: (h, jnp.maximum(qi, ki), 0))
    kspec = lambda w: pl.BlockSpec((None, t, w), lambda h, ki, qi: (h, ki, 0))
    return pl.pallas_call(
        body, name="attn_bwd_dkv", grid=(N_HEADS, nb, nb),
        in_specs=[qspec(HEAD_LANES), kspec(HEAD_LANES), kspec(V_DIM), qspec(V_DIM), qspec(128), qspec(128)],
        out_specs=[kspec(HEAD_LANES), kspec(V_DIM)],
        out_shape=[jax.ShapeDtypeStruct((N_HEADS, s, HEAD_LANES), f32), jax.ShapeDtypeStruct((N_HEADS, s, V_DIM), f32)],
    )(q, k, v, do, lse, delta)


def ssd_bwd(px, pz, plast, states, dyg, params):
    s = px.shape[0]
    nc = s // CHUNK
    per = CHUNK // HALO

    def body(px_ref, halo_ref, pz_ref, pl_ref, st_ref, dyg_ref, cw_ref, cb_ref, dtb_ref, alog_ref, dskip_ref, snw_ref,
             dpx_ref, dpz_ref, dpl_ref, dcw_ref, dcb_ref, ddtb_ref, dalog_ref, ddskip_ref, dsnw_ref, dstate_sc, dhalo_sc):
        t = pl.program_id(0)
        chunk = nc - 1 - t

        @pl.when(t == 0)
        def _():
            dstate_sc[...] = jnp.zeros(dstate_sc.shape, f32)
            dhalo_sc[...] = jnp.zeros(dhalo_sc.shape, f32)

        halo = jnp.where(chunk > 0, halo_ref[...], 0.0)
        xext = jnp.concatenate([halo, px_ref[...]], axis=0)
        _, vjp = jax.vjp(_f_ssd, xext, pz_ref[...], pl_ref[...], st_ref[...], cw_ref[...], cb_ref[...], dtb_ref[...],
                         alog_ref[...], dskip_ref[...], snw_ref[...])
        dxext, dz, dpl, dprev, dcw, dcb, ddtb, dalog, ddskip, dsnw = vjp((dyg_ref[...], dstate_sc[...]))
        dpx_ref[...] = dxext[HALO:]
        dpx_ref[CHUNK - HALO:, :] += dhalo_sc[...]
        dhalo_sc[...] = dxext[:HALO]
        dstate_sc[...] = dprev
        dpz_ref[...] = dz
        dpl_ref[...] = dpl
        _accumulate(t == 0, [dcw_ref, dcb_ref, ddtb_ref, dalog_ref, ddskip_ref, dsnw_ref],
                    [dcw, dcb, ddtb, dalog, ddskip, dsnw])

    rev = lambda w: pl.BlockSpec((CHUNK, w), lambda t: (nc - 1 - t, 0))
    pshapes = [jax.ShapeDtypeStruct((4, D_CONV), f32), jax.ShapeDtypeStruct((1, D_CONV), f32),
               jax.ShapeDtypeStruct((1, 128), f32), jax.ShapeDtypeStruct((1, 128), f32),
               jax.ShapeDtypeStruct((1, 128), f32), jax.ShapeDtypeStruct((1, D_SSD), f32)]
    return pl.pallas_call(
        body, name="ssd_bwd", grid=(nc,),
        in_specs=[rev(D_CONV),
                  pl.BlockSpec((HALO, D_CONV), lambda t: (jnp.maximum((nc - 1 - t) * per - 1, 0), 0)),
                  rev(D_SSD), rev(128),
                  pl.BlockSpec((None, N_HEADS, SSD_HEAD_DIM, SSD_STATE), lambda t: (nc - 1 - t, 0, 0, 0)),
                  rev(D_SSD)] + _ssd_param_specs(),
        out_specs=[rev(D_CONV), rev(D_SSD), rev(128)] + _ssd_param_specs(),
        out_shape=[jax.ShapeDtypeStruct((s, D_CONV), f32), jax.ShapeDtypeStruct((s, D_SSD), f32),
                   jax.ShapeDtypeStruct((s, 128), f32)] + pshapes,
        scratch_shapes=[pltpu.VMEM((N_HEADS, SSD_HEAD_DIM, SSD_STATE), f32), pltpu.VMEM((HALO, D_CONV), f32)],
    )(px, px, pz, plast, states, dyg, *params)


def qkv_bwd(pa, plast, cos_t, sin_t, params, dq, dk, dv):
    s = pa.shape[0]
    ts = _token_block(s)

    def body(pa_ref, pl_ref, cos_ref, sin_ref, *rest):
        prm = [r[...].astype(f32) for r in rest[:8]]
        dq_ref, dk_ref, dv_ref = rest[8:11]
        dpa_ref, dpl_ref = rest[11:13]
        dprm_refs = list(rest[13:])
        cos_t, sin_t = cos_ref[...], sin_ref[...]
        _, vjp = jax.vjp(lambda a, b, *p: _f_qkv(a, b, cos_t, sin_t, *p), pa_ref[...], pl_ref[...], *prm)
        grads = vjp((dq_ref[...], dk_ref[...], dv_ref[...]))
        dpa_ref[...] = grads[0]
        dpl_ref[...] = grads[1]
        _accumulate(pl.program_id(0) == 0, dprm_refs, list(grads[2:]))

    tok = lambda w: pl.BlockSpec((ts, w), lambda i: (i, 0))
    head = lambda w: pl.BlockSpec((N_HEADS, ts, w), lambda i: (0, i, 0))
    pshapes = [jax.ShapeDtypeStruct((1, Q_RANK), f32), jax.ShapeDtypeStruct((1, KV_RANK), f32),
               jax.ShapeDtypeStruct((N_HEADS, Q_RANK, HEAD_LANES), f32), jax.ShapeDtypeStruct((N_HEADS, KV_RANK, HEAD_LANES), f32),
               jax.ShapeDtypeStruct((N_HEADS, KV_RANK, V_DIM), f32), jax.ShapeDtypeStruct((1, HEAD_LANES), f32),
               jax.ShapeDtypeStruct((1, HEAD_LANES), f32), jax.ShapeDtypeStruct((1, HEAD_LANES), f32)]
    return pl.pallas_call(
        body, name="qkv_bwd", grid=(s // ts,),
        in_specs=[tok(384), tok(128), tok(128), tok(128)] + _qkv_param_specs()
                 + [head(HEAD_LANES), head(HEAD_LANES), head(V_DIM)],
        out_specs=[tok(384), tok(128)] + _qkv_param_specs(),
        out_shape=[jax.ShapeDtypeStruct((s, 384), f32), jax.ShapeDtypeStruct((s, 128), f32)] + pshapes,
    )(pa, plast, cos_t, sin_t, *params, dq, dk, dv)


def proj_bwd(x, nw, sh, sc, w, dpa, dpz, dpx, dpl_k, dpl_dt, dres):
    s = x.shape[0]
    ts = _token_block(s)

    ni = s // ts

    def body(x_ref, nw_ref, sh_ref, sc_ref, w_ref, dpa_ref, dpz_ref, dpx_ref, dplk_ref, dpld_ref, dres_ref,
             dx_ref, dnw_ref, dsh_ref, dsc_ref, dw_ref, acc_sc):
        i = pl.program_id(0)
        g = jnp.concatenate([dpa_ref[...], dpz_ref[...], dpx_ref[...], dplk_ref[...] + dpld_ref[...]], axis=1)
        _, vjp = jax.vjp(_f_proj, x_ref[...], nw_ref[...], sh_ref[...], sc_ref[...], w_ref[...].astype(f32))
        dx, dnw, dsh, dsc, dw = vjp(g)
        dx_ref[...] = dx + dres_ref[...]
        _accumulate(i == 0, [dnw_ref, dsh_ref, dsc_ref], [dnw, dsh, dsc])
        _accumulate_then_cast(i == 0, i == ni - 1, [acc_sc], [dw_ref], [dw])

    vec = _const((1, D_MODEL))
    vshape = jax.ShapeDtypeStruct((1, D_MODEL), f32)
    tok = lambda w_: pl.BlockSpec((ts, w_), lambda i: (i, 0))
    return pl.pallas_call(
        body, name="proj_bwd", grid=(ni,), scratch_shapes=[pltpu.VMEM((D_MODEL, D_PROJ), f32)],
        in_specs=[tok(D_MODEL), vec, vec, vec, _const((D_MODEL, D_PROJ)), tok(384), tok(512), tok(1024), tok(128), tok(128),
                  tok(D_MODEL)],
        out_specs=[tok(D_MODEL), vec, vec, vec, _const((D_MODEL, D_PROJ))],
        out_shape=[jax.ShapeDtypeStruct((s, D_MODEL), f32), vshape, vshape, vshape,
                   jax.ShapeDtypeStruct((D_MODEL, D_PROJ), bf16)],
    )(x, nw, sh, sc, w, dpa, dpz, dpx, dpl_k, dpl_dt, dres)


def ada_fwd(c_all, w_ada, b_cols):
    def body(c_ref, w_ref, b_ref, out_ref):
        act = jax.nn.silu(c_ref[...])
        for l in range(2):
            out_ref[l] = jnp.dot(act, w_ref[l], precision=lax.Precision.HIGHEST, preferred_element_type=f32) + b_ref[l]

    return pl.pallas_call(body, name="ada_fwd", out_shape=jax.ShapeDtypeStruct((2, N_DEV, 768), f32))(c_all, w_ada, b_cols)


def ada_bwd(c_all, dmod_cols):
    def body(c_ref, d_ref, out_ref):
        out_ref[0] = lax.dot_general(jax.nn.silu(c_ref[...]), d_ref[0], (((0,), (0,)), ((), ())),
                                     precision=lax.Precision.HIGHEST, preferred_element_type=f32)

    return pl.pallas_call(
        body, name="ada_bwd", grid=(2,),
        in_specs=[_const((N_DEV, D_MODEL)), pl.BlockSpec((1, N_DEV, 768), lambda l: (l, 0, 0))],
        out_specs=pl.BlockSpec((1, D_MODEL, 768), lambda l: (l, 0, 0)),
        out_shape=jax.ShapeDtypeStruct((2, D_MODEL, 768), f32),
    )(c_all, dmod_cols)


def _adamw(w, g, m, v):
    m = ADAM_B1 * m + (1.0 - ADAM_B1) * g
    v = ADAM_B2 * v + (1.0 - ADAM_B2) * (g * g)
    m_hat = m / (1.0 - ADAM_B1 ** ADAM_STEP)
    v_hat = v / (1.0 - ADAM_B2 ** ADAM_STEP)
    delta = -ADAM_LR * (m_hat / (jnp.sqrt(v_hat) + ADAM_EPS) + ADAM_WD * w)
    return delta, m, v


def adamw(parts, w, m, v, name):
    n, nl, r, c = parts.shape
    tr = r
    lanes = -(-c // 128) * 128
    if 2 * (n + 7) * r * lanes * 4 > ADAMW_BLOCK_BYTES:
        tr = next(t for t in (256, 128, 64, 32, 16, 8) if r % t == 0)

    def body(p_ref, w_ref, m_ref, v_ref, g_ref, d_ref, nm_ref, nv_ref):
        g = p_ref[0].astype(f32)
        for k in range(1, n):
            g = g + p_ref[k].astype(f32)
        delta, nm, nv = _adamw(w_ref[...], g, m_ref[...], v_ref[...])
        g_ref[...] = g
        d_ref[...] = delta
        nm_ref[...] = nm
        nv_ref[...] = nv

    blk = pl.BlockSpec((None, tr, c), lambda l, i: (l, i, 0))
    shp = jax.ShapeDtypeStruct((nl, r, c), f32)
    return pl.pallas_call(
        body, name=name, grid=(nl, r // tr),
        in_specs=[pl.BlockSpec((n, None, tr, c), lambda l, i: (0, l, i, 0)), blk, blk, blk],
        out_specs=[blk] * 4, out_shape=[shp] * 4,
    )(parts, w, m, v)


def _my_index():
    return 4 * lax.axis_index("x") + 2 * lax.axis_index("y") + lax.axis_index("c")


def _coords(idx):
    return (idx // 4, (idx // 2) % 2, idx % 2)


def all_gather(shards, name):
    n = len(shards)

    def body(*refs):
        ins, outs = refs[:n], refs[n:2 * n]
        send_sems, recv_sems, local_sems = refs[2 * n:]
        me = _my_index()
        local = [pltpu.make_async_copy(ins[k], outs[k].at[me], local_sems.at[k]) for k in range(n)]
        for cp in local:
            cp.start()
        for p in range(1, N_DEV):
            peer = (me + p) % N_DEV
            for k in range(n):
                pltpu.make_async_remote_copy(src_ref=ins[k], dst_ref=outs[k].at[me], send_sem=send_sems.at[k],
                                             recv_sem=recv_sems.at[k], device_id=_coords(peer), device_id_type=MESH).start()
        for k in range(n):
            seven = outs[k].at[pl.ds(0, N_DEV - 1)]
            drain = pltpu.make_async_remote_copy(src_ref=seven, dst_ref=seven, send_sem=send_sems.at[k],
                                                 recv_sem=recv_sems.at[k], device_id=_coords(me), device_id_type=MESH)
            drain.wait_recv()
            drain.wait_send()
        for cp in local:
            cp.wait()

    return pl.pallas_call(
        body, name=name,
        in_specs=[ANY] * n, out_specs=[ANY] * n,
        out_shape=[jax.ShapeDtypeStruct((N_DEV,) + tuple(a.shape), a.dtype) for a in shards],
        scratch_shapes=[pltpu.SemaphoreType.DMA((n,)), pltpu.SemaphoreType.DMA((n,)), pltpu.SemaphoreType.DMA((n,))],
    )(*shards)


def reduce_scatter_parts(tensors, name):
    n = len(tensors)
    flat, where = [], {}
    for k, layers in enumerate(tensors):
        for l, pieces in enumerate(layers):
            d = 0
            for piece in pieces:
                for b in range(piece.shape[0]):
                    where[k, l, d] = (len(flat), b)
                    d += 1
                flat.append(piece)
            assert d == N_DEV
    n_in = len(flat)

    def body(*refs):
        ins, outs = refs[:n_in], refs[n_in:n_in + n]
        send_sems, recv_sems, local_sems = refs[n_in + n:]
        me = _my_index()

        def block(k, l, d):
            i, b = where[k, l, d]
            return ins[i].at[b]

        for d in range(N_DEV):
            @pl.when(d != me)
            def _():
                for k in range(n):
                    for l in range(2):
                        pltpu.make_async_remote_copy(src_ref=block(k, l, d), dst_ref=outs[k].at[me, l],
                                                     send_sem=send_sems.at[k], recv_sem=recv_sems.at[k],
                                                     device_id=(d // 4, (d // 2) % 2, d % 2), device_id_type=MESH).start()

            @pl.when(d == me)
            def _():
                for k in range(n):
                    for l in range(2):
                        pltpu.make_async_copy(block(k, l, d), outs[k].at[d, l], local_sems.at[k, l]).start()

        for k in range(n):
            seven = outs[k].at[pl.ds(0, N_DEV - 1)]
            drain = pltpu.make_async_remote_copy(src_ref=seven, dst_ref=seven, send_sem=send_sems.at[k],
                                                 recv_sem=recv_sems.at[k], device_id=_coords(me), device_id_type=MESH)
            drain.wait_recv()
            drain.wait_send()
            for l in range(2):
                pltpu.make_async_copy(block(k, l, 0), outs[k].at[0, l], local_sems.at[k, l]).wait()

    return pl.pallas_call(
        body, name=name,
        in_specs=[ANY] * n_in, out_specs=[ANY] * n,
        out_shape=[jax.ShapeDtypeStruct((N_DEV, 2) + tuple(layers[0][0].shape[1:]), layers[0][0].dtype) for layers in tensors],
        scratch_shapes=[pltpu.SemaphoreType.DMA((n,)), pltpu.SemaphoreType.DMA((n,)), pltpu.SemaphoreType.DMA((n, 2))],
    )(*flat)


def _pad_lanes(v, lo, total=128):
    return jnp.pad(v, (lo, total - lo - v.shape[0]))[None, :]


def _layer_weights(lw):
    w_in = lw["w_in"]
    z = jnp.zeros((D_MODEL, 1), w_in.dtype)
    w_proj = jnp.concatenate(
        [w_in[:, :384], w_in[:, 416:928], w_in[:, 928:1952], w_in[:, 1952:1960], jnp.tile(z, (1, 56)),
         w_in[:, 384:416], jnp.tile(z, (1, 32))], axis=1)
    wq = jnp.pad(lw["w_q_up"], ((0, 0), (0, 0), (0, HEAD_LANES - NOPE - ROPE)))
    wk = jnp.pad(lw["w_kv_up"][:, :, :NOPE], ((0, 0), (0, 0), (0, HEAD_LANES - NOPE)))
    wv = lw["w_kv_up"][:, :, NOPE:]
    qkv = (lw["q_a_norm_w"][None, :], lw["kv_a_norm_w"][None, :], wq, wk, wv,
           _pad_lanes(jnp.concatenate([lw["q_nope_norm_w"], lw["q_pe_norm_w"]]), 0),
           _pad_lanes(lw["k_nope_norm_w"], 0), _pad_lanes(lw["k_pe_norm_w"], NOPE))
    ssd = (lw["conv_w"], lw["conv_b"][None, :], _pad_lanes(lw["dt_bias"], 0), _pad_lanes(lw["a_log"], 0),
           _pad_lanes(lw["d_skip"], 0), lw["ssd_norm_w"][None, :])
    return dict(w_proj=w_proj, qkv=qkv, ssd=ssd, wo=lw["w_out"], wgu=lw["w_gate_up"], wd=lw["w_down"],
                n1=lw["norm1_w"][None, :], n2=lw["norm2_w"][None, :])


def layer_fwd(x, mod, kw, cos_t, sin_t):
    sh1, sc1, g1, sh2, sc2, g2 = [mod[i:i + 1] for i in range(6)]
    pa, pz, px, plast = proj_fwd(x, kw["n1"], sh1, sc1, kw["w_proj"])
    q, k, v = qkv_fwd(pa, plast, cos_t, sin_t, kw["qkv"])
    o, lse = attn_fwd(q, k, v)
    yg, states = ssd_fwd(px, pz, plast, kw["ssd"])
    x_mid = out_fwd(x, o, yg, g1, kw["wo"])
    x_out = mlp_fwd(x_mid, kw["n2"], sh2, sc2, g2, kw["wgu"], kw["wd"])
    saved = dict(x=x, pa=pa, pz=pz, px=px, plast=plast, q=q, k=k, v=v, o=o, lse=lse, yg=yg, states=states, x_mid=x_mid)
    return x_out, saved


def layer_bwd(dy, mod, kw, cos_t, sin_t, sv):
    sh1, sc1, g1, sh2, sc2, g2 = [mod[i:i + 1] for i in range(6)]
    dparts, dn2, dsh2, dsc2, dg2, dwg, dwu, dwd = mlp_bwd(sv["x_mid"], dy, kw["n2"], sh2, sc2, g2, kw["wgu"], kw["wd"])
    dmid, do, delta, dyg, dg1, dwo = out_bwd(dy, dparts, sv["o"], sv["yg"], g1, kw["wo"])
    dq = attn_bwd_dq(sv["q"], sv["k"], sv["v"], do, sv["lse"], delta)
    dk, dv = attn_bwd_dkv(sv["q"], sv["k"], sv["v"], do, sv["lse"], delta)
    dpx, dpz, dpl_dt, dcw, dcb, ddtb, dalog, ddskip, dsnw = ssd_bwd(sv["px"], sv["pz"], sv["plast"], sv["states"], dyg, kw["ssd"])
    dpa, dpl_k, dqaw, dkvaw, dwq, dwk, dwv, dqnw, dknw, dkpw = qkv_bwd(sv["pa"], sv["plast"], cos_t, sin_t, kw["qkv"], dq, dk, dv)
    dx, dn1, dsh1, dsc1, dwp = proj_bwd(sv["x"], kw["n1"], sh1, sc1, kw["w_proj"], dpa, dpz, dpx, dpl_k, dpl_dt, dmid)
    dmod = jnp.concatenate([dsh1, dsc1, dg1, dsh2, dsc2, dg2], axis=0)
    dw_in = jnp.concatenate([dwp[:, :384], dwp[:, 1984:2016], dwp[:, 384:1920], dwp[:, 1920:1928]], axis=1)
    grads = dict(
        norm1_w=dn1[0], norm2_w=dn2[0], q_a_norm_w=dqaw[0], kv_a_norm_w=dkvaw[0],
        q_nope_norm_w=dqnw[0, :NOPE], q_pe_norm_w=dqnw[0, NOPE:NOPE + ROPE], k_nope_norm_w=dknw[0, :NOPE],
        k_pe_norm_w=dkpw[0, NOPE:NOPE + ROPE], conv_b=dcb[0], dt_bias=ddtb[0, :N_HEADS], a_log=dalog[0, :N_HEADS],
        d_skip=ddskip[0, :N_HEADS], ssd_norm_w=dsnw[0],
        w_in=[dw_in.reshape(D_MODEL, N_DEV, D_IN // N_DEV).transpose(1, 0, 2)],
        w_q_up=[dwq[:, :, :NOPE + ROPE].astype(bf16)],
        w_kv_up=[jnp.concatenate([dwk[:, :, :NOPE], dwv], axis=2).astype(bf16)],
        conv_w=[dcw.reshape(4, N_DEV, D_CONV // N_DEV).transpose(1, 0, 2).astype(bf16)],
        w_out=[dwo.reshape(N_DEV, D_MODEL // N_DEV, D_MODEL)],
        w_gate_up=[dwg, dwu],
        w_down=[dwd.reshape(N_DEV, D_FF // N_DEV, D_MODEL)],
    )
    return dx, dmod, grads


def _natural_weights(gathered):
    g = gathered
    return dict(
        w_in=g["w_in"].transpose(1, 0, 2).reshape(D_MODEL, D_IN),
        w_q_up=g["w_q_up"], w_kv_up=g["w_kv_up"],
        conv_w=g["conv_w"].astype(f32).transpose(1, 0, 2).reshape(4, D_CONV),
        w_out=g["w_out"].reshape(D_MODEL, D_MODEL),
        w_gate_up=g["w_gate_up"],
        w_down=g["w_down"].reshape(N_DEV // 2, FF_SHARD, D_MODEL),
    )


def _pack_small(get):
    flat = jnp.concatenate([get(name).reshape(-1) for name, _ in SMALL])
    return jnp.pad(flat, (0, SMALL_ROWS * 128 - flat.shape[0])).reshape(SMALL_ROWS, 128)


def _unpack_small(packed):
    flat = packed.reshape(-1)
    out, off = {}, 0
    for name, size in SMALL:
        out[name] = flat[off:off + 2 * size].reshape(2, size)
        off += 2 * size
    return out


def kernel(x, c, positions, norm1_w, norm2_w, w_ada, b_ada, w_in, q_a_norm_w, w_q_up, kv_a_norm_w, w_kv_up, q_nope_norm_w, q_pe_norm_w, k_nope_norm_w, k_pe_norm_w, conv_w, conv_b, dt_bias, a_log, d_skip, ssd_norm_w, w_out, w_gate_up, w_down, loss_target, m_norm1_w, m_norm2_w, m_w_ada, m_b_ada, m_w_in, m_q_a_norm_w, m_w_q_up, m_kv_a_norm_w, m_w_kv_up, m_q_nope_norm_w, m_q_pe_norm_w, m_k_nope_norm_w, m_k_pe_norm_w, m_conv_w, m_conv_b, m_dt_bias, m_a_log, m_d_skip, m_ssd_norm_w, m_w_out, m_w_gate_up, m_w_down, v_norm1_w, v_norm2_w, v_w_ada, v_b_ada, v_w_in, v_q_a_norm_w, v_w_q_up, v_kv_a_norm_w, v_w_kv_up, v_q_nope_norm_w, v_q_pe_norm_w, v_k_nope_norm_w, v_k_pe_norm_w, v_conv_w, v_conv_b, v_dt_bias, v_a_log, v_d_skip, v_ssd_norm_w, v_w_out, v_w_gate_up, v_w_down):
    w = dict(norm1_w=norm1_w, norm2_w=norm2_w, w_ada=w_ada, b_ada=b_ada, w_in=w_in, q_a_norm_w=q_a_norm_w, w_q_up=w_q_up,
             kv_a_norm_w=kv_a_norm_w, w_kv_up=w_kv_up, q_nope_norm_w=q_nope_norm_w, q_pe_norm_w=q_pe_norm_w,
             k_nope_norm_w=k_nope_norm_w, k_pe_norm_w=k_pe_norm_w, conv_w=conv_w, conv_b=conv_b, dt_bias=dt_bias,
             a_log=a_log, d_skip=d_skip, ssd_norm_w=ssd_norm_w, w_out=w_out, w_gate_up=w_gate_up, w_down=w_down)
    m = dict(norm1_w=m_norm1_w, norm2_w=m_norm2_w, w_ada=m_w_ada, b_ada=m_b_ada, w_in=m_w_in, q_a_norm_w=m_q_a_norm_w,
             w_q_up=m_w_q_up, kv_a_norm_w=m_kv_a_norm_w, w_kv_up=m_w_kv_up, q_nope_norm_w=m_q_nope_norm_w,
             q_pe_norm_w=m_q_pe_norm_w, k_nope_norm_w=m_k_nope_norm_w, k_pe_norm_w=m_k_pe_norm_w, conv_w=m_conv_w,
             conv_b=m_conv_b, dt_bias=m_dt_bias, a_log=m_a_log, d_skip=m_d_skip, ssd_norm_w=m_ssd_norm_w, w_out=m_w_out,
             w_gate_up=m_w_gate_up, w_down=m_w_down)
    v = dict(norm1_w=v_norm1_w, norm2_w=v_norm2_w, w_ada=v_w_ada, b_ada=v_b_ada, w_in=v_w_in, q_a_norm_w=v_q_a_norm_w,
             w_q_up=v_w_q_up, kv_a_norm_w=v_kv_a_norm_w, w_kv_up=v_w_kv_up, q_nope_norm_w=v_q_nope_norm_w,
             q_pe_norm_w=v_q_pe_norm_w, k_nope_norm_w=v_k_nope_norm_w, k_pe_norm_w=v_k_pe_norm_w, conv_w=v_conv_w,
             conv_b=v_conv_b, dt_bias=v_dt_bias, a_log=v_a_log, d_skip=v_d_skip, ssd_norm_w=v_ssd_norm_w, w_out=v_w_out,
             w_gate_up=v_w_gate_up, w_down=v_w_down)
    me = _my_index()
    seq = x.shape[1]

    shards = [c] + [w[name][l] if name == "conv_w" else w[name][l].astype(bf16) for l in range(2) for name in BIG]
    gathered = all_gather(shards, "gather_weights")
    c_all = gathered[0].reshape(N_DEV, D_MODEL)
    kws = []
    for l in range(2):
        g = {name: gathered[1 + l * len(BIG) + i] for i, name in enumerate(BIG)}
        lw = _natural_weights(g)
        for name, _ in SMALL:
            if name != "b_ada":
                lw[name] = w[name][l]
        kws.append(_layer_weights(lw))

    b_cols = lax.dynamic_slice_in_dim(b_ada, me * 768, 768, axis=1)
    mod_cols = ada_fwd(c_all, w_ada, b_cols)
    (mod_all,) = all_gather([mod_cols], "gather_mod")
    mod_me = lax.dynamic_index_in_dim(mod_all, me, axis=2, keepdims=False)
    mods = [mod_me[:, l, :].reshape(6, D_MODEL) for l in range(2)]

    inv_freq = 1.0 / (ROPE_THETA ** (jnp.arange(0, ROPE, 2, dtype=f32) / ROPE))
    inv = _pad_lanes(jnp.concatenate([inv_freq, inv_freq]), NOPE)
    cos_t, sin_t = rope_tables(positions.reshape(seq, 1), inv)

    h = x[0]
    saved = []
    for l in range(2):
        h, sv = layer_fwd(h, mods[l], kws[l], cos_t, sin_t)
        saved.append(sv)
    dy, loss_part = loss_fwd(h, loss_target[0])
    loss = lax.psum(loss_part[0, 0], ("x", "y", "c"))

    grads, dmods = [None, None], [None, None]
    for l in (1, 0):
        dy, dmods[l], grads[l] = layer_bwd(dy, mods[l], kws[l], cos_t, sin_t, saved[l])
    grad_x = dy[None]

    small_part = {name: jnp.stack([grads[0][name], grads[1][name]]) for name, _ in SMALL if name != "b_ada"}
    small_part["b_ada"] = jnp.stack([dmods[0].reshape(-1), dmods[1].reshape(-1)])
    (small_all,) = all_gather([_pack_small(lambda n: small_part[n])], "gather_small_grads")
    sg, sd, sm, sv_ = adamw(small_all.reshape(N_DEV, 1, SMALL_ROWS, 128), _pack_small(lambda n: w[n])[None],
                            _pack_small(lambda n: m[n])[None], _pack_small(lambda n: v[n])[None], "adamw_small")
    res = {}
    for key, packed in (("g", sg), ("d", sd), ("m", sm), ("v", sv_)):
        for name, arr in _unpack_small(packed[0]).items():
            res[key, name] = arr

    off = 2 * (1024 + 1024)
    dmod_all = small_all.reshape(N_DEV, -1)[:, off:off + 2 * 6144].reshape(N_DEV, 2, 6144)
    dmod_cols = lax.dynamic_slice_in_dim(dmod_all, me * 768, 768, axis=2).transpose(1, 0, 2)
    g_ada = ada_bwd(c_all, dmod_cols)
    res["g", "w_ada"], res["d", "w_ada"], res["m", "w_ada"], res["v", "w_ada"] = adamw(
        g_ada[None], w_ada, m_w_ada, v_w_ada, "adamw_w_ada")

    parts = reduce_scatter_parts([[grads[0][name], grads[1][name]] for name in BIG], "scatter_grads")
    for name, part in zip(BIG, parts):
        res["g", name], res["d", name], res["m", name], res["v", name] = adamw(part, w[name], m[name], v[name], "adamw_" + name)

    return (loss, grad_x, *[res["g", n] for n in WEIGHTS], *[res["d", n] for n in WEIGHTS],
            *[res["m", n] for n in WEIGHTS], *[res["v", n] for n in WEIGHTS])
```

```python
import functools

import jax
import jax.numpy as jnp
from jax import lax
from jax.experimental import pallas as pl
from jax.experimental.pallas import tpu as pltpu

f32 = jnp.float32
bf16 = jnp.bfloat16

N_DEV = 8
D_MODEL = 1024
N_HEADS = 8
HEAD_LANES = 128
NOPE = 64
ROPE = 32
V_DIM = 64
Q_RANK = 256
KV_RANK = 128
D_SSD = 512
D_CONV = 1024
SSD_STATE = 128
SSD_HEAD_DIM = 64
CHUNK = 128
HALO = 8
D_FF = 2816
FF_SHARD = 704
D_IN = 1960
D_PROJ = 2048
EPS = 1e-6
LOG2E = 1.4426950408889634
LN2 = 0.6931471805599453
Q_SCALE = (NOPE + ROPE) ** -0.5 * LOG2E
ATTN_ROWS_FWD = 256
ATTN_ROWS_BWD = 512
ROPE_THETA = 10000.0
NEG = -1e30

ADAM_LR = 0.001
ADAM_B1 = 0.9
ADAM_B2 = 0.999
ADAM_EPS = 1e-08
ADAM_WD = 0.01
ADAM_STEP = 10
ADAMW_BLOCK_BYTES = 24 << 20

MESH = pl.DeviceIdType.MESH
ANY = pl.BlockSpec(memory_space=pl.ANY)

SMALL = (("norm1_w", 1024), ("norm2_w", 1024), ("b_ada", 6144), ("q_a_norm_w", 256), ("kv_a_norm_w", 128),
         ("q_nope_norm_w", 64), ("q_pe_norm_w", 32), ("k_nope_norm_w", 64), ("k_pe_norm_w", 32),
         ("conv_b", 1024), ("dt_bias", 8), ("a_log", 8), ("d_skip", 8), ("ssd_norm_w", 512))
SMALL_ROWS = 168
BIG = ("w_in", "w_q_up", "w_kv_up", "conv_w", "w_out", "w_gate_up", "w_down")
WEIGHTS = ("norm1_w", "norm2_w", "w_ada", "b_ada", "w_in", "q_a_norm_w", "w_q_up", "kv_a_norm_w", "w_kv_up",
           "q_nope_norm_w", "q_pe_norm_w", "k_nope_norm_w", "k_pe_norm_w", "conv_w", "conv_b", "dt_bias",
           "a_log", "d_skip", "ssd_norm_w", "w_out", "w_gate_up", "w_down")


def _dot(a, b, ca, cb):
    return lax.dot_general(a.astype(bf16), b.astype(bf16), (((ca,), (cb,)), ((), ())), preferred_element_type=f32)


@jax.custom_vjp
def mm(a, b):
    return _dot(a, b, 1, 0)


def _mm_fwd(a, b):
    return _dot(a, b, 1, 0), (a, b)


def _mm_bwd(res, g):
    a, b = res
    return _dot(g, b, 1, 1).astype(a.dtype), _dot(a, g, 0, 0).astype(b.dtype)


mm.defvjp(_mm_fwd, _mm_bwd)


@jax.custom_vjp
def mm_nt(a, b):
    return _dot(a, b, 1, 1)


def _mm_nt_fwd(a, b):
    return _dot(a, b, 1, 1), (a, b)


def _mm_nt_bwd(res, g):
    a, b = res
    return _dot(g, b, 1, 0).astype(a.dtype), _dot(g, a, 0, 0).astype(b.dtype)


mm_nt.defvjp(_mm_nt_fwd, _mm_nt_bwd)


@jax.custom_vjp
def mm_tn(a, b):
    return _dot(a, b, 0, 0)


def _mm_tn_fwd(a, b):
    return _dot(a, b, 0, 0), (a, b)


def _mm_tn_bwd(res, g):
    a, b = res
    return _dot(b, g, 1, 1).astype(a.dtype), _dot(a, g, 1, 0).astype(b.dtype)


mm_tn.defvjp(_mm_tn_fwd, _mm_tn_bwd)


def _rms(x, w):
    return x * lax.rsqrt(jnp.mean(x * x, axis=-1, keepdims=True) + EPS) * w


def _const(shape):
    n = len(shape)
    return pl.BlockSpec(shape, lambda *_: (0,) * n)


def _accumulate(first, refs, vals):
    @pl.when(first)
    def _():
        for r, v in zip(refs, vals):
            r[...] = v

    @pl.when(jnp.logical_not(first))
    def _():
        for r, v in zip(refs, vals):
            r[...] += v


def _accumulate_then_cast(first, last, accs, outs, vals):
    _accumulate(first, accs, vals)

    @pl.when(last)
    def _():
        for a, o in zip(accs, outs):
            o[...] = a[...].astype(o.dtype)


def _token_block(s):
    return min(512, s)


def _f_proj(x, nw, sh, sc, w):
    h = _rms(x, nw) * (1.0 + sc) + sh
    return mm(h, w)


def _f_qkv(pa, plast, cos_t, sin_t, qaw, kvaw, wq, wk, wv, qnw, knw, kpw):
    lane = lax.broadcasted_iota(jnp.int32, (1, HEAD_LANES), 1)
    m_nope = lane < NOPE
    m_pe = (lane >= NOPE) & (lane < NOPE + ROPE)
    rows = pa.shape[0]

    def rope(t):
        half = ROPE // 2
        swapped = jnp.concatenate(
            [jnp.zeros((rows, NOPE), f32), t[:, NOPE + half:NOPE + ROPE], t[:, NOPE:NOPE + half],
             jnp.zeros((rows, HEAD_LANES - NOPE - ROPE), f32)], axis=1)
        return t * cos_t + swapped * sin_t

    qa = _rms(pa[:, :Q_RANK], qaw)
    kva = _rms(pa[:, Q_RANK:Q_RANK + KV_RANK], kvaw)
    kp = jnp.where(m_pe, plast, 0.0)
    kp = kp * lax.rsqrt(jnp.sum(kp * kp, axis=-1, keepdims=True) / ROPE + EPS) * kpw
    k_rot = rope(kp)
    qs, ks, vs = [], [], []
    for h in range(N_HEADS):
        qh = mm(qa, wq[h])
        ss_n = jnp.sum(jnp.where(m_nope, qh * qh, 0.0), axis=-1, keepdims=True) / NOPE
        ss_p = jnp.sum(jnp.where(m_pe, qh * qh, 0.0), axis=-1, keepdims=True) / ROPE
        r = jnp.where(m_nope, lax.rsqrt(ss_n + EPS), lax.rsqrt(ss_p + EPS))
        qs.append(rope(qh * r * qnw) * Q_SCALE)
        kh = mm(kva, wk[h])
        kh = kh * lax.rsqrt(jnp.sum(kh * kh, axis=-1, keepdims=True) / NOPE + EPS) * knw
        ks.append(kh + k_rot)
        vs.append(mm(kva, wv[h]))
    return jnp.stack(qs), jnp.stack(ks), jnp.stack(vs)


def _f_ssd(xext, z, plast, prev, cw, cb, dtb, alog, dskip, snw):
    n = CHUNK
    conv = cb
    for k in range(4):
        conv = conv + cw[k:k + 1] * xext[HALO - 3 + k:HALO - 3 + k + n]
    xc = jax.nn.silu(conv)
    xs, bm, cm = xc[:, :D_SSD], xc[:, D_SSD:D_SSD + 2 * SSD_STATE], xc[:, D_SSD + 2 * SSD_STATE:]
    lane = lax.broadcasted_iota(jnp.int32, (1, 128), 1)
    dt = jax.nn.softplus(jnp.where(lane < N_HEADS, plast, 0.0) + dtb)
    adt = dt * (-jnp.exp(alog))
    row = lax.broadcasted_iota(jnp.int32, (n, n), 0)
    col = lax.broadcasted_iota(jnp.int32, (n, n), 1)
    tri = row >= col
    acs = jnp.dot(tri.astype(f32), adt, precision=lax.Precision.HIGHEST, preferred_element_type=f32)
    acs_t = acs.T
    ys, news = [], []
    for g in range(2):
        bg = bm[:, g * SSD_STATE:(g + 1) * SSD_STATE]
        cg = cm[:, g * SSD_STATE:(g + 1) * SSD_STATE]
        cb_t = mm_nt(cg, bg)
        for r in range(4):
            h = g * 4 + r
            a_col = acs[:, h:h + 1]
            a_row = acs_t[h:h + 1, :]
            decay_ls = jnp.exp(jnp.where(tri, a_col - a_row, -jnp.inf))
            xh = xs[:, h * SSD_HEAD_DIM:(h + 1) * SSD_HEAD_DIM]
            xdt = xh * dt[:, h:h + 1]
            y_diag = mm(cb_t * decay_ls, xdt)
            a_last = acs[n - 1:n, h:h + 1]
            st = mm_tn(xdt * jnp.exp(a_last - a_col), bg)
            news.append(jnp.exp(a_last) * prev[h] + st)
            y_off = mm_nt(cg, prev[h]) * jnp.exp(a_col)
            ys.append(y_diag + y_off + dskip[:, h:h + 1] * xh)
    y = jnp.concatenate(ys, axis=1)
    yg = y * jax.nn.silu(z)
    half = D_SSD // 2
    outs = []
    for g in range(2):
        t = yg[:, g * half:(g + 1) * half]
        outs.append(t * lax.rsqrt(jnp.mean(t * t, axis=-1, keepdims=True) + EPS))
    return jnp.concatenate(outs, axis=1) * snw, jnp.stack(news)


def _f_out(o, yg, g1, wo):
    cat = jnp.concatenate([o[h] for h in range(N_HEADS)] + [yg], axis=1)
    return g1 * mm(cat, wo)


def _f_mlp(x, nw, sh, sc, g2, wg, wu, wd):
    h = _rms(x, nw) * (1.0 + sc) + sh
    act = jax.nn.silu(mm(h, wg)) * mm(h, wu)
    return g2 * mm(act, wd)


def proj_fwd(x, nw, sh, sc, w):
    s = x.shape[0]
    ts = _token_block(s)

    def body(x_ref, nw_ref, sh_ref, sc_ref, w_ref, pa_ref, pz_ref, px_ref, pl_ref):
        p = _f_proj(x_ref[...], nw_ref[...], sh_ref[...], sc_ref[...], w_ref[...])
        pa_ref[...] = p[:, :384]
        pz_ref[...] = p[:, 384:896]
        px_ref[...] = p[:, 896:1920]
        pl_ref[...] = p[:, 1920:]

    vec = _const((1, D_MODEL))
    return pl.pallas_call(
        body, name="proj_fwd", grid=(s // ts,),
        in_specs=[pl.BlockSpec((ts, D_MODEL), lambda i: (i, 0)), vec, vec, vec, _const((D_MODEL, D_PROJ))],
        out_specs=[pl.BlockSpec((ts, 384), lambda i: (i, 0)), pl.BlockSpec((ts, 512), lambda i: (i, 0)),
                   pl.BlockSpec((ts, 1024), lambda i: (i, 0)), pl.BlockSpec((ts, 128), lambda i: (i, 0))],
        out_shape=[jax.ShapeDtypeStruct((s, 384), f32), jax.ShapeDtypeStruct((s, 512), f32),
                   jax.ShapeDtypeStruct((s, 1024), f32), jax.ShapeDtypeStruct((s, 128), f32)],
    )(x, nw, sh, sc, w)


def rope_tables(pos, inv):
    s = pos.shape[0]
    ts = _token_block(s)

    def body(pos_ref, inv_ref, cos_ref, sin_ref):
        ang = pos_ref[...].astype(f32) * inv_ref[...]
        lane = lax.broadcasted_iota(jnp.int32, (1, HEAD_LANES), 1)
        half = ROPE // 2
        cos_ref[...] = jnp.where(lane < NOPE, 1.0, jnp.where(lane < NOPE + ROPE, jnp.cos(ang), 0.0))
        sn = jnp.sin(ang)
        sin_ref[...] = jnp.where((lane >= NOPE) & (lane < NOPE + half), -sn,
                                 jnp.where((lane >= NOPE + half) & (lane < NOPE + ROPE), sn, 0.0))

    return pl.pallas_call(
        body, name="rope_tables", grid=(s // ts,),
        in_specs=[pl.BlockSpec((ts, 1), lambda i: (i, 0)), _const((1, HEAD_LANES))],
        out_specs=[pl.BlockSpec((ts, HEAD_LANES), lambda i: (i, 0))] * 2,
        out_shape=[jax.ShapeDtypeStruct((s, HEAD_LANES), f32)] * 2,
    )(pos, inv)


def _qkv_param_specs():
    return [_const((1, Q_RANK)), _const((1, KV_RANK)), _const((N_HEADS, Q_RANK, HEAD_LANES)),
            _const((N_HEADS, KV_RANK, HEAD_LANES)), _const((N_HEADS, KV_RANK, V_DIM)),
            _const((1, HEAD_LANES)), _const((1, HEAD_LANES)), _const((1, HEAD_LANES))]


def qkv_fwd(pa, plast, cos_t, sin_t, params):
    s = pa.shape[0]
    ts = _token_block(s)

    def body(pa_ref, pl_ref, cos_ref, sin_ref, *rest):
        prm = [r[...] for r in rest[:8]]
        q_ref, k_ref, v_ref = rest[8:]
        q, k, v = _f_qkv(pa_ref[...], pl_ref[...], cos_ref[...], sin_ref[...], *prm)
        q_ref[...] = q.astype(bf16)
        k_ref[...] = k.astype(bf16)
        v_ref[...] = v.astype(bf16)

    tok = lambda w: pl.BlockSpec((ts, w), lambda i: (i, 0))
    return pl.pallas_call(
        body, name="qkv_fwd", grid=(s // ts,),
        in_specs=[tok(384), tok(128), tok(128), tok(128)] + _qkv_param_specs(),
        out_specs=[pl.BlockSpec((N_HEADS, ts, HEAD_LANES), lambda i: (0, i, 0)),
                   pl.BlockSpec((N_HEADS, ts, HEAD_LANES), lambda i: (0, i, 0)),
                   pl.BlockSpec((N_HEADS, ts, V_DIM), lambda i: (0, i, 0))],
        out_shape=[jax.ShapeDtypeStruct((N_HEADS, s, HEAD_LANES), bf16), jax.ShapeDtypeStruct((N_HEADS, s, HEAD_LANES), bf16),
                   jax.ShapeDtypeStruct((N_HEADS, s, V_DIM), bf16)],
    )(pa, plast, cos_t, sin_t, *params)


def _scores(q, k):
    return lax.dot_general(q, k, (((1,), (1,)), ((), ())), preferred_element_type=f32)


def _tril(rows, cols, row_offset):
    row = row_offset + lax.broadcasted_iota(jnp.int32, (rows, cols), 0)
    col = lax.broadcasted_iota(jnp.int32, (rows, cols), 1)
    return row >= col


def attn_fwd(q, k, v):
    s = q.shape[1]
    t = _token_block(s)
    nb = s // t

    rb = min(ATTN_ROWS_FWD, t)

    def body(q_ref, k_ref, v_ref, o_ref, lse_ref, m_sc, l_sc, acc_sc):
        qi = pl.program_id(1)
        m_sc[...] = jnp.full(m_sc.shape, NEG, f32)
        l_sc[...] = jnp.zeros(l_sc.shape, f32)
        acc_sc[...] = jnp.zeros(acc_sc.shape, f32)

        def step(k0, diagonal):
            for r in range(t // rb):
                rows = pl.ds(r * rb, rb)
                nk = (r + 1) * rb if diagonal else t
                keys = pl.ds(k0, nk)
                sc = _scores(q_ref[rows, :], k_ref[keys, :])
                if diagonal:
                    sc = jnp.where(_tril(rb, nk, r * rb), sc, NEG)
                m_prev = m_sc[rows, :1]
                m_new = jnp.maximum(m_prev, jnp.max(sc, axis=-1, keepdims=True))
                p = jnp.exp2(sc - m_new)
                alpha = jnp.exp2(m_prev - m_new)
                l_new = alpha * l_sc[rows, :1] + jnp.sum(p, axis=-1, keepdims=True)
                acc = alpha * acc_sc[rows, :] + jnp.dot(p.astype(bf16), v_ref[keys, :], preferred_element_type=f32)
                if diagonal:
                    o_ref[rows, :] = acc / l_new
                    lse_ref[rows, :] = jnp.broadcast_to(m_new + jnp.log2(l_new), (rb, 128))
                else:
                    acc_sc[rows, :] = acc
                    m_sc[rows, :] = jnp.broadcast_to(m_new, (rb, 128))
                    l_sc[rows, :] = jnp.broadcast_to(l_new, (rb, 128))

        def below(ki, carry):
            step(pl.multiple_of(ki * t, t), False)
            return carry

        lax.fori_loop(0, qi, below, 0)
        step(pl.multiple_of(qi * t, t), True)

    return pl.pallas_call(
        body, name="attn_fwd", grid=(N_HEADS, nb),
        in_specs=[pl.BlockSpec((None, t, HEAD_LANES), lambda h, qi: (h, qi, 0)),
                  pl.BlockSpec((None, s, HEAD_LANES), lambda h, qi: (h, 0, 0)),
                  pl.BlockSpec((None, s, V_DIM), lambda h, qi: (h, 0, 0))],
        out_specs=[pl.BlockSpec((None, t, V_DIM), lambda h, qi: (h, qi, 0)),
                   pl.BlockSpec((None, t, 128), lambda h, qi: (h, qi, 0))],
        out_shape=[jax.ShapeDtypeStruct((N_HEADS, s, V_DIM), f32), jax.ShapeDtypeStruct((N_HEADS, s, 128), f32)],
        scratch_shapes=[pltpu.VMEM((t, 128), f32), pltpu.VMEM((t, 128), f32), pltpu.VMEM((t, V_DIM), f32)],
    )(q, k, v)


def _ssd_param_specs():
    return [_const((4, D_CONV)), _const((1, D_CONV)), _const((1, 128)), _const((1, 128)), _const((1, 128)),
            _const((1, D_SSD))]


def ssd_fwd(px, pz, plast, params):
    s = px.shape[0]
    nc = s // CHUNK

    def body(px_ref, pz_ref, pl_ref, cw_ref, cb_ref, dtb_ref, alog_ref, dskip_ref, snw_ref, yg_ref, st_ref,
             state_sc, halo_sc):
        i = pl.program_id(0)

        @pl.when(i == 0)
        def _():
            state_sc[...] = jnp.zeros(state_sc.shape, f32)
            halo_sc[...] = jnp.zeros(halo_sc.shape, f32)

        x = px_ref[...]
        prev = state_sc[...]
        st_ref[...] = prev
        xext = jnp.concatenate([halo_sc[...], x], axis=0)
        yg, new = _f_ssd(xext, pz_ref[...], pl_ref[...], prev, cw_ref[...], cb_ref[...], dtb_ref[...],
                         alog_ref[...], dskip_ref[...], snw_ref[...])
        yg_ref[...] = yg
        state_sc[...] = new
        halo_sc[...] = x[CHUNK - HALO:]

    tok = lambda w: pl.BlockSpec((CHUNK, w), lambda i: (i, 0))
    return pl.pallas_call(
        body, name="ssd_fwd", grid=(nc,),
        in_specs=[tok(D_CONV), tok(D_SSD), tok(128)] + _ssd_param_specs(),
        out_specs=[tok(D_SSD), pl.BlockSpec((None, N_HEADS, SSD_HEAD_DIM, SSD_STATE), lambda i: (i, 0, 0, 0))],
        out_shape=[jax.ShapeDtypeStruct((s, D_SSD), f32),
                   jax.ShapeDtypeStruct((nc, N_HEADS, SSD_HEAD_DIM, SSD_STATE), f32)],
        scratch_shapes=[pltpu.VMEM((N_HEADS, SSD_HEAD_DIM, SSD_STATE), f32), pltpu.VMEM((HALO, D_CONV), f32)],
    )(px, pz, plast, *params)


def out_fwd(x, o, yg, g1, wo):
    s = x.shape[0]
    ts = _token_block(s)

    def body(x_ref, o_ref, yg_ref, g1_ref, wo_ref, out_ref):
        out_ref[...] = x_ref[...] + _f_out(o_ref[...], yg_ref[...], g1_ref[...], wo_ref[...])

    return pl.pallas_call(
        body, name="out_fwd", grid=(s // ts,),
        in_specs=[pl.BlockSpec((ts, D_MODEL), lambda i: (i, 0)), pl.BlockSpec((N_HEADS, ts, V_DIM), lambda i: (0, i, 0)),
                  pl.BlockSpec((ts, D_SSD), lambda i: (i, 0)), _const((1, D_MODEL)), _const((D_MODEL, D_MODEL))],
        out_specs=pl.BlockSpec((ts, D_MODEL), lambda i: (i, 0)),
        out_shape=jax.ShapeDtypeStruct((s, D_MODEL), f32),
    )(x, o, yg, g1, wo)


def mlp_fwd(x, nw, sh, sc, g2, wgu, wd):
    s = x.shape[0]
    ts = _token_block(s)
    nj = N_DEV // 2

    def body(x_ref, nw_ref, sh_ref, sc_ref, g2_ref, wg_ref, wu_ref, wd_ref, out_ref, acc_sc):
        j = pl.program_id(1)
        part = _f_mlp(x_ref[...], nw_ref[...], sh_ref[...], sc_ref[...], g2_ref[...], wg_ref[...], wu_ref[...], wd_ref[...])

        @pl.when(j == 0)
        def _():
            acc_sc[...] = x_ref[...] + part

        @pl.when(j > 0)
        def _():
            acc_sc[...] += part

        @pl.when(j == nj - 1)
        def _():
            out_ref[...] = acc_sc[...]

    vec = _const((1, D_MODEL))
    return pl.pallas_call(
        body, name="mlp_fwd", grid=(s // ts, nj),
        in_specs=[pl.BlockSpec((ts, D_MODEL), lambda i, j: (i, 0)), vec, vec, vec, vec,
                  pl.BlockSpec((None, D_MODEL, FF_SHARD), lambda i, j: (j, 0, 0)),
                  pl.BlockSpec((None, D_MODEL, FF_SHARD), lambda i, j: (j + nj, 0, 0)),
                  pl.BlockSpec((None, FF_SHARD, D_MODEL), lambda i, j: (j, 0, 0))],
        out_specs=pl.BlockSpec((ts, D_MODEL), lambda i, j: (i, 0)),
        out_shape=jax.ShapeDtypeStruct((s, D_MODEL), f32),
        scratch_shapes=[pltpu.VMEM((ts, D_MODEL), f32)],
    )(x, nw, sh, sc, g2, wgu, wgu, wd)


def loss_fwd(y, target):
    s = y.shape[0]
    ts = _token_block(s)

    def body(y_ref, t_ref, dy_ref, loss_ref):
        d = y_ref[...] - t_ref[...]
        dy_ref[...] = d * (1.0 / D_MODEL)
        part = 0.5 * jnp.sum(jnp.sum(d * d, axis=-1, keepdims=True) * (1.0 / D_MODEL), axis=0, keepdims=True)
        _accumulate(pl.program_id(0) == 0, [loss_ref], [jnp.broadcast_to(part, (8, 128))])

    return pl.pallas_call(
        body, name="loss_fwd", grid=(s // ts,),
        in_specs=[pl.BlockSpec((ts, D_MODEL), lambda i: (i, 0))] * 2,
        out_specs=[pl.BlockSpec((ts, D_MODEL), lambda i: (i, 0)), _const((8, 128))],
        out_shape=[jax.ShapeDtypeStruct((s, D_MODEL), f32), jax.ShapeDtypeStruct((8, 128), f32)],
    )(y, target)


def mlp_bwd(x, dy, nw, sh, sc, g2, wgu, wd):
    s = x.shape[0]
    ts = min(256, s)
    nj = N_DEV // 2
    ni = s // ts

    def body(x_ref, dy_ref, nw_ref, sh_ref, sc_ref, g2_ref, wg_ref, wu_ref, wd_ref,
             dx_ref, dnw_ref, dsh_ref, dsc_ref, dg2_ref, dwg_ref, dwu_ref, dwd_ref, ag_sc, au_sc, ad_sc):
        j, i = pl.program_id(0), pl.program_id(1)
        _, vjp = jax.vjp(_f_mlp, x_ref[...], nw_ref[...], sh_ref[...], sc_ref[...], g2_ref[...],
                         wg_ref[...].astype(f32), wu_ref[...].astype(f32), wd_ref[...].astype(f32))
        dx, dnw, dsh, dsc, dg2, dwg, dwu, dwd = vjp(dy_ref[...])
        dx_ref[...] = dx
        _accumulate((i == 0) & (j == 0), [dnw_ref, dsh_ref, dsc_ref, dg2_ref], [dnw, dsh, dsc, dg2])
        _accumulate_then_cast(i == 0, i == ni - 1, [ag_sc, au_sc, ad_sc], [dwg_ref, dwu_ref, dwd_ref], [dwg, dwu, dwd])

    vec = _const((1, D_MODEL))
    vshape = jax.ShapeDtypeStruct((1, D_MODEL), f32)
    wspec = lambda off: pl.BlockSpec((None, D_MODEL, FF_SHARD), lambda j, i: (j + off, 0, 0))
    dspec = pl.BlockSpec((None, FF_SHARD, D_MODEL), lambda j, i: (j, 0, 0))
    outs = pl.pallas_call(
        body, name="mlp_bwd", grid=(nj, ni),
        in_specs=[pl.BlockSpec((ts, D_MODEL), lambda j, i: (i, 0)), pl.BlockSpec((ts, D_MODEL), lambda j, i: (i, 0)),
                  vec, vec, vec, vec, wspec(0), wspec(nj), dspec],
        out_specs=[pl.BlockSpec((None, ts, D_MODEL), lambda j, i: (j, i, 0)), vec, vec, vec, vec,
                   wspec(0), wspec(0), dspec],
        out_shape=[jax.ShapeDtypeStruct((nj, s, D_MODEL), f32), vshape, vshape, vshape, vshape,
                   jax.ShapeDtypeStruct((nj, D_MODEL, FF_SHARD), bf16), jax.ShapeDtypeStruct((nj, D_MODEL, FF_SHARD), bf16),
                   jax.ShapeDtypeStruct((nj, FF_SHARD, D_MODEL), bf16)],
        scratch_shapes=[pltpu.VMEM((D_MODEL, FF_SHARD), f32), pltpu.VMEM((D_MODEL, FF_SHARD), f32),
                        pltpu.VMEM((FF_SHARD, D_MODEL), f32)],
    )(x, dy, nw, sh, sc, g2, wgu, wgu, wd)
    return outs


def out_bwd(dy, dparts, o, yg, g1, wo):
    s = dy.shape[0]
    ts = _token_block(s)
    nj = dparts.shape[0]

    ni = s // ts

    def body(dy_ref, dp_ref, o_ref, yg_ref, g1_ref, wo_ref, dx_ref, do_ref, delta_ref, dyg_ref, dg1_ref, dwo_ref, acc_sc):
        i = pl.program_id(0)
        g = dy_ref[...]
        for j in range(nj):
            g = g + dp_ref[j]
        dx_ref[...] = g
        o = o_ref[...]
        _, vjp = jax.vjp(_f_out, o, yg_ref[...], g1_ref[...], wo_ref[...].astype(f32))
        do, dyg, dg1, dwo = vjp(g)
        do_ref[...] = do
        dyg_ref[...] = dyg
        delta_ref[...] = jnp.broadcast_to(jnp.sum(do * o, axis=-1, keepdims=True), delta_ref.shape)
        _accumulate(i == 0, [dg1_ref], [dg1])
        _accumulate_then_cast(i == 0, i == ni - 1, [acc_sc], [dwo_ref], [dwo])

    head = pl.BlockSpec((N_HEADS, ts, V_DIM), lambda i: (0, i, 0))
    return pl.pallas_call(
        body, name="out_bwd", grid=(ni,), scratch_shapes=[pltpu.VMEM((D_MODEL, D_MODEL), f32)],
        in_specs=[pl.BlockSpec((ts, D_MODEL), lambda i: (i, 0)), pl.BlockSpec((nj, ts, D_MODEL), lambda i: (0, i, 0)),
                  head, pl.BlockSpec((ts, D_SSD), lambda i: (i, 0)), _const((1, D_MODEL)), _const((D_MODEL, D_MODEL))],
        out_specs=[pl.BlockSpec((ts, D_MODEL), lambda i: (i, 0)), head,
                   pl.BlockSpec((N_HEADS, ts, 128), lambda i: (0, i, 0)), pl.BlockSpec((ts, D_SSD), lambda i: (i, 0)),
                   _const((1, D_MODEL)), _const((D_MODEL, D_MODEL))],
        out_shape=[jax.ShapeDtypeStruct((s, D_MODEL), f32), jax.ShapeDtypeStruct((N_HEADS, s, V_DIM), f32),
                   jax.ShapeDtypeStruct((N_HEADS, s, 128), f32), jax.ShapeDtypeStruct((s, D_SSD), f32),
                   jax.ShapeDtypeStruct((1, D_MODEL), f32), jax.ShapeDtypeStruct((D_MODEL, D_MODEL), bf16)],
    )(dy, dparts, o, yg, g1, wo)


def attn_bwd_dq(q, k, v, do, lse, delta):
    s = q.shape[1]
    t = _token_block(s)
    nb = s // t

    rb = min(ATTN_ROWS_BWD, t)

    def body(q_ref, k_ref, v_ref, do_ref, lse_ref, delta_ref, dq_ref):
        qi = pl.program_id(1)
        dq_ref[...] = jnp.zeros(dq_ref.shape, f32)

        def step(k0, diagonal):
            for r in range(t // rb):
                rows = pl.ds(r * rb, rb)
                nk = (r + 1) * rb if diagonal else t
                k = k_ref[pl.ds(k0, nk), :]
                sc = _scores(q_ref[rows, :], k)
                if diagonal:
                    sc = jnp.where(_tril(rb, nk, r * rb), sc, NEG)
                p = jnp.exp2(sc - lse_ref[rows, :1])
                dp = lax.dot_general(do_ref[rows, :].astype(bf16), v_ref[pl.ds(k0, nk), :], (((1,), (1,)), ((), ())),
                                     preferred_element_type=f32)
                ds = p * (dp - delta_ref[rows, :1])
                dq = dq_ref[rows, :] + jnp.dot(ds.astype(bf16), k, preferred_element_type=f32)
                dq_ref[rows, :] = dq * LN2 if diagonal else dq

        def below(ki, carry):
            step(pl.multiple_of(ki * t, t), False)
            return carry

        lax.fori_loop(0, qi, below, 0)
        step(pl.multiple_of(qi * t, t), True)

    qspec = lambda w: pl.BlockSpec((None, t, w), lambda h, qi: (h, qi, 0))
    kspec = lambda w: pl.BlockSpec((None, s, w), lambda h, qi: (h, 0, 0))
    return pl.pallas_call(
        body, name="attn_bwd_dq", grid=(N_HEADS, nb),
        in_specs=[qspec(HEAD_LANES), kspec(HEAD_LANES), kspec(V_DIM), qspec(V_DIM), qspec(128), qspec(128)],
        out_specs=qspec(HEAD_LANES),
        out_shape=jax.ShapeDtypeStruct((N_HEADS, s, HEAD_LANES), f32),
    )(q, k, v, do, lse, delta)


def attn_bwd_dkv(q, k, v, do, lse, delta):
    s = q.shape[1]
    t = _token_block(s)
    nb = s // t

    cb = min(ATTN_ROWS_BWD, t)

    def body(q_ref, k_ref, v_ref, do_ref, lse_ref, delta_ref, dk_ref, dv_ref):
        ki = pl.program_id(1)
        dk_ref[...] = jnp.zeros(dk_ref.shape, f32)
        dv_ref[...] = jnp.zeros(dv_ref.shape, f32)

        def step(q0, diagonal):
            for c in range(t // cb):
                keys = pl.ds(c * cb, cb)
                r0 = c * cb if diagonal else 0
                rows = pl.ds(q0 + r0, t - r0)
                q = q_ref[rows, :]
                do = do_ref[rows, :].astype(bf16)
                sc = _scores(q, k_ref[keys, :])
                if diagonal:
                    sc = jnp.where(_tril(t - r0, cb, 0), sc, NEG)
                p = jnp.exp2(sc - lse_ref[rows, :1])
                dp = lax.dot_general(do, v_ref[keys, :], (((1,), (1,)), ((), ())), preferred_element_type=f32)
                ds = p * (dp - delta_ref[rows, :1])
                dv_ref[keys, :] += lax.dot_general(p.astype(bf16), do, (((0,), (0,)), ((), ())), preferred_element_type=f32)
                dk_ref[keys, :] += lax.dot_general(ds.astype(bf16), q, (((0,), (0,)), ((), ())), preferred_element_type=f32)

        step(pl.multiple_of(ki * t, t), True)

        def above(qi, carry):
            step(pl.multiple_of(qi * t, t), False)
            return carry

        lax.fori_loop(ki + 1, nb, above, 0)
        dk_ref[...] = dk_ref[...] * LN2

    qspec = lambda w: pl.BlockSpec((None, s, w), lambda h, ki: (h, 0, 0))
    kspec = lambda w: pl.BlockSpec((None, t, w), lambda h, ki: (h, ki, 0))
    return pl.pallas_call(
        body, name="attn_bwd_dkv", grid=(N_HEADS, nb),
        in_specs=[qspec(HEAD_LANES), kspec(HEAD_LANES), kspec(V_DIM), qspec(V_DIM), qspec(128), qspec(128)],
        out_specs=[kspec(HEAD_LANES), kspec(V_DIM)],
        out_shape=[jax.ShapeDtypeStruct((N_HEADS, s, HEAD_LANES), f32), jax.ShapeDtypeStruct((N_HEADS, s, V_DIM), f32)],
    )(q, k, v, do, lse, delta)


def ssd_bwd(px, pz, plast, states, dyg, params):
    s = px.shape[0]
    nc = s // CHUNK
    per = CHUNK // HALO

    def body(px_ref, halo_ref, pz_ref, pl_ref, st_ref, dyg_ref, cw_ref, cb_ref, dtb_ref, alog_ref, dskip_ref, snw_ref,
             dpx_ref, dpz_ref, dpl_ref, dcw_ref, dcb_ref, ddtb_ref, dalog_ref, ddskip_ref, dsnw_ref, dstate_sc, dhalo_sc):
        t = pl.program_id(0)
        chunk = nc - 1 - t

        @pl.when(t == 0)
        def _():
            dstate_sc[...] = jnp.zeros(dstate_sc.shape, f32)
            dhalo_sc[...] = jnp.zeros(dhalo_sc.shape, f32)

        halo = jnp.where(chunk > 0, halo_ref[...], 0.0)
        xext = jnp.concatenate([halo, px_ref[...]], axis=0)
        _, vjp = jax.vjp(_f_ssd, xext, pz_ref[...], pl_ref[...], st_ref[...], cw_ref[...], cb_ref[...], dtb_ref[...],
                         alog_ref[...], dskip_ref[...], snw_ref[...])
        dxext, dz, dpl, dprev, dcw, dcb, ddtb, dalog, ddskip, dsnw = vjp((dyg_ref[...], dstate_sc[...]))
        dpx_ref[...] = dxext[HALO:]
        dpx_ref[CHUNK - HALO:, :] += dhalo_sc[...]
        dhalo_sc[...] = dxext[:HALO]
        dstate_sc[...] = dprev
        dpz_ref[...] = dz
        dpl_ref[...] = dpl
        _accumulate(t == 0, [dcw_ref, dcb_ref, ddtb_ref, dalog_ref, ddskip_ref, dsnw_ref],
                    [dcw, dcb, ddtb, dalog, ddskip, dsnw])

    rev = lambda w: pl.BlockSpec((CHUNK, w), lambda t: (nc - 1 - t, 0))
    pshapes = [jax.ShapeDtypeStruct((4, D_CONV), f32), jax.ShapeDtypeStruct((1, D_CONV), f32),
               jax.ShapeDtypeStruct((1, 128), f32), jax.ShapeDtypeStruct((1, 128), f32),
               jax.ShapeDtypeStruct((1, 128), f32), jax.ShapeDtypeStruct((1, D_SSD), f32)]
    return pl.pallas_call(
        body, name="ssd_bwd", grid=(nc,),
        in_specs=[rev(D_CONV),
                  pl.BlockSpec((HALO, D_CONV), lambda t: (jnp.maximum((nc - 1 - t) * per - 1, 0), 0)),
                  rev(D_SSD), rev(128),
                  pl.BlockSpec((None, N_HEADS, SSD_HEAD_DIM, SSD_STATE), lambda t: (nc - 1 - t, 0, 0, 0)),
                  rev(D_SSD)] + _ssd_param_specs(),
        out_specs=[rev(D_CONV), rev(D_SSD), rev(128)] + _ssd_param_specs(),
        out_shape=[jax.ShapeDtypeStruct((s, D_CONV), f32), jax.ShapeDtypeStruct((s, D_SSD), f32),
                   jax.ShapeDtypeStruct((s, 128), f32)] + pshapes,
        scratch_shapes=[pltpu.VMEM((N_HEADS, SSD_HEAD_DIM, SSD_STATE), f32), pltpu.VMEM((HALO, D_CONV), f32)],
    )(px, px, pz, plast, states, dyg, *params)


def qkv_bwd(pa, plast, cos_t, sin_t, params, dq, dk, dv):
    s = pa.shape[0]
    ts = _token_block(s)

    def body(pa_ref, pl_ref, cos_ref, sin_ref, *rest):
        prm = [r[...].astype(f32) for r in rest[:8]]
        dq_ref, dk_ref, dv_ref = rest[8:11]
        dpa_ref, dpl_ref = rest[11:13]
        dprm_refs = list(rest[13:])
        cos_t, sin_t = cos_ref[...], sin_ref[...]
        _, vjp = jax.vjp(lambda a, b, *p: _f_qkv(a, b, cos_t, sin_t, *p), pa_ref[...], pl_ref[...], *prm)
        grads = vjp((dq_ref[...], dk_ref[...], dv_ref[...]))
        dpa_ref[...] = grads[0]
        dpl_ref[...] = grads[1]
        _accumulate(pl.program_id(0) == 0, dprm_refs, list(grads[2:]))

    tok = lambda w: pl.BlockSpec((ts, w), lambda i: (i, 0))
    head = lambda w: pl.BlockSpec((N_HEADS, ts, w), lambda i: (0, i, 0))
    pshapes = [jax.ShapeDtypeStruct((1, Q_RANK), f32), jax.ShapeDtypeStruct((1, KV_RANK), f32),
               jax.ShapeDtypeStruct((N_HEADS, Q_RANK, HEAD_LANES), f32), jax.ShapeDtypeStruct((N_HEADS, KV_RANK, HEAD_LANES), f32),
               jax.ShapeDtypeStruct((N_HEADS, KV_RANK, V_DIM), f32), jax.ShapeDtypeStruct((1, HEAD_LANES), f32),
               jax.ShapeDtypeStruct((1, HEAD_LANES), f32), jax.ShapeDtypeStruct((1, HEAD_LANES), f32)]
    return pl.pallas_call(
        body, name="qkv_bwd", grid=(s // ts,),
        in_specs=[tok(384), tok(128), tok(128), tok(128)] + _qkv_param_specs()
                 + [head(HEAD_LANES), head(HEAD_LANES), head(V_DIM)],
        out_specs=[tok(384), tok(128)] + _qkv_param_specs(),
        out_shape=[jax.ShapeDtypeStruct((s, 384), f32), jax.ShapeDtypeStruct((s, 128), f32)] + pshapes,
    )(pa, plast, cos_t, sin_t, *params, dq, dk, dv)


def proj_bwd(x, nw, sh, sc, w, dpa, dpz, dpx, dpl_k, dpl_dt, dres):
    s = x.shape[0]
    ts = _token_block(s)

    ni = s // ts

    def body(x_ref, nw_ref, sh_ref, sc_ref, w_ref, dpa_ref, dpz_ref, dpx_ref, dplk_ref, dpld_ref, dres_ref,
             dx_ref, dnw_ref, dsh_ref, dsc_ref, dw_ref, acc_sc):
        i = pl.program_id(0)
        g = jnp.concatenate([dpa_ref[...], dpz_ref[...], dpx_ref[...], dplk_ref[...] + dpld_ref[...]], axis=1)
        _, vjp = jax.vjp(_f_proj, x_ref[...], nw_ref[...], sh_ref[...], sc_ref[...], w_ref[...].astype(f32))
        dx, dnw, dsh, dsc, dw = vjp(g)
        dx_ref[...] = dx + dres_ref[...]
        _accumulate(i == 0, [dnw_ref, dsh_ref, dsc_ref], [dnw, dsh, dsc])
        _accumulate_then_cast(i == 0, i == ni - 1, [acc_sc], [dw_ref], [dw])

    vec = _const((1, D_MODEL))
    vshape = jax.ShapeDtypeStruct((1, D_MODEL), f32)
    tok = lambda w_: pl.BlockSpec((ts, w_), lambda i: (i, 0))
    return pl.pallas_call(
        body, name="proj_bwd", grid=(ni,), scratch_shapes=[pltpu.VMEM((D_MODEL, D_PROJ), f32)],
        in_specs=[tok(D_MODEL), vec, vec, vec, _const((D_MODEL, D_PROJ)), tok(384), tok(512), tok(1024), tok(128), tok(128),
                  tok(D_MODEL)],
        out_specs=[tok(D_MODEL), vec, vec, vec, _const((D_MODEL, D_PROJ))],
        out_shape=[jax.ShapeDtypeStruct((s, D_MODEL), f32), vshape, vshape, vshape,
                   jax.ShapeDtypeStruct((D_MODEL, D_PROJ), bf16)],
    )(x, nw, sh, sc, w, dpa, dpz, dpx, dpl_k, dpl_dt, dres)


def ada_fwd(c_all, w_ada, b_cols):
    def body(c_ref, w_ref, b_ref, out_ref):
        act = jax.nn.silu(c_ref[...])
        for l in range(2):
            out_ref[l] = jnp.dot(act, w_ref[l], precision=lax.Precision.HIGHEST, preferred_element_type=f32) + b_ref[l]

    return pl.pallas_call(body, name="ada_fwd", out_shape=jax.ShapeDtypeStruct((2, N_DEV, 768), f32))(c_all, w_ada, b_cols)


def ada_bwd(c_all, dmod_cols):
    def body(c_ref, d_ref, out_ref):
        out_ref[0] = lax.dot_general(jax.nn.silu(c_ref[...]), d_ref[0], (((0,), (0,)), ((), ())),
                                     precision=lax.Precision.HIGHEST, preferred_element_type=f32)

    return pl.pallas_call(
        body, name="ada_bwd", grid=(2,),
        in_specs=[_const((N_DEV, D_MODEL)), pl.BlockSpec((1, N_DEV, 768), lambda l: (l, 0, 0))],
        out_specs=pl.BlockSpec((1, D_MODEL, 768), lambda l: (l, 0, 0)),
        out_shape=jax.ShapeDtypeStruct((2, D_MODEL, 768), f32),
    )(c_all, dmod_cols)


def _adamw(w, g, m, v):
    m = ADAM_B1 * m + (1.0 - ADAM_B1) * g
    v = ADAM_B2 * v + (1.0 - ADAM_B2) * (g * g)
    m_hat = m / (1.0 - ADAM_B1 ** ADAM_STEP)
    v_hat = v / (1.0 - ADAM_B2 ** ADAM_STEP)
    delta = -ADAM_LR * (m_hat / (jnp.sqrt(v_hat) + ADAM_EPS) + ADAM_WD * w)
    return delta, m, v


def adamw(parts, w, m, v, name):
    n, nl, r, c = parts.shape
    tr = r
    lanes = -(-c // 128) * 128
    if 2 * (n + 7) * r * lanes * 4 > ADAMW_BLOCK_BYTES:
        tr = next(t for t in (256, 128, 64, 32, 16, 8) if r % t == 0)

    def body(p_ref, w_ref, m_ref, v_ref, g_ref, d_ref, nm_ref, nv_ref):
        g = p_ref[0].astype(f32)
        for k in range(1, n):
            g = g + p_ref[k].astype(f32)
        delta, nm, nv = _adamw(w_ref[...], g, m_ref[...], v_ref[...])
        g_ref[...] = g
        d_ref[...] = delta
        nm_ref[...] = nm
        nv_ref[...] = nv

    blk = pl.BlockSpec((None, tr, c), lambda l, i: (l, i, 0))
    shp = jax.ShapeDtypeStruct((nl, r, c), f32)
    return pl.pallas_call(
        body, name=name, grid=(nl, r // tr),
        in_specs=[pl.BlockSpec((n, None, tr, c), lambda l, i: (0, l, i, 0)), blk, blk, blk],
        out_specs=[blk] * 4, out_shape=[shp] * 4,
    )(parts, w, m, v)


def _my_index():
    return 4 * lax.axis_index("x") + 2 * lax.axis_index("y") + lax.axis_index("c")


def _coords(idx):
    return (idx // 4, (idx // 2) % 2, idx % 2)


def all_gather(shards, name):
    n = len(shards)

    def body(*refs):
        ins, outs = refs[:n], refs[n:2 * n]
        send_sems, recv_sems, local_sems = refs[2 * n:]
        me = _my_index()
        local = [pltpu.make_async_copy(ins[k], outs[k].at[me], local_sems.at[k]) for k in range(n)]
        for cp in local:
            cp.start()
        for p in range(1, N_DEV):
            peer = (me + p) % N_DEV
            for k in range(n):
                pltpu.make_async_remote_copy(src_ref=ins[k], dst_ref=outs[k].at[me], send_sem=send_sems.at[k],
                                             recv_sem=recv_sems.at[k], device_id=_coords(peer), device_id_type=MESH).start()
        for k in range(n):
            seven = outs[k].at[pl.ds(0, N_DEV - 1)]
            drain = pltpu.make_async_remote_copy(src_ref=seven, dst_ref=seven, send_sem=send_sems.at[k],
                                                 recv_sem=recv_sems.at[k], device_id=_coords(me), device_id_type=MESH)
            drain.wait_recv()
            drain.wait_send()
        for cp in local:
            cp.wait()

    return pl.pallas_call(
        body, name=name,
        in_specs=[ANY] * n, out_specs=[ANY] * n,
        out_shape=[jax.ShapeDtypeStruct((N_DEV,) + tuple(a.shape), a.dtype) for a in shards],
        scratch_shapes=[pltpu.SemaphoreType.DMA((n,)), pltpu.SemaphoreType.DMA((n,)), pltpu.SemaphoreType.DMA((n,))],
    )(*shards)


def reduce_scatter_parts(tensors, name):
    n = len(tensors)
    flat, where = [], {}
    for k, layers in enumerate(tensors):
        for l, pieces in enumerate(layers):
            d = 0
            for piece in pieces:
                for b in range(piece.shape[0]):
                    where[k, l, d] = (len(flat), b)
                    d += 1
                flat.append(piece)
            assert d == N_DEV
    n_in = len(flat)

    def body(*refs):
        ins, outs = refs[:n_in], refs[n_in:n_in + n]
        send_sems, recv_sems, local_sems = refs[n_in + n:]
        me = _my_index()

        def block(k, l, d):
            i, b = where[k, l, d]
            return ins[i].at[b]

        for d in range(N_DEV):
            @pl.when(d != me)
            def _():
                for k in range(n):
                    for l in range(2):
                        pltpu.make_async_remote_copy(src_ref=block(k, l, d), dst_ref=outs[k].at[me, l],
                                                     send_sem=send_sems.at[k], recv_sem=recv_sems.at[k],
                                                     device_id=(d // 4, (d // 2) % 2, d % 2), device_id_type=MESH).start()

            @pl.when(d == me)
            def _():
                for k in range(n):
                    for l in range(2):
                        pltpu.make_async_copy(block(k, l, d), outs[k].at[d, l], local_sems.at[k, l]).start()

        for k in range(n):
            seven = outs[k].at[pl.ds(0, N_DEV - 1)]
            drain = pltpu.make_async_remote_copy(src_ref=seven, dst_ref=seven, send_sem=send_sems.at[k],
                                                 recv_sem=recv_sems.at[k], device_id=_coords(me), device_id_type=MESH)
            drain.wait_recv()
            drain.wait_send()
            for l in range(2):
                pltpu.make_async_copy(block(k, l, 0), outs[k].at[0, l], local_sems.at[k, l]).wait()

    return pl.pallas_call(
        body, name=name,
        in_specs=[ANY] * n_in, out_specs=[ANY] * n,
        out_shape=[jax.ShapeDtypeStruct((N_DEV, 2) + tuple(layers[0][0].shape[1:]), layers[0][0].dtype) for layers in tensors],
        scratch_shapes=[pltpu.SemaphoreType.DMA((n,)), pltpu.SemaphoreType.DMA((n,)), pltpu.SemaphoreType.DMA((n, 2))],
    )(*flat)


def _pad_lanes(v, lo, total=128):
    return jnp.pad(v, (lo, total - lo - v.shape[0]))[None, :]


def _layer_weights(lw):
    w_in = lw["w_in"]
    z = jnp.zeros((D_MODEL, 1), w_in.dtype)
    w_proj = jnp.concatenate(
        [w_in[:, :384], w_in[:, 416:928], w_in[:, 928:1952], w_in[:, 1952:1960], jnp.tile(z, (1, 56)),
         w_in[:, 384:416], jnp.tile(z, (1, 32))], axis=1)
    wq = jnp.pad(lw["w_q_up"], ((0, 0), (0, 0), (0, HEAD_LANES - NOPE - ROPE)))
    wk = jnp.pad(lw["w_kv_up"][:, :, :NOPE], ((0, 0), (0, 0), (0, HEAD_LANES - NOPE)))
    wv = lw["w_kv_up"][:, :, NOPE:]
    qkv = (lw["q_a_norm_w"][None, :], lw["kv_a_norm_w"][None, :], wq, wk, wv,
           _pad_lanes(jnp.concatenate([lw["q_nope_norm_w"], lw["q_pe_norm_w"]]), 0),
           _pad_lanes(lw["k_nope_norm_w"], 0), _pad_lanes(lw["k_pe_norm_w"], NOPE))
    ssd = (lw["conv_w"], lw["conv_b"][None, :], _pad_lanes(lw["dt_bias"], 0), _pad_lanes(lw["a_log"], 0),
           _pad_lanes(lw["d_skip"], 0), lw["ssd_norm_w"][None, :])
    return dict(w_proj=w_proj, qkv=qkv, ssd=ssd, wo=lw["w_out"], wgu=lw["w_gate_up"], wd=lw["w_down"],
                n1=lw["norm1_w"][None, :], n2=lw["norm2_w"][None, :])


def layer_fwd(x, mod, kw, cos_t, sin_t):
    sh1, sc1, g1, sh2, sc2, g2 = [mod[i:i + 1] for i in range(6)]
    pa, pz, px, plast = proj_fwd(x, kw["n1"], sh1, sc1, kw["w_proj"])
    q, k, v = qkv_fwd(pa, plast, cos_t, sin_t, kw["qkv"])
    o, lse = attn_fwd(q, k, v)
    yg, states = ssd_fwd(px, pz, plast, kw["ssd"])
    x_mid = out_fwd(x, o, yg, g1, kw["wo"])
    x_out = mlp_fwd(x_mid, kw["n2"], sh2, sc2, g2, kw["wgu"], kw["wd"])
    saved = dict(x=x, pa=pa, pz=pz, px=px, plast=plast, q=q, k=k, v=v, o=o, lse=lse, yg=yg, states=states, x_mid=x_mid)
    return x_out, saved


def layer_bwd(dy, mod, kw, cos_t, sin_t, sv):
    sh1, sc1, g1, sh2, sc2, g2 = [mod[i:i + 1] for i in range(6)]
    dparts, dn2, dsh2, dsc2, dg2, dwg, dwu, dwd = mlp_bwd(sv["x_mid"], dy, kw["n2"], sh2, sc2, g2, kw["wgu"], kw["wd"])
    dmid, do, delta, dyg, dg1, dwo = out_bwd(dy, dparts, sv["o"], sv["yg"], g1, kw["wo"])
    dq = attn_bwd_dq(sv["q"], sv["k"], sv["v"], do, sv["lse"], delta)
    dk, dv = attn_bwd_dkv(sv["q"], sv["k"], sv["v"], do, sv["lse"], delta)
    dpx, dpz, dpl_dt, dcw, dcb, ddtb, dalog, ddskip, dsnw = ssd_bwd(sv["px"], sv["pz"], sv["plast"], sv["states"], dyg, kw["ssd"])
    dpa, dpl_k, dqaw, dkvaw, dwq, dwk, dwv, dqnw, dknw, dkpw = qkv_bwd(sv["pa"], sv["plast"], cos_t, sin_t, kw["qkv"], dq, dk, dv)
    dx, dn1, dsh1, dsc1, dwp = proj_bwd(sv["x"], kw["n1"], sh1, sc1, kw["w_proj"], dpa, dpz, dpx, dpl_k, dpl_dt, dmid)
    dmod = jnp.concatenate([dsh1, dsc1, dg1, dsh2, dsc2, dg2], axis=0)
    dw_in = jnp.concatenate([dwp[:, :384], dwp[:, 1984:2016], dwp[:, 384:1920], dwp[:, 1920:1928]], axis=1)
    grads = dict(
        norm1_w=dn1[0], norm2_w=dn2[0], q_a_norm_w=dqaw[0], kv_a_norm_w=dkvaw[0],
        q_nope_norm_w=dqnw[0, :NOPE], q_pe_norm_w=dqnw[0, NOPE:NOPE + ROPE], k_nope_norm_w=dknw[0, :NOPE],
        k_pe_norm_w=dkpw[0, NOPE:NOPE + ROPE], conv_b=dcb[0], dt_bias=ddtb[0, :N_HEADS], a_log=dalog[0, :N_HEADS],
        d_skip=ddskip[0, :N_HEADS], ssd_norm_w=dsnw[0],
        w_in=[dw_in.reshape(D_MODEL, N_DEV, D_IN // N_DEV).transpose(1, 0, 2)],
        w_q_up=[dwq[:, :, :NOPE + ROPE].astype(bf16)],
        w_kv_up=[jnp.concatenate([dwk[:, :, :NOPE], dwv], axis=2).astype(bf16)],
        conv_w=[dcw.reshape(4, N_DEV, D_CONV // N_DEV).transpose(1, 0, 2).astype(bf16)],
        w_out=[dwo.reshape(N_DEV, D_MODEL // N_DEV, D_MODEL)],
        w_gate_up=[dwg, dwu],
        w_down=[dwd.reshape(N_DEV, D_FF // N_DEV, D_MODEL)],
    )
    return dx, dmod, grads


def _natural_weights(gathered):
    g = gathered
    return dict(
        w_in=g["w_in"].transpose(1, 0, 2).reshape(D_MODEL, D_IN),
        w_q_up=g["w_q_up"], w_kv_up=g["w_kv_up"],
        conv_w=g["conv_w"].astype(f32).transpose(1, 0, 2).reshape(4, D_CONV),
        w_out=g["w_out"].reshape(D_MODEL, D_MODEL),
        w_gate_up=g["w_gate_up"],
        w_down=g["w_down"].reshape(N_DEV // 2, FF_SHARD, D_MODEL),
    )


def _pack_small(get):
    flat = jnp.concatenate([get(name).reshape(-1) for name, _ in SMALL])
    return jnp.pad(flat, (0, SMALL_ROWS * 128 - flat.shape[0])).reshape(SMALL_ROWS, 128)


def _unpack_small(packed):
    flat = packed.reshape(-1)
    out, off = {}, 0
    for name, size in SMALL:
        out[name] = flat[off:off + 2 * size].reshape(2, size)
        off += 2 * size
    return out


def kernel(x, c, positions, norm1_w, norm2_w, w_ada, b_ada, w_in, q_a_norm_w, w_q_up, kv_a_norm_w, w_kv_up, q_nope_norm_w, q_pe_norm_w, k_nope_norm_w, k_pe_norm_w, conv_w, conv_b, dt_bias, a_log, d_skip, ssd_norm_w, w_out, w_gate_up, w_down, loss_target, m_norm1_w, m_norm2_w, m_w_ada, m_b_ada, m_w_in, m_q_a_norm_w, m_w_q_up, m_kv_a_norm_w, m_w_kv_up, m_q_nope_norm_w, m_q_pe_norm_w, m_k_nope_norm_w, m_k_pe_norm_w, m_conv_w, m_conv_b, m_dt_bias, m_a_log, m_d_skip, m_ssd_norm_w, m_w_out, m_w_gate_up, m_w_down, v_norm1_w, v_norm2_w, v_w_ada, v_b_ada, v_w_in, v_q_a_norm_w, v_w_q_up, v_kv_a_norm_w, v_w_kv_up, v_q_nope_norm_w, v_q_pe_norm_w, v_k_nope_norm_w, v_k_pe_norm_w, v_conv_w, v_conv_b, v_dt_bias, v_a_log, v_d_skip, v_ssd_norm_w, v_w_out, v_w_gate_up, v_w_down):
    w = dict(norm1_w=norm1_w, norm2_w=norm2_w, w_ada=w_ada, b_ada=b_ada, w_in=w_in, q_a_norm_w=q_a_norm_w, w_q_up=w_q_up,
             kv_a_norm_w=kv_a_norm_w, w_kv_up=w_kv_up, q_nope_norm_w=q_nope_norm_w, q_pe_norm_w=q_pe_norm_w,
             k_nope_norm_w=k_nope_norm_w, k_pe_norm_w=k_pe_norm_w, conv_w=conv_w, conv_b=conv_b, dt_bias=dt_bias,
             a_log=a_log, d_skip=d_skip, ssd_norm_w=ssd_norm_w, w_out=w_out, w_gate_up=w_gate_up, w_down=w_down)
    m = dict(norm1_w=m_norm1_w, norm2_w=m_norm2_w, w_ada=m_w_ada, b_ada=m_b_ada, w_in=m_w_in, q_a_norm_w=m_q_a_norm_w,
             w_q_up=m_w_q_up, kv_a_norm_w=m_kv_a_norm_w, w_kv_up=m_w_kv_up, q_nope_norm_w=m_q_nope_norm_w,
             q_pe_norm_w=m_q_pe_norm_w, k_nope_norm_w=m_k_nope_norm_w, k_pe_norm_w=m_k_pe_norm_w, conv_w=m_conv_w,
             conv_b=m_conv_b, dt_bias=m_dt_bias, a_log=m_a_log, d_skip=m_d_skip, ssd_norm_w=m_ssd_norm_w, w_out=m_w_out,
             w_gate_up=m_w_gate_up, w_down=m_w_down)
    v = dict(norm1_w=v_norm1_w, norm2_w=v_norm2_w, w_ada=v_w_ada, b_ada=v_b_ada, w_in=v_w_in, q_a_norm_w=v_q_a_norm_w,
             w_q_up=v_w_q_up, kv_a_norm_w=v_kv_a_norm_w, w_kv_up=v_w_kv_up, q_nope_norm_w=v_q_nope_norm_w,
             q_pe_norm_w=v_q_pe_norm_w, k_nope_norm_w=v_k_nope_norm_w, k_pe_norm_w=v_k_pe_norm_w, conv_w=v_conv_w,
             conv_b=v_conv_b, dt_bias=v_dt_bias, a_log=v_a_log, d_skip=v_d_skip, ssd_norm_w=v_ssd_norm_w, w_out=v_w_out,
             w_gate_up=v_w_gate_up, w_down=v_w_down)
    me = _my_index()
    seq = x.shape[1]

    shards = [c] + [w[name][l] if name == "conv_w" else w[name][l].astype(bf16) for l in range(2) for name in BIG]
    gathered = all_gather(shards, "gather_weights")
    c_all = gathered[0].reshape(N_DEV, D_MODEL)
    kws = []
    for l in range(2):
        g = {name: gathered[1 + l * len(BIG) + i] for i, name in enumerate(BIG)}
        lw = _natural_weights(g)
        for name, _ in SMALL:
            if name != "b_ada":
                lw[name] = w[name][l]
        kws.append(_layer_weights(lw))

    b_cols = lax.dynamic_slice_in_dim(b_ada, me * 768, 768, axis=1)
    mod_cols = ada_fwd(c_all, w_ada, b_cols)
    (mod_all,) = all_gather([mod_cols], "gather_mod")
    mod_me = lax.dynamic_index_in_dim(mod_all, me, axis=2, keepdims=False)
    mods = [mod_me[:, l, :].reshape(6, D_MODEL) for l in range(2)]

    inv_freq = 1.0 / (ROPE_THETA ** (jnp.arange(0, ROPE, 2, dtype=f32) / ROPE))
    inv = _pad_lanes(jnp.concatenate([inv_freq, inv_freq]), NOPE)
    cos_t, sin_t = rope_tables(positions.reshape(seq, 1), inv)

    h = x[0]
    saved = []
    for l in range(2):
        h, sv = layer_fwd(h, mods[l], kws[l], cos_t, sin_t)
        saved.append(sv)
    dy, loss_part = loss_fwd(h, loss_target[0])
    loss = lax.psum(loss_part[0, 0], ("x", "y", "c"))

    grads, dmods = [None, None], [None, None]
    for l in (1, 0):
        dy, dmods[l], grads[l] = layer_bwd(dy, mods[l], kws[l], cos_t, sin_t, saved[l])
    grad_x = dy[None]

    small_part = {name: jnp.stack([grads[0][name], grads[1][name]]) for name, _ in SMALL if name != "b_ada"}
    small_part["b_ada"] = jnp.stack([dmods[0].reshape(-1), dmods[1].reshape(-1)])
    (small_all,) = all_gather([_pack_small(lambda n: small_part[n])], "gather_small_grads")
    sg, sd, sm, sv_ = adamw(small_all.reshape(N_DEV, 1, SMALL_ROWS, 128), _pack_small(lambda n: w[n])[None],
                            _pack_small(lambda n: m[n])[None], _pack_small(lambda n: v[n])[None], "adamw_small")
    res = {}
    for key, packed in (("g", sg), ("d", sd), ("m", sm), ("v", sv_)):
        for name, arr in _unpack_small(packed[0]).items():
            res[key, name] = arr

    off = 2 * (1024 + 1024)
    dmod_all = small_all.reshape(N_DEV, -1)[:, off:off + 2 * 6144].reshape(N_DEV, 2, 6144)
    dmod_cols = lax.dynamic_slice_in_dim(dmod_all, me * 768, 768, axis=2).transpose(1, 0, 2)
    g_ada = ada_bwd(c_all, dmod_cols)
    res["g", "w_ada"], res["d", "w_ada"], res["m", "w_ada"], res["v", "w_ada"] = adamw(
        g_ada[None], w_ada, m_w_ada, v_w_ada, "adamw_w_ada")

    parts = reduce_scatter_parts([[grads[0][name], grads[1][name]] for name in BIG], "scatter_grads")
    for name, part in zip(BIG, parts):
        res["g", name], res["d", name], res["m", name], res["v", name] = adamw(part, w[name], m[name], v[name], "adamw_" + name)

    return (loss, grad_x, *[res["g", n] for n in WEIGHTS], *[res["d", n] for n in WEIGHTS],
            *[res["m", n] for n in WEIGHTS], *[res["v", n] for n in WEIGHTS])
```

```python
import functools

import jax
import jax.numpy as jnp
from jax import lax
from jax.experimental import pallas as pl
from jax.experimental.pallas import tpu as pltpu

f32 = jnp.float32
bf16 = jnp.bfloat16

N_DEV = 8
D_MODEL = 1024
N_HEADS = 8
HEAD_LANES = 128
NOPE = 64
ROPE = 32
V_DIM = 64
Q_RANK = 256
KV_RANK = 128
D_SSD = 512
D_CONV = 1024
SSD_STATE = 128
SSD_HEAD_DIM = 64
CHUNK = 128
HALO = 8
D_FF = 2816
FF_SHARD = 704
D_IN = 1960
D_PROJ = 2048
EPS = 1e-6
LOG2E = 1.4426950408889634
LN2 = 0.6931471805599453
Q_SCALE = (NOPE + ROPE) ** -0.5 * LOG2E
ATTN_ROWS_FWD = 256
ATTN_ROWS_BWD = 512
ROPE_THETA = 10000.0
NEG = -1e30

ADAM_LR = 0.001
ADAM_B1 = 0.9
ADAM_B2 = 0.999
ADAM_EPS = 1e-08
ADAM_WD = 0.01
ADAM_STEP = 10
ADAMW_BLOCK_BYTES = 24 << 20

MESH = pl.DeviceIdType.MESH
ANY = pl.BlockSpec(memory_space=pl.ANY)

SMALL = (("norm1_w", 1024), ("norm2_w", 1024), ("b_ada", 6144), ("q_a_norm_w", 256), ("kv_a_norm_w", 128),
         ("q_nope_norm_w", 64), ("q_pe_norm_w", 32), ("k_nope_norm_w", 64), ("k_pe_norm_w", 32),
         ("conv_b", 1024), ("dt_bias", 8), ("a_log", 8), ("d_skip", 8), ("ssd_norm_w", 512))
SMALL_ROWS = 168
BIG = ("w_in", "w_q_up", "w_kv_up", "conv_w", "w_out", "w_gate_up", "w_down")
WEIGHTS = ("norm1_w", "norm2_w", "w_ada", "b_ada", "w_in", "q_a_norm_w", "w_q_up", "kv_a_norm_w", "w_kv_up",
           "q_nope_norm_w", "q_pe_norm_w", "k_nope_norm_w", "k_pe_norm_w", "conv_w", "conv_b", "dt_bias",
           "a_log", "d_skip", "ssd_norm_w", "w_out", "w_gate_up", "w_down")


def _dot(a, b, ca, cb):
    return lax.dot_general(a.astype(bf16), b.astype(bf16), (((ca,), (cb,)), ((), ())), preferred_element_type=f32)


@jax.custom_vjp
def mm(a, b):
    return _dot(a, b, 1, 0)


def _mm_fwd(a, b):
    return _dot(a, b, 1, 0), (a, b)


def _mm_bwd(res, g):
    a, b = res
    return _dot(g, b, 1, 1).astype(a.dtype), _dot(a, g, 0, 0).astype(b.dtype)


mm.defvjp(_mm_fwd, _mm_bwd)


@jax.custom_vjp
def mm_nt(a, b):
    return _dot(a, b, 1, 1)


def _mm_nt_fwd(a, b):
    return _dot(a, b, 1, 1), (a, b)


def _mm_nt_bwd(res, g):
    a, b = res
    return _dot(g, b, 1, 0).astype(a.dtype), _dot(g, a, 0, 0).astype(b.dtype)


mm_nt.defvjp(_mm_nt_fwd, _mm_nt_bwd)


@jax.custom_vjp
def mm_tn(a, b):
    return _dot(a, b, 0, 0)


def _mm_tn_fwd(a, b):
    return _dot(a, b, 0, 0), (a, b)


def _mm_tn_bwd(res, g):
    a, b = res
    return _dot(b, g, 1, 1).astype(a.dtype), _dot(a, g, 1, 0).astype(b.dtype)


mm_tn.defvjp(_mm_tn_fwd, _mm_tn_bwd)


def _rms(x, w):
    return x * lax.rsqrt(jnp.mean(x * x, axis=-1, keepdims=True) + EPS) * w


def _const(shape):
    n = len(shape)
    return pl.BlockSpec(shape, lambda *_: (0,) * n)


def _accumulate(first, refs, vals):
    @pl.when(first)
    def _():
        for r, v in zip(refs, vals):
            r[...] = v

    @pl.when(jnp.logical_not(first))
    def _():
        for r, v in zip(refs, vals):
            r[...] += v


def _accumulate_then_cast(first, last, accs, outs, vals):
    _accumulate(first, accs, vals)

    @pl.when(last)
    def _():
        for a, o in zip(accs, outs):
            o[...] = a[...].astype(o.dtype)


def _token_block(s):
    return min(512, s)


def _f_proj(x, nw, sh, sc, w):
    h = _rms(x, nw) * (1.0 + sc) + sh
    return mm(h, w)


def _f_qkv(pa, plast, cos_t, sin_t, qaw, kvaw, wq, wk, wv, qnw, knw, kpw):
    lane = lax.broadcasted_iota(jnp.int32, (1, HEAD_LANES), 1)
    m_nope = lane < NOPE
    m_pe = (lane >= NOPE) & (lane < NOPE + ROPE)
    rows = pa.shape[0]

    def rope(t):
        half = ROPE // 2
        swapped = jnp.concatenate(
            [jnp.zeros((rows, NOPE), f32), t[:, NOPE + half:NOPE + ROPE], t[:, NOPE:NOPE + half],
             jnp.zeros((rows, HEAD_LANES - NOPE - ROPE), f32)], axis=1)
        return t * cos_t + swapped * sin_t

    qa = _rms(pa[:, :Q_RANK], qaw)
    kva = _rms(pa[:, Q_RANK:Q_RANK + KV_RANK], kvaw)
    kp = jnp.where(m_pe, plast, 0.0)
    kp = kp * lax.rsqrt(jnp.sum(kp * kp, axis=-1, keepdims=True) / ROPE + EPS) * kpw
    k_rot = rope(kp)
    qs, ks, vs = [], [], []
    for h in range(N_HEADS):
        qh = mm(qa, wq[h])
        ss_n = jnp.sum(jnp.where(m_nope, qh * qh, 0.0), axis=-1, keepdims=True) / NOPE
        ss_p = jnp.sum(jnp.where(m_pe, qh * qh, 0.0), axis=-1, keepdims=True) / ROPE
        r = jnp.where(m_nope, lax.rsqrt(ss_n + EPS), lax.rsqrt(ss_p + EPS))
        qs.append(rope(qh * r * qnw) * Q_SCALE)
        kh = mm(kva, wk[h])
        kh = kh * lax.rsqrt(jnp.sum(kh * kh, axis=-1, keepdims=True) / NOPE + EPS) * knw
        ks.append(kh + k_rot)
        vs.append(mm(kva, wv[h]))
    return jnp.stack(qs), jnp.stack(ks), jnp.stack(vs)


def _f_ssd(xext, z, plast, prev, cw, cb, dtb, alog, dskip, snw):
    n = CHUNK
    conv = cb
    for k in range(4):
        conv = conv + cw[k:k + 1] * xext[HALO - 3 + k:HALO - 3 + k + n]
    xc = jax.nn.silu(conv)
    xs, bm, cm = xc[:, :D_SSD], xc[:, D_SSD:D_SSD + 2 * SSD_STATE], xc[:, D_SSD + 2 * SSD_STATE:]
    lane = lax.broadcasted_iota(jnp.int32, (1, 128), 1)
    dt = jax.nn.softplus(jnp.where(lane < N_HEADS, plast, 0.0) + dtb)
    adt = dt * (-jnp.exp(alog))
    row = lax.broadcasted_iota(jnp.int32, (n, n), 0)
    col = lax.broadcasted_iota(jnp.int32, (n, n), 1)
    tri = row >= col
    acs = jnp.dot(tri.astype(f32), adt, precision=lax.Precision.HIGHEST, preferred_element_type=f32)
    acs_t = acs.T
    ys, news = [], []
    for g in range(2):
        bg = bm[:, g * SSD_STATE:(g + 1) * SSD_STATE]
        cg = cm[:, g * SSD_STATE:(g + 1) * SSD_STATE]
        cb_t = mm_nt(cg, bg)
        for r in range(4):
            h = g * 4 + r
            a_col = acs[:, h:h + 1]
            a_row = acs_t[h:h + 1, :]
            decay_ls = jnp.exp(jnp.where(tri, a_col - a_row, -jnp.inf))
            xh = xs[:, h * SSD_HEAD_DIM:(h + 1) * SSD_HEAD_DIM]
            xdt = xh * dt[:, h:h + 1]
            y_diag = mm(cb_t * decay_ls, xdt)
            a_last = acs[n - 1:n, h:h + 1]
            st = mm_tn(xdt * jnp.exp(a_last - a_col), bg)
            news.append(jnp.exp(a_last) * prev[h] + st)
            y_off = mm_nt(cg, prev[h]) * jnp.exp(a_col)
            ys.append(y_diag + y_off + dskip[:, h:h + 1] * xh)
    y = jnp.concatenate(ys, axis=1)
    yg = y * jax.nn.silu(z)
    half = D_SSD // 2
    outs = []
    for g in range(2):
        t = yg[:, g * half:(g + 1) * half]
        outs.append(t * lax.rsqrt(jnp.mean(t * t, axis=-1, keepdims=True) + EPS))
    return jnp.concatenate(outs, axis=1) * snw, jnp.stack(news)


def _f_out(o, yg, g1, wo):
    cat = jnp.concatenate([o[h] for h in range(N_HEADS)] + [yg], axis=1)
    return g1 * mm(cat, wo)


def _f_mlp(x, nw, sh, sc, g2, wg, wu, wd):
    h = _rms(x, nw) * (1.0 + sc) + sh
    act = jax.nn.silu(mm(h, wg)) * mm(h, wu)
    return g2 * mm(act, wd)


def proj_fwd(x, nw, sh, sc, w):
    s = x.shape[0]
    ts = _token_block(s)

    def body(x_ref, nw_ref, sh_ref, sc_ref, w_ref, pa_ref, pz_ref, px_ref, pl_ref):
        p = _f_proj(x_ref[...], nw_ref[...], sh_ref[...], sc_ref[...], w_ref[...])
        pa_ref[...] = p[:, :384]
        pz_ref[...] = p[:, 384:896]
        px_ref[...] = p[:, 896:1920]
        pl_ref[...] = p[:, 1920:]

    vec = _const((1, D_MODEL))
    return pl.pallas_call(
        body, name="proj_fwd", grid=(s // ts,),
        in_specs=[pl.BlockSpec((ts, D_MODEL), lambda i: (i, 0)), vec, vec, vec, _const((D_MODEL, D_PROJ))],
        out_specs=[pl.BlockSpec((ts, 384), lambda i: (i, 0)), pl.BlockSpec((ts, 512), lambda i: (i, 0)),
                   pl.BlockSpec((ts, 1024), lambda i: (i, 0)), pl.BlockSpec((ts, 128), lambda i: (i, 0))],
        out_shape=[jax.ShapeDtypeStruct((s, 384), f32), jax.ShapeDtypeStruct((s, 512), f32),
                   jax.ShapeDtypeStruct((s, 1024), f32), jax.ShapeDtypeStruct((s, 128), f32)],
    )(x, nw, sh, sc, w)


def rope_tables(pos, inv):
    s = pos.shape[0]
    ts = _token_block(s)

    def body(pos_ref, inv_ref, cos_ref, sin_ref):
        ang = pos_ref[...].astype(f32) * inv_ref[...]
        lane = lax.broadcasted_iota(jnp.int32, (1, HEAD_LANES), 1)
        half = ROPE // 2
        cos_ref[...] = jnp.where(lane < NOPE, 1.0, jnp.where(lane < NOPE + ROPE, jnp.cos(ang), 0.0))
        sn = jnp.sin(ang)
        sin_ref[...] = jnp.where((lane >= NOPE) & (lane < NOPE + half), -sn,
                                 jnp.where((lane >= NOPE + half) & (lane < NOPE + ROPE), sn, 0.0))

    return pl.pallas_call(
        body, name="rope_tables", grid=(s // ts,),
        in_specs=[pl.BlockSpec((ts, 1), lambda i: (i, 0)), _const((1, HEAD_LANES))],
        out_specs=[pl.BlockSpec((ts, HEAD_LANES), lambda i: (i, 0))] * 2,
        out_shape=[jax.ShapeDtypeStruct((s, HEAD_LANES), f32)] * 2,
    )(pos, inv)


def _qkv_param_specs():
    return [_const((1, Q_RANK)), _const((1, KV_RANK)), _const((N_HEADS, Q_RANK, HEAD_LANES)),
            _const((N_HEADS, KV_RANK, HEAD_LANES)), _const((N_HEADS, KV_RANK, V_DIM)),
            _const((1, HEAD_LANES)), _const((1, HEAD_LANES)), _const((1, HEAD_LANES))]


def qkv_fwd(pa, plast, cos_t, sin_t, params):
    s = pa.shape[0]
    ts = _token_block(s)

    def body(pa_ref, pl_ref, cos_ref, sin_ref, *rest):
        prm = [r[...] for r in rest[:8]]
        q_ref, k_ref, v_ref = rest[8:]
        q, k, v = _f_qkv(pa_ref[...], pl_ref[...], cos_ref[...], sin_ref[...], *prm)
        q_ref[...] = q.astype(bf16)
        k_ref[...] = k.astype(bf16)
        v_ref[...] = v.astype(bf16)

    tok = lambda w: pl.BlockSpec((ts, w), lambda i: (i, 0))
    return pl.pallas_call(
        body, name="qkv_fwd", grid=(s // ts,),
        in_specs=[tok(384), tok(128), tok(128), tok(128)] + _qkv_param_specs(),
        out_specs=[pl.BlockSpec((N_HEADS, ts, HEAD_LANES), lambda i: (0, i, 0)),
                   pl.BlockSpec((N_HEADS, ts, HEAD_LANES), lambda i: (0, i, 0)),
                   pl.BlockSpec((N_HEADS, ts, V_DIM), lambda i: (0, i, 0))],
        out_shape=[jax.ShapeDtypeStruct((N_HEADS, s, HEAD_LANES), bf16), jax.ShapeDtypeStruct((N_HEADS, s, HEAD_LANES), bf16),
                   jax.ShapeDtypeStruct((N_HEADS, s, V_DIM), bf16)],
    )(pa, plast, cos_t, sin_t, *params)


def _scores(q, k):
    return lax.dot_general(q, k, (((1,), (1,)), ((), ())), preferred_element_type=f32)


def _tril(rows, cols, row_offset):
    row = row_offset + lax.broadcasted_iota(jnp.int32, (rows, cols), 0)
    col = lax.broadcasted_iota(jnp.int32, (rows, cols), 1)
    return row >= col


def _call_with_job(body, name, grid, job, in_specs, out_specs, out_shape, scratch_shapes, operands):
    if job is None:
        res = pl.pallas_call(body, name=name, grid=grid, in_specs=in_specs, out_specs=out_specs, out_shape=out_shape,
                             scratch_shapes=scratch_shapes)(*operands)
        return res, None

    def at(step):
        return lambda: functools.reduce(jnp.logical_and, [pl.program_id(a) == step(n) for a, n in enumerate(grid)])

    carrier = _carry(job, body, len(in_specs), len(out_specs), at(lambda n: 0), at(lambda n: n - 1))
    res = pl.pallas_call(
        carrier, name=name, grid=grid,
        in_specs=list(in_specs) + [ANY] * len(job.operands), out_specs=list(out_specs) + [ANY] * len(job.out_shape),
        out_shape=list(out_shape) + list(job.out_shape), scratch_shapes=list(scratch_shapes) + job.scratch,
    )(*operands, *job.operands)
    return res[:len(out_specs)], res[len(out_specs):]


def attn_fwd(q, k, v, job=None):
    s = q.shape[1]
    t = _token_block(s)
    nb = s // t

    rb = min(ATTN_ROWS_FWD, t)

    def body(q_ref, k_ref, v_ref, o_ref, lse_ref, m_sc, l_sc, acc_sc):
        qi = pl.program_id(1)
        m_sc[...] = jnp.full(m_sc.shape, NEG, f32)
        l_sc[...] = jnp.zeros(l_sc.shape, f32)
        acc_sc[...] = jnp.zeros(acc_sc.shape, f32)

        def step(k0, diagonal):
            for r in range(t // rb):
                rows = pl.ds(r * rb, rb)
                nk = (r + 1) * rb if diagonal else t
                keys = pl.ds(k0, nk)
                sc = _scores(q_ref[rows, :], k_ref[keys, :])
                if diagonal:
                    sc = jnp.where(_tril(rb, nk, r * rb), sc, NEG)
                m_prev = m_sc[rows, :1]
                m_new = jnp.maximum(m_prev, jnp.max(sc, axis=-1, keepdims=True))
                p = jnp.exp2(sc - m_new)
                alpha = jnp.exp2(m_prev - m_new)
                l_new = alpha * l_sc[rows, :1] + jnp.sum(p, axis=-1, keepdims=True)
                acc = alpha * acc_sc[rows, :] + jnp.dot(p.astype(bf16), v_ref[keys, :], preferred_element_type=f32)
                if diagonal:
                    o_ref[rows, :] = acc / l_new
                    lse_ref[rows, :] = jnp.broadcast_to(m_new + jnp.log2(l_new), (rb, 128))
                else:
                    acc_sc[rows, :] = acc
                    m_sc[rows, :] = jnp.broadcast_to(m_new, (rb, 128))
                    l_sc[rows, :] = jnp.broadcast_to(l_new, (rb, 128))

        def below(ki, carry):
            step(pl.multiple_of(ki * t, t), False)
            return carry

        lax.fori_loop(0, qi, below, 0)
        step(pl.multiple_of(qi * t, t), True)

    return _call_with_job(
        body, "attn_fwd" if job is None else "attn_fwd_comm", (N_HEADS, nb), job,
        in_specs=[pl.BlockSpec((None, t, HEAD_LANES), lambda h, qi: (h, qi, 0)),
                  pl.BlockSpec((None, s, HEAD_LANES), lambda h, qi: (h, 0, 0)),
                  pl.BlockSpec((None, s, V_DIM), lambda h, qi: (h, 0, 0))],
        out_specs=[pl.BlockSpec((None, t, V_DIM), lambda h, qi: (h, qi, 0)),
                   pl.BlockSpec((None, t, 128), lambda h, qi: (h, qi, 0))],
        out_shape=[jax.ShapeDtypeStruct((N_HEADS, s, V_DIM), f32), jax.ShapeDtypeStruct((N_HEADS, s, 128), f32)],
        scratch_shapes=[pltpu.VMEM((t, 128), f32), pltpu.VMEM((t, 128), f32), pltpu.VMEM((t, V_DIM), f32)],
        operands=(q, k, v))


def _ssd_param_specs():
    return [_const((4, D_CONV)), _const((1, D_CONV)), _const((1, 128)), _const((1, 128)), _const((1, 128)),
            _const((1, D_SSD))]


def ssd_fwd(px, pz, plast, params):
    s = px.shape[0]
    nc = s // CHUNK

    def body(px_ref, pz_ref, pl_ref, cw_ref, cb_ref, dtb_ref, alog_ref, dskip_ref, snw_ref, yg_ref, st_ref,
             state_sc, halo_sc):
        i = pl.program_id(0)

        @pl.when(i == 0)
        def _():
            state_sc[...] = jnp.zeros(state_sc.shape, f32)
            halo_sc[...] = jnp.zeros(halo_sc.shape, f32)

        x = px_ref[...]
        prev = state_sc[...]
        st_ref[...] = prev
        xext = jnp.concatenate([halo_sc[...], x], axis=0)
        yg, new = _f_ssd(xext, pz_ref[...], pl_ref[...], prev, cw_ref[...], cb_ref[...], dtb_ref[...],
                         alog_ref[...], dskip_ref[...], snw_ref[...])
        yg_ref[...] = yg
        state_sc[...] = new
        halo_sc[...] = x[CHUNK - HALO:]

    tok = lambda w: pl.BlockSpec((CHUNK, w), lambda i: (i, 0))
    return pl.pallas_call(
        body, name="ssd_fwd", grid=(nc,),
        in_specs=[tok(D_CONV), tok(D_SSD), tok(128)] + _ssd_param_specs(),
        out_specs=[tok(D_SSD), pl.BlockSpec((None, N_HEADS, SSD_HEAD_DIM, SSD_STATE), lambda i: (i, 0, 0, 0))],
        out_shape=[jax.ShapeDtypeStruct((s, D_SSD), f32),
                   jax.ShapeDtypeStruct((nc, N_HEADS, SSD_HEAD_DIM, SSD_STATE), f32)],
        scratch_shapes=[pltpu.VMEM((N_HEADS, SSD_HEAD_DIM, SSD_STATE), f32), pltpu.VMEM((HALO, D_CONV), f32)],
    )(px, pz, plast, *params)


def out_fwd(x, o, yg, g1, wo):
    s = x.shape[0]
    ts = _token_block(s)

    def body(x_ref, o_ref, yg_ref, g1_ref, wo_ref, out_ref):
        out_ref[...] = x_ref[...] + _f_out(o_ref[...], yg_ref[...], g1_ref[...], wo_ref[...])

    return pl.pallas_call(
        body, name="out_fwd", grid=(s // ts,),
        in_specs=[pl.BlockSpec((ts, D_MODEL), lambda i: (i, 0)), pl.BlockSpec((N_HEADS, ts, V_DIM), lambda i: (0, i, 0)),
                  pl.BlockSpec((ts, D_SSD), lambda i: (i, 0)), _const((1, D_MODEL)), _const((D_MODEL, D_MODEL))],
        out_specs=pl.BlockSpec((ts, D_MODEL), lambda i: (i, 0)),
        out_shape=jax.ShapeDtypeStruct((s, D_MODEL), f32),
    )(x, o, yg, g1, wo)


def mlp_fwd(x, nw, sh, sc, g2, wgu, wd):
    s = x.shape[0]
    ts = _token_block(s)
    nj = N_DEV // 2

    def body(x_ref, nw_ref, sh_ref, sc_ref, g2_ref, wg_ref, wu_ref, wd_ref, out_ref, acc_sc):
        j = pl.program_id(1)
        part = _f_mlp(x_ref[...], nw_ref[...], sh_ref[...], sc_ref[...], g2_ref[...], wg_ref[...], wu_ref[...], wd_ref[...])

        @pl.when(j == 0)
        def _():
            acc_sc[...] = x_ref[...] + part

        @pl.when(j > 0)
        def _():
            acc_sc[...] += part

        @pl.when(j == nj - 1)
        def _():
            out_ref[...] = acc_sc[...]

    vec = _const((1, D_MODEL))
    return pl.pallas_call(
        body, name="mlp_fwd", grid=(s // ts, nj),
        in_specs=[pl.BlockSpec((ts, D_MODEL), lambda i, j: (i, 0)), vec, vec, vec, vec,
                  pl.BlockSpec((None, D_MODEL, FF_SHARD), lambda i, j: (j, 0, 0)),
                  pl.BlockSpec((None, D_MODEL, FF_SHARD), lambda i, j: (j + nj, 0, 0)),
                  pl.BlockSpec((None, FF_SHARD, D_MODEL), lambda i, j: (j, 0, 0))],
        out_specs=pl.BlockSpec((ts, D_MODEL), lambda i, j: (i, 0)),
        out_shape=jax.ShapeDtypeStruct((s, D_MODEL), f32),
        scratch_shapes=[pltpu.VMEM((ts, D_MODEL), f32)],
    )(x, nw, sh, sc, g2, wgu, wgu, wd)


def loss_fwd(y, target):
    s = y.shape[0]
    ts = _token_block(s)

    def body(y_ref, t_ref, dy_ref, loss_ref):
        d = y_ref[...] - t_ref[...]
        dy_ref[...] = d * (1.0 / D_MODEL)
        part = 0.5 * jnp.sum(jnp.sum(d * d, axis=-1, keepdims=True) * (1.0 / D_MODEL), axis=0, keepdims=True)
        _accumulate(pl.program_id(0) == 0, [loss_ref], [jnp.broadcast_to(part, (8, 128))])

    return pl.pallas_call(
        body, name="loss_fwd", grid=(s // ts,),
        in_specs=[pl.BlockSpec((ts, D_MODEL), lambda i: (i, 0))] * 2,
        out_specs=[pl.BlockSpec((ts, D_MODEL), lambda i: (i, 0)), _const((8, 128))],
        out_shape=[jax.ShapeDtypeStruct((s, D_MODEL), f32), jax.ShapeDtypeStruct((8, 128), f32)],
    )(y, target)


def mlp_bwd(x, dy, nw, sh, sc, g2, wgu, wd, job=None):
    s = x.shape[0]
    ts = min(256, s)
    nj = N_DEV // 2
    ni = s // ts

    def body(x_ref, dy_ref, nw_ref, sh_ref, sc_ref, g2_ref, wg_ref, wu_ref, wd_ref,
             dx_ref, dnw_ref, dsh_ref, dsc_ref, dg2_ref, dwg_ref, dwu_ref, dwd_ref, ag_sc, au_sc, ad_sc):
        j, i = pl.program_id(0), pl.program_id(1)
        _, vjp = jax.vjp(_f_mlp, x_ref[...], nw_ref[...], sh_ref[...], sc_ref[...], g2_ref[...],
                         wg_ref[...].astype(f32), wu_ref[...].astype(f32), wd_ref[...].astype(f32))
        dx, dnw, dsh, dsc, dg2, dwg, dwu, dwd = vjp(dy_ref[...])
        dx_ref[...] = dx
        _accumulate((i == 0) & (j == 0), [dnw_ref, dsh_ref, dsc_ref, dg2_ref], [dnw, dsh, dsc, dg2])
        _accumulate_then_cast(i == 0, i == ni - 1, [ag_sc, au_sc, ad_sc], [dwg_ref, dwu_ref, dwd_ref], [dwg, dwu, dwd])

    vec = _const((1, D_MODEL))
    vshape = jax.ShapeDtypeStruct((1, D_MODEL), f32)
    wspec = lambda off: pl.BlockSpec((None, D_MODEL, FF_SHARD), lambda j, i: (j + off, 0, 0))
    dspec = pl.BlockSpec((None, FF_SHARD, D_MODEL), lambda j, i: (j, 0, 0))
    return _call_with_job(
        body, "mlp_bwd" if job is None else "mlp_bwd_comm", (nj, ni), job,
        in_specs=[pl.BlockSpec((ts, D_MODEL), lambda j, i: (i, 0)), pl.BlockSpec((ts, D_MODEL), lambda j, i: (i, 0)),
                  vec, vec, vec, vec, wspec(0), wspec(nj), dspec],
        out_specs=[pl.BlockSpec((None, ts, D_MODEL), lambda j, i: (j, i, 0)), vec, vec, vec, vec,
                   wspec(0), wspec(0), dspec],
        out_shape=[jax.ShapeDtypeStruct((nj, s, D_MODEL), f32), vshape, vshape, vshape, vshape,
                   jax.ShapeDtypeStruct((nj, D_MODEL, FF_SHARD), bf16), jax.ShapeDtypeStruct((nj, D_MODEL, FF_SHARD), bf16),
                   jax.ShapeDtypeStruct((nj, FF_SHARD, D_MODEL), bf16)],
        scratch_shapes=[pltpu.VMEM((D_MODEL, FF_SHARD), f32), pltpu.VMEM((D_MODEL, FF_SHARD), f32),
                        pltpu.VMEM((FF_SHARD, D_MODEL), f32)],
        operands=(x, dy, nw, sh, sc, g2, wgu, wgu, wd))


def out_bwd(dy, dparts, o, yg, g1, wo):
    s = dy.shape[0]
    ts = _token_block(s)
    nj = dparts.shape[0]

    ni = s // ts

    def body(dy_ref, dp_ref, o_ref, yg_ref, g1_ref, wo_ref, dx_ref, do_ref, delta_ref, dyg_ref, dg1_ref, dwo_ref, acc_sc):
        i = pl.program_id(0)
        g = dy_ref[...]
        for j in range(nj):
            g = g + dp_ref[j]
        dx_ref[...] = g
        o = o_ref[...]
        _, vjp = jax.vjp(_f_out, o, yg_ref[...], g1_ref[...], wo_ref[...].astype(f32))
        do, dyg, dg1, dwo = vjp(g)
        do_ref[...] = do
        dyg_ref[...] = dyg
        delta_ref[...] = jnp.broadcast_to(jnp.sum(do * o, axis=-1, keepdims=True), delta_ref.shape)
        _accumulate(i == 0, [dg1_ref], [dg1])
        _accumulate_then_cast(i == 0, i == ni - 1, [acc_sc], [dwo_ref], [dwo])

    head = pl.BlockSpec((N_HEADS, ts, V_DIM), lambda i: (0, i, 0))
    return pl.pallas_call(
        body, name="out_bwd", grid=(ni,), scratch_shapes=[pltpu.VMEM((D_MODEL, D_MODEL), f32)],
        in_specs=[pl.BlockSpec((ts, D_MODEL), lambda i: (i, 0)), pl.BlockSpec((nj, ts, D_MODEL), lambda i: (0, i, 0)),
                  head, pl.BlockSpec((ts, D_SSD), lambda i: (i, 0)), _const((1, D_MODEL)), _const((D_MODEL, D_MODEL))],
        out_specs=[pl.BlockSpec((ts, D_MODEL), lambda i: (i, 0)), head,
                   pl.BlockSpec((N_HEADS, ts, 128), lambda i: (0, i, 0)), pl.BlockSpec((ts, D_SSD), lambda i: (i, 0)),
                   _const((1, D_MODEL)), _const((D_MODEL, D_MODEL))],
        out_shape=[jax.ShapeDtypeStruct((s, D_MODEL), f32), jax.ShapeDtypeStruct((N_HEADS, s, V_DIM), f32),
                   jax.ShapeDtypeStruct((N_HEADS, s, 128), f32), jax.ShapeDtypeStruct((s, D_SSD), f32),
                   jax.ShapeDtypeStruct((1, D_MODEL), f32), jax.ShapeDtypeStruct((D_MODEL, D_MODEL), bf16)],
    )(dy, dparts, o, yg, g1, wo)


def attn_bwd_dq(q, k, v, do, lse, delta):
    s = q.shape[1]
    t = _token_block(s)
    nb = s // t

    rb = min(ATTN_ROWS_BWD, t)

    def body(q_ref, k_ref, v_ref, do_ref, lse_ref, delta_ref, dq_ref):
        qi = pl.program_id(1)
        dq_ref[...] = jnp.zeros(dq_ref.shape, f32)

        def step(k0, diagonal):
            for r in range(t // rb):
                rows = pl.ds(r * rb, rb)
                nk = (r + 1) * rb if diagonal else t
                k = k_ref[pl.ds(k0, nk), :]
                sc = _scores(q_ref[rows, :], k)
                if diagonal:
                    sc = jnp.where(_tril(rb, nk, r * rb), sc, NEG)
                p = jnp.exp2(sc - lse_ref[rows, :1])
                dp = lax.dot_general(do_ref[rows, :].astype(bf16), v_ref[pl.ds(k0, nk), :], (((1,), (1,)), ((), ())),
                                     preferred_element_type=f32)
                ds = p * (dp - delta_ref[rows, :1])
                dq = dq_ref[rows, :] + jnp.dot(ds.astype(bf16), k, preferred_element_type=f32)
                dq_ref[rows, :] = dq * LN2 if diagonal else dq

        def below(ki, carry):
            step(pl.multiple_of(ki * t, t), False)
            return carry

        lax.fori_loop(0, qi, below, 0)
        step(pl.multiple_of(qi * t, t), True)

    qspec = lambda w: pl.BlockSpec((None, t, w), lambda h, qi: (h, qi, 0))
    kspec = lambda w: pl.BlockSpec((None, s, w), lambda h, qi: (h, 0, 0))
    return pl.pallas_call(
        body, name="attn_bwd_dq", grid=(N_HEADS, nb),
        in_specs=[qspec(HEAD_LANES), kspec(HEAD_LANES), kspec(V_DIM), qspec(V_DIM), qspec(128), qspec(128)],
        out_specs=qspec(HEAD_LANES),
        out_shape=jax.ShapeDtypeStruct((N_HEADS, s, HEAD_LANES), f32),
    )(q, k, v, do, lse, delta)


def attn_bwd_dkv(q, k, v, do, lse, delta, job=None):
    s = q.shape[1]
    t = _token_block(s)
    nb = s // t

    cb = min(ATTN_ROWS_BWD, t)

    def body(q_ref, k_ref, v_ref, do_ref, lse_ref, delta_ref, dk_ref, dv_ref):
        ki = pl.program_id(1)
        dk_ref[...] = jnp.zeros(dk_ref.shape, f32)
        dv_ref[...] = jnp.zeros(dv_ref.shape, f32)

        def step(q0, diagonal):
            for c in range(t // cb):
                keys = pl.ds(c * cb, cb)
                r0 = c * cb if diagonal else 0
                rows = pl.ds(q0 + r0, t - r0)
                q = q_ref[rows, :]
                do = do_ref[rows, :].astype(bf16)
                sc = _scores(q, k_ref[keys, :])
                if diagonal:
                    sc = jnp.where(_tril(t - r0, cb, 0), sc, NEG)
                p = jnp.exp2(sc - lse_ref[rows, :1])
                dp = lax.dot_general(do, v_ref[keys, :], (((1,), (1,)), ((), ())), preferred_element_type=f32)
                ds = p * (dp - delta_ref[rows, :1])
                dv_ref[keys, :] += lax.dot_general(p.astype(bf16), do, (((0,), (0,)), ((), ())), preferred_element_type=f32)
                dk_ref[keys, :] += lax.dot_general(ds.astype(bf16), q, (((0,), (0,)), ((), ())), preferred_element_type=f32)

        step(pl.multiple_of(ki * t, t), True)

        def above(qi, carry):
            step(pl.multiple_of(qi * t, t), False)
            return carry

        lax.fori_loop(ki + 1, nb, above, 0)
        dk_ref[...] = dk_ref[...] * LN2

    qspec = lambda w: pl.BlockSpec((None, s, w), lambda h, ki: (h, 0, 0))
    kspec = lambda w: pl.BlockSpec((None, t, w), lambda h, ki: (h, ki, 0))
    return _call_with_job(
        body, "attn_bwd_dkv" if job is None else "attn_bwd_dkv_comm", (N_HEADS, nb), job,
        in_specs=[qspec(HEAD_LANES), kspec(HEAD_LANES), kspec(V_DIM), qspec(V_DIM), qspec(128), qspec(128)],
        out_specs=[kspec(HEAD_LANES), kspec(V_DIM)],
        out_shape=[jax.ShapeDtypeStruct((N_HEADS, s, HEAD_LANES), f32), jax.ShapeDtypeStruct((N_HEADS, s, V_DIM), f32)],
        scratch_shapes=[], operands=(q, k, v, do, lse, delta))


def ssd_bwd(px, pz, plast, states, dyg, params):
    s = px.shape[0]
    nc = s // CHUNK
    per = CHUNK // HALO

    def body(px_ref, halo_ref, pz_ref, pl_ref, st_ref, dyg_ref, cw_ref, cb_ref, dtb_ref, alog_ref, dskip_ref, snw_ref,
             dpx_ref, dpz_ref, dpl_ref, dcw_ref, dcb_ref, ddtb_ref, dalog_ref, ddskip_ref, dsnw_ref, dstate_sc, dhalo_sc):
        t = pl.program_id(0)
        chunk = nc - 1 - t

        @pl.when(t == 0)
        def _():
            dstate_sc[...] = jnp.zeros(dstate_sc.shape, f32)
            dhalo_sc[...] = jnp.zeros(dhalo_sc.shape, f32)

        halo = jnp.where(chunk > 0, halo_ref[...], 0.0)
        xext = jnp.concatenate([halo, px_ref[...]], axis=0)
        _, vjp = jax.vjp(_f_ssd, xext, pz_ref[...], pl_ref[...], st_ref[...], cw_ref[...], cb_ref[...], dtb_ref[...],
                         alog_ref[...], dskip_ref[...], snw_ref[...])
        dxext, dz, dpl, dprev, dcw, dcb, ddtb, dalog, ddskip, dsnw = vjp((dyg_ref[...], dstate_sc[...]))
        dpx_ref[...] = dxext[HALO:]
        dpx_ref[CHUNK - HALO:, :] += dhalo_sc[...]
        dhalo_sc[...] = dxext[:HALO]
        dstate_sc[...] = dprev
        dpz_ref[...] = dz
        dpl_ref[...] = dpl
        _accumulate(t == 0, [dcw_ref, dcb_ref, ddtb_ref, dalog_ref, ddskip_ref, dsnw_ref],
                    [dcw, dcb, ddtb, dalog, ddskip, dsnw])

    rev = lambda w: pl.BlockSpec((CHUNK, w), lambda t: (nc - 1 - t, 0))
    pshapes = [jax.ShapeDtypeStruct((4, D_CONV), f32), jax.ShapeDtypeStruct((1, D_CONV), f32),
               jax.ShapeDtypeStruct((1, 128), f32), jax.ShapeDtypeStruct((1, 128), f32),
               jax.ShapeDtypeStruct((1, 128), f32), jax.ShapeDtypeStruct((1, D_SSD), f32)]
    return pl.pallas_call(
        body, name="ssd_bwd", grid=(nc,),
        in_specs=[rev(D_CONV),
                  pl.BlockSpec((HALO, D_CONV), lambda t: (jnp.maximum((nc - 1 - t) * per - 1, 0), 0)),
                  rev(D_SSD), rev(128),
                  pl.BlockSpec((None, N_HEADS, SSD_HEAD_DIM, SSD_STATE), lambda t: (nc - 1 - t, 0, 0, 0)),
                  rev(D_SSD)] + _ssd_param_specs(),
        out_specs=[rev(D_CONV), rev(D_SSD), rev(128)] + _ssd_param_specs(),
        out_shape=[jax.ShapeDtypeStruct((s, D_CONV), f32), jax.ShapeDtypeStruct((s, D_SSD), f32),
                   jax.ShapeDtypeStruct((s, 128), f32)] + pshapes,
        scratch_shapes=[pltpu.VMEM((N_HEADS, SSD_HEAD_DIM, SSD_STATE), f32), pltpu.VMEM((HALO, D_CONV), f32)],
    )(px, px, pz, plast, states, dyg, *params)


def qkv_bwd(pa, plast, cos_t, sin_t, params, dq, dk, dv):
    s = pa.shape[0]
    ts = _token_block(s)

    def body(pa_ref, pl_ref, cos_ref, sin_ref, *rest):
        prm = [r[...].astype(f32) for r in rest[:8]]
        dq_ref, dk_ref, dv_ref = rest[8:11]
        dpa_ref, dpl_ref = rest[11:13]
        dprm_refs = list(rest[13:])
        cos_t, sin_t = cos_ref[...], sin_ref[...]
        _, vjp = jax.vjp(lambda a, b, *p: _f_qkv(a, b, cos_t, sin_t, *p), pa_ref[...], pl_ref[...], *prm)
        grads = vjp((dq_ref[...], dk_ref[...], dv_ref[...]))
        dpa_ref[...] = grads[0]
        dpl_ref[...] = grads[1]
        _accumulate(pl.program_id(0) == 0, dprm_refs, list(grads[2:]))

    tok = lambda w: pl.BlockSpec((ts, w), lambda i: (i, 0))
    head = lambda w: pl.BlockSpec((N_HEADS, ts, w), lambda i: (0, i, 0))
    pshapes = [jax.ShapeDtypeStruct((1, Q_RANK), f32), jax.ShapeDtypeStruct((1, KV_RANK), f32),
               jax.ShapeDtypeStruct((N_HEADS, Q_RANK, HEAD_LANES), f32), jax.ShapeDtypeStruct((N_HEADS, KV_RANK, HEAD_LANES), f32),
               jax.ShapeDtypeStruct((N_HEADS, KV_RANK, V_DIM), f32), jax.ShapeDtypeStruct((1, HEAD_LANES), f32),
               jax.ShapeDtypeStruct((1, HEAD_LANES), f32), jax.ShapeDtypeStruct((1, HEAD_LANES), f32)]
    return pl.pallas_call(
        body, name="qkv_bwd", grid=(s // ts,),
        in_specs=[tok(384), tok(128), tok(128), tok(128)] + _qkv_param_specs()
                 + [head(HEAD_LANES), head(HEAD_LANES), head(V_DIM)],
        out_specs=[tok(384), tok(128)] + _qkv_param_specs(),
        out_shape=[jax.ShapeDtypeStruct((s, 384), f32), jax.ShapeDtypeStruct((s, 128), f32)] + pshapes,
    )(pa, plast, cos_t, sin_t, *params, dq, dk, dv)


def proj_bwd(x, nw, sh, sc, w, dpa, dpz, dpx, dpl_k, dpl_dt, dres):
    s = x.shape[0]
    ts = _token_block(s)

    ni = s // ts

    def body(x_ref, nw_ref, sh_ref, sc_ref, w_ref, dpa_ref, dpz_ref, dpx_ref, dplk_ref, dpld_ref, dres_ref,
             dx_ref, dnw_ref, dsh_ref, dsc_ref, dw_ref, acc_sc):
        i = pl.program_id(0)
        g = jnp.concatenate([dpa_ref[...], dpz_ref[...], dpx_ref[...], dplk_ref[...] + dpld_ref[...]], axis=1)
        _, vjp = jax.vjp(_f_proj, x_ref[...], nw_ref[...], sh_ref[...], sc_ref[...], w_ref[...].astype(f32))
        dx, dnw, dsh, dsc, dw = vjp(g)
        dx_ref[...] = dx + dres_ref[...]
        _accumulate(i == 0, [dnw_ref, dsh_ref, dsc_ref], [dnw, dsh, dsc])
        _accumulate_then_cast(i == 0, i == ni - 1, [acc_sc], [dw_ref], [dw])

    vec = _const((1, D_MODEL))
    vshape = jax.ShapeDtypeStruct((1, D_MODEL), f32)
    tok = lambda w_: pl.BlockSpec((ts, w_), lambda i: (i, 0))
    return pl.pallas_call(
        body, name="proj_bwd", grid=(ni,), scratch_shapes=[pltpu.VMEM((D_MODEL, D_PROJ), f32)],
        in_specs=[tok(D_MODEL), vec, vec, vec, _const((D_MODEL, D_PROJ)), tok(384), tok(512), tok(1024), tok(128), tok(128),
                  tok(D_MODEL)],
        out_specs=[tok(D_MODEL), vec, vec, vec, _const((D_MODEL, D_PROJ))],
        out_shape=[jax.ShapeDtypeStruct((s, D_MODEL), f32), vshape, vshape, vshape,
                   jax.ShapeDtypeStruct((D_MODEL, D_PROJ), bf16)],
    )(x, nw, sh, sc, w, dpa, dpz, dpx, dpl_k, dpl_dt, dres)


def ada_fwd(c_all, w_ada, b_cols):
    def body(c_ref, w_ref, b_ref, out_ref):
        act = jax.nn.silu(c_ref[...])
        for l in range(2):
            out_ref[l] = jnp.dot(act, w_ref[l], precision=lax.Precision.HIGHEST, preferred_element_type=f32) + b_ref[l]

    return pl.pallas_call(body, name="ada_fwd", out_shape=jax.ShapeDtypeStruct((2, N_DEV, 768), f32))(c_all, w_ada, b_cols)


def ada_bwd(c_all, dmod_cols):
    def body(c_ref, d_ref, out_ref):
        out_ref[0] = lax.dot_general(jax.nn.silu(c_ref[...]), d_ref[0], (((0,), (0,)), ((), ())),
                                     precision=lax.Precision.HIGHEST, preferred_element_type=f32)

    return pl.pallas_call(
        body, name="ada_bwd", grid=(2,),
        in_specs=[_const((N_DEV, D_MODEL)), pl.BlockSpec((1, N_DEV, 768), lambda l: (l, 0, 0))],
        out_specs=pl.BlockSpec((1, D_MODEL, 768), lambda l: (l, 0, 0)),
        out_shape=jax.ShapeDtypeStruct((2, D_MODEL, 768), f32),
    )(c_all, dmod_cols)


def _adamw(w, g, m, v):
    m = ADAM_B1 * m + (1.0 - ADAM_B1) * g
    v = ADAM_B2 * v + (1.0 - ADAM_B2) * (g * g)
    m_hat = m / (1.0 - ADAM_B1 ** ADAM_STEP)
    v_hat = v / (1.0 - ADAM_B2 ** ADAM_STEP)
    delta = -ADAM_LR * (m_hat / (jnp.sqrt(v_hat) + ADAM_EPS) + ADAM_WD * w)
    return delta, m, v


def adamw(parts, w, m, v, layer, prev, name):
    n, r, c = parts.shape
    nl = w.shape[0]
    tr = r
    lanes = -(-c // 128) * 128
    if 2 * (n + 7) * r * lanes * 4 > ADAMW_BLOCK_BYTES:
        tr = next(t for t in (256, 128, 64, 32, 16, 8) if r % t == 0)

    def body(p_ref, w_ref, m_ref, v_ref, *rest):
        g_ref, d_ref, nm_ref, nv_ref = rest[-4:]
        g = p_ref[0].astype(f32)
        for k in range(1, n):
            g = g + p_ref[k].astype(f32)
        delta, nm, nv = _adamw(w_ref[...], g, m_ref[...], v_ref[...])
        g_ref[...] = g
        d_ref[...] = delta
        nm_ref[...] = nm
        nv_ref[...] = nv

    blk = pl.BlockSpec((None, tr, c), lambda i: (layer, i, 0))
    shp = jax.ShapeDtypeStruct((nl, r, c), f32)
    kept = [] if prev is None else list(prev)
    return pl.pallas_call(
        body, name=name, grid=(r // tr,),
        in_specs=[pl.BlockSpec((n, tr, c), lambda i: (0, i, 0)), blk, blk, blk] + [ANY] * len(kept),
        out_specs=[blk] * 4, out_shape=[shp] * 4,
        input_output_aliases={4 + j: j for j in range(len(kept))},
    )(parts, w, m, v, *kept)


def _my_index():
    return 4 * lax.axis_index("x") + 2 * lax.axis_index("y") + lax.axis_index("c")


def _coords(idx):
    return (idx // 4, (idx // 2) % 2, idx % 2)


class CommJob:
    def __init__(self, operands, out_shape, start, finish, n_arrays):
        self.operands, self.out_shape, self.start, self.finish = operands, out_shape, start, finish
        self.scratch = [pltpu.SemaphoreType.DMA((n_arrays,)), pltpu.SemaphoreType.DMA((n_arrays,)),
                        pltpu.SemaphoreType.DMA((n_arrays,))]


def _drain(out, k, sems):
    send_sems, recv_sems, _ = sems
    seven = out.at[pl.ds(0, N_DEV - 1)]
    wait = pltpu.make_async_remote_copy(src_ref=seven, dst_ref=seven, send_sem=send_sems.at[k], recv_sem=recv_sems.at[k],
                                        device_id=_coords(_my_index()), device_id_type=MESH)
    wait.wait_recv()
    wait.wait_send()


def gather_job(shards):
    n = len(shards)

    def start(ins, outs, sems):
        send_sems, recv_sems, local_sems = sems
        me = _my_index()
        for k in range(n):
            pltpu.make_async_copy(ins[k], outs[k].at[me], local_sems.at[k]).start()
        for p in range(1, N_DEV):
            peer = (me + p) % N_DEV
            for k in range(n):
                pltpu.make_async_remote_copy(src_ref=ins[k], dst_ref=outs[k].at[me], send_sem=send_sems.at[k],
                                             recv_sem=recv_sems.at[k], device_id=_coords(peer), device_id_type=MESH).start()

    def finish(ins, outs, sems):
        for k in range(n):
            _drain(outs[k], k, sems)
            pltpu.make_async_copy(ins[k], outs[k].at[0], sems[2].at[k]).wait()

    shapes = [jax.ShapeDtypeStruct((N_DEV,) + tuple(a.shape), a.dtype) for a in shards]
    return CommJob(list(shards), shapes, start, finish, n)


def scatter_job(tensors):
    n = len(tensors)
    flat, where = [], {}
    for k, pieces in enumerate(tensors):
        d = 0
        for piece in pieces:
            for b in range(piece.shape[0]):
                where[k, d] = (len(flat), b)
                d += 1
            flat.append(piece)
        assert d == N_DEV

    def start(ins, outs, sems):
        send_sems, recv_sems, local_sems = sems
        me = _my_index()

        def block(k, d):
            i, b = where[k, d]
            return ins[i].at[b]

        for d in range(N_DEV):
            @pl.when(d != me)
            def _():
                for k in range(n):
                    pltpu.make_async_remote_copy(src_ref=block(k, d), dst_ref=outs[k].at[me], send_sem=send_sems.at[k],
                                                 recv_sem=recv_sems.at[k], device_id=(d // 4, (d // 2) % 2, d % 2),
                                                 device_id_type=MESH).start()

            @pl.when(d == me)
            def _():
                for k in range(n):
                    pltpu.make_async_copy(block(k, d), outs[k].at[d], local_sems.at[k]).start()

    def finish(ins, outs, sems):
        for k in range(n):
            _drain(outs[k], k, sems)
            i, b = where[k, 0]
            pltpu.make_async_copy(ins[i].at[b], outs[k].at[0], sems[2].at[k]).wait()

    shapes = [jax.ShapeDtypeStruct((N_DEV,) + tuple(p[0].shape[1:]), p[0].dtype) for p in tensors]
    return CommJob(flat, shapes, start, finish, n)


def comm_call(job, name):
    ni, no = len(job.operands), len(job.out_shape)

    def body(*refs):
        ins, outs, sems = refs[:ni], refs[ni:ni + no], refs[ni + no:]
        job.start(ins, outs, sems)
        job.finish(ins, outs, sems)

    return pl.pallas_call(body, name=name, in_specs=[ANY] * ni, out_specs=[ANY] * no, out_shape=job.out_shape,
                          scratch_shapes=job.scratch)(*job.operands)


def _carry(job, body, n_in, n_out, first, last):
    ji, jo = len(job.operands), len(job.out_shape)

    def carrier(*refs):
        a, b = n_in, n_in + ji
        c, d = b + n_out, b + n_out + jo
        job_refs = (refs[a:b], refs[c:d], refs[len(refs) - 3:])

        @pl.when(first())
        def _():
            job.start(*job_refs)

        body(*refs[:a], *refs[b:c], *refs[d:len(refs) - 3])

        @pl.when(last())
        def _():
            job.finish(*job_refs)

    return carrier


def _pad_lanes(v, lo, total=128):
    return jnp.pad(v, (lo, total - lo - v.shape[0]))[None, :]


def _layer_weights(lw):
    w_in = lw["w_in"]
    z = jnp.zeros((D_MODEL, 1), w_in.dtype)
    w_proj = jnp.concatenate(
        [w_in[:, :384], w_in[:, 416:928], w_in[:, 928:1952], w_in[:, 1952:1960], jnp.tile(z, (1, 56)),
         w_in[:, 384:416], jnp.tile(z, (1, 32))], axis=1)
    wq = jnp.pad(lw["w_q_up"], ((0, 0), (0, 0), (0, HEAD_LANES - NOPE - ROPE)))
    wk = jnp.pad(lw["w_kv_up"][:, :, :NOPE], ((0, 0), (0, 0), (0, HEAD_LANES - NOPE)))
    wv = lw["w_kv_up"][:, :, NOPE:]
    qkv = (lw["q_a_norm_w"][None, :], lw["kv_a_norm_w"][None, :], wq, wk, wv,
           _pad_lanes(jnp.concatenate([lw["q_nope_norm_w"], lw["q_pe_norm_w"]]), 0),
           _pad_lanes(lw["k_nope_norm_w"], 0), _pad_lanes(lw["k_pe_norm_w"], NOPE))
    ssd = (lw["conv_w"], lw["conv_b"][None, :], _pad_lanes(lw["dt_bias"], 0), _pad_lanes(lw["a_log"], 0),
           _pad_lanes(lw["d_skip"], 0), lw["ssd_norm_w"][None, :])
    return dict(w_proj=w_proj, qkv=qkv, ssd=ssd, wo=lw["w_out"], wgu=lw["w_gate_up"], wd=lw["w_down"],
                n1=lw["norm1_w"][None, :], n2=lw["norm2_w"][None, :])


def layer_fwd(x, mod, kw, cos_t, sin_t, job=None):
    sh1, sc1, g1, sh2, sc2, g2 = [mod[i:i + 1] for i in range(6)]
    pa, pz, px, plast = proj_fwd(x, kw["n1"], sh1, sc1, kw["w_proj"])
    q, k, v = qkv_fwd(pa, plast, cos_t, sin_t, kw["qkv"])
    (o, lse), carried = attn_fwd(q, k, v, job)
    yg, states = ssd_fwd(px, pz, plast, kw["ssd"])
    x_mid = out_fwd(x, o, yg, g1, kw["wo"])
    x_out = mlp_fwd(x_mid, kw["n2"], sh2, sc2, g2, kw["wgu"], kw["wd"])
    saved = dict(x=x, pa=pa, pz=pz, px=px, plast=plast, q=q, k=k, v=v, o=o, lse=lse, yg=yg, states=states, x_mid=x_mid)
    return x_out, saved, carried


def layer_bwd_head(dy, mod, kw, sv, job=None):
    _, _, g1, sh2, sc2, g2 = [mod[i:i + 1] for i in range(6)]
    (dparts, dn2, dsh2, dsc2, dg2, dwg, dwu, dwd), carried = mlp_bwd(
        sv["x_mid"], dy, kw["n2"], sh2, sc2, g2, kw["wgu"], kw["wd"], job)
    dmid, do, delta, dyg, dg1, dwo = out_bwd(dy, dparts, sv["o"], sv["yg"], g1, kw["wo"])
    dq = attn_bwd_dq(sv["q"], sv["k"], sv["v"], do, sv["lse"], delta)
    early = dict(w_out=[dwo.reshape(N_DEV, D_MODEL // N_DEV, D_MODEL)], w_gate_up=[dwg, dwu],
                 w_down=[dwd.reshape(N_DEV, D_FF // N_DEV, D_MODEL)])
    head = dict(dmid=dmid, do=do, delta=delta, dyg=dyg, dq=dq, dn2=dn2, dsh2=dsh2, dsc2=dsc2, dg2=dg2, dg1=dg1)
    return head, early, carried


def layer_bwd_tail(hd, mod, kw, cos_t, sin_t, sv, job=None):
    sh1, sc1 = mod[0:1], mod[1:2]
    (dk, dv), carried = attn_bwd_dkv(sv["q"], sv["k"], sv["v"], hd["do"], sv["lse"], hd["delta"], job)
    dpx, dpz, dpl_dt, dcw, dcb, ddtb, dalog, ddskip, dsnw = ssd_bwd(sv["px"], sv["pz"], sv["plast"], sv["states"],
                                                                   hd["dyg"], kw["ssd"])
    dpa, dpl_k, dqaw, dkvaw, dwq, dwk, dwv, dqnw, dknw, dkpw = qkv_bwd(sv["pa"], sv["plast"], cos_t, sin_t, kw["qkv"],
                                                                       hd["dq"], dk, dv)
    dx, dn1, dsh1, dsc1, dwp = proj_bwd(sv["x"], kw["n1"], sh1, sc1, kw["w_proj"], dpa, dpz, dpx, dpl_k, dpl_dt, hd["dmid"])
    dmod = jnp.concatenate([dsh1, dsc1, hd["dg1"], hd["dsh2"], hd["dsc2"], hd["dg2"]], axis=0)
    dw_in = jnp.concatenate([dwp[:, :384], dwp[:, 1984:2016], dwp[:, 384:1920], dwp[:, 1920:1928]], axis=1)
    grads = dict(
        norm1_w=dn1[0], norm2_w=hd["dn2"][0], q_a_norm_w=dqaw[0], kv_a_norm_w=dkvaw[0],
        q_nope_norm_w=dqnw[0, :NOPE], q_pe_norm_w=dqnw[0, NOPE:NOPE + ROPE], k_nope_norm_w=dknw[0, :NOPE],
        k_pe_norm_w=dkpw[0, NOPE:NOPE + ROPE], conv_b=dcb[0], dt_bias=ddtb[0, :N_HEADS], a_log=dalog[0, :N_HEADS],
        d_skip=ddskip[0, :N_HEADS], ssd_norm_w=dsnw[0],
        w_in=[dw_in.reshape(D_MODEL, N_DEV, D_IN // N_DEV).transpose(1, 0, 2)],
        w_q_up=[dwq[:, :, :NOPE + ROPE].astype(bf16)],
        w_kv_up=[jnp.concatenate([dwk[:, :, :NOPE], dwv], axis=2).astype(bf16)],
        conv_w=[dcw.reshape(4, N_DEV, D_CONV // N_DEV).transpose(1, 0, 2).astype(bf16)],
    )
    return dx, dmod, grads, carried


def _natural_weights(gathered):
    g = gathered
    return dict(
        w_in=g["w_in"].transpose(1, 0, 2).reshape(D_MODEL, D_IN),
        w_q_up=g["w_q_up"], w_kv_up=g["w_kv_up"],
        conv_w=g["conv_w"].astype(f32).transpose(1, 0, 2).reshape(4, D_CONV),
        w_out=g["w_out"].reshape(D_MODEL, D_MODEL),
        w_gate_up=g["w_gate_up"],
        w_down=g["w_down"].reshape(N_DEV // 2, FF_SHARD, D_MODEL),
    )


def _pack_small(get):
    flat = jnp.concatenate([get(name).reshape(-1) for name, _ in SMALL])
    return jnp.pad(flat, (0, SMALL_ROWS * 128 - flat.shape[0])).reshape(SMALL_ROWS, 128)


def _unpack_small(packed):
    flat = packed.reshape(-1)
    out, off = {}, 0
    for name, size in SMALL:
        out[name] = flat[off:off + 2 * size].reshape(2, size)
        off += 2 * size
    return out


def kernel(x, c, positions, norm1_w, norm2_w, w_ada, b_ada, w_in, q_a_norm_w, w_q_up, kv_a_norm_w, w_kv_up, q_nope_norm_w, q_pe_norm_w, k_nope_norm_w, k_pe_norm_w, conv_w, conv_b, dt_bias, a_log, d_skip, ssd_norm_w, w_out, w_gate_up, w_down, loss_target, m_norm1_w, m_norm2_w, m_w_ada, m_b_ada, m_w_in, m_q_a_norm_w, m_w_q_up, m_kv_a_norm_w, m_w_kv_up, m_q_nope_norm_w, m_q_pe_norm_w, m_k_nope_norm_w, m_k_pe_norm_w, m_conv_w, m_conv_b, m_dt_bias, m_a_log, m_d_skip, m_ssd_norm_w, m_w_out, m_w_gate_up, m_w_down, v_norm1_w, v_norm2_w, v_w_ada, v_b_ada, v_w_in, v_q_a_norm_w, v_w_q_up, v_kv_a_norm_w, v_w_kv_up, v_q_nope_norm_w, v_q_pe_norm_w, v_k_nope_norm_w, v_k_pe_norm_w, v_conv_w, v_conv_b, v_dt_bias, v_a_log, v_d_skip, v_ssd_norm_w, v_w_out, v_w_gate_up, v_w_down):
    w = dict(norm1_w=norm1_w, norm2_w=norm2_w, w_ada=w_ada, b_ada=b_ada, w_in=w_in, q_a_norm_w=q_a_norm_w, w_q_up=w_q_up,
             kv_a_norm_w=kv_a_norm_w, w_kv_up=w_kv_up, q_nope_norm_w=q_nope_norm_w, q_pe_norm_w=q_pe_norm_w,
             k_nope_norm_w=k_nope_norm_w, k_pe_norm_w=k_pe_norm_w, conv_w=conv_w, conv_b=conv_b, dt_bias=dt_bias,
             a_log=a_log, d_skip=d_skip, ssd_norm_w=ssd_norm_w, w_out=w_out, w_gate_up=w_gate_up, w_down=w_down)
    m = dict(norm1_w=m_norm1_w, norm2_w=m_norm2_w, w_ada=m_w_ada, b_ada=m_b_ada, w_in=m_w_in, q_a_norm_w=m_q_a_norm_w,
             w_q_up=m_w_q_up, kv_a_norm_w=m_kv_a_norm_w, w_kv_up=m_w_kv_up, q_nope_norm_w=m_q_nope_norm_w,
             q_pe_norm_w=m_q_pe_norm_w, k_nope_norm_w=m_k_nope_norm_w, k_pe_norm_w=m_k_pe_norm_w, conv_w=m_conv_w,
             conv_b=m_conv_b, dt_bias=m_dt_bias, a_log=m_a_log, d_skip=m_d_skip, ssd_norm_w=m_ssd_norm_w, w_out=m_w_out,
             w_gate_up=m_w_gate_up, w_down=m_w_down)
    v = dict(norm1_w=v_norm1_w, norm2_w=v_norm2_w, w_ada=v_w_ada, b_ada=v_b_ada, w_in=v_w_in, q_a_norm_w=v_q_a_norm_w,
             w_q_up=v_w_q_up, kv_a_norm_w=v_kv_a_norm_w, w_kv_up=v_w_kv_up, q_nope_norm_w=v_q_nope_norm_w,
             q_pe_norm_w=v_q_pe_norm_w, k_nope_norm_w=v_k_nope_norm_w, k_pe_norm_w=v_k_pe_norm_w, conv_w=v_conv_w,
             conv_b=v_conv_b, dt_bias=v_dt_bias, a_log=v_a_log, d_skip=v_d_skip, ssd_norm_w=v_ssd_norm_w, w_out=v_w_out,
             w_gate_up=v_w_gate_up, w_down=v_w_down)
    me = _my_index()
    seq = x.shape[1]

    def shards_of(l):
        return [w[name][l] if name == "conv_w" else w[name][l].astype(bf16) for name in BIG]

    def layer_operands(gathered, l):
        lw = _natural_weights(dict(zip(BIG, gathered)))
        for name, _ in SMALL:
            if name != "b_ada":
                lw[name] = w[name][l]
        return _layer_weights(lw)

    first = comm_call(gather_job([c] + shards_of(0)), "gather_layer0")
    c_all = first[0].reshape(N_DEV, D_MODEL)
    kws = [layer_operands(first[1:], 0), None]

    b_cols = lax.dynamic_slice_in_dim(b_ada, me * 768, 768, axis=1)
    mod_cols = ada_fwd(c_all, w_ada, b_cols)
    (mod_all,) = comm_call(gather_job([mod_cols]), "gather_mod")
    mod_me = lax.dynamic_index_in_dim(mod_all, me, axis=2, keepdims=False)
    mods = [mod_me[:, l, :].reshape(6, D_MODEL) for l in range(2)]

    inv_freq = 1.0 / (ROPE_THETA ** (jnp.arange(0, ROPE, 2, dtype=f32) / ROPE))
    inv = _pad_lanes(jnp.concatenate([inv_freq, inv_freq]), NOPE)
    cos_t, sin_t = rope_tables(positions.reshape(seq, 1), inv)

    saved = [None, None]
    h, saved[0], second = layer_fwd(x[0], mods[0], kws[0], cos_t, sin_t, gather_job(shards_of(1)))
    kws[1] = layer_operands(second, 1)
    h, saved[1], _ = layer_fwd(h, mods[1], kws[1], cos_t, sin_t)
    dy, loss_part = loss_fwd(h, loss_target[0])
    loss = lax.psum(loss_part[0, 0], ("x", "y", "c"))

    early, late = ("w_out", "w_gate_up", "w_down"), ("w_in", "w_q_up", "w_kv_up", "conv_w")
    parts = [{}, {}]
    head, pieces, _ = layer_bwd_head(dy, mods[1], kws[1], saved[1])
    dy, dmod1, grads1, got = layer_bwd_tail(head, mods[1], kws[1], cos_t, sin_t, saved[1], scatter_job([pieces[n] for n in early]))
    parts[1].update(zip(early, got))
    head, pieces, got = layer_bwd_head(dy, mods[0], kws[0], saved[0], scatter_job([grads1[n] for n in late]))
    parts[1].update(zip(late, got))
    dy, dmod0, grads0, got = layer_bwd_tail(head, mods[0], kws[0], cos_t, sin_t, saved[0], scatter_job([pieces[n] for n in early]))
    parts[0].update(zip(early, got))
    parts[0].update(zip(late, comm_call(scatter_job([grads0[n] for n in late]), "scatter_layer0_rest")))
    grad_x = dy[None]

    small_part = {name: jnp.stack([grads0[name], grads1[name]]) for name, _ in SMALL if name != "b_ada"}
    small_part["b_ada"] = jnp.stack([dmod0.reshape(-1), dmod1.reshape(-1)])
    (small_all,) = comm_call(gather_job([_pack_small(lambda n: small_part[n])]), "gather_small_grads")
    packed = adamw(small_all, _pack_small(lambda n: w[n])[None], _pack_small(lambda n: m[n])[None],
                   _pack_small(lambda n: v[n])[None], 0, None, "adamw_small")
    res = {}
    for key, arr in zip("gdmv", packed):
        for name, val in _unpack_small(arr[0]).items():
            res[key, name] = val

    off = 2 * (1024 + 1024)
    dmod_all = small_all.reshape(N_DEV, -1)[:, off:off + 2 * 6144].reshape(N_DEV, 2, 6144)
    dmod_cols = lax.dynamic_slice_in_dim(dmod_all, me * 768, 768, axis=2).transpose(1, 0, 2)
    g_ada = ada_bwd(c_all, dmod_cols)
    out = None
    for l in range(2):
        out = adamw(g_ada[l][None], w_ada, m_w_ada, v_w_ada, l, out, "adamw_w_ada")
    res.update(zip([(key, "w_ada") for key in "gdmv"], out))

    for name in BIG:
        out = None
        for l in range(2):
            out = adamw(parts[l][name], w[name], m[name], v[name], l, out, "adamw_" + name)
        res.update(zip([(key, name) for key in "gdmv"], out))

    return (loss, grad_x, *[res["g", n] for n in WEIGHTS], *[res["d", n] for n in WEIGHTS],
            *[res["m", n] for n in WEIGHTS], *[res["v", n] for n in WEIGHTS])
```

```python
import functools

import jax
import jax.numpy as jnp
from jax import lax
from jax.experimental import pallas as pl
from jax.experimental.pallas import tpu as pltpu

f32 = jnp.float32
bf16 = jnp.bfloat16

N_DEV = 8
D_MODEL = 1024
N_HEADS = 8
HEAD_LANES = 128
NOPE = 64
ROPE = 32
V_DIM = 64
Q_RANK = 256
KV_RANK = 128
D_SSD = 512
D_CONV = 1024
SSD_STATE = 128
SSD_HEAD_DIM = 64
CHUNK = 128
HALO = 8
D_FF = 2816
FF_SHARD = 704
D_IN = 1960
D_PROJ = 2048
EPS = 1e-6
LOG2E = 1.4426950408889634
LN2 = 0.6931471805599453
Q_SCALE = (NOPE + ROPE) ** -0.5 * LOG2E
ATTN_ROWS_FWD = 256
ATTN_ROWS_BWD = 512
ROPE_THETA = 10000.0
NEG = -1e30

ADAM_LR = 0.001
ADAM_B1 = 0.9
ADAM_B2 = 0.999
ADAM_EPS = 1e-08
ADAM_WD = 0.01
ADAM_STEP = 10
ADAMW_BLOCK_BYTES = 24 << 20

MESH = pl.DeviceIdType.MESH
ANY = pl.BlockSpec(memory_space=pl.ANY)

SMALL = (("norm1_w", 1024), ("norm2_w", 1024), ("b_ada", 6144), ("q_a_norm_w", 256), ("kv_a_norm_w", 128),
         ("q_nope_norm_w", 64), ("q_pe_norm_w", 32), ("k_nope_norm_w", 64), ("k_pe_norm_w", 32),
         ("conv_b", 1024), ("dt_bias", 8), ("a_log", 8), ("d_skip", 8), ("ssd_norm_w", 512))
SMALL_ROWS = 168
BIG = ("w_in", "w_q_up", "w_kv_up", "conv_w", "w_out", "w_gate_up", "w_down")
WEIGHTS = ("norm1_w", "norm2_w", "w_ada", "b_ada", "w_in", "q_a_norm_w", "w_q_up", "kv_a_norm_w", "w_kv_up",
           "q_nope_norm_w", "q_pe_norm_w", "k_nope_norm_w", "k_pe_norm_w", "conv_w", "conv_b", "dt_bias",
           "a_log", "d_skip", "ssd_norm_w", "w_out", "w_gate_up", "w_down")


def _dot(a, b, ca, cb):
    return lax.dot_general(a.astype(bf16), b.astype(bf16), (((ca,), (cb,)), ((), ())), preferred_element_type=f32)


@jax.custom_vjp
def mm(a, b):
    return _dot(a, b, 1, 0)


def _mm_fwd(a, b):
    return _dot(a, b, 1, 0), (a, b)


def _mm_bwd(res, g):
    a, b = res
    return _dot(g, b, 1, 1).astype(a.dtype), _dot(a, g, 0, 0).astype(b.dtype)


mm.defvjp(_mm_fwd, _mm_bwd)


@jax.custom_vjp
def mm_nt(a, b):
    return _dot(a, b, 1, 1)


def _mm_nt_fwd(a, b):
    return _dot(a, b, 1, 1), (a, b)


def _mm_nt_bwd(res, g):
    a, b = res
    return _dot(g, b, 1, 0).astype(a.dtype), _dot(g, a, 0, 0).astype(b.dtype)


mm_nt.defvjp(_mm_nt_fwd, _mm_nt_bwd)


@jax.custom_vjp
def mm_tn(a, b):
    return _dot(a, b, 0, 0)


def _mm_tn_fwd(a, b):
    return _dot(a, b, 0, 0), (a, b)


def _mm_tn_bwd(res, g):
    a, b = res
    return _dot(b, g, 1, 1).astype(a.dtype), _dot(a, g, 1, 0).astype(b.dtype)


mm_tn.defvjp(_mm_tn_fwd, _mm_tn_bwd)


def _rms(x, w):
    return x * lax.rsqrt(jnp.mean(x * x, axis=-1, keepdims=True) + EPS) * w


def _const(shape):
    n = len(shape)
    return pl.BlockSpec(shape, lambda *_: (0,) * n)


def _accumulate(first, refs, vals):
    @pl.when(first)
    def _():
        for r, v in zip(refs, vals):
            r[...] = v

    @pl.when(jnp.logical_not(first))
    def _():
        for r, v in zip(refs, vals):
            r[...] += v


def _accumulate_then_cast(first, last, accs, outs, vals):
    _accumulate(first, accs, vals)

    @pl.when(last)
    def _():
        for a, o in zip(accs, outs):
            o[...] = a[...].astype(o.dtype)


def _token_block(s):
    return min(512, s)


def _f_proj(x, nw, sh, sc, w):
    h = _rms(x, nw) * (1.0 + sc) + sh
    return mm(h, w)


def _f_qkv(pa, plast, cos_t, sin_t, qaw, kvaw, wq, wk, wv, qnw, knw, kpw):
    lane = lax.broadcasted_iota(jnp.int32, (1, HEAD_LANES), 1)
    m_nope = lane < NOPE
    m_pe = (lane >= NOPE) & (lane < NOPE + ROPE)
    rows = pa.shape[0]

    def rope(t):
        half = ROPE // 2
        swapped = jnp.concatenate(
            [jnp.zeros((rows, NOPE), f32), t[:, NOPE + half:NOPE + ROPE], t[:, NOPE:NOPE + half],
             jnp.zeros((rows, HEAD_LANES - NOPE - ROPE), f32)], axis=1)
        return t * cos_t + swapped * sin_t

    qa = _rms(pa[:, :Q_RANK], qaw)
    kva = _rms(pa[:, Q_RANK:Q_RANK + KV_RANK], kvaw)
    kp = jnp.where(m_pe, plast, 0.0)
    kp = kp * lax.rsqrt(jnp.sum(kp * kp, axis=-1, keepdims=True) / ROPE + EPS) * kpw
    k_rot = rope(kp)
    qs, ks, vs = [], [], []
    for h in range(N_HEADS):
        qh = mm(qa, wq[h])
        ss_n = jnp.sum(jnp.where(m_nope, qh * qh, 0.0), axis=-1, keepdims=True) / NOPE
        ss_p = jnp.sum(jnp.where(m_pe, qh * qh, 0.0), axis=-1, keepdims=True) / ROPE
        r = jnp.where(m_nope, lax.rsqrt(ss_n + EPS), lax.rsqrt(ss_p + EPS))
        qs.append(rope(qh * r * qnw) * Q_SCALE)
        kh = mm(kva, wk[h])
        kh = kh * lax.rsqrt(jnp.sum(kh * kh, axis=-1, keepdims=True) / NOPE + EPS) * knw
        ks.append(kh + k_rot)
        vs.append(mm(kva, wv[h]))
    return jnp.stack(qs), jnp.stack(ks), jnp.stack(vs)


def _f_ssd(xext, z, plast, prev, cw, cb, dtb, alog, dskip, snw):
    n = CHUNK
    conv = cb
    for k in range(4):
        conv = conv + cw[k:k + 1] * xext[HALO - 3 + k:HALO - 3 + k + n]
    xc = jax.nn.silu(conv)
    xs, bm, cm = xc[:, :D_SSD], xc[:, D_SSD:D_SSD + 2 * SSD_STATE], xc[:, D_SSD + 2 * SSD_STATE:]
    lane = lax.broadcasted_iota(jnp.int32, (1, 128), 1)
    dt = jax.nn.softplus(jnp.where(lane < N_HEADS, plast, 0.0) + dtb)
    adt = dt * (-jnp.exp(alog))
    row = lax.broadcasted_iota(jnp.int32, (n, n), 0)
    col = lax.broadcasted_iota(jnp.int32, (n, n), 1)
    tri = row >= col
    acs = jnp.dot(tri.astype(f32), adt, precision=lax.Precision.HIGHEST, preferred_element_type=f32)
    acs_t = acs.T
    ys, news = [], []
    for g in range(2):
        bg = bm[:, g * SSD_STATE:(g + 1) * SSD_STATE]
        cg = cm[:, g * SSD_STATE:(g + 1) * SSD_STATE]
        cb_t = mm_nt(cg, bg)
        for r in range(4):
            h = g * 4 + r
            a_col = acs[:, h:h + 1]
            a_row = acs_t[h:h + 1, :]
            decay_ls = jnp.exp(jnp.where(tri, a_col - a_row, -jnp.inf))
            xh = xs[:, h * SSD_HEAD_DIM:(h + 1) * SSD_HEAD_DIM]
            xdt = xh * dt[:, h:h + 1]
            y_diag = mm(cb_t * decay_ls, xdt)
            a_last = acs[n - 1:n, h:h + 1]
            st = mm_tn(xdt * jnp.exp(a_last - a_col), bg)
            news.append(jnp.exp(a_last) * prev[h] + st)
            y_off = mm_nt(cg, prev[h]) * jnp.exp(a_col)
            ys.append(y_diag + y_off + dskip[:, h:h + 1] * xh)
    y = jnp.concatenate(ys, axis=1)
    yg = y * jax.nn.silu(z)
    half = D_SSD // 2
    outs = []
    for g in range(2):
        t = yg[:, g * half:(g + 1) * half]
        outs.append(t * lax.rsqrt(jnp.mean(t * t, axis=-1, keepdims=True) + EPS))
    return jnp.concatenate(outs, axis=1) * snw, jnp.stack(news)


def _f_out(o, yg, g1, wo):
    cat = jnp.concatenate([o[h] for h in range(N_HEADS)] + [yg], axis=1)
    return g1 * mm(cat, wo)


def _f_mlp(x, nw, sh, sc, g2, wg, wu, wd):
    h = _rms(x, nw) * (1.0 + sc) + sh
    act = jax.nn.silu(mm(h, wg)) * mm(h, wu)
    return g2 * mm(act, wd)


def proj_fwd(x, nw, sh, sc, w):
    s = x.shape[0]
    ts = _token_block(s)

    def body(x_ref, nw_ref, sh_ref, sc_ref, w_ref, pa_ref, pz_ref, px_ref, pl_ref):
        p = _f_proj(x_ref[...], nw_ref[...], sh_ref[...], sc_ref[...], w_ref[...])
        pa_ref[...] = p[:, :384]
        pz_ref[...] = p[:, 384:896]
        px_ref[...] = p[:, 896:1920]
        pl_ref[...] = p[:, 1920:]

    vec = _const((1, D_MODEL))
    return pl.pallas_call(
        body, name="proj_fwd", grid=(s // ts,),
        in_specs=[pl.BlockSpec((ts, D_MODEL), lambda i: (i, 0)), vec, vec, vec, _const((D_MODEL, D_PROJ))],
        out_specs=[pl.BlockSpec((ts, 384), lambda i: (i, 0)), pl.BlockSpec((ts, 512), lambda i: (i, 0)),
                   pl.BlockSpec((ts, 1024), lambda i: (i, 0)), pl.BlockSpec((ts, 128), lambda i: (i, 0))],
        out_shape=[jax.ShapeDtypeStruct((s, 384), f32), jax.ShapeDtypeStruct((s, 512), f32),
                   jax.ShapeDtypeStruct((s, 1024), f32), jax.ShapeDtypeStruct((s, 128), f32)],
    )(x, nw, sh, sc, w)


def rope_tables(pos, inv):
    s = pos.shape[0]
    ts = _token_block(s)

    def body(pos_ref, inv_ref, cos_ref, sin_ref):
        ang = pos_ref[...].astype(f32) * inv_ref[...]
        lane = lax.broadcasted_iota(jnp.int32, (1, HEAD_LANES), 1)
        half = ROPE // 2
        cos_ref[...] = jnp.where(lane < NOPE, 1.0, jnp.where(lane < NOPE + ROPE, jnp.cos(ang), 0.0))
        sn = jnp.sin(ang)
        sin_ref[...] = jnp.where((lane >= NOPE) & (lane < NOPE + half), -sn,
                                 jnp.where((lane >= NOPE + half) & (lane < NOPE + ROPE), sn, 0.0))

    return pl.pallas_call(
        body, name="rope_tables", grid=(s // ts,),
        in_specs=[pl.BlockSpec((ts, 1), lambda i: (i, 0)), _const((1, HEAD_LANES))],
        out_specs=[pl.BlockSpec((ts, HEAD_LANES), lambda i: (i, 0))] * 2,
        out_shape=[jax.ShapeDtypeStruct((s, HEAD_LANES), f32)] * 2,
    )(pos, inv)


def _qkv_param_specs():
    return [_const((1, Q_RANK)), _const((1, KV_RANK)), _const((N_HEADS, Q_RANK, HEAD_LANES)),
            _const((N_HEADS, KV_RANK, HEAD_LANES)), _const((N_HEADS, KV_RANK, V_DIM)),
            _const((1, HEAD_LANES)), _const((1, HEAD_LANES)), _const((1, HEAD_LANES))]


def qkv_fwd(pa, plast, cos_t, sin_t, params):
    s = pa.shape[0]
    ts = _token_block(s)

    def body(pa_ref, pl_ref, cos_ref, sin_ref, *rest):
        prm = [r[...] for r in rest[:8]]
        q_ref, k_ref, v_ref = rest[8:]
        q, k, v = _f_qkv(pa_ref[...], pl_ref[...], cos_ref[...], sin_ref[...], *prm)
        q_ref[...] = q.astype(bf16)
        k_ref[...] = k.astype(bf16)
        v_ref[...] = v.astype(bf16)

    tok = lambda w: pl.BlockSpec((ts, w), lambda i: (i, 0))
    return pl.pallas_call(
        body, name="qkv_fwd", grid=(s // ts,),
        in_specs=[tok(384), tok(128), tok(128), tok(128)] + _qkv_param_specs(),
        out_specs=[pl.BlockSpec((N_HEADS, ts, HEAD_LANES), lambda i: (0, i, 0)),
                   pl.BlockSpec((N_HEADS, ts, HEAD_LANES), lambda i: (0, i, 0)),
                   pl.BlockSpec((N_HEADS, ts, V_DIM), lambda i: (0, i, 0))],
        out_shape=[jax.ShapeDtypeStruct((N_HEADS, s, HEAD_LANES), bf16), jax.ShapeDtypeStruct((N_HEADS, s, HEAD_LANES), bf16),
                   jax.ShapeDtypeStruct((N_HEADS, s, V_DIM), bf16)],
    )(pa, plast, cos_t, sin_t, *params)


def _scores(q, k):
    return lax.dot_general(q, k, (((1,), (1,)), ((), ())), preferred_element_type=f32)


def _tril(rows, cols, row_offset):
    row = row_offset + lax.broadcasted_iota(jnp.int32, (rows, cols), 0)
    col = lax.broadcasted_iota(jnp.int32, (rows, cols), 1)
    return row >= col


def _call_with_job(body, name, grid, job, in_specs, out_specs, out_shape, scratch_shapes, operands):
    if job is None:
        res = pl.pallas_call(body, name=name, grid=grid, in_specs=in_specs, out_specs=out_specs, out_shape=out_shape,
                             scratch_shapes=scratch_shapes)(*operands)
        return res, None

    def at_step(i, n):
        if i == 0:
            want = [0] * len(grid)
        elif i == n - 1:
            want = [g - 1 for g in grid]
        else:
            want = [grid[0] // 2] + [0] * (len(grid) - 1)
        return functools.reduce(jnp.logical_and, [pl.program_id(a) == s for a, s in enumerate(want)])

    carrier = _carry(job, body, len(in_specs), len(out_specs), at_step)
    res = pl.pallas_call(
        carrier, name=name, grid=grid,
        in_specs=list(in_specs) + [ANY] * len(job.operands), out_specs=list(out_specs) + [ANY] * len(job.out_shape),
        out_shape=list(out_shape) + list(job.out_shape), scratch_shapes=list(scratch_shapes) + job.scratch,
    )(*operands, *job.operands)
    return res[:len(out_specs)], res[len(out_specs):]


def attn_fwd(q, k, v, job=None):
    s = q.shape[1]
    t = _token_block(s)
    nb = s // t

    rb = min(ATTN_ROWS_FWD, t)

    def body(q_ref, k_ref, v_ref, o_ref, lse_ref, m_sc, l_sc, acc_sc):
        qi = pl.program_id(1)
        m_sc[...] = jnp.full(m_sc.shape, NEG, f32)
        l_sc[...] = jnp.zeros(l_sc.shape, f32)
        acc_sc[...] = jnp.zeros(acc_sc.shape, f32)

        def step(k0, diagonal):
            for r in range(t // rb):
                rows = pl.ds(r * rb, rb)
                nk = (r + 1) * rb if diagonal else t
                keys = pl.ds(k0, nk)
                sc = _scores(q_ref[rows, :], k_ref[keys, :])
                if diagonal:
                    sc = jnp.where(_tril(rb, nk, r * rb), sc, NEG)
                m_prev = m_sc[rows, :1]
                m_new = jnp.maximum(m_prev, jnp.max(sc, axis=-1, keepdims=True))
                p = jnp.exp2(sc - m_new)
                alpha = jnp.exp2(m_prev - m_new)
                l_new = alpha * l_sc[rows, :1] + jnp.sum(p, axis=-1, keepdims=True)
                acc = alpha * acc_sc[rows, :] + jnp.dot(p.astype(bf16), v_ref[keys, :], preferred_element_type=f32)
                if diagonal:
                    o_ref[rows, :] = acc / l_new
                    lse_ref[rows, :] = jnp.broadcast_to(m_new + jnp.log2(l_new), (rb, 128))
                else:
                    acc_sc[rows, :] = acc
                    m_sc[rows, :] = jnp.broadcast_to(m_new, (rb, 128))
                    l_sc[rows, :] = jnp.broadcast_to(l_new, (rb, 128))

        def below(ki, carry):
            step(pl.multiple_of(ki * t, t), False)
            return carry

        lax.fori_loop(0, qi, below, 0)
        step(pl.multiple_of(qi * t, t), True)

    return _call_with_job(
        body, "attn_fwd" if job is None else "attn_fwd_comm", (N_HEADS, nb), job,
        in_specs=[pl.BlockSpec((None, t, HEAD_LANES), lambda h, qi: (h, qi, 0)),
                  pl.BlockSpec((None, s, HEAD_LANES), lambda h, qi: (h, 0, 0)),
                  pl.BlockSpec((None, s, V_DIM), lambda h, qi: (h, 0, 0))],
        out_specs=[pl.BlockSpec((None, t, V_DIM), lambda h, qi: (h, qi, 0)),
                   pl.BlockSpec((None, t, 128), lambda h, qi: (h, qi, 0))],
        out_shape=[jax.ShapeDtypeStruct((N_HEADS, s, V_DIM), f32), jax.ShapeDtypeStruct((N_HEADS, s, 128), f32)],
        scratch_shapes=[pltpu.VMEM((t, 128), f32), pltpu.VMEM((t, 128), f32), pltpu.VMEM((t, V_DIM), f32)],
        operands=(q, k, v))


def _ssd_param_specs():
    return [_const((4, D_CONV)), _const((1, D_CONV)), _const((1, 128)), _const((1, 128)), _const((1, 128)),
            _const((1, D_SSD))]


def ssd_fwd(px, pz, plast, params):
    s = px.shape[0]
    nc = s // CHUNK

    def body(px_ref, pz_ref, pl_ref, cw_ref, cb_ref, dtb_ref, alog_ref, dskip_ref, snw_ref, yg_ref, st_ref,
             state_sc, halo_sc):
        i = pl.program_id(0)

        @pl.when(i == 0)
        def _():
            state_sc[...] = jnp.zeros(state_sc.shape, f32)
            halo_sc[...] = jnp.zeros(halo_sc.shape, f32)

        x = px_ref[...]
        prev = state_sc[...]
        st_ref[...] = prev
        xext = jnp.concatenate([halo_sc[...], x], axis=0)
        yg, new = _f_ssd(xext, pz_ref[...], pl_ref[...], prev, cw_ref[...], cb_ref[...], dtb_ref[...],
                         alog_ref[...], dskip_ref[...], snw_ref[...])
        yg_ref[...] = yg
        state_sc[...] = new
        halo_sc[...] = x[CHUNK - HALO:]

    tok = lambda w: pl.BlockSpec((CHUNK, w), lambda i: (i, 0))
    return pl.pallas_call(
        body, name="ssd_fwd", grid=(nc,),
        in_specs=[tok(D_CONV), tok(D_SSD), tok(128)] + _ssd_param_specs(),
        out_specs=[tok(D_SSD), pl.BlockSpec((None, N_HEADS, SSD_HEAD_DIM, SSD_STATE), lambda i: (i, 0, 0, 0))],
        out_shape=[jax.ShapeDtypeStruct((s, D_SSD), f32),
                   jax.ShapeDtypeStruct((nc, N_HEADS, SSD_HEAD_DIM, SSD_STATE), f32)],
        scratch_shapes=[pltpu.VMEM((N_HEADS, SSD_HEAD_DIM, SSD_STATE), f32), pltpu.VMEM((HALO, D_CONV), f32)],
    )(px, pz, plast, *params)


def out_fwd(x, o, yg, g1, wo):
    s = x.shape[0]
    ts = _token_block(s)

    def body(x_ref, o_ref, yg_ref, g1_ref, wo_ref, out_ref):
        out_ref[...] = x_ref[...] + _f_out(o_ref[...], yg_ref[...], g1_ref[...], wo_ref[...])

    return pl.pallas_call(
        body, name="out_fwd", grid=(s // ts,),
        in_specs=[pl.BlockSpec((ts, D_MODEL), lambda i: (i, 0)), pl.BlockSpec((N_HEADS, ts, V_DIM), lambda i: (0, i, 0)),
                  pl.BlockSpec((ts, D_SSD), lambda i: (i, 0)), _const((1, D_MODEL)), _const((D_MODEL, D_MODEL))],
        out_specs=pl.BlockSpec((ts, D_MODEL), lambda i: (i, 0)),
        out_shape=jax.ShapeDtypeStruct((s, D_MODEL), f32),
    )(x, o, yg, g1, wo)


def mlp_fwd(x, nw, sh, sc, g2, wgu, wd):
    s = x.shape[0]
    ts = _token_block(s)
    nj = N_DEV // 2

    def body(x_ref, nw_ref, sh_ref, sc_ref, g2_ref, wg_ref, wu_ref, wd_ref, out_ref, acc_sc):
        j = pl.program_id(1)
        part = _f_mlp(x_ref[...], nw_ref[...], sh_ref[...], sc_ref[...], g2_ref[...], wg_ref[...], wu_ref[...], wd_ref[...])

        @pl.when(j == 0)
        def _():
            acc_sc[...] = x_ref[...] + part

        @pl.when(j > 0)
        def _():
            acc_sc[...] += part

        @pl.when(j == nj - 1)
        def _():
            out_ref[...] = acc_sc[...]

    vec = _const((1, D_MODEL))
    return pl.pallas_call(
        body, name="mlp_fwd", grid=(s // ts, nj),
        in_specs=[pl.BlockSpec((ts, D_MODEL), lambda i, j: (i, 0)), vec, vec, vec, vec,
                  pl.BlockSpec((None, D_MODEL, FF_SHARD), lambda i, j: (j, 0, 0)),
                  pl.BlockSpec((None, D_MODEL, FF_SHARD), lambda i, j: (j + nj, 0, 0)),
                  pl.BlockSpec((None, FF_SHARD, D_MODEL), lambda i, j: (j, 0, 0))],
        out_specs=pl.BlockSpec((ts, D_MODEL), lambda i, j: (i, 0)),
        out_shape=jax.ShapeDtypeStruct((s, D_MODEL), f32),
        scratch_shapes=[pltpu.VMEM((ts, D_MODEL), f32)],
    )(x, nw, sh, sc, g2, wgu, wgu, wd)


def loss_fwd(y, target):
    s = y.shape[0]
    ts = _token_block(s)

    def body(y_ref, t_ref, dy_ref, loss_ref):
        d = y_ref[...] - t_ref[...]
        dy_ref[...] = d * (1.0 / D_MODEL)
        part = 0.5 * jnp.sum(jnp.sum(d * d, axis=-1, keepdims=True) * (1.0 / D_MODEL), axis=0, keepdims=True)
        _accumulate(pl.program_id(0) == 0, [loss_ref], [jnp.broadcast_to(part, (8, 128))])

    return pl.pallas_call(
        body, name="loss_fwd", grid=(s // ts,),
        in_specs=[pl.BlockSpec((ts, D_MODEL), lambda i: (i, 0))] * 2,
        out_specs=[pl.BlockSpec((ts, D_MODEL), lambda i: (i, 0)), _const((8, 128))],
        out_shape=[jax.ShapeDtypeStruct((s, D_MODEL), f32), jax.ShapeDtypeStruct((8, 128), f32)],
    )(y, target)


def mlp_bwd(x, dy, nw, sh, sc, g2, wgu, wd, job=None):
    s = x.shape[0]
    ts = min(256, s)
    nj = N_DEV // 2
    ni = s // ts

    def body(x_ref, dy_ref, nw_ref, sh_ref, sc_ref, g2_ref, wg_ref, wu_ref, wd_ref,
             dx_ref, dnw_ref, dsh_ref, dsc_ref, dg2_ref, dwg_ref, dwu_ref, dwd_ref, ag_sc, au_sc, ad_sc):
        j, i = pl.program_id(0), pl.program_id(1)
        _, vjp = jax.vjp(_f_mlp, x_ref[...], nw_ref[...], sh_ref[...], sc_ref[...], g2_ref[...],
                         wg_ref[...].astype(f32), wu_ref[...].astype(f32), wd_ref[...].astype(f32))
        dx, dnw, dsh, dsc, dg2, dwg, dwu, dwd = vjp(dy_ref[...])
        dx_ref[...] = dx
        _accumulate((i == 0) & (j == 0), [dnw_ref, dsh_ref, dsc_ref, dg2_ref], [dnw, dsh, dsc, dg2])
        _accumulate_then_cast(i == 0, i == ni - 1, [ag_sc, au_sc, ad_sc], [dwg_ref, dwu_ref, dwd_ref], [dwg, dwu, dwd])

    vec = _const((1, D_MODEL))
    vshape = jax.ShapeDtypeStruct((1, D_MODEL), f32)
    wspec = lambda off: pl.BlockSpec((None, D_MODEL, FF_SHARD), lambda j, i: (j + off, 0, 0))
    dspec = pl.BlockSpec((None, FF_SHARD, D_MODEL), lambda j, i: (j, 0, 0))
    return _call_with_job(
        body, "mlp_bwd" if job is None else "mlp_bwd_comm", (nj, ni), job,
        in_specs=[pl.BlockSpec((ts, D_MODEL), lambda j, i: (i, 0)), pl.BlockSpec((ts, D_MODEL), lambda j, i: (i, 0)),
                  vec, vec, vec, vec, wspec(0), wspec(nj), dspec],
        out_specs=[pl.BlockSpec((None, ts, D_MODEL), lambda j, i: (j, i, 0)), vec, vec, vec, vec,
                   wspec(0), wspec(0), dspec],
        out_shape=[jax.ShapeDtypeStruct((nj, s, D_MODEL), f32), vshape, vshape, vshape, vshape,
                   jax.ShapeDtypeStruct((nj, D_MODEL, FF_SHARD), bf16), jax.ShapeDtypeStruct((nj, D_MODEL, FF_SHARD), bf16),
                   jax.ShapeDtypeStruct((nj, FF_SHARD, D_MODEL), bf16)],
        scratch_shapes=[pltpu.VMEM((D_MODEL, FF_SHARD), f32), pltpu.VMEM((D_MODEL, FF_SHARD), f32),
                        pltpu.VMEM((FF_SHARD, D_MODEL), f32)],
        operands=(x, dy, nw, sh, sc, g2, wgu, wgu, wd))


def out_bwd(dy, dparts, o, yg, g1, wo):
    s = dy.shape[0]
    ts = _token_block(s)
    nj = dparts.shape[0]

    ni = s // ts

    def body(dy_ref, dp_ref, o_ref, yg_ref, g1_ref, wo_ref, dx_ref, do_ref, delta_ref, dyg_ref, dg1_ref, dwo_ref, acc_sc):
        i = pl.program_id(0)
        g = dy_ref[...]
        for j in range(nj):
            g = g + dp_ref[j]
        dx_ref[...] = g
        o = o_ref[...]
        _, vjp = jax.vjp(_f_out, o, yg_ref[...], g1_ref[...], wo_ref[...].astype(f32))
        do, dyg, dg1, dwo = vjp(g)
        do_ref[...] = do
        dyg_ref[...] = dyg
        delta_ref[...] = jnp.broadcast_to(jnp.sum(do * o, axis=-1, keepdims=True), delta_ref.shape)
        _accumulate(i == 0, [dg1_ref], [dg1])
        _accumulate_then_cast(i == 0, i == ni - 1, [acc_sc], [dwo_ref], [dwo])

    head = pl.BlockSpec((N_HEADS, ts, V_DIM), lambda i: (0, i, 0))
    return pl.pallas_call(
        body, name="out_bwd", grid=(ni,), scratch_shapes=[pltpu.VMEM((D_MODEL, D_MODEL), f32)],
        in_specs=[pl.BlockSpec((ts, D_MODEL), lambda i: (i, 0)), pl.BlockSpec((nj, ts, D_MODEL), lambda i: (0, i, 0)),
                  head, pl.BlockSpec((ts, D_SSD), lambda i: (i, 0)), _const((1, D_MODEL)), _const((D_MODEL, D_MODEL))],
        out_specs=[pl.BlockSpec((ts, D_MODEL), lambda i: (i, 0)), head,
                   pl.BlockSpec((N_HEADS, ts, 128), lambda i: (0, i, 0)), pl.BlockSpec((ts, D_SSD), lambda i: (i, 0)),
                   _const((1, D_MODEL)), _const((D_MODEL, D_MODEL))],
        out_shape=[jax.ShapeDtypeStruct((s, D_MODEL), f32), jax.ShapeDtypeStruct((N_HEADS, s, V_DIM), f32),
                   jax.ShapeDtypeStruct((N_HEADS, s, 128), f32), jax.ShapeDtypeStruct((s, D_SSD), f32),
                   jax.ShapeDtypeStruct((1, D_MODEL), f32), jax.ShapeDtypeStruct((D_MODEL, D_MODEL), bf16)],
    )(dy, dparts, o, yg, g1, wo)


def attn_bwd_dq(q, k, v, do, lse, delta):
    s = q.shape[1]
    t = _token_block(s)
    nb = s // t

    rb = min(ATTN_ROWS_BWD, t)

    def body(q_ref, k_ref, v_ref, do_ref, lse_ref, delta_ref, dq_ref):
        qi = pl.program_id(1)
        dq_ref[...] = jnp.zeros(dq_ref.shape, f32)

        def step(k0, diagonal):
            for r in range(t // rb):
                rows = pl.ds(r * rb, rb)
                nk = (r + 1) * rb if diagonal else t
                k = k_ref[pl.ds(k0, nk), :]
                sc = _scores(q_ref[rows, :], k)
                if diagonal:
                    sc = jnp.where(_tril(rb, nk, r * rb), sc, NEG)
                p = jnp.exp2(sc - lse_ref[rows, :1])
                dp = lax.dot_general(do_ref[rows, :].astype(bf16), v_ref[pl.ds(k0, nk), :], (((1,), (1,)), ((), ())),
                                     preferred_element_type=f32)
                ds = p * (dp - delta_ref[rows, :1])
                dq = dq_ref[rows, :] + jnp.dot(ds.astype(bf16), k, preferred_element_type=f32)
                dq_ref[rows, :] = dq * LN2 if diagonal else dq

        def below(ki, carry):
            step(pl.multiple_of(ki * t, t), False)
            return carry

        lax.fori_loop(0, qi, below, 0)
        step(pl.multiple_of(qi * t, t), True)

    qspec = lambda w: pl.BlockSpec((None, t, w), lambda h, qi: (h, qi, 0))
    kspec = lambda w: pl.BlockSpec((None, s, w), lambda h, qi: (h, 0, 0))
    return pl.pallas_call(
        body, name="attn_bwd_dq", grid=(N_HEADS, nb),
        in_specs=[qspec(HEAD_LANES), kspec(HEAD_LANES), kspec(V_DIM), qspec(V_DIM), qspec(128), qspec(128)],
        out_specs=qspec(HEAD_LANES),
        out_shape=jax.ShapeDtypeStruct((N_HEADS, s, HEAD_LANES), f32),
    )(q, k, v, do, lse, delta)


def attn_bwd_dkv(q, k, v, do, lse, delta, job=None):
    s = q.shape[1]
    t = _token_block(s)
    nb = s // t

    cb = min(ATTN_ROWS_BWD, t)

    def body(q_ref, k_ref, v_ref, do_ref, lse_ref, delta_ref, dk_ref, dv_ref):
        ki = pl.program_id(1)
        dk_ref[...] = jnp.zeros(dk_ref.shape, f32)
        dv_ref[...] = jnp.zeros(dv_ref.shape, f32)

        def step(q0, diagonal):
            for c in range(t // cb):
                keys = pl.ds(c * cb, cb)
                r0 = c * cb if diagonal else 0
                rows = pl.ds(q0 + r0, t - r0)
                q = q_ref[rows, :]
                do = do_ref[rows, :].astype(bf16)
                sc = _scores(q, k_ref[keys, :])
                if diagonal:
                    sc = jnp.where(_tril(t - r0, cb, 0), sc, NEG)
                p = jnp.exp2(sc - lse_ref[rows, :1])
                dp = lax.dot_general(do, v_ref[keys, :], (((1,), (1,)), ((), ())), preferred_element_type=f32)
                ds = p * (dp - delta_ref[rows, :1])
                dv_ref[keys, :] += lax.dot_general(p.astype(bf16), do, (((0,), (0,)), ((), ())), preferred_element_type=f32)
                dk_ref[keys, :] += lax.dot_general(ds.astype(bf16), q, (((0,), (0,)), ((), ())), preferred_element_type=f32)

        step(pl.multiple_of(ki * t, t), True)

        def above(qi, carry):
            step(pl.multiple_of(qi * t, t), False)
            return carry

        lax.fori_loop(ki + 1, nb, above, 0)
        dk_ref[...] = dk_ref[...] * LN2

    qspec = lambda w: pl.BlockSpec((None, s, w), lambda h, ki: (h, 0, 0))
    kspec = lambda w: pl.BlockSpec((None, t, w), lambda h, ki: (h, ki, 0))
    return _call_with_job(
        body, "attn_bwd_dkv" if job is None else "attn_bwd_dkv_comm", (N_HEADS, nb), job,
        in_specs=[qspec(HEAD_LANES), kspec(HEAD_LANES), kspec(V_DIM), qspec(V_DIM), qspec(128), qspec(128)],
        out_specs=[kspec(HEAD_LANES), kspec(V_DIM)],
        out_shape=[jax.ShapeDtypeStruct((N_HEADS, s, HEAD_LANES), f32), jax.ShapeDtypeStruct((N_HEADS, s, V_DIM), f32)],
        scratch_shapes=[], operands=(q, k, v, do, lse, delta))


def ssd_bwd(px, pz, plast, states, dyg, params):
    s = px.shape[0]
    nc = s // CHUNK
    per = CHUNK // HALO

    def body(px_ref, halo_ref, pz_ref, pl_ref, st_ref, dyg_ref, cw_ref, cb_ref, dtb_ref, alog_ref, dskip_ref, snw_ref,
             dpx_ref, dpz_ref, dpl_ref, dcw_ref, dcb_ref, ddtb_ref, dalog_ref, ddskip_ref, dsnw_ref, dstate_sc, dhalo_sc):
        t = pl.program_id(0)
        chunk = nc - 1 - t

        @pl.when(t == 0)
        def _():
            dstate_sc[...] = jnp.zeros(dstate_sc.shape, f32)
            dhalo_sc[...] = jnp.zeros(dhalo_sc.shape, f32)

        halo = jnp.where(chunk > 0, halo_ref[...], 0.0)
        xext = jnp.concatenate([halo, px_ref[...]], axis=0)
        _, vjp = jax.vjp(_f_ssd, xext, pz_ref[...], pl_ref[...], st_ref[...], cw_ref[...], cb_ref[...], dtb_ref[...],
                         alog_ref[...], dskip_ref[...], snw_ref[...])
        dxext, dz, dpl, dprev, dcw, dcb, ddtb, dalog, ddskip, dsnw = vjp((dyg_ref[...], dstate_sc[...]))
        dpx_ref[...] = dxext[HALO:]
        dpx_ref[CHUNK - HALO:, :] += dhalo_sc[...]
        dhalo_sc[...] = dxext[:HALO]
        dstate_sc[...] = dprev
        dpz_ref[...] = dz
        dpl_ref[...] = dpl
        _accumulate(t == 0, [dcw_ref, dcb_ref, ddtb_ref, dalog_ref, ddskip_ref, dsnw_ref],
                    [dcw, dcb, ddtb, dalog, ddskip, dsnw])

    rev = lambda w: pl.BlockSpec((CHUNK, w), lambda t: (nc - 1 - t, 0))
    pshapes = [jax.ShapeDtypeStruct((4, D_CONV), f32), jax.ShapeDtypeStruct((1, D_CONV), f32),
               jax.ShapeDtypeStruct((1, 128), f32), jax.ShapeDtypeStruct((1, 128), f32),
               jax.ShapeDtypeStruct((1, 128), f32), jax.ShapeDtypeStruct((1, D_SSD), f32)]
    return pl.pallas_call(
        body, name="ssd_bwd", grid=(nc,),
        in_specs=[rev(D_CONV),
                  pl.BlockSpec((HALO, D_CONV), lambda t: (jnp.maximum((nc - 1 - t) * per - 1, 0), 0)),
                  rev(D_SSD), rev(128),
                  pl.BlockSpec((None, N_HEADS, SSD_HEAD_DIM, SSD_STATE), lambda t: (nc - 1 - t, 0, 0, 0)),
                  rev(D_SSD)] + _ssd_param_specs(),
        out_specs=[rev(D_CONV), rev(D_SSD), rev(128)] + _ssd_param_specs(),
        out_shape=[jax.ShapeDtypeStruct((s, D_CONV), f32), jax.ShapeDtypeStruct((s, D_SSD), f32),
                   jax.ShapeDtypeStruct((s, 128), f32)] + pshapes,
        scratch_shapes=[pltpu.VMEM((N_HEADS, SSD_HEAD_DIM, SSD_STATE), f32), pltpu.VMEM((HALO, D_CONV), f32)],
    )(px, px, pz, plast, states, dyg, *params)


def qkv_bwd(pa, plast, cos_t, sin_t, params, dq, dk, dv):
    s = pa.shape[0]
    ts = _token_block(s)

    def body(pa_ref, pl_ref, cos_ref, sin_ref, *rest):
        prm = [r[...].astype(f32) for r in rest[:8]]
        dq_ref, dk_ref, dv_ref = rest[8:11]
        dpa_ref, dpl_ref = rest[11:13]
        dprm_refs = list(rest[13:])
        cos_t, sin_t = cos_ref[...], sin_ref[...]
        _, vjp = jax.vjp(lambda a, b, *p: _f_qkv(a, b, cos_t, sin_t, *p), pa_ref[...], pl_ref[...], *prm)
        grads = vjp((dq_ref[...], dk_ref[...], dv_ref[...]))
        dpa_ref[...] = grads[0]
        dpl_ref[...] = grads[1]
        _accumulate(pl.program_id(0) == 0, dprm_refs, list(grads[2:]))

    tok = lambda w: pl.BlockSpec((ts, w), lambda i: (i, 0))
    head = lambda w: pl.BlockSpec((N_HEADS, ts, w), lambda i: (0, i, 0))
    pshapes = [jax.ShapeDtypeStruct((1, Q_RANK), f32), jax.ShapeDtypeStruct((1, KV_RANK), f32),
               jax.ShapeDtypeStruct((N_HEADS, Q_RANK, HEAD_LANES), f32), jax.ShapeDtypeStruct((N_HEADS, KV_RANK, HEAD_LANES), f32),
               jax.ShapeDtypeStruct((N_HEADS, KV_RANK, V_DIM), f32), jax.ShapeDtypeStruct((1, HEAD_LANES), f32),
               jax.ShapeDtypeStruct((1, HEAD_LANES), f32), jax.ShapeDtypeStruct((1, HEAD_LANES), f32)]
    return pl.pallas_call(
        body, name="qkv_bwd", grid=(s // ts,),
        in_specs=[tok(384), tok(128), tok(128), tok(128)] + _qkv_param_specs()
                 + [head(HEAD_LANES), head(HEAD_LANES), head(V_DIM)],
        out_specs=[tok(384), tok(128)] + _qkv_param_specs(),
        out_shape=[jax.ShapeDtypeStruct((s, 384), f32), jax.ShapeDtypeStruct((s, 128), f32)] + pshapes,
    )(pa, plast, cos_t, sin_t, *params, dq, dk, dv)


def proj_bwd(x, nw, sh, sc, w, dpa, dpz, dpx, dpl_k, dpl_dt, dres):
    s = x.shape[0]
    ts = _token_block(s)

    ni = s // ts

    def body(x_ref, nw_ref, sh_ref, sc_ref, w_ref, dpa_ref, dpz_ref, dpx_ref, dplk_ref, dpld_ref, dres_ref,
             dx_ref, dnw_ref, dsh_ref, dsc_ref, dw_ref, acc_sc):
        i = pl.program_id(0)
        g = jnp.concatenate([dpa_ref[...], dpz_ref[...], dpx_ref[...], dplk_ref[...] + dpld_ref[...]], axis=1)
        _, vjp = jax.vjp(_f_proj, x_ref[...], nw_ref[...], sh_ref[...], sc_ref[...], w_ref[...].astype(f32))
        dx, dnw, dsh, dsc, dw = vjp(g)
        dx_ref[...] = dx + dres_ref[...]
        _accumulate(i == 0, [dnw_ref, dsh_ref, dsc_ref], [dnw, dsh, dsc])
        _accumulate_then_cast(i == 0, i == ni - 1, [acc_sc], [dw_ref], [dw])

    vec = _const((1, D_MODEL))
    vshape = jax.ShapeDtypeStruct((1, D_MODEL), f32)
    tok = lambda w_: pl.BlockSpec((ts, w_), lambda i: (i, 0))
    return pl.pallas_call(
        body, name="proj_bwd", grid=(ni,), scratch_shapes=[pltpu.VMEM((D_MODEL, D_PROJ), f32)],
        in_specs=[tok(D_MODEL), vec, vec, vec, _const((D_MODEL, D_PROJ)), tok(384), tok(512), tok(1024), tok(128), tok(128),
                  tok(D_MODEL)],
        out_specs=[tok(D_MODEL), vec, vec, vec, _const((D_MODEL, D_PROJ))],
        out_shape=[jax.ShapeDtypeStruct((s, D_MODEL), f32), vshape, vshape, vshape,
                   jax.ShapeDtypeStruct((D_MODEL, D_PROJ), bf16)],
    )(x, nw, sh, sc, w, dpa, dpz, dpx, dpl_k, dpl_dt, dres)


def ada_fwd(c_all, w_ada, b_cols):
    def body(c_ref, w_ref, b_ref, out_ref):
        act = jax.nn.silu(c_ref[...])
        for l in range(2):
            out_ref[l] = jnp.dot(act, w_ref[l], precision=lax.Precision.HIGHEST, preferred_element_type=f32) + b_ref[l]

    return pl.pallas_call(body, name="ada_fwd", out_shape=jax.ShapeDtypeStruct((2, N_DEV, 768), f32))(c_all, w_ada, b_cols)


def ada_bwd(c_all, dmod_cols):
    def body(c_ref, d_ref, out_ref):
        out_ref[0] = lax.dot_general(jax.nn.silu(c_ref[...]), d_ref[0], (((0,), (0,)), ((), ())),
                                     precision=lax.Precision.HIGHEST, preferred_element_type=f32)

    return pl.pallas_call(
        body, name="ada_bwd", grid=(2,),
        in_specs=[_const((N_DEV, D_MODEL)), pl.BlockSpec((1, N_DEV, 768), lambda l: (l, 0, 0))],
        out_specs=pl.BlockSpec((1, D_MODEL, 768), lambda l: (l, 0, 0)),
        out_shape=jax.ShapeDtypeStruct((2, D_MODEL, 768), f32),
    )(c_all, dmod_cols)


def _adamw(w, g, m, v):
    m = ADAM_B1 * m + (1.0 - ADAM_B1) * g
    v = ADAM_B2 * v + (1.0 - ADAM_B2) * (g * g)
    m_hat = m / (1.0 - ADAM_B1 ** ADAM_STEP)
    v_hat = v / (1.0 - ADAM_B2 ** ADAM_STEP)
    delta = -ADAM_LR * (m_hat / (jnp.sqrt(v_hat) + ADAM_EPS) + ADAM_WD * w)
    return delta, m, v


def adamw(parts, w, m, v, layer, prev, name):
    n, r, c = parts.shape
    nl = w.shape[0]
    tr = r
    lanes = -(-c // 128) * 128
    if 2 * (n + 7) * r * lanes * 4 > ADAMW_BLOCK_BYTES:
        tr = next(t for t in (256, 128, 64, 32, 16, 8) if r % t == 0)

    def body(p_ref, w_ref, m_ref, v_ref, *rest):
        g_ref, d_ref, nm_ref, nv_ref = rest[-4:]
        g = p_ref[0].astype(f32)
        for k in range(1, n):
            g = g + p_ref[k].astype(f32)
        delta, nm, nv = _adamw(w_ref[...], g, m_ref[...], v_ref[...])
        g_ref[...] = g
        d_ref[...] = delta
        nm_ref[...] = nm
        nv_ref[...] = nv

    blk = pl.BlockSpec((None, tr, c), lambda i: (layer, i, 0))
    shp = jax.ShapeDtypeStruct((nl, r, c), f32)
    kept = [] if prev is None else list(prev)
    return pl.pallas_call(
        body, name=name, grid=(r // tr,),
        in_specs=[pl.BlockSpec((n, tr, c), lambda i: (0, i, 0)), blk, blk, blk] + [ANY] * len(kept),
        out_specs=[blk] * 4, out_shape=[shp] * 4,
        input_output_aliases={4 + j: j for j in range(len(kept))},
    )(parts, w, m, v, *kept)


def _my_index():
    return 4 * lax.axis_index("x") + 2 * lax.axis_index("y") + lax.axis_index("c")


def _coords(idx):
    return (idx // 4, (idx // 2) % 2, idx % 2)


class CommJob:
    def __init__(self, operands, out_shape, phases, scratch):
        self.operands, self.out_shape, self.phases, self.scratch = operands, out_shape, phases, scratch


def _wait(out, n_blocks, send_sem, recv_sem, send=True, recv=True):
    span = out.at[pl.ds(0, n_blocks)]
    desc = pltpu.make_async_remote_copy(src_ref=span, dst_ref=span, send_sem=send_sem, recv_sem=recv_sem,
                                        device_id=_coords(_my_index()), device_id_type=MESH)
    if recv:
        desc.wait_recv()
    if send:
        desc.wait_send()


def gather_job(shards):
    n = len(shards)

    def places():
        x, y, c = lax.axis_index("x"), lax.axis_index("y"), lax.axis_index("c")
        return (x, y, c), (x, y, 1 - c), [(1 - x, y), (x, 1 - y), (1 - x, 1 - y)]

    def index(p):
        return 4 * p[0] + 2 * p[1] + p[2]

    def start(ins, outs, sems):
        far_send, far_recv, near_send, near_recv, local = sems
        me, sibling, chips = places()
        for k in range(n):
            pltpu.make_async_copy(ins[k], outs[k].at[index(me)], local.at[k]).start()
            for chip in chips:
                pltpu.make_async_remote_copy(src_ref=ins[k], dst_ref=outs[k].at[index(me)], send_sem=far_send.at[k],
                                             recv_sem=far_recv.at[k], device_id=(*chip, me[2]), device_id_type=MESH).start()
            pltpu.make_async_remote_copy(src_ref=ins[k], dst_ref=outs[k].at[index(me)], send_sem=near_send.at[k],
                                         recv_sem=near_recv.at[k], device_id=sibling, device_id_type=MESH).start()

    def relay(ins, outs, sems):
        far_send, far_recv, near_send, near_recv, local = sems
        me, sibling, chips = places()
        for k in range(n):
            _wait(outs[k], 3, far_send.at[k], far_recv.at[k], send=False)
            for chip in chips:
                block = outs[k].at[index((*chip, me[2]))]
                pltpu.make_async_remote_copy(src_ref=block, dst_ref=block, send_sem=near_send.at[k],
                                             recv_sem=near_recv.at[k], device_id=sibling, device_id_type=MESH).start()

    def finish(ins, outs, sems):
        far_send, far_recv, near_send, near_recv, local = sems
        for k in range(n):
            _wait(outs[k], 4, near_send.at[k], near_recv.at[k])
            _wait(outs[k], 3, far_send.at[k], far_recv.at[k], recv=False)
            pltpu.make_async_copy(ins[k], outs[k].at[0], local.at[k]).wait()

    shapes = [jax.ShapeDtypeStruct((N_DEV,) + tuple(a.shape), a.dtype) for a in shards]
    return CommJob(list(shards), shapes, [start, relay, finish], [pltpu.SemaphoreType.DMA((n,))] * 5)


def scatter_job(tensors):
    n = len(tensors)
    flat, where = [], {}
    for k, pieces in enumerate(tensors):
        d = 0
        for piece in pieces:
            for b in range(piece.shape[0]):
                where[k, d] = (len(flat), b)
                d += 1
            flat.append(piece)
        assert d == N_DEV

    def start(ins, outs, sems):
        send_sems, recv_sems, local_sems = sems
        me = _my_index()

        def block(k, d):
            i, b = where[k, d]
            return ins[i].at[b]

        for d in range(N_DEV):
            @pl.when(d != me)
            def _():
                for k in range(n):
                    pltpu.make_async_remote_copy(src_ref=block(k, d), dst_ref=outs[k].at[me], send_sem=send_sems.at[k],
                                                 recv_sem=recv_sems.at[k], device_id=(d // 4, (d // 2) % 2, d % 2),
                                                 device_id_type=MESH).start()

            @pl.when(d == me)
            def _():
                for k in range(n):
                    pltpu.make_async_copy(block(k, d), outs[k].at[d], local_sems.at[k]).start()

    def finish(ins, outs, sems):
        send_sems, recv_sems, local_sems = sems
        for k in range(n):
            _wait(outs[k], N_DEV - 1, send_sems.at[k], recv_sems.at[k])
            i, b = where[k, 0]
            pltpu.make_async_copy(ins[i].at[b], outs[k].at[0], local_sems.at[k]).wait()

    shapes = [jax.ShapeDtypeStruct((N_DEV,) + tuple(p[0].shape[1:]), p[0].dtype) for p in tensors]
    return CommJob(flat, shapes, [start, finish], [pltpu.SemaphoreType.DMA((n,))] * 3)


def comm_call(job, name):
    ni, no = len(job.operands), len(job.out_shape)

    def body(*refs):
        ins, outs, sems = refs[:ni], refs[ni:ni + no], refs[ni + no:]
        for phase in job.phases:
            phase(ins, outs, sems)

    return pl.pallas_call(body, name=name, in_specs=[ANY] * ni, out_specs=[ANY] * no, out_shape=job.out_shape,
                          scratch_shapes=job.scratch)(*job.operands)


def _carry(job, body, n_in, n_out, at_step):
    ji, jo, js = len(job.operands), len(job.out_shape), len(job.scratch)

    def carrier(*refs):
        a, b = n_in, n_in + ji
        c, d = b + n_out, b + n_out + jo
        e = len(refs) - js
        job_refs = (refs[a:b], refs[c:d], refs[e:])
        n = len(job.phases)

        @pl.when(at_step(0, n))
        def _():
            job.phases[0](*job_refs)

        body(*refs[:a], *refs[b:c], *refs[d:e])

        for i in range(1, n):
            @pl.when(at_step(i, n))
            def _():
                job.phases[i](*job_refs)

    return carrier


def _pad_lanes(v, lo, total=128):
    return jnp.pad(v, (lo, total - lo - v.shape[0]))[None, :]


MIXER_WEIGHTS = ("w_in", "w_q_up", "w_kv_up", "conv_w")
LATE_WEIGHTS = ("w_out", "w_gate_up", "w_down")


def mixer_operands(g, sw):
    w_in = g["w_in"].transpose(1, 0, 2).reshape(D_MODEL, D_IN)
    z = jnp.zeros((D_MODEL, 1), w_in.dtype)
    w_proj = jnp.concatenate(
        [w_in[:, :384], w_in[:, 416:928], w_in[:, 928:1952], w_in[:, 1952:1960], jnp.tile(z, (1, 56)),
         w_in[:, 384:416], jnp.tile(z, (1, 32))], axis=1)
    wq = jnp.pad(g["w_q_up"], ((0, 0), (0, 0), (0, HEAD_LANES - NOPE - ROPE)))
    wk = jnp.pad(g["w_kv_up"][:, :, :NOPE], ((0, 0), (0, 0), (0, HEAD_LANES - NOPE)))
    wv = g["w_kv_up"][:, :, NOPE:]
    qkv = (sw["q_a_norm_w"][None, :], sw["kv_a_norm_w"][None, :], wq, wk, wv,
           _pad_lanes(jnp.concatenate([sw["q_nope_norm_w"], sw["q_pe_norm_w"]]), 0),
           _pad_lanes(sw["k_nope_norm_w"], 0), _pad_lanes(sw["k_pe_norm_w"], NOPE))
    conv_w = g["conv_w"].astype(f32).transpose(1, 0, 2).reshape(4, D_CONV)
    ssd = (conv_w, sw["conv_b"][None, :], _pad_lanes(sw["dt_bias"], 0), _pad_lanes(sw["a_log"], 0),
           _pad_lanes(sw["d_skip"], 0), sw["ssd_norm_w"][None, :])
    return dict(w_proj=w_proj, qkv=qkv, ssd=ssd, n1=sw["norm1_w"][None, :])


def late_operands(g, sw):
    return dict(wo=g["w_out"].reshape(D_MODEL, D_MODEL), wgu=g["w_gate_up"],
                wd=g["w_down"].reshape(N_DEV // 2, FF_SHARD, D_MODEL), n2=sw["norm2_w"][None, :])


def layer_fwd(x, mod, kw, cos_t, sin_t, job=None, late=None):
    sh1, sc1, g1, sh2, sc2, g2 = [mod[i:i + 1] for i in range(6)]
    pa, pz, px, plast = proj_fwd(x, kw["n1"], sh1, sc1, kw["w_proj"])
    q, k, v = qkv_fwd(pa, plast, cos_t, sin_t, kw["qkv"])
    (o, lse), carried = attn_fwd(q, k, v, job)
    if late is not None:
        kw = {**kw, **late(carried)}
    yg, states = ssd_fwd(px, pz, plast, kw["ssd"])
    x_mid = out_fwd(x, o, yg, g1, kw["wo"])
    x_out = mlp_fwd(x_mid, kw["n2"], sh2, sc2, g2, kw["wgu"], kw["wd"])
    saved = dict(x=x, pa=pa, pz=pz, px=px, plast=plast, q=q, k=k, v=v, o=o, lse=lse, yg=yg, states=states, x_mid=x_mid)
    return x_out, saved, kw, carried


def layer_bwd_head(dy, mod, kw, sv, job=None):
    _, _, g1, sh2, sc2, g2 = [mod[i:i + 1] for i in range(6)]
    (dparts, dn2, dsh2, dsc2, dg2, dwg, dwu, dwd), carried = mlp_bwd(
        sv["x_mid"], dy, kw["n2"], sh2, sc2, g2, kw["wgu"], kw["wd"], job)
    dmid, do, delta, dyg, dg1, dwo = out_bwd(dy, dparts, sv["o"], sv["yg"], g1, kw["wo"])
    dq = attn_bwd_dq(sv["q"], sv["k"], sv["v"], do, sv["lse"], delta)
    early = dict(w_out=[dwo.reshape(N_DEV, D_MODEL // N_DEV, D_MODEL)], w_gate_up=[dwg, dwu],
                 w_down=[dwd.reshape(N_DEV, D_FF // N_DEV, D_MODEL)])
    head = dict(dmid=dmid, do=do, delta=delta, dyg=dyg, dq=dq, dn2=dn2, dsh2=dsh2, dsc2=dsc2, dg2=dg2, dg1=dg1)
    return head, early, carried


def layer_bwd_tail(hd, mod, kw, cos_t, sin_t, sv, job=None):
    sh1, sc1 = mod[0:1], mod[1:2]
    (dk, dv), carried = attn_bwd_dkv(sv["q"], sv["k"], sv["v"], hd["do"], sv["lse"], hd["delta"], job)
    dpx, dpz, dpl_dt, dcw, dcb, ddtb, dalog, ddskip, dsnw = ssd_bwd(sv["px"], sv["pz"], sv["plast"], sv["states"],
                                                                   hd["dyg"], kw["ssd"])
    dpa, dpl_k, dqaw, dkvaw, dwq, dwk, dwv, dqnw, dknw, dkpw = qkv_bwd(sv["pa"], sv["plast"], cos_t, sin_t, kw["qkv"],
                                                                       hd["dq"], dk, dv)
    dx, dn1, dsh1, dsc1, dwp = proj_bwd(sv["x"], kw["n1"], sh1, sc1, kw["w_proj"], dpa, dpz, dpx, dpl_k, dpl_dt, hd["dmid"])
    dmod = jnp.concatenate([dsh1, dsc1, hd["dg1"], hd["dsh2"], hd["dsc2"], hd["dg2"]], axis=0)
    dw_in = jnp.concatenate([dwp[:, :384], dwp[:, 1984:2016], dwp[:, 384:1920], dwp[:, 1920:1928]], axis=1)
    grads = dict(
        norm1_w=dn1[0], norm2_w=hd["dn2"][0], q_a_norm_w=dqaw[0], kv_a_norm_w=dkvaw[0],
        q_nope_norm_w=dqnw[0, :NOPE], q_pe_norm_w=dqnw[0, NOPE:NOPE + ROPE], k_nope_norm_w=dknw[0, :NOPE],
        k_pe_norm_w=dkpw[0, NOPE:NOPE + ROPE], conv_b=dcb[0], dt_bias=ddtb[0, :N_HEADS], a_log=dalog[0, :N_HEADS],
        d_skip=ddskip[0, :N_HEADS], ssd_norm_w=dsnw[0],
        w_in=[dw_in.reshape(D_MODEL, N_DEV, D_IN // N_DEV).transpose(1, 0, 2)],
        w_q_up=[dwq[:, :, :NOPE + ROPE].astype(bf16)],
        w_kv_up=[jnp.concatenate([dwk[:, :, :NOPE], dwv], axis=2).astype(bf16)],
        conv_w=[dcw.reshape(4, N_DEV, D_CONV // N_DEV).transpose(1, 0, 2).astype(bf16)],
    )
    return dx, dmod, grads, carried


def _pack_small(get, last=None):
    flat = jnp.concatenate([get(name).reshape(-1) for name, _ in SMALL])
    flat = jnp.pad(flat, (0, SMALL_ROWS * 128 - flat.shape[0]))
    if last is not None:
        flat = flat.at[-1].set(last)
    return flat.reshape(SMALL_ROWS, 128)


def _unpack_small(packed):
    flat = packed.reshape(-1)
    out, off = {}, 0
    for name, size in SMALL:
        out[name] = flat[off:off + 2 * size].reshape(2, size)
        off += 2 * size
    return out


def kernel(x, c, positions, norm1_w, norm2_w, w_ada, b_ada, w_in, q_a_norm_w, w_q_up, kv_a_norm_w, w_kv_up, q_nope_norm_w, q_pe_norm_w, k_nope_norm_w, k_pe_norm_w, conv_w, conv_b, dt_bias, a_log, d_skip, ssd_norm_w, w_out, w_gate_up, w_down, loss_target, m_norm1_w, m_norm2_w, m_w_ada, m_b_ada, m_w_in, m_q_a_norm_w, m_w_q_up, m_kv_a_norm_w, m_w_kv_up, m_q_nope_norm_w, m_q_pe_norm_w, m_k_nope_norm_w, m_k_pe_norm_w, m_conv_w, m_conv_b, m_dt_bias, m_a_log, m_d_skip, m_ssd_norm_w, m_w_out, m_w_gate_up, m_w_down, v_norm1_w, v_norm2_w, v_w_ada, v_b_ada, v_w_in, v_q_a_norm_w, v_w_q_up, v_kv_a_norm_w, v_w_kv_up, v_q_nope_norm_w, v_q_pe_norm_w, v_k_nope_norm_w, v_k_pe_norm_w, v_conv_w, v_conv_b, v_dt_bias, v_a_log, v_d_skip, v_ssd_norm_w, v_w_out, v_w_gate_up, v_w_down):
    w = dict(norm1_w=norm1_w, norm2_w=norm2_w, w_ada=w_ada, b_ada=b_ada, w_in=w_in, q_a_norm_w=q_a_norm_w, w_q_up=w_q_up,
             kv_a_norm_w=kv_a_norm_w, w_kv_up=w_kv_up, q_nope_norm_w=q_nope_norm_w, q_pe_norm_w=q_pe_norm_w,
             k_nope_norm_w=k_nope_norm_w, k_pe_norm_w=k_pe_norm_w, conv_w=conv_w, conv_b=conv_b, dt_bias=dt_bias,
             a_log=a_log, d_skip=d_skip, ssd_norm_w=ssd_norm_w, w_out=w_out, w_gate_up=w_gate_up, w_down=w_down)
    m = dict(norm1_w=m_norm1_w, norm2_w=m_norm2_w, w_ada=m_w_ada, b_ada=m_b_ada, w_in=m_w_in, q_a_norm_w=m_q_a_norm_w,
             w_q_up=m_w_q_up, kv_a_norm_w=m_kv_a_norm_w, w_kv_up=m_w_kv_up, q_nope_norm_w=m_q_nope_norm_w,
             q_pe_norm_w=m_q_pe_norm_w, k_nope_norm_w=m_k_nope_norm_w, k_pe_norm_w=m_k_pe_norm_w, conv_w=m_conv_w,
             conv_b=m_conv_b, dt_bias=m_dt_bias, a_log=m_a_log, d_skip=m_d_skip, ssd_norm_w=m_ssd_norm_w, w_out=m_w_out,
             w_gate_up=m_w_gate_up, w_down=m_w_down)
    v = dict(norm1_w=v_norm1_w, norm2_w=v_norm2_w, w_ada=v_w_ada, b_ada=v_b_ada, w_in=v_w_in, q_a_norm_w=v_q_a_norm_w,
             w_q_up=v_w_q_up, kv_a_norm_w=v_kv_a_norm_w, w_kv_up=v_w_kv_up, q_nope_norm_w=v_q_nope_norm_w,
             q_pe_norm_w=v_q_pe_norm_w, k_nope_norm_w=v_k_nope_norm_w, k_pe_norm_w=v_k_pe_norm_w, conv_w=v_conv_w,
             conv_b=v_conv_b, dt_bias=v_dt_bias, a_log=v_a_log, d_skip=v_d_skip, ssd_norm_w=v_ssd_norm_w, w_out=v_w_out,
             w_gate_up=v_w_gate_up, w_down=v_w_down)
    me = _my_index()
    seq = x.shape[1]

    def shards(names, l):
        return [w[name][l] if name == "conv_w" else w[name][l].astype(bf16) for name in names]

    small = [{name: w[name][l] for name, _ in SMALL if name != "b_ada"} for l in range(2)]
    n_mix, n_late = len(MIXER_WEIGHTS), len(LATE_WEIGHTS)

    first = comm_call(gather_job([c] + shards(MIXER_WEIGHTS, 0)), "gather_first")
    c_all = first[0].reshape(N_DEV, D_MODEL)
    kws = [mixer_operands(dict(zip(MIXER_WEIGHTS, first[1:])), small[0]), None]
    rest_job = gather_job(shards(LATE_WEIGHTS, 0) + shards(MIXER_WEIGHTS, 1) + shards(LATE_WEIGHTS, 1))

    b_cols = lax.dynamic_slice_in_dim(b_ada, me * 768, 768, axis=1)
    mod_cols = ada_fwd(c_all, w_ada, b_cols)
    (mod_all,) = comm_call(gather_job([mod_cols]), "gather_mod")
    mod_me = lax.dynamic_index_in_dim(mod_all, me, axis=2, keepdims=False)
    mods = [mod_me[:, l, :].reshape(6, D_MODEL) for l in range(2)]

    inv_freq = 1.0 / (ROPE_THETA ** (jnp.arange(0, ROPE, 2, dtype=f32) / ROPE))
    inv = _pad_lanes(jnp.concatenate([inv_freq, inv_freq]), NOPE)
    cos_t, sin_t = rope_tables(positions.reshape(seq, 1), inv)

    saved = [None, None]
    h, saved[0], kws[0], rest = layer_fwd(
        x[0], mods[0], kws[0], cos_t, sin_t, rest_job,
        lambda got: late_operands(dict(zip(LATE_WEIGHTS, got[:n_late])), small[0]))
    kws[1] = {**mixer_operands(dict(zip(MIXER_WEIGHTS, rest[n_late:n_late + n_mix])), small[1]),
              **late_operands(dict(zip(LATE_WEIGHTS, rest[n_late + n_mix:])), small[1])}
    h, saved[1], _, _ = layer_fwd(h, mods[1], kws[1], cos_t, sin_t)
    dy, loss_part = loss_fwd(h, loss_target[0])

    early, late = ("w_out", "w_gate_up", "w_down"), ("w_in", "w_q_up", "w_kv_up", "conv_w")
    parts = [{}, {}]
    head, pieces, _ = layer_bwd_head(dy, mods[1], kws[1], saved[1])
    dy, dmod1, grads1, got = layer_bwd_tail(head, mods[1], kws[1], cos_t, sin_t, saved[1], scatter_job([pieces[n] for n in early]))
    parts[1].update(zip(early, got))
    head, pieces, got = layer_bwd_head(dy, mods[0], kws[0], saved[0], scatter_job([grads1[n] for n in late]))
    parts[1].update(zip(late, got))
    dy, dmod0, grads0, got = layer_bwd_tail(head, mods[0], kws[0], cos_t, sin_t, saved[0], scatter_job([pieces[n] for n in early]))
    parts[0].update(zip(early, got))
    parts[0].update(zip(late, comm_call(scatter_job([grads0[n] for n in late]), "scatter_layer0_rest")))
    grad_x = dy[None]

    small_part = {name: jnp.stack([grads0[name], grads1[name]]) for name, _ in SMALL if name != "b_ada"}
    small_part["b_ada"] = jnp.stack([dmod0.reshape(-1), dmod1.reshape(-1)])
    (small_all,) = comm_call(gather_job([_pack_small(lambda n: small_part[n], loss_part[0, 0])]), "gather_small_grads")
    packed = adamw(small_all, _pack_small(lambda n: w[n])[None], _pack_small(lambda n: m[n])[None],
                   _pack_small(lambda n: v[n])[None], 0, None, "adamw_small")
    loss = packed[0][0, -1, -1]
    res = {}
    for key, arr in zip("gdmv", packed):
        for name, val in _unpack_small(arr[0]).items():
            res[key, name] = val

    off = 2 * (1024 + 1024)
    dmod_all = small_all.reshape(N_DEV, -1)[:, off:off + 2 * 6144].reshape(N_DEV, 2, 6144)
    dmod_cols = lax.dynamic_slice_in_dim(dmod_all, me * 768, 768, axis=2).transpose(1, 0, 2)
    g_ada = ada_bwd(c_all, dmod_cols)
    out = None
    for l in range(2):
        out = adamw(g_ada[l][None], w_ada, m_w_ada, v_w_ada, l, out, "adamw_w_ada")
    res.update(zip([(key, "w_ada") for key in "gdmv"], out))

    for name in BIG:
        out = None
        for l in range(2):
            out = adamw(parts[l][name], w[name], m[name], v[name], l, out, "adamw_" + name)
        res.update(zip([(key, name) for key in "gdmv"], out))

    return (loss, grad_x, *[res["g", n] for n in WEIGHTS], *[res["d", n] for n in WEIGHTS],
            *[res["m", n] for n in WEIGHTS], *[res["v", n] for n in WEIGHTS])
```

```python
import functools

import jax
import jax.numpy as jnp
from jax import lax
from jax.experimental import pallas as pl
from jax.experimental.pallas import tpu as pltpu

f32 = jnp.float32
bf16 = jnp.bfloat16

N_DEV = 8
D_MODEL = 1024
N_HEADS = 8
HEAD_LANES = 128
NOPE = 64
ROPE = 32
V_DIM = 64
Q_RANK = 256
KV_RANK = 128
D_SSD = 512
D_CONV = 1024
SSD_STATE = 128
SSD_HEAD_DIM = 64
CHUNK = 128
HALO = 8
D_FF = 2816
FF_SHARD = 704
D_IN = 1960
D_PROJ = 2048
EPS = 1e-6
LOG2E = 1.4426950408889634
LN2 = 0.6931471805599453
Q_SCALE = (NOPE + ROPE) ** -0.5 * LOG2E
ATTN_ROWS_FWD = 256
ATTN_ROWS_BWD = 512
ROPE_THETA = 10000.0
NEG = -1e30

ADAM_LR = 0.001
ADAM_B1 = 0.9
ADAM_B2 = 0.999
ADAM_EPS = 1e-08
ADAM_WD = 0.01
ADAM_STEP = 10
ADAMW_BLOCK_BYTES = 24 << 20

MESH = pl.DeviceIdType.MESH
ANY = pl.BlockSpec(memory_space=pl.ANY)

SMALL = (("norm1_w", 1024), ("norm2_w", 1024), ("b_ada", 6144), ("q_a_norm_w", 256), ("kv_a_norm_w", 128),
         ("q_nope_norm_w", 64), ("q_pe_norm_w", 32), ("k_nope_norm_w", 64), ("k_pe_norm_w", 32),
         ("conv_b", 1024), ("dt_bias", 8), ("a_log", 8), ("d_skip", 8), ("ssd_norm_w", 512))
SMALL_ROWS = 168
BIG = ("w_in", "w_q_up", "w_kv_up", "conv_w", "w_out", "w_gate_up", "w_down")
WEIGHTS = ("norm1_w", "norm2_w", "w_ada", "b_ada", "w_in", "q_a_norm_w", "w_q_up", "kv_a_norm_w", "w_kv_up",
           "q_nope_norm_w", "q_pe_norm_w", "k_nope_norm_w", "k_pe_norm_w", "conv_w", "conv_b", "dt_bias",
           "a_log", "d_skip", "ssd_norm_w", "w_out", "w_gate_up", "w_down")


def _dot(a, b, ca, cb):
    return lax.dot_general(a.astype(bf16), b.astype(bf16), (((ca,), (cb,)), ((), ())), preferred_element_type=f32)


@jax.custom_vjp
def mm(a, b):
    return _dot(a, b, 1, 0)


def _mm_fwd(a, b):
    return _dot(a, b, 1, 0), (a, b)


def _mm_bwd(res, g):
    a, b = res
    return _dot(g, b, 1, 1).astype(a.dtype), _dot(a, g, 0, 0).astype(b.dtype)


mm.defvjp(_mm_fwd, _mm_bwd)


@jax.custom_vjp
def _mm_slot(a, w, slot):
    return _dot(a, w, 1, 0)


def _mm_slot_fwd(a, w, slot):
    return _dot(a, w, 1, 0), (a, w)


def _mm_slot_bwd(res, g):
    a, w = res
    return _dot(g, w, 1, 1).astype(a.dtype), None, _dot(a, g, 0, 0)


_mm_slot.defvjp(_mm_slot_fwd, _mm_slot_bwd)


def mmw(a, w, slot=None):
    return _dot(a, w, 1, 0) if slot is None else _mm_slot(a, w, slot)


@jax.custom_vjp
def mm_nt(a, b):
    return _dot(a, b, 1, 1)


def _mm_nt_fwd(a, b):
    return _dot(a, b, 1, 1), (a, b)


def _mm_nt_bwd(res, g):
    a, b = res
    return _dot(g, b, 1, 0).astype(a.dtype), _dot(g, a, 0, 0).astype(b.dtype)


mm_nt.defvjp(_mm_nt_fwd, _mm_nt_bwd)


@jax.custom_vjp
def mm_tn(a, b):
    return _dot(a, b, 0, 0)


def _mm_tn_fwd(a, b):
    return _dot(a, b, 0, 0), (a, b)


def _mm_tn_bwd(res, g):
    a, b = res
    return _dot(b, g, 1, 1).astype(a.dtype), _dot(a, g, 1, 0).astype(b.dtype)


mm_tn.defvjp(_mm_tn_fwd, _mm_tn_bwd)


def _rms(x, w):
    return x * lax.rsqrt(jnp.mean(x * x, axis=-1, keepdims=True) + EPS) * w


def _const(shape):
    n = len(shape)
    return pl.BlockSpec(shape, lambda *_: (0,) * n)


def _accumulate(first, refs, vals):
    @pl.when(first)
    def _():
        for r, v in zip(refs, vals):
            r[...] = v

    @pl.when(jnp.logical_not(first))
    def _():
        for r, v in zip(refs, vals):
            r[...] += v


def _accumulate_then_cast(first, last, accs, outs, vals):
    _accumulate(first, accs, vals)

    @pl.when(last)
    def _():
        for a, o in zip(accs, outs):
            o[...] = a[...].astype(o.dtype)


def _token_block(s):
    return min(512, s)


def _f_proj(x, nw, sh, sc, w, slot=None):
    h = _rms(x, nw) * (1.0 + sc) + sh
    return mmw(h, w, slot)


def _f_qkv(pa, plast, cos_t, sin_t, qaw, kvaw, wq, wk, wv, qnw, knw, kpw, slots=None):
    sq, sk, sv = slots if slots is not None else ([None] * N_HEADS,) * 3
    lane = lax.broadcasted_iota(jnp.int32, (1, HEAD_LANES), 1)
    m_nope = lane < NOPE
    m_pe = (lane >= NOPE) & (lane < NOPE + ROPE)
    rows = pa.shape[0]

    def rope(t):
        half = ROPE // 2
        swapped = jnp.concatenate(
            [jnp.zeros((rows, NOPE), f32), t[:, NOPE + half:NOPE + ROPE], t[:, NOPE:NOPE + half],
             jnp.zeros((rows, HEAD_LANES - NOPE - ROPE), f32)], axis=1)
        return t * cos_t + swapped * sin_t

    qa = _rms(pa[:, :Q_RANK], qaw)
    kva = _rms(pa[:, Q_RANK:Q_RANK + KV_RANK], kvaw)
    kp = jnp.where(m_pe, plast, 0.0)
    kp = kp * lax.rsqrt(jnp.sum(kp * kp, axis=-1, keepdims=True) / ROPE + EPS) * kpw
    k_rot = rope(kp)
    qs, ks, vs = [], [], []
    for h in range(N_HEADS):
        qh = mmw(qa, wq[h], sq[h])
        ss_n = jnp.sum(jnp.where(m_nope, qh * qh, 0.0), axis=-1, keepdims=True) / NOPE
        ss_p = jnp.sum(jnp.where(m_pe, qh * qh, 0.0), axis=-1, keepdims=True) / ROPE
        r = jnp.where(m_nope, lax.rsqrt(ss_n + EPS), lax.rsqrt(ss_p + EPS))
        qs.append(rope(qh * r * qnw) * Q_SCALE)
        kh = mmw(kva, wk[h], sk[h])
        kh = kh * lax.rsqrt(jnp.sum(kh * kh, axis=-1, keepdims=True) / NOPE + EPS) * knw
        ks.append(kh + k_rot)
        vs.append(mmw(kva, wv[h], sv[h]))
    return jnp.stack(qs), jnp.stack(ks), jnp.stack(vs)


def _f_ssd(xext, z, plast, prev, cw, cb, dtb, alog, dskip, snw):
    n = CHUNK
    conv = cb
    for k in range(4):
        conv = conv + cw[k:k + 1] * xext[HALO - 3 + k:HALO - 3 + k + n]
    xc = jax.nn.silu(conv)
    xs, bm, cm = xc[:, :D_SSD], xc[:, D_SSD:D_SSD + 2 * SSD_STATE], xc[:, D_SSD + 2 * SSD_STATE:]
    lane = lax.broadcasted_iota(jnp.int32, (1, 128), 1)
    dt = jax.nn.softplus(jnp.where(lane < N_HEADS, plast, 0.0) + dtb)
    adt = dt * (-jnp.exp(alog))
    row = lax.broadcasted_iota(jnp.int32, (n, n), 0)
    col = lax.broadcasted_iota(jnp.int32, (n, n), 1)
    tri = row >= col
    acs = jnp.dot(tri.astype(f32), adt, precision=lax.Precision.HIGHEST, preferred_element_type=f32)
    acs_t = acs.T
    ys, news = [], []
    for g in range(2):
        bg = bm[:, g * SSD_STATE:(g + 1) * SSD_STATE]
        cg = cm[:, g * SSD_STATE:(g + 1) * SSD_STATE]
        cb_t = mm_nt(cg, bg)
        for r in range(4):
            h = g * 4 + r
            a_col = acs[:, h:h + 1]
            a_row = acs_t[h:h + 1, :]
            decay_ls = jnp.exp(jnp.where(tri, a_col - a_row, -jnp.inf))
            xh = xs[:, h * SSD_HEAD_DIM:(h + 1) * SSD_HEAD_DIM]
            xdt = xh * dt[:, h:h + 1]
            y_diag = mm(cb_t * decay_ls, xdt)
            a_last = acs[n - 1:n, h:h + 1]
            st = mm_tn(xdt * jnp.exp(a_last - a_col), bg)
            news.append(jnp.exp(a_last) * prev[h] + st)
            y_off = mm_nt(cg, prev[h]) * jnp.exp(a_col)
            ys.append(y_diag + y_off + dskip[:, h:h + 1] * xh)
    y = jnp.concatenate(ys, axis=1)
    yg = y * jax.nn.silu(z)
    half = D_SSD // 2
    outs = []
    for g in range(2):
        t = yg[:, g * half:(g + 1) * half]
        outs.append(t * lax.rsqrt(jnp.mean(t * t, axis=-1, keepdims=True) + EPS))
    return jnp.concatenate(outs, axis=1) * snw, jnp.stack(news)


def _f_out(o, yg, g1, wo, slot=None):
    cat = jnp.concatenate([o[h] for h in range(N_HEADS)] + [yg], axis=1)
    return g1 * mmw(cat, wo, slot)


def _f_gate_up(x, nw, sh, sc, wg, wu, slot_g=None, slot_u=None):
    h = _rms(x, nw) * (1.0 + sc) + sh
    return jax.nn.silu(mmw(h, wg, slot_g)) * mmw(h, wu, slot_u)


def proj_fwd(x, nw, sh, sc, w):
    s = x.shape[0]
    ts = _token_block(s)

    def body(x_ref, nw_ref, sh_ref, sc_ref, w_ref, pa_ref, pz_ref, px_ref, pl_ref):
        p = _f_proj(x_ref[...], nw_ref[...], sh_ref[...], sc_ref[...], w_ref[...])
        pa_ref[...] = p[:, :384]
        pz_ref[...] = p[:, 384:896]
        px_ref[...] = p[:, 896:1920]
        pl_ref[...] = p[:, 1920:]

    vec = _const((1, D_MODEL))
    return pl.pallas_call(
        body, name="proj_fwd", grid=(s // ts,),
        in_specs=[pl.BlockSpec((ts, D_MODEL), lambda i: (i, 0)), vec, vec, vec, _const((D_MODEL, D_PROJ))],
        out_specs=[pl.BlockSpec((ts, 384), lambda i: (i, 0)), pl.BlockSpec((ts, 512), lambda i: (i, 0)),
                   pl.BlockSpec((ts, 1024), lambda i: (i, 0)), pl.BlockSpec((ts, 128), lambda i: (i, 0))],
        out_shape=[jax.ShapeDtypeStruct((s, 384), f32), jax.ShapeDtypeStruct((s, 512), f32),
                   jax.ShapeDtypeStruct((s, 1024), f32), jax.ShapeDtypeStruct((s, 128), f32)],
    )(x, nw, sh, sc, w)


def rope_tables(pos, inv):
    s = pos.shape[0]
    ts = _token_block(s)

    def body(pos_ref, inv_ref, cos_ref, sin_ref):
        ang = pos_ref[...].astype(f32) * inv_ref[...]
        lane = lax.broadcasted_iota(jnp.int32, (1, HEAD_LANES), 1)
        half = ROPE // 2
        cos_ref[...] = jnp.where(lane < NOPE, 1.0, jnp.where(lane < NOPE + ROPE, jnp.cos(ang), 0.0))
        sn = jnp.sin(ang)
        sin_ref[...] = jnp.where((lane >= NOPE) & (lane < NOPE + half), -sn,
                                 jnp.where((lane >= NOPE + half) & (lane < NOPE + ROPE), sn, 0.0))

    return pl.pallas_call(
        body, name="rope_tables", grid=(s // ts,),
        in_specs=[pl.BlockSpec((ts, 1), lambda i: (i, 0)), _const((1, HEAD_LANES))],
        out_specs=[pl.BlockSpec((ts, HEAD_LANES), lambda i: (i, 0))] * 2,
        out_shape=[jax.ShapeDtypeStruct((s, HEAD_LANES), f32)] * 2,
    )(pos, inv)


def _qkv_param_specs():
    return [_const((1, Q_RANK)), _const((1, KV_RANK)), _const((N_HEADS, Q_RANK, HEAD_LANES)),
            _const((N_HEADS, KV_RANK, HEAD_LANES)), _const((N_HEADS, KV_RANK, V_DIM)),
            _const((1, HEAD_LANES)), _const((1, HEAD_LANES)), _const((1, HEAD_LANES))]


def qkv_fwd(pa, plast, cos_t, sin_t, params):
    s = pa.shape[0]
    ts = _token_block(s)

    def body(pa_ref, pl_ref, cos_ref, sin_ref, *rest):
        prm = [r[...] for r in rest[:8]]
        q_ref, k_ref, v_ref = rest[8:]
        q, k, v = _f_qkv(pa_ref[...], pl_ref[...], cos_ref[...], sin_ref[...], *prm)
        q_ref[...] = q.astype(bf16)
        k_ref[...] = k.astype(bf16)
        v_ref[...] = v.astype(bf16)

    tok = lambda w: pl.BlockSpec((ts, w), lambda i: (i, 0))
    return pl.pallas_call(
        body, name="qkv_fwd", grid=(s // ts,),
        in_specs=[tok(384), tok(128), tok(128), tok(128)] + _qkv_param_specs(),
        out_specs=[pl.BlockSpec((N_HEADS, ts, HEAD_LANES), lambda i: (0, i, 0)),
                   pl.BlockSpec((N_HEADS, ts, HEAD_LANES), lambda i: (0, i, 0)),
                   pl.BlockSpec((N_HEADS, ts, V_DIM), lambda i: (0, i, 0))],
        out_shape=[jax.ShapeDtypeStruct((N_HEADS, s, HEAD_LANES), bf16), jax.ShapeDtypeStruct((N_HEADS, s, HEAD_LANES), bf16),
                   jax.ShapeDtypeStruct((N_HEADS, s, V_DIM), bf16)],
    )(pa, plast, cos_t, sin_t, *params)


def _scores(q, k):
    return lax.dot_general(q, k, (((1,), (1,)), ((), ())), preferred_element_type=f32)


def _tril(rows, cols, row_offset):
    row = row_offset + lax.broadcasted_iota(jnp.int32, (rows, cols), 0)
    col = lax.broadcasted_iota(jnp.int32, (rows, cols), 1)
    return row >= col


def _call_with_job(body, name, grid, job, in_specs, out_specs, out_shape, scratch_shapes, operands):
    if job is None:
        res = pl.pallas_call(body, name=name, grid=grid, in_specs=in_specs, out_specs=out_specs, out_shape=out_shape,
                             scratch_shapes=scratch_shapes)(*operands)
        return res, None

    def at_step(i, n):
        if i == 0:
            want = [0] * len(grid)
        elif i == n - 1:
            want = [g - 1 for g in grid]
        else:
            want = [grid[0] - 1] + [0] * (len(grid) - 1)
        return functools.reduce(jnp.logical_and, [pl.program_id(a) == s for a, s in enumerate(want)])

    carrier = _carry(job, body, len(in_specs), len(out_specs), at_step)
    res = pl.pallas_call(
        carrier, name=name, grid=grid,
        in_specs=list(in_specs) + [ANY] * len(job.operands), out_specs=list(out_specs) + [ANY] * len(job.out_shape),
        out_shape=list(out_shape) + list(job.out_shape), scratch_shapes=list(scratch_shapes) + job.scratch,
    )(*operands, *job.operands)
    return res[:len(out_specs)], res[len(out_specs):]


def attn_fwd(q, k, v, job=None):
    s = q.shape[1]
    t = _token_block(s)
    nb = s // t

    rb = min(ATTN_ROWS_FWD, t)

    def body(q_ref, k_ref, v_ref, o_ref, lse_ref, m_sc, l_sc, acc_sc):
        qi = pl.program_id(1)
        m_sc[...] = jnp.full(m_sc.shape, NEG, f32)
        l_sc[...] = jnp.zeros(l_sc.shape, f32)
        acc_sc[...] = jnp.zeros(acc_sc.shape, f32)

        def step(k0, diagonal):
            for r in range(t // rb):
                rows = pl.ds(r * rb, rb)
                nk = (r + 1) * rb if diagonal else t
                keys = pl.ds(k0, nk)
                sc = _scores(q_ref[rows, :], k_ref[keys, :])
                if diagonal:
                    sc = jnp.where(_tril(rb, nk, r * rb), sc, NEG)
                m_prev = m_sc[rows, :1]
                m_new = jnp.maximum(m_prev, jnp.max(sc, axis=-1, keepdims=True))
                p = jnp.exp2(sc - m_new)
                alpha = jnp.exp2(m_prev - m_new)
                l_new = alpha * l_sc[rows, :1] + jnp.sum(p, axis=-1, keepdims=True)
                acc = alpha * acc_sc[rows, :] + jnp.dot(p.astype(bf16), v_ref[keys, :], preferred_element_type=f32)
                if diagonal:
                    o_ref[rows, :] = acc / l_new
                    lse_ref[rows, :] = jnp.broadcast_to(m_new + jnp.log2(l_new), (rb, 128))
                else:
                    acc_sc[rows, :] = acc
                    m_sc[rows, :] = jnp.broadcast_to(m_new, (rb, 128))
                    l_sc[rows, :] = jnp.broadcast_to(l_new, (rb, 128))

        def below(ki, carry):
            step(pl.multiple_of(ki * t, t), False)
            return carry

        lax.fori_loop(0, qi, below, 0)
        step(pl.multiple_of(qi * t, t), True)

    return _call_with_job(
        body, "attn_fwd" if job is None else "attn_fwd_comm", (N_HEADS, nb), job,
        in_specs=[pl.BlockSpec((None, t, HEAD_LANES), lambda h, qi: (h, qi, 0)),
                  pl.BlockSpec((None, s, HEAD_LANES), lambda h, qi: (h, 0, 0)),
                  pl.BlockSpec((None, s, V_DIM), lambda h, qi: (h, 0, 0))],
        out_specs=[pl.BlockSpec((None, t, V_DIM), lambda h, qi: (h, qi, 0)),
                   pl.BlockSpec((None, t, 128), lambda h, qi: (h, qi, 0))],
        out_shape=[jax.ShapeDtypeStruct((N_HEADS, s, V_DIM), f32), jax.ShapeDtypeStruct((N_HEADS, s, 128), f32)],
        scratch_shapes=[pltpu.VMEM((t, 128), f32), pltpu.VMEM((t, 128), f32), pltpu.VMEM((t, V_DIM), f32)],
        operands=(q, k, v))


def _ssd_param_specs():
    return [_const((4, D_CONV)), _const((1, D_CONV)), _const((1, 128)), _const((1, 128)), _const((1, 128)),
            _const((1, D_SSD))]


def ssd_fwd(px, pz, plast, params):
    s = px.shape[0]
    nc = s // CHUNK

    def body(px_ref, pz_ref, pl_ref, cw_ref, cb_ref, dtb_ref, alog_ref, dskip_ref, snw_ref, yg_ref, st_ref,
             state_sc, halo_sc):
        i = pl.program_id(0)

        @pl.when(i == 0)
        def _():
            state_sc[...] = jnp.zeros(state_sc.shape, f32)
            halo_sc[...] = jnp.zeros(halo_sc.shape, f32)

        x = px_ref[...]
        prev = state_sc[...]
        st_ref[...] = prev
        xext = jnp.concatenate([halo_sc[...], x], axis=0)
        yg, new = _f_ssd(xext, pz_ref[...], pl_ref[...], prev, cw_ref[...], cb_ref[...], dtb_ref[...],
                         alog_ref[...], dskip_ref[...], snw_ref[...])
        yg_ref[...] = yg
        state_sc[...] = new
        halo_sc[...] = x[CHUNK - HALO:]

    tok = lambda w: pl.BlockSpec((CHUNK, w), lambda i: (i, 0))
    return pl.pallas_call(
        body, name="ssd_fwd", grid=(nc,),
        in_specs=[tok(D_CONV), tok(D_SSD), tok(128)] + _ssd_param_specs(),
        out_specs=[tok(D_SSD), pl.BlockSpec((None, N_HEADS, SSD_HEAD_DIM, SSD_STATE), lambda i: (i, 0, 0, 0))],
        out_shape=[jax.ShapeDtypeStruct((s, D_SSD), f32),
                   jax.ShapeDtypeStruct((nc, N_HEADS, SSD_HEAD_DIM, SSD_STATE), f32)],
        scratch_shapes=[pltpu.VMEM((N_HEADS, SSD_HEAD_DIM, SSD_STATE), f32), pltpu.VMEM((HALO, D_CONV), f32)],
    )(px, pz, plast, *params)


def out_fwd(x, o, yg, g1, wo):
    s = x.shape[0]
    ts = _token_block(s)

    def body(x_ref, o_ref, yg_ref, g1_ref, wo_ref, out_ref):
        out_ref[...] = x_ref[...] + _f_out(o_ref[...], yg_ref[...], g1_ref[...], wo_ref[...])

    return pl.pallas_call(
        body, name="out_fwd", grid=(s // ts,),
        in_specs=[pl.BlockSpec((ts, D_MODEL), lambda i: (i, 0)), pl.BlockSpec((N_HEADS, ts, V_DIM), lambda i: (0, i, 0)),
                  pl.BlockSpec((ts, D_SSD), lambda i: (i, 0)), _const((1, D_MODEL)), _const((D_MODEL, D_MODEL))],
        out_specs=pl.BlockSpec((ts, D_MODEL), lambda i: (i, 0)),
        out_shape=jax.ShapeDtypeStruct((s, D_MODEL), f32),
    )(x, o, yg, g1, wo)


def mlp_fwd(x, nw, sh, sc, g2, wgu, wd):
    s = x.shape[0]
    ts = _token_block(s)
    nj = N_DEV // 2

    def body(x_ref, nw_ref, sh_ref, sc_ref, g2_ref, wg_ref, wu_ref, wd_ref, out_ref, mix_ref):
        j = pl.program_id(1)
        act = _f_gate_up(x_ref[...], nw_ref[...], sh_ref[...], sc_ref[...], wg_ref[...], wu_ref[...])
        _accumulate(j == 0, [mix_ref], [mmw(act, wd_ref[...])])

        @pl.when(j == nj - 1)
        def _():
            out_ref[...] = x_ref[...] + g2_ref[...] * mix_ref[...]

    vec = _const((1, D_MODEL))
    return pl.pallas_call(
        body, name="mlp_fwd", grid=(s // ts, nj),
        in_specs=[pl.BlockSpec((ts, D_MODEL), lambda i, j: (i, 0)), vec, vec, vec, vec,
                  pl.BlockSpec((None, D_MODEL, FF_SHARD), lambda i, j: (j, 0, 0)),
                  pl.BlockSpec((None, D_MODEL, FF_SHARD), lambda i, j: (j + nj, 0, 0)),
                  pl.BlockSpec((None, FF_SHARD, D_MODEL), lambda i, j: (j, 0, 0))],
        out_specs=[pl.BlockSpec((ts, D_MODEL), lambda i, j: (i, 0))] * 2,
        out_shape=[jax.ShapeDtypeStruct((s, D_MODEL), f32)] * 2,
    )(x, nw, sh, sc, g2, wgu, wgu, wd)


def loss_fwd(y, target):
    s = y.shape[0]
    ts = _token_block(s)

    def body(y_ref, t_ref, dy_ref, loss_ref):
        d = y_ref[...] - t_ref[...]
        dy_ref[...] = d * (1.0 / D_MODEL)
        part = 0.5 * jnp.sum(jnp.sum(d * d, axis=-1, keepdims=True) * (1.0 / D_MODEL), axis=0, keepdims=True)
        _accumulate(pl.program_id(0) == 0, [loss_ref], [jnp.broadcast_to(part, (8, 128))])

    return pl.pallas_call(
        body, name="loss_fwd", grid=(s // ts,),
        in_specs=[pl.BlockSpec((ts, D_MODEL), lambda i: (i, 0))] * 2,
        out_specs=[pl.BlockSpec((ts, D_MODEL), lambda i: (i, 0)), _const((8, 128))],
        out_shape=[jax.ShapeDtypeStruct((s, D_MODEL), f32), jax.ShapeDtypeStruct((8, 128), f32)],
    )(y, target)


def mlp_bwd(x, dy, mix, nw, sh, sc, g2, wgu, wd, job=None):
    s = x.shape[0]
    ts = min(256, s)
    nj = N_DEV // 2
    ni = s // ts

    def body(x_ref, dy_ref, mix_ref, nw_ref, sh_ref, sc_ref, g2_ref, wg_ref, wu_ref, wd_ref,
             dx_ref, dnw_ref, dsh_ref, dsc_ref, dg2_ref, dwg_ref, dwu_ref, dwd_ref, ag_sc, au_sc, ad_sc):
        j, i = pl.program_id(0), pl.program_id(1)
        wg, wu, wd = wg_ref[...], wu_ref[...], wd_ref[...]
        dy = dy_ref[...]
        act, vjp = jax.vjp(lambda x_, nw_, sh_, sc_, sg, su: _f_gate_up(x_, nw_, sh_, sc_, wg, wu, sg, su),
                           x_ref[...], nw_ref[...], sh_ref[...], sc_ref[...],
                           jnp.zeros(wg.shape, f32), jnp.zeros(wu.shape, f32))
        dmix = dy * g2_ref[...]
        dact = _dot(dmix, wd, 1, 1)
        dwd = _dot(act, dmix, 0, 0)
        dx, dnw, dsh, dsc, dwg, dwu = vjp(dact)
        dx_ref[...] = dx
        _accumulate((i == 0) & (j == 0), [dnw_ref, dsh_ref, dsc_ref], [dnw, dsh, dsc])
        _accumulate_then_cast(i == 0, i == ni - 1, [ag_sc, au_sc, ad_sc], [dwg_ref, dwu_ref, dwd_ref], [dwg, dwu, dwd])

        @pl.when(j == 0)
        def _():
            _accumulate(i == 0, [dg2_ref], [jnp.sum(dy * mix_ref[...], axis=0, keepdims=True)])

    vec = _const((1, D_MODEL))
    vshape = jax.ShapeDtypeStruct((1, D_MODEL), f32)
    wspec = lambda off: pl.BlockSpec((None, D_MODEL, FF_SHARD), lambda j, i: (j + off, 0, 0))
    dspec = pl.BlockSpec((None, FF_SHARD, D_MODEL), lambda j, i: (j, 0, 0))
    return _call_with_job(
        body, "mlp_bwd" if job is None else "mlp_bwd_comm", (nj, ni), job,
        in_specs=[pl.BlockSpec((ts, D_MODEL), lambda j, i: (i, 0)), pl.BlockSpec((ts, D_MODEL), lambda j, i: (i, 0)),
                  pl.BlockSpec((ts, D_MODEL), lambda j, i: (jnp.where(j == 0, i, 0), 0)),
                  vec, vec, vec, vec, wspec(0), wspec(nj), dspec],
        out_specs=[pl.BlockSpec((None, ts, D_MODEL), lambda j, i: (j, i, 0)), vec, vec, vec, vec,
                   wspec(0), wspec(0), dspec],
        out_shape=[jax.ShapeDtypeStruct((nj, s, D_MODEL), f32), vshape, vshape, vshape, vshape,
                   jax.ShapeDtypeStruct((nj, D_MODEL, FF_SHARD), bf16), jax.ShapeDtypeStruct((nj, D_MODEL, FF_SHARD), bf16),
                   jax.ShapeDtypeStruct((nj, FF_SHARD, D_MODEL), bf16)],
        scratch_shapes=[pltpu.VMEM((D_MODEL, FF_SHARD), f32), pltpu.VMEM((D_MODEL, FF_SHARD), f32),
                        pltpu.VMEM((FF_SHARD, D_MODEL), f32)],
        operands=(x, dy, mix, nw, sh, sc, g2, wgu, wgu, wd))


def out_bwd(dy, dparts, o, yg, g1, wo):
    s = dy.shape[0]
    ts = _token_block(s)
    nj = dparts.shape[0]

    ni = s // ts

    def body(dy_ref, dp_ref, o_ref, yg_ref, g1_ref, wo_ref, dx_ref, do_ref, delta_ref, dyg_ref, dg1_ref, dwo_ref, acc_sc):
        i = pl.program_id(0)
        g = dy_ref[...]
        for j in range(nj):
            g = g + dp_ref[j]
        dx_ref[...] = g
        o = o_ref[...]
        wo = wo_ref[...]
        _, vjp = jax.vjp(lambda o_, yg_, g1_, slot: _f_out(o_, yg_, g1_, wo, slot), o, yg_ref[...], g1_ref[...],
                         jnp.zeros(wo.shape, f32))
        do, dyg, dg1, dwo = vjp(g)
        do_ref[...] = do
        dyg_ref[...] = dyg
        delta_ref[...] = jnp.broadcast_to(jnp.sum(do * o, axis=-1, keepdims=True), delta_ref.shape)
        _accumulate(i == 0, [dg1_ref], [dg1])
        _accumulate_then_cast(i == 0, i == ni - 1, [acc_sc], [dwo_ref], [dwo])

    head = pl.BlockSpec((N_HEADS, ts, V_DIM), lambda i: (0, i, 0))
    return pl.pallas_call(
        body, name="out_bwd", grid=(ni,), scratch_shapes=[pltpu.VMEM((D_MODEL, D_MODEL), f32)],
        in_specs=[pl.BlockSpec((ts, D_MODEL), lambda i: (i, 0)), pl.BlockSpec((nj, ts, D_MODEL), lambda i: (0, i, 0)),
                  head, pl.BlockSpec((ts, D_SSD), lambda i: (i, 0)), _const((1, D_MODEL)), _const((D_MODEL, D_MODEL))],
        out_specs=[pl.BlockSpec((ts, D_MODEL), lambda i: (i, 0)), head,
                   pl.BlockSpec((N_HEADS, ts, 128), lambda i: (0, i, 0)), pl.BlockSpec((ts, D_SSD), lambda i: (i, 0)),
                   _const((1, D_MODEL)), _const((D_MODEL, D_MODEL))],
        out_shape=[jax.ShapeDtypeStruct((s, D_MODEL), f32), jax.ShapeDtypeStruct((N_HEADS, s, V_DIM), f32),
                   jax.ShapeDtypeStruct((N_HEADS, s, 128), f32), jax.ShapeDtypeStruct((s, D_SSD), f32),
                   jax.ShapeDtypeStruct((1, D_MODEL), f32), jax.ShapeDtypeStruct((D_MODEL, D_MODEL), bf16)],
    )(dy, dparts, o, yg, g1, wo)


def attn_bwd_dq(q, k, v, do, lse, delta):
    s = q.shape[1]
    t = _token_block(s)
    nb = s // t

    rb = min(ATTN_ROWS_BWD, t)

    def body(q_ref, k_ref, v_ref, do_ref, lse_ref, delta_ref, dq_ref):
        qi = pl.program_id(1)
        dq_ref[...] = jnp.zeros(dq_ref.shape, f32)

        def step(k0, diagonal):
            for r in range(t // rb):
                rows = pl.ds(r * rb, rb)
                nk = (r + 1) * rb if diagonal else t
                k = k_ref[pl.ds(k0, nk), :]
                sc = _scores(q_ref[rows, :], k)
                if diagonal:
                    sc = jnp.where(_tril(rb, nk, r * rb), sc, NEG)
                p = jnp.exp2(sc - lse_ref[rows, :1])
                dp = lax.dot_general(do_ref[rows, :].astype(bf16), v_ref[pl.ds(k0, nk), :], (((1,), (1,)), ((), ())),
                                     preferred_element_type=f32)
                ds = p * (dp - delta_ref[rows, :1])
                dq = dq_ref[rows, :] + jnp.dot(ds.astype(bf16), k, preferred_element_type=f32)
                dq_ref[rows, :] = dq * LN2 if diagonal else dq

        def below(ki, carry):
            step(pl.multiple_of(ki * t, t), False)
            return carry

        lax.fori_loop(0, qi, below, 0)
        step(pl.multiple_of(qi * t, t), True)

    qspec = lambda w: pl.BlockSpec((None, t, w), lambda h, qi: (h, qi, 0))
    kspec = lambda w: pl.BlockSpec((None, s, w), lambda h, qi: (h, 0, 0))
    return pl.pallas_call(
        body, name="attn_bwd_dq", grid=(N_HEADS, nb),
        in_specs=[qspec(HEAD_LANES), kspec(HEAD_LANES), kspec(V_DIM), qspec(V_DIM), qspec(128), qspec(128)],
        out_specs=qspec(HEAD_LANES),
        out_shape=jax.ShapeDtypeStruct((N_HEADS, s, HEAD_LANES), f32),
    )(q, k, v, do, lse, delta)


def attn_bwd_dkv(q, k, v, do, lse, delta, job=None):
    s = q.shape[1]
    t = _token_block(s)
    nb = s // t

    cb = min(ATTN_ROWS_BWD, t)

    def body(q_ref, k_ref, v_ref, do_ref, lse_ref, delta_ref, dk_ref, dv_ref):
        ki = pl.program_id(1)
        dk_ref[...] = jnp.zeros(dk_ref.shape, f32)
        dv_ref[...] = jnp.zeros(dv_ref.shape, f32)

        def step(q0, diagonal):
            for c in range(t // cb):
                keys = pl.ds(c * cb, cb)
                r0 = c * cb if diagonal else 0
                rows = pl.ds(q0 + r0, t - r0)
                q = q_ref[rows, :]
                do = do_ref[rows, :].astype(bf16)
                sc = _scores(q, k_ref[keys, :])
                if diagonal:
                    sc = jnp.where(_tril(t - r0, cb, 0), sc, NEG)
                p = jnp.exp2(sc - lse_ref[rows, :1])
                dp = lax.dot_general(do, v_ref[keys, :], (((1,), (1,)), ((), ())), preferred_element_type=f32)
                ds = p * (dp - delta_ref[rows, :1])
                dv_ref[keys, :] += lax.dot_general(p.astype(bf16), do, (((0,), (0,)), ((), ())), preferred_element_type=f32)
                dk_ref[keys, :] += lax.dot_general(ds.astype(bf16), q, (((0,), (0,)), ((), ())), preferred_element_type=f32)

        step(pl.multiple_of(ki * t, t), True)

        def above(qi, carry):
            step(pl.multiple_of(qi * t, t), False)
            return carry

        lax.fori_loop(ki + 1, nb, above, 0)
        dk_ref[...] = dk_ref[...] * LN2

    qspec = lambda w: pl.BlockSpec((None, s, w), lambda h, ki: (h, 0, 0))
    kspec = lambda w: pl.BlockSpec((None, t, w), lambda h, ki: (h, ki, 0))
    return _call_with_job(
        body, "attn_bwd_dkv" if job is None else "attn_bwd_dkv_comm", (N_HEADS, nb), job,
        in_specs=[qspec(HEAD_LANES), kspec(HEAD_LANES), kspec(V_DIM), qspec(V_DIM), qspec(128), qspec(128)],
        out_specs=[kspec(HEAD_LANES), kspec(V_DIM)],
        out_shape=[jax.ShapeDtypeStruct((N_HEADS, s, HEAD_LANES), f32), jax.ShapeDtypeStruct((N_HEADS, s, V_DIM), f32)],
        scratch_shapes=[], operands=(q, k, v, do, lse, delta))


def ssd_bwd(px, pz, plast, states, dyg, params):
    s = px.shape[0]
    nc = s // CHUNK
    per = CHUNK // HALO

    def body(px_ref, halo_ref, pz_ref, pl_ref, st_ref, dyg_ref, cw_ref, cb_ref, dtb_ref, alog_ref, dskip_ref, snw_ref,
             dpx_ref, dpz_ref, dpl_ref, dcw_ref, dcb_ref, ddtb_ref, dalog_ref, ddskip_ref, dsnw_ref, dstate_sc, dhalo_sc):
        t = pl.program_id(0)
        chunk = nc - 1 - t

        @pl.when(t == 0)
        def _():
            dstate_sc[...] = jnp.zeros(dstate_sc.shape, f32)
            dhalo_sc[...] = jnp.zeros(dhalo_sc.shape, f32)

        halo = jnp.where(chunk > 0, halo_ref[...], 0.0)
        xext = jnp.concatenate([halo, px_ref[...]], axis=0)
        _, vjp = jax.vjp(_f_ssd, xext, pz_ref[...], pl_ref[...], st_ref[...], cw_ref[...], cb_ref[...], dtb_ref[...],
                         alog_ref[...], dskip_ref[...], snw_ref[...])
        dxext, dz, dpl, dprev, dcw, dcb, ddtb, dalog, ddskip, dsnw = vjp((dyg_ref[...], dstate_sc[...]))
        dpx_ref[...] = dxext[HALO:]
        dpx_ref[CHUNK - HALO:, :] += dhalo_sc[...]
        dhalo_sc[...] = dxext[:HALO]
        dstate_sc[...] = dprev
        dpz_ref[...] = dz
        dpl_ref[...] = dpl
        _accumulate(t == 0, [dcw_ref, dcb_ref, ddtb_ref, dalog_ref, ddskip_ref, dsnw_ref],
                    [dcw, dcb, ddtb, dalog, ddskip, dsnw])

    rev = lambda w: pl.BlockSpec((CHUNK, w), lambda t: (nc - 1 - t, 0))
    pshapes = [jax.ShapeDtypeStruct((4, D_CONV), f32), jax.ShapeDtypeStruct((1, D_CONV), f32),
               jax.ShapeDtypeStruct((1, 128), f32), jax.ShapeDtypeStruct((1, 128), f32),
               jax.ShapeDtypeStruct((1, 128), f32), jax.ShapeDtypeStruct((1, D_SSD), f32)]
    return pl.pallas_call(
        body, name="ssd_bwd", grid=(nc,),
        in_specs=[rev(D_CONV),
                  pl.BlockSpec((HALO, D_CONV), lambda t: (jnp.maximum((nc - 1 - t) * per - 1, 0), 0)),
                  rev(D_SSD), rev(128),
                  pl.BlockSpec((None, N_HEADS, SSD_HEAD_DIM, SSD_STATE), lambda t: (nc - 1 - t, 0, 0, 0)),
                  rev(D_SSD)] + _ssd_param_specs(),
        out_specs=[rev(D_CONV), rev(D_SSD), rev(128)] + _ssd_param_specs(),
        out_shape=[jax.ShapeDtypeStruct((s, D_CONV), f32), jax.ShapeDtypeStruct((s, D_SSD), f32),
                   jax.ShapeDtypeStruct((s, 128), f32)] + pshapes,
        scratch_shapes=[pltpu.VMEM((N_HEADS, SSD_HEAD_DIM, SSD_STATE), f32), pltpu.VMEM((HALO, D_CONV), f32)],
    )(px, px, pz, plast, states, dyg, *params)


def qkv_bwd(pa, plast, cos_t, sin_t, params, dq, dk, dv):
    s = pa.shape[0]
    ts = _token_block(s)

    def body(pa_ref, pl_ref, cos_ref, sin_ref, *rest):
        qaw, kvaw, wq, wk, wv, qnw, knw, kpw = [r[...] for r in rest[:8]]
        dq_ref, dk_ref, dv_ref = rest[8:11]
        dpa_ref, dpl_ref = rest[11:13]
        dprm_refs = list(rest[13:])
        cos_t, sin_t = cos_ref[...], sin_ref[...]

        def stage(pa_, pl_, qaw_, kvaw_, sq, sk, sv, qnw_, knw_, kpw_):
            return _f_qkv(pa_, pl_, cos_t, sin_t, qaw_, kvaw_, wq, wk, wv, qnw_, knw_, kpw_, (sq, sk, sv))

        _, vjp = jax.vjp(stage, pa_ref[...], pl_ref[...], qaw, kvaw, jnp.zeros(wq.shape, f32), jnp.zeros(wk.shape, f32),
                         jnp.zeros(wv.shape, f32), qnw, knw, kpw)
        grads = vjp((dq_ref[...], dk_ref[...], dv_ref[...]))
        dpa_ref[...] = grads[0]
        dpl_ref[...] = grads[1]
        _accumulate(pl.program_id(0) == 0, dprm_refs, list(grads[2:]))

    tok = lambda w: pl.BlockSpec((ts, w), lambda i: (i, 0))
    head = lambda w: pl.BlockSpec((N_HEADS, ts, w), lambda i: (0, i, 0))
    pshapes = [jax.ShapeDtypeStruct((1, Q_RANK), f32), jax.ShapeDtypeStruct((1, KV_RANK), f32),
               jax.ShapeDtypeStruct((N_HEADS, Q_RANK, HEAD_LANES), f32), jax.ShapeDtypeStruct((N_HEADS, KV_RANK, HEAD_LANES), f32),
               jax.ShapeDtypeStruct((N_HEADS, KV_RANK, V_DIM), f32), jax.ShapeDtypeStruct((1, HEAD_LANES), f32),
               jax.ShapeDtypeStruct((1, HEAD_LANES), f32), jax.ShapeDtypeStruct((1, HEAD_LANES), f32)]
    return pl.pallas_call(
        body, name="qkv_bwd", grid=(s // ts,),
        in_specs=[tok(384), tok(128), tok(128), tok(128)] + _qkv_param_specs()
                 + [head(HEAD_LANES), head(HEAD_LANES), head(V_DIM)],
        out_specs=[tok(384), tok(128)] + _qkv_param_specs(),
        out_shape=[jax.ShapeDtypeStruct((s, 384), f32), jax.ShapeDtypeStruct((s, 128), f32)] + pshapes,
    )(pa, plast, cos_t, sin_t, *params, dq, dk, dv)


def proj_bwd(x, nw, sh, sc, w, dpa, dpz, dpx, dpl_k, dpl_dt, dres):
    s = x.shape[0]
    ts = _token_block(s)

    ni = s // ts

    def body(x_ref, nw_ref, sh_ref, sc_ref, w_ref, dpa_ref, dpz_ref, dpx_ref, dplk_ref, dpld_ref, dres_ref,
             dx_ref, dnw_ref, dsh_ref, dsc_ref, dw_ref, acc_sc):
        i = pl.program_id(0)
        g = jnp.concatenate([dpa_ref[...], dpz_ref[...], dpx_ref[...], dplk_ref[...] + dpld_ref[...]], axis=1)
        w = w_ref[...]
        _, vjp = jax.vjp(lambda x_, nw_, sh_, sc_, slot: _f_proj(x_, nw_, sh_, sc_, w, slot), x_ref[...], nw_ref[...],
                         sh_ref[...], sc_ref[...], jnp.zeros(w.shape, f32))
        dx, dnw, dsh, dsc, dw = vjp(g)
        dx_ref[...] = dx + dres_ref[...]
        _accumulate(i == 0, [dnw_ref, dsh_ref, dsc_ref], [dnw, dsh, dsc])
        _accumulate_then_cast(i == 0, i == ni - 1, [acc_sc], [dw_ref], [dw])

    vec = _const((1, D_MODEL))
    vshape = jax.ShapeDtypeStruct((1, D_MODEL), f32)
    tok = lambda w_: pl.BlockSpec((ts, w_), lambda i: (i, 0))
    return pl.pallas_call(
        body, name="proj_bwd", grid=(ni,), scratch_shapes=[pltpu.VMEM((D_MODEL, D_PROJ), f32)],
        in_specs=[tok(D_MODEL), vec, vec, vec, _const((D_MODEL, D_PROJ)), tok(384), tok(512), tok(1024), tok(128), tok(128),
                  tok(D_MODEL)],
        out_specs=[tok(D_MODEL), vec, vec, vec, _const((D_MODEL, D_PROJ))],
        out_shape=[jax.ShapeDtypeStruct((s, D_MODEL), f32), vshape, vshape, vshape,
                   jax.ShapeDtypeStruct((D_MODEL, D_PROJ), bf16)],
    )(x, nw, sh, sc, w, dpa, dpz, dpx, dpl_k, dpl_dt, dres)


def ada_fwd(c_all, w_ada, b_cols):
    def body(c_ref, w_ref, b_ref, out_ref):
        act = jax.nn.silu(c_ref[...])
        for l in range(2):
            out_ref[l] = jnp.dot(act, w_ref[l], precision=lax.Precision.HIGHEST, preferred_element_type=f32) + b_ref[l]

    return pl.pallas_call(body, name="ada_fwd", out_shape=jax.ShapeDtypeStruct((2, N_DEV, 768), f32))(c_all, w_ada, b_cols)


def ada_bwd(c_all, dmod_cols):
    def body(c_ref, d_ref, out_ref):
        out_ref[0] = lax.dot_general(jax.nn.silu(c_ref[...]), d_ref[0], (((0,), (0,)), ((), ())),
                                     precision=lax.Precision.HIGHEST, preferred_element_type=f32)

    return pl.pallas_call(
        body, name="ada_bwd", grid=(2,),
        in_specs=[_const((N_DEV, D_MODEL)), pl.BlockSpec((1, N_DEV, 768), lambda l: (l, 0, 0))],
        out_specs=pl.BlockSpec((1, D_MODEL, 768), lambda l: (l, 0, 0)),
        out_shape=jax.ShapeDtypeStruct((2, D_MODEL, 768), f32),
    )(c_all, dmod_cols)


def _adamw(w, g, m, v):
    m = ADAM_B1 * m + (1.0 - ADAM_B1) * g
    v = ADAM_B2 * v + (1.0 - ADAM_B2) * (g * g)
    m_hat = m / (1.0 - ADAM_B1 ** ADAM_STEP)
    v_hat = v / (1.0 - ADAM_B2 ** ADAM_STEP)
    delta = -ADAM_LR * (m_hat / (jnp.sqrt(v_hat) + ADAM_EPS) + ADAM_WD * w)
    return delta, m, v


def adamw(parts, w, m, v, layer, prev, name):
    n, r, c = parts.shape
    nl = w.shape[0]
    tr = r
    lanes = -(-c // 128) * 128
    if 2 * (n + 7) * r * lanes * 4 > ADAMW_BLOCK_BYTES:
        tr = next(t for t in (256, 128, 64, 32, 16, 8) if r % t == 0)

    def body(p_ref, w_ref, m_ref, v_ref, *rest):
        g_ref, d_ref, nm_ref, nv_ref = rest[-4:]
        g = p_ref[0].astype(f32)
        for k in range(1, n):
            g = g + p_ref[k].astype(f32)
        delta, nm, nv = _adamw(w_ref[...], g, m_ref[...], v_ref[...])
        g_ref[...] = g
        d_ref[...] = delta
        nm_ref[...] = nm
        nv_ref[...] = nv

    blk = pl.BlockSpec((None, tr, c), lambda i: (layer, i, 0))
    shp = jax.ShapeDtypeStruct((nl, r, c), f32)
    kept = [] if prev is None else list(prev)
    return pl.pallas_call(
        body, name=name, grid=(r // tr,),
        in_specs=[pl.BlockSpec((n, tr, c), lambda i: (0, i, 0)), blk, blk, blk] + [ANY] * len(kept),
        out_specs=[blk] * 4, out_shape=[shp] * 4,
        input_output_aliases={4 + j: j for j in range(len(kept))},
    )(parts, w, m, v, *kept)


def _my_index():
    return 4 * lax.axis_index("x") + 2 * lax.axis_index("y") + lax.axis_index("c")


def _coords(idx):
    return (idx // 4, (idx // 2) % 2, idx % 2)


class CommJob:
    def __init__(self, operands, out_shape, phases, scratch):
        self.operands, self.out_shape, self.phases, self.scratch = operands, out_shape, phases, scratch


def _wait(out, n_blocks, send_sem, recv_sem, send=True, recv=True):
    span = out.at[pl.ds(0, n_blocks)]
    desc = pltpu.make_async_remote_copy(src_ref=span, dst_ref=span, send_sem=send_sem, recv_sem=recv_sem,
                                        device_id=_coords(_my_index()), device_id_type=MESH)
    if recv:
        desc.wait_recv()
    if send:
        desc.wait_send()


def gather_job(shards):
    n = len(shards)

    def places():
        x, y, c = lax.axis_index("x"), lax.axis_index("y"), lax.axis_index("c")
        return (x, y, c), (x, y, 1 - c), [(1 - x, y), (x, 1 - y), (1 - x, 1 - y)]

    def index(p):
        return 4 * p[0] + 2 * p[1] + p[2]

    def start(ins, outs, sems):
        far_send, far_recv, near_send, near_recv, local = sems
        me, sibling, chips = places()
        for k in range(n):
            pltpu.make_async_copy(ins[k], outs[k].at[index(me)], local.at[k]).start()
            for chip in chips:
                pltpu.make_async_remote_copy(src_ref=ins[k], dst_ref=outs[k].at[index(me)], send_sem=far_send.at[k],
                                             recv_sem=far_recv.at[k], device_id=(*chip, me[2]), device_id_type=MESH).start()
            pltpu.make_async_remote_copy(src_ref=ins[k], dst_ref=outs[k].at[index(me)], send_sem=near_send.at[k],
                                         recv_sem=near_recv.at[k], device_id=sibling, device_id_type=MESH).start()

    def relay(ins, outs, sems):
        far_send, far_recv, near_send, near_recv, local = sems
        me, sibling, chips = places()
        for k in range(n):
            _wait(outs[k], 3, far_send.at[k], far_recv.at[k], send=False)
            for chip in chips:
                block = outs[k].at[index((*chip, me[2]))]
                pltpu.make_async_remote_copy(src_ref=block, dst_ref=block, send_sem=near_send.at[k],
                                             recv_sem=near_recv.at[k], device_id=sibling, device_id_type=MESH).start()

    def finish(ins, outs, sems):
        far_send, far_recv, near_send, near_recv, local = sems
        for k in range(n):
            _wait(outs[k], 4, near_send.at[k], near_recv.at[k])
            _wait(outs[k], 3, far_send.at[k], far_recv.at[k], recv=False)
            pltpu.make_async_copy(ins[k], outs[k].at[0], local.at[k]).wait()

    shapes = [jax.ShapeDtypeStruct((N_DEV,) + tuple(a.shape), a.dtype) for a in shards]
    return CommJob(list(shards), shapes, [start, relay, finish], [pltpu.SemaphoreType.DMA((n,))] * 5)


def scatter_job(tensors):
    n = len(tensors)
    flat, where = [], {}
    for k, pieces in enumerate(tensors):
        d = 0
        for piece in pieces:
            for b in range(piece.shape[0]):
                where[k, d] = (len(flat), b)
                d += 1
            flat.append(piece)
        assert d == N_DEV

    def start(ins, outs, sems):
        send_sems, recv_sems, local_sems = sems
        me = _my_index()

        def block(k, d):
            i, b = where[k, d]
            return ins[i].at[b]

        for d in range(N_DEV):
            @pl.when(d != me)
            def _():
                for k in range(n):
                    pltpu.make_async_remote_copy(src_ref=block(k, d), dst_ref=outs[k].at[me], send_sem=send_sems.at[k],
                                                 recv_sem=recv_sems.at[k], device_id=(d // 4, (d // 2) % 2, d % 2),
                                                 device_id_type=MESH).start()

            @pl.when(d == me)
            def _():
                for k in range(n):
                    pltpu.make_async_copy(block(k, d), outs[k].at[d], local_sems.at[k]).start()

    def finish(ins, outs, sems):
        send_sems, recv_sems, local_sems = sems
        for k in range(n):
            _wait(outs[k], N_DEV - 1, send_sems.at[k], recv_sems.at[k])
            i, b = where[k, 0]
            pltpu.make_async_copy(ins[i].at[b], outs[k].at[0], local_sems.at[k]).wait()

    shapes = [jax.ShapeDtypeStruct((N_DEV,) + tuple(p[0].shape[1:]), p[0].dtype) for p in tensors]
    return CommJob(flat, shapes, [start, finish], [pltpu.SemaphoreType.DMA((n,))] * 3)


def comm_call(job, name):
    ni, no = len(job.operands), len(job.out_shape)

    def body(*refs):
        ins, outs, sems = refs[:ni], refs[ni:ni + no], refs[ni + no:]
        for phase in job.phases:
            phase(ins, outs, sems)

    return pl.pallas_call(body, name=name, in_specs=[ANY] * ni, out_specs=[ANY] * no, out_shape=job.out_shape,
                          scratch_shapes=job.scratch)(*job.operands)


def _carry(job, body, n_in, n_out, at_step):
    ji, jo, js = len(job.operands), len(job.out_shape), len(job.scratch)

    def carrier(*refs):
        a, b = n_in, n_in + ji
        c, d = b + n_out, b + n_out + jo
        e = len(refs) - js
        job_refs = (refs[a:b], refs[c:d], refs[e:])
        n = len(job.phases)

        @pl.when(at_step(0, n))
        def _():
            job.phases[0](*job_refs)

        body(*refs[:a], *refs[b:c], *refs[d:e])

        for i in range(1, n):
            @pl.when(at_step(i, n))
            def _():
                job.phases[i](*job_refs)

    return carrier


def _pad_lanes(v, lo, total=128):
    return jnp.pad(v, (lo, total - lo - v.shape[0]))[None, :]


MIXER_WEIGHTS = ("w_in", "w_q_up", "w_kv_up", "conv_w")
LATE_WEIGHTS = ("w_out", "w_gate_up", "w_down")


def mixer_operands(g, sw):
    w_in = g["w_in"].transpose(1, 0, 2).reshape(D_MODEL, D_IN)
    z = jnp.zeros((D_MODEL, 1), w_in.dtype)
    w_proj = jnp.concatenate(
        [w_in[:, :384], w_in[:, 416:928], w_in[:, 928:1952], w_in[:, 1952:1960], jnp.tile(z, (1, 56)),
         w_in[:, 384:416], jnp.tile(z, (1, 32))], axis=1)
    wq = jnp.pad(g["w_q_up"], ((0, 0), (0, 0), (0, HEAD_LANES - NOPE - ROPE)))
    wk = jnp.pad(g["w_kv_up"][:, :, :NOPE], ((0, 0), (0, 0), (0, HEAD_LANES - NOPE)))
    wv = g["w_kv_up"][:, :, NOPE:]
    qkv = (sw["q_a_norm_w"][None, :], sw["kv_a_norm_w"][None, :], wq, wk, wv,
           _pad_lanes(jnp.concatenate([sw["q_nope_norm_w"], sw["q_pe_norm_w"]]), 0),
           _pad_lanes(sw["k_nope_norm_w"], 0), _pad_lanes(sw["k_pe_norm_w"], NOPE))
    conv_w = g["conv_w"].astype(f32).transpose(1, 0, 2).reshape(4, D_CONV)
    ssd = (conv_w, sw["conv_b"][None, :], _pad_lanes(sw["dt_bias"], 0), _pad_lanes(sw["a_log"], 0),
           _pad_lanes(sw["d_skip"], 0), sw["ssd_norm_w"][None, :])
    return dict(w_proj=w_proj, qkv=qkv, ssd=ssd, n1=sw["norm1_w"][None, :])


def late_operands(g, sw):
    return dict(wo=g["w_out"].reshape(D_MODEL, D_MODEL), wgu=g["w_gate_up"],
                wd=g["w_down"].reshape(N_DEV // 2, FF_SHARD, D_MODEL), n2=sw["norm2_w"][None, :])


def layer_fwd(x, mod, kw, cos_t, sin_t, job=None, late=None):
    sh1, sc1, g1, sh2, sc2, g2 = [mod[i:i + 1] for i in range(6)]
    pa, pz, px, plast = proj_fwd(x, kw["n1"], sh1, sc1, kw["w_proj"])
    q, k, v = qkv_fwd(pa, plast, cos_t, sin_t, kw["qkv"])
    (o, lse), carried = attn_fwd(q, k, v, job)
    if late is not None:
        kw = {**kw, **late(carried)}
    yg, states = ssd_fwd(px, pz, plast, kw["ssd"])
    x_mid = out_fwd(x, o, yg, g1, kw["wo"])
    x_out, mix = mlp_fwd(x_mid, kw["n2"], sh2, sc2, g2, kw["wgu"], kw["wd"])
    saved = dict(x=x, pa=pa, pz=pz, px=px, plast=plast, q=q, k=k, v=v, o=o, lse=lse, yg=yg, states=states, x_mid=x_mid,
                 mix=mix)
    return x_out, saved, kw, carried


def layer_bwd_head(dy, mod, kw, sv, job=None):
    _, _, g1, sh2, sc2, g2 = [mod[i:i + 1] for i in range(6)]
    (dparts, dn2, dsh2, dsc2, dg2, dwg, dwu, dwd), carried = mlp_bwd(
        sv["x_mid"], dy, sv["mix"], kw["n2"], sh2, sc2, g2, kw["wgu"], kw["wd"], job)
    dmid, do, delta, dyg, dg1, dwo = out_bwd(dy, dparts, sv["o"], sv["yg"], g1, kw["wo"])
    dq = attn_bwd_dq(sv["q"], sv["k"], sv["v"], do, sv["lse"], delta)
    early = dict(w_out=[dwo.reshape(N_DEV, D_MODEL // N_DEV, D_MODEL)], w_gate_up=[dwg, dwu],
                 w_down=[dwd.reshape(N_DEV, D_FF // N_DEV, D_MODEL)])
    head = dict(dmid=dmid, do=do, delta=delta, dyg=dyg, dq=dq, dn2=dn2, dsh2=dsh2, dsc2=dsc2, dg2=dg2, dg1=dg1)
    return head, early, carried


def layer_bwd_tail(hd, mod, kw, cos_t, sin_t, sv, job=None):
    sh1, sc1 = mod[0:1], mod[1:2]
    (dk, dv), carried = attn_bwd_dkv(sv["q"], sv["k"], sv["v"], hd["do"], sv["lse"], hd["delta"], job)
    dpx, dpz, dpl_dt, dcw, dcb, ddtb, dalog, ddskip, dsnw = ssd_bwd(sv["px"], sv["pz"], sv["plast"], sv["states"],
                                                                   hd["dyg"], kw["ssd"])
    dpa, dpl_k, dqaw, dkvaw, dwq, dwk, dwv, dqnw, dknw, dkpw = qkv_bwd(sv["pa"], sv["plast"], cos_t, sin_t, kw["qkv"],
                                                                       hd["dq"], dk, dv)
    dx, dn1, dsh1, dsc1, dwp = proj_bwd(sv["x"], kw["n1"], sh1, sc1, kw["w_proj"], dpa, dpz, dpx, dpl_k, dpl_dt, hd["dmid"])
    dmod = jnp.concatenate([dsh1, dsc1, hd["dg1"], hd["dsh2"], hd["dsc2"], hd["dg2"]], axis=0)
    dw_in = jnp.concatenate([dwp[:, :384], dwp[:, 1984:2016], dwp[:, 384:1920], dwp[:, 1920:1928]], axis=1)
    grads = dict(
        norm1_w=dn1[0], norm2_w=hd["dn2"][0], q_a_norm_w=dqaw[0], kv_a_norm_w=dkvaw[0],
        q_nope_norm_w=dqnw[0, :NOPE], q_pe_norm_w=dqnw[0, NOPE:NOPE + ROPE], k_nope_norm_w=dknw[0, :NOPE],
        k_pe_norm_w=dkpw[0, NOPE:NOPE + ROPE], conv_b=dcb[0], dt_bias=ddtb[0, :N_HEADS], a_log=dalog[0, :N_HEADS],
        d_skip=ddskip[0, :N_HEADS], ssd_norm_w=dsnw[0],
        w_in=[dw_in.reshape(D_MODEL, N_DEV, D_IN // N_DEV).transpose(1, 0, 2)],
        w_q_up=[dwq[:, :, :NOPE + ROPE].astype(bf16)],
        w_kv_up=[jnp.concatenate([dwk[:, :, :NOPE], dwv], axis=2).astype(bf16)],
        conv_w=[dcw.reshape(4, N_DEV, D_CONV // N_DEV).transpose(1, 0, 2).astype(bf16)],
    )
    return dx, dmod, grads, carried


def _pack_small(get, last=None):
    flat = jnp.concatenate([get(name).reshape(-1) for name, _ in SMALL])
    flat = jnp.pad(flat, (0, SMALL_ROWS * 128 - flat.shape[0]))
    if last is not None:
        flat = flat.at[-1].set(last)
    return flat.reshape(SMALL_ROWS, 128)


def _unpack_small(packed):
    flat = packed.reshape(-1)
    out, off = {}, 0
    for name, size in SMALL:
        out[name] = flat[off:off + 2 * size].reshape(2, size)
        off += 2 * size
    return out


def kernel(x, c, positions, norm1_w, norm2_w, w_ada, b_ada, w_in, q_a_norm_w, w_q_up, kv_a_norm_w, w_kv_up, q_nope_norm_w, q_pe_norm_w, k_nope_norm_w, k_pe_norm_w, conv_w, conv_b, dt_bias, a_log, d_skip, ssd_norm_w, w_out, w_gate_up, w_down, loss_target, m_norm1_w, m_norm2_w, m_w_ada, m_b_ada, m_w_in, m_q_a_norm_w, m_w_q_up, m_kv_a_norm_w, m_w_kv_up, m_q_nope_norm_w, m_q_pe_norm_w, m_k_nope_norm_w, m_k_pe_norm_w, m_conv_w, m_conv_b, m_dt_bias, m_a_log, m_d_skip, m_ssd_norm_w, m_w_out, m_w_gate_up, m_w_down, v_norm1_w, v_norm2_w, v_w_ada, v_b_ada, v_w_in, v_q_a_norm_w, v_w_q_up, v_kv_a_norm_w, v_w_kv_up, v_q_nope_norm_w, v_q_pe_norm_w, v_k_nope_norm_w, v_k_pe_norm_w, v_conv_w, v_conv_b, v_dt_bias, v_a_log, v_d_skip, v_ssd_norm_w, v_w_out, v_w_gate_up, v_w_down):
    w = dict(norm1_w=norm1_w, norm2_w=norm2_w, w_ada=w_ada, b_ada=b_ada, w_in=w_in, q_a_norm_w=q_a_norm_w, w_q_up=w_q_up,
             kv_a_norm_w=kv_a_norm_w, w_kv_up=w_kv_up, q_nope_norm_w=q_nope_norm_w, q_pe_norm_w=q_pe_norm_w,
             k_nope_norm_w=k_nope_norm_w, k_pe_norm_w=k_pe_norm_w, conv_w=conv_w, conv_b=conv_b, dt_bias=dt_bias,
             a_log=a_log, d_skip=d_skip, ssd_norm_w=ssd_norm_w, w_out=w_out, w_gate_up=w_gate_up, w_down=w_down)
    m = dict(norm1_w=m_norm1_w, norm2_w=m_norm2_w, w_ada=m_w_ada, b_ada=m_b_ada, w_in=m_w_in, q_a_norm_w=m_q_a_norm_w,
             w_q_up=m_w_q_up, kv_a_norm_w=m_kv_a_norm_w, w_kv_up=m_w_kv_up, q_nope_norm_w=m_q_nope_norm_w,
             q_pe_norm_w=m_q_pe_norm_w, k_nope_norm_w=m_k_nope_norm_w, k_pe_norm_w=m_k_pe_norm_w, conv_w=m_conv_w,
             conv_b=m_conv_b, dt_bias=m_dt_bias, a_log=m_a_log, d_skip=m_d_skip, ssd_norm_w=m_ssd_norm_w, w_out=m_w_out,
             w_gate_up=m_w_gate_up, w_down=m_w_down)
    v = dict(norm1_w=v_norm1_w, norm2_w=v_norm2_w, w_ada=v_w_ada, b_ada=v_b_ada, w_in=v_w_in, q_a_norm_w=v_q_a_norm_w,
             w_q_up=v_w_q_up, kv_a_norm_w=v_kv_a_norm_w, w_kv_up=v_w_kv_up, q_nope_norm_w=v_q_nope_norm_w,
             q_pe_norm_w=v_q_pe_norm_w, k_nope_norm_w=v_k_nope_norm_w, k_pe_norm_w=v_k_pe_norm_w, conv_w=v_conv_w,
             conv_b=v_conv_b, dt_bias=v_dt_bias, a_log=v_a_log, d_skip=v_d_skip, ssd_norm_w=v_ssd_norm_w, w_out=v_w_out,
             w_gate_up=v_w_gate_up, w_down=v_w_down)
    me = _my_index()
    seq = x.shape[1]

    def shards(names, l):
        return [w[name][l] if name == "conv_w" else w[name][l].astype(bf16) for name in names]

    small = [{name: w[name][l] for name, _ in SMALL if name != "b_ada"} for l in range(2)]
    n_mix, n_late = len(MIXER_WEIGHTS), len(LATE_WEIGHTS)

    first = comm_call(gather_job([c] + shards(MIXER_WEIGHTS, 0)), "gather_first")
    c_all = first[0].reshape(N_DEV, D_MODEL)
    kws = [mixer_operands(dict(zip(MIXER_WEIGHTS, first[1:])), small[0]), None]
    rest_job = gather_job(shards(LATE_WEIGHTS, 0) + shards(MIXER_WEIGHTS, 1) + shards(LATE_WEIGHTS, 1))

    b_cols = lax.dynamic_slice_in_dim(b_ada, me * 768, 768, axis=1)
    mod_cols = ada_fwd(c_all, w_ada, b_cols)
    (mod_all,) = comm_call(gather_job([mod_cols]), "gather_mod")
    mod_me = lax.dynamic_index_in_dim(mod_all, me, axis=2, keepdims=False)
    mods = [mod_me[:, l, :].reshape(6, D_MODEL) for l in range(2)]

    inv_freq = 1.0 / (ROPE_THETA ** (jnp.arange(0, ROPE, 2, dtype=f32) / ROPE))
    inv = _pad_lanes(jnp.concatenate([inv_freq, inv_freq]), NOPE)
    cos_t, sin_t = rope_tables(positions.reshape(seq, 1), inv)

    saved = [None, None]
    h, saved[0], kws[0], rest = layer_fwd(
        x[0], mods[0], kws[0], cos_t, sin_t, rest_job,
        lambda got: late_operands(dict(zip(LATE_WEIGHTS, got[:n_late])), small[0]))
    kws[1] = {**mixer_operands(dict(zip(MIXER_WEIGHTS, rest[n_late:n_late + n_mix])), small[1]),
              **late_operands(dict(zip(LATE_WEIGHTS, rest[n_late + n_mix:])), small[1])}
    h, saved[1], _, _ = layer_fwd(h, mods[1], kws[1], cos_t, sin_t)
    dy, loss_part = loss_fwd(h, loss_target[0])

    early, late = ("w_out", "w_gate_up", "w_down"), ("w_in", "w_q_up", "w_kv_up", "conv_w")
    parts = [{}, {}]
    head, pieces, _ = layer_bwd_head(dy, mods[1], kws[1], saved[1])
    dy, dmod1, grads1, got = layer_bwd_tail(head, mods[1], kws[1], cos_t, sin_t, saved[1], scatter_job([pieces[n] for n in early]))
    parts[1].update(zip(early, got))
    head, pieces, got = layer_bwd_head(dy, mods[0], kws[0], saved[0], scatter_job([grads1[n] for n in late]))
    parts[1].update(zip(late, got))
    dy, dmod0, grads0, got = layer_bwd_tail(head, mods[0], kws[0], cos_t, sin_t, saved[0], scatter_job([pieces[n] for n in early]))
    parts[0].update(zip(early, got))
    parts[0].update(zip(late, comm_call(scatter_job([grads0[n] for n in late]), "scatter_layer0_rest")))
    grad_x = dy[None]

    small_part = {name: jnp.stack([grads0[name], grads1[name]]) for name, _ in SMALL if name != "b_ada"}
    small_part["b_ada"] = jnp.stack([dmod0.reshape(-1), dmod1.reshape(-1)])
    (small_all,) = comm_call(gather_job([_pack_small(lambda n: small_part[n], loss_part[0, 0])]), "gather_small_grads")
    packed = adamw(small_all, _pack_small(lambda n: w[n])[None], _pack_small(lambda n: m[n])[None],
                   _pack_small(lambda n: v[n])[None], 0, None, "adamw_small")
    loss = packed[0][0, -1, -1]
    res = {}
    for key, arr in zip("gdmv", packed):
        for name, val in _unpack_small(arr[0]).items():
            res[key, name] = val

    off = 2 * (1024 + 1024)
    dmod_all = small_all.reshape(N_DEV, -1)[:, off:off + 2 * 6144].reshape(N_DEV, 2, 6144)
    dmod_cols = lax.dynamic_slice_in_dim(dmod_all, me * 768, 768, axis=2).transpose(1, 0, 2)
    g_ada = ada_bwd(c_all, dmod_cols)
    out = None
    for l in range(2):
        out = adamw(g_ada[l][None], w_ada, m_w_ada, v_w_ada, l, out, "adamw_w_ada")
    res.update(zip([(key, "w_ada") for key in "gdmv"], out))

    for name in BIG:
        out = None
        for l in range(2):
            out = adamw(parts[l][name], w[name], m[name], v[name], l, out, "adamw_" + name)
        res.update(zip([(key, name) for key in "gdmv"], out))

    return (loss, grad_x, *[res["g", n] for n in WEIGHTS], *[res["d", n] for n in WEIGHTS],
            *[res["m", n] for n in WEIGHTS], *[res["v", n] for n in WEIGHTS])
```

```python
import functools

import jax
import jax.numpy as jnp
from jax import lax
from jax.experimental import pallas as pl
from jax.experimental.pallas import tpu as pltpu

f32 = jnp.float32
bf16 = jnp.bfloat16

N_DEV = 8
D_MODEL = 1024
N_HEADS = 8
HEAD_LANES = 128
NOPE = 64
ROPE = 32
V_DIM = 64
Q_RANK = 256
KV_RANK = 128
D_SSD = 512
D_CONV = 1024
SSD_STATE = 128
SSD_HEAD_DIM = 64
CHUNK = 128
HALO = 8
D_FF = 2816
FF_SHARD = 704
D_IN = 1960
D_PROJ = 2048
EPS = 1e-6
LOG2E = 1.4426950408889634
LN2 = 0.6931471805599453
Q_SCALE = (NOPE + ROPE) ** -0.5 * LOG2E
ATTN_ROWS_FWD = 256
ROPE_THETA = 10000.0
NEG = -1e30

ADAM_LR = 0.001
ADAM_B1 = 0.9
ADAM_B2 = 0.999
ADAM_EPS = 1e-08
ADAM_WD = 0.01
ADAM_STEP = 10
ADAMW_BLOCK_BYTES = 24 << 20

MESH = pl.DeviceIdType.MESH
ANY = pl.BlockSpec(memory_space=pl.ANY)

SMALL = (("norm1_w", 1024), ("norm2_w", 1024), ("b_ada", 6144), ("q_a_norm_w", 256), ("kv_a_norm_w", 128),
         ("q_nope_norm_w", 64), ("q_pe_norm_w", 32), ("k_nope_norm_w", 64), ("k_pe_norm_w", 32),
         ("conv_b", 1024), ("dt_bias", 8), ("a_log", 8), ("d_skip", 8), ("ssd_norm_w", 512))
SMALL_ROWS = 168
BIG = ("w_in", "w_q_up", "w_kv_up", "conv_w", "w_out", "w_gate_up", "w_down")
WEIGHTS = ("norm1_w", "norm2_w", "w_ada", "b_ada", "w_in", "q_a_norm_w", "w_q_up", "kv_a_norm_w", "w_kv_up",
           "q_nope_norm_w", "q_pe_norm_w", "k_nope_norm_w", "k_pe_norm_w", "conv_w", "conv_b", "dt_bias",
           "a_log", "d_skip", "ssd_norm_w", "w_out", "w_gate_up", "w_down")


def _dot(a, b, ca, cb):
    return lax.dot_general(a.astype(bf16), b.astype(bf16), (((ca,), (cb,)), ((), ())), preferred_element_type=f32)


@jax.custom_vjp
def mm(a, b):
    return _dot(a, b, 1, 0)


def _mm_fwd(a, b):
    return _dot(a, b, 1, 0), (a, b)


def _mm_bwd(res, g):
    a, b = res
    return _dot(g, b, 1, 1).astype(a.dtype), _dot(a, g, 0, 0).astype(b.dtype)


mm.defvjp(_mm_fwd, _mm_bwd)


@jax.custom_vjp
def _mm_slot(a, w, slot):
    return _dot(a, w, 1, 0)


def _mm_slot_fwd(a, w, slot):
    return _dot(a, w, 1, 0), (a, w)


def _mm_slot_bwd(res, g):
    a, w = res
    return _dot(g, w, 1, 1).astype(a.dtype), None, _dot(a, g, 0, 0)


_mm_slot.defvjp(_mm_slot_fwd, _mm_slot_bwd)


def mmw(a, w, slot=None):
    return _dot(a, w, 1, 0) if slot is None else _mm_slot(a, w, slot)


@jax.custom_vjp
def mm_nt(a, b):
    return _dot(a, b, 1, 1)


def _mm_nt_fwd(a, b):
    return _dot(a, b, 1, 1), (a, b)


def _mm_nt_bwd(res, g):
    a, b = res
    return _dot(g, b, 1, 0).astype(a.dtype), _dot(g, a, 0, 0).astype(b.dtype)


mm_nt.defvjp(_mm_nt_fwd, _mm_nt_bwd)


@jax.custom_vjp
def mm_tn(a, b):
    return _dot(a, b, 0, 0)


def _mm_tn_fwd(a, b):
    return _dot(a, b, 0, 0), (a, b)


def _mm_tn_bwd(res, g):
    a, b = res
    return _dot(b, g, 1, 1).astype(a.dtype), _dot(a, g, 1, 0).astype(b.dtype)


mm_tn.defvjp(_mm_tn_fwd, _mm_tn_bwd)


def _rms(x, w):
    return x * lax.rsqrt(jnp.mean(x * x, axis=-1, keepdims=True) + EPS) * w


def _const(shape):
    n = len(shape)
    return pl.BlockSpec(shape, lambda *_: (0,) * n)


def _accumulate(first, refs, vals):
    @pl.when(first)
    def _():
        for r, v in zip(refs, vals):
            r[...] = v

    @pl.when(jnp.logical_not(first))
    def _():
        for r, v in zip(refs, vals):
            r[...] += v


def _accumulate_then_cast(first, last, accs, outs, vals):
    _accumulate(first, accs, vals)

    @pl.when(last)
    def _():
        for a, o in zip(accs, outs):
            o[...] = a[...].astype(o.dtype)


def _token_block(s):
    return min(512, s)


def _f_proj(x, nw, sh, sc, w, slot=None):
    h = _rms(x, nw) * (1.0 + sc) + sh
    return mmw(h, w, slot)


def _f_qkv(pa, plast, cos_t, sin_t, qaw, kvaw, wq, wk, wv, qnw, knw, kpw, slots=None):
    sq, sk, sv = slots if slots is not None else ([None] * N_HEADS,) * 3
    lane = lax.broadcasted_iota(jnp.int32, (1, HEAD_LANES), 1)
    m_nope = lane < NOPE
    m_pe = (lane >= NOPE) & (lane < NOPE + ROPE)
    rows = pa.shape[0]

    def rope(t):
        half = ROPE // 2
        swapped = jnp.concatenate(
            [jnp.zeros((rows, NOPE), f32), t[:, NOPE + half:NOPE + ROPE], t[:, NOPE:NOPE + half],
             jnp.zeros((rows, HEAD_LANES - NOPE - ROPE), f32)], axis=1)
        return t * cos_t + swapped * sin_t

    qa = _rms(pa[:, :Q_RANK], qaw)
    kva = _rms(pa[:, Q_RANK:Q_RANK + KV_RANK], kvaw)
    kp = jnp.where(m_pe, plast, 0.0)
    kp = kp * lax.rsqrt(jnp.sum(kp * kp, axis=-1, keepdims=True) / ROPE + EPS) * kpw
    k_rot = rope(kp)
    qs, ks, vs = [], [], []
    for h in range(N_HEADS):
        qh = mmw(qa, wq[h], sq[h])
        ss_n = jnp.sum(jnp.where(m_nope, qh * qh, 0.0), axis=-1, keepdims=True) / NOPE
        ss_p = jnp.sum(jnp.where(m_pe, qh * qh, 0.0), axis=-1, keepdims=True) / ROPE
        r = jnp.where(m_nope, lax.rsqrt(ss_n + EPS), lax.rsqrt(ss_p + EPS))
        qs.append(rope(qh * r * qnw) * Q_SCALE)
        kh = mmw(kva, wk[h], sk[h])
        kh = kh * lax.rsqrt(jnp.sum(kh * kh, axis=-1, keepdims=True) / NOPE + EPS) * knw
        ks.append(kh + k_rot)
        vs.append(mmw(kva, wv[h], sv[h]))
    return jnp.stack(qs), jnp.stack(ks), jnp.stack(vs)


def _f_ssd(xext, z, plast, prev, cw, cb, dtb, alog, dskip, snw):
    n = CHUNK
    conv = cb
    for k in range(4):
        conv = conv + cw[k:k + 1] * xext[HALO - 3 + k:HALO - 3 + k + n]
    xc = jax.nn.silu(conv)
    xs, bm, cm = xc[:, :D_SSD], xc[:, D_SSD:D_SSD + 2 * SSD_STATE], xc[:, D_SSD + 2 * SSD_STATE:]
    lane = lax.broadcasted_iota(jnp.int32, (1, 128), 1)
    dt = jax.nn.softplus(jnp.where(lane < N_HEADS, plast, 0.0) + dtb)
    adt = dt * (-jnp.exp(alog))
    row = lax.broadcasted_iota(jnp.int32, (n, n), 0)
    col = lax.broadcasted_iota(jnp.int32, (n, n), 1)
    tri = row >= col
    acs = jnp.dot(tri.astype(f32), adt, precision=lax.Precision.HIGHEST, preferred_element_type=f32)
    acs_t = acs.T
    ys, news = [], []
    for g in range(2):
        bg = bm[:, g * SSD_STATE:(g + 1) * SSD_STATE]
        cg = cm[:, g * SSD_STATE:(g + 1) * SSD_STATE]
        cb_t = mm_nt(cg, bg)
        for r in range(4):
            h = g * 4 + r
            a_col = acs[:, h:h + 1]
            a_row = acs_t[h:h + 1, :]
            decay_ls = jnp.exp(jnp.where(tri, a_col - a_row, -jnp.inf))
            xh = xs[:, h * SSD_HEAD_DIM:(h + 1) * SSD_HEAD_DIM]
            xdt = xh * dt[:, h:h + 1]
            y_diag = mm(cb_t * decay_ls, xdt)
            a_last = acs[n - 1:n, h:h + 1]
            st = mm_tn(xdt * jnp.exp(a_last - a_col), bg)
            news.append(jnp.exp(a_last) * prev[h] + st)
            y_off = mm_nt(cg, prev[h]) * jnp.exp(a_col)
            ys.append(y_diag + y_off + dskip[:, h:h + 1] * xh)
    y = jnp.concatenate(ys, axis=1)
    yg = y * jax.nn.silu(z)
    half = D_SSD // 2
    outs = []
    for g in range(2):
        t = yg[:, g * half:(g + 1) * half]
        outs.append(t * lax.rsqrt(jnp.mean(t * t, axis=-1, keepdims=True) + EPS))
    return jnp.concatenate(outs, axis=1) * snw, jnp.stack(news)


def _f_out(o, yg, g1, wo, slot=None):
    cat = jnp.concatenate([o[h] for h in range(N_HEADS)] + [yg], axis=1)
    return g1 * mmw(cat, wo, slot)


def _f_gate_up(x, nw, sh, sc, wg, wu, slot_g=None, slot_u=None):
    h = _rms(x, nw) * (1.0 + sc) + sh
    return jax.nn.silu(mmw(h, wg, slot_g)) * mmw(h, wu, slot_u)


def proj_fwd(x, nw, sh, sc, w):
    s = x.shape[0]
    ts = _token_block(s)

    def body(x_ref, nw_ref, sh_ref, sc_ref, w_ref, pa_ref, pz_ref, px_ref, pl_ref):
        p = _f_proj(x_ref[...], nw_ref[...], sh_ref[...], sc_ref[...], w_ref[...])
        pa_ref[...] = p[:, :384]
        pz_ref[...] = p[:, 384:896]
        px_ref[...] = p[:, 896:1920]
        pl_ref[...] = p[:, 1920:]

    vec = _const((1, D_MODEL))
    return pl.pallas_call(
        body, name="proj_fwd", grid=(s // ts,),
        in_specs=[pl.BlockSpec((ts, D_MODEL), lambda i: (i, 0)), vec, vec, vec, _const((D_MODEL, D_PROJ))],
        out_specs=[pl.BlockSpec((ts, 384), lambda i: (i, 0)), pl.BlockSpec((ts, 512), lambda i: (i, 0)),
                   pl.BlockSpec((ts, 1024), lambda i: (i, 0)), pl.BlockSpec((ts, 128), lambda i: (i, 0))],
        out_shape=[jax.ShapeDtypeStruct((s, 384), f32), jax.ShapeDtypeStruct((s, 512), f32),
                   jax.ShapeDtypeStruct((s, 1024), f32), jax.ShapeDtypeStruct((s, 128), f32)],
    )(x, nw, sh, sc, w)


def rope_tables(pos, inv):
    s = pos.shape[0]
    ts = _token_block(s)

    def body(pos_ref, inv_ref, cos_ref, sin_ref):
        ang = pos_ref[...].astype(f32) * inv_ref[...]
        lane = lax.broadcasted_iota(jnp.int32, (1, HEAD_LANES), 1)
        half = ROPE // 2
        cos_ref[...] = jnp.where(lane < NOPE, 1.0, jnp.where(lane < NOPE + ROPE, jnp.cos(ang), 0.0))
        sn = jnp.sin(ang)
        sin_ref[...] = jnp.where((lane >= NOPE) & (lane < NOPE + half), -sn,
                                 jnp.where((lane >= NOPE + half) & (lane < NOPE + ROPE), sn, 0.0))

    return pl.pallas_call(
        body, name="rope_tables", grid=(s // ts,),
        in_specs=[pl.BlockSpec((ts, 1), lambda i: (i, 0)), _const((1, HEAD_LANES))],
        out_specs=[pl.BlockSpec((ts, HEAD_LANES), lambda i: (i, 0))] * 2,
        out_shape=[jax.ShapeDtypeStruct((s, HEAD_LANES), f32)] * 2,
    )(pos, inv)


def _qkv_param_specs():
    return [_const((1, Q_RANK)), _const((1, KV_RANK)), _const((N_HEADS, Q_RANK, HEAD_LANES)),
            _const((N_HEADS, KV_RANK, HEAD_LANES)), _const((N_HEADS, KV_RANK, V_DIM)),
            _const((1, HEAD_LANES)), _const((1, HEAD_LANES)), _const((1, HEAD_LANES))]


def qkv_fwd(pa, plast, cos_t, sin_t, params):
    s = pa.shape[0]
    ts = _token_block(s)

    def body(pa_ref, pl_ref, cos_ref, sin_ref, *rest):
        prm = [r[...] for r in rest[:8]]
        q_ref, k_ref, v_ref = rest[8:]
        q, k, v = _f_qkv(pa_ref[...], pl_ref[...], cos_ref[...], sin_ref[...], *prm)
        q_ref[...] = q.astype(bf16)
        k_ref[...] = k.astype(bf16)
        v_ref[...] = v.astype(bf16)

    tok = lambda w: pl.BlockSpec((ts, w), lambda i: (i, 0))
    return pl.pallas_call(
        body, name="qkv_fwd", grid=(s // ts,),
        in_specs=[tok(384), tok(128), tok(128), tok(128)] + _qkv_param_specs(),
        out_specs=[pl.BlockSpec((N_HEADS, ts, HEAD_LANES), lambda i: (0, i, 0)),
                   pl.BlockSpec((N_HEADS, ts, HEAD_LANES), lambda i: (0, i, 0)),
                   pl.BlockSpec((N_HEADS, ts, V_DIM), lambda i: (0, i, 0))],
        out_shape=[jax.ShapeDtypeStruct((N_HEADS, s, HEAD_LANES), bf16), jax.ShapeDtypeStruct((N_HEADS, s, HEAD_LANES), bf16),
                   jax.ShapeDtypeStruct((N_HEADS, s, V_DIM), bf16)],
    )(pa, plast, cos_t, sin_t, *params)


def _scores(q, k):
    return lax.dot_general(q, k, (((1,), (1,)), ((), ())), preferred_element_type=f32)


def _tril(rows, cols, row_offset):
    row = row_offset + lax.broadcasted_iota(jnp.int32, (rows, cols), 0)
    col = lax.broadcasted_iota(jnp.int32, (rows, cols), 1)
    return row >= col


def _call_with_job(body, name, grid, job, in_specs, out_specs, out_shape, scratch_shapes, operands):
    if job is None:
        res = pl.pallas_call(body, name=name, grid=grid, in_specs=in_specs, out_specs=out_specs, out_shape=out_shape,
                             scratch_shapes=scratch_shapes)(*operands)
        return res, None

    def at_step(i, n):
        if i == 0:
            want = [0] * len(grid)
        elif i == n - 1:
            want = [g - 1 for g in grid]
        else:
            want = [grid[0] - 1] + [0] * (len(grid) - 1)
        return functools.reduce(jnp.logical_and, [pl.program_id(a) == s for a, s in enumerate(want)])

    carrier = _carry(job, body, len(in_specs), len(out_specs), at_step)
    res = pl.pallas_call(
        carrier, name=name, grid=grid,
        in_specs=list(in_specs) + [ANY] * len(job.operands), out_specs=list(out_specs) + [ANY] * len(job.out_shape),
        out_shape=list(out_shape) + list(job.out_shape), scratch_shapes=list(scratch_shapes) + job.scratch,
    )(*operands, *job.operands)
    return res[:len(out_specs)], res[len(out_specs):]


def attn_fwd(q, k, v, job=None):
    s = q.shape[1]
    t = _token_block(s)
    nb = s // t

    rb = min(ATTN_ROWS_FWD, t)

    def body(q_ref, k_ref, v_ref, o_ref, lse_ref, m_sc, l_sc, acc_sc):
        qi = pl.program_id(1)
        m_sc[...] = jnp.full(m_sc.shape, NEG, f32)
        l_sc[...] = jnp.zeros(l_sc.shape, f32)
        acc_sc[...] = jnp.zeros(acc_sc.shape, f32)

        def step(k0, diagonal):
            for r in range(t // rb):
                rows = pl.ds(r * rb, rb)
                nk = (r + 1) * rb if diagonal else t
                keys = pl.ds(k0, nk)
                sc = _scores(q_ref[rows, :], k_ref[keys, :])
                if diagonal:
                    sc = jnp.where(_tril(rb, nk, r * rb), sc, NEG)
                m_prev = m_sc[rows, :1]
                m_new = jnp.maximum(m_prev, jnp.max(sc, axis=-1, keepdims=True))
                p = jnp.exp2(sc - m_new)
                alpha = jnp.exp2(m_prev - m_new)
                l_new = alpha * l_sc[rows, :1] + jnp.sum(p, axis=-1, keepdims=True)
                acc = alpha * acc_sc[rows, :] + jnp.dot(p.astype(bf16), v_ref[keys, :], preferred_element_type=f32)
                if diagonal:
                    o_ref[rows, :] = acc / l_new
                    lse_ref[rows, :] = jnp.broadcast_to(m_new + jnp.log2(l_new), (rb, 128))
                else:
                    acc_sc[rows, :] = acc
                    m_sc[rows, :] = jnp.broadcast_to(m_new, (rb, 128))
                    l_sc[rows, :] = jnp.broadcast_to(l_new, (rb, 128))

        def below(ki, carry):
            step(pl.multiple_of(ki * t, t), False)
            return carry

        lax.fori_loop(0, qi, below, 0)
        step(pl.multiple_of(qi * t, t), True)

    return _call_with_job(
        body, "attn_fwd" if job is None else "attn_fwd_comm", (N_HEADS, nb), job,
        in_specs=[pl.BlockSpec((None, t, HEAD_LANES), lambda h, qi: (h, qi, 0)),
                  pl.BlockSpec((None, s, HEAD_LANES), lambda h, qi: (h, 0, 0)),
                  pl.BlockSpec((None, s, V_DIM), lambda h, qi: (h, 0, 0))],
        out_specs=[pl.BlockSpec((None, t, V_DIM), lambda h, qi: (h, qi, 0)),
                   pl.BlockSpec((None, t, 128), lambda h, qi: (h, qi, 0))],
        out_shape=[jax.ShapeDtypeStruct((N_HEADS, s, V_DIM), f32), jax.ShapeDtypeStruct((N_HEADS, s, 128), f32)],
        scratch_shapes=[pltpu.VMEM((t, 128), f32), pltpu.VMEM((t, 128), f32), pltpu.VMEM((t, V_DIM), f32)],
        operands=(q, k, v))


def _ssd_param_specs():
    return [_const((4, D_CONV)), _const((1, D_CONV)), _const((1, 128)), _const((1, 128)), _const((1, 128)),
            _const((1, D_SSD))]


def ssd_fwd(px, pz, plast, params):
    s = px.shape[0]
    nc = s // CHUNK

    def body(px_ref, pz_ref, pl_ref, cw_ref, cb_ref, dtb_ref, alog_ref, dskip_ref, snw_ref, yg_ref, st_ref,
             state_sc, halo_sc):
        i = pl.program_id(0)

        @pl.when(i == 0)
        def _():
            state_sc[...] = jnp.zeros(state_sc.shape, f32)
            halo_sc[...] = jnp.zeros(halo_sc.shape, f32)

        x = px_ref[...]
        prev = state_sc[...]
        st_ref[...] = prev
        xext = jnp.concatenate([halo_sc[...], x], axis=0)
        yg, new = _f_ssd(xext, pz_ref[...], pl_ref[...], prev, cw_ref[...], cb_ref[...], dtb_ref[...],
                         alog_ref[...], dskip_ref[...], snw_ref[...])
        yg_ref[...] = yg
        state_sc[...] = new
        halo_sc[...] = x[CHUNK - HALO:]

    tok = lambda w: pl.BlockSpec((CHUNK, w), lambda i: (i, 0))
    return pl.pallas_call(
        body, name="ssd_fwd", grid=(nc,),
        in_specs=[tok(D_CONV), tok(D_SSD), tok(128)] + _ssd_param_specs(),
        out_specs=[tok(D_SSD), pl.BlockSpec((None, N_HEADS, SSD_HEAD_DIM, SSD_STATE), lambda i: (i, 0, 0, 0))],
        out_shape=[jax.ShapeDtypeStruct((s, D_SSD), f32),
                   jax.ShapeDtypeStruct((nc, N_HEADS, SSD_HEAD_DIM, SSD_STATE), f32)],
        scratch_shapes=[pltpu.VMEM((N_HEADS, SSD_HEAD_DIM, SSD_STATE), f32), pltpu.VMEM((HALO, D_CONV), f32)],
    )(px, pz, plast, *params)


def out_fwd(x, o, yg, g1, wo):
    s = x.shape[0]
    ts = _token_block(s)

    def body(x_ref, o_ref, yg_ref, g1_ref, wo_ref, out_ref):
        out_ref[...] = x_ref[...] + _f_out(o_ref[...], yg_ref[...], g1_ref[...], wo_ref[...])

    return pl.pallas_call(
        body, name="out_fwd", grid=(s // ts,),
        in_specs=[pl.BlockSpec((ts, D_MODEL), lambda i: (i, 0)), pl.BlockSpec((N_HEADS, ts, V_DIM), lambda i: (0, i, 0)),
                  pl.BlockSpec((ts, D_SSD), lambda i: (i, 0)), _const((1, D_MODEL)), _const((D_MODEL, D_MODEL))],
        out_specs=pl.BlockSpec((ts, D_MODEL), lambda i: (i, 0)),
        out_shape=jax.ShapeDtypeStruct((s, D_MODEL), f32),
    )(x, o, yg, g1, wo)


def mlp_fwd(x, nw, sh, sc, g2, wgu, wd):
    s = x.shape[0]
    ts = _token_block(s)
    nj = N_DEV // 2

    def body(x_ref, nw_ref, sh_ref, sc_ref, g2_ref, wg_ref, wu_ref, wd_ref, out_ref, mix_ref):
        j = pl.program_id(1)
        act = _f_gate_up(x_ref[...], nw_ref[...], sh_ref[...], sc_ref[...], wg_ref[...], wu_ref[...])
        _accumulate(j == 0, [mix_ref], [mmw(act, wd_ref[...])])

        @pl.when(j == nj - 1)
        def _():
            out_ref[...] = x_ref[...] + g2_ref[...] * mix_ref[...]

    vec = _const((1, D_MODEL))
    return pl.pallas_call(
        body, name="mlp_fwd", grid=(s // ts, nj),
        in_specs=[pl.BlockSpec((ts, D_MODEL), lambda i, j: (i, 0)), vec, vec, vec, vec,
                  pl.BlockSpec((None, D_MODEL, FF_SHARD), lambda i, j: (j, 0, 0)),
                  pl.BlockSpec((None, D_MODEL, FF_SHARD), lambda i, j: (j + nj, 0, 0)),
                  pl.BlockSpec((None, FF_SHARD, D_MODEL), lambda i, j: (j, 0, 0))],
        out_specs=[pl.BlockSpec((ts, D_MODEL), lambda i, j: (i, 0))] * 2,
        out_shape=[jax.ShapeDtypeStruct((s, D_MODEL), f32)] * 2,
    )(x, nw, sh, sc, g2, wgu, wgu, wd)


def loss_fwd(y, target):
    s = y.shape[0]
    ts = _token_block(s)

    def body(y_ref, t_ref, dy_ref, loss_ref):
        d = y_ref[...] - t_ref[...]
        dy_ref[...] = d * (1.0 / D_MODEL)
        part = 0.5 * jnp.sum(jnp.sum(d * d, axis=-1, keepdims=True) * (1.0 / D_MODEL), axis=0, keepdims=True)
        _accumulate(pl.program_id(0) == 0, [loss_ref], [jnp.broadcast_to(part, (8, 128))])

    return pl.pallas_call(
        body, name="loss_fwd", grid=(s // ts,),
        in_specs=[pl.BlockSpec((ts, D_MODEL), lambda i: (i, 0))] * 2,
        out_specs=[pl.BlockSpec((ts, D_MODEL), lambda i: (i, 0)), _const((8, 128))],
        out_shape=[jax.ShapeDtypeStruct((s, D_MODEL), f32), jax.ShapeDtypeStruct((8, 128), f32)],
    )(y, target)


def mlp_bwd(x, dy, mix, nw, sh, sc, g2, wgu, wd, job=None):
    s = x.shape[0]
    ts = min(256, s)
    nj = N_DEV // 2
    ni = s // ts

    def body(x_ref, dy_ref, mix_ref, nw_ref, sh_ref, sc_ref, g2_ref, wg_ref, wu_ref, wd_ref,
             dx_ref, dnw_ref, dsh_ref, dsc_ref, dg2_ref, dwg_ref, dwu_ref, dwd_ref, ag_sc, au_sc, ad_sc):
        j, i = pl.program_id(0), pl.program_id(1)
        wg, wu, wd = wg_ref[...], wu_ref[...], wd_ref[...]
        dy = dy_ref[...]
        act, vjp = jax.vjp(lambda x_, nw_, sh_, sc_, sg, su: _f_gate_up(x_, nw_, sh_, sc_, wg, wu, sg, su),
                           x_ref[...], nw_ref[...], sh_ref[...], sc_ref[...],
                           jnp.zeros(wg.shape, f32), jnp.zeros(wu.shape, f32))
        dmix = dy * g2_ref[...]
        dact = _dot(dmix, wd, 1, 1)
        dwd = _dot(act, dmix, 0, 0)
        dx, dnw, dsh, dsc, dwg, dwu = vjp(dact)
        dx_ref[...] = dx
        _accumulate((i == 0) & (j == 0), [dnw_ref, dsh_ref, dsc_ref], [dnw, dsh, dsc])
        _accumulate_then_cast(i == 0, i == ni - 1, [ag_sc, au_sc, ad_sc], [dwg_ref, dwu_ref, dwd_ref], [dwg, dwu, dwd])

        @pl.when(j == 0)
        def _():
            _accumulate(i == 0, [dg2_ref], [jnp.sum(dy * mix_ref[...], axis=0, keepdims=True)])

    vec = _const((1, D_MODEL))
    vshape = jax.ShapeDtypeStruct((1, D_MODEL), f32)
    wspec = lambda off: pl.BlockSpec((None, D_MODEL, FF_SHARD), lambda j, i: (j + off, 0, 0))
    dspec = pl.BlockSpec((None, FF_SHARD, D_MODEL), lambda j, i: (j, 0, 0))
    return _call_with_job(
        body, "mlp_bwd" if job is None else "mlp_bwd_comm", (nj, ni), job,
        in_specs=[pl.BlockSpec((ts, D_MODEL), lambda j, i: (i, 0)), pl.BlockSpec((ts, D_MODEL), lambda j, i: (i, 0)),
                  pl.BlockSpec((ts, D_MODEL), lambda j, i: (jnp.where(j == 0, i, 0), 0)),
                  vec, vec, vec, vec, wspec(0), wspec(nj), dspec],
        out_specs=[pl.BlockSpec((None, ts, D_MODEL), lambda j, i: (j, i, 0)), vec, vec, vec, vec,
                   wspec(0), wspec(0), dspec],
        out_shape=[jax.ShapeDtypeStruct((nj, s, D_MODEL), f32), vshape, vshape, vshape, vshape,
                   jax.ShapeDtypeStruct((nj, D_MODEL, FF_SHARD), bf16), jax.ShapeDtypeStruct((nj, D_MODEL, FF_SHARD), bf16),
                   jax.ShapeDtypeStruct((nj, FF_SHARD, D_MODEL), bf16)],
        scratch_shapes=[pltpu.VMEM((D_MODEL, FF_SHARD), f32), pltpu.VMEM((D_MODEL, FF_SHARD), f32),
                        pltpu.VMEM((FF_SHARD, D_MODEL), f32)],
        operands=(x, dy, mix, nw, sh, sc, g2, wgu, wgu, wd))


def out_bwd(dy, dparts, o, yg, g1, wo):
    s = dy.shape[0]
    ts = _token_block(s)
    nj = dparts.shape[0]

    ni = s // ts

    def body(dy_ref, dp_ref, o_ref, yg_ref, g1_ref, wo_ref, dx_ref, do_ref, delta_ref, dyg_ref, dg1_ref, dwo_ref, acc_sc):
        i = pl.program_id(0)
        g = dy_ref[...]
        for j in range(nj):
            g = g + dp_ref[j]
        dx_ref[...] = g
        o = o_ref[...]
        wo = wo_ref[...]
        _, vjp = jax.vjp(lambda o_, yg_, g1_, slot: _f_out(o_, yg_, g1_, wo, slot), o, yg_ref[...], g1_ref[...],
                         jnp.zeros(wo.shape, f32))
        do, dyg, dg1, dwo = vjp(g)
        do_ref[...] = do
        dyg_ref[...] = dyg
        delta_ref[...] = jnp.broadcast_to(jnp.sum(do * o, axis=-1, keepdims=True), delta_ref.shape)
        _accumulate(i == 0, [dg1_ref], [dg1])
        _accumulate_then_cast(i == 0, i == ni - 1, [acc_sc], [dwo_ref], [dwo])

    head = pl.BlockSpec((N_HEADS, ts, V_DIM), lambda i: (0, i, 0))
    return pl.pallas_call(
        body, name="out_bwd", grid=(ni,), scratch_shapes=[pltpu.VMEM((D_MODEL, D_MODEL), f32)],
        in_specs=[pl.BlockSpec((ts, D_MODEL), lambda i: (i, 0)), pl.BlockSpec((nj, ts, D_MODEL), lambda i: (0, i, 0)),
                  head, pl.BlockSpec((ts, D_SSD), lambda i: (i, 0)), _const((1, D_MODEL)), _const((D_MODEL, D_MODEL))],
        out_specs=[pl.BlockSpec((ts, D_MODEL), lambda i: (i, 0)), head,
                   pl.BlockSpec((N_HEADS, ts, 128), lambda i: (0, i, 0)), pl.BlockSpec((ts, D_SSD), lambda i: (i, 0)),
                   _const((1, D_MODEL)), _const((D_MODEL, D_MODEL))],
        out_shape=[jax.ShapeDtypeStruct((s, D_MODEL), f32), jax.ShapeDtypeStruct((N_HEADS, s, V_DIM), f32),
                   jax.ShapeDtypeStruct((N_HEADS, s, 128), f32), jax.ShapeDtypeStruct((s, D_SSD), f32),
                   jax.ShapeDtypeStruct((1, D_MODEL), f32), jax.ShapeDtypeStruct((D_MODEL, D_MODEL), bf16)],
    )(dy, dparts, o, yg, g1, wo)


def attn_bwd(q, k, v, do, lse, delta, job=None):
    s = q.shape[1]
    t = _token_block(s)
    nb = s // t

    def body(q_ref, k_ref, v_ref, do_ref, lse_ref, delta_ref, dq_ref, dk_ref, dv_ref):
        ki = pl.program_id(1)

        @pl.when(ki == 0)
        def _():
            dq_ref[...] = jnp.zeros(dq_ref.shape, f32)

        dk_ref[...] = jnp.zeros(dk_ref.shape, f32)
        dv_ref[...] = jnp.zeros(dv_ref.shape, f32)

        def step(q0, diagonal):
            rows = pl.ds(q0, t)
            q = q_ref[rows, :]
            k = k_ref[...]
            do = do_ref[rows, :].astype(bf16)
            sc = _scores(q, k)
            if diagonal:
                sc = jnp.where(_tril(t, t, 0), sc, NEG)
            p = jnp.exp2(sc - lse_ref[rows, :1])
            dp = lax.dot_general(do, v_ref[...], (((1,), (1,)), ((), ())), preferred_element_type=f32)
            ds = (p * (dp - delta_ref[rows, :1])).astype(bf16)
            dv_ref[...] += lax.dot_general(p.astype(bf16), do, (((0,), (0,)), ((), ())), preferred_element_type=f32)
            dk_ref[...] += lax.dot_general(ds, q, (((0,), (0,)), ((), ())), preferred_element_type=f32)
            dq_ref[rows, :] += jnp.dot(ds, k, preferred_element_type=f32)

        step(pl.multiple_of(ki * t, t), True)

        def above(qi, carry):
            step(pl.multiple_of(qi * t, t), False)
            return carry

        lax.fori_loop(ki + 1, nb, above, 0)
        dk_ref[...] = dk_ref[...] * LN2

        @pl.when(ki == nb - 1)
        def _():
            dq_ref[...] = dq_ref[...] * LN2

    qspec = lambda w: pl.BlockSpec((None, s, w), lambda h, ki: (h, 0, 0))
    kspec = lambda w: pl.BlockSpec((None, t, w), lambda h, ki: (h, ki, 0))
    return _call_with_job(
        body, "attn_bwd" if job is None else "attn_bwd_comm", (N_HEADS, nb), job,
        in_specs=[qspec(HEAD_LANES), kspec(HEAD_LANES), kspec(V_DIM), qspec(V_DIM), qspec(128), qspec(128)],
        out_specs=[qspec(HEAD_LANES), kspec(HEAD_LANES), kspec(V_DIM)],
        out_shape=[jax.ShapeDtypeStruct((N_HEADS, s, HEAD_LANES), f32), jax.ShapeDtypeStruct((N_HEADS, s, HEAD_LANES), f32),
                   jax.ShapeDtypeStruct((N_HEADS, s, V_DIM), f32)],
        scratch_shapes=[], operands=(q, k, v, do, lse, delta))


def ssd_bwd(px, pz, plast, states, dyg, params):
    s = px.shape[0]
    nc = s // CHUNK
    per = CHUNK // HALO

    def body(px_ref, halo_ref, pz_ref, pl_ref, st_ref, dyg_ref, cw_ref, cb_ref, dtb_ref, alog_ref, dskip_ref, snw_ref,
             dpx_ref, dpz_ref, dpl_ref, dcw_ref, dcb_ref, ddtb_ref, dalog_ref, ddskip_ref, dsnw_ref, dstate_sc, dhalo_sc):
        t = pl.program_id(0)
        chunk = nc - 1 - t

        @pl.when(t == 0)
        def _():
            dstate_sc[...] = jnp.zeros(dstate_sc.shape, f32)
            dhalo_sc[...] = jnp.zeros(dhalo_sc.shape, f32)

        halo = jnp.where(chunk > 0, halo_ref[...], 0.0)
        xext = jnp.concatenate([halo, px_ref[...]], axis=0)
        _, vjp = jax.vjp(_f_ssd, xext, pz_ref[...], pl_ref[...], st_ref[...], cw_ref[...], cb_ref[...], dtb_ref[...],
                         alog_ref[...], dskip_ref[...], snw_ref[...])
        dxext, dz, dpl, dprev, dcw, dcb, ddtb, dalog, ddskip, dsnw = vjp((dyg_ref[...], dstate_sc[...]))
        dpx_ref[...] = dxext[HALO:]
        dpx_ref[CHUNK - HALO:, :] += dhalo_sc[...]
        dhalo_sc[...] = dxext[:HALO]
        dstate_sc[...] = dprev
        dpz_ref[...] = dz
        dpl_ref[...] = dpl
        _accumulate(t == 0, [dcw_ref, dcb_ref, ddtb_ref, dalog_ref, ddskip_ref, dsnw_ref],
                    [dcw, dcb, ddtb, dalog, ddskip, dsnw])

    rev = lambda w: pl.BlockSpec((CHUNK, w), lambda t: (nc - 1 - t, 0))
    pshapes = [jax.ShapeDtypeStruct((4, D_CONV), f32), jax.ShapeDtypeStruct((1, D_CONV), f32),
               jax.ShapeDtypeStruct((1, 128), f32), jax.ShapeDtypeStruct((1, 128), f32),
               jax.ShapeDtypeStruct((1, 128), f32), jax.ShapeDtypeStruct((1, D_SSD), f32)]
    return pl.pallas_call(
        body, name="ssd_bwd", grid=(nc,),
        in_specs=[rev(D_CONV),
                  pl.BlockSpec((HALO, D_CONV), lambda t: (jnp.maximum((nc - 1 - t) * per - 1, 0), 0)),
                  rev(D_SSD), rev(128),
                  pl.BlockSpec((None, N_HEADS, SSD_HEAD_DIM, SSD_STATE), lambda t: (nc - 1 - t, 0, 0, 0)),
                  rev(D_SSD)] + _ssd_param_specs(),
        out_specs=[rev(D_CONV), rev(D_SSD), rev(128)] + _ssd_param_specs(),
        out_shape=[jax.ShapeDtypeStruct((s, D_CONV), f32), jax.ShapeDtypeStruct((s, D_SSD), f32),
                   jax.ShapeDtypeStruct((s, 128), f32)] + pshapes,
        scratch_shapes=[pltpu.VMEM((N_HEADS, SSD_HEAD_DIM, SSD_STATE), f32), pltpu.VMEM((HALO, D_CONV), f32)],
    )(px, px, pz, plast, states, dyg, *params)


def qkv_bwd(pa, plast, cos_t, sin_t, params, dq, dk, dv):
    s = pa.shape[0]
    ts = _token_block(s)

    def body(pa_ref, pl_ref, cos_ref, sin_ref, *rest):
        qaw, kvaw, wq, wk, wv, qnw, knw, kpw = [r[...] for r in rest[:8]]
        dq_ref, dk_ref, dv_ref = rest[8:11]
        dpa_ref, dpl_ref = rest[11:13]
        dprm_refs = list(rest[13:])
        cos_t, sin_t = cos_ref[...], sin_ref[...]

        def stage(pa_, pl_, qaw_, kvaw_, sq, sk, sv, qnw_, knw_, kpw_):
            return _f_qkv(pa_, pl_, cos_t, sin_t, qaw_, kvaw_, wq, wk, wv, qnw_, knw_, kpw_, (sq, sk, sv))

        _, vjp = jax.vjp(stage, pa_ref[...], pl_ref[...], qaw, kvaw, jnp.zeros(wq.shape, f32), jnp.zeros(wk.shape, f32),
                         jnp.zeros(wv.shape, f32), qnw, knw, kpw)
        grads = vjp((dq_ref[...], dk_ref[...], dv_ref[...]))
        dpa_ref[...] = grads[0]
        dpl_ref[...] = grads[1]
        _accumulate(pl.program_id(0) == 0, dprm_refs, list(grads[2:]))

    tok = lambda w: pl.BlockSpec((ts, w), lambda i: (i, 0))
    head = lambda w: pl.BlockSpec((N_HEADS, ts, w), lambda i: (0, i, 0))
    pshapes = [jax.ShapeDtypeStruct((1, Q_RANK), f32), jax.ShapeDtypeStruct((1, KV_RANK), f32),
               jax.ShapeDtypeStruct((N_HEADS, Q_RANK, HEAD_LANES), f32), jax.ShapeDtypeStruct((N_HEADS, KV_RANK, HEAD_LANES), f32),
               jax.ShapeDtypeStruct((N_HEADS, KV_RANK, V_DIM), f32), jax.ShapeDtypeStruct((1, HEAD_LANES), f32),
               jax.ShapeDtypeStruct((1, HEAD_LANES), f32), jax.ShapeDtypeStruct((1, HEAD_LANES), f32)]
    return pl.pallas_call(
        body, name="qkv_bwd", grid=(s // ts,),
        in_specs=[tok(384), tok(128), tok(128), tok(128)] + _qkv_param_specs()
                 + [head(HEAD_LANES), head(HEAD_LANES), head(V_DIM)],
        out_specs=[tok(384), tok(128)] + _qkv_param_specs(),
        out_shape=[jax.ShapeDtypeStruct((s, 384), f32), jax.ShapeDtypeStruct((s, 128), f32)] + pshapes,
    )(pa, plast, cos_t, sin_t, *params, dq, dk, dv)


def proj_bwd(x, nw, sh, sc, w, dpa, dpz, dpx, dpl_k, dpl_dt, dres):
    s = x.shape[0]
    ts = _token_block(s)

    ni = s // ts

    def body(x_ref, nw_ref, sh_ref, sc_ref, w_ref, dpa_ref, dpz_ref, dpx_ref, dplk_ref, dpld_ref, dres_ref,
             dx_ref, dnw_ref, dsh_ref, dsc_ref, dw_ref, acc_sc):
        i = pl.program_id(0)
        g = jnp.concatenate([dpa_ref[...], dpz_ref[...], dpx_ref[...], dplk_ref[...] + dpld_ref[...]], axis=1)
        w = w_ref[...]
        _, vjp = jax.vjp(lambda x_, nw_, sh_, sc_, slot: _f_proj(x_, nw_, sh_, sc_, w, slot), x_ref[...], nw_ref[...],
                         sh_ref[...], sc_ref[...], jnp.zeros(w.shape, f32))
        dx, dnw, dsh, dsc, dw = vjp(g)
        dx_ref[...] = dx + dres_ref[...]
        _accumulate(i == 0, [dnw_ref, dsh_ref, dsc_ref], [dnw, dsh, dsc])
        _accumulate_then_cast(i == 0, i == ni - 1, [acc_sc], [dw_ref], [dw])

    vec = _const((1, D_MODEL))
    vshape = jax.ShapeDtypeStruct((1, D_MODEL), f32)
    tok = lambda w_: pl.BlockSpec((ts, w_), lambda i: (i, 0))
    return pl.pallas_call(
        body, name="proj_bwd", grid=(ni,), scratch_shapes=[pltpu.VMEM((D_MODEL, D_PROJ), f32)],
        in_specs=[tok(D_MODEL), vec, vec, vec, _const((D_MODEL, D_PROJ)), tok(384), tok(512), tok(1024), tok(128), tok(128),
                  tok(D_MODEL)],
        out_specs=[tok(D_MODEL), vec, vec, vec, _const((D_MODEL, D_PROJ))],
        out_shape=[jax.ShapeDtypeStruct((s, D_MODEL), f32), vshape, vshape, vshape,
                   jax.ShapeDtypeStruct((D_MODEL, D_PROJ), bf16)],
    )(x, nw, sh, sc, w, dpa, dpz, dpx, dpl_k, dpl_dt, dres)


def ada_fwd(c_all, w_ada, b_cols):
    def body(c_ref, w_ref, b_ref, out_ref):
        act = jax.nn.silu(c_ref[...])
        for l in range(2):
            out_ref[l] = jnp.dot(act, w_ref[l], precision=lax.Precision.HIGHEST, preferred_element_type=f32) + b_ref[l]

    return pl.pallas_call(body, name="ada_fwd", out_shape=jax.ShapeDtypeStruct((2, N_DEV, 768), f32))(c_all, w_ada, b_cols)


def ada_bwd(c_all, dmod_cols):
    def body(c_ref, d_ref, out_ref):
        out_ref[0] = lax.dot_general(jax.nn.silu(c_ref[...]), d_ref[0], (((0,), (0,)), ((), ())),
                                     precision=lax.Precision.HIGHEST, preferred_element_type=f32)

    return pl.pallas_call(
        body, name="ada_bwd", grid=(2,),
        in_specs=[_const((N_DEV, D_MODEL)), pl.BlockSpec((1, N_DEV, 768), lambda l: (l, 0, 0))],
        out_specs=pl.BlockSpec((1, D_MODEL, 768), lambda l: (l, 0, 0)),
        out_shape=jax.ShapeDtypeStruct((2, D_MODEL, 768), f32),
    )(c_all, dmod_cols)


def _adamw(w, g, m, v):
    m = ADAM_B1 * m + (1.0 - ADAM_B1) * g
    v = ADAM_B2 * v + (1.0 - ADAM_B2) * (g * g)
    m_hat = m / (1.0 - ADAM_B1 ** ADAM_STEP)
    v_hat = v / (1.0 - ADAM_B2 ** ADAM_STEP)
    delta = -ADAM_LR * (m_hat / (jnp.sqrt(v_hat) + ADAM_EPS) + ADAM_WD * w)
    return delta, m, v


def adamw(parts, w, m, v, layer, prev, name):
    n, r, c = parts.shape
    nl = w.shape[0]
    tr = r
    lanes = -(-c // 128) * 128
    if 2 * (n + 7) * r * lanes * 4 > ADAMW_BLOCK_BYTES:
        tr = next(t for t in (256, 128, 64, 32, 16, 8) if r % t == 0)

    def body(p_ref, w_ref, m_ref, v_ref, *rest):
        g_ref, d_ref, nm_ref, nv_ref = rest[-4:]
        g = p_ref[0].astype(f32)
        for k in range(1, n):
            g = g + p_ref[k].astype(f32)
        delta, nm, nv = _adamw(w_ref[...], g, m_ref[...], v_ref[...])
        g_ref[...] = g
        d_ref[...] = delta
        nm_ref[...] = nm
        nv_ref[...] = nv

    blk = pl.BlockSpec((None, tr, c), lambda i: (layer, i, 0))
    shp = jax.ShapeDtypeStruct((nl, r, c), f32)
    kept = [] if prev is None else list(prev)
    return pl.pallas_call(
        body, name=name, grid=(r // tr,),
        in_specs=[pl.BlockSpec((n, tr, c), lambda i: (0, i, 0)), blk, blk, blk] + [ANY] * len(kept),
        out_specs=[blk] * 4, out_shape=[shp] * 4,
        input_output_aliases={4 + j: j for j in range(len(kept))},
    )(parts, w, m, v, *kept)


def _my_index():
    return 4 * lax.axis_index("x") + 2 * lax.axis_index("y") + lax.axis_index("c")


def _coords(idx):
    return (idx // 4, (idx // 2) % 2, idx % 2)


class CommJob:
    def __init__(self, operands, out_shape, phases, scratch):
        self.operands, self.out_shape, self.phases, self.scratch = operands, out_shape, phases, scratch


def _wait(out, n_blocks, send_sem, recv_sem, send=True, recv=True):
    span = out.at[pl.ds(0, n_blocks)]
    desc = pltpu.make_async_remote_copy(src_ref=span, dst_ref=span, send_sem=send_sem, recv_sem=recv_sem,
                                        device_id=_coords(_my_index()), device_id_type=MESH)
    if recv:
        desc.wait_recv()
    if send:
        desc.wait_send()


def gather_job(shards):
    n = len(shards)

    def places():
        x, y, c = lax.axis_index("x"), lax.axis_index("y"), lax.axis_index("c")
        return (x, y, c), (x, y, 1 - c), [(1 - x, y), (x, 1 - y), (1 - x, 1 - y)]

    def index(p):
        return 4 * p[0] + 2 * p[1] + p[2]

    def start(ins, outs, sems):
        far_send, far_recv, near_send, near_recv, local = sems
        me, sibling, chips = places()
        for k in range(n):
            pltpu.make_async_copy(ins[k], outs[k].at[index(me)], local.at[k]).start()
            for chip in chips:
                pltpu.make_async_remote_copy(src_ref=ins[k], dst_ref=outs[k].at[index(me)], send_sem=far_send.at[k],
                                             recv_sem=far_recv.at[k], device_id=(*chip, me[2]), device_id_type=MESH).start()
            pltpu.make_async_remote_copy(src_ref=ins[k], dst_ref=outs[k].at[index(me)], send_sem=near_send.at[k],
                                         recv_sem=near_recv.at[k], device_id=sibling, device_id_type=MESH).start()

    def relay(ins, outs, sems):
        far_send, far_recv, near_send, near_recv, local = sems
        me, sibling, chips = places()
        for k in range(n):
            _wait(outs[k], 3, far_send.at[k], far_recv.at[k], send=False)
            for chip in chips:
                block = outs[k].at[index((*chip, me[2]))]
                pltpu.make_async_remote_copy(src_ref=block, dst_ref=block, send_sem=near_send.at[k],
                                             recv_sem=near_recv.at[k], device_id=sibling, device_id_type=MESH).start()

    def finish(ins, outs, sems):
        far_send, far_recv, near_send, near_recv, local = sems
        for k in range(n):
            _wait(outs[k], 4, near_send.at[k], near_recv.at[k])
            _wait(outs[k], 3, far_send.at[k], far_recv.at[k], recv=False)
            pltpu.make_async_copy(ins[k], outs[k].at[0], local.at[k]).wait()

    shapes = [jax.ShapeDtypeStruct((N_DEV,) + tuple(a.shape), a.dtype) for a in shards]
    return CommJob(list(shards), shapes, [start, relay, finish], [pltpu.SemaphoreType.DMA((n,))] * 5)


def scatter_job(tensors):
    n = len(tensors)
    flat, where = [], {}
    for k, pieces in enumerate(tensors):
        d = 0
        for piece in pieces:
            for b in range(piece.shape[0]):
                where[k, d] = (len(flat), b)
                d += 1
            flat.append(piece)
        assert d == N_DEV

    def start(ins, outs, sems):
        send_sems, recv_sems, local_sems = sems
        me = _my_index()

        def block(k, d):
            i, b = where[k, d]
            return ins[i].at[b]

        for d in range(N_DEV):
            @pl.when(d != me)
            def _():
                for k in range(n):
                    pltpu.make_async_remote_copy(src_ref=block(k, d), dst_ref=outs[k].at[me], send_sem=send_sems.at[k],
                                                 recv_sem=recv_sems.at[k], device_id=(d // 4, (d // 2) % 2, d % 2),
                                                 device_id_type=MESH).start()

            @pl.when(d == me)
            def _():
                for k in range(n):
                    pltpu.make_async_copy(block(k, d), outs[k].at[d], local_sems.at[k]).start()

    def finish(ins, outs, sems):
        send_sems, recv_sems, local_sems = sems
        for k in range(n):
            _wait(outs[k], N_DEV - 1, send_sems.at[k], recv_sems.at[k])
            i, b = where[k, 0]
            pltpu.make_async_copy(ins[i].at[b], outs[k].at[0], local_sems.at[k]).wait()

    shapes = [jax.ShapeDtypeStruct((N_DEV,) + tuple(p[0].shape[1:]), p[0].dtype) for p in tensors]
    return CommJob(flat, shapes, [start, finish], [pltpu.SemaphoreType.DMA((n,))] * 3)


def comm_call(job, name):
    ni, no = len(job.operands), len(job.out_shape)

    def body(*refs):
        ins, outs, sems = refs[:ni], refs[ni:ni + no], refs[ni + no:]
        for phase in job.phases:
            phase(ins, outs, sems)

    return pl.pallas_call(body, name=name, in_specs=[ANY] * ni, out_specs=[ANY] * no, out_shape=job.out_shape,
                          scratch_shapes=job.scratch)(*job.operands)


def _carry(job, body, n_in, n_out, at_step):
    ji, jo, js = len(job.operands), len(job.out_shape), len(job.scratch)

    def carrier(*refs):
        a, b = n_in, n_in + ji
        c, d = b + n_out, b + n_out + jo
        e = len(refs) - js
        job_refs = (refs[a:b], refs[c:d], refs[e:])
        n = len(job.phases)

        @pl.when(at_step(0, n))
        def _():
            job.phases[0](*job_refs)

        body(*refs[:a], *refs[b:c], *refs[d:e])

        for i in range(1, n):
            @pl.when(at_step(i, n))
            def _():
                job.phases[i](*job_refs)

    return carrier


def _pad_lanes(v, lo, total=128):
    return jnp.pad(v, (lo, total - lo - v.shape[0]))[None, :]


MIXER_WEIGHTS = ("w_in", "w_q_up", "w_kv_up", "conv_w")
LATE_WEIGHTS = ("w_out", "w_gate_up", "w_down")


def mixer_operands(g, sw):
    w_in = g["w_in"].transpose(1, 0, 2).reshape(D_MODEL, D_IN)
    z = jnp.zeros((D_MODEL, 1), w_in.dtype)
    w_proj = jnp.concatenate(
        [w_in[:, :384], w_in[:, 416:928], w_in[:, 928:1952], w_in[:, 1952:1960], jnp.tile(z, (1, 56)),
         w_in[:, 384:416], jnp.tile(z, (1, 32))], axis=1)
    wq = jnp.pad(g["w_q_up"], ((0, 0), (0, 0), (0, HEAD_LANES - NOPE - ROPE)))
    wk = jnp.pad(g["w_kv_up"][:, :, :NOPE], ((0, 0), (0, 0), (0, HEAD_LANES - NOPE)))
    wv = g["w_kv_up"][:, :, NOPE:]
    qkv = (sw["q_a_norm_w"][None, :], sw["kv_a_norm_w"][None, :], wq, wk, wv,
           _pad_lanes(jnp.concatenate([sw["q_nope_norm_w"], sw["q_pe_norm_w"]]), 0),
           _pad_lanes(sw["k_nope_norm_w"], 0), _pad_lanes(sw["k_pe_norm_w"], NOPE))
    conv_w = g["conv_w"].astype(f32).transpose(1, 0, 2).reshape(4, D_CONV)
    ssd = (conv_w, sw["conv_b"][None, :], _pad_lanes(sw["dt_bias"], 0), _pad_lanes(sw["a_log"], 0),
           _pad_lanes(sw["d_skip"], 0), sw["ssd_norm_w"][None, :])
    return dict(w_proj=w_proj, qkv=qkv, ssd=ssd, n1=sw["norm1_w"][None, :])


def late_operands(g, sw):
    return dict(wo=g["w_out"].reshape(D_MODEL, D_MODEL), wgu=g["w_gate_up"],
                wd=g["w_down"].reshape(N_DEV // 2, FF_SHARD, D_MODEL), n2=sw["norm2_w"][None, :])


def layer_fwd(x, mod, kw, cos_t, sin_t, job=None, late=None):
    sh1, sc1, g1, sh2, sc2, g2 = [mod[i:i + 1] for i in range(6)]
    pa, pz, px, plast = proj_fwd(x, kw["n1"], sh1, sc1, kw["w_proj"])
    q, k, v = qkv_fwd(pa, plast, cos_t, sin_t, kw["qkv"])
    (o, lse), carried = attn_fwd(q, k, v, job)
    if late is not None:
        kw = {**kw, **late(carried)}
    yg, states = ssd_fwd(px, pz, plast, kw["ssd"])
    x_mid = out_fwd(x, o, yg, g1, kw["wo"])
    x_out, mix = mlp_fwd(x_mid, kw["n2"], sh2, sc2, g2, kw["wgu"], kw["wd"])
    saved = dict(x=x, pa=pa, pz=pz, px=px, plast=plast, q=q, k=k, v=v, o=o, lse=lse, yg=yg, states=states, x_mid=x_mid,
                 mix=mix)
    return x_out, saved, kw, carried


def layer_bwd_head(dy, mod, kw, sv, job=None):
    _, _, g1, sh2, sc2, g2 = [mod[i:i + 1] for i in range(6)]
    (dparts, dn2, dsh2, dsc2, dg2, dwg, dwu, dwd), carried = mlp_bwd(
        sv["x_mid"], dy, sv["mix"], kw["n2"], sh2, sc2, g2, kw["wgu"], kw["wd"], job)
    dmid, do, delta, dyg, dg1, dwo = out_bwd(dy, dparts, sv["o"], sv["yg"], g1, kw["wo"])
    early = dict(w_out=[dwo.reshape(N_DEV, D_MODEL // N_DEV, D_MODEL)], w_gate_up=[dwg, dwu],
                 w_down=[dwd.reshape(N_DEV, D_FF // N_DEV, D_MODEL)])
    head = dict(dmid=dmid, do=do, delta=delta, dyg=dyg, dn2=dn2, dsh2=dsh2, dsc2=dsc2, dg2=dg2, dg1=dg1)
    return head, early, carried


def layer_bwd_tail(hd, mod, kw, cos_t, sin_t, sv, job=None):
    sh1, sc1 = mod[0:1], mod[1:2]
    (dq, dk, dv), carried = attn_bwd(sv["q"], sv["k"], sv["v"], hd["do"], sv["lse"], hd["delta"], job)
    dpx, dpz, dpl_dt, dcw, dcb, ddtb, dalog, ddskip, dsnw = ssd_bwd(sv["px"], sv["pz"], sv["plast"], sv["states"],
                                                                   hd["dyg"], kw["ssd"])
    dpa, dpl_k, dqaw, dkvaw, dwq, dwk, dwv, dqnw, dknw, dkpw = qkv_bwd(sv["pa"], sv["plast"], cos_t, sin_t, kw["qkv"],
                                                                       dq, dk, dv)
    dx, dn1, dsh1, dsc1, dwp = proj_bwd(sv["x"], kw["n1"], sh1, sc1, kw["w_proj"], dpa, dpz, dpx, dpl_k, dpl_dt, hd["dmid"])
    dmod = jnp.concatenate([dsh1, dsc1, hd["dg1"], hd["dsh2"], hd["dsc2"], hd["dg2"]], axis=0)
    dw_in = jnp.concatenate([dwp[:, :384], dwp[:, 1984:2016], dwp[:, 384:1920], dwp[:, 1920:1928]], axis=1)
    grads = dict(
        norm1_w=dn1[0], norm2_w=hd["dn2"][0], q_a_norm_w=dqaw[0], kv_a_norm_w=dkvaw[0],
        q_nope_norm_w=dqnw[0, :NOPE], q_pe_norm_w=dqnw[0, NOPE:NOPE + ROPE], k_nope_norm_w=dknw[0, :NOPE],
        k_pe_norm_w=dkpw[0, NOPE:NOPE + ROPE], conv_b=dcb[0], dt_bias=ddtb[0, :N_HEADS], a_log=dalog[0, :N_HEADS],
        d_skip=ddskip[0, :N_HEADS], ssd_norm_w=dsnw[0],
        w_in=[dw_in.reshape(D_MODEL, N_DEV, D_IN // N_DEV).transpose(1, 0, 2)],
        w_q_up=[dwq[:, :, :NOPE + ROPE].astype(bf16)],
        w_kv_up=[jnp.concatenate([dwk[:, :, :NOPE], dwv], axis=2).astype(bf16)],
        conv_w=[dcw.reshape(4, N_DEV, D_CONV // N_DEV).transpose(1, 0, 2).astype(bf16)],
    )
    return dx, dmod, grads, carried


def _pack_small(get, last=None):
    flat = jnp.concatenate([get(name).reshape(-1) for name, _ in SMALL])
    flat = jnp.pad(flat, (0, SMALL_ROWS * 128 - flat.shape[0]))
    if last is not None:
        flat = flat.at[-1].set(last)
    return flat.reshape(SMALL_ROWS, 128)


def _unpack_small(packed):
    flat = packed.reshape(-1)
    out, off = {}, 0
    for name, size in SMALL:
        out[name] = flat[off:off + 2 * size].reshape(2, size)
        off += 2 * size
    return out


def kernel(x, c, positions, norm1_w, norm2_w, w_ada, b_ada, w_in, q_a_norm_w, w_q_up, kv_a_norm_w, w_kv_up, q_nope_norm_w, q_pe_norm_w, k_nope_norm_w, k_pe_norm_w, conv_w, conv_b, dt_bias, a_log, d_skip, ssd_norm_w, w_out, w_gate_up, w_down, loss_target, m_norm1_w, m_norm2_w, m_w_ada, m_b_ada, m_w_in, m_q_a_norm_w, m_w_q_up, m_kv_a_norm_w, m_w_kv_up, m_q_nope_norm_w, m_q_pe_norm_w, m_k_nope_norm_w, m_k_pe_norm_w, m_conv_w, m_conv_b, m_dt_bias, m_a_log, m_d_skip, m_ssd_norm_w, m_w_out, m_w_gate_up, m_w_down, v_norm1_w, v_norm2_w, v_w_ada, v_b_ada, v_w_in, v_q_a_norm_w, v_w_q_up, v_kv_a_norm_w, v_w_kv_up, v_q_nope_norm_w, v_q_pe_norm_w, v_k_nope_norm_w, v_k_pe_norm_w, v_conv_w, v_conv_b, v_dt_bias, v_a_log, v_d_skip, v_ssd_norm_w, v_w_out, v_w_gate_up, v_w_down):
    w = dict(norm1_w=norm1_w, norm2_w=norm2_w, w_ada=w_ada, b_ada=b_ada, w_in=w_in, q_a_norm_w=q_a_norm_w, w_q_up=w_q_up,
             kv_a_norm_w=kv_a_norm_w, w_kv_up=w_kv_up, q_nope_norm_w=q_nope_norm_w, q_pe_norm_w=q_pe_norm_w,
             k_nope_norm_w=k_nope_norm_w, k_pe_norm_w=k_pe_norm_w, conv_w=conv_w, conv_b=conv_b, dt_bias=dt_bias,
             a_log=a_log, d_skip=d_skip, ssd_norm_w=ssd_norm_w, w_out=w_out, w_gate_up=w_gate_up, w_down=w_down)
    m = dict(norm1_w=m_norm1_w, norm2_w=m_norm2_w, w_ada=m_w_ada, b_ada=m_b_ada, w_in=m_w_in, q_a_norm_w=m_q_a_norm_w,
             w_q_up=m_w_q_up, kv_a_norm_w=m_kv_a_norm_w, w_kv_up=m_w_kv_up, q_nope_norm_w=m_q_nope_norm_w,
             q_pe_norm_w=m_q_pe_norm_w, k_nope_norm_w=m_k_nope_norm_w, k_pe_norm_w=m_k_pe_norm_w, conv_w=m_conv_w,
             conv_b=m_conv_b, dt_bias=m_dt_bias, a_log=m_a_log, d_skip=m_d_skip, ssd_norm_w=m_ssd_norm_w, w_out=m_w_out,
             w_gate_up=m_w_gate_up, w_down=m_w_down)
    v = dict(norm1_w=v_norm1_w, norm2_w=v_norm2_w, w_ada=v_w_ada, b_ada=v_b_ada, w_in=v_w_in, q_a_norm_w=v_q_a_norm_w,
             w_q_up=v_w_q_up, kv_a_norm_w=v_kv_a_norm_w, w_kv_up=v_w_kv_up, q_nope_norm_w=v_q_nope_norm_w,
             q_pe_norm_w=v_q_pe_norm_w, k_nope_norm_w=v_k_nope_norm_w, k_pe_norm_w=v_k_pe_norm_w, conv_w=v_conv_w,
             conv_b=v_conv_b, dt_bias=v_dt_bias, a_log=v_a_log, d_skip=v_d_skip, ssd_norm_w=v_ssd_norm_w, w_out=v_w_out,
             w_gate_up=v_w_gate_up, w_down=v_w_down)
    me = _my_index()
    seq = x.shape[1]

    def shards(names, l):
        return [w[name][l] if name == "conv_w" else w[name][l].astype(bf16) for name in names]

    small = [{name: w[name][l] for name, _ in SMALL if name != "b_ada"} for l in range(2)]
    n_mix, n_late = len(MIXER_WEIGHTS), len(LATE_WEIGHTS)

    first = comm_call(gather_job([c] + shards(MIXER_WEIGHTS, 0)), "gather_first")
    c_all = first[0].reshape(N_DEV, D_MODEL)
    kws = [mixer_operands(dict(zip(MIXER_WEIGHTS, first[1:])), small[0]), None]
    rest_job = gather_job(shards(LATE_WEIGHTS, 0) + shards(MIXER_WEIGHTS, 1) + shards(LATE_WEIGHTS, 1))

    b_cols = lax.dynamic_slice_in_dim(b_ada, me * 768, 768, axis=1)
    mod_cols = ada_fwd(c_all, w_ada, b_cols)
    (mod_all,) = comm_call(gather_job([mod_cols]), "gather_mod")
    mod_me = lax.dynamic_index_in_dim(mod_all, me, axis=2, keepdims=False)
    mods = [mod_me[:, l, :].reshape(6, D_MODEL) for l in range(2)]

    inv_freq = 1.0 / (ROPE_THETA ** (jnp.arange(0, ROPE, 2, dtype=f32) / ROPE))
    inv = _pad_lanes(jnp.concatenate([inv_freq, inv_freq]), NOPE)
    cos_t, sin_t = rope_tables(positions.reshape(seq, 1), inv)

    saved = [None, None]
    h, saved[0], kws[0], rest = layer_fwd(
        x[0], mods[0], kws[0], cos_t, sin_t, rest_job,
        lambda got: late_operands(dict(zip(LATE_WEIGHTS, got[:n_late])), small[0]))
    kws[1] = {**mixer_operands(dict(zip(MIXER_WEIGHTS, rest[n_late:n_late + n_mix])), small[1]),
              **late_operands(dict(zip(LATE_WEIGHTS, rest[n_late + n_mix:])), small[1])}
    h, saved[1], _, _ = layer_fwd(h, mods[1], kws[1], cos_t, sin_t)
    dy, loss_part = loss_fwd(h, loss_target[0])

    early, late = ("w_out", "w_gate_up", "w_down"), ("w_in", "w_q_up", "w_kv_up", "conv_w")
    parts = [{}, {}]
    head, pieces, _ = layer_bwd_head(dy, mods[1], kws[1], saved[1])
    dy, dmod1, grads1, got = layer_bwd_tail(head, mods[1], kws[1], cos_t, sin_t, saved[1], scatter_job([pieces[n] for n in early]))
    parts[1].update(zip(early, got))
    head, pieces, got = layer_bwd_head(dy, mods[0], kws[0], saved[0], scatter_job([grads1[n] for n in late]))
    parts[1].update(zip(late, got))
    dy, dmod0, grads0, got = layer_bwd_tail(head, mods[0], kws[0], cos_t, sin_t, saved[0], scatter_job([pieces[n] for n in early]))
    parts[0].update(zip(early, got))
    parts[0].update(zip(late, comm_call(scatter_job([grads0[n] for n in late]), "scatter_layer0_rest")))
    grad_x = dy[None]

    small_part = {name: jnp.stack([grads0[name], grads1[name]]) for name, _ in SMALL if name != "b_ada"}
    small_part["b_ada"] = jnp.stack([dmod0.reshape(-1), dmod1.reshape(-1)])
    (small_all,) = comm_call(gather_job([_pack_small(lambda n: small_part[n], loss_part[0, 0])]), "gather_small_grads")
    packed = adamw(small_all, _pack_small(lambda n: w[n])[None], _pack_small(lambda n: m[n])[None],
                   _pack_small(lambda n: v[n])[None], 0, None, "adamw_small")
    loss = packed[0][0, -1, -1]
    res = {}
    for key, arr in zip("gdmv", packed):
        for name, val in _unpack_small(arr[0]).items():
            res[key, name] = val

    off = 2 * (1024 + 1024)
    dmod_all = small_all.reshape(N_DEV, -1)[:, off:off + 2 * 6144].reshape(N_DEV, 2, 6144)
    dmod_cols = lax.dynamic_slice_in_dim(dmod_all, me * 768, 768, axis=2).transpose(1, 0, 2)
    g_ada = ada_bwd(c_all, dmod_cols)
    out = None
    for l in range(2):
        out = adamw(g_ada[l][None], w_ada, m_w_ada, v_w_ada, l, out, "adamw_w_ada")
    res.update(zip([(key, "w_ada") for key in "gdmv"], out))

    for name in BIG:
        out = None
        for l in range(2):
            out = adamw(parts[l][name], w[name], m[name], v[name], l, out, "adamw_" + name)
        res.update(zip([(key, name) for key in "gdmv"], out))

    return (loss, grad_x, *[res["g", n] for n in WEIGHTS], *[res["d", n] for n in WEIGHTS],
            *[res["m", n] for n in WEIGHTS], *[res["v", n] for n in WEIGHTS])
```

```python
import functools

import jax
import jax.numpy as jnp
from jax import lax
from jax.experimental import pallas as pl
from jax.experimental.pallas import tpu as pltpu

f32 = jnp.float32
bf16 = jnp.bfloat16

N_DEV = 8
D_MODEL = 1024
N_HEADS = 8
HEAD_LANES = 128
NOPE = 64
ROPE = 32
V_DIM = 64
Q_RANK = 256
KV_RANK = 128
D_SSD = 512
D_CONV = 1024
SSD_STATE = 128
SSD_HEAD_DIM = 64
CHUNK = 128
HALO = 8
D_FF = 2816
FF_SHARD = 704
D_IN = 1960
D_PROJ = 2048
EPS = 1e-6
LOG2E = 1.4426950408889634
LN2 = 0.6931471805599453
Q_SCALE = (NOPE + ROPE) ** -0.5 * LOG2E
ATTN_ROWS_FWD = 256
ATTN_HEADS_FWD = 4
ATTN_HEADS_BWD = 2
ROPE_THETA = 10000.0
NEG = -1e30

ADAM_LR = 0.001
ADAM_B1 = 0.9
ADAM_B2 = 0.999
ADAM_EPS = 1e-08
ADAM_WD = 0.01
ADAM_STEP = 10
ADAMW_BLOCK_BYTES = 24 << 20

MESH = pl.DeviceIdType.MESH
ANY = pl.BlockSpec(memory_space=pl.ANY)

SMALL = (("norm1_w", 1024), ("norm2_w", 1024), ("b_ada", 6144), ("q_a_norm_w", 256), ("kv_a_norm_w", 128),
         ("q_nope_norm_w", 64), ("q_pe_norm_w", 32), ("k_nope_norm_w", 64), ("k_pe_norm_w", 32),
         ("conv_b", 1024), ("dt_bias", 8), ("a_log", 8), ("d_skip", 8), ("ssd_norm_w", 512))
SMALL_ROWS = 168
BIG = ("w_in", "w_q_up", "w_kv_up", "conv_w", "w_out", "w_gate_up", "w_down")
WEIGHTS = ("norm1_w", "norm2_w", "w_ada", "b_ada", "w_in", "q_a_norm_w", "w_q_up", "kv_a_norm_w", "w_kv_up",
           "q_nope_norm_w", "q_pe_norm_w", "k_nope_norm_w", "k_pe_norm_w", "conv_w", "conv_b", "dt_bias",
           "a_log", "d_skip", "ssd_norm_w", "w_out", "w_gate_up", "w_down")


def _dot(a, b, ca, cb):
    return lax.dot_general(a.astype(bf16), b.astype(bf16), (((ca,), (cb,)), ((), ())), preferred_element_type=f32)


@jax.custom_vjp
def mm(a, b):
    return _dot(a, b, 1, 0)


def _mm_fwd(a, b):
    return _dot(a, b, 1, 0), (a, b)


def _mm_bwd(res, g):
    a, b = res
    return _dot(g, b, 1, 1).astype(a.dtype), _dot(a, g, 0, 0).astype(b.dtype)


mm.defvjp(_mm_fwd, _mm_bwd)


@jax.custom_vjp
def _mm_slot(a, w, slot):
    return _dot(a, w, 1, 0)


def _mm_slot_fwd(a, w, slot):
    return _dot(a, w, 1, 0), (a, w)


def _mm_slot_bwd(res, g):
    a, w = res
    return _dot(g, w, 1, 1).astype(a.dtype), None, _dot(a, g, 0, 0)


_mm_slot.defvjp(_mm_slot_fwd, _mm_slot_bwd)


def mmw(a, w, slot=None):
    return _dot(a, w, 1, 0) if slot is None else _mm_slot(a, w, slot)


@jax.custom_vjp
def mm_nt(a, b):
    return _dot(a, b, 1, 1)


def _mm_nt_fwd(a, b):
    return _dot(a, b, 1, 1), (a, b)


def _mm_nt_bwd(res, g):
    a, b = res
    return _dot(g, b, 1, 0).astype(a.dtype), _dot(g, a, 0, 0).astype(b.dtype)


mm_nt.defvjp(_mm_nt_fwd, _mm_nt_bwd)


@jax.custom_vjp
def mm_tn(a, b):
    return _dot(a, b, 0, 0)


def _mm_tn_fwd(a, b):
    return _dot(a, b, 0, 0), (a, b)


def _mm_tn_bwd(res, g):
    a, b = res
    return _dot(b, g, 1, 1).astype(a.dtype), _dot(a, g, 1, 0).astype(b.dtype)


mm_tn.defvjp(_mm_tn_fwd, _mm_tn_bwd)


def _rms(x, w):
    return x * lax.rsqrt(jnp.mean(x * x, axis=-1, keepdims=True) + EPS) * w


def _const(shape):
    n = len(shape)
    return pl.BlockSpec(shape, lambda *_: (0,) * n)


def _accumulate(first, refs, vals):
    @pl.when(first)
    def _():
        for r, v in zip(refs, vals):
            r[...] = v

    @pl.when(jnp.logical_not(first))
    def _():
        for r, v in zip(refs, vals):
            r[...] += v


def _accumulate_then_cast(first, last, accs, outs, vals):
    _accumulate(first, accs, vals)

    @pl.when(last)
    def _():
        for a, o in zip(accs, outs):
            o[...] = a[...].astype(o.dtype)


def _token_block(s):
    return min(512, s)


def _f_proj(x, nw, sh, sc, w, slot=None):
    h = _rms(x, nw) * (1.0 + sc) + sh
    return mmw(h, w, slot)


def _f_qkv(pa, plast, cos_t, sin_t, qaw, kvaw, wq, wk, wv, qnw, knw, kpw, slots=None):
    sq, sk, sv = slots if slots is not None else ([None] * N_HEADS,) * 3
    lane = lax.broadcasted_iota(jnp.int32, (1, HEAD_LANES), 1)
    m_nope = lane < NOPE
    m_pe = (lane >= NOPE) & (lane < NOPE + ROPE)
    rows = pa.shape[0]

    def rope(t):
        half = ROPE // 2
        swapped = jnp.concatenate(
            [jnp.zeros((rows, NOPE), f32), t[:, NOPE + half:NOPE + ROPE], t[:, NOPE:NOPE + half],
             jnp.zeros((rows, HEAD_LANES - NOPE - ROPE), f32)], axis=1)
        return t * cos_t + swapped * sin_t

    qa = _rms(pa[:, :Q_RANK], qaw)
    kva = _rms(pa[:, Q_RANK:Q_RANK + KV_RANK], kvaw)
    kp = jnp.where(m_pe, plast, 0.0)
    kp = kp * lax.rsqrt(jnp.sum(kp * kp, axis=-1, keepdims=True) / ROPE + EPS) * kpw
    k_rot = rope(kp)
    qs, ks, vs = [], [], []
    for h in range(N_HEADS):
        qh = mmw(qa, wq[h], sq[h])
        ss_n = jnp.sum(jnp.where(m_nope, qh * qh, 0.0), axis=-1, keepdims=True) / NOPE
        ss_p = jnp.sum(jnp.where(m_pe, qh * qh, 0.0), axis=-1, keepdims=True) / ROPE
        r = jnp.where(m_nope, lax.rsqrt(ss_n + EPS), lax.rsqrt(ss_p + EPS))
        qs.append(rope(qh * r * qnw) * Q_SCALE)
        kh = mmw(kva, wk[h], sk[h])
        kh = kh * lax.rsqrt(jnp.sum(kh * kh, axis=-1, keepdims=True) / NOPE + EPS) * knw
        ks.append(kh + k_rot)
        vs.append(mmw(kva, wv[h], sv[h]))
    return jnp.stack(qs), jnp.stack(ks), jnp.stack(vs)


def _f_ssd(xext, z, plast, prev, cw, cb, dtb, alog, dskip, snw):
    n = CHUNK
    conv = cb
    for k in range(4):
        conv = conv + cw[k:k + 1] * xext[HALO - 3 + k:HALO - 3 + k + n]
    xc = jax.nn.silu(conv)
    xs, bm, cm = xc[:, :D_SSD], xc[:, D_SSD:D_SSD + 2 * SSD_STATE], xc[:, D_SSD + 2 * SSD_STATE:]
    lane = lax.broadcasted_iota(jnp.int32, (1, 128), 1)
    dt = jax.nn.softplus(jnp.where(lane < N_HEADS, plast, 0.0) + dtb)
    adt = dt * (-jnp.exp(alog))
    row = lax.broadcasted_iota(jnp.int32, (n, n), 0)
    col = lax.broadcasted_iota(jnp.int32, (n, n), 1)
    tri = row >= col
    acs = jnp.dot(tri.astype(f32), adt, precision=lax.Precision.HIGHEST, preferred_element_type=f32)
    acs_t = acs.T
    ys, news = [], []
    for g in range(2):
        bg = bm[:, g * SSD_STATE:(g + 1) * SSD_STATE]
        cg = cm[:, g * SSD_STATE:(g + 1) * SSD_STATE]
        cb_t = mm_nt(cg, bg)
        for r in range(4):
            h = g * 4 + r
            a_col = acs[:, h:h + 1]
            a_row = acs_t[h:h + 1, :]
            decay_ls = jnp.exp(jnp.where(tri, a_col - a_row, -jnp.inf))
            xh = xs[:, h * SSD_HEAD_DIM:(h + 1) * SSD_HEAD_DIM]
            xdt = xh * dt[:, h:h + 1]
            y_diag = mm(cb_t * decay_ls, xdt)
            a_last = acs[n - 1:n, h:h + 1]
            st = mm_tn(xdt * jnp.exp(a_last - a_col), bg)
            news.append(jnp.exp(a_last) * prev[h] + st)
            y_off = mm_nt(cg, prev[h]) * jnp.exp(a_col)
            ys.append(y_diag + y_off + dskip[:, h:h + 1] * xh)
    y = jnp.concatenate(ys, axis=1)
    yg = y * jax.nn.silu(z)
    half = D_SSD // 2
    outs = []
    for g in range(2):
        t = yg[:, g * half:(g + 1) * half]
        outs.append(t * lax.rsqrt(jnp.mean(t * t, axis=-1, keepdims=True) + EPS))
    return jnp.concatenate(outs, axis=1) * snw, jnp.stack(news)


def _f_out(o, yg, g1, wo, slot=None):
    cat = jnp.concatenate([o[h] for h in range(N_HEADS)] + [yg], axis=1)
    return g1 * mmw(cat, wo, slot)


def _f_gate_up(x, nw, sh, sc, wg, wu, slot_g=None, slot_u=None):
    h = _rms(x, nw) * (1.0 + sc) + sh
    return jax.nn.silu(mmw(h, wg, slot_g)) * mmw(h, wu, slot_u)


def proj_fwd(x, nw, sh, sc, w):
    s = x.shape[0]
    ts = _token_block(s)

    def body(x_ref, nw_ref, sh_ref, sc_ref, w_ref, pa_ref, pz_ref, px_ref, pl_ref):
        p = _f_proj(x_ref[...], nw_ref[...], sh_ref[...], sc_ref[...], w_ref[...])
        pa_ref[...] = p[:, :384]
        pz_ref[...] = p[:, 384:896]
        px_ref[...] = p[:, 896:1920]
        pl_ref[...] = p[:, 1920:]

    vec = _const((1, D_MODEL))
    return pl.pallas_call(
        body, name="proj_fwd", grid=(s // ts,),
        in_specs=[pl.BlockSpec((ts, D_MODEL), lambda i: (i, 0)), vec, vec, vec, _const((D_MODEL, D_PROJ))],
        out_specs=[pl.BlockSpec((ts, 384), lambda i: (i, 0)), pl.BlockSpec((ts, 512), lambda i: (i, 0)),
                   pl.BlockSpec((ts, 1024), lambda i: (i, 0)), pl.BlockSpec((ts, 128), lambda i: (i, 0))],
        out_shape=[jax.ShapeDtypeStruct((s, 384), f32), jax.ShapeDtypeStruct((s, 512), f32),
                   jax.ShapeDtypeStruct((s, 1024), f32), jax.ShapeDtypeStruct((s, 128), f32)],
    )(x, nw, sh, sc, w)


def rope_tables(pos, inv):
    s = pos.shape[0]
    ts = _token_block(s)

    def body(pos_ref, inv_ref, cos_ref, sin_ref):
        ang = pos_ref[...].astype(f32) * inv_ref[...]
        lane = lax.broadcasted_iota(jnp.int32, (1, HEAD_LANES), 1)
        half = ROPE // 2
        cos_ref[...] = jnp.where(lane < NOPE, 1.0, jnp.where(lane < NOPE + ROPE, jnp.cos(ang), 0.0))
        sn = jnp.sin(ang)
        sin_ref[...] = jnp.where((lane >= NOPE) & (lane < NOPE + half), -sn,
                                 jnp.where((lane >= NOPE + half) & (lane < NOPE + ROPE), sn, 0.0))

    return pl.pallas_call(
        body, name="rope_tables", grid=(s // ts,),
        in_specs=[pl.BlockSpec((ts, 1), lambda i: (i, 0)), _const((1, HEAD_LANES))],
        out_specs=[pl.BlockSpec((ts, HEAD_LANES), lambda i: (i, 0))] * 2,
        out_shape=[jax.ShapeDtypeStruct((s, HEAD_LANES), f32)] * 2,
    )(pos, inv)


def _qkv_param_specs():
    return [_const((1, Q_RANK)), _const((1, KV_RANK)), _const((N_HEADS, Q_RANK, HEAD_LANES)),
            _const((N_HEADS, KV_RANK, HEAD_LANES)), _const((N_HEADS, KV_RANK, V_DIM)),
            _const((1, HEAD_LANES)), _const((1, HEAD_LANES)), _const((1, HEAD_LANES))]


def qkv_fwd(pa, plast, cos_t, sin_t, params):
    s = pa.shape[0]
    ts = _token_block(s)

    def body(pa_ref, pl_ref, cos_ref, sin_ref, *rest):
        prm = [r[...] for r in rest[:8]]
        q_ref, k_ref, v_ref = rest[8:]
        q, k, v = _f_qkv(pa_ref[...], pl_ref[...], cos_ref[...], sin_ref[...], *prm)
        q_ref[...] = q.astype(bf16)
        k_ref[...] = k.astype(bf16)
        v_ref[...] = jnp.concatenate([v, jnp.ones_like(v)], axis=-1).astype(bf16)

    tok = lambda w: pl.BlockSpec((ts, w), lambda i: (i, 0))
    head = pl.BlockSpec((N_HEADS, ts, HEAD_LANES), lambda i: (0, i, 0))
    return pl.pallas_call(
        body, name="qkv_fwd", grid=(s // ts,),
        in_specs=[tok(384), tok(128), tok(128), tok(128)] + _qkv_param_specs(),
        out_specs=[head] * 3, out_shape=[jax.ShapeDtypeStruct((N_HEADS, s, HEAD_LANES), bf16)] * 3,
    )(pa, plast, cos_t, sin_t, *params)


def _scores(q, k):
    return lax.dot_general(q, k, (((1,), (1,)), ((), ())), preferred_element_type=f32)


def _tril(rows, cols, row_offset):
    row = row_offset + lax.broadcasted_iota(jnp.int32, (rows, cols), 0)
    col = lax.broadcasted_iota(jnp.int32, (rows, cols), 1)
    return row >= col


def _call_with_job(body, name, grid, job, in_specs, out_specs, out_shape, scratch_shapes, operands, relay_at=None):
    if job is None:
        res = pl.pallas_call(body, name=name, grid=grid, in_specs=in_specs, out_specs=out_specs, out_shape=out_shape,
                             scratch_shapes=scratch_shapes)(*operands)
        return res, None

    def at_step(i, n):
        if i == 0:
            want = [0] * len(grid)
        elif i == n - 1:
            want = [g - 1 for g in grid]
        else:
            want = relay_at
        return functools.reduce(jnp.logical_and, [pl.program_id(a) == s for a, s in enumerate(want)])

    carrier = _carry(job, body, len(in_specs), len(out_specs), at_step)
    res = pl.pallas_call(
        carrier, name=name, grid=grid,
        in_specs=list(in_specs) + [ANY] * len(job.operands), out_specs=list(out_specs) + [ANY] * len(job.out_shape),
        out_shape=list(out_shape) + list(job.out_shape), scratch_shapes=list(scratch_shapes) + job.scratch,
    )(*operands, *job.operands)
    return res[:len(out_specs)], res[len(out_specs):]


def attn_fwd(q, k, v, job=None):
    s = q.shape[1]
    t = _token_block(s)
    nb = s // t

    rb = min(ATTN_ROWS_FWD, t)

    hp = ATTN_HEADS_FWD

    def body(q_ref, k_ref, v_ref, o_ref, lse_ref, m_sc, acc_sc):
        qi = pl.program_id(1)
        m_sc[...] = jnp.full(m_sc.shape, NEG, f32)
        acc_sc[...] = jnp.zeros(acc_sc.shape, f32)

        def step(k0, diagonal):
            chains = [(hh, r) for hh in range(hp) for r in range(t // rb)]

            def scores(hh, r):
                nk = (r + 1) * rb if diagonal else t
                sc = _scores(q_ref[hh, pl.ds(r * rb, rb), :], k_ref[hh, pl.ds(k0, nk), :])
                return jnp.where(_tril(rb, nk, r * rb), sc, NEG) if diagonal else sc

            ahead = scores(*chains[0])
            for c, (hh, r) in enumerate(chains):
                sc = ahead
                if c + 1 < len(chains):
                    ahead = scores(*chains[c + 1])
                rows = pl.ds(r * rb, rb)
                keys = pl.ds(k0, (r + 1) * rb if diagonal else t)
                m_prev = m_sc[hh, rows, :1]
                m_new = jnp.maximum(m_prev, jnp.max(sc, axis=-1, keepdims=True))
                p = jnp.exp2(sc - m_new)
                alpha = jnp.exp2(m_prev - m_new)
                acc = alpha * acc_sc[hh, rows, :] + jnp.dot(p.astype(bf16), v_ref[hh, keys, :], preferred_element_type=f32)
                if diagonal:
                    l = acc[:, V_DIM:V_DIM + 1]
                    o_ref[hh, rows, :] = acc[:, :V_DIM] / l
                    lse_ref[hh, rows, :] = jnp.broadcast_to(m_new + jnp.log2(l), (rb, 128))
                else:
                    acc_sc[hh, rows, :] = acc
                    m_sc[hh, rows, :] = jnp.broadcast_to(m_new, (rb, 128))

        def below(ki, carry):
            step(pl.multiple_of(ki * t, t), False)
            return carry

        lax.fori_loop(0, qi, below, 0)
        step(pl.multiple_of(qi * t, t), True)

    return _call_with_job(
        body, "attn_fwd" if job is None else "attn_fwd_comm", (N_HEADS // hp, nb), job,
        in_specs=[pl.BlockSpec((hp, t, HEAD_LANES), lambda h, qi: (h, qi, 0)),
                  pl.BlockSpec((hp, s, HEAD_LANES), lambda h, qi: (h, 0, 0)),
                  pl.BlockSpec((hp, s, HEAD_LANES), lambda h, qi: (h, 0, 0))],
        out_specs=[pl.BlockSpec((hp, t, V_DIM), lambda h, qi: (h, qi, 0)),
                   pl.BlockSpec((hp, t, 128), lambda h, qi: (h, qi, 0))],
        out_shape=[jax.ShapeDtypeStruct((N_HEADS, s, V_DIM), f32), jax.ShapeDtypeStruct((N_HEADS, s, 128), f32)],
        scratch_shapes=[pltpu.VMEM((hp, t, 128), f32), pltpu.VMEM((hp, t, HEAD_LANES), f32)],
        operands=(q, k, v), relay_at=(N_HEADS // hp - 1, max(nb - 2, 0)))


def _ssd_param_specs():
    return [_const((4, D_CONV)), _const((1, D_CONV)), _const((1, 128)), _const((1, 128)), _const((1, 128)),
            _const((1, D_SSD))]


def ssd_fwd(px, pz, plast, params):
    s = px.shape[0]
    nc = s // CHUNK

    def body(px_ref, pz_ref, pl_ref, cw_ref, cb_ref, dtb_ref, alog_ref, dskip_ref, snw_ref, yg_ref, st_ref,
             state_sc, halo_sc):
        i = pl.program_id(0)

        @pl.when(i == 0)
        def _():
            state_sc[...] = jnp.zeros(state_sc.shape, f32)
            halo_sc[...] = jnp.zeros(halo_sc.shape, f32)

        x = px_ref[...]
        prev = state_sc[...]
        st_ref[...] = prev
        xext = jnp.concatenate([halo_sc[...], x], axis=0)
        yg, new = _f_ssd(xext, pz_ref[...], pl_ref[...], prev, cw_ref[...], cb_ref[...], dtb_ref[...],
                         alog_ref[...], dskip_ref[...], snw_ref[...])
        yg_ref[...] = yg
        state_sc[...] = new
        halo_sc[...] = x[CHUNK - HALO:]

    tok = lambda w: pl.BlockSpec((CHUNK, w), lambda i: (i, 0))
    return pl.pallas_call(
        body, name="ssd_fwd", grid=(nc,),
        in_specs=[tok(D_CONV), tok(D_SSD), tok(128)] + _ssd_param_specs(),
        out_specs=[tok(D_SSD), pl.BlockSpec((None, N_HEADS, SSD_HEAD_DIM, SSD_STATE), lambda i: (i, 0, 0, 0))],
        out_shape=[jax.ShapeDtypeStruct((s, D_SSD), f32),
                   jax.ShapeDtypeStruct((nc, N_HEADS, SSD_HEAD_DIM, SSD_STATE), f32)],
        scratch_shapes=[pltpu.VMEM((N_HEADS, SSD_HEAD_DIM, SSD_STATE), f32), pltpu.VMEM((HALO, D_CONV), f32)],
    )(px, pz, plast, *params)


def out_fwd(x, o, yg, g1, wo):
    s = x.shape[0]
    ts = _token_block(s)

    def body(x_ref, o_ref, yg_ref, g1_ref, wo_ref, out_ref):
        out_ref[...] = x_ref[...] + _f_out(o_ref[...], yg_ref[...], g1_ref[...], wo_ref[...])

    return pl.pallas_call(
        body, name="out_fwd", grid=(s // ts,),
        in_specs=[pl.BlockSpec((ts, D_MODEL), lambda i: (i, 0)), pl.BlockSpec((N_HEADS, ts, V_DIM), lambda i: (0, i, 0)),
                  pl.BlockSpec((ts, D_SSD), lambda i: (i, 0)), _const((1, D_MODEL)), _const((D_MODEL, D_MODEL))],
        out_specs=pl.BlockSpec((ts, D_MODEL), lambda i: (i, 0)),
        out_shape=jax.ShapeDtypeStruct((s, D_MODEL), f32),
    )(x, o, yg, g1, wo)


def mlp_fwd(x, nw, sh, sc, g2, wgu, wd):
    s = x.shape[0]
    ts = _token_block(s)
    nj = N_DEV // 2

    def body(x_ref, nw_ref, sh_ref, sc_ref, g2_ref, wg_ref, wu_ref, wd_ref, out_ref, mix_ref):
        j = pl.program_id(1)
        act = _f_gate_up(x_ref[...], nw_ref[...], sh_ref[...], sc_ref[...], wg_ref[...], wu_ref[...])
        _accumulate(j == 0, [mix_ref], [mmw(act, wd_ref[...])])

        @pl.when(j == nj - 1)
        def _():
            out_ref[...] = x_ref[...] + g2_ref[...] * mix_ref[...]

    vec = _const((1, D_MODEL))
    return pl.pallas_call(
        body, name="mlp_fwd", grid=(s // ts, nj),
        in_specs=[pl.BlockSpec((ts, D_MODEL), lambda i, j: (i, 0)), vec, vec, vec, vec,
                  pl.BlockSpec((None, D_MODEL, FF_SHARD), lambda i, j: (j, 0, 0)),
                  pl.BlockSpec((None, D_MODEL, FF_SHARD), lambda i, j: (j + nj, 0, 0)),
                  pl.BlockSpec((None, FF_SHARD, D_MODEL), lambda i, j: (j, 0, 0))],
        out_specs=[pl.BlockSpec((ts, D_MODEL), lambda i, j: (i, 0))] * 2,
        out_shape=[jax.ShapeDtypeStruct((s, D_MODEL), f32)] * 2,
    )(x, nw, sh, sc, g2, wgu, wgu, wd)


def loss_fwd(y, target):
    s = y.shape[0]
    ts = _token_block(s)

    def body(y_ref, t_ref, dy_ref, loss_ref):
        d = y_ref[...] - t_ref[...]
        dy_ref[...] = d * (1.0 / D_MODEL)
        part = 0.5 * jnp.sum(jnp.sum(d * d, axis=-1, keepdims=True) * (1.0 / D_MODEL), axis=0, keepdims=True)
        _accumulate(pl.program_id(0) == 0, [loss_ref], [jnp.broadcast_to(part, (8, 128))])

    return pl.pallas_call(
        body, name="loss_fwd", grid=(s // ts,),
        in_specs=[pl.BlockSpec((ts, D_MODEL), lambda i: (i, 0))] * 2,
        out_specs=[pl.BlockSpec((ts, D_MODEL), lambda i: (i, 0)), _const((8, 128))],
        out_shape=[jax.ShapeDtypeStruct((s, D_MODEL), f32), jax.ShapeDtypeStruct((8, 128), f32)],
    )(y, target)


def mlp_bwd(x, dy, mix, nw, sh, sc, g2, wgu, wd, job=None):
    s = x.shape[0]
    ts = min(256, s)
    nj = N_DEV // 2
    ni = s // ts

    def body(x_ref, dy_ref, mix_ref, nw_ref, sh_ref, sc_ref, g2_ref, wg_ref, wu_ref, wd_ref,
             dx_ref, dnw_ref, dsh_ref, dsc_ref, dg2_ref, dwg_ref, dwu_ref, dwd_ref, ag_sc, au_sc, ad_sc):
        j, i = pl.program_id(0), pl.program_id(1)
        wg, wu, wd = wg_ref[...], wu_ref[...], wd_ref[...]
        dy = dy_ref[...]
        act, vjp = jax.vjp(lambda x_, nw_, sh_, sc_, sg, su: _f_gate_up(x_, nw_, sh_, sc_, wg, wu, sg, su),
                           x_ref[...], nw_ref[...], sh_ref[...], sc_ref[...],
                           jnp.zeros(wg.shape, f32), jnp.zeros(wu.shape, f32))
        dmix = dy * g2_ref[...]
        dact = _dot(dmix, wd, 1, 1)
        dwd = _dot(act, dmix, 0, 0)
        dx, dnw, dsh, dsc, dwg, dwu = vjp(dact)
        dx_ref[...] = dx
        _accumulate((i == 0) & (j == 0), [dnw_ref, dsh_ref, dsc_ref], [dnw, dsh, dsc])
        _accumulate_then_cast(i == 0, i == ni - 1, [ag_sc, au_sc, ad_sc], [dwg_ref, dwu_ref, dwd_ref], [dwg, dwu, dwd])

        @pl.when(j == 0)
        def _():
            _accumulate(i == 0, [dg2_ref], [jnp.sum(dy * mix_ref[...], axis=0, keepdims=True)])

    vec = _const((1, D_MODEL))
    vshape = jax.ShapeDtypeStruct((1, D_MODEL), f32)
    wspec = lambda off: pl.BlockSpec((None, D_MODEL, FF_SHARD), lambda j, i: (j + off, 0, 0))
    dspec = pl.BlockSpec((None, FF_SHARD, D_MODEL), lambda j, i: (j, 0, 0))
    return _call_with_job(
        body, "mlp_bwd" if job is None else "mlp_bwd_comm", (nj, ni), job,
        in_specs=[pl.BlockSpec((ts, D_MODEL), lambda j, i: (i, 0)), pl.BlockSpec((ts, D_MODEL), lambda j, i: (i, 0)),
                  pl.BlockSpec((ts, D_MODEL), lambda j, i: (jnp.where(j == 0, i, 0), 0)),
                  vec, vec, vec, vec, wspec(0), wspec(nj), dspec],
        out_specs=[pl.BlockSpec((None, ts, D_MODEL), lambda j, i: (j, i, 0)), vec, vec, vec, vec,
                   wspec(0), wspec(0), dspec],
        out_shape=[jax.ShapeDtypeStruct((nj, s, D_MODEL), f32), vshape, vshape, vshape, vshape,
                   jax.ShapeDtypeStruct((nj, D_MODEL, FF_SHARD), bf16), jax.ShapeDtypeStruct((nj, D_MODEL, FF_SHARD), bf16),
                   jax.ShapeDtypeStruct((nj, FF_SHARD, D_MODEL), bf16)],
        scratch_shapes=[pltpu.VMEM((D_MODEL, FF_SHARD), f32), pltpu.VMEM((D_MODEL, FF_SHARD), f32),
                        pltpu.VMEM((FF_SHARD, D_MODEL), f32)],
        operands=(x, dy, mix, nw, sh, sc, g2, wgu, wgu, wd))


def out_bwd(dy, dparts, o, yg, g1, wo):
    s = dy.shape[0]
    ts = _token_block(s)
    nj = dparts.shape[0]

    ni = s // ts

    def body(dy_ref, dp_ref, o_ref, yg_ref, g1_ref, wo_ref, dx_ref, do_ref, delta_ref, dyg_ref, dg1_ref, dwo_ref, acc_sc):
        i = pl.program_id(0)
        g = dy_ref[...]
        for j in range(nj):
            g = g + dp_ref[j]
        dx_ref[...] = g
        o = o_ref[...]
        wo = wo_ref[...]
        _, vjp = jax.vjp(lambda o_, yg_, g1_, slot: _f_out(o_, yg_, g1_, wo, slot), o, yg_ref[...], g1_ref[...],
                         jnp.zeros(wo.shape, f32))
        do, dyg, dg1, dwo = vjp(g)
        do_ref[...] = do.astype(bf16)
        dyg_ref[...] = dyg
        delta_ref[...] = jnp.broadcast_to(jnp.sum(do * o, axis=-1, keepdims=True), delta_ref.shape)
        _accumulate(i == 0, [dg1_ref], [dg1])
        _accumulate_then_cast(i == 0, i == ni - 1, [acc_sc], [dwo_ref], [dwo])

    head = pl.BlockSpec((N_HEADS, ts, V_DIM), lambda i: (0, i, 0))
    return pl.pallas_call(
        body, name="out_bwd", grid=(ni,), scratch_shapes=[pltpu.VMEM((D_MODEL, D_MODEL), f32)],
        in_specs=[pl.BlockSpec((ts, D_MODEL), lambda i: (i, 0)), pl.BlockSpec((nj, ts, D_MODEL), lambda i: (0, i, 0)),
                  head, pl.BlockSpec((ts, D_SSD), lambda i: (i, 0)), _const((1, D_MODEL)), _const((D_MODEL, D_MODEL))],
        out_specs=[pl.BlockSpec((ts, D_MODEL), lambda i: (i, 0)), head,
                   pl.BlockSpec((N_HEADS, ts, 128), lambda i: (0, i, 0)), pl.BlockSpec((ts, D_SSD), lambda i: (i, 0)),
                   _const((1, D_MODEL)), _const((D_MODEL, D_MODEL))],
        out_shape=[jax.ShapeDtypeStruct((s, D_MODEL), f32), jax.ShapeDtypeStruct((N_HEADS, s, V_DIM), bf16),
                   jax.ShapeDtypeStruct((N_HEADS, s, 128), f32), jax.ShapeDtypeStruct((s, D_SSD), f32),
                   jax.ShapeDtypeStruct((1, D_MODEL), f32), jax.ShapeDtypeStruct((D_MODEL, D_MODEL), bf16)],
    )(dy, dparts, o, yg, g1, wo)


def attn_bwd(q, k, v, do, lse, delta, job=None):
    s = q.shape[1]
    t = _token_block(s)
    nb = s // t

    hp = ATTN_HEADS_BWD

    def body(q_ref, k_ref, v_ref, do_ref, lse_ref, delta_ref, dq_ref, dk_ref, dv_ref):
        ki = pl.program_id(1)

        @pl.when(ki == 0)
        def _():
            dq_ref[...] = jnp.zeros(dq_ref.shape, f32)

        dk_ref[...] = jnp.zeros(dk_ref.shape, f32)
        dv_ref[...] = jnp.zeros(dv_ref.shape, f32)

        def step(q0, diagonal):
            rows = pl.ds(q0, t)

            def products(hh):
                sc = _scores(q_ref[hh, rows, :], k_ref[hh])
                dp = lax.dot_general(do_ref[hh, rows, :], v_ref[hh, :, :V_DIM], (((1,), (1,)), ((), ())),
                                     preferred_element_type=f32)
                return (jnp.where(_tril(t, t, 0), sc, NEG) if diagonal else sc), dp

            ahead = products(0)
            for hh in range(hp):
                sc, dp = ahead
                if hh + 1 < hp:
                    ahead = products(hh + 1)
                p = jnp.exp2(sc - jnp.tile(lse_ref[hh, rows, :], (1, t // 128)))
                ds = (p * (dp - jnp.tile(delta_ref[hh, rows, :], (1, t // 128)))).astype(bf16)
                dv_ref[hh] += lax.dot_general(p.astype(bf16), do_ref[hh, rows, :], (((0,), (0,)), ((), ())),
                                              preferred_element_type=f32)
                dk_ref[hh] += lax.dot_general(ds, q_ref[hh, rows, :], (((0,), (0,)), ((), ())), preferred_element_type=f32)
                dq_ref[hh, rows, :] += jnp.dot(ds, k_ref[hh], preferred_element_type=f32)

        step(pl.multiple_of(ki * t, t), True)

        def above(qi, carry):
            step(pl.multiple_of(qi * t, t), False)
            return carry

        lax.fori_loop(ki + 1, nb, above, 0)
        dk_ref[...] = dk_ref[...] * LN2

        @pl.when(ki == nb - 1)
        def _():
            dq_ref[...] = dq_ref[...] * LN2

    qspec = lambda w: pl.BlockSpec((hp, s, w), lambda h, ki: (h, 0, 0))
    kspec = lambda w: pl.BlockSpec((hp, t, w), lambda h, ki: (h, ki, 0))
    return _call_with_job(
        body, "attn_bwd" if job is None else "attn_bwd_comm", (N_HEADS // hp, nb), job,
        in_specs=[qspec(HEAD_LANES), kspec(HEAD_LANES), kspec(HEAD_LANES), qspec(V_DIM), qspec(128), qspec(128)],
        out_specs=[qspec(HEAD_LANES), kspec(HEAD_LANES), kspec(V_DIM)],
        out_shape=[jax.ShapeDtypeStruct((N_HEADS, s, HEAD_LANES), f32), jax.ShapeDtypeStruct((N_HEADS, s, HEAD_LANES), f32),
                   jax.ShapeDtypeStruct((N_HEADS, s, V_DIM), f32)],
        scratch_shapes=[], operands=(q, k, v, do, lse, delta))


def ssd_bwd(px, pz, plast, states, dyg, params):
    s = px.shape[0]
    nc = s // CHUNK
    per = CHUNK // HALO

    def body(px_ref, halo_ref, pz_ref, pl_ref, st_ref, dyg_ref, cw_ref, cb_ref, dtb_ref, alog_ref, dskip_ref, snw_ref,
             dpx_ref, dpz_ref, dpl_ref, dcw_ref, dcb_ref, ddtb_ref, dalog_ref, ddskip_ref, dsnw_ref, dstate_sc, dhalo_sc):
        t = pl.program_id(0)
        chunk = nc - 1 - t

        @pl.when(t == 0)
        def _():
            dstate_sc[...] = jnp.zeros(dstate_sc.shape, f32)
            dhalo_sc[...] = jnp.zeros(dhalo_sc.shape, f32)

        halo = jnp.where(chunk > 0, halo_ref[...], 0.0)
        xext = jnp.concatenate([halo, px_ref[...]], axis=0)
        _, vjp = jax.vjp(_f_ssd, xext, pz_ref[...], pl_ref[...], st_ref[...], cw_ref[...], cb_ref[...], dtb_ref[...],
                         alog_ref[...], dskip_ref[...], snw_ref[...])
        dxext, dz, dpl, dprev, dcw, dcb, ddtb, dalog, ddskip, dsnw = vjp((dyg_ref[...], dstate_sc[...]))
        dpx_ref[...] = dxext[HALO:]
        dpx_ref[CHUNK - HALO:, :] += dhalo_sc[...]
        dhalo_sc[...] = dxext[:HALO]
        dstate_sc[...] = dprev
        dpz_ref[...] = dz
        dpl_ref[...] = dpl
        _accumulate(t == 0, [dcw_ref, dcb_ref, ddtb_ref, dalog_ref, ddskip_ref, dsnw_ref],
                    [dcw, dcb, ddtb, dalog, ddskip, dsnw])

    rev = lambda w: pl.BlockSpec((CHUNK, w), lambda t: (nc - 1 - t, 0))
    pshapes = [jax.ShapeDtypeStruct((4, D_CONV), f32), jax.ShapeDtypeStruct((1, D_CONV), f32),
               jax.ShapeDtypeStruct((1, 128), f32), jax.ShapeDtypeStruct((1, 128), f32),
               jax.ShapeDtypeStruct((1, 128), f32), jax.ShapeDtypeStruct((1, D_SSD), f32)]
    return pl.pallas_call(
        body, name="ssd_bwd", grid=(nc,),
        in_specs=[rev(D_CONV),
                  pl.BlockSpec((HALO, D_CONV), lambda t: (jnp.maximum((nc - 1 - t) * per - 1, 0), 0)),
                  rev(D_SSD), rev(128),
                  pl.BlockSpec((None, N_HEADS, SSD_HEAD_DIM, SSD_STATE), lambda t: (nc - 1 - t, 0, 0, 0)),
                  rev(D_SSD)] + _ssd_param_specs(),
        out_specs=[rev(D_CONV), rev(D_SSD), rev(128)] + _ssd_param_specs(),
        out_shape=[jax.ShapeDtypeStruct((s, D_CONV), f32), jax.ShapeDtypeStruct((s, D_SSD), f32),
                   jax.ShapeDtypeStruct((s, 128), f32)] + pshapes,
        scratch_shapes=[pltpu.VMEM((N_HEADS, SSD_HEAD_DIM, SSD_STATE), f32), pltpu.VMEM((HALO, D_CONV), f32)],
    )(px, px, pz, plast, states, dyg, *params)


def qkv_bwd(pa, plast, cos_t, sin_t, params, dq, dk, dv):
    s = pa.shape[0]
    ts = _token_block(s)

    def body(pa_ref, pl_ref, cos_ref, sin_ref, *rest):
        qaw, kvaw, wq, wk, wv, qnw, knw, kpw = [r[...] for r in rest[:8]]
        dq_ref, dk_ref, dv_ref = rest[8:11]
        dpa_ref, dpl_ref = rest[11:13]
        dprm_refs = list(rest[13:])
        cos_t, sin_t = cos_ref[...], sin_ref[...]

        def stage(pa_, pl_, qaw_, kvaw_, sq, sk, sv, qnw_, knw_, kpw_):
            return _f_qkv(pa_, pl_, cos_t, sin_t, qaw_, kvaw_, wq, wk, wv, qnw_, knw_, kpw_, (sq, sk, sv))

        _, vjp = jax.vjp(stage, pa_ref[...], pl_ref[...], qaw, kvaw, jnp.zeros(wq.shape, f32), jnp.zeros(wk.shape, f32),
                         jnp.zeros(wv.shape, f32), qnw, knw, kpw)
        grads = vjp((dq_ref[...], dk_ref[...], dv_ref[...]))
        dpa_ref[...] = grads[0]
        dpl_ref[...] = grads[1]
        _accumulate(pl.program_id(0) == 0, dprm_refs, list(grads[2:]))

    tok = lambda w: pl.BlockSpec((ts, w), lambda i: (i, 0))
    head = lambda w: pl.BlockSpec((N_HEADS, ts, w), lambda i: (0, i, 0))
    pshapes = [jax.ShapeDtypeStruct((1, Q_RANK), f32), jax.ShapeDtypeStruct((1, KV_RANK), f32),
               jax.ShapeDtypeStruct((N_HEADS, Q_RANK, HEAD_LANES), f32), jax.ShapeDtypeStruct((N_HEADS, KV_RANK, HEAD_LANES), f32),
               jax.ShapeDtypeStruct((N_HEADS, KV_RANK, V_DIM), f32), jax.ShapeDtypeStruct((1, HEAD_LANES), f32),
               jax.ShapeDtypeStruct((1, HEAD_LANES), f32), jax.ShapeDtypeStruct((1, HEAD_LANES), f32)]
    return pl.pallas_call(
        body, name="qkv_bwd", grid=(s // ts,),
        in_specs=[tok(384), tok(128), tok(128), tok(128)] + _qkv_param_specs()
                 + [head(HEAD_LANES), head(HEAD_LANES), head(V_DIM)],
        out_specs=[tok(384), tok(128)] + _qkv_param_specs(),
        out_shape=[jax.ShapeDtypeStruct((s, 384), f32), jax.ShapeDtypeStruct((s, 128), f32)] + pshapes,
    )(pa, plast, cos_t, sin_t, *params, dq, dk, dv)


def proj_bwd(x, nw, sh, sc, w, dpa, dpz, dpx, dpl_k, dpl_dt, dres):
    s = x.shape[0]
    ts = _token_block(s)

    ni = s // ts

    def body(x_ref, nw_ref, sh_ref, sc_ref, w_ref, dpa_ref, dpz_ref, dpx_ref, dplk_ref, dpld_ref, dres_ref,
             dx_ref, dnw_ref, dsh_ref, dsc_ref, dw_ref, acc_sc):
        i = pl.program_id(0)
        g = jnp.concatenate([dpa_ref[...], dpz_ref[...], dpx_ref[...], dplk_ref[...] + dpld_ref[...]], axis=1)
        w = w_ref[...]
        _, vjp = jax.vjp(lambda x_, nw_, sh_, sc_, slot: _f_proj(x_, nw_, sh_, sc_, w, slot), x_ref[...], nw_ref[...],
                         sh_ref[...], sc_ref[...], jnp.zeros(w.shape, f32))
        dx, dnw, dsh, dsc, dw = vjp(g)
        dx_ref[...] = dx + dres_ref[...]
        _accumulate(i == 0, [dnw_ref, dsh_ref, dsc_ref], [dnw, dsh, dsc])
        _accumulate_then_cast(i == 0, i == ni - 1, [acc_sc], [dw_ref], [dw])

    vec = _const((1, D_MODEL))
    vshape = jax.ShapeDtypeStruct((1, D_MODEL), f32)
    tok = lambda w_: pl.BlockSpec((ts, w_), lambda i: (i, 0))
    return pl.pallas_call(
        body, name="proj_bwd", grid=(ni,), scratch_shapes=[pltpu.VMEM((D_MODEL, D_PROJ), f32)],
        in_specs=[tok(D_MODEL), vec, vec, vec, _const((D_MODEL, D_PROJ)), tok(384), tok(512), tok(1024), tok(128), tok(128),
                  tok(D_MODEL)],
        out_specs=[tok(D_MODEL), vec, vec, vec, _const((D_MODEL, D_PROJ))],
        out_shape=[jax.ShapeDtypeStruct((s, D_MODEL), f32), vshape, vshape, vshape,
                   jax.ShapeDtypeStruct((D_MODEL, D_PROJ), bf16)],
    )(x, nw, sh, sc, w, dpa, dpz, dpx, dpl_k, dpl_dt, dres)


def ada_fwd(c_all, w_ada, b_cols):
    def body(c_ref, w_ref, b_ref, out_ref):
        act = jax.nn.silu(c_ref[...])
        for l in range(2):
            out_ref[l] = jnp.dot(act, w_ref[l], precision=lax.Precision.HIGHEST, preferred_element_type=f32) + b_ref[l]

    return pl.pallas_call(body, name="ada_fwd", out_shape=jax.ShapeDtypeStruct((2, N_DEV, 768), f32))(c_all, w_ada, b_cols)


def ada_bwd(c_all, dmod_cols):
    def body(c_ref, d_ref, out_ref):
        out_ref[0] = lax.dot_general(jax.nn.silu(c_ref[...]), d_ref[0], (((0,), (0,)), ((), ())),
                                     precision=lax.Precision.HIGHEST, preferred_element_type=f32)

    return pl.pallas_call(
        body, name="ada_bwd", grid=(2,),
        in_specs=[_const((N_DEV, D_MODEL)), pl.BlockSpec((1, N_DEV, 768), lambda l: (l, 0, 0))],
        out_specs=pl.BlockSpec((1, D_MODEL, 768), lambda l: (l, 0, 0)),
        out_shape=jax.ShapeDtypeStruct((2, D_MODEL, 768), f32),
    )(c_all, dmod_cols)


def _adamw(w, g, m, v):
    m = ADAM_B1 * m + (1.0 - ADAM_B1) * g
    v = ADAM_B2 * v + (1.0 - ADAM_B2) * (g * g)
    m_hat = m / (1.0 - ADAM_B1 ** ADAM_STEP)
    v_hat = v / (1.0 - ADAM_B2 ** ADAM_STEP)
    delta = -ADAM_LR * (m_hat / (jnp.sqrt(v_hat) + ADAM_EPS) + ADAM_WD * w)
    return delta, m, v


def adamw(parts, w, m, v, layer, prev, name):
    n, r, c = parts.shape
    nl = w.shape[0]
    tr = r
    lanes = -(-c // 128) * 128
    if 2 * (n + 7) * r * lanes * 4 > ADAMW_BLOCK_BYTES:
        tr = next(t for t in (256, 128, 64, 32, 16, 8) if r % t == 0)

    def body(p_ref, w_ref, m_ref, v_ref, *rest):
        g_ref, d_ref, nm_ref, nv_ref = rest[-4:]
        g = p_ref[0].astype(f32)
        for k in range(1, n):
            g = g + p_ref[k].astype(f32)
        delta, nm, nv = _adamw(w_ref[...], g, m_ref[...], v_ref[...])
        g_ref[...] = g
        d_ref[...] = delta
        nm_ref[...] = nm
        nv_ref[...] = nv

    blk = pl.BlockSpec((None, tr, c), lambda i: (layer, i, 0))
    shp = jax.ShapeDtypeStruct((nl, r, c), f32)
    kept = [] if prev is None else list(prev)
    return pl.pallas_call(
        body, name=name, grid=(r // tr,),
        in_specs=[pl.BlockSpec((n, tr, c), lambda i: (0, i, 0)), blk, blk, blk] + [ANY] * len(kept),
        out_specs=[blk] * 4, out_shape=[shp] * 4,
        input_output_aliases={4 + j: j for j in range(len(kept))},
    )(parts, w, m, v, *kept)


def _my_index():
    return 4 * lax.axis_index("x") + 2 * lax.axis_index("y") + lax.axis_index("c")


def _coords(idx):
    return (idx // 4, (idx // 2) % 2, idx % 2)


class CommJob:
    def __init__(self, operands, out_shape, phases, scratch):
        self.operands, self.out_shape, self.phases, self.scratch = operands, out_shape, phases, scratch


def _wait(out, n_blocks, send_sem, recv_sem, send=True, recv=True):
    span = out.at[pl.ds(0, n_blocks)]
    desc = pltpu.make_async_remote_copy(src_ref=span, dst_ref=span, send_sem=send_sem, recv_sem=recv_sem,
                                        device_id=_coords(_my_index()), device_id_type=MESH)
    if recv:
        desc.wait_recv()
    if send:
        desc.wait_send()


def gather_job(shards):
    n = len(shards)

    def places():
        x, y, c = lax.axis_index("x"), lax.axis_index("y"), lax.axis_index("c")
        return (x, y, c), (x, y, 1 - c), [(1 - x, y), (x, 1 - y), (1 - x, 1 - y)]

    def index(p):
        return 4 * p[0] + 2 * p[1] + p[2]

    def start(ins, outs, sems):
        far_send, far_recv, near_send, near_recv, local = sems
        me, sibling, chips = places()
        for k in range(n):
            pltpu.make_async_copy(ins[k], outs[k].at[index(me)], local.at[k]).start()
            for chip in chips:
                pltpu.make_async_remote_copy(src_ref=ins[k], dst_ref=outs[k].at[index(me)], send_sem=far_send.at[k],
                                             recv_sem=far_recv.at[k], device_id=(*chip, me[2]), device_id_type=MESH).start()
            pltpu.make_async_remote_copy(src_ref=ins[k], dst_ref=outs[k].at[index(me)], send_sem=near_send.at[k],
                                         recv_sem=near_recv.at[k], device_id=sibling, device_id_type=MESH).start()

    def relay(ins, outs, sems):
        far_send, far_recv, near_send, near_recv, local = sems
        me, sibling, chips = places()
        for k in range(n):
            _wait(outs[k], 3, far_send.at[k], far_recv.at[k], send=False)
            for chip in chips:
                block = outs[k].at[index((*chip, me[2]))]
                pltpu.make_async_remote_copy(src_ref=block, dst_ref=block, send_sem=near_send.at[k],
                                             recv_sem=near_recv.at[k], device_id=sibling, device_id_type=MESH).start()

    def finish(ins, outs, sems):
        far_send, far_recv, near_send, near_recv, local = sems
        for k in range(n):
            _wait(outs[k], 4, near_send.at[k], near_recv.at[k])
            _wait(outs[k], 3, far_send.at[k], far_recv.at[k], recv=False)
            pltpu.make_async_copy(ins[k], outs[k].at[0], local.at[k]).wait()

    shapes = [jax.ShapeDtypeStruct((N_DEV,) + tuple(a.shape), a.dtype) for a in shards]
    return CommJob(list(shards), shapes, [start, relay, finish], [pltpu.SemaphoreType.DMA((n,))] * 5)


def scatter_job(tensors):
    n = len(tensors)
    flat, where = [], {}
    for k, pieces in enumerate(tensors):
        d = 0
        for piece in pieces:
            for b in range(piece.shape[0]):
                where[k, d] = (len(flat), b)
                d += 1
            flat.append(piece)
        assert d == N_DEV

    def start(ins, outs, sems):
        send_sems, recv_sems, local_sems = sems
        me = _my_index()

        def block(k, d):
            i, b = where[k, d]
            return ins[i].at[b]

        for d in range(N_DEV):
            @pl.when(d != me)
            def _():
                for k in range(n):
                    pltpu.make_async_remote_copy(src_ref=block(k, d), dst_ref=outs[k].at[me], send_sem=send_sems.at[k],
                                                 recv_sem=recv_sems.at[k], device_id=(d // 4, (d // 2) % 2, d % 2),
                                                 device_id_type=MESH).start()

            @pl.when(d == me)
            def _():
                for k in range(n):
                    pltpu.make_async_copy(block(k, d), outs[k].at[d], local_sems.at[k]).start()

    def finish(ins, outs, sems):
        send_sems, recv_sems, local_sems = sems
        for k in range(n):
            _wait(outs[k], N_DEV - 1, send_sems.at[k], recv_sems.at[k])
            i, b = where[k, 0]
            pltpu.make_async_copy(ins[i].at[b], outs[k].at[0], local_sems.at[k]).wait()

    shapes = [jax.ShapeDtypeStruct((N_DEV,) + tuple(p[0].shape[1:]), p[0].dtype) for p in tensors]
    return CommJob(flat, shapes, [start, finish], [pltpu.SemaphoreType.DMA((n,))] * 3)


def comm_call(job, name):
    ni, no = len(job.operands), len(job.out_shape)

    def body(*refs):
        ins, outs, sems = refs[:ni], refs[ni:ni + no], refs[ni + no:]
        for phase in job.phases:
            phase(ins, outs, sems)

    return pl.pallas_call(body, name=name, in_specs=[ANY] * ni, out_specs=[ANY] * no, out_shape=job.out_shape,
                          scratch_shapes=job.scratch)(*job.operands)


def _carry(job, body, n_in, n_out, at_step):
    ji, jo, js = len(job.operands), len(job.out_shape), len(job.scratch)

    def carrier(*refs):
        a, b = n_in, n_in + ji
        c, d = b + n_out, b + n_out + jo
        e = len(refs) - js
        job_refs = (refs[a:b], refs[c:d], refs[e:])
        n = len(job.phases)

        @pl.when(at_step(0, n))
        def _():
            job.phases[0](*job_refs)

        body(*refs[:a], *refs[b:c], *refs[d:e])

        for i in range(1, n):
            @pl.when(at_step(i, n))
            def _():
                job.phases[i](*job_refs)

    return carrier


def _pad_lanes(v, lo, total=128):
    return jnp.pad(v, (lo, total - lo - v.shape[0]))[None, :]


MIXER_WEIGHTS = ("w_in", "w_q_up", "w_kv_up", "conv_w")
LATE_WEIGHTS = ("w_out", "w_gate_up", "w_down")


def mixer_operands(g, sw):
    w_in = g["w_in"].transpose(1, 0, 2).reshape(D_MODEL, D_IN)
    z = jnp.zeros((D_MODEL, 1), w_in.dtype)
    w_proj = jnp.concatenate(
        [w_in[:, :384], w_in[:, 416:928], w_in[:, 928:1952], w_in[:, 1952:1960], jnp.tile(z, (1, 56)),
         w_in[:, 384:416], jnp.tile(z, (1, 32))], axis=1)
    wq = jnp.pad(g["w_q_up"], ((0, 0), (0, 0), (0, HEAD_LANES - NOPE - ROPE)))
    wk = jnp.pad(g["w_kv_up"][:, :, :NOPE], ((0, 0), (0, 0), (0, HEAD_LANES - NOPE)))
    wv = g["w_kv_up"][:, :, NOPE:]
    qkv = (sw["q_a_norm_w"][None, :], sw["kv_a_norm_w"][None, :], wq, wk, wv,
           _pad_lanes(jnp.concatenate([sw["q_nope_norm_w"], sw["q_pe_norm_w"]]), 0),
           _pad_lanes(sw["k_nope_norm_w"], 0), _pad_lanes(sw["k_pe_norm_w"], NOPE))
    conv_w = g["conv_w"].astype(f32).transpose(1, 0, 2).reshape(4, D_CONV)
    ssd = (conv_w, sw["conv_b"][None, :], _pad_lanes(sw["dt_bias"], 0), _pad_lanes(sw["a_log"], 0),
           _pad_lanes(sw["d_skip"], 0), sw["ssd_norm_w"][None, :])
    return dict(w_proj=w_proj, qkv=qkv, ssd=ssd, n1=sw["norm1_w"][None, :])


def late_operands(g, sw):
    return dict(wo=g["w_out"].reshape(D_MODEL, D_MODEL), wgu=g["w_gate_up"],
                wd=g["w_down"].reshape(N_DEV // 2, FF_SHARD, D_MODEL), n2=sw["norm2_w"][None, :])


def layer_fwd(x, mod, kw, cos_t, sin_t, job=None, late=None):
    sh1, sc1, g1, sh2, sc2, g2 = [mod[i:i + 1] for i in range(6)]
    pa, pz, px, plast = proj_fwd(x, kw["n1"], sh1, sc1, kw["w_proj"])
    q, k, v = qkv_fwd(pa, plast, cos_t, sin_t, kw["qkv"])
    (o, lse), carried = attn_fwd(q, k, v, job)
    if late is not None:
        kw = {**kw, **late(carried)}
    yg, states = ssd_fwd(px, pz, plast, kw["ssd"])
    x_mid = out_fwd(x, o, yg, g1, kw["wo"])
    x_out, mix = mlp_fwd(x_mid, kw["n2"], sh2, sc2, g2, kw["wgu"], kw["wd"])
    saved = dict(x=x, pa=pa, pz=pz, px=px, plast=plast, q=q, k=k, v=v, o=o, lse=lse, yg=yg, states=states, x_mid=x_mid,
                 mix=mix)
    return x_out, saved, kw, carried


def layer_bwd_head(dy, mod, kw, sv, job=None):
    _, _, g1, sh2, sc2, g2 = [mod[i:i + 1] for i in range(6)]
    (dparts, dn2, dsh2, dsc2, dg2, dwg, dwu, dwd), carried = mlp_bwd(
        sv["x_mid"], dy, sv["mix"], kw["n2"], sh2, sc2, g2, kw["wgu"], kw["wd"], job)
    dmid, do, delta, dyg, dg1, dwo = out_bwd(dy, dparts, sv["o"], sv["yg"], g1, kw["wo"])
    early = dict(w_out=[dwo.reshape(N_DEV, D_MODEL // N_DEV, D_MODEL)], w_gate_up=[dwg, dwu],
                 w_down=[dwd.reshape(N_DEV, D_FF // N_DEV, D_MODEL)])
    head = dict(dmid=dmid, do=do, delta=delta, dyg=dyg, dn2=dn2, dsh2=dsh2, dsc2=dsc2, dg2=dg2, dg1=dg1)
    return head, early, carried


def layer_bwd_tail(hd, mod, kw, cos_t, sin_t, sv, job=None):
    sh1, sc1 = mod[0:1], mod[1:2]
    (dq, dk, dv), carried = attn_bwd(sv["q"], sv["k"], sv["v"], hd["do"], sv["lse"], hd["delta"], job)
    dpx, dpz, dpl_dt, dcw, dcb, ddtb, dalog, ddskip, dsnw = ssd_bwd(sv["px"], sv["pz"], sv["plast"], sv["states"],
                                                                   hd["dyg"], kw["ssd"])
    dpa, dpl_k, dqaw, dkvaw, dwq, dwk, dwv, dqnw, dknw, dkpw = qkv_bwd(sv["pa"], sv["plast"], cos_t, sin_t, kw["qkv"],
                                                                       dq, dk, dv)
    dx, dn1, dsh1, dsc1, dwp = proj_bwd(sv["x"], kw["n1"], sh1, sc1, kw["w_proj"], dpa, dpz, dpx, dpl_k, dpl_dt, hd["dmid"])
    dmod = jnp.concatenate([dsh1, dsc1, hd["dg1"], hd["dsh2"], hd["dsc2"], hd["dg2"]], axis=0)
    dw_in = jnp.concatenate([dwp[:, :384], dwp[:, 1984:2016], dwp[:, 384:1920], dwp[:, 1920:1928]], axis=1)
    grads = dict(
        norm1_w=dn1[0], norm2_w=hd["dn2"][0], q_a_norm_w=dqaw[0], kv_a_norm_w=dkvaw[0],
        q_nope_norm_w=dqnw[0, :NOPE], q_pe_norm_w=dqnw[0, NOPE:NOPE + ROPE], k_nope_norm_w=dknw[0, :NOPE],
        k_pe_norm_w=dkpw[0, NOPE:NOPE + ROPE], conv_b=dcb[0], dt_bias=ddtb[0, :N_HEADS], a_log=dalog[0, :N_HEADS],
        d_skip=ddskip[0, :N_HEADS], ssd_norm_w=dsnw[0],
        w_in=[dw_in.reshape(D_MODEL, N_DEV, D_IN // N_DEV).transpose(1, 0, 2)],
        w_q_up=[dwq[:, :, :NOPE + ROPE].astype(bf16)],
        w_kv_up=[jnp.concatenate([dwk[:, :, :NOPE], dwv], axis=2).astype(bf16)],
        conv_w=[dcw.reshape(4, N_DEV, D_CONV // N_DEV).transpose(1, 0, 2).astype(bf16)],
    )
    return dx, dmod, grads, carried


def _pack_small(get, last=None):
    flat = jnp.concatenate([get(name).reshape(-1) for name, _ in SMALL])
    flat = jnp.pad(flat, (0, SMALL_ROWS * 128 - flat.shape[0]))
    if last is not None:
        flat = flat.at[-1].set(last)
    return flat.reshape(SMALL_ROWS, 128)


def _unpack_small(packed):
    flat = packed.reshape(-1)
    out, off = {}, 0
    for name, size in SMALL:
        out[name] = flat[off:off + 2 * size].reshape(2, size)
        off += 2 * size
    return out


def kernel(x, c, positions, norm1_w, norm2_w, w_ada, b_ada, w_in, q_a_norm_w, w_q_up, kv_a_norm_w, w_kv_up, q_nope_norm_w, q_pe_norm_w, k_nope_norm_w, k_pe_norm_w, conv_w, conv_b, dt_bias, a_log, d_skip, ssd_norm_w, w_out, w_gate_up, w_down, loss_target, m_norm1_w, m_norm2_w, m_w_ada, m_b_ada, m_w_in, m_q_a_norm_w, m_w_q_up, m_kv_a_norm_w, m_w_kv_up, m_q_nope_norm_w, m_q_pe_norm_w, m_k_nope_norm_w, m_k_pe_norm_w, m_conv_w, m_conv_b, m_dt_bias, m_a_log, m_d_skip, m_ssd_norm_w, m_w_out, m_w_gate_up, m_w_down, v_norm1_w, v_norm2_w, v_w_ada, v_b_ada, v_w_in, v_q_a_norm_w, v_w_q_up, v_kv_a_norm_w, v_w_kv_up, v_q_nope_norm_w, v_q_pe_norm_w, v_k_nope_norm_w, v_k_pe_norm_w, v_conv_w, v_conv_b, v_dt_bias, v_a_log, v_d_skip, v_ssd_norm_w, v_w_out, v_w_gate_up, v_w_down):
    w = dict(norm1_w=norm1_w, norm2_w=norm2_w, w_ada=w_ada, b_ada=b_ada, w_in=w_in, q_a_norm_w=q_a_norm_w, w_q_up=w_q_up,
             kv_a_norm_w=kv_a_norm_w, w_kv_up=w_kv_up, q_nope_norm_w=q_nope_norm_w, q_pe_norm_w=q_pe_norm_w,
             k_nope_norm_w=k_nope_norm_w, k_pe_norm_w=k_pe_norm_w, conv_w=conv_w, conv_b=conv_b, dt_bias=dt_bias,
             a_log=a_log, d_skip=d_skip, ssd_norm_w=ssd_norm_w, w_out=w_out, w_gate_up=w_gate_up, w_down=w_down)
    m = dict(norm1_w=m_norm1_w, norm2_w=m_norm2_w, w_ada=m_w_ada, b_ada=m_b_ada, w_in=m_w_in, q_a_norm_w=m_q_a_norm_w,
             w_q_up=m_w_q_up, kv_a_norm_w=m_kv_a_norm_w, w_kv_up=m_w_kv_up, q_nope_norm_w=m_q_nope_norm_w,
             q_pe_norm_w=m_q_pe_norm_w, k_nope_norm_w=m_k_nope_norm_w, k_pe_norm_w=m_k_pe_norm_w, conv_w=m_conv_w,
             conv_b=m_conv_b, dt_bias=m_dt_bias, a_log=m_a_log, d_skip=m_d_skip, ssd_norm_w=m_ssd_norm_w, w_out=m_w_out,
             w_gate_up=m_w_gate_up, w_down=m_w_down)
    v = dict(norm1_w=v_norm1_w, norm2_w=v_norm2_w, w_ada=v_w_ada, b_ada=v_b_ada, w_in=v_w_in, q_a_norm_w=v_q_a_norm_w,
             w_q_up=v_w_q_up, kv_a_norm_w=v_kv_a_norm_w, w_kv_up=v_w_kv_up, q_nope_norm_w=v_q_nope_norm_w,
             q_pe_norm_w=v_q_pe_norm_w, k_nope_norm_w=v_k_nope_norm_w, k_pe_norm_w=v_k_pe_norm_w, conv_w=v_conv_w,
             conv_b=v_conv_b, dt_bias=v_dt_bias, a_log=v_a_log, d_skip=v_d_skip, ssd_norm_w=v_ssd_norm_w, w_out=v_w_out,
             w_gate_up=v_w_gate_up, w_down=v_w_down)
    me = _my_index()
    seq = x.shape[1]

    def shards(names, l):
        return [w[name][l] if name == "conv_w" else w[name][l].astype(bf16) for name in names]

    small = [{name: w[name][l] for name, _ in SMALL if name != "b_ada"} for l in range(2)]
    n_mix, n_late = len(MIXER_WEIGHTS), len(LATE_WEIGHTS)

    first = comm_call(gather_job([c] + shards(MIXER_WEIGHTS, 0)), "gather_first")
    c_all = first[0].reshape(N_DEV, D_MODEL)
    kws = [mixer_operands(dict(zip(MIXER_WEIGHTS, first[1:])), small[0]), None]

    b_cols = lax.dynamic_slice_in_dim(b_ada, me * 768, 768, axis=1)
    mod_cols = ada_fwd(c_all, w_ada, b_cols)
    (mod_all,) = comm_call(gather_job([mod_cols]), "gather_mod")
    mod_me = lax.dynamic_index_in_dim(mod_all, me, axis=2, keepdims=False)
    mods = [mod_me[:, l, :].reshape(6, D_MODEL) for l in range(2)]

    inv_freq = 1.0 / (ROPE_THETA ** (jnp.arange(0, ROPE, 2, dtype=f32) / ROPE))
    inv = _pad_lanes(jnp.concatenate([inv_freq, inv_freq]), NOPE)
    cos_t, sin_t = rope_tables(positions.reshape(seq, 1), inv)

    saved = [None, None]
    h, saved[0], kws[0], got = layer_fwd(
        x[0], mods[0], kws[0], cos_t, sin_t, gather_job(shards(LATE_WEIGHTS, 0) + shards(MIXER_WEIGHTS, 1)),
        lambda got: late_operands(dict(zip(LATE_WEIGHTS, got[:n_late])), small[0]))
    kws[1] = mixer_operands(dict(zip(MIXER_WEIGHTS, got[n_late:])), small[1])
    h, saved[1], kws[1], _ = layer_fwd(
        h, mods[1], kws[1], cos_t, sin_t, gather_job(shards(LATE_WEIGHTS, 1)),
        lambda got: late_operands(dict(zip(LATE_WEIGHTS, got)), small[1]))
    dy, loss_part = loss_fwd(h, loss_target[0])

    early, late = ("w_out", "w_gate_up", "w_down"), ("w_in", "w_q_up", "w_kv_up", "conv_w")
    parts = [{}, {}]
    head, pieces, _ = layer_bwd_head(dy, mods[1], kws[1], saved[1])
    dy, dmod1, grads1, got = layer_bwd_tail(head, mods[1], kws[1], cos_t, sin_t, saved[1], scatter_job([pieces[n] for n in early]))
    parts[1].update(zip(early, got))
    head, pieces, got = layer_bwd_head(dy, mods[0], kws[0], saved[0], scatter_job([grads1[n] for n in late]))
    parts[1].update(zip(late, got))
    dy, dmod0, grads0, got = layer_bwd_tail(head, mods[0], kws[0], cos_t, sin_t, saved[0], scatter_job([pieces[n] for n in early]))
    parts[0].update(zip(early, got))
    parts[0].update(zip(late, comm_call(scatter_job([grads0[n] for n in late]), "scatter_layer0_rest")))
    grad_x = dy[None]

    small_part = {name: jnp.stack([grads0[name], grads1[name]]) for name, _ in SMALL if name != "b_ada"}
    small_part["b_ada"] = jnp.stack([dmod0.reshape(-1), dmod1.reshape(-1)])
    (small_all,) = comm_call(gather_job([_pack_small(lambda n: small_part[n], loss_part[0, 0])]), "gather_small_grads")
    packed = adamw(small_all, _pack_small(lambda n: w[n])[None], _pack_small(lambda n: m[n])[None],
                   _pack_small(lambda n: v[n])[None], 0, None, "adamw_small")
    loss = packed[0][0, -1, -1]
    res = {}
    for key, arr in zip("gdmv", packed):
        for name, val in _unpack_small(arr[0]).items():
            res[key, name] = val

    off = 2 * (1024 + 1024)
    dmod_all = small_all.reshape(N_DEV, -1)[:, off:off + 2 * 6144].reshape(N_DEV, 2, 6144)
    dmod_cols = lax.dynamic_slice_in_dim(dmod_all, me * 768, 768, axis=2).transpose(1, 0, 2)
    g_ada = ada_bwd(c_all, dmod_cols)
    out = None
    for l in range(2):
        out = adamw(g_ada[l][None], w_ada, m_w_ada, v_w_ada, l, out, "adamw_w_ada")
    res.update(zip([(key, "w_ada") for key in "gdmv"], out))

    for name in BIG:
        out = None
        for l in range(2):
            out = adamw(parts[l][name], w[name], m[name], v[name], l, out, "adamw_" + name)
        res.update(zip([(key, name) for key in "gdmv"], out))

    return (loss, grad_x, *[res["g", n] for n in WEIGHTS], *[res["d", n] for n in WEIGHTS],
            *[res["m", n] for n in WEIGHTS], *[res["v", n] for n in WEIGHTS])
```

```python
import functools

import jax
import jax.numpy as jnp
from jax import lax
from jax.experimental import pallas as pl
from jax.experimental.pallas import tpu as pltpu

f32 = jnp.float32
bf16 = jnp.bfloat16

N_DEV = 8
D_MODEL = 1024
N_HEADS = 8
HEAD_LANES = 128
NOPE = 64
ROPE = 32
V_DIM = 64
Q_RANK = 256
KV_RANK = 128
D_SSD = 512
D_CONV = 1024
SSD_STATE = 128
SSD_HEAD_DIM = 64
CHUNK = 128
HALO = 8
D_FF = 2816
FF_SHARD = 704
D_IN = 1960
D_PROJ = 2048
EPS = 1e-6
LOG2E = 1.4426950408889634
LN2 = 0.6931471805599453
Q_SCALE = (NOPE + ROPE) ** -0.5 * LOG2E
ATTN_ROWS_FWD = 256
ATTN_HEADS_FWD = 4
ATTN_HEADS_BWD = 2
MLP_BWD_ROWS = 512
ROPE_THETA = 10000.0
NEG = -1e30

ADAM_LR = 0.001
ADAM_B1 = 0.9
ADAM_B2 = 0.999
ADAM_EPS = 1e-08
ADAM_WD = 0.01
ADAM_STEP = 10
ADAMW_BLOCK_BYTES = 24 << 20

MESH = pl.DeviceIdType.MESH
ANY = pl.BlockSpec(memory_space=pl.ANY)

SMALL = (("norm1_w", 1024), ("norm2_w", 1024), ("b_ada", 6144), ("q_a_norm_w", 256), ("kv_a_norm_w", 128),
         ("q_nope_norm_w", 64), ("q_pe_norm_w", 32), ("k_nope_norm_w", 64), ("k_pe_norm_w", 32),
         ("conv_b", 1024), ("dt_bias", 8), ("a_log", 8), ("d_skip", 8), ("ssd_norm_w", 512))
SMALL_ROWS = 168
BIG = ("w_in", "w_q_up", "w_kv_up", "conv_w", "w_out", "w_gate_up", "w_down")
WEIGHTS = ("norm1_w", "norm2_w", "w_ada", "b_ada", "w_in", "q_a_norm_w", "w_q_up", "kv_a_norm_w", "w_kv_up",
           "q_nope_norm_w", "q_pe_norm_w", "k_nope_norm_w", "k_pe_norm_w", "conv_w", "conv_b", "dt_bias",
           "a_log", "d_skip", "ssd_norm_w", "w_out", "w_gate_up", "w_down")


def _dot(a, b, ca, cb):
    return lax.dot_general(a.astype(bf16), b.astype(bf16), (((ca,), (cb,)), ((), ())), preferred_element_type=f32)


@jax.custom_vjp
def mm(a, b):
    return _dot(a, b, 1, 0)


def _mm_fwd(a, b):
    return _dot(a, b, 1, 0), (a, b)


def _mm_bwd(res, g):
    a, b = res
    return _dot(g, b, 1, 1).astype(a.dtype), _dot(a, g, 0, 0).astype(b.dtype)


mm.defvjp(_mm_fwd, _mm_bwd)


@jax.custom_vjp
def _mm_slot(a, w, slot):
    return _dot(a, w, 1, 0)


def _mm_slot_fwd(a, w, slot):
    return _dot(a, w, 1, 0), (a, w)


def _mm_slot_bwd(res, g):
    a, w = res
    return _dot(g, w, 1, 1).astype(a.dtype), None, _dot(a, g, 0, 0)


_mm_slot.defvjp(_mm_slot_fwd, _mm_slot_bwd)


def mmw(a, w, slot=None):
    return _dot(a, w, 1, 0) if slot is None else _mm_slot(a, w, slot)


@jax.custom_vjp
def mm_nt(a, b):
    return _dot(a, b, 1, 1)


def _mm_nt_fwd(a, b):
    return _dot(a, b, 1, 1), (a, b)


def _mm_nt_bwd(res, g):
    a, b = res
    return _dot(g, b, 1, 0).astype(a.dtype), _dot(g, a, 0, 0).astype(b.dtype)


mm_nt.defvjp(_mm_nt_fwd, _mm_nt_bwd)


@jax.custom_vjp
def mm_tn(a, b):
    return _dot(a, b, 0, 0)


def _mm_tn_fwd(a, b):
    return _dot(a, b, 0, 0), (a, b)


def _mm_tn_bwd(res, g):
    a, b = res
    return _dot(b, g, 1, 1).astype(a.dtype), _dot(a, g, 1, 0).astype(b.dtype)


mm_tn.defvjp(_mm_tn_fwd, _mm_tn_bwd)


def _rms(x, w):
    return x * lax.rsqrt(jnp.mean(x * x, axis=-1, keepdims=True) + EPS) * w


def _const(shape):
    n = len(shape)
    return pl.BlockSpec(shape, lambda *_: (0,) * n)


def _accumulate(first, refs, vals):
    @pl.when(first)
    def _():
        for r, v in zip(refs, vals):
            r[...] = v

    @pl.when(jnp.logical_not(first))
    def _():
        for r, v in zip(refs, vals):
            r[...] += v


def _accumulate_then_cast(first, last, accs, outs, vals):
    _accumulate(first, accs, vals)

    @pl.when(last)
    def _():
        for a, o in zip(accs, outs):
            o[...] = a[...].astype(o.dtype)


def _token_block(s):
    return min(512, s)


def _f_proj(x, nw, sh, sc, w, slot=None):
    h = _rms(x, nw) * (1.0 + sc) + sh
    return mmw(h, w, slot)


def _f_qkv(pa, plast, cos_t, sin_t, qaw, kvaw, wq, wk, wv, qnw, knw, kpw, slots=None):
    sq, sk, sv = slots if slots is not None else ([None] * N_HEADS,) * 3
    lane = lax.broadcasted_iota(jnp.int32, (1, HEAD_LANES), 1)
    m_nope = lane < NOPE
    m_pe = (lane >= NOPE) & (lane < NOPE + ROPE)
    rows = pa.shape[0]

    def rope(t):
        half = ROPE // 2
        swapped = jnp.concatenate(
            [jnp.zeros((rows, NOPE), f32), t[:, NOPE + half:NOPE + ROPE], t[:, NOPE:NOPE + half],
             jnp.zeros((rows, HEAD_LANES - NOPE - ROPE), f32)], axis=1)
        return t * cos_t + swapped * sin_t

    qa = _rms(pa[:, :Q_RANK], qaw)
    kva = _rms(pa[:, Q_RANK:Q_RANK + KV_RANK], kvaw)
    kp = jnp.where(m_pe, plast, 0.0)
    kp = kp * lax.rsqrt(jnp.sum(kp * kp, axis=-1, keepdims=True) / ROPE + EPS) * kpw
    k_rot = rope(kp)
    qs, ks, vs = [], [], []
    for h in range(N_HEADS):
        qh = mmw(qa, wq[h], sq[h])
        ss_n = jnp.sum(jnp.where(m_nope, qh * qh, 0.0), axis=-1, keepdims=True) / NOPE
        ss_p = jnp.sum(jnp.where(m_pe, qh * qh, 0.0), axis=-1, keepdims=True) / ROPE
        r = jnp.where(m_nope, lax.rsqrt(ss_n + EPS), lax.rsqrt(ss_p + EPS))
        qs.append(rope(qh * r * qnw) * Q_SCALE)
        kh = mmw(kva, wk[h], sk[h])
        kh = kh * lax.rsqrt(jnp.sum(kh * kh, axis=-1, keepdims=True) / NOPE + EPS) * knw
        ks.append(kh + k_rot)
        vs.append(mmw(kva, wv[h], sv[h]))
    return jnp.stack(qs), jnp.stack(ks), jnp.stack(vs)


def _f_ssd(xext, z, plast, prev, cw, cb, dtb, alog, dskip, snw):
    n = CHUNK
    conv = cb
    for k in range(4):
        conv = conv + cw[k:k + 1] * xext[HALO - 3 + k:HALO - 3 + k + n]
    xc = jax.nn.silu(conv)
    xs, bm, cm = xc[:, :D_SSD], xc[:, D_SSD:D_SSD + 2 * SSD_STATE], xc[:, D_SSD + 2 * SSD_STATE:]
    lane = lax.broadcasted_iota(jnp.int32, (1, 128), 1)
    dt = jax.nn.softplus(jnp.where(lane < N_HEADS, plast, 0.0) + dtb)
    adt = dt * (-jnp.exp(alog))
    row = lax.broadcasted_iota(jnp.int32, (n, n), 0)
    col = lax.broadcasted_iota(jnp.int32, (n, n), 1)
    tri = row >= col
    acs = jnp.dot(tri.astype(f32), adt, precision=lax.Precision.HIGHEST, preferred_element_type=f32)
    acs_t = acs.T
    bgs = [bm[:, g * SSD_STATE:(g + 1) * SSD_STATE] for g in range(2)]
    cgs = [cm[:, g * SSD_STATE:(g + 1) * SSD_STATE] for g in range(2)]
    cb_ts = [mm_nt(cgs[g], bgs[g]) for g in range(2)]
    pre = []
    for h in range(N_HEADS):
        a_col = acs[:, h:h + 1]
        a_row = acs_t[h:h + 1, :]
        decay_ls = jnp.exp(jnp.where(tri, a_col - a_row, -jnp.inf))
        xh = xs[:, h * SSD_HEAD_DIM:(h + 1) * SSD_HEAD_DIM]
        xdt = xh * dt[:, h:h + 1]
        a_last = acs[n - 1:n, h:h + 1]
        pre.append((cb_ts[h // 4] * decay_ls, xdt, xdt * jnp.exp(a_last - a_col), jnp.exp(a_last), jnp.exp(a_col), xh))
    prods = []
    for h in range(N_HEADS):
        scores, xdt, weighted, _, _, _ = pre[h]
        prods.append((mm(scores, xdt), mm_tn(weighted, bgs[h // 4]), mm_nt(cgs[h // 4], prev[h])))
    ys, news = [], []
    for h in range(N_HEADS):
        y_diag, st, y_off = prods[h]
        _, _, _, chunk_decay, in_decay, xh = pre[h]
        news.append(chunk_decay * prev[h] + st)
        ys.append(y_diag + y_off * in_decay + dskip[:, h:h + 1] * xh)
    y = jnp.concatenate(ys, axis=1)
    yg = y * jax.nn.silu(z)
    half = D_SSD // 2
    outs = []
    for g in range(2):
        t = yg[:, g * half:(g + 1) * half]
        outs.append(t * lax.rsqrt(jnp.mean(t * t, axis=-1, keepdims=True) + EPS))
    return jnp.concatenate(outs, axis=1) * snw, jnp.stack(news)


def _f_out(o, yg, g1, wo, slot=None):
    cat = jnp.concatenate([o[h] for h in range(N_HEADS)] + [yg], axis=1)
    return g1 * mmw(cat, wo, slot)


def _f_gate_up(x, nw, sh, sc, wg, wu, slot_g=None, slot_u=None):
    h = _rms(x, nw) * (1.0 + sc) + sh
    return jax.nn.silu(mmw(h, wg, slot_g)) * mmw(h, wu, slot_u)


def proj_fwd(x, nw, sh, sc, w):
    s = x.shape[0]
    ts = _token_block(s)

    def body(x_ref, nw_ref, sh_ref, sc_ref, w_ref, pa_ref, pz_ref, px_ref, pl_ref):
        p = _f_proj(x_ref[...], nw_ref[...], sh_ref[...], sc_ref[...], w_ref[...])
        pa_ref[...] = p[:, :384]
        pz_ref[...] = p[:, 384:896]
        px_ref[...] = p[:, 896:1920]
        pl_ref[...] = p[:, 1920:]

    vec = _const((1, D_MODEL))
    return pl.pallas_call(
        body, name="proj_fwd", grid=(s // ts,),
        in_specs=[pl.BlockSpec((ts, D_MODEL), lambda i: (i, 0)), vec, vec, vec, _const((D_MODEL, D_PROJ))],
        out_specs=[pl.BlockSpec((ts, 384), lambda i: (i, 0)), pl.BlockSpec((ts, 512), lambda i: (i, 0)),
                   pl.BlockSpec((ts, 1024), lambda i: (i, 0)), pl.BlockSpec((ts, 128), lambda i: (i, 0))],
        out_shape=[jax.ShapeDtypeStruct((s, 384), f32), jax.ShapeDtypeStruct((s, 512), f32),
                   jax.ShapeDtypeStruct((s, 1024), f32), jax.ShapeDtypeStruct((s, 128), f32)],
    )(x, nw, sh, sc, w)


def rope_tables(pos, inv):
    s = pos.shape[0]
    ts = _token_block(s)

    def body(pos_ref, inv_ref, cos_ref, sin_ref):
        ang = pos_ref[...].astype(f32) * inv_ref[...]
        lane = lax.broadcasted_iota(jnp.int32, (1, HEAD_LANES), 1)
        half = ROPE // 2
        cos_ref[...] = jnp.where(lane < NOPE, 1.0, jnp.where(lane < NOPE + ROPE, jnp.cos(ang), 0.0))
        sn = jnp.sin(ang)
        sin_ref[...] = jnp.where((lane >= NOPE) & (lane < NOPE + half), -sn,
                                 jnp.where((lane >= NOPE + half) & (lane < NOPE + ROPE), sn, 0.0))

    return pl.pallas_call(
        body, name="rope_tables", grid=(s // ts,),
        in_specs=[pl.BlockSpec((ts, 1), lambda i: (i, 0)), _const((1, HEAD_LANES))],
        out_specs=[pl.BlockSpec((ts, HEAD_LANES), lambda i: (i, 0))] * 2,
        out_shape=[jax.ShapeDtypeStruct((s, HEAD_LANES), f32)] * 2,
    )(pos, inv)


def _qkv_param_specs():
    return [_const((1, Q_RANK)), _const((1, KV_RANK)), _const((N_HEADS, Q_RANK, HEAD_LANES)),
            _const((N_HEADS, KV_RANK, HEAD_LANES)), _const((N_HEADS, KV_RANK, V_DIM)),
            _const((1, HEAD_LANES)), _const((1, HEAD_LANES)), _const((1, HEAD_LANES))]


def qkv_fwd(pa, plast, cos_t, sin_t, params):
    s = pa.shape[0]
    ts = _token_block(s)

    def body(pa_ref, pl_ref, cos_ref, sin_ref, *rest):
        prm = [r[...] for r in rest[:8]]
        q_ref, k_ref, v_ref = rest[8:]
        q, k, v = _f_qkv(pa_ref[...], pl_ref[...], cos_ref[...], sin_ref[...], *prm)
        q_ref[...] = q.astype(bf16)
        k_ref[...] = k.astype(bf16)
        v_ref[...] = jnp.concatenate([v, jnp.ones_like(v)], axis=-1).astype(bf16)

    tok = lambda w: pl.BlockSpec((ts, w), lambda i: (i, 0))
    head = pl.BlockSpec((N_HEADS, ts, HEAD_LANES), lambda i: (0, i, 0))
    return pl.pallas_call(
        body, name="qkv_fwd", grid=(s // ts,),
        in_specs=[tok(384), tok(128), tok(128), tok(128)] + _qkv_param_specs(),
        out_specs=[head] * 3, out_shape=[jax.ShapeDtypeStruct((N_HEADS, s, HEAD_LANES), bf16)] * 3,
    )(pa, plast, cos_t, sin_t, *params)


def _scores(q, k):
    return lax.dot_general(q, k, (((1,), (1,)), ((), ())), preferred_element_type=f32)


def _tril(rows, cols, row_offset):
    row = row_offset + lax.broadcasted_iota(jnp.int32, (rows, cols), 0)
    col = lax.broadcasted_iota(jnp.int32, (rows, cols), 1)
    return row >= col


def _call_with_job(body, name, grid, job, in_specs, out_specs, out_shape, scratch_shapes, operands, relay_at=None):
    if job is None:
        res = pl.pallas_call(body, name=name, grid=grid, in_specs=in_specs, out_specs=out_specs, out_shape=out_shape,
                             scratch_shapes=scratch_shapes)(*operands)
        return res, None

    def at_step(i, n):
        if i == 0:
            want = [0] * len(grid)
        elif i == n - 1:
            want = [g - 1 for g in grid]
        else:
            want = relay_at
        return functools.reduce(jnp.logical_and, [pl.program_id(a) == s for a, s in enumerate(want)])

    carrier = _carry(job, body, len(in_specs), len(out_specs), at_step)
    res = pl.pallas_call(
        carrier, name=name, grid=grid,
        in_specs=list(in_specs) + [ANY] * len(job.operands), out_specs=list(out_specs) + [ANY] * len(job.out_shape),
        out_shape=list(out_shape) + list(job.out_shape), scratch_shapes=list(scratch_shapes) + job.scratch,
    )(*operands, *job.operands)
    return res[:len(out_specs)], res[len(out_specs):]


def attn_fwd(q, k, v, job=None):
    s = q.shape[1]
    t = _token_block(s)
    nb = s // t

    rb = min(ATTN_ROWS_FWD, t)

    hp = ATTN_HEADS_FWD

    def body(q_ref, k_ref, v_ref, o_ref, lse_ref, m_sc, acc_sc):
        qi = pl.program_id(1)
        m_sc[...] = jnp.full(m_sc.shape, NEG, f32)
        acc_sc[...] = jnp.zeros(acc_sc.shape, f32)

        def step(k0, diagonal):
            chains = [(hh, r) for hh in range(hp) for r in range(t // rb)]

            def scores(hh, r):
                nk = (r + 1) * rb if diagonal else t
                sc = _scores(q_ref[hh, pl.ds(r * rb, rb), :], k_ref[hh, pl.ds(k0, nk), :])
                return jnp.where(_tril(rb, nk, r * rb), sc, NEG) if diagonal else sc

            ahead = scores(*chains[0])
            for c, (hh, r) in enumerate(chains):
                sc = ahead
                if c + 1 < len(chains):
                    ahead = scores(*chains[c + 1])
                rows = pl.ds(r * rb, rb)
                keys = pl.ds(k0, (r + 1) * rb if diagonal else t)
                m_prev = m_sc[hh, rows, :1]
                m_new = jnp.maximum(m_prev, jnp.max(sc, axis=-1, keepdims=True))
                p = jnp.exp2(sc - m_new)
                alpha = jnp.exp2(m_prev - m_new)
                acc = alpha * acc_sc[hh, rows, :] + jnp.dot(p.astype(bf16), v_ref[hh, keys, :], preferred_element_type=f32)
                if diagonal:
                    l = acc[:, V_DIM:V_DIM + 1]
                    o_ref[hh, rows, :] = acc[:, :V_DIM] / l
                    lse_ref[hh, rows, :] = jnp.broadcast_to(m_new + jnp.log2(l), (rb, 128))
                else:
                    acc_sc[hh, rows, :] = acc
                    m_sc[hh, rows, :] = jnp.broadcast_to(m_new, (rb, 128))

        def below(ki, carry):
            step(pl.multiple_of(ki * t, t), False)
            return carry

        lax.fori_loop(0, qi, below, 0)
        step(pl.multiple_of(qi * t, t), True)

    return _call_with_job(
        body, "attn_fwd" if job is None else "attn_fwd_comm", (N_HEADS // hp, nb), job,
        in_specs=[pl.BlockSpec((hp, t, HEAD_LANES), lambda h, qi: (h, qi, 0)),
                  pl.BlockSpec((hp, s, HEAD_LANES), lambda h, qi: (h, 0, 0)),
                  pl.BlockSpec((hp, s, HEAD_LANES), lambda h, qi: (h, 0, 0))],
        out_specs=[pl.BlockSpec((hp, t, V_DIM), lambda h, qi: (h, qi, 0)),
                   pl.BlockSpec((hp, t, 128), lambda h, qi: (h, qi, 0))],
        out_shape=[jax.ShapeDtypeStruct((N_HEADS, s, V_DIM), f32), jax.ShapeDtypeStruct((N_HEADS, s, 128), f32)],
        scratch_shapes=[pltpu.VMEM((hp, t, 128), f32), pltpu.VMEM((hp, t, HEAD_LANES), f32)],
        operands=(q, k, v), relay_at=(N_HEADS // hp - 1, max(nb - 2, 0)))


def _ssd_param_specs():
    return [_const((4, D_CONV)), _const((1, D_CONV)), _const((1, 128)), _const((1, 128)), _const((1, 128)),
            _const((1, D_SSD))]


def ssd_fwd(px, pz, plast, params):
    s = px.shape[0]
    nc = s // CHUNK

    def body(px_ref, pz_ref, pl_ref, cw_ref, cb_ref, dtb_ref, alog_ref, dskip_ref, snw_ref, yg_ref, st_ref,
             state_sc, halo_sc):
        i = pl.program_id(0)

        @pl.when(i == 0)
        def _():
            state_sc[...] = jnp.zeros(state_sc.shape, f32)
            halo_sc[...] = jnp.zeros(halo_sc.shape, f32)

        x = px_ref[...]
        prev = state_sc[...]
        st_ref[...] = prev
        xext = jnp.concatenate([halo_sc[...], x], axis=0)
        yg, new = _f_ssd(xext, pz_ref[...], pl_ref[...], prev, cw_ref[...], cb_ref[...], dtb_ref[...],
                         alog_ref[...], dskip_ref[...], snw_ref[...])
        yg_ref[...] = yg
        state_sc[...] = new
        halo_sc[...] = x[CHUNK - HALO:]

    tok = lambda w: pl.BlockSpec((CHUNK, w), lambda i: (i, 0))
    return pl.pallas_call(
        body, name="ssd_fwd", grid=(nc,),
        in_specs=[tok(D_CONV), tok(D_SSD), tok(128)] + _ssd_param_specs(),
        out_specs=[tok(D_SSD), pl.BlockSpec((None, N_HEADS, SSD_HEAD_DIM, SSD_STATE), lambda i: (i, 0, 0, 0))],
        out_shape=[jax.ShapeDtypeStruct((s, D_SSD), f32),
                   jax.ShapeDtypeStruct((nc, N_HEADS, SSD_HEAD_DIM, SSD_STATE), f32)],
        scratch_shapes=[pltpu.VMEM((N_HEADS, SSD_HEAD_DIM, SSD_STATE), f32), pltpu.VMEM((HALO, D_CONV), f32)],
    )(px, pz, plast, *params)


def out_fwd(x, o, yg, g1, wo):
    s = x.shape[0]
    ts = _token_block(s)

    def body(x_ref, o_ref, yg_ref, g1_ref, wo_ref, out_ref):
        out_ref[...] = x_ref[...] + _f_out(o_ref[...], yg_ref[...], g1_ref[...], wo_ref[...])

    return pl.pallas_call(
        body, name="out_fwd", grid=(s // ts,),
        in_specs=[pl.BlockSpec((ts, D_MODEL), lambda i: (i, 0)), pl.BlockSpec((N_HEADS, ts, V_DIM), lambda i: (0, i, 0)),
                  pl.BlockSpec((ts, D_SSD), lambda i: (i, 0)), _const((1, D_MODEL)), _const((D_MODEL, D_MODEL))],
        out_specs=pl.BlockSpec((ts, D_MODEL), lambda i: (i, 0)),
        out_shape=jax.ShapeDtypeStruct((s, D_MODEL), f32),
    )(x, o, yg, g1, wo)


def mlp_fwd(x, nw, sh, sc, g2, wgu, wd):
    s = x.shape[0]
    ts = _token_block(s)
    nj = N_DEV // 2

    def body(x_ref, nw_ref, sh_ref, sc_ref, g2_ref, wg_ref, wu_ref, wd_ref, out_ref, mix_ref):
        j = pl.program_id(1)
        act = _f_gate_up(x_ref[...], nw_ref[...], sh_ref[...], sc_ref[...], wg_ref[...], wu_ref[...])
        _accumulate(j == 0, [mix_ref], [mmw(act, wd_ref[...])])

        @pl.when(j == nj - 1)
        def _():
            out_ref[...] = x_ref[...] + g2_ref[...] * mix_ref[...]

    vec = _const((1, D_MODEL))
    return pl.pallas_call(
        body, name="mlp_fwd", grid=(s // ts, nj),
        in_specs=[pl.BlockSpec((ts, D_MODEL), lambda i, j: (i, 0)), vec, vec, vec, vec,
                  pl.BlockSpec((None, D_MODEL, FF_SHARD), lambda i, j: (j, 0, 0)),
                  pl.BlockSpec((None, D_MODEL, FF_SHARD), lambda i, j: (j + nj, 0, 0)),
                  pl.BlockSpec((None, FF_SHARD, D_MODEL), lambda i, j: (j, 0, 0))],
        out_specs=[pl.BlockSpec((ts, D_MODEL), lambda i, j: (i, 0))] * 2,
        out_shape=[jax.ShapeDtypeStruct((s, D_MODEL), f32)] * 2,
    )(x, nw, sh, sc, g2, wgu, wgu, wd)


def loss_fwd(y, target):
    s = y.shape[0]
    ts = _token_block(s)

    def body(y_ref, t_ref, dy_ref, loss_ref):
        d = y_ref[...] - t_ref[...]
        dy_ref[...] = d * (1.0 / D_MODEL)
        part = 0.5 * jnp.sum(jnp.sum(d * d, axis=-1, keepdims=True) * (1.0 / D_MODEL), axis=0, keepdims=True)
        _accumulate(pl.program_id(0) == 0, [loss_ref], [jnp.broadcast_to(part, (8, 128))])

    return pl.pallas_call(
        body, name="loss_fwd", grid=(s // ts,),
        in_specs=[pl.BlockSpec((ts, D_MODEL), lambda i: (i, 0))] * 2,
        out_specs=[pl.BlockSpec((ts, D_MODEL), lambda i: (i, 0)), _const((8, 128))],
        out_shape=[jax.ShapeDtypeStruct((s, D_MODEL), f32), jax.ShapeDtypeStruct((8, 128), f32)],
    )(y, target)


def mlp_bwd(x, dy, nw, sh, sc, g2, wgu, wd, job=None):
    s = x.shape[0]
    ts = min(MLP_BWD_ROWS, s)
    nj = N_DEV // 2
    ni = s // ts

    def body(x_ref, dy_ref, nw_ref, sh_ref, sc_ref, g2_ref, wg_ref, wu_ref, wd_ref,
             dx_ref, dnw_ref, dsh_ref, dsc_ref, dwg_ref, dwu_ref, dwd_ref, ag_sc, au_sc, ad_sc):
        j, i = pl.program_id(0), pl.program_id(1)
        wg, wu, wd = wg_ref[...], wu_ref[...], wd_ref[...]
        act, vjp = jax.vjp(lambda x_, nw_, sh_, sc_, sg, su: _f_gate_up(x_, nw_, sh_, sc_, wg, wu, sg, su),
                           x_ref[...], nw_ref[...], sh_ref[...], sc_ref[...],
                           jnp.zeros(wg.shape, f32), jnp.zeros(wu.shape, f32))
        dmix = dy_ref[...] * g2_ref[...]
        dact = _dot(dmix, wd, 1, 1)
        dwd = _dot(act, dmix, 0, 0)
        dx, dnw, dsh, dsc, dwg, dwu = vjp(dact)
        dx_ref[...] = dx.astype(bf16)
        _accumulate((i == 0) & (j == 0), [dnw_ref, dsh_ref, dsc_ref], [dnw, dsh, dsc])
        _accumulate_then_cast(i == 0, i == ni - 1, [ag_sc, au_sc, ad_sc], [dwg_ref, dwu_ref, dwd_ref], [dwg, dwu, dwd])

    vec = _const((1, D_MODEL))
    vshape = jax.ShapeDtypeStruct((1, D_MODEL), f32)
    once = pl.Buffered(1)
    wspec = lambda off: pl.BlockSpec((None, D_MODEL, FF_SHARD), lambda j, i: (j + off, 0, 0), pipeline_mode=once)
    dspec = pl.BlockSpec((None, FF_SHARD, D_MODEL), lambda j, i: (j, 0, 0), pipeline_mode=once)
    return _call_with_job(
        body, "mlp_bwd" if job is None else "mlp_bwd_comm", (nj, ni), job,
        in_specs=[pl.BlockSpec((ts, D_MODEL), lambda j, i: (i, 0)), pl.BlockSpec((ts, D_MODEL), lambda j, i: (i, 0)),
                  vec, vec, vec, vec, wspec(0), wspec(nj), dspec],
        out_specs=[pl.BlockSpec((None, ts, D_MODEL), lambda j, i: (j, i, 0)), vec, vec, vec,
                   wspec(0), wspec(0), dspec],
        out_shape=[jax.ShapeDtypeStruct((nj, s, D_MODEL), bf16), vshape, vshape, vshape,
                   jax.ShapeDtypeStruct((nj, D_MODEL, FF_SHARD), bf16), jax.ShapeDtypeStruct((nj, D_MODEL, FF_SHARD), bf16),
                   jax.ShapeDtypeStruct((nj, FF_SHARD, D_MODEL), bf16)],
        scratch_shapes=[pltpu.VMEM((D_MODEL, FF_SHARD), f32), pltpu.VMEM((D_MODEL, FF_SHARD), f32),
                        pltpu.VMEM((FF_SHARD, D_MODEL), f32)],
        operands=(x, dy, nw, sh, sc, g2, wgu, wgu, wd))


def out_bwd(dy, dparts, mix, o, yg, g1, wo):
    s = dy.shape[0]
    ts = _token_block(s)
    nj = dparts.shape[0]

    ni = s // ts

    def body(dy_ref, dp_ref, mix_ref, o_ref, yg_ref, g1_ref, wo_ref, dx_ref, do_ref, delta_ref, dyg_ref, dg1_ref,
             dg2_ref, dwo_ref, acc_sc):
        i = pl.program_id(0)
        g = dy_ref[...]
        _accumulate(i == 0, [dg2_ref], [jnp.sum(g * mix_ref[...], axis=0, keepdims=True)])
        for j in range(nj):
            g = g + dp_ref[j].astype(f32)
        dx_ref[...] = g
        o = o_ref[...]
        wo = wo_ref[...]
        _, vjp = jax.vjp(lambda o_, yg_, g1_, slot: _f_out(o_, yg_, g1_, wo, slot), o, yg_ref[...], g1_ref[...],
                         jnp.zeros(wo.shape, f32))
        do, dyg, dg1, dwo = vjp(g)
        do_ref[...] = do.astype(bf16)
        dyg_ref[...] = dyg
        delta_ref[...] = jnp.broadcast_to(jnp.sum(do * o, axis=-1, keepdims=True), delta_ref.shape)
        _accumulate(i == 0, [dg1_ref], [dg1])
        _accumulate_then_cast(i == 0, i == ni - 1, [acc_sc], [dwo_ref], [dwo])

    head = pl.BlockSpec((N_HEADS, ts, V_DIM), lambda i: (0, i, 0))
    return pl.pallas_call(
        body, name="out_bwd", grid=(ni,), scratch_shapes=[pltpu.VMEM((D_MODEL, D_MODEL), f32)],
        in_specs=[pl.BlockSpec((ts, D_MODEL), lambda i: (i, 0)), pl.BlockSpec((nj, ts, D_MODEL), lambda i: (0, i, 0)),
                  pl.BlockSpec((ts, D_MODEL), lambda i: (i, 0)),
                  head, pl.BlockSpec((ts, D_SSD), lambda i: (i, 0)), _const((1, D_MODEL)), _const((D_MODEL, D_MODEL))],
        out_specs=[pl.BlockSpec((ts, D_MODEL), lambda i: (i, 0)), head,
                   pl.BlockSpec((N_HEADS, ts, 128), lambda i: (0, i, 0)), pl.BlockSpec((ts, D_SSD), lambda i: (i, 0)),
                   _const((1, D_MODEL)), _const((1, D_MODEL)), _const((D_MODEL, D_MODEL))],
        out_shape=[jax.ShapeDtypeStruct((s, D_MODEL), f32), jax.ShapeDtypeStruct((N_HEADS, s, V_DIM), bf16),
                   jax.ShapeDtypeStruct((N_HEADS, s, 128), f32), jax.ShapeDtypeStruct((s, D_SSD), f32),
                   jax.ShapeDtypeStruct((1, D_MODEL), f32), jax.ShapeDtypeStruct((1, D_MODEL), f32),
                   jax.ShapeDtypeStruct((D_MODEL, D_MODEL), bf16)],
    )(dy, dparts, mix, o, yg, g1, wo)


def attn_bwd(q, k, v, do, lse, delta, job=None):
    s = q.shape[1]
    t = _token_block(s)
    nb = s // t

    hp = ATTN_HEADS_BWD

    def body(q_ref, k_ref, v_ref, do_ref, lse_ref, delta_ref, dq_ref, dk_ref, dv_ref):
        ki = pl.program_id(1)

        @pl.when(ki == 0)
        def _():
            dq_ref[...] = jnp.zeros(dq_ref.shape, f32)

        dk_ref[...] = jnp.zeros(dk_ref.shape, f32)
        dv_ref[...] = jnp.zeros(dv_ref.shape, f32)

        def step(q0, diagonal):
            rows = pl.ds(q0, t)

            def products(hh):
                sc = _scores(q_ref[hh, rows, :], k_ref[hh])
                dp = lax.dot_general(do_ref[hh, rows, :], v_ref[hh, :, :V_DIM], (((1,), (1,)), ((), ())),
                                     preferred_element_type=f32)
                return (jnp.where(_tril(t, t, 0), sc, NEG) if diagonal else sc), dp

            ahead = products(0)
            for hh in range(hp):
                sc, dp = ahead
                if hh + 1 < hp:
                    ahead = products(hh + 1)
                p = jnp.exp2(sc - jnp.tile(lse_ref[hh, rows, :], (1, t // 128)))
                ds = (p * (dp - jnp.tile(delta_ref[hh, rows, :], (1, t // 128)))).astype(bf16)
                dv_ref[hh] += lax.dot_general(p.astype(bf16), do_ref[hh, rows, :], (((0,), (0,)), ((), ())),
                                              preferred_element_type=f32)
                dk_ref[hh] += lax.dot_general(ds, q_ref[hh, rows, :], (((0,), (0,)), ((), ())), preferred_element_type=f32)
                dq_ref[hh, rows, :] += jnp.dot(ds, k_ref[hh], preferred_element_type=f32)

        step(pl.multiple_of(ki * t, t), True)

        def above(qi, carry):
            step(pl.multiple_of(qi * t, t), False)
            return carry

        lax.fori_loop(ki + 1, nb, above, 0)
        dk_ref[...] = dk_ref[...] * LN2

        @pl.when(ki == nb - 1)
        def _():
            dq_ref[...] = dq_ref[...] * LN2

    qspec = lambda w: pl.BlockSpec((hp, s, w), lambda h, ki: (h, 0, 0))
    kspec = lambda w: pl.BlockSpec((hp, t, w), lambda h, ki: (h, ki, 0))
    return _call_with_job(
        body, "attn_bwd" if job is None else "attn_bwd_comm", (N_HEADS // hp, nb), job,
        in_specs=[qspec(HEAD_LANES), kspec(HEAD_LANES), kspec(HEAD_LANES), qspec(V_DIM), qspec(128), qspec(128)],
        out_specs=[qspec(HEAD_LANES), kspec(HEAD_LANES), kspec(V_DIM)],
        out_shape=[jax.ShapeDtypeStruct((N_HEADS, s, HEAD_LANES), f32), jax.ShapeDtypeStruct((N_HEADS, s, HEAD_LANES), f32),
                   jax.ShapeDtypeStruct((N_HEADS, s, V_DIM), f32)],
        scratch_shapes=[], operands=(q, k, v, do, lse, delta))


def ssd_bwd(px, pz, plast, states, dyg, params):
    s = px.shape[0]
    nc = s // CHUNK
    per = CHUNK // HALO

    def body(px_ref, halo_ref, pz_ref, pl_ref, st_ref, dyg_ref, cw_ref, cb_ref, dtb_ref, alog_ref, dskip_ref, snw_ref,
             dpx_ref, dpz_ref, dpl_ref, dcw_ref, dcb_ref, ddtb_ref, dalog_ref, ddskip_ref, dsnw_ref, dstate_sc, dhalo_sc):
        t = pl.program_id(0)
        chunk = nc - 1 - t

        @pl.when(t == 0)
        def _():
            dstate_sc[...] = jnp.zeros(dstate_sc.shape, f32)
            dhalo_sc[...] = jnp.zeros(dhalo_sc.shape, f32)

        halo = jnp.where(chunk > 0, halo_ref[...], 0.0)
        xext = jnp.concatenate([halo, px_ref[...]], axis=0)
        _, vjp = jax.vjp(_f_ssd, xext, pz_ref[...], pl_ref[...], st_ref[...], cw_ref[...], cb_ref[...], dtb_ref[...],
                         alog_ref[...], dskip_ref[...], snw_ref[...])
        dxext, dz, dpl, dprev, dcw, dcb, ddtb, dalog, ddskip, dsnw = vjp((dyg_ref[...], dstate_sc[...]))
        dpx_ref[...] = dxext[HALO:]
        dpx_ref[CHUNK - HALO:, :] += dhalo_sc[...]
        dhalo_sc[...] = dxext[:HALO]
        dstate_sc[...] = dprev
        dpz_ref[...] = dz
        dpl_ref[...] = dpl
        _accumulate(t == 0, [dcw_ref, dcb_ref, ddtb_ref, dalog_ref, ddskip_ref, dsnw_ref],
                    [dcw, dcb, ddtb, dalog, ddskip, dsnw])

    rev = lambda w: pl.BlockSpec((CHUNK, w), lambda t: (nc - 1 - t, 0))
    pshapes = [jax.ShapeDtypeStruct((4, D_CONV), f32), jax.ShapeDtypeStruct((1, D_CONV), f32),
               jax.ShapeDtypeStruct((1, 128), f32), jax.ShapeDtypeStruct((1, 128), f32),
               jax.ShapeDtypeStruct((1, 128), f32), jax.ShapeDtypeStruct((1, D_SSD), f32)]
    return pl.pallas_call(
        body, name="ssd_bwd", grid=(nc,),
        in_specs=[rev(D_CONV),
                  pl.BlockSpec((HALO, D_CONV), lambda t: (jnp.maximum((nc - 1 - t) * per - 1, 0), 0)),
                  rev(D_SSD), rev(128),
                  pl.BlockSpec((None, N_HEADS, SSD_HEAD_DIM, SSD_STATE), lambda t: (nc - 1 - t, 0, 0, 0)),
                  rev(D_SSD)] + _ssd_param_specs(),
        out_specs=[rev(D_CONV), rev(D_SSD), rev(128)] + _ssd_param_specs(),
        out_shape=[jax.ShapeDtypeStruct((s, D_CONV), f32), jax.ShapeDtypeStruct((s, D_SSD), f32),
                   jax.ShapeDtypeStruct((s, 128), f32)] + pshapes,
        scratch_shapes=[pltpu.VMEM((N_HEADS, SSD_HEAD_DIM, SSD_STATE), f32), pltpu.VMEM((HALO, D_CONV), f32)],
    )(px, px, pz, plast, states, dyg, *params)


def qkv_bwd(pa, plast, cos_t, sin_t, params, dq, dk, dv):
    s = pa.shape[0]
    ts = _token_block(s)

    def body(pa_ref, pl_ref, cos_ref, sin_ref, *rest):
        qaw, kvaw, wq, wk, wv, qnw, knw, kpw = [r[...] for r in rest[:8]]
        dq_ref, dk_ref, dv_ref = rest[8:11]
        dpa_ref, dpl_ref = rest[11:13]
        dprm_refs = list(rest[13:])
        cos_t, sin_t = cos_ref[...], sin_ref[...]

        def stage(pa_, pl_, qaw_, kvaw_, sq, sk, sv, qnw_, knw_, kpw_):
            return _f_qkv(pa_, pl_, cos_t, sin_t, qaw_, kvaw_, wq, wk, wv, qnw_, knw_, kpw_, (sq, sk, sv))

        _, vjp = jax.vjp(stage, pa_ref[...], pl_ref[...], qaw, kvaw, jnp.zeros(wq.shape, f32), jnp.zeros(wk.shape, f32),
                         jnp.zeros(wv.shape, f32), qnw, knw, kpw)
        grads = vjp((dq_ref[...], dk_ref[...], dv_ref[...]))
        dpa_ref[...] = grads[0]
        dpl_ref[...] = grads[1]
        _accumulate(pl.program_id(0) == 0, dprm_refs, list(grads[2:]))

    tok = lambda w: pl.BlockSpec((ts, w), lambda i: (i, 0))
    head = lambda w: pl.BlockSpec((N_HEADS, ts, w), lambda i: (0, i, 0))
    pshapes = [jax.ShapeDtypeStruct((1, Q_RANK), f32), jax.ShapeDtypeStruct((1, KV_RANK), f32),
               jax.ShapeDtypeStruct((N_HEADS, Q_RANK, HEAD_LANES), f32), jax.ShapeDtypeStruct((N_HEADS, KV_RANK, HEAD_LANES), f32),
               jax.ShapeDtypeStruct((N_HEADS, KV_RANK, V_DIM), f32), jax.ShapeDtypeStruct((1, HEAD_LANES), f32),
               jax.ShapeDtypeStruct((1, HEAD_LANES), f32), jax.ShapeDtypeStruct((1, HEAD_LANES), f32)]
    return pl.pallas_call(
        body, name="qkv_bwd", grid=(s // ts,),
        in_specs=[tok(384), tok(128), tok(128), tok(128)] + _qkv_param_specs()
                 + [head(HEAD_LANES), head(HEAD_LANES), head(V_DIM)],
        out_specs=[tok(384), tok(128)] + _qkv_param_specs(),
        out_shape=[jax.ShapeDtypeStruct((s, 384), f32), jax.ShapeDtypeStruct((s, 128), f32)] + pshapes,
    )(pa, plast, cos_t, sin_t, *params, dq, dk, dv)


def proj_bwd(x, nw, sh, sc, w, dpa, dpz, dpx, dpl_k, dpl_dt, dres):
    s = x.shape[0]
    ts = _token_block(s)

    ni = s // ts

    def body(x_ref, nw_ref, sh_ref, sc_ref, w_ref, dpa_ref, dpz_ref, dpx_ref, dplk_ref, dpld_ref, dres_ref,
             dx_ref, dnw_ref, dsh_ref, dsc_ref, dw_ref, acc_sc):
        i = pl.program_id(0)
        g = jnp.concatenate([dpa_ref[...], dpz_ref[...], dpx_ref[...], dplk_ref[...] + dpld_ref[...]], axis=1)
        w = w_ref[...]
        _, vjp = jax.vjp(lambda x_, nw_, sh_, sc_, slot: _f_proj(x_, nw_, sh_, sc_, w, slot), x_ref[...], nw_ref[...],
                         sh_ref[...], sc_ref[...], jnp.zeros(w.shape, f32))
        dx, dnw, dsh, dsc, dw = vjp(g)
        dx_ref[...] = dx + dres_ref[...]
        _accumulate(i == 0, [dnw_ref, dsh_ref, dsc_ref], [dnw, dsh, dsc])
        _accumulate_then_cast(i == 0, i == ni - 1, [acc_sc], [dw_ref], [dw])

    vec = _const((1, D_MODEL))
    vshape = jax.ShapeDtypeStruct((1, D_MODEL), f32)
    tok = lambda w_: pl.BlockSpec((ts, w_), lambda i: (i, 0))
    return pl.pallas_call(
        body, name="proj_bwd", grid=(ni,), scratch_shapes=[pltpu.VMEM((D_MODEL, D_PROJ), f32)],
        in_specs=[tok(D_MODEL), vec, vec, vec, _const((D_MODEL, D_PROJ)), tok(384), tok(512), tok(1024), tok(128), tok(128),
                  tok(D_MODEL)],
        out_specs=[tok(D_MODEL), vec, vec, vec, _const((D_MODEL, D_PROJ))],
        out_shape=[jax.ShapeDtypeStruct((s, D_MODEL), f32), vshape, vshape, vshape,
                   jax.ShapeDtypeStruct((D_MODEL, D_PROJ), bf16)],
    )(x, nw, sh, sc, w, dpa, dpz, dpx, dpl_k, dpl_dt, dres)


def ada_fwd(c_all, w_ada, b_cols):
    def body(c_ref, w_ref, b_ref, out_ref):
        act = jax.nn.silu(c_ref[...])
        for l in range(2):
            out_ref[l] = jnp.dot(act, w_ref[l], precision=lax.Precision.HIGHEST, preferred_element_type=f32) + b_ref[l]

    return pl.pallas_call(body, name="ada_fwd", out_shape=jax.ShapeDtypeStruct((2, N_DEV, 768), f32))(c_all, w_ada, b_cols)


def ada_bwd(c_all, dmod_cols):
    def body(c_ref, d_ref, out_ref):
        out_ref[0] = lax.dot_general(jax.nn.silu(c_ref[...]), d_ref[0], (((0,), (0,)), ((), ())),
                                     precision=lax.Precision.HIGHEST, preferred_element_type=f32)

    return pl.pallas_call(
        body, name="ada_bwd", grid=(2,),
        in_specs=[_const((N_DEV, D_MODEL)), pl.BlockSpec((1, N_DEV, 768), lambda l: (l, 0, 0))],
        out_specs=pl.BlockSpec((1, D_MODEL, 768), lambda l: (l, 0, 0)),
        out_shape=jax.ShapeDtypeStruct((2, D_MODEL, 768), f32),
    )(c_all, dmod_cols)


def _adamw(w, g, m, v):
    m = ADAM_B1 * m + (1.0 - ADAM_B1) * g
    v = ADAM_B2 * v + (1.0 - ADAM_B2) * (g * g)
    m_hat = m / (1.0 - ADAM_B1 ** ADAM_STEP)
    v_hat = v / (1.0 - ADAM_B2 ** ADAM_STEP)
    delta = -ADAM_LR * (m_hat / (jnp.sqrt(v_hat) + ADAM_EPS) + ADAM_WD * w)
    return delta, m, v


def adamw(parts, w, m, v, layer, prev, name):
    n, r, c = parts.shape
    nl = w.shape[0]
    tr = r
    lanes = -(-c // 128) * 128
    if 2 * (n + 7) * r * lanes * 4 > ADAMW_BLOCK_BYTES:
        tr = next(t for t in (256, 128, 64, 32, 16, 8) if r % t == 0)

    def body(p_ref, w_ref, m_ref, v_ref, *rest):
        g_ref, d_ref, nm_ref, nv_ref = rest[-4:]
        g = p_ref[0].astype(f32)
        for k in range(1, n):
            g = g + p_ref[k].astype(f32)
        delta, nm, nv = _adamw(w_ref[...], g, m_ref[...], v_ref[...])
        g_ref[...] = g
        d_ref[...] = delta
        nm_ref[...] = nm
        nv_ref[...] = nv

    blk = pl.BlockSpec((None, tr, c), lambda i: (layer, i, 0))
    shp = jax.ShapeDtypeStruct((nl, r, c), f32)
    kept = [] if prev is None else list(prev)
    return pl.pallas_call(
        body, name=name, grid=(r // tr,),
        in_specs=[pl.BlockSpec((n, tr, c), lambda i: (0, i, 0)), blk, blk, blk] + [ANY] * len(kept),
        out_specs=[blk] * 4, out_shape=[shp] * 4,
        input_output_aliases={4 + j: j for j in range(len(kept))},
    )(parts, w, m, v, *kept)


def _my_index():
    return 4 * lax.axis_index("x") + 2 * lax.axis_index("y") + lax.axis_index("c")


def _coords(idx):
    return (idx // 4, (idx // 2) % 2, idx % 2)


class CommJob:
    def __init__(self, operands, out_shape, phases, scratch):
        self.operands, self.out_shape, self.phases, self.scratch = operands, out_shape, phases, scratch


def _wait(out, n_blocks, send_sem, recv_sem, send=True, recv=True):
    span = out.at[pl.ds(0, n_blocks)]
    desc = pltpu.make_async_remote_copy(src_ref=span, dst_ref=span, send_sem=send_sem, recv_sem=recv_sem,
                                        device_id=_coords(_my_index()), device_id_type=MESH)
    if recv:
        desc.wait_recv()
    if send:
        desc.wait_send()


def gather_job(shards):
    n = len(shards)

    def places():
        x, y, c = lax.axis_index("x"), lax.axis_index("y"), lax.axis_index("c")
        return (x, y, c), (x, y, 1 - c), [(1 - x, y), (x, 1 - y), (1 - x, 1 - y)]

    def index(p):
        return 4 * p[0] + 2 * p[1] + p[2]

    def start(ins, outs, sems):
        far_send, far_recv, near_send, near_recv, local = sems
        me, sibling, chips = places()
        for k in range(n):
            pltpu.make_async_copy(ins[k], outs[k].at[index(me)], local.at[k]).start()
            for chip in chips:
                pltpu.make_async_remote_copy(src_ref=ins[k], dst_ref=outs[k].at[index(me)], send_sem=far_send.at[k],
                                             recv_sem=far_recv.at[k], device_id=(*chip, me[2]), device_id_type=MESH).start()
            pltpu.make_async_remote_copy(src_ref=ins[k], dst_ref=outs[k].at[index(me)], send_sem=near_send.at[k],
                                         recv_sem=near_recv.at[k], device_id=sibling, device_id_type=MESH).start()

    def relay(ins, outs, sems):
        far_send, far_recv, near_send, near_recv, local = sems
        me, sibling, chips = places()
        for k in range(n):
            _wait(outs[k], 3, far_send.at[k], far_recv.at[k], send=False)
            for chip in chips:
                block = outs[k].at[index((*chip, me[2]))]
                pltpu.make_async_remote_copy(src_ref=block, dst_ref=block, send_sem=near_send.at[k],
                                             recv_sem=near_recv.at[k], device_id=sibling, device_id_type=MESH).start()

    def finish(ins, outs, sems):
        far_send, far_recv, near_send, near_recv, local = sems
        for k in range(n):
            _wait(outs[k], 4, near_send.at[k], near_recv.at[k])
            _wait(outs[k], 3, far_send.at[k], far_recv.at[k], recv=False)
            pltpu.make_async_copy(ins[k], outs[k].at[0], local.at[k]).wait()

    shapes = [jax.ShapeDtypeStruct((N_DEV,) + tuple(a.shape), a.dtype) for a in shards]
    return CommJob(list(shards), shapes, [start, relay, finish], [pltpu.SemaphoreType.DMA((n,))] * 5)


def scatter_job(tensors):
    n = len(tensors)
    flat, where = [], {}
    for k, pieces in enumerate(tensors):
        d = 0
        for piece in pieces:
            for b in range(piece.shape[0]):
                where[k, d] = (len(flat), b)
                d += 1
            flat.append(piece)
        assert d == N_DEV

    def start(ins, outs, sems):
        send_sems, recv_sems, local_sems = sems
        me = _my_index()

        def block(k, d):
            i, b = where[k, d]
            return ins[i].at[b]

        for d in range(N_DEV):
            @pl.when(d != me)
            def _():
                for k in range(n):
                    pltpu.make_async_remote_copy(src_ref=block(k, d), dst_ref=outs[k].at[me], send_sem=send_sems.at[k],
                                                 recv_sem=recv_sems.at[k], device_id=(d // 4, (d // 2) % 2, d % 2),
                                                 device_id_type=MESH).start()

            @pl.when(d == me)
            def _():
                for k in range(n):
                    pltpu.make_async_copy(block(k, d), outs[k].at[d], local_sems.at[k]).start()

    def finish(ins, outs, sems):
        send_sems, recv_sems, local_sems = sems
        for k in range(n):
            _wait(outs[k], N_DEV - 1, send_sems.at[k], recv_sems.at[k])
            i, b = where[k, 0]
            pltpu.make_async_copy(ins[i].at[b], outs[k].at[0], local_sems.at[k]).wait()

    shapes = [jax.ShapeDtypeStruct((N_DEV,) + tuple(p[0].shape[1:]), p[0].dtype) for p in tensors]
    return CommJob(flat, shapes, [start, finish], [pltpu.SemaphoreType.DMA((n,))] * 3)


def comm_call(job, name):
    ni, no = len(job.operands), len(job.out_shape)

    def body(*refs):
        ins, outs, sems = refs[:ni], refs[ni:ni + no], refs[ni + no:]
        for phase in job.phases:
            phase(ins, outs, sems)

    return pl.pallas_call(body, name=name, in_specs=[ANY] * ni, out_specs=[ANY] * no, out_shape=job.out_shape,
                          scratch_shapes=job.scratch)(*job.operands)


def _carry(job, body, n_in, n_out, at_step):
    ji, jo, js = len(job.operands), len(job.out_shape), len(job.scratch)

    def carrier(*refs):
        a, b = n_in, n_in + ji
        c, d = b + n_out, b + n_out + jo
        e = len(refs) - js
        job_refs = (refs[a:b], refs[c:d], refs[e:])
        n = len(job.phases)

        @pl.when(at_step(0, n))
        def _():
            job.phases[0](*job_refs)

        body(*refs[:a], *refs[b:c], *refs[d:e])

        for i in range(1, n):
            @pl.when(at_step(i, n))
            def _():
                job.phases[i](*job_refs)

    return carrier


def _pad_lanes(v, lo, total=128):
    return jnp.pad(v, (lo, total - lo - v.shape[0]))[None, :]


MIXER_WEIGHTS = ("w_in", "w_q_up", "w_kv_up", "conv_w")
LATE_WEIGHTS = ("w_out", "w_gate_up", "w_down")


def mixer_operands(g, sw):
    w_in = g["w_in"].transpose(1, 0, 2).reshape(D_MODEL, D_IN)
    z = jnp.zeros((D_MODEL, 1), w_in.dtype)
    w_proj = jnp.concatenate(
        [w_in[:, :384], w_in[:, 416:928], w_in[:, 928:1952], w_in[:, 1952:1960], jnp.tile(z, (1, 56)),
         w_in[:, 384:416], jnp.tile(z, (1, 32))], axis=1)
    wq = jnp.pad(g["w_q_up"], ((0, 0), (0, 0), (0, HEAD_LANES - NOPE - ROPE)))
    wk = jnp.pad(g["w_kv_up"][:, :, :NOPE], ((0, 0), (0, 0), (0, HEAD_LANES - NOPE)))
    wv = g["w_kv_up"][:, :, NOPE:]
    qkv = (sw["q_a_norm_w"][None, :], sw["kv_a_norm_w"][None, :], wq, wk, wv,
           _pad_lanes(jnp.concatenate([sw["q_nope_norm_w"], sw["q_pe_norm_w"]]), 0),
           _pad_lanes(sw["k_nope_norm_w"], 0), _pad_lanes(sw["k_pe_norm_w"], NOPE))
    conv_w = g["conv_w"].astype(f32).transpose(1, 0, 2).reshape(4, D_CONV)
    ssd = (conv_w, sw["conv_b"][None, :], _pad_lanes(sw["dt_bias"], 0), _pad_lanes(sw["a_log"], 0),
           _pad_lanes(sw["d_skip"], 0), sw["ssd_norm_w"][None, :])
    return dict(w_proj=w_proj, qkv=qkv, ssd=ssd, n1=sw["norm1_w"][None, :])


def late_operands(g, sw):
    return dict(wo=g["w_out"].reshape(D_MODEL, D_MODEL), wgu=g["w_gate_up"],
                wd=g["w_down"].reshape(N_DEV // 2, FF_SHARD, D_MODEL), n2=sw["norm2_w"][None, :])


def layer_fwd(x, mod, kw, cos_t, sin_t, job=None, late=None):
    sh1, sc1, g1, sh2, sc2, g2 = [mod[i:i + 1] for i in range(6)]
    pa, pz, px, plast = proj_fwd(x, kw["n1"], sh1, sc1, kw["w_proj"])
    q, k, v = qkv_fwd(pa, plast, cos_t, sin_t, kw["qkv"])
    (o, lse), carried = attn_fwd(q, k, v, job)
    if late is not None:
        kw = {**kw, **late(carried)}
    yg, states = ssd_fwd(px, pz, plast, kw["ssd"])
    x_mid = out_fwd(x, o, yg, g1, kw["wo"])
    x_out, mix = mlp_fwd(x_mid, kw["n2"], sh2, sc2, g2, kw["wgu"], kw["wd"])
    saved = dict(x=x, pa=pa, pz=pz, px=px, plast=plast, q=q, k=k, v=v, o=o, lse=lse, yg=yg, states=states, x_mid=x_mid,
                 mix=mix)
    return x_out, saved, kw, carried


def layer_bwd_head(dy, mod, kw, sv, job=None):
    _, _, g1, sh2, sc2, g2 = [mod[i:i + 1] for i in range(6)]
    (dparts, dn2, dsh2, dsc2, dwg, dwu, dwd), carried = mlp_bwd(
        sv["x_mid"], dy, kw["n2"], sh2, sc2, g2, kw["wgu"], kw["wd"], job)
    dmid, do, delta, dyg, dg1, dg2, dwo = out_bwd(dy, dparts, sv["mix"], sv["o"], sv["yg"], g1, kw["wo"])
    early = dict(w_out=[dwo.reshape(N_DEV, D_MODEL // N_DEV, D_MODEL)], w_gate_up=[dwg, dwu],
                 w_down=[dwd.reshape(N_DEV, D_FF // N_DEV, D_MODEL)])
    head = dict(dmid=dmid, do=do, delta=delta, dyg=dyg, dn2=dn2, dsh2=dsh2, dsc2=dsc2, dg2=dg2, dg1=dg1)
    return head, early, carried


def layer_bwd_tail(hd, mod, kw, cos_t, sin_t, sv, job=None):
    sh1, sc1 = mod[0:1], mod[1:2]
    (dq, dk, dv), carried = attn_bwd(sv["q"], sv["k"], sv["v"], hd["do"], sv["lse"], hd["delta"], job)
    dpx, dpz, dpl_dt, dcw, dcb, ddtb, dalog, ddskip, dsnw = ssd_bwd(sv["px"], sv["pz"], sv["plast"], sv["states"],
                                                                   hd["dyg"], kw["ssd"])
    dpa, dpl_k, dqaw, dkvaw, dwq, dwk, dwv, dqnw, dknw, dkpw = qkv_bwd(sv["pa"], sv["plast"], cos_t, sin_t, kw["qkv"],
                                                                       dq, dk, dv)
    dx, dn1, dsh1, dsc1, dwp = proj_bwd(sv["x"], kw["n1"], sh1, sc1, kw["w_proj"], dpa, dpz, dpx, dpl_k, dpl_dt, hd["dmid"])
    dmod = jnp.concatenate([dsh1, dsc1, hd["dg1"], hd["dsh2"], hd["dsc2"], hd["dg2"]], axis=0)
    dw_in = jnp.concatenate([dwp[:, :384], dwp[:, 1984:2016], dwp[:, 384:1920], dwp[:, 1920:1928]], axis=1)
    grads = dict(
        norm1_w=dn1[0], norm2_w=hd["dn2"][0], q_a_norm_w=dqaw[0], kv_a_norm_w=dkvaw[0],
        q_nope_norm_w=dqnw[0, :NOPE], q_pe_norm_w=dqnw[0, NOPE:NOPE + ROPE], k_nope_norm_w=dknw[0, :NOPE],
        k_pe_norm_w=dkpw[0, NOPE:NOPE + ROPE], conv_b=dcb[0], dt_bias=ddtb[0, :N_HEADS], a_log=dalog[0, :N_HEADS],
        d_skip=ddskip[0, :N_HEADS], ssd_norm_w=dsnw[0],
        w_in=[dw_in.reshape(D_MODEL, N_DEV, D_IN // N_DEV).transpose(1, 0, 2)],
        w_q_up=[dwq[:, :, :NOPE + ROPE].astype(bf16)],
        w_kv_up=[jnp.concatenate([dwk[:, :, :NOPE], dwv], axis=2).astype(bf16)],
        conv_w=[dcw.reshape(4, N_DEV, D_CONV // N_DEV).transpose(1, 0, 2).astype(bf16)],
    )
    return dx, dmod, grads, carried


def _pack_small(get, last=None):
    flat = jnp.concatenate([get(name).reshape(-1) for name, _ in SMALL])
    flat = jnp.pad(flat, (0, SMALL_ROWS * 128 - flat.shape[0]))
    if last is not None:
        flat = flat.at[-1].set(last)
    return flat.reshape(SMALL_ROWS, 128)


def _unpack_small(packed):
    flat = packed.reshape(-1)
    out, off = {}, 0
    for name, size in SMALL:
        out[name] = flat[off:off + 2 * size].reshape(2, size)
        off += 2 * size
    return out


def kernel(x, c, positions, norm1_w, norm2_w, w_ada, b_ada, w_in, q_a_norm_w, w_q_up, kv_a_norm_w, w_kv_up, q_nope_norm_w, q_pe_norm_w, k_nope_norm_w, k_pe_norm_w, conv_w, conv_b, dt_bias, a_log, d_skip, ssd_norm_w, w_out, w_gate_up, w_down, loss_target, m_norm1_w, m_norm2_w, m_w_ada, m_b_ada, m_w_in, m_q_a_norm_w, m_w_q_up, m_kv_a_norm_w, m_w_kv_up, m_q_nope_norm_w, m_q_pe_norm_w, m_k_nope_norm_w, m_k_pe_norm_w, m_conv_w, m_conv_b, m_dt_bias, m_a_log, m_d_skip, m_ssd_norm_w, m_w_out, m_w_gate_up, m_w_down, v_norm1_w, v_norm2_w, v_w_ada, v_b_ada, v_w_in, v_q_a_norm_w, v_w_q_up, v_kv_a_norm_w, v_w_kv_up, v_q_nope_norm_w, v_q_pe_norm_w, v_k_nope_norm_w, v_k_pe_norm_w, v_conv_w, v_conv_b, v_dt_bias, v_a_log, v_d_skip, v_ssd_norm_w, v_w_out, v_w_gate_up, v_w_down):
    w = dict(norm1_w=norm1_w, norm2_w=norm2_w, w_ada=w_ada, b_ada=b_ada, w_in=w_in, q_a_norm_w=q_a_norm_w, w_q_up=w_q_up,
             kv_a_norm_w=kv_a_norm_w, w_kv_up=w_kv_up, q_nope_norm_w=q_nope_norm_w, q_pe_norm_w=q_pe_norm_w,
             k_nope_norm_w=k_nope_norm_w, k_pe_norm_w=k_pe_norm_w, conv_w=conv_w, conv_b=conv_b, dt_bias=dt_bias,
             a_log=a_log, d_skip=d_skip, ssd_norm_w=ssd_norm_w, w_out=w_out, w_gate_up=w_gate_up, w_down=w_down)
    m = dict(norm1_w=m_norm1_w, norm2_w=m_norm2_w, w_ada=m_w_ada, b_ada=m_b_ada, w_in=m_w_in, q_a_norm_w=m_q_a_norm_w,
             w_q_up=m_w_q_up, kv_a_norm_w=m_kv_a_norm_w, w_kv_up=m_w_kv_up, q_nope_norm_w=m_q_nope_norm_w,
             q_pe_norm_w=m_q_pe_norm_w, k_nope_norm_w=m_k_nope_norm_w, k_pe_norm_w=m_k_pe_norm_w, conv_w=m_conv_w,
             conv_b=m_conv_b, dt_bias=m_dt_bias, a_log=m_a_log, d_skip=m_d_skip, ssd_norm_w=m_ssd_norm_w, w_out=m_w_out,
             w_gate_up=m_w_gate_up, w_down=m_w_down)
    v = dict(norm1_w=v_norm1_w, norm2_w=v_norm2_w, w_ada=v_w_ada, b_ada=v_b_ada, w_in=v_w_in, q_a_norm_w=v_q_a_norm_w,
             w_q_up=v_w_q_up, kv_a_norm_w=v_kv_a_norm_w, w_kv_up=v_w_kv_up, q_nope_norm_w=v_q_nope_norm_w,
             q_pe_norm_w=v_q_pe_norm_w, k_nope_norm_w=v_k_nope_norm_w, k_pe_norm_w=v_k_pe_norm_w, conv_w=v_conv_w,
             conv_b=v_conv_b, dt_bias=v_dt_bias, a_log=v_a_log, d_skip=v_d_skip, ssd_norm_w=v_ssd_norm_w, w_out=v_w_out,
             w_gate_up=v_w_gate_up, w_down=v_w_down)
    me = _my_index()
    seq = x.shape[1]

    def shards(names, l):
        return [w[name][l] if name == "conv_w" else w[name][l].astype(bf16) for name in names]

    small = [{name: w[name][l] for name, _ in SMALL if name != "b_ada"} for l in range(2)]
    n_mix, n_late = len(MIXER_WEIGHTS), len(LATE_WEIGHTS)

    first = comm_call(gather_job([c] + shards(MIXER_WEIGHTS, 0)), "gather_first")
    c_all = first[0].reshape(N_DEV, D_MODEL)
    kws = [mixer_operands(dict(zip(MIXER_WEIGHTS, first[1:])), small[0]), None]

    b_cols = lax.dynamic_slice_in_dim(b_ada, me * 768, 768, axis=1)
    mod_cols = ada_fwd(c_all, w_ada, b_cols)
    (mod_all,) = comm_call(gather_job([mod_cols]), "gather_mod")
    mod_me = lax.dynamic_index_in_dim(mod_all, me, axis=2, keepdims=False)
    mods = [mod_me[:, l, :].reshape(6, D_MODEL) for l in range(2)]

    inv_freq = 1.0 / (ROPE_THETA ** (jnp.arange(0, ROPE, 2, dtype=f32) / ROPE))
    inv = _pad_lanes(jnp.concatenate([inv_freq, inv_freq]), NOPE)
    cos_t, sin_t = rope_tables(positions.reshape(seq, 1), inv)

    saved = [None, None]
    h, saved[0], kws[0], got = layer_fwd(
        x[0], mods[0], kws[0], cos_t, sin_t, gather_job(shards(LATE_WEIGHTS, 0) + shards(MIXER_WEIGHTS, 1)),
        lambda got: late_operands(dict(zip(LATE_WEIGHTS, got[:n_late])), small[0]))
    kws[1] = mixer_operands(dict(zip(MIXER_WEIGHTS, got[n_late:])), small[1])
    h, saved[1], kws[1], _ = layer_fwd(
        h, mods[1], kws[1], cos_t, sin_t, gather_job(shards(LATE_WEIGHTS, 1)),
        lambda got: late_operands(dict(zip(LATE_WEIGHTS, got)), small[1]))
    dy, loss_part = loss_fwd(h, loss_target[0])

    early, late = ("w_out", "w_gate_up", "w_down"), ("w_in", "w_q_up", "w_kv_up", "conv_w")
    parts = [{}, {}]
    head, pieces, _ = layer_bwd_head(dy, mods[1], kws[1], saved[1])
    dy, dmod1, grads1, got = layer_bwd_tail(head, mods[1], kws[1], cos_t, sin_t, saved[1], scatter_job([pieces[n] for n in early]))
    parts[1].update(zip(early, got))
    head, pieces, got = layer_bwd_head(dy, mods[0], kws[0], saved[0], scatter_job([grads1[n] for n in late]))
    parts[1].update(zip(late, got))
    dy, dmod0, grads0, got = layer_bwd_tail(head, mods[0], kws[0], cos_t, sin_t, saved[0], scatter_job([pieces[n] for n in early]))
    parts[0].update(zip(early, got))
    parts[0].update(zip(late, comm_call(scatter_job([grads0[n] for n in late]), "scatter_layer0_rest")))
    grad_x = dy[None]

    small_part = {name: jnp.stack([grads0[name], grads1[name]]) for name, _ in SMALL if name != "b_ada"}
    small_part["b_ada"] = jnp.stack([dmod0.reshape(-1), dmod1.reshape(-1)])
    (small_all,) = comm_call(gather_job([_pack_small(lambda n: small_part[n], loss_part[0, 0])]), "gather_small_grads")
    packed = adamw(small_all, _pack_small(lambda n: w[n])[None], _pack_small(lambda n: m[n])[None],
                   _pack_small(lambda n: v[n])[None], 0, None, "adamw_small")
    loss = packed[0][0, -1, -1]
    res = {}
    for key, arr in zip("gdmv", packed):
        for name, val in _unpack_small(arr[0]).items():
            res[key, name] = val

    off = 2 * (1024 + 1024)
    dmod_all = small_all.reshape(N_DEV, -1)[:, off:off + 2 * 6144].reshape(N_DEV, 2, 6144)
    dmod_cols = lax.dynamic_slice_in_dim(dmod_all, me * 768, 768, axis=2).transpose(1, 0, 2)
    g_ada = ada_bwd(c_all, dmod_cols)
    out = None
    for l in range(2):
        out = adamw(g_ada[l][None], w_ada, m_w_ada, v_w_ada, l, out, "adamw_w_ada")
    res.update(zip([(key, "w_ada") for key in "gdmv"], out))

    for name in BIG:
        out = None
        for l in range(2):
            out = adamw(parts[l][name], w[name], m[name], v[name], l, out, "adamw_" + name)
        res.update(zip([(key, name) for key in "gdmv"], out))

    return (loss, grad_x, *[res["g", n] for n in WEIGHTS], *[res["d", n] for n in WEIGHTS],
            *[res["m", n] for n in WEIGHTS], *[res["v", n] for n in WEIGHTS])
```

```python
import functools

import jax
import jax.numpy as jnp
from jax import lax
from jax.experimental import pallas as pl
from jax.experimental.pallas import tpu as pltpu

f32 = jnp.float32
bf16 = jnp.bfloat16

N_DEV = 8
D_MODEL = 1024
N_HEADS = 8
HEAD_LANES = 128
NOPE = 64
ROPE = 32
V_DIM = 64
Q_RANK = 256
KV_RANK = 128
D_SSD = 512
D_CONV = 1024
SSD_STATE = 128
SSD_HEAD_DIM = 64
CHUNK = 128
HALO = 8
D_FF = 2816
FF_SHARD = 704
D_IN = 1960
D_PROJ = 2048
EPS = 1e-6
LOG2E = 1.4426950408889634
LN2 = 0.6931471805599453
Q_SCALE = (NOPE + ROPE) ** -0.5 * LOG2E
ATTN_ROWS_FWD = 256
ATTN_HEADS_FWD = 4
ATTN_HEADS_BWD = 2
MLP_BWD_ROWS = 512
ROPE_THETA = 10000.0
NEG = -1e30

ADAM_LR = 0.001
ADAM_B1 = 0.9
ADAM_B2 = 0.999
ADAM_EPS = 1e-08
ADAM_WD = 0.01
ADAM_STEP = 10
ADAMW_BLOCK_BYTES = 24 << 20

MESH = pl.DeviceIdType.MESH
ANY = pl.BlockSpec(memory_space=pl.ANY)

SMALL = (("norm1_w", 1024), ("norm2_w", 1024), ("b_ada", 6144), ("q_a_norm_w", 256), ("kv_a_norm_w", 128),
         ("q_nope_norm_w", 64), ("q_pe_norm_w", 32), ("k_nope_norm_w", 64), ("k_pe_norm_w", 32),
         ("conv_b", 1024), ("dt_bias", 8), ("a_log", 8), ("d_skip", 8), ("ssd_norm_w", 512))
SMALL_ROWS = 168
BIG = ("w_in", "w_q_up", "w_kv_up", "conv_w", "w_out", "w_gate_up", "w_down")
TRANSPOSED = ("w_gate_up",)
WEIGHTS = ("norm1_w", "norm2_w", "w_ada", "b_ada", "w_in", "q_a_norm_w", "w_q_up", "kv_a_norm_w", "w_kv_up",
           "q_nope_norm_w", "q_pe_norm_w", "k_nope_norm_w", "k_pe_norm_w", "conv_w", "conv_b", "dt_bias",
           "a_log", "d_skip", "ssd_norm_w", "w_out", "w_gate_up", "w_down")


def _dot(a, b, ca, cb):
    return lax.dot_general(a.astype(bf16), b.astype(bf16), (((ca,), (cb,)), ((), ())), preferred_element_type=f32)


@jax.custom_vjp
def mm(a, b):
    return _dot(a, b, 1, 0)


def _mm_fwd(a, b):
    return _dot(a, b, 1, 0), (a, b)


def _mm_bwd(res, g):
    a, b = res
    return _dot(g, b, 1, 1).astype(a.dtype), _dot(a, g, 0, 0).astype(b.dtype)


mm.defvjp(_mm_fwd, _mm_bwd)


@jax.custom_vjp
def _mm_slot(a, w, slot):
    return _dot(a, w, 1, 0)


def _mm_slot_fwd(a, w, slot):
    return _dot(a, w, 1, 0), (a, w)


def _mm_slot_bwd(res, g):
    a, w = res
    return _dot(g, w, 1, 1).astype(a.dtype), None, _dot(a, g, 0, 0)


_mm_slot.defvjp(_mm_slot_fwd, _mm_slot_bwd)


def mmw(a, w, slot=None):
    return _dot(a, w, 1, 0) if slot is None else _mm_slot(a, w, slot)


@jax.custom_vjp
def _mm_slot_t(a, wt, slot):
    return _dot(a, wt, 1, 1)


def _mm_slot_t_fwd(a, wt, slot):
    return _dot(a, wt, 1, 1), (a, wt)


def _mm_slot_t_bwd(res, g):
    a, wt = res
    return _dot(g, wt, 1, 0).astype(a.dtype), None, _dot(g, a, 0, 0)


_mm_slot_t.defvjp(_mm_slot_t_fwd, _mm_slot_t_bwd)


def mmw_t(a, wt, slot=None):
    return _dot(a, wt, 1, 1) if slot is None else _mm_slot_t(a, wt, slot)


@jax.custom_vjp
def mm_nt(a, b):
    return _dot(a, b, 1, 1)


def _mm_nt_fwd(a, b):
    return _dot(a, b, 1, 1), (a, b)


def _mm_nt_bwd(res, g):
    a, b = res
    return _dot(g, b, 1, 0).astype(a.dtype), _dot(g, a, 0, 0).astype(b.dtype)


mm_nt.defvjp(_mm_nt_fwd, _mm_nt_bwd)


@jax.custom_vjp
def mm_tn(a, b):
    return _dot(a, b, 0, 0)


def _mm_tn_fwd(a, b):
    return _dot(a, b, 0, 0), (a, b)


def _mm_tn_bwd(res, g):
    a, b = res
    return _dot(b, g, 1, 1).astype(a.dtype), _dot(a, g, 1, 0).astype(b.dtype)


mm_tn.defvjp(_mm_tn_fwd, _mm_tn_bwd)


def _rms(x, w):
    return x * lax.rsqrt(jnp.mean(x * x, axis=-1, keepdims=True) + EPS) * w


def _const(shape):
    n = len(shape)
    return pl.BlockSpec(shape, lambda *_: (0,) * n)


def _accumulate(first, refs, vals):
    @pl.when(first)
    def _():
        for r, v in zip(refs, vals):
            r[...] = v

    @pl.when(jnp.logical_not(first))
    def _():
        for r, v in zip(refs, vals):
            r[...] += v


def _accumulate_then_cast(first, last, accs, outs, vals):
    _accumulate(first, accs, vals)

    @pl.when(last)
    def _():
        for a, o in zip(accs, outs):
            o[...] = a[...].astype(o.dtype)


def _token_block(s):
    return min(512, s)


def _f_proj(x, nw, sh, sc, w, slot=None):
    h = _rms(x, nw) * (1.0 + sc) + sh
    return mmw(h, w, slot)


def _f_qkv(pa, plast, cos_t, sin_t, qaw, kvaw, wq, wk, wv, qnw, knw, kpw, slots=None):
    sq, sk, sv = slots if slots is not None else ([None] * N_HEADS,) * 3
    lane = lax.broadcasted_iota(jnp.int32, (1, HEAD_LANES), 1)
    m_nope = lane < NOPE
    m_pe = (lane >= NOPE) & (lane < NOPE + ROPE)
    rows = pa.shape[0]

    def rope(t):
        half = ROPE // 2
        swapped = jnp.concatenate(
            [jnp.zeros((rows, NOPE), f32), t[:, NOPE + half:NOPE + ROPE], t[:, NOPE:NOPE + half],
             jnp.zeros((rows, HEAD_LANES - NOPE - ROPE), f32)], axis=1)
        return t * cos_t + swapped * sin_t

    qa = _rms(pa[:, :Q_RANK], qaw)
    kva = _rms(pa[:, Q_RANK:Q_RANK + KV_RANK], kvaw)
    kp = jnp.where(m_pe, plast, 0.0)
    kp = kp * lax.rsqrt(jnp.sum(kp * kp, axis=-1, keepdims=True) / ROPE + EPS) * kpw
    k_rot = rope(kp)
    qs, ks, vs = [], [], []
    for h in range(N_HEADS):
        qh = mmw(qa, wq[h], sq[h])
        ss_n = jnp.sum(jnp.where(m_nope, qh * qh, 0.0), axis=-1, keepdims=True) / NOPE
        ss_p = jnp.sum(jnp.where(m_pe, qh * qh, 0.0), axis=-1, keepdims=True) / ROPE
        r = jnp.where(m_nope, lax.rsqrt(ss_n + EPS), lax.rsqrt(ss_p + EPS))
        qs.append(rope(qh * r * qnw) * Q_SCALE)
        kh = mmw(kva, wk[h], sk[h])
        kh = kh * lax.rsqrt(jnp.sum(kh * kh, axis=-1, keepdims=True) / NOPE + EPS) * knw
        ks.append(kh + k_rot)
        vs.append(mmw(kva, wv[h], sv[h]))
    return jnp.stack(qs), jnp.stack(ks), jnp.stack(vs)


def _f_ssd(xext, z, plast, prev, cw, cb, dtb, alog, dskip, snw):
    n = CHUNK
    conv = cb
    for k in range(4):
        conv = conv + cw[k:k + 1] * xext[HALO - 3 + k:HALO - 3 + k + n]
    xc = jax.nn.silu(conv)
    xs, bm, cm = xc[:, :D_SSD], xc[:, D_SSD:D_SSD + 2 * SSD_STATE], xc[:, D_SSD + 2 * SSD_STATE:]
    lane = lax.broadcasted_iota(jnp.int32, (1, 128), 1)
    dt = jax.nn.softplus(jnp.where(lane < N_HEADS, plast, 0.0) + dtb)
    adt = dt * (-jnp.exp(alog))
    row = lax.broadcasted_iota(jnp.int32, (n, n), 0)
    col = lax.broadcasted_iota(jnp.int32, (n, n), 1)
    tri = row >= col
    acs = jnp.dot(tri.astype(f32), adt, precision=lax.Precision.HIGHEST, preferred_element_type=f32)
    acs_t = acs.T
    bgs = [bm[:, g * SSD_STATE:(g + 1) * SSD_STATE] for g in range(2)]
    cgs = [cm[:, g * SSD_STATE:(g + 1) * SSD_STATE] for g in range(2)]
    cb_ts = [mm_nt(cgs[g], bgs[g]) for g in range(2)]
    pre = []
    for h in range(N_HEADS):
        a_col = acs[:, h:h + 1]
        a_row = acs_t[h:h + 1, :]
        decay_ls = jnp.exp(jnp.where(tri, a_col - a_row, -jnp.inf))
        xh = xs[:, h * SSD_HEAD_DIM:(h + 1) * SSD_HEAD_DIM]
        xdt = xh * dt[:, h:h + 1]
        a_last = acs[n - 1:n, h:h + 1]
        pre.append((cb_ts[h // 4] * decay_ls, xdt, xdt * jnp.exp(a_last - a_col), jnp.exp(a_last), jnp.exp(a_col), xh))
    prods = []
    for h in range(N_HEADS):
        scores, xdt, weighted, _, _, _ = pre[h]
        prods.append((mm(scores, xdt), mm_tn(weighted, bgs[h // 4]), mm_nt(cgs[h // 4], prev[h])))
    ys, news = [], []
    for h in range(N_HEADS):
        y_diag, st, y_off = prods[h]
        _, _, _, chunk_decay, in_decay, xh = pre[h]
        news.append(chunk_decay * prev[h] + st)
        ys.append(y_diag + y_off * in_decay + dskip[:, h:h + 1] * xh)
    y = jnp.concatenate(ys, axis=1)
    yg = y * jax.nn.silu(z)
    half = D_SSD // 2
    outs = []
    for g in range(2):
        t = yg[:, g * half:(g + 1) * half]
        outs.append(t * lax.rsqrt(jnp.mean(t * t, axis=-1, keepdims=True) + EPS))
    return jnp.concatenate(outs, axis=1) * snw, jnp.stack(news)


def _f_out(o, yg, g1, wo, slot=None):
    cat = jnp.concatenate([o[h] for h in range(N_HEADS)] + [yg], axis=1)
    return g1 * mmw(cat, wo, slot)


def _f_gate_up(x, nw, sh, sc, wg, wu, slot_g=None, slot_u=None):
    h = _rms(x, nw) * (1.0 + sc) + sh
    return jax.nn.silu(mmw_t(h, wg, slot_g)) * mmw_t(h, wu, slot_u)


def proj_fwd(x, nw, sh, sc, w):
    s = x.shape[0]
    ts = _token_block(s)

    def body(x_ref, nw_ref, sh_ref, sc_ref, w_ref, pa_ref, pz_ref, px_ref, pl_ref):
        p = _f_proj(x_ref[...], nw_ref[...], sh_ref[...], sc_ref[...], w_ref[...])
        pa_ref[...] = p[:, :384]
        pz_ref[...] = p[:, 384:896]
        px_ref[...] = p[:, 896:1920]
        pl_ref[...] = p[:, 1920:]

    vec = _const((1, D_MODEL))
    return pl.pallas_call(
        body, name="proj_fwd", grid=(s // ts,),
        in_specs=[pl.BlockSpec((ts, D_MODEL), lambda i: (i, 0)), vec, vec, vec, _const((D_MODEL, D_PROJ))],
        out_specs=[pl.BlockSpec((ts, 384), lambda i: (i, 0)), pl.BlockSpec((ts, 512), lambda i: (i, 0)),
                   pl.BlockSpec((ts, 1024), lambda i: (i, 0)), pl.BlockSpec((ts, 128), lambda i: (i, 0))],
        out_shape=[jax.ShapeDtypeStruct((s, 384), f32), jax.ShapeDtypeStruct((s, 512), f32),
                   jax.ShapeDtypeStruct((s, 1024), f32), jax.ShapeDtypeStruct((s, 128), f32)],
    )(x, nw, sh, sc, w)


def rope_tables(pos, inv):
    s = pos.shape[0]
    ts = _token_block(s)

    def body(pos_ref, inv_ref, cos_ref, sin_ref):
        ang = pos_ref[...].astype(f32) * inv_ref[...]
        lane = lax.broadcasted_iota(jnp.int32, (1, HEAD_LANES), 1)
        half = ROPE // 2
        cos_ref[...] = jnp.where(lane < NOPE, 1.0, jnp.where(lane < NOPE + ROPE, jnp.cos(ang), 0.0))
        sn = jnp.sin(ang)
        sin_ref[...] = jnp.where((lane >= NOPE) & (lane < NOPE + half), -sn,
                                 jnp.where((lane >= NOPE + half) & (lane < NOPE + ROPE), sn, 0.0))

    return pl.pallas_call(
        body, name="rope_tables", grid=(s // ts,),
        in_specs=[pl.BlockSpec((ts, 1), lambda i: (i, 0)), _const((1, HEAD_LANES))],
        out_specs=[pl.BlockSpec((ts, HEAD_LANES), lambda i: (i, 0))] * 2,
        out_shape=[jax.ShapeDtypeStruct((s, HEAD_LANES), f32)] * 2,
    )(pos, inv)


def _qkv_param_specs():
    return [_const((1, Q_RANK)), _const((1, KV_RANK)), _const((N_HEADS, Q_RANK, HEAD_LANES)),
            _const((N_HEADS, KV_RANK, HEAD_LANES)), _const((N_HEADS, KV_RANK, V_DIM)),
            _const((1, HEAD_LANES)), _const((1, HEAD_LANES)), _const((1, HEAD_LANES))]


def qkv_fwd(pa, plast, cos_t, sin_t, params):
    s = pa.shape[0]
    ts = _token_block(s)

    def body(pa_ref, pl_ref, cos_ref, sin_ref, *rest):
        prm = [r[...] for r in rest[:8]]
        q_ref, k_ref, v_ref = rest[8:]
        q, k, v = _f_qkv(pa_ref[...], pl_ref[...], cos_ref[...], sin_ref[...], *prm)
        q_ref[...] = q.astype(bf16)
        k_ref[...] = k.astype(bf16)
        v_ref[...] = jnp.concatenate([v, jnp.ones_like(v)], axis=-1).astype(bf16)

    tok = lambda w: pl.BlockSpec((ts, w), lambda i: (i, 0))
    head = pl.BlockSpec((N_HEADS, ts, HEAD_LANES), lambda i: (0, i, 0))
    return pl.pallas_call(
        body, name="qkv_fwd", grid=(s // ts,),
        in_specs=[tok(384), tok(128), tok(128), tok(128)] + _qkv_param_specs(),
        out_specs=[head] * 3, out_shape=[jax.ShapeDtypeStruct((N_HEADS, s, HEAD_LANES), bf16)] * 3,
    )(pa, plast, cos_t, sin_t, *params)


def _scores(q, k):
    return lax.dot_general(q, k, (((1,), (1,)), ((), ())), preferred_element_type=f32)


def _tril(rows, cols, row_offset):
    row = row_offset + lax.broadcasted_iota(jnp.int32, (rows, cols), 0)
    col = lax.broadcasted_iota(jnp.int32, (rows, cols), 1)
    return row >= col


def _call_with_job(body, name, grid, job, in_specs, out_specs, out_shape, scratch_shapes, operands, relay_at=None):
    if job is None:
        res = pl.pallas_call(body, name=name, grid=grid, in_specs=in_specs, out_specs=out_specs, out_shape=out_shape,
                             scratch_shapes=scratch_shapes)(*operands)
        return res, None

    def at_step(i, n):
        if i == 0:
            want = [0] * len(grid)
        elif i == n - 1:
            want = [g - 1 for g in grid]
        else:
            want = relay_at
        return functools.reduce(jnp.logical_and, [pl.program_id(a) == s for a, s in enumerate(want)])

    carrier = _carry(job, body, len(in_specs), len(out_specs), at_step)
    res = pl.pallas_call(
        carrier, name=name, grid=grid,
        in_specs=list(in_specs) + [ANY] * len(job.operands), out_specs=list(out_specs) + [ANY] * len(job.out_shape),
        out_shape=list(out_shape) + list(job.out_shape), scratch_shapes=list(scratch_shapes) + job.scratch,
    )(*operands, *job.operands)
    return res[:len(out_specs)], res[len(out_specs):]


def attn_fwd(q, k, v, job=None):
    s = q.shape[1]
    t = _token_block(s)
    nb = s // t

    rb = min(ATTN_ROWS_FWD, t)

    hp = ATTN_HEADS_FWD

    def body(q_ref, k_ref, v_ref, o_ref, lse_ref, m_sc, acc_sc):
        qi = pl.program_id(1)
        m_sc[...] = jnp.full(m_sc.shape, NEG, f32)
        acc_sc[...] = jnp.zeros(acc_sc.shape, f32)

        def step(k0, diagonal):
            chains = [(hh, r) for hh in range(hp) for r in range(t // rb)]

            def scores(hh, r):
                nk = (r + 1) * rb if diagonal else t
                sc = _scores(q_ref[hh, pl.ds(r * rb, rb), :], k_ref[hh, pl.ds(k0, nk), :])
                return jnp.where(_tril(rb, nk, r * rb), sc, NEG) if diagonal else sc

            ahead = scores(*chains[0])
            for c, (hh, r) in enumerate(chains):
                sc = ahead
                if c + 1 < len(chains):
                    ahead = scores(*chains[c + 1])
                rows = pl.ds(r * rb, rb)
                keys = pl.ds(k0, (r + 1) * rb if diagonal else t)
                m_prev = m_sc[hh, rows, :1]
                m_new = jnp.maximum(m_prev, jnp.max(sc, axis=-1, keepdims=True))
                p = jnp.exp2(sc - m_new)
                alpha = jnp.exp2(m_prev - m_new)
                acc = alpha * acc_sc[hh, rows, :] + jnp.dot(p.astype(bf16), v_ref[hh, keys, :], preferred_element_type=f32)
                if diagonal:
                    l = acc[:, V_DIM:V_DIM + 1]
                    o_ref[hh, rows, :] = acc[:, :V_DIM] / l
                    lse_ref[hh, rows, :] = jnp.broadcast_to(m_new + jnp.log2(l), (rb, 128))
                else:
                    acc_sc[hh, rows, :] = acc
                    m_sc[hh, rows, :] = jnp.broadcast_to(m_new, (rb, 128))

        def below(ki, carry):
            step(pl.multiple_of(ki * t, t), False)
            return carry

        lax.fori_loop(0, qi, below, 0)
        step(pl.multiple_of(qi * t, t), True)

    return _call_with_job(
        body, "attn_fwd" if job is None else "attn_fwd_comm", (N_HEADS // hp, nb), job,
        in_specs=[pl.BlockSpec((hp, t, HEAD_LANES), lambda h, qi: (h, qi, 0)),
                  pl.BlockSpec((hp, s, HEAD_LANES), lambda h, qi: (h, 0, 0)),
                  pl.BlockSpec((hp, s, HEAD_LANES), lambda h, qi: (h, 0, 0))],
        out_specs=[pl.BlockSpec((hp, t, V_DIM), lambda h, qi: (h, qi, 0)),
                   pl.BlockSpec((hp, t, 128), lambda h, qi: (h, qi, 0))],
        out_shape=[jax.ShapeDtypeStruct((N_HEADS, s, V_DIM), f32), jax.ShapeDtypeStruct((N_HEADS, s, 128), f32)],
        scratch_shapes=[pltpu.VMEM((hp, t, 128), f32), pltpu.VMEM((hp, t, HEAD_LANES), f32)],
        operands=(q, k, v), relay_at=(N_HEADS // hp - 1, max(nb - 2, 0)))


def _ssd_param_specs():
    return [_const((4, D_CONV)), _const((1, D_CONV)), _const((1, 128)), _const((1, 128)), _const((1, 128)),
            _const((1, D_SSD))]


def ssd_fwd(px, pz, plast, params):
    s = px.shape[0]
    nc = s // CHUNK

    def body(px_ref, pz_ref, pl_ref, cw_ref, cb_ref, dtb_ref, alog_ref, dskip_ref, snw_ref, yg_ref, st_ref,
             state_sc, halo_sc):
        i = pl.program_id(0)

        @pl.when(i == 0)
        def _():
            state_sc[...] = jnp.zeros(state_sc.shape, f32)
            halo_sc[...] = jnp.zeros(halo_sc.shape, f32)

        x = px_ref[...]
        prev = state_sc[...]
        st_ref[...] = prev
        xext = jnp.concatenate([halo_sc[...], x], axis=0)
        yg, new = _f_ssd(xext, pz_ref[...], pl_ref[...], prev, cw_ref[...], cb_ref[...], dtb_ref[...],
                         alog_ref[...], dskip_ref[...], snw_ref[...])
        yg_ref[...] = yg
        state_sc[...] = new
        halo_sc[...] = x[CHUNK - HALO:]

    tok = lambda w: pl.BlockSpec((CHUNK, w), lambda i: (i, 0))
    return pl.pallas_call(
        body, name="ssd_fwd", grid=(nc,),
        in_specs=[tok(D_CONV), tok(D_SSD), tok(128)] + _ssd_param_specs(),
        out_specs=[tok(D_SSD), pl.BlockSpec((None, N_HEADS, SSD_HEAD_DIM, SSD_STATE), lambda i: (i, 0, 0, 0))],
        out_shape=[jax.ShapeDtypeStruct((s, D_SSD), f32),
                   jax.ShapeDtypeStruct((nc, N_HEADS, SSD_HEAD_DIM, SSD_STATE), f32)],
        scratch_shapes=[pltpu.VMEM((N_HEADS, SSD_HEAD_DIM, SSD_STATE), f32), pltpu.VMEM((HALO, D_CONV), f32)],
    )(px, pz, plast, *params)


def out_fwd(x, o, yg, g1, wo):
    s = x.shape[0]
    ts = _token_block(s)

    def body(x_ref, o_ref, yg_ref, g1_ref, wo_ref, out_ref):
        out_ref[...] = x_ref[...] + _f_out(o_ref[...], yg_ref[...], g1_ref[...], wo_ref[...])

    return pl.pallas_call(
        body, name="out_fwd", grid=(s // ts,),
        in_specs=[pl.BlockSpec((ts, D_MODEL), lambda i: (i, 0)), pl.BlockSpec((N_HEADS, ts, V_DIM), lambda i: (0, i, 0)),
                  pl.BlockSpec((ts, D_SSD), lambda i: (i, 0)), _const((1, D_MODEL)), _const((D_MODEL, D_MODEL))],
        out_specs=pl.BlockSpec((ts, D_MODEL), lambda i: (i, 0)),
        out_shape=jax.ShapeDtypeStruct((s, D_MODEL), f32),
    )(x, o, yg, g1, wo)


def mlp_fwd(x, nw, sh, sc, g2, wgu, wd):
    s = x.shape[0]
    ts = _token_block(s)
    nj = N_DEV // 2

    def body(x_ref, nw_ref, sh_ref, sc_ref, g2_ref, wg_ref, wu_ref, wd_ref, out_ref, mix_ref):
        j = pl.program_id(1)
        act = _f_gate_up(x_ref[...], nw_ref[...], sh_ref[...], sc_ref[...], wg_ref[...], wu_ref[...])
        _accumulate(j == 0, [mix_ref], [mmw(act, wd_ref[...])])

        @pl.when(j == nj - 1)
        def _():
            out_ref[...] = x_ref[...] + g2_ref[...] * mix_ref[...]

    vec = _const((1, D_MODEL))
    return pl.pallas_call(
        body, name="mlp_fwd", grid=(s // ts, nj),
        in_specs=[pl.BlockSpec((ts, D_MODEL), lambda i, j: (i, 0)), vec, vec, vec, vec,
                  pl.BlockSpec((None, FF_SHARD, D_MODEL), lambda i, j: (j, 0, 0)),
                  pl.BlockSpec((None, FF_SHARD, D_MODEL), lambda i, j: (j + nj, 0, 0)),
                  pl.BlockSpec((None, FF_SHARD, D_MODEL), lambda i, j: (j, 0, 0))],
        out_specs=[pl.BlockSpec((ts, D_MODEL), lambda i, j: (i, 0))] * 2,
        out_shape=[jax.ShapeDtypeStruct((s, D_MODEL), f32)] * 2,
    )(x, nw, sh, sc, g2, wgu, wgu, wd)


def loss_fwd(y, target):
    s = y.shape[0]
    ts = _token_block(s)

    def body(y_ref, t_ref, dy_ref, loss_ref):
        d = y_ref[...] - t_ref[...]
        dy_ref[...] = d * (1.0 / D_MODEL)
        part = 0.5 * jnp.sum(jnp.sum(d * d, axis=-1, keepdims=True) * (1.0 / D_MODEL), axis=0, keepdims=True)
        _accumulate(pl.program_id(0) == 0, [loss_ref], [jnp.broadcast_to(part, (8, 128))])

    return pl.pallas_call(
        body, name="loss_fwd", grid=(s // ts,),
        in_specs=[pl.BlockSpec((ts, D_MODEL), lambda i: (i, 0))] * 2,
        out_specs=[pl.BlockSpec((ts, D_MODEL), lambda i: (i, 0)), _const((8, 128))],
        out_shape=[jax.ShapeDtypeStruct((s, D_MODEL), f32), jax.ShapeDtypeStruct((8, 128), f32)],
    )(y, target)


def mlp_bwd(x, dy, nw, sh, sc, g2, wgu, wd, job=None):
    s = x.shape[0]
    ts = min(MLP_BWD_ROWS, s)
    nj = N_DEV // 2
    ni = s // ts

    def body(x_ref, dy_ref, nw_ref, sh_ref, sc_ref, g2_ref, wg_ref, wu_ref, wd_ref,
             dx_ref, dnw_ref, dsh_ref, dsc_ref, dwg_ref, dwu_ref, dwd_ref, ag_sc, au_sc, ad_sc):
        j, i = pl.program_id(0), pl.program_id(1)
        wg, wu, wd = wg_ref[...], wu_ref[...], wd_ref[...]
        act, vjp = jax.vjp(lambda x_, nw_, sh_, sc_, sg, su: _f_gate_up(x_, nw_, sh_, sc_, wg, wu, sg, su),
                           x_ref[...], nw_ref[...], sh_ref[...], sc_ref[...],
                           jnp.zeros(wg.shape, f32), jnp.zeros(wu.shape, f32))
        dmix = dy_ref[...] * g2_ref[...]
        dact = _dot(dmix, wd, 1, 1)
        dwd = _dot(act, dmix, 0, 0)
        dx, dnw, dsh, dsc, dwg, dwu = vjp(dact)
        dx_ref[...] = dx.astype(bf16)
        _accumulate((i == 0) & (j == 0), [dnw_ref, dsh_ref, dsc_ref], [dnw, dsh, dsc])
        _accumulate_then_cast(i == 0, i == ni - 1, [ag_sc, au_sc, ad_sc], [dwg_ref, dwu_ref, dwd_ref], [dwg, dwu, dwd])

    vec = _const((1, D_MODEL))
    vshape = jax.ShapeDtypeStruct((1, D_MODEL), f32)
    once = pl.Buffered(1)
    wspec = lambda off: pl.BlockSpec((None, FF_SHARD, D_MODEL), lambda j, i: (j + off, 0, 0), pipeline_mode=once)
    dspec = pl.BlockSpec((None, FF_SHARD, D_MODEL), lambda j, i: (j, 0, 0), pipeline_mode=once)
    return _call_with_job(
        body, "mlp_bwd" if job is None else "mlp_bwd_comm", (nj, ni), job,
        in_specs=[pl.BlockSpec((ts, D_MODEL), lambda j, i: (i, 0)), pl.BlockSpec((ts, D_MODEL), lambda j, i: (i, 0)),
                  vec, vec, vec, vec, wspec(0), wspec(nj), dspec],
        out_specs=[pl.BlockSpec((None, ts, D_MODEL), lambda j, i: (j, i, 0)), vec, vec, vec,
                   wspec(0), wspec(0), dspec],
        out_shape=[jax.ShapeDtypeStruct((nj, s, D_MODEL), bf16), vshape, vshape, vshape,
                   jax.ShapeDtypeStruct((nj, FF_SHARD, D_MODEL), bf16), jax.ShapeDtypeStruct((nj, FF_SHARD, D_MODEL), bf16),
                   jax.ShapeDtypeStruct((nj, FF_SHARD, D_MODEL), bf16)],
        scratch_shapes=[pltpu.VMEM((FF_SHARD, D_MODEL), f32), pltpu.VMEM((FF_SHARD, D_MODEL), f32),
                        pltpu.VMEM((FF_SHARD, D_MODEL), f32)],
        operands=(x, dy, nw, sh, sc, g2, wgu, wgu, wd))


def out_bwd(dy, dparts, mix, o, yg, g1, wo):
    s = dy.shape[0]
    ts = _token_block(s)
    nj = dparts.shape[0]

    ni = s // ts

    def body(dy_ref, dp_ref, mix_ref, o_ref, yg_ref, g1_ref, wo_ref, dx_ref, do_ref, delta_ref, dyg_ref, dg1_ref,
             dg2_ref, dwo_ref, acc_sc):
        i = pl.program_id(0)
        g = dy_ref[...]
        _accumulate(i == 0, [dg2_ref], [jnp.sum(g * mix_ref[...], axis=0, keepdims=True)])
        for j in range(nj):
            g = g + dp_ref[j].astype(f32)
        dx_ref[...] = g
        o = o_ref[...]
        wo = wo_ref[...]
        _, vjp = jax.vjp(lambda o_, yg_, g1_, slot: _f_out(o_, yg_, g1_, wo, slot), o, yg_ref[...], g1_ref[...],
                         jnp.zeros(wo.shape, f32))
        do, dyg, dg1, dwo = vjp(g)
        do_ref[...] = do.astype(bf16)
        dyg_ref[...] = dyg
        delta_ref[...] = jnp.broadcast_to(jnp.sum(do * o, axis=-1, keepdims=True), delta_ref.shape)
        _accumulate(i == 0, [dg1_ref], [dg1])
        _accumulate_then_cast(i == 0, i == ni - 1, [acc_sc], [dwo_ref], [dwo])

    head = pl.BlockSpec((N_HEADS, ts, V_DIM), lambda i: (0, i, 0))
    return pl.pallas_call(
        body, name="out_bwd", grid=(ni,), scratch_shapes=[pltpu.VMEM((D_MODEL, D_MODEL), f32)],
        in_specs=[pl.BlockSpec((ts, D_MODEL), lambda i: (i, 0)), pl.BlockSpec((nj, ts, D_MODEL), lambda i: (0, i, 0)),
                  pl.BlockSpec((ts, D_MODEL), lambda i: (i, 0)),
                  head, pl.BlockSpec((ts, D_SSD), lambda i: (i, 0)), _const((1, D_MODEL)), _const((D_MODEL, D_MODEL))],
        out_specs=[pl.BlockSpec((ts, D_MODEL), lambda i: (i, 0)), head,
                   pl.BlockSpec((N_HEADS, ts, 128), lambda i: (0, i, 0)), pl.BlockSpec((ts, D_SSD), lambda i: (i, 0)),
                   _const((1, D_MODEL)), _const((1, D_MODEL)), _const((D_MODEL, D_MODEL))],
        out_shape=[jax.ShapeDtypeStruct((s, D_MODEL), f32), jax.ShapeDtypeStruct((N_HEADS, s, V_DIM), bf16),
                   jax.ShapeDtypeStruct((N_HEADS, s, 128), f32), jax.ShapeDtypeStruct((s, D_SSD), f32),
                   jax.ShapeDtypeStruct((1, D_MODEL), f32), jax.ShapeDtypeStruct((1, D_MODEL), f32),
                   jax.ShapeDtypeStruct((D_MODEL, D_MODEL), bf16)],
    )(dy, dparts, mix, o, yg, g1, wo)


def attn_bwd(q, k, v, do, lse, delta, job=None):
    s = q.shape[1]
    t = _token_block(s)
    nb = s // t

    hp = ATTN_HEADS_BWD

    def body(q_ref, k_ref, v_ref, do_ref, lse_ref, delta_ref, dq_ref, dk_ref, dv_ref):
        ki = pl.program_id(1)

        @pl.when(ki == 0)
        def _():
            dq_ref[...] = jnp.zeros(dq_ref.shape, f32)

        dk_ref[...] = jnp.zeros(dk_ref.shape, f32)
        dv_ref[...] = jnp.zeros(dv_ref.shape, f32)

        def step(q0, diagonal):
            rows = pl.ds(q0, t)

            def products(hh):
                sc = _scores(q_ref[hh, rows, :], k_ref[hh])
                dp = lax.dot_general(do_ref[hh, rows, :], v_ref[hh, :, :V_DIM], (((1,), (1,)), ((), ())),
                                     preferred_element_type=f32)
                return (jnp.where(_tril(t, t, 0), sc, NEG) if diagonal else sc), dp

            ahead = products(0)
            for hh in range(hp):
                sc, dp = ahead
                if hh + 1 < hp:
                    ahead = products(hh + 1)
                p = jnp.exp2(sc - jnp.tile(lse_ref[hh, rows, :], (1, t // 128)))
                ds = (p * (dp - jnp.tile(delta_ref[hh, rows, :], (1, t // 128)))).astype(bf16)
                dv_ref[hh] += lax.dot_general(p.astype(bf16), do_ref[hh, rows, :], (((0,), (0,)), ((), ())),
                                              preferred_element_type=f32)
                dk_ref[hh] += lax.dot_general(ds, q_ref[hh, rows, :], (((0,), (0,)), ((), ())), preferred_element_type=f32)
                dq_ref[hh, rows, :] += jnp.dot(ds, k_ref[hh], preferred_element_type=f32)

        step(pl.multiple_of(ki * t, t), True)

        def above(qi, carry):
            step(pl.multiple_of(qi * t, t), False)
            return carry

        lax.fori_loop(ki + 1, nb, above, 0)
        dk_ref[...] = dk_ref[...] * LN2

        @pl.when(ki == nb - 1)
        def _():
            dq_ref[...] = dq_ref[...] * LN2

    qspec = lambda w: pl.BlockSpec((hp, s, w), lambda h, ki: (h, 0, 0))
    kspec = lambda w: pl.BlockSpec((hp, t, w), lambda h, ki: (h, ki, 0))
    return _call_with_job(
        body, "attn_bwd" if job is None else "attn_bwd_comm", (N_HEADS // hp, nb), job,
        in_specs=[qspec(HEAD_LANES), kspec(HEAD_LANES), kspec(HEAD_LANES), qspec(V_DIM), qspec(128), qspec(128)],
        out_specs=[qspec(HEAD_LANES), kspec(HEAD_LANES), kspec(V_DIM)],
        out_shape=[jax.ShapeDtypeStruct((N_HEADS, s, HEAD_LANES), f32), jax.ShapeDtypeStruct((N_HEADS, s, HEAD_LANES), f32),
                   jax.ShapeDtypeStruct((N_HEADS, s, V_DIM), f32)],
        scratch_shapes=[], operands=(q, k, v, do, lse, delta))


def ssd_bwd(px, pz, plast, states, dyg, params):
    s = px.shape[0]
    nc = s // CHUNK
    per = CHUNK // HALO

    def body(px_ref, halo_ref, pz_ref, pl_ref, st_ref, dyg_ref, cw_ref, cb_ref, dtb_ref, alog_ref, dskip_ref, snw_ref,
             dpx_ref, dpz_ref, dpl_ref, dcw_ref, dcb_ref, ddtb_ref, dalog_ref, ddskip_ref, dsnw_ref, dstate_sc, dhalo_sc):
        t = pl.program_id(0)
        chunk = nc - 1 - t

        @pl.when(t == 0)
        def _():
            dstate_sc[...] = jnp.zeros(dstate_sc.shape, f32)
            dhalo_sc[...] = jnp.zeros(dhalo_sc.shape, f32)

        halo = jnp.where(chunk > 0, halo_ref[...], 0.0)
        xext = jnp.concatenate([halo, px_ref[...]], axis=0)
        _, vjp = jax.vjp(_f_ssd, xext, pz_ref[...], pl_ref[...], st_ref[...], cw_ref[...], cb_ref[...], dtb_ref[...],
                         alog_ref[...], dskip_ref[...], snw_ref[...])
        dxext, dz, dpl, dprev, dcw, dcb, ddtb, dalog, ddskip, dsnw = vjp((dyg_ref[...], dstate_sc[...]))
        dpx_ref[...] = dxext[HALO:]
        dpx_ref[CHUNK - HALO:, :] += dhalo_sc[...]
        dhalo_sc[...] = dxext[:HALO]
        dstate_sc[...] = dprev
        dpz_ref[...] = dz
        dpl_ref[...] = dpl
        _accumulate(t == 0, [dcw_ref, dcb_ref, ddtb_ref, dalog_ref, ddskip_ref, dsnw_ref],
                    [dcw, dcb, ddtb, dalog, ddskip, dsnw])

    rev = lambda w: pl.BlockSpec((CHUNK, w), lambda t: (nc - 1 - t, 0))
    pshapes = [jax.ShapeDtypeStruct((4, D_CONV), f32), jax.ShapeDtypeStruct((1, D_CONV), f32),
               jax.ShapeDtypeStruct((1, 128), f32), jax.ShapeDtypeStruct((1, 128), f32),
               jax.ShapeDtypeStruct((1, 128), f32), jax.ShapeDtypeStruct((1, D_SSD), f32)]
    return pl.pallas_call(
        body, name="ssd_bwd", grid=(nc,),
        in_specs=[rev(D_CONV),
                  pl.BlockSpec((HALO, D_CONV), lambda t: (jnp.maximum((nc - 1 - t) * per - 1, 0), 0)),
                  rev(D_SSD), rev(128),
                  pl.BlockSpec((None, N_HEADS, SSD_HEAD_DIM, SSD_STATE), lambda t: (nc - 1 - t, 0, 0, 0)),
                  rev(D_SSD)] + _ssd_param_specs(),
        out_specs=[rev(D_CONV), rev(D_SSD), rev(128)] + _ssd_param_specs(),
        out_shape=[jax.ShapeDtypeStruct((s, D_CONV), f32), jax.ShapeDtypeStruct((s, D_SSD), f32),
                   jax.ShapeDtypeStruct((s, 128), f32)] + pshapes,
        scratch_shapes=[pltpu.VMEM((N_HEADS, SSD_HEAD_DIM, SSD_STATE), f32), pltpu.VMEM((HALO, D_CONV), f32)],
    )(px, px, pz, plast, states, dyg, *params)


def qkv_bwd(pa, plast, cos_t, sin_t, params, dq, dk, dv):
    s = pa.shape[0]
    ts = _token_block(s)

    def body(pa_ref, pl_ref, cos_ref, sin_ref, *rest):
        qaw, kvaw, wq, wk, wv, qnw, knw, kpw = [r[...] for r in rest[:8]]
        dq_ref, dk_ref, dv_ref = rest[8:11]
        dpa_ref, dpl_ref = rest[11:13]
        dprm_refs = list(rest[13:])
        cos_t, sin_t = cos_ref[...], sin_ref[...]

        def stage(pa_, pl_, qaw_, kvaw_, sq, sk, sv, qnw_, knw_, kpw_):
            return _f_qkv(pa_, pl_, cos_t, sin_t, qaw_, kvaw_, wq, wk, wv, qnw_, knw_, kpw_, (sq, sk, sv))

        _, vjp = jax.vjp(stage, pa_ref[...], pl_ref[...], qaw, kvaw, jnp.zeros(wq.shape, f32), jnp.zeros(wk.shape, f32),
                         jnp.zeros(wv.shape, f32), qnw, knw, kpw)
        grads = vjp((dq_ref[...], dk_ref[...], dv_ref[...]))
        dpa_ref[...] = grads[0]
        dpl_ref[...] = grads[1]
        _accumulate(pl.program_id(0) == 0, dprm_refs, list(grads[2:]))

    tok = lambda w: pl.BlockSpec((ts, w), lambda i: (i, 0))
    head = lambda w: pl.BlockSpec((N_HEADS, ts, w), lambda i: (0, i, 0))
    pshapes = [jax.ShapeDtypeStruct((1, Q_RANK), f32), jax.ShapeDtypeStruct((1, KV_RANK), f32),
               jax.ShapeDtypeStruct((N_HEADS, Q_RANK, HEAD_LANES), f32), jax.ShapeDtypeStruct((N_HEADS, KV_RANK, HEAD_LANES), f32),
               jax.ShapeDtypeStruct((N_HEADS, KV_RANK, V_DIM), f32), jax.ShapeDtypeStruct((1, HEAD_LANES), f32),
               jax.ShapeDtypeStruct((1, HEAD_LANES), f32), jax.ShapeDtypeStruct((1, HEAD_LANES), f32)]
    return pl.pallas_call(
        body, name="qkv_bwd", grid=(s // ts,),
        in_specs=[tok(384), tok(128), tok(128), tok(128)] + _qkv_param_specs()
                 + [head(HEAD_LANES), head(HEAD_LANES), head(V_DIM)],
        out_specs=[tok(384), tok(128)] + _qkv_param_specs(),
        out_shape=[jax.ShapeDtypeStruct((s, 384), f32), jax.ShapeDtypeStruct((s, 128), f32)] + pshapes,
    )(pa, plast, cos_t, sin_t, *params, dq, dk, dv)


def proj_bwd(x, nw, sh, sc, w, dpa, dpz, dpx, dpl_k, dpl_dt, dres):
    s = x.shape[0]
    ts = _token_block(s)

    ni = s // ts

    def body(x_ref, nw_ref, sh_ref, sc_ref, w_ref, dpa_ref, dpz_ref, dpx_ref, dplk_ref, dpld_ref, dres_ref,
             dx_ref, dnw_ref, dsh_ref, dsc_ref, dw_ref, acc_sc):
        i = pl.program_id(0)
        g = jnp.concatenate([dpa_ref[...], dpz_ref[...], dpx_ref[...], dplk_ref[...] + dpld_ref[...]], axis=1)
        w = w_ref[...]
        _, vjp = jax.vjp(lambda x_, nw_, sh_, sc_, slot: _f_proj(x_, nw_, sh_, sc_, w, slot), x_ref[...], nw_ref[...],
                         sh_ref[...], sc_ref[...], jnp.zeros(w.shape, f32))
        dx, dnw, dsh, dsc, dw = vjp(g)
        dx_ref[...] = dx + dres_ref[...]
        _accumulate(i == 0, [dnw_ref, dsh_ref, dsc_ref], [dnw, dsh, dsc])
        _accumulate_then_cast(i == 0, i == ni - 1, [acc_sc], [dw_ref], [dw])

    vec = _const((1, D_MODEL))
    vshape = jax.ShapeDtypeStruct((1, D_MODEL), f32)
    tok = lambda w_: pl.BlockSpec((ts, w_), lambda i: (i, 0))
    return pl.pallas_call(
        body, name="proj_bwd", grid=(ni,), scratch_shapes=[pltpu.VMEM((D_MODEL, D_PROJ), f32)],
        in_specs=[tok(D_MODEL), vec, vec, vec, _const((D_MODEL, D_PROJ)), tok(384), tok(512), tok(1024), tok(128), tok(128),
                  tok(D_MODEL)],
        out_specs=[tok(D_MODEL), vec, vec, vec, _const((D_MODEL, D_PROJ))],
        out_shape=[jax.ShapeDtypeStruct((s, D_MODEL), f32), vshape, vshape, vshape,
                   jax.ShapeDtypeStruct((D_MODEL, D_PROJ), bf16)],
    )(x, nw, sh, sc, w, dpa, dpz, dpx, dpl_k, dpl_dt, dres)


def ada_fwd(c_all, w_ada, b_cols):
    def body(c_ref, w_ref, b_ref, out_ref):
        act = jax.nn.silu(c_ref[...])
        for l in range(2):
            out_ref[l] = jnp.dot(act, w_ref[l], precision=lax.Precision.HIGHEST, preferred_element_type=f32) + b_ref[l]

    return pl.pallas_call(body, name="ada_fwd", out_shape=jax.ShapeDtypeStruct((2, N_DEV, 768), f32))(c_all, w_ada, b_cols)


def ada_bwd(c_all, dmod_cols):
    def body(c_ref, d_ref, out_ref):
        out_ref[0] = lax.dot_general(jax.nn.silu(c_ref[...]), d_ref[0], (((0,), (0,)), ((), ())),
                                     precision=lax.Precision.HIGHEST, preferred_element_type=f32)

    return pl.pallas_call(
        body, name="ada_bwd", grid=(2,),
        in_specs=[_const((N_DEV, D_MODEL)), pl.BlockSpec((1, N_DEV, 768), lambda l: (l, 0, 0))],
        out_specs=pl.BlockSpec((1, D_MODEL, 768), lambda l: (l, 0, 0)),
        out_shape=jax.ShapeDtypeStruct((2, D_MODEL, 768), f32),
    )(c_all, dmod_cols)


def _adamw(w, g, m, v):
    m = ADAM_B1 * m + (1.0 - ADAM_B1) * g
    v = ADAM_B2 * v + (1.0 - ADAM_B2) * (g * g)
    m_hat = m / (1.0 - ADAM_B1 ** ADAM_STEP)
    v_hat = v / (1.0 - ADAM_B2 ** ADAM_STEP)
    delta = -ADAM_LR * (m_hat / (jnp.sqrt(v_hat) + ADAM_EPS) + ADAM_WD * w)
    return delta, m, v


def adamw(parts, w, m, v, layer, prev, name):
    n, r, c = parts.shape
    nl = w.shape[0]
    tr = r
    lanes = -(-c // 128) * 128
    if 2 * (n + 7) * r * lanes * 4 > ADAMW_BLOCK_BYTES:
        tr = next(t for t in (256, 128, 64, 32, 16, 8) if r % t == 0)

    def body(p_ref, w_ref, m_ref, v_ref, *rest):
        g_ref, d_ref, nm_ref, nv_ref = rest[-4:]
        g = p_ref[0].astype(f32)
        for k in range(1, n):
            g = g + p_ref[k].astype(f32)
        delta, nm, nv = _adamw(w_ref[...], g, m_ref[...], v_ref[...])
        g_ref[...] = g
        d_ref[...] = delta
        nm_ref[...] = nm
        nv_ref[...] = nv

    blk = pl.BlockSpec((None, tr, c), lambda i: (layer, i, 0))
    shp = jax.ShapeDtypeStruct((nl, r, c), f32)
    kept = [] if prev is None else list(prev)
    return pl.pallas_call(
        body, name=name, grid=(r // tr,),
        in_specs=[pl.BlockSpec((n, tr, c), lambda i: (0, i, 0)), blk, blk, blk] + [ANY] * len(kept),
        out_specs=[blk] * 4, out_shape=[shp] * 4,
        input_output_aliases={4 + j: j for j in range(len(kept))},
    )(parts, w, m, v, *kept)


def _my_index():
    return 4 * lax.axis_index("x") + 2 * lax.axis_index("y") + lax.axis_index("c")


def _coords(idx):
    return (idx // 4, (idx // 2) % 2, idx % 2)


class CommJob:
    def __init__(self, operands, out_shape, phases, scratch):
        self.operands, self.out_shape, self.phases, self.scratch = operands, out_shape, phases, scratch


def _wait(out, n_blocks, send_sem, recv_sem, send=True, recv=True):
    span = out.at[pl.ds(0, n_blocks)]
    desc = pltpu.make_async_remote_copy(src_ref=span, dst_ref=span, send_sem=send_sem, recv_sem=recv_sem,
                                        device_id=_coords(_my_index()), device_id_type=MESH)
    if recv:
        desc.wait_recv()
    if send:
        desc.wait_send()


def gather_job(shards):
    n = len(shards)

    def places():
        x, y, c = lax.axis_index("x"), lax.axis_index("y"), lax.axis_index("c")
        return (x, y, c), (x, y, 1 - c), [(1 - x, y), (x, 1 - y), (1 - x, 1 - y)]

    def index(p):
        return 4 * p[0] + 2 * p[1] + p[2]

    def start(ins, outs, sems):
        far_send, far_recv, near_send, near_recv, local = sems
        me, sibling, chips = places()
        for k in range(n):
            pltpu.make_async_copy(ins[k], outs[k].at[index(me)], local.at[k]).start()
            for chip in chips:
                pltpu.make_async_remote_copy(src_ref=ins[k], dst_ref=outs[k].at[index(me)], send_sem=far_send.at[k],
                                             recv_sem=far_recv.at[k], device_id=(*chip, me[2]), device_id_type=MESH).start()
            pltpu.make_async_remote_copy(src_ref=ins[k], dst_ref=outs[k].at[index(me)], send_sem=near_send.at[k],
                                         recv_sem=near_recv.at[k], device_id=sibling, device_id_type=MESH).start()

    def relay(ins, outs, sems):
        far_send, far_recv, near_send, near_recv, local = sems
        me, sibling, chips = places()
        for k in range(n):
            _wait(outs[k], 3, far_send.at[k], far_recv.at[k], send=False)
            for chip in chips:
                block = outs[k].at[index((*chip, me[2]))]
                pltpu.make_async_remote_copy(src_ref=block, dst_ref=block, send_sem=near_send.at[k],
                                             recv_sem=near_recv.at[k], device_id=sibling, device_id_type=MESH).start()

    def finish(ins, outs, sems):
        far_send, far_recv, near_send, near_recv, local = sems
        for k in range(n):
            _wait(outs[k], 4, near_send.at[k], near_recv.at[k])
            _wait(outs[k], 3, far_send.at[k], far_recv.at[k], recv=False)
            pltpu.make_async_copy(ins[k], outs[k].at[0], local.at[k]).wait()

    shapes = [jax.ShapeDtypeStruct((N_DEV,) + tuple(a.shape), a.dtype) for a in shards]
    return CommJob(list(shards), shapes, [start, relay, finish], [pltpu.SemaphoreType.DMA((n,))] * 5)


def scatter_job(tensors):
    n = len(tensors)
    flat, where = [], {}
    for k, pieces in enumerate(tensors):
        d = 0
        for piece in pieces:
            for b in range(piece.shape[0]):
                where[k, d] = (len(flat), b)
                d += 1
            flat.append(piece)
        assert d == N_DEV

    def start(ins, outs, sems):
        send_sems, recv_sems, local_sems = sems
        me = _my_index()

        def block(k, d):
            i, b = where[k, d]
            return ins[i].at[b]

        for d in range(N_DEV):
            @pl.when(d != me)
            def _():
                for k in range(n):
                    pltpu.make_async_remote_copy(src_ref=block(k, d), dst_ref=outs[k].at[me], send_sem=send_sems.at[k],
                                                 recv_sem=recv_sems.at[k], device_id=(d // 4, (d // 2) % 2, d % 2),
                                                 device_id_type=MESH).start()

            @pl.when(d == me)
            def _():
                for k in range(n):
                    pltpu.make_async_copy(block(k, d), outs[k].at[d], local_sems.at[k]).start()

    def finish(ins, outs, sems):
        send_sems, recv_sems, local_sems = sems
        for k in range(n):
            _wait(outs[k], N_DEV - 1, send_sems.at[k], recv_sems.at[k])
            i, b = where[k, 0]
            pltpu.make_async_copy(ins[i].at[b], outs[k].at[0], local_sems.at[k]).wait()

    shapes = [jax.ShapeDtypeStruct((N_DEV,) + tuple(p[0].shape[1:]), p[0].dtype) for p in tensors]
    return CommJob(flat, shapes, [start, finish], [pltpu.SemaphoreType.DMA((n,))] * 3)


def comm_call(job, name):
    ni, no = len(job.operands), len(job.out_shape)

    def body(*refs):
        ins, outs, sems = refs[:ni], refs[ni:ni + no], refs[ni + no:]
        for phase in job.phases:
            phase(ins, outs, sems)

    return pl.pallas_call(body, name=name, in_specs=[ANY] * ni, out_specs=[ANY] * no, out_shape=job.out_shape,
                          scratch_shapes=job.scratch)(*job.operands)


def _carry(job, body, n_in, n_out, at_step):
    ji, jo, js = len(job.operands), len(job.out_shape), len(job.scratch)

    def carrier(*refs):
        a, b = n_in, n_in + ji
        c, d = b + n_out, b + n_out + jo
        e = len(refs) - js
        job_refs = (refs[a:b], refs[c:d], refs[e:])
        n = len(job.phases)

        @pl.when(at_step(0, n))
        def _():
            job.phases[0](*job_refs)

        body(*refs[:a], *refs[b:c], *refs[d:e])

        for i in range(1, n):
            @pl.when(at_step(i, n))
            def _():
                job.phases[i](*job_refs)

    return carrier


def _pad_lanes(v, lo, total=128):
    return jnp.pad(v, (lo, total - lo - v.shape[0]))[None, :]


MIXER_WEIGHTS = ("w_in", "w_q_up", "w_kv_up", "conv_w")
LATE_WEIGHTS = ("w_out", "w_gate_up", "w_down")


def mixer_operands(g, sw):
    w_in = g["w_in"].transpose(1, 0, 2).reshape(D_MODEL, D_IN)
    z = jnp.zeros((D_MODEL, 1), w_in.dtype)
    w_proj = jnp.concatenate(
        [w_in[:, :384], w_in[:, 416:928], w_in[:, 928:1952], w_in[:, 1952:1960], jnp.tile(z, (1, 56)),
         w_in[:, 384:416], jnp.tile(z, (1, 32))], axis=1)
    wq = jnp.pad(g["w_q_up"], ((0, 0), (0, 0), (0, HEAD_LANES - NOPE - ROPE)))
    wk = jnp.pad(g["w_kv_up"][:, :, :NOPE], ((0, 0), (0, 0), (0, HEAD_LANES - NOPE)))
    wv = g["w_kv_up"][:, :, NOPE:]
    qkv = (sw["q_a_norm_w"][None, :], sw["kv_a_norm_w"][None, :], wq, wk, wv,
           _pad_lanes(jnp.concatenate([sw["q_nope_norm_w"], sw["q_pe_norm_w"]]), 0),
           _pad_lanes(sw["k_nope_norm_w"], 0), _pad_lanes(sw["k_pe_norm_w"], NOPE))
    conv_w = g["conv_w"].astype(f32).transpose(1, 0, 2).reshape(4, D_CONV)
    ssd = (conv_w, sw["conv_b"][None, :], _pad_lanes(sw["dt_bias"], 0), _pad_lanes(sw["a_log"], 0),
           _pad_lanes(sw["d_skip"], 0), sw["ssd_norm_w"][None, :])
    return dict(w_proj=w_proj, qkv=qkv, ssd=ssd, n1=sw["norm1_w"][None, :])


def late_operands(g, sw):
    return dict(wo=g["w_out"].reshape(D_MODEL, D_MODEL), wgu=g["w_gate_up"],
                wd=g["w_down"].reshape(N_DEV // 2, FF_SHARD, D_MODEL), n2=sw["norm2_w"][None, :])


def layer_fwd(x, mod, kw, cos_t, sin_t, job=None, late=None):
    sh1, sc1, g1, sh2, sc2, g2 = [mod[i:i + 1] for i in range(6)]
    pa, pz, px, plast = proj_fwd(x, kw["n1"], sh1, sc1, kw["w_proj"])
    q, k, v = qkv_fwd(pa, plast, cos_t, sin_t, kw["qkv"])
    (o, lse), carried = attn_fwd(q, k, v, job)
    if late is not None:
        kw = {**kw, **late(carried)}
    yg, states = ssd_fwd(px, pz, plast, kw["ssd"])
    x_mid = out_fwd(x, o, yg, g1, kw["wo"])
    x_out, mix = mlp_fwd(x_mid, kw["n2"], sh2, sc2, g2, kw["wgu"], kw["wd"])
    saved = dict(x=x, pa=pa, pz=pz, px=px, plast=plast, q=q, k=k, v=v, o=o, lse=lse, yg=yg, states=states, x_mid=x_mid,
                 mix=mix)
    return x_out, saved, kw, carried


def layer_bwd_head(dy, mod, kw, sv, job=None):
    _, _, g1, sh2, sc2, g2 = [mod[i:i + 1] for i in range(6)]
    (dparts, dn2, dsh2, dsc2, dwg, dwu, dwd), carried = mlp_bwd(
        sv["x_mid"], dy, kw["n2"], sh2, sc2, g2, kw["wgu"], kw["wd"], job)
    dmid, do, delta, dyg, dg1, dg2, dwo = out_bwd(dy, dparts, sv["mix"], sv["o"], sv["yg"], g1, kw["wo"])
    early = dict(w_out=[dwo.reshape(N_DEV, D_MODEL // N_DEV, D_MODEL)], w_gate_up=[dwg, dwu],
                 w_down=[dwd.reshape(N_DEV, D_FF // N_DEV, D_MODEL)])
    head = dict(dmid=dmid, do=do, delta=delta, dyg=dyg, dn2=dn2, dsh2=dsh2, dsc2=dsc2, dg2=dg2, dg1=dg1)
    return head, early, carried


def layer_bwd_tail(hd, mod, kw, cos_t, sin_t, sv, job=None):
    sh1, sc1 = mod[0:1], mod[1:2]
    (dq, dk, dv), carried = attn_bwd(sv["q"], sv["k"], sv["v"], hd["do"], sv["lse"], hd["delta"], job)
    dpx, dpz, dpl_dt, dcw, dcb, ddtb, dalog, ddskip, dsnw = ssd_bwd(sv["px"], sv["pz"], sv["plast"], sv["states"],
                                                                   hd["dyg"], kw["ssd"])
    dpa, dpl_k, dqaw, dkvaw, dwq, dwk, dwv, dqnw, dknw, dkpw = qkv_bwd(sv["pa"], sv["plast"], cos_t, sin_t, kw["qkv"],
                                                                       dq, dk, dv)
    dx, dn1, dsh1, dsc1, dwp = proj_bwd(sv["x"], kw["n1"], sh1, sc1, kw["w_proj"], dpa, dpz, dpx, dpl_k, dpl_dt, hd["dmid"])
    dmod = jnp.concatenate([dsh1, dsc1, hd["dg1"], hd["dsh2"], hd["dsc2"], hd["dg2"]], axis=0)
    dw_in = jnp.concatenate([dwp[:, :384], dwp[:, 1984:2016], dwp[:, 384:1920], dwp[:, 1920:1928]], axis=1)
    grads = dict(
        norm1_w=dn1[0], norm2_w=hd["dn2"][0], q_a_norm_w=dqaw[0], kv_a_norm_w=dkvaw[0],
        q_nope_norm_w=dqnw[0, :NOPE], q_pe_norm_w=dqnw[0, NOPE:NOPE + ROPE], k_nope_norm_w=dknw[0, :NOPE],
        k_pe_norm_w=dkpw[0, NOPE:NOPE + ROPE], conv_b=dcb[0], dt_bias=ddtb[0, :N_HEADS], a_log=dalog[0, :N_HEADS],
        d_skip=ddskip[0, :N_HEADS], ssd_norm_w=dsnw[0],
        w_in=[dw_in.reshape(D_MODEL, N_DEV, D_IN // N_DEV).transpose(1, 0, 2)],
        w_q_up=[dwq[:, :, :NOPE + ROPE].astype(bf16)],
        w_kv_up=[jnp.concatenate([dwk[:, :, :NOPE], dwv], axis=2).astype(bf16)],
        conv_w=[dcw.reshape(4, N_DEV, D_CONV // N_DEV).transpose(1, 0, 2).astype(bf16)],
    )
    return dx, dmod, grads, carried


def _pack_small(get, last=None):
    flat = jnp.concatenate([get(name).reshape(-1) for name, _ in SMALL])
    flat = jnp.pad(flat, (0, SMALL_ROWS * 128 - flat.shape[0]))
    if last is not None:
        flat = flat.at[-1].set(last)
    return flat.reshape(SMALL_ROWS, 128)


def _unpack_small(packed):
    flat = packed.reshape(-1)
    out, off = {}, 0
    for name, size in SMALL:
        out[name] = flat[off:off + 2 * size].reshape(2, size)
        off += 2 * size
    return out


def kernel(x, c, positions, norm1_w, norm2_w, w_ada, b_ada, w_in, q_a_norm_w, w_q_up, kv_a_norm_w, w_kv_up, q_nope_norm_w, q_pe_norm_w, k_nope_norm_w, k_pe_norm_w, conv_w, conv_b, dt_bias, a_log, d_skip, ssd_norm_w, w_out, w_gate_up, w_down, loss_target, m_norm1_w, m_norm2_w, m_w_ada, m_b_ada, m_w_in, m_q_a_norm_w, m_w_q_up, m_kv_a_norm_w, m_w_kv_up, m_q_nope_norm_w, m_q_pe_norm_w, m_k_nope_norm_w, m_k_pe_norm_w, m_conv_w, m_conv_b, m_dt_bias, m_a_log, m_d_skip, m_ssd_norm_w, m_w_out, m_w_gate_up, m_w_down, v_norm1_w, v_norm2_w, v_w_ada, v_b_ada, v_w_in, v_q_a_norm_w, v_w_q_up, v_kv_a_norm_w, v_w_kv_up, v_q_nope_norm_w, v_q_pe_norm_w, v_k_nope_norm_w, v_k_pe_norm_w, v_conv_w, v_conv_b, v_dt_bias, v_a_log, v_d_skip, v_ssd_norm_w, v_w_out, v_w_gate_up, v_w_down):
    w = dict(norm1_w=norm1_w, norm2_w=norm2_w, w_ada=w_ada, b_ada=b_ada, w_in=w_in, q_a_norm_w=q_a_norm_w, w_q_up=w_q_up,
             kv_a_norm_w=kv_a_norm_w, w_kv_up=w_kv_up, q_nope_norm_w=q_nope_norm_w, q_pe_norm_w=q_pe_norm_w,
             k_nope_norm_w=k_nope_norm_w, k_pe_norm_w=k_pe_norm_w, conv_w=conv_w, conv_b=conv_b, dt_bias=dt_bias,
             a_log=a_log, d_skip=d_skip, ssd_norm_w=ssd_norm_w, w_out=w_out, w_gate_up=w_gate_up, w_down=w_down)
    m = dict(norm1_w=m_norm1_w, norm2_w=m_norm2_w, w_ada=m_w_ada, b_ada=m_b_ada, w_in=m_w_in, q_a_norm_w=m_q_a_norm_w,
             w_q_up=m_w_q_up, kv_a_norm_w=m_kv_a_norm_w, w_kv_up=m_w_kv_up, q_nope_norm_w=m_q_nope_norm_w,
             q_pe_norm_w=m_q_pe_norm_w, k_nope_norm_w=m_k_nope_norm_w, k_pe_norm_w=m_k_pe_norm_w, conv_w=m_conv_w,
             conv_b=m_conv_b, dt_bias=m_dt_bias, a_log=m_a_log, d_skip=m_d_skip, ssd_norm_w=m_ssd_norm_w, w_out=m_w_out,
             w_gate_up=m_w_gate_up, w_down=m_w_down)
    v = dict(norm1_w=v_norm1_w, norm2_w=v_norm2_w, w_ada=v_w_ada, b_ada=v_b_ada, w_in=v_w_in, q_a_norm_w=v_q_a_norm_w,
             w_q_up=v_w_q_up, kv_a_norm_w=v_kv_a_norm_w, w_kv_up=v_w_kv_up, q_nope_norm_w=v_q_nope_norm_w,
             q_pe_norm_w=v_q_pe_norm_w, k_nope_norm_w=v_k_nope_norm_w, k_pe_norm_w=v_k_pe_norm_w, conv_w=v_conv_w,
             conv_b=v_conv_b, dt_bias=v_dt_bias, a_log=v_a_log, d_skip=v_d_skip, ssd_norm_w=v_ssd_norm_w, w_out=v_w_out,
             w_gate_up=v_w_gate_up, w_down=v_w_down)
    me = _my_index()
    seq = x.shape[1]

    def shard(name, l):
        if name == "conv_w":
            return w[name][l]
        if name in TRANSPOSED:
            return jnp.swapaxes(w[name][l], 0, 1).astype(bf16)
        return w[name][l].astype(bf16)

    def shards(names, l):
        return [shard(name, l) for name in names]

    small = [{name: w[name][l] for name, _ in SMALL if name != "b_ada"} for l in range(2)]
    n_mix, n_late = len(MIXER_WEIGHTS), len(LATE_WEIGHTS)

    first = comm_call(gather_job([c] + shards(MIXER_WEIGHTS, 0)), "gather_first")
    c_all = first[0].reshape(N_DEV, D_MODEL)
    kws = [mixer_operands(dict(zip(MIXER_WEIGHTS, first[1:])), small[0]), None]

    b_cols = lax.dynamic_slice_in_dim(b_ada, me * 768, 768, axis=1)
    mod_cols = ada_fwd(c_all, w_ada, b_cols)
    (mod_all,) = comm_call(gather_job([mod_cols]), "gather_mod")
    mod_me = lax.dynamic_index_in_dim(mod_all, me, axis=2, keepdims=False)
    mods = [mod_me[:, l, :].reshape(6, D_MODEL) for l in range(2)]

    inv_freq = 1.0 / (ROPE_THETA ** (jnp.arange(0, ROPE, 2, dtype=f32) / ROPE))
    inv = _pad_lanes(jnp.concatenate([inv_freq, inv_freq]), NOPE)
    cos_t, sin_t = rope_tables(positions.reshape(seq, 1), inv)

    saved = [None, None]
    h, saved[0], kws[0], got = layer_fwd(
        x[0], mods[0], kws[0], cos_t, sin_t, gather_job(shards(LATE_WEIGHTS, 0) + shards(MIXER_WEIGHTS, 1)),
        lambda got: late_operands(dict(zip(LATE_WEIGHTS, got[:n_late])), small[0]))
    kws[1] = mixer_operands(dict(zip(MIXER_WEIGHTS, got[n_late:])), small[1])
    h, saved[1], kws[1], _ = layer_fwd(
        h, mods[1], kws[1], cos_t, sin_t, gather_job(shards(LATE_WEIGHTS, 1)),
        lambda got: late_operands(dict(zip(LATE_WEIGHTS, got)), small[1]))
    dy, loss_part = loss_fwd(h, loss_target[0])

    early, late = ("w_out", "w_gate_up", "w_down"), ("w_in", "w_q_up", "w_kv_up", "conv_w")
    parts = [{}, {}]
    head, pieces, _ = layer_bwd_head(dy, mods[1], kws[1], saved[1])
    dy, dmod1, grads1, got = layer_bwd_tail(head, mods[1], kws[1], cos_t, sin_t, saved[1], scatter_job([pieces[n] for n in early]))
    parts[1].update(zip(early, got))
    head, pieces, got = layer_bwd_head(dy, mods[0], kws[0], saved[0], scatter_job([grads1[n] for n in late]))
    parts[1].update(zip(late, got))
    dy, dmod0, grads0, got = layer_bwd_tail(head, mods[0], kws[0], cos_t, sin_t, saved[0], scatter_job([pieces[n] for n in early]))
    parts[0].update(zip(early, got))
    parts[0].update(zip(late, comm_call(scatter_job([grads0[n] for n in late]), "scatter_layer0_rest")))
    grad_x = dy[None]

    small_part = {name: jnp.stack([grads0[name], grads1[name]]) for name, _ in SMALL if name != "b_ada"}
    small_part["b_ada"] = jnp.stack([dmod0.reshape(-1), dmod1.reshape(-1)])
    (small_all,) = comm_call(gather_job([_pack_small(lambda n: small_part[n], loss_part[0, 0])]), "gather_small_grads")
    packed = adamw(small_all, _pack_small(lambda n: w[n])[None], _pack_small(lambda n: m[n])[None],
                   _pack_small(lambda n: v[n])[None], 0, None, "adamw_small")
    loss = packed[0][0, -1, -1]
    res = {}
    for key, arr in zip("gdmv", packed):
        for name, val in _unpack_small(arr[0]).items():
            res[key, name] = val

    off = 2 * (1024 + 1024)
    dmod_all = small_all.reshape(N_DEV, -1)[:, off:off + 2 * 6144].reshape(N_DEV, 2, 6144)
    dmod_cols = lax.dynamic_slice_in_dim(dmod_all, me * 768, 768, axis=2).transpose(1, 0, 2)
    g_ada = ada_bwd(c_all, dmod_cols)
    out = None
    for l in range(2):
        out = adamw(g_ada[l][None], w_ada, m_w_ada, v_w_ada, l, out, "adamw_w_ada")
    res.update(zip([(key, "w_ada") for key in "gdmv"], out))

    for name in BIG:
        view = (lambda a: jnp.swapaxes(a, 1, 2)) if name in TRANSPOSED else (lambda a: a)
        out = None
        for l in range(2):
            out = adamw(parts[l][name], view(w[name]), view(m[name]), view(v[name]), l, out, "adamw_" + name)
        res.update(zip([(key, name) for key in "gdmv"], [view(a) for a in out]))

    return (loss, grad_x, *[res["g", n] for n in WEIGHTS], *[res["d", n] for n in WEIGHTS],
            *[res["m", n] for n in WEIGHTS], *[res["v", n] for n in WEIGHTS])
```

```python
import functools

import jax
import jax.numpy as jnp
from jax import lax
from jax.experimental import pallas as pl
from jax.experimental.pallas import tpu as pltpu

f32 = jnp.float32
bf16 = jnp.bfloat16

N_DEV = 8
D_MODEL = 1024
N_HEADS = 8
HEAD_LANES = 128
NOPE = 64
ROPE = 32
V_DIM = 64
Q_RANK = 256
KV_RANK = 128
D_SSD = 512
D_CONV = 1024
SSD_STATE = 128
SSD_HEAD_DIM = 64
CHUNK = 128
HALO = 8
D_FF = 2816
FF_SHARD = 704
D_IN = 1960
D_PROJ = 2048
EPS = 1e-6
LOG2E = 1.4426950408889634
LN2 = 0.6931471805599453
Q_SCALE = (NOPE + ROPE) ** -0.5 * LOG2E
ATTN_ROWS_FWD = 256
ATTN_HEADS_FWD = 4
ATTN_HEADS_BWD = 2
MLP_BWD_ROWS = 512
ROPE_THETA = 10000.0
NEG = -1e30

ADAM_LR = 0.001
ADAM_B1 = 0.9
ADAM_B2 = 0.999
ADAM_EPS = 1e-08
ADAM_WD = 0.01
ADAM_STEP = 10
ADAMW_BLOCK_BYTES = 24 << 20

MESH = pl.DeviceIdType.MESH
ANY = pl.BlockSpec(memory_space=pl.ANY)

SMALL = (("norm1_w", 1024), ("norm2_w", 1024), ("b_ada", 6144), ("q_a_norm_w", 256), ("kv_a_norm_w", 128),
         ("q_nope_norm_w", 64), ("q_pe_norm_w", 32), ("k_nope_norm_w", 64), ("k_pe_norm_w", 32),
         ("conv_b", 1024), ("dt_bias", 8), ("a_log", 8), ("d_skip", 8), ("ssd_norm_w", 512))
SMALL_ROWS = 168
BIG = ("w_in", "w_q_up", "w_kv_up", "conv_w", "w_out", "w_gate_up", "w_down")
TRANSPOSED = ("w_gate_up",)
WEIGHTS = ("norm1_w", "norm2_w", "w_ada", "b_ada", "w_in", "q_a_norm_w", "w_q_up", "kv_a_norm_w", "w_kv_up",
           "q_nope_norm_w", "q_pe_norm_w", "k_nope_norm_w", "k_pe_norm_w", "conv_w", "conv_b", "dt_bias",
           "a_log", "d_skip", "ssd_norm_w", "w_out", "w_gate_up", "w_down")


def _dot(a, b, ca, cb):
    return lax.dot_general(a.astype(bf16), b.astype(bf16), (((ca,), (cb,)), ((), ())), preferred_element_type=f32)


@jax.custom_vjp
def mm(a, b):
    return _dot(a, b, 1, 0)


def _mm_fwd(a, b):
    return _dot(a, b, 1, 0), (a, b)


def _mm_bwd(res, g):
    a, b = res
    return _dot(g, b, 1, 1).astype(a.dtype), _dot(a, g, 0, 0).astype(b.dtype)


mm.defvjp(_mm_fwd, _mm_bwd)


@jax.custom_vjp
def _mm_slot(a, w, slot):
    return _dot(a, w, 1, 0)


def _mm_slot_fwd(a, w, slot):
    return _dot(a, w, 1, 0), (a, w)


def _mm_slot_bwd(res, g):
    a, w = res
    return _dot(g, w, 1, 1).astype(a.dtype), None, _dot(a, g, 0, 0)


_mm_slot.defvjp(_mm_slot_fwd, _mm_slot_bwd)


def mmw(a, w, slot=None):
    return _dot(a, w, 1, 0) if slot is None else _mm_slot(a, w, slot)


@jax.custom_vjp
def _mm_slot_t(a, wt, slot):
    return _dot(a, wt, 1, 1)


def _mm_slot_t_fwd(a, wt, slot):
    return _dot(a, wt, 1, 1), (a, wt)


def _mm_slot_t_bwd(res, g):
    a, wt = res
    return _dot(g, wt, 1, 0).astype(a.dtype), None, _dot(g, a, 0, 0)


_mm_slot_t.defvjp(_mm_slot_t_fwd, _mm_slot_t_bwd)


def mmw_t(a, wt, slot=None):
    return _dot(a, wt, 1, 1) if slot is None else _mm_slot_t(a, wt, slot)


@jax.custom_vjp
def mm_nt(a, b):
    return _dot(a, b, 1, 1)


def _mm_nt_fwd(a, b):
    return _dot(a, b, 1, 1), (a, b)


def _mm_nt_bwd(res, g):
    a, b = res
    return _dot(g, b, 1, 0).astype(a.dtype), _dot(g, a, 0, 0).astype(b.dtype)


mm_nt.defvjp(_mm_nt_fwd, _mm_nt_bwd)


@jax.custom_vjp
def mm_tn(a, b):
    return _dot(a, b, 0, 0)


def _mm_tn_fwd(a, b):
    return _dot(a, b, 0, 0), (a, b)


def _mm_tn_bwd(res, g):
    a, b = res
    return _dot(b, g, 1, 1).astype(a.dtype), _dot(a, g, 1, 0).astype(b.dtype)


mm_tn.defvjp(_mm_tn_fwd, _mm_tn_bwd)


def _rms(x, w):
    return x * lax.rsqrt(jnp.mean(x * x, axis=-1, keepdims=True) + EPS) * w


def _const(shape):
    n = len(shape)
    return pl.BlockSpec(shape, lambda *_: (0,) * n)


def _accumulate(first, refs, vals):
    @pl.when(first)
    def _():
        for r, v in zip(refs, vals):
            r[...] = v

    @pl.when(jnp.logical_not(first))
    def _():
        for r, v in zip(refs, vals):
            r[...] += v


def _accumulate_then_cast(first, last, accs, outs, vals):
    _accumulate(first, accs, vals)

    @pl.when(last)
    def _():
        for a, o in zip(accs, outs):
            o[...] = a[...].astype(o.dtype)


def _token_block(s):
    return min(512, s)


def _f_proj(x, nw, sh, sc, w, slot=None):
    h = _rms(x, nw) * (1.0 + sc) + sh
    return mmw(h, w, slot)


def _f_qkv(pa, plast, cos_t, sin_t, qaw, kvaw, wq, wk, wv, qnw, knw, kpw, slots=None):
    sq, sk, sv = slots if slots is not None else ([None] * N_HEADS,) * 3
    lane = lax.broadcasted_iota(jnp.int32, (1, HEAD_LANES), 1)
    m_nope = lane < NOPE
    m_pe = (lane >= NOPE) & (lane < NOPE + ROPE)
    rows = pa.shape[0]

    def rope(t):
        half = ROPE // 2
        swapped = jnp.concatenate(
            [jnp.zeros((rows, NOPE), f32), t[:, NOPE + half:NOPE + ROPE], t[:, NOPE:NOPE + half],
             jnp.zeros((rows, HEAD_LANES - NOPE - ROPE), f32)], axis=1)
        return t * cos_t + swapped * sin_t

    qa = _rms(pa[:, :Q_RANK], qaw)
    kva = _rms(pa[:, Q_RANK:Q_RANK + KV_RANK], kvaw)
    kp = jnp.where(m_pe, plast, 0.0)
    kp = kp * lax.rsqrt(jnp.sum(kp * kp, axis=-1, keepdims=True) / ROPE + EPS) * kpw
    k_rot = rope(kp)
    qs, ks, vs = [], [], []
    for h in range(N_HEADS):
        qh = mmw(qa, wq[h], sq[h])
        ss_n = jnp.sum(jnp.where(m_nope, qh * qh, 0.0), axis=-1, keepdims=True) / NOPE
        ss_p = jnp.sum(jnp.where(m_pe, qh * qh, 0.0), axis=-1, keepdims=True) / ROPE
        r = jnp.where(m_nope, lax.rsqrt(ss_n + EPS), lax.rsqrt(ss_p + EPS))
        qs.append(rope(qh * r * qnw) * Q_SCALE)
        kh = mmw(kva, wk[h], sk[h])
        kh = kh * lax.rsqrt(jnp.sum(kh * kh, axis=-1, keepdims=True) / NOPE + EPS) * knw
        ks.append(kh + k_rot)
        vs.append(mmw(kva, wv[h], sv[h]))
    return jnp.stack(qs), jnp.stack(ks), jnp.stack(vs)


def _f_ssd(xext, z, plast, prev, cw, cb, dtb, alog, dskip, snw):
    n = CHUNK
    conv = cb
    for k in range(4):
        conv = conv + cw[k:k + 1] * xext[HALO - 3 + k:HALO - 3 + k + n]
    xc = jax.nn.silu(conv)
    xs, bm, cm = xc[:, :D_SSD], xc[:, D_SSD:D_SSD + 2 * SSD_STATE], xc[:, D_SSD + 2 * SSD_STATE:]
    lane = lax.broadcasted_iota(jnp.int32, (1, 128), 1)
    dt = jax.nn.softplus(jnp.where(lane < N_HEADS, plast, 0.0) + dtb)
    adt = dt * (-jnp.exp(alog))
    row = lax.broadcasted_iota(jnp.int32, (n, n), 0)
    col = lax.broadcasted_iota(jnp.int32, (n, n), 1)
    tri = row >= col
    acs = jnp.dot(tri.astype(f32), adt, precision=lax.Precision.HIGHEST, preferred_element_type=f32)
    acs_t = acs.T
    bgs = [bm[:, g * SSD_STATE:(g + 1) * SSD_STATE] for g in range(2)]
    cgs = [cm[:, g * SSD_STATE:(g + 1) * SSD_STATE] for g in range(2)]
    cb_ts = [mm_nt(cgs[g], bgs[g]) for g in range(2)]
    low = lane < SSD_HEAD_DIM
    low_rows = lax.broadcasted_iota(jnp.int32, (2 * SSD_HEAD_DIM, 1), 0) < SSD_HEAD_DIM

    def both(a0, a1):
        return jnp.where(low, a0, a1)

    pre = []
    for i in range(N_HEADS // 2):
        h0, h1 = 2 * i, 2 * i + 1
        col0, col1 = acs[:, h0:h0 + 1], acs[:, h1:h1 + 1]
        last0, last1 = acs[n - 1:n, h0:h0 + 1], acs[n - 1:n, h1:h1 + 1]
        cb_t = cb_ts[i // 2]
        scores0 = cb_t * jnp.exp(jnp.where(tri, col0 - acs_t[h0:h0 + 1, :], -jnp.inf))
        scores1 = cb_t * jnp.exp(jnp.where(tri, col1 - acs_t[h1:h1 + 1, :], -jnp.inf))
        xp = xs[:, i * 128:(i + 1) * 128]
        xdt = xp * both(dt[:, h0:h0 + 1], dt[:, h1:h1 + 1])
        weighted = xdt * both(jnp.exp(last0 - col0), jnp.exp(last1 - col1))
        chunk_decay = jnp.where(low_rows, jnp.exp(last0), jnp.exp(last1))
        in_decay = both(jnp.exp(col0), jnp.exp(col1))
        skip = both(dskip[:, h0:h0 + 1], dskip[:, h1:h1 + 1]) * xp
        pre.append((scores0, scores1, xdt, weighted, chunk_decay, in_decay, skip))
    prods = []
    for i in range(N_HEADS // 2):
        scores0, scores1, xdt, weighted, _, _, _ = pre[i]
        g = i // 2
        y_diag = mm(scores0, jnp.where(low, xdt, 0.0)) + mm(scores1, jnp.where(low, 0.0, xdt))
        prods.append((y_diag, mm_tn(weighted, bgs[g]), mm_nt(cgs[g], prev[i])))
    ys, news = [], []
    for i in range(N_HEADS // 2):
        y_diag, st, y_off = prods[i]
        _, _, _, _, chunk_decay, in_decay, skip = pre[i]
        news.append(chunk_decay * prev[i] + st)
        ys.append(y_diag + y_off * in_decay + skip)
    y = jnp.concatenate(ys, axis=1)
    yg = y * jax.nn.silu(z)
    half = D_SSD // 2
    outs = []
    for g in range(2):
        t = yg[:, g * half:(g + 1) * half]
        outs.append(t * lax.rsqrt(jnp.mean(t * t, axis=-1, keepdims=True) + EPS))
    return jnp.concatenate(outs, axis=1) * snw, jnp.stack(news)


def _f_out(o, yg, g1, wo, slot=None):
    cat = jnp.concatenate([o[h] for h in range(N_HEADS)] + [yg], axis=1)
    return g1 * mmw(cat, wo, slot)


def _f_gate_up(x, nw, sh, sc, wg, wu, slot_g=None, slot_u=None):
    h = _rms(x, nw) * (1.0 + sc) + sh
    return jax.nn.silu(mmw_t(h, wg, slot_g)) * mmw_t(h, wu, slot_u)


def proj_fwd(x, nw, sh, sc, w):
    s = x.shape[0]
    ts = _token_block(s)

    def body(x_ref, nw_ref, sh_ref, sc_ref, w_ref, pa_ref, pz_ref, px_ref, pl_ref):
        p = _f_proj(x_ref[...], nw_ref[...], sh_ref[...], sc_ref[...], w_ref[...])
        pa_ref[...] = p[:, :384]
        pz_ref[...] = p[:, 384:896]
        px_ref[...] = p[:, 896:1920]
        pl_ref[...] = p[:, 1920:]

    vec = _const((1, D_MODEL))
    return pl.pallas_call(
        body, name="proj_fwd", grid=(s // ts,),
        in_specs=[pl.BlockSpec((ts, D_MODEL), lambda i: (i, 0)), vec, vec, vec, _const((D_MODEL, D_PROJ))],
        out_specs=[pl.BlockSpec((ts, 384), lambda i: (i, 0)), pl.BlockSpec((ts, 512), lambda i: (i, 0)),
                   pl.BlockSpec((ts, 1024), lambda i: (i, 0)), pl.BlockSpec((ts, 128), lambda i: (i, 0))],
        out_shape=[jax.ShapeDtypeStruct((s, 384), f32), jax.ShapeDtypeStruct((s, 512), f32),
                   jax.ShapeDtypeStruct((s, 1024), f32), jax.ShapeDtypeStruct((s, 128), f32)],
    )(x, nw, sh, sc, w)


def rope_tables(pos, inv):
    s = pos.shape[0]
    ts = _token_block(s)

    def body(pos_ref, inv_ref, cos_ref, sin_ref):
        ang = pos_ref[...].astype(f32) * inv_ref[...]
        lane = lax.broadcasted_iota(jnp.int32, (1, HEAD_LANES), 1)
        half = ROPE // 2
        cos_ref[...] = jnp.where(lane < NOPE, 1.0, jnp.where(lane < NOPE + ROPE, jnp.cos(ang), 0.0))
        sn = jnp.sin(ang)
        sin_ref[...] = jnp.where((lane >= NOPE) & (lane < NOPE + half), -sn,
                                 jnp.where((lane >= NOPE + half) & (lane < NOPE + ROPE), sn, 0.0))

    return pl.pallas_call(
        body, name="rope_tables", grid=(s // ts,),
        in_specs=[pl.BlockSpec((ts, 1), lambda i: (i, 0)), _const((1, HEAD_LANES))],
        out_specs=[pl.BlockSpec((ts, HEAD_LANES), lambda i: (i, 0))] * 2,
        out_shape=[jax.ShapeDtypeStruct((s, HEAD_LANES), f32)] * 2,
    )(pos, inv)


def _qkv_param_specs():
    return [_const((1, Q_RANK)), _const((1, KV_RANK)), _const((N_HEADS, Q_RANK, HEAD_LANES)),
            _const((N_HEADS, KV_RANK, HEAD_LANES)), _const((N_HEADS, KV_RANK, V_DIM)),
            _const((1, HEAD_LANES)), _const((1, HEAD_LANES)), _const((1, HEAD_LANES))]


def qkv_fwd(pa, plast, cos_t, sin_t, params):
    s = pa.shape[0]
    ts = _token_block(s)

    def body(pa_ref, pl_ref, cos_ref, sin_ref, *rest):
        prm = [r[...] for r in rest[:8]]
        q_ref, k_ref, v_ref = rest[8:]
        q, k, v = _f_qkv(pa_ref[...], pl_ref[...], cos_ref[...], sin_ref[...], *prm)
        q_ref[...] = q.astype(bf16)
        k_ref[...] = k.astype(bf16)
        v_ref[...] = jnp.concatenate([v, jnp.ones_like(v)], axis=-1).astype(bf16)

    tok = lambda w: pl.BlockSpec((ts, w), lambda i: (i, 0))
    head = pl.BlockSpec((N_HEADS, ts, HEAD_LANES), lambda i: (0, i, 0))
    return pl.pallas_call(
        body, name="qkv_fwd", grid=(s // ts,),
        in_specs=[tok(384), tok(128), tok(128), tok(128)] + _qkv_param_specs(),
        out_specs=[head] * 3, out_shape=[jax.ShapeDtypeStruct((N_HEADS, s, HEAD_LANES), bf16)] * 3,
    )(pa, plast, cos_t, sin_t, *params)


def _scores(q, k):
    return lax.dot_general(q, k, (((1,), (1,)), ((), ())), preferred_element_type=f32)


def _tril(rows, cols, row_offset):
    row = row_offset + lax.broadcasted_iota(jnp.int32, (rows, cols), 0)
    col = lax.broadcasted_iota(jnp.int32, (rows, cols), 1)
    return row >= col


def _call_with_job(body, name, grid, job, in_specs, out_specs, out_shape, scratch_shapes, operands, relay_at=None):
    if job is None:
        res = pl.pallas_call(body, name=name, grid=grid, in_specs=in_specs, out_specs=out_specs, out_shape=out_shape,
                             scratch_shapes=scratch_shapes)(*operands)
        return res, None

    def at_step(i, n):
        if i == 0:
            want = [0] * len(grid)
        elif i == n - 1:
            want = [g - 1 for g in grid]
        else:
            want = relay_at
        return functools.reduce(jnp.logical_and, [pl.program_id(a) == s for a, s in enumerate(want)])

    carrier = _carry(job, body, len(in_specs), len(out_specs), at_step)
    res = pl.pallas_call(
        carrier, name=name, grid=grid,
        in_specs=list(in_specs) + [ANY] * len(job.operands), out_specs=list(out_specs) + [ANY] * len(job.out_shape),
        out_shape=list(out_shape) + list(job.out_shape), scratch_shapes=list(scratch_shapes) + job.scratch,
    )(*operands, *job.operands)
    return res[:len(out_specs)], res[len(out_specs):]


def attn_fwd(q, k, v, job=None):
    s = q.shape[1]
    t = _token_block(s)
    nb = s // t

    rb = min(ATTN_ROWS_FWD, t)

    hp = ATTN_HEADS_FWD

    def body(q_ref, k_ref, v_ref, o_ref, lse_ref, m_sc, acc_sc):
        qi = pl.program_id(1)
        m_sc[...] = jnp.full(m_sc.shape, NEG, f32)
        acc_sc[...] = jnp.zeros(acc_sc.shape, f32)

        def step(k0, diagonal):
            chains = [(hh, r) for hh in range(hp) for r in range(t // rb)]

            def scores(hh, r):
                nk = (r + 1) * rb if diagonal else t
                sc = _scores(q_ref[hh, pl.ds(r * rb, rb), :], k_ref[hh, pl.ds(k0, nk), :])
                return jnp.where(_tril(rb, nk, r * rb), sc, NEG) if diagonal else sc

            ahead = scores(*chains[0])
            for c, (hh, r) in enumerate(chains):
                sc = ahead
                if c + 1 < len(chains):
                    ahead = scores(*chains[c + 1])
                rows = pl.ds(r * rb, rb)
                keys = pl.ds(k0, (r + 1) * rb if diagonal else t)
                m_prev = m_sc[hh, rows, :1]
                m_new = jnp.maximum(m_prev, jnp.max(sc, axis=-1, keepdims=True))
                p = jnp.exp2(sc - m_new)
                alpha = jnp.exp2(m_prev - m_new)
                acc = alpha * acc_sc[hh, rows, :] + jnp.dot(p.astype(bf16), v_ref[hh, keys, :], preferred_element_type=f32)
                if diagonal:
                    l = acc[:, V_DIM:V_DIM + 1]
                    o_ref[hh, rows, :] = acc[:, :V_DIM] / l
                    lse_ref[hh, rows, :] = jnp.broadcast_to(m_new + jnp.log2(l), (rb, 128))
                else:
                    acc_sc[hh, rows, :] = acc
                    m_sc[hh, rows, :] = jnp.broadcast_to(m_new, (rb, 128))

        def below(ki, carry):
            step(pl.multiple_of(ki * t, t), False)
            return carry

        lax.fori_loop(0, qi, below, 0)
        step(pl.multiple_of(qi * t, t), True)

    return _call_with_job(
        body, "attn_fwd" if job is None else "attn_fwd_comm", (N_HEADS // hp, nb), job,
        in_specs=[pl.BlockSpec((hp, t, HEAD_LANES), lambda h, qi: (h, qi, 0)),
                  pl.BlockSpec((hp, s, HEAD_LANES), lambda h, qi: (h, 0, 0)),
                  pl.BlockSpec((hp, s, HEAD_LANES), lambda h, qi: (h, 0, 0))],
        out_specs=[pl.BlockSpec((hp, t, V_DIM), lambda h, qi: (h, qi, 0)),
                   pl.BlockSpec((hp, t, 128), lambda h, qi: (h, qi, 0))],
        out_shape=[jax.ShapeDtypeStruct((N_HEADS, s, V_DIM), f32), jax.ShapeDtypeStruct((N_HEADS, s, 128), f32)],
        scratch_shapes=[pltpu.VMEM((hp, t, 128), f32), pltpu.VMEM((hp, t, HEAD_LANES), f32)],
        operands=(q, k, v), relay_at=(N_HEADS // hp - 1, max(nb - 2, 0)))


def _ssd_param_specs():
    return [_const((4, D_CONV)), _const((1, D_CONV)), _const((1, 128)), _const((1, 128)), _const((1, 128)),
            _const((1, D_SSD))]


def ssd_fwd(px, pz, plast, params):
    s = px.shape[0]
    nc = s // CHUNK

    def body(px_ref, pz_ref, pl_ref, cw_ref, cb_ref, dtb_ref, alog_ref, dskip_ref, snw_ref, yg_ref, st_ref,
             state_sc, halo_sc):
        i = pl.program_id(0)

        @pl.when(i == 0)
        def _():
            state_sc[...] = jnp.zeros(state_sc.shape, f32)
            halo_sc[...] = jnp.zeros(halo_sc.shape, f32)

        x = px_ref[...]
        prev = state_sc[...]
        st_ref[...] = prev
        xext = jnp.concatenate([halo_sc[...], x], axis=0)
        yg, new = _f_ssd(xext, pz_ref[...], pl_ref[...], prev, cw_ref[...], cb_ref[...], dtb_ref[...],
                         alog_ref[...], dskip_ref[...], snw_ref[...])
        yg_ref[...] = yg
        state_sc[...] = new
        halo_sc[...] = x[CHUNK - HALO:]

    tok = lambda w: pl.BlockSpec((CHUNK, w), lambda i: (i, 0))
    return pl.pallas_call(
        body, name="ssd_fwd", grid=(nc,),
        in_specs=[tok(D_CONV), tok(D_SSD), tok(128)] + _ssd_param_specs(),
        out_specs=[tok(D_SSD), pl.BlockSpec((None, N_HEADS // 2, 2 * SSD_HEAD_DIM, SSD_STATE), lambda i: (i, 0, 0, 0))],
        out_shape=[jax.ShapeDtypeStruct((s, D_SSD), f32),
                   jax.ShapeDtypeStruct((nc, N_HEADS // 2, 2 * SSD_HEAD_DIM, SSD_STATE), f32)],
        scratch_shapes=[pltpu.VMEM((N_HEADS // 2, 2 * SSD_HEAD_DIM, SSD_STATE), f32), pltpu.VMEM((HALO, D_CONV), f32)],
    )(px, pz, plast, *params)


def out_fwd(x, o, yg, g1, wo):
    s = x.shape[0]
    ts = _token_block(s)

    def body(x_ref, o_ref, yg_ref, g1_ref, wo_ref, out_ref):
        out_ref[...] = x_ref[...] + _f_out(o_ref[...], yg_ref[...], g1_ref[...], wo_ref[...])

    return pl.pallas_call(
        body, name="out_fwd", grid=(s // ts,),
        in_specs=[pl.BlockSpec((ts, D_MODEL), lambda i: (i, 0)), pl.BlockSpec((N_HEADS, ts, V_DIM), lambda i: (0, i, 0)),
                  pl.BlockSpec((ts, D_SSD), lambda i: (i, 0)), _const((1, D_MODEL)), _const((D_MODEL, D_MODEL))],
        out_specs=pl.BlockSpec((ts, D_MODEL), lambda i: (i, 0)),
        out_shape=jax.ShapeDtypeStruct((s, D_MODEL), f32),
    )(x, o, yg, g1, wo)


def mlp_fwd(x, nw, sh, sc, g2, wgu, wd):
    s = x.shape[0]
    ts = _token_block(s)
    nj = N_DEV // 2

    def body(x_ref, nw_ref, sh_ref, sc_ref, g2_ref, wg_ref, wu_ref, wd_ref, out_ref, mix_ref):
        j = pl.program_id(1)
        act = _f_gate_up(x_ref[...], nw_ref[...], sh_ref[...], sc_ref[...], wg_ref[...], wu_ref[...])
        _accumulate(j == 0, [mix_ref], [mmw(act, wd_ref[...])])

        @pl.when(j == nj - 1)
        def _():
            out_ref[...] = x_ref[...] + g2_ref[...] * mix_ref[...]

    vec = _const((1, D_MODEL))
    return pl.pallas_call(
        body, name="mlp_fwd", grid=(s // ts, nj),
        in_specs=[pl.BlockSpec((ts, D_MODEL), lambda i, j: (i, 0)), vec, vec, vec, vec,
                  pl.BlockSpec((None, FF_SHARD, D_MODEL), lambda i, j: (j, 0, 0)),
                  pl.BlockSpec((None, FF_SHARD, D_MODEL), lambda i, j: (j + nj, 0, 0)),
                  pl.BlockSpec((None, FF_SHARD, D_MODEL), lambda i, j: (j, 0, 0))],
        out_specs=[pl.BlockSpec((ts, D_MODEL), lambda i, j: (i, 0))] * 2,
        out_shape=[jax.ShapeDtypeStruct((s, D_MODEL), f32)] * 2,
    )(x, nw, sh, sc, g2, wgu, wgu, wd)


def loss_fwd(y, target):
    s = y.shape[0]
    ts = _token_block(s)

    def body(y_ref, t_ref, dy_ref, loss_ref):
        d = y_ref[...] - t_ref[...]
        dy_ref[...] = d * (1.0 / D_MODEL)
        part = 0.5 * jnp.sum(jnp.sum(d * d, axis=-1, keepdims=True) * (1.0 / D_MODEL), axis=0, keepdims=True)
        _accumulate(pl.program_id(0) == 0, [loss_ref], [jnp.broadcast_to(part, (8, 128))])

    return pl.pallas_call(
        body, name="loss_fwd", grid=(s // ts,),
        in_specs=[pl.BlockSpec((ts, D_MODEL), lambda i: (i, 0))] * 2,
        out_specs=[pl.BlockSpec((ts, D_MODEL), lambda i: (i, 0)), _const((8, 128))],
        out_shape=[jax.ShapeDtypeStruct((s, D_MODEL), f32), jax.ShapeDtypeStruct((8, 128), f32)],
    )(y, target)


def mlp_bwd(x, dy, nw, sh, sc, g2, wgu, wd, job=None):
    s = x.shape[0]
    ts = min(MLP_BWD_ROWS, s)
    nj = N_DEV // 2
    ni = s // ts

    def body(x_ref, dy_ref, nw_ref, sh_ref, sc_ref, g2_ref, wg_ref, wu_ref, wd_ref,
             dx_ref, dnw_ref, dsh_ref, dsc_ref, dwg_ref, dwu_ref, dwd_ref, ag_sc, au_sc, ad_sc):
        j, i = pl.program_id(0), pl.program_id(1)
        wg, wu, wd = wg_ref[...], wu_ref[...], wd_ref[...]
        act, vjp = jax.vjp(lambda x_, nw_, sh_, sc_, sg, su: _f_gate_up(x_, nw_, sh_, sc_, wg, wu, sg, su),
                           x_ref[...], nw_ref[...], sh_ref[...], sc_ref[...],
                           jnp.zeros(wg.shape, f32), jnp.zeros(wu.shape, f32))
        dmix = dy_ref[...] * g2_ref[...]
        dact = _dot(dmix, wd, 1, 1)
        dwd = _dot(act, dmix, 0, 0)
        dx, dnw, dsh, dsc, dwg, dwu = vjp(dact)
        dx_ref[...] = dx.astype(bf16)
        _accumulate((i == 0) & (j == 0), [dnw_ref, dsh_ref, dsc_ref], [dnw, dsh, dsc])
        _accumulate_then_cast(i == 0, i == ni - 1, [ag_sc, au_sc, ad_sc], [dwg_ref, dwu_ref, dwd_ref], [dwg, dwu, dwd])

    vec = _const((1, D_MODEL))
    vshape = jax.ShapeDtypeStruct((1, D_MODEL), f32)
    once = pl.Buffered(1)
    wspec = lambda off: pl.BlockSpec((None, FF_SHARD, D_MODEL), lambda j, i: (j + off, 0, 0), pipeline_mode=once)
    dspec = pl.BlockSpec((None, FF_SHARD, D_MODEL), lambda j, i: (j, 0, 0), pipeline_mode=once)
    return _call_with_job(
        body, "mlp_bwd" if job is None else "mlp_bwd_comm", (nj, ni), job,
        in_specs=[pl.BlockSpec((ts, D_MODEL), lambda j, i: (i, 0)), pl.BlockSpec((ts, D_MODEL), lambda j, i: (i, 0)),
                  vec, vec, vec, vec, wspec(0), wspec(nj), dspec],
        out_specs=[pl.BlockSpec((None, ts, D_MODEL), lambda j, i: (j, i, 0)), vec, vec, vec,
                   wspec(0), wspec(0), dspec],
        out_shape=[jax.ShapeDtypeStruct((nj, s, D_MODEL), bf16), vshape, vshape, vshape,
                   jax.ShapeDtypeStruct((nj, FF_SHARD, D_MODEL), bf16), jax.ShapeDtypeStruct((nj, FF_SHARD, D_MODEL), bf16),
                   jax.ShapeDtypeStruct((nj, FF_SHARD, D_MODEL), bf16)],
        scratch_shapes=[pltpu.VMEM((FF_SHARD, D_MODEL), f32), pltpu.VMEM((FF_SHARD, D_MODEL), f32),
                        pltpu.VMEM((FF_SHARD, D_MODEL), f32)],
        operands=(x, dy, nw, sh, sc, g2, wgu, wgu, wd))


def out_bwd(dy, dparts, mix, o, yg, g1, wo):
    s = dy.shape[0]
    ts = _token_block(s)
    nj = dparts.shape[0]

    ni = s // ts

    def body(dy_ref, dp_ref, mix_ref, o_ref, yg_ref, g1_ref, wo_ref, dx_ref, do_ref, delta_ref, dyg_ref, dg1_ref,
             dg2_ref, dwo_ref, acc_sc):
        i = pl.program_id(0)
        g = dy_ref[...]
        _accumulate(i == 0, [dg2_ref], [jnp.sum(g * mix_ref[...], axis=0, keepdims=True)])
        for j in range(nj):
            g = g + dp_ref[j].astype(f32)
        dx_ref[...] = g
        o = o_ref[...]
        wo = wo_ref[...]
        _, vjp = jax.vjp(lambda o_, yg_, g1_, slot: _f_out(o_, yg_, g1_, wo, slot), o, yg_ref[...], g1_ref[...],
                         jnp.zeros(wo.shape, f32))
        do, dyg, dg1, dwo = vjp(g)
        do_ref[...] = do.astype(bf16)
        dyg_ref[...] = dyg
        delta_ref[...] = jnp.broadcast_to(jnp.sum(do * o, axis=-1, keepdims=True), delta_ref.shape)
        _accumulate(i == 0, [dg1_ref], [dg1])
        _accumulate_then_cast(i == 0, i == ni - 1, [acc_sc], [dwo_ref], [dwo])

    head = pl.BlockSpec((N_HEADS, ts, V_DIM), lambda i: (0, i, 0))
    return pl.pallas_call(
        body, name="out_bwd", grid=(ni,), scratch_shapes=[pltpu.VMEM((D_MODEL, D_MODEL), f32)],
        in_specs=[pl.BlockSpec((ts, D_MODEL), lambda i: (i, 0)), pl.BlockSpec((nj, ts, D_MODEL), lambda i: (0, i, 0)),
                  pl.BlockSpec((ts, D_MODEL), lambda i: (i, 0)),
                  head, pl.BlockSpec((ts, D_SSD), lambda i: (i, 0)), _const((1, D_MODEL)), _const((D_MODEL, D_MODEL))],
        out_specs=[pl.BlockSpec((ts, D_MODEL), lambda i: (i, 0)), head,
                   pl.BlockSpec((N_HEADS, ts, 128), lambda i: (0, i, 0)), pl.BlockSpec((ts, D_SSD), lambda i: (i, 0)),
                   _const((1, D_MODEL)), _const((1, D_MODEL)), _const((D_MODEL, D_MODEL))],
        out_shape=[jax.ShapeDtypeStruct((s, D_MODEL), f32), jax.ShapeDtypeStruct((N_HEADS, s, V_DIM), bf16),
                   jax.ShapeDtypeStruct((N_HEADS, s, 128), f32), jax.ShapeDtypeStruct((s, D_SSD), f32),
                   jax.ShapeDtypeStruct((1, D_MODEL), f32), jax.ShapeDtypeStruct((1, D_MODEL), f32),
                   jax.ShapeDtypeStruct((D_MODEL, D_MODEL), bf16)],
    )(dy, dparts, mix, o, yg, g1, wo)


def attn_bwd(q, k, v, do, lse, delta, job=None):
    s = q.shape[1]
    t = _token_block(s)
    nb = s // t

    hp = ATTN_HEADS_BWD

    def body(q_ref, k_ref, v_ref, do_ref, lse_ref, delta_ref, dq_ref, dk_ref, dv_ref):
        ki = pl.program_id(1)

        @pl.when(ki == 0)
        def _():
            dq_ref[...] = jnp.zeros(dq_ref.shape, f32)

        dk_ref[...] = jnp.zeros(dk_ref.shape, f32)
        dv_ref[...] = jnp.zeros(dv_ref.shape, f32)

        def step(q0, diagonal):
            rows = pl.ds(q0, t)

            def products(hh):
                sc = _scores(q_ref[hh, rows, :], k_ref[hh])
                dp = lax.dot_general(do_ref[hh, rows, :], v_ref[hh, :, :V_DIM], (((1,), (1,)), ((), ())),
                                     preferred_element_type=f32)
                return (jnp.where(_tril(t, t, 0), sc, NEG) if diagonal else sc), dp

            ahead = products(0)
            for hh in range(hp):
                sc, dp = ahead
                if hh + 1 < hp:
                    ahead = products(hh + 1)
                p = jnp.exp2(sc - jnp.tile(lse_ref[hh, rows, :], (1, t // 128)))
                ds = (p * (dp - jnp.tile(delta_ref[hh, rows, :], (1, t // 128)))).astype(bf16)
                dv_ref[hh] += lax.dot_general(p.astype(bf16), do_ref[hh, rows, :], (((0,), (0,)), ((), ())),
                                              preferred_element_type=f32)
                dk_ref[hh] += lax.dot_general(ds, q_ref[hh, rows, :], (((0,), (0,)), ((), ())), preferred_element_type=f32)
                dq_ref[hh, rows, :] += jnp.dot(ds, k_ref[hh], preferred_element_type=f32)

        step(pl.multiple_of(ki * t, t), True)

        def above(qi, carry):
            step(pl.multiple_of(qi * t, t), False)
            return carry

        lax.fori_loop(ki + 1, nb, above, 0)
        dk_ref[...] = dk_ref[...] * LN2

        @pl.when(ki == nb - 1)
        def _():
            dq_ref[...] = dq_ref[...] * LN2

    qspec = lambda w: pl.BlockSpec((hp, s, w), lambda h, ki: (h, 0, 0))
    kspec = lambda w: pl.BlockSpec((hp, t, w), lambda h, ki: (h, ki, 0))
    return _call_with_job(
        body, "attn_bwd" if job is None else "attn_bwd_comm", (N_HEADS // hp, nb), job,
        in_specs=[qspec(HEAD_LANES), kspec(HEAD_LANES), kspec(HEAD_LANES), qspec(V_DIM), qspec(128), qspec(128)],
        out_specs=[qspec(HEAD_LANES), kspec(HEAD_LANES), kspec(V_DIM)],
        out_shape=[jax.ShapeDtypeStruct((N_HEADS, s, HEAD_LANES), f32), jax.ShapeDtypeStruct((N_HEADS, s, HEAD_LANES), f32),
                   jax.ShapeDtypeStruct((N_HEADS, s, V_DIM), f32)],
        scratch_shapes=[], operands=(q, k, v, do, lse, delta))


def ssd_bwd(px, pz, plast, states, dyg, params):
    s = px.shape[0]
    nc = s // CHUNK
    per = CHUNK // HALO

    def body(px_ref, halo_ref, pz_ref, pl_ref, st_ref, dyg_ref, cw_ref, cb_ref, dtb_ref, alog_ref, dskip_ref, snw_ref,
             dpx_ref, dpz_ref, dpl_ref, dcw_ref, dcb_ref, ddtb_ref, dalog_ref, ddskip_ref, dsnw_ref, dstate_sc, dhalo_sc):
        t = pl.program_id(0)
        chunk = nc - 1 - t

        @pl.when(t == 0)
        def _():
            dstate_sc[...] = jnp.zeros(dstate_sc.shape, f32)
            dhalo_sc[...] = jnp.zeros(dhalo_sc.shape, f32)

        halo = jnp.where(chunk > 0, halo_ref[...], 0.0)
        xext = jnp.concatenate([halo, px_ref[...]], axis=0)
        _, vjp = jax.vjp(_f_ssd, xext, pz_ref[...], pl_ref[...], st_ref[...], cw_ref[...], cb_ref[...], dtb_ref[...],
                         alog_ref[...], dskip_ref[...], snw_ref[...])
        dxext, dz, dpl, dprev, dcw, dcb, ddtb, dalog, ddskip, dsnw = vjp((dyg_ref[...], dstate_sc[...]))
        dpx_ref[...] = dxext[HALO:]
        dpx_ref[CHUNK - HALO:, :] += dhalo_sc[...]
        dhalo_sc[...] = dxext[:HALO]
        dstate_sc[...] = dprev
        dpz_ref[...] = dz
        dpl_ref[...] = dpl
        _accumulate(t == 0, [dcw_ref, dcb_ref, ddtb_ref, dalog_ref, ddskip_ref, dsnw_ref],
                    [dcw, dcb, ddtb, dalog, ddskip, dsnw])

    rev = lambda w: pl.BlockSpec((CHUNK, w), lambda t: (nc - 1 - t, 0))
    pshapes = [jax.ShapeDtypeStruct((4, D_CONV), f32), jax.ShapeDtypeStruct((1, D_CONV), f32),
               jax.ShapeDtypeStruct((1, 128), f32), jax.ShapeDtypeStruct((1, 128), f32),
               jax.ShapeDtypeStruct((1, 128), f32), jax.ShapeDtypeStruct((1, D_SSD), f32)]
    return pl.pallas_call(
        body, name="ssd_bwd", grid=(nc,),
        in_specs=[rev(D_CONV),
                  pl.BlockSpec((HALO, D_CONV), lambda t: (jnp.maximum((nc - 1 - t) * per - 1, 0), 0)),
                  rev(D_SSD), rev(128),
                  pl.BlockSpec((None, N_HEADS // 2, 2 * SSD_HEAD_DIM, SSD_STATE), lambda t: (nc - 1 - t, 0, 0, 0)),
                  rev(D_SSD)] + _ssd_param_specs(),
        out_specs=[rev(D_CONV), rev(D_SSD), rev(128)] + _ssd_param_specs(),
        out_shape=[jax.ShapeDtypeStruct((s, D_CONV), f32), jax.ShapeDtypeStruct((s, D_SSD), f32),
                   jax.ShapeDtypeStruct((s, 128), f32)] + pshapes,
        scratch_shapes=[pltpu.VMEM((N_HEADS // 2, 2 * SSD_HEAD_DIM, SSD_STATE), f32), pltpu.VMEM((HALO, D_CONV), f32)],
    )(px, px, pz, plast, states, dyg, *params)


def qkv_bwd(pa, plast, cos_t, sin_t, params, dq, dk, dv):
    s = pa.shape[0]
    ts = _token_block(s)

    def body(pa_ref, pl_ref, cos_ref, sin_ref, *rest):
        qaw, kvaw, wq, wk, wv, qnw, knw, kpw = [r[...] for r in rest[:8]]
        dq_ref, dk_ref, dv_ref = rest[8:11]
        dpa_ref, dpl_ref = rest[11:13]
        dprm_refs = list(rest[13:])
        cos_t, sin_t = cos_ref[...], sin_ref[...]

        def stage(pa_, pl_, qaw_, kvaw_, sq, sk, sv, qnw_, knw_, kpw_):
            return _f_qkv(pa_, pl_, cos_t, sin_t, qaw_, kvaw_, wq, wk, wv, qnw_, knw_, kpw_, (sq, sk, sv))

        _, vjp = jax.vjp(stage, pa_ref[...], pl_ref[...], qaw, kvaw, jnp.zeros(wq.shape, f32), jnp.zeros(wk.shape, f32),
                         jnp.zeros(wv.shape, f32), qnw, knw, kpw)
        grads = vjp((dq_ref[...], dk_ref[...], dv_ref[...]))
        dpa_ref[...] = grads[0]
        dpl_ref[...] = grads[1]
        _accumulate(pl.program_id(0) == 0, dprm_refs, list(grads[2:]))

    tok = lambda w: pl.BlockSpec((ts, w), lambda i: (i, 0))
    head = lambda w: pl.BlockSpec((N_HEADS, ts, w), lambda i: (0, i, 0))
    pshapes = [jax.ShapeDtypeStruct((1, Q_RANK), f32), jax.ShapeDtypeStruct((1, KV_RANK), f32),
               jax.ShapeDtypeStruct((N_HEADS, Q_RANK, HEAD_LANES), f32), jax.ShapeDtypeStruct((N_HEADS, KV_RANK, HEAD_LANES), f32),
               jax.ShapeDtypeStruct((N_HEADS, KV_RANK, V_DIM), f32), jax.ShapeDtypeStruct((1, HEAD_LANES), f32),
               jax.ShapeDtypeStruct((1, HEAD_LANES), f32), jax.ShapeDtypeStruct((1, HEAD_LANES), f32)]
    return pl.pallas_call(
        body, name="qkv_bwd", grid=(s // ts,),
        in_specs=[tok(384), tok(128), tok(128), tok(128)] + _qkv_param_specs()
                 + [head(HEAD_LANES), head(HEAD_LANES), head(V_DIM)],
        out_specs=[tok(384), tok(128)] + _qkv_param_specs(),
        out_shape=[jax.ShapeDtypeStruct((s, 384), f32), jax.ShapeDtypeStruct((s, 128), f32)] + pshapes,
    )(pa, plast, cos_t, sin_t, *params, dq, dk, dv)


def proj_bwd(x, nw, sh, sc, w, dpa, dpz, dpx, dpl_k, dpl_dt, dres):
    s = x.shape[0]
    ts = _token_block(s)

    ni = s // ts

    def body(x_ref, nw_ref, sh_ref, sc_ref, w_ref, dpa_ref, dpz_ref, dpx_ref, dplk_ref, dpld_ref, dres_ref,
             dx_ref, dnw_ref, dsh_ref, dsc_ref, dw_ref, acc_sc):
        i = pl.program_id(0)
        g = jnp.concatenate([dpa_ref[...], dpz_ref[...], dpx_ref[...], dplk_ref[...] + dpld_ref[...]], axis=1)
        w = w_ref[...]
        _, vjp = jax.vjp(lambda x_, nw_, sh_, sc_, slot: _f_proj(x_, nw_, sh_, sc_, w, slot), x_ref[...], nw_ref[...],
                         sh_ref[...], sc_ref[...], jnp.zeros(w.shape, f32))
        dx, dnw, dsh, dsc, dw = vjp(g)
        dx_ref[...] = dx + dres_ref[...]
        _accumulate(i == 0, [dnw_ref, dsh_ref, dsc_ref], [dnw, dsh, dsc])
        _accumulate_then_cast(i == 0, i == ni - 1, [acc_sc], [dw_ref], [dw])

    vec = _const((1, D_MODEL))
    vshape = jax.ShapeDtypeStruct((1, D_MODEL), f32)
    tok = lambda w_: pl.BlockSpec((ts, w_), lambda i: (i, 0))
    return pl.pallas_call(
        body, name="proj_bwd", grid=(ni,), scratch_shapes=[pltpu.VMEM((D_MODEL, D_PROJ), f32)],
        in_specs=[tok(D_MODEL), vec, vec, vec, _const((D_MODEL, D_PROJ)), tok(384), tok(512), tok(1024), tok(128), tok(128),
                  tok(D_MODEL)],
        out_specs=[tok(D_MODEL), vec, vec, vec, _const((D_MODEL, D_PROJ))],
        out_shape=[jax.ShapeDtypeStruct((s, D_MODEL), f32), vshape, vshape, vshape,
                   jax.ShapeDtypeStruct((D_MODEL, D_PROJ), bf16)],
    )(x, nw, sh, sc, w, dpa, dpz, dpx, dpl_k, dpl_dt, dres)


def ada_fwd(c_all, w_ada, b_cols):
    def body(c_ref, w_ref, b_ref, out_ref):
        act = jax.nn.silu(c_ref[...])
        for l in range(2):
            out_ref[l] = jnp.dot(act, w_ref[l], precision=lax.Precision.HIGHEST, preferred_element_type=f32) + b_ref[l]

    return pl.pallas_call(body, name="ada_fwd", out_shape=jax.ShapeDtypeStruct((2, N_DEV, 768), f32))(c_all, w_ada, b_cols)


def ada_bwd(c_all, dmod_cols):
    def body(c_ref, d_ref, out_ref):
        out_ref[0] = lax.dot_general(jax.nn.silu(c_ref[...]), d_ref[0], (((0,), (0,)), ((), ())),
                                     precision=lax.Precision.HIGHEST, preferred_element_type=f32)

    return pl.pallas_call(
        body, name="ada_bwd", grid=(2,),
        in_specs=[_const((N_DEV, D_MODEL)), pl.BlockSpec((1, N_DEV, 768), lambda l: (l, 0, 0))],
        out_specs=pl.BlockSpec((1, D_MODEL, 768), lambda l: (l, 0, 0)),
        out_shape=jax.ShapeDtypeStruct((2, D_MODEL, 768), f32),
    )(c_all, dmod_cols)


def _adamw(w, g, m, v):
    m = ADAM_B1 * m + (1.0 - ADAM_B1) * g
    v = ADAM_B2 * v + (1.0 - ADAM_B2) * (g * g)
    m_hat = m / (1.0 - ADAM_B1 ** ADAM_STEP)
    v_hat = v / (1.0 - ADAM_B2 ** ADAM_STEP)
    delta = -ADAM_LR * (m_hat / (jnp.sqrt(v_hat) + ADAM_EPS) + ADAM_WD * w)
    return delta, m, v


def adamw(parts, w, m, v, layer, prev, name):
    n, r, c = parts.shape
    nl = w.shape[0]
    tr = r
    lanes = -(-c // 128) * 128
    if 2 * (n + 7) * r * lanes * 4 > ADAMW_BLOCK_BYTES:
        tr = next(t for t in (256, 128, 64, 32, 16, 8) if r % t == 0)

    def body(p_ref, w_ref, m_ref, v_ref, *rest):
        g_ref, d_ref, nm_ref, nv_ref = rest[-4:]
        g = p_ref[0].astype(f32)
        for k in range(1, n):
            g = g + p_ref[k].astype(f32)
        delta, nm, nv = _adamw(w_ref[...], g, m_ref[...], v_ref[...])
        g_ref[...] = g
        d_ref[...] = delta
        nm_ref[...] = nm
        nv_ref[...] = nv

    blk = pl.BlockSpec((None, tr, c), lambda i: (layer, i, 0))
    shp = jax.ShapeDtypeStruct((nl, r, c), f32)
    kept = [] if prev is None else list(prev)
    return pl.pallas_call(
        body, name=name, grid=(r // tr,),
        in_specs=[pl.BlockSpec((n, tr, c), lambda i: (0, i, 0)), blk, blk, blk] + [ANY] * len(kept),
        out_specs=[blk] * 4, out_shape=[shp] * 4,
        input_output_aliases={4 + j: j for j in range(len(kept))},
    )(parts, w, m, v, *kept)


def _my_index():
    return 4 * lax.axis_index("x") + 2 * lax.axis_index("y") + lax.axis_index("c")


def _coords(idx):
    return (idx // 4, (idx // 2) % 2, idx % 2)


class CommJob:
    def __init__(self, operands, out_shape, phases, scratch):
        self.operands, self.out_shape, self.phases, self.scratch = operands, out_shape, phases, scratch


def _wait(out, n_blocks, send_sem, recv_sem, send=True, recv=True):
    span = out.at[pl.ds(0, n_blocks)]
    desc = pltpu.make_async_remote_copy(src_ref=span, dst_ref=span, send_sem=send_sem, recv_sem=recv_sem,
                                        device_id=_coords(_my_index()), device_id_type=MESH)
    if recv:
        desc.wait_recv()
    if send:
        desc.wait_send()


def gather_job(shards):
    n = len(shards)

    def places():
        x, y, c = lax.axis_index("x"), lax.axis_index("y"), lax.axis_index("c")
        return (x, y, c), (x, y, 1 - c), [(1 - x, y), (x, 1 - y), (1 - x, 1 - y)]

    def index(p):
        return 4 * p[0] + 2 * p[1] + p[2]

    def start(ins, outs, sems):
        far_send, far_recv, near_send, near_recv, local = sems
        me, sibling, chips = places()
        for k in range(n):
            pltpu.make_async_copy(ins[k], outs[k].at[index(me)], local.at[k]).start()
            for chip in chips:
                pltpu.make_async_remote_copy(src_ref=ins[k], dst_ref=outs[k].at[index(me)], send_sem=far_send.at[k],
                                             recv_sem=far_recv.at[k], device_id=(*chip, me[2]), device_id_type=MESH).start()
            pltpu.make_async_remote_copy(src_ref=ins[k], dst_ref=outs[k].at[index(me)], send_sem=near_send.at[k],
                                         recv_sem=near_recv.at[k], device_id=sibling, device_id_type=MESH).start()

    def relay(ins, outs, sems):
        far_send, far_recv, near_send, near_recv, local = sems
        me, sibling, chips = places()
        for k in range(n):
            _wait(outs[k], 3, far_send.at[k], far_recv.at[k], send=False)
            for chip in chips:
                block = outs[k].at[index((*chip, me[2]))]
                pltpu.make_async_remote_copy(src_ref=block, dst_ref=block, send_sem=near_send.at[k],
                                             recv_sem=near_recv.at[k], device_id=sibling, device_id_type=MESH).start()

    def finish(ins, outs, sems):
        far_send, far_recv, near_send, near_recv, local = sems
        for k in range(n):
            _wait(outs[k], 4, near_send.at[k], near_recv.at[k])
            _wait(outs[k], 3, far_send.at[k], far_recv.at[k], recv=False)
            pltpu.make_async_copy(ins[k], outs[k].at[0], local.at[k]).wait()

    shapes = [jax.ShapeDtypeStruct((N_DEV,) + tuple(a.shape), a.dtype) for a in shards]
    return CommJob(list(shards), shapes, [start, relay, finish], [pltpu.SemaphoreType.DMA((n,))] * 5)


def scatter_job(tensors):
    n = len(tensors)
    flat, where = [], {}
    for k, pieces in enumerate(tensors):
        d = 0
        for piece in pieces:
            for b in range(piece.shape[0]):
                where[k, d] = (len(flat), b)
                d += 1
            flat.append(piece)
        assert d == N_DEV

    def start(ins, outs, sems):
        send_sems, recv_sems, local_sems = sems
        me = _my_index()

        def block(k, d):
            i, b = where[k, d]
            return ins[i].at[b]

        for d in range(N_DEV):
            @pl.when(d != me)
            def _():
                for k in range(n):
                    pltpu.make_async_remote_copy(src_ref=block(k, d), dst_ref=outs[k].at[me], send_sem=send_sems.at[k],
                                                 recv_sem=recv_sems.at[k], device_id=(d // 4, (d // 2) % 2, d % 2),
                                                 device_id_type=MESH).start()

            @pl.when(d == me)
            def _():
                for k in range(n):
                    pltpu.make_async_copy(block(k, d), outs[k].at[d], local_sems.at[k]).start()

    def finish(ins, outs, sems):
        send_sems, recv_sems, local_sems = sems
        for k in range(n):
            _wait(outs[k], N_DEV - 1, send_sems.at[k], recv_sems.at[k])
            i, b = where[k, 0]
            pltpu.make_async_copy(ins[i].at[b], outs[k].at[0], local_sems.at[k]).wait()

    shapes = [jax.ShapeDtypeStruct((N_DEV,) + tuple(p[0].shape[1:]), p[0].dtype) for p in tensors]
    return CommJob(flat, shapes, [start, finish], [pltpu.SemaphoreType.DMA((n,))] * 3)


def comm_call(job, name):
    ni, no = len(job.operands), len(job.out_shape)

    def body(*refs):
        ins, outs, sems = refs[:ni], refs[ni:ni + no], refs[ni + no:]
        for phase in job.phases:
            phase(ins, outs, sems)

    return pl.pallas_call(body, name=name, in_specs=[ANY] * ni, out_specs=[ANY] * no, out_shape=job.out_shape,
                          scratch_shapes=job.scratch)(*job.operands)


def _carry(job, body, n_in, n_out, at_step):
    ji, jo, js = len(job.operands), len(job.out_shape), len(job.scratch)

    def carrier(*refs):
        a, b = n_in, n_in + ji
        c, d = b + n_out, b + n_out + jo
        e = len(refs) - js
        job_refs = (refs[a:b], refs[c:d], refs[e:])
        n = len(job.phases)

        @pl.when(at_step(0, n))
        def _():
            job.phases[0](*job_refs)

        body(*refs[:a], *refs[b:c], *refs[d:e])

        for i in range(1, n):
            @pl.when(at_step(i, n))
            def _():
                job.phases[i](*job_refs)

    return carrier


def _pad_lanes(v, lo, total=128):
    return jnp.pad(v, (lo, total - lo - v.shape[0]))[None, :]


MIXER_WEIGHTS = ("w_in", "w_q_up", "w_kv_up", "conv_w")
LATE_WEIGHTS = ("w_out", "w_gate_up", "w_down")


def mixer_operands(g, sw):
    w_in = g["w_in"].transpose(1, 0, 2).reshape(D_MODEL, D_IN)
    z = jnp.zeros((D_MODEL, 1), w_in.dtype)
    w_proj = jnp.concatenate(
        [w_in[:, :384], w_in[:, 416:928], w_in[:, 928:1952], w_in[:, 1952:1960], jnp.tile(z, (1, 56)),
         w_in[:, 384:416], jnp.tile(z, (1, 32))], axis=1)
    wq = jnp.pad(g["w_q_up"], ((0, 0), (0, 0), (0, HEAD_LANES - NOPE - ROPE)))
    wk = jnp.pad(g["w_kv_up"][:, :, :NOPE], ((0, 0), (0, 0), (0, HEAD_LANES - NOPE)))
    wv = g["w_kv_up"][:, :, NOPE:]
    qkv = (sw["q_a_norm_w"][None, :], sw["kv_a_norm_w"][None, :], wq, wk, wv,
           _pad_lanes(jnp.concatenate([sw["q_nope_norm_w"], sw["q_pe_norm_w"]]), 0),
           _pad_lanes(sw["k_nope_norm_w"], 0), _pad_lanes(sw["k_pe_norm_w"], NOPE))
    conv_w = g["conv_w"].astype(f32).transpose(1, 0, 2).reshape(4, D_CONV)
    ssd = (conv_w, sw["conv_b"][None, :], _pad_lanes(sw["dt_bias"], 0), _pad_lanes(sw["a_log"], 0),
           _pad_lanes(sw["d_skip"], 0), sw["ssd_norm_w"][None, :])
    return dict(w_proj=w_proj, qkv=qkv, ssd=ssd, n1=sw["norm1_w"][None, :])


def late_operands(g, sw):
    return dict(wo=g["w_out"].reshape(D_MODEL, D_MODEL), wgu=g["w_gate_up"],
                wd=g["w_down"].reshape(N_DEV // 2, FF_SHARD, D_MODEL), n2=sw["norm2_w"][None, :])


def layer_fwd(x, mod, kw, cos_t, sin_t, job=None, late=None):
    sh1, sc1, g1, sh2, sc2, g2 = [mod[i:i + 1] for i in range(6)]
    pa, pz, px, plast = proj_fwd(x, kw["n1"], sh1, sc1, kw["w_proj"])
    q, k, v = qkv_fwd(pa, plast, cos_t, sin_t, kw["qkv"])
    (o, lse), carried = attn_fwd(q, k, v, job)
    if late is not None:
        kw = {**kw, **late(carried)}
    yg, states = ssd_fwd(px, pz, plast, kw["ssd"])
    x_mid = out_fwd(x, o, yg, g1, kw["wo"])
    x_out, mix = mlp_fwd(x_mid, kw["n2"], sh2, sc2, g2, kw["wgu"], kw["wd"])
    saved = dict(x=x, pa=pa, pz=pz, px=px, plast=plast, q=q, k=k, v=v, o=o, lse=lse, yg=yg, states=states, x_mid=x_mid,
                 mix=mix)
    return x_out, saved, kw, carried


def layer_bwd_head(dy, mod, kw, sv, job=None):
    _, _, g1, sh2, sc2, g2 = [mod[i:i + 1] for i in range(6)]
    (dparts, dn2, dsh2, dsc2, dwg, dwu, dwd), carried = mlp_bwd(
        sv["x_mid"], dy, kw["n2"], sh2, sc2, g2, kw["wgu"], kw["wd"], job)
    dmid, do, delta, dyg, dg1, dg2, dwo = out_bwd(dy, dparts, sv["mix"], sv["o"], sv["yg"], g1, kw["wo"])
    early = dict(w_out=[dwo.reshape(N_DEV, D_MODEL // N_DEV, D_MODEL)], w_gate_up=[dwg, dwu],
                 w_down=[dwd.reshape(N_DEV, D_FF // N_DEV, D_MODEL)])
    head = dict(dmid=dmid, do=do, delta=delta, dyg=dyg, dn2=dn2, dsh2=dsh2, dsc2=dsc2, dg2=dg2, dg1=dg1)
    return head, early, carried


def layer_bwd_tail(hd, mod, kw, cos_t, sin_t, sv, job=None):
    sh1, sc1 = mod[0:1], mod[1:2]
    (dq, dk, dv), carried = attn_bwd(sv["q"], sv["k"], sv["v"], hd["do"], sv["lse"], hd["delta"], job)
    dpx, dpz, dpl_dt, dcw, dcb, ddtb, dalog, ddskip, dsnw = ssd_bwd(sv["px"], sv["pz"], sv["plast"], sv["states"],
                                                                   hd["dyg"], kw["ssd"])
    dpa, dpl_k, dqaw, dkvaw, dwq, dwk, dwv, dqnw, dknw, dkpw = qkv_bwd(sv["pa"], sv["plast"], cos_t, sin_t, kw["qkv"],
                                                                       dq, dk, dv)
    dx, dn1, dsh1, dsc1, dwp = proj_bwd(sv["x"], kw["n1"], sh1, sc1, kw["w_proj"], dpa, dpz, dpx, dpl_k, dpl_dt, hd["dmid"])
    dmod = jnp.concatenate([dsh1, dsc1, hd["dg1"], hd["dsh2"], hd["dsc2"], hd["dg2"]], axis=0)
    dw_in = jnp.concatenate([dwp[:, :384], dwp[:, 1984:2016], dwp[:, 384:1920], dwp[:, 1920:1928]], axis=1)
    grads = dict(
        norm1_w=dn1[0], norm2_w=hd["dn2"][0], q_a_norm_w=dqaw[0], kv_a_norm_w=dkvaw[0],
        q_nope_norm_w=dqnw[0, :NOPE], q_pe_norm_w=dqnw[0, NOPE:NOPE + ROPE], k_nope_norm_w=dknw[0, :NOPE],
        k_pe_norm_w=dkpw[0, NOPE:NOPE + ROPE], conv_b=dcb[0], dt_bias=ddtb[0, :N_HEADS], a_log=dalog[0, :N_HEADS],
        d_skip=ddskip[0, :N_HEADS], ssd_norm_w=dsnw[0],
        w_in=[dw_in.reshape(D_MODEL, N_DEV, D_IN // N_DEV).transpose(1, 0, 2)],
        w_q_up=[dwq[:, :, :NOPE + ROPE].astype(bf16)],
        w_kv_up=[jnp.concatenate([dwk[:, :, :NOPE], dwv], axis=2).astype(bf16)],
        conv_w=[dcw.reshape(4, N_DEV, D_CONV // N_DEV).transpose(1, 0, 2).astype(bf16)],
    )
    return dx, dmod, grads, carried


def _pack_small(get, last=None):
    flat = jnp.concatenate([get(name).reshape(-1) for name, _ in SMALL])
    flat = jnp.pad(flat, (0, SMALL_ROWS * 128 - flat.shape[0]))
    if last is not None:
        flat = flat.at[-1].set(last)
    return flat.reshape(SMALL_ROWS, 128)


def _unpack_small(packed):
    flat = packed.reshape(-1)
    out, off = {}, 0
    for name, size in SMALL:
        out[name] = flat[off:off + 2 * size].reshape(2, size)
        off += 2 * size
    return out


def kernel(x, c, positions, norm1_w, norm2_w, w_ada, b_ada, w_in, q_a_norm_w, w_q_up, kv_a_norm_w, w_kv_up, q_nope_norm_w, q_pe_norm_w, k_nope_norm_w, k_pe_norm_w, conv_w, conv_b, dt_bias, a_log, d_skip, ssd_norm_w, w_out, w_gate_up, w_down, loss_target, m_norm1_w, m_norm2_w, m_w_ada, m_b_ada, m_w_in, m_q_a_norm_w, m_w_q_up, m_kv_a_norm_w, m_w_kv_up, m_q_nope_norm_w, m_q_pe_norm_w, m_k_nope_norm_w, m_k_pe_norm_w, m_conv_w, m_conv_b, m_dt_bias, m_a_log, m_d_skip, m_ssd_norm_w, m_w_out, m_w_gate_up, m_w_down, v_norm1_w, v_norm2_w, v_w_ada, v_b_ada, v_w_in, v_q_a_norm_w, v_w_q_up, v_kv_a_norm_w, v_w_kv_up, v_q_nope_norm_w, v_q_pe_norm_w, v_k_nope_norm_w, v_k_pe_norm_w, v_conv_w, v_conv_b, v_dt_bias, v_a_log, v_d_skip, v_ssd_norm_w, v_w_out, v_w_gate_up, v_w_down):
    w = dict(norm1_w=norm1_w, norm2_w=norm2_w, w_ada=w_ada, b_ada=b_ada, w_in=w_in, q_a_norm_w=q_a_norm_w, w_q_up=w_q_up,
             kv_a_norm_w=kv_a_norm_w, w_kv_up=w_kv_up, q_nope_norm_w=q_nope_norm_w, q_pe_norm_w=q_pe_norm_w,
             k_nope_norm_w=k_nope_norm_w, k_pe_norm_w=k_pe_norm_w, conv_w=conv_w, conv_b=conv_b, dt_bias=dt_bias,
             a_log=a_log, d_skip=d_skip, ssd_norm_w=ssd_norm_w, w_out=w_out, w_gate_up=w_gate_up, w_down=w_down)
    m = dict(norm1_w=m_norm1_w, norm2_w=m_norm2_w, w_ada=m_w_ada, b_ada=m_b_ada, w_in=m_w_in, q_a_norm_w=m_q_a_norm_w,
             w_q_up=m_w_q_up, kv_a_norm_w=m_kv_a_norm_w, w_kv_up=m_w_kv_up, q_nope_norm_w=m_q_nope_norm_w,
             q_pe_norm_w=m_q_pe_norm_w, k_nope_norm_w=m_k_nope_norm_w, k_pe_norm_w=m_k_pe_norm_w, conv_w=m_conv_w,
             conv_b=m_conv_b, dt_bias=m_dt_bias, a_log=m_a_log, d_skip=m_d_skip, ssd_norm_w=m_ssd_norm_w, w_out=m_w_out,
             w_gate_up=m_w_gate_up, w_down=m_w_down)
    v = dict(norm1_w=v_norm1_w, norm2_w=v_norm2_w, w_ada=v_w_ada, b_ada=v_b_ada, w_in=v_w_in, q_a_norm_w=v_q_a_norm_w,
             w_q_up=v_w_q_up, kv_a_norm_w=v_kv_a_norm_w, w_kv_up=v_w_kv_up, q_nope_norm_w=v_q_nope_norm_w,
             q_pe_norm_w=v_q_pe_norm_w, k_nope_norm_w=v_k_nope_norm_w, k_pe_norm_w=v_k_pe_norm_w, conv_w=v_conv_w,
             conv_b=v_conv_b, dt_bias=v_dt_bias, a_log=v_a_log, d_skip=v_d_skip, ssd_norm_w=v_ssd_norm_w, w_out=v_w_out,
             w_gate_up=v_w_gate_up, w_down=v_w_down)
    me = _my_index()
    seq = x.shape[1]

    def shard(name, l):
        if name == "conv_w":
            return w[name][l]
        if name in TRANSPOSED:
            return jnp.swapaxes(w[name][l], 0, 1).astype(bf16)
        return w[name][l].astype(bf16)

    def shards(names, l):
        return [shard(name, l) for name in names]

    small = [{name: w[name][l] for name, _ in SMALL if name != "b_ada"} for l in range(2)]
    n_mix, n_late = len(MIXER_WEIGHTS), len(LATE_WEIGHTS)

    first = comm_call(gather_job([c] + shards(MIXER_WEIGHTS, 0)), "gather_first")
    c_all = first[0].reshape(N_DEV, D_MODEL)
    kws = [mixer_operands(dict(zip(MIXER_WEIGHTS, first[1:])), small[0]), None]

    b_cols = lax.dynamic_slice_in_dim(b_ada, me * 768, 768, axis=1)
    mod_cols = ada_fwd(c_all, w_ada, b_cols)
    (mod_all,) = comm_call(gather_job([mod_cols]), "gather_mod")
    mod_me = lax.dynamic_index_in_dim(mod_all, me, axis=2, keepdims=False)
    mods = [mod_me[:, l, :].reshape(6, D_MODEL) for l in range(2)]

    inv_freq = 1.0 / (ROPE_THETA ** (jnp.arange(0, ROPE, 2, dtype=f32) / ROPE))
    inv = _pad_lanes(jnp.concatenate([inv_freq, inv_freq]), NOPE)
    cos_t, sin_t = rope_tables(positions.reshape(seq, 1), inv)

    saved = [None, None]
    h, saved[0], kws[0], got = layer_fwd(
        x[0], mods[0], kws[0], cos_t, sin_t, gather_job(shards(LATE_WEIGHTS, 0) + shards(MIXER_WEIGHTS, 1)),
        lambda got: late_operands(dict(zip(LATE_WEIGHTS, got[:n_late])), small[0]))
    kws[1] = mixer_operands(dict(zip(MIXER_WEIGHTS, got[n_late:])), small[1])
    h, saved[1], kws[1], _ = layer_fwd(
        h, mods[1], kws[1], cos_t, sin_t, gather_job(shards(LATE_WEIGHTS, 1)),
        lambda got: late_operands(dict(zip(LATE_WEIGHTS, got)), small[1]))
    dy, loss_part = loss_fwd(h, loss_target[0])

    early, late = ("w_out", "w_gate_up", "w_down"), ("w_in", "w_q_up", "w_kv_up", "conv_w")
    parts = [{}, {}]
    head, pieces, _ = layer_bwd_head(dy, mods[1], kws[1], saved[1])
    dy, dmod1, grads1, got = layer_bwd_tail(head, mods[1], kws[1], cos_t, sin_t, saved[1], scatter_job([pieces[n] for n in early]))
    parts[1].update(zip(early, got))
    head, pieces, got = layer_bwd_head(dy, mods[0], kws[0], saved[0], scatter_job([grads1[n] for n in late]))
    parts[1].update(zip(late, got))
    dy, dmod0, grads0, got = layer_bwd_tail(head, mods[0], kws[0], cos_t, sin_t, saved[0], scatter_job([pieces[n] for n in early]))
    parts[0].update(zip(early, got))
    parts[0].update(zip(late, comm_call(scatter_job([grads0[n] for n in late]), "scatter_layer0_rest")))
    grad_x = dy[None]

    small_part = {name: jnp.stack([grads0[name], grads1[name]]) for name, _ in SMALL if name != "b_ada"}
    small_part["b_ada"] = jnp.stack([dmod0.reshape(-1), dmod1.reshape(-1)])
    (small_all,) = comm_call(gather_job([_pack_small(lambda n: small_part[n], loss_part[0, 0])]), "gather_small_grads")
    packed = adamw(small_all, _pack_small(lambda n: w[n])[None], _pack_small(lambda n: m[n])[None],
                   _pack_small(lambda n: v[n])[None], 0, None, "adamw_small")
    loss = packed[0][0, -1, -1]
    res = {}
    for key, arr in zip("gdmv", packed):
        for name, val in _unpack_small(arr[0]).items():
            res[key, name] = val

    off = 2 * (1024 + 1024)
    dmod_all = small_all.reshape(N_DEV, -1)[:, off:off + 2 * 6144].reshape(N_DEV, 2, 6144)
    dmod_cols = lax.dynamic_slice_in_dim(dmod_all, me * 768, 768, axis=2).transpose(1, 0, 2)
    g_ada = ada_bwd(c_all, dmod_cols)
    out = None
    for l in range(2):
        out = adamw(g_ada[l][None], w_ada, m_w_ada, v_w_ada, l, out, "adamw_w_ada")
    res.update(zip([(key, "w_ada") for key in "gdmv"], out))

    for name in BIG:
        view = (lambda a: jnp.swapaxes(a, 1, 2)) if name in TRANSPOSED else (lambda a: a)
        out = None
        for l in range(2):
            out = adamw(parts[l][name], view(w[name]), view(m[name]), view(v[name]), l, out, "adamw_" + name)
        res.update(zip([(key, name) for key in "gdmv"], [view(a) for a in out]))

    return (loss, grad_x, *[res["g", n] for n in WEIGHTS], *[res["d", n] for n in WEIGHTS],
            *[res["m", n] for n in WEIGHTS], *[res["v", n] for n in WEIGHTS])
```

```python
import functools

import jax
import jax.numpy as jnp
from jax import lax
from jax.experimental import pallas as pl
from jax.experimental.pallas import tpu as pltpu

f32 = jnp.float32
bf16 = jnp.bfloat16

N_DEV = 8
D_MODEL = 1024
N_HEADS = 8
HEAD_LANES = 128
NOPE = 64
ROPE = 32
V_DIM = 64
Q_RANK = 256
KV_RANK = 128
D_SSD = 512
D_CONV = 1024
SSD_STATE = 128
SSD_HEAD_DIM = 64
CHUNK = 128
HALO = 8
D_FF = 2816
FF_SHARD = 704
D_IN = 1960
D_PROJ = 2048
EPS = 1e-6
LOG2E = 1.4426950408889634
LN2 = 0.6931471805599453
Q_SCALE = (NOPE + ROPE) ** -0.5 * LOG2E
ATTN_ROWS_FWD = 256
ATTN_HEADS_FWD = 4
ATTN_HEADS_BWD = 2
MLP_BWD_ROWS = 512
ROPE_THETA = 10000.0
NEG = -1e30

ADAM_LR = 0.001
ADAM_B1 = 0.9
ADAM_B2 = 0.999
ADAM_EPS = 1e-08
ADAM_WD = 0.01
ADAM_STEP = 10
ADAMW_BLOCK_BYTES = 24 << 20

MESH = pl.DeviceIdType.MESH
ANY = pl.BlockSpec(memory_space=pl.ANY)

SMALL = (("norm1_w", 1024), ("norm2_w", 1024), ("b_ada", 6144), ("q_a_norm_w", 256), ("kv_a_norm_w", 128),
         ("q_nope_norm_w", 64), ("q_pe_norm_w", 32), ("k_nope_norm_w", 64), ("k_pe_norm_w", 32),
         ("conv_b", 1024), ("dt_bias", 8), ("a_log", 8), ("d_skip", 8), ("ssd_norm_w", 512))
SMALL_ROWS = 168
BIG = ("w_in", "w_q_up", "w_kv_up", "conv_w", "w_out", "w_gate_up", "w_down")
TRANSPOSED = ("w_in", "w_gate_up")
WEIGHTS = ("norm1_w", "norm2_w", "w_ada", "b_ada", "w_in", "q_a_norm_w", "w_q_up", "kv_a_norm_w", "w_kv_up",
           "q_nope_norm_w", "q_pe_norm_w", "k_nope_norm_w", "k_pe_norm_w", "conv_w", "conv_b", "dt_bias",
           "a_log", "d_skip", "ssd_norm_w", "w_out", "w_gate_up", "w_down")


def _dot(a, b, ca, cb):
    return lax.dot_general(a.astype(bf16), b.astype(bf16), (((ca,), (cb,)), ((), ())), preferred_element_type=f32)


@jax.custom_vjp
def mm(a, b):
    return _dot(a, b, 1, 0)


def _mm_fwd(a, b):
    return _dot(a, b, 1, 0), (a, b)


def _mm_bwd(res, g):
    a, b = res
    return _dot(g, b, 1, 1).astype(a.dtype), _dot(a, g, 0, 0).astype(b.dtype)


mm.defvjp(_mm_fwd, _mm_bwd)


@jax.custom_vjp
def _mm_slot(a, w, slot):
    return _dot(a, w, 1, 0)


def _mm_slot_fwd(a, w, slot):
    return _dot(a, w, 1, 0), (a, w)


def _mm_slot_bwd(res, g):
    a, w = res
    return _dot(g, w, 1, 1).astype(a.dtype), None, _dot(a, g, 0, 0)


_mm_slot.defvjp(_mm_slot_fwd, _mm_slot_bwd)


def mmw(a, w, slot=None):
    return _dot(a, w, 1, 0) if slot is None else _mm_slot(a, w, slot)


@jax.custom_vjp
def _mm_slot_t(a, wt, slot):
    return _dot(a, wt, 1, 1)


def _mm_slot_t_fwd(a, wt, slot):
    return _dot(a, wt, 1, 1), (a, wt)


def _mm_slot_t_bwd(res, g):
    a, wt = res
    return _dot(g, wt, 1, 0).astype(a.dtype), None, _dot(g, a, 0, 0)


_mm_slot_t.defvjp(_mm_slot_t_fwd, _mm_slot_t_bwd)


def mmw_t(a, wt, slot=None):
    return _dot(a, wt, 1, 1) if slot is None else _mm_slot_t(a, wt, slot)


@jax.custom_vjp
def mm_nt(a, b):
    return _dot(a, b, 1, 1)


def _mm_nt_fwd(a, b):
    return _dot(a, b, 1, 1), (a, b)


def _mm_nt_bwd(res, g):
    a, b = res
    return _dot(g, b, 1, 0).astype(a.dtype), _dot(g, a, 0, 0).astype(b.dtype)


mm_nt.defvjp(_mm_nt_fwd, _mm_nt_bwd)


@jax.custom_vjp
def mm_tn(a, b):
    return _dot(a, b, 0, 0)


def _mm_tn_fwd(a, b):
    return _dot(a, b, 0, 0), (a, b)


def _mm_tn_bwd(res, g):
    a, b = res
    return _dot(b, g, 1, 1).astype(a.dtype), _dot(a, g, 1, 0).astype(b.dtype)


mm_tn.defvjp(_mm_tn_fwd, _mm_tn_bwd)


def _rms(x, w):
    return x * lax.rsqrt(jnp.mean(x * x, axis=-1, keepdims=True) + EPS) * w


def _const(shape):
    n = len(shape)
    return pl.BlockSpec(shape, lambda *_: (0,) * n)


def _accumulate(first, refs, vals):
    @pl.when(first)
    def _():
        for r, v in zip(refs, vals):
            r[...] = v

    @pl.when(jnp.logical_not(first))
    def _():
        for r, v in zip(refs, vals):
            r[...] += v


def _accumulate_then_cast(first, last, accs, outs, vals):
    _accumulate(first, accs, vals)

    @pl.when(last)
    def _():
        for a, o in zip(accs, outs):
            o[...] = a[...].astype(o.dtype)


def _token_block(s):
    return min(512, s)


def _f_proj(x, nw, sh, sc, w, slot=None):
    h = _rms(x, nw) * (1.0 + sc) + sh
    return mmw_t(h, w, slot)


def _f_qkv(pa, plast, cos_t, sin_t, qaw, kvaw, wq, wk, wv, qnw, knw, kpw, slots=None):
    sq, sk, sv = slots if slots is not None else ([None] * N_HEADS,) * 3
    lane = lax.broadcasted_iota(jnp.int32, (1, HEAD_LANES), 1)
    m_nope = lane < NOPE
    m_pe = (lane >= NOPE) & (lane < NOPE + ROPE)
    rows = pa.shape[0]

    def rope(t):
        half = ROPE // 2
        swapped = jnp.concatenate(
            [jnp.zeros((rows, NOPE), f32), t[:, NOPE + half:NOPE + ROPE], t[:, NOPE:NOPE + half],
             jnp.zeros((rows, HEAD_LANES - NOPE - ROPE), f32)], axis=1)
        return t * cos_t + swapped * sin_t

    qa = _rms(pa[:, :Q_RANK], qaw)
    kva = _rms(pa[:, Q_RANK:Q_RANK + KV_RANK], kvaw)
    kp = jnp.where(m_pe, plast, 0.0)
    kp = kp * lax.rsqrt(jnp.sum(kp * kp, axis=-1, keepdims=True) / ROPE + EPS) * kpw
    k_rot = rope(kp)
    qs, ks, vs = [], [], []
    for h in range(N_HEADS):
        qh = mmw(qa, wq[h], sq[h])
        ss_n = jnp.sum(jnp.where(m_nope, qh * qh, 0.0), axis=-1, keepdims=True) / NOPE
        ss_p = jnp.sum(jnp.where(m_pe, qh * qh, 0.0), axis=-1, keepdims=True) / ROPE
        r = jnp.where(m_nope, lax.rsqrt(ss_n + EPS), lax.rsqrt(ss_p + EPS))
        qs.append(rope(qh * r * qnw) * Q_SCALE)
        kh = mmw(kva, wk[h], sk[h])
        kh = kh * lax.rsqrt(jnp.sum(kh * kh, axis=-1, keepdims=True) / NOPE + EPS) * knw
        ks.append(kh + k_rot)
        vs.append(mmw(kva, wv[h], sv[h]))
    return jnp.stack(qs), jnp.stack(ks), jnp.stack(vs)


def _f_ssd(xext, z, plast, prev, cw, cb, dtb, alog, dskip, snw):
    n = CHUNK
    conv = cb
    for k in range(4):
        conv = conv + cw[k:k + 1] * xext[HALO - 3 + k:HALO - 3 + k + n]
    xc = jax.nn.silu(conv)
    xs, bm, cm = xc[:, :D_SSD], xc[:, D_SSD:D_SSD + 2 * SSD_STATE], xc[:, D_SSD + 2 * SSD_STATE:]
    lane = lax.broadcasted_iota(jnp.int32, (1, 128), 1)
    dt = jax.nn.softplus(jnp.where(lane < N_HEADS, plast, 0.0) + dtb)
    adt = dt * (-jnp.exp(alog))
    row = lax.broadcasted_iota(jnp.int32, (n, n), 0)
    col = lax.broadcasted_iota(jnp.int32, (n, n), 1)
    tri = row >= col
    acs = jnp.dot(tri.astype(f32), adt, precision=lax.Precision.HIGHEST, preferred_element_type=f32)
    acs_t = acs.T
    bgs = [bm[:, g * SSD_STATE:(g + 1) * SSD_STATE] for g in range(2)]
    cgs = [cm[:, g * SSD_STATE:(g + 1) * SSD_STATE] for g in range(2)]
    cb_ts = [mm_nt(cgs[g], bgs[g]) for g in range(2)]
    low = lane < SSD_HEAD_DIM
    low_rows = lax.broadcasted_iota(jnp.int32, (2 * SSD_HEAD_DIM, 1), 0) < SSD_HEAD_DIM

    def both(a0, a1):
        return jnp.where(low, a0, a1)

    pre = []
    for i in range(N_HEADS // 2):
        h0, h1 = 2 * i, 2 * i + 1
        col0, col1 = acs[:, h0:h0 + 1], acs[:, h1:h1 + 1]
        last0, last1 = acs[n - 1:n, h0:h0 + 1], acs[n - 1:n, h1:h1 + 1]
        cb_t = cb_ts[i // 2]
        scores0 = cb_t * jnp.exp(jnp.where(tri, col0 - acs_t[h0:h0 + 1, :], -jnp.inf))
        scores1 = cb_t * jnp.exp(jnp.where(tri, col1 - acs_t[h1:h1 + 1, :], -jnp.inf))
        xp = xs[:, i * 128:(i + 1) * 128]
        xdt = xp * both(dt[:, h0:h0 + 1], dt[:, h1:h1 + 1])
        weighted = xdt * both(jnp.exp(last0 - col0), jnp.exp(last1 - col1))
        chunk_decay = jnp.where(low_rows, jnp.exp(last0), jnp.exp(last1))
        in_decay = both(jnp.exp(col0), jnp.exp(col1))
        skip = both(dskip[:, h0:h0 + 1], dskip[:, h1:h1 + 1]) * xp
        pre.append((scores0, scores1, xdt, weighted, chunk_decay, in_decay, skip))
    prods = []
    for i in range(N_HEADS // 2):
        scores0, scores1, xdt, weighted, _, _, _ = pre[i]
        g = i // 2
        y_diag = mm(scores0, jnp.where(low, xdt, 0.0)) + mm(scores1, jnp.where(low, 0.0, xdt))
        prods.append((y_diag, mm_tn(weighted, bgs[g]), mm_nt(cgs[g], prev[i])))
    ys, news = [], []
    for i in range(N_HEADS // 2):
        y_diag, st, y_off = prods[i]
        _, _, _, _, chunk_decay, in_decay, skip = pre[i]
        news.append(chunk_decay * prev[i] + st)
        ys.append(y_diag + y_off * in_decay + skip)
    y = jnp.concatenate(ys, axis=1)
    yg = y * jax.nn.silu(z)
    half = D_SSD // 2
    outs = []
    for g in range(2):
        t = yg[:, g * half:(g + 1) * half]
        outs.append(t * lax.rsqrt(jnp.mean(t * t, axis=-1, keepdims=True) + EPS))
    return jnp.concatenate(outs, axis=1) * snw, jnp.stack(news)


def _f_out(o, yg, g1, wo, slot=None):
    cat = jnp.concatenate([o[h] for h in range(N_HEADS)] + [yg], axis=1)
    return g1 * mmw(cat, wo, slot)


def _f_modulate(x, nw, sh, sc):
    return _rms(x, nw) * (1.0 + sc) + sh


def _f_gate_up(h, wg, wu, slot_g=None, slot_u=None):
    return jax.nn.silu(mmw_t(h, wg, slot_g)) * mmw_t(h, wu, slot_u)


def proj_fwd(x, nw, sh, sc, w):
    s = x.shape[0]
    ts = _token_block(s)

    def body(x_ref, nw_ref, sh_ref, sc_ref, w_ref, pa_ref, pz_ref, px_ref, pl_ref):
        p = _f_proj(x_ref[...], nw_ref[...], sh_ref[...], sc_ref[...], w_ref[...])
        pa_ref[...] = p[:, :384]
        pz_ref[...] = p[:, 384:896]
        px_ref[...] = p[:, 896:1920]
        pl_ref[...] = p[:, 1920:]

    vec = _const((1, D_MODEL))
    return pl.pallas_call(
        body, name="proj_fwd", grid=(s // ts,),
        in_specs=[pl.BlockSpec((ts, D_MODEL), lambda i: (i, 0)), vec, vec, vec, _const((D_PROJ, D_MODEL))],
        out_specs=[pl.BlockSpec((ts, 384), lambda i: (i, 0)), pl.BlockSpec((ts, 512), lambda i: (i, 0)),
                   pl.BlockSpec((ts, 1024), lambda i: (i, 0)), pl.BlockSpec((ts, 128), lambda i: (i, 0))],
        out_shape=[jax.ShapeDtypeStruct((s, 384), f32), jax.ShapeDtypeStruct((s, 512), f32),
                   jax.ShapeDtypeStruct((s, 1024), f32), jax.ShapeDtypeStruct((s, 128), f32)],
    )(x, nw, sh, sc, w)


def rope_tables(pos, inv):
    s = pos.shape[0]
    ts = _token_block(s)

    def body(pos_ref, inv_ref, cos_ref, sin_ref):
        ang = pos_ref[...].astype(f32) * inv_ref[...]
        lane = lax.broadcasted_iota(jnp.int32, (1, HEAD_LANES), 1)
        half = ROPE // 2
        cos_ref[...] = jnp.where(lane < NOPE, 1.0, jnp.where(lane < NOPE + ROPE, jnp.cos(ang), 0.0))
        sn = jnp.sin(ang)
        sin_ref[...] = jnp.where((lane >= NOPE) & (lane < NOPE + half), -sn,
                                 jnp.where((lane >= NOPE + half) & (lane < NOPE + ROPE), sn, 0.0))

    return pl.pallas_call(
        body, name="rope_tables", grid=(s // ts,),
        in_specs=[pl.BlockSpec((ts, 1), lambda i: (i, 0)), _const((1, HEAD_LANES))],
        out_specs=[pl.BlockSpec((ts, HEAD_LANES), lambda i: (i, 0))] * 2,
        out_shape=[jax.ShapeDtypeStruct((s, HEAD_LANES), f32)] * 2,
    )(pos, inv)


def _qkv_param_specs():
    return [_const((1, Q_RANK)), _const((1, KV_RANK)), _const((N_HEADS, Q_RANK, HEAD_LANES)),
            _const((N_HEADS, KV_RANK, HEAD_LANES)), _const((N_HEADS, KV_RANK, V_DIM)),
            _const((1, HEAD_LANES)), _const((1, HEAD_LANES)), _const((1, HEAD_LANES))]


def qkv_fwd(pa, plast, cos_t, sin_t, params):
    s = pa.shape[0]
    ts = _token_block(s)

    def body(pa_ref, pl_ref, cos_ref, sin_ref, *rest):
        prm = [r[...] for r in rest[:8]]
        q_ref, k_ref, v_ref = rest[8:]
        q, k, v = _f_qkv(pa_ref[...], pl_ref[...], cos_ref[...], sin_ref[...], *prm)
        q_ref[...] = q.astype(bf16)
        k_ref[...] = k.astype(bf16)
        v_ref[...] = jnp.concatenate([v, jnp.ones_like(v)], axis=-1).astype(bf16)

    tok = lambda w: pl.BlockSpec((ts, w), lambda i: (i, 0))
    head = pl.BlockSpec((N_HEADS, ts, HEAD_LANES), lambda i: (0, i, 0))
    return pl.pallas_call(
        body, name="qkv_fwd", grid=(s // ts,),
        in_specs=[tok(384), tok(128), tok(128), tok(128)] + _qkv_param_specs(),
        out_specs=[head] * 3, out_shape=[jax.ShapeDtypeStruct((N_HEADS, s, HEAD_LANES), bf16)] * 3,
    )(pa, plast, cos_t, sin_t, *params)


def _scores(q, k):
    return lax.dot_general(q, k, (((1,), (1,)), ((), ())), preferred_element_type=f32)


def _tril(rows, cols, row_offset):
    row = row_offset + lax.broadcasted_iota(jnp.int32, (rows, cols), 0)
    col = lax.broadcasted_iota(jnp.int32, (rows, cols), 1)
    return row >= col


def _call_with_job(body, name, grid, job, in_specs, out_specs, out_shape, scratch_shapes, operands, relay_at=None):
    if job is None:
        res = pl.pallas_call(body, name=name, grid=grid, in_specs=in_specs, out_specs=out_specs, out_shape=out_shape,
                             scratch_shapes=scratch_shapes)(*operands)
        return res, None

    def at_step(i, n):
        if i == 0:
            want = [0] * len(grid)
        elif i == n - 1:
            want = [g - 1 for g in grid]
        else:
            want = relay_at
        return functools.reduce(jnp.logical_and, [pl.program_id(a) == s for a, s in enumerate(want)])

    carrier = _carry(job, body, len(in_specs), len(out_specs), at_step)
    res = pl.pallas_call(
        carrier, name=name, grid=grid,
        in_specs=list(in_specs) + [ANY] * len(job.operands), out_specs=list(out_specs) + [ANY] * len(job.out_shape),
        out_shape=list(out_shape) + list(job.out_shape), scratch_shapes=list(scratch_shapes) + job.scratch,
    )(*operands, *job.operands)
    return res[:len(out_specs)], res[len(out_specs):]


def attn_fwd(q, k, v, job=None):
    s = q.shape[1]
    t = _token_block(s)
    nb = s // t

    rb = min(ATTN_ROWS_FWD, t)

    hp = ATTN_HEADS_FWD

    def body(q_ref, k_ref, v_ref, o_ref, lse_ref, m_sc, acc_sc):
        qi = pl.program_id(1)
        m_sc[...] = jnp.full(m_sc.shape, NEG, f32)
        acc_sc[...] = jnp.zeros(acc_sc.shape, f32)

        def step(k0, diagonal):
            chains = [(hh, r) for hh in range(hp) for r in range(t // rb)]

            def scores(hh, r):
                nk = (r + 1) * rb if diagonal else t
                sc = _scores(q_ref[hh, pl.ds(r * rb, rb), :], k_ref[hh, pl.ds(k0, nk), :])
                return jnp.where(_tril(rb, nk, r * rb), sc, NEG) if diagonal else sc

            ahead = scores(*chains[0])
            for c, (hh, r) in enumerate(chains):
                sc = ahead
                if c + 1 < len(chains):
                    ahead = scores(*chains[c + 1])
                rows = pl.ds(r * rb, rb)
                keys = pl.ds(k0, (r + 1) * rb if diagonal else t)
                m_prev = m_sc[hh, rows, :1]
                m_new = jnp.maximum(m_prev, jnp.max(sc, axis=-1, keepdims=True))
                p = jnp.exp2(sc - m_new)
                alpha = jnp.exp2(m_prev - m_new)
                acc = alpha * acc_sc[hh, rows, :] + jnp.dot(p.astype(bf16), v_ref[hh, keys, :], preferred_element_type=f32)
                if diagonal:
                    l = acc[:, V_DIM:V_DIM + 1]
                    o_ref[hh, rows, :] = acc[:, :V_DIM] / l
                    lse_ref[hh, rows, :] = jnp.broadcast_to(m_new + jnp.log2(l), (rb, 128))
                else:
                    acc_sc[hh, rows, :] = acc
                    m_sc[hh, rows, :] = jnp.broadcast_to(m_new, (rb, 128))

        def below(ki, carry):
            step(pl.multiple_of(ki * t, t), False)
            return carry

        lax.fori_loop(0, qi, below, 0)
        step(pl.multiple_of(qi * t, t), True)

    return _call_with_job(
        body, "attn_fwd" if job is None else "attn_fwd_comm", (N_HEADS // hp, nb), job,
        in_specs=[pl.BlockSpec((hp, t, HEAD_LANES), lambda h, qi: (h, qi, 0)),
                  pl.BlockSpec((hp, s, HEAD_LANES), lambda h, qi: (h, 0, 0)),
                  pl.BlockSpec((hp, s, HEAD_LANES), lambda h, qi: (h, 0, 0))],
        out_specs=[pl.BlockSpec((hp, t, V_DIM), lambda h, qi: (h, qi, 0)),
                   pl.BlockSpec((hp, t, 128), lambda h, qi: (h, qi, 0))],
        out_shape=[jax.ShapeDtypeStruct((N_HEADS, s, V_DIM), f32), jax.ShapeDtypeStruct((N_HEADS, s, 128), f32)],
        scratch_shapes=[pltpu.VMEM((hp, t, 128), f32), pltpu.VMEM((hp, t, HEAD_LANES), f32)],
        operands=(q, k, v), relay_at=(N_HEADS // hp - 1, max(nb - 2, 0)))


def _ssd_param_specs():
    return [_const((4, D_CONV)), _const((1, D_CONV)), _const((1, 128)), _const((1, 128)), _const((1, 128)),
            _const((1, D_SSD))]


def ssd_fwd(px, pz, plast, params):
    s = px.shape[0]
    nc = s // CHUNK

    def body(px_ref, pz_ref, pl_ref, cw_ref, cb_ref, dtb_ref, alog_ref, dskip_ref, snw_ref, yg_ref, st_ref,
             state_sc, halo_sc):
        i = pl.program_id(0)

        @pl.when(i == 0)
        def _():
            state_sc[...] = jnp.zeros(state_sc.shape, f32)
            halo_sc[...] = jnp.zeros(halo_sc.shape, f32)

        x = px_ref[...]
        prev = state_sc[...]
        st_ref[...] = prev
        xext = jnp.concatenate([halo_sc[...], x], axis=0)
        yg, new = _f_ssd(xext, pz_ref[...], pl_ref[...], prev, cw_ref[...], cb_ref[...], dtb_ref[...],
                         alog_ref[...], dskip_ref[...], snw_ref[...])
        yg_ref[...] = yg
        state_sc[...] = new
        halo_sc[...] = x[CHUNK - HALO:]

    tok = lambda w: pl.BlockSpec((CHUNK, w), lambda i: (i, 0))
    return pl.pallas_call(
        body, name="ssd_fwd", grid=(nc,),
        in_specs=[tok(D_CONV), tok(D_SSD), tok(128)] + _ssd_param_specs(),
        out_specs=[tok(D_SSD), pl.BlockSpec((None, N_HEADS // 2, 2 * SSD_HEAD_DIM, SSD_STATE), lambda i: (i, 0, 0, 0))],
        out_shape=[jax.ShapeDtypeStruct((s, D_SSD), f32),
                   jax.ShapeDtypeStruct((nc, N_HEADS // 2, 2 * SSD_HEAD_DIM, SSD_STATE), f32)],
        scratch_shapes=[pltpu.VMEM((N_HEADS // 2, 2 * SSD_HEAD_DIM, SSD_STATE), f32), pltpu.VMEM((HALO, D_CONV), f32)],
    )(px, pz, plast, *params)


def out_fwd(x, o, yg, g1, wo):
    s = x.shape[0]
    ts = _token_block(s)

    def body(x_ref, o_ref, yg_ref, g1_ref, wo_ref, out_ref):
        out_ref[...] = x_ref[...] + _f_out(o_ref[...], yg_ref[...], g1_ref[...], wo_ref[...])

    return pl.pallas_call(
        body, name="out_fwd", grid=(s // ts,),
        in_specs=[pl.BlockSpec((ts, D_MODEL), lambda i: (i, 0)), pl.BlockSpec((N_HEADS, ts, V_DIM), lambda i: (0, i, 0)),
                  pl.BlockSpec((ts, D_SSD), lambda i: (i, 0)), _const((1, D_MODEL)), _const((D_MODEL, D_MODEL))],
        out_specs=pl.BlockSpec((ts, D_MODEL), lambda i: (i, 0)),
        out_shape=jax.ShapeDtypeStruct((s, D_MODEL), f32),
    )(x, o, yg, g1, wo)


def mlp_fwd(x, nw, sh, sc, g2, wgu, wd):
    s = x.shape[0]
    ts = _token_block(s)
    nj = N_DEV // 2

    def body(x_ref, nw_ref, sh_ref, sc_ref, g2_ref, wg_ref, wu_ref, wd_ref, out_ref, mix_ref, h_ref):
        j = pl.program_id(1)

        @pl.when(j == 0)
        def _():
            h_ref[...] = _f_modulate(x_ref[...], nw_ref[...], sh_ref[...], sc_ref[...]).astype(bf16)

        act = _f_gate_up(h_ref[...], wg_ref[...], wu_ref[...])
        _accumulate(j == 0, [mix_ref], [mmw(act, wd_ref[...])])

        @pl.when(j == nj - 1)
        def _():
            out_ref[...] = x_ref[...] + g2_ref[...] * mix_ref[...]

    vec = _const((1, D_MODEL))
    tok = pl.BlockSpec((ts, D_MODEL), lambda i, j: (i, 0))
    return pl.pallas_call(
        body, name="mlp_fwd", grid=(s // ts, nj),
        in_specs=[tok, vec, vec, vec, vec,
                  pl.BlockSpec((None, FF_SHARD, D_MODEL), lambda i, j: (j, 0, 0)),
                  pl.BlockSpec((None, FF_SHARD, D_MODEL), lambda i, j: (j + nj, 0, 0)),
                  pl.BlockSpec((None, FF_SHARD, D_MODEL), lambda i, j: (j, 0, 0))],
        out_specs=[tok] * 3,
        out_shape=[jax.ShapeDtypeStruct((s, D_MODEL), f32), jax.ShapeDtypeStruct((s, D_MODEL), f32),
                   jax.ShapeDtypeStruct((s, D_MODEL), bf16)],
    )(x, nw, sh, sc, g2, wgu, wgu, wd)


def loss_fwd(y, target):
    s = y.shape[0]
    ts = _token_block(s)

    def body(y_ref, t_ref, dy_ref, loss_ref):
        d = y_ref[...] - t_ref[...]
        dy_ref[...] = d * (1.0 / D_MODEL)
        part = 0.5 * jnp.sum(jnp.sum(d * d, axis=-1, keepdims=True) * (1.0 / D_MODEL), axis=0, keepdims=True)
        _accumulate(pl.program_id(0) == 0, [loss_ref], [jnp.broadcast_to(part, (8, 128))])

    return pl.pallas_call(
        body, name="loss_fwd", grid=(s // ts,),
        in_specs=[pl.BlockSpec((ts, D_MODEL), lambda i: (i, 0))] * 2,
        out_specs=[pl.BlockSpec((ts, D_MODEL), lambda i: (i, 0)), _const((8, 128))],
        out_shape=[jax.ShapeDtypeStruct((s, D_MODEL), f32), jax.ShapeDtypeStruct((8, 128), f32)],
    )(y, target)


def mlp_bwd(h, dy, g2, wgu, wd, job=None):
    s = h.shape[0]
    ts = min(MLP_BWD_ROWS, s)
    nj = N_DEV // 2
    ni = s // ts

    def body(h_ref, dy_ref, g2_ref, wg_ref, wu_ref, wd_ref, dh_ref, dwg_ref, dwu_ref, dwd_ref, ag_sc, au_sc, ad_sc):
        i = pl.program_id(1)
        wg, wu, wd = wg_ref[...], wu_ref[...], wd_ref[...]
        act, vjp = jax.vjp(lambda h_, sg, su: _f_gate_up(h_, wg, wu, sg, su), h_ref[...].astype(f32),
                           jnp.zeros(wg.shape, f32), jnp.zeros(wu.shape, f32))
        dmix = dy_ref[...] * g2_ref[...]
        dact = _dot(dmix, wd, 1, 1)
        dwd = _dot(act, dmix, 0, 0)
        dh, dwg, dwu = vjp(dact)
        dh_ref[...] = dh.astype(bf16)
        _accumulate_then_cast(i == 0, i == ni - 1, [ag_sc, au_sc, ad_sc], [dwg_ref, dwu_ref, dwd_ref], [dwg, dwu, dwd])

    once = pl.Buffered(1)
    wspec = lambda off: pl.BlockSpec((None, FF_SHARD, D_MODEL), lambda j, i: (j + off, 0, 0), pipeline_mode=once)
    dspec = pl.BlockSpec((None, FF_SHARD, D_MODEL), lambda j, i: (j, 0, 0), pipeline_mode=once)
    return _call_with_job(
        body, "mlp_bwd" if job is None else "mlp_bwd_comm", (nj, ni), job,
        in_specs=[pl.BlockSpec((ts, D_MODEL), lambda j, i: (i, 0)), pl.BlockSpec((ts, D_MODEL), lambda j, i: (i, 0)),
                  _const((1, D_MODEL)), wspec(0), wspec(nj), dspec],
        out_specs=[pl.BlockSpec((None, ts, D_MODEL), lambda j, i: (j, i, 0)), wspec(0), wspec(0), dspec],
        out_shape=[jax.ShapeDtypeStruct((nj, s, D_MODEL), bf16),
                   jax.ShapeDtypeStruct((nj, FF_SHARD, D_MODEL), bf16), jax.ShapeDtypeStruct((nj, FF_SHARD, D_MODEL), bf16),
                   jax.ShapeDtypeStruct((nj, FF_SHARD, D_MODEL), bf16)],
        scratch_shapes=[pltpu.VMEM((FF_SHARD, D_MODEL), f32), pltpu.VMEM((FF_SHARD, D_MODEL), f32),
                        pltpu.VMEM((FF_SHARD, D_MODEL), f32)],
        operands=(h, dy, g2, wgu, wgu, wd))


def out_bwd(dy, dhparts, x, nw, sh, sc, mix, o, yg, g1, wo):
    s = dy.shape[0]
    ts = _token_block(s)
    nj = dhparts.shape[0]

    ni = s // ts

    def body(dy_ref, dp_ref, x_ref, nw_ref, sh_ref, sc_ref, mix_ref, o_ref, yg_ref, g1_ref, wo_ref,
             dx_ref, dnw_ref, dsh_ref, dsc_ref, do_ref, delta_ref, dyg_ref, dg1_ref, dg2_ref, dwo_ref, acc_sc):
        i = pl.program_id(0)
        g = dy_ref[...]
        _accumulate(i == 0, [dg2_ref], [jnp.sum(g * mix_ref[...], axis=0, keepdims=True)])
        dh = dp_ref[0].astype(f32)
        for j in range(1, nj):
            dh = dh + dp_ref[j].astype(f32)
        _, vjp_mod = jax.vjp(_f_modulate, x_ref[...], nw_ref[...], sh_ref[...], sc_ref[...])
        dx_mod, dnw, dsh, dsc = vjp_mod(dh)
        _accumulate(i == 0, [dnw_ref, dsh_ref, dsc_ref], [dnw, dsh, dsc])
        g = g + dx_mod
        dx_ref[...] = g
        o = o_ref[...]
        wo = wo_ref[...]
        _, vjp = jax.vjp(lambda o_, yg_, g1_, slot: _f_out(o_, yg_, g1_, wo, slot), o, yg_ref[...], g1_ref[...],
                         jnp.zeros(wo.shape, f32))
        do, dyg, dg1, dwo = vjp(g)
        do_ref[...] = do.astype(bf16)
        dyg_ref[...] = dyg
        delta_ref[...] = jnp.broadcast_to(jnp.sum(do * o, axis=-1, keepdims=True), delta_ref.shape)
        _accumulate(i == 0, [dg1_ref], [dg1])
        _accumulate_then_cast(i == 0, i == ni - 1, [acc_sc], [dwo_ref], [dwo])

    head = pl.BlockSpec((N_HEADS, ts, V_DIM), lambda i: (0, i, 0))
    tok = pl.BlockSpec((ts, D_MODEL), lambda i: (i, 0))
    vec = _const((1, D_MODEL))
    vshape = jax.ShapeDtypeStruct((1, D_MODEL), f32)
    return pl.pallas_call(
        body, name="out_bwd", grid=(ni,), scratch_shapes=[pltpu.VMEM((D_MODEL, D_MODEL), f32)],
        in_specs=[tok, pl.BlockSpec((nj, ts, D_MODEL), lambda i: (0, i, 0)), tok, vec, vec, vec, tok,
                  head, pl.BlockSpec((ts, D_SSD), lambda i: (i, 0)), vec, _const((D_MODEL, D_MODEL))],
        out_specs=[tok, vec, vec, vec, head,
                   pl.BlockSpec((N_HEADS, ts, 128), lambda i: (0, i, 0)), pl.BlockSpec((ts, D_SSD), lambda i: (i, 0)),
                   vec, vec, _const((D_MODEL, D_MODEL))],
        out_shape=[jax.ShapeDtypeStruct((s, D_MODEL), f32), vshape, vshape, vshape,
                   jax.ShapeDtypeStruct((N_HEADS, s, V_DIM), bf16),
                   jax.ShapeDtypeStruct((N_HEADS, s, 128), f32), jax.ShapeDtypeStruct((s, D_SSD), f32),
                   vshape, vshape, jax.ShapeDtypeStruct((D_MODEL, D_MODEL), bf16)],
    )(dy, dhparts, x, nw, sh, sc, mix, o, yg, g1, wo)


def attn_bwd(q, k, v, do, lse, delta, job=None):
    s = q.shape[1]
    t = _token_block(s)
    nb = s // t

    hp = ATTN_HEADS_BWD

    def body(q_ref, k_ref, v_ref, do_ref, lse_ref, delta_ref, dq_ref, dk_ref, dv_ref):
        ki = pl.program_id(1)

        @pl.when(ki == 0)
        def _():
            dq_ref[...] = jnp.zeros(dq_ref.shape, f32)

        dk_ref[...] = jnp.zeros(dk_ref.shape, f32)
        dv_ref[...] = jnp.zeros(dv_ref.shape, f32)

        def step(q0, diagonal):
            rows = pl.ds(q0, t)

            def products(hh):
                sc = _scores(q_ref[hh, rows, :], k_ref[hh])
                dp = lax.dot_general(do_ref[hh, rows, :], v_ref[hh, :, :V_DIM], (((1,), (1,)), ((), ())),
                                     preferred_element_type=f32)
                return (jnp.where(_tril(t, t, 0), sc, NEG) if diagonal else sc), dp

            ahead = products(0)
            for hh in range(hp):
                sc, dp = ahead
                if hh + 1 < hp:
                    ahead = products(hh + 1)
                p = jnp.exp2(sc - jnp.tile(lse_ref[hh, rows, :], (1, t // 128)))
                ds = (p * (dp - jnp.tile(delta_ref[hh, rows, :], (1, t // 128)))).astype(bf16)
                dv_ref[hh] += lax.dot_general(p.astype(bf16), do_ref[hh, rows, :], (((0,), (0,)), ((), ())),
                                              preferred_element_type=f32)
                dk_ref[hh] += lax.dot_general(ds, q_ref[hh, rows, :], (((0,), (0,)), ((), ())), preferred_element_type=f32)
                dq_ref[hh, rows, :] += jnp.dot(ds, k_ref[hh], preferred_element_type=f32)

        step(pl.multiple_of(ki * t, t), True)

        def above(qi, carry):
            step(pl.multiple_of(qi * t, t), False)
            return carry

        lax.fori_loop(ki + 1, nb, above, 0)
        dk_ref[...] = dk_ref[...] * LN2

        @pl.when(ki == nb - 1)
        def _():
            dq_ref[...] = dq_ref[...] * LN2

    qspec = lambda w: pl.BlockSpec((hp, s, w), lambda h, ki: (h, 0, 0))
    kspec = lambda w: pl.BlockSpec((hp, t, w), lambda h, ki: (h, ki, 0))
    return _call_with_job(
        body, "attn_bwd" if job is None else "attn_bwd_comm", (N_HEADS // hp, nb), job,
        in_specs=[qspec(HEAD_LANES), kspec(HEAD_LANES), kspec(HEAD_LANES), qspec(V_DIM), qspec(128), qspec(128)],
        out_specs=[qspec(HEAD_LANES), kspec(HEAD_LANES), kspec(V_DIM)],
        out_shape=[jax.ShapeDtypeStruct((N_HEADS, s, HEAD_LANES), f32), jax.ShapeDtypeStruct((N_HEADS, s, HEAD_LANES), f32),
                   jax.ShapeDtypeStruct((N_HEADS, s, V_DIM), f32)],
        scratch_shapes=[], operands=(q, k, v, do, lse, delta))


def ssd_bwd(px, pz, plast, states, dyg, params):
    s = px.shape[0]
    nc = s // CHUNK
    per = CHUNK // HALO

    def body(px_ref, halo_ref, pz_ref, pl_ref, st_ref, dyg_ref, cw_ref, cb_ref, dtb_ref, alog_ref, dskip_ref, snw_ref,
             dpx_ref, dpz_ref, dpl_ref, dcw_ref, dcb_ref, ddtb_ref, dalog_ref, ddskip_ref, dsnw_ref, dstate_sc, dhalo_sc):
        t = pl.program_id(0)
        chunk = nc - 1 - t

        @pl.when(t == 0)
        def _():
            dstate_sc[...] = jnp.zeros(dstate_sc.shape, f32)
            dhalo_sc[...] = jnp.zeros(dhalo_sc.shape, f32)

        halo = jnp.where(chunk > 0, halo_ref[...], 0.0)
        xext = jnp.concatenate([halo, px_ref[...]], axis=0)
        _, vjp = jax.vjp(_f_ssd, xext, pz_ref[...], pl_ref[...], st_ref[...], cw_ref[...], cb_ref[...], dtb_ref[...],
                         alog_ref[...], dskip_ref[...], snw_ref[...])
        dxext, dz, dpl, dprev, dcw, dcb, ddtb, dalog, ddskip, dsnw = vjp((dyg_ref[...], dstate_sc[...]))
        dpx_ref[...] = dxext[HALO:]
        dpx_ref[CHUNK - HALO:, :] += dhalo_sc[...]
        dhalo_sc[...] = dxext[:HALO]
        dstate_sc[...] = dprev
        dpz_ref[...] = dz
        dpl_ref[...] = dpl
        _accumulate(t == 0, [dcw_ref, dcb_ref, ddtb_ref, dalog_ref, ddskip_ref, dsnw_ref],
                    [dcw, dcb, ddtb, dalog, ddskip, dsnw])

    rev = lambda w: pl.BlockSpec((CHUNK, w), lambda t: (nc - 1 - t, 0))
    pshapes = [jax.ShapeDtypeStruct((4, D_CONV), f32), jax.ShapeDtypeStruct((1, D_CONV), f32),
               jax.ShapeDtypeStruct((1, 128), f32), jax.ShapeDtypeStruct((1, 128), f32),
               jax.ShapeDtypeStruct((1, 128), f32), jax.ShapeDtypeStruct((1, D_SSD), f32)]
    return pl.pallas_call(
        body, name="ssd_bwd", grid=(nc,),
        in_specs=[rev(D_CONV),
                  pl.BlockSpec((HALO, D_CONV), lambda t: (jnp.maximum((nc - 1 - t) * per - 1, 0), 0)),
                  rev(D_SSD), rev(128),
                  pl.BlockSpec((None, N_HEADS // 2, 2 * SSD_HEAD_DIM, SSD_STATE), lambda t: (nc - 1 - t, 0, 0, 0)),
                  rev(D_SSD)] + _ssd_param_specs(),
        out_specs=[rev(D_CONV), rev(D_SSD), rev(128)] + _ssd_param_specs(),
        out_shape=[jax.ShapeDtypeStruct((s, D_CONV), f32), jax.ShapeDtypeStruct((s, D_SSD), f32),
                   jax.ShapeDtypeStruct((s, 128), f32)] + pshapes,
        scratch_shapes=[pltpu.VMEM((N_HEADS // 2, 2 * SSD_HEAD_DIM, SSD_STATE), f32), pltpu.VMEM((HALO, D_CONV), f32)],
    )(px, px, pz, plast, states, dyg, *params)


def qkv_bwd(pa, plast, cos_t, sin_t, params, dq, dk, dv):
    s = pa.shape[0]
    ts = _token_block(s)

    def body(pa_ref, pl_ref, cos_ref, sin_ref, *rest):
        qaw, kvaw, wq, wk, wv, qnw, knw, kpw = [r[...] for r in rest[:8]]
        dq_ref, dk_ref, dv_ref = rest[8:11]
        dpa_ref, dpl_ref = rest[11:13]
        dprm_refs = list(rest[13:])
        cos_t, sin_t = cos_ref[...], sin_ref[...]

        def stage(pa_, pl_, qaw_, kvaw_, sq, sk, sv, qnw_, knw_, kpw_):
            return _f_qkv(pa_, pl_, cos_t, sin_t, qaw_, kvaw_, wq, wk, wv, qnw_, knw_, kpw_, (sq, sk, sv))

        _, vjp = jax.vjp(stage, pa_ref[...], pl_ref[...], qaw, kvaw, jnp.zeros(wq.shape, f32), jnp.zeros(wk.shape, f32),
                         jnp.zeros(wv.shape, f32), qnw, knw, kpw)
        grads = vjp((dq_ref[...], dk_ref[...], dv_ref[...]))
        dpa_ref[...] = grads[0]
        dpl_ref[...] = grads[1]
        _accumulate(pl.program_id(0) == 0, dprm_refs, list(grads[2:]))

    tok = lambda w: pl.BlockSpec((ts, w), lambda i: (i, 0))
    head = lambda w: pl.BlockSpec((N_HEADS, ts, w), lambda i: (0, i, 0))
    pshapes = [jax.ShapeDtypeStruct((1, Q_RANK), f32), jax.ShapeDtypeStruct((1, KV_RANK), f32),
               jax.ShapeDtypeStruct((N_HEADS, Q_RANK, HEAD_LANES), f32), jax.ShapeDtypeStruct((N_HEADS, KV_RANK, HEAD_LANES), f32),
               jax.ShapeDtypeStruct((N_HEADS, KV_RANK, V_DIM), f32), jax.ShapeDtypeStruct((1, HEAD_LANES), f32),
               jax.ShapeDtypeStruct((1, HEAD_LANES), f32), jax.ShapeDtypeStruct((1, HEAD_LANES), f32)]
    return pl.pallas_call(
        body, name="qkv_bwd", grid=(s // ts,),
        in_specs=[tok(384), tok(128), tok(128), tok(128)] + _qkv_param_specs()
                 + [head(HEAD_LANES), head(HEAD_LANES), head(V_DIM)],
        out_specs=[tok(384), tok(128)] + _qkv_param_specs(),
        out_shape=[jax.ShapeDtypeStruct((s, 384), f32), jax.ShapeDtypeStruct((s, 128), f32)] + pshapes,
    )(pa, plast, cos_t, sin_t, *params, dq, dk, dv)


def proj_bwd(x, nw, sh, sc, w, dpa, dpz, dpx, dpl_k, dpl_dt, dres):
    s = x.shape[0]
    ts = _token_block(s)

    ni = s // ts

    def body(x_ref, nw_ref, sh_ref, sc_ref, w_ref, dpa_ref, dpz_ref, dpx_ref, dplk_ref, dpld_ref, dres_ref,
             dx_ref, dnw_ref, dsh_ref, dsc_ref, dw_ref, acc_sc):
        i = pl.program_id(0)
        g = jnp.concatenate([dpa_ref[...], dpz_ref[...], dpx_ref[...], dplk_ref[...] + dpld_ref[...]], axis=1)
        w = w_ref[...]
        _, vjp = jax.vjp(lambda x_, nw_, sh_, sc_, slot: _f_proj(x_, nw_, sh_, sc_, w, slot), x_ref[...], nw_ref[...],
                         sh_ref[...], sc_ref[...], jnp.zeros(w.shape, f32))
        dx, dnw, dsh, dsc, dw = vjp(g)
        dx_ref[...] = dx + dres_ref[...]
        _accumulate(i == 0, [dnw_ref, dsh_ref, dsc_ref], [dnw, dsh, dsc])
        _accumulate_then_cast(i == 0, i == ni - 1, [acc_sc], [dw_ref], [dw])

    vec = _const((1, D_MODEL))
    vshape = jax.ShapeDtypeStruct((1, D_MODEL), f32)
    tok = lambda w_: pl.BlockSpec((ts, w_), lambda i: (i, 0))
    return pl.pallas_call(
        body, name="proj_bwd", grid=(ni,), scratch_shapes=[pltpu.VMEM((D_PROJ, D_MODEL), f32)],
        in_specs=[tok(D_MODEL), vec, vec, vec, _const((D_PROJ, D_MODEL)), tok(384), tok(512), tok(1024), tok(128), tok(128),
                  tok(D_MODEL)],
        out_specs=[tok(D_MODEL), vec, vec, vec, _const((D_PROJ, D_MODEL))],
        out_shape=[jax.ShapeDtypeStruct((s, D_MODEL), f32), vshape, vshape, vshape,
                   jax.ShapeDtypeStruct((D_PROJ, D_MODEL), bf16)],
    )(x, nw, sh, sc, w, dpa, dpz, dpx, dpl_k, dpl_dt, dres)


def ada_fwd(c_all, w_ada, b_cols):
    def body(c_ref, w_ref, b_ref, out_ref):
        act = jax.nn.silu(c_ref[...])
        for l in range(2):
            out_ref[l] = jnp.dot(act, w_ref[l], precision=lax.Precision.HIGHEST, preferred_element_type=f32) + b_ref[l]

    return pl.pallas_call(body, name="ada_fwd", out_shape=jax.ShapeDtypeStruct((2, N_DEV, 768), f32))(c_all, w_ada, b_cols)


def ada_bwd(c_all, dmod_cols):
    def body(c_ref, d_ref, out_ref):
        out_ref[0] = lax.dot_general(jax.nn.silu(c_ref[...]), d_ref[0], (((0,), (0,)), ((), ())),
                                     precision=lax.Precision.HIGHEST, preferred_element_type=f32)

    return pl.pallas_call(
        body, name="ada_bwd", grid=(2,),
        in_specs=[_const((N_DEV, D_MODEL)), pl.BlockSpec((1, N_DEV, 768), lambda l: (l, 0, 0))],
        out_specs=pl.BlockSpec((1, D_MODEL, 768), lambda l: (l, 0, 0)),
        out_shape=jax.ShapeDtypeStruct((2, D_MODEL, 768), f32),
    )(c_all, dmod_cols)


def _adamw(w, g, m, v):
    m = ADAM_B1 * m + (1.0 - ADAM_B1) * g
    v = ADAM_B2 * v + (1.0 - ADAM_B2) * (g * g)
    m_hat = m / (1.0 - ADAM_B1 ** ADAM_STEP)
    v_hat = v / (1.0 - ADAM_B2 ** ADAM_STEP)
    delta = -ADAM_LR * (m_hat / (jnp.sqrt(v_hat) + ADAM_EPS) + ADAM_WD * w)
    return delta, m, v


def adamw(parts, w, m, v, layer, prev, name):
    n, r, c = parts.shape
    nl = w.shape[0]
    per_elem = 2 * (n * parts.dtype.itemsize + 7 * 4)
    lanes = -(-c // 128) * 128
    tr, tc = r, c
    if per_elem * r * lanes > ADAMW_BLOCK_BYTES:
        fits = [t for t in (256, 128, 64, 32, 16, 8) if r % t == 0]
        if fits:
            tr = fits[0]
        else:
            tc = next(t for t in (512, 256, 128) if c % t == 0)

    def body(p_ref, w_ref, m_ref, v_ref, *rest):
        g_ref, d_ref, nm_ref, nv_ref = rest[-4:]
        g = p_ref[0].astype(f32)
        for k in range(1, n):
            g = g + p_ref[k].astype(f32)
        delta, nm, nv = _adamw(w_ref[...], g, m_ref[...], v_ref[...])
        g_ref[...] = g
        d_ref[...] = delta
        nm_ref[...] = nm
        nv_ref[...] = nv

    blk = pl.BlockSpec((None, tr, tc), lambda i, j: (layer, i, j))
    shp = jax.ShapeDtypeStruct((nl, r, c), f32)
    kept = [] if prev is None else list(prev)
    return pl.pallas_call(
        body, name=name, grid=(r // tr, c // tc),
        in_specs=[pl.BlockSpec((n, tr, tc), lambda i, j: (0, i, j)), blk, blk, blk] + [ANY] * len(kept),
        out_specs=[blk] * 4, out_shape=[shp] * 4,
        input_output_aliases={4 + j: j for j in range(len(kept))},
    )(parts, w, m, v, *kept)


def _my_index():
    return 4 * lax.axis_index("x") + 2 * lax.axis_index("y") + lax.axis_index("c")


def _coords(idx):
    return (idx // 4, (idx // 2) % 2, idx % 2)


class CommJob:
    def __init__(self, operands, out_shape, phases, scratch):
        self.operands, self.out_shape, self.phases, self.scratch = operands, out_shape, phases, scratch


def _wait(out, n_blocks, send_sem, recv_sem, send=True, recv=True):
    span = out.at[pl.ds(0, n_blocks)]
    desc = pltpu.make_async_remote_copy(src_ref=span, dst_ref=span, send_sem=send_sem, recv_sem=recv_sem,
                                        device_id=_coords(_my_index()), device_id_type=MESH)
    if recv:
        desc.wait_recv()
    if send:
        desc.wait_send()


def gather_job(shards):
    n = len(shards)

    def places():
        x, y, c = lax.axis_index("x"), lax.axis_index("y"), lax.axis_index("c")
        return (x, y, c), (x, y, 1 - c), [(1 - x, y), (x, 1 - y), (1 - x, 1 - y)]

    def index(p):
        return 4 * p[0] + 2 * p[1] + p[2]

    def start(ins, outs, sems):
        far_send, far_recv, near_send, near_recv, local = sems
        me, sibling, chips = places()
        for k in range(n):
            pltpu.make_async_copy(ins[k], outs[k].at[index(me)], local.at[k]).start()
            for chip in chips:
                pltpu.make_async_remote_copy(src_ref=ins[k], dst_ref=outs[k].at[index(me)], send_sem=far_send.at[k],
                                             recv_sem=far_recv.at[k], device_id=(*chip, me[2]), device_id_type=MESH).start()
            pltpu.make_async_remote_copy(src_ref=ins[k], dst_ref=outs[k].at[index(me)], send_sem=near_send.at[k],
                                         recv_sem=near_recv.at[k], device_id=sibling, device_id_type=MESH).start()

    def relay(ins, outs, sems):
        far_send, far_recv, near_send, near_recv, local = sems
        me, sibling, chips = places()
        for k in range(n):
            _wait(outs[k], 3, far_send.at[k], far_recv.at[k], send=False)
            for chip in chips:
                block = outs[k].at[index((*chip, me[2]))]
                pltpu.make_async_remote_copy(src_ref=block, dst_ref=block, send_sem=near_send.at[k],
                                             recv_sem=near_recv.at[k], device_id=sibling, device_id_type=MESH).start()

    def finish(ins, outs, sems):
        far_send, far_recv, near_send, near_recv, local = sems
        for k in range(n):
            _wait(outs[k], 4, near_send.at[k], near_recv.at[k])
            _wait(outs[k], 3, far_send.at[k], far_recv.at[k], recv=False)
            pltpu.make_async_copy(ins[k], outs[k].at[0], local.at[k]).wait()

    shapes = [jax.ShapeDtypeStruct((N_DEV,) + tuple(a.shape), a.dtype) for a in shards]
    return CommJob(list(shards), shapes, [start, relay, finish], [pltpu.SemaphoreType.DMA((n,))] * 5)


def scatter_job(tensors):
    n = len(tensors)
    flat, where = [], {}
    for k, pieces in enumerate(tensors):
        d = 0
        for piece in pieces:
            for b in range(piece.shape[0]):
                where[k, d] = (len(flat), b)
                d += 1
            flat.append(piece)
        assert d == N_DEV

    def start(ins, outs, sems):
        send_sems, recv_sems, local_sems = sems
        me = _my_index()

        def block(k, d):
            i, b = where[k, d]
            return ins[i].at[b]

        for d in range(N_DEV):
            @pl.when(d != me)
            def _():
                for k in range(n):
                    pltpu.make_async_remote_copy(src_ref=block(k, d), dst_ref=outs[k].at[me], send_sem=send_sems.at[k],
                                                 recv_sem=recv_sems.at[k], device_id=(d // 4, (d // 2) % 2, d % 2),
                                                 device_id_type=MESH).start()

            @pl.when(d == me)
            def _():
                for k in range(n):
                    pltpu.make_async_copy(block(k, d), outs[k].at[d], local_sems.at[k]).start()

    def finish(ins, outs, sems):
        send_sems, recv_sems, local_sems = sems
        for k in range(n):
            _wait(outs[k], N_DEV - 1, send_sems.at[k], recv_sems.at[k])
            i, b = where[k, 0]
            pltpu.make_async_copy(ins[i].at[b], outs[k].at[0], local_sems.at[k]).wait()

    shapes = [jax.ShapeDtypeStruct((N_DEV,) + tuple(p[0].shape[1:]), p[0].dtype) for p in tensors]
    return CommJob(flat, shapes, [start, finish], [pltpu.SemaphoreType.DMA((n,))] * 3)


def comm_call(job, name):
    ni, no = len(job.operands), len(job.out_shape)

    def body(*refs):
        ins, outs, sems = refs[:ni], refs[ni:ni + no], refs[ni + no:]
        for phase in job.phases:
            phase(ins, outs, sems)

    return pl.pallas_call(body, name=name, in_specs=[ANY] * ni, out_specs=[ANY] * no, out_shape=job.out_shape,
                          scratch_shapes=job.scratch)(*job.operands)


def _carry(job, body, n_in, n_out, at_step):
    ji, jo, js = len(job.operands), len(job.out_shape), len(job.scratch)

    def carrier(*refs):
        a, b = n_in, n_in + ji
        c, d = b + n_out, b + n_out + jo
        e = len(refs) - js
        job_refs = (refs[a:b], refs[c:d], refs[e:])
        n = len(job.phases)

        @pl.when(at_step(0, n))
        def _():
            job.phases[0](*job_refs)

        body(*refs[:a], *refs[b:c], *refs[d:e])

        for i in range(1, n):
            @pl.when(at_step(i, n))
            def _():
                job.phases[i](*job_refs)

    return carrier


def _pad_lanes(v, lo, total=128):
    return jnp.pad(v, (lo, total - lo - v.shape[0]))[None, :]


MIXER_WEIGHTS = ("w_in", "w_q_up", "w_kv_up", "conv_w")
LATE_WEIGHTS = ("w_out", "w_gate_up", "w_down")


def mixer_operands(g, sw):
    w_in = g["w_in"].reshape(D_IN, D_MODEL)
    zero = lambda rows: jnp.zeros((rows, D_MODEL), w_in.dtype)
    w_proj = jnp.concatenate(
        [w_in[:384], w_in[416:928], w_in[928:1952], w_in[1952:1960], zero(56), w_in[384:416], zero(32)], axis=0)
    wq = jnp.pad(g["w_q_up"], ((0, 0), (0, 0), (0, HEAD_LANES - NOPE - ROPE)))
    wk = jnp.pad(g["w_kv_up"][:, :, :NOPE], ((0, 0), (0, 0), (0, HEAD_LANES - NOPE)))
    wv = g["w_kv_up"][:, :, NOPE:]
    qkv = (sw["q_a_norm_w"][None, :], sw["kv_a_norm_w"][None, :], wq, wk, wv,
           _pad_lanes(jnp.concatenate([sw["q_nope_norm_w"], sw["q_pe_norm_w"]]), 0),
           _pad_lanes(sw["k_nope_norm_w"], 0), _pad_lanes(sw["k_pe_norm_w"], NOPE))
    conv_w = g["conv_w"].astype(f32).transpose(1, 0, 2).reshape(4, D_CONV)
    ssd = (conv_w, sw["conv_b"][None, :], _pad_lanes(sw["dt_bias"], 0), _pad_lanes(sw["a_log"], 0),
           _pad_lanes(sw["d_skip"], 0), sw["ssd_norm_w"][None, :])
    return dict(w_proj=w_proj, qkv=qkv, ssd=ssd, n1=sw["norm1_w"][None, :])


def late_operands(g, sw):
    return dict(wo=g["w_out"].reshape(D_MODEL, D_MODEL), wgu=g["w_gate_up"],
                wd=g["w_down"].reshape(N_DEV // 2, FF_SHARD, D_MODEL), n2=sw["norm2_w"][None, :])


def layer_fwd(x, mod, kw, cos_t, sin_t, job=None, late=None):
    sh1, sc1, g1, sh2, sc2, g2 = [mod[i:i + 1] for i in range(6)]
    pa, pz, px, plast = proj_fwd(x, kw["n1"], sh1, sc1, kw["w_proj"])
    q, k, v = qkv_fwd(pa, plast, cos_t, sin_t, kw["qkv"])
    (o, lse), carried = attn_fwd(q, k, v, job)
    if late is not None:
        kw = {**kw, **late(carried)}
    yg, states = ssd_fwd(px, pz, plast, kw["ssd"])
    x_mid = out_fwd(x, o, yg, g1, kw["wo"])
    x_out, mix, h_mid = mlp_fwd(x_mid, kw["n2"], sh2, sc2, g2, kw["wgu"], kw["wd"])
    saved = dict(x=x, pa=pa, pz=pz, px=px, plast=plast, q=q, k=k, v=v, o=o, lse=lse, yg=yg, states=states, x_mid=x_mid,
                 mix=mix, h_mid=h_mid)
    return x_out, saved, kw, carried


def layer_bwd_head(dy, mod, kw, sv, job=None):
    _, _, g1, sh2, sc2, g2 = [mod[i:i + 1] for i in range(6)]
    (dhparts, dwg, dwu, dwd), carried = mlp_bwd(sv["h_mid"], dy, g2, kw["wgu"], kw["wd"], job)
    dmid, dn2, dsh2, dsc2, do, delta, dyg, dg1, dg2, dwo = out_bwd(
        dy, dhparts, sv["x_mid"], kw["n2"], sh2, sc2, sv["mix"], sv["o"], sv["yg"], g1, kw["wo"])
    early = dict(w_out=[dwo.reshape(N_DEV, D_MODEL // N_DEV, D_MODEL)], w_gate_up=[dwg, dwu],
                 w_down=[dwd.reshape(N_DEV, D_FF // N_DEV, D_MODEL)])
    head = dict(dmid=dmid, do=do, delta=delta, dyg=dyg, dn2=dn2, dsh2=dsh2, dsc2=dsc2, dg2=dg2, dg1=dg1)
    return head, early, carried


def layer_bwd_tail(hd, mod, kw, cos_t, sin_t, sv, job=None):
    sh1, sc1 = mod[0:1], mod[1:2]
    (dq, dk, dv), carried = attn_bwd(sv["q"], sv["k"], sv["v"], hd["do"], sv["lse"], hd["delta"], job)
    dpx, dpz, dpl_dt, dcw, dcb, ddtb, dalog, ddskip, dsnw = ssd_bwd(sv["px"], sv["pz"], sv["plast"], sv["states"],
                                                                   hd["dyg"], kw["ssd"])
    dpa, dpl_k, dqaw, dkvaw, dwq, dwk, dwv, dqnw, dknw, dkpw = qkv_bwd(sv["pa"], sv["plast"], cos_t, sin_t, kw["qkv"],
                                                                       dq, dk, dv)
    dx, dn1, dsh1, dsc1, dwp = proj_bwd(sv["x"], kw["n1"], sh1, sc1, kw["w_proj"], dpa, dpz, dpx, dpl_k, dpl_dt, hd["dmid"])
    dmod = jnp.concatenate([dsh1, dsc1, hd["dg1"], hd["dsh2"], hd["dsc2"], hd["dg2"]], axis=0)
    dw_in = jnp.concatenate([dwp[:384], dwp[1984:2016], dwp[384:1920], dwp[1920:1928]], axis=0)
    grads = dict(
        norm1_w=dn1[0], norm2_w=hd["dn2"][0], q_a_norm_w=dqaw[0], kv_a_norm_w=dkvaw[0],
        q_nope_norm_w=dqnw[0, :NOPE], q_pe_norm_w=dqnw[0, NOPE:NOPE + ROPE], k_nope_norm_w=dknw[0, :NOPE],
        k_pe_norm_w=dkpw[0, NOPE:NOPE + ROPE], conv_b=dcb[0], dt_bias=ddtb[0, :N_HEADS], a_log=dalog[0, :N_HEADS],
        d_skip=ddskip[0, :N_HEADS], ssd_norm_w=dsnw[0],
        w_in=[dw_in.reshape(N_DEV, D_IN // N_DEV, D_MODEL)],
        w_q_up=[dwq[:, :, :NOPE + ROPE].astype(bf16)],
        w_kv_up=[jnp.concatenate([dwk[:, :, :NOPE], dwv], axis=2).astype(bf16)],
        conv_w=[dcw.reshape(4, N_DEV, D_CONV // N_DEV).transpose(1, 0, 2).astype(bf16)],
    )
    return dx, dmod, grads, carried


def _pack_small(get, last=None):
    flat = jnp.concatenate([get(name).reshape(-1) for name, _ in SMALL])
    flat = jnp.pad(flat, (0, SMALL_ROWS * 128 - flat.shape[0]))
    if last is not None:
        flat = flat.at[-1].set(last)
    return flat.reshape(SMALL_ROWS, 128)


def _unpack_small(packed):
    flat = packed.reshape(-1)
    out, off = {}, 0
    for name, size in SMALL:
        out[name] = flat[off:off + 2 * size].reshape(2, size)
        off += 2 * size
    return out


def kernel(x, c, positions, norm1_w, norm2_w, w_ada, b_ada, w_in, q_a_norm_w, w_q_up, kv_a_norm_w, w_kv_up, q_nope_norm_w, q_pe_norm_w, k_nope_norm_w, k_pe_norm_w, conv_w, conv_b, dt_bias, a_log, d_skip, ssd_norm_w, w_out, w_gate_up, w_down, loss_target, m_norm1_w, m_norm2_w, m_w_ada, m_b_ada, m_w_in, m_q_a_norm_w, m_w_q_up, m_kv_a_norm_w, m_w_kv_up, m_q_nope_norm_w, m_q_pe_norm_w, m_k_nope_norm_w, m_k_pe_norm_w, m_conv_w, m_conv_b, m_dt_bias, m_a_log, m_d_skip, m_ssd_norm_w, m_w_out, m_w_gate_up, m_w_down, v_norm1_w, v_norm2_w, v_w_ada, v_b_ada, v_w_in, v_q_a_norm_w, v_w_q_up, v_kv_a_norm_w, v_w_kv_up, v_q_nope_norm_w, v_q_pe_norm_w, v_k_nope_norm_w, v_k_pe_norm_w, v_conv_w, v_conv_b, v_dt_bias, v_a_log, v_d_skip, v_ssd_norm_w, v_w_out, v_w_gate_up, v_w_down):
    w = dict(norm1_w=norm1_w, norm2_w=norm2_w, w_ada=w_ada, b_ada=b_ada, w_in=w_in, q_a_norm_w=q_a_norm_w, w_q_up=w_q_up,
             kv_a_norm_w=kv_a_norm_w, w_kv_up=w_kv_up, q_nope_norm_w=q_nope_norm_w, q_pe_norm_w=q_pe_norm_w,
             k_nope_norm_w=k_nope_norm_w, k_pe_norm_w=k_pe_norm_w, conv_w=conv_w, conv_b=conv_b, dt_bias=dt_bias,
             a_log=a_log, d_skip=d_skip, ssd_norm_w=ssd_norm_w, w_out=w_out, w_gate_up=w_gate_up, w_down=w_down)
    m = dict(norm1_w=m_norm1_w, norm2_w=m_norm2_w, w_ada=m_w_ada, b_ada=m_b_ada, w_in=m_w_in, q_a_norm_w=m_q_a_norm_w,
             w_q_up=m_w_q_up, kv_a_norm_w=m_kv_a_norm_w, w_kv_up=m_w_kv_up, q_nope_norm_w=m_q_nope_norm_w,
             q_pe_norm_w=m_q_pe_norm_w, k_nope_norm_w=m_k_nope_norm_w, k_pe_norm_w=m_k_pe_norm_w, conv_w=m_conv_w,
             conv_b=m_conv_b, dt_bias=m_dt_bias, a_log=m_a_log, d_skip=m_d_skip, ssd_norm_w=m_ssd_norm_w, w_out=m_w_out,
             w_gate_up=m_w_gate_up, w_down=m_w_down)
    v = dict(norm1_w=v_norm1_w, norm2_w=v_norm2_w, w_ada=v_w_ada, b_ada=v_b_ada, w_in=v_w_in, q_a_norm_w=v_q_a_norm_w,
             w_q_up=v_w_q_up, kv_a_norm_w=v_kv_a_norm_w, w_kv_up=v_w_kv_up, q_nope_norm_w=v_q_nope_norm_w,
             q_pe_norm_w=v_q_pe_norm_w, k_nope_norm_w=v_k_nope_norm_w, k_pe_norm_w=v_k_pe_norm_w, conv_w=v_conv_w,
             conv_b=v_conv_b, dt_bias=v_dt_bias, a_log=v_a_log, d_skip=v_d_skip, ssd_norm_w=v_ssd_norm_w, w_out=v_w_out,
             w_gate_up=v_w_gate_up, w_down=v_w_down)
    me = _my_index()
    seq = x.shape[1]

    def shard(name, l):
        if name == "conv_w":
            return w[name][l]
        if name in TRANSPOSED:
            return jnp.swapaxes(w[name][l], 0, 1).astype(bf16)
        return w[name][l].astype(bf16)

    def shards(names, l):
        return [shard(name, l) for name in names]

    small = [{name: w[name][l] for name, _ in SMALL if name != "b_ada"} for l in range(2)]
    n_mix, n_late = len(MIXER_WEIGHTS), len(LATE_WEIGHTS)

    first = comm_call(gather_job([c] + shards(MIXER_WEIGHTS, 0)), "gather_first")
    c_all = first[0].reshape(N_DEV, D_MODEL)
    kws = [mixer_operands(dict(zip(MIXER_WEIGHTS, first[1:])), small[0]), None]

    b_cols = lax.dynamic_slice_in_dim(b_ada, me * 768, 768, axis=1)
    mod_cols = ada_fwd(c_all, w_ada, b_cols)
    (mod_all,) = comm_call(gather_job([mod_cols]), "gather_mod")
    mod_me = lax.dynamic_index_in_dim(mod_all, me, axis=2, keepdims=False)
    mods = [mod_me[:, l, :].reshape(6, D_MODEL) for l in range(2)]

    inv_freq = 1.0 / (ROPE_THETA ** (jnp.arange(0, ROPE, 2, dtype=f32) / ROPE))
    inv = _pad_lanes(jnp.concatenate([inv_freq, inv_freq]), NOPE)
    cos_t, sin_t = rope_tables(positions.reshape(seq, 1), inv)

    saved = [None, None]
    h, saved[0], kws[0], got = layer_fwd(
        x[0], mods[0], kws[0], cos_t, sin_t, gather_job(shards(LATE_WEIGHTS, 0) + shards(MIXER_WEIGHTS, 1)),
        lambda got: late_operands(dict(zip(LATE_WEIGHTS, got[:n_late])), small[0]))
    kws[1] = mixer_operands(dict(zip(MIXER_WEIGHTS, got[n_late:])), small[1])
    h, saved[1], kws[1], _ = layer_fwd(
        h, mods[1], kws[1], cos_t, sin_t, gather_job(shards(LATE_WEIGHTS, 1)),
        lambda got: late_operands(dict(zip(LATE_WEIGHTS, got)), small[1]))
    dy, loss_part = loss_fwd(h, loss_target[0])

    early, late = ("w_out", "w_gate_up", "w_down"), ("w_in", "w_q_up", "w_kv_up", "conv_w")
    parts = [{}, {}]
    head, pieces, _ = layer_bwd_head(dy, mods[1], kws[1], saved[1])
    dy, dmod1, grads1, got = layer_bwd_tail(head, mods[1], kws[1], cos_t, sin_t, saved[1], scatter_job([pieces[n] for n in early]))
    parts[1].update(zip(early, got))
    head, pieces, got = layer_bwd_head(dy, mods[0], kws[0], saved[0], scatter_job([grads1[n] for n in late]))
    parts[1].update(zip(late, got))
    dy, dmod0, grads0, got = layer_bwd_tail(head, mods[0], kws[0], cos_t, sin_t, saved[0], scatter_job([pieces[n] for n in early]))
    parts[0].update(zip(early, got))
    parts[0].update(zip(late, comm_call(scatter_job([grads0[n] for n in late]), "scatter_layer0_rest")))
    grad_x = dy[None]

    small_part = {name: jnp.stack([grads0[name], grads1[name]]) for name, _ in SMALL if name != "b_ada"}
    small_part["b_ada"] = jnp.stack([dmod0.reshape(-1), dmod1.reshape(-1)])
    (small_all,) = comm_call(gather_job([_pack_small(lambda n: small_part[n], loss_part[0, 0])]), "gather_small_grads")
    packed = adamw(small_all, _pack_small(lambda n: w[n])[None], _pack_small(lambda n: m[n])[None],
                   _pack_small(lambda n: v[n])[None], 0, None, "adamw_small")
    loss = packed[0][0, -1, -1]
    res = {}
    for key, arr in zip("gdmv", packed):
        for name, val in _unpack_small(arr[0]).items():
            res[key, name] = val

    off = 2 * (1024 + 1024)
    dmod_all = small_all.reshape(N_DEV, -1)[:, off:off + 2 * 6144].reshape(N_DEV, 2, 6144)
    dmod_cols = lax.dynamic_slice_in_dim(dmod_all, me * 768, 768, axis=2).transpose(1, 0, 2)
    g_ada = ada_bwd(c_all, dmod_cols)
    out = None
    for l in range(2):
        out = adamw(g_ada[l][None], w_ada, m_w_ada, v_w_ada, l, out, "adamw_w_ada")
    res.update(zip([(key, "w_ada") for key in "gdmv"], out))

    for name in BIG:
        view = (lambda a: jnp.swapaxes(a, 1, 2)) if name in TRANSPOSED else (lambda a: a)
        out = None
        for l in range(2):
            out = adamw(parts[l][name], view(w[name]), view(m[name]), view(v[name]), l, out, "adamw_" + name)
        res.update(zip([(key, name) for key in "gdmv"], [view(a) for a in out]))

    return (loss, grad_x, *[res["g", n] for n in WEIGHTS], *[res["d", n] for n in WEIGHTS],
            *[res["m", n] for n in WEIGHTS], *[res["v", n] for n in WEIGHTS])
```

```python
import functools

import jax
import jax.numpy as jnp
from jax import lax
from jax.experimental import pallas as pl
from jax.experimental.pallas import tpu as pltpu

f32 = jnp.float32
bf16 = jnp.bfloat16

N_DEV = 8
D_MODEL = 1024
N_HEADS = 8
HEAD_LANES = 128
NOPE = 64
ROPE = 32
V_DIM = 64
Q_RANK = 256
KV_RANK = 128
D_SSD = 512
D_CONV = 1024
SSD_STATE = 128
SSD_HEAD_DIM = 64
CHUNK = 128
HALO = 8
D_FF = 2816
FF_SHARD = 704
D_IN = 1960
D_PROJ = 2048
EPS = 1e-6
LOG2E = 1.4426950408889634
LN2 = 0.6931471805599453
Q_SCALE = (NOPE + ROPE) ** -0.5 * LOG2E
SPARE_Q = NOPE + ROPE
SPARE_V = V_DIM
ATTN_ROWS_FWD = 256
ATTN_HEADS_FWD = 4
ATTN_HEADS_BWD = 2
MLP_BWD_ROWS = 512
ROPE_THETA = 10000.0
NEG = -1e30

ADAM_LR = 0.001
ADAM_B1 = 0.9
ADAM_B2 = 0.999
ADAM_EPS = 1e-08
ADAM_WD = 0.01
ADAM_STEP = 10
ADAMW_BLOCK_BYTES = 24 << 20

MESH = pl.DeviceIdType.MESH
ANY = pl.BlockSpec(memory_space=pl.ANY)

SMALL = (("norm1_w", 1024), ("norm2_w", 1024), ("b_ada", 6144), ("q_a_norm_w", 256), ("kv_a_norm_w", 128),
         ("q_nope_norm_w", 64), ("q_pe_norm_w", 32), ("k_nope_norm_w", 64), ("k_pe_norm_w", 32),
         ("conv_b", 1024), ("dt_bias", 8), ("a_log", 8), ("d_skip", 8), ("ssd_norm_w", 512))
SMALL_ROWS = 168
BIG = ("w_in", "w_q_up", "w_kv_up", "conv_w", "w_out", "w_gate_up", "w_down")
TRANSPOSED = ("w_in", "w_gate_up")
WEIGHTS = ("norm1_w", "norm2_w", "w_ada", "b_ada", "w_in", "q_a_norm_w", "w_q_up", "kv_a_norm_w", "w_kv_up",
           "q_nope_norm_w", "q_pe_norm_w", "k_nope_norm_w", "k_pe_norm_w", "conv_w", "conv_b", "dt_bias",
           "a_log", "d_skip", "ssd_norm_w", "w_out", "w_gate_up", "w_down")


def _dot(a, b, ca, cb):
    return lax.dot_general(a.astype(bf16), b.astype(bf16), (((ca,), (cb,)), ((), ())), preferred_element_type=f32)


@jax.custom_vjp
def mm(a, b):
    return _dot(a, b, 1, 0)


def _mm_fwd(a, b):
    return _dot(a, b, 1, 0), (a, b)


def _mm_bwd(res, g):
    a, b = res
    return _dot(g, b, 1, 1).astype(a.dtype), _dot(a, g, 0, 0).astype(b.dtype)


mm.defvjp(_mm_fwd, _mm_bwd)


@jax.custom_vjp
def _mm_slot(a, w, slot):
    return _dot(a, w, 1, 0)


def _mm_slot_fwd(a, w, slot):
    return _dot(a, w, 1, 0), (a, w)


def _mm_slot_bwd(res, g):
    a, w = res
    return _dot(g, w, 1, 1).astype(a.dtype), None, _dot(a, g, 0, 0)


_mm_slot.defvjp(_mm_slot_fwd, _mm_slot_bwd)


def mmw(a, w, slot=None):
    return _dot(a, w, 1, 0) if slot is None else _mm_slot(a, w, slot)


@jax.custom_vjp
def _mm_slot_t(a, wt, slot):
    return _dot(a, wt, 1, 1)


def _mm_slot_t_fwd(a, wt, slot):
    return _dot(a, wt, 1, 1), (a, wt)


def _mm_slot_t_bwd(res, g):
    a, wt = res
    return _dot(g, wt, 1, 0).astype(a.dtype), None, _dot(g, a, 0, 0)


_mm_slot_t.defvjp(_mm_slot_t_fwd, _mm_slot_t_bwd)


def mmw_t(a, wt, slot=None):
    return _dot(a, wt, 1, 1) if slot is None else _mm_slot_t(a, wt, slot)


@jax.custom_vjp
def mm_nt(a, b):
    return _dot(a, b, 1, 1)


def _mm_nt_fwd(a, b):
    return _dot(a, b, 1, 1), (a, b)


def _mm_nt_bwd(res, g):
    a, b = res
    return _dot(g, b, 1, 0).astype(a.dtype), _dot(g, a, 0, 0).astype(b.dtype)


mm_nt.defvjp(_mm_nt_fwd, _mm_nt_bwd)


@jax.custom_vjp
def mm_tn(a, b):
    return _dot(a, b, 0, 0)


def _mm_tn_fwd(a, b):
    return _dot(a, b, 0, 0), (a, b)


def _mm_tn_bwd(res, g):
    a, b = res
    return _dot(b, g, 1, 1).astype(a.dtype), _dot(a, g, 1, 0).astype(b.dtype)


mm_tn.defvjp(_mm_tn_fwd, _mm_tn_bwd)


def _rms(x, w):
    return x * lax.rsqrt(jnp.mean(x * x, axis=-1, keepdims=True) + EPS) * w


def _const(shape):
    n = len(shape)
    return pl.BlockSpec(shape, lambda *_: (0,) * n)


def _accumulate(first, refs, vals):
    @pl.when(first)
    def _():
        for r, v in zip(refs, vals):
            r[...] = v

    @pl.when(jnp.logical_not(first))
    def _():
        for r, v in zip(refs, vals):
            r[...] += v


def _accumulate_then_cast(first, last, accs, outs, vals):
    _accumulate(first, accs, vals)

    @pl.when(last)
    def _():
        for a, o in zip(accs, outs):
            o[...] = a[...].astype(o.dtype)


def _token_block(s):
    return min(512, s)


def _f_proj(x, nw, sh, sc, w, slot=None):
    h = _rms(x, nw) * (1.0 + sc) + sh
    return mmw_t(h, w, slot)


def _f_qkv(pa, plast, cos_t, sin_t, qaw, kvaw, wq, wk, wv, qnw, knw, kpw, slots=None):
    sq, sk, sv = slots if slots is not None else ([None] * N_HEADS,) * 3
    lane = lax.broadcasted_iota(jnp.int32, (1, HEAD_LANES), 1)
    m_nope = lane < NOPE
    m_pe = (lane >= NOPE) & (lane < NOPE + ROPE)
    rows = pa.shape[0]

    def rope(t):
        half = ROPE // 2
        swapped = jnp.concatenate(
            [jnp.zeros((rows, NOPE), f32), t[:, NOPE + half:NOPE + ROPE], t[:, NOPE:NOPE + half],
             jnp.zeros((rows, HEAD_LANES - NOPE - ROPE), f32)], axis=1)
        return t * cos_t + swapped * sin_t

    qa = _rms(pa[:, :Q_RANK], qaw)
    kva = _rms(pa[:, Q_RANK:Q_RANK + KV_RANK], kvaw)
    kp = jnp.where(m_pe, plast, 0.0)
    kp = kp * lax.rsqrt(jnp.sum(kp * kp, axis=-1, keepdims=True) / ROPE + EPS) * kpw
    k_rot = rope(kp)
    qs, ks, vs = [], [], []
    for h in range(N_HEADS):
        qh = mmw(qa, wq[h], sq[h])
        ss_n = jnp.sum(jnp.where(m_nope, qh * qh, 0.0), axis=-1, keepdims=True) / NOPE
        ss_p = jnp.sum(jnp.where(m_pe, qh * qh, 0.0), axis=-1, keepdims=True) / ROPE
        r = jnp.where(m_nope, lax.rsqrt(ss_n + EPS), lax.rsqrt(ss_p + EPS))
        qs.append(rope(qh * r * qnw) * Q_SCALE)
        kh = mmw(kva, wk[h], sk[h])
        kh = kh * lax.rsqrt(jnp.sum(kh * kh, axis=-1, keepdims=True) / NOPE + EPS) * knw
        ks.append(kh + k_rot)
        vs.append(mmw(kva, wv[h], sv[h]))
    return jnp.stack(qs), jnp.stack(ks), jnp.stack(vs)


def _f_ssd(xext, z, plast, prev, cw, cb, dtb, alog, dskip, snw):
    n = CHUNK
    conv = cb
    for k in range(4):
        conv = conv + cw[k:k + 1] * xext[HALO - 3 + k:HALO - 3 + k + n]
    xc = jax.nn.silu(conv)
    xs, bm, cm = xc[:, :D_SSD], xc[:, D_SSD:D_SSD + 2 * SSD_STATE], xc[:, D_SSD + 2 * SSD_STATE:]
    lane = lax.broadcasted_iota(jnp.int32, (1, 128), 1)
    dt = jax.nn.softplus(jnp.where(lane < N_HEADS, plast, 0.0) + dtb)
    adt = dt * (-jnp.exp(alog))
    row = lax.broadcasted_iota(jnp.int32, (n, n), 0)
    col = lax.broadcasted_iota(jnp.int32, (n, n), 1)
    tri = row >= col
    acs = jnp.dot(tri.astype(f32), adt, precision=lax.Precision.HIGHEST, preferred_element_type=f32)
    acs_t = acs.T
    bgs = [bm[:, g * SSD_STATE:(g + 1) * SSD_STATE] for g in range(2)]
    cgs = [cm[:, g * SSD_STATE:(g + 1) * SSD_STATE] for g in range(2)]
    cb_ts = [mm_nt(cgs[g], bgs[g]) for g in range(2)]
    low = lane < SSD_HEAD_DIM
    low_rows = lax.broadcasted_iota(jnp.int32, (2 * SSD_HEAD_DIM, 1), 0) < SSD_HEAD_DIM

    def both(a0, a1):
        return jnp.where(low, a0, a1)

    pre = []
    for i in range(N_HEADS // 2):
        h0, h1 = 2 * i, 2 * i + 1
        col0, col1 = acs[:, h0:h0 + 1], acs[:, h1:h1 + 1]
        last0, last1 = acs[n - 1:n, h0:h0 + 1], acs[n - 1:n, h1:h1 + 1]
        cb_t = cb_ts[i // 2]
        scores0 = cb_t * jnp.exp(jnp.where(tri, col0 - acs_t[h0:h0 + 1, :], -jnp.inf))
        scores1 = cb_t * jnp.exp(jnp.where(tri, col1 - acs_t[h1:h1 + 1, :], -jnp.inf))
        xp = xs[:, i * 128:(i + 1) * 128]
        xdt = xp * both(dt[:, h0:h0 + 1], dt[:, h1:h1 + 1])
        weighted = xdt * both(jnp.exp(last0 - col0), jnp.exp(last1 - col1))
        chunk_decay = jnp.where(low_rows, jnp.exp(last0), jnp.exp(last1))
        in_decay = both(jnp.exp(col0), jnp.exp(col1))
        skip = both(dskip[:, h0:h0 + 1], dskip[:, h1:h1 + 1]) * xp
        pre.append((scores0, scores1, xdt, weighted, chunk_decay, in_decay, skip))
    prods = []
    for i in range(N_HEADS // 2):
        scores0, scores1, xdt, weighted, _, _, _ = pre[i]
        g = i // 2
        y_diag = mm(scores0, jnp.where(low, xdt, 0.0)) + mm(scores1, jnp.where(low, 0.0, xdt))
        prods.append((y_diag, mm_tn(weighted, bgs[g]), mm_nt(cgs[g], prev[i])))
    ys, news = [], []
    for i in range(N_HEADS // 2):
        y_diag, st, y_off = prods[i]
        _, _, _, _, chunk_decay, in_decay, skip = pre[i]
        news.append(chunk_decay * prev[i] + st)
        ys.append(y_diag + y_off * in_decay + skip)
    y = jnp.concatenate(ys, axis=1)
    yg = y * jax.nn.silu(z)
    half = D_SSD // 2
    outs = []
    for g in range(2):
        t = yg[:, g * half:(g + 1) * half]
        outs.append(t * lax.rsqrt(jnp.mean(t * t, axis=-1, keepdims=True) + EPS))
    return jnp.concatenate(outs, axis=1) * snw, jnp.stack(news)


def _f_out(o, yg, g1, wo, slot=None):
    cat = jnp.concatenate([o[h] for h in range(N_HEADS)] + [yg], axis=1)
    return g1 * mmw(cat, wo, slot)


def _f_modulate(x, nw, sh, sc):
    return _rms(x, nw) * (1.0 + sc) + sh


def _f_gate_up(h, wg, wu, slot_g=None, slot_u=None):
    return jax.nn.silu(mmw_t(h, wg, slot_g)) * mmw_t(h, wu, slot_u)


def proj_fwd(x, nw, sh, sc, w):
    s = x.shape[0]
    ts = _token_block(s)

    def body(x_ref, nw_ref, sh_ref, sc_ref, w_ref, pa_ref, pz_ref, px_ref, pl_ref):
        p = _f_proj(x_ref[...], nw_ref[...], sh_ref[...], sc_ref[...], w_ref[...])
        pa_ref[...] = p[:, :384]
        pz_ref[...] = p[:, 384:896]
        px_ref[...] = p[:, 896:1920]
        pl_ref[...] = p[:, 1920:]

    vec = _const((1, D_MODEL))
    return pl.pallas_call(
        body, name="proj_fwd", grid=(s // ts,),
        in_specs=[pl.BlockSpec((ts, D_MODEL), lambda i: (i, 0)), vec, vec, vec, _const((D_PROJ, D_MODEL))],
        out_specs=[pl.BlockSpec((ts, 384), lambda i: (i, 0)), pl.BlockSpec((ts, 512), lambda i: (i, 0)),
                   pl.BlockSpec((ts, 1024), lambda i: (i, 0)), pl.BlockSpec((ts, 128), lambda i: (i, 0))],
        out_shape=[jax.ShapeDtypeStruct((s, 384), f32), jax.ShapeDtypeStruct((s, 512), f32),
                   jax.ShapeDtypeStruct((s, 1024), f32), jax.ShapeDtypeStruct((s, 128), f32)],
    )(x, nw, sh, sc, w)


def rope_tables(pos, inv):
    s = pos.shape[0]
    ts = _token_block(s)

    def body(pos_ref, inv_ref, cos_ref, sin_ref):
        ang = pos_ref[...].astype(f32) * inv_ref[...]
        lane = lax.broadcasted_iota(jnp.int32, (1, HEAD_LANES), 1)
        half = ROPE // 2
        cos_ref[...] = jnp.where(lane < NOPE, 1.0, jnp.where(lane < NOPE + ROPE, jnp.cos(ang), 0.0))
        sn = jnp.sin(ang)
        sin_ref[...] = jnp.where((lane >= NOPE) & (lane < NOPE + half), -sn,
                                 jnp.where((lane >= NOPE + half) & (lane < NOPE + ROPE), sn, 0.0))

    return pl.pallas_call(
        body, name="rope_tables", grid=(s // ts,),
        in_specs=[pl.BlockSpec((ts, 1), lambda i: (i, 0)), _const((1, HEAD_LANES))],
        out_specs=[pl.BlockSpec((ts, HEAD_LANES), lambda i: (i, 0))] * 2,
        out_shape=[jax.ShapeDtypeStruct((s, HEAD_LANES), f32)] * 2,
    )(pos, inv)


def _qkv_param_specs():
    return [_const((1, Q_RANK)), _const((1, KV_RANK)), _const((N_HEADS, Q_RANK, HEAD_LANES)),
            _const((N_HEADS, KV_RANK, HEAD_LANES)), _const((N_HEADS, KV_RANK, V_DIM)),
            _const((1, HEAD_LANES)), _const((1, HEAD_LANES)), _const((1, HEAD_LANES))]


def qkv_fwd(pa, plast, cos_t, sin_t, params):
    s = pa.shape[0]
    ts = _token_block(s)

    def body(pa_ref, pl_ref, cos_ref, sin_ref, *rest):
        prm = [r[...] for r in rest[:8]]
        q_ref, k_ref, v_ref = rest[8:]
        q, k, v = _f_qkv(pa_ref[...], pl_ref[...], cos_ref[...], sin_ref[...], *prm)
        q_ref[...] = q.astype(bf16)
        lane = lax.broadcasted_iota(jnp.int32, (1, 1, HEAD_LANES), 2)
        k_ref[...] = jnp.where((lane == SPARE_Q) | (lane == SPARE_Q + 1), 1.0, k).astype(bf16)
        v_ref[...] = jnp.concatenate([v, jnp.ones_like(v)], axis=-1).astype(bf16)

    tok = lambda w: pl.BlockSpec((ts, w), lambda i: (i, 0))
    head = pl.BlockSpec((N_HEADS, ts, HEAD_LANES), lambda i: (0, i, 0))
    return pl.pallas_call(
        body, name="qkv_fwd", grid=(s // ts,),
        in_specs=[tok(384), tok(128), tok(128), tok(128)] + _qkv_param_specs(),
        out_specs=[head] * 3, out_shape=[jax.ShapeDtypeStruct((N_HEADS, s, HEAD_LANES), bf16)] * 3,
    )(pa, plast, cos_t, sin_t, *params)


def _scores(q, k):
    return lax.dot_general(q, k, (((1,), (1,)), ((), ())), preferred_element_type=f32)


def _tril(rows, cols, row_offset):
    row = row_offset + lax.broadcasted_iota(jnp.int32, (rows, cols), 0)
    col = lax.broadcasted_iota(jnp.int32, (rows, cols), 1)
    return row >= col


def _call_with_job(body, name, grid, job, in_specs, out_specs, out_shape, scratch_shapes, operands, relay_at=None):
    if job is None:
        res = pl.pallas_call(body, name=name, grid=grid, in_specs=in_specs, out_specs=out_specs, out_shape=out_shape,
                             scratch_shapes=scratch_shapes)(*operands)
        return res, None

    def at_step(i, n):
        if i == 0:
            want = [0] * len(grid)
        elif i == n - 1:
            want = [g - 1 for g in grid]
        else:
            want = relay_at
        return functools.reduce(jnp.logical_and, [pl.program_id(a) == s for a, s in enumerate(want)])

    carrier = _carry(job, body, len(in_specs), len(out_specs), at_step)
    res = pl.pallas_call(
        carrier, name=name, grid=grid,
        in_specs=list(in_specs) + [ANY] * len(job.operands), out_specs=list(out_specs) + [ANY] * len(job.out_shape),
        out_shape=list(out_shape) + list(job.out_shape), scratch_shapes=list(scratch_shapes) + job.scratch,
    )(*operands, *job.operands)
    return res[:len(out_specs)], res[len(out_specs):]


def attn_fwd(q, k, v, job=None):
    s = q.shape[1]
    t = _token_block(s)
    nb = s // t

    rb = min(ATTN_ROWS_FWD, t)

    hp = ATTN_HEADS_FWD

    def body(q_ref, k_ref, v_ref, o_ref, qx_ref, m_sc, acc_sc):
        qi = pl.program_id(1)
        m_sc[...] = jnp.full(m_sc.shape, NEG, f32)
        acc_sc[...] = jnp.zeros(acc_sc.shape, f32)

        def step(k0, diagonal):
            chains = [(hh, r) for hh in range(hp) for r in range(t // rb)]

            def scores(hh, r):
                nk = (r + 1) * rb if diagonal else t
                sc = _scores(q_ref[hh, pl.ds(r * rb, rb), :], k_ref[hh, pl.ds(k0, nk), :])
                return jnp.where(_tril(rb, nk, r * rb), sc, NEG) if diagonal else sc

            ahead = scores(*chains[0])
            for c, (hh, r) in enumerate(chains):
                sc = ahead
                if c + 1 < len(chains):
                    ahead = scores(*chains[c + 1])
                rows = pl.ds(r * rb, rb)
                keys = pl.ds(k0, (r + 1) * rb if diagonal else t)
                m_prev = m_sc[hh, rows, :1]
                m_new = jnp.maximum(m_prev, jnp.max(sc, axis=-1, keepdims=True))
                p = jnp.exp2(sc - m_new)
                alpha = jnp.exp2(m_prev - m_new)
                acc = alpha * acc_sc[hh, rows, :] + jnp.dot(p.astype(bf16), v_ref[hh, keys, :], preferred_element_type=f32)
                if diagonal:
                    l = acc[:, V_DIM:V_DIM + 1]
                    o_ref[hh, rows, :] = acc[:, :V_DIM] / l
                    lse = m_new + jnp.log2(l)
                    high = lse.astype(bf16)
                    low = (lse - high.astype(f32)).astype(bf16)
                    lane = lax.broadcasted_iota(jnp.int32, (1, HEAD_LANES), 1)
                    qx_ref[hh, rows, :] = jnp.where(lane == SPARE_Q, -high,
                                                    jnp.where(lane == SPARE_Q + 1, -low, q_ref[hh, rows, :]))
                else:
                    acc_sc[hh, rows, :] = acc
                    m_sc[hh, rows, :] = jnp.broadcast_to(m_new, (rb, 128))

        def below(ki, carry):
            step(pl.multiple_of(ki * t, t), False)
            return carry

        lax.fori_loop(0, qi, below, 0)
        step(pl.multiple_of(qi * t, t), True)

    return _call_with_job(
        body, "attn_fwd" if job is None else "attn_fwd_comm", (N_HEADS // hp, nb), job,
        in_specs=[pl.BlockSpec((hp, t, HEAD_LANES), lambda h, qi: (h, qi, 0)),
                  pl.BlockSpec((hp, s, HEAD_LANES), lambda h, qi: (h, 0, 0)),
                  pl.BlockSpec((hp, s, HEAD_LANES), lambda h, qi: (h, 0, 0))],
        out_specs=[pl.BlockSpec((hp, t, V_DIM), lambda h, qi: (h, qi, 0)),
                   pl.BlockSpec((hp, t, HEAD_LANES), lambda h, qi: (h, qi, 0))],
        out_shape=[jax.ShapeDtypeStruct((N_HEADS, s, V_DIM), f32), jax.ShapeDtypeStruct((N_HEADS, s, HEAD_LANES), bf16)],
        scratch_shapes=[pltpu.VMEM((hp, t, 128), f32), pltpu.VMEM((hp, t, HEAD_LANES), f32)],
        operands=(q, k, v), relay_at=(N_HEADS // hp - 1, max(nb - 2, 0)))


def _ssd_param_specs():
    return [_const((4, D_CONV)), _const((1, D_CONV)), _const((1, 128)), _const((1, 128)), _const((1, 128)),
            _const((1, D_SSD))]


def ssd_fwd(px, pz, plast, params):
    s = px.shape[0]
    nc = s // CHUNK

    def body(px_ref, pz_ref, pl_ref, cw_ref, cb_ref, dtb_ref, alog_ref, dskip_ref, snw_ref, yg_ref, st_ref,
             state_sc, halo_sc):
        i = pl.program_id(0)

        @pl.when(i == 0)
        def _():
            state_sc[...] = jnp.zeros(state_sc.shape, f32)
            halo_sc[...] = jnp.zeros(halo_sc.shape, f32)

        x = px_ref[...]
        prev = state_sc[...]
        st_ref[...] = prev
        xext = jnp.concatenate([halo_sc[...], x], axis=0)
        yg, new = _f_ssd(xext, pz_ref[...], pl_ref[...], prev, cw_ref[...], cb_ref[...], dtb_ref[...],
                         alog_ref[...], dskip_ref[...], snw_ref[...])
        yg_ref[...] = yg
        state_sc[...] = new
        halo_sc[...] = x[CHUNK - HALO:]

    tok = lambda w: pl.BlockSpec((CHUNK, w), lambda i: (i, 0))
    return pl.pallas_call(
        body, name="ssd_fwd", grid=(nc,),
        in_specs=[tok(D_CONV), tok(D_SSD), tok(128)] + _ssd_param_specs(),
        out_specs=[tok(D_SSD), pl.BlockSpec((None, N_HEADS // 2, 2 * SSD_HEAD_DIM, SSD_STATE), lambda i: (i, 0, 0, 0))],
        out_shape=[jax.ShapeDtypeStruct((s, D_SSD), f32),
                   jax.ShapeDtypeStruct((nc, N_HEADS // 2, 2 * SSD_HEAD_DIM, SSD_STATE), f32)],
        scratch_shapes=[pltpu.VMEM((N_HEADS // 2, 2 * SSD_HEAD_DIM, SSD_STATE), f32), pltpu.VMEM((HALO, D_CONV), f32)],
    )(px, pz, plast, *params)


def out_fwd(x, o, yg, g1, wo):
    s = x.shape[0]
    ts = _token_block(s)

    def body(x_ref, o_ref, yg_ref, g1_ref, wo_ref, out_ref):
        out_ref[...] = x_ref[...] + _f_out(o_ref[...], yg_ref[...], g1_ref[...], wo_ref[...])

    return pl.pallas_call(
        body, name="out_fwd", grid=(s // ts,),
        in_specs=[pl.BlockSpec((ts, D_MODEL), lambda i: (i, 0)), pl.BlockSpec((N_HEADS, ts, V_DIM), lambda i: (0, i, 0)),
                  pl.BlockSpec((ts, D_SSD), lambda i: (i, 0)), _const((1, D_MODEL)), _const((D_MODEL, D_MODEL))],
        out_specs=pl.BlockSpec((ts, D_MODEL), lambda i: (i, 0)),
        out_shape=jax.ShapeDtypeStruct((s, D_MODEL), f32),
    )(x, o, yg, g1, wo)


def mlp_fwd(x, nw, sh, sc, g2, wgu, wd):
    s = x.shape[0]
    ts = _token_block(s)
    nj = N_DEV // 2

    def body(x_ref, nw_ref, sh_ref, sc_ref, g2_ref, wg_ref, wu_ref, wd_ref, out_ref, mix_ref, h_ref):
        j = pl.program_id(1)

        @pl.when(j == 0)
        def _():
            h_ref[...] = _f_modulate(x_ref[...], nw_ref[...], sh_ref[...], sc_ref[...]).astype(bf16)

        act = _f_gate_up(h_ref[...], wg_ref[...], wu_ref[...])
        _accumulate(j == 0, [mix_ref], [mmw(act, wd_ref[...])])

        @pl.when(j == nj - 1)
        def _():
            out_ref[...] = x_ref[...] + g2_ref[...] * mix_ref[...]

    vec = _const((1, D_MODEL))
    tok = pl.BlockSpec((ts, D_MODEL), lambda i, j: (i, 0))
    return pl.pallas_call(
        body, name="mlp_fwd", grid=(s // ts, nj),
        in_specs=[tok, vec, vec, vec, vec,
                  pl.BlockSpec((None, FF_SHARD, D_MODEL), lambda i, j: (j, 0, 0)),
                  pl.BlockSpec((None, FF_SHARD, D_MODEL), lambda i, j: (j + nj, 0, 0)),
                  pl.BlockSpec((None, FF_SHARD, D_MODEL), lambda i, j: (j, 0, 0))],
        out_specs=[tok] * 3,
        out_shape=[jax.ShapeDtypeStruct((s, D_MODEL), f32), jax.ShapeDtypeStruct((s, D_MODEL), f32),
                   jax.ShapeDtypeStruct((s, D_MODEL), bf16)],
    )(x, nw, sh, sc, g2, wgu, wgu, wd)


def loss_fwd(y, target):
    s = y.shape[0]
    ts = _token_block(s)

    def body(y_ref, t_ref, dy_ref, loss_ref):
        d = y_ref[...] - t_ref[...]
        dy_ref[...] = d * (1.0 / D_MODEL)
        part = 0.5 * jnp.sum(jnp.sum(d * d, axis=-1, keepdims=True) * (1.0 / D_MODEL), axis=0, keepdims=True)
        _accumulate(pl.program_id(0) == 0, [loss_ref], [jnp.broadcast_to(part, (8, 128))])

    return pl.pallas_call(
        body, name="loss_fwd", grid=(s // ts,),
        in_specs=[pl.BlockSpec((ts, D_MODEL), lambda i: (i, 0))] * 2,
        out_specs=[pl.BlockSpec((ts, D_MODEL), lambda i: (i, 0)), _const((8, 128))],
        out_shape=[jax.ShapeDtypeStruct((s, D_MODEL), f32), jax.ShapeDtypeStruct((8, 128), f32)],
    )(y, target)


def mlp_bwd(h, dy, g2, wgu, wd, job=None):
    s = h.shape[0]
    ts = min(MLP_BWD_ROWS, s)
    nj = N_DEV // 2
    ni = s // ts

    def body(h_ref, dy_ref, g2_ref, wg_ref, wu_ref, wd_ref, dh_ref, dwg_ref, dwu_ref, dwd_ref, ag_sc, au_sc, ad_sc):
        i = pl.program_id(1)
        wg, wu, wd = wg_ref[...], wu_ref[...], wd_ref[...]
        act, vjp = jax.vjp(lambda h_, sg, su: _f_gate_up(h_, wg, wu, sg, su), h_ref[...].astype(f32),
                           jnp.zeros(wg.shape, f32), jnp.zeros(wu.shape, f32))
        dmix = dy_ref[...] * g2_ref[...]
        dact = _dot(dmix, wd, 1, 1)
        dwd = _dot(act, dmix, 0, 0)
        dh, dwg, dwu = vjp(dact)
        dh_ref[...] = dh.astype(bf16)
        _accumulate_then_cast(i == 0, i == ni - 1, [ag_sc, au_sc, ad_sc], [dwg_ref, dwu_ref, dwd_ref], [dwg, dwu, dwd])

    once = pl.Buffered(1)
    wspec = lambda off: pl.BlockSpec((None, FF_SHARD, D_MODEL), lambda j, i: (j + off, 0, 0), pipeline_mode=once)
    dspec = pl.BlockSpec((None, FF_SHARD, D_MODEL), lambda j, i: (j, 0, 0), pipeline_mode=once)
    return _call_with_job(
        body, "mlp_bwd" if job is None else "mlp_bwd_comm", (nj, ni), job,
        in_specs=[pl.BlockSpec((ts, D_MODEL), lambda j, i: (i, 0)), pl.BlockSpec((ts, D_MODEL), lambda j, i: (i, 0)),
                  _const((1, D_MODEL)), wspec(0), wspec(nj), dspec],
        out_specs=[pl.BlockSpec((None, ts, D_MODEL), lambda j, i: (j, i, 0)), wspec(0), wspec(0), dspec],
        out_shape=[jax.ShapeDtypeStruct((nj, s, D_MODEL), bf16),
                   jax.ShapeDtypeStruct((nj, FF_SHARD, D_MODEL), bf16), jax.ShapeDtypeStruct((nj, FF_SHARD, D_MODEL), bf16),
                   jax.ShapeDtypeStruct((nj, FF_SHARD, D_MODEL), bf16)],
        scratch_shapes=[pltpu.VMEM((FF_SHARD, D_MODEL), f32), pltpu.VMEM((FF_SHARD, D_MODEL), f32),
                        pltpu.VMEM((FF_SHARD, D_MODEL), f32)],
        operands=(h, dy, g2, wgu, wgu, wd))


def out_bwd(dy, dhparts, x, nw, sh, sc, mix, o, yg, g1, wo):
    s = dy.shape[0]
    ts = _token_block(s)
    nj = dhparts.shape[0]

    ni = s // ts

    def body(dy_ref, dp_ref, x_ref, nw_ref, sh_ref, sc_ref, mix_ref, o_ref, yg_ref, g1_ref, wo_ref,
             dx_ref, dnw_ref, dsh_ref, dsc_ref, do_ref, dyg_ref, dg1_ref, dg2_ref, dwo_ref, acc_sc):
        i = pl.program_id(0)
        g = dy_ref[...]
        _accumulate(i == 0, [dg2_ref], [jnp.sum(g * mix_ref[...], axis=0, keepdims=True)])
        dh = dp_ref[0].astype(f32)
        for j in range(1, nj):
            dh = dh + dp_ref[j].astype(f32)
        _, vjp_mod = jax.vjp(_f_modulate, x_ref[...], nw_ref[...], sh_ref[...], sc_ref[...])
        dx_mod, dnw, dsh, dsc = vjp_mod(dh)
        _accumulate(i == 0, [dnw_ref, dsh_ref, dsc_ref], [dnw, dsh, dsc])
        g = g + dx_mod
        dx_ref[...] = g
        o = o_ref[...]
        wo = wo_ref[...]
        _, vjp = jax.vjp(lambda o_, yg_, g1_, slot: _f_out(o_, yg_, g1_, wo, slot), o, yg_ref[...], g1_ref[...],
                         jnp.zeros(wo.shape, f32))
        do, dyg, dg1, dwo = vjp(g)
        delta = jnp.sum(do * o, axis=-1, keepdims=True)
        high = delta.astype(bf16)
        low = (delta - high.astype(f32)).astype(bf16)
        lane = lax.broadcasted_iota(jnp.int32, (1, 1, HEAD_LANES), 2)
        wide = jnp.concatenate([do.astype(bf16), jnp.zeros(do.shape, bf16)], axis=-1)
        do_ref[...] = jnp.where(lane == SPARE_V, -high, jnp.where(lane == SPARE_V + 1, -low, wide))
        dyg_ref[...] = dyg
        _accumulate(i == 0, [dg1_ref], [dg1])
        _accumulate_then_cast(i == 0, i == ni - 1, [acc_sc], [dwo_ref], [dwo])

    head = pl.BlockSpec((N_HEADS, ts, V_DIM), lambda i: (0, i, 0))
    tok = pl.BlockSpec((ts, D_MODEL), lambda i: (i, 0))
    vec = _const((1, D_MODEL))
    vshape = jax.ShapeDtypeStruct((1, D_MODEL), f32)
    return pl.pallas_call(
        body, name="out_bwd", grid=(ni,), scratch_shapes=[pltpu.VMEM((D_MODEL, D_MODEL), f32)],
        in_specs=[tok, pl.BlockSpec((nj, ts, D_MODEL), lambda i: (0, i, 0)), tok, vec, vec, vec, tok,
                  head, pl.BlockSpec((ts, D_SSD), lambda i: (i, 0)), vec, _const((D_MODEL, D_MODEL))],
        out_specs=[tok, vec, vec, vec, pl.BlockSpec((N_HEADS, ts, HEAD_LANES), lambda i: (0, i, 0)),
                   pl.BlockSpec((ts, D_SSD), lambda i: (i, 0)), vec, vec, _const((D_MODEL, D_MODEL))],
        out_shape=[jax.ShapeDtypeStruct((s, D_MODEL), f32), vshape, vshape, vshape,
                   jax.ShapeDtypeStruct((N_HEADS, s, HEAD_LANES), bf16), jax.ShapeDtypeStruct((s, D_SSD), f32),
                   vshape, vshape, jax.ShapeDtypeStruct((D_MODEL, D_MODEL), bf16)],
    )(dy, dhparts, x, nw, sh, sc, mix, o, yg, g1, wo)


def attn_bwd(qx, k, v, do, job=None):
    s = qx.shape[1]
    t = _token_block(s)
    nb = s // t

    hp = ATTN_HEADS_BWD

    def body(q_ref, k_ref, v_ref, do_ref, dq_ref, dk_ref, dv_ref, dv_sc):
        ki = pl.program_id(1)

        @pl.when(ki == 0)
        def _():
            dq_ref[...] = jnp.zeros(dq_ref.shape, f32)

        dk_ref[...] = jnp.zeros(dk_ref.shape, f32)
        dv_sc[...] = jnp.zeros(dv_sc.shape, f32)

        def step(q0, diagonal):
            rows = pl.ds(q0, t)

            def products(hh):
                sc = _scores(q_ref[hh, rows, :], k_ref[hh])
                dps = _scores(do_ref[hh, rows, :], v_ref[hh])
                return (jnp.where(_tril(t, t, 0), sc, NEG) if diagonal else sc), dps

            ahead = products(0)
            for hh in range(hp):
                sc, dps = ahead
                if hh + 1 < hp:
                    ahead = products(hh + 1)
                p = jnp.exp2(sc)
                ds = (p * dps).astype(bf16)
                dv_sc[hh] += lax.dot_general(p.astype(bf16), do_ref[hh, rows, :], (((0,), (0,)), ((), ())),
                                             preferred_element_type=f32)
                dk_ref[hh] += lax.dot_general(ds, q_ref[hh, rows, :], (((0,), (0,)), ((), ())), preferred_element_type=f32)
                dq_ref[hh, rows, :] += jnp.dot(ds, k_ref[hh], preferred_element_type=f32)

        step(pl.multiple_of(ki * t, t), True)

        def above(qi, carry):
            step(pl.multiple_of(qi * t, t), False)
            return carry

        lax.fori_loop(ki + 1, nb, above, 0)
        real = lax.broadcasted_iota(jnp.int32, (1, 1, HEAD_LANES), 2) < SPARE_Q
        dk_ref[...] = jnp.where(real, dk_ref[...] * LN2, 0.0)
        dv_ref[...] = dv_sc[:, :, :V_DIM]

        @pl.when(ki == nb - 1)
        def _():
            dq_ref[...] = jnp.where(real, dq_ref[...] * LN2, 0.0)

    qspec = pl.BlockSpec((hp, s, HEAD_LANES), lambda h, ki: (h, 0, 0))
    kspec = lambda w: pl.BlockSpec((hp, t, w), lambda h, ki: (h, ki, 0))
    return _call_with_job(
        body, "attn_bwd" if job is None else "attn_bwd_comm", (N_HEADS // hp, nb), job,
        in_specs=[qspec, kspec(HEAD_LANES), kspec(HEAD_LANES), qspec],
        out_specs=[qspec, kspec(HEAD_LANES), kspec(V_DIM)],
        out_shape=[jax.ShapeDtypeStruct((N_HEADS, s, HEAD_LANES), f32), jax.ShapeDtypeStruct((N_HEADS, s, HEAD_LANES), f32),
                   jax.ShapeDtypeStruct((N_HEADS, s, V_DIM), f32)],
        scratch_shapes=[pltpu.VMEM((hp, t, HEAD_LANES), f32)], operands=(qx, k, v, do))


def ssd_bwd(px, pz, plast, states, dyg, params):
    s = px.shape[0]
    nc = s // CHUNK
    per = CHUNK // HALO

    def body(px_ref, halo_ref, pz_ref, pl_ref, st_ref, dyg_ref, cw_ref, cb_ref, dtb_ref, alog_ref, dskip_ref, snw_ref,
             dpx_ref, dpz_ref, dpl_ref, dcw_ref, dcb_ref, ddtb_ref, dalog_ref, ddskip_ref, dsnw_ref, dstate_sc, dhalo_sc):
        t = pl.program_id(0)
        chunk = nc - 1 - t

        @pl.when(t == 0)
        def _():
            dstate_sc[...] = jnp.zeros(dstate_sc.shape, f32)
            dhalo_sc[...] = jnp.zeros(dhalo_sc.shape, f32)

        halo = jnp.where(chunk > 0, halo_ref[...], 0.0)
        xext = jnp.concatenate([halo, px_ref[...]], axis=0)
        _, vjp = jax.vjp(_f_ssd, xext, pz_ref[...], pl_ref[...], st_ref[...], cw_ref[...], cb_ref[...], dtb_ref[...],
                         alog_ref[...], dskip_ref[...], snw_ref[...])
        dxext, dz, dpl, dprev, dcw, dcb, ddtb, dalog, ddskip, dsnw = vjp((dyg_ref[...], dstate_sc[...]))
        dpx_ref[...] = dxext[HALO:]
        dpx_ref[CHUNK - HALO:, :] += dhalo_sc[...]
        dhalo_sc[...] = dxext[:HALO]
        dstate_sc[...] = dprev
        dpz_ref[...] = dz
        dpl_ref[...] = dpl
        _accumulate(t == 0, [dcw_ref, dcb_ref, ddtb_ref, dalog_ref, ddskip_ref, dsnw_ref],
                    [dcw, dcb, ddtb, dalog, ddskip, dsnw])

    rev = lambda w: pl.BlockSpec((CHUNK, w), lambda t: (nc - 1 - t, 0))
    pshapes = [jax.ShapeDtypeStruct((4, D_CONV), f32), jax.ShapeDtypeStruct((1, D_CONV), f32),
               jax.ShapeDtypeStruct((1, 128), f32), jax.ShapeDtypeStruct((1, 128), f32),
               jax.ShapeDtypeStruct((1, 128), f32), jax.ShapeDtypeStruct((1, D_SSD), f32)]
    return pl.pallas_call(
        body, name="ssd_bwd", grid=(nc,),
        in_specs=[rev(D_CONV),
                  pl.BlockSpec((HALO, D_CONV), lambda t: (jnp.maximum((nc - 1 - t) * per - 1, 0), 0)),
                  rev(D_SSD), rev(128),
                  pl.BlockSpec((None, N_HEADS // 2, 2 * SSD_HEAD_DIM, SSD_STATE), lambda t: (nc - 1 - t, 0, 0, 0)),
                  rev(D_SSD)] + _ssd_param_specs(),
        out_specs=[rev(D_CONV), rev(D_SSD), rev(128)] + _ssd_param_specs(),
        out_shape=[jax.ShapeDtypeStruct((s, D_CONV), f32), jax.ShapeDtypeStruct((s, D_SSD), f32),
                   jax.ShapeDtypeStruct((s, 128), f32)] + pshapes,
        scratch_shapes=[pltpu.VMEM((N_HEADS // 2, 2 * SSD_HEAD_DIM, SSD_STATE), f32), pltpu.VMEM((HALO, D_CONV), f32)],
    )(px, px, pz, plast, states, dyg, *params)


def qkv_bwd(pa, plast, cos_t, sin_t, params, dq, dk, dv):
    s = pa.shape[0]
    ts = _token_block(s)

    def body(pa_ref, pl_ref, cos_ref, sin_ref, *rest):
        qaw, kvaw, wq, wk, wv, qnw, knw, kpw = [r[...] for r in rest[:8]]
        dq_ref, dk_ref, dv_ref = rest[8:11]
        dpa_ref, dpl_ref = rest[11:13]
        dprm_refs = list(rest[13:])
        cos_t, sin_t = cos_ref[...], sin_ref[...]

        def stage(pa_, pl_, qaw_, kvaw_, sq, sk, sv, qnw_, knw_, kpw_):
            return _f_qkv(pa_, pl_, cos_t, sin_t, qaw_, kvaw_, wq, wk, wv, qnw_, knw_, kpw_, (sq, sk, sv))

        _, vjp = jax.vjp(stage, pa_ref[...], pl_ref[...], qaw, kvaw, jnp.zeros(wq.shape, f32), jnp.zeros(wk.shape, f32),
                         jnp.zeros(wv.shape, f32), qnw, knw, kpw)
        grads = vjp((dq_ref[...], dk_ref[...], dv_ref[...]))
        dpa_ref[...] = grads[0]
        dpl_ref[...] = grads[1]
        _accumulate(pl.program_id(0) == 0, dprm_refs, list(grads[2:]))

    tok = lambda w: pl.BlockSpec((ts, w), lambda i: (i, 0))
    head = lambda w: pl.BlockSpec((N_HEADS, ts, w), lambda i: (0, i, 0))
    pshapes = [jax.ShapeDtypeStruct((1, Q_RANK), f32), jax.ShapeDtypeStruct((1, KV_RANK), f32),
               jax.ShapeDtypeStruct((N_HEADS, Q_RANK, HEAD_LANES), f32), jax.ShapeDtypeStruct((N_HEADS, KV_RANK, HEAD_LANES), f32),
               jax.ShapeDtypeStruct((N_HEADS, KV_RANK, V_DIM), f32), jax.ShapeDtypeStruct((1, HEAD_LANES), f32),
               jax.ShapeDtypeStruct((1, HEAD_LANES), f32), jax.ShapeDtypeStruct((1, HEAD_LANES), f32)]
    return pl.pallas_call(
        body, name="qkv_bwd", grid=(s // ts,),
        in_specs=[tok(384), tok(128), tok(128), tok(128)] + _qkv_param_specs()
                 + [head(HEAD_LANES), head(HEAD_LANES), head(V_DIM)],
        out_specs=[tok(384), tok(128)] + _qkv_param_specs(),
        out_shape=[jax.ShapeDtypeStruct((s, 384), f32), jax.ShapeDtypeStruct((s, 128), f32)] + pshapes,
    )(pa, plast, cos_t, sin_t, *params, dq, dk, dv)


def proj_bwd(x, nw, sh, sc, w, dpa, dpz, dpx, dpl_k, dpl_dt, dres):
    s = x.shape[0]
    ts = _token_block(s)

    ni = s // ts

    def body(x_ref, nw_ref, sh_ref, sc_ref, w_ref, dpa_ref, dpz_ref, dpx_ref, dplk_ref, dpld_ref, dres_ref,
             dx_ref, dnw_ref, dsh_ref, dsc_ref, dw_ref, acc_sc):
        i = pl.program_id(0)
        g = jnp.concatenate([dpa_ref[...], dpz_ref[...], dpx_ref[...], dplk_ref[...] + dpld_ref[...]], axis=1)
        w = w_ref[...]
        _, vjp = jax.vjp(lambda x_, nw_, sh_, sc_, slot: _f_proj(x_, nw_, sh_, sc_, w, slot), x_ref[...], nw_ref[...],
                         sh_ref[...], sc_ref[...], jnp.zeros(w.shape, f32))
        dx, dnw, dsh, dsc, dw = vjp(g)
        dx_ref[...] = dx + dres_ref[...]
        _accumulate(i == 0, [dnw_ref, dsh_ref, dsc_ref], [dnw, dsh, dsc])
        _accumulate_then_cast(i == 0, i == ni - 1, [acc_sc], [dw_ref], [dw])

    vec = _const((1, D_MODEL))
    vshape = jax.ShapeDtypeStruct((1, D_MODEL), f32)
    tok = lambda w_: pl.BlockSpec((ts, w_), lambda i: (i, 0))
    return pl.pallas_call(
        body, name="proj_bwd", grid=(ni,), scratch_shapes=[pltpu.VMEM((D_PROJ, D_MODEL), f32)],
        in_specs=[tok(D_MODEL), vec, vec, vec, _const((D_PROJ, D_MODEL)), tok(384), tok(512), tok(1024), tok(128), tok(128),
                  tok(D_MODEL)],
        out_specs=[tok(D_MODEL), vec, vec, vec, _const((D_PROJ, D_MODEL))],
        out_shape=[jax.ShapeDtypeStruct((s, D_MODEL), f32), vshape, vshape, vshape,
                   jax.ShapeDtypeStruct((D_PROJ, D_MODEL), bf16)],
    )(x, nw, sh, sc, w, dpa, dpz, dpx, dpl_k, dpl_dt, dres)


def ada_fwd(c_all, w_ada, b_cols):
    def body(c_ref, w_ref, b_ref, out_ref):
        act = jax.nn.silu(c_ref[...])
        for l in range(2):
            out_ref[l] = jnp.dot(act, w_ref[l], precision=lax.Precision.HIGHEST, preferred_element_type=f32) + b_ref[l]

    return pl.pallas_call(body, name="ada_fwd", out_shape=jax.ShapeDtypeStruct((2, N_DEV, 768), f32))(c_all, w_ada, b_cols)


def ada_bwd(c_all, dmod_cols):
    def body(c_ref, d_ref, out_ref):
        out_ref[0] = lax.dot_general(jax.nn.silu(c_ref[...]), d_ref[0], (((0,), (0,)), ((), ())),
                                     precision=lax.Precision.HIGHEST, preferred_element_type=f32)

    return pl.pallas_call(
        body, name="ada_bwd", grid=(2,),
        in_specs=[_const((N_DEV, D_MODEL)), pl.BlockSpec((1, N_DEV, 768), lambda l: (l, 0, 0))],
        out_specs=pl.BlockSpec((1, D_MODEL, 768), lambda l: (l, 0, 0)),
        out_shape=jax.ShapeDtypeStruct((2, D_MODEL, 768), f32),
    )(c_all, dmod_cols)


def _adamw(w, g, m, v):
    m = ADAM_B1 * m + (1.0 - ADAM_B1) * g
    v = ADAM_B2 * v + (1.0 - ADAM_B2) * (g * g)
    m_hat = m / (1.0 - ADAM_B1 ** ADAM_STEP)
    v_hat = v / (1.0 - ADAM_B2 ** ADAM_STEP)
    delta = -ADAM_LR * (m_hat / (jnp.sqrt(v_hat) + ADAM_EPS) + ADAM_WD * w)
    return delta, m, v


def adamw(parts, w, m, v, layer, prev, name):
    n, r, c = parts.shape
    nl = w.shape[0]
    per_elem = 2 * (n * parts.dtype.itemsize + 7 * 4)
    lanes = -(-c // 128) * 128
    tr, tc = r, c
    if per_elem * r * lanes > ADAMW_BLOCK_BYTES:
        fits = [t for t in (256, 128, 64, 32, 16, 8) if r % t == 0]
        if fits:
            tr = fits[0]
        else:
            tc = next(t for t in (512, 256, 128) if c % t == 0)

    def body(p_ref, w_ref, m_ref, v_ref, *rest):
        g_ref, d_ref, nm_ref, nv_ref = rest[-4:]
        g = p_ref[0].astype(f32)
        for k in range(1, n):
            g = g + p_ref[k].astype(f32)
        delta, nm, nv = _adamw(w_ref[...], g, m_ref[...], v_ref[...])
        g_ref[...] = g
        d_ref[...] = delta
        nm_ref[...] = nm
        nv_ref[...] = nv

    blk = pl.BlockSpec((None, tr, tc), lambda i, j: (layer, i, j))
    shp = jax.ShapeDtypeStruct((nl, r, c), f32)
    kept = [] if prev is None else list(prev)
    return pl.pallas_call(
        body, name=name, grid=(r // tr, c // tc),
        in_specs=[pl.BlockSpec((n, tr, tc), lambda i, j: (0, i, j)), blk, blk, blk] + [ANY] * len(kept),
        out_specs=[blk] * 4, out_shape=[shp] * 4,
        input_output_aliases={4 + j: j for j in range(len(kept))},
    )(parts, w, m, v, *kept)


def _my_index():
    return 4 * lax.axis_index("x") + 2 * lax.axis_index("y") + lax.axis_index("c")


def _coords(idx):
    return (idx // 4, (idx // 2) % 2, idx % 2)


class CommJob:
    def __init__(self, operands, out_shape, phases, scratch):
        self.operands, self.out_shape, self.phases, self.scratch = operands, out_shape, phases, scratch


def _wait(out, n_blocks, send_sem, recv_sem, send=True, recv=True):
    span = out.at[pl.ds(0, n_blocks)]
    desc = pltpu.make_async_remote_copy(src_ref=span, dst_ref=span, send_sem=send_sem, recv_sem=recv_sem,
                                        device_id=_coords(_my_index()), device_id_type=MESH)
    if recv:
        desc.wait_recv()
    if send:
        desc.wait_send()


def gather_job(shards):
    n = len(shards)

    def places():
        x, y, c = lax.axis_index("x"), lax.axis_index("y"), lax.axis_index("c")
        return (x, y, c), (x, y, 1 - c), [(1 - x, y), (x, 1 - y), (1 - x, 1 - y)]

    def index(p):
        return 4 * p[0] + 2 * p[1] + p[2]

    def start(ins, outs, sems):
        far_send, far_recv, near_send, near_recv, local = sems
        me, sibling, chips = places()
        for k in range(n):
            pltpu.make_async_copy(ins[k], outs[k].at[index(me)], local.at[k]).start()
            for chip in chips:
                pltpu.make_async_remote_copy(src_ref=ins[k], dst_ref=outs[k].at[index(me)], send_sem=far_send.at[k],
                                             recv_sem=far_recv.at[k], device_id=(*chip, me[2]), device_id_type=MESH).start()
            pltpu.make_async_remote_copy(src_ref=ins[k], dst_ref=outs[k].at[index(me)], send_sem=near_send.at[k],
                                         recv_sem=near_recv.at[k], device_id=sibling, device_id_type=MESH).start()

    def relay(ins, outs, sems):
        far_send, far_recv, near_send, near_recv, local = sems
        me, sibling, chips = places()
        for k in range(n):
            _wait(outs[k], 3, far_send.at[k], far_recv.at[k], send=False)
            for chip in chips:
                block = outs[k].at[index((*chip, me[2]))]
                pltpu.make_async_remote_copy(src_ref=block, dst_ref=block, send_sem=near_send.at[k],
                                             recv_sem=near_recv.at[k], device_id=sibling, device_id_type=MESH).start()

    def finish(ins, outs, sems):
        far_send, far_recv, near_send, near_recv, local = sems
        for k in range(n):
            _wait(outs[k], 4, near_send.at[k], near_recv.at[k])
            _wait(outs[k], 3, far_send.at[k], far_recv.at[k], recv=False)
            pltpu.make_async_copy(ins[k], outs[k].at[0], local.at[k]).wait()

    shapes = [jax.ShapeDtypeStruct((N_DEV,) + tuple(a.shape), a.dtype) for a in shards]
    return CommJob(list(shards), shapes, [start, relay, finish], [pltpu.SemaphoreType.DMA((n,))] * 5)


def scatter_job(tensors):
    n = len(tensors)
    flat, where = [], {}
    for k, pieces in enumerate(tensors):
        d = 0
        for piece in pieces:
            for b in range(piece.shape[0]):
                where[k, d] = (len(flat), b)
                d += 1
            flat.append(piece)
        assert d == N_DEV

    def start(ins, outs, sems):
        send_sems, recv_sems, local_sems = sems
        me = _my_index()

        def block(k, d):
            i, b = where[k, d]
            return ins[i].at[b]

        for d in range(N_DEV):
            @pl.when(d != me)
            def _():
                for k in range(n):
                    pltpu.make_async_remote_copy(src_ref=block(k, d), dst_ref=outs[k].at[me], send_sem=send_sems.at[k],
                                                 recv_sem=recv_sems.at[k], device_id=(d // 4, (d // 2) % 2, d % 2),
                                                 device_id_type=MESH).start()

            @pl.when(d == me)
            def _():
                for k in range(n):
                    pltpu.make_async_copy(block(k, d), outs[k].at[d], local_sems.at[k]).start()

    def finish(ins, outs, sems):
        send_sems, recv_sems, local_sems = sems
        for k in range(n):
            _wait(outs[k], N_DEV - 1, send_sems.at[k], recv_sems.at[k])
            i, b = where[k, 0]
            pltpu.make_async_copy(ins[i].at[b], outs[k].at[0], local_sems.at[k]).wait()

    shapes = [jax.ShapeDtypeStruct((N_DEV,) + tuple(p[0].shape[1:]), p[0].dtype) for p in tensors]
    return CommJob(flat, shapes, [start, finish], [pltpu.SemaphoreType.DMA((n,))] * 3)


def comm_call(job, name):
    ni, no = len(job.operands), len(job.out_shape)

    def body(*refs):
        ins, outs, sems = refs[:ni], refs[ni:ni + no], refs[ni + no:]
        for phase in job.phases:
            phase(ins, outs, sems)

    return pl.pallas_call(body, name=name, in_specs=[ANY] * ni, out_specs=[ANY] * no, out_shape=job.out_shape,
                          scratch_shapes=job.scratch)(*job.operands)


def _carry(job, body, n_in, n_out, at_step):
    ji, jo, js = len(job.operands), len(job.out_shape), len(job.scratch)

    def carrier(*refs):
        a, b = n_in, n_in + ji
        c, d = b + n_out, b + n_out + jo
        e = len(refs) - js
        job_refs = (refs[a:b], refs[c:d], refs[e:])
        n = len(job.phases)

        @pl.when(at_step(0, n))
        def _():
            job.phases[0](*job_refs)

        body(*refs[:a], *refs[b:c], *refs[d:e])

        for i in range(1, n):
            @pl.when(at_step(i, n))
            def _():
                job.phases[i](*job_refs)

    return carrier


def _pad_lanes(v, lo, total=128):
    return jnp.pad(v, (lo, total - lo - v.shape[0]))[None, :]


MIXER_WEIGHTS = ("w_in", "w_q_up", "w_kv_up", "conv_w")
LATE_WEIGHTS = ("w_out", "w_gate_up", "w_down")


def mixer_operands(g, sw):
    w_in = g["w_in"].reshape(D_IN, D_MODEL)
    zero = lambda rows: jnp.zeros((rows, D_MODEL), w_in.dtype)
    w_proj = jnp.concatenate(
        [w_in[:384], w_in[416:928], w_in[928:1952], w_in[1952:1960], zero(56), w_in[384:416], zero(32)], axis=0)
    wq = jnp.pad(g["w_q_up"], ((0, 0), (0, 0), (0, HEAD_LANES - NOPE - ROPE)))
    wk = jnp.pad(g["w_kv_up"][:, :, :NOPE], ((0, 0), (0, 0), (0, HEAD_LANES - NOPE)))
    wv = g["w_kv_up"][:, :, NOPE:]
    qkv = (sw["q_a_norm_w"][None, :], sw["kv_a_norm_w"][None, :], wq, wk, wv,
           _pad_lanes(jnp.concatenate([sw["q_nope_norm_w"], sw["q_pe_norm_w"]]), 0),
           _pad_lanes(sw["k_nope_norm_w"], 0), _pad_lanes(sw["k_pe_norm_w"], NOPE))
    conv_w = g["conv_w"].astype(f32).transpose(1, 0, 2).reshape(4, D_CONV)
    ssd = (conv_w, sw["conv_b"][None, :], _pad_lanes(sw["dt_bias"], 0), _pad_lanes(sw["a_log"], 0),
           _pad_lanes(sw["d_skip"], 0), sw["ssd_norm_w"][None, :])
    return dict(w_proj=w_proj, qkv=qkv, ssd=ssd, n1=sw["norm1_w"][None, :])


def late_operands(g, sw):
    return dict(wo=g["w_out"].reshape(D_MODEL, D_MODEL), wgu=g["w_gate_up"],
                wd=g["w_down"].reshape(N_DEV // 2, FF_SHARD, D_MODEL), n2=sw["norm2_w"][None, :])


def layer_fwd(x, mod, kw, cos_t, sin_t, job=None, late=None):
    sh1, sc1, g1, sh2, sc2, g2 = [mod[i:i + 1] for i in range(6)]
    pa, pz, px, plast = proj_fwd(x, kw["n1"], sh1, sc1, kw["w_proj"])
    q, k, v = qkv_fwd(pa, plast, cos_t, sin_t, kw["qkv"])
    (o, qx), carried = attn_fwd(q, k, v, job)
    if late is not None:
        kw = {**kw, **late(carried)}
    yg, states = ssd_fwd(px, pz, plast, kw["ssd"])
    x_mid = out_fwd(x, o, yg, g1, kw["wo"])
    x_out, mix, h_mid = mlp_fwd(x_mid, kw["n2"], sh2, sc2, g2, kw["wgu"], kw["wd"])
    saved = dict(x=x, pa=pa, pz=pz, px=px, plast=plast, qx=qx, k=k, v=v, o=o, yg=yg, states=states, x_mid=x_mid,
                 mix=mix, h_mid=h_mid)
    return x_out, saved, kw, carried


def layer_bwd_head(dy, mod, kw, sv, job=None):
    _, _, g1, sh2, sc2, g2 = [mod[i:i + 1] for i in range(6)]
    (dhparts, dwg, dwu, dwd), carried = mlp_bwd(sv["h_mid"], dy, g2, kw["wgu"], kw["wd"], job)
    dmid, dn2, dsh2, dsc2, do, dyg, dg1, dg2, dwo = out_bwd(
        dy, dhparts, sv["x_mid"], kw["n2"], sh2, sc2, sv["mix"], sv["o"], sv["yg"], g1, kw["wo"])
    early = dict(w_out=[dwo.reshape(N_DEV, D_MODEL // N_DEV, D_MODEL)], w_gate_up=[dwg, dwu],
                 w_down=[dwd.reshape(N_DEV, D_FF // N_DEV, D_MODEL)])
    head = dict(dmid=dmid, do=do, dyg=dyg, dn2=dn2, dsh2=dsh2, dsc2=dsc2, dg2=dg2, dg1=dg1)
    return head, early, carried


def layer_bwd_tail(hd, mod, kw, cos_t, sin_t, sv, job=None):
    sh1, sc1 = mod[0:1], mod[1:2]
    (dq, dk, dv), carried = attn_bwd(sv["qx"], sv["k"], sv["v"], hd["do"], job)
    dpx, dpz, dpl_dt, dcw, dcb, ddtb, dalog, ddskip, dsnw = ssd_bwd(sv["px"], sv["pz"], sv["plast"], sv["states"],
                                                                   hd["dyg"], kw["ssd"])
    dpa, dpl_k, dqaw, dkvaw, dwq, dwk, dwv, dqnw, dknw, dkpw = qkv_bwd(sv["pa"], sv["plast"], cos_t, sin_t, kw["qkv"],
                                                                       dq, dk, dv)
    dx, dn1, dsh1, dsc1, dwp = proj_bwd(sv["x"], kw["n1"], sh1, sc1, kw["w_proj"], dpa, dpz, dpx, dpl_k, dpl_dt, hd["dmid"])
    dmod = jnp.concatenate([dsh1, dsc1, hd["dg1"], hd["dsh2"], hd["dsc2"], hd["dg2"]], axis=0)
    dw_in = jnp.concatenate([dwp[:384], dwp[1984:2016], dwp[384:1920], dwp[1920:1928]], axis=0)
    grads = dict(
        norm1_w=dn1[0], norm2_w=hd["dn2"][0], q_a_norm_w=dqaw[0], kv_a_norm_w=dkvaw[0],
        q_nope_norm_w=dqnw[0, :NOPE], q_pe_norm_w=dqnw[0, NOPE:NOPE + ROPE], k_nope_norm_w=dknw[0, :NOPE],
        k_pe_norm_w=dkpw[0, NOPE:NOPE + ROPE], conv_b=dcb[0], dt_bias=ddtb[0, :N_HEADS], a_log=dalog[0, :N_HEADS],
        d_skip=ddskip[0, :N_HEADS], ssd_norm_w=dsnw[0],
        w_in=[dw_in.reshape(N_DEV, D_IN // N_DEV, D_MODEL)],
        w_q_up=[dwq[:, :, :NOPE + ROPE].astype(bf16)],
        w_kv_up=[jnp.concatenate([dwk[:, :, :NOPE], dwv], axis=2).astype(bf16)],
        conv_w=[dcw.reshape(4, N_DEV, D_CONV // N_DEV).transpose(1, 0, 2).astype(bf16)],
    )
    return dx, dmod, grads, carried


def _pack_small(get, last=None):
    flat = jnp.concatenate([get(name).reshape(-1) for name, _ in SMALL])
    flat = jnp.pad(flat, (0, SMALL_ROWS * 128 - flat.shape[0]))
    if last is not None:
        flat = flat.at[-1].set(last)
    return flat.reshape(SMALL_ROWS, 128)


def _unpack_small(packed):
    flat = packed.reshape(-1)
    out, off = {}, 0
    for name, size in SMALL:
        out[name] = flat[off:off + 2 * size].reshape(2, size)
        off += 2 * size
    return out


def kernel(x, c, positions, norm1_w, norm2_w, w_ada, b_ada, w_in, q_a_norm_w, w_q_up, kv_a_norm_w, w_kv_up, q_nope_norm_w, q_pe_norm_w, k_nope_norm_w, k_pe_norm_w, conv_w, conv_b, dt_bias, a_log, d_skip, ssd_norm_w, w_out, w_gate_up, w_down, loss_target, m_norm1_w, m_norm2_w, m_w_ada, m_b_ada, m_w_in, m_q_a_norm_w, m_w_q_up, m_kv_a_norm_w, m_w_kv_up, m_q_nope_norm_w, m_q_pe_norm_w, m_k_nope_norm_w, m_k_pe_norm_w, m_conv_w, m_conv_b, m_dt_bias, m_a_log, m_d_skip, m_ssd_norm_w, m_w_out, m_w_gate_up, m_w_down, v_norm1_w, v_norm2_w, v_w_ada, v_b_ada, v_w_in, v_q_a_norm_w, v_w_q_up, v_kv_a_norm_w, v_w_kv_up, v_q_nope_norm_w, v_q_pe_norm_w, v_k_nope_norm_w, v_k_pe_norm_w, v_conv_w, v_conv_b, v_dt_bias, v_a_log, v_d_skip, v_ssd_norm_w, v_w_out, v_w_gate_up, v_w_down):
    w = dict(norm1_w=norm1_w, norm2_w=norm2_w, w_ada=w_ada, b_ada=b_ada, w_in=w_in, q_a_norm_w=q_a_norm_w, w_q_up=w_q_up,
             kv_a_norm_w=kv_a_norm_w, w_kv_up=w_kv_up, q_nope_norm_w=q_nope_norm_w, q_pe_norm_w=q_pe_norm_w,
             k_nope_norm_w=k_nope_norm_w, k_pe_norm_w=k_pe_norm_w, conv_w=conv_w, conv_b=conv_b, dt_bias=dt_bias,
             a_log=a_log, d_skip=d_skip, ssd_norm_w=ssd_norm_w, w_out=w_out, w_gate_up=w_gate_up, w_down=w_down)
    m = dict(norm1_w=m_norm1_w, norm2_w=m_norm2_w, w_ada=m_w_ada, b_ada=m_b_ada, w_in=m_w_in, q_a_norm_w=m_q_a_norm_w,
             w_q_up=m_w_q_up, kv_a_norm_w=m_kv_a_norm_w, w_kv_up=m_w_kv_up, q_nope_norm_w=m_q_nope_norm_w,
             q_pe_norm_w=m_q_pe_norm_w, k_nope_norm_w=m_k_nope_norm_w, k_pe_norm_w=m_k_pe_norm_w, conv_w=m_conv_w,
             conv_b=m_conv_b, dt_bias=m_dt_bias, a_log=m_a_log, d_skip=m_d_skip, ssd_norm_w=m_ssd_norm_w, w_out=m_w_out,
             w_gate_up=m_w_gate_up, w_down=m_w_down)
    v = dict(norm1_w=v_norm1_w, norm2_w=v_norm2_w, w_ada=v_w_ada, b_ada=v_b_ada, w_in=v_w_in, q_a_norm_w=v_q_a_norm_w,
             w_q_up=v_w_q_up, kv_a_norm_w=v_kv_a_norm_w, w_kv_up=v_w_kv_up, q_nope_norm_w=v_q_nope_norm_w,
             q_pe_norm_w=v_q_pe_norm_w, k_nope_norm_w=v_k_nope_norm_w, k_pe_norm_w=v_k_pe_norm_w, conv_w=v_conv_w,
             conv_b=v_conv_b, dt_bias=v_dt_bias, a_log=v_a_log, d_skip=v_d_skip, ssd_norm_w=v_ssd_norm_w, w_out=v_w_out,
             w_gate_up=v_w_gate_up, w_down=v_w_down)
    me = _my_index()
    seq = x.shape[1]

    def shard(name, l):
        if name == "conv_w":
            return w[name][l]
        if name in TRANSPOSED:
            return jnp.swapaxes(w[name][l], 0, 1).astype(bf16)
        return w[name][l].astype(bf16)

    def shards(names, l):
        return [shard(name, l) for name in names]

    small = [{name: w[name][l] for name, _ in SMALL if name != "b_ada"} for l in range(2)]
    n_mix, n_late = len(MIXER_WEIGHTS), len(LATE_WEIGHTS)

    first = comm_call(gather_job([c] + shards(MIXER_WEIGHTS, 0)), "gather_first")
    c_all = first[0].reshape(N_DEV, D_MODEL)
    kws = [mixer_operands(dict(zip(MIXER_WEIGHTS, first[1:])), small[0]), None]

    b_cols = lax.dynamic_slice_in_dim(b_ada, me * 768, 768, axis=1)
    mod_cols = ada_fwd(c_all, w_ada, b_cols)
    (mod_all,) = comm_call(gather_job([mod_cols]), "gather_mod")
    mod_me = lax.dynamic_index_in_dim(mod_all, me, axis=2, keepdims=False)
    mods = [mod_me[:, l, :].reshape(6, D_MODEL) for l in range(2)]

    inv_freq = 1.0 / (ROPE_THETA ** (jnp.arange(0, ROPE, 2, dtype=f32) / ROPE))
    inv = _pad_lanes(jnp.concatenate([inv_freq, inv_freq]), NOPE)
    cos_t, sin_t = rope_tables(positions.reshape(seq, 1), inv)

    saved = [None, None]
    h, saved[0], kws[0], got = layer_fwd(
        x[0], mods[0], kws[0], cos_t, sin_t, gather_job(shards(LATE_WEIGHTS, 0) + shards(MIXER_WEIGHTS, 1)),
        lambda got: late_operands(dict(zip(LATE_WEIGHTS, got[:n_late])), small[0]))
    kws[1] = mixer_operands(dict(zip(MIXER_WEIGHTS, got[n_late:])), small[1])
    h, saved[1], kws[1], _ = layer_fwd(
        h, mods[1], kws[1], cos_t, sin_t, gather_job(shards(LATE_WEIGHTS, 1)),
        lambda got: late_operands(dict(zip(LATE_WEIGHTS, got)), small[1]))
    dy, loss_part = loss_fwd(h, loss_target[0])

    early, late = ("w_out", "w_gate_up", "w_down"), ("w_in", "w_q_up", "w_kv_up", "conv_w")
    parts = [{}, {}]
    head, pieces, _ = layer_bwd_head(dy, mods[1], kws[1], saved[1])
    dy, dmod1, grads1, got = layer_bwd_tail(head, mods[1], kws[1], cos_t, sin_t, saved[1], scatter_job([pieces[n] for n in early]))
    parts[1].update(zip(early, got))
    head, pieces, got = layer_bwd_head(dy, mods[0], kws[0], saved[0], scatter_job([grads1[n] for n in late]))
    parts[1].update(zip(late, got))
    dy, dmod0, grads0, got = layer_bwd_tail(head, mods[0], kws[0], cos_t, sin_t, saved[0], scatter_job([pieces[n] for n in early]))
    parts[0].update(zip(early, got))
    parts[0].update(zip(late, comm_call(scatter_job([grads0[n] for n in late]), "scatter_layer0_rest")))
    grad_x = dy[None]

    small_part = {name: jnp.stack([grads0[name], grads1[name]]) for name, _ in SMALL if name != "b_ada"}
    small_part["b_ada"] = jnp.stack([dmod0.reshape(-1), dmod1.reshape(-1)])
    (small_all,) = comm_call(gather_job([_pack_small(lambda n: small_part[n], loss_part[0, 0])]), "gather_small_grads")
    packed = adamw(small_all, _pack_small(lambda n: w[n])[None], _pack_small(lambda n: m[n])[None],
                   _pack_small(lambda n: v[n])[None], 0, None, "adamw_small")
    loss = packed[0][0, -1, -1]
    res = {}
    for key, arr in zip("gdmv", packed):
        for name, val in _unpack_small(arr[0]).items():
            res[key, name] = val

    off = 2 * (1024 + 1024)
    dmod_all = small_all.reshape(N_DEV, -1)[:, off:off + 2 * 6144].reshape(N_DEV, 2, 6144)
    dmod_cols = lax.dynamic_slice_in_dim(dmod_all, me * 768, 768, axis=2).transpose(1, 0, 2)
    g_ada = ada_bwd(c_all, dmod_cols)
    out = None
    for l in range(2):
        out = adamw(g_ada[l][None], w_ada, m_w_ada, v_w_ada, l, out, "adamw_w_ada")
    res.update(zip([(key, "w_ada") for key in "gdmv"], out))

    for name in BIG:
        view = (lambda a: jnp.swapaxes(a, 1, 2)) if name in TRANSPOSED else (lambda a: a)
        out = None
        for l in range(2):
            out = adamw(parts[l][name], view(w[name]), view(m[name]), view(v[name]), l, out, "adamw_" + name)
        res.update(zip([(key, name) for key in "gdmv"], [view(a) for a in out]))

    return (loss, grad_x, *[res["g", n] for n in WEIGHTS], *[res["d", n] for n in WEIGHTS],
            *[res["m", n] for n in WEIGHTS], *[res["v", n] for n in WEIGHTS])
```

```python
import functools

import jax
import jax.numpy as jnp
from jax import lax
from jax.experimental import pallas as pl
from jax.experimental.pallas import tpu as pltpu

f32 = jnp.float32
bf16 = jnp.bfloat16

N_DEV = 8
D_MODEL = 1024
N_HEADS = 8
HEAD_LANES = 128
NOPE = 64
ROPE = 32
V_DIM = 64
Q_RANK = 256
KV_RANK = 128
D_SSD = 512
D_CONV = 1024
SSD_STATE = 128
SSD_HEAD_DIM = 64
CHUNK = 128
HALO = 8
D_FF = 2816
FF_SHARD = 704
D_IN = 1960
D_PROJ = 2048
EPS = 1e-6
LOG2E = 1.4426950408889634
LN2 = 0.6931471805599453
Q_SCALE = (NOPE + ROPE) ** -0.5 * LOG2E
SPARE_Q = NOPE + ROPE
SPARE_V = V_DIM
ATTN_ROWS_FWD = 256
ATTN_HEADS_FWD = 4
ATTN_HEADS_BWD = 4
MLP_BWD_ROWS = 512
ROPE_THETA = 10000.0
NEG = -1e30

ADAM_LR = 0.001
ADAM_B1 = 0.9
ADAM_B2 = 0.999
ADAM_EPS = 1e-08
ADAM_WD = 0.01
ADAM_STEP = 10
ADAMW_BLOCK_BYTES = 24 << 20

MESH = pl.DeviceIdType.MESH
ANY = pl.BlockSpec(memory_space=pl.ANY)

SMALL = (("norm1_w", 1024), ("norm2_w", 1024), ("b_ada", 6144), ("q_a_norm_w", 256), ("kv_a_norm_w", 128),
         ("q_nope_norm_w", 64), ("q_pe_norm_w", 32), ("k_nope_norm_w", 64), ("k_pe_norm_w", 32),
         ("conv_b", 1024), ("dt_bias", 8), ("a_log", 8), ("d_skip", 8), ("ssd_norm_w", 512))
SMALL_ROWS = 168
BIG = ("w_in", "w_q_up", "w_kv_up", "conv_w", "w_out", "w_gate_up", "w_down")
TRANSPOSED = ("w_in", "w_gate_up")
WEIGHTS = ("norm1_w", "norm2_w", "w_ada", "b_ada", "w_in", "q_a_norm_w", "w_q_up", "kv_a_norm_w", "w_kv_up",
           "q_nope_norm_w", "q_pe_norm_w", "k_nope_norm_w", "k_pe_norm_w", "conv_w", "conv_b", "dt_bias",
           "a_log", "d_skip", "ssd_norm_w", "w_out", "w_gate_up", "w_down")


def _dot(a, b, ca, cb):
    return lax.dot_general(a.astype(bf16), b.astype(bf16), (((ca,), (cb,)), ((), ())), preferred_element_type=f32)


@jax.custom_vjp
def mm(a, b):
    return _dot(a, b, 1, 0)


def _mm_fwd(a, b):
    return _dot(a, b, 1, 0), (a, b)


def _mm_bwd(res, g):
    a, b = res
    return _dot(g, b, 1, 1).astype(a.dtype), _dot(a, g, 0, 0).astype(b.dtype)


mm.defvjp(_mm_fwd, _mm_bwd)


@jax.custom_vjp
def _mm_slot(a, w, slot):
    return _dot(a, w, 1, 0)


def _mm_slot_fwd(a, w, slot):
    return _dot(a, w, 1, 0), (a, w)


def _mm_slot_bwd(res, g):
    a, w = res
    return _dot(g, w, 1, 1).astype(a.dtype), None, _dot(a, g, 0, 0)


_mm_slot.defvjp(_mm_slot_fwd, _mm_slot_bwd)


def mmw(a, w, slot=None):
    return _dot(a, w, 1, 0) if slot is None else _mm_slot(a, w, slot)


@jax.custom_vjp
def _mm_slot_t(a, wt, slot):
    return _dot(a, wt, 1, 1)


def _mm_slot_t_fwd(a, wt, slot):
    return _dot(a, wt, 1, 1), (a, wt)


def _mm_slot_t_bwd(res, g):
    a, wt = res
    return _dot(g, wt, 1, 0).astype(a.dtype), None, _dot(g, a, 0, 0)


_mm_slot_t.defvjp(_mm_slot_t_fwd, _mm_slot_t_bwd)


def mmw_t(a, wt, slot=None):
    return _dot(a, wt, 1, 1) if slot is None else _mm_slot_t(a, wt, slot)


@jax.custom_vjp
def mm_nt(a, b):
    return _dot(a, b, 1, 1)


def _mm_nt_fwd(a, b):
    return _dot(a, b, 1, 1), (a, b)


def _mm_nt_bwd(res, g):
    a, b = res
    return _dot(g, b, 1, 0).astype(a.dtype), _dot(g, a, 0, 0).astype(b.dtype)


mm_nt.defvjp(_mm_nt_fwd, _mm_nt_bwd)


@jax.custom_vjp
def mm_tn(a, b):
    return _dot(a, b, 0, 0)


def _mm_tn_fwd(a, b):
    return _dot(a, b, 0, 0), (a, b)


def _mm_tn_bwd(res, g):
    a, b = res
    return _dot(b, g, 1, 1).astype(a.dtype), _dot(a, g, 1, 0).astype(b.dtype)


mm_tn.defvjp(_mm_tn_fwd, _mm_tn_bwd)


def _rms(x, w):
    return x * lax.rsqrt(jnp.mean(x * x, axis=-1, keepdims=True) + EPS) * w


def _const(shape):
    n = len(shape)
    return pl.BlockSpec(shape, lambda *_: (0,) * n)


def _accumulate(first, refs, vals):
    @pl.when(first)
    def _():
        for r, v in zip(refs, vals):
            r[...] = v

    @pl.when(jnp.logical_not(first))
    def _():
        for r, v in zip(refs, vals):
            r[...] += v


def _accumulate_then_cast(first, last, accs, outs, vals):
    _accumulate(first, accs, vals)

    @pl.when(last)
    def _():
        for a, o in zip(accs, outs):
            o[...] = a[...].astype(o.dtype)


def _token_block(s):
    return min(512, s)


def _f_proj(x, nw, sh, sc, w, slot=None):
    h = _rms(x, nw) * (1.0 + sc) + sh
    return mmw_t(h, w, slot)


def _f_qkv(pa, plast, cos_t, sin_t, qaw, kvaw, wq, wk, wv, qnw, knw, kpw, slots=None):
    sq, sk, sv = slots if slots is not None else ([None] * N_HEADS,) * 3
    lane = lax.broadcasted_iota(jnp.int32, (1, HEAD_LANES), 1)
    m_nope = lane < NOPE
    m_pe = (lane >= NOPE) & (lane < NOPE + ROPE)
    rows = pa.shape[0]

    def rope(t):
        half = ROPE // 2
        swapped = jnp.concatenate(
            [jnp.zeros((rows, NOPE), f32), t[:, NOPE + half:NOPE + ROPE], t[:, NOPE:NOPE + half],
             jnp.zeros((rows, HEAD_LANES - NOPE - ROPE), f32)], axis=1)
        return t * cos_t + swapped * sin_t

    qa = _rms(pa[:, :Q_RANK], qaw)
    kva = _rms(pa[:, Q_RANK:Q_RANK + KV_RANK], kvaw)
    kp = jnp.where(m_pe, plast, 0.0)
    kp = kp * lax.rsqrt(jnp.sum(kp * kp, axis=-1, keepdims=True) / ROPE + EPS) * kpw
    k_rot = rope(kp)
    qs, ks, vs = [], [], []
    for h in range(N_HEADS):
        qh = mmw(qa, wq[h], sq[h])
        ss_n = jnp.sum(jnp.where(m_nope, qh * qh, 0.0), axis=-1, keepdims=True) / NOPE
        ss_p = jnp.sum(jnp.where(m_pe, qh * qh, 0.0), axis=-1, keepdims=True) / ROPE
        r = jnp.where(m_nope, lax.rsqrt(ss_n + EPS), lax.rsqrt(ss_p + EPS))
        qs.append(rope(qh * r * qnw) * Q_SCALE)
        kh = mmw(kva, wk[h], sk[h])
        kh = kh * lax.rsqrt(jnp.sum(kh * kh, axis=-1, keepdims=True) / NOPE + EPS) * knw
        ks.append(kh + k_rot)
        vs.append(mmw(kva, wv[h], sv[h]))
    return jnp.stack(qs), jnp.stack(ks), jnp.stack(vs)


def _f_ssd(xext, z, plast, prev, cw, cb, dtb, alog, dskip, snw):
    n = CHUNK
    conv = cb
    for k in range(4):
        conv = conv + cw[k:k + 1] * xext[HALO - 3 + k:HALO - 3 + k + n]
    xc = jax.nn.silu(conv)
    xs, bm, cm = xc[:, :D_SSD], xc[:, D_SSD:D_SSD + 2 * SSD_STATE], xc[:, D_SSD + 2 * SSD_STATE:]
    lane = lax.broadcasted_iota(jnp.int32, (1, 128), 1)
    dt = jax.nn.softplus(jnp.where(lane < N_HEADS, plast, 0.0) + dtb)
    adt = dt * (-jnp.exp(alog))
    row = lax.broadcasted_iota(jnp.int32, (n, n), 0)
    col = lax.broadcasted_iota(jnp.int32, (n, n), 1)
    tri = row >= col
    acs = jnp.dot(tri.astype(f32), adt, precision=lax.Precision.HIGHEST, preferred_element_type=f32)
    acs_t = acs.T
    bgs = [bm[:, g * SSD_STATE:(g + 1) * SSD_STATE] for g in range(2)]
    cgs = [cm[:, g * SSD_STATE:(g + 1) * SSD_STATE] for g in range(2)]
    cb_ts = [mm_nt(cgs[g], bgs[g]) for g in range(2)]
    low = lane < SSD_HEAD_DIM
    low_rows = lax.broadcasted_iota(jnp.int32, (2 * SSD_HEAD_DIM, 1), 0) < SSD_HEAD_DIM

    def both(a0, a1):
        return jnp.where(low, a0, a1)

    pre = []
    for i in range(N_HEADS // 2):
        h0, h1 = 2 * i, 2 * i + 1
        col0, col1 = acs[:, h0:h0 + 1], acs[:, h1:h1 + 1]
        last0, last1 = acs[n - 1:n, h0:h0 + 1], acs[n - 1:n, h1:h1 + 1]
        cb_t = cb_ts[i // 2]
        scores0 = cb_t * jnp.exp(jnp.where(tri, col0 - acs_t[h0:h0 + 1, :], -jnp.inf))
        scores1 = cb_t * jnp.exp(jnp.where(tri, col1 - acs_t[h1:h1 + 1, :], -jnp.inf))
        xp = xs[:, i * 128:(i + 1) * 128]
        xdt = xp * both(dt[:, h0:h0 + 1], dt[:, h1:h1 + 1])
        weighted = xdt * both(jnp.exp(last0 - col0), jnp.exp(last1 - col1))
        chunk_decay = jnp.where(low_rows, jnp.exp(last0), jnp.exp(last1))
        in_decay = both(jnp.exp(col0), jnp.exp(col1))
        skip = both(dskip[:, h0:h0 + 1], dskip[:, h1:h1 + 1]) * xp
        pre.append((scores0, scores1, xdt, weighted, chunk_decay, in_decay, skip))
    prods = []
    for i in range(N_HEADS // 2):
        scores0, scores1, xdt, weighted, _, _, _ = pre[i]
        g = i // 2
        y_diag = mm(scores0, jnp.where(low, xdt, 0.0)) + mm(scores1, jnp.where(low, 0.0, xdt))
        prods.append((y_diag, mm_tn(weighted, bgs[g]), mm_nt(cgs[g], prev[i])))
    ys, news = [], []
    for i in range(N_HEADS // 2):
        y_diag, st, y_off = prods[i]
        _, _, _, _, chunk_decay, in_decay, skip = pre[i]
        news.append(chunk_decay * prev[i] + st)
        ys.append(y_diag + y_off * in_decay + skip)
    y = jnp.concatenate(ys, axis=1)
    yg = y * jax.nn.silu(z)
    half = D_SSD // 2
    outs = []
    for g in range(2):
        t = yg[:, g * half:(g + 1) * half]
        outs.append(t * lax.rsqrt(jnp.mean(t * t, axis=-1, keepdims=True) + EPS))
    return jnp.concatenate(outs, axis=1) * snw, jnp.stack(news)


def _f_out(o, yg, g1, wo, slot=None):
    cat = jnp.concatenate([o[h] for h in range(N_HEADS)] + [yg], axis=1)
    return g1 * mmw(cat, wo, slot)


def _f_modulate(x, nw, sh, sc):
    return _rms(x, nw) * (1.0 + sc) + sh


def _f_gate_up(h, wg, wu, slot_g=None, slot_u=None):
    return jax.nn.silu(mmw_t(h, wg, slot_g)) * mmw_t(h, wu, slot_u)


def proj_fwd(x, nw, sh, sc, w):
    s = x.shape[0]
    ts = _token_block(s)

    def body(x_ref, nw_ref, sh_ref, sc_ref, w_ref, pa_ref, pz_ref, px_ref, pl_ref):
        p = _f_proj(x_ref[...], nw_ref[...], sh_ref[...], sc_ref[...], w_ref[...])
        pa_ref[...] = p[:, :384]
        pz_ref[...] = p[:, 384:896]
        px_ref[...] = p[:, 896:1920]
        pl_ref[...] = p[:, 1920:]

    vec = _const((1, D_MODEL))
    return pl.pallas_call(
        body, name="proj_fwd", grid=(s // ts,),
        in_specs=[pl.BlockSpec((ts, D_MODEL), lambda i: (i, 0)), vec, vec, vec, _const((D_PROJ, D_MODEL))],
        out_specs=[pl.BlockSpec((ts, 384), lambda i: (i, 0)), pl.BlockSpec((ts, 512), lambda i: (i, 0)),
                   pl.BlockSpec((ts, 1024), lambda i: (i, 0)), pl.BlockSpec((ts, 128), lambda i: (i, 0))],
        out_shape=[jax.ShapeDtypeStruct((s, 384), f32), jax.ShapeDtypeStruct((s, 512), f32),
                   jax.ShapeDtypeStruct((s, 1024), f32), jax.ShapeDtypeStruct((s, 128), f32)],
    )(x, nw, sh, sc, w)


def rope_tables(pos, inv):
    s = pos.shape[0]
    ts = _token_block(s)

    def body(pos_ref, inv_ref, cos_ref, sin_ref):
        ang = pos_ref[...].astype(f32) * inv_ref[...]
        lane = lax.broadcasted_iota(jnp.int32, (1, HEAD_LANES), 1)
        half = ROPE // 2
        cos_ref[...] = jnp.where(lane < NOPE, 1.0, jnp.where(lane < NOPE + ROPE, jnp.cos(ang), 0.0))
        sn = jnp.sin(ang)
        sin_ref[...] = jnp.where((lane >= NOPE) & (lane < NOPE + half), -sn,
                                 jnp.where((lane >= NOPE + half) & (lane < NOPE + ROPE), sn, 0.0))

    return pl.pallas_call(
        body, name="rope_tables", grid=(s // ts,),
        in_specs=[pl.BlockSpec((ts, 1), lambda i: (i, 0)), _const((1, HEAD_LANES))],
        out_specs=[pl.BlockSpec((ts, HEAD_LANES), lambda i: (i, 0))] * 2,
        out_shape=[jax.ShapeDtypeStruct((s, HEAD_LANES), f32)] * 2,
    )(pos, inv)


def _qkv_param_specs():
    return [_const((1, Q_RANK)), _const((1, KV_RANK)), _const((N_HEADS, Q_RANK, HEAD_LANES)),
            _const((N_HEADS, KV_RANK, HEAD_LANES)), _const((N_HEADS, KV_RANK, V_DIM)),
            _const((1, HEAD_LANES)), _const((1, HEAD_LANES)), _const((1, HEAD_LANES))]


def qkv_fwd(pa, plast, cos_t, sin_t, params):
    s = pa.shape[0]
    ts = _token_block(s)

    def body(pa_ref, pl_ref, cos_ref, sin_ref, *rest):
        prm = [r[...] for r in rest[:8]]
        q_ref, k_ref, v_ref = rest[8:]
        q, k, v = _f_qkv(pa_ref[...], pl_ref[...], cos_ref[...], sin_ref[...], *prm)
        q_ref[...] = q.astype(bf16)
        lane = lax.broadcasted_iota(jnp.int32, (1, 1, HEAD_LANES), 2)
        k_ref[...] = jnp.where((lane == SPARE_Q) | (lane == SPARE_Q + 1), 1.0, k).astype(bf16)
        v_ref[...] = jnp.concatenate([v, jnp.ones_like(v)], axis=-1).astype(bf16)

    tok = lambda w: pl.BlockSpec((ts, w), lambda i: (i, 0))
    head = pl.BlockSpec((N_HEADS, ts, HEAD_LANES), lambda i: (0, i, 0))
    return pl.pallas_call(
        body, name="qkv_fwd", grid=(s // ts,),
        in_specs=[tok(384), tok(128), tok(128), tok(128)] + _qkv_param_specs(),
        out_specs=[head] * 3, out_shape=[jax.ShapeDtypeStruct((N_HEADS, s, HEAD_LANES), bf16)] * 3,
    )(pa, plast, cos_t, sin_t, *params)


def _scores(q, k):
    return lax.dot_general(q, k, (((1,), (1,)), ((), ())), preferred_element_type=f32)


def _tril(rows, cols, row_offset):
    row = row_offset + lax.broadcasted_iota(jnp.int32, (rows, cols), 0)
    col = lax.broadcasted_iota(jnp.int32, (rows, cols), 1)
    return row >= col


def _call_with_job(body, name, grid, job, in_specs, out_specs, out_shape, scratch_shapes, operands, relay_at=None):
    if job is None:
        res = pl.pallas_call(body, name=name, grid=grid, in_specs=in_specs, out_specs=out_specs, out_shape=out_shape,
                             scratch_shapes=scratch_shapes)(*operands)
        return res, None

    def at_step(i, n):
        if i == 0:
            want = [0] * len(grid)
        elif i == n - 1:
            want = [g - 1 for g in grid]
        else:
            want = relay_at
        return functools.reduce(jnp.logical_and, [pl.program_id(a) == s for a, s in enumerate(want)])

    carrier = _carry(job, body, len(in_specs), len(out_specs), at_step)
    res = pl.pallas_call(
        carrier, name=name, grid=grid,
        in_specs=list(in_specs) + [ANY] * len(job.operands), out_specs=list(out_specs) + [ANY] * len(job.out_shape),
        out_shape=list(out_shape) + list(job.out_shape), scratch_shapes=list(scratch_shapes) + job.scratch,
    )(*operands, *job.operands)
    return res[:len(out_specs)], res[len(out_specs):]


def attn_fwd(q, k, v, job=None):
    s = q.shape[1]
    t = _token_block(s)
    nb = s // t

    rb = min(ATTN_ROWS_FWD, t)

    hp = ATTN_HEADS_FWD

    def body(q_ref, k_ref, v_ref, o_ref, qx_ref, m_sc, acc_sc):
        qi = pl.program_id(1)
        m_sc[...] = jnp.full(m_sc.shape, NEG, f32)
        acc_sc[...] = jnp.zeros(acc_sc.shape, f32)

        def step(k0, diagonal):
            chains = [(hh, r) for hh in range(hp) for r in range(t // rb)]

            def scores(hh, r):
                nk = (r + 1) * rb if diagonal else t
                sc = _scores(q_ref[hh, pl.ds(r * rb, rb), :], k_ref[hh, pl.ds(k0, nk), :])
                return jnp.where(_tril(rb, nk, r * rb), sc, NEG) if diagonal else sc

            ahead = scores(*chains[0])
            for c, (hh, r) in enumerate(chains):
                sc = ahead
                if c + 1 < len(chains):
                    ahead = scores(*chains[c + 1])
                rows = pl.ds(r * rb, rb)
                keys = pl.ds(k0, (r + 1) * rb if diagonal else t)
                m_prev = m_sc[hh, rows, :1]
                m_new = jnp.maximum(m_prev, jnp.max(sc, axis=-1, keepdims=True))
                p = jnp.exp2(sc - m_new)
                alpha = jnp.exp2(m_prev - m_new)
                acc = alpha * acc_sc[hh, rows, :] + jnp.dot(p.astype(bf16), v_ref[hh, keys, :], preferred_element_type=f32)
                if diagonal:
                    l = acc[:, V_DIM:V_DIM + 1]
                    o_ref[hh, rows, :] = acc[:, :V_DIM] / l
                    lse = m_new + jnp.log2(l)
                    high = lse.astype(bf16)
                    low = (lse - high.astype(f32)).astype(bf16)
                    lane = lax.broadcasted_iota(jnp.int32, (1, HEAD_LANES), 1)
                    qx_ref[hh, rows, :] = jnp.where(lane == SPARE_Q, -high,
                                                    jnp.where(lane == SPARE_Q + 1, -low, q_ref[hh, rows, :]))
                else:
                    acc_sc[hh, rows, :] = acc
                    m_sc[hh, rows, :] = jnp.broadcast_to(m_new, (rb, 128))

        def below(ki, carry):
            step(pl.multiple_of(ki * t, t), False)
            return carry

        lax.fori_loop(0, qi, below, 0)
        step(pl.multiple_of(qi * t, t), True)

    return _call_with_job(
        body, "attn_fwd" if job is None else "attn_fwd_comm", (N_HEADS // hp, nb), job,
        in_specs=[pl.BlockSpec((hp, t, HEAD_LANES), lambda h, qi: (h, qi, 0)),
                  pl.BlockSpec((hp, s, HEAD_LANES), lambda h, qi: (h, 0, 0)),
                  pl.BlockSpec((hp, s, HEAD_LANES), lambda h, qi: (h, 0, 0))],
        out_specs=[pl.BlockSpec((hp, t, V_DIM), lambda h, qi: (h, qi, 0)),
                   pl.BlockSpec((hp, t, HEAD_LANES), lambda h, qi: (h, qi, 0))],
        out_shape=[jax.ShapeDtypeStruct((N_HEADS, s, V_DIM), f32), jax.ShapeDtypeStruct((N_HEADS, s, HEAD_LANES), bf16)],
        scratch_shapes=[pltpu.VMEM((hp, t, 128), f32), pltpu.VMEM((hp, t, HEAD_LANES), f32)],
        operands=(q, k, v), relay_at=(N_HEADS // hp - 1, max(nb - 2, 0)))


def _ssd_param_specs():
    return [_const((4, D_CONV)), _const((1, D_CONV)), _const((1, 128)), _const((1, 128)), _const((1, 128)),
            _const((1, D_SSD))]


def ssd_fwd(px, pz, plast, params):
    s = px.shape[0]
    nc = s // CHUNK

    def body(px_ref, pz_ref, pl_ref, cw_ref, cb_ref, dtb_ref, alog_ref, dskip_ref, snw_ref, yg_ref, st_ref,
             state_sc, halo_sc):
        i = pl.program_id(0)

        @pl.when(i == 0)
        def _():
            state_sc[...] = jnp.zeros(state_sc.shape, f32)
            halo_sc[...] = jnp.zeros(halo_sc.shape, f32)

        x = px_ref[...]
        prev = state_sc[...]
        st_ref[...] = prev
        xext = jnp.concatenate([halo_sc[...], x], axis=0)
        yg, new = _f_ssd(xext, pz_ref[...], pl_ref[...], prev, cw_ref[...], cb_ref[...], dtb_ref[...],
                         alog_ref[...], dskip_ref[...], snw_ref[...])
        yg_ref[...] = yg
        state_sc[...] = new
        halo_sc[...] = x[CHUNK - HALO:]

    tok = lambda w: pl.BlockSpec((CHUNK, w), lambda i: (i, 0))
    return pl.pallas_call(
        body, name="ssd_fwd", grid=(nc,),
        in_specs=[tok(D_CONV), tok(D_SSD), tok(128)] + _ssd_param_specs(),
        out_specs=[tok(D_SSD), pl.BlockSpec((None, N_HEADS // 2, 2 * SSD_HEAD_DIM, SSD_STATE), lambda i: (i, 0, 0, 0))],
        out_shape=[jax.ShapeDtypeStruct((s, D_SSD), f32),
                   jax.ShapeDtypeStruct((nc, N_HEADS // 2, 2 * SSD_HEAD_DIM, SSD_STATE), f32)],
        scratch_shapes=[pltpu.VMEM((N_HEADS // 2, 2 * SSD_HEAD_DIM, SSD_STATE), f32), pltpu.VMEM((HALO, D_CONV), f32)],
    )(px, pz, plast, *params)


def out_fwd(x, o, yg, g1, wo):
    s = x.shape[0]
    ts = _token_block(s)

    def body(x_ref, o_ref, yg_ref, g1_ref, wo_ref, out_ref):
        out_ref[...] = x_ref[...] + _f_out(o_ref[...], yg_ref[...], g1_ref[...], wo_ref[...])

    return pl.pallas_call(
        body, name="out_fwd", grid=(s // ts,),
        in_specs=[pl.BlockSpec((ts, D_MODEL), lambda i: (i, 0)), pl.BlockSpec((N_HEADS, ts, V_DIM), lambda i: (0, i, 0)),
                  pl.BlockSpec((ts, D_SSD), lambda i: (i, 0)), _const((1, D_MODEL)), _const((D_MODEL, D_MODEL))],
        out_specs=pl.BlockSpec((ts, D_MODEL), lambda i: (i, 0)),
        out_shape=jax.ShapeDtypeStruct((s, D_MODEL), f32),
    )(x, o, yg, g1, wo)


def mlp_fwd(x, nw, sh, sc, g2, wgu, wd):
    s = x.shape[0]
    ts = _token_block(s)
    nj = N_DEV // 2

    def body(x_ref, nw_ref, sh_ref, sc_ref, g2_ref, wg_ref, wu_ref, wd_ref, out_ref, mix_ref, h_ref):
        j = pl.program_id(1)

        @pl.when(j == 0)
        def _():
            h_ref[...] = _f_modulate(x_ref[...], nw_ref[...], sh_ref[...], sc_ref[...]).astype(bf16)
            mix_ref[...] = jnp.zeros(mix_ref.shape, f32)

        nr = 2 if ts % 2 == 0 else 1
        half = ts // nr
        wg, wu, wd = wg_ref[...], wu_ref[...], wd_ref[...]
        products = lambda r: (mmw_t(h_ref[pl.ds(r * half, half), :], wg), mmw_t(h_ref[pl.ds(r * half, half), :], wu))
        ahead = products(0)
        for r in range(nr):
            gate, up = ahead
            if r + 1 < nr:
                ahead = products(r + 1)
            mix_ref[pl.ds(r * half, half), :] += mmw(jax.nn.silu(gate) * up, wd)

        @pl.when(j == nj - 1)
        def _():
            out_ref[...] = x_ref[...] + g2_ref[...] * mix_ref[...]

    vec = _const((1, D_MODEL))
    tok = pl.BlockSpec((ts, D_MODEL), lambda i, j: (i, 0))
    return pl.pallas_call(
        body, name="mlp_fwd", grid=(s // ts, nj),
        in_specs=[tok, vec, vec, vec, vec,
                  pl.BlockSpec((None, FF_SHARD, D_MODEL), lambda i, j: (j, 0, 0)),
                  pl.BlockSpec((None, FF_SHARD, D_MODEL), lambda i, j: (j + nj, 0, 0)),
                  pl.BlockSpec((None, FF_SHARD, D_MODEL), lambda i, j: (j, 0, 0))],
        out_specs=[tok] * 3,
        out_shape=[jax.ShapeDtypeStruct((s, D_MODEL), f32), jax.ShapeDtypeStruct((s, D_MODEL), f32),
                   jax.ShapeDtypeStruct((s, D_MODEL), bf16)],
    )(x, nw, sh, sc, g2, wgu, wgu, wd)


def loss_fwd(y, target):
    s = y.shape[0]
    ts = _token_block(s)

    def body(y_ref, t_ref, dy_ref, loss_ref):
        d = y_ref[...] - t_ref[...]
        dy_ref[...] = d * (1.0 / D_MODEL)
        part = 0.5 * jnp.sum(jnp.sum(d * d, axis=-1, keepdims=True) * (1.0 / D_MODEL), axis=0, keepdims=True)
        _accumulate(pl.program_id(0) == 0, [loss_ref], [jnp.broadcast_to(part, (8, 128))])

    return pl.pallas_call(
        body, name="loss_fwd", grid=(s // ts,),
        in_specs=[pl.BlockSpec((ts, D_MODEL), lambda i: (i, 0))] * 2,
        out_specs=[pl.BlockSpec((ts, D_MODEL), lambda i: (i, 0)), _const((8, 128))],
        out_shape=[jax.ShapeDtypeStruct((s, D_MODEL), f32), jax.ShapeDtypeStruct((8, 128), f32)],
    )(y, target)


def mlp_bwd(h, dy, g2, wgu, wd, job=None):
    s = h.shape[0]
    ts = min(MLP_BWD_ROWS, s)
    nj = N_DEV // 2
    ni = s // ts

    def body(h_ref, dy_ref, g2_ref, wg_ref, wu_ref, wd_ref, dh_ref, dwg_ref, dwu_ref, dwd_ref, ag_sc, au_sc, ad_sc):
        i = pl.program_id(1)
        wg, wu, wd = wg_ref[...], wu_ref[...], wd_ref[...]
        act, vjp = jax.vjp(lambda h_, sg, su: _f_gate_up(h_, wg, wu, sg, su), h_ref[...].astype(f32),
                           jnp.zeros(wg.shape, f32), jnp.zeros(wu.shape, f32))
        dmix = dy_ref[...] * g2_ref[...]
        dact = _dot(dmix, wd, 1, 1)
        dwd = _dot(act, dmix, 0, 0)
        dh, dwg, dwu = vjp(dact)
        dh_ref[...] = dh.astype(bf16)
        _accumulate_then_cast(i == 0, i == ni - 1, [ag_sc, au_sc, ad_sc], [dwg_ref, dwu_ref, dwd_ref], [dwg, dwu, dwd])

    once = pl.Buffered(1)
    wspec = lambda off: pl.BlockSpec((None, FF_SHARD, D_MODEL), lambda j, i: (j + off, 0, 0), pipeline_mode=once)
    dspec = pl.BlockSpec((None, FF_SHARD, D_MODEL), lambda j, i: (j, 0, 0), pipeline_mode=once)
    return _call_with_job(
        body, "mlp_bwd" if job is None else "mlp_bwd_comm", (nj, ni), job,
        in_specs=[pl.BlockSpec((ts, D_MODEL), lambda j, i: (i, 0)), pl.BlockSpec((ts, D_MODEL), lambda j, i: (i, 0)),
                  _const((1, D_MODEL)), wspec(0), wspec(nj), dspec],
        out_specs=[pl.BlockSpec((None, ts, D_MODEL), lambda j, i: (j, i, 0)), wspec(0), wspec(0), dspec],
        out_shape=[jax.ShapeDtypeStruct((nj, s, D_MODEL), bf16),
                   jax.ShapeDtypeStruct((nj, FF_SHARD, D_MODEL), bf16), jax.ShapeDtypeStruct((nj, FF_SHARD, D_MODEL), bf16),
                   jax.ShapeDtypeStruct((nj, FF_SHARD, D_MODEL), bf16)],
        scratch_shapes=[pltpu.VMEM((FF_SHARD, D_MODEL), f32), pltpu.VMEM((FF_SHARD, D_MODEL), f32),
                        pltpu.VMEM((FF_SHARD, D_MODEL), f32)],
        operands=(h, dy, g2, wgu, wgu, wd))


def out_bwd(dy, dhparts, x, nw, sh, sc, mix, o, yg, g1, wo):
    s = dy.shape[0]
    ts = _token_block(s)
    nj = dhparts.shape[0]

    ni = s // ts

    def body(dy_ref, dp_ref, x_ref, nw_ref, sh_ref, sc_ref, mix_ref, o_ref, yg_ref, g1_ref, wo_ref,
             dx_ref, dnw_ref, dsh_ref, dsc_ref, do_ref, dyg_ref, dg1_ref, dg2_ref, dwo_ref, acc_sc):
        i = pl.program_id(0)
        g = dy_ref[...]
        _accumulate(i == 0, [dg2_ref], [jnp.sum(g * mix_ref[...], axis=0, keepdims=True)])
        dh = dp_ref[0].astype(f32)
        for j in range(1, nj):
            dh = dh + dp_ref[j].astype(f32)
        _, vjp_mod = jax.vjp(_f_modulate, x_ref[...], nw_ref[...], sh_ref[...], sc_ref[...])
        dx_mod, dnw, dsh, dsc = vjp_mod(dh)
        _accumulate(i == 0, [dnw_ref, dsh_ref, dsc_ref], [dnw, dsh, dsc])
        g = g + dx_mod
        dx_ref[...] = g
        o = o_ref[...]
        wo = wo_ref[...]
        _, vjp = jax.vjp(lambda o_, yg_, g1_, slot: _f_out(o_, yg_, g1_, wo, slot), o, yg_ref[...], g1_ref[...],
                         jnp.zeros(wo.shape, f32))
        do, dyg, dg1, dwo = vjp(g)
        delta = jnp.sum(do * o, axis=-1, keepdims=True)
        high = delta.astype(bf16)
        low = (delta - high.astype(f32)).astype(bf16)
        lane = lax.broadcasted_iota(jnp.int32, (1, 1, HEAD_LANES), 2)
        wide = jnp.concatenate([do.astype(bf16), jnp.zeros(do.shape, bf16)], axis=-1)
        do_ref[...] = jnp.where(lane == SPARE_V, -high, jnp.where(lane == SPARE_V + 1, -low, wide))
        dyg_ref[...] = dyg
        _accumulate(i == 0, [dg1_ref], [dg1])
        _accumulate_then_cast(i == 0, i == ni - 1, [acc_sc], [dwo_ref], [dwo])

    head = pl.BlockSpec((N_HEADS, ts, V_DIM), lambda i: (0, i, 0))
    tok = pl.BlockSpec((ts, D_MODEL), lambda i: (i, 0))
    vec = _const((1, D_MODEL))
    vshape = jax.ShapeDtypeStruct((1, D_MODEL), f32)
    return pl.pallas_call(
        body, name="out_bwd", grid=(ni,), scratch_shapes=[pltpu.VMEM((D_MODEL, D_MODEL), f32)],
        in_specs=[tok, pl.BlockSpec((nj, ts, D_MODEL), lambda i: (0, i, 0)), tok, vec, vec, vec, tok,
                  head, pl.BlockSpec((ts, D_SSD), lambda i: (i, 0)), vec, _const((D_MODEL, D_MODEL))],
        out_specs=[tok, vec, vec, vec, pl.BlockSpec((N_HEADS, ts, HEAD_LANES), lambda i: (0, i, 0)),
                   pl.BlockSpec((ts, D_SSD), lambda i: (i, 0)), vec, vec, _const((D_MODEL, D_MODEL))],
        out_shape=[jax.ShapeDtypeStruct((s, D_MODEL), f32), vshape, vshape, vshape,
                   jax.ShapeDtypeStruct((N_HEADS, s, HEAD_LANES), bf16), jax.ShapeDtypeStruct((s, D_SSD), f32),
                   vshape, vshape, jax.ShapeDtypeStruct((D_MODEL, D_MODEL), bf16)],
    )(dy, dhparts, x, nw, sh, sc, mix, o, yg, g1, wo)


def attn_bwd(qx, k, v, do, job=None):
    s = qx.shape[1]
    t = _token_block(s)
    nb = s // t

    hp = ATTN_HEADS_BWD

    def body(q_ref, k_ref, v_ref, do_ref, dq_ref, dk_ref, dv_ref, dv_sc):
        ki = pl.program_id(1)

        @pl.when(ki == 0)
        def _():
            dq_ref[...] = jnp.zeros(dq_ref.shape, f32)

        dk_ref[...] = jnp.zeros(dk_ref.shape, f32)
        dv_sc[...] = jnp.zeros(dv_sc.shape, f32)

        def step(q0, diagonal):
            rows = pl.ds(q0, t)

            def products(hh):
                sc = _scores(q_ref[hh, rows, :], k_ref[hh])
                dps = _scores(do_ref[hh, rows, :], v_ref[hh])
                return (jnp.where(_tril(t, t, 0), sc, NEG) if diagonal else sc), dps

            ahead = products(0)
            for hh in range(hp):
                sc, dps = ahead
                if hh + 1 < hp:
                    ahead = products(hh + 1)
                p = jnp.exp2(sc)
                ds = (p * dps).astype(bf16)
                dv_sc[hh] += lax.dot_general(p.astype(bf16), do_ref[hh, rows, :], (((0,), (0,)), ((), ())),
                                             preferred_element_type=f32)
                dk_ref[hh] += lax.dot_general(ds, q_ref[hh, rows, :], (((0,), (0,)), ((), ())), preferred_element_type=f32)
                dq_ref[hh, rows, :] += jnp.dot(ds, k_ref[hh], preferred_element_type=f32)

        step(pl.multiple_of(ki * t, t), True)

        def above(qi, carry):
            step(pl.multiple_of(qi * t, t), False)
            return carry

        lax.fori_loop(ki + 1, nb, above, 0)
        real = lax.broadcasted_iota(jnp.int32, (1, 1, HEAD_LANES), 2) < SPARE_Q
        dk_ref[...] = jnp.where(real, dk_ref[...] * LN2, 0.0)
        dv_ref[...] = dv_sc[:, :, :V_DIM]

        @pl.when(ki == nb - 1)
        def _():
            dq_ref[...] = jnp.where(real, dq_ref[...] * LN2, 0.0)

    qspec = pl.BlockSpec((hp, s, HEAD_LANES), lambda h, ki: (h, 0, 0))
    kspec = lambda w: pl.BlockSpec((hp, t, w), lambda h, ki: (h, ki, 0))
    return _call_with_job(
        body, "attn_bwd" if job is None else "attn_bwd_comm", (N_HEADS // hp, nb), job,
        in_specs=[qspec, kspec(HEAD_LANES), kspec(HEAD_LANES), qspec],
        out_specs=[qspec, kspec(HEAD_LANES), kspec(V_DIM)],
        out_shape=[jax.ShapeDtypeStruct((N_HEADS, s, HEAD_LANES), f32), jax.ShapeDtypeStruct((N_HEADS, s, HEAD_LANES), f32),
                   jax.ShapeDtypeStruct((N_HEADS, s, V_DIM), f32)],
        scratch_shapes=[pltpu.VMEM((hp, t, HEAD_LANES), f32)], operands=(qx, k, v, do))


def ssd_bwd(px, pz, plast, states, dyg, params):
    s = px.shape[0]
    nc = s // CHUNK
    per = CHUNK // HALO

    def body(px_ref, halo_ref, pz_ref, pl_ref, st_ref, dyg_ref, cw_ref, cb_ref, dtb_ref, alog_ref, dskip_ref, snw_ref,
             dpx_ref, dpz_ref, dpl_ref, dcw_ref, dcb_ref, ddtb_ref, dalog_ref, ddskip_ref, dsnw_ref, dstate_sc, dhalo_sc):
        t = pl.program_id(0)
        chunk = nc - 1 - t

        @pl.when(t == 0)
        def _():
            dstate_sc[...] = jnp.zeros(dstate_sc.shape, f32)
            dhalo_sc[...] = jnp.zeros(dhalo_sc.shape, f32)

        halo = jnp.where(chunk > 0, halo_ref[...], 0.0)
        xext = jnp.concatenate([halo, px_ref[...]], axis=0)
        _, vjp = jax.vjp(_f_ssd, xext, pz_ref[...], pl_ref[...], st_ref[...], cw_ref[...], cb_ref[...], dtb_ref[...],
                         alog_ref[...], dskip_ref[...], snw_ref[...])
        dxext, dz, dpl, dprev, dcw, dcb, ddtb, dalog, ddskip, dsnw = vjp((dyg_ref[...], dstate_sc[...]))
        dpx_ref[...] = dxext[HALO:]
        dpx_ref[CHUNK - HALO:, :] += dhalo_sc[...]
        dhalo_sc[...] = dxext[:HALO]
        dstate_sc[...] = dprev
        dpz_ref[...] = dz
        dpl_ref[...] = dpl
        _accumulate(t == 0, [dcw_ref, dcb_ref, ddtb_ref, dalog_ref, ddskip_ref, dsnw_ref],
                    [dcw, dcb, ddtb, dalog, ddskip, dsnw])

    rev = lambda w: pl.BlockSpec((CHUNK, w), lambda t: (nc - 1 - t, 0))
    pshapes = [jax.ShapeDtypeStruct((4, D_CONV), f32), jax.ShapeDtypeStruct((1, D_CONV), f32),
               jax.ShapeDtypeStruct((1, 128), f32), jax.ShapeDtypeStruct((1, 128), f32),
               jax.ShapeDtypeStruct((1, 128), f32), jax.ShapeDtypeStruct((1, D_SSD), f32)]
    return pl.pallas_call(
        body, name="ssd_bwd", grid=(nc,),
        in_specs=[rev(D_CONV),
                  pl.BlockSpec((HALO, D_CONV), lambda t: (jnp.maximum((nc - 1 - t) * per - 1, 0), 0)),
                  rev(D_SSD), rev(128),
                  pl.BlockSpec((None, N_HEADS // 2, 2 * SSD_HEAD_DIM, SSD_STATE), lambda t: (nc - 1 - t, 0, 0, 0)),
                  rev(D_SSD)] + _ssd_param_specs(),
        out_specs=[rev(D_CONV), rev(D_SSD), rev(128)] + _ssd_param_specs(),
        out_shape=[jax.ShapeDtypeStruct((s, D_CONV), f32), jax.ShapeDtypeStruct((s, D_SSD), f32),
                   jax.ShapeDtypeStruct((s, 128), f32)] + pshapes,
        scratch_shapes=[pltpu.VMEM((N_HEADS // 2, 2 * SSD_HEAD_DIM, SSD_STATE), f32), pltpu.VMEM((HALO, D_CONV), f32)],
    )(px, px, pz, plast, states, dyg, *params)


def qkv_bwd(pa, plast, cos_t, sin_t, params, dq, dk, dv):
    s = pa.shape[0]
    ts = _token_block(s)

    def body(pa_ref, pl_ref, cos_ref, sin_ref, *rest):
        qaw, kvaw, wq, wk, wv, qnw, knw, kpw = [r[...] for r in rest[:8]]
        dq_ref, dk_ref, dv_ref = rest[8:11]
        dpa_ref, dpl_ref = rest[11:13]
        dprm_refs = list(rest[13:])
        cos_t, sin_t = cos_ref[...], sin_ref[...]

        def stage(pa_, pl_, qaw_, kvaw_, sq, sk, sv, qnw_, knw_, kpw_):
            return _f_qkv(pa_, pl_, cos_t, sin_t, qaw_, kvaw_, wq, wk, wv, qnw_, knw_, kpw_, (sq, sk, sv))

        _, vjp = jax.vjp(stage, pa_ref[...], pl_ref[...], qaw, kvaw, jnp.zeros(wq.shape, f32), jnp.zeros(wk.shape, f32),
                         jnp.zeros(wv.shape, f32), qnw, knw, kpw)
        grads = vjp((dq_ref[...], dk_ref[...], dv_ref[...]))
        dpa_ref[...] = grads[0]
        dpl_ref[...] = grads[1]
        _accumulate(pl.program_id(0) == 0, dprm_refs, list(grads[2:]))

    tok = lambda w: pl.BlockSpec((ts, w), lambda i: (i, 0))
    head = lambda w: pl.BlockSpec((N_HEADS, ts, w), lambda i: (0, i, 0))
    pshapes = [jax.ShapeDtypeStruct((1, Q_RANK), f32), jax.ShapeDtypeStruct((1, KV_RANK), f32),
               jax.ShapeDtypeStruct((N_HEADS, Q_RANK, HEAD_LANES), f32), jax.ShapeDtypeStruct((N_HEADS, KV_RANK, HEAD_LANES), f32),
               jax.ShapeDtypeStruct((N_HEADS, KV_RANK, V_DIM), f32), jax.ShapeDtypeStruct((1, HEAD_LANES), f32),
               jax.ShapeDtypeStruct((1, HEAD_LANES), f32), jax.ShapeDtypeStruct((1, HEAD_LANES), f32)]
    return pl.pallas_call(
        body, name="qkv_bwd", grid=(s // ts,),
        in_specs=[tok(384), tok(128), tok(128), tok(128)] + _qkv_param_specs()
                 + [head(HEAD_LANES), head(HEAD_LANES), head(V_DIM)],
        out_specs=[tok(384), tok(128)] + _qkv_param_specs(),
        out_shape=[jax.ShapeDtypeStruct((s, 384), f32), jax.ShapeDtypeStruct((s, 128), f32)] + pshapes,
    )(pa, plast, cos_t, sin_t, *params, dq, dk, dv)


def proj_bwd(x, nw, sh, sc, w, dpa, dpz, dpx, dpl_k, dpl_dt, dres):
    s = x.shape[0]
    ts = _token_block(s)

    ni = s // ts

    def body(x_ref, nw_ref, sh_ref, sc_ref, w_ref, dpa_ref, dpz_ref, dpx_ref, dplk_ref, dpld_ref, dres_ref,
             dx_ref, dnw_ref, dsh_ref, dsc_ref, dw_ref, acc_sc):
        i = pl.program_id(0)
        g = jnp.concatenate([dpa_ref[...], dpz_ref[...], dpx_ref[...], dplk_ref[...] + dpld_ref[...]], axis=1)
        w = w_ref[...]
        _, vjp = jax.vjp(lambda x_, nw_, sh_, sc_, slot: _f_proj(x_, nw_, sh_, sc_, w, slot), x_ref[...], nw_ref[...],
                         sh_ref[...], sc_ref[...], jnp.zeros(w.shape, f32))
        dx, dnw, dsh, dsc, dw = vjp(g)
        dx_ref[...] = dx + dres_ref[...]
        _accumulate(i == 0, [dnw_ref, dsh_ref, dsc_ref], [dnw, dsh, dsc])
        _accumulate_then_cast(i == 0, i == ni - 1, [acc_sc], [dw_ref], [dw])

    vec = _const((1, D_MODEL))
    vshape = jax.ShapeDtypeStruct((1, D_MODEL), f32)
    tok = lambda w_: pl.BlockSpec((ts, w_), lambda i: (i, 0))
    return pl.pallas_call(
        body, name="proj_bwd", grid=(ni,), scratch_shapes=[pltpu.VMEM((D_PROJ, D_MODEL), f32)],
        in_specs=[tok(D_MODEL), vec, vec, vec, _const((D_PROJ, D_MODEL)), tok(384), tok(512), tok(1024), tok(128), tok(128),
                  tok(D_MODEL)],
        out_specs=[tok(D_MODEL), vec, vec, vec, _const((D_PROJ, D_MODEL))],
        out_shape=[jax.ShapeDtypeStruct((s, D_MODEL), f32), vshape, vshape, vshape,
                   jax.ShapeDtypeStruct((D_PROJ, D_MODEL), bf16)],
    )(x, nw, sh, sc, w, dpa, dpz, dpx, dpl_k, dpl_dt, dres)


def ada_fwd(c_all, w_ada, b_cols):
    def body(c_ref, w_ref, b_ref, out_ref):
        act = jax.nn.silu(c_ref[...])
        for l in range(2):
            out_ref[l] = jnp.dot(act, w_ref[l], precision=lax.Precision.HIGHEST, preferred_element_type=f32) + b_ref[l]

    return pl.pallas_call(body, name="ada_fwd", out_shape=jax.ShapeDtypeStruct((2, N_DEV, 768), f32))(c_all, w_ada, b_cols)


def ada_bwd(c_all, dmod_cols):
    def body(c_ref, d_ref, out_ref):
        out_ref[0] = lax.dot_general(jax.nn.silu(c_ref[...]), d_ref[0], (((0,), (0,)), ((), ())),
                                     precision=lax.Precision.HIGHEST, preferred_element_type=f32)

    return pl.pallas_call(
        body, name="ada_bwd", grid=(2,),
        in_specs=[_const((N_DEV, D_MODEL)), pl.BlockSpec((1, N_DEV, 768), lambda l: (l, 0, 0))],
        out_specs=pl.BlockSpec((1, D_MODEL, 768), lambda l: (l, 0, 0)),
        out_shape=jax.ShapeDtypeStruct((2, D_MODEL, 768), f32),
    )(c_all, dmod_cols)


def _adamw(w, g, m, v):
    m = ADAM_B1 * m + (1.0 - ADAM_B1) * g
    v = ADAM_B2 * v + (1.0 - ADAM_B2) * (g * g)
    m_hat = m / (1.0 - ADAM_B1 ** ADAM_STEP)
    v_hat = v / (1.0 - ADAM_B2 ** ADAM_STEP)
    delta = -ADAM_LR * (m_hat / (jnp.sqrt(v_hat) + ADAM_EPS) + ADAM_WD * w)
    return delta, m, v


def adamw(parts, w, m, v, layer, prev, name):
    n, r, c = parts.shape
    nl = w.shape[0]
    per_elem = 2 * (n * parts.dtype.itemsize + 7 * 4)
    lanes = -(-c // 128) * 128
    tr, tc = r, c
    if per_elem * r * lanes > ADAMW_BLOCK_BYTES:
        fits = [t for t in (256, 128, 64, 32, 16, 8) if r % t == 0]
        if fits:
            tr = fits[0]
        else:
            tc = next(t for t in (512, 256, 128) if c % t == 0)

    def body(p_ref, w_ref, m_ref, v_ref, *rest):
        g_ref, d_ref, nm_ref, nv_ref = rest[-4:]
        g = p_ref[0].astype(f32)
        for k in range(1, n):
            g = g + p_ref[k].astype(f32)
        delta, nm, nv = _adamw(w_ref[...], g, m_ref[...], v_ref[...])
        g_ref[...] = g
        d_ref[...] = delta
        nm_ref[...] = nm
        nv_ref[...] = nv

    blk = pl.BlockSpec((None, tr, tc), lambda i, j: (layer, i, j))
    shp = jax.ShapeDtypeStruct((nl, r, c), f32)
    kept = [] if prev is None else list(prev)
    return pl.pallas_call(
        body, name=name, grid=(r // tr, c // tc),
        in_specs=[pl.BlockSpec((n, tr, tc), lambda i, j: (0, i, j)), blk, blk, blk] + [ANY] * len(kept),
        out_specs=[blk] * 4, out_shape=[shp] * 4,
        input_output_aliases={4 + j: j for j in range(len(kept))},
    )(parts, w, m, v, *kept)


def _my_index():
    return 4 * lax.axis_index("x") + 2 * lax.axis_index("y") + lax.axis_index("c")


def _coords(idx):
    return (idx // 4, (idx // 2) % 2, idx % 2)


class CommJob:
    def __init__(self, operands, out_shape, phases, scratch):
        self.operands, self.out_shape, self.phases, self.scratch = operands, out_shape, phases, scratch


def _wait(out, n_blocks, send_sem, recv_sem, send=True, recv=True):
    span = out.at[pl.ds(0, n_blocks)]
    desc = pltpu.make_async_remote_copy(src_ref=span, dst_ref=span, send_sem=send_sem, recv_sem=recv_sem,
                                        device_id=_coords(_my_index()), device_id_type=MESH)
    if recv:
        desc.wait_recv()
    if send:
        desc.wait_send()


def gather_job(shards):
    n = len(shards)

    def places():
        x, y, c = lax.axis_index("x"), lax.axis_index("y"), lax.axis_index("c")
        return (x, y, c), (x, y, 1 - c), [(1 - x, y), (x, 1 - y), (1 - x, 1 - y)]

    def index(p):
        return 4 * p[0] + 2 * p[1] + p[2]

    def start(ins, outs, sems):
        far_send, far_recv, near_send, near_recv, local = sems
        me, sibling, chips = places()
        for k in range(n):
            pltpu.make_async_copy(ins[k], outs[k].at[index(me)], local.at[k]).start()
            for chip in chips:
                pltpu.make_async_remote_copy(src_ref=ins[k], dst_ref=outs[k].at[index(me)], send_sem=far_send.at[k],
                                             recv_sem=far_recv.at[k], device_id=(*chip, me[2]), device_id_type=MESH).start()
            pltpu.make_async_remote_copy(src_ref=ins[k], dst_ref=outs[k].at[index(me)], send_sem=near_send.at[k],
                                         recv_sem=near_recv.at[k], device_id=sibling, device_id_type=MESH).start()

    def relay(ins, outs, sems):
        far_send, far_recv, near_send, near_recv, local = sems
        me, sibling, chips = places()
        for k in range(n):
            _wait(outs[k], 3, far_send.at[k], far_recv.at[k], send=False)
            for chip in chips:
                block = outs[k].at[index((*chip, me[2]))]
                pltpu.make_async_remote_copy(src_ref=block, dst_ref=block, send_sem=near_send.at[k],
                                             recv_sem=near_recv.at[k], device_id=sibling, device_id_type=MESH).start()

    def finish(ins, outs, sems):
        far_send, far_recv, near_send, near_recv, local = sems
        for k in range(n):
            _wait(outs[k], 4, near_send.at[k], near_recv.at[k])
            _wait(outs[k], 3, far_send.at[k], far_recv.at[k], recv=False)
            pltpu.make_async_copy(ins[k], outs[k].at[0], local.at[k]).wait()

    shapes = [jax.ShapeDtypeStruct((N_DEV,) + tuple(a.shape), a.dtype) for a in shards]
    return CommJob(list(shards), shapes, [start, relay, finish], [pltpu.SemaphoreType.DMA((n,))] * 5)


def scatter_job(tensors):
    n = len(tensors)
    flat, where = [], {}
    for k, pieces in enumerate(tensors):
        d = 0
        for piece in pieces:
            for b in range(piece.shape[0]):
                where[k, d] = (len(flat), b)
                d += 1
            flat.append(piece)
        assert d == N_DEV

    def start(ins, outs, sems):
        send_sems, recv_sems, local_sems = sems
        me = _my_index()

        def block(k, d):
            i, b = where[k, d]
            return ins[i].at[b]

        for d in range(N_DEV):
            @pl.when(d != me)
            def _():
                for k in range(n):
                    pltpu.make_async_remote_copy(src_ref=block(k, d), dst_ref=outs[k].at[me], send_sem=send_sems.at[k],
                                                 recv_sem=recv_sems.at[k], device_id=(d // 4, (d // 2) % 2, d % 2),
                                                 device_id_type=MESH).start()

            @pl.when(d == me)
            def _():
                for k in range(n):
                    pltpu.make_async_copy(block(k, d), outs[k].at[d], local_sems.at[k]).start()

    def finish(ins, outs, sems):
        send_sems, recv_sems, local_sems = sems
        for k in range(n):
            _wait(outs[k], N_DEV - 1, send_sems.at[k], recv_sems.at[k])
            i, b = where[k, 0]
            pltpu.make_async_copy(ins[i].at[b], outs[k].at[0], local_sems.at[k]).wait()

    shapes = [jax.ShapeDtypeStruct((N_DEV,) + tuple(p[0].shape[1:]), p[0].dtype) for p in tensors]
    return CommJob(flat, shapes, [start, finish], [pltpu.SemaphoreType.DMA((n,))] * 3)


def comm_call(job, name):
    ni, no = len(job.operands), len(job.out_shape)

    def body(*refs):
        ins, outs, sems = refs[:ni], refs[ni:ni + no], refs[ni + no:]
        for phase in job.phases:
            phase(ins, outs, sems)

    return pl.pallas_call(body, name=name, in_specs=[ANY] * ni, out_specs=[ANY] * no, out_shape=job.out_shape,
                          scratch_shapes=job.scratch)(*job.operands)


def _carry(job, body, n_in, n_out, at_step):
    ji, jo, js = len(job.operands), len(job.out_shape), len(job.scratch)

    def carrier(*refs):
        a, b = n_in, n_in + ji
        c, d = b + n_out, b + n_out + jo
        e = len(refs) - js
        job_refs = (refs[a:b], refs[c:d], refs[e:])
        n = len(job.phases)

        @pl.when(at_step(0, n))
        def _():
            job.phases[0](*job_refs)

        body(*refs[:a], *refs[b:c], *refs[d:e])

        for i in range(1, n):
            @pl.when(at_step(i, n))
            def _():
                job.phases[i](*job_refs)

    return carrier


def _pad_lanes(v, lo, total=128):
    return jnp.pad(v, (lo, total - lo - v.shape[0]))[None, :]


MIXER_WEIGHTS = ("w_in", "w_q_up", "w_kv_up", "conv_w")
LATE_WEIGHTS = ("w_out", "w_gate_up", "w_down")


def mixer_operands(g, sw):
    w_in = g["w_in"].reshape(D_IN, D_MODEL)
    zero = lambda rows: jnp.zeros((rows, D_MODEL), w_in.dtype)
    w_proj = jnp.concatenate(
        [w_in[:384], w_in[416:928], w_in[928:1952], w_in[1952:1960], zero(56), w_in[384:416], zero(32)], axis=0)
    wq = jnp.pad(g["w_q_up"], ((0, 0), (0, 0), (0, HEAD_LANES - NOPE - ROPE)))
    wk = jnp.pad(g["w_kv_up"][:, :, :NOPE], ((0, 0), (0, 0), (0, HEAD_LANES - NOPE)))
    wv = g["w_kv_up"][:, :, NOPE:]
    qkv = (sw["q_a_norm_w"][None, :], sw["kv_a_norm_w"][None, :], wq, wk, wv,
           _pad_lanes(jnp.concatenate([sw["q_nope_norm_w"], sw["q_pe_norm_w"]]), 0),
           _pad_lanes(sw["k_nope_norm_w"], 0), _pad_lanes(sw["k_pe_norm_w"], NOPE))
    conv_w = g["conv_w"].astype(f32).transpose(1, 0, 2).reshape(4, D_CONV)
    ssd = (conv_w, sw["conv_b"][None, :], _pad_lanes(sw["dt_bias"], 0), _pad_lanes(sw["a_log"], 0),
           _pad_lanes(sw["d_skip"], 0), sw["ssd_norm_w"][None, :])
    return dict(w_proj=w_proj, qkv=qkv, ssd=ssd, n1=sw["norm1_w"][None, :])


def late_operands(g, sw):
    return dict(wo=g["w_out"].reshape(D_MODEL, D_MODEL), wgu=g["w_gate_up"],
                wd=g["w_down"].reshape(N_DEV // 2, FF_SHARD, D_MODEL), n2=sw["norm2_w"][None, :])


def layer_fwd(x, mod, kw, cos_t, sin_t, job=None, late=None):
    sh1, sc1, g1, sh2, sc2, g2 = [mod[i:i + 1] for i in range(6)]
    pa, pz, px, plast = proj_fwd(x, kw["n1"], sh1, sc1, kw["w_proj"])
    q, k, v = qkv_fwd(pa, plast, cos_t, sin_t, kw["qkv"])
    (o, qx), carried = attn_fwd(q, k, v, job)
    if late is not None:
        kw = {**kw, **late(carried)}
    yg, states = ssd_fwd(px, pz, plast, kw["ssd"])
    x_mid = out_fwd(x, o, yg, g1, kw["wo"])
    x_out, mix, h_mid = mlp_fwd(x_mid, kw["n2"], sh2, sc2, g2, kw["wgu"], kw["wd"])
    saved = dict(x=x, pa=pa, pz=pz, px=px, plast=plast, qx=qx, k=k, v=v, o=o, yg=yg, states=states, x_mid=x_mid,
                 mix=mix, h_mid=h_mid)
    return x_out, saved, kw, carried


def layer_bwd_head(dy, mod, kw, sv, job=None):
    _, _, g1, sh2, sc2, g2 = [mod[i:i + 1] for i in range(6)]
    (dhparts, dwg, dwu, dwd), carried = mlp_bwd(sv["h_mid"], dy, g2, kw["wgu"], kw["wd"], job)
    dmid, dn2, dsh2, dsc2, do, dyg, dg1, dg2, dwo = out_bwd(
        dy, dhparts, sv["x_mid"], kw["n2"], sh2, sc2, sv["mix"], sv["o"], sv["yg"], g1, kw["wo"])
    early = dict(w_out=[dwo.reshape(N_DEV, D_MODEL // N_DEV, D_MODEL)], w_gate_up=[dwg, dwu],
                 w_down=[dwd.reshape(N_DEV, D_FF // N_DEV, D_MODEL)])
    head = dict(dmid=dmid, do=do, dyg=dyg, dn2=dn2, dsh2=dsh2, dsc2=dsc2, dg2=dg2, dg1=dg1)
    return head, early, carried


def layer_bwd_tail(hd, mod, kw, cos_t, sin_t, sv, job=None):
    sh1, sc1 = mod[0:1], mod[1:2]
    (dq, dk, dv), carried = attn_bwd(sv["qx"], sv["k"], sv["v"], hd["do"], job)
    dpx, dpz, dpl_dt, dcw, dcb, ddtb, dalog, ddskip, dsnw = ssd_bwd(sv["px"], sv["pz"], sv["plast"], sv["states"],
                                                                   hd["dyg"], kw["ssd"])
    dpa, dpl_k, dqaw, dkvaw, dwq, dwk, dwv, dqnw, dknw, dkpw = qkv_bwd(sv["pa"], sv["plast"], cos_t, sin_t, kw["qkv"],
                                                                       dq, dk, dv)
    dx, dn1, dsh1, dsc1, dwp = proj_bwd(sv["x"], kw["n1"], sh1, sc1, kw["w_proj"], dpa, dpz, dpx, dpl_k, dpl_dt, hd["dmid"])
    dmod = jnp.concatenate([dsh1, dsc1, hd["dg1"], hd["dsh2"], hd["dsc2"], hd["dg2"]], axis=0)
    dw_in = jnp.concatenate([dwp[:384], dwp[1984:2016], dwp[384:1920], dwp[1920:1928]], axis=0)
    grads = dict(
        norm1_w=dn1[0], norm2_w=hd["dn2"][0], q_a_norm_w=dqaw[0], kv_a_norm_w=dkvaw[0],
        q_nope_norm_w=dqnw[0, :NOPE], q_pe_norm_w=dqnw[0, NOPE:NOPE + ROPE], k_nope_norm_w=dknw[0, :NOPE],
        k_pe_norm_w=dkpw[0, NOPE:NOPE + ROPE], conv_b=dcb[0], dt_bias=ddtb[0, :N_HEADS], a_log=dalog[0, :N_HEADS],
        d_skip=ddskip[0, :N_HEADS], ssd_norm_w=dsnw[0],
        w_in=[dw_in.reshape(N_DEV, D_IN // N_DEV, D_MODEL)],
        w_q_up=[dwq[:, :, :NOPE + ROPE].astype(bf16)],
        w_kv_up=[jnp.concatenate([dwk[:, :, :NOPE], dwv], axis=2).astype(bf16)],
        conv_w=[dcw.reshape(4, N_DEV, D_CONV // N_DEV).transpose(1, 0, 2).astype(bf16)],
    )
    return dx, dmod, grads, carried


def _pack_small(get, last=None):
    flat = jnp.concatenate([get(name).reshape(-1) for name, _ in SMALL])
    flat = jnp.pad(flat, (0, SMALL_ROWS * 128 - flat.shape[0]))
    if last is not None:
        flat = flat.at[-1].set(last)
    return flat.reshape(SMALL_ROWS, 128)


def _unpack_small(packed):
    flat = packed.reshape(-1)
    out, off = {}, 0
    for name, size in SMALL:
        out[name] = flat[off:off + 2 * size].reshape(2, size)
        off += 2 * size
    return out


def kernel(x, c, positions, norm1_w, norm2_w, w_ada, b_ada, w_in, q_a_norm_w, w_q_up, kv_a_norm_w, w_kv_up, q_nope_norm_w, q_pe_norm_w, k_nope_norm_w, k_pe_norm_w, conv_w, conv_b, dt_bias, a_log, d_skip, ssd_norm_w, w_out, w_gate_up, w_down, loss_target, m_norm1_w, m_norm2_w, m_w_ada, m_b_ada, m_w_in, m_q_a_norm_w, m_w_q_up, m_kv_a_norm_w, m_w_kv_up, m_q_nope_norm_w, m_q_pe_norm_w, m_k_nope_norm_w, m_k_pe_norm_w, m_conv_w, m_conv_b, m_dt_bias, m_a_log, m_d_skip, m_ssd_norm_w, m_w_out, m_w_gate_up, m_w_down, v_norm1_w, v_norm2_w, v_w_ada, v_b_ada, v_w_in, v_q_a_norm_w, v_w_q_up, v_kv_a_norm_w, v_w_kv_up, v_q_nope_norm_w, v_q_pe_norm_w, v_k_nope_norm_w, v_k_pe_norm_w, v_conv_w, v_conv_b, v_dt_bias, v_a_log, v_d_skip, v_ssd_norm_w, v_w_out, v_w_gate_up, v_w_down):
    w = dict(norm1_w=norm1_w, norm2_w=norm2_w, w_ada=w_ada, b_ada=b_ada, w_in=w_in, q_a_norm_w=q_a_norm_w, w_q_up=w_q_up,
             kv_a_norm_w=kv_a_norm_w, w_kv_up=w_kv_up, q_nope_norm_w=q_nope_norm_w, q_pe_norm_w=q_pe_norm_w,
             k_nope_norm_w=k_nope_norm_w, k_pe_norm_w=k_pe_norm_w, conv_w=conv_w, conv_b=conv_b, dt_bias=dt_bias,
             a_log=a_log, d_skip=d_skip, ssd_norm_w=ssd_norm_w, w_out=w_out, w_gate_up=w_gate_up, w_down=w_down)
    m = dict(norm1_w=m_norm1_w, norm2_w=m_norm2_w, w_ada=m_w_ada, b_ada=m_b_ada, w_in=m_w_in, q_a_norm_w=m_q_a_norm_w,
             w_q_up=m_w_q_up, kv_a_norm_w=m_kv_a_norm_w, w_kv_up=m_w_kv_up, q_nope_norm_w=m_q_nope_norm_w,
             q_pe_norm_w=m_q_pe_norm_w, k_nope_norm_w=m_k_nope_norm_w, k_pe_norm_w=m_k_pe_norm_w, conv_w=m_conv_w,
             conv_b=m_conv_b, dt_bias=m_dt_bias, a_log=m_a_log, d_skip=m_d_skip, ssd_norm_w=m_ssd_norm_w, w_out=m_w_out,
             w_gate_up=m_w_gate_up, w_down=m_w_down)
    v = dict(norm1_w=v_norm1_w, norm2_w=v_norm2_w, w_ada=v_w_ada, b_ada=v_b_ada, w_in=v_w_in, q_a_norm_w=v_q_a_norm_w,
             w_q_up=v_w_q_up, kv_a_norm_w=v_kv_a_norm_w, w_kv_up=v_w_kv_up, q_nope_norm_w=v_q_nope_norm_w,
             q_pe_norm_w=v_q_pe_norm_w, k_nope_norm_w=v_k_nope_norm_w, k_pe_norm_w=v_k_pe_norm_w, conv_w=v_conv_w,
             conv_b=v_conv_b, dt_bias=v_dt_bias, a_log=v_a_log, d_skip=v_d_skip, ssd_norm_w=v_ssd_norm_w, w_out=v_w_out,
             w_gate_up=v_w_gate_up, w_down=v_w_down)
    me = _my_index()
    seq = x.shape[1]

    def shard(name, l):
        if name == "conv_w":
            return w[name][l]
        if name in TRANSPOSED:
            return jnp.swapaxes(w[name][l], 0, 1).astype(bf16)
        return w[name][l].astype(bf16)

    def shards(names, l):
        return [shard(name, l) for name in names]

    small = [{name: w[name][l] for name, _ in SMALL if name != "b_ada"} for l in range(2)]
    n_mix, n_late = len(MIXER_WEIGHTS), len(LATE_WEIGHTS)

    first = comm_call(gather_job([c] + shards(MIXER_WEIGHTS, 0)), "gather_first")
    c_all = first[0].reshape(N_DEV, D_MODEL)
    kws = [mixer_operands(dict(zip(MIXER_WEIGHTS, first[1:])), small[0]), None]

    b_cols = lax.dynamic_slice_in_dim(b_ada, me * 768, 768, axis=1)
    mod_cols = ada_fwd(c_all, w_ada, b_cols)
    (mod_all,) = comm_call(gather_job([mod_cols]), "gather_mod")
    mod_me = lax.dynamic_index_in_dim(mod_all, me, axis=2, keepdims=False)
    mods = [mod_me[:, l, :].reshape(6, D_MODEL) for l in range(2)]

    inv_freq = 1.0 / (ROPE_THETA ** (jnp.arange(0, ROPE, 2, dtype=f32) / ROPE))
    inv = _pad_lanes(jnp.concatenate([inv_freq, inv_freq]), NOPE)
    cos_t, sin_t = rope_tables(positions.reshape(seq, 1), inv)

    saved = [None, None]
    h, saved[0], kws[0], got = layer_fwd(
        x[0], mods[0], kws[0], cos_t, sin_t, gather_job(shards(LATE_WEIGHTS, 0) + shards(MIXER_WEIGHTS, 1)),
        lambda got: late_operands(dict(zip(LATE_WEIGHTS, got[:n_late])), small[0]))
    kws[1] = mixer_operands(dict(zip(MIXER_WEIGHTS, got[n_late:])), small[1])
    h, saved[1], kws[1], _ = layer_fwd(
        h, mods[1], kws[1], cos_t, sin_t, gather_job(shards(LATE_WEIGHTS, 1)),
        lambda got: late_operands(dict(zip(LATE_WEIGHTS, got)), small[1]))
    dy, loss_part = loss_fwd(h, loss_target[0])

    early, late = ("w_out", "w_gate_up", "w_down"), ("w_in", "w_q_up", "w_kv_up", "conv_w")
    parts = [{}, {}]
    head, pieces, _ = layer_bwd_head(dy, mods[1], kws[1], saved[1])
    dy, dmod1, grads1, got = layer_bwd_tail(head, mods[1], kws[1], cos_t, sin_t, saved[1], scatter_job([pieces[n] for n in early]))
    parts[1].update(zip(early, got))
    head, pieces, got = layer_bwd_head(dy, mods[0], kws[0], saved[0], scatter_job([grads1[n] for n in late]))
    parts[1].update(zip(late, got))
    dy, dmod0, grads0, got = layer_bwd_tail(head, mods[0], kws[0], cos_t, sin_t, saved[0], scatter_job([pieces[n] for n in early]))
    parts[0].update(zip(early, got))
    parts[0].update(zip(late, comm_call(scatter_job([grads0[n] for n in late]), "scatter_layer0_rest")))
    grad_x = dy[None]

    small_part = {name: jnp.stack([grads0[name], grads1[name]]) for name, _ in SMALL if name != "b_ada"}
    small_part["b_ada"] = jnp.stack([dmod0.reshape(-1), dmod1.reshape(-1)])
    (small_all,) = comm_call(gather_job([_pack_small(lambda n: small_part[n], loss_part[0, 0])]), "gather_small_grads")
    packed = adamw(small_all, _pack_small(lambda n: w[n])[None], _pack_small(lambda n: m[n])[None],
                   _pack_small(lambda n: v[n])[None], 0, None, "adamw_small")
    loss = packed[0][0, -1, -1]
    res = {}
    for key, arr in zip("gdmv", packed):
        for name, val in _unpack_small(arr[0]).items():
            res[key, name] = val

    off = 2 * (1024 + 1024)
    dmod_all = small_all.reshape(N_DEV, -1)[:, off:off + 2 * 6144].reshape(N_DEV, 2, 6144)
    dmod_cols = lax.dynamic_slice_in_dim(dmod_all, me * 768, 768, axis=2).transpose(1, 0, 2)
    g_ada = ada_bwd(c_all, dmod_cols)
    out = None
    for l in range(2):
        out = adamw(g_ada[l][None], w_ada, m_w_ada, v_w_ada, l, out, "adamw_w_ada")
    res.update(zip([(key, "w_ada") for key in "gdmv"], out))

    for name in BIG:
        view = (lambda a: jnp.swapaxes(a, 1, 2)) if name in TRANSPOSED else (lambda a: a)
        out = None
        for l in range(2):
            out = adamw(parts[l][name], view(w[name]), view(m[name]), view(v[name]), l, out, "adamw_" + name)
        res.update(zip([(key, name) for key in "gdmv"], [view(a) for a in out]))

    return (loss, grad_x, *[res["g", n] for n in WEIGHTS], *[res["d", n] for n in WEIGHTS],
            *[res["m", n] for n in WEIGHTS], *[res["v", n] for n in WEIGHTS])
```

```python
import functools

import jax
import jax.numpy as jnp
from jax import lax
from jax.experimental import pallas as pl
from jax.experimental.pallas import tpu as pltpu

f32 = jnp.float32
bf16 = jnp.bfloat16

N_DEV = 8
D_MODEL = 1024
N_HEADS = 8
HEAD_LANES = 128
NOPE = 64
ROPE = 32
V_DIM = 64
Q_RANK = 256
KV_RANK = 128
D_SSD = 512
D_CONV = 1024
SSD_STATE = 128
SSD_HEAD_DIM = 64
CHUNK = 128
HALO = 8
D_FF = 2816
FF_SHARD = 704
D_IN = 1960
D_PROJ = 2048
EPS = 1e-6
LOG2E = 1.4426950408889634
LN2 = 0.6931471805599453
Q_SCALE = (NOPE + ROPE) ** -0.5 * LOG2E
SPARE_Q = NOPE + ROPE
SPARE_V = V_DIM
ATTN_ROWS_FWD = 256
ATTN_HEADS_FWD = 4
ATTN_HEADS_BWD = 4
MLP_BWD_ROWS = 512
ROPE_THETA = 10000.0
NEG = -1e30

ADAM_LR = 0.001
ADAM_B1 = 0.9
ADAM_B2 = 0.999
ADAM_EPS = 1e-08
ADAM_WD = 0.01
ADAM_STEP = 10
ADAMW_BLOCK_BYTES = 24 << 20

MESH = pl.DeviceIdType.MESH
ANY = pl.BlockSpec(memory_space=pl.ANY)

SMALL = (("norm1_w", 1024), ("norm2_w", 1024), ("b_ada", 6144), ("q_a_norm_w", 256), ("kv_a_norm_w", 128),
         ("q_nope_norm_w", 64), ("q_pe_norm_w", 32), ("k_nope_norm_w", 64), ("k_pe_norm_w", 32),
         ("conv_b", 1024), ("dt_bias", 8), ("a_log", 8), ("d_skip", 8), ("ssd_norm_w", 512))
SMALL_ROWS = 168
BIG = ("w_in", "w_q_up", "w_kv_up", "conv_w", "w_out", "w_gate_up", "w_down")
TRANSPOSED = ("w_in", "w_gate_up")
WEIGHTS = ("norm1_w", "norm2_w", "w_ada", "b_ada", "w_in", "q_a_norm_w", "w_q_up", "kv_a_norm_w", "w_kv_up",
           "q_nope_norm_w", "q_pe_norm_w", "k_nope_norm_w", "k_pe_norm_w", "conv_w", "conv_b", "dt_bias",
           "a_log", "d_skip", "ssd_norm_w", "w_out", "w_gate_up", "w_down")


def _dot(a, b, ca, cb):
    return lax.dot_general(a.astype(bf16), b.astype(bf16), (((ca,), (cb,)), ((), ())), preferred_element_type=f32)


@jax.custom_vjp
def mm(a, b):
    return _dot(a, b, 1, 0)


def _mm_fwd(a, b):
    return _dot(a, b, 1, 0), (a, b)


def _mm_bwd(res, g):
    a, b = res
    return _dot(g, b, 1, 1).astype(a.dtype), _dot(a, g, 0, 0).astype(b.dtype)


mm.defvjp(_mm_fwd, _mm_bwd)


@jax.custom_vjp
def _mm_slot(a, w, slot):
    return _dot(a, w, 1, 0)


def _mm_slot_fwd(a, w, slot):
    return _dot(a, w, 1, 0), (a, w)


def _mm_slot_bwd(res, g):
    a, w = res
    return _dot(g, w, 1, 1).astype(a.dtype), None, _dot(a, g, 0, 0)


_mm_slot.defvjp(_mm_slot_fwd, _mm_slot_bwd)


def mmw(a, w, slot=None):
    return _dot(a, w, 1, 0) if slot is None else _mm_slot(a, w, slot)


@jax.custom_vjp
def _mm_slot_t(a, wt, slot):
    return _dot(a, wt, 1, 1)


def _mm_slot_t_fwd(a, wt, slot):
    return _dot(a, wt, 1, 1), (a, wt)


def _mm_slot_t_bwd(res, g):
    a, wt = res
    return _dot(g, wt, 1, 0).astype(a.dtype), None, _dot(g, a, 0, 0)


_mm_slot_t.defvjp(_mm_slot_t_fwd, _mm_slot_t_bwd)


def mmw_t(a, wt, slot=None):
    return _dot(a, wt, 1, 1) if slot is None else _mm_slot_t(a, wt, slot)


@jax.custom_vjp
def mm_nt(a, b):
    return _dot(a, b, 1, 1)


def _mm_nt_fwd(a, b):
    return _dot(a, b, 1, 1), (a, b)


def _mm_nt_bwd(res, g):
    a, b = res
    return _dot(g, b, 1, 0).astype(a.dtype), _dot(g, a, 0, 0).astype(b.dtype)


mm_nt.defvjp(_mm_nt_fwd, _mm_nt_bwd)


@jax.custom_vjp
def mm_tn(a, b):
    return _dot(a, b, 0, 0)


def _mm_tn_fwd(a, b):
    return _dot(a, b, 0, 0), (a, b)


def _mm_tn_bwd(res, g):
    a, b = res
    return _dot(b, g, 1, 1).astype(a.dtype), _dot(a, g, 1, 0).astype(b.dtype)


mm_tn.defvjp(_mm_tn_fwd, _mm_tn_bwd)


def _rms(x, w):
    return x * lax.rsqrt(jnp.mean(x * x, axis=-1, keepdims=True) + EPS) * w


def _const(shape):
    n = len(shape)
    return pl.BlockSpec(shape, lambda *_: (0,) * n)


def _accumulate(first, refs, vals):
    @pl.when(first)
    def _():
        for r, v in zip(refs, vals):
            r[...] = v

    @pl.when(jnp.logical_not(first))
    def _():
        for r, v in zip(refs, vals):
            r[...] += v


def _accumulate_then_cast(first, last, accs, outs, vals):
    _accumulate(first, accs, vals)

    @pl.when(last)
    def _():
        for a, o in zip(accs, outs):
            o[...] = a[...].astype(o.dtype)


def _token_block(s):
    return min(512, s)


def _f_proj(x, nw, sh, sc, w, slot=None):
    h = _rms(x, nw) * (1.0 + sc) + sh
    return mmw_t(h, w, slot)


def _f_qkv(pa, plast, cos_t, sin_t, qaw, kvaw, wq, wk, wv, qnw, knw, kpw, slots=None):
    sq, sk, sv = slots if slots is not None else ([None] * N_HEADS,) * 3
    lane = lax.broadcasted_iota(jnp.int32, (1, HEAD_LANES), 1)
    m_nope = lane < NOPE
    m_pe = (lane >= NOPE) & (lane < NOPE + ROPE)
    rows = pa.shape[0]

    def rope(t):
        half = ROPE // 2
        swapped = jnp.concatenate(
            [jnp.zeros((rows, NOPE), f32), t[:, NOPE + half:NOPE + ROPE], t[:, NOPE:NOPE + half],
             jnp.zeros((rows, HEAD_LANES - NOPE - ROPE), f32)], axis=1)
        return t * cos_t + swapped * sin_t

    qa = _rms(pa[:, :Q_RANK], qaw)
    kva = _rms(pa[:, Q_RANK:Q_RANK + KV_RANK], kvaw)
    kp = jnp.where(m_pe, plast, 0.0)
    kp = kp * lax.rsqrt(jnp.sum(kp * kp, axis=-1, keepdims=True) / ROPE + EPS) * kpw
    k_rot = rope(kp)
    qs, ks, vs = [], [], []
    for h in range(N_HEADS):
        qh = mmw(qa, wq[h], sq[h])
        ss_n = jnp.sum(jnp.where(m_nope, qh * qh, 0.0), axis=-1, keepdims=True) / NOPE
        ss_p = jnp.sum(jnp.where(m_pe, qh * qh, 0.0), axis=-1, keepdims=True) / ROPE
        r = jnp.where(m_nope, lax.rsqrt(ss_n + EPS), lax.rsqrt(ss_p + EPS))
        qs.append(rope(qh * r * qnw) * Q_SCALE)
        kh = mmw(kva, wk[h], sk[h])
        kh = kh * lax.rsqrt(jnp.sum(kh * kh, axis=-1, keepdims=True) / NOPE + EPS) * knw
        ks.append(kh + k_rot)
        vs.append(mmw(kva, wv[h], sv[h]))
    return jnp.stack(qs), jnp.stack(ks), jnp.stack(vs)


def _f_ssd(xext, z, plast, prev, cw, cb, dtb, alog, dskip, snw):
    n = CHUNK
    conv = cb
    for k in range(4):
        conv = conv + cw[k:k + 1] * xext[HALO - 3 + k:HALO - 3 + k + n]
    xc = jax.nn.silu(conv)
    xs, bm, cm = xc[:, :D_SSD], xc[:, D_SSD:D_SSD + 2 * SSD_STATE], xc[:, D_SSD + 2 * SSD_STATE:]
    lane = lax.broadcasted_iota(jnp.int32, (1, 128), 1)
    dt = jax.nn.softplus(jnp.where(lane < N_HEADS, plast, 0.0) + dtb)
    adt = dt * (-jnp.exp(alog))
    row = lax.broadcasted_iota(jnp.int32, (n, n), 0)
    col = lax.broadcasted_iota(jnp.int32, (n, n), 1)
    tri = row >= col
    acs = jnp.dot(tri.astype(f32), adt, precision=lax.Precision.HIGHEST, preferred_element_type=f32)
    acs_t = acs.T
    bgs = [bm[:, g * SSD_STATE:(g + 1) * SSD_STATE] for g in range(2)]
    cgs = [cm[:, g * SSD_STATE:(g + 1) * SSD_STATE] for g in range(2)]
    cb_ts = [mm_nt(cgs[g], bgs[g]) for g in range(2)]
    low = lane < SSD_HEAD_DIM
    low_rows = lax.broadcasted_iota(jnp.int32, (2 * SSD_HEAD_DIM, 1), 0) < SSD_HEAD_DIM

    def both(a0, a1):
        return jnp.where(low, a0, a1)

    pre = []
    for i in range(N_HEADS // 2):
        h0, h1 = 2 * i, 2 * i + 1
        col0, col1 = acs[:, h0:h0 + 1], acs[:, h1:h1 + 1]
        last0, last1 = acs[n - 1:n, h0:h0 + 1], acs[n - 1:n, h1:h1 + 1]
        cb_t = cb_ts[i // 2]
        scores0 = cb_t * jnp.exp(jnp.where(tri, col0 - acs_t[h0:h0 + 1, :], -jnp.inf))
        scores1 = cb_t * jnp.exp(jnp.where(tri, col1 - acs_t[h1:h1 + 1, :], -jnp.inf))
        xp = xs[:, i * 128:(i + 1) * 128]
        xdt = xp * both(dt[:, h0:h0 + 1], dt[:, h1:h1 + 1])
        weighted = xdt * both(jnp.exp(last0 - col0), jnp.exp(last1 - col1))
        chunk_decay = jnp.where(low_rows, jnp.exp(last0), jnp.exp(last1))
        in_decay = both(jnp.exp(col0), jnp.exp(col1))
        skip = both(dskip[:, h0:h0 + 1], dskip[:, h1:h1 + 1]) * xp
        pre.append((scores0, scores1, xdt, weighted, chunk_decay, in_decay, skip))
    prods = []
    for i in range(N_HEADS // 2):
        scores0, scores1, xdt, weighted, _, _, _ = pre[i]
        g = i // 2
        y_diag = mm(scores0, jnp.where(low, xdt, 0.0)) + mm(scores1, jnp.where(low, 0.0, xdt))
        prods.append((y_diag, mm_tn(weighted, bgs[g]), mm_nt(cgs[g], prev[i])))
    ys, news = [], []
    for i in range(N_HEADS // 2):
        y_diag, st, y_off = prods[i]
        _, _, _, _, chunk_decay, in_decay, skip = pre[i]
        news.append(chunk_decay * prev[i] + st)
        ys.append(y_diag + y_off * in_decay + skip)
    y = jnp.concatenate(ys, axis=1)
    yg = y * jax.nn.silu(z)
    half = D_SSD // 2
    outs = []
    for g in range(2):
        t = yg[:, g * half:(g + 1) * half]
        outs.append(t * lax.rsqrt(jnp.mean(t * t, axis=-1, keepdims=True) + EPS))
    return jnp.concatenate(outs, axis=1) * snw, jnp.stack(news)


def _f_out(o, yg, g1, wo, slot=None):
    cat = jnp.concatenate([o[h] for h in range(N_HEADS)] + [yg], axis=1)
    return g1 * mmw(cat, wo, slot)


def _f_modulate(x, nw, sh, sc):
    return _rms(x, nw) * (1.0 + sc) + sh


def proj_fwd(x, nw, sh, sc, w):
    s = x.shape[0]
    ts = _token_block(s)

    def body(x_ref, nw_ref, sh_ref, sc_ref, w_ref, pa_ref, pz_ref, px_ref, pl_ref):
        p = _f_proj(x_ref[...], nw_ref[...], sh_ref[...], sc_ref[...], w_ref[...])
        pa_ref[...] = p[:, :384]
        pz_ref[...] = p[:, 384:896]
        px_ref[...] = p[:, 896:1920]
        pl_ref[...] = p[:, 1920:]

    vec = _const((1, D_MODEL))
    return pl.pallas_call(
        body, name="proj_fwd", grid=(s // ts,),
        in_specs=[pl.BlockSpec((ts, D_MODEL), lambda i: (i, 0)), vec, vec, vec, _const((D_PROJ, D_MODEL))],
        out_specs=[pl.BlockSpec((ts, 384), lambda i: (i, 0)), pl.BlockSpec((ts, 512), lambda i: (i, 0)),
                   pl.BlockSpec((ts, 1024), lambda i: (i, 0)), pl.BlockSpec((ts, 128), lambda i: (i, 0))],
        out_shape=[jax.ShapeDtypeStruct((s, 384), f32), jax.ShapeDtypeStruct((s, 512), f32),
                   jax.ShapeDtypeStruct((s, 1024), f32), jax.ShapeDtypeStruct((s, 128), f32)],
    )(x, nw, sh, sc, w)


def rope_tables(pos, inv):
    s = pos.shape[0]
    ts = _token_block(s)

    def body(pos_ref, inv_ref, cos_ref, sin_ref):
        ang = pos_ref[...].astype(f32) * inv_ref[...]
        lane = lax.broadcasted_iota(jnp.int32, (1, HEAD_LANES), 1)
        half = ROPE // 2
        cos_ref[...] = jnp.where(lane < NOPE, 1.0, jnp.where(lane < NOPE + ROPE, jnp.cos(ang), 0.0))
        sn = jnp.sin(ang)
        sin_ref[...] = jnp.where((lane >= NOPE) & (lane < NOPE + half), -sn,
                                 jnp.where((lane >= NOPE + half) & (lane < NOPE + ROPE), sn, 0.0))

    return pl.pallas_call(
        body, name="rope_tables", grid=(s // ts,),
        in_specs=[pl.BlockSpec((ts, 1), lambda i: (i, 0)), _const((1, HEAD_LANES))],
        out_specs=[pl.BlockSpec((ts, HEAD_LANES), lambda i: (i, 0))] * 2,
        out_shape=[jax.ShapeDtypeStruct((s, HEAD_LANES), f32)] * 2,
    )(pos, inv)


def _qkv_param_specs():
    return [_const((1, Q_RANK)), _const((1, KV_RANK)), _const((N_HEADS, Q_RANK, HEAD_LANES)),
            _const((N_HEADS, KV_RANK, HEAD_LANES)), _const((N_HEADS, KV_RANK, V_DIM)),
            _const((1, HEAD_LANES)), _const((1, HEAD_LANES)), _const((1, HEAD_LANES))]


def qkv_fwd(pa, plast, cos_t, sin_t, params):
    s = pa.shape[0]
    ts = _token_block(s)

    def body(pa_ref, pl_ref, cos_ref, sin_ref, *rest):
        prm = [r[...] for r in rest[:8]]
        q_ref, k_ref, v_ref = rest[8:]
        q, k, v = _f_qkv(pa_ref[...], pl_ref[...], cos_ref[...], sin_ref[...], *prm)
        q_ref[...] = q.astype(bf16)
        lane = lax.broadcasted_iota(jnp.int32, (1, 1, HEAD_LANES), 2)
        k_ref[...] = jnp.where((lane == SPARE_Q) | (lane == SPARE_Q + 1), 1.0, k).astype(bf16)
        v_ref[...] = jnp.concatenate([v, jnp.ones_like(v)], axis=-1).astype(bf16)

    tok = lambda w: pl.BlockSpec((ts, w), lambda i: (i, 0))
    head = pl.BlockSpec((N_HEADS, ts, HEAD_LANES), lambda i: (0, i, 0))
    return pl.pallas_call(
        body, name="qkv_fwd", grid=(s // ts,),
        in_specs=[tok(384), tok(128), tok(128), tok(128)] + _qkv_param_specs(),
        out_specs=[head] * 3, out_shape=[jax.ShapeDtypeStruct((N_HEADS, s, HEAD_LANES), bf16)] * 3,
    )(pa, plast, cos_t, sin_t, *params)


def _scores(q, k):
    return lax.dot_general(q, k, (((1,), (1,)), ((), ())), preferred_element_type=f32)


def _tril(rows, cols, row_offset):
    row = row_offset + lax.broadcasted_iota(jnp.int32, (rows, cols), 0)
    col = lax.broadcasted_iota(jnp.int32, (rows, cols), 1)
    return row >= col


def _call_with_job(body, name, grid, job, in_specs, out_specs, out_shape, scratch_shapes, operands, relay_at=None):
    if job is None:
        res = pl.pallas_call(body, name=name, grid=grid, in_specs=in_specs, out_specs=out_specs, out_shape=out_shape,
                             scratch_shapes=scratch_shapes)(*operands)
        return res, None

    def at_step(i, n):
        if i == 0:
            want = [0] * len(grid)
        elif i == n - 1:
            want = [g - 1 for g in grid]
        else:
            want = relay_at
        return functools.reduce(jnp.logical_and, [pl.program_id(a) == s for a, s in enumerate(want)])

    carrier = _carry(job, body, len(in_specs), len(out_specs), at_step)
    res = pl.pallas_call(
        carrier, name=name, grid=grid,
        in_specs=list(in_specs) + [ANY] * len(job.operands), out_specs=list(out_specs) + [ANY] * len(job.out_shape),
        out_shape=list(out_shape) + list(job.out_shape), scratch_shapes=list(scratch_shapes) + job.scratch,
    )(*operands, *job.operands)
    return res[:len(out_specs)], res[len(out_specs):]


def attn_fwd(q, k, v, job=None):
    s = q.shape[1]
    t = _token_block(s)
    nb = s // t

    rb = min(ATTN_ROWS_FWD, t)

    hp = ATTN_HEADS_FWD

    def body(q_ref, k_ref, v_ref, o_ref, qx_ref, m_sc, acc_sc):
        qi = pl.program_id(1)
        m_sc[...] = jnp.full(m_sc.shape, NEG, f32)
        acc_sc[...] = jnp.zeros(acc_sc.shape, f32)

        def step(k0, diagonal):
            chains = [(hh, r) for hh in range(hp) for r in range(t // rb)]

            def scores(hh, r):
                nk = (r + 1) * rb if diagonal else t
                sc = _scores(q_ref[hh, pl.ds(r * rb, rb), :], k_ref[hh, pl.ds(k0, nk), :])
                return jnp.where(_tril(rb, nk, r * rb), sc, NEG) if diagonal else sc

            ahead = scores(*chains[0])
            for c, (hh, r) in enumerate(chains):
                sc = ahead
                if c + 1 < len(chains):
                    ahead = scores(*chains[c + 1])
                rows = pl.ds(r * rb, rb)
                keys = pl.ds(k0, (r + 1) * rb if diagonal else t)
                m_prev = m_sc[hh, rows, :1]
                m_new = jnp.maximum(m_prev, jnp.max(sc, axis=-1, keepdims=True))
                p = jnp.exp2(sc - m_new)
                alpha = jnp.exp2(m_prev - m_new)
                acc = alpha * acc_sc[hh, rows, :] + jnp.dot(p.astype(bf16), v_ref[hh, keys, :], preferred_element_type=f32)
                if diagonal:
                    l = acc[:, V_DIM:V_DIM + 1]
                    o_ref[hh, rows, :] = acc[:, :V_DIM] / l
                    lse = m_new + jnp.log2(l)
                    high = lse.astype(bf16)
                    low = (lse - high.astype(f32)).astype(bf16)
                    lane = lax.broadcasted_iota(jnp.int32, (1, HEAD_LANES), 1)
                    qx_ref[hh, rows, :] = jnp.where(lane == SPARE_Q, -high,
                                                    jnp.where(lane == SPARE_Q + 1, -low, q_ref[hh, rows, :]))
                else:
                    acc_sc[hh, rows, :] = acc
                    m_sc[hh, rows, :] = jnp.broadcast_to(m_new, (rb, 128))

        def below(ki, carry):
            step(pl.multiple_of(ki * t, t), False)
            return carry

        lax.fori_loop(0, qi, below, 0)
        step(pl.multiple_of(qi * t, t), True)

    return _call_with_job(
        body, "attn_fwd" if job is None else "attn_fwd_comm", (N_HEADS // hp, nb), job,
        in_specs=[pl.BlockSpec((hp, t, HEAD_LANES), lambda h, qi: (h, qi, 0)),
                  pl.BlockSpec((hp, s, HEAD_LANES), lambda h, qi: (h, 0, 0)),
                  pl.BlockSpec((hp, s, HEAD_LANES), lambda h, qi: (h, 0, 0))],
        out_specs=[pl.BlockSpec((hp, t, V_DIM), lambda h, qi: (h, qi, 0)),
                   pl.BlockSpec((hp, t, HEAD_LANES), lambda h, qi: (h, qi, 0))],
        out_shape=[jax.ShapeDtypeStruct((N_HEADS, s, V_DIM), f32), jax.ShapeDtypeStruct((N_HEADS, s, HEAD_LANES), bf16)],
        scratch_shapes=[pltpu.VMEM((hp, t, 128), f32), pltpu.VMEM((hp, t, HEAD_LANES), f32)],
        operands=(q, k, v), relay_at=(N_HEADS // hp - 1, max(nb - 2, 0)))


def _ssd_param_specs():
    return [_const((4, D_CONV)), _const((1, D_CONV)), _const((1, 128)), _const((1, 128)), _const((1, 128)),
            _const((1, D_SSD))]


def ssd_fwd(px, pz, plast, params):
    s = px.shape[0]
    nc = s // CHUNK

    def body(px_ref, pz_ref, pl_ref, cw_ref, cb_ref, dtb_ref, alog_ref, dskip_ref, snw_ref, yg_ref, st_ref,
             state_sc, halo_sc):
        i = pl.program_id(0)

        @pl.when(i == 0)
        def _():
            state_sc[...] = jnp.zeros(state_sc.shape, f32)
            halo_sc[...] = jnp.zeros(halo_sc.shape, f32)

        x = px_ref[...]
        prev = state_sc[...]
        st_ref[...] = prev
        xext = jnp.concatenate([halo_sc[...], x], axis=0)
        yg, new = _f_ssd(xext, pz_ref[...], pl_ref[...], prev, cw_ref[...], cb_ref[...], dtb_ref[...],
                         alog_ref[...], dskip_ref[...], snw_ref[...])
        yg_ref[...] = yg
        state_sc[...] = new
        halo_sc[...] = x[CHUNK - HALO:]

    tok = lambda w: pl.BlockSpec((CHUNK, w), lambda i: (i, 0))
    return pl.pallas_call(
        body, name="ssd_fwd", grid=(nc,),
        in_specs=[tok(D_CONV), tok(D_SSD), tok(128)] + _ssd_param_specs(),
        out_specs=[tok(D_SSD), pl.BlockSpec((None, N_HEADS // 2, 2 * SSD_HEAD_DIM, SSD_STATE), lambda i: (i, 0, 0, 0))],
        out_shape=[jax.ShapeDtypeStruct((s, D_SSD), f32),
                   jax.ShapeDtypeStruct((nc, N_HEADS // 2, 2 * SSD_HEAD_DIM, SSD_STATE), f32)],
        scratch_shapes=[pltpu.VMEM((N_HEADS // 2, 2 * SSD_HEAD_DIM, SSD_STATE), f32), pltpu.VMEM((HALO, D_CONV), f32)],
    )(px, pz, plast, *params)


def out_fwd(x, o, yg, g1, wo):
    s = x.shape[0]
    ts = _token_block(s)

    def body(x_ref, o_ref, yg_ref, g1_ref, wo_ref, out_ref):
        out_ref[...] = x_ref[...] + _f_out(o_ref[...], yg_ref[...], g1_ref[...], wo_ref[...])

    return pl.pallas_call(
        body, name="out_fwd", grid=(s // ts,),
        in_specs=[pl.BlockSpec((ts, D_MODEL), lambda i: (i, 0)), pl.BlockSpec((N_HEADS, ts, V_DIM), lambda i: (0, i, 0)),
                  pl.BlockSpec((ts, D_SSD), lambda i: (i, 0)), _const((1, D_MODEL)), _const((D_MODEL, D_MODEL))],
        out_specs=pl.BlockSpec((ts, D_MODEL), lambda i: (i, 0)),
        out_shape=jax.ShapeDtypeStruct((s, D_MODEL), f32),
    )(x, o, yg, g1, wo)


def mlp_fwd(x, nw, sh, sc, g2, wgu, wd):
    s = x.shape[0]
    ts = _token_block(s)
    nj = N_DEV // 2

    def body(x_ref, nw_ref, sh_ref, sc_ref, g2_ref, wg_ref, wu_ref, wd_ref, out_ref, mix_ref, h_ref, gate_ref, up_ref):
        j = pl.program_id(1)

        @pl.when(j == 0)
        def _():
            h_ref[...] = _f_modulate(x_ref[...], nw_ref[...], sh_ref[...], sc_ref[...]).astype(bf16)
            mix_ref[...] = jnp.zeros(mix_ref.shape, f32)

        nr = 2 if ts % 2 == 0 else 1
        half = ts // nr
        wg, wu, wd = wg_ref[...], wu_ref[...], wd_ref[...]
        products = lambda r: (mmw_t(h_ref[pl.ds(r * half, half), :], wg), mmw_t(h_ref[pl.ds(r * half, half), :], wu))
        ahead = products(0)
        for r in range(nr):
            gate, up = ahead
            if r + 1 < nr:
                ahead = products(r + 1)
            rows = pl.ds(r * half, half)
            gate_ref[rows, :] = gate.astype(bf16)
            up_ref[rows, :] = up.astype(bf16)
            mix_ref[rows, :] += mmw(jax.nn.silu(gate) * up, wd)

        @pl.when(j == nj - 1)
        def _():
            out_ref[...] = x_ref[...] + g2_ref[...] * mix_ref[...]

    vec = _const((1, D_MODEL))
    tok = pl.BlockSpec((ts, D_MODEL), lambda i, j: (i, 0))
    wide = pl.BlockSpec((None, ts, FF_SHARD), lambda i, j: (j, i, 0))
    return pl.pallas_call(
        body, name="mlp_fwd", grid=(s // ts, nj),
        in_specs=[tok, vec, vec, vec, vec,
                  pl.BlockSpec((None, FF_SHARD, D_MODEL), lambda i, j: (j, 0, 0)),
                  pl.BlockSpec((None, FF_SHARD, D_MODEL), lambda i, j: (j + nj, 0, 0)),
                  pl.BlockSpec((None, FF_SHARD, D_MODEL), lambda i, j: (j, 0, 0))],
        out_specs=[tok] * 3 + [wide] * 2,
        out_shape=[jax.ShapeDtypeStruct((s, D_MODEL), f32), jax.ShapeDtypeStruct((s, D_MODEL), f32),
                   jax.ShapeDtypeStruct((s, D_MODEL), bf16)] + [jax.ShapeDtypeStruct((nj, s, FF_SHARD), bf16)] * 2,
    )(x, nw, sh, sc, g2, wgu, wgu, wd)


def loss_fwd(y, target):
    s = y.shape[0]
    ts = _token_block(s)

    def body(y_ref, t_ref, dy_ref, loss_ref):
        d = y_ref[...] - t_ref[...]
        dy_ref[...] = d * (1.0 / D_MODEL)
        part = 0.5 * jnp.sum(jnp.sum(d * d, axis=-1, keepdims=True) * (1.0 / D_MODEL), axis=0, keepdims=True)
        _accumulate(pl.program_id(0) == 0, [loss_ref], [jnp.broadcast_to(part, (8, 128))])

    return pl.pallas_call(
        body, name="loss_fwd", grid=(s // ts,),
        in_specs=[pl.BlockSpec((ts, D_MODEL), lambda i: (i, 0))] * 2,
        out_specs=[pl.BlockSpec((ts, D_MODEL), lambda i: (i, 0)), _const((8, 128))],
        out_shape=[jax.ShapeDtypeStruct((s, D_MODEL), f32), jax.ShapeDtypeStruct((8, 128), f32)],
    )(y, target)


def mlp_bwd(h, dy, gate, up, g2, wgu, wd, job=None):
    s = h.shape[0]
    ts = min(MLP_BWD_ROWS, s)
    nj = N_DEV // 2
    ni = s // ts

    def body(h_ref, dy_ref, gate_ref, up_ref, g2_ref, wg_ref, wu_ref, wd_ref, dh_ref, dwg_ref, dwu_ref, dwd_ref,
             ag_sc, au_sc, ad_sc):
        i = pl.program_id(1)
        wg, wu, wd = wg_ref[...], wu_ref[...], wd_ref[...]
        h = h_ref[...]
        act, vjp = jax.vjp(lambda g, u: jax.nn.silu(g) * u, gate_ref[...].astype(f32), up_ref[...].astype(f32))
        dmix = (dy_ref[...] * g2_ref[...]).astype(bf16)
        dgate, dup = vjp(_dot(dmix, wd, 1, 1))
        dgate, dup = dgate.astype(bf16), dup.astype(bf16)
        dh_ref[...] = (_dot(dgate, wg, 1, 0) + _dot(dup, wu, 1, 0)).astype(bf16)
        grads = [_dot(dgate, h, 0, 0), _dot(dup, h, 0, 0), _dot(act, dmix, 0, 0)]
        _accumulate_then_cast(i == 0, i == ni - 1, [ag_sc, au_sc, ad_sc], [dwg_ref, dwu_ref, dwd_ref], grads)

    once = pl.Buffered(1)
    wspec = lambda off: pl.BlockSpec((None, FF_SHARD, D_MODEL), lambda j, i: (j + off, 0, 0), pipeline_mode=once)
    dspec = pl.BlockSpec((None, FF_SHARD, D_MODEL), lambda j, i: (j, 0, 0), pipeline_mode=once)
    wide = pl.BlockSpec((None, ts, FF_SHARD), lambda j, i: (j, i, 0))
    return _call_with_job(
        body, "mlp_bwd" if job is None else "mlp_bwd_comm", (nj, ni), job,
        in_specs=[pl.BlockSpec((ts, D_MODEL), lambda j, i: (i, 0)), pl.BlockSpec((ts, D_MODEL), lambda j, i: (i, 0)),
                  wide, wide, _const((1, D_MODEL)), wspec(0), wspec(nj), dspec],
        out_specs=[pl.BlockSpec((None, ts, D_MODEL), lambda j, i: (j, i, 0)), wspec(0), wspec(0), dspec],
        out_shape=[jax.ShapeDtypeStruct((nj, s, D_MODEL), bf16),
                   jax.ShapeDtypeStruct((nj, FF_SHARD, D_MODEL), bf16), jax.ShapeDtypeStruct((nj, FF_SHARD, D_MODEL), bf16),
                   jax.ShapeDtypeStruct((nj, FF_SHARD, D_MODEL), bf16)],
        scratch_shapes=[pltpu.VMEM((FF_SHARD, D_MODEL), f32), pltpu.VMEM((FF_SHARD, D_MODEL), f32),
                        pltpu.VMEM((FF_SHARD, D_MODEL), f32)],
        operands=(h, dy, gate, up, g2, wgu, wgu, wd))


def out_bwd(dy, dhparts, x, nw, sh, sc, mix, o, yg, g1, wo):
    s = dy.shape[0]
    ts = _token_block(s)
    nj = dhparts.shape[0]

    ni = s // ts

    def body(dy_ref, dp_ref, x_ref, nw_ref, sh_ref, sc_ref, mix_ref, o_ref, yg_ref, g1_ref, wo_ref,
             dx_ref, dnw_ref, dsh_ref, dsc_ref, do_ref, dyg_ref, dg1_ref, dg2_ref, dwo_ref, acc_sc):
        i = pl.program_id(0)
        g = dy_ref[...]
        _accumulate(i == 0, [dg2_ref], [jnp.sum(g * mix_ref[...], axis=0, keepdims=True)])
        dh = dp_ref[0].astype(f32)
        for j in range(1, nj):
            dh = dh + dp_ref[j].astype(f32)
        _, vjp_mod = jax.vjp(_f_modulate, x_ref[...], nw_ref[...], sh_ref[...], sc_ref[...])
        dx_mod, dnw, dsh, dsc = vjp_mod(dh)
        _accumulate(i == 0, [dnw_ref, dsh_ref, dsc_ref], [dnw, dsh, dsc])
        g = g + dx_mod
        dx_ref[...] = g
        o = o_ref[...]
        wo = wo_ref[...]
        _, vjp = jax.vjp(lambda o_, yg_, g1_, slot: _f_out(o_, yg_, g1_, wo, slot), o, yg_ref[...], g1_ref[...],
                         jnp.zeros(wo.shape, f32))
        do, dyg, dg1, dwo = vjp(g)
        delta = jnp.sum(do * o, axis=-1, keepdims=True)
        high = delta.astype(bf16)
        low = (delta - high.astype(f32)).astype(bf16)
        lane = lax.broadcasted_iota(jnp.int32, (1, 1, HEAD_LANES), 2)
        wide = jnp.concatenate([do.astype(bf16), jnp.zeros(do.shape, bf16)], axis=-1)
        do_ref[...] = jnp.where(lane == SPARE_V, -high, jnp.where(lane == SPARE_V + 1, -low, wide))
        dyg_ref[...] = dyg
        _accumulate(i == 0, [dg1_ref], [dg1])
        _accumulate_then_cast(i == 0, i == ni - 1, [acc_sc], [dwo_ref], [dwo])

    head = pl.BlockSpec((N_HEADS, ts, V_DIM), lambda i: (0, i, 0))
    tok = pl.BlockSpec((ts, D_MODEL), lambda i: (i, 0))
    vec = _const((1, D_MODEL))
    vshape = jax.ShapeDtypeStruct((1, D_MODEL), f32)
    return pl.pallas_call(
        body, name="out_bwd", grid=(ni,), scratch_shapes=[pltpu.VMEM((D_MODEL, D_MODEL), f32)],
        in_specs=[tok, pl.BlockSpec((nj, ts, D_MODEL), lambda i: (0, i, 0)), tok, vec, vec, vec, tok,
                  head, pl.BlockSpec((ts, D_SSD), lambda i: (i, 0)), vec, _const((D_MODEL, D_MODEL))],
        out_specs=[tok, vec, vec, vec, pl.BlockSpec((N_HEADS, ts, HEAD_LANES), lambda i: (0, i, 0)),
                   pl.BlockSpec((ts, D_SSD), lambda i: (i, 0)), vec, vec, _const((D_MODEL, D_MODEL))],
        out_shape=[jax.ShapeDtypeStruct((s, D_MODEL), f32), vshape, vshape, vshape,
                   jax.ShapeDtypeStruct((N_HEADS, s, HEAD_LANES), bf16), jax.ShapeDtypeStruct((s, D_SSD), f32),
                   vshape, vshape, jax.ShapeDtypeStruct((D_MODEL, D_MODEL), bf16)],
    )(dy, dhparts, x, nw, sh, sc, mix, o, yg, g1, wo)


def attn_bwd(qx, k, v, do, job=None):
    s = qx.shape[1]
    t = _token_block(s)
    nb = s // t

    hp = ATTN_HEADS_BWD

    def body(q_ref, k_ref, v_ref, do_ref, dq_ref, dk_ref, dv_ref, dv_sc):
        ki = pl.program_id(1)

        @pl.when(ki == 0)
        def _():
            dq_ref[...] = jnp.zeros(dq_ref.shape, f32)

        dk_ref[...] = jnp.zeros(dk_ref.shape, f32)
        dv_sc[...] = jnp.zeros(dv_sc.shape, f32)

        def step(q0, diagonal):
            rows = pl.ds(q0, t)

            def products(hh):
                sc = _scores(q_ref[hh, rows, :], k_ref[hh])
                dps = _scores(do_ref[hh, rows, :], v_ref[hh])
                return (jnp.where(_tril(t, t, 0), sc, NEG) if diagonal else sc), dps

            ahead = products(0)
            for hh in range(hp):
                sc, dps = ahead
                if hh + 1 < hp:
                    ahead = products(hh + 1)
                p = jnp.exp2(sc)
                ds = (p * dps).astype(bf16)
                dv_sc[hh] += lax.dot_general(p.astype(bf16), do_ref[hh, rows, :], (((0,), (0,)), ((), ())),
                                             preferred_element_type=f32)
                dk_ref[hh] += lax.dot_general(ds, q_ref[hh, rows, :], (((0,), (0,)), ((), ())), preferred_element_type=f32)
                dq_ref[hh, rows, :] += jnp.dot(ds, k_ref[hh], preferred_element_type=f32)

        step(pl.multiple_of(ki * t, t), True)

        def above(qi, carry):
            step(pl.multiple_of(qi * t, t), False)
            return carry

        lax.fori_loop(ki + 1, nb, above, 0)
        real = lax.broadcasted_iota(jnp.int32, (1, 1, HEAD_LANES), 2) < SPARE_Q
        dk_ref[...] = jnp.where(real, dk_ref[...] * LN2, 0.0)
        dv_ref[...] = dv_sc[:, :, :V_DIM]

        @pl.when(ki == nb - 1)
        def _():
            dq_ref[...] = jnp.where(real, dq_ref[...] * LN2, 0.0)

    qspec = pl.BlockSpec((hp, s, HEAD_LANES), lambda h, ki: (h, 0, 0))
    kspec = lambda w: pl.BlockSpec((hp, t, w), lambda h, ki: (h, ki, 0))
    return _call_with_job(
        body, "attn_bwd" if job is None else "attn_bwd_comm", (N_HEADS // hp, nb), job,
        in_specs=[qspec, kspec(HEAD_LANES), kspec(HEAD_LANES), qspec],
        out_specs=[qspec, kspec(HEAD_LANES), kspec(V_DIM)],
        out_shape=[jax.ShapeDtypeStruct((N_HEADS, s, HEAD_LANES), f32), jax.ShapeDtypeStruct((N_HEADS, s, HEAD_LANES), f32),
                   jax.ShapeDtypeStruct((N_HEADS, s, V_DIM), f32)],
        scratch_shapes=[pltpu.VMEM((hp, t, HEAD_LANES), f32)], operands=(qx, k, v, do))


def ssd_bwd(px, pz, plast, states, dyg, params):
    s = px.shape[0]
    nc = s // CHUNK
    per = CHUNK // HALO

    def body(px_ref, halo_ref, pz_ref, pl_ref, st_ref, dyg_ref, cw_ref, cb_ref, dtb_ref, alog_ref, dskip_ref, snw_ref,
             dpx_ref, dpz_ref, dpl_ref, dcw_ref, dcb_ref, ddtb_ref, dalog_ref, ddskip_ref, dsnw_ref, dstate_sc, dhalo_sc):
        t = pl.program_id(0)
        chunk = nc - 1 - t

        @pl.when(t == 0)
        def _():
            dstate_sc[...] = jnp.zeros(dstate_sc.shape, f32)
            dhalo_sc[...] = jnp.zeros(dhalo_sc.shape, f32)

        halo = jnp.where(chunk > 0, halo_ref[...], 0.0)
        xext = jnp.concatenate([halo, px_ref[...]], axis=0)
        _, vjp = jax.vjp(_f_ssd, xext, pz_ref[...], pl_ref[...], st_ref[...], cw_ref[...], cb_ref[...], dtb_ref[...],
                         alog_ref[...], dskip_ref[...], snw_ref[...])
        dxext, dz, dpl, dprev, dcw, dcb, ddtb, dalog, ddskip, dsnw = vjp((dyg_ref[...], dstate_sc[...]))
        dpx_ref[...] = dxext[HALO:]
        dpx_ref[CHUNK - HALO:, :] += dhalo_sc[...]
        dhalo_sc[...] = dxext[:HALO]
        dstate_sc[...] = dprev
        dpz_ref[...] = dz
        dpl_ref[...] = dpl
        _accumulate(t == 0, [dcw_ref, dcb_ref, ddtb_ref, dalog_ref, ddskip_ref, dsnw_ref],
                    [dcw, dcb, ddtb, dalog, ddskip, dsnw])

    rev = lambda w: pl.BlockSpec((CHUNK, w), lambda t: (nc - 1 - t, 0))
    pshapes = [jax.ShapeDtypeStruct((4, D_CONV), f32), jax.ShapeDtypeStruct((1, D_CONV), f32),
               jax.ShapeDtypeStruct((1, 128), f32), jax.ShapeDtypeStruct((1, 128), f32),
               jax.ShapeDtypeStruct((1, 128), f32), jax.ShapeDtypeStruct((1, D_SSD), f32)]
    return pl.pallas_call(
        body, name="ssd_bwd", grid=(nc,),
        in_specs=[rev(D_CONV),
                  pl.BlockSpec((HALO, D_CONV), lambda t: (jnp.maximum((nc - 1 - t) * per - 1, 0), 0)),
                  rev(D_SSD), rev(128),
                  pl.BlockSpec((None, N_HEADS // 2, 2 * SSD_HEAD_DIM, SSD_STATE), lambda t: (nc - 1 - t, 0, 0, 0)),
                  rev(D_SSD)] + _ssd_param_specs(),
        out_specs=[rev(D_CONV), rev(D_SSD), rev(128)] + _ssd_param_specs(),
        out_shape=[jax.ShapeDtypeStruct((s, D_CONV), f32), jax.ShapeDtypeStruct((s, D_SSD), f32),
                   jax.ShapeDtypeStruct((s, 128), f32)] + pshapes,
        scratch_shapes=[pltpu.VMEM((N_HEADS // 2, 2 * SSD_HEAD_DIM, SSD_STATE), f32), pltpu.VMEM((HALO, D_CONV), f32)],
    )(px, px, pz, plast, states, dyg, *params)


def qkv_bwd(pa, plast, cos_t, sin_t, params, dq, dk, dv):
    s = pa.shape[0]
    ts = _token_block(s)

    def body(pa_ref, pl_ref, cos_ref, sin_ref, *rest):
        qaw, kvaw, wq, wk, wv, qnw, knw, kpw = [r[...] for r in rest[:8]]
        dq_ref, dk_ref, dv_ref = rest[8:11]
        dpa_ref, dpl_ref = rest[11:13]
        dprm_refs = list(rest[13:])
        cos_t, sin_t = cos_ref[...], sin_ref[...]

        def stage(pa_, pl_, qaw_, kvaw_, sq, sk, sv, qnw_, knw_, kpw_):
            return _f_qkv(pa_, pl_, cos_t, sin_t, qaw_, kvaw_, wq, wk, wv, qnw_, knw_, kpw_, (sq, sk, sv))

        _, vjp = jax.vjp(stage, pa_ref[...], pl_ref[...], qaw, kvaw, jnp.zeros(wq.shape, f32), jnp.zeros(wk.shape, f32),
                         jnp.zeros(wv.shape, f32), qnw, knw, kpw)
        grads = vjp((dq_ref[...], dk_ref[...], dv_ref[...]))
        dpa_ref[...] = grads[0]
        dpl_ref[...] = grads[1]
        _accumulate(pl.program_id(0) == 0, dprm_refs, list(grads[2:]))

    tok = lambda w: pl.BlockSpec((ts, w), lambda i: (i, 0))
    head = lambda w: pl.BlockSpec((N_HEADS, ts, w), lambda i: (0, i, 0))
    pshapes = [jax.ShapeDtypeStruct((1, Q_RANK), f32), jax.ShapeDtypeStruct((1, KV_RANK), f32),
               jax.ShapeDtypeStruct((N_HEADS, Q_RANK, HEAD_LANES), f32), jax.ShapeDtypeStruct((N_HEADS, KV_RANK, HEAD_LANES), f32),
               jax.ShapeDtypeStruct((N_HEADS, KV_RANK, V_DIM), f32), jax.ShapeDtypeStruct((1, HEAD_LANES), f32),
               jax.ShapeDtypeStruct((1, HEAD_LANES), f32), jax.ShapeDtypeStruct((1, HEAD_LANES), f32)]
    return pl.pallas_call(
        body, name="qkv_bwd", grid=(s // ts,),
        in_specs=[tok(384), tok(128), tok(128), tok(128)] + _qkv_param_specs()
                 + [head(HEAD_LANES), head(HEAD_LANES), head(V_DIM)],
        out_specs=[tok(384), tok(128)] + _qkv_param_specs(),
        out_shape=[jax.ShapeDtypeStruct((s, 384), f32), jax.ShapeDtypeStruct((s, 128), f32)] + pshapes,
    )(pa, plast, cos_t, sin_t, *params, dq, dk, dv)


def proj_bwd(x, nw, sh, sc, w, dpa, dpz, dpx, dpl_k, dpl_dt, dres):
    s = x.shape[0]
    ts = _token_block(s)

    ni = s // ts

    def body(x_ref, nw_ref, sh_ref, sc_ref, w_ref, dpa_ref, dpz_ref, dpx_ref, dplk_ref, dpld_ref, dres_ref,
             dx_ref, dnw_ref, dsh_ref, dsc_ref, dw_ref, acc_sc):
        i = pl.program_id(0)
        g = jnp.concatenate([dpa_ref[...], dpz_ref[...], dpx_ref[...], dplk_ref[...] + dpld_ref[...]], axis=1)
        w = w_ref[...]
        _, vjp = jax.vjp(lambda x_, nw_, sh_, sc_, slot: _f_proj(x_, nw_, sh_, sc_, w, slot), x_ref[...], nw_ref[...],
                         sh_ref[...], sc_ref[...], jnp.zeros(w.shape, f32))
        dx, dnw, dsh, dsc, dw = vjp(g)
        dx_ref[...] = dx + dres_ref[...]
        _accumulate(i == 0, [dnw_ref, dsh_ref, dsc_ref], [dnw, dsh, dsc])
        _accumulate_then_cast(i == 0, i == ni - 1, [acc_sc], [dw_ref], [dw])

    vec = _const((1, D_MODEL))
    vshape = jax.ShapeDtypeStruct((1, D_MODEL), f32)
    tok = lambda w_: pl.BlockSpec((ts, w_), lambda i: (i, 0))
    return pl.pallas_call(
        body, name="proj_bwd", grid=(ni,), scratch_shapes=[pltpu.VMEM((D_PROJ, D_MODEL), f32)],
        in_specs=[tok(D_MODEL), vec, vec, vec, _const((D_PROJ, D_MODEL)), tok(384), tok(512), tok(1024), tok(128), tok(128),
                  tok(D_MODEL)],
        out_specs=[tok(D_MODEL), vec, vec, vec, _const((D_PROJ, D_MODEL))],
        out_shape=[jax.ShapeDtypeStruct((s, D_MODEL), f32), vshape, vshape, vshape,
                   jax.ShapeDtypeStruct((D_PROJ, D_MODEL), bf16)],
    )(x, nw, sh, sc, w, dpa, dpz, dpx, dpl_k, dpl_dt, dres)


def ada_fwd(c_all, w_ada, b_cols):
    def body(c_ref, w_ref, b_ref, out_ref):
        act = jax.nn.silu(c_ref[...])
        for l in range(2):
            out_ref[l] = jnp.dot(act, w_ref[l], precision=lax.Precision.HIGHEST, preferred_element_type=f32) + b_ref[l]

    return pl.pallas_call(body, name="ada_fwd", out_shape=jax.ShapeDtypeStruct((2, N_DEV, 768), f32))(c_all, w_ada, b_cols)


def ada_bwd(c_all, dmod_cols):
    def body(c_ref, d_ref, out_ref):
        out_ref[0] = lax.dot_general(jax.nn.silu(c_ref[...]), d_ref[0], (((0,), (0,)), ((), ())),
                                     precision=lax.Precision.HIGHEST, preferred_element_type=f32)

    return pl.pallas_call(
        body, name="ada_bwd", grid=(2,),
        in_specs=[_const((N_DEV, D_MODEL)), pl.BlockSpec((1, N_DEV, 768), lambda l: (l, 0, 0))],
        out_specs=pl.BlockSpec((1, D_MODEL, 768), lambda l: (l, 0, 0)),
        out_shape=jax.ShapeDtypeStruct((2, D_MODEL, 768), f32),
    )(c_all, dmod_cols)


def _adamw(w, g, m, v):
    m = ADAM_B1 * m + (1.0 - ADAM_B1) * g
    v = ADAM_B2 * v + (1.0 - ADAM_B2) * (g * g)
    m_hat = m / (1.0 - ADAM_B1 ** ADAM_STEP)
    v_hat = v / (1.0 - ADAM_B2 ** ADAM_STEP)
    delta = -ADAM_LR * (m_hat / (jnp.sqrt(v_hat) + ADAM_EPS) + ADAM_WD * w)
    return delta, m, v


def adamw(parts, w, m, v, layer, prev, name):
    n, r, c = parts.shape
    nl = w.shape[0]
    per_elem = 2 * (n * parts.dtype.itemsize + 7 * 4)
    lanes = -(-c // 128) * 128
    tr, tc = r, c
    if per_elem * r * lanes > ADAMW_BLOCK_BYTES:
        fits = [t for t in (256, 128, 64, 32, 16, 8) if r % t == 0]
        if fits:
            tr = fits[0]
        else:
            tc = next(t for t in (512, 256, 128) if c % t == 0)

    def body(p_ref, w_ref, m_ref, v_ref, *rest):
        g_ref, d_ref, nm_ref, nv_ref = rest[-4:]
        g = p_ref[0].astype(f32)
        for k in range(1, n):
            g = g + p_ref[k].astype(f32)
        delta, nm, nv = _adamw(w_ref[...], g, m_ref[...], v_ref[...])
        g_ref[...] = g
        d_ref[...] = delta
        nm_ref[...] = nm
        nv_ref[...] = nv

    blk = pl.BlockSpec((None, tr, tc), lambda i, j: (layer, i, j))
    shp = jax.ShapeDtypeStruct((nl, r, c), f32)
    kept = [] if prev is None else list(prev)
    return pl.pallas_call(
        body, name=name, grid=(r // tr, c // tc),
        in_specs=[pl.BlockSpec((n, tr, tc), lambda i, j: (0, i, j)), blk, blk, blk] + [ANY] * len(kept),
        out_specs=[blk] * 4, out_shape=[shp] * 4,
        input_output_aliases={4 + j: j for j in range(len(kept))},
    )(parts, w, m, v, *kept)


def _my_index():
    return 4 * lax.axis_index("x") + 2 * lax.axis_index("y") + lax.axis_index("c")


def _coords(idx):
    return (idx // 4, (idx // 2) % 2, idx % 2)


class CommJob:
    def __init__(self, operands, out_shape, phases, scratch):
        self.operands, self.out_shape, self.phases, self.scratch = operands, out_shape, phases, scratch


def _wait(out, n_blocks, send_sem, recv_sem, send=True, recv=True):
    span = out.at[pl.ds(0, n_blocks)]
    desc = pltpu.make_async_remote_copy(src_ref=span, dst_ref=span, send_sem=send_sem, recv_sem=recv_sem,
                                        device_id=_coords(_my_index()), device_id_type=MESH)
    if recv:
        desc.wait_recv()
    if send:
        desc.wait_send()


def gather_job(shards):
    n = len(shards)

    def places():
        x, y, c = lax.axis_index("x"), lax.axis_index("y"), lax.axis_index("c")
        return (x, y, c), (x, y, 1 - c), [(1 - x, y), (x, 1 - y), (1 - x, 1 - y)]

    def index(p):
        return 4 * p[0] + 2 * p[1] + p[2]

    def start(ins, outs, sems):
        far_send, far_recv, near_send, near_recv, local = sems
        me, sibling, chips = places()
        for k in range(n):
            pltpu.make_async_copy(ins[k], outs[k].at[index(me)], local.at[k]).start()
            for chip in chips:
                pltpu.make_async_remote_copy(src_ref=ins[k], dst_ref=outs[k].at[index(me)], send_sem=far_send.at[k],
                                             recv_sem=far_recv.at[k], device_id=(*chip, me[2]), device_id_type=MESH).start()
            pltpu.make_async_remote_copy(src_ref=ins[k], dst_ref=outs[k].at[index(me)], send_sem=near_send.at[k],
                                         recv_sem=near_recv.at[k], device_id=sibling, device_id_type=MESH).start()

    def relay(ins, outs, sems):
        far_send, far_recv, near_send, near_recv, local = sems
        me, sibling, chips = places()
        for k in range(n):
            _wait(outs[k], 3, far_send.at[k], far_recv.at[k], send=False)
            for chip in chips:
                block = outs[k].at[index((*chip, me[2]))]
                pltpu.make_async_remote_copy(src_ref=block, dst_ref=block, send_sem=near_send.at[k],
                                             recv_sem=near_recv.at[k], device_id=sibling, device_id_type=MESH).start()

    def finish(ins, outs, sems):
        far_send, far_recv, near_send, near_recv, local = sems
        for k in range(n):
            _wait(outs[k], 4, near_send.at[k], near_recv.at[k])
            _wait(outs[k], 3, far_send.at[k], far_recv.at[k], recv=False)
            pltpu.make_async_copy(ins[k], outs[k].at[0], local.at[k]).wait()

    shapes = [jax.ShapeDtypeStruct((N_DEV,) + tuple(a.shape), a.dtype) for a in shards]
    return CommJob(list(shards), shapes, [start, relay, finish], [pltpu.SemaphoreType.DMA((n,))] * 5)


def scatter_job(tensors):
    n = len(tensors)
    flat, where = [], {}
    for k, pieces in enumerate(tensors):
        d = 0
        for piece in pieces:
            for b in range(piece.shape[0]):
                where[k, d] = (len(flat), b)
                d += 1
            flat.append(piece)
        assert d == N_DEV

    def start(ins, outs, sems):
        send_sems, recv_sems, local_sems = sems
        me = _my_index()

        def block(k, d):
            i, b = where[k, d]
            return ins[i].at[b]

        for d in range(N_DEV):
            @pl.when(d != me)
            def _():
                for k in range(n):
                    pltpu.make_async_remote_copy(src_ref=block(k, d), dst_ref=outs[k].at[me], send_sem=send_sems.at[k],
                                                 recv_sem=recv_sems.at[k], device_id=(d // 4, (d // 2) % 2, d % 2),
                                                 device_id_type=MESH).start()

            @pl.when(d == me)
            def _():
                for k in range(n):
                    pltpu.make_async_copy(block(k, d), outs[k].at[d], local_sems.at[k]).start()

    def finish(ins, outs, sems):
        send_sems, recv_sems, local_sems = sems
        for k in range(n):
            _wait(outs[k], N_DEV - 1, send_sems.at[k], recv_sems.at[k])
            i, b = where[k, 0]
            pltpu.make_async_copy(ins[i].at[b], outs[k].at[0], local_sems.at[k]).wait()

    shapes = [jax.ShapeDtypeStruct((N_DEV,) + tuple(p[0].shape[1:]), p[0].dtype) for p in tensors]
    return CommJob(flat, shapes, [start, finish], [pltpu.SemaphoreType.DMA((n,))] * 3)


def comm_call(job, name):
    ni, no = len(job.operands), len(job.out_shape)

    def body(*refs):
        ins, outs, sems = refs[:ni], refs[ni:ni + no], refs[ni + no:]
        for phase in job.phases:
            phase(ins, outs, sems)

    return pl.pallas_call(body, name=name, in_specs=[ANY] * ni, out_specs=[ANY] * no, out_shape=job.out_shape,
                          scratch_shapes=job.scratch)(*job.operands)


def _carry(job, body, n_in, n_out, at_step):
    ji, jo, js = len(job.operands), len(job.out_shape), len(job.scratch)

    def carrier(*refs):
        a, b = n_in, n_in + ji
        c, d = b + n_out, b + n_out + jo
        e = len(refs) - js
        job_refs = (refs[a:b], refs[c:d], refs[e:])
        n = len(job.phases)

        @pl.when(at_step(0, n))
        def _():
            job.phases[0](*job_refs)

        body(*refs[:a], *refs[b:c], *refs[d:e])

        for i in range(1, n):
            @pl.when(at_step(i, n))
            def _():
                job.phases[i](*job_refs)

    return carrier


def _pad_lanes(v, lo, total=128):
    return jnp.pad(v, (lo, total - lo - v.shape[0]))[None, :]


MIXER_WEIGHTS = ("w_in", "w_q_up", "w_kv_up", "conv_w")
LATE_WEIGHTS = ("w_out", "w_gate_up", "w_down")


def mixer_operands(g, sw):
    w_in = g["w_in"].reshape(D_IN, D_MODEL)
    zero = lambda rows: jnp.zeros((rows, D_MODEL), w_in.dtype)
    w_proj = jnp.concatenate(
        [w_in[:384], w_in[416:928], w_in[928:1952], w_in[1952:1960], zero(56), w_in[384:416], zero(32)], axis=0)
    wq = jnp.pad(g["w_q_up"], ((0, 0), (0, 0), (0, HEAD_LANES - NOPE - ROPE)))
    wk = jnp.pad(g["w_kv_up"][:, :, :NOPE], ((0, 0), (0, 0), (0, HEAD_LANES - NOPE)))
    wv = g["w_kv_up"][:, :, NOPE:]
    qkv = (sw["q_a_norm_w"][None, :], sw["kv_a_norm_w"][None, :], wq, wk, wv,
           _pad_lanes(jnp.concatenate([sw["q_nope_norm_w"], sw["q_pe_norm_w"]]), 0),
           _pad_lanes(sw["k_nope_norm_w"], 0), _pad_lanes(sw["k_pe_norm_w"], NOPE))
    conv_w = g["conv_w"].astype(f32).transpose(1, 0, 2).reshape(4, D_CONV)
    ssd = (conv_w, sw["conv_b"][None, :], _pad_lanes(sw["dt_bias"], 0), _pad_lanes(sw["a_log"], 0),
           _pad_lanes(sw["d_skip"], 0), sw["ssd_norm_w"][None, :])
    return dict(w_proj=w_proj, qkv=qkv, ssd=ssd, n1=sw["norm1_w"][None, :])


def late_operands(g, sw):
    return dict(wo=g["w_out"].reshape(D_MODEL, D_MODEL), wgu=g["w_gate_up"],
                wd=g["w_down"].reshape(N_DEV // 2, FF_SHARD, D_MODEL), n2=sw["norm2_w"][None, :])


def layer_fwd(x, mod, kw, cos_t, sin_t, job=None, late=None):
    sh1, sc1, g1, sh2, sc2, g2 = [mod[i:i + 1] for i in range(6)]
    pa, pz, px, plast = proj_fwd(x, kw["n1"], sh1, sc1, kw["w_proj"])
    q, k, v = qkv_fwd(pa, plast, cos_t, sin_t, kw["qkv"])
    (o, qx), carried = attn_fwd(q, k, v, job)
    if late is not None:
        kw = {**kw, **late(carried)}
    yg, states = ssd_fwd(px, pz, plast, kw["ssd"])
    x_mid = out_fwd(x, o, yg, g1, kw["wo"])
    x_out, mix, h_mid, gate, up = mlp_fwd(x_mid, kw["n2"], sh2, sc2, g2, kw["wgu"], kw["wd"])
    saved = dict(x=x, pa=pa, pz=pz, px=px, plast=plast, qx=qx, k=k, v=v, o=o, yg=yg, states=states, x_mid=x_mid,
                 mix=mix, h_mid=h_mid, gate=gate, up=up)
    return x_out, saved, kw, carried


def layer_bwd_head(dy, mod, kw, sv, job=None):
    _, _, g1, sh2, sc2, g2 = [mod[i:i + 1] for i in range(6)]
    (dhparts, dwg, dwu, dwd), carried = mlp_bwd(sv["h_mid"], dy, sv["gate"], sv["up"], g2, kw["wgu"], kw["wd"], job)
    dmid, dn2, dsh2, dsc2, do, dyg, dg1, dg2, dwo = out_bwd(
        dy, dhparts, sv["x_mid"], kw["n2"], sh2, sc2, sv["mix"], sv["o"], sv["yg"], g1, kw["wo"])
    early = dict(w_out=[dwo.reshape(N_DEV, D_MODEL // N_DEV, D_MODEL)], w_gate_up=[dwg, dwu],
                 w_down=[dwd.reshape(N_DEV, D_FF // N_DEV, D_MODEL)])
    head = dict(dmid=dmid, do=do, dyg=dyg, dn2=dn2, dsh2=dsh2, dsc2=dsc2, dg2=dg2, dg1=dg1)
    return head, early, carried


def layer_bwd_tail(hd, mod, kw, cos_t, sin_t, sv, job=None):
    sh1, sc1 = mod[0:1], mod[1:2]
    (dq, dk, dv), carried = attn_bwd(sv["qx"], sv["k"], sv["v"], hd["do"], job)
    dpx, dpz, dpl_dt, dcw, dcb, ddtb, dalog, ddskip, dsnw = ssd_bwd(sv["px"], sv["pz"], sv["plast"], sv["states"],
                                                                   hd["dyg"], kw["ssd"])
    dpa, dpl_k, dqaw, dkvaw, dwq, dwk, dwv, dqnw, dknw, dkpw = qkv_bwd(sv["pa"], sv["plast"], cos_t, sin_t, kw["qkv"],
                                                                       dq, dk, dv)
    dx, dn1, dsh1, dsc1, dwp = proj_bwd(sv["x"], kw["n1"], sh1, sc1, kw["w_proj"], dpa, dpz, dpx, dpl_k, dpl_dt, hd["dmid"])
    dmod = jnp.concatenate([dsh1, dsc1, hd["dg1"], hd["dsh2"], hd["dsc2"], hd["dg2"]], axis=0)
    dw_in = jnp.concatenate([dwp[:384], dwp[1984:2016], dwp[384:1920], dwp[1920:1928]], axis=0)
    grads = dict(
        norm1_w=dn1[0], norm2_w=hd["dn2"][0], q_a_norm_w=dqaw[0], kv_a_norm_w=dkvaw[0],
        q_nope_norm_w=dqnw[0, :NOPE], q_pe_norm_w=dqnw[0, NOPE:NOPE + ROPE], k_nope_norm_w=dknw[0, :NOPE],
        k_pe_norm_w=dkpw[0, NOPE:NOPE + ROPE], conv_b=dcb[0], dt_bias=ddtb[0, :N_HEADS], a_log=dalog[0, :N_HEADS],
        d_skip=ddskip[0, :N_HEADS], ssd_norm_w=dsnw[0],
        w_in=[dw_in.reshape(N_DEV, D_IN // N_DEV, D_MODEL)],
        w_q_up=[dwq[:, :, :NOPE + ROPE].astype(bf16)],
        w_kv_up=[jnp.concatenate([dwk[:, :, :NOPE], dwv], axis=2).astype(bf16)],
        conv_w=[dcw.reshape(4, N_DEV, D_CONV // N_DEV).transpose(1, 0, 2).astype(bf16)],
    )
    return dx, dmod, grads, carried


def _pack_small(get, last=None):
    flat = jnp.concatenate([get(name).reshape(-1) for name, _ in SMALL])
    flat = jnp.pad(flat, (0, SMALL_ROWS * 128 - flat.shape[0]))
    if last is not None:
        flat = flat.at[-1].set(last)
    return flat.reshape(SMALL_ROWS, 128)


def _unpack_small(packed):
    flat = packed.reshape(-1)
    out, off = {}, 0
    for name, size in SMALL:
        out[name] = flat[off:off + 2 * size].reshape(2, size)
        off += 2 * size
    return out


def kernel(x, c, positions, norm1_w, norm2_w, w_ada, b_ada, w_in, q_a_norm_w, w_q_up, kv_a_norm_w, w_kv_up, q_nope_norm_w, q_pe_norm_w, k_nope_norm_w, k_pe_norm_w, conv_w, conv_b, dt_bias, a_log, d_skip, ssd_norm_w, w_out, w_gate_up, w_down, loss_target, m_norm1_w, m_norm2_w, m_w_ada, m_b_ada, m_w_in, m_q_a_norm_w, m_w_q_up, m_kv_a_norm_w, m_w_kv_up, m_q_nope_norm_w, m_q_pe_norm_w, m_k_nope_norm_w, m_k_pe_norm_w, m_conv_w, m_conv_b, m_dt_bias, m_a_log, m_d_skip, m_ssd_norm_w, m_w_out, m_w_gate_up, m_w_down, v_norm1_w, v_norm2_w, v_w_ada, v_b_ada, v_w_in, v_q_a_norm_w, v_w_q_up, v_kv_a_norm_w, v_w_kv_up, v_q_nope_norm_w, v_q_pe_norm_w, v_k_nope_norm_w, v_k_pe_norm_w, v_conv_w, v_conv_b, v_dt_bias, v_a_log, v_d_skip, v_ssd_norm_w, v_w_out, v_w_gate_up, v_w_down):
    w = dict(norm1_w=norm1_w, norm2_w=norm2_w, w_ada=w_ada, b_ada=b_ada, w_in=w_in, q_a_norm_w=q_a_norm_w, w_q_up=w_q_up,
             kv_a_norm_w=kv_a_norm_w, w_kv_up=w_kv_up, q_nope_norm_w=q_nope_norm_w, q_pe_norm_w=q_pe_norm_w,
             k_nope_norm_w=k_nope_norm_w, k_pe_norm_w=k_pe_norm_w, conv_w=conv_w, conv_b=conv_b, dt_bias=dt_bias,
             a_log=a_log, d_skip=d_skip, ssd_norm_w=ssd_norm_w, w_out=w_out, w_gate_up=w_gate_up, w_down=w_down)
    m = dict(norm1_w=m_norm1_w, norm2_w=m_norm2_w, w_ada=m_w_ada, b_ada=m_b_ada, w_in=m_w_in, q_a_norm_w=m_q_a_norm_w,
             w_q_up=m_w_q_up, kv_a_norm_w=m_kv_a_norm_w, w_kv_up=m_w_kv_up, q_nope_norm_w=m_q_nope_norm_w,
             q_pe_norm_w=m_q_pe_norm_w, k_nope_norm_w=m_k_nope_norm_w, k_pe_norm_w=m_k_pe_norm_w, conv_w=m_conv_w,
             conv_b=m_conv_b, dt_bias=m_dt_bias, a_log=m_a_log, d_skip=m_d_skip, ssd_norm_w=m_ssd_norm_w, w_out=m_w_out,
             w_gate_up=m_w_gate_up, w_down=m_w_down)
    v = dict(norm1_w=v_norm1_w, norm2_w=v_norm2_w, w_ada=v_w_ada, b_ada=v_b_ada, w_in=v_w_in, q_a_norm_w=v_q_a_norm_w,
             w_q_up=v_w_q_up, kv_a_norm_w=v_kv_a_norm_w, w_kv_up=v_w_kv_up, q_nope_norm_w=v_q_nope_norm_w,
             q_pe_norm_w=v_q_pe_norm_w, k_nope_norm_w=v_k_nope_norm_w, k_pe_norm_w=v_k_pe_norm_w, conv_w=v_conv_w,
             conv_b=v_conv_b, dt_bias=v_dt_bias, a_log=v_a_log, d_skip=v_d_skip, ssd_norm_w=v_ssd_norm_w, w_out=v_w_out,
             w_gate_up=v_w_gate_up, w_down=v_w_down)
    me = _my_index()
    seq = x.shape[1]

    def shard(name, l):
        if name == "conv_w":
            return w[name][l]
        if name in TRANSPOSED:
            return jnp.swapaxes(w[name][l], 0, 1).astype(bf16)
        return w[name][l].astype(bf16)

    def shards(names, l):
        return [shard(name, l) for name in names]

    small = [{name: w[name][l] for name, _ in SMALL if name != "b_ada"} for l in range(2)]
    n_mix, n_late = len(MIXER_WEIGHTS), len(LATE_WEIGHTS)

    first = comm_call(gather_job([c] + shards(MIXER_WEIGHTS, 0)), "gather_first")
    c_all = first[0].reshape(N_DEV, D_MODEL)
    kws = [mixer_operands(dict(zip(MIXER_WEIGHTS, first[1:])), small[0]), None]

    b_cols = lax.dynamic_slice_in_dim(b_ada, me * 768, 768, axis=1)
    mod_cols = ada_fwd(c_all, w_ada, b_cols)
    (mod_all,) = comm_call(gather_job([mod_cols]), "gather_mod")
    mod_me = lax.dynamic_index_in_dim(mod_all, me, axis=2, keepdims=False)
    mods = [mod_me[:, l, :].reshape(6, D_MODEL) for l in range(2)]

    inv_freq = 1.0 / (ROPE_THETA ** (jnp.arange(0, ROPE, 2, dtype=f32) / ROPE))
    inv = _pad_lanes(jnp.concatenate([inv_freq, inv_freq]), NOPE)
    cos_t, sin_t = rope_tables(positions.reshape(seq, 1), inv)

    saved = [None, None]
    h, saved[0], kws[0], got = layer_fwd(
        x[0], mods[0], kws[0], cos_t, sin_t, gather_job(shards(LATE_WEIGHTS, 0) + shards(MIXER_WEIGHTS, 1)),
        lambda got: late_operands(dict(zip(LATE_WEIGHTS, got[:n_late])), small[0]))
    kws[1] = mixer_operands(dict(zip(MIXER_WEIGHTS, got[n_late:])), small[1])
    h, saved[1], kws[1], _ = layer_fwd(
        h, mods[1], kws[1], cos_t, sin_t, gather_job(shards(LATE_WEIGHTS, 1)),
        lambda got: late_operands(dict(zip(LATE_WEIGHTS, got)), small[1]))
    dy, loss_part = loss_fwd(h, loss_target[0])

    early, late = ("w_out", "w_gate_up", "w_down"), ("w_in", "w_q_up", "w_kv_up", "conv_w")
    parts = [{}, {}]
    head, pieces, _ = layer_bwd_head(dy, mods[1], kws[1], saved[1])
    dy, dmod1, grads1, got = layer_bwd_tail(head, mods[1], kws[1], cos_t, sin_t, saved[1], scatter_job([pieces[n] for n in early]))
    parts[1].update(zip(early, got))
    head, pieces, got = layer_bwd_head(dy, mods[0], kws[0], saved[0], scatter_job([grads1[n] for n in late]))
    parts[1].update(zip(late, got))
    dy, dmod0, grads0, got = layer_bwd_tail(head, mods[0], kws[0], cos_t, sin_t, saved[0], scatter_job([pieces[n] for n in early]))
    parts[0].update(zip(early, got))
    parts[0].update(zip(late, comm_call(scatter_job([grads0[n] for n in late]), "scatter_layer0_rest")))
    grad_x = dy[None]

    small_part = {name: jnp.stack([grads0[name], grads1[name]]) for name, _ in SMALL if name != "b_ada"}
    small_part["b_ada"] = jnp.stack([dmod0.reshape(-1), dmod1.reshape(-1)])
    (small_all,) = comm_call(gather_job([_pack_small(lambda n: small_part[n], loss_part[0, 0])]), "gather_small_grads")
    packed = adamw(small_all, _pack_small(lambda n: w[n])[None], _pack_small(lambda n: m[n])[None],
                   _pack_small(lambda n: v[n])[None], 0, None, "adamw_small")
    loss = packed[0][0, -1, -1]
    res = {}
    for key, arr in zip("gdmv", packed):
        for name, val in _unpack_small(arr[0]).items():
            res[key, name] = val

    off = 2 * (1024 + 1024)
    dmod_all = small_all.reshape(N_DEV, -1)[:, off:off + 2 * 6144].reshape(N_DEV, 2, 6144)
    dmod_cols = lax.dynamic_slice_in_dim(dmod_all, me * 768, 768, axis=2).transpose(1, 0, 2)
    g_ada = ada_bwd(c_all, dmod_cols)
    out = None
    for l in range(2):
        out = adamw(g_ada[l][None], w_ada, m_w_ada, v_w_ada, l, out, "adamw_w_ada")
    res.update(zip([(key, "w_ada") for key in "gdmv"], out))

    for name in BIG:
        view = (lambda a: jnp.swapaxes(a, 1, 2)) if name in TRANSPOSED else (lambda a: a)
        out = None
        for l in range(2):
            out = adamw(parts[l][name], view(w[name]), view(m[name]), view(v[name]), l, out, "adamw_" + name)
        res.update(zip([(key, name) for key in "gdmv"], [view(a) for a in out]))

    return (loss, grad_x, *[res["g", n] for n in WEIGHTS], *[res["d", n] for n in WEIGHTS],
            *[res["m", n] for n in WEIGHTS], *[res["v", n] for n in WEIGHTS])
```

```python
import functools

import jax
import jax.numpy as jnp
from jax import lax
from jax.experimental import pallas as pl
from jax.experimental.pallas import tpu as pltpu

f32 = jnp.float32
bf16 = jnp.bfloat16

N_DEV = 8
D_MODEL = 1024
N_HEADS = 8
HEAD_LANES = 128
NOPE = 64
ROPE = 32
V_DIM = 64
Q_RANK = 256
KV_RANK = 128
D_SSD = 512
D_CONV = 1024
SSD_STATE = 128
SSD_HEAD_DIM = 64
CHUNK = 128
HALO = 8
D_FF = 2816
FF_SHARD = 704
D_IN = 1960
D_PROJ = 2048
EPS = 1e-6
LOG2E = 1.4426950408889634
LN2 = 0.6931471805599453
Q_SCALE = (NOPE + ROPE) ** -0.5 * LOG2E
SPARE_Q = NOPE + ROPE
SPARE_V = V_DIM
ATTN_ROWS_FWD = 256
ATTN_HEADS_FWD = 8
ATTN_HEADS_BWD = 4
MLP_FWD_ROWS = 1024
MLP_BWD_ROWS = 512
ROPE_THETA = 10000.0
NEG = -1e30

ADAM_LR = 0.001
ADAM_B1 = 0.9
ADAM_B2 = 0.999
ADAM_EPS = 1e-08
ADAM_WD = 0.01
ADAM_STEP = 10
ADAMW_BLOCK_BYTES = 24 << 20

MESH = pl.DeviceIdType.MESH
ANY = pl.BlockSpec(memory_space=pl.ANY)

SMALL = (("norm1_w", 1024), ("norm2_w", 1024), ("b_ada", 6144), ("q_a_norm_w", 256), ("kv_a_norm_w", 128),
         ("q_nope_norm_w", 64), ("q_pe_norm_w", 32), ("k_nope_norm_w", 64), ("k_pe_norm_w", 32),
         ("conv_b", 1024), ("dt_bias", 8), ("a_log", 8), ("d_skip", 8), ("ssd_norm_w", 512))
SMALL_ROWS = 168
BIG = ("w_in", "w_q_up", "w_kv_up", "conv_w", "w_out", "w_gate_up", "w_down")
TRANSPOSED = ("w_in", "w_gate_up")
WEIGHTS = ("norm1_w", "norm2_w", "w_ada", "b_ada", "w_in", "q_a_norm_w", "w_q_up", "kv_a_norm_w", "w_kv_up",
           "q_nope_norm_w", "q_pe_norm_w", "k_nope_norm_w", "k_pe_norm_w", "conv_w", "conv_b", "dt_bias",
           "a_log", "d_skip", "ssd_norm_w", "w_out", "w_gate_up", "w_down")


def _dot(a, b, ca, cb):
    return lax.dot_general(a.astype(bf16), b.astype(bf16), (((ca,), (cb,)), ((), ())), preferred_element_type=f32)


@jax.custom_vjp
def mm(a, b):
    return _dot(a, b, 1, 0)


def _mm_fwd(a, b):
    return _dot(a, b, 1, 0), (a, b)


def _mm_bwd(res, g):
    a, b = res
    return _dot(g, b, 1, 1).astype(a.dtype), _dot(a, g, 0, 0).astype(b.dtype)


mm.defvjp(_mm_fwd, _mm_bwd)


@jax.custom_vjp
def _mm_slot(a, w, slot):
    return _dot(a, w, 1, 0)


def _mm_slot_fwd(a, w, slot):
    return _dot(a, w, 1, 0), (a, w)


def _mm_slot_bwd(res, g):
    a, w = res
    return _dot(g, w, 1, 1).astype(a.dtype), None, _dot(a, g, 0, 0)


_mm_slot.defvjp(_mm_slot_fwd, _mm_slot_bwd)


def mmw(a, w, slot=None):
    return _dot(a, w, 1, 0) if slot is None else _mm_slot(a, w, slot)


@jax.custom_vjp
def _mm_slot_t(a, wt, slot):
    return _dot(a, wt, 1, 1)


def _mm_slot_t_fwd(a, wt, slot):
    return _dot(a, wt, 1, 1), (a, wt)


def _mm_slot_t_bwd(res, g):
    a, wt = res
    return _dot(g, wt, 1, 0).astype(a.dtype), None, _dot(g, a, 0, 0)


_mm_slot_t.defvjp(_mm_slot_t_fwd, _mm_slot_t_bwd)


def mmw_t(a, wt, slot=None):
    return _dot(a, wt, 1, 1) if slot is None else _mm_slot_t(a, wt, slot)


@jax.custom_vjp
def mm_nt(a, b):
    return _dot(a, b, 1, 1)


def _mm_nt_fwd(a, b):
    return _dot(a, b, 1, 1), (a, b)


def _mm_nt_bwd(res, g):
    a, b = res
    return _dot(g, b, 1, 0).astype(a.dtype), _dot(g, a, 0, 0).astype(b.dtype)


mm_nt.defvjp(_mm_nt_fwd, _mm_nt_bwd)


@jax.custom_vjp
def mm_tn(a, b):
    return _dot(a, b, 0, 0)


def _mm_tn_fwd(a, b):
    return _dot(a, b, 0, 0), (a, b)


def _mm_tn_bwd(res, g):
    a, b = res
    return _dot(b, g, 1, 1).astype(a.dtype), _dot(a, g, 1, 0).astype(b.dtype)


mm_tn.defvjp(_mm_tn_fwd, _mm_tn_bwd)


def _rms(x, w):
    return x * lax.rsqrt(jnp.mean(x * x, axis=-1, keepdims=True) + EPS) * w


def _const(shape):
    n = len(shape)
    return pl.BlockSpec(shape, lambda *_: (0,) * n)


def _accumulate(first, refs, vals):
    @pl.when(first)
    def _():
        for r, v in zip(refs, vals):
            r[...] = v

    @pl.when(jnp.logical_not(first))
    def _():
        for r, v in zip(refs, vals):
            r[...] += v


def _accumulate_then_cast(first, last, accs, outs, vals):
    _accumulate(first, accs, vals)

    @pl.when(last)
    def _():
        for a, o in zip(accs, outs):
            o[...] = a[...].astype(o.dtype)


def _token_block(s):
    return min(512, s)


def _f_proj(x, nw, sh, sc, w, slot=None):
    h = _rms(x, nw) * (1.0 + sc) + sh
    return mmw_t(h, w, slot)


def _f_qkv(pa, plast, cos_t, sin_t, qaw, kvaw, wq, wk, wv, qnw, knw, kpw, slots=None):
    sq, sk, sv = slots if slots is not None else ([None] * N_HEADS,) * 3
    lane = lax.broadcasted_iota(jnp.int32, (1, HEAD_LANES), 1)
    m_nope = lane < NOPE
    m_pe = (lane >= NOPE) & (lane < NOPE + ROPE)
    rows = pa.shape[0]

    def rope(t):
        half = ROPE // 2
        swapped = jnp.concatenate(
            [jnp.zeros((rows, NOPE), f32), t[:, NOPE + half:NOPE + ROPE], t[:, NOPE:NOPE + half],
             jnp.zeros((rows, HEAD_LANES - NOPE - ROPE), f32)], axis=1)
        return t * cos_t + swapped * sin_t

    qa = _rms(pa[:, :Q_RANK], qaw)
    kva = _rms(pa[:, Q_RANK:Q_RANK + KV_RANK], kvaw)
    kp = jnp.where(m_pe, plast, 0.0)
    kp = kp * lax.rsqrt(jnp.sum(kp * kp, axis=-1, keepdims=True) / ROPE + EPS) * kpw
    k_rot = rope(kp)
    qs, ks, vs = [], [], []
    for h in range(N_HEADS):
        qh = mmw(qa, wq[h], sq[h])
        ss_n = jnp.sum(jnp.where(m_nope, qh * qh, 0.0), axis=-1, keepdims=True) / NOPE
        ss_p = jnp.sum(jnp.where(m_pe, qh * qh, 0.0), axis=-1, keepdims=True) / ROPE
        r = jnp.where(m_nope, lax.rsqrt(ss_n + EPS), lax.rsqrt(ss_p + EPS))
        qs.append(rope(qh * r * qnw) * Q_SCALE)
        kh = mmw(kva, wk[h], sk[h])
        kh = kh * lax.rsqrt(jnp.sum(kh * kh, axis=-1, keepdims=True) / NOPE + EPS) * knw
        ks.append(kh + k_rot)
        vs.append(mmw(kva, wv[h], sv[h]))
    return jnp.stack(qs), jnp.stack(ks), jnp.stack(vs)


def _f_ssd(xext, z, plast, prev, cw, cb, dtb, alog, dskip, snw):
    n = CHUNK
    conv = cb
    for k in range(4):
        conv = conv + cw[k:k + 1] * xext[HALO - 3 + k:HALO - 3 + k + n]
    xc = jax.nn.silu(conv)
    xs, bm, cm = xc[:, :D_SSD], xc[:, D_SSD:D_SSD + 2 * SSD_STATE], xc[:, D_SSD + 2 * SSD_STATE:]
    lane = lax.broadcasted_iota(jnp.int32, (1, 128), 1)
    dt = jax.nn.softplus(jnp.where(lane < N_HEADS, plast, 0.0) + dtb)
    adt = dt * (-jnp.exp(alog))
    row = lax.broadcasted_iota(jnp.int32, (n, n), 0)
    col = lax.broadcasted_iota(jnp.int32, (n, n), 1)
    tri = row >= col
    acs = jnp.dot(tri.astype(f32), adt, precision=lax.Precision.HIGHEST, preferred_element_type=f32)
    acs_t = acs.T
    bgs = [bm[:, g * SSD_STATE:(g + 1) * SSD_STATE] for g in range(2)]
    cgs = [cm[:, g * SSD_STATE:(g + 1) * SSD_STATE] for g in range(2)]
    cb_ts = [mm_nt(cgs[g], bgs[g]) for g in range(2)]
    low = lane < SSD_HEAD_DIM
    low_rows = lax.broadcasted_iota(jnp.int32, (2 * SSD_HEAD_DIM, 1), 0) < SSD_HEAD_DIM

    def both(a0, a1):
        return jnp.where(low, a0, a1)

    pre = []
    for i in range(N_HEADS // 2):
        h0, h1 = 2 * i, 2 * i + 1
        col0, col1 = acs[:, h0:h0 + 1], acs[:, h1:h1 + 1]
        last0, last1 = acs[n - 1:n, h0:h0 + 1], acs[n - 1:n, h1:h1 + 1]
        cb_t = cb_ts[i // 2]
        scores0 = cb_t * jnp.exp(jnp.where(tri, col0 - acs_t[h0:h0 + 1, :], -jnp.inf))
        scores1 = cb_t * jnp.exp(jnp.where(tri, col1 - acs_t[h1:h1 + 1, :], -jnp.inf))
        xp = xs[:, i * 128:(i + 1) * 128]
        xdt = xp * both(dt[:, h0:h0 + 1], dt[:, h1:h1 + 1])
        weighted = xdt * both(jnp.exp(last0 - col0), jnp.exp(last1 - col1))
        chunk_decay = jnp.where(low_rows, jnp.exp(last0), jnp.exp(last1))
        in_decay = both(jnp.exp(col0), jnp.exp(col1))
        skip = both(dskip[:, h0:h0 + 1], dskip[:, h1:h1 + 1]) * xp
        pre.append((scores0, scores1, xdt, weighted, chunk_decay, in_decay, skip))
    prods = []
    for i in range(N_HEADS // 2):
        scores0, scores1, xdt, weighted, _, _, _ = pre[i]
        g = i // 2
        y_diag = mm(scores0, jnp.where(low, xdt, 0.0)) + mm(scores1, jnp.where(low, 0.0, xdt))
        prods.append((y_diag, mm_tn(weighted, bgs[g]), mm_nt(cgs[g], prev[i])))
    ys, news = [], []
    for i in range(N_HEADS // 2):
        y_diag, st, y_off = prods[i]
        _, _, _, _, chunk_decay, in_decay, skip = pre[i]
        news.append(chunk_decay * prev[i] + st)
        ys.append(y_diag + y_off * in_decay + skip)
    y = jnp.concatenate(ys, axis=1)
    yg = y * jax.nn.silu(z)
    half = D_SSD // 2
    outs = []
    for g in range(2):
        t = yg[:, g * half:(g + 1) * half]
        outs.append(t * lax.rsqrt(jnp.mean(t * t, axis=-1, keepdims=True) + EPS))
    return jnp.concatenate(outs, axis=1) * snw, jnp.stack(news)


def _f_out(o, yg, g1, wo, slot=None):
    cat = jnp.concatenate([o[h] for h in range(N_HEADS)] + [yg], axis=1)
    return g1 * mmw(cat, wo, slot)


def _f_modulate(x, nw, sh, sc):
    return _rms(x, nw) * (1.0 + sc) + sh


def proj_fwd(x, nw, sh, sc, w):
    s = x.shape[0]
    ts = _token_block(s)

    def body(x_ref, nw_ref, sh_ref, sc_ref, w_ref, pa_ref, pz_ref, px_ref, pl_ref):
        p = _f_proj(x_ref[...], nw_ref[...], sh_ref[...], sc_ref[...], w_ref[...])
        pa_ref[...] = p[:, :384]
        pz_ref[...] = p[:, 384:896]
        px_ref[...] = p[:, 896:1920]
        pl_ref[...] = p[:, 1920:]

    vec = _const((1, D_MODEL))
    return pl.pallas_call(
        body, name="proj_fwd", grid=(s // ts,),
        in_specs=[pl.BlockSpec((ts, D_MODEL), lambda i: (i, 0)), vec, vec, vec, _const((D_PROJ, D_MODEL))],
        out_specs=[pl.BlockSpec((ts, 384), lambda i: (i, 0)), pl.BlockSpec((ts, 512), lambda i: (i, 0)),
                   pl.BlockSpec((ts, 1024), lambda i: (i, 0)), pl.BlockSpec((ts, 128), lambda i: (i, 0))],
        out_shape=[jax.ShapeDtypeStruct((s, 384), f32), jax.ShapeDtypeStruct((s, 512), f32),
                   jax.ShapeDtypeStruct((s, 1024), f32), jax.ShapeDtypeStruct((s, 128), f32)],
    )(x, nw, sh, sc, w)


def rope_tables(pos, inv):
    s = pos.shape[0]
    ts = _token_block(s)

    def body(pos_ref, inv_ref, cos_ref, sin_ref):
        ang = pos_ref[...].astype(f32) * inv_ref[...]
        lane = lax.broadcasted_iota(jnp.int32, (1, HEAD_LANES), 1)
        half = ROPE // 2
        cos_ref[...] = jnp.where(lane < NOPE, 1.0, jnp.where(lane < NOPE + ROPE, jnp.cos(ang), 0.0))
        sn = jnp.sin(ang)
        sin_ref[...] = jnp.where((lane >= NOPE) & (lane < NOPE + half), -sn,
                                 jnp.where((lane >= NOPE + half) & (lane < NOPE + ROPE), sn, 0.0))

    return pl.pallas_call(
        body, name="rope_tables", grid=(s // ts,),
        in_specs=[pl.BlockSpec((ts, 1), lambda i: (i, 0)), _const((1, HEAD_LANES))],
        out_specs=[pl.BlockSpec((ts, HEAD_LANES), lambda i: (i, 0))] * 2,
        out_shape=[jax.ShapeDtypeStruct((s, HEAD_LANES), f32)] * 2,
    )(pos, inv)


def _qkv_param_specs():
    return [_const((1, Q_RANK)), _const((1, KV_RANK)), _const((N_HEADS, Q_RANK, HEAD_LANES)),
            _const((N_HEADS, KV_RANK, HEAD_LANES)), _const((N_HEADS, KV_RANK, V_DIM)),
            _const((1, HEAD_LANES)), _const((1, HEAD_LANES)), _const((1, HEAD_LANES))]


def qkv_fwd(pa, plast, cos_t, sin_t, params):
    s = pa.shape[0]
    ts = _token_block(s)

    def body(pa_ref, pl_ref, cos_ref, sin_ref, *rest):
        prm = [r[...] for r in rest[:8]]
        q_ref, k_ref, v_ref = rest[8:]
        q, k, v = _f_qkv(pa_ref[...], pl_ref[...], cos_ref[...], sin_ref[...], *prm)
        q_ref[...] = q.astype(bf16)
        lane = lax.broadcasted_iota(jnp.int32, (1, 1, HEAD_LANES), 2)
        k_ref[...] = jnp.where((lane == SPARE_Q) | (lane == SPARE_Q + 1), 1.0, k).astype(bf16)
        v_ref[...] = jnp.concatenate([v, jnp.ones_like(v)], axis=-1).astype(bf16)

    tok = lambda w: pl.BlockSpec((ts, w), lambda i: (i, 0))
    head = pl.BlockSpec((N_HEADS, ts, HEAD_LANES), lambda i: (0, i, 0))
    return pl.pallas_call(
        body, name="qkv_fwd", grid=(s // ts,),
        in_specs=[tok(384), tok(128), tok(128), tok(128)] + _qkv_param_specs(),
        out_specs=[head] * 3, out_shape=[jax.ShapeDtypeStruct((N_HEADS, s, HEAD_LANES), bf16)] * 3,
    )(pa, plast, cos_t, sin_t, *params)


def _scores(q, k):
    return lax.dot_general(q, k, (((1,), (1,)), ((), ())), preferred_element_type=f32)


def _tril(rows, cols, row_offset):
    row = row_offset + lax.broadcasted_iota(jnp.int32, (rows, cols), 0)
    col = lax.broadcasted_iota(jnp.int32, (rows, cols), 1)
    return row >= col


def _call_with_job(body, name, grid, job, in_specs, out_specs, out_shape, scratch_shapes, operands, relay_at=None):
    if job is None:
        res = pl.pallas_call(body, name=name, grid=grid, in_specs=in_specs, out_specs=out_specs, out_shape=out_shape,
                             scratch_shapes=scratch_shapes)(*operands)
        return res, None

    def at_step(i, n):
        if i == 0:
            want = [0] * len(grid)
        elif i == n - 1:
            want = [g - 1 for g in grid]
        else:
            want = relay_at
        return functools.reduce(jnp.logical_and, [pl.program_id(a) == s for a, s in enumerate(want)])

    carrier = _carry(job, body, len(in_specs), len(out_specs), at_step)
    res = pl.pallas_call(
        carrier, name=name, grid=grid,
        in_specs=list(in_specs) + [ANY] * len(job.operands), out_specs=list(out_specs) + [ANY] * len(job.out_shape),
        out_shape=list(out_shape) + list(job.out_shape), scratch_shapes=list(scratch_shapes) + job.scratch,
    )(*operands, *job.operands)
    return res[:len(out_specs)], res[len(out_specs):]


def attn_fwd(q, k, v, job=None):
    s = q.shape[1]
    t = _token_block(s)
    nb = s // t

    rb = min(ATTN_ROWS_FWD, t)

    hp = ATTN_HEADS_FWD

    def body(q_ref, k_ref, v_ref, o_ref, qx_ref, m_sc, acc_sc):
        qi = pl.program_id(1)
        m_sc[...] = jnp.full(m_sc.shape, NEG, f32)
        acc_sc[...] = jnp.zeros(acc_sc.shape, f32)

        def step(k0, diagonal):
            chains = [(hh, r) for hh in range(hp) for r in range(t // rb)]

            def scores(hh, r):
                nk = (r + 1) * rb if diagonal else t
                sc = _scores(q_ref[hh, pl.ds(r * rb, rb), :], k_ref[hh, pl.ds(k0, nk), :])
                return jnp.where(_tril(rb, nk, r * rb), sc, NEG) if diagonal else sc

            ahead = scores(*chains[0])
            for c, (hh, r) in enumerate(chains):
                sc = ahead
                if c + 1 < len(chains):
                    ahead = scores(*chains[c + 1])
                rows = pl.ds(r * rb, rb)
                keys = pl.ds(k0, (r + 1) * rb if diagonal else t)
                m_prev = m_sc[hh, rows, :1]
                m_new = jnp.maximum(m_prev, jnp.max(sc, axis=-1, keepdims=True))
                p = jnp.exp2(sc - m_new)
                alpha = jnp.exp2(m_prev - m_new)
                acc = alpha * acc_sc[hh, rows, :] + jnp.dot(p.astype(bf16), v_ref[hh, keys, :], preferred_element_type=f32)
                if diagonal:
                    l = acc[:, V_DIM:V_DIM + 1]
                    o_ref[hh, rows, :] = acc[:, :V_DIM] / l
                    lse = m_new + jnp.log2(l)
                    high = lse.astype(bf16)
                    low = (lse - high.astype(f32)).astype(bf16)
                    lane = lax.broadcasted_iota(jnp.int32, (1, HEAD_LANES), 1)
                    qx_ref[hh, rows, :] = jnp.where(lane == SPARE_Q, -high,
                                                    jnp.where(lane == SPARE_Q + 1, -low, q_ref[hh, rows, :]))
                else:
                    acc_sc[hh, rows, :] = acc
                    m_sc[hh, rows, :] = jnp.broadcast_to(m_new, (rb, 128))

        def below(ki, carry):
            step(pl.multiple_of(ki * t, t), False)
            return carry

        lax.fori_loop(0, qi, below, 0)
        step(pl.multiple_of(qi * t, t), True)

    return _call_with_job(
        body, "attn_fwd" if job is None else "attn_fwd_comm", (N_HEADS // hp, nb), job,
        in_specs=[pl.BlockSpec((hp, t, HEAD_LANES), lambda h, qi: (h, qi, 0)),
                  pl.BlockSpec((hp, s, HEAD_LANES), lambda h, qi: (h, 0, 0)),
                  pl.BlockSpec((hp, s, HEAD_LANES), lambda h, qi: (h, 0, 0))],
        out_specs=[pl.BlockSpec((hp, t, V_DIM), lambda h, qi: (h, qi, 0)),
                   pl.BlockSpec((hp, t, HEAD_LANES), lambda h, qi: (h, qi, 0))],
        out_shape=[jax.ShapeDtypeStruct((N_HEADS, s, V_DIM), f32), jax.ShapeDtypeStruct((N_HEADS, s, HEAD_LANES), bf16)],
        scratch_shapes=[pltpu.VMEM((hp, t, 128), f32), pltpu.VMEM((hp, t, HEAD_LANES), f32)],
        operands=(q, k, v), relay_at=(N_HEADS // hp - 1, max(nb - 2, 0)))


def _ssd_param_specs():
    return [_const((4, D_CONV)), _const((1, D_CONV)), _const((1, 128)), _const((1, 128)), _const((1, 128)),
            _const((1, D_SSD))]


def ssd_fwd(px, pz, plast, params):
    s = px.shape[0]
    nc = s // CHUNK

    def body(px_ref, pz_ref, pl_ref, cw_ref, cb_ref, dtb_ref, alog_ref, dskip_ref, snw_ref, yg_ref, st_ref,
             state_sc, halo_sc):
        i = pl.program_id(0)

        @pl.when(i == 0)
        def _():
            state_sc[...] = jnp.zeros(state_sc.shape, f32)
            halo_sc[...] = jnp.zeros(halo_sc.shape, f32)

        x = px_ref[...]
        prev = state_sc[...]
        st_ref[...] = prev
        xext = jnp.concatenate([halo_sc[...], x], axis=0)
        yg, new = _f_ssd(xext, pz_ref[...], pl_ref[...], prev, cw_ref[...], cb_ref[...], dtb_ref[...],
                         alog_ref[...], dskip_ref[...], snw_ref[...])
        yg_ref[...] = yg
        state_sc[...] = new
        halo_sc[...] = x[CHUNK - HALO:]

    tok = lambda w: pl.BlockSpec((CHUNK, w), lambda i: (i, 0))
    return pl.pallas_call(
        body, name="ssd_fwd", grid=(nc,),
        in_specs=[tok(D_CONV), tok(D_SSD), tok(128)] + _ssd_param_specs(),
        out_specs=[tok(D_SSD), pl.BlockSpec((None, N_HEADS // 2, 2 * SSD_HEAD_DIM, SSD_STATE), lambda i: (i, 0, 0, 0))],
        out_shape=[jax.ShapeDtypeStruct((s, D_SSD), f32),
                   jax.ShapeDtypeStruct((nc, N_HEADS // 2, 2 * SSD_HEAD_DIM, SSD_STATE), f32)],
        scratch_shapes=[pltpu.VMEM((N_HEADS // 2, 2 * SSD_HEAD_DIM, SSD_STATE), f32), pltpu.VMEM((HALO, D_CONV), f32)],
    )(px, pz, plast, *params)


def out_fwd(x, o, yg, g1, wo):
    s = x.shape[0]
    ts = _token_block(s)

    def body(x_ref, o_ref, yg_ref, g1_ref, wo_ref, out_ref):
        out_ref[...] = x_ref[...] + _f_out(o_ref[...], yg_ref[...], g1_ref[...], wo_ref[...])

    return pl.pallas_call(
        body, name="out_fwd", grid=(s // ts,),
        in_specs=[pl.BlockSpec((ts, D_MODEL), lambda i: (i, 0)), pl.BlockSpec((N_HEADS, ts, V_DIM), lambda i: (0, i, 0)),
                  pl.BlockSpec((ts, D_SSD), lambda i: (i, 0)), _const((1, D_MODEL)), _const((D_MODEL, D_MODEL))],
        out_specs=pl.BlockSpec((ts, D_MODEL), lambda i: (i, 0)),
        out_shape=jax.ShapeDtypeStruct((s, D_MODEL), f32),
    )(x, o, yg, g1, wo)


def mlp_fwd(x, nw, sh, sc, g2, wgu, wd):
    s = x.shape[0]
    ts = min(MLP_FWD_ROWS, s)
    nj = N_DEV // 2

    def body(x_ref, nw_ref, sh_ref, sc_ref, g2_ref, wg_ref, wu_ref, wd_ref, out_ref, mix_ref, h_ref, gate_ref, up_ref):
        j = pl.program_id(1)

        @pl.when(j == 0)
        def _():
            h_ref[...] = _f_modulate(x_ref[...], nw_ref[...], sh_ref[...], sc_ref[...]).astype(bf16)
            mix_ref[...] = jnp.zeros(mix_ref.shape, f32)

        nr = max(ts // 512, 1)
        half = ts // nr
        wg, wu, wd = wg_ref[...], wu_ref[...], wd_ref[...]
        products = lambda r: (mmw_t(h_ref[pl.ds(r * half, half), :], wg), mmw_t(h_ref[pl.ds(r * half, half), :], wu))
        ahead = products(0)
        for r in range(nr):
            gate, up = ahead
            if r + 1 < nr:
                ahead = products(r + 1)
            rows = pl.ds(r * half, half)
            gate_ref[rows, :] = gate.astype(bf16)
            up_ref[rows, :] = up.astype(bf16)
            mix_ref[rows, :] += mmw(jax.nn.silu(gate) * up, wd)

        @pl.when(j == nj - 1)
        def _():
            out_ref[...] = x_ref[...] + g2_ref[...] * mix_ref[...]

    vec = _const((1, D_MODEL))
    tok = pl.BlockSpec((ts, D_MODEL), lambda i, j: (i, 0))
    wide = pl.BlockSpec((None, ts, FF_SHARD), lambda i, j: (j, i, 0))
    return pl.pallas_call(
        body, name="mlp_fwd", grid=(s // ts, nj),
        in_specs=[tok, vec, vec, vec, vec,
                  pl.BlockSpec((None, FF_SHARD, D_MODEL), lambda i, j: (j, 0, 0)),
                  pl.BlockSpec((None, FF_SHARD, D_MODEL), lambda i, j: (j + nj, 0, 0)),
                  pl.BlockSpec((None, FF_SHARD, D_MODEL), lambda i, j: (j, 0, 0))],
        out_specs=[tok] * 3 + [wide] * 2,
        out_shape=[jax.ShapeDtypeStruct((s, D_MODEL), f32), jax.ShapeDtypeStruct((s, D_MODEL), f32),
                   jax.ShapeDtypeStruct((s, D_MODEL), bf16)] + [jax.ShapeDtypeStruct((nj, s, FF_SHARD), bf16)] * 2,
    )(x, nw, sh, sc, g2, wgu, wgu, wd)


def loss_fwd(y, target):
    s = y.shape[0]
    ts = _token_block(s)

    def body(y_ref, t_ref, dy_ref, loss_ref):
        d = y_ref[...] - t_ref[...]
        dy_ref[...] = d * (1.0 / D_MODEL)
        part = 0.5 * jnp.sum(jnp.sum(d * d, axis=-1, keepdims=True) * (1.0 / D_MODEL), axis=0, keepdims=True)
        _accumulate(pl.program_id(0) == 0, [loss_ref], [jnp.broadcast_to(part, (8, 128))])

    return pl.pallas_call(
        body, name="loss_fwd", grid=(s // ts,),
        in_specs=[pl.BlockSpec((ts, D_MODEL), lambda i: (i, 0))] * 2,
        out_specs=[pl.BlockSpec((ts, D_MODEL), lambda i: (i, 0)), _const((8, 128))],
        out_shape=[jax.ShapeDtypeStruct((s, D_MODEL), f32), jax.ShapeDtypeStruct((8, 128), f32)],
    )(y, target)


def mlp_bwd(h, dy, gate, up, g2, wgu, wd, job=None):
    s = h.shape[0]
    ts = min(MLP_BWD_ROWS, s)
    nj = N_DEV // 2
    ni = s // ts

    def body(h_ref, dy_ref, gate_ref, up_ref, g2_ref, wg_ref, wu_ref, wd_ref, dh_ref, dwg_ref, dwu_ref, dwd_ref,
             ag_sc, au_sc, ad_sc):
        i = pl.program_id(1)
        wg, wu, wd = wg_ref[...], wu_ref[...], wd_ref[...]
        h = h_ref[...]
        act, vjp = jax.vjp(lambda g, u: jax.nn.silu(g) * u, gate_ref[...].astype(f32), up_ref[...].astype(f32))
        dmix = (dy_ref[...] * g2_ref[...]).astype(bf16)
        dgate, dup = vjp(_dot(dmix, wd, 1, 1))
        dgate, dup = dgate.astype(bf16), dup.astype(bf16)
        dh_ref[...] = (_dot(dgate, wg, 1, 0) + _dot(dup, wu, 1, 0)).astype(bf16)
        grads = [_dot(dgate, h, 0, 0), _dot(dup, h, 0, 0), _dot(act, dmix, 0, 0)]
        _accumulate_then_cast(i == 0, i == ni - 1, [ag_sc, au_sc, ad_sc], [dwg_ref, dwu_ref, dwd_ref], grads)

    once = pl.Buffered(1)
    wspec = lambda off: pl.BlockSpec((None, FF_SHARD, D_MODEL), lambda j, i: (j + off, 0, 0), pipeline_mode=once)
    dspec = pl.BlockSpec((None, FF_SHARD, D_MODEL), lambda j, i: (j, 0, 0), pipeline_mode=once)
    wide = pl.BlockSpec((None, ts, FF_SHARD), lambda j, i: (j, i, 0))
    return _call_with_job(
        body, "mlp_bwd" if job is None else "mlp_bwd_comm", (nj, ni), job,
        in_specs=[pl.BlockSpec((ts, D_MODEL), lambda j, i: (i, 0)), pl.BlockSpec((ts, D_MODEL), lambda j, i: (i, 0)),
                  wide, wide, _const((1, D_MODEL)), wspec(0), wspec(nj), dspec],
        out_specs=[pl.BlockSpec((None, ts, D_MODEL), lambda j, i: (j, i, 0)), wspec(0), wspec(0), dspec],
        out_shape=[jax.ShapeDtypeStruct((nj, s, D_MODEL), bf16),
                   jax.ShapeDtypeStruct((nj, FF_SHARD, D_MODEL), bf16), jax.ShapeDtypeStruct((nj, FF_SHARD, D_MODEL), bf16),
                   jax.ShapeDtypeStruct((nj, FF_SHARD, D_MODEL), bf16)],
        scratch_shapes=[pltpu.VMEM((FF_SHARD, D_MODEL), f32), pltpu.VMEM((FF_SHARD, D_MODEL), f32),
                        pltpu.VMEM((FF_SHARD, D_MODEL), f32)],
        operands=(h, dy, gate, up, g2, wgu, wgu, wd))


def out_bwd(dy, dhparts, x, nw, sh, sc, mix, o, yg, g1, wo):
    s = dy.shape[0]
    ts = _token_block(s)
    nj = dhparts.shape[0]

    ni = s // ts

    def body(dy_ref, dp_ref, x_ref, nw_ref, sh_ref, sc_ref, mix_ref, o_ref, yg_ref, g1_ref, wo_ref,
             dx_ref, dnw_ref, dsh_ref, dsc_ref, do_ref, dyg_ref, dg1_ref, dg2_ref, dwo_ref, acc_sc):
        i = pl.program_id(0)
        g = dy_ref[...]
        _accumulate(i == 0, [dg2_ref], [jnp.sum(g * mix_ref[...], axis=0, keepdims=True)])
        dh = dp_ref[0].astype(f32)
        for j in range(1, nj):
            dh = dh + dp_ref[j].astype(f32)
        _, vjp_mod = jax.vjp(_f_modulate, x_ref[...], nw_ref[...], sh_ref[...], sc_ref[...])
        dx_mod, dnw, dsh, dsc = vjp_mod(dh)
        _accumulate(i == 0, [dnw_ref, dsh_ref, dsc_ref], [dnw, dsh, dsc])
        g = g + dx_mod
        dx_ref[...] = g
        o = o_ref[...]
        wo = wo_ref[...]
        _, vjp = jax.vjp(lambda o_, yg_, g1_, slot: _f_out(o_, yg_, g1_, wo, slot), o, yg_ref[...], g1_ref[...],
                         jnp.zeros(wo.shape, f32))
        do, dyg, dg1, dwo = vjp(g)
        delta = jnp.sum(do * o, axis=-1, keepdims=True)
        high = delta.astype(bf16)
        low = (delta - high.astype(f32)).astype(bf16)
        lane = lax.broadcasted_iota(jnp.int32, (1, 1, HEAD_LANES), 2)
        wide = jnp.concatenate([do.astype(bf16), jnp.zeros(do.shape, bf16)], axis=-1)
        do_ref[...] = jnp.where(lane == SPARE_V, -high, jnp.where(lane == SPARE_V + 1, -low, wide))
        dyg_ref[...] = dyg
        _accumulate(i == 0, [dg1_ref], [dg1])
        _accumulate_then_cast(i == 0, i == ni - 1, [acc_sc], [dwo_ref], [dwo])

    head = pl.BlockSpec((N_HEADS, ts, V_DIM), lambda i: (0, i, 0))
    tok = pl.BlockSpec((ts, D_MODEL), lambda i: (i, 0))
    vec = _const((1, D_MODEL))
    vshape = jax.ShapeDtypeStruct((1, D_MODEL), f32)
    return pl.pallas_call(
        body, name="out_bwd", grid=(ni,), scratch_shapes=[pltpu.VMEM((D_MODEL, D_MODEL), f32)],
        in_specs=[tok, pl.BlockSpec((nj, ts, D_MODEL), lambda i: (0, i, 0)), tok, vec, vec, vec, tok,
                  head, pl.BlockSpec((ts, D_SSD), lambda i: (i, 0)), vec, _const((D_MODEL, D_MODEL))],
        out_specs=[tok, vec, vec, vec, pl.BlockSpec((N_HEADS, ts, HEAD_LANES), lambda i: (0, i, 0)),
                   pl.BlockSpec((ts, D_SSD), lambda i: (i, 0)), vec, vec, _const((D_MODEL, D_MODEL))],
        out_shape=[jax.ShapeDtypeStruct((s, D_MODEL), f32), vshape, vshape, vshape,
                   jax.ShapeDtypeStruct((N_HEADS, s, HEAD_LANES), bf16), jax.ShapeDtypeStruct((s, D_SSD), f32),
                   vshape, vshape, jax.ShapeDtypeStruct((D_MODEL, D_MODEL), bf16)],
    )(dy, dhparts, x, nw, sh, sc, mix, o, yg, g1, wo)


def attn_bwd(qx, k, v, do, job=None):
    s = qx.shape[1]
    t = _token_block(s)
    nb = s // t

    hp = ATTN_HEADS_BWD

    def body(q_ref, k_ref, v_ref, do_ref, dq_ref, dk_ref, dv_ref, dv_sc):
        ki = pl.program_id(1)

        @pl.when(ki == 0)
        def _():
            dq_ref[...] = jnp.zeros(dq_ref.shape, f32)

        dk_ref[...] = jnp.zeros(dk_ref.shape, f32)
        dv_sc[...] = jnp.zeros(dv_sc.shape, f32)

        def step(q0, diagonal):
            rows = pl.ds(q0, t)

            def products(hh):
                sc = _scores(q_ref[hh, rows, :], k_ref[hh])
                dps = _scores(do_ref[hh, rows, :], v_ref[hh])
                return (jnp.where(_tril(t, t, 0), sc, NEG) if diagonal else sc), dps

            ahead = products(0)
            for hh in range(hp):
                sc, dps = ahead
                if hh + 1 < hp:
                    ahead = products(hh + 1)
                p = jnp.exp2(sc)
                ds = (p * dps).astype(bf16)
                dv_sc[hh] += lax.dot_general(p.astype(bf16), do_ref[hh, rows, :], (((0,), (0,)), ((), ())),
                                             preferred_element_type=f32)
                dk_ref[hh] += lax.dot_general(ds, q_ref[hh, rows, :], (((0,), (0,)), ((), ())), preferred_element_type=f32)
                dq_ref[hh, rows, :] += jnp.dot(ds, k_ref[hh], preferred_element_type=f32)

        step(pl.multiple_of(ki * t, t), True)

        def above(qi, carry):
            step(pl.multiple_of(qi * t, t), False)
            return carry

        lax.fori_loop(ki + 1, nb, above, 0)
        real = lax.broadcasted_iota(jnp.int32, (1, 1, HEAD_LANES), 2) < SPARE_Q
        dk_ref[...] = jnp.where(real, dk_ref[...] * LN2, 0.0)
        dv_ref[...] = dv_sc[:, :, :V_DIM]

        @pl.when(ki == nb - 1)
        def _():
            dq_ref[...] = jnp.where(real, dq_ref[...] * LN2, 0.0)

    qspec = pl.BlockSpec((hp, s, HEAD_LANES), lambda h, ki: (h, 0, 0))
    kspec = lambda w: pl.BlockSpec((hp, t, w), lambda h, ki: (h, ki, 0))
    return _call_with_job(
        body, "attn_bwd" if job is None else "attn_bwd_comm", (N_HEADS // hp, nb), job,
        in_specs=[qspec, kspec(HEAD_LANES), kspec(HEAD_LANES), qspec],
        out_specs=[qspec, kspec(HEAD_LANES), kspec(V_DIM)],
        out_shape=[jax.ShapeDtypeStruct((N_HEADS, s, HEAD_LANES), f32), jax.ShapeDtypeStruct((N_HEADS, s, HEAD_LANES), f32),
                   jax.ShapeDtypeStruct((N_HEADS, s, V_DIM), f32)],
        scratch_shapes=[pltpu.VMEM((hp, t, HEAD_LANES), f32)], operands=(qx, k, v, do))


def ssd_bwd(px, pz, plast, states, dyg, params):
    s = px.shape[0]
    nc = s // CHUNK
    per = CHUNK // HALO

    def body(px_ref, halo_ref, pz_ref, pl_ref, st_ref, dyg_ref, cw_ref, cb_ref, dtb_ref, alog_ref, dskip_ref, snw_ref,
             dpx_ref, dpz_ref, dpl_ref, dcw_ref, dcb_ref, ddtb_ref, dalog_ref, ddskip_ref, dsnw_ref, dstate_sc, dhalo_sc):
        t = pl.program_id(0)
        chunk = nc - 1 - t

        @pl.when(t == 0)
        def _():
            dstate_sc[...] = jnp.zeros(dstate_sc.shape, f32)
            dhalo_sc[...] = jnp.zeros(dhalo_sc.shape, f32)

        halo = jnp.where(chunk > 0, halo_ref[...], 0.0)
        xext = jnp.concatenate([halo, px_ref[...]], axis=0)
        _, vjp = jax.vjp(_f_ssd, xext, pz_ref[...], pl_ref[...], st_ref[...], cw_ref[...], cb_ref[...], dtb_ref[...],
                         alog_ref[...], dskip_ref[...], snw_ref[...])
        dxext, dz, dpl, dprev, dcw, dcb, ddtb, dalog, ddskip, dsnw = vjp((dyg_ref[...], dstate_sc[...]))
        dpx_ref[...] = dxext[HALO:]
        dpx_ref[CHUNK - HALO:, :] += dhalo_sc[...]
        dhalo_sc[...] = dxext[:HALO]
        dstate_sc[...] = dprev
        dpz_ref[...] = dz
        dpl_ref[...] = dpl
        _accumulate(t == 0, [dcw_ref, dcb_ref, ddtb_ref, dalog_ref, ddskip_ref, dsnw_ref],
                    [dcw, dcb, ddtb, dalog, ddskip, dsnw])

    rev = lambda w: pl.BlockSpec((CHUNK, w), lambda t: (nc - 1 - t, 0))
    pshapes = [jax.ShapeDtypeStruct((4, D_CONV), f32), jax.ShapeDtypeStruct((1, D_CONV), f32),
               jax.ShapeDtypeStruct((1, 128), f32), jax.ShapeDtypeStruct((1, 128), f32),
               jax.ShapeDtypeStruct((1, 128), f32), jax.ShapeDtypeStruct((1, D_SSD), f32)]
    return pl.pallas_call(
        body, name="ssd_bwd", grid=(nc,),
        in_specs=[rev(D_CONV),
                  pl.BlockSpec((HALO, D_CONV), lambda t: (jnp.maximum((nc - 1 - t) * per - 1, 0), 0)),
                  rev(D_SSD), rev(128),
                  pl.BlockSpec((None, N_HEADS // 2, 2 * SSD_HEAD_DIM, SSD_STATE), lambda t: (nc - 1 - t, 0, 0, 0)),
                  rev(D_SSD)] + _ssd_param_specs(),
        out_specs=[rev(D_CONV), rev(D_SSD), rev(128)] + _ssd_param_specs(),
        out_shape=[jax.ShapeDtypeStruct((s, D_CONV), f32), jax.ShapeDtypeStruct((s, D_SSD), f32),
                   jax.ShapeDtypeStruct((s, 128), f32)] + pshapes,
        scratch_shapes=[pltpu.VMEM((N_HEADS // 2, 2 * SSD_HEAD_DIM, SSD_STATE), f32), pltpu.VMEM((HALO, D_CONV), f32)],
    )(px, px, pz, plast, states, dyg, *params)


def qkv_bwd(pa, plast, cos_t, sin_t, params, dq, dk, dv):
    s = pa.shape[0]
    ts = _token_block(s)

    def body(pa_ref, pl_ref, cos_ref, sin_ref, *rest):
        qaw, kvaw, wq, wk, wv, qnw, knw, kpw = [r[...] for r in rest[:8]]
        dq_ref, dk_ref, dv_ref = rest[8:11]
        dpa_ref, dpl_ref = rest[11:13]
        dprm_refs = list(rest[13:])
        cos_t, sin_t = cos_ref[...], sin_ref[...]

        def stage(pa_, pl_, qaw_, kvaw_, sq, sk, sv, qnw_, knw_, kpw_):
            return _f_qkv(pa_, pl_, cos_t, sin_t, qaw_, kvaw_, wq, wk, wv, qnw_, knw_, kpw_, (sq, sk, sv))

        _, vjp = jax.vjp(stage, pa_ref[...], pl_ref[...], qaw, kvaw, jnp.zeros(wq.shape, f32), jnp.zeros(wk.shape, f32),
                         jnp.zeros(wv.shape, f32), qnw, knw, kpw)
        grads = vjp((dq_ref[...], dk_ref[...], dv_ref[...]))
        dpa_ref[...] = grads[0]
        dpl_ref[...] = grads[1]
        _accumulate(pl.program_id(0) == 0, dprm_refs, list(grads[2:]))

    tok = lambda w: pl.BlockSpec((ts, w), lambda i: (i, 0))
    head = lambda w: pl.BlockSpec((N_HEADS, ts, w), lambda i: (0, i, 0))
    pshapes = [jax.ShapeDtypeStruct((1, Q_RANK), f32), jax.ShapeDtypeStruct((1, KV_RANK), f32),
               jax.ShapeDtypeStruct((N_HEADS, Q_RANK, HEAD_LANES), f32), jax.ShapeDtypeStruct((N_HEADS, KV_RANK, HEAD_LANES), f32),
               jax.ShapeDtypeStruct((N_HEADS, KV_RANK, V_DIM), f32), jax.ShapeDtypeStruct((1, HEAD_LANES), f32),
               jax.ShapeDtypeStruct((1, HEAD_LANES), f32), jax.ShapeDtypeStruct((1, HEAD_LANES), f32)]
    return pl.pallas_call(
        body, name="qkv_bwd", grid=(s // ts,),
        in_specs=[tok(384), tok(128), tok(128), tok(128)] + _qkv_param_specs()
                 + [head(HEAD_LANES), head(HEAD_LANES), head(V_DIM)],
        out_specs=[tok(384), tok(128)] + _qkv_param_specs(),
        out_shape=[jax.ShapeDtypeStruct((s, 384), f32), jax.ShapeDtypeStruct((s, 128), f32)] + pshapes,
    )(pa, plast, cos_t, sin_t, *params, dq, dk, dv)


def proj_bwd(x, nw, sh, sc, w, dpa, dpz, dpx, dpl_k, dpl_dt, dres):
    s = x.shape[0]
    ts = _token_block(s)

    ni = s // ts

    def body(x_ref, nw_ref, sh_ref, sc_ref, w_ref, dpa_ref, dpz_ref, dpx_ref, dplk_ref, dpld_ref, dres_ref,
             dx_ref, dnw_ref, dsh_ref, dsc_ref, dw_ref, acc_sc):
        i = pl.program_id(0)
        g = jnp.concatenate([dpa_ref[...], dpz_ref[...], dpx_ref[...], dplk_ref[...] + dpld_ref[...]], axis=1)
        w = w_ref[...]
        _, vjp = jax.vjp(lambda x_, nw_, sh_, sc_, slot: _f_proj(x_, nw_, sh_, sc_, w, slot), x_ref[...], nw_ref[...],
                         sh_ref[...], sc_ref[...], jnp.zeros(w.shape, f32))
        dx, dnw, dsh, dsc, dw = vjp(g)
        dx_ref[...] = dx + dres_ref[...]
        _accumulate(i == 0, [dnw_ref, dsh_ref, dsc_ref], [dnw, dsh, dsc])
        _accumulate_then_cast(i == 0, i == ni - 1, [acc_sc], [dw_ref], [dw])

    vec = _const((1, D_MODEL))
    vshape = jax.ShapeDtypeStruct((1, D_MODEL), f32)
    tok = lambda w_: pl.BlockSpec((ts, w_), lambda i: (i, 0))
    return pl.pallas_call(
        body, name="proj_bwd", grid=(ni,), scratch_shapes=[pltpu.VMEM((D_PROJ, D_MODEL), f32)],
        in_specs=[tok(D_MODEL), vec, vec, vec, _const((D_PROJ, D_MODEL)), tok(384), tok(512), tok(1024), tok(128), tok(128),
                  tok(D_MODEL)],
        out_specs=[tok(D_MODEL), vec, vec, vec, _const((D_PROJ, D_MODEL))],
        out_shape=[jax.ShapeDtypeStruct((s, D_MODEL), f32), vshape, vshape, vshape,
                   jax.ShapeDtypeStruct((D_PROJ, D_MODEL), bf16)],
    )(x, nw, sh, sc, w, dpa, dpz, dpx, dpl_k, dpl_dt, dres)


def ada_fwd(c_all, w_ada, b_cols):
    def body(c_ref, w_ref, b_ref, out_ref):
        act = jax.nn.silu(c_ref[...])
        for l in range(2):
            out_ref[l] = jnp.dot(act, w_ref[l], precision=lax.Precision.HIGHEST, preferred_element_type=f32) + b_ref[l]

    return pl.pallas_call(body, name="ada_fwd", out_shape=jax.ShapeDtypeStruct((2, N_DEV, 768), f32))(c_all, w_ada, b_cols)


def ada_bwd(c_all, dmod_cols):
    def body(c_ref, d_ref, out_ref):
        out_ref[0] = lax.dot_general(jax.nn.silu(c_ref[...]), d_ref[0], (((0,), (0,)), ((), ())),
                                     precision=lax.Precision.HIGHEST, preferred_element_type=f32)

    return pl.pallas_call(
        body, name="ada_bwd", grid=(2,),
        in_specs=[_const((N_DEV, D_MODEL)), pl.BlockSpec((1, N_DEV, 768), lambda l: (l, 0, 0))],
        out_specs=pl.BlockSpec((1, D_MODEL, 768), lambda l: (l, 0, 0)),
        out_shape=jax.ShapeDtypeStruct((2, D_MODEL, 768), f32),
    )(c_all, dmod_cols)


def _adamw(w, g, m, v):
    m = ADAM_B1 * m + (1.0 - ADAM_B1) * g
    v = ADAM_B2 * v + (1.0 - ADAM_B2) * (g * g)
    m_hat = m / (1.0 - ADAM_B1 ** ADAM_STEP)
    v_hat = v / (1.0 - ADAM_B2 ** ADAM_STEP)
    delta = -ADAM_LR * (m_hat / (jnp.sqrt(v_hat) + ADAM_EPS) + ADAM_WD * w)
    return delta, m, v


def adamw(parts, w, m, v, layer, prev, name):
    n, r, c = parts.shape
    nl = w.shape[0]
    per_elem = 2 * (n * parts.dtype.itemsize + 7 * 4)
    lanes = -(-c // 128) * 128
    tr, tc = r, c
    if per_elem * r * lanes > ADAMW_BLOCK_BYTES:
        fits = [t for t in (256, 128, 64, 32, 16, 8) if r % t == 0]
        if fits:
            tr = fits[0]
        else:
            tc = next(t for t in (512, 256, 128) if c % t == 0)

    def body(p_ref, w_ref, m_ref, v_ref, *rest):
        g_ref, d_ref, nm_ref, nv_ref = rest[-4:]
        g = p_ref[0].astype(f32)
        for k in range(1, n):
            g = g + p_ref[k].astype(f32)
        delta, nm, nv = _adamw(w_ref[...], g, m_ref[...], v_ref[...])
        g_ref[...] = g
        d_ref[...] = delta
        nm_ref[...] = nm
        nv_ref[...] = nv

    blk = pl.BlockSpec((None, tr, tc), lambda i, j: (layer, i, j))
    shp = jax.ShapeDtypeStruct((nl, r, c), f32)
    kept = [] if prev is None else list(prev)
    return pl.pallas_call(
        body, name=name, grid=(r // tr, c // tc),
        in_specs=[pl.BlockSpec((n, tr, tc), lambda i, j: (0, i, j)), blk, blk, blk] + [ANY] * len(kept),
        out_specs=[blk] * 4, out_shape=[shp] * 4,
        input_output_aliases={4 + j: j for j in range(len(kept))},
    )(parts, w, m, v, *kept)


def _my_index():
    return 4 * lax.axis_index("x") + 2 * lax.axis_index("y") + lax.axis_index("c")


def _coords(idx):
    return (idx // 4, (idx // 2) % 2, idx % 2)


class CommJob:
    def __init__(self, operands, out_shape, phases, scratch):
        self.operands, self.out_shape, self.phases, self.scratch = operands, out_shape, phases, scratch


def _wait(out, n_blocks, send_sem, recv_sem, send=True, recv=True):
    span = out.at[pl.ds(0, n_blocks)]
    desc = pltpu.make_async_remote_copy(src_ref=span, dst_ref=span, send_sem=send_sem, recv_sem=recv_sem,
                                        device_id=_coords(_my_index()), device_id_type=MESH)
    if recv:
        desc.wait_recv()
    if send:
        desc.wait_send()


def gather_job(shards):
    n = len(shards)

    def places():
        x, y, c = lax.axis_index("x"), lax.axis_index("y"), lax.axis_index("c")
        return (x, y, c), (x, y, 1 - c), [(1 - x, y), (x, 1 - y), (1 - x, 1 - y)]

    def index(p):
        return 4 * p[0] + 2 * p[1] + p[2]

    def start(ins, outs, sems):
        far_send, far_recv, near_send, near_recv, local = sems
        me, sibling, chips = places()
        for k in range(n):
            pltpu.make_async_copy(ins[k], outs[k].at[index(me)], local.at[k]).start()
            for chip in chips:
                pltpu.make_async_remote_copy(src_ref=ins[k], dst_ref=outs[k].at[index(me)], send_sem=far_send.at[k],
                                             recv_sem=far_recv.at[k], device_id=(*chip, me[2]), device_id_type=MESH).start()
            pltpu.make_async_remote_copy(src_ref=ins[k], dst_ref=outs[k].at[index(me)], send_sem=near_send.at[k],
                                         recv_sem=near_recv.at[k], device_id=sibling, device_id_type=MESH).start()

    def relay(ins, outs, sems):
        far_send, far_recv, near_send, near_recv, local = sems
        me, sibling, chips = places()
        for k in range(n):
            _wait(outs[k], 3, far_send.at[k], far_recv.at[k], send=False)
            for chip in chips:
                block = outs[k].at[index((*chip, me[2]))]
                pltpu.make_async_remote_copy(src_ref=block, dst_ref=block, send_sem=near_send.at[k],
                                             recv_sem=near_recv.at[k], device_id=sibling, device_id_type=MESH).start()

    def finish(ins, outs, sems):
        far_send, far_recv, near_send, near_recv, local = sems
        for k in range(n):
            _wait(outs[k], 4, near_send.at[k], near_recv.at[k])
            _wait(outs[k], 3, far_send.at[k], far_recv.at[k], recv=False)
            pltpu.make_async_copy(ins[k], outs[k].at[0], local.at[k]).wait()

    shapes = [jax.ShapeDtypeStruct((N_DEV,) + tuple(a.shape), a.dtype) for a in shards]
    return CommJob(list(shards), shapes, [start, relay, finish], [pltpu.SemaphoreType.DMA((n,))] * 5)


def scatter_job(tensors):
    n = len(tensors)
    flat, where = [], {}
    for k, pieces in enumerate(tensors):
        d = 0
        for piece in pieces:
            for b in range(piece.shape[0]):
                where[k, d] = (len(flat), b)
                d += 1
            flat.append(piece)
        assert d == N_DEV

    def start(ins, outs, sems):
        send_sems, recv_sems, local_sems = sems
        me = _my_index()

        def block(k, d):
            i, b = where[k, d]
            return ins[i].at[b]

        for d in range(N_DEV):
            @pl.when(d != me)
            def _():
                for k in range(n):
                    pltpu.make_async_remote_copy(src_ref=block(k, d), dst_ref=outs[k].at[me], send_sem=send_sems.at[k],
                                                 recv_sem=recv_sems.at[k], device_id=(d // 4, (d // 2) % 2, d % 2),
                                                 device_id_type=MESH).start()

            @pl.when(d == me)
            def _():
                for k in range(n):
                    pltpu.make_async_copy(block(k, d), outs[k].at[d], local_sems.at[k]).start()

    def finish(ins, outs, sems):
        send_sems, recv_sems, local_sems = sems
        for k in range(n):
            _wait(outs[k], N_DEV - 1, send_sems.at[k], recv_sems.at[k])
            i, b = where[k, 0]
            pltpu.make_async_copy(ins[i].at[b], outs[k].at[0], local_sems.at[k]).wait()

    shapes = [jax.ShapeDtypeStruct((N_DEV,) + tuple(p[0].shape[1:]), p[0].dtype) for p in tensors]
    return CommJob(flat, shapes, [start, finish], [pltpu.SemaphoreType.DMA((n,))] * 3)


def comm_call(job, name):
    ni, no = len(job.operands), len(job.out_shape)

    def body(*refs):
        ins, outs, sems = refs[:ni], refs[ni:ni + no], refs[ni + no:]
        for phase in job.phases:
            phase(ins, outs, sems)

    return pl.pallas_call(body, name=name, in_specs=[ANY] * ni, out_specs=[ANY] * no, out_shape=job.out_shape,
                          scratch_shapes=job.scratch)(*job.operands)


def _carry(job, body, n_in, n_out, at_step):
    ji, jo, js = len(job.operands), len(job.out_shape), len(job.scratch)

    def carrier(*refs):
        a, b = n_in, n_in + ji
        c, d = b + n_out, b + n_out + jo
        e = len(refs) - js
        job_refs = (refs[a:b], refs[c:d], refs[e:])
        n = len(job.phases)

        @pl.when(at_step(0, n))
        def _():
            job.phases[0](*job_refs)

        body(*refs[:a], *refs[b:c], *refs[d:e])

        for i in range(1, n):
            @pl.when(at_step(i, n))
            def _():
                job.phases[i](*job_refs)

    return carrier


def _pad_lanes(v, lo, total=128):
    return jnp.pad(v, (lo, total - lo - v.shape[0]))[None, :]


MIXER_WEIGHTS = ("w_in", "w_q_up", "w_kv_up", "conv_w")
LATE_WEIGHTS = ("w_out", "w_gate_up", "w_down")


def mixer_operands(g, sw):
    w_in = g["w_in"].reshape(D_IN, D_MODEL)
    zero = lambda rows: jnp.zeros((rows, D_MODEL), w_in.dtype)
    w_proj = jnp.concatenate(
        [w_in[:384], w_in[416:928], w_in[928:1952], w_in[1952:1960], zero(56), w_in[384:416], zero(32)], axis=0)
    wq = jnp.pad(g["w_q_up"], ((0, 0), (0, 0), (0, HEAD_LANES - NOPE - ROPE)))
    wk = jnp.pad(g["w_kv_up"][:, :, :NOPE], ((0, 0), (0, 0), (0, HEAD_LANES - NOPE)))
    wv = g["w_kv_up"][:, :, NOPE:]
    qkv = (sw["q_a_norm_w"][None, :], sw["kv_a_norm_w"][None, :], wq, wk, wv,
           _pad_lanes(jnp.concatenate([sw["q_nope_norm_w"], sw["q_pe_norm_w"]]), 0),
           _pad_lanes(sw["k_nope_norm_w"], 0), _pad_lanes(sw["k_pe_norm_w"], NOPE))
    conv_w = g["conv_w"].astype(f32).transpose(1, 0, 2).reshape(4, D_CONV)
    ssd = (conv_w, sw["conv_b"][None, :], _pad_lanes(sw["dt_bias"], 0), _pad_lanes(sw["a_log"], 0),
           _pad_lanes(sw["d_skip"], 0), sw["ssd_norm_w"][None, :])
    return dict(w_proj=w_proj, qkv=qkv, ssd=ssd, n1=sw["norm1_w"][None, :])


def late_operands(g, sw):
    return dict(wo=g["w_out"].reshape(D_MODEL, D_MODEL), wgu=g["w_gate_up"],
                wd=g["w_down"].reshape(N_DEV // 2, FF_SHARD, D_MODEL), n2=sw["norm2_w"][None, :])


def layer_fwd(x, mod, kw, cos_t, sin_t, job=None, late=None):
    sh1, sc1, g1, sh2, sc2, g2 = [mod[i:i + 1] for i in range(6)]
    pa, pz, px, plast = proj_fwd(x, kw["n1"], sh1, sc1, kw["w_proj"])
    q, k, v = qkv_fwd(pa, plast, cos_t, sin_t, kw["qkv"])
    (o, qx), carried = attn_fwd(q, k, v, job)
    if late is not None:
        kw = {**kw, **late(carried)}
    yg, states = ssd_fwd(px, pz, plast, kw["ssd"])
    x_mid = out_fwd(x, o, yg, g1, kw["wo"])
    x_out, mix, h_mid, gate, up = mlp_fwd(x_mid, kw["n2"], sh2, sc2, g2, kw["wgu"], kw["wd"])
    saved = dict(x=x, pa=pa, pz=pz, px=px, plast=plast, qx=qx, k=k, v=v, o=o, yg=yg, states=states, x_mid=x_mid,
                 mix=mix, h_mid=h_mid, gate=gate, up=up)
    return x_out, saved, kw, carried


def layer_bwd_head(dy, mod, kw, sv, job=None):
    _, _, g1, sh2, sc2, g2 = [mod[i:i + 1] for i in range(6)]
    (dhparts, dwg, dwu, dwd), carried = mlp_bwd(sv["h_mid"], dy, sv["gate"], sv["up"], g2, kw["wgu"], kw["wd"], job)
    dmid, dn2, dsh2, dsc2, do, dyg, dg1, dg2, dwo = out_bwd(
        dy, dhparts, sv["x_mid"], kw["n2"], sh2, sc2, sv["mix"], sv["o"], sv["yg"], g1, kw["wo"])
    early = dict(w_out=[dwo.reshape(N_DEV, D_MODEL // N_DEV, D_MODEL)], w_gate_up=[dwg, dwu],
                 w_down=[dwd.reshape(N_DEV, D_FF // N_DEV, D_MODEL)])
    head = dict(dmid=dmid, do=do, dyg=dyg, dn2=dn2, dsh2=dsh2, dsc2=dsc2, dg2=dg2, dg1=dg1)
    return head, early, carried


def layer_bwd_tail(hd, mod, kw, cos_t, sin_t, sv, job=None):
    sh1, sc1 = mod[0:1], mod[1:2]
    (dq, dk, dv), carried = attn_bwd(sv["qx"], sv["k"], sv["v"], hd["do"], job)
    dpx, dpz, dpl_dt, dcw, dcb, ddtb, dalog, ddskip, dsnw = ssd_bwd(sv["px"], sv["pz"], sv["plast"], sv["states"],
                                                                   hd["dyg"], kw["ssd"])
    dpa, dpl_k, dqaw, dkvaw, dwq, dwk, dwv, dqnw, dknw, dkpw = qkv_bwd(sv["pa"], sv["plast"], cos_t, sin_t, kw["qkv"],
                                                                       dq, dk, dv)
    dx, dn1, dsh1, dsc1, dwp = proj_bwd(sv["x"], kw["n1"], sh1, sc1, kw["w_proj"], dpa, dpz, dpx, dpl_k, dpl_dt, hd["dmid"])
    dmod = jnp.concatenate([dsh1, dsc1, hd["dg1"], hd["dsh2"], hd["dsc2"], hd["dg2"]], axis=0)
    dw_in = jnp.concatenate([dwp[:384], dwp[1984:2016], dwp[384:1920], dwp[1920:1928]], axis=0)
    grads = dict(
        norm1_w=dn1[0], norm2_w=hd["dn2"][0], q_a_norm_w=dqaw[0], kv_a_norm_w=dkvaw[0],
        q_nope_norm_w=dqnw[0, :NOPE], q_pe_norm_w=dqnw[0, NOPE:NOPE + ROPE], k_nope_norm_w=dknw[0, :NOPE],
        k_pe_norm_w=dkpw[0, NOPE:NOPE + ROPE], conv_b=dcb[0], dt_bias=ddtb[0, :N_HEADS], a_log=dalog[0, :N_HEADS],
        d_skip=ddskip[0, :N_HEADS], ssd_norm_w=dsnw[0],
        w_in=[dw_in.reshape(N_DEV, D_IN // N_DEV, D_MODEL)],
        w_q_up=[dwq[:, :, :NOPE + ROPE].astype(bf16)],
        w_kv_up=[jnp.concatenate([dwk[:, :, :NOPE], dwv], axis=2).astype(bf16)],
        conv_w=[dcw.reshape(4, N_DEV, D_CONV // N_DEV).transpose(1, 0, 2).astype(bf16)],
    )
    return dx, dmod, grads, carried


def _pack_small(get, last=None):
    flat = jnp.concatenate([get(name).reshape(-1) for name, _ in SMALL])
    flat = jnp.pad(flat, (0, SMALL_ROWS * 128 - flat.shape[0]))
    if last is not None:
        flat = flat.at[-1].set(last)
    return flat.reshape(SMALL_ROWS, 128)


def _unpack_small(packed):
    flat = packed.reshape(-1)
    out, off = {}, 0
    for name, size in SMALL:
        out[name] = flat[off:off + 2 * size].reshape(2, size)
        off += 2 * size
    return out


def kernel(x, c, positions, norm1_w, norm2_w, w_ada, b_ada, w_in, q_a_norm_w, w_q_up, kv_a_norm_w, w_kv_up, q_nope_norm_w, q_pe_norm_w, k_nope_norm_w, k_pe_norm_w, conv_w, conv_b, dt_bias, a_log, d_skip, ssd_norm_w, w_out, w_gate_up, w_down, loss_target, m_norm1_w, m_norm2_w, m_w_ada, m_b_ada, m_w_in, m_q_a_norm_w, m_w_q_up, m_kv_a_norm_w, m_w_kv_up, m_q_nope_norm_w, m_q_pe_norm_w, m_k_nope_norm_w, m_k_pe_norm_w, m_conv_w, m_conv_b, m_dt_bias, m_a_log, m_d_skip, m_ssd_norm_w, m_w_out, m_w_gate_up, m_w_down, v_norm1_w, v_norm2_w, v_w_ada, v_b_ada, v_w_in, v_q_a_norm_w, v_w_q_up, v_kv_a_norm_w, v_w_kv_up, v_q_nope_norm_w, v_q_pe_norm_w, v_k_nope_norm_w, v_k_pe_norm_w, v_conv_w, v_conv_b, v_dt_bias, v_a_log, v_d_skip, v_ssd_norm_w, v_w_out, v_w_gate_up, v_w_down):
    w = dict(norm1_w=norm1_w, norm2_w=norm2_w, w_ada=w_ada, b_ada=b_ada, w_in=w_in, q_a_norm_w=q_a_norm_w, w_q_up=w_q_up,
             kv_a_norm_w=kv_a_norm_w, w_kv_up=w_kv_up, q_nope_norm_w=q_nope_norm_w, q_pe_norm_w=q_pe_norm_w,
             k_nope_norm_w=k_nope_norm_w, k_pe_norm_w=k_pe_norm_w, conv_w=conv_w, conv_b=conv_b, dt_bias=dt_bias,
             a_log=a_log, d_skip=d_skip, ssd_norm_w=ssd_norm_w, w_out=w_out, w_gate_up=w_gate_up, w_down=w_down)
    m = dict(norm1_w=m_norm1_w, norm2_w=m_norm2_w, w_ada=m_w_ada, b_ada=m_b_ada, w_in=m_w_in, q_a_norm_w=m_q_a_norm_w,
             w_q_up=m_w_q_up, kv_a_norm_w=m_kv_a_norm_w, w_kv_up=m_w_kv_up, q_nope_norm_w=m_q_nope_norm_w,
             q_pe_norm_w=m_q_pe_norm_w, k_nope_norm_w=m_k_nope_norm_w, k_pe_norm_w=m_k_pe_norm_w, conv_w=m_conv_w,
             conv_b=m_conv_b, dt_bias=m_dt_bias, a_log=m_a_log, d_skip=m_d_skip, ssd_norm_w=m_ssd_norm_w, w_out=m_w_out,
             w_gate_up=m_w_gate_up, w_down=m_w_down)
    v = dict(norm1_w=v_norm1_w, norm2_w=v_norm2_w, w_ada=v_w_ada, b_ada=v_b_ada, w_in=v_w_in, q_a_norm_w=v_q_a_norm_w,
             w_q_up=v_w_q_up, kv_a_norm_w=v_kv_a_norm_w, w_kv_up=v_w_kv_up, q_nope_norm_w=v_q_nope_norm_w,
             q_pe_norm_w=v_q_pe_norm_w, k_nope_norm_w=v_k_nope_norm_w, k_pe_norm_w=v_k_pe_norm_w, conv_w=v_conv_w,
             conv_b=v_conv_b, dt_bias=v_dt_bias, a_log=v_a_log, d_skip=v_d_skip, ssd_norm_w=v_ssd_norm_w, w_out=v_w_out,
             w_gate_up=v_w_gate_up, w_down=v_w_down)
    me = _my_index()
    seq = x.shape[1]

    def shard(name, l):
        if name == "conv_w":
            return w[name][l]
        if name in TRANSPOSED:
            return jnp.swapaxes(w[name][l], 0, 1).astype(bf16)
        return w[name][l].astype(bf16)

    def shards(names, l):
        return [shard(name, l) for name in names]

    small = [{name: w[name][l] for name, _ in SMALL if name != "b_ada"} for l in range(2)]
    n_mix, n_late = len(MIXER_WEIGHTS), len(LATE_WEIGHTS)

    first = comm_call(gather_job([c] + shards(MIXER_WEIGHTS, 0)), "gather_first")
    c_all = first[0].reshape(N_DEV, D_MODEL)
    kws = [mixer_operands(dict(zip(MIXER_WEIGHTS, first[1:])), small[0]), None]

    b_cols = lax.dynamic_slice_in_dim(b_ada, me * 768, 768, axis=1)
    mod_cols = ada_fwd(c_all, w_ada, b_cols)
    (mod_all,) = comm_call(gather_job([mod_cols]), "gather_mod")
    mod_me = lax.dynamic_index_in_dim(mod_all, me, axis=2, keepdims=False)
    mods = [mod_me[:, l, :].reshape(6, D_MODEL) for l in range(2)]

    inv_freq = 1.0 / (ROPE_THETA ** (jnp.arange(0, ROPE, 2, dtype=f32) / ROPE))
    inv = _pad_lanes(jnp.concatenate([inv_freq, inv_freq]), NOPE)
    cos_t, sin_t = rope_tables(positions.reshape(seq, 1), inv)

    saved = [None, None]
    h, saved[0], kws[0], got = layer_fwd(
        x[0], mods[0], kws[0], cos_t, sin_t, gather_job(shards(LATE_WEIGHTS, 0) + shards(MIXER_WEIGHTS, 1)),
        lambda got: late_operands(dict(zip(LATE_WEIGHTS, got[:n_late])), small[0]))
    kws[1] = mixer_operands(dict(zip(MIXER_WEIGHTS, got[n_late:])), small[1])
    h, saved[1], kws[1], _ = layer_fwd(
        h, mods[1], kws[1], cos_t, sin_t, gather_job(shards(LATE_WEIGHTS, 1)),
        lambda got: late_operands(dict(zip(LATE_WEIGHTS, got)), small[1]))
    dy, loss_part = loss_fwd(h, loss_target[0])

    early, late = ("w_out", "w_gate_up", "w_down"), ("w_in", "w_q_up", "w_kv_up", "conv_w")
    parts = [{}, {}]
    head, pieces, _ = layer_bwd_head(dy, mods[1], kws[1], saved[1])
    dy, dmod1, grads1, got = layer_bwd_tail(head, mods[1], kws[1], cos_t, sin_t, saved[1], scatter_job([pieces[n] for n in early]))
    parts[1].update(zip(early, got))
    head, pieces, got = layer_bwd_head(dy, mods[0], kws[0], saved[0], scatter_job([grads1[n] for n in late]))
    parts[1].update(zip(late, got))
    dy, dmod0, grads0, got = layer_bwd_tail(head, mods[0], kws[0], cos_t, sin_t, saved[0], scatter_job([pieces[n] for n in early]))
    parts[0].update(zip(early, got))
    parts[0].update(zip(late, comm_call(scatter_job([grads0[n] for n in late]), "scatter_layer0_rest")))
    grad_x = dy[None]

    small_part = {name: jnp.stack([grads0[name], grads1[name]]) for name, _ in SMALL if name != "b_ada"}
    small_part["b_ada"] = jnp.stack([dmod0.reshape(-1), dmod1.reshape(-1)])
    (small_all,) = comm_call(gather_job([_pack_small(lambda n: small_part[n], loss_part[0, 0])]), "gather_small_grads")
    packed = adamw(small_all, _pack_small(lambda n: w[n])[None], _pack_small(lambda n: m[n])[None],
                   _pack_small(lambda n: v[n])[None], 0, None, "adamw_small")
    loss = packed[0][0, -1, -1]
    res = {}
    for key, arr in zip("gdmv", packed):
        for name, val in _unpack_small(arr[0]).items():
            res[key, name] = val

    off = 2 * (1024 + 1024)
    dmod_all = small_all.reshape(N_DEV, -1)[:, off:off + 2 * 6144].reshape(N_DEV, 2, 6144)
    dmod_cols = lax.dynamic_slice_in_dim(dmod_all, me * 768, 768, axis=2).transpose(1, 0, 2)
    g_ada = ada_bwd(c_all, dmod_cols)
    out = None
    for l in range(2):
        out = adamw(g_ada[l][None], w_ada, m_w_ada, v_w_ada, l, out, "adamw_w_ada")
    res.update(zip([(key, "w_ada") for key in "gdmv"], out))

    for name in BIG:
        view = (lambda a: jnp.swapaxes(a, 1, 2)) if name in TRANSPOSED else (lambda a: a)
        out = None
        for l in range(2):
            out = adamw(parts[l][name], view(w[name]), view(m[name]), view(v[name]), l, out, "adamw_" + name)
        res.update(zip([(key, name) for key in "gdmv"], [view(a) for a in out]))

    return (loss, grad_x, *[res["g", n] for n in WEIGHTS], *[res["d", n] for n in WEIGHTS],
            *[res["m", n] for n in WEIGHTS], *[res["v", n] for n in WEIGHTS])
```

```python
import functools

import jax
import jax.numpy as jnp
from jax import lax
from jax.experimental import pallas as pl
from jax.experimental.pallas import tpu as pltpu

f32 = jnp.float32
bf16 = jnp.bfloat16

N_DEV = 8
D_MODEL = 1024
N_HEADS = 8
HEAD_LANES = 128
NOPE = 64
ROPE = 32
V_DIM = 64
Q_RANK = 256
KV_RANK = 128
D_SSD = 512
D_CONV = 1024
SSD_STATE = 128
SSD_HEAD_DIM = 64
CHUNK = 128
HALO = 8
D_FF = 2816
FF_SHARD = 704
D_IN = 1960
D_PROJ = 2048
EPS = 1e-6
LOG2E = 1.4426950408889634
LN2 = 0.6931471805599453
Q_SCALE = (NOPE + ROPE) ** -0.5 * LOG2E
SPARE_Q = NOPE + ROPE
SPARE_V = V_DIM
ATTN_ROWS_FWD = 256
ATTN_HEADS_FWD = 8
ATTN_HEADS_BWD = 4
MLP_FWD_ROWS = 1024
MLP_BWD_CHUNK = 256
MLP_BWD_ROWS = 1024
ROPE_THETA = 10000.0
NEG = -1e30

ADAM_LR = 0.001
ADAM_B1 = 0.9
ADAM_B2 = 0.999
ADAM_EPS = 1e-08
ADAM_WD = 0.01
ADAM_STEP = 10
ADAMW_BLOCK_BYTES = 24 << 20

MESH = pl.DeviceIdType.MESH
ANY = pl.BlockSpec(memory_space=pl.ANY)

SMALL = (("norm1_w", 1024), ("norm2_w", 1024), ("b_ada", 6144), ("q_a_norm_w", 256), ("kv_a_norm_w", 128),
         ("q_nope_norm_w", 64), ("q_pe_norm_w", 32), ("k_nope_norm_w", 64), ("k_pe_norm_w", 32),
         ("conv_b", 1024), ("dt_bias", 8), ("a_log", 8), ("d_skip", 8), ("ssd_norm_w", 512))
SMALL_ROWS = 168
BIG = ("w_in", "w_q_up", "w_kv_up", "conv_w", "w_out", "w_gate_up", "w_down")
TRANSPOSED = ("w_in", "w_gate_up")
WEIGHTS = ("norm1_w", "norm2_w", "w_ada", "b_ada", "w_in", "q_a_norm_w", "w_q_up", "kv_a_norm_w", "w_kv_up",
           "q_nope_norm_w", "q_pe_norm_w", "k_nope_norm_w", "k_pe_norm_w", "conv_w", "conv_b", "dt_bias",
           "a_log", "d_skip", "ssd_norm_w", "w_out", "w_gate_up", "w_down")


def _dot(a, b, ca, cb):
    return lax.dot_general(a.astype(bf16), b.astype(bf16), (((ca,), (cb,)), ((), ())), preferred_element_type=f32)


@jax.custom_vjp
def mm(a, b):
    return _dot(a, b, 1, 0)


def _mm_fwd(a, b):
    return _dot(a, b, 1, 0), (a, b)


def _mm_bwd(res, g):
    a, b = res
    return _dot(g, b, 1, 1).astype(a.dtype), _dot(a, g, 0, 0).astype(b.dtype)


mm.defvjp(_mm_fwd, _mm_bwd)


@jax.custom_vjp
def _mm_slot(a, w, slot):
    return _dot(a, w, 1, 0)


def _mm_slot_fwd(a, w, slot):
    return _dot(a, w, 1, 0), (a, w)


def _mm_slot_bwd(res, g):
    a, w = res
    return _dot(g, w, 1, 1).astype(a.dtype), None, _dot(a, g, 0, 0)


_mm_slot.defvjp(_mm_slot_fwd, _mm_slot_bwd)


def mmw(a, w, slot=None):
    return _dot(a, w, 1, 0) if slot is None else _mm_slot(a, w, slot)


@jax.custom_vjp
def _mm_slot_t(a, wt, slot):
    return _dot(a, wt, 1, 1)


def _mm_slot_t_fwd(a, wt, slot):
    return _dot(a, wt, 1, 1), (a, wt)


def _mm_slot_t_bwd(res, g):
    a, wt = res
    return _dot(g, wt, 1, 0).astype(a.dtype), None, _dot(g, a, 0, 0)


_mm_slot_t.defvjp(_mm_slot_t_fwd, _mm_slot_t_bwd)


def mmw_t(a, wt, slot=None):
    return _dot(a, wt, 1, 1) if slot is None else _mm_slot_t(a, wt, slot)


@jax.custom_vjp
def mm_nt(a, b):
    return _dot(a, b, 1, 1)


def _mm_nt_fwd(a, b):
    return _dot(a, b, 1, 1), (a, b)


def _mm_nt_bwd(res, g):
    a, b = res
    return _dot(g, b, 1, 0).astype(a.dtype), _dot(g, a, 0, 0).astype(b.dtype)


mm_nt.defvjp(_mm_nt_fwd, _mm_nt_bwd)


@jax.custom_vjp
def mm_tn(a, b):
    return _dot(a, b, 0, 0)


def _mm_tn_fwd(a, b):
    return _dot(a, b, 0, 0), (a, b)


def _mm_tn_bwd(res, g):
    a, b = res
    return _dot(b, g, 1, 1).astype(a.dtype), _dot(a, g, 1, 0).astype(b.dtype)


mm_tn.defvjp(_mm_tn_fwd, _mm_tn_bwd)


def _rms(x, w):
    return x * lax.rsqrt(jnp.mean(x * x, axis=-1, keepdims=True) + EPS) * w


def _const(shape):
    n = len(shape)
    return pl.BlockSpec(shape, lambda *_: (0,) * n)


def _accumulate(first, refs, vals):
    @pl.when(first)
    def _():
        for r, v in zip(refs, vals):
            r[...] = v

    @pl.when(jnp.logical_not(first))
    def _():
        for r, v in zip(refs, vals):
            r[...] += v


def _accumulate_then_cast(first, last, accs, outs, vals):
    _accumulate(first, accs, vals)

    @pl.when(last)
    def _():
        for a, o in zip(accs, outs):
            o[...] = a[...].astype(o.dtype)


def _token_block(s):
    return min(512, s)


def _f_proj(x, nw, sh, sc, w, slot=None):
    h = _rms(x, nw) * (1.0 + sc) + sh
    return mmw_t(h, w, slot)


def _f_qkv(pa, plast, cos_t, sin_t, qaw, kvaw, wq, wk, wv, qnw, knw, kpw, slots=None):
    sq, sk, sv = slots if slots is not None else ([None] * N_HEADS,) * 3
    lane = lax.broadcasted_iota(jnp.int32, (1, HEAD_LANES), 1)
    m_nope = lane < NOPE
    m_pe = (lane >= NOPE) & (lane < NOPE + ROPE)
    rows = pa.shape[0]

    def rope(t):
        half = ROPE // 2
        swapped = jnp.concatenate(
            [jnp.zeros((rows, NOPE), f32), t[:, NOPE + half:NOPE + ROPE], t[:, NOPE:NOPE + half],
             jnp.zeros((rows, HEAD_LANES - NOPE - ROPE), f32)], axis=1)
        return t * cos_t + swapped * sin_t

    qa = _rms(pa[:, :Q_RANK], qaw)
    kva = _rms(pa[:, Q_RANK:Q_RANK + KV_RANK], kvaw)
    kp = jnp.where(m_pe, plast, 0.0)
    kp = kp * lax.rsqrt(jnp.sum(kp * kp, axis=-1, keepdims=True) / ROPE + EPS) * kpw
    k_rot = rope(kp)
    qs, ks, vs = [], [], []
    for h in range(N_HEADS):
        qh = mmw(qa, wq[h], sq[h])
        ss_n = jnp.sum(jnp.where(m_nope, qh * qh, 0.0), axis=-1, keepdims=True) / NOPE
        ss_p = jnp.sum(jnp.where(m_pe, qh * qh, 0.0), axis=-1, keepdims=True) / ROPE
        r = jnp.where(m_nope, lax.rsqrt(ss_n + EPS), lax.rsqrt(ss_p + EPS))
        qs.append(rope(qh * r * qnw) * Q_SCALE)
        kh = mmw(kva, wk[h], sk[h])
        kh = kh * lax.rsqrt(jnp.sum(kh * kh, axis=-1, keepdims=True) / NOPE + EPS) * knw
        ks.append(kh + k_rot)
        vs.append(mmw(kva, wv[h], sv[h]))
    return jnp.stack(qs), jnp.stack(ks), jnp.stack(vs)


def _f_ssd(xext, z, plast, prev, cw, cb, dtb, alog, dskip, snw):
    n = CHUNK
    conv = cb
    for k in range(4):
        conv = conv + cw[k:k + 1] * xext[HALO - 3 + k:HALO - 3 + k + n]
    xc = jax.nn.silu(conv)
    xs, bm, cm = xc[:, :D_SSD], xc[:, D_SSD:D_SSD + 2 * SSD_STATE], xc[:, D_SSD + 2 * SSD_STATE:]
    lane = lax.broadcasted_iota(jnp.int32, (1, 128), 1)
    dt = jax.nn.softplus(jnp.where(lane < N_HEADS, plast, 0.0) + dtb)
    adt = dt * (-jnp.exp(alog))
    row = lax.broadcasted_iota(jnp.int32, (n, n), 0)
    col = lax.broadcasted_iota(jnp.int32, (n, n), 1)
    tri = row >= col
    acs = jnp.dot(tri.astype(f32), adt, precision=lax.Precision.HIGHEST, preferred_element_type=f32)
    acs_t = acs.T
    bgs = [bm[:, g * SSD_STATE:(g + 1) * SSD_STATE] for g in range(2)]
    cgs = [cm[:, g * SSD_STATE:(g + 1) * SSD_STATE] for g in range(2)]
    cb_ts = [mm_nt(cgs[g], bgs[g]) for g in range(2)]
    low = lane < SSD_HEAD_DIM
    low_rows = lax.broadcasted_iota(jnp.int32, (2 * SSD_HEAD_DIM, 1), 0) < SSD_HEAD_DIM

    def both(a0, a1):
        return jnp.where(low, a0, a1)

    pre = []
    for i in range(N_HEADS // 2):
        h0, h1 = 2 * i, 2 * i + 1
        col0, col1 = acs[:, h0:h0 + 1], acs[:, h1:h1 + 1]
        last0, last1 = acs[n - 1:n, h0:h0 + 1], acs[n - 1:n, h1:h1 + 1]
        cb_t = cb_ts[i // 2]
        scores0 = cb_t * jnp.exp(jnp.where(tri, col0 - acs_t[h0:h0 + 1, :], -jnp.inf))
        scores1 = cb_t * jnp.exp(jnp.where(tri, col1 - acs_t[h1:h1 + 1, :], -jnp.inf))
        xp = xs[:, i * 128:(i + 1) * 128]
        xdt = xp * both(dt[:, h0:h0 + 1], dt[:, h1:h1 + 1])
        weighted = xdt * both(jnp.exp(last0 - col0), jnp.exp(last1 - col1))
        chunk_decay = jnp.where(low_rows, jnp.exp(last0), jnp.exp(last1))
        in_decay = both(jnp.exp(col0), jnp.exp(col1))
        skip = both(dskip[:, h0:h0 + 1], dskip[:, h1:h1 + 1]) * xp
        pre.append((scores0, scores1, xdt, weighted, chunk_decay, in_decay, skip))
    prods = []
    for i in range(N_HEADS // 2):
        scores0, scores1, xdt, weighted, _, _, _ = pre[i]
        g = i // 2
        y_diag = mm(scores0, jnp.where(low, xdt, 0.0)) + mm(scores1, jnp.where(low, 0.0, xdt))
        prods.append((y_diag, mm_tn(weighted, bgs[g]), mm_nt(cgs[g], prev[i])))
    ys, news = [], []
    for i in range(N_HEADS // 2):
        y_diag, st, y_off = prods[i]
        _, _, _, _, chunk_decay, in_decay, skip = pre[i]
        news.append(chunk_decay * prev[i] + st)
        ys.append(y_diag + y_off * in_decay + skip)
    y = jnp.concatenate(ys, axis=1)
    yg = y * jax.nn.silu(z)
    half = D_SSD // 2
    outs = []
    for g in range(2):
        t = yg[:, g * half:(g + 1) * half]
        outs.append(t * lax.rsqrt(jnp.mean(t * t, axis=-1, keepdims=True) + EPS))
    return jnp.concatenate(outs, axis=1) * snw, jnp.stack(news)


def _f_out(o, yg, g1, wo, slot=None):
    cat = jnp.concatenate([o[h] for h in range(N_HEADS)] + [yg], axis=1)
    return g1 * mmw(cat, wo, slot)


def _f_modulate(x, nw, sh, sc):
    return _rms(x, nw) * (1.0 + sc) + sh


def proj_fwd(x, nw, sh, sc, w):
    s = x.shape[0]
    ts = _token_block(s)

    def body(x_ref, nw_ref, sh_ref, sc_ref, w_ref, pa_ref, pz_ref, px_ref, pl_ref):
        p = _f_proj(x_ref[...], nw_ref[...], sh_ref[...], sc_ref[...], w_ref[...])
        pa_ref[...] = p[:, :384]
        pz_ref[...] = p[:, 384:896]
        px_ref[...] = p[:, 896:1920]
        pl_ref[...] = p[:, 1920:]

    vec = _const((1, D_MODEL))
    return pl.pallas_call(
        body, name="proj_fwd", grid=(s // ts,),
        in_specs=[pl.BlockSpec((ts, D_MODEL), lambda i: (i, 0)), vec, vec, vec, _const((D_PROJ, D_MODEL))],
        out_specs=[pl.BlockSpec((ts, 384), lambda i: (i, 0)), pl.BlockSpec((ts, 512), lambda i: (i, 0)),
                   pl.BlockSpec((ts, 1024), lambda i: (i, 0)), pl.BlockSpec((ts, 128), lambda i: (i, 0))],
        out_shape=[jax.ShapeDtypeStruct((s, 384), f32), jax.ShapeDtypeStruct((s, 512), f32),
                   jax.ShapeDtypeStruct((s, 1024), f32), jax.ShapeDtypeStruct((s, 128), f32)],
    )(x, nw, sh, sc, w)


def rope_tables(pos, inv):
    s = pos.shape[0]
    ts = _token_block(s)

    def body(pos_ref, inv_ref, cos_ref, sin_ref):
        ang = pos_ref[...].astype(f32) * inv_ref[...]
        lane = lax.broadcasted_iota(jnp.int32, (1, HEAD_LANES), 1)
        half = ROPE // 2
        cos_ref[...] = jnp.where(lane < NOPE, 1.0, jnp.where(lane < NOPE + ROPE, jnp.cos(ang), 0.0))
        sn = jnp.sin(ang)
        sin_ref[...] = jnp.where((lane >= NOPE) & (lane < NOPE + half), -sn,
                                 jnp.where((lane >= NOPE + half) & (lane < NOPE + ROPE), sn, 0.0))

    return pl.pallas_call(
        body, name="rope_tables", grid=(s // ts,),
        in_specs=[pl.BlockSpec((ts, 1), lambda i: (i, 0)), _const((1, HEAD_LANES))],
        out_specs=[pl.BlockSpec((ts, HEAD_LANES), lambda i: (i, 0))] * 2,
        out_shape=[jax.ShapeDtypeStruct((s, HEAD_LANES), f32)] * 2,
    )(pos, inv)


def _qkv_param_specs():
    return [_const((1, Q_RANK)), _const((1, KV_RANK)), _const((N_HEADS, Q_RANK, HEAD_LANES)),
            _const((N_HEADS, KV_RANK, HEAD_LANES)), _const((N_HEADS, KV_RANK, V_DIM)),
            _const((1, HEAD_LANES)), _const((1, HEAD_LANES)), _const((1, HEAD_LANES))]


def qkv_fwd(pa, plast, cos_t, sin_t, params):
    s = pa.shape[0]
    ts = _token_block(s)

    def body(pa_ref, pl_ref, cos_ref, sin_ref, *rest):
        prm = [r[...] for r in rest[:8]]
        q_ref, k_ref, v_ref = rest[8:]
        q, k, v = _f_qkv(pa_ref[...], pl_ref[...], cos_ref[...], sin_ref[...], *prm)
        q_ref[...] = q.astype(bf16)
        lane = lax.broadcasted_iota(jnp.int32, (1, 1, HEAD_LANES), 2)
        k_ref[...] = jnp.where((lane == SPARE_Q) | (lane == SPARE_Q + 1), 1.0, k).astype(bf16)
        v_ref[...] = jnp.concatenate([v, jnp.ones_like(v)], axis=-1).astype(bf16)

    tok = lambda w: pl.BlockSpec((ts, w), lambda i: (i, 0))
    head = pl.BlockSpec((N_HEADS, ts, HEAD_LANES), lambda i: (0, i, 0))
    return pl.pallas_call(
        body, name="qkv_fwd", grid=(s // ts,),
        in_specs=[tok(384), tok(128), tok(128), tok(128)] + _qkv_param_specs(),
        out_specs=[head] * 3, out_shape=[jax.ShapeDtypeStruct((N_HEADS, s, HEAD_LANES), bf16)] * 3,
    )(pa, plast, cos_t, sin_t, *params)


def _scores(q, k):
    return lax.dot_general(q, k, (((1,), (1,)), ((), ())), preferred_element_type=f32)


def _tril(rows, cols, row_offset):
    row = row_offset + lax.broadcasted_iota(jnp.int32, (rows, cols), 0)
    col = lax.broadcasted_iota(jnp.int32, (rows, cols), 1)
    return row >= col


def _call_with_job(body, name, grid, job, in_specs, out_specs, out_shape, scratch_shapes, operands, relay_at=None):
    if job is None:
        res = pl.pallas_call(body, name=name, grid=grid, in_specs=in_specs, out_specs=out_specs, out_shape=out_shape,
                             scratch_shapes=scratch_shapes)(*operands)
        return res, None

    def at_step(i, n):
        if i == 0:
            want = [0] * len(grid)
        elif i == n - 1:
            want = [g - 1 for g in grid]
        else:
            want = relay_at
        return functools.reduce(jnp.logical_and, [pl.program_id(a) == s for a, s in enumerate(want)])

    carrier = _carry(job, body, len(in_specs), len(out_specs), at_step)
    res = pl.pallas_call(
        carrier, name=name, grid=grid,
        in_specs=list(in_specs) + [ANY] * len(job.operands), out_specs=list(out_specs) + [ANY] * len(job.out_shape),
        out_shape=list(out_shape) + list(job.out_shape), scratch_shapes=list(scratch_shapes) + job.scratch,
    )(*operands, *job.operands)
    return res[:len(out_specs)], res[len(out_specs):]


def attn_fwd(q, k, v, job=None):
    s = q.shape[1]
    t = _token_block(s)
    nb = s // t

    rb = min(ATTN_ROWS_FWD, t)

    hp = ATTN_HEADS_FWD

    def body(q_ref, k_ref, v_ref, o_ref, qx_ref, m_sc, acc_sc):
        qi = pl.program_id(1)
        m_sc[...] = jnp.full(m_sc.shape, NEG, f32)
        acc_sc[...] = jnp.zeros(acc_sc.shape, f32)

        def step(k0, diagonal):
            chains = [(hh, r) for hh in range(hp) for r in range(t // rb)]

            def scores(hh, r):
                nk = (r + 1) * rb if diagonal else t
                sc = _scores(q_ref[hh, pl.ds(r * rb, rb), :], k_ref[hh, pl.ds(k0, nk), :])
                return jnp.where(_tril(rb, nk, r * rb), sc, NEG) if diagonal else sc

            ahead = scores(*chains[0])
            for c, (hh, r) in enumerate(chains):
                sc = ahead
                if c + 1 < len(chains):
                    ahead = scores(*chains[c + 1])
                rows = pl.ds(r * rb, rb)
                keys = pl.ds(k0, (r + 1) * rb if diagonal else t)
                m_prev = m_sc[hh, rows, :1]
                m_new = jnp.maximum(m_prev, jnp.max(sc, axis=-1, keepdims=True))
                p = jnp.exp2(sc - m_new)
                alpha = jnp.exp2(m_prev - m_new)
                acc = alpha * acc_sc[hh, rows, :] + jnp.dot(p.astype(bf16), v_ref[hh, keys, :], preferred_element_type=f32)
                if diagonal:
                    l = acc[:, V_DIM:V_DIM + 1]
                    o_ref[hh, rows, :] = acc[:, :V_DIM] / l
                    lse = m_new + jnp.log2(l)
                    high = lse.astype(bf16)
                    low = (lse - high.astype(f32)).astype(bf16)
                    lane = lax.broadcasted_iota(jnp.int32, (1, HEAD_LANES), 1)
                    qx_ref[hh, rows, :] = jnp.where(lane == SPARE_Q, -high,
                                                    jnp.where(lane == SPARE_Q + 1, -low, q_ref[hh, rows, :]))
                else:
                    acc_sc[hh, rows, :] = acc
                    m_sc[hh, rows, :] = jnp.broadcast_to(m_new, (rb, 128))

        def below(ki, carry):
            step(pl.multiple_of(ki * t, t), False)
            return carry

        lax.fori_loop(0, qi, below, 0)
        step(pl.multiple_of(qi * t, t), True)

    return _call_with_job(
        body, "attn_fwd" if job is None else "attn_fwd_comm", (N_HEADS // hp, nb), job,
        in_specs=[pl.BlockSpec((hp, t, HEAD_LANES), lambda h, qi: (h, qi, 0)),
                  pl.BlockSpec((hp, s, HEAD_LANES), lambda h, qi: (h, 0, 0)),
                  pl.BlockSpec((hp, s, HEAD_LANES), lambda h, qi: (h, 0, 0))],
        out_specs=[pl.BlockSpec((hp, t, V_DIM), lambda h, qi: (h, qi, 0)),
                   pl.BlockSpec((hp, t, HEAD_LANES), lambda h, qi: (h, qi, 0))],
        out_shape=[jax.ShapeDtypeStruct((N_HEADS, s, V_DIM), f32), jax.ShapeDtypeStruct((N_HEADS, s, HEAD_LANES), bf16)],
        scratch_shapes=[pltpu.VMEM((hp, t, 128), f32), pltpu.VMEM((hp, t, HEAD_LANES), f32)],
        operands=(q, k, v), relay_at=(N_HEADS // hp - 1, max(nb - 2, 0)))


def _ssd_param_specs():
    return [_const((4, D_CONV)), _const((1, D_CONV)), _const((1, 128)), _const((1, 128)), _const((1, 128)),
            _const((1, D_SSD))]


def ssd_fwd(px, pz, plast, params):
    s = px.shape[0]
    nc = s // CHUNK

    def body(px_ref, pz_ref, pl_ref, cw_ref, cb_ref, dtb_ref, alog_ref, dskip_ref, snw_ref, yg_ref, st_ref,
             state_sc, halo_sc):
        i = pl.program_id(0)

        @pl.when(i == 0)
        def _():
            state_sc[...] = jnp.zeros(state_sc.shape, f32)
            halo_sc[...] = jnp.zeros(halo_sc.shape, f32)

        x = px_ref[...]
        prev = state_sc[...]
        st_ref[...] = prev
        xext = jnp.concatenate([halo_sc[...], x], axis=0)
        yg, new = _f_ssd(xext, pz_ref[...], pl_ref[...], prev, cw_ref[...], cb_ref[...], dtb_ref[...],
                         alog_ref[...], dskip_ref[...], snw_ref[...])
        yg_ref[...] = yg
        state_sc[...] = new
        halo_sc[...] = x[CHUNK - HALO:]

    tok = lambda w: pl.BlockSpec((CHUNK, w), lambda i: (i, 0))
    return pl.pallas_call(
        body, name="ssd_fwd", grid=(nc,),
        in_specs=[tok(D_CONV), tok(D_SSD), tok(128)] + _ssd_param_specs(),
        out_specs=[tok(D_SSD), pl.BlockSpec((None, N_HEADS // 2, 2 * SSD_HEAD_DIM, SSD_STATE), lambda i: (i, 0, 0, 0))],
        out_shape=[jax.ShapeDtypeStruct((s, D_SSD), f32),
                   jax.ShapeDtypeStruct((nc, N_HEADS // 2, 2 * SSD_HEAD_DIM, SSD_STATE), f32)],
        scratch_shapes=[pltpu.VMEM((N_HEADS // 2, 2 * SSD_HEAD_DIM, SSD_STATE), f32), pltpu.VMEM((HALO, D_CONV), f32)],
    )(px, pz, plast, *params)


def out_fwd(x, o, yg, g1, wo):
    s = x.shape[0]
    ts = _token_block(s)

    def body(x_ref, o_ref, yg_ref, g1_ref, wo_ref, out_ref):
        out_ref[...] = x_ref[...] + _f_out(o_ref[...], yg_ref[...], g1_ref[...], wo_ref[...])

    return pl.pallas_call(
        body, name="out_fwd", grid=(s // ts,),
        in_specs=[pl.BlockSpec((ts, D_MODEL), lambda i: (i, 0)), pl.BlockSpec((N_HEADS, ts, V_DIM), lambda i: (0, i, 0)),
                  pl.BlockSpec((ts, D_SSD), lambda i: (i, 0)), _const((1, D_MODEL)), _const((D_MODEL, D_MODEL))],
        out_specs=pl.BlockSpec((ts, D_MODEL), lambda i: (i, 0)),
        out_shape=jax.ShapeDtypeStruct((s, D_MODEL), f32),
    )(x, o, yg, g1, wo)


def mlp_fwd(x, nw, sh, sc, g2, wgu, wd):
    s = x.shape[0]
    ts = min(MLP_FWD_ROWS, s)
    nj = N_DEV // 2

    def body(x_ref, nw_ref, sh_ref, sc_ref, g2_ref, wg_ref, wu_ref, wd_ref, out_ref, mix_ref, h_ref, gate_ref, up_ref):
        j = pl.program_id(1)

        @pl.when(j == 0)
        def _():
            h_ref[...] = _f_modulate(x_ref[...], nw_ref[...], sh_ref[...], sc_ref[...]).astype(bf16)
            mix_ref[...] = jnp.zeros(mix_ref.shape, f32)

        nr = max(ts // 512, 1)
        half = ts // nr
        wg, wu, wd = wg_ref[...], wu_ref[...], wd_ref[...]
        products = lambda r: (mmw_t(h_ref[pl.ds(r * half, half), :], wg), mmw_t(h_ref[pl.ds(r * half, half), :], wu))
        ahead = products(0)
        for r in range(nr):
            gate, up = ahead
            if r + 1 < nr:
                ahead = products(r + 1)
            rows = pl.ds(r * half, half)
            gate_ref[rows, :] = gate.astype(bf16)
            up_ref[rows, :] = up.astype(bf16)
            mix_ref[rows, :] += mmw(jax.nn.silu(gate) * up, wd)

        @pl.when(j == nj - 1)
        def _():
            out_ref[...] = x_ref[...] + g2_ref[...] * mix_ref[...]

    vec = _const((1, D_MODEL))
    tok = pl.BlockSpec((ts, D_MODEL), lambda i, j: (i, 0))
    wide = pl.BlockSpec((None, ts, FF_SHARD), lambda i, j: (j, i, 0))
    return pl.pallas_call(
        body, name="mlp_fwd", grid=(s // ts, nj),
        in_specs=[tok, vec, vec, vec, vec,
                  pl.BlockSpec((None, FF_SHARD, D_MODEL), lambda i, j: (j, 0, 0)),
                  pl.BlockSpec((None, FF_SHARD, D_MODEL), lambda i, j: (j + nj, 0, 0)),
                  pl.BlockSpec((None, FF_SHARD, D_MODEL), lambda i, j: (j, 0, 0))],
        out_specs=[tok] * 3 + [wide] * 2,
        out_shape=[jax.ShapeDtypeStruct((s, D_MODEL), f32), jax.ShapeDtypeStruct((s, D_MODEL), f32),
                   jax.ShapeDtypeStruct((s, D_MODEL), bf16)] + [jax.ShapeDtypeStruct((nj, s, FF_SHARD), bf16)] * 2,
    )(x, nw, sh, sc, g2, wgu, wgu, wd)


def loss_fwd(y, target):
    s = y.shape[0]
    ts = _token_block(s)

    def body(y_ref, t_ref, dy_ref, loss_ref):
        d = y_ref[...] - t_ref[...]
        dy_ref[...] = d * (1.0 / D_MODEL)
        part = 0.5 * jnp.sum(jnp.sum(d * d, axis=-1, keepdims=True) * (1.0 / D_MODEL), axis=0, keepdims=True)
        _accumulate(pl.program_id(0) == 0, [loss_ref], [jnp.broadcast_to(part, (8, 128))])

    return pl.pallas_call(
        body, name="loss_fwd", grid=(s // ts,),
        in_specs=[pl.BlockSpec((ts, D_MODEL), lambda i: (i, 0))] * 2,
        out_specs=[pl.BlockSpec((ts, D_MODEL), lambda i: (i, 0)), _const((8, 128))],
        out_shape=[jax.ShapeDtypeStruct((s, D_MODEL), f32), jax.ShapeDtypeStruct((8, 128), f32)],
    )(y, target)


def mlp_bwd(h, dy, gate, up, g2, wgu, wd, job=None):
    s = h.shape[0]
    ts = min(MLP_BWD_ROWS, s)
    nj = N_DEV // 2
    ni = s // ts

    rows_per = min(MLP_BWD_CHUNK, ts)

    def body(h_ref, dy_ref, gate_ref, up_ref, g2_ref, wg_ref, wu_ref, wd_ref, dh_ref, dwg_ref, dwu_ref, dwd_ref,
             ag_sc, au_sc, ad_sc, act_sc, dgate_sc, dup_sc, dmix_sc):
        i = pl.program_id(1)
        wg, wu, wd = wg_ref[...], wu_ref[...], wd_ref[...]
        g2 = g2_ref[...]
        for r in range(ts // rows_per):
            rows = pl.ds(r * rows_per, rows_per)
            act, vjp = jax.vjp(lambda g, u: jax.nn.silu(g) * u, gate_ref[rows, :].astype(f32), up_ref[rows, :].astype(f32))
            dmix = (dy_ref[rows, :] * g2).astype(bf16)
            dgate, dup = vjp(_dot(dmix, wd, 1, 1))
            dgate, dup = dgate.astype(bf16), dup.astype(bf16)
            dh_ref[rows, :] = (_dot(dgate, wg, 1, 0) + _dot(dup, wu, 1, 0)).astype(bf16)
            act_sc[rows, :] = act.astype(bf16)
            dgate_sc[rows, :] = dgate
            dup_sc[rows, :] = dup
            dmix_sc[rows, :] = dmix
        h = h_ref[...]
        grads = [_dot(dgate_sc[...], h, 0, 0), _dot(dup_sc[...], h, 0, 0), _dot(act_sc[...], dmix_sc[...], 0, 0)]
        _accumulate_then_cast(i == 0, i == ni - 1, [ag_sc, au_sc, ad_sc], [dwg_ref, dwu_ref, dwd_ref], grads)

    once = pl.Buffered(1)
    wspec = lambda off: pl.BlockSpec((None, FF_SHARD, D_MODEL), lambda j, i: (j + off, 0, 0), pipeline_mode=once)
    dspec = pl.BlockSpec((None, FF_SHARD, D_MODEL), lambda j, i: (j, 0, 0), pipeline_mode=once)
    wide = pl.BlockSpec((None, ts, FF_SHARD), lambda j, i: (j, i, 0))
    return _call_with_job(
        body, "mlp_bwd" if job is None else "mlp_bwd_comm", (nj, ni), job,
        in_specs=[pl.BlockSpec((ts, D_MODEL), lambda j, i: (i, 0)), pl.BlockSpec((ts, D_MODEL), lambda j, i: (i, 0)),
                  wide, wide, _const((1, D_MODEL)), wspec(0), wspec(nj), dspec],
        out_specs=[pl.BlockSpec((None, ts, D_MODEL), lambda j, i: (j, i, 0)), wspec(0), wspec(0), dspec],
        out_shape=[jax.ShapeDtypeStruct((nj, s, D_MODEL), bf16),
                   jax.ShapeDtypeStruct((nj, FF_SHARD, D_MODEL), bf16), jax.ShapeDtypeStruct((nj, FF_SHARD, D_MODEL), bf16),
                   jax.ShapeDtypeStruct((nj, FF_SHARD, D_MODEL), bf16)],
        scratch_shapes=[pltpu.VMEM((FF_SHARD, D_MODEL), f32), pltpu.VMEM((FF_SHARD, D_MODEL), f32),
                        pltpu.VMEM((FF_SHARD, D_MODEL), f32), pltpu.VMEM((ts, FF_SHARD), bf16),
                        pltpu.VMEM((ts, FF_SHARD), bf16), pltpu.VMEM((ts, FF_SHARD), bf16), pltpu.VMEM((ts, D_MODEL), bf16)],
        operands=(h, dy, gate, up, g2, wgu, wgu, wd))


def out_bwd(dy, dhparts, x, nw, sh, sc, mix, o, yg, g1, wo):
    s = dy.shape[0]
    ts = _token_block(s)
    nj = dhparts.shape[0]

    ni = s // ts

    def body(dy_ref, dp_ref, x_ref, nw_ref, sh_ref, sc_ref, mix_ref, o_ref, yg_ref, g1_ref, wo_ref,
             dx_ref, dnw_ref, dsh_ref, dsc_ref, do_ref, dyg_ref, dg1_ref, dg2_ref, dwo_ref, acc_sc):
        i = pl.program_id(0)
        g = dy_ref[...]
        _accumulate(i == 0, [dg2_ref], [jnp.sum(g * mix_ref[...], axis=0, keepdims=True)])
        dh = dp_ref[0].astype(f32)
        for j in range(1, nj):
            dh = dh + dp_ref[j].astype(f32)
        _, vjp_mod = jax.vjp(_f_modulate, x_ref[...], nw_ref[...], sh_ref[...], sc_ref[...])
        dx_mod, dnw, dsh, dsc = vjp_mod(dh)
        _accumulate(i == 0, [dnw_ref, dsh_ref, dsc_ref], [dnw, dsh, dsc])
        g = g + dx_mod
        dx_ref[...] = g
        o = o_ref[...]
        wo = wo_ref[...]
        _, vjp = jax.vjp(lambda o_, yg_, g1_, slot: _f_out(o_, yg_, g1_, wo, slot), o, yg_ref[...], g1_ref[...],
                         jnp.zeros(wo.shape, f32))
        do, dyg, dg1, dwo = vjp(g)
        delta = jnp.sum(do * o, axis=-1, keepdims=True)
        high = delta.astype(bf16)
        low = (delta - high.astype(f32)).astype(bf16)
        lane = lax.broadcasted_iota(jnp.int32, (1, 1, HEAD_LANES), 2)
        wide = jnp.concatenate([do.astype(bf16), jnp.zeros(do.shape, bf16)], axis=-1)
        do_ref[...] = jnp.where(lane == SPARE_V, -high, jnp.where(lane == SPARE_V + 1, -low, wide))
        dyg_ref[...] = dyg
        _accumulate(i == 0, [dg1_ref], [dg1])
        _accumulate_then_cast(i == 0, i == ni - 1, [acc_sc], [dwo_ref], [dwo])

    head = pl.BlockSpec((N_HEADS, ts, V_DIM), lambda i: (0, i, 0))
    tok = pl.BlockSpec((ts, D_MODEL), lambda i: (i, 0))
    vec = _const((1, D_MODEL))
    vshape = jax.ShapeDtypeStruct((1, D_MODEL), f32)
    return pl.pallas_call(
        body, name="out_bwd", grid=(ni,), scratch_shapes=[pltpu.VMEM((D_MODEL, D_MODEL), f32)],
        in_specs=[tok, pl.BlockSpec((nj, ts, D_MODEL), lambda i: (0, i, 0)), tok, vec, vec, vec, tok,
                  head, pl.BlockSpec((ts, D_SSD), lambda i: (i, 0)), vec, _const((D_MODEL, D_MODEL))],
        out_specs=[tok, vec, vec, vec, pl.BlockSpec((N_HEADS, ts, HEAD_LANES), lambda i: (0, i, 0)),
                   pl.BlockSpec((ts, D_SSD), lambda i: (i, 0)), vec, vec, _const((D_MODEL, D_MODEL))],
        out_shape=[jax.ShapeDtypeStruct((s, D_MODEL), f32), vshape, vshape, vshape,
                   jax.ShapeDtypeStruct((N_HEADS, s, HEAD_LANES), bf16), jax.ShapeDtypeStruct((s, D_SSD), f32),
                   vshape, vshape, jax.ShapeDtypeStruct((D_MODEL, D_MODEL), bf16)],
    )(dy, dhparts, x, nw, sh, sc, mix, o, yg, g1, wo)


def attn_bwd(qx, k, v, do, job=None):
    s = qx.shape[1]
    t = _token_block(s)
    nb = s // t

    hp = ATTN_HEADS_BWD

    def body(q_ref, k_ref, v_ref, do_ref, dq_ref, dk_ref, dv_ref, dv_sc):
        ki = pl.program_id(1)

        @pl.when(ki == 0)
        def _():
            dq_ref[...] = jnp.zeros(dq_ref.shape, f32)

        dk_ref[...] = jnp.zeros(dk_ref.shape, f32)
        dv_sc[...] = jnp.zeros(dv_sc.shape, f32)

        def step(q0, diagonal):
            rows = pl.ds(q0, t)

            def products(hh):
                sc = _scores(q_ref[hh, rows, :], k_ref[hh])
                dps = _scores(do_ref[hh, rows, :], v_ref[hh])
                return (jnp.where(_tril(t, t, 0), sc, NEG) if diagonal else sc), dps

            ahead = products(0)
            for hh in range(hp):
                sc, dps = ahead
                if hh + 1 < hp:
                    ahead = products(hh + 1)
                p = jnp.exp2(sc)
                ds = (p * dps).astype(bf16)
                dv_sc[hh] += lax.dot_general(p.astype(bf16), do_ref[hh, rows, :], (((0,), (0,)), ((), ())),
                                             preferred_element_type=f32)
                dk_ref[hh] += lax.dot_general(ds, q_ref[hh, rows, :], (((0,), (0,)), ((), ())), preferred_element_type=f32)
                dq_ref[hh, rows, :] += jnp.dot(ds, k_ref[hh], preferred_element_type=f32)

        step(pl.multiple_of(ki * t, t), True)

        def above(qi, carry):
            step(pl.multiple_of(qi * t, t), False)
            return carry

        lax.fori_loop(ki + 1, nb, above, 0)
        real = lax.broadcasted_iota(jnp.int32, (1, 1, HEAD_LANES), 2) < SPARE_Q
        dk_ref[...] = jnp.where(real, dk_ref[...] * LN2, 0.0)
        dv_ref[...] = dv_sc[:, :, :V_DIM]

        @pl.when(ki == nb - 1)
        def _():
            dq_ref[...] = jnp.where(real, dq_ref[...] * LN2, 0.0)

    qspec = pl.BlockSpec((hp, s, HEAD_LANES), lambda h, ki: (h, 0, 0))
    kspec = lambda w: pl.BlockSpec((hp, t, w), lambda h, ki: (h, ki, 0))
    return _call_with_job(
        body, "attn_bwd" if job is None else "attn_bwd_comm", (N_HEADS // hp, nb), job,
        in_specs=[qspec, kspec(HEAD_LANES), kspec(HEAD_LANES), qspec],
        out_specs=[qspec, kspec(HEAD_LANES), kspec(V_DIM)],
        out_shape=[jax.ShapeDtypeStruct((N_HEADS, s, HEAD_LANES), f32), jax.ShapeDtypeStruct((N_HEADS, s, HEAD_LANES), f32),
                   jax.ShapeDtypeStruct((N_HEADS, s, V_DIM), f32)],
        scratch_shapes=[pltpu.VMEM((hp, t, HEAD_LANES), f32)], operands=(qx, k, v, do))


def ssd_bwd(px, pz, plast, states, dyg, params):
    s = px.shape[0]
    nc = s // CHUNK
    per = CHUNK // HALO

    def body(px_ref, halo_ref, pz_ref, pl_ref, st_ref, dyg_ref, cw_ref, cb_ref, dtb_ref, alog_ref, dskip_ref, snw_ref,
             dpx_ref, dpz_ref, dpl_ref, dcw_ref, dcb_ref, ddtb_ref, dalog_ref, ddskip_ref, dsnw_ref, dstate_sc, dhalo_sc):
        t = pl.program_id(0)
        chunk = nc - 1 - t

        @pl.when(t == 0)
        def _():
            dstate_sc[...] = jnp.zeros(dstate_sc.shape, f32)
            dhalo_sc[...] = jnp.zeros(dhalo_sc.shape, f32)

        halo = jnp.where(chunk > 0, halo_ref[...], 0.0)
        xext = jnp.concatenate([halo, px_ref[...]], axis=0)
        _, vjp = jax.vjp(_f_ssd, xext, pz_ref[...], pl_ref[...], st_ref[...], cw_ref[...], cb_ref[...], dtb_ref[...],
                         alog_ref[...], dskip_ref[...], snw_ref[...])
        dxext, dz, dpl, dprev, dcw, dcb, ddtb, dalog, ddskip, dsnw = vjp((dyg_ref[...], dstate_sc[...]))
        dpx_ref[...] = dxext[HALO:]
        dpx_ref[CHUNK - HALO:, :] += dhalo_sc[...]
        dhalo_sc[...] = dxext[:HALO]
        dstate_sc[...] = dprev
        dpz_ref[...] = dz
        dpl_ref[...] = dpl
        _accumulate(t == 0, [dcw_ref, dcb_ref, ddtb_ref, dalog_ref, ddskip_ref, dsnw_ref],
                    [dcw, dcb, ddtb, dalog, ddskip, dsnw])

    rev = lambda w: pl.BlockSpec((CHUNK, w), lambda t: (nc - 1 - t, 0))
    pshapes = [jax.ShapeDtypeStruct((4, D_CONV), f32), jax.ShapeDtypeStruct((1, D_CONV), f32),
               jax.ShapeDtypeStruct((1, 128), f32), jax.ShapeDtypeStruct((1, 128), f32),
               jax.ShapeDtypeStruct((1, 128), f32), jax.ShapeDtypeStruct((1, D_SSD), f32)]
    return pl.pallas_call(
        body, name="ssd_bwd", grid=(nc,),
        in_specs=[rev(D_CONV),
                  pl.BlockSpec((HALO, D_CONV), lambda t: (jnp.maximum((nc - 1 - t) * per - 1, 0), 0)),
                  rev(D_SSD), rev(128),
                  pl.BlockSpec((None, N_HEADS // 2, 2 * SSD_HEAD_DIM, SSD_STATE), lambda t: (nc - 1 - t, 0, 0, 0)),
                  rev(D_SSD)] + _ssd_param_specs(),
        out_specs=[rev(D_CONV), rev(D_SSD), rev(128)] + _ssd_param_specs(),
        out_shape=[jax.ShapeDtypeStruct((s, D_CONV), f32), jax.ShapeDtypeStruct((s, D_SSD), f32),
                   jax.ShapeDtypeStruct((s, 128), f32)] + pshapes,
        scratch_shapes=[pltpu.VMEM((N_HEADS // 2, 2 * SSD_HEAD_DIM, SSD_STATE), f32), pltpu.VMEM((HALO, D_CONV), f32)],
    )(px, px, pz, plast, states, dyg, *params)


def qkv_bwd(pa, plast, cos_t, sin_t, params, dq, dk, dv):
    s = pa.shape[0]
    ts = _token_block(s)

    def body(pa_ref, pl_ref, cos_ref, sin_ref, *rest):
        qaw, kvaw, wq, wk, wv, qnw, knw, kpw = [r[...] for r in rest[:8]]
        dq_ref, dk_ref, dv_ref = rest[8:11]
        dpa_ref, dpl_ref = rest[11:13]
        dprm_refs = list(rest[13:])
        cos_t, sin_t = cos_ref[...], sin_ref[...]

        def stage(pa_, pl_, qaw_, kvaw_, sq, sk, sv, qnw_, knw_, kpw_):
            return _f_qkv(pa_, pl_, cos_t, sin_t, qaw_, kvaw_, wq, wk, wv, qnw_, knw_, kpw_, (sq, sk, sv))

        _, vjp = jax.vjp(stage, pa_ref[...], pl_ref[...], qaw, kvaw, jnp.zeros(wq.shape, f32), jnp.zeros(wk.shape, f32),
                         jnp.zeros(wv.shape, f32), qnw, knw, kpw)
        grads = vjp((dq_ref[...], dk_ref[...], dv_ref[...]))
        dpa_ref[...] = grads[0]
        dpl_ref[...] = grads[1]
        _accumulate(pl.program_id(0) == 0, dprm_refs, list(grads[2:]))

    tok = lambda w: pl.BlockSpec((ts, w), lambda i: (i, 0))
    head = lambda w: pl.BlockSpec((N_HEADS, ts, w), lambda i: (0, i, 0))
    pshapes = [jax.ShapeDtypeStruct((1, Q_RANK), f32), jax.ShapeDtypeStruct((1, KV_RANK), f32),
               jax.ShapeDtypeStruct((N_HEADS, Q_RANK, HEAD_LANES), f32), jax.ShapeDtypeStruct((N_HEADS, KV_RANK, HEAD_LANES), f32),
               jax.ShapeDtypeStruct((N_HEADS, KV_RANK, V_DIM), f32), jax.ShapeDtypeStruct((1, HEAD_LANES), f32),
               jax.ShapeDtypeStruct((1, HEAD_LANES), f32), jax.ShapeDtypeStruct((1, HEAD_LANES), f32)]
    return pl.pallas_call(
        body, name="qkv_bwd", grid=(s // ts,),
        in_specs=[tok(384), tok(128), tok(128), tok(128)] + _qkv_param_specs()
                 + [head(HEAD_LANES), head(HEAD_LANES), head(V_DIM)],
        out_specs=[tok(384), tok(128)] + _qkv_param_specs(),
        out_shape=[jax.ShapeDtypeStruct((s, 384), f32), jax.ShapeDtypeStruct((s, 128), f32)] + pshapes,
    )(pa, plast, cos_t, sin_t, *params, dq, dk, dv)


def proj_bwd(x, nw, sh, sc, w, dpa, dpz, dpx, dpl_k, dpl_dt, dres):
    s = x.shape[0]
    ts = _token_block(s)

    ni = s // ts

    def body(x_ref, nw_ref, sh_ref, sc_ref, w_ref, dpa_ref, dpz_ref, dpx_ref, dplk_ref, dpld_ref, dres_ref,
             dx_ref, dnw_ref, dsh_ref, dsc_ref, dw_ref, acc_sc):
        i = pl.program_id(0)
        g = jnp.concatenate([dpa_ref[...], dpz_ref[...], dpx_ref[...], dplk_ref[...] + dpld_ref[...]], axis=1)
        w = w_ref[...]
        _, vjp = jax.vjp(lambda x_, nw_, sh_, sc_, slot: _f_proj(x_, nw_, sh_, sc_, w, slot), x_ref[...], nw_ref[...],
                         sh_ref[...], sc_ref[...], jnp.zeros(w.shape, f32))
        dx, dnw, dsh, dsc, dw = vjp(g)
        dx_ref[...] = dx + dres_ref[...]
        _accumulate(i == 0, [dnw_ref, dsh_ref, dsc_ref], [dnw, dsh, dsc])
        _accumulate_then_cast(i == 0, i == ni - 1, [acc_sc], [dw_ref], [dw])

    vec = _const((1, D_MODEL))
    vshape = jax.ShapeDtypeStruct((1, D_MODEL), f32)
    tok = lambda w_: pl.BlockSpec((ts, w_), lambda i: (i, 0))
    return pl.pallas_call(
        body, name="proj_bwd", grid=(ni,), scratch_shapes=[pltpu.VMEM((D_PROJ, D_MODEL), f32)],
        in_specs=[tok(D_MODEL), vec, vec, vec, _const((D_PROJ, D_MODEL)), tok(384), tok(512), tok(1024), tok(128), tok(128),
                  tok(D_MODEL)],
        out_specs=[tok(D_MODEL), vec, vec, vec, _const((D_PROJ, D_MODEL))],
        out_shape=[jax.ShapeDtypeStruct((s, D_MODEL), f32), vshape, vshape, vshape,
                   jax.ShapeDtypeStruct((D_PROJ, D_MODEL), bf16)],
    )(x, nw, sh, sc, w, dpa, dpz, dpx, dpl_k, dpl_dt, dres)


def ada_fwd(c_all, w_ada, b_cols):
    def body(c_ref, w_ref, b_ref, out_ref):
        act = jax.nn.silu(c_ref[...])
        for l in range(2):
            out_ref[l] = jnp.dot(act, w_ref[l], precision=lax.Precision.HIGHEST, preferred_element_type=f32) + b_ref[l]

    return pl.pallas_call(body, name="ada_fwd", out_shape=jax.ShapeDtypeStruct((2, N_DEV, 768), f32))(c_all, w_ada, b_cols)


def ada_bwd(c_all, dmod_cols):
    def body(c_ref, d_ref, out_ref):
        out_ref[0] = lax.dot_general(jax.nn.silu(c_ref[...]), d_ref[0], (((0,), (0,)), ((), ())),
                                     precision=lax.Precision.HIGHEST, preferred_element_type=f32)

    return pl.pallas_call(
        body, name="ada_bwd", grid=(2,),
        in_specs=[_const((N_DEV, D_MODEL)), pl.BlockSpec((1, N_DEV, 768), lambda l: (l, 0, 0))],
        out_specs=pl.BlockSpec((1, D_MODEL, 768), lambda l: (l, 0, 0)),
        out_shape=jax.ShapeDtypeStruct((2, D_MODEL, 768), f32),
    )(c_all, dmod_cols)


def _adamw(w, g, m, v):
    m = ADAM_B1 * m + (1.0 - ADAM_B1) * g
    v = ADAM_B2 * v + (1.0 - ADAM_B2) * (g * g)
    m_hat = m / (1.0 - ADAM_B1 ** ADAM_STEP)
    v_hat = v / (1.0 - ADAM_B2 ** ADAM_STEP)
    delta = -ADAM_LR * (m_hat / (jnp.sqrt(v_hat) + ADAM_EPS) + ADAM_WD * w)
    return delta, m, v


def adamw(parts, w, m, v, layer, prev, name):
    n, r, c = parts.shape
    nl = w.shape[0]
    per_elem = 2 * (n * parts.dtype.itemsize + 7 * 4)
    lanes = -(-c // 128) * 128
    tr, tc = r, c
    if per_elem * r * lanes > ADAMW_BLOCK_BYTES:
        fits = [t for t in (256, 128, 64, 32, 16, 8) if r % t == 0]
        if fits:
            tr = fits[0]
        else:
            tc = next(t for t in (512, 256, 128) if c % t == 0)

    def body(p_ref, w_ref, m_ref, v_ref, *rest):
        g_ref, d_ref, nm_ref, nv_ref = rest[-4:]
        g = p_ref[0].astype(f32)
        for k in range(1, n):
            g = g + p_ref[k].astype(f32)
        delta, nm, nv = _adamw(w_ref[...], g, m_ref[...], v_ref[...])
        g_ref[...] = g
        d_ref[...] = delta
        nm_ref[...] = nm
        nv_ref[...] = nv

    blk = pl.BlockSpec((None, tr, tc), lambda i, j: (layer, i, j))
    shp = jax.ShapeDtypeStruct((nl, r, c), f32)
    kept = [] if prev is None else list(prev)
    return pl.pallas_call(
        body, name=name, grid=(r // tr, c // tc),
        in_specs=[pl.BlockSpec((n, tr, tc), lambda i, j: (0, i, j)), blk, blk, blk] + [ANY] * len(kept),
        out_specs=[blk] * 4, out_shape=[shp] * 4,
        input_output_aliases={4 + j: j for j in range(len(kept))},
    )(parts, w, m, v, *kept)


def _my_index():
    return 4 * lax.axis_index("x") + 2 * lax.axis_index("y") + lax.axis_index("c")


def _coords(idx):
    return (idx // 4, (idx // 2) % 2, idx % 2)


class CommJob:
    def __init__(self, operands, out_shape, phases, scratch):
        self.operands, self.out_shape, self.phases, self.scratch = operands, out_shape, phases, scratch


def _wait(out, n_blocks, send_sem, recv_sem, send=True, recv=True):
    span = out.at[pl.ds(0, n_blocks)]
    desc = pltpu.make_async_remote_copy(src_ref=span, dst_ref=span, send_sem=send_sem, recv_sem=recv_sem,
                                        device_id=_coords(_my_index()), device_id_type=MESH)
    if recv:
        desc.wait_recv()
    if send:
        desc.wait_send()


def gather_job(shards):
    n = len(shards)

    def places():
        x, y, c = lax.axis_index("x"), lax.axis_index("y"), lax.axis_index("c")
        return (x, y, c), (x, y, 1 - c), [(1 - x, y), (x, 1 - y), (1 - x, 1 - y)]

    def index(p):
        return 4 * p[0] + 2 * p[1] + p[2]

    def start(ins, outs, sems):
        far_send, far_recv, near_send, near_recv, local = sems
        me, sibling, chips = places()
        for k in range(n):
            pltpu.make_async_copy(ins[k], outs[k].at[index(me)], local.at[k]).start()
            for chip in chips:
                pltpu.make_async_remote_copy(src_ref=ins[k], dst_ref=outs[k].at[index(me)], send_sem=far_send.at[k],
                                             recv_sem=far_recv.at[k], device_id=(*chip, me[2]), device_id_type=MESH).start()
            pltpu.make_async_remote_copy(src_ref=ins[k], dst_ref=outs[k].at[index(me)], send_sem=near_send.at[k],
                                         recv_sem=near_recv.at[k], device_id=sibling, device_id_type=MESH).start()

    def relay(ins, outs, sems):
        far_send, far_recv, near_send, near_recv, local = sems
        me, sibling, chips = places()
        for k in range(n):
            _wait(outs[k], 3, far_send.at[k], far_recv.at[k], send=False)
            for chip in chips:
                block = outs[k].at[index((*chip, me[2]))]
                pltpu.make_async_remote_copy(src_ref=block, dst_ref=block, send_sem=near_send.at[k],
                                             recv_sem=near_recv.at[k], device_id=sibling, device_id_type=MESH).start()

    def finish(ins, outs, sems):
        far_send, far_recv, near_send, near_recv, local = sems
        for k in range(n):
            _wait(outs[k], 4, near_send.at[k], near_recv.at[k])
            _wait(outs[k], 3, far_send.at[k], far_recv.at[k], recv=False)
            pltpu.make_async_copy(ins[k], outs[k].at[0], local.at[k]).wait()

    shapes = [jax.ShapeDtypeStruct((N_DEV,) + tuple(a.shape), a.dtype) for a in shards]
    return CommJob(list(shards), shapes, [start, relay, finish], [pltpu.SemaphoreType.DMA((n,))] * 5)


def scatter_job(tensors):
    n = len(tensors)
    flat, where = [], {}
    for k, pieces in enumerate(tensors):
        d = 0
        for piece in pieces:
            for b in range(piece.shape[0]):
                where[k, d] = (len(flat), b)
                d += 1
            flat.append(piece)
        assert d == N_DEV

    def start(ins, outs, sems):
        send_sems, recv_sems, local_sems = sems
        me = _my_index()

        def block(k, d):
            i, b = where[k, d]
            return ins[i].at[b]

        for d in range(N_DEV):
            @pl.when(d != me)
            def _():
                for k in range(n):
                    pltpu.make_async_remote_copy(src_ref=block(k, d), dst_ref=outs[k].at[me], send_sem=send_sems.at[k],
                                                 recv_sem=recv_sems.at[k], device_id=(d // 4, (d // 2) % 2, d % 2),
                                                 device_id_type=MESH).start()

            @pl.when(d == me)
            def _():
                for k in range(n):
                    pltpu.make_async_copy(block(k, d), outs[k].at[d], local_sems.at[k]).start()

    def finish(ins, outs, sems):
        send_sems, recv_sems, local_sems = sems
        for k in range(n):
            _wait(outs[k], N_DEV - 1, send_sems.at[k], recv_sems.at[k])
            i, b = where[k, 0]
            pltpu.make_async_copy(ins[i].at[b], outs[k].at[0], local_sems.at[k]).wait()

    shapes = [jax.ShapeDtypeStruct((N_DEV,) + tuple(p[0].shape[1:]), p[0].dtype) for p in tensors]
    return CommJob(flat, shapes, [start, finish], [pltpu.SemaphoreType.DMA((n,))] * 3)


def comm_call(job, name):
    ni, no = len(job.operands), len(job.out_shape)

    def body(*refs):
        ins, outs, sems = refs[:ni], refs[ni:ni + no], refs[ni + no:]
        for phase in job.phases:
            phase(ins, outs, sems)

    return pl.pallas_call(body, name=name, in_specs=[ANY] * ni, out_specs=[ANY] * no, out_shape=job.out_shape,
                          scratch_shapes=job.scratch)(*job.operands)


def _carry(job, body, n_in, n_out, at_step):
    ji, jo, js = len(job.operands), len(job.out_shape), len(job.scratch)

    def carrier(*refs):
        a, b = n_in, n_in + ji
        c, d = b + n_out, b + n_out + jo
        e = len(refs) - js
        job_refs = (refs[a:b], refs[c:d], refs[e:])
        n = len(job.phases)

        @pl.when(at_step(0, n))
        def _():
            job.phases[0](*job_refs)

        body(*refs[:a], *refs[b:c], *refs[d:e])

        for i in range(1, n):
            @pl.when(at_step(i, n))
            def _():
                job.phases[i](*job_refs)

    return carrier


def _pad_lanes(v, lo, total=128):
    return jnp.pad(v, (lo, total - lo - v.shape[0]))[None, :]


MIXER_WEIGHTS = ("w_in", "w_q_up", "w_kv_up", "conv_w")
LATE_WEIGHTS = ("w_out", "w_gate_up", "w_down")


def mixer_operands(g, sw):
    w_in = g["w_in"].reshape(D_IN, D_MODEL)
    zero = lambda rows: jnp.zeros((rows, D_MODEL), w_in.dtype)
    w_proj = jnp.concatenate(
        [w_in[:384], w_in[416:928], w_in[928:1952], w_in[1952:1960], zero(56), w_in[384:416], zero(32)], axis=0)
    wq = jnp.pad(g["w_q_up"], ((0, 0), (0, 0), (0, HEAD_LANES - NOPE - ROPE)))
    wk = jnp.pad(g["w_kv_up"][:, :, :NOPE], ((0, 0), (0, 0), (0, HEAD_LANES - NOPE)))
    wv = g["w_kv_up"][:, :, NOPE:]
    qkv = (sw["q_a_norm_w"][None, :], sw["kv_a_norm_w"][None, :], wq, wk, wv,
           _pad_lanes(jnp.concatenate([sw["q_nope_norm_w"], sw["q_pe_norm_w"]]), 0),
           _pad_lanes(sw["k_nope_norm_w"], 0), _pad_lanes(sw["k_pe_norm_w"], NOPE))
    conv_w = g["conv_w"].astype(f32).transpose(1, 0, 2).reshape(4, D_CONV)
    ssd = (conv_w, sw["conv_b"][None, :], _pad_lanes(sw["dt_bias"], 0), _pad_lanes(sw["a_log"], 0),
           _pad_lanes(sw["d_skip"], 0), sw["ssd_norm_w"][None, :])
    return dict(w_proj=w_proj, qkv=qkv, ssd=ssd, n1=sw["norm1_w"][None, :])


def late_operands(g, sw):
    return dict(wo=g["w_out"].reshape(D_MODEL, D_MODEL), wgu=g["w_gate_up"],
                wd=g["w_down"].reshape(N_DEV // 2, FF_SHARD, D_MODEL), n2=sw["norm2_w"][None, :])


def layer_fwd(x, mod, kw, cos_t, sin_t, job=None, late=None):
    sh1, sc1, g1, sh2, sc2, g2 = [mod[i:i + 1] for i in range(6)]
    pa, pz, px, plast = proj_fwd(x, kw["n1"], sh1, sc1, kw["w_proj"])
    q, k, v = qkv_fwd(pa, plast, cos_t, sin_t, kw["qkv"])
    (o, qx), carried = attn_fwd(q, k, v, job)
    if late is not None:
        kw = {**kw, **late(carried)}
    yg, states = ssd_fwd(px, pz, plast, kw["ssd"])
    x_mid = out_fwd(x, o, yg, g1, kw["wo"])
    x_out, mix, h_mid, gate, up = mlp_fwd(x_mid, kw["n2"], sh2, sc2, g2, kw["wgu"], kw["wd"])
    saved = dict(x=x, pa=pa, pz=pz, px=px, plast=plast, qx=qx, k=k, v=v, o=o, yg=yg, states=states, x_mid=x_mid,
                 mix=mix, h_mid=h_mid, gate=gate, up=up)
    return x_out, saved, kw, carried


def layer_bwd_head(dy, mod, kw, sv, job=None):
    _, _, g1, sh2, sc2, g2 = [mod[i:i + 1] for i in range(6)]
    (dhparts, dwg, dwu, dwd), carried = mlp_bwd(sv["h_mid"], dy, sv["gate"], sv["up"], g2, kw["wgu"], kw["wd"], job)
    dmid, dn2, dsh2, dsc2, do, dyg, dg1, dg2, dwo = out_bwd(
        dy, dhparts, sv["x_mid"], kw["n2"], sh2, sc2, sv["mix"], sv["o"], sv["yg"], g1, kw["wo"])
    early = dict(w_out=[dwo.reshape(N_DEV, D_MODEL // N_DEV, D_MODEL)], w_gate_up=[dwg, dwu],
                 w_down=[dwd.reshape(N_DEV, D_FF // N_DEV, D_MODEL)])
    head = dict(dmid=dmid, do=do, dyg=dyg, dn2=dn2, dsh2=dsh2, dsc2=dsc2, dg2=dg2, dg1=dg1)
    return head, early, carried


def layer_bwd_tail(hd, mod, kw, cos_t, sin_t, sv, job=None):
    sh1, sc1 = mod[0:1], mod[1:2]
    (dq, dk, dv), carried = attn_bwd(sv["qx"], sv["k"], sv["v"], hd["do"], job)
    dpx, dpz, dpl_dt, dcw, dcb, ddtb, dalog, ddskip, dsnw = ssd_bwd(sv["px"], sv["pz"], sv["plast"], sv["states"],
                                                                   hd["dyg"], kw["ssd"])
    dpa, dpl_k, dqaw, dkvaw, dwq, dwk, dwv, dqnw, dknw, dkpw = qkv_bwd(sv["pa"], sv["plast"], cos_t, sin_t, kw["qkv"],
                                                                       dq, dk, dv)
    dx, dn1, dsh1, dsc1, dwp = proj_bwd(sv["x"], kw["n1"], sh1, sc1, kw["w_proj"], dpa, dpz, dpx, dpl_k, dpl_dt, hd["dmid"])
    dmod = jnp.concatenate([dsh1, dsc1, hd["dg1"], hd["dsh2"], hd["dsc2"], hd["dg2"]], axis=0)
    dw_in = jnp.concatenate([dwp[:384], dwp[1984:2016], dwp[384:1920], dwp[1920:1928]], axis=0)
    grads = dict(
        norm1_w=dn1[0], norm2_w=hd["dn2"][0], q_a_norm_w=dqaw[0], kv_a_norm_w=dkvaw[0],
        q_nope_norm_w=dqnw[0, :NOPE], q_pe_norm_w=dqnw[0, NOPE:NOPE + ROPE], k_nope_norm_w=dknw[0, :NOPE],
        k_pe_norm_w=dkpw[0, NOPE:NOPE + ROPE], conv_b=dcb[0], dt_bias=ddtb[0, :N_HEADS], a_log=dalog[0, :N_HEADS],
        d_skip=ddskip[0, :N_HEADS], ssd_norm_w=dsnw[0],
        w_in=[dw_in.reshape(N_DEV, D_IN // N_DEV, D_MODEL)],
        w_q_up=[dwq[:, :, :NOPE + ROPE].astype(bf16)],
        w_kv_up=[jnp.concatenate([dwk[:, :, :NOPE], dwv], axis=2).astype(bf16)],
        conv_w=[dcw.reshape(4, N_DEV, D_CONV // N_DEV).transpose(1, 0, 2).astype(bf16)],
    )
    return dx, dmod, grads, carried


def _pack_small(get, last=None):
    flat = jnp.concatenate([get(name).reshape(-1) for name, _ in SMALL])
    flat = jnp.pad(flat, (0, SMALL_ROWS * 128 - flat.shape[0]))
    if last is not None:
        flat = flat.at[-1].set(last)
    return flat.reshape(SMALL_ROWS, 128)


def _unpack_small(packed):
    flat = packed.reshape(-1)
    out, off = {}, 0
    for name, size in SMALL:
        out[name] = flat[off:off + 2 * size].reshape(2, size)
        off += 2 * size
    return out


def kernel(x, c, positions, norm1_w, norm2_w, w_ada, b_ada, w_in, q_a_norm_w, w_q_up, kv_a_norm_w, w_kv_up, q_nope_norm_w, q_pe_norm_w, k_nope_norm_w, k_pe_norm_w, conv_w, conv_b, dt_bias, a_log, d_skip, ssd_norm_w, w_out, w_gate_up, w_down, loss_target, m_norm1_w, m_norm2_w, m_w_ada, m_b_ada, m_w_in, m_q_a_norm_w, m_w_q_up, m_kv_a_norm_w, m_w_kv_up, m_q_nope_norm_w, m_q_pe_norm_w, m_k_nope_norm_w, m_k_pe_norm_w, m_conv_w, m_conv_b, m_dt_bias, m_a_log, m_d_skip, m_ssd_norm_w, m_w_out, m_w_gate_up, m_w_down, v_norm1_w, v_norm2_w, v_w_ada, v_b_ada, v_w_in, v_q_a_norm_w, v_w_q_up, v_kv_a_norm_w, v_w_kv_up, v_q_nope_norm_w, v_q_pe_norm_w, v_k_nope_norm_w, v_k_pe_norm_w, v_conv_w, v_conv_b, v_dt_bias, v_a_log, v_d_skip, v_ssd_norm_w, v_w_out, v_w_gate_up, v_w_down):
    w = dict(norm1_w=norm1_w, norm2_w=norm2_w, w_ada=w_ada, b_ada=b_ada, w_in=w_in, q_a_norm_w=q_a_norm_w, w_q_up=w_q_up,
             kv_a_norm_w=kv_a_norm_w, w_kv_up=w_kv_up, q_nope_norm_w=q_nope_norm_w, q_pe_norm_w=q_pe_norm_w,
             k_nope_norm_w=k_nope_norm_w, k_pe_norm_w=k_pe_norm_w, conv_w=conv_w, conv_b=conv_b, dt_bias=dt_bias,
             a_log=a_log, d_skip=d_skip, ssd_norm_w=ssd_norm_w, w_out=w_out, w_gate_up=w_gate_up, w_down=w_down)
    m = dict(norm1_w=m_norm1_w, norm2_w=m_norm2_w, w_ada=m_w_ada, b_ada=m_b_ada, w_in=m_w_in, q_a_norm_w=m_q_a_norm_w,
             w_q_up=m_w_q_up, kv_a_norm_w=m_kv_a_norm_w, w_kv_up=m_w_kv_up, q_nope_norm_w=m_q_nope_norm_w,
             q_pe_norm_w=m_q_pe_norm_w, k_nope_norm_w=m_k_nope_norm_w, k_pe_norm_w=m_k_pe_norm_w, conv_w=m_conv_w,
             conv_b=m_conv_b, dt_bias=m_dt_bias, a_log=m_a_log, d_skip=m_d_skip, ssd_norm_w=m_ssd_norm_w, w_out=m_w_out,
             w_gate_up=m_w_gate_up, w_down=m_w_down)
    v = dict(norm1_w=v_norm1_w, norm2_w=v_norm2_w, w_ada=v_w_ada, b_ada=v_b_ada, w_in=v_w_in, q_a_norm_w=v_q_a_norm_w,
             w_q_up=v_w_q_up, kv_a_norm_w=v_kv_a_norm_w, w_kv_up=v_w_kv_up, q_nope_norm_w=v_q_nope_norm_w,
             q_pe_norm_w=v_q_pe_norm_w, k_nope_norm_w=v_k_nope_norm_w, k_pe_norm_w=v_k_pe_norm_w, conv_w=v_conv_w,
             conv_b=v_conv_b, dt_bias=v_dt_bias, a_log=v_a_log, d_skip=v_d_skip, ssd_norm_w=v_ssd_norm_w, w_out=v_w_out,
             w_gate_up=v_w_gate_up, w_down=v_w_down)
    me = _my_index()
    seq = x.shape[1]

    def shard(name, l):
        if name == "conv_w":
            return w[name][l]
        if name in TRANSPOSED:
            return jnp.swapaxes(w[name][l], 0, 1).astype(bf16)
        return w[name][l].astype(bf16)

    def shards(names, l):
        return [shard(name, l) for name in names]

    small = [{name: w[name][l] for name, _ in SMALL if name != "b_ada"} for l in range(2)]
    n_mix, n_late = len(MIXER_WEIGHTS), len(LATE_WEIGHTS)

    first = comm_call(gather_job([c] + shards(MIXER_WEIGHTS, 0)), "gather_first")
    c_all = first[0].reshape(N_DEV, D_MODEL)
    kws = [mixer_operands(dict(zip(MIXER_WEIGHTS, first[1:])), small[0]), None]

    b_cols = lax.dynamic_slice_in_dim(b_ada, me * 768, 768, axis=1)
    mod_cols = ada_fwd(c_all, w_ada, b_cols)
    (mod_all,) = comm_call(gather_job([mod_cols]), "gather_mod")
    mod_me = lax.dynamic_index_in_dim(mod_all, me, axis=2, keepdims=False)
    mods = [mod_me[:, l, :].reshape(6, D_MODEL) for l in range(2)]

    inv_freq = 1.0 / (ROPE_THETA ** (jnp.arange(0, ROPE, 2, dtype=f32) / ROPE))
    inv = _pad_lanes(jnp.concatenate([inv_freq, inv_freq]), NOPE)
    cos_t, sin_t = rope_tables(positions.reshape(seq, 1), inv)

    saved = [None, None]
    h, saved[0], kws[0], got = layer_fwd(
        x[0], mods[0], kws[0], cos_t, sin_t, gather_job(shards(LATE_WEIGHTS, 0) + shards(MIXER_WEIGHTS, 1)),
        lambda got: late_operands(dict(zip(LATE_WEIGHTS, got[:n_late])), small[0]))
    kws[1] = mixer_operands(dict(zip(MIXER_WEIGHTS, got[n_late:])), small[1])
    h, saved[1], kws[1], _ = layer_fwd(
        h, mods[1], kws[1], cos_t, sin_t, gather_job(shards(LATE_WEIGHTS, 1)),
        lambda got: late_operands(dict(zip(LATE_WEIGHTS, got)), small[1]))
    dy, loss_part = loss_fwd(h, loss_target[0])

    early, late = ("w_out", "w_gate_up", "w_down"), ("w_in", "w_q_up", "w_kv_up", "conv_w")
    parts = [{}, {}]
    head, pieces, _ = layer_bwd_head(dy, mods[1], kws[1], saved[1])
    dy, dmod1, grads1, got = layer_bwd_tail(head, mods[1], kws[1], cos_t, sin_t, saved[1], scatter_job([pieces[n] for n in early]))
    parts[1].update(zip(early, got))
    head, pieces, got = layer_bwd_head(dy, mods[0], kws[0], saved[0], scatter_job([grads1[n] for n in late]))
    parts[1].update(zip(late, got))
    dy, dmod0, grads0, got = layer_bwd_tail(head, mods[0], kws[0], cos_t, sin_t, saved[0], scatter_job([pieces[n] for n in early]))
    parts[0].update(zip(early, got))
    parts[0].update(zip(late, comm_call(scatter_job([grads0[n] for n in late]), "scatter_layer0_rest")))
    grad_x = dy[None]

    small_part = {name: jnp.stack([grads0[name], grads1[name]]) for name, _ in SMALL if name != "b_ada"}
    small_part["b_ada"] = jnp.stack([dmod0.reshape(-1), dmod1.reshape(-1)])
    (small_all,) = comm_call(gather_job([_pack_small(lambda n: small_part[n], loss_part[0, 0])]), "gather_small_grads")
    packed = adamw(small_all, _pack_small(lambda n: w[n])[None], _pack_small(lambda n: m[n])[None],
                   _pack_small(lambda n: v[n])[None], 0, None, "adamw_small")
    loss = packed[0][0, -1, -1]
    res = {}
    for key, arr in zip("gdmv", packed):
        for name, val in _unpack_small(arr[0]).items():
            res[key, name] = val

    off = 2 * (1024 + 1024)
    dmod_all = small_all.reshape(N_DEV, -1)[:, off:off + 2 * 6144].reshape(N_DEV, 2, 6144)
    dmod_cols = lax.dynamic_slice_in_dim(dmod_all, me * 768, 768, axis=2).transpose(1, 0, 2)
    g_ada = ada_bwd(c_all, dmod_cols)
    out = None
    for l in range(2):
        out = adamw(g_ada[l][None], w_ada, m_w_ada, v_w_ada, l, out, "adamw_w_ada")
    res.update(zip([(key, "w_ada") for key in "gdmv"], out))

    for name in BIG:
        view = (lambda a: jnp.swapaxes(a, 1, 2)) if name in TRANSPOSED else (lambda a: a)
        out = None
        for l in range(2):
            out = adamw(parts[l][name], view(w[name]), view(m[name]), view(v[name]), l, out, "adamw_" + name)
        res.update(zip([(key, name) for key in "gdmv"], [view(a) for a in out]))

    return (loss, grad_x, *[res["g", n] for n in WEIGHTS], *[res["d", n] for n in WEIGHTS],
            *[res["m", n] for n in WEIGHTS], *[res["v", n] for n in WEIGHTS])
```

```python
import functools

import jax
import jax.numpy as jnp
from jax import lax
from jax.experimental import pallas as pl
from jax.experimental.pallas import tpu as pltpu

f32 = jnp.float32
bf16 = jnp.bfloat16

N_DEV = 8
D_MODEL = 1024
N_HEADS = 8
HEAD_LANES = 128
NOPE = 64
ROPE = 32
V_DIM = 64
Q_RANK = 256
KV_RANK = 128
D_SSD = 512
D_CONV = 1024
SSD_STATE = 128
SSD_HEAD_DIM = 64
CHUNK = 128
HALO = 8
D_FF = 2816
FF_SHARD = 704
D_IN = 1960
D_PROJ = 2048
EPS = 1e-6
LOG2E = 1.4426950408889634
LN2 = 0.6931471805599453
Q_SCALE = (NOPE + ROPE) ** -0.5 * LOG2E
SPARE_Q = NOPE + ROPE
SPARE_V = V_DIM
ATTN_ROWS_FWD = 256
ATTN_HEADS_FWD = 8
ATTN_HEADS_BWD = 4
MLP_FWD_ROWS = 1024
MLP_BWD_CHUNK = 256
MLP_BWD_ROWS = 1024
ROPE_THETA = 10000.0
NEG = -1e30

ADAM_LR = 0.001
ADAM_B1 = 0.9
ADAM_B2 = 0.999
ADAM_EPS = 1e-08
ADAM_WD = 0.01
ADAM_STEP = 10
ADAMW_BLOCK_BYTES = 36 << 20

MESH = pl.DeviceIdType.MESH
ANY = pl.BlockSpec(memory_space=pl.ANY)

SMALL = (("norm1_w", 1024), ("norm2_w", 1024), ("b_ada", 6144), ("q_a_norm_w", 256), ("kv_a_norm_w", 128),
         ("q_nope_norm_w", 64), ("q_pe_norm_w", 32), ("k_nope_norm_w", 64), ("k_pe_norm_w", 32),
         ("conv_b", 1024), ("dt_bias", 8), ("a_log", 8), ("d_skip", 8), ("ssd_norm_w", 512))
SMALL_ROWS = 168
BIG = ("w_in", "w_q_up", "w_kv_up", "conv_w", "w_out", "w_gate_up", "w_down")
TRANSPOSED = ("w_in", "w_gate_up")
WEIGHTS = ("norm1_w", "norm2_w", "w_ada", "b_ada", "w_in", "q_a_norm_w", "w_q_up", "kv_a_norm_w", "w_kv_up",
           "q_nope_norm_w", "q_pe_norm_w", "k_nope_norm_w", "k_pe_norm_w", "conv_w", "conv_b", "dt_bias",
           "a_log", "d_skip", "ssd_norm_w", "w_out", "w_gate_up", "w_down")


def _dot(a, b, ca, cb):
    return lax.dot_general(a.astype(bf16), b.astype(bf16), (((ca,), (cb,)), ((), ())), preferred_element_type=f32)


@jax.custom_vjp
def mm(a, b):
    return _dot(a, b, 1, 0)


def _mm_fwd(a, b):
    return _dot(a, b, 1, 0), (a, b)


def _mm_bwd(res, g):
    a, b = res
    return _dot(g, b, 1, 1).astype(a.dtype), _dot(a, g, 0, 0).astype(b.dtype)


mm.defvjp(_mm_fwd, _mm_bwd)


@jax.custom_vjp
def _mm_slot(a, w, slot):
    return _dot(a, w, 1, 0)


def _mm_slot_fwd(a, w, slot):
    return _dot(a, w, 1, 0), (a, w)


def _mm_slot_bwd(res, g):
    a, w = res
    return _dot(g, w, 1, 1).astype(a.dtype), None, _dot(a, g, 0, 0)


_mm_slot.defvjp(_mm_slot_fwd, _mm_slot_bwd)


def mmw(a, w, slot=None):
    return _dot(a, w, 1, 0) if slot is None else _mm_slot(a, w, slot)


@jax.custom_vjp
def _mm_slot_t(a, wt, slot):
    return _dot(a, wt, 1, 1)


def _mm_slot_t_fwd(a, wt, slot):
    return _dot(a, wt, 1, 1), (a, wt)


def _mm_slot_t_bwd(res, g):
    a, wt = res
    return _dot(g, wt, 1, 0).astype(a.dtype), None, _dot(g, a, 0, 0)


_mm_slot_t.defvjp(_mm_slot_t_fwd, _mm_slot_t_bwd)


def mmw_t(a, wt, slot=None):
    return _dot(a, wt, 1, 1) if slot is None else _mm_slot_t(a, wt, slot)


@jax.custom_vjp
def mm_nt(a, b):
    return _dot(a, b, 1, 1)


def _mm_nt_fwd(a, b):
    return _dot(a, b, 1, 1), (a, b)


def _mm_nt_bwd(res, g):
    a, b = res
    return _dot(g, b, 1, 0).astype(a.dtype), _dot(g, a, 0, 0).astype(b.dtype)


mm_nt.defvjp(_mm_nt_fwd, _mm_nt_bwd)


@jax.custom_vjp
def mm_tn(a, b):
    return _dot(a, b, 0, 0)


def _mm_tn_fwd(a, b):
    return _dot(a, b, 0, 0), (a, b)


def _mm_tn_bwd(res, g):
    a, b = res
    return _dot(b, g, 1, 1).astype(a.dtype), _dot(a, g, 1, 0).astype(b.dtype)


mm_tn.defvjp(_mm_tn_fwd, _mm_tn_bwd)


def _rms(x, w):
    return x * lax.rsqrt(jnp.mean(x * x, axis=-1, keepdims=True) + EPS) * w


def _const(shape):
    n = len(shape)
    return pl.BlockSpec(shape, lambda *_: (0,) * n)


def _accumulate(first, refs, vals):
    @pl.when(first)
    def _():
        for r, v in zip(refs, vals):
            r[...] = v

    @pl.when(jnp.logical_not(first))
    def _():
        for r, v in zip(refs, vals):
            r[...] += v


def _accumulate_then_cast(first, last, accs, outs, vals):
    _accumulate(first, accs, vals)

    @pl.when(last)
    def _():
        for a, o in zip(accs, outs):
            o[...] = a[...].astype(o.dtype)


def _token_block(s):
    return min(512, s)


def _f_proj(x, nw, sh, sc, w, slot=None):
    h = _rms(x, nw) * (1.0 + sc) + sh
    return mmw_t(h, w, slot)


def _f_qkv(pa, plast, cos_t, sin_t, qaw, kvaw, wq, wk, wv, qnw, knw, kpw, slots=None):
    sq, sk, sv = slots if slots is not None else ([None] * N_HEADS,) * 3
    lane = lax.broadcasted_iota(jnp.int32, (1, HEAD_LANES), 1)
    m_nope = lane < NOPE
    m_pe = (lane >= NOPE) & (lane < NOPE + ROPE)
    rows = pa.shape[0]

    def rope(t):
        half = ROPE // 2
        swapped = jnp.concatenate(
            [jnp.zeros((rows, NOPE), f32), t[:, NOPE + half:NOPE + ROPE], t[:, NOPE:NOPE + half],
             jnp.zeros((rows, HEAD_LANES - NOPE - ROPE), f32)], axis=1)
        return t * cos_t + swapped * sin_t

    qa = _rms(pa[:, :Q_RANK], qaw)
    kva = _rms(pa[:, Q_RANK:Q_RANK + KV_RANK], kvaw)
    kp = jnp.where(m_pe, plast, 0.0)
    kp = kp * lax.rsqrt(jnp.sum(kp * kp, axis=-1, keepdims=True) / ROPE + EPS) * kpw
    k_rot = rope(kp)
    qs, ks, vs = [], [], []
    for h in range(N_HEADS):
        qh = mmw(qa, wq[h], sq[h])
        ss_n = jnp.sum(jnp.where(m_nope, qh * qh, 0.0), axis=-1, keepdims=True) / NOPE
        ss_p = jnp.sum(jnp.where(m_pe, qh * qh, 0.0), axis=-1, keepdims=True) / ROPE
        r = jnp.where(m_nope, lax.rsqrt(ss_n + EPS), lax.rsqrt(ss_p + EPS))
        qs.append(rope(qh * r * qnw) * Q_SCALE)
        kh = mmw(kva, wk[h], sk[h])
        kh = kh * lax.rsqrt(jnp.sum(kh * kh, axis=-1, keepdims=True) / NOPE + EPS) * knw
        ks.append(kh + k_rot)
        vs.append(mmw(kva, wv[h], sv[h]))
    return jnp.stack(qs), jnp.stack(ks), jnp.stack(vs)


def _f_ssd(xext, z, plast, prev, cw, cb, dtb, alog, dskip, snw):
    n = CHUNK
    conv = cb
    for k in range(4):
        conv = conv + cw[k:k + 1] * xext[HALO - 3 + k:HALO - 3 + k + n]
    xc = jax.nn.silu(conv)
    xs, bm, cm = xc[:, :D_SSD], xc[:, D_SSD:D_SSD + 2 * SSD_STATE], xc[:, D_SSD + 2 * SSD_STATE:]
    lane = lax.broadcasted_iota(jnp.int32, (1, 128), 1)
    dt = jax.nn.softplus(jnp.where(lane < N_HEADS, plast, 0.0) + dtb)
    adt = dt * (-jnp.exp(alog))
    row = lax.broadcasted_iota(jnp.int32, (n, n), 0)
    col = lax.broadcasted_iota(jnp.int32, (n, n), 1)
    tri = row >= col
    acs = jnp.dot(tri.astype(f32), adt, precision=lax.Precision.HIGHEST, preferred_element_type=f32)
    acs_t = acs.T
    bgs = [bm[:, g * SSD_STATE:(g + 1) * SSD_STATE] for g in range(2)]
    cgs = [cm[:, g * SSD_STATE:(g + 1) * SSD_STATE] for g in range(2)]
    cb_ts = [mm_nt(cgs[g], bgs[g]) for g in range(2)]
    low = lane < SSD_HEAD_DIM
    low_rows = lax.broadcasted_iota(jnp.int32, (2 * SSD_HEAD_DIM, 1), 0) < SSD_HEAD_DIM

    def both(a0, a1):
        return jnp.where(low, a0, a1)

    pre = []
    for i in range(N_HEADS // 2):
        h0, h1 = 2 * i, 2 * i + 1
        col0, col1 = acs[:, h0:h0 + 1], acs[:, h1:h1 + 1]
        last0, last1 = acs[n - 1:n, h0:h0 + 1], acs[n - 1:n, h1:h1 + 1]
        cb_t = cb_ts[i // 2]
        scores0 = cb_t * jnp.exp(jnp.where(tri, col0 - acs_t[h0:h0 + 1, :], -jnp.inf))
        scores1 = cb_t * jnp.exp(jnp.where(tri, col1 - acs_t[h1:h1 + 1, :], -jnp.inf))
        xp = xs[:, i * 128:(i + 1) * 128]
        xdt = xp * both(dt[:, h0:h0 + 1], dt[:, h1:h1 + 1])
        weighted = xdt * both(jnp.exp(last0 - col0), jnp.exp(last1 - col1))
        chunk_decay = jnp.where(low_rows, jnp.exp(last0), jnp.exp(last1))
        in_decay = both(jnp.exp(col0), jnp.exp(col1))
        skip = both(dskip[:, h0:h0 + 1], dskip[:, h1:h1 + 1]) * xp
        pre.append((scores0, scores1, xdt, weighted, chunk_decay, in_decay, skip))
    prods = []
    for i in range(N_HEADS // 2):
        scores0, scores1, xdt, weighted, _, _, _ = pre[i]
        g = i // 2
        y_diag = mm(scores0, jnp.where(low, xdt, 0.0)) + mm(scores1, jnp.where(low, 0.0, xdt))
        prods.append((y_diag, mm_tn(weighted, bgs[g]), mm_nt(cgs[g], prev[i])))
    ys, news = [], []
    for i in range(N_HEADS // 2):
        y_diag, st, y_off = prods[i]
        _, _, _, _, chunk_decay, in_decay, skip = pre[i]
        news.append(chunk_decay * prev[i] + st)
        ys.append(y_diag + y_off * in_decay + skip)
    y = jnp.concatenate(ys, axis=1)
    yg = y * jax.nn.silu(z)
    half = D_SSD // 2
    outs = []
    for g in range(2):
        t = yg[:, g * half:(g + 1) * half]
        outs.append(t * lax.rsqrt(jnp.mean(t * t, axis=-1, keepdims=True) + EPS))
    return jnp.concatenate(outs, axis=1) * snw, jnp.stack(news)


def _f_out(o, yg, g1, wo, slot=None):
    cat = jnp.concatenate([o[h] for h in range(N_HEADS)] + [yg], axis=1)
    return g1 * mmw(cat, wo, slot)


def _f_modulate(x, nw, sh, sc):
    return _rms(x, nw) * (1.0 + sc) + sh


def proj_fwd(x, nw, sh, sc, w):
    s = x.shape[0]
    ts = _token_block(s)

    def body(x_ref, nw_ref, sh_ref, sc_ref, w_ref, pa_ref, pz_ref, px_ref, pl_ref):
        p = _f_proj(x_ref[...], nw_ref[...], sh_ref[...], sc_ref[...], w_ref[...])
        pa_ref[...] = p[:, :384]
        pz_ref[...] = p[:, 384:896]
        px_ref[...] = p[:, 896:1920]
        pl_ref[...] = p[:, 1920:]

    vec = _const((1, D_MODEL))
    return pl.pallas_call(
        body, name="proj_fwd", grid=(s // ts,),
        in_specs=[pl.BlockSpec((ts, D_MODEL), lambda i: (i, 0)), vec, vec, vec, _const((D_PROJ, D_MODEL))],
        out_specs=[pl.BlockSpec((ts, 384), lambda i: (i, 0)), pl.BlockSpec((ts, 512), lambda i: (i, 0)),
                   pl.BlockSpec((ts, 1024), lambda i: (i, 0)), pl.BlockSpec((ts, 128), lambda i: (i, 0))],
        out_shape=[jax.ShapeDtypeStruct((s, 384), f32), jax.ShapeDtypeStruct((s, 512), f32),
                   jax.ShapeDtypeStruct((s, 1024), f32), jax.ShapeDtypeStruct((s, 128), f32)],
    )(x, nw, sh, sc, w)


def rope_tables(pos, inv):
    s = pos.shape[0]
    ts = _token_block(s)

    def body(pos_ref, inv_ref, cos_ref, sin_ref):
        ang = pos_ref[...].astype(f32) * inv_ref[...]
        lane = lax.broadcasted_iota(jnp.int32, (1, HEAD_LANES), 1)
        half = ROPE // 2
        cos_ref[...] = jnp.where(lane < NOPE, 1.0, jnp.where(lane < NOPE + ROPE, jnp.cos(ang), 0.0))
        sn = jnp.sin(ang)
        sin_ref[...] = jnp.where((lane >= NOPE) & (lane < NOPE + half), -sn,
                                 jnp.where((lane >= NOPE + half) & (lane < NOPE + ROPE), sn, 0.0))

    return pl.pallas_call(
        body, name="rope_tables", grid=(s // ts,),
        in_specs=[pl.BlockSpec((ts, 1), lambda i: (i, 0)), _const((1, HEAD_LANES))],
        out_specs=[pl.BlockSpec((ts, HEAD_LANES), lambda i: (i, 0))] * 2,
        out_shape=[jax.ShapeDtypeStruct((s, HEAD_LANES), f32)] * 2,
    )(pos, inv)


def _qkv_param_specs():
    return [_const((1, Q_RANK)), _const((1, KV_RANK)), _const((N_HEADS, Q_RANK, HEAD_LANES)),
            _const((N_HEADS, KV_RANK, HEAD_LANES)), _const((N_HEADS, KV_RANK, V_DIM)),
            _const((1, HEAD_LANES)), _const((1, HEAD_LANES)), _const((1, HEAD_LANES))]


def qkv_fwd(pa, plast, cos_t, sin_t, params):
    s = pa.shape[0]
    ts = _token_block(s)

    def body(pa_ref, pl_ref, cos_ref, sin_ref, *rest):
        prm = [r[...] for r in rest[:8]]
        q_ref, k_ref, v_ref = rest[8:]
        q, k, v = _f_qkv(pa_ref[...], pl_ref[...], cos_ref[...], sin_ref[...], *prm)
        q_ref[...] = q.astype(bf16)
        lane = lax.broadcasted_iota(jnp.int32, (1, 1, HEAD_LANES), 2)
        k_ref[...] = jnp.where((lane == SPARE_Q) | (lane == SPARE_Q + 1), 1.0, k).astype(bf16)
        v_ref[...] = jnp.concatenate([v, jnp.ones_like(v)], axis=-1).astype(bf16)

    tok = lambda w: pl.BlockSpec((ts, w), lambda i: (i, 0))
    head = pl.BlockSpec((N_HEADS, ts, HEAD_LANES), lambda i: (0, i, 0))
    return pl.pallas_call(
        body, name="qkv_fwd", grid=(s // ts,),
        in_specs=[tok(384), tok(128), tok(128), tok(128)] + _qkv_param_specs(),
        out_specs=[head] * 3, out_shape=[jax.ShapeDtypeStruct((N_HEADS, s, HEAD_LANES), bf16)] * 3,
    )(pa, plast, cos_t, sin_t, *params)


def _scores(q, k):
    return lax.dot_general(q, k, (((1,), (1,)), ((), ())), preferred_element_type=f32)


def _tril(rows, cols, row_offset):
    row = row_offset + lax.broadcasted_iota(jnp.int32, (rows, cols), 0)
    col = lax.broadcasted_iota(jnp.int32, (rows, cols), 1)
    return row >= col


def _call_with_job(body, name, grid, job, in_specs, out_specs, out_shape, scratch_shapes, operands, relay_at=None):
    if job is None:
        res = pl.pallas_call(body, name=name, grid=grid, in_specs=in_specs, out_specs=out_specs, out_shape=out_shape,
                             scratch_shapes=scratch_shapes)(*operands)
        return res, None

    def at_step(i, n):
        if i == 0:
            want = [0] * len(grid)
        elif i == n - 1:
            want = [g - 1 for g in grid]
        else:
            want = relay_at
        return functools.reduce(jnp.logical_and, [pl.program_id(a) == s for a, s in enumerate(want)])

    carrier = _carry(job, body, len(in_specs), len(out_specs), at_step)
    res = pl.pallas_call(
        carrier, name=name, grid=grid,
        in_specs=list(in_specs) + [ANY] * len(job.operands), out_specs=list(out_specs) + [ANY] * len(job.out_shape),
        out_shape=list(out_shape) + list(job.out_shape), scratch_shapes=list(scratch_shapes) + job.scratch,
    )(*operands, *job.operands)
    return res[:len(out_specs)], res[len(out_specs):]


def attn_fwd(q, k, v, job=None):
    s = q.shape[1]
    t = _token_block(s)
    nb = s // t

    rb = min(ATTN_ROWS_FWD, t)

    hp = ATTN_HEADS_FWD

    def body(q_ref, k_ref, v_ref, o_ref, qx_ref, m_sc, acc_sc):
        qi = pl.program_id(1)
        m_sc[...] = jnp.full(m_sc.shape, NEG, f32)
        acc_sc[...] = jnp.zeros(acc_sc.shape, f32)

        def step(k0, diagonal):
            chains = [(hh, r) for hh in range(hp) for r in range(t // rb)]

            def scores(hh, r):
                nk = (r + 1) * rb if diagonal else t
                sc = _scores(q_ref[hh, pl.ds(r * rb, rb), :], k_ref[hh, pl.ds(k0, nk), :])
                return jnp.where(_tril(rb, nk, r * rb), sc, NEG) if diagonal else sc

            ahead = scores(*chains[0])
            for c, (hh, r) in enumerate(chains):
                sc = ahead
                if c + 1 < len(chains):
                    ahead = scores(*chains[c + 1])
                rows = pl.ds(r * rb, rb)
                keys = pl.ds(k0, (r + 1) * rb if diagonal else t)
                m_prev = m_sc[hh, rows, :1]
                m_new = jnp.maximum(m_prev, jnp.max(sc, axis=-1, keepdims=True))
                p = jnp.exp2(sc - m_new)
                alpha = jnp.exp2(m_prev - m_new)
                acc = alpha * acc_sc[hh, rows, :] + jnp.dot(p.astype(bf16), v_ref[hh, keys, :], preferred_element_type=f32)
                if diagonal:
                    l = acc[:, V_DIM:V_DIM + 1]
                    o_ref[hh, rows, :] = acc[:, :V_DIM] / l
                    lse = m_new + jnp.log2(l)
                    high = lse.astype(bf16)
                    low = (lse - high.astype(f32)).astype(bf16)
                    lane = lax.broadcasted_iota(jnp.int32, (1, HEAD_LANES), 1)
                    qx_ref[hh, rows, :] = jnp.where(lane == SPARE_Q, -high,
                                                    jnp.where(lane == SPARE_Q + 1, -low, q_ref[hh, rows, :]))
                else:
                    acc_sc[hh, rows, :] = acc
                    m_sc[hh, rows, :] = jnp.broadcast_to(m_new, (rb, 128))

        def below(ki, carry):
            step(pl.multiple_of(ki * t, t), False)
            return carry

        lax.fori_loop(0, qi, below, 0)
        step(pl.multiple_of(qi * t, t), True)

    return _call_with_job(
        body, "attn_fwd" if job is None else "attn_fwd_comm", (N_HEADS // hp, nb), job,
        in_specs=[pl.BlockSpec((hp, t, HEAD_LANES), lambda h, qi: (h, qi, 0)),
                  pl.BlockSpec((hp, s, HEAD_LANES), lambda h, qi: (h, 0, 0)),
                  pl.BlockSpec((hp, s, HEAD_LANES), lambda h, qi: (h, 0, 0))],
        out_specs=[pl.BlockSpec((hp, t, V_DIM), lambda h, qi: (h, qi, 0)),
                   pl.BlockSpec((hp, t, HEAD_LANES), lambda h, qi: (h, qi, 0))],
        out_shape=[jax.ShapeDtypeStruct((N_HEADS, s, V_DIM), f32), jax.ShapeDtypeStruct((N_HEADS, s, HEAD_LANES), bf16)],
        scratch_shapes=[pltpu.VMEM((hp, t, 128), f32), pltpu.VMEM((hp, t, HEAD_LANES), f32)],
        operands=(q, k, v), relay_at=(N_HEADS // hp - 1, max(nb - 2, 0)))


def _ssd_param_specs():
    return [_const((4, D_CONV)), _const((1, D_CONV)), _const((1, 128)), _const((1, 128)), _const((1, 128)),
            _const((1, D_SSD))]


def ssd_fwd(px, pz, plast, params):
    s = px.shape[0]
    nc = s // CHUNK

    def body(px_ref, pz_ref, pl_ref, cw_ref, cb_ref, dtb_ref, alog_ref, dskip_ref, snw_ref, yg_ref, st_ref,
             state_sc, halo_sc):
        i = pl.program_id(0)

        @pl.when(i == 0)
        def _():
            state_sc[...] = jnp.zeros(state_sc.shape, f32)
            halo_sc[...] = jnp.zeros(halo_sc.shape, f32)

        x = px_ref[...]
        prev = state_sc[...]
        st_ref[...] = prev
        xext = jnp.concatenate([halo_sc[...], x], axis=0)
        yg, new = _f_ssd(xext, pz_ref[...], pl_ref[...], prev, cw_ref[...], cb_ref[...], dtb_ref[...],
                         alog_ref[...], dskip_ref[...], snw_ref[...])
        yg_ref[...] = yg
        state_sc[...] = new
        halo_sc[...] = x[CHUNK - HALO:]

    tok = lambda w: pl.BlockSpec((CHUNK, w), lambda i: (i, 0))
    return pl.pallas_call(
        body, name="ssd_fwd", grid=(nc,),
        in_specs=[tok(D_CONV), tok(D_SSD), tok(128)] + _ssd_param_specs(),
        out_specs=[tok(D_SSD), pl.BlockSpec((None, N_HEADS // 2, 2 * SSD_HEAD_DIM, SSD_STATE), lambda i: (i, 0, 0, 0))],
        out_shape=[jax.ShapeDtypeStruct((s, D_SSD), f32),
                   jax.ShapeDtypeStruct((nc, N_HEADS // 2, 2 * SSD_HEAD_DIM, SSD_STATE), f32)],
        scratch_shapes=[pltpu.VMEM((N_HEADS // 2, 2 * SSD_HEAD_DIM, SSD_STATE), f32), pltpu.VMEM((HALO, D_CONV), f32)],
    )(px, pz, plast, *params)


def out_fwd(x, o, yg, g1, wo):
    s = x.shape[0]
    ts = _token_block(s)

    def body(x_ref, o_ref, yg_ref, g1_ref, wo_ref, out_ref):
        out_ref[...] = x_ref[...] + _f_out(o_ref[...], yg_ref[...], g1_ref[...], wo_ref[...])

    return pl.pallas_call(
        body, name="out_fwd", grid=(s // ts,),
        in_specs=[pl.BlockSpec((ts, D_MODEL), lambda i: (i, 0)), pl.BlockSpec((N_HEADS, ts, V_DIM), lambda i: (0, i, 0)),
                  pl.BlockSpec((ts, D_SSD), lambda i: (i, 0)), _const((1, D_MODEL)), _const((D_MODEL, D_MODEL))],
        out_specs=pl.BlockSpec((ts, D_MODEL), lambda i: (i, 0)),
        out_shape=jax.ShapeDtypeStruct((s, D_MODEL), f32),
    )(x, o, yg, g1, wo)


def mlp_fwd(x, nw, sh, sc, g2, wgu, wd):
    s = x.shape[0]
    ts = min(MLP_FWD_ROWS, s)
    nj = N_DEV // 2

    def body(x_ref, nw_ref, sh_ref, sc_ref, g2_ref, wg_ref, wu_ref, wd_ref, out_ref, mix_ref, h_ref, gate_ref, up_ref):
        j = pl.program_id(1)

        @pl.when(j == 0)
        def _():
            h_ref[...] = _f_modulate(x_ref[...], nw_ref[...], sh_ref[...], sc_ref[...]).astype(bf16)
            mix_ref[...] = jnp.zeros(mix_ref.shape, f32)

        nr = max(ts // 512, 1)
        half = ts // nr
        wg, wu, wd = wg_ref[...], wu_ref[...], wd_ref[...]
        products = lambda r: (mmw_t(h_ref[pl.ds(r * half, half), :], wg), mmw_t(h_ref[pl.ds(r * half, half), :], wu))
        ahead = products(0)
        for r in range(nr):
            gate, up = ahead
            if r + 1 < nr:
                ahead = products(r + 1)
            rows = pl.ds(r * half, half)
            gate_ref[rows, :] = gate.astype(bf16)
            up_ref[rows, :] = up.astype(bf16)
            mix_ref[rows, :] += mmw(jax.nn.silu(gate) * up, wd)

        @pl.when(j == nj - 1)
        def _():
            out_ref[...] = x_ref[...] + g2_ref[...] * mix_ref[...]

    vec = _const((1, D_MODEL))
    tok = pl.BlockSpec((ts, D_MODEL), lambda i, j: (i, 0))
    wide = pl.BlockSpec((None, ts, FF_SHARD), lambda i, j: (j, i, 0))
    return pl.pallas_call(
        body, name="mlp_fwd", grid=(s // ts, nj),
        in_specs=[tok, vec, vec, vec, vec,
                  pl.BlockSpec((None, FF_SHARD, D_MODEL), lambda i, j: (j, 0, 0)),
                  pl.BlockSpec((None, FF_SHARD, D_MODEL), lambda i, j: (j + nj, 0, 0)),
                  pl.BlockSpec((None, FF_SHARD, D_MODEL), lambda i, j: (j, 0, 0))],
        out_specs=[tok] * 3 + [wide] * 2,
        out_shape=[jax.ShapeDtypeStruct((s, D_MODEL), f32), jax.ShapeDtypeStruct((s, D_MODEL), f32),
                   jax.ShapeDtypeStruct((s, D_MODEL), bf16)] + [jax.ShapeDtypeStruct((nj, s, FF_SHARD), bf16)] * 2,
    )(x, nw, sh, sc, g2, wgu, wgu, wd)


def loss_fwd(y, target):
    s = y.shape[0]
    ts = _token_block(s)

    def body(y_ref, t_ref, dy_ref, loss_ref):
        d = y_ref[...] - t_ref[...]
        dy_ref[...] = d * (1.0 / D_MODEL)
        part = 0.5 * jnp.sum(jnp.sum(d * d, axis=-1, keepdims=True) * (1.0 / D_MODEL), axis=0, keepdims=True)
        _accumulate(pl.program_id(0) == 0, [loss_ref], [jnp.broadcast_to(part, (8, 128))])

    return pl.pallas_call(
        body, name="loss_fwd", grid=(s // ts,),
        in_specs=[pl.BlockSpec((ts, D_MODEL), lambda i: (i, 0))] * 2,
        out_specs=[pl.BlockSpec((ts, D_MODEL), lambda i: (i, 0)), _const((8, 128))],
        out_shape=[jax.ShapeDtypeStruct((s, D_MODEL), f32), jax.ShapeDtypeStruct((8, 128), f32)],
    )(y, target)


def mlp_bwd(h, dy, gate, up, g2, wgu, wd, job=None):
    s = h.shape[0]
    ts = min(MLP_BWD_ROWS, s)
    nj = N_DEV // 2
    ni = s // ts

    rows_per = min(MLP_BWD_CHUNK, ts)

    def body(h_ref, dy_ref, gate_ref, up_ref, g2_ref, wg_ref, wu_ref, wd_ref, dh_ref, dwg_ref, dwu_ref, dwd_ref,
             ag_sc, au_sc, ad_sc, act_sc, dgate_sc, dup_sc, dmix_sc):
        i = pl.program_id(1)
        wg, wu, wd = wg_ref[...], wu_ref[...], wd_ref[...]
        g2 = g2_ref[...]
        for r in range(ts // rows_per):
            rows = pl.ds(r * rows_per, rows_per)
            act, vjp = jax.vjp(lambda g, u: jax.nn.silu(g) * u, gate_ref[rows, :].astype(f32), up_ref[rows, :].astype(f32))
            dmix = (dy_ref[rows, :] * g2).astype(bf16)
            dgate, dup = vjp(_dot(dmix, wd, 1, 1))
            dgate, dup = dgate.astype(bf16), dup.astype(bf16)
            dh_ref[rows, :] = (_dot(dgate, wg, 1, 0) + _dot(dup, wu, 1, 0)).astype(bf16)
            act_sc[rows, :] = act.astype(bf16)
            dgate_sc[rows, :] = dgate
            dup_sc[rows, :] = dup
            dmix_sc[rows, :] = dmix
        h = h_ref[...]
        grads = [_dot(dgate_sc[...], h, 0, 0), _dot(dup_sc[...], h, 0, 0), _dot(act_sc[...], dmix_sc[...], 0, 0)]
        _accumulate_then_cast(i == 0, i == ni - 1, [ag_sc, au_sc, ad_sc], [dwg_ref, dwu_ref, dwd_ref], grads)

    once = pl.Buffered(1)
    wspec = lambda off: pl.BlockSpec((None, FF_SHARD, D_MODEL), lambda j, i: (j + off, 0, 0), pipeline_mode=once)
    dspec = pl.BlockSpec((None, FF_SHARD, D_MODEL), lambda j, i: (j, 0, 0), pipeline_mode=once)
    wide = pl.BlockSpec((None, ts, FF_SHARD), lambda j, i: (j, i, 0))
    return _call_with_job(
        body, "mlp_bwd" if job is None else "mlp_bwd_comm", (nj, ni), job,
        in_specs=[pl.BlockSpec((ts, D_MODEL), lambda j, i: (i, 0)), pl.BlockSpec((ts, D_MODEL), lambda j, i: (i, 0)),
                  wide, wide, _const((1, D_MODEL)), wspec(0), wspec(nj), dspec],
        out_specs=[pl.BlockSpec((None, ts, D_MODEL), lambda j, i: (j, i, 0)), wspec(0), wspec(0), dspec],
        out_shape=[jax.ShapeDtypeStruct((nj, s, D_MODEL), bf16),
                   jax.ShapeDtypeStruct((nj, FF_SHARD, D_MODEL), bf16), jax.ShapeDtypeStruct((nj, FF_SHARD, D_MODEL), bf16),
                   jax.ShapeDtypeStruct((nj, FF_SHARD, D_MODEL), bf16)],
        scratch_shapes=[pltpu.VMEM((FF_SHARD, D_MODEL), f32), pltpu.VMEM((FF_SHARD, D_MODEL), f32),
                        pltpu.VMEM((FF_SHARD, D_MODEL), f32), pltpu.VMEM((ts, FF_SHARD), bf16),
                        pltpu.VMEM((ts, FF_SHARD), bf16), pltpu.VMEM((ts, FF_SHARD), bf16), pltpu.VMEM((ts, D_MODEL), bf16)],
        operands=(h, dy, gate, up, g2, wgu, wgu, wd))


def out_bwd(dy, dhparts, x, nw, sh, sc, mix, o, yg, g1, wo):
    s = dy.shape[0]
    ts = _token_block(s)
    nj = dhparts.shape[0]

    ni = s // ts

    def body(dy_ref, dp_ref, x_ref, nw_ref, sh_ref, sc_ref, mix_ref, o_ref, yg_ref, g1_ref, wo_ref,
             dx_ref, dnw_ref, dsh_ref, dsc_ref, do_ref, dyg_ref, dg1_ref, dg2_ref, dwo_ref, acc_sc):
        i = pl.program_id(0)
        g = dy_ref[...]
        _accumulate(i == 0, [dg2_ref], [jnp.sum(g * mix_ref[...], axis=0, keepdims=True)])
        dh = dp_ref[0].astype(f32)
        for j in range(1, nj):
            dh = dh + dp_ref[j].astype(f32)
        _, vjp_mod = jax.vjp(_f_modulate, x_ref[...], nw_ref[...], sh_ref[...], sc_ref[...])
        dx_mod, dnw, dsh, dsc = vjp_mod(dh)
        _accumulate(i == 0, [dnw_ref, dsh_ref, dsc_ref], [dnw, dsh, dsc])
        g = g + dx_mod
        dx_ref[...] = g
        o = o_ref[...]
        wo = wo_ref[...]
        _, vjp = jax.vjp(lambda o_, yg_, g1_, slot: _f_out(o_, yg_, g1_, wo, slot), o, yg_ref[...], g1_ref[...],
                         jnp.zeros(wo.shape, f32))
        do, dyg, dg1, dwo = vjp(g)
        delta = jnp.sum(do * o, axis=-1, keepdims=True)
        high = delta.astype(bf16)
        low = (delta - high.astype(f32)).astype(bf16)
        lane = lax.broadcasted_iota(jnp.int32, (1, 1, HEAD_LANES), 2)
        wide = jnp.concatenate([do.astype(bf16), jnp.zeros(do.shape, bf16)], axis=-1)
        do_ref[...] = jnp.where(lane == SPARE_V, -high, jnp.where(lane == SPARE_V + 1, -low, wide))
        dyg_ref[...] = dyg
        _accumulate(i == 0, [dg1_ref], [dg1])
        _accumulate_then_cast(i == 0, i == ni - 1, [acc_sc], [dwo_ref], [dwo])

    head = pl.BlockSpec((N_HEADS, ts, V_DIM), lambda i: (0, i, 0))
    tok = pl.BlockSpec((ts, D_MODEL), lambda i: (i, 0))
    vec = _const((1, D_MODEL))
    vshape = jax.ShapeDtypeStruct((1, D_MODEL), f32)
    return pl.pallas_call(
        body, name="out_bwd", grid=(ni,), scratch_shapes=[pltpu.VMEM((D_MODEL, D_MODEL), f32)],
        in_specs=[tok, pl.BlockSpec((nj, ts, D_MODEL), lambda i: (0, i, 0)), tok, vec, vec, vec, tok,
                  head, pl.BlockSpec((ts, D_SSD), lambda i: (i, 0)), vec, _const((D_MODEL, D_MODEL))],
        out_specs=[tok, vec, vec, vec, pl.BlockSpec((N_HEADS, ts, HEAD_LANES), lambda i: (0, i, 0)),
                   pl.BlockSpec((ts, D_SSD), lambda i: (i, 0)), vec, vec, _const((D_MODEL, D_MODEL))],
        out_shape=[jax.ShapeDtypeStruct((s, D_MODEL), f32), vshape, vshape, vshape,
                   jax.ShapeDtypeStruct((N_HEADS, s, HEAD_LANES), bf16), jax.ShapeDtypeStruct((s, D_SSD), f32),
                   vshape, vshape, jax.ShapeDtypeStruct((D_MODEL, D_MODEL), bf16)],
    )(dy, dhparts, x, nw, sh, sc, mix, o, yg, g1, wo)


def attn_bwd(qx, k, v, do, job=None):
    s = qx.shape[1]
    t = _token_block(s)
    nb = s // t

    hp = ATTN_HEADS_BWD

    def body(q_ref, k_ref, v_ref, do_ref, dq_ref, dk_ref, dv_ref, dv_sc):
        ki = pl.program_id(1)

        @pl.when(ki == 0)
        def _():
            dq_ref[...] = jnp.zeros(dq_ref.shape, f32)

        dk_ref[...] = jnp.zeros(dk_ref.shape, f32)
        dv_sc[...] = jnp.zeros(dv_sc.shape, f32)

        def step(q0, diagonal):
            rows = pl.ds(q0, t)

            def products(hh):
                sc = _scores(q_ref[hh, rows, :], k_ref[hh])
                dps = _scores(do_ref[hh, rows, :], v_ref[hh])
                return (jnp.where(_tril(t, t, 0), sc, NEG) if diagonal else sc), dps

            ahead = products(0)
            for hh in range(hp):
                sc, dps = ahead
                if hh + 1 < hp:
                    ahead = products(hh + 1)
                p = jnp.exp2(sc)
                ds = (p * dps).astype(bf16)
                dv_sc[hh] += lax.dot_general(p.astype(bf16), do_ref[hh, rows, :], (((0,), (0,)), ((), ())),
                                             preferred_element_type=f32)
                dk_ref[hh] += lax.dot_general(ds, q_ref[hh, rows, :], (((0,), (0,)), ((), ())), preferred_element_type=f32)
                dq_ref[hh, rows, :] += jnp.dot(ds, k_ref[hh], preferred_element_type=f32)

        step(pl.multiple_of(ki * t, t), True)

        def above(qi, carry):
            step(pl.multiple_of(qi * t, t), False)
            return carry

        lax.fori_loop(ki + 1, nb, above, 0)
        real = lax.broadcasted_iota(jnp.int32, (1, 1, HEAD_LANES), 2) < SPARE_Q
        dk_ref[...] = jnp.where(real, dk_ref[...] * LN2, 0.0)
        dv_ref[...] = dv_sc[:, :, :V_DIM]

        @pl.when(ki == nb - 1)
        def _():
            dq_ref[...] = jnp.where(real, dq_ref[...] * LN2, 0.0)

    qspec = pl.BlockSpec((hp, s, HEAD_LANES), lambda h, ki: (h, 0, 0))
    kspec = lambda w: pl.BlockSpec((hp, t, w), lambda h, ki: (h, ki, 0))
    return _call_with_job(
        body, "attn_bwd" if job is None else "attn_bwd_comm", (N_HEADS // hp, nb), job,
        in_specs=[qspec, kspec(HEAD_LANES), kspec(HEAD_LANES), qspec],
        out_specs=[qspec, kspec(HEAD_LANES), kspec(V_DIM)],
        out_shape=[jax.ShapeDtypeStruct((N_HEADS, s, HEAD_LANES), f32), jax.ShapeDtypeStruct((N_HEADS, s, HEAD_LANES), f32),
                   jax.ShapeDtypeStruct((N_HEADS, s, V_DIM), f32)],
        scratch_shapes=[pltpu.VMEM((hp, t, HEAD_LANES), f32)], operands=(qx, k, v, do))


def ssd_bwd(px, pz, plast, states, dyg, params):
    s = px.shape[0]
    nc = s // CHUNK
    per = CHUNK // HALO

    def body(px_ref, halo_ref, pz_ref, pl_ref, st_ref, dyg_ref, cw_ref, cb_ref, dtb_ref, alog_ref, dskip_ref, snw_ref,
             dpx_ref, dpz_ref, dpl_ref, dcw_ref, dcb_ref, ddtb_ref, dalog_ref, ddskip_ref, dsnw_ref, dstate_sc, dhalo_sc):
        t = pl.program_id(0)
        chunk = nc - 1 - t

        @pl.when(t == 0)
        def _():
            dstate_sc[...] = jnp.zeros(dstate_sc.shape, f32)
            dhalo_sc[...] = jnp.zeros(dhalo_sc.shape, f32)

        halo = jnp.where(chunk > 0, halo_ref[...], 0.0)
        xext = jnp.concatenate([halo, px_ref[...]], axis=0)
        _, vjp = jax.vjp(_f_ssd, xext, pz_ref[...], pl_ref[...], st_ref[...], cw_ref[...], cb_ref[...], dtb_ref[...],
                         alog_ref[...], dskip_ref[...], snw_ref[...])
        dxext, dz, dpl, dprev, dcw, dcb, ddtb, dalog, ddskip, dsnw = vjp((dyg_ref[...], dstate_sc[...]))
        dpx_ref[...] = dxext[HALO:]
        dpx_ref[CHUNK - HALO:, :] += dhalo_sc[...]
        dhalo_sc[...] = dxext[:HALO]
        dstate_sc[...] = dprev
        dpz_ref[...] = dz
        dpl_ref[...] = dpl
        _accumulate(t == 0, [dcw_ref, dcb_ref, ddtb_ref, dalog_ref, ddskip_ref, dsnw_ref],
                    [dcw, dcb, ddtb, dalog, ddskip, dsnw])

    rev = lambda w: pl.BlockSpec((CHUNK, w), lambda t: (nc - 1 - t, 0))
    pshapes = [jax.ShapeDtypeStruct((4, D_CONV), f32), jax.ShapeDtypeStruct((1, D_CONV), f32),
               jax.ShapeDtypeStruct((1, 128), f32), jax.ShapeDtypeStruct((1, 128), f32),
               jax.ShapeDtypeStruct((1, 128), f32), jax.ShapeDtypeStruct((1, D_SSD), f32)]
    return pl.pallas_call(
        body, name="ssd_bwd", grid=(nc,),
        in_specs=[rev(D_CONV),
                  pl.BlockSpec((HALO, D_CONV), lambda t: (jnp.maximum((nc - 1 - t) * per - 1, 0), 0)),
                  rev(D_SSD), rev(128),
                  pl.BlockSpec((None, N_HEADS // 2, 2 * SSD_HEAD_DIM, SSD_STATE), lambda t: (nc - 1 - t, 0, 0, 0)),
                  rev(D_SSD)] + _ssd_param_specs(),
        out_specs=[rev(D_CONV), rev(D_SSD), rev(128)] + _ssd_param_specs(),
        out_shape=[jax.ShapeDtypeStruct((s, D_CONV), f32), jax.ShapeDtypeStruct((s, D_SSD), f32),
                   jax.ShapeDtypeStruct((s, 128), f32)] + pshapes,
        scratch_shapes=[pltpu.VMEM((N_HEADS // 2, 2 * SSD_HEAD_DIM, SSD_STATE), f32), pltpu.VMEM((HALO, D_CONV), f32)],
    )(px, px, pz, plast, states, dyg, *params)


def qkv_bwd(pa, plast, cos_t, sin_t, params, dq, dk, dv):
    s = pa.shape[0]
    ts = _token_block(s)

    def body(pa_ref, pl_ref, cos_ref, sin_ref, *rest):
        qaw, kvaw, wq, wk, wv, qnw, knw, kpw = [r[...] for r in rest[:8]]
        dq_ref, dk_ref, dv_ref = rest[8:11]
        dpa_ref, dpl_ref = rest[11:13]
        dprm_refs = list(rest[13:])
        cos_t, sin_t = cos_ref[...], sin_ref[...]

        def stage(pa_, pl_, qaw_, kvaw_, sq, sk, sv, qnw_, knw_, kpw_):
            return _f_qkv(pa_, pl_, cos_t, sin_t, qaw_, kvaw_, wq, wk, wv, qnw_, knw_, kpw_, (sq, sk, sv))

        _, vjp = jax.vjp(stage, pa_ref[...], pl_ref[...], qaw, kvaw, jnp.zeros(wq.shape, f32), jnp.zeros(wk.shape, f32),
                         jnp.zeros(wv.shape, f32), qnw, knw, kpw)
        grads = vjp((dq_ref[...], dk_ref[...], dv_ref[...]))
        dpa_ref[...] = grads[0]
        dpl_ref[...] = grads[1]
        _accumulate(pl.program_id(0) == 0, dprm_refs, list(grads[2:]))

    tok = lambda w: pl.BlockSpec((ts, w), lambda i: (i, 0))
    head = lambda w: pl.BlockSpec((N_HEADS, ts, w), lambda i: (0, i, 0))
    pshapes = [jax.ShapeDtypeStruct((1, Q_RANK), f32), jax.ShapeDtypeStruct((1, KV_RANK), f32),
               jax.ShapeDtypeStruct((N_HEADS, Q_RANK, HEAD_LANES), f32), jax.ShapeDtypeStruct((N_HEADS, KV_RANK, HEAD_LANES), f32),
               jax.ShapeDtypeStruct((N_HEADS, KV_RANK, V_DIM), f32), jax.ShapeDtypeStruct((1, HEAD_LANES), f32),
               jax.ShapeDtypeStruct((1, HEAD_LANES), f32), jax.ShapeDtypeStruct((1, HEAD_LANES), f32)]
    return pl.pallas_call(
        body, name="qkv_bwd", grid=(s // ts,),
        in_specs=[tok(384), tok(128), tok(128), tok(128)] + _qkv_param_specs()
                 + [head(HEAD_LANES), head(HEAD_LANES), head(V_DIM)],
        out_specs=[tok(384), tok(128)] + _qkv_param_specs(),
        out_shape=[jax.ShapeDtypeStruct((s, 384), f32), jax.ShapeDtypeStruct((s, 128), f32)] + pshapes,
    )(pa, plast, cos_t, sin_t, *params, dq, dk, dv)


def proj_bwd(x, nw, sh, sc, w, dpa, dpz, dpx, dpl_k, dpl_dt, dres):
    s = x.shape[0]
    ts = _token_block(s)

    ni = s // ts

    def body(x_ref, nw_ref, sh_ref, sc_ref, w_ref, dpa_ref, dpz_ref, dpx_ref, dplk_ref, dpld_ref, dres_ref,
             dx_ref, dnw_ref, dsh_ref, dsc_ref, dw_ref, acc_sc):
        i = pl.program_id(0)
        g = jnp.concatenate([dpa_ref[...], dpz_ref[...], dpx_ref[...], dplk_ref[...] + dpld_ref[...]], axis=1)
        w = w_ref[...]
        _, vjp = jax.vjp(lambda x_, nw_, sh_, sc_, slot: _f_proj(x_, nw_, sh_, sc_, w, slot), x_ref[...], nw_ref[...],
                         sh_ref[...], sc_ref[...], jnp.zeros(w.shape, f32))
        dx, dnw, dsh, dsc, dw = vjp(g)
        dx_ref[...] = dx + dres_ref[...]
        _accumulate(i == 0, [dnw_ref, dsh_ref, dsc_ref], [dnw, dsh, dsc])
        _accumulate_then_cast(i == 0, i == ni - 1, [acc_sc], [dw_ref], [dw])

    vec = _const((1, D_MODEL))
    vshape = jax.ShapeDtypeStruct((1, D_MODEL), f32)
    tok = lambda w_: pl.BlockSpec((ts, w_), lambda i: (i, 0))
    return pl.pallas_call(
        body, name="proj_bwd", grid=(ni,), scratch_shapes=[pltpu.VMEM((D_PROJ, D_MODEL), f32)],
        in_specs=[tok(D_MODEL), vec, vec, vec, _const((D_PROJ, D_MODEL)), tok(384), tok(512), tok(1024), tok(128), tok(128),
                  tok(D_MODEL)],
        out_specs=[tok(D_MODEL), vec, vec, vec, _const((D_PROJ, D_MODEL))],
        out_shape=[jax.ShapeDtypeStruct((s, D_MODEL), f32), vshape, vshape, vshape,
                   jax.ShapeDtypeStruct((D_PROJ, D_MODEL), bf16)],
    )(x, nw, sh, sc, w, dpa, dpz, dpx, dpl_k, dpl_dt, dres)


def ada_fwd(c_all, w_ada, b_cols):
    def body(c_ref, w_ref, b_ref, out_ref):
        act = jax.nn.silu(c_ref[...])
        for l in range(2):
            out_ref[l] = jnp.dot(act, w_ref[l], precision=lax.Precision.HIGHEST, preferred_element_type=f32) + b_ref[l]

    return pl.pallas_call(body, name="ada_fwd", out_shape=jax.ShapeDtypeStruct((2, N_DEV, 768), f32))(c_all, w_ada, b_cols)


def ada_bwd(c_all, dmod_cols):
    def body(c_ref, d_ref, out_ref):
        out_ref[0] = lax.dot_general(jax.nn.silu(c_ref[...]), d_ref[0], (((0,), (0,)), ((), ())),
                                     precision=lax.Precision.HIGHEST, preferred_element_type=f32)

    return pl.pallas_call(
        body, name="ada_bwd", grid=(2,),
        in_specs=[_const((N_DEV, D_MODEL)), pl.BlockSpec((1, N_DEV, 768), lambda l: (l, 0, 0))],
        out_specs=pl.BlockSpec((1, D_MODEL, 768), lambda l: (l, 0, 0)),
        out_shape=jax.ShapeDtypeStruct((2, D_MODEL, 768), f32),
    )(c_all, dmod_cols)


def _adamw(w, g, m, v):
    m = ADAM_B1 * m + (1.0 - ADAM_B1) * g
    v = ADAM_B2 * v + (1.0 - ADAM_B2) * (g * g)
    m_hat = m / (1.0 - ADAM_B1 ** ADAM_STEP)
    v_hat = v / (1.0 - ADAM_B2 ** ADAM_STEP)
    delta = -ADAM_LR * (m_hat / (jnp.sqrt(v_hat) + ADAM_EPS) + ADAM_WD * w)
    return delta, m, v


def adamw(parts, w, m, v, layer, prev, name):
    n, r, c = parts.shape
    nl = w.shape[0]
    per_elem = 2 * (n * parts.dtype.itemsize + 7 * 4)
    lanes = -(-c // 128) * 128
    tr, tc = r, c
    if per_elem * r * lanes > ADAMW_BLOCK_BYTES:
        fits = [t for t in range(r // 2, 15, -1) if r % t == 0 and t % 16 == 0 and per_elem * t * lanes <= ADAMW_BLOCK_BYTES]
        if fits:
            tr = fits[0]
        else:
            tc = next(t for t in (512, 256, 128) if c % t == 0)

    def body(p_ref, w_ref, m_ref, v_ref, *rest):
        g_ref, d_ref, nm_ref, nv_ref = rest[-4:]
        g = p_ref[0].astype(f32)
        for k in range(1, n):
            g = g + p_ref[k].astype(f32)
        delta, nm, nv = _adamw(w_ref[...], g, m_ref[...], v_ref[...])
        g_ref[...] = g
        d_ref[...] = delta
        nm_ref[...] = nm
        nv_ref[...] = nv

    blk = pl.BlockSpec((None, tr, tc), lambda i, j: (layer, i, j))
    shp = jax.ShapeDtypeStruct((nl, r, c), f32)
    kept = [] if prev is None else list(prev)
    return pl.pallas_call(
        body, name=name, grid=(r // tr, c // tc),
        in_specs=[pl.BlockSpec((n, tr, tc), lambda i, j: (0, i, j)), blk, blk, blk] + [ANY] * len(kept),
        out_specs=[blk] * 4, out_shape=[shp] * 4,
        input_output_aliases={4 + j: j for j in range(len(kept))},
    )(parts, w, m, v, *kept)


def _my_index():
    return 4 * lax.axis_index("x") + 2 * lax.axis_index("y") + lax.axis_index("c")


def _coords(idx):
    return (idx // 4, (idx // 2) % 2, idx % 2)


class CommJob:
    def __init__(self, operands, out_shape, phases, scratch):
        self.operands, self.out_shape, self.phases, self.scratch = operands, out_shape, phases, scratch


def _wait(out, n_blocks, send_sem, recv_sem, send=True, recv=True):
    span = out.at[pl.ds(0, n_blocks)]
    desc = pltpu.make_async_remote_copy(src_ref=span, dst_ref=span, send_sem=send_sem, recv_sem=recv_sem,
                                        device_id=_coords(_my_index()), device_id_type=MESH)
    if recv:
        desc.wait_recv()
    if send:
        desc.wait_send()


def gather_job(shards):
    n = len(shards)

    def places():
        x, y, c = lax.axis_index("x"), lax.axis_index("y"), lax.axis_index("c")
        return (x, y, c), (x, y, 1 - c), [(1 - x, y), (x, 1 - y), (1 - x, 1 - y)]

    def index(p):
        return 4 * p[0] + 2 * p[1] + p[2]

    def start(ins, outs, sems):
        far_send, far_recv, near_send, near_recv, local = sems
        me, sibling, chips = places()
        for k in range(n):
            pltpu.make_async_copy(ins[k], outs[k].at[index(me)], local.at[k]).start()
            for chip in chips:
                pltpu.make_async_remote_copy(src_ref=ins[k], dst_ref=outs[k].at[index(me)], send_sem=far_send.at[k],
                                             recv_sem=far_recv.at[k], device_id=(*chip, me[2]), device_id_type=MESH).start()
            pltpu.make_async_remote_copy(src_ref=ins[k], dst_ref=outs[k].at[index(me)], send_sem=near_send.at[k],
                                         recv_sem=near_recv.at[k], device_id=sibling, device_id_type=MESH).start()

    def relay(ins, outs, sems):
        far_send, far_recv, near_send, near_recv, local = sems
        me, sibling, chips = places()
        for k in range(n):
            _wait(outs[k], 3, far_send.at[k], far_recv.at[k], send=False)
            for chip in chips:
                block = outs[k].at[index((*chip, me[2]))]
                pltpu.make_async_remote_copy(src_ref=block, dst_ref=block, send_sem=near_send.at[k],
                                             recv_sem=near_recv.at[k], device_id=sibling, device_id_type=MESH).start()

    def finish(ins, outs, sems):
        far_send, far_recv, near_send, near_recv, local = sems
        for k in range(n):
            _wait(outs[k], 4, near_send.at[k], near_recv.at[k])
            _wait(outs[k], 3, far_send.at[k], far_recv.at[k], recv=False)
            pltpu.make_async_copy(ins[k], outs[k].at[0], local.at[k]).wait()

    shapes = [jax.ShapeDtypeStruct((N_DEV,) + tuple(a.shape), a.dtype) for a in shards]
    return CommJob(list(shards), shapes, [start, relay, finish], [pltpu.SemaphoreType.DMA((n,))] * 5)


def scatter_job(tensors):
    n = len(tensors)
    flat, where = [], {}
    for k, pieces in enumerate(tensors):
        d = 0
        for piece in pieces:
            for b in range(piece.shape[0]):
                where[k, d] = (len(flat), b)
                d += 1
            flat.append(piece)
        assert d == N_DEV

    def start(ins, outs, sems):
        send_sems, recv_sems, local_sems = sems
        me = _my_index()

        def block(k, d):
            i, b = where[k, d]
            return ins[i].at[b]

        for d in range(N_DEV):
            @pl.when(d != me)
            def _():
                for k in range(n):
                    pltpu.make_async_remote_copy(src_ref=block(k, d), dst_ref=outs[k].at[me], send_sem=send_sems.at[k],
                                                 recv_sem=recv_sems.at[k], device_id=(d // 4, (d // 2) % 2, d % 2),
                                                 device_id_type=MESH).start()

            @pl.when(d == me)
            def _():
                for k in range(n):
                    pltpu.make_async_copy(block(k, d), outs[k].at[d], local_sems.at[k]).start()

    def finish(ins, outs, sems):
        send_sems, recv_sems, local_sems = sems
        for k in range(n):
            _wait(outs[k], N_DEV - 1, send_sems.at[k], recv_sems.at[k])
            i, b = where[k, 0]
            pltpu.make_async_copy(ins[i].at[b], outs[k].at[0], local_sems.at[k]).wait()

    shapes = [jax.ShapeDtypeStruct((N_DEV,) + tuple(p[0].shape[1:]), p[0].dtype) for p in tensors]
    return CommJob(flat, shapes, [start, finish], [pltpu.SemaphoreType.DMA((n,))] * 3)


def comm_call(job, name):
    ni, no = len(job.operands), len(job.out_shape)

    def body(*refs):
        ins, outs, sems = refs[:ni], refs[ni:ni + no], refs[ni + no:]
        for phase in job.phases:
            phase(ins, outs, sems)

    return pl.pallas_call(body, name=name, in_specs=[ANY] * ni, out_specs=[ANY] * no, out_shape=job.out_shape,
                          scratch_shapes=job.scratch)(*job.operands)


def _carry(job, body, n_in, n_out, at_step):
    ji, jo, js = len(job.operands), len(job.out_shape), len(job.scratch)

    def carrier(*refs):
        a, b = n_in, n_in + ji
        c, d = b + n_out, b + n_out + jo
        e = len(refs) - js
        job_refs = (refs[a:b], refs[c:d], refs[e:])
        n = len(job.phases)

        @pl.when(at_step(0, n))
        def _():
            job.phases[0](*job_refs)

        body(*refs[:a], *refs[b:c], *refs[d:e])

        for i in range(1, n):
            @pl.when(at_step(i, n))
            def _():
                job.phases[i](*job_refs)

    return carrier


def _pad_lanes(v, lo, total=128):
    return jnp.pad(v, (lo, total - lo - v.shape[0]))[None, :]


MIXER_WEIGHTS = ("w_in", "w_q_up", "w_kv_up", "conv_w")
LATE_WEIGHTS = ("w_out", "w_gate_up", "w_down")


def mixer_operands(g, sw):
    w_in = g["w_in"].reshape(D_IN, D_MODEL)
    zero = lambda rows: jnp.zeros((rows, D_MODEL), w_in.dtype)
    w_proj = jnp.concatenate(
        [w_in[:384], w_in[416:928], w_in[928:1952], w_in[1952:1960], zero(56), w_in[384:416], zero(32)], axis=0)
    wq = jnp.pad(g["w_q_up"], ((0, 0), (0, 0), (0, HEAD_LANES - NOPE - ROPE)))
    wk = jnp.pad(g["w_kv_up"][:, :, :NOPE], ((0, 0), (0, 0), (0, HEAD_LANES - NOPE)))
    wv = g["w_kv_up"][:, :, NOPE:]
    qkv = (sw["q_a_norm_w"][None, :], sw["kv_a_norm_w"][None, :], wq, wk, wv,
           _pad_lanes(jnp.concatenate([sw["q_nope_norm_w"], sw["q_pe_norm_w"]]), 0),
           _pad_lanes(sw["k_nope_norm_w"], 0), _pad_lanes(sw["k_pe_norm_w"], NOPE))
    conv_w = g["conv_w"].astype(f32).transpose(1, 0, 2).reshape(4, D_CONV)
    ssd = (conv_w, sw["conv_b"][None, :], _pad_lanes(sw["dt_bias"], 0), _pad_lanes(sw["a_log"], 0),
           _pad_lanes(sw["d_skip"], 0), sw["ssd_norm_w"][None, :])
    return dict(w_proj=w_proj, qkv=qkv, ssd=ssd, n1=sw["norm1_w"][None, :])


def late_operands(g, sw):
    return dict(wo=g["w_out"].reshape(D_MODEL, D_MODEL), wgu=g["w_gate_up"],
                wd=g["w_down"].reshape(N_DEV // 2, FF_SHARD, D_MODEL), n2=sw["norm2_w"][None, :])


def layer_fwd(x, mod, kw, cos_t, sin_t, job=None, late=None):
    sh1, sc1, g1, sh2, sc2, g2 = [mod[i:i + 1] for i in range(6)]
    pa, pz, px, plast = proj_fwd(x, kw["n1"], sh1, sc1, kw["w_proj"])
    q, k, v = qkv_fwd(pa, plast, cos_t, sin_t, kw["qkv"])
    (o, qx), carried = attn_fwd(q, k, v, job)
    if late is not None:
        kw = {**kw, **late(carried)}
    yg, states = ssd_fwd(px, pz, plast, kw["ssd"])
    x_mid = out_fwd(x, o, yg, g1, kw["wo"])
    x_out, mix, h_mid, gate, up = mlp_fwd(x_mid, kw["n2"], sh2, sc2, g2, kw["wgu"], kw["wd"])
    saved = dict(x=x, pa=pa, pz=pz, px=px, plast=plast, qx=qx, k=k, v=v, o=o, yg=yg, states=states, x_mid=x_mid,
                 mix=mix, h_mid=h_mid, gate=gate, up=up)
    return x_out, saved, kw, carried


def layer_bwd_head(dy, mod, kw, sv, job=None):
    _, _, g1, sh2, sc2, g2 = [mod[i:i + 1] for i in range(6)]
    (dhparts, dwg, dwu, dwd), carried = mlp_bwd(sv["h_mid"], dy, sv["gate"], sv["up"], g2, kw["wgu"], kw["wd"], job)
    dmid, dn2, dsh2, dsc2, do, dyg, dg1, dg2, dwo = out_bwd(
        dy, dhparts, sv["x_mid"], kw["n2"], sh2, sc2, sv["mix"], sv["o"], sv["yg"], g1, kw["wo"])
    early = dict(w_out=[dwo.reshape(N_DEV, D_MODEL // N_DEV, D_MODEL)], w_gate_up=[dwg, dwu],
                 w_down=[dwd.reshape(N_DEV, D_FF // N_DEV, D_MODEL)])
    head = dict(dmid=dmid, do=do, dyg=dyg, dn2=dn2, dsh2=dsh2, dsc2=dsc2, dg2=dg2, dg1=dg1)
    return head, early, carried


def layer_bwd_tail(hd, mod, kw, cos_t, sin_t, sv, job=None):
    sh1, sc1 = mod[0:1], mod[1:2]
    (dq, dk, dv), carried = attn_bwd(sv["qx"], sv["k"], sv["v"], hd["do"], job)
    dpx, dpz, dpl_dt, dcw, dcb, ddtb, dalog, ddskip, dsnw = ssd_bwd(sv["px"], sv["pz"], sv["plast"], sv["states"],
                                                                   hd["dyg"], kw["ssd"])
    dpa, dpl_k, dqaw, dkvaw, dwq, dwk, dwv, dqnw, dknw, dkpw = qkv_bwd(sv["pa"], sv["plast"], cos_t, sin_t, kw["qkv"],
                                                                       dq, dk, dv)
    dx, dn1, dsh1, dsc1, dwp = proj_bwd(sv["x"], kw["n1"], sh1, sc1, kw["w_proj"], dpa, dpz, dpx, dpl_k, dpl_dt, hd["dmid"])
    dmod = jnp.concatenate([dsh1, dsc1, hd["dg1"], hd["dsh2"], hd["dsc2"], hd["dg2"]], axis=0)
    dw_in = jnp.concatenate([dwp[:384], dwp[1984:2016], dwp[384:1920], dwp[1920:1928]], axis=0)
    grads = dict(
        norm1_w=dn1[0], norm2_w=hd["dn2"][0], q_a_norm_w=dqaw[0], kv_a_norm_w=dkvaw[0],
        q_nope_norm_w=dqnw[0, :NOPE], q_pe_norm_w=dqnw[0, NOPE:NOPE + ROPE], k_nope_norm_w=dknw[0, :NOPE],
        k_pe_norm_w=dkpw[0, NOPE:NOPE + ROPE], conv_b=dcb[0], dt_bias=ddtb[0, :N_HEADS], a_log=dalog[0, :N_HEADS],
        d_skip=ddskip[0, :N_HEADS], ssd_norm_w=dsnw[0],
        w_in=[dw_in.reshape(N_DEV, D_IN // N_DEV, D_MODEL)],
        w_q_up=[dwq[:, :, :NOPE + ROPE].astype(bf16)],
        w_kv_up=[jnp.concatenate([dwk[:, :, :NOPE], dwv], axis=2).astype(bf16)],
        conv_w=[dcw.reshape(4, N_DEV, D_CONV // N_DEV).transpose(1, 0, 2).astype(bf16)],
    )
    return dx, dmod, grads, carried


def _pack_small(get, last=None):
    flat = jnp.concatenate([get(name).reshape(-1) for name, _ in SMALL])
    flat = jnp.pad(flat, (0, SMALL_ROWS * 128 - flat.shape[0]))
    if last is not None:
        flat = flat.at[-1].set(last)
    return flat.reshape(SMALL_ROWS, 128)


def _unpack_small(packed):
    flat = packed.reshape(-1)
    out, off = {}, 0
    for name, size in SMALL:
        out[name] = flat[off:off + 2 * size].reshape(2, size)
        off += 2 * size
    return out


def kernel(x, c, positions, norm1_w, norm2_w, w_ada, b_ada, w_in, q_a_norm_w, w_q_up, kv_a_norm_w, w_kv_up, q_nope_norm_w, q_pe_norm_w, k_nope_norm_w, k_pe_norm_w, conv_w, conv_b, dt_bias, a_log, d_skip, ssd_norm_w, w_out, w_gate_up, w_down, loss_target, m_norm1_w, m_norm2_w, m_w_ada, m_b_ada, m_w_in, m_q_a_norm_w, m_w_q_up, m_kv_a_norm_w, m_w_kv_up, m_q_nope_norm_w, m_q_pe_norm_w, m_k_nope_norm_w, m_k_pe_norm_w, m_conv_w, m_conv_b, m_dt_bias, m_a_log, m_d_skip, m_ssd_norm_w, m_w_out, m_w_gate_up, m_w_down, v_norm1_w, v_norm2_w, v_w_ada, v_b_ada, v_w_in, v_q_a_norm_w, v_w_q_up, v_kv_a_norm_w, v_w_kv_up, v_q_nope_norm_w, v_q_pe_norm_w, v_k_nope_norm_w, v_k_pe_norm_w, v_conv_w, v_conv_b, v_dt_bias, v_a_log, v_d_skip, v_ssd_norm_w, v_w_out, v_w_gate_up, v_w_down):
    w = dict(norm1_w=norm1_w, norm2_w=norm2_w, w_ada=w_ada, b_ada=b_ada, w_in=w_in, q_a_norm_w=q_a_norm_w, w_q_up=w_q_up,
             kv_a_norm_w=kv_a_norm_w, w_kv_up=w_kv_up, q_nope_norm_w=q_nope_norm_w, q_pe_norm_w=q_pe_norm_w,
             k_nope_norm_w=k_nope_norm_w, k_pe_norm_w=k_pe_norm_w, conv_w=conv_w, conv_b=conv_b, dt_bias=dt_bias,
             a_log=a_log, d_skip=d_skip, ssd_norm_w=ssd_norm_w, w_out=w_out, w_gate_up=w_gate_up, w_down=w_down)
    m = dict(norm1_w=m_norm1_w, norm2_w=m_norm2_w, w_ada=m_w_ada, b_ada=m_b_ada, w_in=m_w_in, q_a_norm_w=m_q_a_norm_w,
             w_q_up=m_w_q_up, kv_a_norm_w=m_kv_a_norm_w, w_kv_up=m_w_kv_up, q_nope_norm_w=m_q_nope_norm_w,
             q_pe_norm_w=m_q_pe_norm_w, k_nope_norm_w=m_k_nope_norm_w, k_pe_norm_w=m_k_pe_norm_w, conv_w=m_conv_w,
             conv_b=m_conv_b, dt_bias=m_dt_bias, a_log=m_a_log, d_skip=m_d_skip, ssd_norm_w=m_ssd_norm_w, w_out=m_w_out,
             w_gate_up=m_w_gate_up, w_down=m_w_down)
    v = dict(norm1_w=v_norm1_w, norm2_w=v_norm2_w, w_ada=v_w_ada, b_ada=v_b_ada, w_in=v_w_in, q_a_norm_w=v_q_a_norm_w,
             w_q_up=v_w_q_up, kv_a_norm_w=v_kv_a_norm_w, w_kv_up=v_w_kv_up, q_nope_norm_w=v_q_nope_norm_w,
             q_pe_norm_w=v_q_pe_norm_w, k_nope_norm_w=v_k_nope_norm_w, k_pe_norm_w=v_k_pe_norm_w, conv_w=v_conv_w,
             conv_b=v_conv_b, dt_bias=v_dt_bias, a_log=v_a_log, d_skip=v_d_skip, ssd_norm_w=v_ssd_norm_w, w_out=v_w_out,
             w_gate_up=v_w_gate_up, w_down=v_w_down)
    me = _my_index()
    seq = x.shape[1]

    def shard(name, l):
        if name == "conv_w":
            return w[name][l]
        if name in TRANSPOSED:
            return jnp.swapaxes(w[name][l], 0, 1).astype(bf16)
        return w[name][l].astype(bf16)

    def shards(names, l):
        return [shard(name, l) for name in names]

    small = [{name: w[name][l] for name, _ in SMALL if name != "b_ada"} for l in range(2)]
    n_late = len(LATE_WEIGHTS)

    first = comm_call(gather_job([c] + shards(MIXER_WEIGHTS, 0)), "gather_first")
    c_all = first[0].reshape(N_DEV, D_MODEL)
    kws = [mixer_operands(dict(zip(MIXER_WEIGHTS, first[1:])), small[0]), None]

    b_cols = lax.dynamic_slice_in_dim(b_ada, me * 768, 768, axis=1)
    mod_cols = ada_fwd(c_all, w_ada, b_cols)
    (mod_all,) = comm_call(gather_job([mod_cols]), "gather_mod")
    mod_me = lax.dynamic_index_in_dim(mod_all, me, axis=2, keepdims=False)
    mods = [mod_me[:, l, :].reshape(6, D_MODEL) for l in range(2)]

    inv_freq = 1.0 / (ROPE_THETA ** (jnp.arange(0, ROPE, 2, dtype=f32) / ROPE))
    inv = _pad_lanes(jnp.concatenate([inv_freq, inv_freq]), NOPE)
    cos_t, sin_t = rope_tables(positions.reshape(seq, 1), inv)

    saved = [None, None]
    h, saved[0], kws[0], got = layer_fwd(
        x[0], mods[0], kws[0], cos_t, sin_t, gather_job(shards(LATE_WEIGHTS, 0) + shards(MIXER_WEIGHTS, 1)),
        lambda got: late_operands(dict(zip(LATE_WEIGHTS, got[:n_late])), small[0]))
    kws[1] = mixer_operands(dict(zip(MIXER_WEIGHTS, got[n_late:])), small[1])
    h, saved[1], kws[1], _ = layer_fwd(
        h, mods[1], kws[1], cos_t, sin_t, gather_job(shards(LATE_WEIGHTS, 1)),
        lambda got: late_operands(dict(zip(LATE_WEIGHTS, got)), small[1]))
    dy, loss_part = loss_fwd(h, loss_target[0])

    early, late = ("w_out", "w_gate_up", "w_down"), ("w_in", "w_q_up", "w_kv_up", "conv_w")
    parts = [{}, {}]
    head, pieces, _ = layer_bwd_head(dy, mods[1], kws[1], saved[1])
    dy, dmod1, grads1, got = layer_bwd_tail(head, mods[1], kws[1], cos_t, sin_t, saved[1], scatter_job([pieces[n] for n in early]))
    parts[1].update(zip(early, got))
    head, pieces, got = layer_bwd_head(dy, mods[0], kws[0], saved[0], scatter_job([grads1[n] for n in late]))
    parts[1].update(zip(late, got))
    dy, dmod0, grads0, got = layer_bwd_tail(head, mods[0], kws[0], cos_t, sin_t, saved[0], scatter_job([pieces[n] for n in early]))
    parts[0].update(zip(early, got))
    parts[0].update(zip(late, comm_call(scatter_job([grads0[n] for n in late]), "scatter_layer0_rest")))
    grad_x = dy[None]

    small_part = {name: jnp.stack([grads0[name], grads1[name]]) for name, _ in SMALL if name != "b_ada"}
    small_part["b_ada"] = jnp.stack([dmod0.reshape(-1), dmod1.reshape(-1)])
    (small_all,) = comm_call(gather_job([_pack_small(lambda n: small_part[n], loss_part[0, 0])]), "gather_small_grads")
    packed = adamw(small_all, _pack_small(lambda n: w[n])[None], _pack_small(lambda n: m[n])[None],
                   _pack_small(lambda n: v[n])[None], 0, None, "adamw_small")
    loss = packed[0][0, -1, -1]
    res = {}
    for key, arr in zip("gdmv", packed):
        for name, val in _unpack_small(arr[0]).items():
            res[key, name] = val

    off = 2 * (1024 + 1024)
    dmod_all = small_all.reshape(N_DEV, -1)[:, off:off + 2 * 6144].reshape(N_DEV, 2, 6144)
    dmod_cols = lax.dynamic_slice_in_dim(dmod_all, me * 768, 768, axis=2).transpose(1, 0, 2)
    g_ada = ada_bwd(c_all, dmod_cols)
    out = None
    for l in range(2):
        out = adamw(g_ada[l][None], w_ada, m_w_ada, v_w_ada, l, out, "adamw_w_ada")
    res.update(zip([(key, "w_ada") for key in "gdmv"], out))

    for name in BIG:
        view = (lambda a: jnp.swapaxes(a, 1, 2)) if name in TRANSPOSED else (lambda a: a)
        out = None
        for l in range(2):
            out = adamw(parts[l][name], view(w[name]), view(m[name]), view(v[name]), l, out, "adamw_" + name)
        res.update(zip([(key, name) for key in "gdmv"], [view(a) for a in out]))

    return (loss, grad_x, *[res["g", n] for n in WEIGHTS], *[res["d", n] for n in WEIGHTS],
            *[res["m", n] for n in WEIGHTS], *[res["v", n] for n in WEIGHTS])
```

```python
import functools

import jax
import jax.numpy as jnp
from jax import lax
from jax.experimental import pallas as pl
from jax.experimental.pallas import tpu as pltpu

f32 = jnp.float32
bf16 = jnp.bfloat16

N_DEV = 8
D_MODEL = 1024
N_HEADS = 8
HEAD_LANES = 128
NOPE = 64
ROPE = 32
V_DIM = 64
Q_RANK = 256
KV_RANK = 128
D_SSD = 512
D_CONV = 1024
SSD_STATE = 128
SSD_HEAD_DIM = 64
CHUNK = 128
HALO = 8
D_FF = 2816
FF_SHARD = 704
D_IN = 1960
D_PROJ = 2048
EPS = 1e-6
LOG2E = 1.4426950408889634
LN2 = 0.6931471805599453
Q_SCALE = (NOPE + ROPE) ** -0.5 * LOG2E
SPARE_Q = NOPE + ROPE
SPARE_V = V_DIM
ATTN_ROWS_FWD = 256
ATTN_HEADS_FWD = 8
ATTN_HEADS_BWD = 4
MLP_FWD_ROWS = 1024
MLP_BWD_CHUNK = 256
MLP_BWD_ROWS = 1024
ROPE_THETA = 10000.0
NEG = -1e30

ADAM_LR = 0.001
ADAM_B1 = 0.9
ADAM_B2 = 0.999
ADAM_EPS = 1e-08
ADAM_WD = 0.01
ADAM_STEP = 10
ADAMW_BLOCK_BYTES = 36 << 20

MESH = pl.DeviceIdType.MESH
ANY = pl.BlockSpec(memory_space=pl.ANY)

SMALL = (("norm1_w", 1024), ("norm2_w", 1024), ("b_ada", 6144), ("q_a_norm_w", 256), ("kv_a_norm_w", 128),
         ("q_nope_norm_w", 64), ("q_pe_norm_w", 32), ("k_nope_norm_w", 64), ("k_pe_norm_w", 32),
         ("conv_b", 1024), ("dt_bias", 8), ("a_log", 8), ("d_skip", 8), ("ssd_norm_w", 512))
SMALL_ROWS = 168
BIG = ("w_in", "w_q_up", "w_kv_up", "conv_w", "w_out", "w_gate_up", "w_down")
TRANSPOSED = ("w_in", "w_gate_up")
WEIGHTS = ("norm1_w", "norm2_w", "w_ada", "b_ada", "w_in", "q_a_norm_w", "w_q_up", "kv_a_norm_w", "w_kv_up",
           "q_nope_norm_w", "q_pe_norm_w", "k_nope_norm_w", "k_pe_norm_w", "conv_w", "conv_b", "dt_bias",
           "a_log", "d_skip", "ssd_norm_w", "w_out", "w_gate_up", "w_down")


def _dot(a, b, ca, cb):
    return lax.dot_general(a.astype(bf16), b.astype(bf16), (((ca,), (cb,)), ((), ())), preferred_element_type=f32)


@jax.custom_vjp
def mm(a, b):
    return _dot(a, b, 1, 0)


def _mm_fwd(a, b):
    return _dot(a, b, 1, 0), (a, b)


def _mm_bwd(res, g):
    a, b = res
    return _dot(g, b, 1, 1).astype(a.dtype), _dot(a, g, 0, 0).astype(b.dtype)


mm.defvjp(_mm_fwd, _mm_bwd)


@jax.custom_vjp
def _mm_slot(a, w, slot):
    return _dot(a, w, 1, 0)


def _mm_slot_fwd(a, w, slot):
    return _dot(a, w, 1, 0), (a, w)


def _mm_slot_bwd(res, g):
    a, w = res
    return _dot(g, w, 1, 1).astype(a.dtype), None, _dot(a, g, 0, 0)


_mm_slot.defvjp(_mm_slot_fwd, _mm_slot_bwd)


def mmw(a, w, slot=None):
    return _dot(a, w, 1, 0) if slot is None else _mm_slot(a, w, slot)


@jax.custom_vjp
def _mm_slot_t(a, wt, slot):
    return _dot(a, wt, 1, 1)


def _mm_slot_t_fwd(a, wt, slot):
    return _dot(a, wt, 1, 1), (a, wt)


def _mm_slot_t_bwd(res, g):
    a, wt = res
    return _dot(g, wt, 1, 0).astype(a.dtype), None, _dot(g, a, 0, 0)


_mm_slot_t.defvjp(_mm_slot_t_fwd, _mm_slot_t_bwd)


def mmw_t(a, wt, slot=None):
    return _dot(a, wt, 1, 1) if slot is None else _mm_slot_t(a, wt, slot)


@jax.custom_vjp
def mm_nt(a, b):
    return _dot(a, b, 1, 1)


def _mm_nt_fwd(a, b):
    return _dot(a, b, 1, 1), (a, b)


def _mm_nt_bwd(res, g):
    a, b = res
    return _dot(g, b, 1, 0).astype(a.dtype), _dot(g, a, 0, 0).astype(b.dtype)


mm_nt.defvjp(_mm_nt_fwd, _mm_nt_bwd)


@jax.custom_vjp
def mm_tn(a, b):
    return _dot(a, b, 0, 0)


def _mm_tn_fwd(a, b):
    return _dot(a, b, 0, 0), (a, b)


def _mm_tn_bwd(res, g):
    a, b = res
    return _dot(b, g, 1, 1).astype(a.dtype), _dot(a, g, 1, 0).astype(b.dtype)


mm_tn.defvjp(_mm_tn_fwd, _mm_tn_bwd)


def _rms(x, w):
    return x * lax.rsqrt(jnp.mean(x * x, axis=-1, keepdims=True) + EPS) * w


def _const(shape):
    n = len(shape)
    return pl.BlockSpec(shape, lambda *_: (0,) * n)


def _accumulate(first, refs, vals):
    @pl.when(first)
    def _():
        for r, v in zip(refs, vals):
            r[...] = v

    @pl.when(jnp.logical_not(first))
    def _():
        for r, v in zip(refs, vals):
            r[...] += v


def _accumulate_then_cast(first, last, accs, outs, vals):
    _accumulate(first, accs, vals)

    @pl.when(last)
    def _():
        for a, o in zip(accs, outs):
            o[...] = a[...].astype(o.dtype)


def _token_block(s):
    return min(512, s)


def _f_proj(x, nw, sh, sc, w, slot=None):
    h = _rms(x, nw) * (1.0 + sc) + sh
    return mmw_t(h, w, slot)


def _f_qkv(pa, plast, cos_t, sin_t, qaw, kvaw, wq, wk, wv, qnw, knw, kpw, slots=None):
    sq, sk, sv = slots if slots is not None else ([None] * N_HEADS,) * 3
    lane = lax.broadcasted_iota(jnp.int32, (1, HEAD_LANES), 1)
    m_nope = lane < NOPE
    m_pe = (lane >= NOPE) & (lane < NOPE + ROPE)
    rows = pa.shape[0]

    def rope(t):
        half = ROPE // 2
        swapped = jnp.concatenate(
            [jnp.zeros((rows, NOPE), f32), t[:, NOPE + half:NOPE + ROPE], t[:, NOPE:NOPE + half],
             jnp.zeros((rows, HEAD_LANES - NOPE - ROPE), f32)], axis=1)
        return t * cos_t + swapped * sin_t

    qa = _rms(pa[:, :Q_RANK], qaw)
    kva = _rms(pa[:, Q_RANK:Q_RANK + KV_RANK], kvaw)
    kp = jnp.where(m_pe, plast, 0.0)
    kp = kp * lax.rsqrt(jnp.sum(kp * kp, axis=-1, keepdims=True) / ROPE + EPS) * kpw
    k_rot = rope(kp)
    qs, ks, vs = [], [], []
    for h in range(N_HEADS):
        qh = mmw(qa, wq[h], sq[h])
        ss_n = jnp.sum(jnp.where(m_nope, qh * qh, 0.0), axis=-1, keepdims=True) / NOPE
        ss_p = jnp.sum(jnp.where(m_pe, qh * qh, 0.0), axis=-1, keepdims=True) / ROPE
        r = jnp.where(m_nope, lax.rsqrt(ss_n + EPS), lax.rsqrt(ss_p + EPS))
        qs.append(rope(qh * r * qnw) * Q_SCALE)
        kh = mmw(kva, wk[h], sk[h])
        kh = kh * lax.rsqrt(jnp.sum(kh * kh, axis=-1, keepdims=True) / NOPE + EPS) * knw
        ks.append(kh + k_rot)
        vs.append(mmw(kva, wv[h], sv[h]))
    return jnp.stack(qs), jnp.stack(ks), jnp.stack(vs)


def _f_ssd(xext, z, plast, prev, cw, cb, dtb, alog, dskip, snw):
    n = CHUNK
    conv = cb
    for k in range(4):
        conv = conv + cw[k:k + 1] * xext[HALO - 3 + k:HALO - 3 + k + n]
    xc = jax.nn.silu(conv)
    xs, bm, cm = xc[:, :D_SSD], xc[:, D_SSD:D_SSD + 2 * SSD_STATE], xc[:, D_SSD + 2 * SSD_STATE:]
    lane = lax.broadcasted_iota(jnp.int32, (1, 128), 1)
    dt = jax.nn.softplus(jnp.where(lane < N_HEADS, plast, 0.0) + dtb)
    adt = dt * (-jnp.exp(alog))
    row = lax.broadcasted_iota(jnp.int32, (n, n), 0)
    col = lax.broadcasted_iota(jnp.int32, (n, n), 1)
    tri = row >= col
    acs = jnp.dot(tri.astype(f32), adt, precision=lax.Precision.HIGHEST, preferred_element_type=f32)
    acs_t = acs.T
    bgs = [bm[:, g * SSD_STATE:(g + 1) * SSD_STATE] for g in range(2)]
    cgs = [cm[:, g * SSD_STATE:(g + 1) * SSD_STATE] for g in range(2)]
    cb_ts = [mm_nt(cgs[g], bgs[g]) for g in range(2)]
    low = lane < SSD_HEAD_DIM
    low_rows = lax.broadcasted_iota(jnp.int32, (2 * SSD_HEAD_DIM, 1), 0) < SSD_HEAD_DIM

    def both(a0, a1):
        return jnp.where(low, a0, a1)

    pre = []
    for i in range(N_HEADS // 2):
        h0, h1 = 2 * i, 2 * i + 1
        col0, col1 = acs[:, h0:h0 + 1], acs[:, h1:h1 + 1]
        last0, last1 = acs[n - 1:n, h0:h0 + 1], acs[n - 1:n, h1:h1 + 1]
        cb_t = cb_ts[i // 2]
        scores0 = cb_t * jnp.exp(jnp.where(tri, col0 - acs_t[h0:h0 + 1, :], -jnp.inf))
        scores1 = cb_t * jnp.exp(jnp.where(tri, col1 - acs_t[h1:h1 + 1, :], -jnp.inf))
        xp = xs[:, i * 128:(i + 1) * 128]
        xdt = xp * both(dt[:, h0:h0 + 1], dt[:, h1:h1 + 1])
        weighted = xdt * both(jnp.exp(last0 - col0), jnp.exp(last1 - col1))
        chunk_decay = jnp.where(low_rows, jnp.exp(last0), jnp.exp(last1))
        in_decay = both(jnp.exp(col0), jnp.exp(col1))
        skip = both(dskip[:, h0:h0 + 1], dskip[:, h1:h1 + 1]) * xp
        pre.append((scores0, scores1, xdt, weighted, chunk_decay, in_decay, skip))
    prods = []
    for i in range(N_HEADS // 2):
        scores0, scores1, xdt, weighted, _, _, _ = pre[i]
        g = i // 2
        y_diag = mm(scores0, jnp.where(low, xdt, 0.0)) + mm(scores1, jnp.where(low, 0.0, xdt))
        prods.append((y_diag, mm_tn(weighted, bgs[g]), mm_nt(cgs[g], prev[i])))
    ys, news = [], []
    for i in range(N_HEADS // 2):
        y_diag, st, y_off = prods[i]
        _, _, _, _, chunk_decay, in_decay, skip = pre[i]
        news.append(chunk_decay * prev[i] + st)
        ys.append(y_diag + y_off * in_decay + skip)
    y = jnp.concatenate(ys, axis=1)
    yg = y * jax.nn.silu(z)
    half = D_SSD // 2
    outs = []
    for g in range(2):
        t = yg[:, g * half:(g + 1) * half]
        outs.append(t * lax.rsqrt(jnp.mean(t * t, axis=-1, keepdims=True) + EPS))
    return jnp.concatenate(outs, axis=1) * snw, jnp.stack(news)


def _f_out(o, yg, g1, wo, slot=None):
    cat = jnp.concatenate([o[h] for h in range(N_HEADS)] + [yg], axis=1)
    return g1 * mmw(cat, wo, slot)


def _f_modulate(x, nw, sh, sc):
    return _rms(x, nw) * (1.0 + sc) + sh


def proj_fwd(x, nw, sh, sc, w):
    s = x.shape[0]
    ts = _token_block(s)

    def body(x_ref, nw_ref, sh_ref, sc_ref, w_ref, pa_ref, pz_ref, px_ref, pl_ref):
        p = _f_proj(x_ref[...], nw_ref[...], sh_ref[...], sc_ref[...], w_ref[...])
        pa_ref[...] = p[:, :384]
        pz_ref[...] = p[:, 384:896]
        px_ref[...] = p[:, 896:1920]
        pl_ref[...] = p[:, 1920:]

    vec = _const((1, D_MODEL))
    return pl.pallas_call(
        body, name="proj_fwd", grid=(s // ts,),
        in_specs=[pl.BlockSpec((ts, D_MODEL), lambda i: (i, 0)), vec, vec, vec, _const((D_PROJ, D_MODEL))],
        out_specs=[pl.BlockSpec((ts, 384), lambda i: (i, 0)), pl.BlockSpec((ts, 512), lambda i: (i, 0)),
                   pl.BlockSpec((ts, 1024), lambda i: (i, 0)), pl.BlockSpec((ts, 128), lambda i: (i, 0))],
        out_shape=[jax.ShapeDtypeStruct((s, 384), f32), jax.ShapeDtypeStruct((s, 512), f32),
                   jax.ShapeDtypeStruct((s, 1024), f32), jax.ShapeDtypeStruct((s, 128), f32)],
    )(x, nw, sh, sc, w)


def rope_tables(pos, inv):
    s = pos.shape[0]
    ts = _token_block(s)

    def body(pos_ref, inv_ref, cos_ref, sin_ref):
        ang = pos_ref[...].astype(f32) * inv_ref[...]
        lane = lax.broadcasted_iota(jnp.int32, (1, HEAD_LANES), 1)
        half = ROPE // 2
        cos_ref[...] = jnp.where(lane < NOPE, 1.0, jnp.where(lane < NOPE + ROPE, jnp.cos(ang), 0.0))
        sn = jnp.sin(ang)
        sin_ref[...] = jnp.where((lane >= NOPE) & (lane < NOPE + half), -sn,
                                 jnp.where((lane >= NOPE + half) & (lane < NOPE + ROPE), sn, 0.0))

    return pl.pallas_call(
        body, name="rope_tables", grid=(s // ts,),
        in_specs=[pl.BlockSpec((ts, 1), lambda i: (i, 0)), _const((1, HEAD_LANES))],
        out_specs=[pl.BlockSpec((ts, HEAD_LANES), lambda i: (i, 0))] * 2,
        out_shape=[jax.ShapeDtypeStruct((s, HEAD_LANES), f32)] * 2,
    )(pos, inv)


def _qkv_param_specs():
    return [_const((1, Q_RANK)), _const((1, KV_RANK)), _const((N_HEADS, Q_RANK, HEAD_LANES)),
            _const((N_HEADS, KV_RANK, HEAD_LANES)), _const((N_HEADS, KV_RANK, V_DIM)),
            _const((1, HEAD_LANES)), _const((1, HEAD_LANES)), _const((1, HEAD_LANES))]


def qkv_fwd(pa, plast, cos_t, sin_t, params):
    s = pa.shape[0]
    ts = _token_block(s)

    def body(pa_ref, pl_ref, cos_ref, sin_ref, *rest):
        prm = [r[...] for r in rest[:8]]
        q_ref, k_ref, v_ref = rest[8:]
        q, k, v = _f_qkv(pa_ref[...], pl_ref[...], cos_ref[...], sin_ref[...], *prm)
        q_ref[...] = q.astype(bf16)
        lane = lax.broadcasted_iota(jnp.int32, (1, 1, HEAD_LANES), 2)
        k_ref[...] = jnp.where((lane == SPARE_Q) | (lane == SPARE_Q + 1), 1.0, k).astype(bf16)
        v_ref[...] = jnp.concatenate([v, jnp.ones_like(v)], axis=-1).astype(bf16)

    tok = lambda w: pl.BlockSpec((ts, w), lambda i: (i, 0))
    head = pl.BlockSpec((N_HEADS, ts, HEAD_LANES), lambda i: (0, i, 0))
    return pl.pallas_call(
        body, name="qkv_fwd", grid=(s // ts,),
        in_specs=[tok(384), tok(128), tok(128), tok(128)] + _qkv_param_specs(),
        out_specs=[head] * 3, out_shape=[jax.ShapeDtypeStruct((N_HEADS, s, HEAD_LANES), bf16)] * 3,
    )(pa, plast, cos_t, sin_t, *params)


def _scores(q, k):
    return lax.dot_general(q, k, (((1,), (1,)), ((), ())), preferred_element_type=f32)


def _tril(rows, cols, row_offset):
    row = row_offset + lax.broadcasted_iota(jnp.int32, (rows, cols), 0)
    col = lax.broadcasted_iota(jnp.int32, (rows, cols), 1)
    return row >= col


def _call_with_job(body, name, grid, job, in_specs, out_specs, out_shape, scratch_shapes, operands, relay_at=None):
    if job is None:
        res = pl.pallas_call(body, name=name, grid=grid, in_specs=in_specs, out_specs=out_specs, out_shape=out_shape,
                             scratch_shapes=scratch_shapes)(*operands)
        return res, None

    def at_step(i, n):
        if i == 0:
            want = [0] * len(grid)
        elif i == n - 1:
            want = [g - 1 for g in grid]
        else:
            want = relay_at
        return functools.reduce(jnp.logical_and, [pl.program_id(a) == s for a, s in enumerate(want)])

    carrier = _carry(job, body, len(in_specs), len(out_specs), at_step)
    res = pl.pallas_call(
        carrier, name=name, grid=grid,
        in_specs=list(in_specs) + [ANY] * len(job.operands), out_specs=list(out_specs) + [ANY] * len(job.out_shape),
        out_shape=list(out_shape) + list(job.out_shape), scratch_shapes=list(scratch_shapes) + job.scratch,
    )(*operands, *job.operands)
    return res[:len(out_specs)], res[len(out_specs):]


def attn_fwd(q, k, v, job=None):
    s = q.shape[1]
    t = _token_block(s)
    nb = s // t

    rb = min(ATTN_ROWS_FWD, t)

    hp = ATTN_HEADS_FWD

    def body(q_ref, k_ref, v_ref, o_ref, qx_ref, m_sc, acc_sc):
        qi = pl.program_id(1)
        m_sc[...] = jnp.full(m_sc.shape, NEG, f32)
        acc_sc[...] = jnp.zeros(acc_sc.shape, f32)

        def step(k0, diagonal):
            chains = [(hh, r) for hh in range(hp) for r in range(t // rb)]

            def scores(hh, r):
                nk = (r + 1) * rb if diagonal else t
                sc = _scores(q_ref[hh, pl.ds(r * rb, rb), :], k_ref[hh, pl.ds(k0, nk), :])
                return jnp.where(_tril(rb, nk, r * rb), sc, NEG) if diagonal else sc

            ahead = scores(*chains[0])
            for c, (hh, r) in enumerate(chains):
                sc = ahead
                if c + 1 < len(chains):
                    ahead = scores(*chains[c + 1])
                rows = pl.ds(r * rb, rb)
                keys = pl.ds(k0, (r + 1) * rb if diagonal else t)
                m_prev = m_sc[hh, rows, :1]
                m_new = jnp.maximum(m_prev, jnp.max(sc, axis=-1, keepdims=True))
                p = jnp.exp2(sc - m_new)
                alpha = jnp.exp2(m_prev - m_new)
                acc = alpha * acc_sc[hh, rows, :] + jnp.dot(p.astype(bf16), v_ref[hh, keys, :], preferred_element_type=f32)
                if diagonal:
                    l = acc[:, V_DIM:V_DIM + 1]
                    o_ref[hh, rows, :] = acc[:, :V_DIM] / l
                    lse = m_new + jnp.log2(l)
                    high = lse.astype(bf16)
                    low = (lse - high.astype(f32)).astype(bf16)
                    lane = lax.broadcasted_iota(jnp.int32, (1, HEAD_LANES), 1)
                    qx_ref[hh, rows, :] = jnp.where(lane == SPARE_Q, -high,
                                                    jnp.where(lane == SPARE_Q + 1, -low, q_ref[hh, rows, :]))
                else:
                    acc_sc[hh, rows, :] = acc
                    m_sc[hh, rows, :] = jnp.broadcast_to(m_new, (rb, 128))

        def below(ki, carry):
            step(pl.multiple_of(ki * t, t), False)
            return carry

        lax.fori_loop(0, qi, below, 0)
        step(pl.multiple_of(qi * t, t), True)

    return _call_with_job(
        body, "attn_fwd" if job is None else "attn_fwd_comm", (N_HEADS // hp, nb), job,
        in_specs=[pl.BlockSpec((hp, t, HEAD_LANES), lambda h, qi: (h, qi, 0)),
                  pl.BlockSpec((hp, s, HEAD_LANES), lambda h, qi: (h, 0, 0)),
                  pl.BlockSpec((hp, s, HEAD_LANES), lambda h, qi: (h, 0, 0))],
        out_specs=[pl.BlockSpec((hp, t, V_DIM), lambda h, qi: (h, qi, 0)),
                   pl.BlockSpec((hp, t, HEAD_LANES), lambda h, qi: (h, qi, 0))],
        out_shape=[jax.ShapeDtypeStruct((N_HEADS, s, V_DIM), f32), jax.ShapeDtypeStruct((N_HEADS, s, HEAD_LANES), bf16)],
        scratch_shapes=[pltpu.VMEM((hp, t, 128), f32), pltpu.VMEM((hp, t, HEAD_LANES), f32)],
        operands=(q, k, v), relay_at=(N_HEADS // hp - 1, max(nb - 2, 0)))


def _ssd_param_specs():
    return [_const((4, D_CONV)), _const((1, D_CONV)), _const((1, 128)), _const((1, 128)), _const((1, 128)),
            _const((1, D_SSD))]


def ssd_fwd(px, pz, plast, params):
    s = px.shape[0]
    nc = s // CHUNK

    def body(px_ref, pz_ref, pl_ref, cw_ref, cb_ref, dtb_ref, alog_ref, dskip_ref, snw_ref, yg_ref, st_ref,
             state_sc, halo_sc):
        i = pl.program_id(0)

        @pl.when(i == 0)
        def _():
            state_sc[...] = jnp.zeros(state_sc.shape, f32)
            halo_sc[...] = jnp.zeros(halo_sc.shape, f32)

        x = px_ref[...]
        prev = state_sc[...]
        st_ref[...] = prev
        xext = jnp.concatenate([halo_sc[...], x], axis=0)
        yg, new = _f_ssd(xext, pz_ref[...], pl_ref[...], prev, cw_ref[...], cb_ref[...], dtb_ref[...],
                         alog_ref[...], dskip_ref[...], snw_ref[...])
        yg_ref[...] = yg
        state_sc[...] = new
        halo_sc[...] = x[CHUNK - HALO:]

    tok = lambda w: pl.BlockSpec((CHUNK, w), lambda i: (i, 0))
    return pl.pallas_call(
        body, name="ssd_fwd", grid=(nc,),
        in_specs=[tok(D_CONV), tok(D_SSD), tok(128)] + _ssd_param_specs(),
        out_specs=[tok(D_SSD), pl.BlockSpec((None, N_HEADS // 2, 2 * SSD_HEAD_DIM, SSD_STATE), lambda i: (i, 0, 0, 0))],
        out_shape=[jax.ShapeDtypeStruct((s, D_SSD), f32),
                   jax.ShapeDtypeStruct((nc, N_HEADS // 2, 2 * SSD_HEAD_DIM, SSD_STATE), f32)],
        scratch_shapes=[pltpu.VMEM((N_HEADS // 2, 2 * SSD_HEAD_DIM, SSD_STATE), f32), pltpu.VMEM((HALO, D_CONV), f32)],
    )(px, pz, plast, *params)


def out_fwd(x, o, yg, g1, wo):
    s = x.shape[0]
    ts = _token_block(s)

    def body(x_ref, o_ref, yg_ref, g1_ref, wo_ref, out_ref):
        out_ref[...] = x_ref[...] + _f_out(o_ref[...], yg_ref[...], g1_ref[...], wo_ref[...])

    return pl.pallas_call(
        body, name="out_fwd", grid=(s // ts,),
        in_specs=[pl.BlockSpec((ts, D_MODEL), lambda i: (i, 0)), pl.BlockSpec((N_HEADS, ts, V_DIM), lambda i: (0, i, 0)),
                  pl.BlockSpec((ts, D_SSD), lambda i: (i, 0)), _const((1, D_MODEL)), _const((D_MODEL, D_MODEL))],
        out_specs=pl.BlockSpec((ts, D_MODEL), lambda i: (i, 0)),
        out_shape=jax.ShapeDtypeStruct((s, D_MODEL), f32),
    )(x, o, yg, g1, wo)


def mlp_fwd(x, nw, sh, sc, g2, wgu, wd):
    s = x.shape[0]
    ts = min(MLP_FWD_ROWS, s)
    nj = N_DEV // 2

    def body(x_ref, nw_ref, sh_ref, sc_ref, g2_ref, wg_ref, wu_ref, wd_ref, out_ref, mix_ref, h_ref, gate_ref, up_ref):
        j = pl.program_id(1)

        @pl.when(j == 0)
        def _():
            h_ref[...] = _f_modulate(x_ref[...], nw_ref[...], sh_ref[...], sc_ref[...]).astype(bf16)
            mix_ref[...] = jnp.zeros(mix_ref.shape, f32)

        nr = max(ts // 512, 1)
        half = ts // nr
        wg, wu, wd = wg_ref[...], wu_ref[...], wd_ref[...]
        products = lambda r: (mmw_t(h_ref[pl.ds(r * half, half), :], wg), mmw_t(h_ref[pl.ds(r * half, half), :], wu))
        ahead = products(0)
        for r in range(nr):
            gate, up = ahead
            if r + 1 < nr:
                ahead = products(r + 1)
            rows = pl.ds(r * half, half)
            gate_ref[rows, :] = gate.astype(bf16)
            up_ref[rows, :] = up.astype(bf16)
            mix_ref[rows, :] += mmw(jax.nn.silu(gate) * up, wd)

        @pl.when(j == nj - 1)
        def _():
            out_ref[...] = x_ref[...] + g2_ref[...] * mix_ref[...]

    vec = _const((1, D_MODEL))
    tok = pl.BlockSpec((ts, D_MODEL), lambda i, j: (i, 0))
    wide = pl.BlockSpec((None, ts, FF_SHARD), lambda i, j: (j, i, 0))
    return pl.pallas_call(
        body, name="mlp_fwd", grid=(s // ts, nj),
        in_specs=[tok, vec, vec, vec, vec,
                  pl.BlockSpec((None, FF_SHARD, D_MODEL), lambda i, j: (j, 0, 0)),
                  pl.BlockSpec((None, FF_SHARD, D_MODEL), lambda i, j: (j + nj, 0, 0)),
                  pl.BlockSpec((None, FF_SHARD, D_MODEL), lambda i, j: (j, 0, 0))],
        out_specs=[tok] * 3 + [wide] * 2,
        out_shape=[jax.ShapeDtypeStruct((s, D_MODEL), f32), jax.ShapeDtypeStruct((s, D_MODEL), f32),
                   jax.ShapeDtypeStruct((s, D_MODEL), bf16)] + [jax.ShapeDtypeStruct((nj, s, FF_SHARD), bf16)] * 2,
    )(x, nw, sh, sc, g2, wgu, wgu, wd)


def loss_fwd(y, target):
    s = y.shape[0]
    ts = _token_block(s)

    def body(y_ref, t_ref, dy_ref, loss_ref):
        d = y_ref[...] - t_ref[...]
        dy_ref[...] = d * (1.0 / D_MODEL)
        part = 0.5 * jnp.sum(jnp.sum(d * d, axis=-1, keepdims=True) * (1.0 / D_MODEL), axis=0, keepdims=True)
        _accumulate(pl.program_id(0) == 0, [loss_ref], [jnp.broadcast_to(part, (8, 128))])

    return pl.pallas_call(
        body, name="loss_fwd", grid=(s // ts,),
        in_specs=[pl.BlockSpec((ts, D_MODEL), lambda i: (i, 0))] * 2,
        out_specs=[pl.BlockSpec((ts, D_MODEL), lambda i: (i, 0)), _const((8, 128))],
        out_shape=[jax.ShapeDtypeStruct((s, D_MODEL), f32), jax.ShapeDtypeStruct((8, 128), f32)],
    )(y, target)


def mlp_bwd(h, dy, gate, up, g2, wgu, wd, job=None):
    s = h.shape[0]
    ts = min(MLP_BWD_ROWS, s)
    nj = N_DEV // 2
    ni = s // ts

    rows_per = min(MLP_BWD_CHUNK, ts)

    def body(h_ref, dy_ref, gate_ref, up_ref, g2_ref, wg_ref, wu_ref, wd_ref, dh_ref, dwg_ref, dwu_ref, dwd_ref,
             ag_sc, au_sc, ad_sc, act_sc, dgate_sc, dup_sc, dmix_sc):
        i = pl.program_id(1)
        wg, wu, wd = wg_ref[...], wu_ref[...], wd_ref[...]
        g2 = g2_ref[...]
        for r in range(ts // rows_per):
            rows = pl.ds(r * rows_per, rows_per)
            act, vjp = jax.vjp(lambda g, u: jax.nn.silu(g) * u, gate_ref[rows, :].astype(f32), up_ref[rows, :].astype(f32))
            dmix = (dy_ref[rows, :] * g2).astype(bf16)
            dgate, dup = vjp(_dot(dmix, wd, 1, 1))
            dgate, dup = dgate.astype(bf16), dup.astype(bf16)
            dh_ref[rows, :] = (_dot(dgate, wg, 1, 0) + _dot(dup, wu, 1, 0)).astype(bf16)
            act_sc[rows, :] = act.astype(bf16)
            dgate_sc[rows, :] = dgate
            dup_sc[rows, :] = dup
            dmix_sc[rows, :] = dmix
        h = h_ref[...]
        grads = [_dot(dgate_sc[...], h, 0, 0), _dot(dup_sc[...], h, 0, 0), _dot(act_sc[...], dmix_sc[...], 0, 0)]
        _accumulate_then_cast(i == 0, i == ni - 1, [ag_sc, au_sc, ad_sc], [dwg_ref, dwu_ref, dwd_ref], grads)

    once = pl.Buffered(1)
    wspec = lambda off: pl.BlockSpec((None, FF_SHARD, D_MODEL), lambda j, i: (j + off, 0, 0), pipeline_mode=once)
    dspec = pl.BlockSpec((None, FF_SHARD, D_MODEL), lambda j, i: (j, 0, 0), pipeline_mode=once)
    wide = pl.BlockSpec((None, ts, FF_SHARD), lambda j, i: (j, i, 0))
    return _call_with_job(
        body, "mlp_bwd" if job is None else "mlp_bwd_comm", (nj, ni), job,
        in_specs=[pl.BlockSpec((ts, D_MODEL), lambda j, i: (i, 0)), pl.BlockSpec((ts, D_MODEL), lambda j, i: (i, 0)),
                  wide, wide, _const((1, D_MODEL)), wspec(0), wspec(nj), dspec],
        out_specs=[pl.BlockSpec((None, ts, D_MODEL), lambda j, i: (j, i, 0)), wspec(0), wspec(0), dspec],
        out_shape=[jax.ShapeDtypeStruct((nj, s, D_MODEL), bf16),
                   jax.ShapeDtypeStruct((nj, FF_SHARD, D_MODEL), bf16), jax.ShapeDtypeStruct((nj, FF_SHARD, D_MODEL), bf16),
                   jax.ShapeDtypeStruct((nj, FF_SHARD, D_MODEL), bf16)],
        scratch_shapes=[pltpu.VMEM((FF_SHARD, D_MODEL), f32), pltpu.VMEM((FF_SHARD, D_MODEL), f32),
                        pltpu.VMEM((FF_SHARD, D_MODEL), f32), pltpu.VMEM((ts, FF_SHARD), bf16),
                        pltpu.VMEM((ts, FF_SHARD), bf16), pltpu.VMEM((ts, FF_SHARD), bf16), pltpu.VMEM((ts, D_MODEL), bf16)],
        operands=(h, dy, gate, up, g2, wgu, wgu, wd))


def out_bwd(dy, dhparts, x, nw, sh, sc, mix, o, yg, g1, wo):
    s = dy.shape[0]
    ts = _token_block(s)
    nj = dhparts.shape[0]

    ni = s // ts

    def body(dy_ref, dp_ref, x_ref, nw_ref, sh_ref, sc_ref, mix_ref, o_ref, yg_ref, g1_ref, wo_ref,
             dx_ref, dnw_ref, dsh_ref, dsc_ref, do_ref, dyg_ref, dg1_ref, dg2_ref, dwo_ref, acc_sc):
        i = pl.program_id(0)
        g = dy_ref[...]
        _accumulate(i == 0, [dg2_ref], [jnp.sum(g * mix_ref[...], axis=0, keepdims=True)])
        dh = dp_ref[0].astype(f32)
        for j in range(1, nj):
            dh = dh + dp_ref[j].astype(f32)
        _, vjp_mod = jax.vjp(_f_modulate, x_ref[...], nw_ref[...], sh_ref[...], sc_ref[...])
        dx_mod, dnw, dsh, dsc = vjp_mod(dh)
        _accumulate(i == 0, [dnw_ref, dsh_ref, dsc_ref], [dnw, dsh, dsc])
        g = g + dx_mod
        dx_ref[...] = g
        o = o_ref[...]
        wo = wo_ref[...]
        _, vjp = jax.vjp(lambda o_, yg_, g1_, slot: _f_out(o_, yg_, g1_, wo, slot), o, yg_ref[...], g1_ref[...],
                         jnp.zeros(wo.shape, f32))
        do, dyg, dg1, dwo = vjp(g)
        delta = jnp.sum(do * o, axis=-1, keepdims=True)
        high = delta.astype(bf16)
        low = (delta - high.astype(f32)).astype(bf16)
        lane = lax.broadcasted_iota(jnp.int32, (1, 1, HEAD_LANES), 2)
        wide = jnp.concatenate([do.astype(bf16), jnp.zeros(do.shape, bf16)], axis=-1)
        do_ref[...] = jnp.where(lane == SPARE_V, -high, jnp.where(lane == SPARE_V + 1, -low, wide))
        dyg_ref[...] = dyg
        _accumulate(i == 0, [dg1_ref], [dg1])
        _accumulate_then_cast(i == 0, i == ni - 1, [acc_sc], [dwo_ref], [dwo])

    head = pl.BlockSpec((N_HEADS, ts, V_DIM), lambda i: (0, i, 0))
    tok = pl.BlockSpec((ts, D_MODEL), lambda i: (i, 0))
    vec = _const((1, D_MODEL))
    vshape = jax.ShapeDtypeStruct((1, D_MODEL), f32)
    return pl.pallas_call(
        body, name="out_bwd", grid=(ni,), scratch_shapes=[pltpu.VMEM((D_MODEL, D_MODEL), f32)],
        in_specs=[tok, pl.BlockSpec((nj, ts, D_MODEL), lambda i: (0, i, 0)), tok, vec, vec, vec, tok,
                  head, pl.BlockSpec((ts, D_SSD), lambda i: (i, 0)), vec, _const((D_MODEL, D_MODEL))],
        out_specs=[tok, vec, vec, vec, pl.BlockSpec((N_HEADS, ts, HEAD_LANES), lambda i: (0, i, 0)),
                   pl.BlockSpec((ts, D_SSD), lambda i: (i, 0)), vec, vec, _const((D_MODEL, D_MODEL))],
        out_shape=[jax.ShapeDtypeStruct((s, D_MODEL), f32), vshape, vshape, vshape,
                   jax.ShapeDtypeStruct((N_HEADS, s, HEAD_LANES), bf16), jax.ShapeDtypeStruct((s, D_SSD), f32),
                   vshape, vshape, jax.ShapeDtypeStruct((D_MODEL, D_MODEL), bf16)],
    )(dy, dhparts, x, nw, sh, sc, mix, o, yg, g1, wo)


def attn_bwd(qx, k, v, do, job=None):
    s = qx.shape[1]
    t = _token_block(s)
    nb = s // t

    hp = ATTN_HEADS_BWD

    def body(q_ref, k_ref, v_ref, do_ref, dq_ref, dk_ref, dv_ref, dv_sc):
        ki = pl.program_id(1)

        @pl.when(ki == 0)
        def _():
            dq_ref[...] = jnp.zeros(dq_ref.shape, f32)

        dk_ref[...] = jnp.zeros(dk_ref.shape, f32)
        dv_sc[...] = jnp.zeros(dv_sc.shape, f32)

        def step(q0, diagonal):
            rows = pl.ds(q0, t)

            def products(hh):
                sc = _scores(q_ref[hh, rows, :], k_ref[hh])
                dps = _scores(do_ref[hh, rows, :], v_ref[hh])
                return (jnp.where(_tril(t, t, 0), sc, NEG) if diagonal else sc), dps

            ahead = products(0)
            for hh in range(hp):
                sc, dps = ahead
                if hh + 1 < hp:
                    ahead = products(hh + 1)
                p = jnp.exp2(sc)
                ds = (p * dps).astype(bf16)
                dv_sc[hh] += lax.dot_general(p.astype(bf16), do_ref[hh, rows, :], (((0,), (0,)), ((), ())),
                                             preferred_element_type=f32)
                dk_ref[hh] += lax.dot_general(ds, q_ref[hh, rows, :], (((0,), (0,)), ((), ())), preferred_element_type=f32)
                dq_ref[hh, rows, :] += jnp.dot(ds, k_ref[hh], preferred_element_type=f32)

        step(pl.multiple_of(ki * t, t), True)

        def above(qi, carry):
            step(pl.multiple_of(qi * t, t), False)
            return carry

        lax.fori_loop(ki + 1, nb, above, 0)
        real = lax.broadcasted_iota(jnp.int32, (1, 1, HEAD_LANES), 2) < SPARE_Q
        dk_ref[...] = jnp.where(real, dk_ref[...] * LN2, 0.0)
        dv_ref[...] = dv_sc[:, :, :V_DIM]

        @pl.when(ki == nb - 1)
        def _():
            dq_ref[...] = jnp.where(real, dq_ref[...] * LN2, 0.0)

    qspec = pl.BlockSpec((hp, s, HEAD_LANES), lambda h, ki: (h, 0, 0))
    kspec = lambda w: pl.BlockSpec((hp, t, w), lambda h, ki: (h, ki, 0))
    return _call_with_job(
        body, "attn_bwd" if job is None else "attn_bwd_comm", (N_HEADS // hp, nb), job,
        in_specs=[qspec, kspec(HEAD_LANES), kspec(HEAD_LANES), qspec],
        out_specs=[qspec, kspec(HEAD_LANES), kspec(V_DIM)],
        out_shape=[jax.ShapeDtypeStruct((N_HEADS, s, HEAD_LANES), f32), jax.ShapeDtypeStruct((N_HEADS, s, HEAD_LANES), f32),
                   jax.ShapeDtypeStruct((N_HEADS, s, V_DIM), f32)],
        scratch_shapes=[pltpu.VMEM((hp, t, HEAD_LANES), f32)], operands=(qx, k, v, do))


def ssd_bwd(px, pz, plast, states, dyg, params):
    s = px.shape[0]
    nc = s // CHUNK
    per = CHUNK // HALO

    def body(px_ref, halo_ref, pz_ref, pl_ref, st_ref, dyg_ref, cw_ref, cb_ref, dtb_ref, alog_ref, dskip_ref, snw_ref,
             dpx_ref, dpz_ref, dpl_ref, dcw_ref, dcb_ref, ddtb_ref, dalog_ref, ddskip_ref, dsnw_ref, dstate_sc, dhalo_sc):
        t = pl.program_id(0)
        chunk = nc - 1 - t

        @pl.when(t == 0)
        def _():
            dstate_sc[...] = jnp.zeros(dstate_sc.shape, f32)
            dhalo_sc[...] = jnp.zeros(dhalo_sc.shape, f32)

        halo = jnp.where(chunk > 0, halo_ref[...], 0.0)
        xext = jnp.concatenate([halo, px_ref[...]], axis=0)
        _, vjp = jax.vjp(_f_ssd, xext, pz_ref[...], pl_ref[...], st_ref[...], cw_ref[...], cb_ref[...], dtb_ref[...],
                         alog_ref[...], dskip_ref[...], snw_ref[...])
        dxext, dz, dpl, dprev, dcw, dcb, ddtb, dalog, ddskip, dsnw = vjp((dyg_ref[...], dstate_sc[...]))
        dpx_ref[...] = dxext[HALO:]
        dpx_ref[CHUNK - HALO:, :] += dhalo_sc[...]
        dhalo_sc[...] = dxext[:HALO]
        dstate_sc[...] = dprev
        dpz_ref[...] = dz
        dpl_ref[...] = dpl
        _accumulate(t == 0, [dcw_ref, dcb_ref, ddtb_ref, dalog_ref, ddskip_ref, dsnw_ref],
                    [dcw, dcb, ddtb, dalog, ddskip, dsnw])

    rev = lambda w: pl.BlockSpec((CHUNK, w), lambda t: (nc - 1 - t, 0))
    pshapes = [jax.ShapeDtypeStruct((4, D_CONV), f32), jax.ShapeDtypeStruct((1, D_CONV), f32),
               jax.ShapeDtypeStruct((1, 128), f32), jax.ShapeDtypeStruct((1, 128), f32),
               jax.ShapeDtypeStruct((1, 128), f32), jax.ShapeDtypeStruct((1, D_SSD), f32)]
    return pl.pallas_call(
        body, name="ssd_bwd", grid=(nc,),
        in_specs=[rev(D_CONV),
                  pl.BlockSpec((HALO, D_CONV), lambda t: (jnp.maximum((nc - 1 - t) * per - 1, 0), 0)),
                  rev(D_SSD), rev(128),
                  pl.BlockSpec((None, N_HEADS // 2, 2 * SSD_HEAD_DIM, SSD_STATE), lambda t: (nc - 1 - t, 0, 0, 0)),
                  rev(D_SSD)] + _ssd_param_specs(),
        out_specs=[rev(D_CONV), rev(D_SSD), rev(128)] + _ssd_param_specs(),
        out_shape=[jax.ShapeDtypeStruct((s, D_CONV), f32), jax.ShapeDtypeStruct((s, D_SSD), f32),
                   jax.ShapeDtypeStruct((s, 128), f32)] + pshapes,
        scratch_shapes=[pltpu.VMEM((N_HEADS // 2, 2 * SSD_HEAD_DIM, SSD_STATE), f32), pltpu.VMEM((HALO, D_CONV), f32)],
    )(px, px, pz, plast, states, dyg, *params)


def qkv_bwd(pa, plast, cos_t, sin_t, params, dq, dk, dv):
    s = pa.shape[0]
    ts = _token_block(s)

    def body(pa_ref, pl_ref, cos_ref, sin_ref, *rest):
        qaw, kvaw, wq, wk, wv, qnw, knw, kpw = [r[...] for r in rest[:8]]
        dq_ref, dk_ref, dv_ref = rest[8:11]
        dpa_ref, dpl_ref = rest[11:13]
        dprm_refs = list(rest[13:])
        cos_t, sin_t = cos_ref[...], sin_ref[...]

        def stage(pa_, pl_, qaw_, kvaw_, sq, sk, sv, qnw_, knw_, kpw_):
            return _f_qkv(pa_, pl_, cos_t, sin_t, qaw_, kvaw_, wq, wk, wv, qnw_, knw_, kpw_, (sq, sk, sv))

        _, vjp = jax.vjp(stage, pa_ref[...], pl_ref[...], qaw, kvaw, jnp.zeros(wq.shape, f32), jnp.zeros(wk.shape, f32),
                         jnp.zeros(wv.shape, f32), qnw, knw, kpw)
        grads = vjp((dq_ref[...], dk_ref[...], dv_ref[...]))
        dpa_ref[...] = grads[0]
        dpl_ref[...] = grads[1]
        _accumulate(pl.program_id(0) == 0, dprm_refs, list(grads[2:]))

    tok = lambda w: pl.BlockSpec((ts, w), lambda i: (i, 0))
    head = lambda w: pl.BlockSpec((N_HEADS, ts, w), lambda i: (0, i, 0))
    pshapes = [jax.ShapeDtypeStruct((1, Q_RANK), f32), jax.ShapeDtypeStruct((1, KV_RANK), f32),
               jax.ShapeDtypeStruct((N_HEADS, Q_RANK, HEAD_LANES), f32), jax.ShapeDtypeStruct((N_HEADS, KV_RANK, HEAD_LANES), f32),
               jax.ShapeDtypeStruct((N_HEADS, KV_RANK, V_DIM), f32), jax.ShapeDtypeStruct((1, HEAD_LANES), f32),
               jax.ShapeDtypeStruct((1, HEAD_LANES), f32), jax.ShapeDtypeStruct((1, HEAD_LANES), f32)]
    return pl.pallas_call(
        body, name="qkv_bwd", grid=(s // ts,),
        in_specs=[tok(384), tok(128), tok(128), tok(128)] + _qkv_param_specs()
                 + [head(HEAD_LANES), head(HEAD_LANES), head(V_DIM)],
        out_specs=[tok(384), tok(128)] + _qkv_param_specs(),
        out_shape=[jax.ShapeDtypeStruct((s, 384), f32), jax.ShapeDtypeStruct((s, 128), f32)] + pshapes,
    )(pa, plast, cos_t, sin_t, *params, dq, dk, dv)


def proj_bwd(x, nw, sh, sc, w, dpa, dpz, dpx, dpl_k, dpl_dt, dres):
    s = x.shape[0]
    ts = _token_block(s)

    ni = s // ts

    def body(x_ref, nw_ref, sh_ref, sc_ref, w_ref, dpa_ref, dpz_ref, dpx_ref, dplk_ref, dpld_ref, dres_ref,
             dx_ref, dnw_ref, dsh_ref, dsc_ref, dw_ref, acc_sc):
        i = pl.program_id(0)
        g = jnp.concatenate([dpa_ref[...], dpz_ref[...], dpx_ref[...], dplk_ref[...] + dpld_ref[...]], axis=1)
        w = w_ref[...]
        _, vjp = jax.vjp(lambda x_, nw_, sh_, sc_, slot: _f_proj(x_, nw_, sh_, sc_, w, slot), x_ref[...], nw_ref[...],
                         sh_ref[...], sc_ref[...], jnp.zeros(w.shape, f32))
        dx, dnw, dsh, dsc, dw = vjp(g)
        dx_ref[...] = dx + dres_ref[...]
        _accumulate(i == 0, [dnw_ref, dsh_ref, dsc_ref], [dnw, dsh, dsc])
        _accumulate_then_cast(i == 0, i == ni - 1, [acc_sc], [dw_ref], [dw])

    vec = _const((1, D_MODEL))
    vshape = jax.ShapeDtypeStruct((1, D_MODEL), f32)
    tok = lambda w_: pl.BlockSpec((ts, w_), lambda i: (i, 0))
    return pl.pallas_call(
        body, name="proj_bwd", grid=(ni,), scratch_shapes=[pltpu.VMEM((D_PROJ, D_MODEL), f32)],
        in_specs=[tok(D_MODEL), vec, vec, vec, _const((D_PROJ, D_MODEL)), tok(384), tok(512), tok(1024), tok(128), tok(128),
                  tok(D_MODEL)],
        out_specs=[tok(D_MODEL), vec, vec, vec, _const((D_PROJ, D_MODEL))],
        out_shape=[jax.ShapeDtypeStruct((s, D_MODEL), f32), vshape, vshape, vshape,
                   jax.ShapeDtypeStruct((D_PROJ, D_MODEL), bf16)],
    )(x, nw, sh, sc, w, dpa, dpz, dpx, dpl_k, dpl_dt, dres)


def ada_fwd(c_all, w_ada, b_cols):
    def body(c_ref, w_ref, b_ref, out_ref):
        act = jax.nn.silu(c_ref[...])
        for l in range(2):
            out_ref[l] = jnp.dot(act, w_ref[l], precision=lax.Precision.HIGHEST, preferred_element_type=f32) + b_ref[l]

    return pl.pallas_call(body, name="ada_fwd", out_shape=jax.ShapeDtypeStruct((2, N_DEV, 768), f32))(c_all, w_ada, b_cols)


def ada_bwd(c_all, dmod_cols):
    def body(c_ref, d_ref, out_ref):
        out_ref[0] = lax.dot_general(jax.nn.silu(c_ref[...]), d_ref[0], (((0,), (0,)), ((), ())),
                                     precision=lax.Precision.HIGHEST, preferred_element_type=f32)

    return pl.pallas_call(
        body, name="ada_bwd", grid=(2,),
        in_specs=[_const((N_DEV, D_MODEL)), pl.BlockSpec((1, N_DEV, 768), lambda l: (l, 0, 0))],
        out_specs=pl.BlockSpec((1, D_MODEL, 768), lambda l: (l, 0, 0)),
        out_shape=jax.ShapeDtypeStruct((2, D_MODEL, 768), f32),
    )(c_all, dmod_cols)


def _adamw(w, g, m, v):
    m = ADAM_B1 * m + (1.0 - ADAM_B1) * g
    v = ADAM_B2 * v + (1.0 - ADAM_B2) * (g * g)
    m_hat = m / (1.0 - ADAM_B1 ** ADAM_STEP)
    v_hat = v / (1.0 - ADAM_B2 ** ADAM_STEP)
    delta = -ADAM_LR * (m_hat / (jnp.sqrt(v_hat) + ADAM_EPS) + ADAM_WD * w)
    return delta, m, v


def adamw(parts, w, m, v, layer, prev, name):
    n, r, c = parts.shape
    nl = w.shape[0]
    per_elem = 2 * (n * parts.dtype.itemsize + 7 * 4)
    lanes = -(-c // 128) * 128
    tr, tc = r, c
    if per_elem * r * lanes > ADAMW_BLOCK_BYTES:
        fits = [t for t in range(r // 2, 15, -1) if r % t == 0 and t % 16 == 0 and per_elem * t * lanes <= ADAMW_BLOCK_BYTES]
        if fits:
            tr = fits[0]
        else:
            tc = next(t for t in (512, 256, 128) if c % t == 0)

    def body(p_ref, w_ref, m_ref, v_ref, *rest):
        g_ref, d_ref, nm_ref, nv_ref = rest[-4:]
        g = p_ref[0].astype(f32)
        for k in range(1, n):
            g = g + p_ref[k].astype(f32)
        delta, nm, nv = _adamw(w_ref[...], g, m_ref[...], v_ref[...])
        g_ref[...] = g
        d_ref[...] = delta
        nm_ref[...] = nm
        nv_ref[...] = nv

    blk = pl.BlockSpec((None, tr, tc), lambda i, j: (layer, i, j))
    shp = jax.ShapeDtypeStruct((nl, r, c), f32)
    kept = [] if prev is None else list(prev)
    return pl.pallas_call(
        body, name=name, grid=(r // tr, c // tc),
        in_specs=[pl.BlockSpec((n, tr, tc), lambda i, j: (0, i, j)), blk, blk, blk] + [ANY] * len(kept),
        out_specs=[blk] * 4, out_shape=[shp] * 4,
        input_output_aliases={4 + j: j for j in range(len(kept))},
    )(parts, w, m, v, *kept)


def _my_index():
    return 4 * lax.axis_index("x") + 2 * lax.axis_index("y") + lax.axis_index("c")


def _coords(idx):
    return (idx // 4, (idx // 2) % 2, idx % 2)


class CommJob:
    def __init__(self, operands, out_shape, phases, scratch):
        self.operands, self.out_shape, self.phases, self.scratch = operands, out_shape, phases, scratch


def _wait(out, n_blocks, send_sem, recv_sem, send=True, recv=True):
    span = out.at[pl.ds(0, n_blocks)]
    desc = pltpu.make_async_remote_copy(src_ref=span, dst_ref=span, send_sem=send_sem, recv_sem=recv_sem,
                                        device_id=_coords(_my_index()), device_id_type=MESH)
    if recv:
        desc.wait_recv()
    if send:
        desc.wait_send()


def gather_job(shards):
    n = len(shards)

    def places():
        x, y, c = lax.axis_index("x"), lax.axis_index("y"), lax.axis_index("c")
        return (x, y, c), (x, y, 1 - c), [(1 - x, y), (x, 1 - y), (1 - x, 1 - y)]

    def index(p):
        return 4 * p[0] + 2 * p[1] + p[2]

    def start(ins, outs, sems):
        far_send, far_recv, near_send, near_recv, local = sems
        me, sibling, chips = places()
        for k in range(n):
            pltpu.make_async_copy(ins[k], outs[k].at[index(me)], local.at[k]).start()
            for chip in chips:
                pltpu.make_async_remote_copy(src_ref=ins[k], dst_ref=outs[k].at[index(me)], send_sem=far_send.at[k],
                                             recv_sem=far_recv.at[k], device_id=(*chip, me[2]), device_id_type=MESH).start()
            pltpu.make_async_remote_copy(src_ref=ins[k], dst_ref=outs[k].at[index(me)], send_sem=near_send.at[k],
                                         recv_sem=near_recv.at[k], device_id=sibling, device_id_type=MESH).start()

    def relay(ins, outs, sems):
        far_send, far_recv, near_send, near_recv, local = sems
        me, sibling, chips = places()
        for k in range(n):
            _wait(outs[k], 3, far_send.at[k], far_recv.at[k], send=False)
            for chip in chips:
                block = outs[k].at[index((*chip, me[2]))]
                pltpu.make_async_remote_copy(src_ref=block, dst_ref=block, send_sem=near_send.at[k],
                                             recv_sem=near_recv.at[k], device_id=sibling, device_id_type=MESH).start()

    def finish(ins, outs, sems):
        far_send, far_recv, near_send, near_recv, local = sems
        for k in range(n):
            _wait(outs[k], 4, near_send.at[k], near_recv.at[k])
            _wait(outs[k], 3, far_send.at[k], far_recv.at[k], recv=False)
            pltpu.make_async_copy(ins[k], outs[k].at[0], local.at[k]).wait()

    shapes = [jax.ShapeDtypeStruct((N_DEV,) + tuple(a.shape), a.dtype) for a in shards]
    return CommJob(list(shards), shapes, [start, relay, finish], [pltpu.SemaphoreType.DMA((n,))] * 5)


def scatter_job(tensors):
    n = len(tensors)
    flat, where = [], {}
    for k, pieces in enumerate(tensors):
        d = 0
        for piece in pieces:
            for b in range(piece.shape[0]):
                where[k, d] = (len(flat), b)
                d += 1
            flat.append(piece)
        assert d == N_DEV

    def start(ins, outs, sems):
        send_sems, recv_sems, local_sems = sems
        me = _my_index()

        def block(k, d):
            i, b = where[k, d]
            return ins[i].at[b]

        for d in range(N_DEV):
            @pl.when(d != me)
            def _():
                for k in range(n):
                    pltpu.make_async_remote_copy(src_ref=block(k, d), dst_ref=outs[k].at[me], send_sem=send_sems.at[k],
                                                 recv_sem=recv_sems.at[k], device_id=(d // 4, (d // 2) % 2, d % 2),
                                                 device_id_type=MESH).start()

            @pl.when(d == me)
            def _():
                for k in range(n):
                    pltpu.make_async_copy(block(k, d), outs[k].at[d], local_sems.at[k]).start()

    def finish(ins, outs, sems):
        send_sems, recv_sems, local_sems = sems
        for k in range(n):
            _wait(outs[k], N_DEV - 1, send_sems.at[k], recv_sems.at[k])
            i, b = where[k, 0]
            pltpu.make_async_copy(ins[i].at[b], outs[k].at[0], local_sems.at[k]).wait()

    shapes = [jax.ShapeDtypeStruct((N_DEV,) + tuple(p[0].shape[1:]), p[0].dtype) for p in tensors]
    return CommJob(flat, shapes, [start, finish], [pltpu.SemaphoreType.DMA((n,))] * 3)


def merge_jobs(a, b):
    def on(job, off):
        oi, oo, os_ = off
        ni, no, ns = len(job.operands), len(job.out_shape), len(job.scratch)
        return lambda phase: (lambda ins, outs, sems: phase(ins[oi:oi + ni], outs[oo:oo + no], sems[os_:os_ + ns]))

    wrap_a = on(a, (0, 0, 0))
    wrap_b = on(b, (len(a.operands), len(a.out_shape), len(a.scratch)))
    pa, pb = [wrap_a(p) for p in a.phases], [wrap_b(p) for p in b.phases]

    def together(*phases):
        def run(ins, outs, sems):
            for p in phases:
                p(ins, outs, sems)
        return run

    middle = pa[1:-1] + pb[1:-1]
    phases = [together(pa[0], pb[0])] + middle + [together(pa[-1], pb[-1])]
    return CommJob(a.operands + b.operands, a.out_shape + b.out_shape, phases, a.scratch + b.scratch)


def comm_call(job, name):
    ni, no = len(job.operands), len(job.out_shape)

    def body(*refs):
        ins, outs, sems = refs[:ni], refs[ni:ni + no], refs[ni + no:]
        for phase in job.phases:
            phase(ins, outs, sems)

    return pl.pallas_call(body, name=name, in_specs=[ANY] * ni, out_specs=[ANY] * no, out_shape=job.out_shape,
                          scratch_shapes=job.scratch)(*job.operands)


def _carry(job, body, n_in, n_out, at_step):
    ji, jo, js = len(job.operands), len(job.out_shape), len(job.scratch)

    def carrier(*refs):
        a, b = n_in, n_in + ji
        c, d = b + n_out, b + n_out + jo
        e = len(refs) - js
        job_refs = (refs[a:b], refs[c:d], refs[e:])
        n = len(job.phases)

        @pl.when(at_step(0, n))
        def _():
            job.phases[0](*job_refs)

        body(*refs[:a], *refs[b:c], *refs[d:e])

        for i in range(1, n):
            @pl.when(at_step(i, n))
            def _():
                job.phases[i](*job_refs)

    return carrier


def _pad_lanes(v, lo, total=128):
    return jnp.pad(v, (lo, total - lo - v.shape[0]))[None, :]


MIXER_WEIGHTS = ("w_in", "w_q_up", "w_kv_up", "conv_w")
LATE_WEIGHTS = ("w_out", "w_gate_up", "w_down")


def mixer_operands(g, sw):
    w_in = g["w_in"].reshape(D_IN, D_MODEL)
    zero = lambda rows: jnp.zeros((rows, D_MODEL), w_in.dtype)
    w_proj = jnp.concatenate(
        [w_in[:384], w_in[416:928], w_in[928:1952], w_in[1952:1960], zero(56), w_in[384:416], zero(32)], axis=0)
    wq = jnp.pad(g["w_q_up"], ((0, 0), (0, 0), (0, HEAD_LANES - NOPE - ROPE)))
    wk = jnp.pad(g["w_kv_up"][:, :, :NOPE], ((0, 0), (0, 0), (0, HEAD_LANES - NOPE)))
    wv = g["w_kv_up"][:, :, NOPE:]
    qkv = (sw["q_a_norm_w"][None, :], sw["kv_a_norm_w"][None, :], wq, wk, wv,
           _pad_lanes(jnp.concatenate([sw["q_nope_norm_w"], sw["q_pe_norm_w"]]), 0),
           _pad_lanes(sw["k_nope_norm_w"], 0), _pad_lanes(sw["k_pe_norm_w"], NOPE))
    conv_w = g["conv_w"].astype(f32).transpose(1, 0, 2).reshape(4, D_CONV)
    ssd = (conv_w, sw["conv_b"][None, :], _pad_lanes(sw["dt_bias"], 0), _pad_lanes(sw["a_log"], 0),
           _pad_lanes(sw["d_skip"], 0), sw["ssd_norm_w"][None, :])
    return dict(w_proj=w_proj, qkv=qkv, ssd=ssd, n1=sw["norm1_w"][None, :])


def late_operands(g, sw):
    return dict(wo=g["w_out"].reshape(D_MODEL, D_MODEL), wgu=g["w_gate_up"],
                wd=g["w_down"].reshape(N_DEV // 2, FF_SHARD, D_MODEL), n2=sw["norm2_w"][None, :])


def layer_fwd(x, mod, kw, cos_t, sin_t, job=None, late=None):
    sh1, sc1, g1, sh2, sc2, g2 = [mod[i:i + 1] for i in range(6)]
    pa, pz, px, plast = proj_fwd(x, kw["n1"], sh1, sc1, kw["w_proj"])
    q, k, v = qkv_fwd(pa, plast, cos_t, sin_t, kw["qkv"])
    (o, qx), carried = attn_fwd(q, k, v, job)
    if late is not None:
        kw = {**kw, **late(carried)}
    yg, states = ssd_fwd(px, pz, plast, kw["ssd"])
    x_mid = out_fwd(x, o, yg, g1, kw["wo"])
    x_out, mix, h_mid, gate, up = mlp_fwd(x_mid, kw["n2"], sh2, sc2, g2, kw["wgu"], kw["wd"])
    saved = dict(x=x, pa=pa, pz=pz, px=px, plast=plast, qx=qx, k=k, v=v, o=o, yg=yg, states=states, x_mid=x_mid,
                 mix=mix, h_mid=h_mid, gate=gate, up=up)
    return x_out, saved, kw, carried


def layer_bwd_head(dy, mod, kw, sv, job=None):
    _, _, g1, sh2, sc2, g2 = [mod[i:i + 1] for i in range(6)]
    (dhparts, dwg, dwu, dwd), carried = mlp_bwd(sv["h_mid"], dy, sv["gate"], sv["up"], g2, kw["wgu"], kw["wd"], job)
    dmid, dn2, dsh2, dsc2, do, dyg, dg1, dg2, dwo = out_bwd(
        dy, dhparts, sv["x_mid"], kw["n2"], sh2, sc2, sv["mix"], sv["o"], sv["yg"], g1, kw["wo"])
    early = dict(w_out=[dwo.reshape(N_DEV, D_MODEL // N_DEV, D_MODEL)], w_gate_up=[dwg, dwu],
                 w_down=[dwd.reshape(N_DEV, D_FF // N_DEV, D_MODEL)])
    head = dict(dmid=dmid, do=do, dyg=dyg, dn2=dn2, dsh2=dsh2, dsc2=dsc2, dg2=dg2, dg1=dg1)
    return head, early, carried


def layer_bwd_tail(hd, mod, kw, cos_t, sin_t, sv, job=None):
    sh1, sc1 = mod[0:1], mod[1:2]
    (dq, dk, dv), carried = attn_bwd(sv["qx"], sv["k"], sv["v"], hd["do"], job)
    dpx, dpz, dpl_dt, dcw, dcb, ddtb, dalog, ddskip, dsnw = ssd_bwd(sv["px"], sv["pz"], sv["plast"], sv["states"],
                                                                   hd["dyg"], kw["ssd"])
    dpa, dpl_k, dqaw, dkvaw, dwq, dwk, dwv, dqnw, dknw, dkpw = qkv_bwd(sv["pa"], sv["plast"], cos_t, sin_t, kw["qkv"],
                                                                       dq, dk, dv)
    dx, dn1, dsh1, dsc1, dwp = proj_bwd(sv["x"], kw["n1"], sh1, sc1, kw["w_proj"], dpa, dpz, dpx, dpl_k, dpl_dt, hd["dmid"])
    dmod = jnp.concatenate([dsh1, dsc1, hd["dg1"], hd["dsh2"], hd["dsc2"], hd["dg2"]], axis=0)
    dw_in = jnp.concatenate([dwp[:384], dwp[1984:2016], dwp[384:1920], dwp[1920:1928]], axis=0)
    grads = dict(
        norm1_w=dn1[0], norm2_w=hd["dn2"][0], q_a_norm_w=dqaw[0], kv_a_norm_w=dkvaw[0],
        q_nope_norm_w=dqnw[0, :NOPE], q_pe_norm_w=dqnw[0, NOPE:NOPE + ROPE], k_nope_norm_w=dknw[0, :NOPE],
        k_pe_norm_w=dkpw[0, NOPE:NOPE + ROPE], conv_b=dcb[0], dt_bias=ddtb[0, :N_HEADS], a_log=dalog[0, :N_HEADS],
        d_skip=ddskip[0, :N_HEADS], ssd_norm_w=dsnw[0],
        w_in=[dw_in.reshape(N_DEV, D_IN // N_DEV, D_MODEL)],
        w_q_up=[dwq[:, :, :NOPE + ROPE].astype(bf16)],
        w_kv_up=[jnp.concatenate([dwk[:, :, :NOPE], dwv], axis=2).astype(bf16)],
        conv_w=[dcw.reshape(4, N_DEV, D_CONV // N_DEV).transpose(1, 0, 2).astype(bf16)],
    )
    return dx, dmod, grads, carried


def _pack_small(get, last=None):
    flat = jnp.concatenate([get(name).reshape(-1) for name, _ in SMALL])
    flat = jnp.pad(flat, (0, SMALL_ROWS * 128 - flat.shape[0]))
    if last is not None:
        flat = flat.at[-1].set(last)
    return flat.reshape(SMALL_ROWS, 128)


def _unpack_small(packed):
    flat = packed.reshape(-1)
    out, off = {}, 0
    for name, size in SMALL:
        out[name] = flat[off:off + 2 * size].reshape(2, size)
        off += 2 * size
    return out


def kernel(x, c, positions, norm1_w, norm2_w, w_ada, b_ada, w_in, q_a_norm_w, w_q_up, kv_a_norm_w, w_kv_up, q_nope_norm_w, q_pe_norm_w, k_nope_norm_w, k_pe_norm_w, conv_w, conv_b, dt_bias, a_log, d_skip, ssd_norm_w, w_out, w_gate_up, w_down, loss_target, m_norm1_w, m_norm2_w, m_w_ada, m_b_ada, m_w_in, m_q_a_norm_w, m_w_q_up, m_kv_a_norm_w, m_w_kv_up, m_q_nope_norm_w, m_q_pe_norm_w, m_k_nope_norm_w, m_k_pe_norm_w, m_conv_w, m_conv_b, m_dt_bias, m_a_log, m_d_skip, m_ssd_norm_w, m_w_out, m_w_gate_up, m_w_down, v_norm1_w, v_norm2_w, v_w_ada, v_b_ada, v_w_in, v_q_a_norm_w, v_w_q_up, v_kv_a_norm_w, v_w_kv_up, v_q_nope_norm_w, v_q_pe_norm_w, v_k_nope_norm_w, v_k_pe_norm_w, v_conv_w, v_conv_b, v_dt_bias, v_a_log, v_d_skip, v_ssd_norm_w, v_w_out, v_w_gate_up, v_w_down):
    w = dict(norm1_w=norm1_w, norm2_w=norm2_w, w_ada=w_ada, b_ada=b_ada, w_in=w_in, q_a_norm_w=q_a_norm_w, w_q_up=w_q_up,
             kv_a_norm_w=kv_a_norm_w, w_kv_up=w_kv_up, q_nope_norm_w=q_nope_norm_w, q_pe_norm_w=q_pe_norm_w,
             k_nope_norm_w=k_nope_norm_w, k_pe_norm_w=k_pe_norm_w, conv_w=conv_w, conv_b=conv_b, dt_bias=dt_bias,
             a_log=a_log, d_skip=d_skip, ssd_norm_w=ssd_norm_w, w_out=w_out, w_gate_up=w_gate_up, w_down=w_down)
    m = dict(norm1_w=m_norm1_w, norm2_w=m_norm2_w, w_ada=m_w_ada, b_ada=m_b_ada, w_in=m_w_in, q_a_norm_w=m_q_a_norm_w,
             w_q_up=m_w_q_up, kv_a_norm_w=m_kv_a_norm_w, w_kv_up=m_w_kv_up, q_nope_norm_w=m_q_nope_norm_w,
             q_pe_norm_w=m_q_pe_norm_w, k_nope_norm_w=m_k_nope_norm_w, k_pe_norm_w=m_k_pe_norm_w, conv_w=m_conv_w,
             conv_b=m_conv_b, dt_bias=m_dt_bias, a_log=m_a_log, d_skip=m_d_skip, ssd_norm_w=m_ssd_norm_w, w_out=m_w_out,
             w_gate_up=m_w_gate_up, w_down=m_w_down)
    v = dict(norm1_w=v_norm1_w, norm2_w=v_norm2_w, w_ada=v_w_ada, b_ada=v_b_ada, w_in=v_w_in, q_a_norm_w=v_q_a_norm_w,
             w_q_up=v_w_q_up, kv_a_norm_w=v_kv_a_norm_w, w_kv_up=v_w_kv_up, q_nope_norm_w=v_q_nope_norm_w,
             q_pe_norm_w=v_q_pe_norm_w, k_nope_norm_w=v_k_nope_norm_w, k_pe_norm_w=v_k_pe_norm_w, conv_w=v_conv_w,
             conv_b=v_conv_b, dt_bias=v_dt_bias, a_log=v_a_log, d_skip=v_d_skip, ssd_norm_w=v_ssd_norm_w, w_out=v_w_out,
             w_gate_up=v_w_gate_up, w_down=v_w_down)
    me = _my_index()
    seq = x.shape[1]

    def shard(name, l):
        if name == "conv_w":
            return w[name][l]
        if name in TRANSPOSED:
            return jnp.swapaxes(w[name][l], 0, 1).astype(bf16)
        return w[name][l].astype(bf16)

    def shards(names, l):
        return [shard(name, l) for name in names]

    small = [{name: w[name][l] for name, _ in SMALL if name != "b_ada"} for l in range(2)]
    n_late = len(LATE_WEIGHTS)

    first = comm_call(gather_job([c] + shards(MIXER_WEIGHTS, 0)), "gather_first")
    c_all = first[0].reshape(N_DEV, D_MODEL)
    kws = [mixer_operands(dict(zip(MIXER_WEIGHTS, first[1:])), small[0]), None]

    b_cols = lax.dynamic_slice_in_dim(b_ada, me * 768, 768, axis=1)
    mod_cols = ada_fwd(c_all, w_ada, b_cols)
    (mod_all,) = comm_call(gather_job([mod_cols]), "gather_mod")
    mod_me = lax.dynamic_index_in_dim(mod_all, me, axis=2, keepdims=False)
    mods = [mod_me[:, l, :].reshape(6, D_MODEL) for l in range(2)]

    inv_freq = 1.0 / (ROPE_THETA ** (jnp.arange(0, ROPE, 2, dtype=f32) / ROPE))
    inv = _pad_lanes(jnp.concatenate([inv_freq, inv_freq]), NOPE)
    cos_t, sin_t = rope_tables(positions.reshape(seq, 1), inv)

    saved = [None, None]
    h, saved[0], kws[0], got = layer_fwd(
        x[0], mods[0], kws[0], cos_t, sin_t, gather_job(shards(LATE_WEIGHTS, 0) + shards(MIXER_WEIGHTS, 1)),
        lambda got: late_operands(dict(zip(LATE_WEIGHTS, got[:n_late])), small[0]))
    kws[1] = mixer_operands(dict(zip(MIXER_WEIGHTS, got[n_late:])), small[1])
    h, saved[1], kws[1], _ = layer_fwd(
        h, mods[1], kws[1], cos_t, sin_t, gather_job(shards(LATE_WEIGHTS, 1)),
        lambda got: late_operands(dict(zip(LATE_WEIGHTS, got)), small[1]))
    dy, loss_part = loss_fwd(h, loss_target[0])

    early, late = ("w_out", "w_gate_up", "w_down"), ("w_in", "w_q_up", "w_kv_up", "conv_w")
    parts = [{}, {}]
    head, pieces, _ = layer_bwd_head(dy, mods[1], kws[1], saved[1])
    dy, dmod1, grads1, got = layer_bwd_tail(head, mods[1], kws[1], cos_t, sin_t, saved[1], scatter_job([pieces[n] for n in early]))
    parts[1].update(zip(early, got))
    head, pieces, got = layer_bwd_head(dy, mods[0], kws[0], saved[0], scatter_job([grads1[n] for n in late]))
    parts[1].update(zip(late, got))
    dy, dmod0, grads0, got = layer_bwd_tail(head, mods[0], kws[0], cos_t, sin_t, saved[0], scatter_job([pieces[n] for n in early]))
    parts[0].update(zip(early, got))
    grad_x = dy[None]

    small_part = {name: jnp.stack([grads0[name], grads1[name]]) for name, _ in SMALL if name != "b_ada"}
    small_part["b_ada"] = jnp.stack([dmod0.reshape(-1), dmod1.reshape(-1)])
    last = comm_call(merge_jobs(scatter_job([grads0[n] for n in late]),
                                gather_job([_pack_small(lambda n: small_part[n], loss_part[0, 0])])), "exchange_last")
    parts[0].update(zip(late, last[:len(late)]))
    small_all = last[len(late)]
    packed = adamw(small_all, _pack_small(lambda n: w[n])[None], _pack_small(lambda n: m[n])[None],
                   _pack_small(lambda n: v[n])[None], 0, None, "adamw_small")
    loss = packed[0][0, -1, -1]
    res = {}
    for key, arr in zip("gdmv", packed):
        for name, val in _unpack_small(arr[0]).items():
            res[key, name] = val

    off = 2 * (1024 + 1024)
    dmod_all = small_all.reshape(N_DEV, -1)[:, off:off + 2 * 6144].reshape(N_DEV, 2, 6144)
    dmod_cols = lax.dynamic_slice_in_dim(dmod_all, me * 768, 768, axis=2).transpose(1, 0, 2)
    g_ada = ada_bwd(c_all, dmod_cols)
    out = None
    for l in range(2):
        out = adamw(g_ada[l][None], w_ada, m_w_ada, v_w_ada, l, out, "adamw_w_ada")
    res.update(zip([(key, "w_ada") for key in "gdmv"], out))

    for name in BIG:
        view = (lambda a: jnp.swapaxes(a, 1, 2)) if name in TRANSPOSED else (lambda a: a)
        out = None
        for l in range(2):
            out = adamw(parts[l][name], view(w[name]), view(m[name]), view(v[name]), l, out, "adamw_" + name)
        res.update(zip([(key, name) for key in "gdmv"], [view(a) for a in out]))

    return (loss, grad_x, *[res["g", n] for n in WEIGHTS], *[res["d", n] for n in WEIGHTS],
            *[res["m", n] for n in WEIGHTS], *[res["v", n] for n in WEIGHTS])
```

```python
import functools

import jax
import jax.numpy as jnp
from jax import lax
from jax.experimental import pallas as pl
from jax.experimental.pallas import tpu as pltpu

f32 = jnp.float32
bf16 = jnp.bfloat16

N_DEV = 8
D_MODEL = 1024
N_HEADS = 8
HEAD_LANES = 128
NOPE = 64
ROPE = 32
V_DIM = 64
Q_RANK = 256
KV_RANK = 128
D_SSD = 512
D_CONV = 1024
SSD_STATE = 128
SSD_HEAD_DIM = 64
CHUNK = 128
HALO = 8
D_FF = 2816
FF_SHARD = 704
D_IN = 1960
D_PROJ = 2048
EPS = 1e-6
LOG2E = 1.4426950408889634
LN2 = 0.6931471805599453
Q_SCALE = (NOPE + ROPE) ** -0.5 * LOG2E
SPARE_Q = NOPE + ROPE
SPARE_V = V_DIM
ATTN_ROWS_FWD = 256
ATTN_HEADS_FWD = 8
ATTN_HEADS_BWD = 4
MLP_FWD_ROWS = 1024
MLP_BWD_CHUNK = 256
MLP_BWD_ROWS = 1024
ROPE_THETA = 10000.0
NEG = -1e30

ADAM_LR = 0.001
ADAM_B1 = 0.9
ADAM_B2 = 0.999
ADAM_EPS = 1e-08
ADAM_WD = 0.01
ADAM_STEP = 10
ADAMW_BLOCK_BYTES = 36 << 20

MESH = pl.DeviceIdType.MESH
ANY = pl.BlockSpec(memory_space=pl.ANY)

SMALL = (("norm1_w", 1024), ("norm2_w", 1024), ("b_ada", 6144), ("q_a_norm_w", 256), ("kv_a_norm_w", 128),
         ("q_nope_norm_w", 64), ("q_pe_norm_w", 32), ("k_nope_norm_w", 64), ("k_pe_norm_w", 32),
         ("conv_b", 1024), ("dt_bias", 8), ("a_log", 8), ("d_skip", 8), ("ssd_norm_w", 512))
SMALL_ROWS = 168
BIG = ("w_in", "w_q_up", "w_kv_up", "conv_w", "w_out", "w_gate_up", "w_down")
TRANSPOSED = ("w_in", "w_gate_up")
WEIGHTS = ("norm1_w", "norm2_w", "w_ada", "b_ada", "w_in", "q_a_norm_w", "w_q_up", "kv_a_norm_w", "w_kv_up",
           "q_nope_norm_w", "q_pe_norm_w", "k_nope_norm_w", "k_pe_norm_w", "conv_w", "conv_b", "dt_bias",
           "a_log", "d_skip", "ssd_norm_w", "w_out", "w_gate_up", "w_down")


def _dot(a, b, ca, cb):
    return lax.dot_general(a.astype(bf16), b.astype(bf16), (((ca,), (cb,)), ((), ())), preferred_element_type=f32)


@jax.custom_vjp
def mm(a, b):
    return _dot(a, b, 1, 0)


def _mm_fwd(a, b):
    return _dot(a, b, 1, 0), (a, b)


def _mm_bwd(res, g):
    a, b = res
    return _dot(g, b, 1, 1).astype(a.dtype), _dot(a, g, 0, 0).astype(b.dtype)


mm.defvjp(_mm_fwd, _mm_bwd)


@jax.custom_vjp
def _mm_slot(a, w, slot):
    return _dot(a, w, 1, 0)


def _mm_slot_fwd(a, w, slot):
    return _dot(a, w, 1, 0), (a, w)


def _mm_slot_bwd(res, g):
    a, w = res
    return _dot(g, w, 1, 1).astype(a.dtype), None, _dot(a, g, 0, 0)


_mm_slot.defvjp(_mm_slot_fwd, _mm_slot_bwd)


def mmw(a, w, slot=None):
    return _dot(a, w, 1, 0) if slot is None else _mm_slot(a, w, slot)


@jax.custom_vjp
def _mm_slot_t(a, wt, slot):
    return _dot(a, wt, 1, 1)


def _mm_slot_t_fwd(a, wt, slot):
    return _dot(a, wt, 1, 1), (a, wt)


def _mm_slot_t_bwd(res, g):
    a, wt = res
    return _dot(g, wt, 1, 0).astype(a.dtype), None, _dot(g, a, 0, 0)


_mm_slot_t.defvjp(_mm_slot_t_fwd, _mm_slot_t_bwd)


def mmw_t(a, wt, slot=None):
    return _dot(a, wt, 1, 1) if slot is None else _mm_slot_t(a, wt, slot)


@jax.custom_vjp
def mm_nt(a, b):
    return _dot(a, b, 1, 1)


def _mm_nt_fwd(a, b):
    return _dot(a, b, 1, 1), (a, b)


def _mm_nt_bwd(res, g):
    a, b = res
    return _dot(g, b, 1, 0).astype(a.dtype), _dot(g, a, 0, 0).astype(b.dtype)


mm_nt.defvjp(_mm_nt_fwd, _mm_nt_bwd)


@jax.custom_vjp
def mm_tn(a, b):
    return _dot(a, b, 0, 0)


def _mm_tn_fwd(a, b):
    return _dot(a, b, 0, 0), (a, b)


def _mm_tn_bwd(res, g):
    a, b = res
    return _dot(b, g, 1, 1).astype(a.dtype), _dot(a, g, 1, 0).astype(b.dtype)


mm_tn.defvjp(_mm_tn_fwd, _mm_tn_bwd)


def _rms(x, w):
    return x * lax.rsqrt(jnp.mean(x * x, axis=-1, keepdims=True) + EPS) * w


def _const(shape):
    n = len(shape)
    return pl.BlockSpec(shape, lambda *_: (0,) * n)


def _accumulate(first, refs, vals):
    @pl.when(first)
    def _():
        for r, v in zip(refs, vals):
            r[...] = v

    @pl.when(jnp.logical_not(first))
    def _():
        for r, v in zip(refs, vals):
            r[...] += v


def _accumulate_then_cast(first, last, accs, outs, vals):
    _accumulate(first, accs, vals)

    @pl.when(last)
    def _():
        for a, o in zip(accs, outs):
            o[...] = a[...].astype(o.dtype)


def _token_block(s):
    return min(512, s)


def _f_proj(x, nw, sh, sc, w, slot=None):
    h = _rms(x, nw) * (1.0 + sc) + sh
    return mmw_t(h, w, slot)


def _f_qkv(pa, plast, cos_t, sin_t, qaw, kvaw, wq, wk, wv, qnw, knw, kpw, slots=None):
    sq, sk, sv = slots if slots is not None else ([None] * N_HEADS,) * 3
    lane = lax.broadcasted_iota(jnp.int32, (1, HEAD_LANES), 1)
    m_nope = lane < NOPE
    m_pe = (lane >= NOPE) & (lane < NOPE + ROPE)
    rows = pa.shape[0]

    def rope(t):
        half = ROPE // 2
        swapped = jnp.concatenate(
            [jnp.zeros((rows, NOPE), f32), t[:, NOPE + half:NOPE + ROPE], t[:, NOPE:NOPE + half],
             jnp.zeros((rows, HEAD_LANES - NOPE - ROPE), f32)], axis=1)
        return t * cos_t + swapped * sin_t

    qa = _rms(pa[:, :Q_RANK], qaw)
    kva = _rms(pa[:, Q_RANK:Q_RANK + KV_RANK], kvaw)
    kp = jnp.where(m_pe, plast, 0.0)
    kp = kp * lax.rsqrt(jnp.sum(kp * kp, axis=-1, keepdims=True) / ROPE + EPS) * kpw
    k_rot = rope(kp)
    qs, ks, vs = [], [], []
    for h in range(N_HEADS):
        qh = mmw(qa, wq[h], sq[h])
        ss_n = jnp.sum(jnp.where(m_nope, qh * qh, 0.0), axis=-1, keepdims=True) / NOPE
        ss_p = jnp.sum(jnp.where(m_pe, qh * qh, 0.0), axis=-1, keepdims=True) / ROPE
        r = jnp.where(m_nope, lax.rsqrt(ss_n + EPS), lax.rsqrt(ss_p + EPS))
        qs.append(rope(qh * r * qnw) * Q_SCALE)
        kh = mmw(kva, wk[h], sk[h])
        kh = kh * lax.rsqrt(jnp.sum(kh * kh, axis=-1, keepdims=True) / NOPE + EPS) * knw
        ks.append(kh + k_rot)
        vs.append(mmw(kva, wv[h], sv[h]))
    return jnp.stack(qs), jnp.stack(ks), jnp.stack(vs)


def _f_ssd(xext, z, plast, prev, cw, cb, dtb, alog, dskip, snw):
    n = CHUNK
    conv = cb
    for k in range(4):
        conv = conv + cw[k:k + 1] * xext[HALO - 3 + k:HALO - 3 + k + n]
    xc = jax.nn.silu(conv)
    xs, bm, cm = xc[:, :D_SSD], xc[:, D_SSD:D_SSD + 2 * SSD_STATE], xc[:, D_SSD + 2 * SSD_STATE:]
    lane = lax.broadcasted_iota(jnp.int32, (1, 128), 1)
    dt = jax.nn.softplus(jnp.where(lane < N_HEADS, plast, 0.0) + dtb)
    adt = dt * (-jnp.exp(alog))
    row = lax.broadcasted_iota(jnp.int32, (n, n), 0)
    col = lax.broadcasted_iota(jnp.int32, (n, n), 1)
    tri = row >= col
    acs = jnp.dot(tri.astype(f32), adt, precision=lax.Precision.HIGHEST, preferred_element_type=f32)
    acs_t = acs.T
    bgs = [bm[:, g * SSD_STATE:(g + 1) * SSD_STATE] for g in range(2)]
    cgs = [cm[:, g * SSD_STATE:(g + 1) * SSD_STATE] for g in range(2)]
    cb_ts = [mm_nt(cgs[g], bgs[g]) for g in range(2)]
    low = lane < SSD_HEAD_DIM
    low_rows = lax.broadcasted_iota(jnp.int32, (2 * SSD_HEAD_DIM, 1), 0) < SSD_HEAD_DIM

    def both(a0, a1):
        return jnp.where(low, a0, a1)

    pre = []
    for i in range(N_HEADS // 2):
        h0, h1 = 2 * i, 2 * i + 1
        col0, col1 = acs[:, h0:h0 + 1], acs[:, h1:h1 + 1]
        last0, last1 = acs[n - 1:n, h0:h0 + 1], acs[n - 1:n, h1:h1 + 1]
        cb_t = cb_ts[i // 2]
        scores0 = cb_t * jnp.exp(jnp.where(tri, col0 - acs_t[h0:h0 + 1, :], -jnp.inf))
        scores1 = cb_t * jnp.exp(jnp.where(tri, col1 - acs_t[h1:h1 + 1, :], -jnp.inf))
        xp = xs[:, i * 128:(i + 1) * 128]
        xdt = xp * both(dt[:, h0:h0 + 1], dt[:, h1:h1 + 1])
        weighted = xdt * both(jnp.exp(last0 - col0), jnp.exp(last1 - col1))
        chunk_decay = jnp.where(low_rows, jnp.exp(last0), jnp.exp(last1))
        in_decay = both(jnp.exp(col0), jnp.exp(col1))
        skip = both(dskip[:, h0:h0 + 1], dskip[:, h1:h1 + 1]) * xp
        pre.append((scores0, scores1, xdt, weighted, chunk_decay, in_decay, skip))
    prods = []
    for i in range(N_HEADS // 2):
        scores0, scores1, xdt, weighted, _, _, _ = pre[i]
        g = i // 2
        y_diag = mm(scores0, jnp.where(low, xdt, 0.0)) + mm(scores1, jnp.where(low, 0.0, xdt))
        prods.append((y_diag, mm_tn(weighted, bgs[g]), mm_nt(cgs[g], prev[i])))
    ys, news = [], []
    for i in range(N_HEADS // 2):
        y_diag, st, y_off = prods[i]
        _, _, _, _, chunk_decay, in_decay, skip = pre[i]
        news.append(chunk_decay * prev[i] + st)
        ys.append(y_diag + y_off * in_decay + skip)
    y = jnp.concatenate(ys, axis=1)
    yg = y * jax.nn.silu(z)
    half = D_SSD // 2
    outs = []
    for g in range(2):
        t = yg[:, g * half:(g + 1) * half]
        outs.append(t * lax.rsqrt(jnp.mean(t * t, axis=-1, keepdims=True) + EPS))
    return jnp.concatenate(outs, axis=1) * snw, jnp.stack(news)


def _f_out(o, yg, g1, wo, slot=None):
    cat = jnp.concatenate([o[h] for h in range(N_HEADS)] + [yg], axis=1)
    return g1 * mmw(cat, wo, slot)


def _f_modulate(x, nw, sh, sc):
    return _rms(x, nw) * (1.0 + sc) + sh


def proj_fwd(x, nw, sh, sc, w):
    s = x.shape[0]
    ts = _token_block(s)

    def body(x_ref, nw_ref, sh_ref, sc_ref, w_ref, pa_ref, pz_ref, px_ref, pl_ref):
        p = _f_proj(x_ref[...], nw_ref[...], sh_ref[...], sc_ref[...], w_ref[...])
        pa_ref[...] = p[:, :384]
        pz_ref[...] = p[:, 384:896]
        px_ref[...] = p[:, 896:1920]
        pl_ref[...] = p[:, 1920:]

    vec = _const((1, D_MODEL))
    return pl.pallas_call(
        body, name="proj_fwd", grid=(s // ts,),
        in_specs=[pl.BlockSpec((ts, D_MODEL), lambda i: (i, 0)), vec, vec, vec, _const((D_PROJ, D_MODEL))],
        out_specs=[pl.BlockSpec((ts, 384), lambda i: (i, 0)), pl.BlockSpec((ts, 512), lambda i: (i, 0)),
                   pl.BlockSpec((ts, 1024), lambda i: (i, 0)), pl.BlockSpec((ts, 128), lambda i: (i, 0))],
        out_shape=[jax.ShapeDtypeStruct((s, 384), f32), jax.ShapeDtypeStruct((s, 512), f32),
                   jax.ShapeDtypeStruct((s, 1024), f32), jax.ShapeDtypeStruct((s, 128), f32)],
    )(x, nw, sh, sc, w)


def rope_tables(pos, inv):
    s = pos.shape[0]
    ts = _token_block(s)

    def body(pos_ref, inv_ref, cos_ref, sin_ref):
        ang = pos_ref[...].astype(f32) * inv_ref[...]
        lane = lax.broadcasted_iota(jnp.int32, (1, HEAD_LANES), 1)
        half = ROPE // 2
        cos_ref[...] = jnp.where(lane < NOPE, 1.0, jnp.where(lane < NOPE + ROPE, jnp.cos(ang), 0.0))
        sn = jnp.sin(ang)
        sin_ref[...] = jnp.where((lane >= NOPE) & (lane < NOPE + half), -sn,
                                 jnp.where((lane >= NOPE + half) & (lane < NOPE + ROPE), sn, 0.0))

    return pl.pallas_call(
        body, name="rope_tables", grid=(s // ts,),
        in_specs=[pl.BlockSpec((ts, 1), lambda i: (i, 0)), _const((1, HEAD_LANES))],
        out_specs=[pl.BlockSpec((ts, HEAD_LANES), lambda i: (i, 0))] * 2,
        out_shape=[jax.ShapeDtypeStruct((s, HEAD_LANES), f32)] * 2,
    )(pos, inv)


def _qkv_param_specs():
    return [_const((1, Q_RANK)), _const((1, KV_RANK)), _const((N_HEADS, Q_RANK, HEAD_LANES)),
            _const((N_HEADS, KV_RANK, HEAD_LANES)), _const((N_HEADS, KV_RANK, V_DIM)),
            _const((1, HEAD_LANES)), _const((1, HEAD_LANES)), _const((1, HEAD_LANES))]


def qkv_fwd(pa, plast, cos_t, sin_t, params):
    s = pa.shape[0]
    ts = _token_block(s)

    def body(pa_ref, pl_ref, cos_ref, sin_ref, *rest):
        prm = [r[...] for r in rest[:8]]
        q_ref, k_ref, v_ref = rest[8:]
        q, k, v = _f_qkv(pa_ref[...], pl_ref[...], cos_ref[...], sin_ref[...], *prm)
        q_ref[...] = q.astype(bf16)
        lane = lax.broadcasted_iota(jnp.int32, (1, 1, HEAD_LANES), 2)
        k_ref[...] = jnp.where((lane == SPARE_Q) | (lane == SPARE_Q + 1), 1.0, k).astype(bf16)
        v_ref[...] = jnp.concatenate([v, jnp.ones_like(v)], axis=-1).astype(bf16)

    tok = lambda w: pl.BlockSpec((ts, w), lambda i: (i, 0))
    head = pl.BlockSpec((N_HEADS, ts, HEAD_LANES), lambda i: (0, i, 0))
    return pl.pallas_call(
        body, name="qkv_fwd", grid=(s // ts,),
        in_specs=[tok(384), tok(128), tok(128), tok(128)] + _qkv_param_specs(),
        out_specs=[head] * 3, out_shape=[jax.ShapeDtypeStruct((N_HEADS, s, HEAD_LANES), bf16)] * 3,
    )(pa, plast, cos_t, sin_t, *params)


def _scores(q, k):
    return lax.dot_general(q, k, (((1,), (1,)), ((), ())), preferred_element_type=f32)


def _tril(rows, cols, row_offset):
    row = row_offset + lax.broadcasted_iota(jnp.int32, (rows, cols), 0)
    col = lax.broadcasted_iota(jnp.int32, (rows, cols), 1)
    return row >= col


def _call_with_job(body, name, grid, job, in_specs, out_specs, out_shape, scratch_shapes, operands, relay_at=None):
    if job is None:
        res = pl.pallas_call(body, name=name, grid=grid, in_specs=in_specs, out_specs=out_specs, out_shape=out_shape,
                             scratch_shapes=scratch_shapes)(*operands)
        return res, None

    def at_step(i, n):
        if i == 0:
            want = [0] * len(grid)
        elif i == n - 1:
            want = [g - 1 for g in grid]
        else:
            want = relay_at
        return functools.reduce(jnp.logical_and, [pl.program_id(a) == s for a, s in enumerate(want)])

    carrier = _carry(job, body, len(in_specs), len(out_specs), at_step)
    res = pl.pallas_call(
        carrier, name=name, grid=grid,
        in_specs=list(in_specs) + [ANY] * len(job.operands), out_specs=list(out_specs) + [ANY] * len(job.out_shape),
        out_shape=list(out_shape) + list(job.out_shape), scratch_shapes=list(scratch_shapes) + job.scratch,
    )(*operands, *job.operands)
    return res[:len(out_specs)], res[len(out_specs):]


def attn_fwd(q, k, v, job=None):
    s = q.shape[1]
    t = _token_block(s)
    nb = s // t

    rb = min(ATTN_ROWS_FWD, t)

    hp = ATTN_HEADS_FWD

    def body(q_ref, k_ref, v_ref, o_ref, qx_ref, m_sc, acc_sc):
        qi = pl.program_id(1)
        m_sc[...] = jnp.full(m_sc.shape, NEG, f32)
        acc_sc[...] = jnp.zeros(acc_sc.shape, f32)

        def step(k0, diagonal):
            chains = [(hh, r) for hh in range(hp) for r in range(t // rb)]

            def scores(hh, r):
                nk = (r + 1) * rb if diagonal else t
                sc = _scores(q_ref[hh, pl.ds(r * rb, rb), :], k_ref[hh, pl.ds(k0, nk), :])
                return jnp.where(_tril(rb, nk, r * rb), sc, NEG) if diagonal else sc

            ahead = scores(*chains[0])
            for c, (hh, r) in enumerate(chains):
                sc = ahead
                if c + 1 < len(chains):
                    ahead = scores(*chains[c + 1])
                rows = pl.ds(r * rb, rb)
                keys = pl.ds(k0, (r + 1) * rb if diagonal else t)
                m_prev = m_sc[hh, rows, :1]
                m_new = jnp.maximum(m_prev, jnp.max(sc, axis=-1, keepdims=True))
                p = jnp.exp2(sc - m_new)
                alpha = jnp.exp2(m_prev - m_new)
                acc = alpha * acc_sc[hh, rows, :] + jnp.dot(p.astype(bf16), v_ref[hh, keys, :], preferred_element_type=f32)
                if diagonal:
                    l = acc[:, V_DIM:V_DIM + 1]
                    o_ref[hh, rows, :] = acc[:, :V_DIM] / l
                    lse = m_new + jnp.log2(l)
                    high = lse.astype(bf16)
                    low = (lse - high.astype(f32)).astype(bf16)
                    lane = lax.broadcasted_iota(jnp.int32, (1, HEAD_LANES), 1)
                    qx_ref[hh, rows, :] = jnp.where(lane == SPARE_Q, -high,
                                                    jnp.where(lane == SPARE_Q + 1, -low, q_ref[hh, rows, :]))
                else:
                    acc_sc[hh, rows, :] = acc
                    m_sc[hh, rows, :] = jnp.broadcast_to(m_new, (rb, 128))

        def below(ki, carry):
            step(pl.multiple_of(ki * t, t), False)
            return carry

        lax.fori_loop(0, qi, below, 0)
        step(pl.multiple_of(qi * t, t), True)

    return _call_with_job(
        body, "attn_fwd" if job is None else "attn_fwd_comm", (N_HEADS // hp, nb), job,
        in_specs=[pl.BlockSpec((hp, t, HEAD_LANES), lambda h, qi: (h, qi, 0)),
                  pl.BlockSpec((hp, s, HEAD_LANES), lambda h, qi: (h, 0, 0)),
                  pl.BlockSpec((hp, s, HEAD_LANES), lambda h, qi: (h, 0, 0))],
        out_specs=[pl.BlockSpec((hp, t, V_DIM), lambda h, qi: (h, qi, 0)),
                   pl.BlockSpec((hp, t, HEAD_LANES), lambda h, qi: (h, qi, 0))],
        out_shape=[jax.ShapeDtypeStruct((N_HEADS, s, V_DIM), f32), jax.ShapeDtypeStruct((N_HEADS, s, HEAD_LANES), bf16)],
        scratch_shapes=[pltpu.VMEM((hp, t, 128), f32), pltpu.VMEM((hp, t, HEAD_LANES), f32)],
        operands=(q, k, v), relay_at=(N_HEADS // hp - 1, max(nb - 2, 0)))


def _ssd_param_specs():
    return [_const((4, D_CONV)), _const((1, D_CONV)), _const((1, 128)), _const((1, 128)), _const((1, 128)),
            _const((1, D_SSD))]


def ssd_fwd(px, pz, plast, params):
    s = px.shape[0]
    nc = s // CHUNK

    def body(px_ref, pz_ref, pl_ref, cw_ref, cb_ref, dtb_ref, alog_ref, dskip_ref, snw_ref, yg_ref, st_ref,
             state_sc, halo_sc):
        i = pl.program_id(0)

        @pl.when(i == 0)
        def _():
            state_sc[...] = jnp.zeros(state_sc.shape, f32)
            halo_sc[...] = jnp.zeros(halo_sc.shape, f32)

        x = px_ref[...]
        prev = state_sc[...]
        st_ref[...] = prev
        xext = jnp.concatenate([halo_sc[...], x], axis=0)
        yg, new = _f_ssd(xext, pz_ref[...], pl_ref[...], prev, cw_ref[...], cb_ref[...], dtb_ref[...],
                         alog_ref[...], dskip_ref[...], snw_ref[...])
        yg_ref[...] = yg
        state_sc[...] = new
        halo_sc[...] = x[CHUNK - HALO:]

    tok = lambda w: pl.BlockSpec((CHUNK, w), lambda i: (i, 0))
    return pl.pallas_call(
        body, name="ssd_fwd", grid=(nc,),
        in_specs=[tok(D_CONV), tok(D_SSD), tok(128)] + _ssd_param_specs(),
        out_specs=[tok(D_SSD), pl.BlockSpec((None, N_HEADS // 2, 2 * SSD_HEAD_DIM, SSD_STATE), lambda i: (i, 0, 0, 0))],
        out_shape=[jax.ShapeDtypeStruct((s, D_SSD), f32),
                   jax.ShapeDtypeStruct((nc, N_HEADS // 2, 2 * SSD_HEAD_DIM, SSD_STATE), f32)],
        scratch_shapes=[pltpu.VMEM((N_HEADS // 2, 2 * SSD_HEAD_DIM, SSD_STATE), f32), pltpu.VMEM((HALO, D_CONV), f32)],
    )(px, pz, plast, *params)


def out_fwd(x, o, yg, g1, wo):
    s = x.shape[0]
    ts = _token_block(s)

    def body(x_ref, o_ref, yg_ref, g1_ref, wo_ref, out_ref):
        out_ref[...] = x_ref[...] + _f_out(o_ref[...], yg_ref[...], g1_ref[...], wo_ref[...])

    return pl.pallas_call(
        body, name="out_fwd", grid=(s // ts,),
        in_specs=[pl.BlockSpec((ts, D_MODEL), lambda i: (i, 0)), pl.BlockSpec((N_HEADS, ts, V_DIM), lambda i: (0, i, 0)),
                  pl.BlockSpec((ts, D_SSD), lambda i: (i, 0)), _const((1, D_MODEL)), _const((D_MODEL, D_MODEL))],
        out_specs=pl.BlockSpec((ts, D_MODEL), lambda i: (i, 0)),
        out_shape=jax.ShapeDtypeStruct((s, D_MODEL), f32),
    )(x, o, yg, g1, wo)


def mlp_fwd(x, nw, sh, sc, g2, wgu, wd, target=None):
    s = x.shape[0]
    ts = min(MLP_FWD_ROWS, s)
    nj = N_DEV // 2

    def body(x_ref, nw_ref, sh_ref, sc_ref, g2_ref, wg_ref, wu_ref, wd_ref, *rest):
        if target is None:
            out_ref, mix_ref, h_ref, gate_ref, up_ref = rest
        else:
            t_ref, out_ref, mix_ref, h_ref, gate_ref, up_ref, loss_ref = rest
        j = pl.program_id(1)
        first_block = pl.program_id(0) == 0

        @pl.when(j == 0)
        def _():
            h_ref[...] = _f_modulate(x_ref[...], nw_ref[...], sh_ref[...], sc_ref[...]).astype(bf16)
            mix_ref[...] = jnp.zeros(mix_ref.shape, f32)

        nr = max(ts // 512, 1)
        half = ts // nr
        wg, wu, wd = wg_ref[...], wu_ref[...], wd_ref[...]
        products = lambda r: (mmw_t(h_ref[pl.ds(r * half, half), :], wg), mmw_t(h_ref[pl.ds(r * half, half), :], wu))
        ahead = products(0)
        for r in range(nr):
            gate, up = ahead
            if r + 1 < nr:
                ahead = products(r + 1)
            rows = pl.ds(r * half, half)
            gate_ref[rows, :] = gate.astype(bf16)
            up_ref[rows, :] = up.astype(bf16)
            mix_ref[rows, :] += mmw(jax.nn.silu(gate) * up, wd)

        @pl.when(j == nj - 1)
        def _():
            y = x_ref[...] + g2_ref[...] * mix_ref[...]
            if target is None:
                out_ref[...] = y
            else:
                d = y - t_ref[...]
                out_ref[...] = d * (1.0 / D_MODEL)
                part = 0.5 * jnp.sum(jnp.sum(d * d, axis=-1, keepdims=True) * (1.0 / D_MODEL), axis=0, keepdims=True)
                _accumulate(first_block, [loss_ref], [jnp.broadcast_to(part, (8, 128))])

    vec = _const((1, D_MODEL))
    tok = pl.BlockSpec((ts, D_MODEL), lambda i, j: (i, 0))
    wide = pl.BlockSpec((None, ts, FF_SHARD), lambda i, j: (j, i, 0))
    last = target is not None
    return pl.pallas_call(
        body, name="mlp_fwd_loss" if last else "mlp_fwd", grid=(s // ts, nj),
        in_specs=[tok, vec, vec, vec, vec,
                  pl.BlockSpec((None, FF_SHARD, D_MODEL), lambda i, j: (j, 0, 0)),
                  pl.BlockSpec((None, FF_SHARD, D_MODEL), lambda i, j: (j + nj, 0, 0)),
                  pl.BlockSpec((None, FF_SHARD, D_MODEL), lambda i, j: (j, 0, 0))] + [tok] * last,
        out_specs=[tok] * 3 + [wide] * 2 + [_const((8, 128))] * last,
        out_shape=[jax.ShapeDtypeStruct((s, D_MODEL), f32), jax.ShapeDtypeStruct((s, D_MODEL), f32),
                   jax.ShapeDtypeStruct((s, D_MODEL), bf16)] + [jax.ShapeDtypeStruct((nj, s, FF_SHARD), bf16)] * 2
                  + [jax.ShapeDtypeStruct((8, 128), f32)] * last,
    )(x, nw, sh, sc, g2, wgu, wgu, wd, *([target] if last else []))


def mlp_bwd(h, dy, gate, up, g2, wgu, wd, job=None):
    s = h.shape[0]
    ts = min(MLP_BWD_ROWS, s)
    nj = N_DEV // 2
    ni = s // ts

    rows_per = min(MLP_BWD_CHUNK, ts)

    def body(h_ref, dy_ref, gate_ref, up_ref, g2_ref, wg_ref, wu_ref, wd_ref, dh_ref, dwg_ref, dwu_ref, dwd_ref,
             ag_sc, au_sc, ad_sc, act_sc, dgate_sc, dup_sc, dmix_sc):
        i = pl.program_id(1)
        wg, wu, wd = wg_ref[...], wu_ref[...], wd_ref[...]
        g2 = g2_ref[...]
        for r in range(ts // rows_per):
            rows = pl.ds(r * rows_per, rows_per)
            act, vjp = jax.vjp(lambda g, u: jax.nn.silu(g) * u, gate_ref[rows, :].astype(f32), up_ref[rows, :].astype(f32))
            dmix = (dy_ref[rows, :] * g2).astype(bf16)
            dgate, dup = vjp(_dot(dmix, wd, 1, 1))
            dgate, dup = dgate.astype(bf16), dup.astype(bf16)
            dh_ref[rows, :] = (_dot(dgate, wg, 1, 0) + _dot(dup, wu, 1, 0)).astype(bf16)
            act_sc[rows, :] = act.astype(bf16)
            dgate_sc[rows, :] = dgate
            dup_sc[rows, :] = dup
            dmix_sc[rows, :] = dmix
        h = h_ref[...]
        grads = [_dot(dgate_sc[...], h, 0, 0), _dot(dup_sc[...], h, 0, 0), _dot(act_sc[...], dmix_sc[...], 0, 0)]
        _accumulate_then_cast(i == 0, i == ni - 1, [ag_sc, au_sc, ad_sc], [dwg_ref, dwu_ref, dwd_ref], grads)

    once = pl.Buffered(1)
    wspec = lambda off: pl.BlockSpec((None, FF_SHARD, D_MODEL), lambda j, i: (j + off, 0, 0), pipeline_mode=once)
    dspec = pl.BlockSpec((None, FF_SHARD, D_MODEL), lambda j, i: (j, 0, 0), pipeline_mode=once)
    wide = pl.BlockSpec((None, ts, FF_SHARD), lambda j, i: (j, i, 0))
    return _call_with_job(
        body, "mlp_bwd" if job is None else "mlp_bwd_comm", (nj, ni), job,
        in_specs=[pl.BlockSpec((ts, D_MODEL), lambda j, i: (i, 0)), pl.BlockSpec((ts, D_MODEL), lambda j, i: (i, 0)),
                  wide, wide, _const((1, D_MODEL)), wspec(0), wspec(nj), dspec],
        out_specs=[pl.BlockSpec((None, ts, D_MODEL), lambda j, i: (j, i, 0)), wspec(0), wspec(0), dspec],
        out_shape=[jax.ShapeDtypeStruct((nj, s, D_MODEL), bf16),
                   jax.ShapeDtypeStruct((nj, FF_SHARD, D_MODEL), bf16), jax.ShapeDtypeStruct((nj, FF_SHARD, D_MODEL), bf16),
                   jax.ShapeDtypeStruct((nj, FF_SHARD, D_MODEL), bf16)],
        scratch_shapes=[pltpu.VMEM((FF_SHARD, D_MODEL), f32), pltpu.VMEM((FF_SHARD, D_MODEL), f32),
                        pltpu.VMEM((FF_SHARD, D_MODEL), f32), pltpu.VMEM((ts, FF_SHARD), bf16),
                        pltpu.VMEM((ts, FF_SHARD), bf16), pltpu.VMEM((ts, FF_SHARD), bf16), pltpu.VMEM((ts, D_MODEL), bf16)],
        operands=(h, dy, gate, up, g2, wgu, wgu, wd))


def out_bwd(dy, dhparts, x, nw, sh, sc, mix, o, yg, g1, wo):
    s = dy.shape[0]
    ts = _token_block(s)
    nj = dhparts.shape[0]

    ni = s // ts

    def body(dy_ref, dp_ref, x_ref, nw_ref, sh_ref, sc_ref, mix_ref, o_ref, yg_ref, g1_ref, wo_ref,
             dx_ref, dnw_ref, dsh_ref, dsc_ref, do_ref, dyg_ref, dg1_ref, dg2_ref, dwo_ref, acc_sc):
        i = pl.program_id(0)
        g = dy_ref[...]
        _accumulate(i == 0, [dg2_ref], [jnp.sum(g * mix_ref[...], axis=0, keepdims=True)])
        dh = dp_ref[0].astype(f32)
        for j in range(1, nj):
            dh = dh + dp_ref[j].astype(f32)
        _, vjp_mod = jax.vjp(_f_modulate, x_ref[...], nw_ref[...], sh_ref[...], sc_ref[...])
        dx_mod, dnw, dsh, dsc = vjp_mod(dh)
        _accumulate(i == 0, [dnw_ref, dsh_ref, dsc_ref], [dnw, dsh, dsc])
        g = g + dx_mod
        dx_ref[...] = g
        o = o_ref[...]
        wo = wo_ref[...]
        _, vjp = jax.vjp(lambda o_, yg_, g1_, slot: _f_out(o_, yg_, g1_, wo, slot), o, yg_ref[...], g1_ref[...],
                         jnp.zeros(wo.shape, f32))
        do, dyg, dg1, dwo = vjp(g)
        delta = jnp.sum(do * o, axis=-1, keepdims=True)
        high = delta.astype(bf16)
        low = (delta - high.astype(f32)).astype(bf16)
        lane = lax.broadcasted_iota(jnp.int32, (1, 1, HEAD_LANES), 2)
        wide = jnp.concatenate([do.astype(bf16), jnp.zeros(do.shape, bf16)], axis=-1)
        do_ref[...] = jnp.where(lane == SPARE_V, -high, jnp.where(lane == SPARE_V + 1, -low, wide))
        dyg_ref[...] = dyg
        _accumulate(i == 0, [dg1_ref], [dg1])
        _accumulate_then_cast(i == 0, i == ni - 1, [acc_sc], [dwo_ref], [dwo])

    head = pl.BlockSpec((N_HEADS, ts, V_DIM), lambda i: (0, i, 0))
    tok = pl.BlockSpec((ts, D_MODEL), lambda i: (i, 0))
    vec = _const((1, D_MODEL))
    vshape = jax.ShapeDtypeStruct((1, D_MODEL), f32)
    return pl.pallas_call(
        body, name="out_bwd", grid=(ni,), scratch_shapes=[pltpu.VMEM((D_MODEL, D_MODEL), f32)],
        in_specs=[tok, pl.BlockSpec((nj, ts, D_MODEL), lambda i: (0, i, 0)), tok, vec, vec, vec, tok,
                  head, pl.BlockSpec((ts, D_SSD), lambda i: (i, 0)), vec, _const((D_MODEL, D_MODEL))],
        out_specs=[tok, vec, vec, vec, pl.BlockSpec((N_HEADS, ts, HEAD_LANES), lambda i: (0, i, 0)),
                   pl.BlockSpec((ts, D_SSD), lambda i: (i, 0)), vec, vec, _const((D_MODEL, D_MODEL))],
        out_shape=[jax.ShapeDtypeStruct((s, D_MODEL), f32), vshape, vshape, vshape,
                   jax.ShapeDtypeStruct((N_HEADS, s, HEAD_LANES), bf16), jax.ShapeDtypeStruct((s, D_SSD), f32),
                   vshape, vshape, jax.ShapeDtypeStruct((D_MODEL, D_MODEL), bf16)],
    )(dy, dhparts, x, nw, sh, sc, mix, o, yg, g1, wo)


def attn_bwd(qx, k, v, do, job=None):
    s = qx.shape[1]
    t = _token_block(s)
    nb = s // t

    hp = ATTN_HEADS_BWD

    def body(q_ref, k_ref, v_ref, do_ref, dq_ref, dk_ref, dv_ref, dv_sc):
        ki = pl.program_id(1)

        @pl.when(ki == 0)
        def _():
            dq_ref[...] = jnp.zeros(dq_ref.shape, f32)

        dk_ref[...] = jnp.zeros(dk_ref.shape, f32)
        dv_sc[...] = jnp.zeros(dv_sc.shape, f32)

        def step(q0, diagonal):
            rows = pl.ds(q0, t)

            def products(hh):
                sc = _scores(q_ref[hh, rows, :], k_ref[hh])
                dps = _scores(do_ref[hh, rows, :], v_ref[hh])
                return (jnp.where(_tril(t, t, 0), sc, NEG) if diagonal else sc), dps

            ahead = products(0)
            for hh in range(hp):
                sc, dps = ahead
                if hh + 1 < hp:
                    ahead = products(hh + 1)
                p = jnp.exp2(sc)
                ds = (p * dps).astype(bf16)
                dv_sc[hh] += lax.dot_general(p.astype(bf16), do_ref[hh, rows, :], (((0,), (0,)), ((), ())),
                                             preferred_element_type=f32)
                dk_ref[hh] += lax.dot_general(ds, q_ref[hh, rows, :], (((0,), (0,)), ((), ())), preferred_element_type=f32)
                dq_ref[hh, rows, :] += jnp.dot(ds, k_ref[hh], preferred_element_type=f32)

        step(pl.multiple_of(ki * t, t), True)

        def above(qi, carry):
            step(pl.multiple_of(qi * t, t), False)
            return carry

        lax.fori_loop(ki + 1, nb, above, 0)
        real = lax.broadcasted_iota(jnp.int32, (1, 1, HEAD_LANES), 2) < SPARE_Q
        dk_ref[...] = jnp.where(real, dk_ref[...] * LN2, 0.0)
        dv_ref[...] = dv_sc[:, :, :V_DIM]

        @pl.when(ki == nb - 1)
        def _():
            dq_ref[...] = jnp.where(real, dq_ref[...] * LN2, 0.0)

    qspec = pl.BlockSpec((hp, s, HEAD_LANES), lambda h, ki: (h, 0, 0))
    kspec = lambda w: pl.BlockSpec((hp, t, w), lambda h, ki: (h, ki, 0))
    return _call_with_job(
        body, "attn_bwd" if job is None else "attn_bwd_comm", (N_HEADS // hp, nb), job,
        in_specs=[qspec, kspec(HEAD_LANES), kspec(HEAD_LANES), qspec],
        out_specs=[qspec, kspec(HEAD_LANES), kspec(V_DIM)],
        out_shape=[jax.ShapeDtypeStruct((N_HEADS, s, HEAD_LANES), f32), jax.ShapeDtypeStruct((N_HEADS, s, HEAD_LANES), f32),
                   jax.ShapeDtypeStruct((N_HEADS, s, V_DIM), f32)],
        scratch_shapes=[pltpu.VMEM((hp, t, HEAD_LANES), f32)], operands=(qx, k, v, do))


def ssd_bwd(px, pz, plast, states, dyg, params):
    s = px.shape[0]
    nc = s // CHUNK
    per = CHUNK // HALO

    def body(px_ref, halo_ref, pz_ref, pl_ref, st_ref, dyg_ref, cw_ref, cb_ref, dtb_ref, alog_ref, dskip_ref, snw_ref,
             dpx_ref, dpz_ref, dpl_ref, dcw_ref, dcb_ref, ddtb_ref, dalog_ref, ddskip_ref, dsnw_ref, dstate_sc, dhalo_sc):
        t = pl.program_id(0)
        chunk = nc - 1 - t

        @pl.when(t == 0)
        def _():
            dstate_sc[...] = jnp.zeros(dstate_sc.shape, f32)
            dhalo_sc[...] = jnp.zeros(dhalo_sc.shape, f32)

        halo = jnp.where(chunk > 0, halo_ref[...], 0.0)
        xext = jnp.concatenate([halo, px_ref[...]], axis=0)
        _, vjp = jax.vjp(_f_ssd, xext, pz_ref[...], pl_ref[...], st_ref[...], cw_ref[...], cb_ref[...], dtb_ref[...],
                         alog_ref[...], dskip_ref[...], snw_ref[...])
        dxext, dz, dpl, dprev, dcw, dcb, ddtb, dalog, ddskip, dsnw = vjp((dyg_ref[...], dstate_sc[...]))
        dpx_ref[...] = dxext[HALO:]
        dpx_ref[CHUNK - HALO:, :] += dhalo_sc[...]
        dhalo_sc[...] = dxext[:HALO]
        dstate_sc[...] = dprev
        dpz_ref[...] = dz
        dpl_ref[...] = dpl
        _accumulate(t == 0, [dcw_ref, dcb_ref, ddtb_ref, dalog_ref, ddskip_ref, dsnw_ref],
                    [dcw, dcb, ddtb, dalog, ddskip, dsnw])

    rev = lambda w: pl.BlockSpec((CHUNK, w), lambda t: (nc - 1 - t, 0))
    pshapes = [jax.ShapeDtypeStruct((4, D_CONV), f32), jax.ShapeDtypeStruct((1, D_CONV), f32),
               jax.ShapeDtypeStruct((1, 128), f32), jax.ShapeDtypeStruct((1, 128), f32),
               jax.ShapeDtypeStruct((1, 128), f32), jax.ShapeDtypeStruct((1, D_SSD), f32)]
    return pl.pallas_call(
        body, name="ssd_bwd", grid=(nc,),
        in_specs=[rev(D_CONV),
                  pl.BlockSpec((HALO, D_CONV), lambda t: (jnp.maximum((nc - 1 - t) * per - 1, 0), 0)),
                  rev(D_SSD), rev(128),
                  pl.BlockSpec((None, N_HEADS // 2, 2 * SSD_HEAD_DIM, SSD_STATE), lambda t: (nc - 1 - t, 0, 0, 0)),
                  rev(D_SSD)] + _ssd_param_specs(),
        out_specs=[rev(D_CONV), rev(D_SSD), rev(128)] + _ssd_param_specs(),
        out_shape=[jax.ShapeDtypeStruct((s, D_CONV), f32), jax.ShapeDtypeStruct((s, D_SSD), f32),
                   jax.ShapeDtypeStruct((s, 128), f32)] + pshapes,
        scratch_shapes=[pltpu.VMEM((N_HEADS // 2, 2 * SSD_HEAD_DIM, SSD_STATE), f32), pltpu.VMEM((HALO, D_CONV), f32)],
    )(px, px, pz, plast, states, dyg, *params)


def qkv_bwd(pa, plast, cos_t, sin_t, params, dq, dk, dv):
    s = pa.shape[0]
    ts = _token_block(s)

    def body(pa_ref, pl_ref, cos_ref, sin_ref, *rest):
        qaw, kvaw, wq, wk, wv, qnw, knw, kpw = [r[...] for r in rest[:8]]
        dq_ref, dk_ref, dv_ref = rest[8:11]
        dpa_ref, dpl_ref = rest[11:13]
        dprm_refs = list(rest[13:])
        cos_t, sin_t = cos_ref[...], sin_ref[...]

        def stage(pa_, pl_, qaw_, kvaw_, sq, sk, sv, qnw_, knw_, kpw_):
            return _f_qkv(pa_, pl_, cos_t, sin_t, qaw_, kvaw_, wq, wk, wv, qnw_, knw_, kpw_, (sq, sk, sv))

        _, vjp = jax.vjp(stage, pa_ref[...], pl_ref[...], qaw, kvaw, jnp.zeros(wq.shape, f32), jnp.zeros(wk.shape, f32),
                         jnp.zeros(wv.shape, f32), qnw, knw, kpw)
        grads = vjp((dq_ref[...], dk_ref[...], dv_ref[...]))
        dpa_ref[...] = grads[0]
        dpl_ref[...] = grads[1]
        _accumulate(pl.program_id(0) == 0, dprm_refs, list(grads[2:]))

    tok = lambda w: pl.BlockSpec((ts, w), lambda i: (i, 0))
    head = lambda w: pl.BlockSpec((N_HEADS, ts, w), lambda i: (0, i, 0))
    pshapes = [jax.ShapeDtypeStruct((1, Q_RANK), f32), jax.ShapeDtypeStruct((1, KV_RANK), f32),
               jax.ShapeDtypeStruct((N_HEADS, Q_RANK, HEAD_LANES), f32), jax.ShapeDtypeStruct((N_HEADS, KV_RANK, HEAD_LANES), f32),
               jax.ShapeDtypeStruct((N_HEADS, KV_RANK, V_DIM), f32), jax.ShapeDtypeStruct((1, HEAD_LANES), f32),
               jax.ShapeDtypeStruct((1, HEAD_LANES), f32), jax.ShapeDtypeStruct((1, HEAD_LANES), f32)]
    return pl.pallas_call(
        body, name="qkv_bwd", grid=(s // ts,),
        in_specs=[tok(384), tok(128), tok(128), tok(128)] + _qkv_param_specs()
                 + [head(HEAD_LANES), head(HEAD_LANES), head(V_DIM)],
        out_specs=[tok(384), tok(128)] + _qkv_param_specs(),
        out_shape=[jax.ShapeDtypeStruct((s, 384), f32), jax.ShapeDtypeStruct((s, 128), f32)] + pshapes,
    )(pa, plast, cos_t, sin_t, *params, dq, dk, dv)


def proj_bwd(x, nw, sh, sc, w, dpa, dpz, dpx, dpl_k, dpl_dt, dres):
    s = x.shape[0]
    ts = _token_block(s)

    ni = s // ts

    def body(x_ref, nw_ref, sh_ref, sc_ref, w_ref, dpa_ref, dpz_ref, dpx_ref, dplk_ref, dpld_ref, dres_ref,
             dx_ref, dnw_ref, dsh_ref, dsc_ref, dw_ref, acc_sc):
        i = pl.program_id(0)
        g = jnp.concatenate([dpa_ref[...], dpz_ref[...], dpx_ref[...], dplk_ref[...] + dpld_ref[...]], axis=1)
        w = w_ref[...]
        _, vjp = jax.vjp(lambda x_, nw_, sh_, sc_, slot: _f_proj(x_, nw_, sh_, sc_, w, slot), x_ref[...], nw_ref[...],
                         sh_ref[...], sc_ref[...], jnp.zeros(w.shape, f32))
        dx, dnw, dsh, dsc, dw = vjp(g)
        dx_ref[...] = dx + dres_ref[...]
        _accumulate(i == 0, [dnw_ref, dsh_ref, dsc_ref], [dnw, dsh, dsc])
        _accumulate_then_cast(i == 0, i == ni - 1, [acc_sc], [dw_ref], [dw])

    vec = _const((1, D_MODEL))
    vshape = jax.ShapeDtypeStruct((1, D_MODEL), f32)
    tok = lambda w_: pl.BlockSpec((ts, w_), lambda i: (i, 0))
    return pl.pallas_call(
        body, name="proj_bwd", grid=(ni,), scratch_shapes=[pltpu.VMEM((D_PROJ, D_MODEL), f32)],
        in_specs=[tok(D_MODEL), vec, vec, vec, _const((D_PROJ, D_MODEL)), tok(384), tok(512), tok(1024), tok(128), tok(128),
                  tok(D_MODEL)],
        out_specs=[tok(D_MODEL), vec, vec, vec, _const((D_PROJ, D_MODEL))],
        out_shape=[jax.ShapeDtypeStruct((s, D_MODEL), f32), vshape, vshape, vshape,
                   jax.ShapeDtypeStruct((D_PROJ, D_MODEL), bf16)],
    )(x, nw, sh, sc, w, dpa, dpz, dpx, dpl_k, dpl_dt, dres)


def ada_fwd(c_all, w_ada, b_cols):
    def body(c_ref, w_ref, b_ref, out_ref):
        act = jax.nn.silu(c_ref[...])
        for l in range(2):
            out_ref[l] = jnp.dot(act, w_ref[l], precision=lax.Precision.HIGHEST, preferred_element_type=f32) + b_ref[l]

    return pl.pallas_call(body, name="ada_fwd", out_shape=jax.ShapeDtypeStruct((2, N_DEV, 768), f32))(c_all, w_ada, b_cols)


def ada_bwd(c_all, dmod_cols):
    def body(c_ref, d_ref, out_ref):
        out_ref[0] = lax.dot_general(jax.nn.silu(c_ref[...]), d_ref[0], (((0,), (0,)), ((), ())),
                                     precision=lax.Precision.HIGHEST, preferred_element_type=f32)

    return pl.pallas_call(
        body, name="ada_bwd", grid=(2,),
        in_specs=[_const((N_DEV, D_MODEL)), pl.BlockSpec((1, N_DEV, 768), lambda l: (l, 0, 0))],
        out_specs=pl.BlockSpec((1, D_MODEL, 768), lambda l: (l, 0, 0)),
        out_shape=jax.ShapeDtypeStruct((2, D_MODEL, 768), f32),
    )(c_all, dmod_cols)


def _adamw(w, g, m, v):
    m = ADAM_B1 * m + (1.0 - ADAM_B1) * g
    v = ADAM_B2 * v + (1.0 - ADAM_B2) * (g * g)
    m_hat = m / (1.0 - ADAM_B1 ** ADAM_STEP)
    v_hat = v / (1.0 - ADAM_B2 ** ADAM_STEP)
    delta = -ADAM_LR * (m_hat / (jnp.sqrt(v_hat) + ADAM_EPS) + ADAM_WD * w)
    return delta, m, v


def adamw(parts, w, m, v, layer, prev, name):
    n, r, c = parts.shape
    nl = w.shape[0]
    per_elem = 2 * (n * parts.dtype.itemsize + 7 * 4)
    lanes = -(-c // 128) * 128
    tr, tc = r, c
    if per_elem * r * lanes > ADAMW_BLOCK_BYTES:
        fits = [t for t in range(r // 2, 15, -1) if r % t == 0 and t % 16 == 0 and per_elem * t * lanes <= ADAMW_BLOCK_BYTES]
        if fits:
            tr = fits[0]
        else:
            tc = next(t for t in (512, 256, 128) if c % t == 0)

    def body(p_ref, w_ref, m_ref, v_ref, *rest):
        g_ref, d_ref, nm_ref, nv_ref = rest[-4:]
        g = p_ref[0].astype(f32)
        for k in range(1, n):
            g = g + p_ref[k].astype(f32)
        delta, nm, nv = _adamw(w_ref[...], g, m_ref[...], v_ref[...])
        g_ref[...] = g
        d_ref[...] = delta
        nm_ref[...] = nm
        nv_ref[...] = nv

    blk = pl.BlockSpec((None, tr, tc), lambda i, j: (layer, i, j))
    shp = jax.ShapeDtypeStruct((nl, r, c), f32)
    kept = [] if prev is None else list(prev)
    return pl.pallas_call(
        body, name=name, grid=(r // tr, c // tc),
        in_specs=[pl.BlockSpec((n, tr, tc), lambda i, j: (0, i, j)), blk, blk, blk] + [ANY] * len(kept),
        out_specs=[blk] * 4, out_shape=[shp] * 4,
        input_output_aliases={4 + j: j for j in range(len(kept))},
    )(parts, w, m, v, *kept)


def _my_index():
    return 4 * lax.axis_index("x") + 2 * lax.axis_index("y") + lax.axis_index("c")


def _coords(idx):
    return (idx // 4, (idx // 2) % 2, idx % 2)


class CommJob:
    def __init__(self, operands, out_shape, phases, scratch):
        self.operands, self.out_shape, self.phases, self.scratch = operands, out_shape, phases, scratch


def _wait(out, n_blocks, send_sem, recv_sem, send=True, recv=True):
    span = out.at[pl.ds(0, n_blocks)]
    desc = pltpu.make_async_remote_copy(src_ref=span, dst_ref=span, send_sem=send_sem, recv_sem=recv_sem,
                                        device_id=_coords(_my_index()), device_id_type=MESH)
    if recv:
        desc.wait_recv()
    if send:
        desc.wait_send()


def gather_job(shards):
    n = len(shards)

    def places():
        x, y, c = lax.axis_index("x"), lax.axis_index("y"), lax.axis_index("c")
        return (x, y, c), (x, y, 1 - c), [(1 - x, y), (x, 1 - y), (1 - x, 1 - y)]

    def index(p):
        return 4 * p[0] + 2 * p[1] + p[2]

    def start(ins, outs, sems):
        far_send, far_recv, near_send, near_recv, local = sems
        me, sibling, chips = places()
        for k in range(n):
            pltpu.make_async_copy(ins[k], outs[k].at[index(me)], local.at[k]).start()
            for chip in chips:
                pltpu.make_async_remote_copy(src_ref=ins[k], dst_ref=outs[k].at[index(me)], send_sem=far_send.at[k],
                                             recv_sem=far_recv.at[k], device_id=(*chip, me[2]), device_id_type=MESH).start()
            pltpu.make_async_remote_copy(src_ref=ins[k], dst_ref=outs[k].at[index(me)], send_sem=near_send.at[k],
                                         recv_sem=near_recv.at[k], device_id=sibling, device_id_type=MESH).start()

    def relay(ins, outs, sems):
        far_send, far_recv, near_send, near_recv, local = sems
        me, sibling, chips = places()
        for k in range(n):
            _wait(outs[k], 3, far_send.at[k], far_recv.at[k], send=False)
            for chip in chips:
                block = outs[k].at[index((*chip, me[2]))]
                pltpu.make_async_remote_copy(src_ref=block, dst_ref=block, send_sem=near_send.at[k],
                                             recv_sem=near_recv.at[k], device_id=sibling, device_id_type=MESH).start()

    def finish(ins, outs, sems):
        far_send, far_recv, near_send, near_recv, local = sems
        for k in range(n):
            _wait(outs[k], 4, near_send.at[k], near_recv.at[k])
            _wait(outs[k], 3, far_send.at[k], far_recv.at[k], recv=False)
            pltpu.make_async_copy(ins[k], outs[k].at[0], local.at[k]).wait()

    shapes = [jax.ShapeDtypeStruct((N_DEV,) + tuple(a.shape), a.dtype) for a in shards]
    return CommJob(list(shards), shapes, [start, relay, finish], [pltpu.SemaphoreType.DMA((n,))] * 5)


def scatter_job(tensors):
    n = len(tensors)
    flat, where = [], {}
    for k, pieces in enumerate(tensors):
        d = 0
        for piece in pieces:
            for b in range(piece.shape[0]):
                where[k, d] = (len(flat), b)
                d += 1
            flat.append(piece)
        assert d == N_DEV

    def start(ins, outs, sems):
        send_sems, recv_sems, local_sems = sems
        me = _my_index()

        def block(k, d):
            i, b = where[k, d]
            return ins[i].at[b]

        for d in range(N_DEV):
            @pl.when(d != me)
            def _():
                for k in range(n):
                    pltpu.make_async_remote_copy(src_ref=block(k, d), dst_ref=outs[k].at[me], send_sem=send_sems.at[k],
                                                 recv_sem=recv_sems.at[k], device_id=(d // 4, (d // 2) % 2, d % 2),
                                                 device_id_type=MESH).start()

            @pl.when(d == me)
            def _():
                for k in range(n):
                    pltpu.make_async_copy(block(k, d), outs[k].at[d], local_sems.at[k]).start()

    def finish(ins, outs, sems):
        send_sems, recv_sems, local_sems = sems
        for k in range(n):
            _wait(outs[k], N_DEV - 1, send_sems.at[k], recv_sems.at[k])
            i, b = where[k, 0]
            pltpu.make_async_copy(ins[i].at[b], outs[k].at[0], local_sems.at[k]).wait()

    shapes = [jax.ShapeDtypeStruct((N_DEV,) + tuple(p[0].shape[1:]), p[0].dtype) for p in tensors]
    return CommJob(flat, shapes, [start, finish], [pltpu.SemaphoreType.DMA((n,))] * 3)


def merge_jobs(a, b):
    def on(job, off):
        oi, oo, os_ = off
        ni, no, ns = len(job.operands), len(job.out_shape), len(job.scratch)
        return lambda phase: (lambda ins, outs, sems: phase(ins[oi:oi + ni], outs[oo:oo + no], sems[os_:os_ + ns]))

    wrap_a = on(a, (0, 0, 0))
    wrap_b = on(b, (len(a.operands), len(a.out_shape), len(a.scratch)))
    pa, pb = [wrap_a(p) for p in a.phases], [wrap_b(p) for p in b.phases]

    def together(*phases):
        def run(ins, outs, sems):
            for p in phases:
                p(ins, outs, sems)
        return run

    middle = pa[1:-1] + pb[1:-1]
    phases = [together(pa[0], pb[0])] + middle + [together(pa[-1], pb[-1])]
    return CommJob(a.operands + b.operands, a.out_shape + b.out_shape, phases, a.scratch + b.scratch)


def comm_call(job, name):
    ni, no = len(job.operands), len(job.out_shape)

    def body(*refs):
        ins, outs, sems = refs[:ni], refs[ni:ni + no], refs[ni + no:]
        for phase in job.phases:
            phase(ins, outs, sems)

    return pl.pallas_call(body, name=name, in_specs=[ANY] * ni, out_specs=[ANY] * no, out_shape=job.out_shape,
                          scratch_shapes=job.scratch)(*job.operands)


def _carry(job, body, n_in, n_out, at_step):
    ji, jo, js = len(job.operands), len(job.out_shape), len(job.scratch)

    def carrier(*refs):
        a, b = n_in, n_in + ji
        c, d = b + n_out, b + n_out + jo
        e = len(refs) - js
        job_refs = (refs[a:b], refs[c:d], refs[e:])
        n = len(job.phases)

        @pl.when(at_step(0, n))
        def _():
            job.phases[0](*job_refs)

        body(*refs[:a], *refs[b:c], *refs[d:e])

        for i in range(1, n):
            @pl.when(at_step(i, n))
            def _():
                job.phases[i](*job_refs)

    return carrier


def _pad_lanes(v, lo, total=128):
    return jnp.pad(v, (lo, total - lo - v.shape[0]))[None, :]


MIXER_WEIGHTS = ("w_in", "w_q_up", "w_kv_up", "conv_w")
LATE_WEIGHTS = ("w_out", "w_gate_up", "w_down")


def mixer_operands(g, sw):
    w_in = g["w_in"].reshape(D_IN, D_MODEL)
    zero = lambda rows: jnp.zeros((rows, D_MODEL), w_in.dtype)
    w_proj = jnp.concatenate(
        [w_in[:384], w_in[416:928], w_in[928:1952], w_in[1952:1960], zero(56), w_in[384:416], zero(32)], axis=0)
    wq = jnp.pad(g["w_q_up"], ((0, 0), (0, 0), (0, HEAD_LANES - NOPE - ROPE)))
    wk = jnp.pad(g["w_kv_up"][:, :, :NOPE], ((0, 0), (0, 0), (0, HEAD_LANES - NOPE)))
    wv = g["w_kv_up"][:, :, NOPE:]
    qkv = (sw["q_a_norm_w"][None, :], sw["kv_a_norm_w"][None, :], wq, wk, wv,
           _pad_lanes(jnp.concatenate([sw["q_nope_norm_w"], sw["q_pe_norm_w"]]), 0),
           _pad_lanes(sw["k_nope_norm_w"], 0), _pad_lanes(sw["k_pe_norm_w"], NOPE))
    conv_w = g["conv_w"].astype(f32).transpose(1, 0, 2).reshape(4, D_CONV)
    ssd = (conv_w, sw["conv_b"][None, :], _pad_lanes(sw["dt_bias"], 0), _pad_lanes(sw["a_log"], 0),
           _pad_lanes(sw["d_skip"], 0), sw["ssd_norm_w"][None, :])
    return dict(w_proj=w_proj, qkv=qkv, ssd=ssd, n1=sw["norm1_w"][None, :])


def late_operands(g, sw):
    return dict(wo=g["w_out"].reshape(D_MODEL, D_MODEL), wgu=g["w_gate_up"],
                wd=g["w_down"].reshape(N_DEV // 2, FF_SHARD, D_MODEL), n2=sw["norm2_w"][None, :])


def layer_fwd(x, mod, kw, cos_t, sin_t, job=None, late=None, target=None):
    sh1, sc1, g1, sh2, sc2, g2 = [mod[i:i + 1] for i in range(6)]
    pa, pz, px, plast = proj_fwd(x, kw["n1"], sh1, sc1, kw["w_proj"])
    q, k, v = qkv_fwd(pa, plast, cos_t, sin_t, kw["qkv"])
    (o, qx), carried = attn_fwd(q, k, v, job)
    if late is not None:
        kw = {**kw, **late(carried)}
    yg, states = ssd_fwd(px, pz, plast, kw["ssd"])
    x_mid = out_fwd(x, o, yg, g1, kw["wo"])
    x_out, mix, h_mid, gate, up, *loss_part = mlp_fwd(x_mid, kw["n2"], sh2, sc2, g2, kw["wgu"], kw["wd"], target)
    saved = dict(x=x, pa=pa, pz=pz, px=px, plast=plast, qx=qx, k=k, v=v, o=o, yg=yg, states=states, x_mid=x_mid,
                 mix=mix, h_mid=h_mid, gate=gate, up=up)
    return (x_out if target is None else (x_out, loss_part[0])), saved, kw, carried


def layer_bwd_head(dy, mod, kw, sv, job=None):
    _, _, g1, sh2, sc2, g2 = [mod[i:i + 1] for i in range(6)]
    (dhparts, dwg, dwu, dwd), carried = mlp_bwd(sv["h_mid"], dy, sv["gate"], sv["up"], g2, kw["wgu"], kw["wd"], job)
    dmid, dn2, dsh2, dsc2, do, dyg, dg1, dg2, dwo = out_bwd(
        dy, dhparts, sv["x_mid"], kw["n2"], sh2, sc2, sv["mix"], sv["o"], sv["yg"], g1, kw["wo"])
    early = dict(w_out=[dwo.reshape(N_DEV, D_MODEL // N_DEV, D_MODEL)], w_gate_up=[dwg, dwu],
                 w_down=[dwd.reshape(N_DEV, D_FF // N_DEV, D_MODEL)])
    head = dict(dmid=dmid, do=do, dyg=dyg, dn2=dn2, dsh2=dsh2, dsc2=dsc2, dg2=dg2, dg1=dg1)
    return head, early, carried


def layer_bwd_tail(hd, mod, kw, cos_t, sin_t, sv, job=None):
    sh1, sc1 = mod[0:1], mod[1:2]
    (dq, dk, dv), carried = attn_bwd(sv["qx"], sv["k"], sv["v"], hd["do"], job)
    dpx, dpz, dpl_dt, dcw, dcb, ddtb, dalog, ddskip, dsnw = ssd_bwd(sv["px"], sv["pz"], sv["plast"], sv["states"],
                                                                   hd["dyg"], kw["ssd"])
    dpa, dpl_k, dqaw, dkvaw, dwq, dwk, dwv, dqnw, dknw, dkpw = qkv_bwd(sv["pa"], sv["plast"], cos_t, sin_t, kw["qkv"],
                                                                       dq, dk, dv)
    dx, dn1, dsh1, dsc1, dwp = proj_bwd(sv["x"], kw["n1"], sh1, sc1, kw["w_proj"], dpa, dpz, dpx, dpl_k, dpl_dt, hd["dmid"])
    dmod = jnp.concatenate([dsh1, dsc1, hd["dg1"], hd["dsh2"], hd["dsc2"], hd["dg2"]], axis=0)
    dw_in = jnp.concatenate([dwp[:384], dwp[1984:2016], dwp[384:1920], dwp[1920:1928]], axis=0)
    grads = dict(
        norm1_w=dn1[0], norm2_w=hd["dn2"][0], q_a_norm_w=dqaw[0], kv_a_norm_w=dkvaw[0],
        q_nope_norm_w=dqnw[0, :NOPE], q_pe_norm_w=dqnw[0, NOPE:NOPE + ROPE], k_nope_norm_w=dknw[0, :NOPE],
        k_pe_norm_w=dkpw[0, NOPE:NOPE + ROPE], conv_b=dcb[0], dt_bias=ddtb[0, :N_HEADS], a_log=dalog[0, :N_HEADS],
        d_skip=ddskip[0, :N_HEADS], ssd_norm_w=dsnw[0],
        w_in=[dw_in.reshape(N_DEV, D_IN // N_DEV, D_MODEL)],
        w_q_up=[dwq[:, :, :NOPE + ROPE].astype(bf16)],
        w_kv_up=[jnp.concatenate([dwk[:, :, :NOPE], dwv], axis=2).astype(bf16)],
        conv_w=[dcw.reshape(4, N_DEV, D_CONV // N_DEV).transpose(1, 0, 2).astype(bf16)],
    )
    return dx, dmod, grads, carried


def _pack_small(get, last=None):
    flat = jnp.concatenate([get(name).reshape(-1) for name, _ in SMALL])
    flat = jnp.pad(flat, (0, SMALL_ROWS * 128 - flat.shape[0]))
    if last is not None:
        flat = flat.at[-1].set(last)
    return flat.reshape(SMALL_ROWS, 128)


def _unpack_small(packed):
    flat = packed.reshape(-1)
    out, off = {}, 0
    for name, size in SMALL:
        out[name] = flat[off:off + 2 * size].reshape(2, size)
        off += 2 * size
    return out


def kernel(x, c, positions, norm1_w, norm2_w, w_ada, b_ada, w_in, q_a_norm_w, w_q_up, kv_a_norm_w, w_kv_up, q_nope_norm_w, q_pe_norm_w, k_nope_norm_w, k_pe_norm_w, conv_w, conv_b, dt_bias, a_log, d_skip, ssd_norm_w, w_out, w_gate_up, w_down, loss_target, m_norm1_w, m_norm2_w, m_w_ada, m_b_ada, m_w_in, m_q_a_norm_w, m_w_q_up, m_kv_a_norm_w, m_w_kv_up, m_q_nope_norm_w, m_q_pe_norm_w, m_k_nope_norm_w, m_k_pe_norm_w, m_conv_w, m_conv_b, m_dt_bias, m_a_log, m_d_skip, m_ssd_norm_w, m_w_out, m_w_gate_up, m_w_down, v_norm1_w, v_norm2_w, v_w_ada, v_b_ada, v_w_in, v_q_a_norm_w, v_w_q_up, v_kv_a_norm_w, v_w_kv_up, v_q_nope_norm_w, v_q_pe_norm_w, v_k_nope_norm_w, v_k_pe_norm_w, v_conv_w, v_conv_b, v_dt_bias, v_a_log, v_d_skip, v_ssd_norm_w, v_w_out, v_w_gate_up, v_w_down):
    w = dict(norm1_w=norm1_w, norm2_w=norm2_w, w_ada=w_ada, b_ada=b_ada, w_in=w_in, q_a_norm_w=q_a_norm_w, w_q_up=w_q_up,
             kv_a_norm_w=kv_a_norm_w, w_kv_up=w_kv_up, q_nope_norm_w=q_nope_norm_w, q_pe_norm_w=q_pe_norm_w,
             k_nope_norm_w=k_nope_norm_w, k_pe_norm_w=k_pe_norm_w, conv_w=conv_w, conv_b=conv_b, dt_bias=dt_bias,
             a_log=a_log, d_skip=d_skip, ssd_norm_w=ssd_norm_w, w_out=w_out, w_gate_up=w_gate_up, w_down=w_down)
    m = dict(norm1_w=m_norm1_w, norm2_w=m_norm2_w, w_ada=m_w_ada, b_ada=m_b_ada, w_in=m_w_in, q_a_norm_w=m_q_a_norm_w,
             w_q_up=m_w_q_up, kv_a_norm_w=m_kv_a_norm_w, w_kv_up=m_w_kv_up, q_nope_norm_w=m_q_nope_norm_w,
             q_pe_norm_w=m_q_pe_norm_w, k_nope_norm_w=m_k_nope_norm_w, k_pe_norm_w=m_k_pe_norm_w, conv_w=m_conv_w,
             conv_b=m_conv_b, dt_bias=m_dt_bias, a_log=m_a_log, d_skip=m_d_skip, ssd_norm_w=m_ssd_norm_w, w_out=m_w_out,
             w_gate_up=m_w_gate_up, w_down=m_w_down)
    v = dict(norm1_w=v_norm1_w, norm2_w=v_norm2_w, w_ada=v_w_ada, b_ada=v_b_ada, w_in=v_w_in, q_a_norm_w=v_q_a_norm_w,
             w_q_up=v_w_q_up, kv_a_norm_w=v_kv_a_norm_w, w_kv_up=v_w_kv_up, q_nope_norm_w=v_q_nope_norm_w,
             q_pe_norm_w=v_q_pe_norm_w, k_nope_norm_w=v_k_nope_norm_w, k_pe_norm_w=v_k_pe_norm_w, conv_w=v_conv_w,
             conv_b=v_conv_b, dt_bias=v_dt_bias, a_log=v_a_log, d_skip=v_d_skip, ssd_norm_w=v_ssd_norm_w, w_out=v_w_out,
             w_gate_up=v_w_gate_up, w_down=v_w_down)
    me = _my_index()
    seq = x.shape[1]

    def shard(name, l):
        if name == "conv_w":
            return w[name][l]
        if name in TRANSPOSED:
            return jnp.swapaxes(w[name][l], 0, 1).astype(bf16)
        return w[name][l].astype(bf16)

    def shards(names, l):
        return [shard(name, l) for name in names]

    small = [{name: w[name][l] for name, _ in SMALL if name != "b_ada"} for l in range(2)]
    n_late = len(LATE_WEIGHTS)

    first = comm_call(gather_job([c] + shards(MIXER_WEIGHTS, 0)), "gather_first")
    c_all = first[0].reshape(N_DEV, D_MODEL)
    kws = [mixer_operands(dict(zip(MIXER_WEIGHTS, first[1:])), small[0]), None]

    b_cols = lax.dynamic_slice_in_dim(b_ada, me * 768, 768, axis=1)
    mod_cols = ada_fwd(c_all, w_ada, b_cols)
    (mod_all,) = comm_call(gather_job([mod_cols]), "gather_mod")
    mod_me = lax.dynamic_index_in_dim(mod_all, me, axis=2, keepdims=False)
    mods = [mod_me[:, l, :].reshape(6, D_MODEL) for l in range(2)]

    inv_freq = 1.0 / (ROPE_THETA ** (jnp.arange(0, ROPE, 2, dtype=f32) / ROPE))
    inv = _pad_lanes(jnp.concatenate([inv_freq, inv_freq]), NOPE)
    cos_t, sin_t = rope_tables(positions.reshape(seq, 1), inv)

    saved = [None, None]
    h, saved[0], kws[0], got = layer_fwd(
        x[0], mods[0], kws[0], cos_t, sin_t, gather_job(shards(LATE_WEIGHTS, 0) + shards(MIXER_WEIGHTS, 1)),
        lambda got: late_operands(dict(zip(LATE_WEIGHTS, got[:n_late])), small[0]))
    kws[1] = mixer_operands(dict(zip(MIXER_WEIGHTS, got[n_late:])), small[1])
    (dy, loss_part), saved[1], kws[1], _ = layer_fwd(
        h, mods[1], kws[1], cos_t, sin_t, gather_job(shards(LATE_WEIGHTS, 1)),
        lambda got: late_operands(dict(zip(LATE_WEIGHTS, got)), small[1]), loss_target[0])

    early, late = ("w_out", "w_gate_up", "w_down"), ("w_in", "w_q_up", "w_kv_up", "conv_w")
    parts = [{}, {}]
    head, pieces, _ = layer_bwd_head(dy, mods[1], kws[1], saved[1])
    dy, dmod1, grads1, got = layer_bwd_tail(head, mods[1], kws[1], cos_t, sin_t, saved[1], scatter_job([pieces[n] for n in early]))
    parts[1].update(zip(early, got))
    head, pieces, got = layer_bwd_head(dy, mods[0], kws[0], saved[0], scatter_job([grads1[n] for n in late]))
    parts[1].update(zip(late, got))
    dy, dmod0, grads0, got = layer_bwd_tail(head, mods[0], kws[0], cos_t, sin_t, saved[0], scatter_job([pieces[n] for n in early]))
    parts[0].update(zip(early, got))
    grad_x = dy[None]

    small_part = {name: jnp.stack([grads0[name], grads1[name]]) for name, _ in SMALL if name != "b_ada"}
    small_part["b_ada"] = jnp.stack([dmod0.reshape(-1), dmod1.reshape(-1)])
    last = comm_call(merge_jobs(scatter_job([grads0[n] for n in late]),
                                gather_job([_pack_small(lambda n: small_part[n], loss_part[0, 0])])), "exchange_last")
    parts[0].update(zip(late, last[:len(late)]))
    small_all = last[len(late)]
    packed = adamw(small_all, _pack_small(lambda n: w[n])[None], _pack_small(lambda n: m[n])[None],
                   _pack_small(lambda n: v[n])[None], 0, None, "adamw_small")
    loss = packed[0][0, -1, -1]
    res = {}
    for key, arr in zip("gdmv", packed):
        for name, val in _unpack_small(arr[0]).items():
            res[key, name] = val

    off = 2 * (1024 + 1024)
    dmod_all = small_all.reshape(N_DEV, -1)[:, off:off + 2 * 6144].reshape(N_DEV, 2, 6144)
    dmod_cols = lax.dynamic_slice_in_dim(dmod_all, me * 768, 768, axis=2).transpose(1, 0, 2)
    g_ada = ada_bwd(c_all, dmod_cols)
    out = None
    for l in range(2):
        out = adamw(g_ada[l][None], w_ada, m_w_ada, v_w_ada, l, out, "adamw_w_ada")
    res.update(zip([(key, "w_ada") for key in "gdmv"], out))

    for name in BIG:
        view = (lambda a: jnp.swapaxes(a, 1, 2)) if name in TRANSPOSED else (lambda a: a)
        out = None
        for l in range(2):
            out = adamw(parts[l][name], view(w[name]), view(m[name]), view(v[name]), l, out, "adamw_" + name)
        res.update(zip([(key, name) for key in "gdmv"], [view(a) for a in out]))

    return (loss, grad_x, *[res["g", n] for n in WEIGHTS], *[res["d", n] for n in WEIGHTS],
            *[res["m", n] for n in WEIGHTS], *[res["v", n] for n in WEIGHTS])
```

```python
import functools

import jax
import jax.numpy as jnp
from jax import lax
from jax.experimental import pallas as pl
from jax.experimental.pallas import tpu as pltpu

f32 = jnp.float32
bf16 = jnp.bfloat16

N_DEV = 8
D_MODEL = 1024
N_HEADS = 8
HEAD_LANES = 128
NOPE = 64
ROPE = 32
V_DIM = 64
Q_RANK = 256
KV_RANK = 128
D_SSD = 512
D_CONV = 1024
SSD_STATE = 128
SSD_HEAD_DIM = 64
CHUNK = 128
HALO = 8
D_FF = 2816
FF_SHARD = 704
D_IN = 1960
D_PROJ = 2048
EPS = 1e-6
LOG2E = 1.4426950408889634
LN2 = 0.6931471805599453
Q_SCALE = (NOPE + ROPE) ** -0.5 * LOG2E
SPARE_Q = NOPE + ROPE
SPARE_V = V_DIM
ATTN_ROWS_FWD = 256
ATTN_HEADS_FWD = 8
ATTN_HEADS_BWD = 4
MLP_FWD_ROWS = 1024
MLP_BWD_CHUNK = 256
MLP_BWD_ROWS = 1024
ROPE_THETA = 10000.0
NEG = -1e30

ADAM_LR = 0.001
ADAM_B1 = 0.9
ADAM_B2 = 0.999
ADAM_EPS = 1e-08
ADAM_WD = 0.01
ADAM_STEP = 10
ADAMW_BLOCK_BYTES = 36 << 20

MESH = pl.DeviceIdType.MESH
ANY = pl.BlockSpec(memory_space=pl.ANY)

SMALL = (("norm1_w", 1024), ("norm2_w", 1024), ("b_ada", 6144), ("q_a_norm_w", 256), ("kv_a_norm_w", 128),
         ("q_nope_norm_w", 64), ("q_pe_norm_w", 32), ("k_nope_norm_w", 64), ("k_pe_norm_w", 32),
         ("conv_b", 1024), ("dt_bias", 8), ("a_log", 8), ("d_skip", 8), ("ssd_norm_w", 512))
SMALL_ROWS = 168
BIG = ("w_in", "w_q_up", "w_kv_up", "conv_w", "w_out", "w_gate_up", "w_down")
TRANSPOSED = ("w_in", "w_gate_up")
WEIGHTS = ("norm1_w", "norm2_w", "w_ada", "b_ada", "w_in", "q_a_norm_w", "w_q_up", "kv_a_norm_w", "w_kv_up",
           "q_nope_norm_w", "q_pe_norm_w", "k_nope_norm_w", "k_pe_norm_w", "conv_w", "conv_b", "dt_bias",
           "a_log", "d_skip", "ssd_norm_w", "w_out", "w_gate_up", "w_down")


def _dot(a, b, ca, cb):
    return lax.dot_general(a.astype(bf16), b.astype(bf16), (((ca,), (cb,)), ((), ())), preferred_element_type=f32)


@jax.custom_vjp
def mm(a, b):
    return _dot(a, b, 1, 0)


def _mm_fwd(a, b):
    return _dot(a, b, 1, 0), (a, b)


def _mm_bwd(res, g):
    a, b = res
    return _dot(g, b, 1, 1).astype(a.dtype), _dot(a, g, 0, 0).astype(b.dtype)


mm.defvjp(_mm_fwd, _mm_bwd)


@jax.custom_vjp
def _mm_slot(a, w, slot):
    return _dot(a, w, 1, 0)


def _mm_slot_fwd(a, w, slot):
    return _dot(a, w, 1, 0), (a, w)


def _mm_slot_bwd(res, g):
    a, w = res
    return _dot(g, w, 1, 1).astype(a.dtype), None, _dot(a, g, 0, 0)


_mm_slot.defvjp(_mm_slot_fwd, _mm_slot_bwd)


def mmw(a, w, slot=None):
    return _dot(a, w, 1, 0) if slot is None else _mm_slot(a, w, slot)


@jax.custom_vjp
def _mm_slot_t(a, wt, slot):
    return _dot(a, wt, 1, 1)


def _mm_slot_t_fwd(a, wt, slot):
    return _dot(a, wt, 1, 1), (a, wt)


def _mm_slot_t_bwd(res, g):
    a, wt = res
    return _dot(g, wt, 1, 0).astype(a.dtype), None, _dot(g, a, 0, 0)


_mm_slot_t.defvjp(_mm_slot_t_fwd, _mm_slot_t_bwd)


def mmw_t(a, wt, slot=None):
    return _dot(a, wt, 1, 1) if slot is None else _mm_slot_t(a, wt, slot)


@jax.custom_vjp
def mm_nt(a, b):
    return _dot(a, b, 1, 1)


def _mm_nt_fwd(a, b):
    return _dot(a, b, 1, 1), (a, b)


def _mm_nt_bwd(res, g):
    a, b = res
    return _dot(g, b, 1, 0).astype(a.dtype), _dot(g, a, 0, 0).astype(b.dtype)


mm_nt.defvjp(_mm_nt_fwd, _mm_nt_bwd)


@jax.custom_vjp
def mm_tn(a, b):
    return _dot(a, b, 0, 0)


def _mm_tn_fwd(a, b):
    return _dot(a, b, 0, 0), (a, b)


def _mm_tn_bwd(res, g):
    a, b = res
    return _dot(b, g, 1, 1).astype(a.dtype), _dot(a, g, 1, 0).astype(b.dtype)


mm_tn.defvjp(_mm_tn_fwd, _mm_tn_bwd)


def _rms(x, w):
    return x * lax.rsqrt(jnp.mean(x * x, axis=-1, keepdims=True) + EPS) * w


def _const(shape):
    n = len(shape)
    return pl.BlockSpec(shape, lambda *_: (0,) * n)


def _accumulate(first, refs, vals):
    @pl.when(first)
    def _():
        for r, v in zip(refs, vals):
            r[...] = v

    @pl.when(jnp.logical_not(first))
    def _():
        for r, v in zip(refs, vals):
            r[...] += v


def _accumulate_then_cast(first, last, accs, outs, vals):
    _accumulate(first, accs, vals)

    @pl.when(last)
    def _():
        for a, o in zip(accs, outs):
            o[...] = a[...].astype(o.dtype)


def _token_block(s):
    return min(512, s)


def _f_proj(x, nw, sh, sc, w, slot=None):
    h = _rms(x, nw) * (1.0 + sc) + sh
    return mmw_t(h, w, slot)


def _f_qkv(pa, plast, cos_t, sin_t, qaw, kvaw, wq, wk, wv, qnw, knw, kpw, slots=None):
    sq, sk, sv = slots if slots is not None else ([None] * N_HEADS,) * 3
    lane = lax.broadcasted_iota(jnp.int32, (1, HEAD_LANES), 1)
    m_nope = lane < NOPE
    m_pe = (lane >= NOPE) & (lane < NOPE + ROPE)
    rows = pa.shape[0]

    def rope(t):
        half = ROPE // 2
        swapped = jnp.concatenate(
            [jnp.zeros((rows, NOPE), f32), t[:, NOPE + half:NOPE + ROPE], t[:, NOPE:NOPE + half],
             jnp.zeros((rows, HEAD_LANES - NOPE - ROPE), f32)], axis=1)
        return t * cos_t + swapped * sin_t

    qa = _rms(pa[:, :Q_RANK], qaw)
    kva = _rms(pa[:, Q_RANK:Q_RANK + KV_RANK], kvaw)
    kp = jnp.where(m_pe, plast, 0.0)
    kp = kp * lax.rsqrt(jnp.sum(kp * kp, axis=-1, keepdims=True) / ROPE + EPS) * kpw
    k_rot = rope(kp)
    qs, ks, vs = [], [], []
    for h in range(N_HEADS):
        qh = mmw(qa, wq[h], sq[h])
        ss_n = jnp.sum(jnp.where(m_nope, qh * qh, 0.0), axis=-1, keepdims=True) / NOPE
        ss_p = jnp.sum(jnp.where(m_pe, qh * qh, 0.0), axis=-1, keepdims=True) / ROPE
        r = jnp.where(m_nope, lax.rsqrt(ss_n + EPS), lax.rsqrt(ss_p + EPS))
        qs.append(rope(qh * r * qnw) * Q_SCALE)
        kh = mmw(kva, wk[h], sk[h])
        kh = kh * lax.rsqrt(jnp.sum(kh * kh, axis=-1, keepdims=True) / NOPE + EPS) * knw
        ks.append(kh + k_rot)
        vs.append(mmw(kva, wv[h], sv[h]))
    return jnp.stack(qs), jnp.stack(ks), jnp.stack(vs)


def _f_ssd(xext, z, plast, prev, cw, cb, dtb, alog, dskip, snw):
    n = CHUNK
    conv = cb
    for k in range(4):
        conv = conv + cw[k:k + 1] * xext[HALO - 3 + k:HALO - 3 + k + n]
    xc = jax.nn.silu(conv)
    xs, bm, cm = xc[:, :D_SSD], xc[:, D_SSD:D_SSD + 2 * SSD_STATE], xc[:, D_SSD + 2 * SSD_STATE:]
    lane = lax.broadcasted_iota(jnp.int32, (1, 128), 1)
    dt = jax.nn.softplus(jnp.where(lane < N_HEADS, plast, 0.0) + dtb)
    adt = dt * (-jnp.exp(alog))
    row = lax.broadcasted_iota(jnp.int32, (n, n), 0)
    col = lax.broadcasted_iota(jnp.int32, (n, n), 1)
    tri = row >= col
    acs = jnp.dot(tri.astype(f32), adt, precision=lax.Precision.HIGHEST, preferred_element_type=f32)
    acs_t = acs.T
    bgs = [bm[:, g * SSD_STATE:(g + 1) * SSD_STATE] for g in range(2)]
    cgs = [cm[:, g * SSD_STATE:(g + 1) * SSD_STATE] for g in range(2)]
    cb_ts = [mm_nt(cgs[g], bgs[g]) for g in range(2)]
    low = lane < SSD_HEAD_DIM
    low_rows = lax.broadcasted_iota(jnp.int32, (2 * SSD_HEAD_DIM, 1), 0) < SSD_HEAD_DIM

    def both(a0, a1):
        return jnp.where(low, a0, a1)

    pre = []
    for i in range(N_HEADS // 2):
        h0, h1 = 2 * i, 2 * i + 1
        col0, col1 = acs[:, h0:h0 + 1], acs[:, h1:h1 + 1]
        last0, last1 = acs[n - 1:n, h0:h0 + 1], acs[n - 1:n, h1:h1 + 1]
        cb_t = cb_ts[i // 2]
        scores0 = cb_t * jnp.exp(jnp.where(tri, col0 - acs_t[h0:h0 + 1, :], -jnp.inf))
        scores1 = cb_t * jnp.exp(jnp.where(tri, col1 - acs_t[h1:h1 + 1, :], -jnp.inf))
        xp = xs[:, i * 128:(i + 1) * 128]
        xdt = xp * both(dt[:, h0:h0 + 1], dt[:, h1:h1 + 1])
        weighted = xdt * both(jnp.exp(last0 - col0), jnp.exp(last1 - col1))
        chunk_decay = jnp.where(low_rows, jnp.exp(last0), jnp.exp(last1))
        in_decay = both(jnp.exp(col0), jnp.exp(col1))
        skip = both(dskip[:, h0:h0 + 1], dskip[:, h1:h1 + 1]) * xp
        pre.append((scores0, scores1, xdt, weighted, chunk_decay, in_decay, skip))
    prods = []
    for i in range(N_HEADS // 2):
        scores0, scores1, xdt, weighted, _, _, _ = pre[i]
        g = i // 2
        y_diag = mm(scores0, jnp.where(low, xdt, 0.0)) + mm(scores1, jnp.where(low, 0.0, xdt))
        prods.append((y_diag, mm_tn(weighted, bgs[g]), mm_nt(cgs[g], prev[i])))
    ys, news = [], []
    for i in range(N_HEADS // 2):
        y_diag, st, y_off = prods[i]
        _, _, _, _, chunk_decay, in_decay, skip = pre[i]
        news.append(chunk_decay * prev[i] + st)
        ys.append(y_diag + y_off * in_decay + skip)
    y = jnp.concatenate(ys, axis=1)
    yg = y * jax.nn.silu(z)
    half = D_SSD // 2
    outs = []
    for g in range(2):
        t = yg[:, g * half:(g + 1) * half]
        outs.append(t * lax.rsqrt(jnp.mean(t * t, axis=-1, keepdims=True) + EPS))
    return jnp.concatenate(outs, axis=1) * snw, jnp.stack(news)


def _f_out(o, yg, g1, wo, slot=None):
    cat = jnp.concatenate([o[h] for h in range(N_HEADS)] + [yg], axis=1)
    return g1 * mmw(cat, wo, slot)


def _f_modulate(x, nw, sh, sc):
    return _rms(x, nw) * (1.0 + sc) + sh


def proj_fwd(x, nw, sh, sc, w):
    s = x.shape[0]
    ts = _token_block(s)

    def body(x_ref, nw_ref, sh_ref, sc_ref, w_ref, pa_ref, pz_ref, px_ref, pl_ref):
        p = _f_proj(x_ref[...], nw_ref[...], sh_ref[...], sc_ref[...], w_ref[...])
        pa_ref[...] = p[:, :384]
        pz_ref[...] = p[:, 384:896]
        px_ref[...] = p[:, 896:1920]
        pl_ref[...] = p[:, 1920:]

    vec = _const((1, D_MODEL))
    return pl.pallas_call(
        body, name="proj_fwd", grid=(s // ts,),
        in_specs=[pl.BlockSpec((ts, D_MODEL), lambda i: (i, 0)), vec, vec, vec, _const((D_PROJ, D_MODEL))],
        out_specs=[pl.BlockSpec((ts, 384), lambda i: (i, 0)), pl.BlockSpec((ts, 512), lambda i: (i, 0)),
                   pl.BlockSpec((ts, 1024), lambda i: (i, 0)), pl.BlockSpec((ts, 128), lambda i: (i, 0))],
        out_shape=[jax.ShapeDtypeStruct((s, 384), f32), jax.ShapeDtypeStruct((s, 512), f32),
                   jax.ShapeDtypeStruct((s, 1024), f32), jax.ShapeDtypeStruct((s, 128), f32)],
    )(x, nw, sh, sc, w)


def rope_tables(pos, inv):
    s = pos.shape[0]
    ts = _token_block(s)

    def body(pos_ref, inv_ref, cos_ref, sin_ref):
        ang = pos_ref[...].astype(f32) * inv_ref[...]
        lane = lax.broadcasted_iota(jnp.int32, (1, HEAD_LANES), 1)
        half = ROPE // 2
        cos_ref[...] = jnp.where(lane < NOPE, 1.0, jnp.where(lane < NOPE + ROPE, jnp.cos(ang), 0.0))
        sn = jnp.sin(ang)
        sin_ref[...] = jnp.where((lane >= NOPE) & (lane < NOPE + half), -sn,
                                 jnp.where((lane >= NOPE + half) & (lane < NOPE + ROPE), sn, 0.0))

    return pl.pallas_call(
        body, name="rope_tables", grid=(s // ts,),
        in_specs=[pl.BlockSpec((ts, 1), lambda i: (i, 0)), _const((1, HEAD_LANES))],
        out_specs=[pl.BlockSpec((ts, HEAD_LANES), lambda i: (i, 0))] * 2,
        out_shape=[jax.ShapeDtypeStruct((s, HEAD_LANES), f32)] * 2,
    )(pos, inv)


def _qkv_param_specs():
    return [_const((1, Q_RANK)), _const((1, KV_RANK)), _const((N_HEADS, Q_RANK, HEAD_LANES)),
            _const((N_HEADS, KV_RANK, HEAD_LANES)), _const((N_HEADS, KV_RANK, V_DIM)),
            _const((1, HEAD_LANES)), _const((1, HEAD_LANES)), _const((1, HEAD_LANES))]


def qkv_fwd(pa, plast, cos_t, sin_t, params):
    s = pa.shape[0]
    ts = _token_block(s)

    def body(pa_ref, pl_ref, cos_ref, sin_ref, *rest):
        prm = [r[...] for r in rest[:8]]
        q_ref, k_ref, v_ref = rest[8:]
        q, k, v = _f_qkv(pa_ref[...], pl_ref[...], cos_ref[...], sin_ref[...], *prm)
        q_ref[...] = q.astype(bf16)
        lane = lax.broadcasted_iota(jnp.int32, (1, 1, HEAD_LANES), 2)
        k_ref[...] = jnp.where((lane == SPARE_Q) | (lane == SPARE_Q + 1), 1.0, k).astype(bf16)
        v_ref[...] = jnp.concatenate([v, jnp.ones_like(v)], axis=-1).astype(bf16)

    tok = lambda w: pl.BlockSpec((ts, w), lambda i: (i, 0))
    head = pl.BlockSpec((N_HEADS, ts, HEAD_LANES), lambda i: (0, i, 0))
    return pl.pallas_call(
        body, name="qkv_fwd", grid=(s // ts,),
        in_specs=[tok(384), tok(128), tok(128), tok(128)] + _qkv_param_specs(),
        out_specs=[head] * 3, out_shape=[jax.ShapeDtypeStruct((N_HEADS, s, HEAD_LANES), bf16)] * 3,
    )(pa, plast, cos_t, sin_t, *params)


def _scores(q, k):
    return lax.dot_general(q, k, (((1,), (1,)), ((), ())), preferred_element_type=f32)


def _tril(rows, cols, row_offset):
    row = row_offset + lax.broadcasted_iota(jnp.int32, (rows, cols), 0)
    col = lax.broadcasted_iota(jnp.int32, (rows, cols), 1)
    return row >= col


def _call_with_job(body, name, grid, job, in_specs, out_specs, out_shape, scratch_shapes, operands, relay_at=None):
    if job is None:
        res = pl.pallas_call(body, name=name, grid=grid, in_specs=in_specs, out_specs=out_specs, out_shape=out_shape,
                             scratch_shapes=scratch_shapes)(*operands)
        return res, None

    def at_step(i, n):
        if i == 0:
            want = [0] * len(grid)
        elif i == n - 1:
            want = [g - 1 for g in grid]
        else:
            want = relay_at
        return functools.reduce(jnp.logical_and, [pl.program_id(a) == s for a, s in enumerate(want)])

    carrier = _carry(job, body, len(in_specs), len(out_specs), at_step)
    res = pl.pallas_call(
        carrier, name=name, grid=grid,
        in_specs=list(in_specs) + [ANY] * len(job.operands), out_specs=list(out_specs) + [ANY] * len(job.out_shape),
        out_shape=list(out_shape) + list(job.out_shape), scratch_shapes=list(scratch_shapes) + job.scratch,
    )(*operands, *job.operands)
    return res[:len(out_specs)], res[len(out_specs):]


def attn_fwd(q, k, v, job=None):
    s = q.shape[1]
    t = _token_block(s)
    nb = s // t

    rb = min(ATTN_ROWS_FWD, t)

    hp = ATTN_HEADS_FWD

    def body(q_ref, k_ref, v_ref, o_ref, qx_ref, m_sc, acc_sc):
        qi = pl.program_id(1)
        m_sc[...] = jnp.full(m_sc.shape, NEG, f32)
        acc_sc[...] = jnp.zeros(acc_sc.shape, f32)

        def step(k0, diagonal):
            chains = [(hh, r) for hh in range(hp) for r in range(t // rb)]

            def scores(hh, r):
                nk = (r + 1) * rb if diagonal else t
                sc = _scores(q_ref[hh, pl.ds(r * rb, rb), :], k_ref[hh, pl.ds(k0, nk), :])
                return jnp.where(_tril(rb, nk, r * rb), sc, NEG) if diagonal else sc

            ahead = scores(*chains[0])
            for c, (hh, r) in enumerate(chains):
                sc = ahead
                if c + 1 < len(chains):
                    ahead = scores(*chains[c + 1])
                rows = pl.ds(r * rb, rb)
                keys = pl.ds(k0, (r + 1) * rb if diagonal else t)
                m_prev = m_sc[hh, rows, :1]
                m_new = jnp.maximum(m_prev, jnp.max(sc, axis=-1, keepdims=True))
                p = jnp.exp2(sc - m_new)
                alpha = jnp.exp2(m_prev - m_new)
                acc = alpha * acc_sc[hh, rows, :] + jnp.dot(p.astype(bf16), v_ref[hh, keys, :], preferred_element_type=f32)
                if diagonal:
                    l = acc[:, V_DIM:V_DIM + 1]
                    o_ref[hh, rows, :] = acc[:, :V_DIM] / l
                    lse = m_new + jnp.log2(l)
                    high = lse.astype(bf16)
                    low = (lse - high.astype(f32)).astype(bf16)
                    lane = lax.broadcasted_iota(jnp.int32, (1, HEAD_LANES), 1)
                    qx_ref[hh, rows, :] = jnp.where(lane == SPARE_Q, -high,
                                                    jnp.where(lane == SPARE_Q + 1, -low, q_ref[hh, rows, :]))
                else:
                    acc_sc[hh, rows, :] = acc
                    m_sc[hh, rows, :] = jnp.broadcast_to(m_new, (rb, 128))

        def below(ki, carry):
            step(pl.multiple_of(ki * t, t), False)
            return carry

        lax.fori_loop(0, qi, below, 0)
        step(pl.multiple_of(qi * t, t), True)

    return _call_with_job(
        body, "attn_fwd" if job is None else "attn_fwd_comm", (N_HEADS // hp, nb), job,
        in_specs=[pl.BlockSpec((hp, t, HEAD_LANES), lambda h, qi: (h, qi, 0)),
                  pl.BlockSpec((hp, s, HEAD_LANES), lambda h, qi: (h, 0, 0)),
                  pl.BlockSpec((hp, s, HEAD_LANES), lambda h, qi: (h, 0, 0))],
        out_specs=[pl.BlockSpec((hp, t, V_DIM), lambda h, qi: (h, qi, 0)),
                   pl.BlockSpec((hp, t, HEAD_LANES), lambda h, qi: (h, qi, 0))],
        out_shape=[jax.ShapeDtypeStruct((N_HEADS, s, V_DIM), f32), jax.ShapeDtypeStruct((N_HEADS, s, HEAD_LANES), bf16)],
        scratch_shapes=[pltpu.VMEM((hp, t, 128), f32), pltpu.VMEM((hp, t, HEAD_LANES), f32)],
        operands=(q, k, v), relay_at=(N_HEADS // hp - 1, max(nb - 2, 0)))


def _ssd_param_specs():
    return [_const((4, D_CONV)), _const((1, D_CONV)), _const((1, 128)), _const((1, 128)), _const((1, 128)),
            _const((1, D_SSD))]


def ssd_fwd(px, pz, plast, params):
    s = px.shape[0]
    nc = s // CHUNK

    def body(px_ref, pz_ref, pl_ref, cw_ref, cb_ref, dtb_ref, alog_ref, dskip_ref, snw_ref, yg_ref, st_ref,
             state_sc, halo_sc):
        i = pl.program_id(0)

        @pl.when(i == 0)
        def _():
            state_sc[...] = jnp.zeros(state_sc.shape, f32)
            halo_sc[...] = jnp.zeros(halo_sc.shape, f32)

        x = px_ref[...]
        prev = state_sc[...]
        st_ref[...] = prev
        xext = jnp.concatenate([halo_sc[...], x], axis=0)
        yg, new = _f_ssd(xext, pz_ref[...], pl_ref[...], prev, cw_ref[...], cb_ref[...], dtb_ref[...],
                         alog_ref[...], dskip_ref[...], snw_ref[...])
        yg_ref[...] = yg
        state_sc[...] = new
        halo_sc[...] = x[CHUNK - HALO:]

    tok = lambda w: pl.BlockSpec((CHUNK, w), lambda i: (i, 0))
    return pl.pallas_call(
        body, name="ssd_fwd", grid=(nc,),
        in_specs=[tok(D_CONV), tok(D_SSD), tok(128)] + _ssd_param_specs(),
        out_specs=[tok(D_SSD), pl.BlockSpec((None, N_HEADS // 2, 2 * SSD_HEAD_DIM, SSD_STATE), lambda i: (i, 0, 0, 0))],
        out_shape=[jax.ShapeDtypeStruct((s, D_SSD), f32),
                   jax.ShapeDtypeStruct((nc, N_HEADS // 2, 2 * SSD_HEAD_DIM, SSD_STATE), f32)],
        scratch_shapes=[pltpu.VMEM((N_HEADS // 2, 2 * SSD_HEAD_DIM, SSD_STATE), f32), pltpu.VMEM((HALO, D_CONV), f32)],
    )(px, pz, plast, *params)


def out_fwd(x, o, yg, g1, wo):
    s = x.shape[0]
    ts = _token_block(s)

    def body(x_ref, o_ref, yg_ref, g1_ref, wo_ref, out_ref):
        out_ref[...] = x_ref[...] + _f_out(o_ref[...], yg_ref[...], g1_ref[...], wo_ref[...])

    return pl.pallas_call(
        body, name="out_fwd", grid=(s // ts,),
        in_specs=[pl.BlockSpec((ts, D_MODEL), lambda i: (i, 0)), pl.BlockSpec((N_HEADS, ts, V_DIM), lambda i: (0, i, 0)),
                  pl.BlockSpec((ts, D_SSD), lambda i: (i, 0)), _const((1, D_MODEL)), _const((D_MODEL, D_MODEL))],
        out_specs=pl.BlockSpec((ts, D_MODEL), lambda i: (i, 0)),
        out_shape=jax.ShapeDtypeStruct((s, D_MODEL), f32),
    )(x, o, yg, g1, wo)


def mlp_fwd(x, nw, sh, sc, g2, wgu, wd, target=None):
    s = x.shape[0]
    ts = min(MLP_FWD_ROWS, s)
    nj = N_DEV // 2

    def body(x_ref, nw_ref, sh_ref, sc_ref, g2_ref, wg_ref, wu_ref, wd_ref, *rest):
        if target is None:
            out_ref, mix_ref, h_ref, gate_ref, up_ref = rest
        else:
            t_ref, out_ref, mix_ref, h_ref, gate_ref, up_ref, loss_ref = rest
        j = pl.program_id(1)
        first_block = pl.program_id(0) == 0

        @pl.when(j == 0)
        def _():
            h_ref[...] = _f_modulate(x_ref[...], nw_ref[...], sh_ref[...], sc_ref[...]).astype(bf16)
            mix_ref[...] = jnp.zeros(mix_ref.shape, f32)

        nr = max(ts // 512, 1)
        half = ts // nr
        wg, wu, wd = wg_ref[...], wu_ref[...], wd_ref[...]
        products = lambda r: (mmw_t(h_ref[pl.ds(r * half, half), :], wg), mmw_t(h_ref[pl.ds(r * half, half), :], wu))
        ahead = products(0)
        for r in range(nr):
            gate, up = ahead
            if r + 1 < nr:
                ahead = products(r + 1)
            rows = pl.ds(r * half, half)
            gate_ref[rows, :] = gate.astype(bf16)
            up_ref[rows, :] = up.astype(bf16)
            mix_ref[rows, :] += mmw(jax.nn.silu(gate) * up, wd)

        @pl.when(j == nj - 1)
        def _():
            y = x_ref[...] + g2_ref[...] * mix_ref[...]
            if target is None:
                out_ref[...] = y
            else:
                d = y - t_ref[...]
                out_ref[...] = d * (1.0 / D_MODEL)
                part = 0.5 * jnp.sum(jnp.sum(d * d, axis=-1, keepdims=True) * (1.0 / D_MODEL), axis=0, keepdims=True)
                _accumulate(first_block, [loss_ref], [jnp.broadcast_to(part, (8, 128))])

    vec = _const((1, D_MODEL))
    tok = pl.BlockSpec((ts, D_MODEL), lambda i, j: (i, 0))
    wide = pl.BlockSpec((None, ts, FF_SHARD), lambda i, j: (j, i, 0))
    last = target is not None
    return pl.pallas_call(
        body, name="mlp_fwd_loss" if last else "mlp_fwd", grid=(s // ts, nj),
        in_specs=[tok, vec, vec, vec, vec,
                  pl.BlockSpec((None, FF_SHARD, D_MODEL), lambda i, j: (j, 0, 0)),
                  pl.BlockSpec((None, FF_SHARD, D_MODEL), lambda i, j: (j + nj, 0, 0)),
                  pl.BlockSpec((None, FF_SHARD, D_MODEL), lambda i, j: (j, 0, 0))] + [tok] * last,
        out_specs=[tok] * 3 + [wide] * 2 + [_const((8, 128))] * last,
        out_shape=[jax.ShapeDtypeStruct((s, D_MODEL), f32), jax.ShapeDtypeStruct((s, D_MODEL), f32),
                   jax.ShapeDtypeStruct((s, D_MODEL), bf16)] + [jax.ShapeDtypeStruct((nj, s, FF_SHARD), bf16)] * 2
                  + [jax.ShapeDtypeStruct((8, 128), f32)] * last,
    )(x, nw, sh, sc, g2, wgu, wgu, wd, *([target] if last else []))


def mlp_bwd(h, dy, gate, up, g2, wgu, wd, job=None):
    s = h.shape[0]
    ts = min(MLP_BWD_ROWS, s)
    nj = N_DEV // 2
    ni = s // ts

    rows_per = min(MLP_BWD_CHUNK, ts)

    def body(h_ref, dy_ref, gate_ref, up_ref, g2_ref, wg_ref, wu_ref, wd_ref, dh_ref, dwg_ref, dwu_ref, dwd_ref,
             ag_sc, au_sc, ad_sc, act_sc, dgate_sc, dup_sc, dmix_sc):
        i = pl.program_id(1)
        wg, wu, wd = wg_ref[...], wu_ref[...], wd_ref[...]
        g2 = g2_ref[...]
        for r in range(ts // rows_per):
            rows = pl.ds(r * rows_per, rows_per)
            act, vjp = jax.vjp(lambda g, u: jax.nn.silu(g) * u, gate_ref[rows, :].astype(f32), up_ref[rows, :].astype(f32))
            dmix = (dy_ref[rows, :] * g2).astype(bf16)
            dgate, dup = vjp(_dot(dmix, wd, 1, 1))
            dgate, dup = dgate.astype(bf16), dup.astype(bf16)
            dh_ref[rows, :] = (_dot(dgate, wg, 1, 0) + _dot(dup, wu, 1, 0)).astype(bf16)
            act_sc[rows, :] = act.astype(bf16)
            dgate_sc[rows, :] = dgate
            dup_sc[rows, :] = dup
            dmix_sc[rows, :] = dmix
        h = h_ref[...]
        grads = [_dot(dgate_sc[...], h, 0, 0), _dot(dup_sc[...], h, 0, 0), _dot(act_sc[...], dmix_sc[...], 0, 0)]
        _accumulate_then_cast(i == 0, i == ni - 1, [ag_sc, au_sc, ad_sc], [dwg_ref, dwu_ref, dwd_ref], grads)

    once = pl.Buffered(1)
    wspec = lambda off: pl.BlockSpec((None, FF_SHARD, D_MODEL), lambda j, i: (j + off, 0, 0), pipeline_mode=once)
    dspec = pl.BlockSpec((None, FF_SHARD, D_MODEL), lambda j, i: (j, 0, 0), pipeline_mode=once)
    wide = pl.BlockSpec((None, ts, FF_SHARD), lambda j, i: (j, i, 0))
    return _call_with_job(
        body, "mlp_bwd" if job is None else "mlp_bwd_comm", (nj, ni), job,
        in_specs=[pl.BlockSpec((ts, D_MODEL), lambda j, i: (i, 0)), pl.BlockSpec((ts, D_MODEL), lambda j, i: (i, 0)),
                  wide, wide, _const((1, D_MODEL)), wspec(0), wspec(nj), dspec],
        out_specs=[pl.BlockSpec((None, ts, D_MODEL), lambda j, i: (j, i, 0)), wspec(0), wspec(0), dspec],
        out_shape=[jax.ShapeDtypeStruct((nj, s, D_MODEL), bf16),
                   jax.ShapeDtypeStruct((nj, FF_SHARD, D_MODEL), bf16), jax.ShapeDtypeStruct((nj, FF_SHARD, D_MODEL), bf16),
                   jax.ShapeDtypeStruct((nj, FF_SHARD, D_MODEL), bf16)],
        scratch_shapes=[pltpu.VMEM((FF_SHARD, D_MODEL), f32), pltpu.VMEM((FF_SHARD, D_MODEL), f32),
                        pltpu.VMEM((FF_SHARD, D_MODEL), f32), pltpu.VMEM((ts, FF_SHARD), bf16),
                        pltpu.VMEM((ts, FF_SHARD), bf16), pltpu.VMEM((ts, FF_SHARD), bf16), pltpu.VMEM((ts, D_MODEL), bf16)],
        operands=(h, dy, gate, up, g2, wgu, wgu, wd))


def out_bwd(dy, dhparts, x, nw, sh, sc, mix, o, yg, g1, wo):
    s = dy.shape[0]
    ts = _token_block(s)
    nj = dhparts.shape[0]

    ni = s // ts

    def body(dy_ref, dp_ref, x_ref, nw_ref, sh_ref, sc_ref, mix_ref, o_ref, yg_ref, g1_ref, wo_ref,
             dx_ref, dnw_ref, dsh_ref, dsc_ref, do_ref, dyg_ref, dg1_ref, dg2_ref, dwo_ref, acc_sc):
        i = pl.program_id(0)
        g = dy_ref[...]
        _accumulate(i == 0, [dg2_ref], [jnp.sum(g * mix_ref[...], axis=0, keepdims=True)])
        dh = dp_ref[0].astype(f32)
        for j in range(1, nj):
            dh = dh + dp_ref[j].astype(f32)
        _, vjp_mod = jax.vjp(_f_modulate, x_ref[...], nw_ref[...], sh_ref[...], sc_ref[...])
        dx_mod, dnw, dsh, dsc = vjp_mod(dh)
        _accumulate(i == 0, [dnw_ref, dsh_ref, dsc_ref], [dnw, dsh, dsc])
        g = g + dx_mod
        dx_ref[...] = g
        o = o_ref[...]
        wo = wo_ref[...]
        _, vjp = jax.vjp(lambda o_, yg_, g1_, slot: _f_out(o_, yg_, g1_, wo, slot), o, yg_ref[...], g1_ref[...],
                         jnp.zeros(wo.shape, f32))
        do, dyg, dg1, dwo = vjp(g)
        delta = jnp.sum(do * o, axis=-1, keepdims=True)
        high = delta.astype(bf16)
        low = (delta - high.astype(f32)).astype(bf16)
        lane = lax.broadcasted_iota(jnp.int32, (1, 1, HEAD_LANES), 2)
        wide = jnp.concatenate([do.astype(bf16), jnp.zeros(do.shape, bf16)], axis=-1)
        do_ref[...] = jnp.where(lane == SPARE_V, -high, jnp.where(lane == SPARE_V + 1, -low, wide))
        dyg_ref[...] = dyg
        _accumulate(i == 0, [dg1_ref], [dg1])
        _accumulate_then_cast(i == 0, i == ni - 1, [acc_sc], [dwo_ref], [dwo])

    head = pl.BlockSpec((N_HEADS, ts, V_DIM), lambda i: (0, i, 0))
    tok = pl.BlockSpec((ts, D_MODEL), lambda i: (i, 0))
    vec = _const((1, D_MODEL))
    vshape = jax.ShapeDtypeStruct((1, D_MODEL), f32)
    return pl.pallas_call(
        body, name="out_bwd", grid=(ni,), scratch_shapes=[pltpu.VMEM((D_MODEL, D_MODEL), f32)],
        in_specs=[tok, pl.BlockSpec((nj, ts, D_MODEL), lambda i: (0, i, 0)), tok, vec, vec, vec, tok,
                  head, pl.BlockSpec((ts, D_SSD), lambda i: (i, 0)), vec, _const((D_MODEL, D_MODEL))],
        out_specs=[tok, vec, vec, vec, pl.BlockSpec((N_HEADS, ts, HEAD_LANES), lambda i: (0, i, 0)),
                   pl.BlockSpec((ts, D_SSD), lambda i: (i, 0)), vec, vec, _const((D_MODEL, D_MODEL))],
        out_shape=[jax.ShapeDtypeStruct((s, D_MODEL), f32), vshape, vshape, vshape,
                   jax.ShapeDtypeStruct((N_HEADS, s, HEAD_LANES), bf16), jax.ShapeDtypeStruct((s, D_SSD), f32),
                   vshape, vshape, jax.ShapeDtypeStruct((D_MODEL, D_MODEL), bf16)],
    )(dy, dhparts, x, nw, sh, sc, mix, o, yg, g1, wo)


def attn_bwd(qx, k, v, do, job=None):
    s = qx.shape[1]
    t = _token_block(s)
    nb = s // t

    hp = ATTN_HEADS_BWD

    def body(q_ref, k_ref, v_ref, do_ref, dq_ref, dk_ref, dv_ref, dv_sc):
        ki = pl.program_id(1)

        @pl.when(ki == 0)
        def _():
            dq_ref[...] = jnp.zeros(dq_ref.shape, f32)

        dk_ref[...] = jnp.zeros(dk_ref.shape, f32)
        dv_sc[...] = jnp.zeros(dv_sc.shape, f32)

        def step(q0, diagonal):
            rows = pl.ds(q0, t)

            def products(hh):
                sc = _scores(q_ref[hh, rows, :], k_ref[hh])
                dps = _scores(do_ref[hh, rows, :], v_ref[hh])
                return (jnp.where(_tril(t, t, 0), sc, NEG) if diagonal else sc), dps

            ahead = products(0)
            for hh in range(hp):
                sc, dps = ahead
                if hh + 1 < hp:
                    ahead = products(hh + 1)
                p = jnp.exp2(sc)
                ds = (p * dps).astype(bf16)
                dv_sc[hh] += lax.dot_general(p.astype(bf16), do_ref[hh, rows, :], (((0,), (0,)), ((), ())),
                                             preferred_element_type=f32)
                dk_ref[hh] += lax.dot_general(ds, q_ref[hh, rows, :], (((0,), (0,)), ((), ())), preferred_element_type=f32)
                dq_ref[hh, rows, :] += jnp.dot(ds, k_ref[hh], preferred_element_type=f32)

        step(pl.multiple_of(ki * t, t), True)

        def above(qi, carry):
            step(pl.multiple_of(qi * t, t), False)
            return carry

        lax.fori_loop(ki + 1, nb, above, 0)
        real = lax.broadcasted_iota(jnp.int32, (1, 1, HEAD_LANES), 2) < SPARE_Q
        dk_ref[...] = jnp.where(real, dk_ref[...] * LN2, 0.0)
        dv_ref[...] = dv_sc[:, :, :V_DIM]

        @pl.when(ki == nb - 1)
        def _():
            dq_ref[...] = jnp.where(real, dq_ref[...] * LN2, 0.0)

    qspec = pl.BlockSpec((hp, s, HEAD_LANES), lambda h, ki: (h, 0, 0))
    kspec = lambda w: pl.BlockSpec((hp, t, w), lambda h, ki: (h, ki, 0))
    return _call_with_job(
        body, "attn_bwd" if job is None else "attn_bwd_comm", (N_HEADS // hp, nb), job,
        in_specs=[qspec, kspec(HEAD_LANES), kspec(HEAD_LANES), qspec],
        out_specs=[qspec, kspec(HEAD_LANES), kspec(V_DIM)],
        out_shape=[jax.ShapeDtypeStruct((N_HEADS, s, HEAD_LANES), f32), jax.ShapeDtypeStruct((N_HEADS, s, HEAD_LANES), f32),
                   jax.ShapeDtypeStruct((N_HEADS, s, V_DIM), f32)],
        scratch_shapes=[pltpu.VMEM((hp, t, HEAD_LANES), f32)], operands=(qx, k, v, do))


def ssd_bwd(px, pz, plast, states, dyg, params):
    s = px.shape[0]
    nc = s // CHUNK
    per = CHUNK // HALO

    def body(px_ref, halo_ref, pz_ref, pl_ref, st_ref, dyg_ref, cw_ref, cb_ref, dtb_ref, alog_ref, dskip_ref, snw_ref,
             dpx_ref, dpz_ref, dpl_ref, dcw_ref, dcb_ref, ddtb_ref, dalog_ref, ddskip_ref, dsnw_ref, dstate_sc, dhalo_sc):
        t = pl.program_id(0)
        chunk = nc - 1 - t

        @pl.when(t == 0)
        def _():
            dstate_sc[...] = jnp.zeros(dstate_sc.shape, f32)
            dhalo_sc[...] = jnp.zeros(dhalo_sc.shape, f32)

        halo = jnp.where(chunk > 0, halo_ref[...], 0.0)
        xext = jnp.concatenate([halo, px_ref[...]], axis=0)
        _, vjp = jax.vjp(_f_ssd, xext, pz_ref[...], pl_ref[...], st_ref[...], cw_ref[...], cb_ref[...], dtb_ref[...],
                         alog_ref[...], dskip_ref[...], snw_ref[...])
        dxext, dz, dpl, dprev, dcw, dcb, ddtb, dalog, ddskip, dsnw = vjp((dyg_ref[...], dstate_sc[...]))
        dpx_ref[...] = dxext[HALO:]
        dpx_ref[CHUNK - HALO:, :] += dhalo_sc[...]
        dhalo_sc[...] = dxext[:HALO]
        dstate_sc[...] = dprev
        dpz_ref[...] = dz
        dpl_ref[...] = dpl
        _accumulate(t == 0, [dcw_ref, dcb_ref, ddtb_ref, dalog_ref, ddskip_ref, dsnw_ref],
                    [dcw, dcb, ddtb, dalog, ddskip, dsnw])

    rev = lambda w: pl.BlockSpec((CHUNK, w), lambda t: (nc - 1 - t, 0))
    pshapes = [jax.ShapeDtypeStruct((4, D_CONV), f32), jax.ShapeDtypeStruct((1, D_CONV), f32),
               jax.ShapeDtypeStruct((1, 128), f32), jax.ShapeDtypeStruct((1, 128), f32),
               jax.ShapeDtypeStruct((1, 128), f32), jax.ShapeDtypeStruct((1, D_SSD), f32)]
    return pl.pallas_call(
        body, name="ssd_bwd", grid=(nc,),
        in_specs=[rev(D_CONV),
                  pl.BlockSpec((HALO, D_CONV), lambda t: (jnp.maximum((nc - 1 - t) * per - 1, 0), 0)),
                  rev(D_SSD), rev(128),
                  pl.BlockSpec((None, N_HEADS // 2, 2 * SSD_HEAD_DIM, SSD_STATE), lambda t: (nc - 1 - t, 0, 0, 0)),
                  rev(D_SSD)] + _ssd_param_specs(),
        out_specs=[rev(D_CONV), rev(D_SSD), rev(128)] + _ssd_param_specs(),
        out_shape=[jax.ShapeDtypeStruct((s, D_CONV), f32), jax.ShapeDtypeStruct((s, D_SSD), f32),
                   jax.ShapeDtypeStruct((s, 128), f32)] + pshapes,
        scratch_shapes=[pltpu.VMEM((N_HEADS // 2, 2 * SSD_HEAD_DIM, SSD_STATE), f32), pltpu.VMEM((HALO, D_CONV), f32)],
    )(px, px, pz, plast, states, dyg, *params)


def qkv_bwd(pa, plast, cos_t, sin_t, params, dq, dk, dv):
    s = pa.shape[0]
    ts = _token_block(s)

    def body(pa_ref, pl_ref, cos_ref, sin_ref, *rest):
        qaw, kvaw, wq, wk, wv, qnw, knw, kpw = [r[...] for r in rest[:8]]
        dq_ref, dk_ref, dv_ref = rest[8:11]
        dpa_ref, dpl_ref = rest[11:13]
        dprm_refs = list(rest[13:])
        cos_t, sin_t = cos_ref[...], sin_ref[...]

        def stage(pa_, pl_, qaw_, kvaw_, sq, sk, sv, qnw_, knw_, kpw_):
            return _f_qkv(pa_, pl_, cos_t, sin_t, qaw_, kvaw_, wq, wk, wv, qnw_, knw_, kpw_, (sq, sk, sv))

        _, vjp = jax.vjp(stage, pa_ref[...], pl_ref[...], qaw, kvaw, jnp.zeros(wq.shape, f32), jnp.zeros(wk.shape, f32),
                         jnp.zeros(wv.shape, f32), qnw, knw, kpw)
        grads = vjp((dq_ref[...], dk_ref[...], dv_ref[...]))
        dpa_ref[...] = grads[0]
        dpl_ref[...] = grads[1]
        _accumulate(pl.program_id(0) == 0, dprm_refs, list(grads[2:]))

    tok = lambda w: pl.BlockSpec((ts, w), lambda i: (i, 0))
    head = lambda w: pl.BlockSpec((N_HEADS, ts, w), lambda i: (0, i, 0))
    pshapes = [jax.ShapeDtypeStruct((1, Q_RANK), f32), jax.ShapeDtypeStruct((1, KV_RANK), f32),
               jax.ShapeDtypeStruct((N_HEADS, Q_RANK, HEAD_LANES), f32), jax.ShapeDtypeStruct((N_HEADS, KV_RANK, HEAD_LANES), f32),
               jax.ShapeDtypeStruct((N_HEADS, KV_RANK, V_DIM), f32), jax.ShapeDtypeStruct((1, HEAD_LANES), f32),
               jax.ShapeDtypeStruct((1, HEAD_LANES), f32), jax.ShapeDtypeStruct((1, HEAD_LANES), f32)]
    return pl.pallas_call(
        body, name="qkv_bwd", grid=(s // ts,),
        in_specs=[tok(384), tok(128), tok(128), tok(128)] + _qkv_param_specs()
                 + [head(HEAD_LANES), head(HEAD_LANES), head(V_DIM)],
        out_specs=[tok(384), tok(128)] + _qkv_param_specs(),
        out_shape=[jax.ShapeDtypeStruct((s, 384), f32), jax.ShapeDtypeStruct((s, 128), f32)] + pshapes,
    )(pa, plast, cos_t, sin_t, *params, dq, dk, dv)


def proj_bwd(x, nw, sh, sc, w, dpa, dpz, dpx, dpl_k, dpl_dt, dres):
    s = x.shape[0]
    ts = _token_block(s)

    ni = s // ts

    def body(x_ref, nw_ref, sh_ref, sc_ref, w_ref, dpa_ref, dpz_ref, dpx_ref, dplk_ref, dpld_ref, dres_ref,
             dx_ref, dnw_ref, dsh_ref, dsc_ref, dw_ref, acc_sc):
        i = pl.program_id(0)
        g = jnp.concatenate([dpa_ref[...], dpz_ref[...], dpx_ref[...], dplk_ref[...] + dpld_ref[...]], axis=1)
        w = w_ref[...]
        _, vjp = jax.vjp(lambda x_, nw_, sh_, sc_, slot: _f_proj(x_, nw_, sh_, sc_, w, slot), x_ref[...], nw_ref[...],
                         sh_ref[...], sc_ref[...], jnp.zeros(w.shape, f32))
        dx, dnw, dsh, dsc, dw = vjp(g)
        dx_ref[...] = dx + dres_ref[...]
        _accumulate(i == 0, [dnw_ref, dsh_ref, dsc_ref], [dnw, dsh, dsc])
        _accumulate_then_cast(i == 0, i == ni - 1, [acc_sc], [dw_ref], [dw])

    vec = _const((1, D_MODEL))
    vshape = jax.ShapeDtypeStruct((1, D_MODEL), f32)
    tok = lambda w_: pl.BlockSpec((ts, w_), lambda i: (i, 0))
    return pl.pallas_call(
        body, name="proj_bwd", grid=(ni,), scratch_shapes=[pltpu.VMEM((D_PROJ, D_MODEL), f32)],
        in_specs=[tok(D_MODEL), vec, vec, vec, _const((D_PROJ, D_MODEL)), tok(384), tok(512), tok(1024), tok(128), tok(128),
                  tok(D_MODEL)],
        out_specs=[tok(D_MODEL), vec, vec, vec, _const((D_PROJ, D_MODEL))],
        out_shape=[jax.ShapeDtypeStruct((s, D_MODEL), f32), vshape, vshape, vshape,
                   jax.ShapeDtypeStruct((D_PROJ, D_MODEL), bf16)],
    )(x, nw, sh, sc, w, dpa, dpz, dpx, dpl_k, dpl_dt, dres)


def ada_fwd(c_all, w_ada, b_cols):
    def body(c_ref, w_ref, b_ref, out_ref):
        act = jax.nn.silu(c_ref[...])
        for l in range(2):
            out_ref[l] = jnp.dot(act, w_ref[l], precision=lax.Precision.HIGHEST, preferred_element_type=f32) + b_ref[l]

    return pl.pallas_call(body, name="ada_fwd", out_shape=jax.ShapeDtypeStruct((2, N_DEV, 768), f32))(c_all, w_ada, b_cols)


def ada_bwd(c_all, dmod_cols):
    def body(c_ref, d_ref, out_ref):
        out_ref[0] = lax.dot_general(jax.nn.silu(c_ref[...]), d_ref[0], (((0,), (0,)), ((), ())),
                                     precision=lax.Precision.HIGHEST, preferred_element_type=f32)

    return pl.pallas_call(
        body, name="ada_bwd", grid=(2,),
        in_specs=[_const((N_DEV, D_MODEL)), pl.BlockSpec((1, N_DEV, 768), lambda l: (l, 0, 0))],
        out_specs=pl.BlockSpec((1, D_MODEL, 768), lambda l: (l, 0, 0)),
        out_shape=jax.ShapeDtypeStruct((2, D_MODEL, 768), f32),
    )(c_all, dmod_cols)


def _adamw(w, g, m, v):
    m = ADAM_B1 * m + (1.0 - ADAM_B1) * g
    v = ADAM_B2 * v + (1.0 - ADAM_B2) * (g * g)
    m_hat = m / (1.0 - ADAM_B1 ** ADAM_STEP)
    v_hat = v / (1.0 - ADAM_B2 ** ADAM_STEP)
    delta = -ADAM_LR * (m_hat / (jnp.sqrt(v_hat) + ADAM_EPS) + ADAM_WD * w)
    return delta, m, v


def adamw(parts, w, m, v, layer, prev, name):
    n, r, c = parts.shape
    nl = w.shape[0]
    per_elem = 2 * (n * parts.dtype.itemsize + 7 * 4)
    lanes = -(-c // 128) * 128
    tr, tc = r, c
    if per_elem * r * lanes > ADAMW_BLOCK_BYTES:
        fits = [t for t in range(r // 2, 15, -1) if r % t == 0 and t % 16 == 0 and per_elem * t * lanes <= ADAMW_BLOCK_BYTES]
        if fits:
            tr = fits[0]
        else:
            tc = next(t for t in (512, 256, 128) if c % t == 0)

    def body(p_ref, w_ref, m_ref, v_ref, *rest):
        g_ref, d_ref, nm_ref, nv_ref = rest[-4:]
        g = p_ref[0].astype(f32)
        for k in range(1, n):
            g = g + p_ref[k].astype(f32)
        delta, nm, nv = _adamw(w_ref[...], g, m_ref[...], v_ref[...])
        g_ref[...] = g
        d_ref[...] = delta
        nm_ref[...] = nm
        nv_ref[...] = nv

    blk = pl.BlockSpec((None, tr, tc), lambda i, j: (layer, i, j))
    shp = jax.ShapeDtypeStruct((nl, r, c), f32)
    kept = [] if prev is None else list(prev)
    return pl.pallas_call(
        body, name=name, grid=(r // tr, c // tc),
        in_specs=[pl.BlockSpec((n, tr, tc), lambda i, j: (0, i, j)), blk, blk, blk] + [ANY] * len(kept),
        out_specs=[blk] * 4, out_shape=[shp] * 4,
        input_output_aliases={4 + j: j for j in range(len(kept))},
    )(parts, w, m, v, *kept)


def adamw_layers_inside(parts, w, m, v, name):
    n, r, c = parts[0].shape
    nl = w.shape[1]
    tc = next(t for t in (256, 128) if c % t == 0)

    def body(*refs):
        p_refs = refs[:nl]
        w_ref, m_ref, v_ref, g_ref, d_ref, nm_ref, nv_ref = refs[nl:]
        for l in range(nl):
            g = p_refs[l][0].astype(f32)
            for k in range(1, n):
                g = g + p_refs[l][k].astype(f32)
            delta, nm, nv = _adamw(w_ref[:, l, :], g, m_ref[:, l, :], v_ref[:, l, :])
            g_ref[:, l, :] = g
            d_ref[:, l, :] = delta
            nm_ref[:, l, :] = nm
            nv_ref[:, l, :] = nv

    blk = pl.BlockSpec((r, nl, tc), lambda j: (0, 0, j))
    shp = jax.ShapeDtypeStruct((r, nl, c), f32)
    return pl.pallas_call(
        body, name=name, grid=(c // tc,),
        in_specs=[pl.BlockSpec((n, r, tc), lambda j: (0, 0, j))] * nl + [blk] * 3,
        out_specs=[blk] * 4, out_shape=[shp] * 4,
    )(*parts, w, m, v)


def _my_index():
    return 4 * lax.axis_index("x") + 2 * lax.axis_index("y") + lax.axis_index("c")


def _coords(idx):
    return (idx // 4, (idx // 2) % 2, idx % 2)


class CommJob:
    def __init__(self, operands, out_shape, phases, scratch):
        self.operands, self.out_shape, self.phases, self.scratch = operands, out_shape, phases, scratch


def _wait(out, n_blocks, send_sem, recv_sem, send=True, recv=True):
    span = out.at[pl.ds(0, n_blocks)]
    desc = pltpu.make_async_remote_copy(src_ref=span, dst_ref=span, send_sem=send_sem, recv_sem=recv_sem,
                                        device_id=_coords(_my_index()), device_id_type=MESH)
    if recv:
        desc.wait_recv()
    if send:
        desc.wait_send()


def gather_job(shards):
    n = len(shards)

    def places():
        x, y, c = lax.axis_index("x"), lax.axis_index("y"), lax.axis_index("c")
        return (x, y, c), (x, y, 1 - c), [(1 - x, y), (x, 1 - y), (1 - x, 1 - y)]

    def index(p):
        return 4 * p[0] + 2 * p[1] + p[2]

    def start(ins, outs, sems):
        far_send, far_recv, near_send, near_recv, local = sems
        me, sibling, chips = places()
        for k in range(n):
            pltpu.make_async_copy(ins[k], outs[k].at[index(me)], local.at[k]).start()
            for chip in chips:
                pltpu.make_async_remote_copy(src_ref=ins[k], dst_ref=outs[k].at[index(me)], send_sem=far_send.at[k],
                                             recv_sem=far_recv.at[k], device_id=(*chip, me[2]), device_id_type=MESH).start()
            pltpu.make_async_remote_copy(src_ref=ins[k], dst_ref=outs[k].at[index(me)], send_sem=near_send.at[k],
                                         recv_sem=near_recv.at[k], device_id=sibling, device_id_type=MESH).start()

    def relay(ins, outs, sems):
        far_send, far_recv, near_send, near_recv, local = sems
        me, sibling, chips = places()
        for k in range(n):
            _wait(outs[k], 3, far_send.at[k], far_recv.at[k], send=False)
            for chip in chips:
                block = outs[k].at[index((*chip, me[2]))]
                pltpu.make_async_remote_copy(src_ref=block, dst_ref=block, send_sem=near_send.at[k],
                                             recv_sem=near_recv.at[k], device_id=sibling, device_id_type=MESH).start()

    def finish(ins, outs, sems):
        far_send, far_recv, near_send, near_recv, local = sems
        for k in range(n):
            _wait(outs[k], 4, near_send.at[k], near_recv.at[k])
            _wait(outs[k], 3, far_send.at[k], far_recv.at[k], recv=False)
            pltpu.make_async_copy(ins[k], outs[k].at[0], local.at[k]).wait()

    shapes = [jax.ShapeDtypeStruct((N_DEV,) + tuple(a.shape), a.dtype) for a in shards]
    return CommJob(list(shards), shapes, [start, relay, finish], [pltpu.SemaphoreType.DMA((n,))] * 5)


def scatter_job(tensors):
    n = len(tensors)
    flat, where = [], {}
    for k, pieces in enumerate(tensors):
        d = 0
        for piece in pieces:
            for b in range(piece.shape[0]):
                where[k, d] = (len(flat), b)
                d += 1
            flat.append(piece)
        assert d == N_DEV

    def start(ins, outs, sems):
        send_sems, recv_sems, local_sems = sems
        me = _my_index()

        def block(k, d):
            i, b = where[k, d]
            return ins[i].at[b]

        for d in range(N_DEV):
            @pl.when(d != me)
            def _():
                for k in range(n):
                    pltpu.make_async_remote_copy(src_ref=block(k, d), dst_ref=outs[k].at[me], send_sem=send_sems.at[k],
                                                 recv_sem=recv_sems.at[k], device_id=(d // 4, (d // 2) % 2, d % 2),
                                                 device_id_type=MESH).start()

            @pl.when(d == me)
            def _():
                for k in range(n):
                    pltpu.make_async_copy(block(k, d), outs[k].at[d], local_sems.at[k]).start()

    def finish(ins, outs, sems):
        send_sems, recv_sems, local_sems = sems
        for k in range(n):
            _wait(outs[k], N_DEV - 1, send_sems.at[k], recv_sems.at[k])
            i, b = where[k, 0]
            pltpu.make_async_copy(ins[i].at[b], outs[k].at[0], local_sems.at[k]).wait()

    shapes = [jax.ShapeDtypeStruct((N_DEV,) + tuple(p[0].shape[1:]), p[0].dtype) for p in tensors]
    return CommJob(flat, shapes, [start, finish], [pltpu.SemaphoreType.DMA((n,))] * 3)


def merge_jobs(a, b):
    def on(job, off):
        oi, oo, os_ = off
        ni, no, ns = len(job.operands), len(job.out_shape), len(job.scratch)
        return lambda phase: (lambda ins, outs, sems: phase(ins[oi:oi + ni], outs[oo:oo + no], sems[os_:os_ + ns]))

    wrap_a = on(a, (0, 0, 0))
    wrap_b = on(b, (len(a.operands), len(a.out_shape), len(a.scratch)))
    pa, pb = [wrap_a(p) for p in a.phases], [wrap_b(p) for p in b.phases]

    def together(*phases):
        def run(ins, outs, sems):
            for p in phases:
                p(ins, outs, sems)
        return run

    middle = pa[1:-1] + pb[1:-1]
    phases = [together(pa[0], pb[0])] + middle + [together(pa[-1], pb[-1])]
    return CommJob(a.operands + b.operands, a.out_shape + b.out_shape, phases, a.scratch + b.scratch)


def comm_call(job, name):
    ni, no = len(job.operands), len(job.out_shape)

    def body(*refs):
        ins, outs, sems = refs[:ni], refs[ni:ni + no], refs[ni + no:]
        for phase in job.phases:
            phase(ins, outs, sems)

    return pl.pallas_call(body, name=name, in_specs=[ANY] * ni, out_specs=[ANY] * no, out_shape=job.out_shape,
                          scratch_shapes=job.scratch)(*job.operands)


def _carry(job, body, n_in, n_out, at_step):
    ji, jo, js = len(job.operands), len(job.out_shape), len(job.scratch)

    def carrier(*refs):
        a, b = n_in, n_in + ji
        c, d = b + n_out, b + n_out + jo
        e = len(refs) - js
        job_refs = (refs[a:b], refs[c:d], refs[e:])
        n = len(job.phases)

        @pl.when(at_step(0, n))
        def _():
            job.phases[0](*job_refs)

        body(*refs[:a], *refs[b:c], *refs[d:e])

        for i in range(1, n):
            @pl.when(at_step(i, n))
            def _():
                job.phases[i](*job_refs)

    return carrier


def _pad_lanes(v, lo, total=128):
    return jnp.pad(v, (lo, total - lo - v.shape[0]))[None, :]


MIXER_WEIGHTS = ("w_in", "w_q_up", "w_kv_up", "conv_w")
LATE_WEIGHTS = ("w_out", "w_gate_up", "w_down")


def mixer_operands(g, sw):
    w_in = g["w_in"].reshape(D_IN, D_MODEL)
    zero = lambda rows: jnp.zeros((rows, D_MODEL), w_in.dtype)
    w_proj = jnp.concatenate(
        [w_in[:384], w_in[416:928], w_in[928:1952], w_in[1952:1960], zero(56), w_in[384:416], zero(32)], axis=0)
    wq = jnp.pad(g["w_q_up"], ((0, 0), (0, 0), (0, HEAD_LANES - NOPE - ROPE)))
    wk = jnp.pad(g["w_kv_up"][:, :, :NOPE], ((0, 0), (0, 0), (0, HEAD_LANES - NOPE)))
    wv = g["w_kv_up"][:, :, NOPE:]
    qkv = (sw["q_a_norm_w"][None, :], sw["kv_a_norm_w"][None, :], wq, wk, wv,
           _pad_lanes(jnp.concatenate([sw["q_nope_norm_w"], sw["q_pe_norm_w"]]), 0),
           _pad_lanes(sw["k_nope_norm_w"], 0), _pad_lanes(sw["k_pe_norm_w"], NOPE))
    conv_w = g["conv_w"].astype(f32).transpose(1, 0, 2).reshape(4, D_CONV)
    ssd = (conv_w, sw["conv_b"][None, :], _pad_lanes(sw["dt_bias"], 0), _pad_lanes(sw["a_log"], 0),
           _pad_lanes(sw["d_skip"], 0), sw["ssd_norm_w"][None, :])
    return dict(w_proj=w_proj, qkv=qkv, ssd=ssd, n1=sw["norm1_w"][None, :])


def late_operands(g, sw):
    return dict(wo=g["w_out"].reshape(D_MODEL, D_MODEL), wgu=g["w_gate_up"],
                wd=g["w_down"].reshape(N_DEV // 2, FF_SHARD, D_MODEL), n2=sw["norm2_w"][None, :])


def layer_fwd(x, mod, kw, cos_t, sin_t, job=None, late=None, target=None):
    sh1, sc1, g1, sh2, sc2, g2 = [mod[i:i + 1] for i in range(6)]
    pa, pz, px, plast = proj_fwd(x, kw["n1"], sh1, sc1, kw["w_proj"])
    q, k, v = qkv_fwd(pa, plast, cos_t, sin_t, kw["qkv"])
    (o, qx), carried = attn_fwd(q, k, v, job)
    if late is not None:
        kw = {**kw, **late(carried)}
    yg, states = ssd_fwd(px, pz, plast, kw["ssd"])
    x_mid = out_fwd(x, o, yg, g1, kw["wo"])
    x_out, mix, h_mid, gate, up, *loss_part = mlp_fwd(x_mid, kw["n2"], sh2, sc2, g2, kw["wgu"], kw["wd"], target)
    saved = dict(x=x, pa=pa, pz=pz, px=px, plast=plast, qx=qx, k=k, v=v, o=o, yg=yg, states=states, x_mid=x_mid,
                 mix=mix, h_mid=h_mid, gate=gate, up=up)
    return (x_out if target is None else (x_out, loss_part[0])), saved, kw, carried


def layer_bwd_head(dy, mod, kw, sv, job=None):
    _, _, g1, sh2, sc2, g2 = [mod[i:i + 1] for i in range(6)]
    (dhparts, dwg, dwu, dwd), carried = mlp_bwd(sv["h_mid"], dy, sv["gate"], sv["up"], g2, kw["wgu"], kw["wd"], job)
    dmid, dn2, dsh2, dsc2, do, dyg, dg1, dg2, dwo = out_bwd(
        dy, dhparts, sv["x_mid"], kw["n2"], sh2, sc2, sv["mix"], sv["o"], sv["yg"], g1, kw["wo"])
    early = dict(w_out=[dwo.reshape(N_DEV, D_MODEL // N_DEV, D_MODEL)], w_gate_up=[dwg, dwu],
                 w_down=[dwd.reshape(N_DEV, D_FF // N_DEV, D_MODEL)])
    head = dict(dmid=dmid, do=do, dyg=dyg, dn2=dn2, dsh2=dsh2, dsc2=dsc2, dg2=dg2, dg1=dg1)
    return head, early, carried


def layer_bwd_tail(hd, mod, kw, cos_t, sin_t, sv, job=None):
    sh1, sc1 = mod[0:1], mod[1:2]
    (dq, dk, dv), carried = attn_bwd(sv["qx"], sv["k"], sv["v"], hd["do"], job)
    dpx, dpz, dpl_dt, dcw, dcb, ddtb, dalog, ddskip, dsnw = ssd_bwd(sv["px"], sv["pz"], sv["plast"], sv["states"],
                                                                   hd["dyg"], kw["ssd"])
    dpa, dpl_k, dqaw, dkvaw, dwq, dwk, dwv, dqnw, dknw, dkpw = qkv_bwd(sv["pa"], sv["plast"], cos_t, sin_t, kw["qkv"],
                                                                       dq, dk, dv)
    dx, dn1, dsh1, dsc1, dwp = proj_bwd(sv["x"], kw["n1"], sh1, sc1, kw["w_proj"], dpa, dpz, dpx, dpl_k, dpl_dt, hd["dmid"])
    dmod = jnp.concatenate([dsh1, dsc1, hd["dg1"], hd["dsh2"], hd["dsc2"], hd["dg2"]], axis=0)
    dw_in = jnp.concatenate([dwp[:384], dwp[1984:2016], dwp[384:1920], dwp[1920:1928]], axis=0)
    grads = dict(
        norm1_w=dn1[0], norm2_w=hd["dn2"][0], q_a_norm_w=dqaw[0], kv_a_norm_w=dkvaw[0],
        q_nope_norm_w=dqnw[0, :NOPE], q_pe_norm_w=dqnw[0, NOPE:NOPE + ROPE], k_nope_norm_w=dknw[0, :NOPE],
        k_pe_norm_w=dkpw[0, NOPE:NOPE + ROPE], conv_b=dcb[0], dt_bias=ddtb[0, :N_HEADS], a_log=dalog[0, :N_HEADS],
        d_skip=ddskip[0, :N_HEADS], ssd_norm_w=dsnw[0],
        w_in=[dw_in.reshape(N_DEV, D_IN // N_DEV, D_MODEL)],
        w_q_up=[dwq[:, :, :NOPE + ROPE].astype(bf16)],
        w_kv_up=[jnp.concatenate([dwk[:, :, :NOPE], dwv], axis=2).astype(bf16)],
        conv_w=[dcw.reshape(4, N_DEV, D_CONV // N_DEV).transpose(1, 0, 2).astype(bf16)],
    )
    return dx, dmod, grads, carried


def _pack_small(get, last=None):
    flat = jnp.concatenate([get(name).reshape(-1) for name, _ in SMALL])
    flat = jnp.pad(flat, (0, SMALL_ROWS * 128 - flat.shape[0]))
    if last is not None:
        flat = flat.at[-1].set(last)
    return flat.reshape(SMALL_ROWS, 128)


def _unpack_small(packed):
    flat = packed.reshape(-1)
    out, off = {}, 0
    for name, size in SMALL:
        out[name] = flat[off:off + 2 * size].reshape(2, size)
        off += 2 * size
    return out


def kernel(x, c, positions, norm1_w, norm2_w, w_ada, b_ada, w_in, q_a_norm_w, w_q_up, kv_a_norm_w, w_kv_up, q_nope_norm_w, q_pe_norm_w, k_nope_norm_w, k_pe_norm_w, conv_w, conv_b, dt_bias, a_log, d_skip, ssd_norm_w, w_out, w_gate_up, w_down, loss_target, m_norm1_w, m_norm2_w, m_w_ada, m_b_ada, m_w_in, m_q_a_norm_w, m_w_q_up, m_kv_a_norm_w, m_w_kv_up, m_q_nope_norm_w, m_q_pe_norm_w, m_k_nope_norm_w, m_k_pe_norm_w, m_conv_w, m_conv_b, m_dt_bias, m_a_log, m_d_skip, m_ssd_norm_w, m_w_out, m_w_gate_up, m_w_down, v_norm1_w, v_norm2_w, v_w_ada, v_b_ada, v_w_in, v_q_a_norm_w, v_w_q_up, v_kv_a_norm_w, v_w_kv_up, v_q_nope_norm_w, v_q_pe_norm_w, v_k_nope_norm_w, v_k_pe_norm_w, v_conv_w, v_conv_b, v_dt_bias, v_a_log, v_d_skip, v_ssd_norm_w, v_w_out, v_w_gate_up, v_w_down):
    w = dict(norm1_w=norm1_w, norm2_w=norm2_w, w_ada=w_ada, b_ada=b_ada, w_in=w_in, q_a_norm_w=q_a_norm_w, w_q_up=w_q_up,
             kv_a_norm_w=kv_a_norm_w, w_kv_up=w_kv_up, q_nope_norm_w=q_nope_norm_w, q_pe_norm_w=q_pe_norm_w,
             k_nope_norm_w=k_nope_norm_w, k_pe_norm_w=k_pe_norm_w, conv_w=conv_w, conv_b=conv_b, dt_bias=dt_bias,
             a_log=a_log, d_skip=d_skip, ssd_norm_w=ssd_norm_w, w_out=w_out, w_gate_up=w_gate_up, w_down=w_down)
    m = dict(norm1_w=m_norm1_w, norm2_w=m_norm2_w, w_ada=m_w_ada, b_ada=m_b_ada, w_in=m_w_in, q_a_norm_w=m_q_a_norm_w,
             w_q_up=m_w_q_up, kv_a_norm_w=m_kv_a_norm_w, w_kv_up=m_w_kv_up, q_nope_norm_w=m_q_nope_norm_w,
             q_pe_norm_w=m_q_pe_norm_w, k_nope_norm_w=m_k_nope_norm_w, k_pe_norm_w=m_k_pe_norm_w, conv_w=m_conv_w,
             conv_b=m_conv_b, dt_bias=m_dt_bias, a_log=m_a_log, d_skip=m_d_skip, ssd_norm_w=m_ssd_norm_w, w_out=m_w_out,
             w_gate_up=m_w_gate_up, w_down=m_w_down)
    v = dict(norm1_w=v_norm1_w, norm2_w=v_norm2_w, w_ada=v_w_ada, b_ada=v_b_ada, w_in=v_w_in, q_a_norm_w=v_q_a_norm_w,
             w_q_up=v_w_q_up, kv_a_norm_w=v_kv_a_norm_w, w_kv_up=v_w_kv_up, q_nope_norm_w=v_q_nope_norm_w,
             q_pe_norm_w=v_q_pe_norm_w, k_nope_norm_w=v_k_nope_norm_w, k_pe_norm_w=v_k_pe_norm_w, conv_w=v_conv_w,
             conv_b=v_conv_b, dt_bias=v_dt_bias, a_log=v_a_log, d_skip=v_d_skip, ssd_norm_w=v_ssd_norm_w, w_out=v_w_out,
             w_gate_up=v_w_gate_up, w_down=v_w_down)
    me = _my_index()
    seq = x.shape[1]

    def shard(name, l):
        if name == "conv_w":
            return w[name][l]
        if name in TRANSPOSED:
            return jnp.swapaxes(w[name][l], 0, 1).astype(bf16)
        return w[name][l].astype(bf16)

    def shards(names, l):
        return [shard(name, l) for name in names]

    small = [{name: w[name][l] for name, _ in SMALL if name != "b_ada"} for l in range(2)]
    n_late = len(LATE_WEIGHTS)

    first = comm_call(gather_job([c] + shards(MIXER_WEIGHTS, 0)), "gather_first")
    c_all = first[0].reshape(N_DEV, D_MODEL)
    kws = [mixer_operands(dict(zip(MIXER_WEIGHTS, first[1:])), small[0]), None]

    b_cols = lax.dynamic_slice_in_dim(b_ada, me * 768, 768, axis=1)
    mod_cols = ada_fwd(c_all, w_ada, b_cols)
    (mod_all,) = comm_call(gather_job([mod_cols]), "gather_mod")
    mod_me = lax.dynamic_index_in_dim(mod_all, me, axis=2, keepdims=False)
    mods = [mod_me[:, l, :].reshape(6, D_MODEL) for l in range(2)]

    inv_freq = 1.0 / (ROPE_THETA ** (jnp.arange(0, ROPE, 2, dtype=f32) / ROPE))
    inv = _pad_lanes(jnp.concatenate([inv_freq, inv_freq]), NOPE)
    cos_t, sin_t = rope_tables(positions.reshape(seq, 1), inv)

    saved = [None, None]
    h, saved[0], kws[0], got = layer_fwd(
        x[0], mods[0], kws[0], cos_t, sin_t, gather_job(shards(LATE_WEIGHTS, 0) + shards(MIXER_WEIGHTS, 1)),
        lambda got: late_operands(dict(zip(LATE_WEIGHTS, got[:n_late])), small[0]))
    kws[1] = mixer_operands(dict(zip(MIXER_WEIGHTS, got[n_late:])), small[1])
    (dy, loss_part), saved[1], kws[1], _ = layer_fwd(
        h, mods[1], kws[1], cos_t, sin_t, gather_job(shards(LATE_WEIGHTS, 1)),
        lambda got: late_operands(dict(zip(LATE_WEIGHTS, got)), small[1]), loss_target[0])

    early, late = ("w_out", "w_gate_up", "w_down"), ("w_in", "w_q_up", "w_kv_up", "conv_w")
    parts = [{}, {}]
    head, pieces, _ = layer_bwd_head(dy, mods[1], kws[1], saved[1])
    dy, dmod1, grads1, got = layer_bwd_tail(head, mods[1], kws[1], cos_t, sin_t, saved[1], scatter_job([pieces[n] for n in early]))
    parts[1].update(zip(early, got))
    head, pieces, got = layer_bwd_head(dy, mods[0], kws[0], saved[0], scatter_job([grads1[n] for n in late]))
    parts[1].update(zip(late, got))
    dy, dmod0, grads0, got = layer_bwd_tail(head, mods[0], kws[0], cos_t, sin_t, saved[0], scatter_job([pieces[n] for n in early]))
    parts[0].update(zip(early, got))
    grad_x = dy[None]

    small_part = {name: jnp.stack([grads0[name], grads1[name]]) for name, _ in SMALL if name != "b_ada"}
    small_part["b_ada"] = jnp.stack([dmod0.reshape(-1), dmod1.reshape(-1)])
    last = comm_call(merge_jobs(scatter_job([grads0[n] for n in late]),
                                gather_job([_pack_small(lambda n: small_part[n], loss_part[0, 0])])), "exchange_last")
    parts[0].update(zip(late, last[:len(late)]))
    small_all = last[len(late)]
    packed = adamw(small_all, _pack_small(lambda n: w[n])[None], _pack_small(lambda n: m[n])[None],
                   _pack_small(lambda n: v[n])[None], 0, None, "adamw_small")
    loss = packed[0][0, -1, -1]
    res = {}
    for key, arr in zip("gdmv", packed):
        for name, val in _unpack_small(arr[0]).items():
            res[key, name] = val

    off = 2 * (1024 + 1024)
    dmod_all = small_all.reshape(N_DEV, -1)[:, off:off + 2 * 6144].reshape(N_DEV, 2, 6144)
    dmod_cols = lax.dynamic_slice_in_dim(dmod_all, me * 768, 768, axis=2).transpose(1, 0, 2)
    g_ada = ada_bwd(c_all, dmod_cols)
    out = None
    for l in range(2):
        out = adamw(g_ada[l][None], w_ada, m_w_ada, v_w_ada, l, out, "adamw_w_ada")
    res.update(zip([(key, "w_ada") for key in "gdmv"], out))

    inside = lambda a: jnp.transpose(a, (2, 0, 1))
    out = adamw_layers_inside([parts[l]["w_in"] for l in range(2)], inside(w_in), inside(m_w_in), inside(v_w_in), "adamw_w_in")
    res.update(zip([(key, "w_in") for key in "gdmv"], [jnp.transpose(a, (1, 2, 0)) for a in out]))
    for name in BIG:
        if name == "w_in":
            continue
        view = (lambda a: jnp.swapaxes(a, 1, 2)) if name in TRANSPOSED else (lambda a: a)
        out = None
        for l in range(2):
            out = adamw(parts[l][name], view(w[name]), view(m[name]), view(v[name]), l, out, "adamw_" + name)
        res.update(zip([(key, name) for key in "gdmv"], [view(a) for a in out]))

    return (loss, grad_x, *[res["g", n] for n in WEIGHTS], *[res["d", n] for n in WEIGHTS],
            *[res["m", n] for n in WEIGHTS], *[res["v", n] for n in WEIGHTS])
```

```python
import functools

import jax
import jax.numpy as jnp
from jax import lax
from jax.experimental import pallas as pl
from jax.experimental.pallas import tpu as pltpu

f32 = jnp.float32
bf16 = jnp.bfloat16

N_DEV = 8
D_MODEL = 1024
N_HEADS = 8
HEAD_LANES = 128
NOPE = 64
ROPE = 32
V_DIM = 64
Q_RANK = 256
KV_RANK = 128
D_SSD = 512
D_CONV = 1024
SSD_STATE = 128
SSD_HEAD_DIM = 64
CHUNK = 128
HALO = 8
D_FF = 2816
FF_SHARD = 704
D_IN = 1960
D_PROJ = 2048
EPS = 1e-6
LOG2E = 1.4426950408889634
LN2 = 0.6931471805599453
Q_SCALE = (NOPE + ROPE) ** -0.5 * LOG2E
SPARE_Q = NOPE + ROPE
SPARE_V = V_DIM
ATTN_ROWS_FWD = 256
ATTN_HEADS_FWD = 8
ATTN_HEADS_BWD = 4
MLP_FWD_ROWS = 1024
MLP_BWD_CHUNK = 256
MLP_BWD_ROWS = 1024
ROPE_THETA = 10000.0
NEG = -1e30

ADAM_LR = 0.001
ADAM_B1 = 0.9
ADAM_B2 = 0.999
ADAM_EPS = 1e-08
ADAM_WD = 0.01
ADAM_STEP = 10
ADAMW_BLOCK_BYTES = 36 << 20

MESH = pl.DeviceIdType.MESH
ANY = pl.BlockSpec(memory_space=pl.ANY)

SMALL = (("norm1_w", 1024), ("norm2_w", 1024), ("b_ada", 6144), ("q_a_norm_w", 256), ("kv_a_norm_w", 128),
         ("q_nope_norm_w", 64), ("q_pe_norm_w", 32), ("k_nope_norm_w", 64), ("k_pe_norm_w", 32),
         ("conv_b", 1024), ("dt_bias", 8), ("a_log", 8), ("d_skip", 8), ("ssd_norm_w", 512))
SMALL_ROWS = 168
BIG = ("w_in", "w_q_up", "w_kv_up", "conv_w", "w_out", "w_gate_up", "w_down")
TRANSPOSED = ("w_in", "w_gate_up")
WEIGHTS = ("norm1_w", "norm2_w", "w_ada", "b_ada", "w_in", "q_a_norm_w", "w_q_up", "kv_a_norm_w", "w_kv_up",
           "q_nope_norm_w", "q_pe_norm_w", "k_nope_norm_w", "k_pe_norm_w", "conv_w", "conv_b", "dt_bias",
           "a_log", "d_skip", "ssd_norm_w", "w_out", "w_gate_up", "w_down")


def _dot(a, b, ca, cb):
    return lax.dot_general(a.astype(bf16), b.astype(bf16), (((ca,), (cb,)), ((), ())), preferred_element_type=f32)


@jax.custom_vjp
def mm(a, b):
    return _dot(a, b, 1, 0)


def _mm_fwd(a, b):
    return _dot(a, b, 1, 0), (a, b)


def _mm_bwd(res, g):
    a, b = res
    return _dot(g, b, 1, 1).astype(a.dtype), _dot(a, g, 0, 0).astype(b.dtype)


mm.defvjp(_mm_fwd, _mm_bwd)


@jax.custom_vjp
def _mm_slot(a, w, slot):
    return _dot(a, w, 1, 0)


def _mm_slot_fwd(a, w, slot):
    return _dot(a, w, 1, 0), (a, w)


def _mm_slot_bwd(res, g):
    a, w = res
    return _dot(g, w, 1, 1).astype(a.dtype), None, _dot(a, g, 0, 0)


_mm_slot.defvjp(_mm_slot_fwd, _mm_slot_bwd)


def mmw(a, w, slot=None):
    return _dot(a, w, 1, 0) if slot is None else _mm_slot(a, w, slot)


@jax.custom_vjp
def _mm_slot_t(a, wt, slot):
    return _dot(a, wt, 1, 1)


def _mm_slot_t_fwd(a, wt, slot):
    return _dot(a, wt, 1, 1), (a, wt)


def _mm_slot_t_bwd(res, g):
    a, wt = res
    return _dot(g, wt, 1, 0).astype(a.dtype), None, _dot(g, a, 0, 0)


_mm_slot_t.defvjp(_mm_slot_t_fwd, _mm_slot_t_bwd)


def mmw_t(a, wt, slot=None):
    return _dot(a, wt, 1, 1) if slot is None else _mm_slot_t(a, wt, slot)


@jax.custom_vjp
def mm_nt(a, b):
    return _dot(a, b, 1, 1)


def _mm_nt_fwd(a, b):
    return _dot(a, b, 1, 1), (a, b)


def _mm_nt_bwd(res, g):
    a, b = res
    return _dot(g, b, 1, 0).astype(a.dtype), _dot(g, a, 0, 0).astype(b.dtype)


mm_nt.defvjp(_mm_nt_fwd, _mm_nt_bwd)


@jax.custom_vjp
def mm_tn(a, b):
    return _dot(a, b, 0, 0)


def _mm_tn_fwd(a, b):
    return _dot(a, b, 0, 0), (a, b)


def _mm_tn_bwd(res, g):
    a, b = res
    return _dot(b, g, 1, 1).astype(a.dtype), _dot(a, g, 1, 0).astype(b.dtype)


mm_tn.defvjp(_mm_tn_fwd, _mm_tn_bwd)


def _rms(x, w):
    return x * lax.rsqrt(jnp.mean(x * x, axis=-1, keepdims=True) + EPS) * w


def _const(shape):
    n = len(shape)
    return pl.BlockSpec(shape, lambda *_: (0,) * n)


def _accumulate(first, refs, vals):
    @pl.when(first)
    def _():
        for r, v in zip(refs, vals):
            r[...] = v

    @pl.when(jnp.logical_not(first))
    def _():
        for r, v in zip(refs, vals):
            r[...] += v


def _accumulate_then_cast(first, last, accs, outs, vals):
    _accumulate(first, accs, vals)

    @pl.when(last)
    def _():
        for a, o in zip(accs, outs):
            o[...] = a[...].astype(o.dtype)


def _token_block(s):
    return min(512, s)


def _f_proj(x, nw, sh, sc, w, slot=None):
    h = _rms(x, nw) * (1.0 + sc) + sh
    return mmw_t(h, w, slot)


def _f_qkv(pa, plast, cos_t, sin_t, qaw, kvaw, wq, wk, wv, qnw, knw, kpw, slots=None):
    sq, sk, sv = slots if slots is not None else ([None] * N_HEADS,) * 3
    lane = lax.broadcasted_iota(jnp.int32, (1, HEAD_LANES), 1)
    m_nope = lane < NOPE
    m_pe = (lane >= NOPE) & (lane < NOPE + ROPE)
    rows = pa.shape[0]

    def rope(t):
        half = ROPE // 2
        swapped = jnp.concatenate(
            [jnp.zeros((rows, NOPE), f32), t[:, NOPE + half:NOPE + ROPE], t[:, NOPE:NOPE + half],
             jnp.zeros((rows, HEAD_LANES - NOPE - ROPE), f32)], axis=1)
        return t * cos_t + swapped * sin_t

    qa = _rms(pa[:, :Q_RANK], qaw)
    kva = _rms(pa[:, Q_RANK:Q_RANK + KV_RANK], kvaw)
    kp = jnp.where(m_pe, plast, 0.0)
    kp = kp * lax.rsqrt(jnp.sum(kp * kp, axis=-1, keepdims=True) / ROPE + EPS) * kpw
    k_rot = rope(kp)
    qs, ks, vs = [], [], []
    for h in range(N_HEADS):
        qh = mmw(qa, wq[h], sq[h])
        ss_n = jnp.sum(jnp.where(m_nope, qh * qh, 0.0), axis=-1, keepdims=True) / NOPE
        ss_p = jnp.sum(jnp.where(m_pe, qh * qh, 0.0), axis=-1, keepdims=True) / ROPE
        r = jnp.where(m_nope, lax.rsqrt(ss_n + EPS), lax.rsqrt(ss_p + EPS))
        qs.append(rope(qh * r * qnw) * Q_SCALE)
        kh = mmw(kva, wk[h], sk[h])
        kh = kh * lax.rsqrt(jnp.sum(kh * kh, axis=-1, keepdims=True) / NOPE + EPS) * knw
        ks.append(kh + k_rot)
        vs.append(mmw(kva, wv[h], sv[h]))
    return jnp.stack(qs), jnp.stack(ks), jnp.stack(vs)


def _f_ssd(xext, z, plast, prev, cw, cb, dtb, alog, dskip, snw):
    n = CHUNK
    conv = cb
    for k in range(4):
        conv = conv + cw[k:k + 1] * xext[HALO - 3 + k:HALO - 3 + k + n]
    xc = jax.nn.silu(conv)
    xs, bm, cm = xc[:, :D_SSD], xc[:, D_SSD:D_SSD + 2 * SSD_STATE], xc[:, D_SSD + 2 * SSD_STATE:]
    lane = lax.broadcasted_iota(jnp.int32, (1, 128), 1)
    dt = jax.nn.softplus(jnp.where(lane < N_HEADS, plast, 0.0) + dtb)
    adt = dt * (-jnp.exp(alog))
    row = lax.broadcasted_iota(jnp.int32, (n, n), 0)
    col = lax.broadcasted_iota(jnp.int32, (n, n), 1)
    tri = row >= col
    acs = jnp.dot(tri.astype(f32), adt, precision=lax.Precision.HIGHEST, preferred_element_type=f32)
    acs_t = acs.T
    bgs = [bm[:, g * SSD_STATE:(g + 1) * SSD_STATE] for g in range(2)]
    cgs = [cm[:, g * SSD_STATE:(g + 1) * SSD_STATE] for g in range(2)]
    cb_ts = [mm_nt(cgs[g], bgs[g]) for g in range(2)]
    low = lane < SSD_HEAD_DIM
    low_rows = lax.broadcasted_iota(jnp.int32, (2 * SSD_HEAD_DIM, 1), 0) < SSD_HEAD_DIM

    def both(a0, a1):
        return jnp.where(low, a0, a1)

    pre = []
    for i in range(N_HEADS // 2):
        h0, h1 = 2 * i, 2 * i + 1
        col0, col1 = acs[:, h0:h0 + 1], acs[:, h1:h1 + 1]
        last0, last1 = acs[n - 1:n, h0:h0 + 1], acs[n - 1:n, h1:h1 + 1]
        cb_t = cb_ts[i // 2]
        scores0 = cb_t * jnp.exp(jnp.where(tri, col0 - acs_t[h0:h0 + 1, :], -jnp.inf))
        scores1 = cb_t * jnp.exp(jnp.where(tri, col1 - acs_t[h1:h1 + 1, :], -jnp.inf))
        xp = xs[:, i * 128:(i + 1) * 128]
        xdt = xp * both(dt[:, h0:h0 + 1], dt[:, h1:h1 + 1])
        weighted = xdt * both(jnp.exp(last0 - col0), jnp.exp(last1 - col1))
        chunk_decay = jnp.where(low_rows, jnp.exp(last0), jnp.exp(last1))
        in_decay = both(jnp.exp(col0), jnp.exp(col1))
        skip = both(dskip[:, h0:h0 + 1], dskip[:, h1:h1 + 1]) * xp
        pre.append((scores0, scores1, xdt, weighted, chunk_decay, in_decay, skip))
    prods = []
    for i in range(N_HEADS // 2):
        scores0, scores1, xdt, weighted, _, _, _ = pre[i]
        g = i // 2
        y_diag = mm(scores0, jnp.where(low, xdt, 0.0)) + mm(scores1, jnp.where(low, 0.0, xdt))
        prods.append((y_diag, mm_tn(weighted, bgs[g]), mm_nt(cgs[g], prev[i])))
    ys, news = [], []
    for i in range(N_HEADS // 2):
        y_diag, st, y_off = prods[i]
        _, _, _, _, chunk_decay, in_decay, skip = pre[i]
        news.append(chunk_decay * prev[i] + st)
        ys.append(y_diag + y_off * in_decay + skip)
    y = jnp.concatenate(ys, axis=1)
    yg = y * jax.nn.silu(z)
    half = D_SSD // 2
    outs = []
    for g in range(2):
        t = yg[:, g * half:(g + 1) * half]
        outs.append(t * lax.rsqrt(jnp.mean(t * t, axis=-1, keepdims=True) + EPS))
    return jnp.concatenate(outs, axis=1) * snw, jnp.stack(news)


def _f_out(o, yg, g1, wo, slot=None):
    cat = jnp.concatenate([o[h] for h in range(N_HEADS)] + [yg], axis=1)
    return g1 * mmw(cat, wo, slot)


def _f_modulate(x, nw, sh, sc):
    return _rms(x, nw) * (1.0 + sc) + sh


def proj_fwd(x, nw, sh, sc, w):
    s = x.shape[0]
    ts = _token_block(s)

    def body(x_ref, nw_ref, sh_ref, sc_ref, w_ref, pa_ref, pz_ref, px_ref, pl_ref):
        p = _f_proj(x_ref[...], nw_ref[...], sh_ref[...], sc_ref[...], w_ref[...])
        pa_ref[...] = p[:, :384]
        pz_ref[...] = p[:, 384:896]
        px_ref[...] = p[:, 896:1920]
        pl_ref[...] = p[:, 1920:]

    vec = _const((1, D_MODEL))
    return pl.pallas_call(
        body, name="proj_fwd", grid=(s // ts,),
        in_specs=[pl.BlockSpec((ts, D_MODEL), lambda i: (i, 0)), vec, vec, vec, _const((D_PROJ, D_MODEL))],
        out_specs=[pl.BlockSpec((ts, 384), lambda i: (i, 0)), pl.BlockSpec((ts, 512), lambda i: (i, 0)),
                   pl.BlockSpec((ts, 1024), lambda i: (i, 0)), pl.BlockSpec((ts, 128), lambda i: (i, 0))],
        out_shape=[jax.ShapeDtypeStruct((s, 384), f32), jax.ShapeDtypeStruct((s, 512), f32),
                   jax.ShapeDtypeStruct((s, 1024), f32), jax.ShapeDtypeStruct((s, 128), f32)],
    )(x, nw, sh, sc, w)


def rope_tables(pos, inv):
    s = pos.shape[0]
    ts = _token_block(s)

    def body(pos_ref, inv_ref, cos_ref, sin_ref):
        ang = pos_ref[...].astype(f32) * inv_ref[...]
        lane = lax.broadcasted_iota(jnp.int32, (1, HEAD_LANES), 1)
        half = ROPE // 2
        cos_ref[...] = jnp.where(lane < NOPE, 1.0, jnp.where(lane < NOPE + ROPE, jnp.cos(ang), 0.0))
        sn = jnp.sin(ang)
        sin_ref[...] = jnp.where((lane >= NOPE) & (lane < NOPE + half), -sn,
                                 jnp.where((lane >= NOPE + half) & (lane < NOPE + ROPE), sn, 0.0))

    return pl.pallas_call(
        body, name="rope_tables", grid=(s // ts,),
        in_specs=[pl.BlockSpec((ts, 1), lambda i: (i, 0)), _const((1, HEAD_LANES))],
        out_specs=[pl.BlockSpec((ts, HEAD_LANES), lambda i: (i, 0))] * 2,
        out_shape=[jax.ShapeDtypeStruct((s, HEAD_LANES), f32)] * 2,
    )(pos, inv)


def _qkv_param_specs():
    return [_const((1, Q_RANK)), _const((1, KV_RANK)), _const((N_HEADS, Q_RANK, HEAD_LANES)),
            _const((N_HEADS, KV_RANK, HEAD_LANES)), _const((N_HEADS, KV_RANK, V_DIM)),
            _const((1, HEAD_LANES)), _const((1, HEAD_LANES)), _const((1, HEAD_LANES))]


def qkv_fwd(pa, plast, cos_t, sin_t, params):
    s = pa.shape[0]
    ts = _token_block(s)

    def body(pa_ref, pl_ref, cos_ref, sin_ref, *rest):
        prm = [r[...] for r in rest[:8]]
        q_ref, k_ref, v_ref = rest[8:]
        q, k, v = _f_qkv(pa_ref[...], pl_ref[...], cos_ref[...], sin_ref[...], *prm)
        q_ref[...] = q.astype(bf16)
        lane = lax.broadcasted_iota(jnp.int32, (1, 1, HEAD_LANES), 2)
        k_ref[...] = jnp.where((lane == SPARE_Q) | (lane == SPARE_Q + 1), 1.0, k).astype(bf16)
        v_ref[...] = jnp.concatenate([v, jnp.ones_like(v)], axis=-1).astype(bf16)

    tok = lambda w: pl.BlockSpec((ts, w), lambda i: (i, 0))
    head = pl.BlockSpec((N_HEADS, ts, HEAD_LANES), lambda i: (0, i, 0))
    return pl.pallas_call(
        body, name="qkv_fwd", grid=(s // ts,),
        in_specs=[tok(384), tok(128), tok(128), tok(128)] + _qkv_param_specs(),
        out_specs=[head] * 3, out_shape=[jax.ShapeDtypeStruct((N_HEADS, s, HEAD_LANES), bf16)] * 3,
    )(pa, plast, cos_t, sin_t, *params)


def _scores(q, k):
    return lax.dot_general(q, k, (((1,), (1,)), ((), ())), preferred_element_type=f32)


def _tril(rows, cols, row_offset):
    row = row_offset + lax.broadcasted_iota(jnp.int32, (rows, cols), 0)
    col = lax.broadcasted_iota(jnp.int32, (rows, cols), 1)
    return row >= col


def _call_with_job(body, name, grid, job, in_specs, out_specs, out_shape, scratch_shapes, operands, relay_at=None):
    if job is None:
        res = pl.pallas_call(body, name=name, grid=grid, in_specs=in_specs, out_specs=out_specs, out_shape=out_shape,
                             scratch_shapes=scratch_shapes)(*operands)
        return res, None

    def at_step(i, n):
        if i == 0:
            want = [0] * len(grid)
        elif i == n - 1:
            want = [g - 1 for g in grid]
        else:
            want = relay_at
        return functools.reduce(jnp.logical_and, [pl.program_id(a) == s for a, s in enumerate(want)])

    carrier = _carry(job, body, len(in_specs), len(out_specs), at_step)
    res = pl.pallas_call(
        carrier, name=name, grid=grid,
        in_specs=list(in_specs) + [ANY] * len(job.operands), out_specs=list(out_specs) + [ANY] * len(job.out_shape),
        out_shape=list(out_shape) + list(job.out_shape), scratch_shapes=list(scratch_shapes) + job.scratch,
    )(*operands, *job.operands)
    return res[:len(out_specs)], res[len(out_specs):]


def attn_fwd(q, k, v, job=None):
    s = q.shape[1]
    t = _token_block(s)
    nb = s // t

    rb = min(ATTN_ROWS_FWD, t)

    hp = ATTN_HEADS_FWD

    def body(q_ref, k_ref, v_ref, o_ref, qx_ref, m_sc, acc_sc):
        qi = pl.program_id(1)
        m_sc[...] = jnp.full(m_sc.shape, NEG, f32)
        acc_sc[...] = jnp.zeros(acc_sc.shape, f32)

        def step(k0, diagonal):
            chains = [(hh, r) for hh in range(hp) for r in range(t // rb)]

            def scores(hh, r):
                nk = (r + 1) * rb if diagonal else t
                sc = _scores(q_ref[hh, pl.ds(r * rb, rb), :], k_ref[hh, pl.ds(k0, nk), :])
                return jnp.where(_tril(rb, nk, r * rb), sc, NEG) if diagonal else sc

            ahead = scores(*chains[0])
            for c, (hh, r) in enumerate(chains):
                sc = ahead
                if c + 1 < len(chains):
                    ahead = scores(*chains[c + 1])
                rows = pl.ds(r * rb, rb)
                keys = pl.ds(k0, (r + 1) * rb if diagonal else t)
                m_prev = m_sc[hh, rows, :1]
                m_new = jnp.maximum(m_prev, jnp.max(sc, axis=-1, keepdims=True))
                p = jnp.exp2(sc - m_new)
                alpha = jnp.exp2(m_prev - m_new)
                acc = alpha * acc_sc[hh, rows, :] + jnp.dot(p.astype(bf16), v_ref[hh, keys, :], preferred_element_type=f32)
                if diagonal:
                    l = acc[:, V_DIM:V_DIM + 1]
                    o_ref[hh, rows, :] = acc[:, :V_DIM] / l
                    lse = m_new + jnp.log2(l)
                    high = lse.astype(bf16)
                    low = (lse - high.astype(f32)).astype(bf16)
                    lane = lax.broadcasted_iota(jnp.int32, (1, HEAD_LANES), 1)
                    qx_ref[hh, rows, :] = jnp.where(lane == SPARE_Q, -high,
                                                    jnp.where(lane == SPARE_Q + 1, -low, q_ref[hh, rows, :]))
                else:
                    acc_sc[hh, rows, :] = acc
                    m_sc[hh, rows, :] = jnp.broadcast_to(m_new, (rb, 128))

        def below(ki, carry):
            step(pl.multiple_of(ki * t, t), False)
            return carry

        lax.fori_loop(0, qi, below, 0)
        step(pl.multiple_of(qi * t, t), True)

    return _call_with_job(
        body, "attn_fwd" if job is None else "attn_fwd_comm", (N_HEADS // hp, nb), job,
        in_specs=[pl.BlockSpec((hp, t, HEAD_LANES), lambda h, qi: (h, qi, 0)),
                  pl.BlockSpec((hp, s, HEAD_LANES), lambda h, qi: (h, 0, 0)),
                  pl.BlockSpec((hp, s, HEAD_LANES), lambda h, qi: (h, 0, 0))],
        out_specs=[pl.BlockSpec((hp, t, V_DIM), lambda h, qi: (h, qi, 0)),
                   pl.BlockSpec((hp, t, HEAD_LANES), lambda h, qi: (h, qi, 0))],
        out_shape=[jax.ShapeDtypeStruct((N_HEADS, s, V_DIM), f32), jax.ShapeDtypeStruct((N_HEADS, s, HEAD_LANES), bf16)],
        scratch_shapes=[pltpu.VMEM((hp, t, 128), f32), pltpu.VMEM((hp, t, HEAD_LANES), f32)],
        operands=(q, k, v), relay_at=(N_HEADS // hp - 1, max(nb - 2, 0)))


def _ssd_param_specs():
    return [_const((4, D_CONV)), _const((1, D_CONV)), _const((1, 128)), _const((1, 128)), _const((1, 128)),
            _const((1, D_SSD))]


def ssd_fwd(px, pz, plast, params):
    s = px.shape[0]
    nc = s // CHUNK

    def body(px_ref, pz_ref, pl_ref, cw_ref, cb_ref, dtb_ref, alog_ref, dskip_ref, snw_ref, yg_ref, st_ref,
             state_sc, halo_sc):
        i = pl.program_id(0)

        @pl.when(i == 0)
        def _():
            state_sc[...] = jnp.zeros(state_sc.shape, f32)
            halo_sc[...] = jnp.zeros(halo_sc.shape, f32)

        x = px_ref[...]
        prev = state_sc[...]
        st_ref[...] = prev
        xext = jnp.concatenate([halo_sc[...], x], axis=0)
        yg, new = _f_ssd(xext, pz_ref[...], pl_ref[...], prev, cw_ref[...], cb_ref[...], dtb_ref[...],
                         alog_ref[...], dskip_ref[...], snw_ref[...])
        yg_ref[...] = yg
        state_sc[...] = new
        halo_sc[...] = x[CHUNK - HALO:]

    tok = lambda w: pl.BlockSpec((CHUNK, w), lambda i: (i, 0))
    return pl.pallas_call(
        body, name="ssd_fwd", grid=(nc,),
        in_specs=[tok(D_CONV), tok(D_SSD), tok(128)] + _ssd_param_specs(),
        out_specs=[tok(D_SSD), pl.BlockSpec((None, N_HEADS // 2, 2 * SSD_HEAD_DIM, SSD_STATE), lambda i: (i, 0, 0, 0))],
        out_shape=[jax.ShapeDtypeStruct((s, D_SSD), f32),
                   jax.ShapeDtypeStruct((nc, N_HEADS // 2, 2 * SSD_HEAD_DIM, SSD_STATE), f32)],
        scratch_shapes=[pltpu.VMEM((N_HEADS // 2, 2 * SSD_HEAD_DIM, SSD_STATE), f32), pltpu.VMEM((HALO, D_CONV), f32)],
    )(px, pz, plast, *params)


def out_fwd(x, o, yg, g1, wo):
    s = x.shape[0]
    ts = _token_block(s)

    def body(x_ref, o_ref, yg_ref, g1_ref, wo_ref, out_ref):
        out_ref[...] = x_ref[...] + _f_out(o_ref[...], yg_ref[...], g1_ref[...], wo_ref[...])

    return pl.pallas_call(
        body, name="out_fwd", grid=(s // ts,),
        in_specs=[pl.BlockSpec((ts, D_MODEL), lambda i: (i, 0)), pl.BlockSpec((N_HEADS, ts, V_DIM), lambda i: (0, i, 0)),
                  pl.BlockSpec((ts, D_SSD), lambda i: (i, 0)), _const((1, D_MODEL)), _const((D_MODEL, D_MODEL))],
        out_specs=pl.BlockSpec((ts, D_MODEL), lambda i: (i, 0)),
        out_shape=jax.ShapeDtypeStruct((s, D_MODEL), f32),
    )(x, o, yg, g1, wo)


def mlp_fwd(x, nw, sh, sc, g2, wgu, wd, target=None):
    s = x.shape[0]
    ts = min(MLP_FWD_ROWS, s)
    nj = N_DEV // 2

    def body(x_ref, nw_ref, sh_ref, sc_ref, g2_ref, wg_ref, wu_ref, wd_ref, *rest):
        if target is None:
            out_ref, mix_ref, h_ref, gate_ref, up_ref = rest
        else:
            t_ref, out_ref, mix_ref, h_ref, gate_ref, up_ref, loss_ref = rest
        j = pl.program_id(1)
        first_block = pl.program_id(0) == 0

        @pl.when(j == 0)
        def _():
            h_ref[...] = _f_modulate(x_ref[...], nw_ref[...], sh_ref[...], sc_ref[...]).astype(bf16)
            mix_ref[...] = jnp.zeros(mix_ref.shape, f32)

        nr = max(ts // 512, 1)
        half = ts // nr
        wg, wu, wd = wg_ref[...], wu_ref[...], wd_ref[...]
        products = lambda r: (mmw_t(h_ref[pl.ds(r * half, half), :], wg), mmw_t(h_ref[pl.ds(r * half, half), :], wu))
        ahead = products(0)
        for r in range(nr):
            gate, up = ahead
            if r + 1 < nr:
                ahead = products(r + 1)
            rows = pl.ds(r * half, half)
            gate_ref[rows, :] = gate.astype(bf16)
            up_ref[rows, :] = up.astype(bf16)
            mix_ref[rows, :] += mmw(jax.nn.silu(gate) * up, wd)

        @pl.when(j == nj - 1)
        def _():
            y = x_ref[...] + g2_ref[...] * mix_ref[...]
            if target is None:
                out_ref[...] = y
            else:
                d = y - t_ref[...]
                out_ref[...] = d * (1.0 / D_MODEL)
                part = 0.5 * jnp.sum(jnp.sum(d * d, axis=-1, keepdims=True) * (1.0 / D_MODEL), axis=0, keepdims=True)
                _accumulate(first_block, [loss_ref], [jnp.broadcast_to(part, (8, 128))])

    vec = _const((1, D_MODEL))
    tok = pl.BlockSpec((ts, D_MODEL), lambda i, j: (i, 0))
    wide = pl.BlockSpec((None, ts, FF_SHARD), lambda i, j: (j, i, 0))
    last = target is not None
    return pl.pallas_call(
        body, name="mlp_fwd_loss" if last else "mlp_fwd", grid=(s // ts, nj),
        in_specs=[tok, vec, vec, vec, vec,
                  pl.BlockSpec((None, FF_SHARD, D_MODEL), lambda i, j: (j, 0, 0)),
                  pl.BlockSpec((None, FF_SHARD, D_MODEL), lambda i, j: (j + nj, 0, 0)),
                  pl.BlockSpec((None, FF_SHARD, D_MODEL), lambda i, j: (j, 0, 0))] + [tok] * last,
        out_specs=[tok] * 3 + [wide] * 2 + [_const((8, 128))] * last,
        out_shape=[jax.ShapeDtypeStruct((s, D_MODEL), f32), jax.ShapeDtypeStruct((s, D_MODEL), f32),
                   jax.ShapeDtypeStruct((s, D_MODEL), bf16)] + [jax.ShapeDtypeStruct((nj, s, FF_SHARD), bf16)] * 2
                  + [jax.ShapeDtypeStruct((8, 128), f32)] * last,
    )(x, nw, sh, sc, g2, wgu, wgu, wd, *([target] if last else []))


def mlp_bwd(h, dy, gate, up, g2, wgu, wd, job=None):
    s = h.shape[0]
    ts = min(MLP_BWD_ROWS, s)
    nj = N_DEV // 2
    ni = s // ts

    rows_per = min(MLP_BWD_CHUNK, ts)

    def body(h_ref, dy_ref, gate_ref, up_ref, g2_ref, wg_ref, wu_ref, wd_ref, dh_ref, dwg_ref, dwu_ref, dwd_ref,
             ag_sc, au_sc, ad_sc, act_sc, dgate_sc, dup_sc, dmix_sc):
        i = pl.program_id(1)
        wg, wu, wd = wg_ref[...], wu_ref[...], wd_ref[...]
        g2 = g2_ref[...]
        for r in range(ts // rows_per):
            rows = pl.ds(r * rows_per, rows_per)
            act, vjp = jax.vjp(lambda g, u: jax.nn.silu(g) * u, gate_ref[rows, :].astype(f32), up_ref[rows, :].astype(f32))
            dmix = (dy_ref[rows, :] * g2).astype(bf16)
            dgate, dup = vjp(_dot(dmix, wd, 1, 1))
            dgate, dup = dgate.astype(bf16), dup.astype(bf16)
            dh_ref[rows, :] = (_dot(dgate, wg, 1, 0) + _dot(dup, wu, 1, 0)).astype(bf16)
            act_sc[rows, :] = act.astype(bf16)
            dgate_sc[rows, :] = dgate
            dup_sc[rows, :] = dup
            dmix_sc[rows, :] = dmix
        h = h_ref[...]
        grads = [_dot(dgate_sc[...], h, 0, 0), _dot(dup_sc[...], h, 0, 0), _dot(act_sc[...], dmix_sc[...], 0, 0)]
        _accumulate_then_cast(i == 0, i == ni - 1, [ag_sc, au_sc, ad_sc], [dwg_ref, dwu_ref, dwd_ref], grads)

    once = pl.Buffered(1)
    wspec = lambda off: pl.BlockSpec((None, FF_SHARD, D_MODEL), lambda j, i: (j + off, 0, 0), pipeline_mode=once)
    dspec = pl.BlockSpec((None, FF_SHARD, D_MODEL), lambda j, i: (j, 0, 0), pipeline_mode=once)
    wide = pl.BlockSpec((None, ts, FF_SHARD), lambda j, i: (j, i, 0))
    return _call_with_job(
        body, "mlp_bwd" if job is None else "mlp_bwd_comm", (nj, ni), job,
        in_specs=[pl.BlockSpec((ts, D_MODEL), lambda j, i: (i, 0)), pl.BlockSpec((ts, D_MODEL), lambda j, i: (i, 0)),
                  wide, wide, _const((1, D_MODEL)), wspec(0), wspec(nj), dspec],
        out_specs=[pl.BlockSpec((None, ts, D_MODEL), lambda j, i: (j, i, 0)), wspec(0), wspec(0), dspec],
        out_shape=[jax.ShapeDtypeStruct((nj, s, D_MODEL), bf16),
                   jax.ShapeDtypeStruct((nj, FF_SHARD, D_MODEL), bf16), jax.ShapeDtypeStruct((nj, FF_SHARD, D_MODEL), bf16),
                   jax.ShapeDtypeStruct((nj, FF_SHARD, D_MODEL), bf16)],
        scratch_shapes=[pltpu.VMEM((FF_SHARD, D_MODEL), f32), pltpu.VMEM((FF_SHARD, D_MODEL), f32),
                        pltpu.VMEM((FF_SHARD, D_MODEL), f32), pltpu.VMEM((ts, FF_SHARD), bf16),
                        pltpu.VMEM((ts, FF_SHARD), bf16), pltpu.VMEM((ts, FF_SHARD), bf16), pltpu.VMEM((ts, D_MODEL), bf16)],
        operands=(h, dy, gate, up, g2, wgu, wgu, wd))


def out_bwd(dy, dhparts, x, nw, sh, sc, mix, o, yg, g1, wo):
    s = dy.shape[0]
    ts = _token_block(s)
    nj = dhparts.shape[0]

    ni = s // ts

    def body(dy_ref, dp_ref, x_ref, nw_ref, sh_ref, sc_ref, mix_ref, o_ref, yg_ref, g1_ref, wo_ref,
             dx_ref, dnw_ref, dsh_ref, dsc_ref, do_ref, dyg_ref, dg1_ref, dg2_ref, dwo_ref, acc_sc):
        i = pl.program_id(0)
        g = dy_ref[...]
        _accumulate(i == 0, [dg2_ref], [jnp.sum(g * mix_ref[...], axis=0, keepdims=True)])
        dh = dp_ref[0].astype(f32)
        for j in range(1, nj):
            dh = dh + dp_ref[j].astype(f32)
        _, vjp_mod = jax.vjp(_f_modulate, x_ref[...], nw_ref[...], sh_ref[...], sc_ref[...])
        dx_mod, dnw, dsh, dsc = vjp_mod(dh)
        _accumulate(i == 0, [dnw_ref, dsh_ref, dsc_ref], [dnw, dsh, dsc])
        g = g + dx_mod
        dx_ref[...] = g
        o = o_ref[...]
        wo = wo_ref[...]
        _, vjp = jax.vjp(lambda o_, yg_, g1_, slot: _f_out(o_, yg_, g1_, wo, slot), o, yg_ref[...], g1_ref[...],
                         jnp.zeros(wo.shape, f32))
        do, dyg, dg1, dwo = vjp(g)
        delta = jnp.sum(do * o, axis=-1, keepdims=True)
        high = delta.astype(bf16)
        low = (delta - high.astype(f32)).astype(bf16)
        lane = lax.broadcasted_iota(jnp.int32, (1, 1, HEAD_LANES), 2)
        wide = jnp.concatenate([do.astype(bf16), jnp.zeros(do.shape, bf16)], axis=-1)
        do_ref[...] = jnp.where(lane == SPARE_V, -high, jnp.where(lane == SPARE_V + 1, -low, wide))
        dyg_ref[...] = dyg
        _accumulate(i == 0, [dg1_ref], [dg1])
        _accumulate_then_cast(i == 0, i == ni - 1, [acc_sc], [dwo_ref], [dwo])

    head = pl.BlockSpec((N_HEADS, ts, V_DIM), lambda i: (0, i, 0))
    tok = pl.BlockSpec((ts, D_MODEL), lambda i: (i, 0))
    vec = _const((1, D_MODEL))
    vshape = jax.ShapeDtypeStruct((1, D_MODEL), f32)
    return pl.pallas_call(
        body, name="out_bwd", grid=(ni,), scratch_shapes=[pltpu.VMEM((D_MODEL, D_MODEL), f32)],
        in_specs=[tok, pl.BlockSpec((nj, ts, D_MODEL), lambda i: (0, i, 0)), tok, vec, vec, vec, tok,
                  head, pl.BlockSpec((ts, D_SSD), lambda i: (i, 0)), vec, _const((D_MODEL, D_MODEL))],
        out_specs=[tok, vec, vec, vec, pl.BlockSpec((N_HEADS, ts, HEAD_LANES), lambda i: (0, i, 0)),
                   pl.BlockSpec((ts, D_SSD), lambda i: (i, 0)), vec, vec, _const((D_MODEL, D_MODEL))],
        out_shape=[jax.ShapeDtypeStruct((s, D_MODEL), f32), vshape, vshape, vshape,
                   jax.ShapeDtypeStruct((N_HEADS, s, HEAD_LANES), bf16), jax.ShapeDtypeStruct((s, D_SSD), f32),
                   vshape, vshape, jax.ShapeDtypeStruct((D_MODEL, D_MODEL), bf16)],
    )(dy, dhparts, x, nw, sh, sc, mix, o, yg, g1, wo)


def attn_bwd(qx, k, v, do, job=None):
    s = qx.shape[1]
    t = _token_block(s)
    nb = s // t

    hp = ATTN_HEADS_BWD

    def body(q_ref, k_ref, v_ref, do_ref, dq_ref, dk_ref, dv_ref, dv_sc):
        ki = pl.program_id(1)

        @pl.when(ki == 0)
        def _():
            dq_ref[...] = jnp.zeros(dq_ref.shape, f32)

        dk_ref[...] = jnp.zeros(dk_ref.shape, f32)
        dv_sc[...] = jnp.zeros(dv_sc.shape, f32)

        def step(q0, diagonal):
            half = t // 2
            subs = [(0, half, half), (half, half, t)] if diagonal and half % 128 == 0 else [(0, t, t)]
            chains = [(hh, sub) for hh in range(hp) for sub in subs]

            def products(hh, sub):
                r0, nr, nk = sub
                rows = pl.ds(q0 + r0, nr)
                sc = _scores(q_ref[hh, rows, :], k_ref[hh, :nk, :])
                dps = _scores(do_ref[hh, rows, :], v_ref[hh, :nk, :])
                return (jnp.where(_tril(nr, nk, r0), sc, NEG) if diagonal else sc), dps

            ahead = products(*chains[0])
            for c, (hh, (r0, nr, nk)) in enumerate(chains):
                sc, dps = ahead
                if c + 1 < len(chains):
                    ahead = products(*chains[c + 1])
                rows = pl.ds(q0 + r0, nr)
                p = jnp.exp2(sc)
                ds = (p * dps).astype(bf16)
                dv_sc[hh, :nk, :] += lax.dot_general(p.astype(bf16), do_ref[hh, rows, :], (((0,), (0,)), ((), ())),
                                                     preferred_element_type=f32)
                dk_ref[hh, :nk, :] += lax.dot_general(ds, q_ref[hh, rows, :], (((0,), (0,)), ((), ())),
                                                      preferred_element_type=f32)
                dq_ref[hh, rows, :] += jnp.dot(ds, k_ref[hh, :nk, :], preferred_element_type=f32)

        step(pl.multiple_of(ki * t, t), True)

        def above(qi, carry):
            step(pl.multiple_of(qi * t, t), False)
            return carry

        lax.fori_loop(ki + 1, nb, above, 0)
        real = lax.broadcasted_iota(jnp.int32, (1, 1, HEAD_LANES), 2) < SPARE_Q
        dk_ref[...] = jnp.where(real, dk_ref[...] * LN2, 0.0)
        dv_ref[...] = dv_sc[:, :, :V_DIM]

        @pl.when(ki == nb - 1)
        def _():
            dq_ref[...] = jnp.where(real, dq_ref[...] * LN2, 0.0)

    qspec = pl.BlockSpec((hp, s, HEAD_LANES), lambda h, ki: (h, 0, 0))
    kspec = lambda w: pl.BlockSpec((hp, t, w), lambda h, ki: (h, ki, 0))
    return _call_with_job(
        body, "attn_bwd" if job is None else "attn_bwd_comm", (N_HEADS // hp, nb), job,
        in_specs=[qspec, kspec(HEAD_LANES), kspec(HEAD_LANES), qspec],
        out_specs=[qspec, kspec(HEAD_LANES), kspec(V_DIM)],
        out_shape=[jax.ShapeDtypeStruct((N_HEADS, s, HEAD_LANES), f32), jax.ShapeDtypeStruct((N_HEADS, s, HEAD_LANES), f32),
                   jax.ShapeDtypeStruct((N_HEADS, s, V_DIM), f32)],
        scratch_shapes=[pltpu.VMEM((hp, t, HEAD_LANES), f32)], operands=(qx, k, v, do))


def ssd_bwd(px, pz, plast, states, dyg, params):
    s = px.shape[0]
    nc = s // CHUNK
    per = CHUNK // HALO

    def body(px_ref, halo_ref, pz_ref, pl_ref, st_ref, dyg_ref, cw_ref, cb_ref, dtb_ref, alog_ref, dskip_ref, snw_ref,
             dpx_ref, dpz_ref, dpl_ref, dcw_ref, dcb_ref, ddtb_ref, dalog_ref, ddskip_ref, dsnw_ref, dstate_sc, dhalo_sc):
        t = pl.program_id(0)
        chunk = nc - 1 - t

        @pl.when(t == 0)
        def _():
            dstate_sc[...] = jnp.zeros(dstate_sc.shape, f32)
            dhalo_sc[...] = jnp.zeros(dhalo_sc.shape, f32)

        halo = jnp.where(chunk > 0, halo_ref[...], 0.0)
        xext = jnp.concatenate([halo, px_ref[...]], axis=0)
        _, vjp = jax.vjp(_f_ssd, xext, pz_ref[...], pl_ref[...], st_ref[...], cw_ref[...], cb_ref[...], dtb_ref[...],
                         alog_ref[...], dskip_ref[...], snw_ref[...])
        dxext, dz, dpl, dprev, dcw, dcb, ddtb, dalog, ddskip, dsnw = vjp((dyg_ref[...], dstate_sc[...]))
        dpx_ref[...] = dxext[HALO:]
        dpx_ref[CHUNK - HALO:, :] += dhalo_sc[...]
        dhalo_sc[...] = dxext[:HALO]
        dstate_sc[...] = dprev
        dpz_ref[...] = dz
        dpl_ref[...] = dpl
        _accumulate(t == 0, [dcw_ref, dcb_ref, ddtb_ref, dalog_ref, ddskip_ref, dsnw_ref],
                    [dcw, dcb, ddtb, dalog, ddskip, dsnw])

    rev = lambda w: pl.BlockSpec((CHUNK, w), lambda t: (nc - 1 - t, 0))
    pshapes = [jax.ShapeDtypeStruct((4, D_CONV), f32), jax.ShapeDtypeStruct((1, D_CONV), f32),
               jax.ShapeDtypeStruct((1, 128), f32), jax.ShapeDtypeStruct((1, 128), f32),
               jax.ShapeDtypeStruct((1, 128), f32), jax.ShapeDtypeStruct((1, D_SSD), f32)]
    return pl.pallas_call(
        body, name="ssd_bwd", grid=(nc,),
        in_specs=[rev(D_CONV),
                  pl.BlockSpec((HALO, D_CONV), lambda t: (jnp.maximum((nc - 1 - t) * per - 1, 0), 0)),
                  rev(D_SSD), rev(128),
                  pl.BlockSpec((None, N_HEADS // 2, 2 * SSD_HEAD_DIM, SSD_STATE), lambda t: (nc - 1 - t, 0, 0, 0)),
                  rev(D_SSD)] + _ssd_param_specs(),
        out_specs=[rev(D_CONV), rev(D_SSD), rev(128)] + _ssd_param_specs(),
        out_shape=[jax.ShapeDtypeStruct((s, D_CONV), f32), jax.ShapeDtypeStruct((s, D_SSD), f32),
                   jax.ShapeDtypeStruct((s, 128), f32)] + pshapes,
        scratch_shapes=[pltpu.VMEM((N_HEADS // 2, 2 * SSD_HEAD_DIM, SSD_STATE), f32), pltpu.VMEM((HALO, D_CONV), f32)],
    )(px, px, pz, plast, states, dyg, *params)


def qkv_bwd(pa, plast, cos_t, sin_t, params, dq, dk, dv):
    s = pa.shape[0]
    ts = _token_block(s)

    def body(pa_ref, pl_ref, cos_ref, sin_ref, *rest):
        qaw, kvaw, wq, wk, wv, qnw, knw, kpw = [r[...] for r in rest[:8]]
        dq_ref, dk_ref, dv_ref = rest[8:11]
        dpa_ref, dpl_ref = rest[11:13]
        dprm_refs = list(rest[13:])
        cos_t, sin_t = cos_ref[...], sin_ref[...]

        def stage(pa_, pl_, qaw_, kvaw_, sq, sk, sv, qnw_, knw_, kpw_):
            return _f_qkv(pa_, pl_, cos_t, sin_t, qaw_, kvaw_, wq, wk, wv, qnw_, knw_, kpw_, (sq, sk, sv))

        _, vjp = jax.vjp(stage, pa_ref[...], pl_ref[...], qaw, kvaw, jnp.zeros(wq.shape, f32), jnp.zeros(wk.shape, f32),
                         jnp.zeros(wv.shape, f32), qnw, knw, kpw)
        grads = vjp((dq_ref[...], dk_ref[...], dv_ref[...]))
        dpa_ref[...] = grads[0]
        dpl_ref[...] = grads[1]
        _accumulate(pl.program_id(0) == 0, dprm_refs, list(grads[2:]))

    tok = lambda w: pl.BlockSpec((ts, w), lambda i: (i, 0))
    head = lambda w: pl.BlockSpec((N_HEADS, ts, w), lambda i: (0, i, 0))
    pshapes = [jax.ShapeDtypeStruct((1, Q_RANK), f32), jax.ShapeDtypeStruct((1, KV_RANK), f32),
               jax.ShapeDtypeStruct((N_HEADS, Q_RANK, HEAD_LANES), f32), jax.ShapeDtypeStruct((N_HEADS, KV_RANK, HEAD_LANES), f32),
               jax.ShapeDtypeStruct((N_HEADS, KV_RANK, V_DIM), f32), jax.ShapeDtypeStruct((1, HEAD_LANES), f32),
               jax.ShapeDtypeStruct((1, HEAD_LANES), f32), jax.ShapeDtypeStruct((1, HEAD_LANES), f32)]
    return pl.pallas_call(
        body, name="qkv_bwd", grid=(s // ts,),
        in_specs=[tok(384), tok(128), tok(128), tok(128)] + _qkv_param_specs()
                 + [head(HEAD_LANES), head(HEAD_LANES), head(V_DIM)],
        out_specs=[tok(384), tok(128)] + _qkv_param_specs(),
        out_shape=[jax.ShapeDtypeStruct((s, 384), f32), jax.ShapeDtypeStruct((s, 128), f32)] + pshapes,
    )(pa, plast, cos_t, sin_t, *params, dq, dk, dv)


def proj_bwd(x, nw, sh, sc, w, dpa, dpz, dpx, dpl_k, dpl_dt, dres):
    s = x.shape[0]
    ts = _token_block(s)

    ni = s // ts

    def body(x_ref, nw_ref, sh_ref, sc_ref, w_ref, dpa_ref, dpz_ref, dpx_ref, dplk_ref, dpld_ref, dres_ref,
             dx_ref, dnw_ref, dsh_ref, dsc_ref, dw_ref, acc_sc):
        i = pl.program_id(0)
        g = jnp.concatenate([dpa_ref[...], dpz_ref[...], dpx_ref[...], dplk_ref[...] + dpld_ref[...]], axis=1)
        w = w_ref[...]
        _, vjp = jax.vjp(lambda x_, nw_, sh_, sc_, slot: _f_proj(x_, nw_, sh_, sc_, w, slot), x_ref[...], nw_ref[...],
                         sh_ref[...], sc_ref[...], jnp.zeros(w.shape, f32))
        dx, dnw, dsh, dsc, dw = vjp(g)
        dx_ref[...] = dx + dres_ref[...]
        _accumulate(i == 0, [dnw_ref, dsh_ref, dsc_ref], [dnw, dsh, dsc])
        _accumulate_then_cast(i == 0, i == ni - 1, [acc_sc], [dw_ref], [dw])

    vec = _const((1, D_MODEL))
    vshape = jax.ShapeDtypeStruct((1, D_MODEL), f32)
    tok = lambda w_: pl.BlockSpec((ts, w_), lambda i: (i, 0))
    return pl.pallas_call(
        body, name="proj_bwd", grid=(ni,), scratch_shapes=[pltpu.VMEM((D_PROJ, D_MODEL), f32)],
        in_specs=[tok(D_MODEL), vec, vec, vec, _const((D_PROJ, D_MODEL)), tok(384), tok(512), tok(1024), tok(128), tok(128),
                  tok(D_MODEL)],
        out_specs=[tok(D_MODEL), vec, vec, vec, _const((D_PROJ, D_MODEL))],
        out_shape=[jax.ShapeDtypeStruct((s, D_MODEL), f32), vshape, vshape, vshape,
                   jax.ShapeDtypeStruct((D_PROJ, D_MODEL), bf16)],
    )(x, nw, sh, sc, w, dpa, dpz, dpx, dpl_k, dpl_dt, dres)


def ada_fwd(c_all, w_ada, b_cols):
    def body(c_ref, w_ref, b_ref, out_ref):
        act = jax.nn.silu(c_ref[...])
        for l in range(2):
            out_ref[l] = jnp.dot(act, w_ref[l], precision=lax.Precision.HIGHEST, preferred_element_type=f32) + b_ref[l]

    return pl.pallas_call(body, name="ada_fwd", out_shape=jax.ShapeDtypeStruct((2, N_DEV, 768), f32))(c_all, w_ada, b_cols)


def ada_bwd(c_all, dmod_cols):
    def body(c_ref, d_ref, out_ref):
        out_ref[0] = lax.dot_general(jax.nn.silu(c_ref[...]), d_ref[0], (((0,), (0,)), ((), ())),
                                     precision=lax.Precision.HIGHEST, preferred_element_type=f32)

    return pl.pallas_call(
        body, name="ada_bwd", grid=(2,),
        in_specs=[_const((N_DEV, D_MODEL)), pl.BlockSpec((1, N_DEV, 768), lambda l: (l, 0, 0))],
        out_specs=pl.BlockSpec((1, D_MODEL, 768), lambda l: (l, 0, 0)),
        out_shape=jax.ShapeDtypeStruct((2, D_MODEL, 768), f32),
    )(c_all, dmod_cols)


def _adamw(w, g, m, v):
    m = ADAM_B1 * m + (1.0 - ADAM_B1) * g
    v = ADAM_B2 * v + (1.0 - ADAM_B2) * (g * g)
    m_hat = m / (1.0 - ADAM_B1 ** ADAM_STEP)
    v_hat = v / (1.0 - ADAM_B2 ** ADAM_STEP)
    delta = -ADAM_LR * (m_hat / (jnp.sqrt(v_hat) + ADAM_EPS) + ADAM_WD * w)
    return delta, m, v


def adamw(parts, w, m, v, layer, prev, name):
    n, r, c = parts.shape
    nl = w.shape[0]
    per_elem = 2 * (n * parts.dtype.itemsize + 7 * 4)
    lanes = -(-c // 128) * 128
    tr, tc = r, c
    if per_elem * r * lanes > ADAMW_BLOCK_BYTES:
        fits = [t for t in range(r // 2, 15, -1) if r % t == 0 and t % 16 == 0 and per_elem * t * lanes <= ADAMW_BLOCK_BYTES]
        if fits:
            tr = fits[0]
        else:
            tc = next(t for t in (512, 256, 128) if c % t == 0)

    def body(p_ref, w_ref, m_ref, v_ref, *rest):
        g_ref, d_ref, nm_ref, nv_ref = rest[-4:]
        g = p_ref[0].astype(f32)
        for k in range(1, n):
            g = g + p_ref[k].astype(f32)
        delta, nm, nv = _adamw(w_ref[...], g, m_ref[...], v_ref[...])
        g_ref[...] = g
        d_ref[...] = delta
        nm_ref[...] = nm
        nv_ref[...] = nv

    blk = pl.BlockSpec((None, tr, tc), lambda i, j: (layer, i, j))
    shp = jax.ShapeDtypeStruct((nl, r, c), f32)
    kept = [] if prev is None else list(prev)
    return pl.pallas_call(
        body, name=name, grid=(r // tr, c // tc),
        in_specs=[pl.BlockSpec((n, tr, tc), lambda i, j: (0, i, j)), blk, blk, blk] + [ANY] * len(kept),
        out_specs=[blk] * 4, out_shape=[shp] * 4,
        input_output_aliases={4 + j: j for j in range(len(kept))},
    )(parts, w, m, v, *kept)


def adamw_layers_inside(parts, w, m, v, name):
    n, r, c = parts[0].shape
    nl = w.shape[1]
    tc = next(t for t in (256, 128) if c % t == 0)

    def body(*refs):
        p_refs = refs[:nl]
        w_ref, m_ref, v_ref, g_ref, d_ref, nm_ref, nv_ref = refs[nl:]
        for l in range(nl):
            g = p_refs[l][0].astype(f32)
            for k in range(1, n):
                g = g + p_refs[l][k].astype(f32)
            delta, nm, nv = _adamw(w_ref[:, l, :], g, m_ref[:, l, :], v_ref[:, l, :])
            g_ref[:, l, :] = g
            d_ref[:, l, :] = delta
            nm_ref[:, l, :] = nm
            nv_ref[:, l, :] = nv

    blk = pl.BlockSpec((r, nl, tc), lambda j: (0, 0, j))
    shp = jax.ShapeDtypeStruct((r, nl, c), f32)
    return pl.pallas_call(
        body, name=name, grid=(c // tc,),
        in_specs=[pl.BlockSpec((n, r, tc), lambda j: (0, 0, j))] * nl + [blk] * 3,
        out_specs=[blk] * 4, out_shape=[shp] * 4,
    )(*parts, w, m, v)


def _my_index():
    return 4 * lax.axis_index("x") + 2 * lax.axis_index("y") + lax.axis_index("c")


def _coords(idx):
    return (idx // 4, (idx // 2) % 2, idx % 2)


class CommJob:
    def __init__(self, operands, out_shape, phases, scratch):
        self.operands, self.out_shape, self.phases, self.scratch = operands, out_shape, phases, scratch


def _wait(out, n_blocks, send_sem, recv_sem, send=True, recv=True):
    span = out.at[pl.ds(0, n_blocks)]
    desc = pltpu.make_async_remote_copy(src_ref=span, dst_ref=span, send_sem=send_sem, recv_sem=recv_sem,
                                        device_id=_coords(_my_index()), device_id_type=MESH)
    if recv:
        desc.wait_recv()
    if send:
        desc.wait_send()


def gather_job(shards):
    n = len(shards)

    def places():
        x, y, c = lax.axis_index("x"), lax.axis_index("y"), lax.axis_index("c")
        return (x, y, c), (x, y, 1 - c), [(1 - x, y), (x, 1 - y), (1 - x, 1 - y)]

    def index(p):
        return 4 * p[0] + 2 * p[1] + p[2]

    def start(ins, outs, sems):
        far_send, far_recv, near_send, near_recv, local = sems
        me, sibling, chips = places()
        for k in range(n):
            pltpu.make_async_copy(ins[k], outs[k].at[index(me)], local.at[k]).start()
            for chip in chips:
                pltpu.make_async_remote_copy(src_ref=ins[k], dst_ref=outs[k].at[index(me)], send_sem=far_send.at[k],
                                             recv_sem=far_recv.at[k], device_id=(*chip, me[2]), device_id_type=MESH).start()
            pltpu.make_async_remote_copy(src_ref=ins[k], dst_ref=outs[k].at[index(me)], send_sem=near_send.at[k],
                                         recv_sem=near_recv.at[k], device_id=sibling, device_id_type=MESH).start()

    def relay(ins, outs, sems):
        far_send, far_recv, near_send, near_recv, local = sems
        me, sibling, chips = places()
        for k in range(n):
            _wait(outs[k], 3, far_send.at[k], far_recv.at[k], send=False)
            for chip in chips:
                block = outs[k].at[index((*chip, me[2]))]
                pltpu.make_async_remote_copy(src_ref=block, dst_ref=block, send_sem=near_send.at[k],
                                             recv_sem=near_recv.at[k], device_id=sibling, device_id_type=MESH).start()

    def finish(ins, outs, sems):
        far_send, far_recv, near_send, near_recv, local = sems
        for k in range(n):
            _wait(outs[k], 4, near_send.at[k], near_recv.at[k])
            _wait(outs[k], 3, far_send.at[k], far_recv.at[k], recv=False)
            pltpu.make_async_copy(ins[k], outs[k].at[0], local.at[k]).wait()

    shapes = [jax.ShapeDtypeStruct((N_DEV,) + tuple(a.shape), a.dtype) for a in shards]
    return CommJob(list(shards), shapes, [start, relay, finish], [pltpu.SemaphoreType.DMA((n,))] * 5)


def scatter_job(tensors):
    n = len(tensors)
    flat, where = [], {}
    for k, pieces in enumerate(tensors):
        d = 0
        for piece in pieces:
            for b in range(piece.shape[0]):
                where[k, d] = (len(flat), b)
                d += 1
            flat.append(piece)
        assert d == N_DEV

    def start(ins, outs, sems):
        send_sems, recv_sems, local_sems = sems
        me = _my_index()

        def block(k, d):
            i, b = where[k, d]
            return ins[i].at[b]

        for d in range(N_DEV):
            @pl.when(d != me)
            def _():
                for k in range(n):
                    pltpu.make_async_remote_copy(src_ref=block(k, d), dst_ref=outs[k].at[me], send_sem=send_sems.at[k],
                                                 recv_sem=recv_sems.at[k], device_id=(d // 4, (d // 2) % 2, d % 2),
                                                 device_id_type=MESH).start()

            @pl.when(d == me)
            def _():
                for k in range(n):
                    pltpu.make_async_copy(block(k, d), outs[k].at[d], local_sems.at[k]).start()

    def finish(ins, outs, sems):
        send_sems, recv_sems, local_sems = sems
        for k in range(n):
            _wait(outs[k], N_DEV - 1, send_sems.at[k], recv_sems.at[k])
            i, b = where[k, 0]
            pltpu.make_async_copy(ins[i].at[b], outs[k].at[0], local_sems.at[k]).wait()

    shapes = [jax.ShapeDtypeStruct((N_DEV,) + tuple(p[0].shape[1:]), p[0].dtype) for p in tensors]
    return CommJob(flat, shapes, [start, finish], [pltpu.SemaphoreType.DMA((n,))] * 3)


def merge_jobs(a, b):
    def on(job, off):
        oi, oo, os_ = off
        ni, no, ns = len(job.operands), len(job.out_shape), len(job.scratch)
        return lambda phase: (lambda ins, outs, sems: phase(ins[oi:oi + ni], outs[oo:oo + no], sems[os_:os_ + ns]))

    wrap_a = on(a, (0, 0, 0))
    wrap_b = on(b, (len(a.operands), len(a.out_shape), len(a.scratch)))
    pa, pb = [wrap_a(p) for p in a.phases], [wrap_b(p) for p in b.phases]

    def together(*phases):
        def run(ins, outs, sems):
            for p in phases:
                p(ins, outs, sems)
        return run

    middle = pa[1:-1] + pb[1:-1]
    phases = [together(pa[0], pb[0])] + middle + [together(pa[-1], pb[-1])]
    return CommJob(a.operands + b.operands, a.out_shape + b.out_shape, phases, a.scratch + b.scratch)


def comm_call(job, name):
    ni, no = len(job.operands), len(job.out_shape)

    def body(*refs):
        ins, outs, sems = refs[:ni], refs[ni:ni + no], refs[ni + no:]
        for phase in job.phases:
            phase(ins, outs, sems)

    return pl.pallas_call(body, name=name, in_specs=[ANY] * ni, out_specs=[ANY] * no, out_shape=job.out_shape,
                          scratch_shapes=job.scratch)(*job.operands)


def _carry(job, body, n_in, n_out, at_step):
    ji, jo, js = len(job.operands), len(job.out_shape), len(job.scratch)

    def carrier(*refs):
        a, b = n_in, n_in + ji
        c, d = b + n_out, b + n_out + jo
        e = len(refs) - js
        job_refs = (refs[a:b], refs[c:d], refs[e:])
        n = len(job.phases)

        @pl.when(at_step(0, n))
        def _():
            job.phases[0](*job_refs)

        body(*refs[:a], *refs[b:c], *refs[d:e])

        for i in range(1, n):
            @pl.when(at_step(i, n))
            def _():
                job.phases[i](*job_refs)

    return carrier


def _pad_lanes(v, lo, total=128):
    return jnp.pad(v, (lo, total - lo - v.shape[0]))[None, :]


MIXER_WEIGHTS = ("w_in", "w_q_up", "w_kv_up", "conv_w")
LATE_WEIGHTS = ("w_out", "w_gate_up", "w_down")


def mixer_operands(g, sw):
    w_in = g["w_in"].reshape(D_IN, D_MODEL)
    zero = lambda rows: jnp.zeros((rows, D_MODEL), w_in.dtype)
    w_proj = jnp.concatenate(
        [w_in[:384], w_in[416:928], w_in[928:1952], w_in[1952:1960], zero(56), w_in[384:416], zero(32)], axis=0)
    wq = jnp.pad(g["w_q_up"], ((0, 0), (0, 0), (0, HEAD_LANES - NOPE - ROPE)))
    wk = jnp.pad(g["w_kv_up"][:, :, :NOPE], ((0, 0), (0, 0), (0, HEAD_LANES - NOPE)))
    wv = g["w_kv_up"][:, :, NOPE:]
    qkv = (sw["q_a_norm_w"][None, :], sw["kv_a_norm_w"][None, :], wq, wk, wv,
           _pad_lanes(jnp.concatenate([sw["q_nope_norm_w"], sw["q_pe_norm_w"]]), 0),
           _pad_lanes(sw["k_nope_norm_w"], 0), _pad_lanes(sw["k_pe_norm_w"], NOPE))
    conv_w = g["conv_w"].astype(f32).transpose(1, 0, 2).reshape(4, D_CONV)
    ssd = (conv_w, sw["conv_b"][None, :], _pad_lanes(sw["dt_bias"], 0), _pad_lanes(sw["a_log"], 0),
           _pad_lanes(sw["d_skip"], 0), sw["ssd_norm_w"][None, :])
    return dict(w_proj=w_proj, qkv=qkv, ssd=ssd, n1=sw["norm1_w"][None, :])


def late_operands(g, sw):
    return dict(wo=g["w_out"].reshape(D_MODEL, D_MODEL), wgu=g["w_gate_up"],
                wd=g["w_down"].reshape(N_DEV // 2, FF_SHARD, D_MODEL), n2=sw["norm2_w"][None, :])


def layer_fwd(x, mod, kw, cos_t, sin_t, job=None, late=None, target=None):
    sh1, sc1, g1, sh2, sc2, g2 = [mod[i:i + 1] for i in range(6)]
    pa, pz, px, plast = proj_fwd(x, kw["n1"], sh1, sc1, kw["w_proj"])
    q, k, v = qkv_fwd(pa, plast, cos_t, sin_t, kw["qkv"])
    (o, qx), carried = attn_fwd(q, k, v, job)
    if late is not None:
        kw = {**kw, **late(carried)}
    yg, states = ssd_fwd(px, pz, plast, kw["ssd"])
    x_mid = out_fwd(x, o, yg, g1, kw["wo"])
    x_out, mix, h_mid, gate, up, *loss_part = mlp_fwd(x_mid, kw["n2"], sh2, sc2, g2, kw["wgu"], kw["wd"], target)
    saved = dict(x=x, pa=pa, pz=pz, px=px, plast=plast, qx=qx, k=k, v=v, o=o, yg=yg, states=states, x_mid=x_mid,
                 mix=mix, h_mid=h_mid, gate=gate, up=up)
    return (x_out if target is None else (x_out, loss_part[0])), saved, kw, carried


def layer_bwd_head(dy, mod, kw, sv, job=None):
    _, _, g1, sh2, sc2, g2 = [mod[i:i + 1] for i in range(6)]
    (dhparts, dwg, dwu, dwd), carried = mlp_bwd(sv["h_mid"], dy, sv["gate"], sv["up"], g2, kw["wgu"], kw["wd"], job)
    dmid, dn2, dsh2, dsc2, do, dyg, dg1, dg2, dwo = out_bwd(
        dy, dhparts, sv["x_mid"], kw["n2"], sh2, sc2, sv["mix"], sv["o"], sv["yg"], g1, kw["wo"])
    early = dict(w_out=[dwo.reshape(N_DEV, D_MODEL // N_DEV, D_MODEL)], w_gate_up=[dwg, dwu],
                 w_down=[dwd.reshape(N_DEV, D_FF // N_DEV, D_MODEL)])
    head = dict(dmid=dmid, do=do, dyg=dyg, dn2=dn2, dsh2=dsh2, dsc2=dsc2, dg2=dg2, dg1=dg1)
    return head, early, carried


def layer_bwd_tail(hd, mod, kw, cos_t, sin_t, sv, job=None):
    sh1, sc1 = mod[0:1], mod[1:2]
    (dq, dk, dv), carried = attn_bwd(sv["qx"], sv["k"], sv["v"], hd["do"], job)
    dpx, dpz, dpl_dt, dcw, dcb, ddtb, dalog, ddskip, dsnw = ssd_bwd(sv["px"], sv["pz"], sv["plast"], sv["states"],
                                                                   hd["dyg"], kw["ssd"])
    dpa, dpl_k, dqaw, dkvaw, dwq, dwk, dwv, dqnw, dknw, dkpw = qkv_bwd(sv["pa"], sv["plast"], cos_t, sin_t, kw["qkv"],
                                                                       dq, dk, dv)
    dx, dn1, dsh1, dsc1, dwp = proj_bwd(sv["x"], kw["n1"], sh1, sc1, kw["w_proj"], dpa, dpz, dpx, dpl_k, dpl_dt, hd["dmid"])
    dmod = jnp.concatenate([dsh1, dsc1, hd["dg1"], hd["dsh2"], hd["dsc2"], hd["dg2"]], axis=0)
    dw_in = jnp.concatenate([dwp[:384], dwp[1984:2016], dwp[384:1920], dwp[1920:1928]], axis=0)
    grads = dict(
        norm1_w=dn1[0], norm2_w=hd["dn2"][0], q_a_norm_w=dqaw[0], kv_a_norm_w=dkvaw[0],
        q_nope_norm_w=dqnw[0, :NOPE], q_pe_norm_w=dqnw[0, NOPE:NOPE + ROPE], k_nope_norm_w=dknw[0, :NOPE],
        k_pe_norm_w=dkpw[0, NOPE:NOPE + ROPE], conv_b=dcb[0], dt_bias=ddtb[0, :N_HEADS], a_log=dalog[0, :N_HEADS],
        d_skip=ddskip[0, :N_HEADS], ssd_norm_w=dsnw[0],
        w_in=[dw_in.reshape(N_DEV, D_IN // N_DEV, D_MODEL)],
        w_q_up=[dwq[:, :, :NOPE + ROPE].astype(bf16)],
        w_kv_up=[jnp.concatenate([dwk[:, :, :NOPE], dwv], axis=2).astype(bf16)],
        conv_w=[dcw.reshape(4, N_DEV, D_CONV // N_DEV).transpose(1, 0, 2).astype(bf16)],
    )
    return dx, dmod, grads, carried


def _pack_small(get, last=None):
    flat = jnp.concatenate([get(name).reshape(-1) for name, _ in SMALL])
    flat = jnp.pad(flat, (0, SMALL_ROWS * 128 - flat.shape[0]))
    if last is not None:
        flat = flat.at[-1].set(last)
    return flat.reshape(SMALL_ROWS, 128)


def _unpack_small(packed):
    flat = packed.reshape(-1)
    out, off = {}, 0
    for name, size in SMALL:
        out[name] = flat[off:off + 2 * size].reshape(2, size)
        off += 2 * size
    return out


def kernel(x, c, positions, norm1_w, norm2_w, w_ada, b_ada, w_in, q_a_norm_w, w_q_up, kv_a_norm_w, w_kv_up, q_nope_norm_w, q_pe_norm_w, k_nope_norm_w, k_pe_norm_w, conv_w, conv_b, dt_bias, a_log, d_skip, ssd_norm_w, w_out, w_gate_up, w_down, loss_target, m_norm1_w, m_norm2_w, m_w_ada, m_b_ada, m_w_in, m_q_a_norm_w, m_w_q_up, m_kv_a_norm_w, m_w_kv_up, m_q_nope_norm_w, m_q_pe_norm_w, m_k_nope_norm_w, m_k_pe_norm_w, m_conv_w, m_conv_b, m_dt_bias, m_a_log, m_d_skip, m_ssd_norm_w, m_w_out, m_w_gate_up, m_w_down, v_norm1_w, v_norm2_w, v_w_ada, v_b_ada, v_w_in, v_q_a_norm_w, v_w_q_up, v_kv_a_norm_w, v_w_kv_up, v_q_nope_norm_w, v_q_pe_norm_w, v_k_nope_norm_w, v_k_pe_norm_w, v_conv_w, v_conv_b, v_dt_bias, v_a_log, v_d_skip, v_ssd_norm_w, v_w_out, v_w_gate_up, v_w_down):
    w = dict(norm1_w=norm1_w, norm2_w=norm2_w, w_ada=w_ada, b_ada=b_ada, w_in=w_in, q_a_norm_w=q_a_norm_w, w_q_up=w_q_up,
             kv_a_norm_w=kv_a_norm_w, w_kv_up=w_kv_up, q_nope_norm_w=q_nope_norm_w, q_pe_norm_w=q_pe_norm_w,
             k_nope_norm_w=k_nope_norm_w, k_pe_norm_w=k_pe_norm_w, conv_w=conv_w, conv_b=conv_b, dt_bias=dt_bias,
             a_log=a_log, d_skip=d_skip, ssd_norm_w=ssd_norm_w, w_out=w_out, w_gate_up=w_gate_up, w_down=w_down)
    m = dict(norm1_w=m_norm1_w, norm2_w=m_norm2_w, w_ada=m_w_ada, b_ada=m_b_ada, w_in=m_w_in, q_a_norm_w=m_q_a_norm_w,
             w_q_up=m_w_q_up, kv_a_norm_w=m_kv_a_norm_w, w_kv_up=m_w_kv_up, q_nope_norm_w=m_q_nope_norm_w,
             q_pe_norm_w=m_q_pe_norm_w, k_nope_norm_w=m_k_nope_norm_w, k_pe_norm_w=m_k_pe_norm_w, conv_w=m_conv_w,
             conv_b=m_conv_b, dt_bias=m_dt_bias, a_log=m_a_log, d_skip=m_d_skip, ssd_norm_w=m_ssd_norm_w, w_out=m_w_out,
             w_gate_up=m_w_gate_up, w_down=m_w_down)
    v = dict(norm1_w=v_norm1_w, norm2_w=v_norm2_w, w_ada=v_w_ada, b_ada=v_b_ada, w_in=v_w_in, q_a_norm_w=v_q_a_norm_w,
             w_q_up=v_w_q_up, kv_a_norm_w=v_kv_a_norm_w, w_kv_up=v_w_kv_up, q_nope_norm_w=v_q_nope_norm_w,
             q_pe_norm_w=v_q_pe_norm_w, k_nope_norm_w=v_k_nope_norm_w, k_pe_norm_w=v_k_pe_norm_w, conv_w=v_conv_w,
             conv_b=v_conv_b, dt_bias=v_dt_bias, a_log=v_a_log, d_skip=v_d_skip, ssd_norm_w=v_ssd_norm_w, w_out=v_w_out,
             w_gate_up=v_w_gate_up, w_down=v_w_down)
    me = _my_index()
    seq = x.shape[1]

    def shard(name, l):
        if name == "conv_w":
            return w[name][l]
        if name in TRANSPOSED:
            return jnp.swapaxes(w[name][l], 0, 1).astype(bf16)
        return w[name][l].astype(bf16)

    def shards(names, l):
        return [shard(name, l) for name in names]

    small = [{name: w[name][l] for name, _ in SMALL if name != "b_ada"} for l in range(2)]
    n_late = len(LATE_WEIGHTS)

    first = comm_call(gather_job([c] + shards(MIXER_WEIGHTS, 0)), "gather_first")
    c_all = first[0].reshape(N_DEV, D_MODEL)
    kws = [mixer_operands(dict(zip(MIXER_WEIGHTS, first[1:])), small[0]), None]

    b_cols = lax.dynamic_slice_in_dim(b_ada, me * 768, 768, axis=1)
    mod_cols = ada_fwd(c_all, w_ada, b_cols)
    (mod_all,) = comm_call(gather_job([mod_cols]), "gather_mod")
    mod_me = lax.dynamic_index_in_dim(mod_all, me, axis=2, keepdims=False)
    mods = [mod_me[:, l, :].reshape(6, D_MODEL) for l in range(2)]

    inv_freq = 1.0 / (ROPE_THETA ** (jnp.arange(0, ROPE, 2, dtype=f32) / ROPE))
    inv = _pad_lanes(jnp.concatenate([inv_freq, inv_freq]), NOPE)
    cos_t, sin_t = rope_tables(positions.reshape(seq, 1), inv)

    saved = [None, None]
    h, saved[0], kws[0], got = layer_fwd(
        x[0], mods[0], kws[0], cos_t, sin_t, gather_job(shards(LATE_WEIGHTS, 0) + shards(MIXER_WEIGHTS, 1)),
        lambda got: late_operands(dict(zip(LATE_WEIGHTS, got[:n_late])), small[0]))
    kws[1] = mixer_operands(dict(zip(MIXER_WEIGHTS, got[n_late:])), small[1])
    (dy, loss_part), saved[1], kws[1], _ = layer_fwd(
        h, mods[1], kws[1], cos_t, sin_t, gather_job(shards(LATE_WEIGHTS, 1)),
        lambda got: late_operands(dict(zip(LATE_WEIGHTS, got)), small[1]), loss_target[0])

    early, late = ("w_out", "w_gate_up", "w_down"), ("w_in", "w_q_up", "w_kv_up", "conv_w")
    parts = [{}, {}]
    head, pieces, _ = layer_bwd_head(dy, mods[1], kws[1], saved[1])
    dy, dmod1, grads1, got = layer_bwd_tail(head, mods[1], kws[1], cos_t, sin_t, saved[1], scatter_job([pieces[n] for n in early]))
    parts[1].update(zip(early, got))
    head, pieces, got = layer_bwd_head(dy, mods[0], kws[0], saved[0], scatter_job([grads1[n] for n in late]))
    parts[1].update(zip(late, got))
    dy, dmod0, grads0, got = layer_bwd_tail(head, mods[0], kws[0], cos_t, sin_t, saved[0], scatter_job([pieces[n] for n in early]))
    parts[0].update(zip(early, got))
    grad_x = dy[None]

    small_part = {name: jnp.stack([grads0[name], grads1[name]]) for name, _ in SMALL if name != "b_ada"}
    small_part["b_ada"] = jnp.stack([dmod0.reshape(-1), dmod1.reshape(-1)])
    last = comm_call(merge_jobs(scatter_job([grads0[n] for n in late]),
                                gather_job([_pack_small(lambda n: small_part[n], loss_part[0, 0])])), "exchange_last")
    parts[0].update(zip(late, last[:len(late)]))
    small_all = last[len(late)]
    packed = adamw(small_all, _pack_small(lambda n: w[n])[None], _pack_small(lambda n: m[n])[None],
                   _pack_small(lambda n: v[n])[None], 0, None, "adamw_small")
    loss = packed[0][0, -1, -1]
    res = {}
    for key, arr in zip("gdmv", packed):
        for name, val in _unpack_small(arr[0]).items():
            res[key, name] = val

    off = 2 * (1024 + 1024)
    dmod_all = small_all.reshape(N_DEV, -1)[:, off:off + 2 * 6144].reshape(N_DEV, 2, 6144)
    dmod_cols = lax.dynamic_slice_in_dim(dmod_all, me * 768, 768, axis=2).transpose(1, 0, 2)
    g_ada = ada_bwd(c_all, dmod_cols)
    out = None
    for l in range(2):
        out = adamw(g_ada[l][None], w_ada, m_w_ada, v_w_ada, l, out, "adamw_w_ada")
    res.update(zip([(key, "w_ada") for key in "gdmv"], out))

    inside = lambda a: jnp.transpose(a, (2, 0, 1))
    out = adamw_layers_inside([parts[l]["w_in"] for l in range(2)], inside(w_in), inside(m_w_in), inside(v_w_in), "adamw_w_in")
    res.update(zip([(key, "w_in") for key in "gdmv"], [jnp.transpose(a, (1, 2, 0)) for a in out]))
    for name in BIG:
        if name == "w_in":
            continue
        view = (lambda a: jnp.swapaxes(a, 1, 2)) if name in TRANSPOSED else (lambda a: a)
        out = None
        for l in range(2):
            out = adamw(parts[l][name], view(w[name]), view(m[name]), view(v[name]), l, out, "adamw_" + name)
        res.update(zip([(key, name) for key in "gdmv"], [view(a) for a in out]))

    return (loss, grad_x, *[res["g", n] for n in WEIGHTS], *[res["d", n] for n in WEIGHTS],
            *[res["m", n] for n in WEIGHTS], *[res["v", n] for n in WEIGHTS])
```

```python
import functools

import jax
import jax.numpy as jnp
from jax import lax
from jax.experimental import pallas as pl
from jax.experimental.pallas import tpu as pltpu

f32 = jnp.float32
bf16 = jnp.bfloat16

N_DEV = 8
D_MODEL = 1024
N_HEADS = 8
HEAD_LANES = 128
NOPE = 64
ROPE = 32
V_DIM = 64
Q_RANK = 256
KV_RANK = 128
D_SSD = 512
D_CONV = 1024
SSD_STATE = 128
SSD_HEAD_DIM = 64
CHUNK = 128
HALO = 8
D_FF = 2816
FF_SHARD = 704
D_IN = 1960
D_PROJ = 2048
EPS = 1e-6
LOG2E = 1.4426950408889634
LN2 = 0.6931471805599453
Q_SCALE = (NOPE + ROPE) ** -0.5 * LOG2E
SPARE_Q = NOPE + ROPE
SPARE_V = V_DIM
ATTN_ROWS_FWD = 256
ATTN_HEADS_FWD = 8
ATTN_HEADS_BWD = 4
MLP_FWD_ROWS = 1024
MLP_BWD_CHUNK = 256
MLP_BWD_ROWS = 1024
ROPE_THETA = 10000.0
NEG = -1e30

ADAM_LR = 0.001
ADAM_B1 = 0.9
ADAM_B2 = 0.999
ADAM_EPS = 1e-08
ADAM_WD = 0.01
ADAM_STEP = 10
ADAMW_BLOCK_BYTES = 36 << 20

MESH = pl.DeviceIdType.MESH
ANY = pl.BlockSpec(memory_space=pl.ANY)

SMALL = (("norm1_w", 1024), ("norm2_w", 1024), ("b_ada", 6144), ("q_a_norm_w", 256), ("kv_a_norm_w", 128),
         ("q_nope_norm_w", 64), ("q_pe_norm_w", 32), ("k_nope_norm_w", 64), ("k_pe_norm_w", 32),
         ("conv_b", 1024), ("dt_bias", 8), ("a_log", 8), ("d_skip", 8), ("ssd_norm_w", 512))
SMALL_ROWS = 168
BIG = ("w_in", "w_q_up", "w_kv_up", "conv_w", "w_out", "w_gate_up", "w_down")
TRANSPOSED = ("w_in", "w_gate_up")
WEIGHTS = ("norm1_w", "norm2_w", "w_ada", "b_ada", "w_in", "q_a_norm_w", "w_q_up", "kv_a_norm_w", "w_kv_up",
           "q_nope_norm_w", "q_pe_norm_w", "k_nope_norm_w", "k_pe_norm_w", "conv_w", "conv_b", "dt_bias",
           "a_log", "d_skip", "ssd_norm_w", "w_out", "w_gate_up", "w_down")


def _dot(a, b, ca, cb):
    return lax.dot_general(a.astype(bf16), b.astype(bf16), (((ca,), (cb,)), ((), ())), preferred_element_type=f32)


@jax.custom_vjp
def mm(a, b):
    return _dot(a, b, 1, 0)


def _mm_fwd(a, b):
    return _dot(a, b, 1, 0), (a, b)


def _mm_bwd(res, g):
    a, b = res
    return _dot(g, b, 1, 1).astype(a.dtype), _dot(a, g, 0, 0).astype(b.dtype)


mm.defvjp(_mm_fwd, _mm_bwd)


@jax.custom_vjp
def _mm_slot(a, w, slot):
    return _dot(a, w, 1, 0)


def _mm_slot_fwd(a, w, slot):
    return _dot(a, w, 1, 0), (a, w)


def _mm_slot_bwd(res, g):
    a, w = res
    return _dot(g, w, 1, 1).astype(a.dtype), None, _dot(a, g, 0, 0)


_mm_slot.defvjp(_mm_slot_fwd, _mm_slot_bwd)


def mmw(a, w, slot=None):
    return _dot(a, w, 1, 0) if slot is None else _mm_slot(a, w, slot)


@jax.custom_vjp
def _mm_slot_t(a, wt, slot):
    return _dot(a, wt, 1, 1)


def _mm_slot_t_fwd(a, wt, slot):
    return _dot(a, wt, 1, 1), (a, wt)


def _mm_slot_t_bwd(res, g):
    a, wt = res
    return _dot(g, wt, 1, 0).astype(a.dtype), None, _dot(g, a, 0, 0)


_mm_slot_t.defvjp(_mm_slot_t_fwd, _mm_slot_t_bwd)


def mmw_t(a, wt, slot=None):
    return _dot(a, wt, 1, 1) if slot is None else _mm_slot_t(a, wt, slot)


@jax.custom_vjp
def mm_nt(a, b):
    return _dot(a, b, 1, 1)


def _mm_nt_fwd(a, b):
    return _dot(a, b, 1, 1), (a, b)


def _mm_nt_bwd(res, g):
    a, b = res
    return _dot(g, b, 1, 0).astype(a.dtype), _dot(g, a, 0, 0).astype(b.dtype)


mm_nt.defvjp(_mm_nt_fwd, _mm_nt_bwd)


@jax.custom_vjp
def mm_tn(a, b):
    return _dot(a, b, 0, 0)


def _mm_tn_fwd(a, b):
    return _dot(a, b, 0, 0), (a, b)


def _mm_tn_bwd(res, g):
    a, b = res
    return _dot(b, g, 1, 1).astype(a.dtype), _dot(a, g, 1, 0).astype(b.dtype)


mm_tn.defvjp(_mm_tn_fwd, _mm_tn_bwd)


def _rms(x, w):
    return x * lax.rsqrt(jnp.mean(x * x, axis=-1, keepdims=True) + EPS) * w


def _const(shape):
    n = len(shape)
    return pl.BlockSpec(shape, lambda *_: (0,) * n)


def _accumulate(first, refs, vals):
    @pl.when(first)
    def _():
        for r, v in zip(refs, vals):
            r[...] = v

    @pl.when(jnp.logical_not(first))
    def _():
        for r, v in zip(refs, vals):
            r[...] += v


def _accumulate_then_cast(first, last, accs, outs, vals):
    _accumulate(first, accs, vals)

    @pl.when(last)
    def _():
        for a, o in zip(accs, outs):
            o[...] = a[...].astype(o.dtype)


def _token_block(s):
    return min(512, s)


def _f_proj(x, nw, sh, sc, w, slot=None):
    h = _rms(x, nw) * (1.0 + sc) + sh
    return mmw_t(h, w, slot)


def _f_qkv(pa, plast, cos_t, sin_t, qaw, kvaw, wq, wk, wv, qnw, knw, kpw, slots=None):
    sq, sk, sv = slots if slots is not None else ([None] * N_HEADS,) * 3
    lane = lax.broadcasted_iota(jnp.int32, (1, HEAD_LANES), 1)
    m_nope = lane < NOPE
    m_pe = (lane >= NOPE) & (lane < NOPE + ROPE)
    rows = pa.shape[0]

    def rope(t):
        half = ROPE // 2
        swapped = jnp.concatenate(
            [jnp.zeros((rows, NOPE), f32), t[:, NOPE + half:NOPE + ROPE], t[:, NOPE:NOPE + half],
             jnp.zeros((rows, HEAD_LANES - NOPE - ROPE), f32)], axis=1)
        return t * cos_t + swapped * sin_t

    qa = _rms(pa[:, :Q_RANK], qaw)
    kva = _rms(pa[:, Q_RANK:Q_RANK + KV_RANK], kvaw)
    kp = jnp.where(m_pe, plast, 0.0)
    kp = kp * lax.rsqrt(jnp.sum(kp * kp, axis=-1, keepdims=True) / ROPE + EPS) * kpw
    k_rot = rope(kp)
    qs, ks, vs = [], [], []
    for h in range(N_HEADS):
        qh = mmw(qa, wq[h], sq[h])
        ss_n = jnp.sum(jnp.where(m_nope, qh * qh, 0.0), axis=-1, keepdims=True) / NOPE
        ss_p = jnp.sum(jnp.where(m_pe, qh * qh, 0.0), axis=-1, keepdims=True) / ROPE
        r = jnp.where(m_nope, lax.rsqrt(ss_n + EPS), lax.rsqrt(ss_p + EPS))
        qs.append(rope(qh * r * qnw) * Q_SCALE)
        kh = mmw(kva, wk[h], sk[h])
        kh = kh * lax.rsqrt(jnp.sum(kh * kh, axis=-1, keepdims=True) / NOPE + EPS) * knw
        ks.append(kh + k_rot)
        vs.append(mmw(kva, wv[h], sv[h]))
    return jnp.stack(qs), jnp.stack(ks), jnp.stack(vs)


def _f_ssd(xext, z, plast, prev, cw, cb, dtb, alog, dskip, snw):
    n = CHUNK
    conv = cb
    for k in range(4):
        conv = conv + cw[k:k + 1] * xext[HALO - 3 + k:HALO - 3 + k + n]
    xc = jax.nn.silu(conv)
    xs, bm, cm = xc[:, :D_SSD], xc[:, D_SSD:D_SSD + 2 * SSD_STATE], xc[:, D_SSD + 2 * SSD_STATE:]
    lane = lax.broadcasted_iota(jnp.int32, (1, 128), 1)
    dt = jax.nn.softplus(jnp.where(lane < N_HEADS, plast, 0.0) + dtb)
    adt = dt * (-jnp.exp(alog))
    row = lax.broadcasted_iota(jnp.int32, (n, n), 0)
    col = lax.broadcasted_iota(jnp.int32, (n, n), 1)
    tri = row >= col
    acs = jnp.dot(tri.astype(f32), adt, precision=lax.Precision.HIGHEST, preferred_element_type=f32)
    acs_t = acs.T
    bgs = [bm[:, g * SSD_STATE:(g + 1) * SSD_STATE] for g in range(2)]
    cgs = [cm[:, g * SSD_STATE:(g + 1) * SSD_STATE] for g in range(2)]
    cb_ts = [mm_nt(cgs[g], bgs[g]) for g in range(2)]
    low = lane < SSD_HEAD_DIM
    low_rows = lax.broadcasted_iota(jnp.int32, (2 * SSD_HEAD_DIM, 1), 0) < SSD_HEAD_DIM

    def both(a0, a1):
        return jnp.where(low, a0, a1)

    pre = []
    for i in range(N_HEADS // 2):
        h0, h1 = 2 * i, 2 * i + 1
        col0, col1 = acs[:, h0:h0 + 1], acs[:, h1:h1 + 1]
        last0, last1 = acs[n - 1:n, h0:h0 + 1], acs[n - 1:n, h1:h1 + 1]
        cb_t = cb_ts[i // 2]
        scores0 = cb_t * jnp.exp(jnp.where(tri, col0 - acs_t[h0:h0 + 1, :], -jnp.inf))
        scores1 = cb_t * jnp.exp(jnp.where(tri, col1 - acs_t[h1:h1 + 1, :], -jnp.inf))
        xp = xs[:, i * 128:(i + 1) * 128]
        xdt = xp * both(dt[:, h0:h0 + 1], dt[:, h1:h1 + 1])
        weighted = xdt * both(jnp.exp(last0 - col0), jnp.exp(last1 - col1))
        chunk_decay = jnp.where(low_rows, jnp.exp(last0), jnp.exp(last1))
        in_decay = both(jnp.exp(col0), jnp.exp(col1))
        skip = both(dskip[:, h0:h0 + 1], dskip[:, h1:h1 + 1]) * xp
        pre.append((scores0, scores1, xdt, weighted, chunk_decay, in_decay, skip))
    prods = []
    for i in range(N_HEADS // 2):
        scores0, scores1, xdt, weighted, _, _, _ = pre[i]
        g = i // 2
        y_diag = mm(scores0, jnp.where(low, xdt, 0.0)) + mm(scores1, jnp.where(low, 0.0, xdt))
        prods.append((y_diag, mm_tn(weighted, bgs[g]), mm_nt(cgs[g], prev[i])))
    ys, news = [], []
    for i in range(N_HEADS // 2):
        y_diag, st, y_off = prods[i]
        _, _, _, _, chunk_decay, in_decay, skip = pre[i]
        news.append(chunk_decay * prev[i] + st)
        ys.append(y_diag + y_off * in_decay + skip)
    y = jnp.concatenate(ys, axis=1)
    yg = y * jax.nn.silu(z)
    half = D_SSD // 2
    outs = []
    for g in range(2):
        t = yg[:, g * half:(g + 1) * half]
        outs.append(t * lax.rsqrt(jnp.mean(t * t, axis=-1, keepdims=True) + EPS))
    return jnp.concatenate(outs, axis=1) * snw, jnp.stack(news)


def _f_out(o, yg, g1, wo, slot=None):
    cat = jnp.concatenate([o[h] for h in range(N_HEADS)] + [yg], axis=1)
    return g1 * mmw(cat, wo, slot)


def _f_modulate(x, nw, sh, sc):
    return _rms(x, nw) * (1.0 + sc) + sh


def proj_fwd(x, nw, sh, sc, w):
    s = x.shape[0]
    ts = _token_block(s)

    def body(x_ref, nw_ref, sh_ref, sc_ref, w_ref, pa_ref, pz_ref, px_ref, pl_ref):
        p = _f_proj(x_ref[...], nw_ref[...], sh_ref[...], sc_ref[...], w_ref[...])
        pa_ref[...] = p[:, :384]
        pz_ref[...] = p[:, 384:896]
        px_ref[...] = p[:, 896:1920]
        pl_ref[...] = p[:, 1920:]

    vec = _const((1, D_MODEL))
    return pl.pallas_call(
        body, name="proj_fwd", grid=(s // ts,),
        in_specs=[pl.BlockSpec((ts, D_MODEL), lambda i: (i, 0)), vec, vec, vec, _const((D_PROJ, D_MODEL))],
        out_specs=[pl.BlockSpec((ts, 384), lambda i: (i, 0)), pl.BlockSpec((ts, 512), lambda i: (i, 0)),
                   pl.BlockSpec((ts, 1024), lambda i: (i, 0)), pl.BlockSpec((ts, 128), lambda i: (i, 0))],
        out_shape=[jax.ShapeDtypeStruct((s, 384), f32), jax.ShapeDtypeStruct((s, 512), f32),
                   jax.ShapeDtypeStruct((s, 1024), f32), jax.ShapeDtypeStruct((s, 128), f32)],
    )(x, nw, sh, sc, w)


def rope_tables(pos, inv):
    s = pos.shape[0]
    ts = _token_block(s)

    def body(pos_ref, inv_ref, cos_ref, sin_ref):
        ang = pos_ref[...].astype(f32) * inv_ref[...]
        lane = lax.broadcasted_iota(jnp.int32, (1, HEAD_LANES), 1)
        half = ROPE // 2
        cos_ref[...] = jnp.where(lane < NOPE, 1.0, jnp.where(lane < NOPE + ROPE, jnp.cos(ang), 0.0))
        sn = jnp.sin(ang)
        sin_ref[...] = jnp.where((lane >= NOPE) & (lane < NOPE + half), -sn,
                                 jnp.where((lane >= NOPE + half) & (lane < NOPE + ROPE), sn, 0.0))

    return pl.pallas_call(
        body, name="rope_tables", grid=(s // ts,),
        in_specs=[pl.BlockSpec((ts, 1), lambda i: (i, 0)), _const((1, HEAD_LANES))],
        out_specs=[pl.BlockSpec((ts, HEAD_LANES), lambda i: (i, 0))] * 2,
        out_shape=[jax.ShapeDtypeStruct((s, HEAD_LANES), f32)] * 2,
    )(pos, inv)


def _qkv_param_specs():
    return [_const((1, Q_RANK)), _const((1, KV_RANK)), _const((N_HEADS, Q_RANK, HEAD_LANES)),
            _const((N_HEADS, KV_RANK, HEAD_LANES)), _const((N_HEADS, KV_RANK, V_DIM)),
            _const((1, HEAD_LANES)), _const((1, HEAD_LANES)), _const((1, HEAD_LANES))]


def qkv_fwd(pa, plast, cos_t, sin_t, params):
    s = pa.shape[0]
    ts = _token_block(s)

    def body(pa_ref, pl_ref, cos_ref, sin_ref, *rest):
        prm = [r[...] for r in rest[:8]]
        q_ref, k_ref, v_ref = rest[8:]
        q, k, v = _f_qkv(pa_ref[...], pl_ref[...], cos_ref[...], sin_ref[...], *prm)
        q_ref[...] = q.astype(bf16)
        lane = lax.broadcasted_iota(jnp.int32, (1, 1, HEAD_LANES), 2)
        k_ref[...] = jnp.where((lane == SPARE_Q) | (lane == SPARE_Q + 1), 1.0, k).astype(bf16)
        v_ref[...] = jnp.concatenate([v, jnp.ones_like(v)], axis=-1).astype(bf16)

    tok = lambda w: pl.BlockSpec((ts, w), lambda i: (i, 0))
    head = pl.BlockSpec((N_HEADS, ts, HEAD_LANES), lambda i: (0, i, 0))
    return pl.pallas_call(
        body, name="qkv_fwd", grid=(s // ts,),
        in_specs=[tok(384), tok(128), tok(128), tok(128)] + _qkv_param_specs(),
        out_specs=[head] * 3, out_shape=[jax.ShapeDtypeStruct((N_HEADS, s, HEAD_LANES), bf16)] * 3,
    )(pa, plast, cos_t, sin_t, *params)


def _scores(q, k):
    return lax.dot_general(q, k, (((1,), (1,)), ((), ())), preferred_element_type=f32)


def _tril(rows, cols, row_offset):
    row = row_offset + lax.broadcasted_iota(jnp.int32, (rows, cols), 0)
    col = lax.broadcasted_iota(jnp.int32, (rows, cols), 1)
    return row >= col


def _call_with_job(body, name, grid, job, in_specs, out_specs, out_shape, scratch_shapes, operands, relay_at=None):
    if job is None:
        res = pl.pallas_call(body, name=name, grid=grid, in_specs=in_specs, out_specs=out_specs, out_shape=out_shape,
                             scratch_shapes=scratch_shapes)(*operands)
        return res, None

    def at_step(i, n):
        if i == 0:
            want = [0] * len(grid)
        elif i == n - 1:
            want = [g - 1 for g in grid]
        else:
            want = relay_at
        return functools.reduce(jnp.logical_and, [pl.program_id(a) == s for a, s in enumerate(want)])

    carrier = _carry(job, body, len(in_specs), len(out_specs), at_step)
    res = pl.pallas_call(
        carrier, name=name, grid=grid,
        in_specs=list(in_specs) + [ANY] * len(job.operands), out_specs=list(out_specs) + [ANY] * len(job.out_shape),
        out_shape=list(out_shape) + list(job.out_shape), scratch_shapes=list(scratch_shapes) + job.scratch,
    )(*operands, *job.operands)
    return res[:len(out_specs)], res[len(out_specs):]


def attn_fwd(q, k, v, job=None):
    s = q.shape[1]
    t = _token_block(s)
    nb = s // t

    rb = min(ATTN_ROWS_FWD, t)

    hp = ATTN_HEADS_FWD

    def body(q_ref, k_ref, v_ref, o_ref, qx_ref, m_sc, acc_sc):
        qi = pl.program_id(1)
        m_sc[...] = jnp.full(m_sc.shape, NEG, f32)
        acc_sc[...] = jnp.zeros(acc_sc.shape, f32)

        def step(k0, diagonal):
            chains = [(hh, r) for hh in range(hp) for r in range(t // rb)]

            def scores(hh, r):
                nk = (r + 1) * rb if diagonal else t
                sc = _scores(q_ref[hh, pl.ds(r * rb, rb), :], k_ref[hh, pl.ds(k0, nk), :])
                return jnp.where(_tril(rb, nk, r * rb), sc, NEG) if diagonal else sc

            ahead = scores(*chains[0])
            for c, (hh, r) in enumerate(chains):
                sc = ahead
                if c + 1 < len(chains):
                    ahead = scores(*chains[c + 1])
                rows = pl.ds(r * rb, rb)
                keys = pl.ds(k0, (r + 1) * rb if diagonal else t)
                m_prev = m_sc[hh, rows, :1]
                m_new = jnp.maximum(m_prev, jnp.max(sc, axis=-1, keepdims=True))
                p = jnp.exp2(sc - m_new)
                alpha = jnp.exp2(m_prev - m_new)
                acc = alpha * acc_sc[hh, rows, :] + jnp.dot(p.astype(bf16), v_ref[hh, keys, :], preferred_element_type=f32)
                if diagonal:
                    l = acc[:, V_DIM:V_DIM + 1]
                    o_ref[hh, rows, :] = acc[:, :V_DIM] / l
                    lse = m_new + jnp.log2(l)
                    high = lse.astype(bf16)
                    low = (lse - high.astype(f32)).astype(bf16)
                    lane = lax.broadcasted_iota(jnp.int32, (1, HEAD_LANES), 1)
                    qx_ref[hh, rows, :] = jnp.where(lane == SPARE_Q, -high,
                                                    jnp.where(lane == SPARE_Q + 1, -low, q_ref[hh, rows, :]))
                else:
                    acc_sc[hh, rows, :] = acc
                    m_sc[hh, rows, :] = jnp.broadcast_to(m_new, (rb, 128))

        def below(ki, carry):
            step(pl.multiple_of(ki * t, t), False)
            return carry

        lax.fori_loop(0, qi, below, 0)
        step(pl.multiple_of(qi * t, t), True)

    return _call_with_job(
        body, "attn_fwd" if job is None else "attn_fwd_comm", (N_HEADS // hp, nb), job,
        in_specs=[pl.BlockSpec((hp, t, HEAD_LANES), lambda h, qi: (h, qi, 0)),
                  pl.BlockSpec((hp, s, HEAD_LANES), lambda h, qi: (h, 0, 0)),
                  pl.BlockSpec((hp, s, HEAD_LANES), lambda h, qi: (h, 0, 0))],
        out_specs=[pl.BlockSpec((hp, t, V_DIM), lambda h, qi: (h, qi, 0)),
                   pl.BlockSpec((hp, t, HEAD_LANES), lambda h, qi: (h, qi, 0))],
        out_shape=[jax.ShapeDtypeStruct((N_HEADS, s, V_DIM), f32), jax.ShapeDtypeStruct((N_HEADS, s, HEAD_LANES), bf16)],
        scratch_shapes=[pltpu.VMEM((hp, t, 128), f32), pltpu.VMEM((hp, t, HEAD_LANES), f32)],
        operands=(q, k, v), relay_at=(N_HEADS // hp - 1, max(nb - 2, 0)))


def _ssd_param_specs():
    return [_const((4, D_CONV)), _const((1, D_CONV)), _const((1, 128)), _const((1, 128)), _const((1, 128)),
            _const((1, D_SSD))]


def ssd_fwd(px, pz, plast, params):
    s = px.shape[0]
    nc = s // CHUNK

    def body(px_ref, pz_ref, pl_ref, cw_ref, cb_ref, dtb_ref, alog_ref, dskip_ref, snw_ref, yg_ref, st_ref,
             state_sc, halo_sc):
        i = pl.program_id(0)

        @pl.when(i == 0)
        def _():
            state_sc[...] = jnp.zeros(state_sc.shape, f32)
            halo_sc[...] = jnp.zeros(halo_sc.shape, f32)

        x = px_ref[...]
        prev = state_sc[...]
        st_ref[...] = prev
        xext = jnp.concatenate([halo_sc[...], x], axis=0)
        yg, new = _f_ssd(xext, pz_ref[...], pl_ref[...], prev, cw_ref[...], cb_ref[...], dtb_ref[...],
                         alog_ref[...], dskip_ref[...], snw_ref[...])
        yg_ref[...] = yg
        state_sc[...] = new
        halo_sc[...] = x[CHUNK - HALO:]

    tok = lambda w: pl.BlockSpec((CHUNK, w), lambda i: (i, 0))
    return pl.pallas_call(
        body, name="ssd_fwd", grid=(nc,),
        in_specs=[tok(D_CONV), tok(D_SSD), tok(128)] + _ssd_param_specs(),
        out_specs=[tok(D_SSD), pl.BlockSpec((None, N_HEADS // 2, 2 * SSD_HEAD_DIM, SSD_STATE), lambda i: (i, 0, 0, 0))],
        out_shape=[jax.ShapeDtypeStruct((s, D_SSD), f32),
                   jax.ShapeDtypeStruct((nc, N_HEADS // 2, 2 * SSD_HEAD_DIM, SSD_STATE), f32)],
        scratch_shapes=[pltpu.VMEM((N_HEADS // 2, 2 * SSD_HEAD_DIM, SSD_STATE), f32), pltpu.VMEM((HALO, D_CONV), f32)],
    )(px, pz, plast, *params)


def out_fwd(x, o, yg, g1, wo):
    s = x.shape[0]
    ts = _token_block(s)

    def body(x_ref, o_ref, yg_ref, g1_ref, wo_ref, out_ref):
        out_ref[...] = x_ref[...] + _f_out(o_ref[...], yg_ref[...], g1_ref[...], wo_ref[...])

    return pl.pallas_call(
        body, name="out_fwd", grid=(s // ts,),
        in_specs=[pl.BlockSpec((ts, D_MODEL), lambda i: (i, 0)), pl.BlockSpec((N_HEADS, ts, V_DIM), lambda i: (0, i, 0)),
                  pl.BlockSpec((ts, D_SSD), lambda i: (i, 0)), _const((1, D_MODEL)), _const((D_MODEL, D_MODEL))],
        out_specs=pl.BlockSpec((ts, D_MODEL), lambda i: (i, 0)),
        out_shape=jax.ShapeDtypeStruct((s, D_MODEL), f32),
    )(x, o, yg, g1, wo)


def mlp_fwd(x, nw, sh, sc, g2, wgu, wd, target=None):
    s = x.shape[0]
    ts = min(MLP_FWD_ROWS, s)
    nj = N_DEV // 2

    def body(x_ref, nw_ref, sh_ref, sc_ref, g2_ref, wg_ref, wu_ref, wd_ref, *rest):
        if target is None:
            out_ref, mix_ref, h_ref, gate_ref, up_ref = rest
        else:
            t_ref, out_ref, mix_ref, h_ref, gate_ref, up_ref, loss_ref = rest
        j = pl.program_id(1)
        first_block = pl.program_id(0) == 0

        @pl.when(j == 0)
        def _():
            h_ref[...] = _f_modulate(x_ref[...], nw_ref[...], sh_ref[...], sc_ref[...]).astype(bf16)
            mix_ref[...] = jnp.zeros(mix_ref.shape, f32)

        nr = max(ts // 512, 1)
        half = ts // nr
        wg, wu, wd = wg_ref[...], wu_ref[...], wd_ref[...]
        products = lambda r: (mmw_t(h_ref[pl.ds(r * half, half), :], wg), mmw_t(h_ref[pl.ds(r * half, half), :], wu))
        ahead = products(0)
        for r in range(nr):
            gate, up = ahead
            if r + 1 < nr:
                ahead = products(r + 1)
            rows = pl.ds(r * half, half)
            gate_ref[rows, :] = gate.astype(bf16)
            up_ref[rows, :] = up.astype(bf16)
            mix_ref[rows, :] += mmw(jax.nn.silu(gate) * up, wd)

        @pl.when(j == nj - 1)
        def _():
            y = x_ref[...] + g2_ref[...] * mix_ref[...]
            if target is None:
                out_ref[...] = y
            else:
                d = y - t_ref[...]
                out_ref[...] = d * (1.0 / D_MODEL)
                part = 0.5 * jnp.sum(jnp.sum(d * d, axis=-1, keepdims=True) * (1.0 / D_MODEL), axis=0, keepdims=True)
                _accumulate(first_block, [loss_ref], [jnp.broadcast_to(part, (8, 128))])

    vec = _const((1, D_MODEL))
    tok = pl.BlockSpec((ts, D_MODEL), lambda i, j: (i, 0))
    wide = pl.BlockSpec((None, ts, FF_SHARD), lambda i, j: (j, i, 0))
    last = target is not None
    return pl.pallas_call(
        body, name="mlp_fwd_loss" if last else "mlp_fwd", grid=(s // ts, nj),
        in_specs=[tok, vec, vec, vec, vec,
                  pl.BlockSpec((None, FF_SHARD, D_MODEL), lambda i, j: (j, 0, 0)),
                  pl.BlockSpec((None, FF_SHARD, D_MODEL), lambda i, j: (j + nj, 0, 0)),
                  pl.BlockSpec((None, FF_SHARD, D_MODEL), lambda i, j: (j, 0, 0))] + [tok] * last,
        out_specs=[tok] * 3 + [wide] * 2 + [_const((8, 128))] * last,
        out_shape=[jax.ShapeDtypeStruct((s, D_MODEL), f32), jax.ShapeDtypeStruct((s, D_MODEL), f32),
                   jax.ShapeDtypeStruct((s, D_MODEL), bf16)] + [jax.ShapeDtypeStruct((nj, s, FF_SHARD), bf16)] * 2
                  + [jax.ShapeDtypeStruct((8, 128), f32)] * last,
    )(x, nw, sh, sc, g2, wgu, wgu, wd, *([target] if last else []))


def mlp_bwd(h, dy, gate, up, g2, wgu, wd, job=None):
    s = h.shape[0]
    ts = min(MLP_BWD_ROWS, s)
    nj = N_DEV // 2
    ni = s // ts

    rows_per = min(MLP_BWD_CHUNK, ts)

    def body(h_ref, dy_ref, gate_ref, up_ref, g2_ref, wg_ref, wu_ref, wd_ref, dh_ref, dwg_ref, dwu_ref, dwd_ref,
             ag_sc, au_sc, ad_sc, act_sc, dgate_sc, dup_sc, dmix_sc):
        i = pl.program_id(1)
        wg, wu, wd = wg_ref[...], wu_ref[...], wd_ref[...]
        g2 = g2_ref[...]
        for r in range(ts // rows_per):
            rows = pl.ds(r * rows_per, rows_per)
            act, vjp = jax.vjp(lambda g, u: jax.nn.silu(g) * u, gate_ref[rows, :].astype(f32), up_ref[rows, :].astype(f32))
            dmix = (dy_ref[rows, :] * g2).astype(bf16)
            dgate, dup = vjp(_dot(dmix, wd, 1, 1))
            dgate, dup = dgate.astype(bf16), dup.astype(bf16)
            dh_ref[rows, :] = (_dot(dgate, wg, 1, 0) + _dot(dup, wu, 1, 0)).astype(bf16)
            act_sc[rows, :] = act.astype(bf16)
            dgate_sc[rows, :] = dgate
            dup_sc[rows, :] = dup
            dmix_sc[rows, :] = dmix
        h = h_ref[...]
        grads = [_dot(dgate_sc[...], h, 0, 0), _dot(dup_sc[...], h, 0, 0), _dot(act_sc[...], dmix_sc[...], 0, 0)]
        _accumulate_then_cast(i == 0, i == ni - 1, [ag_sc, au_sc, ad_sc], [dwg_ref, dwu_ref, dwd_ref], grads)

    once = pl.Buffered(1)
    wspec = lambda off: pl.BlockSpec((None, FF_SHARD, D_MODEL), lambda j, i: (j + off, 0, 0), pipeline_mode=once)
    dspec = pl.BlockSpec((None, FF_SHARD, D_MODEL), lambda j, i: (j, 0, 0), pipeline_mode=once)
    wide = pl.BlockSpec((None, ts, FF_SHARD), lambda j, i: (j, i, 0))
    return _call_with_job(
        body, "mlp_bwd" if job is None else "mlp_bwd_comm", (nj, ni), job,
        in_specs=[pl.BlockSpec((ts, D_MODEL), lambda j, i: (i, 0)), pl.BlockSpec((ts, D_MODEL), lambda j, i: (i, 0)),
                  wide, wide, _const((1, D_MODEL)), wspec(0), wspec(nj), dspec],
        out_specs=[pl.BlockSpec((None, ts, D_MODEL), lambda j, i: (j, i, 0)), wspec(0), wspec(0), dspec],
        out_shape=[jax.ShapeDtypeStruct((nj, s, D_MODEL), bf16),
                   jax.ShapeDtypeStruct((nj, FF_SHARD, D_MODEL), bf16), jax.ShapeDtypeStruct((nj, FF_SHARD, D_MODEL), bf16),
                   jax.ShapeDtypeStruct((nj, FF_SHARD, D_MODEL), bf16)],
        scratch_shapes=[pltpu.VMEM((FF_SHARD, D_MODEL), f32), pltpu.VMEM((FF_SHARD, D_MODEL), f32),
                        pltpu.VMEM((FF_SHARD, D_MODEL), f32), pltpu.VMEM((ts, FF_SHARD), bf16),
                        pltpu.VMEM((ts, FF_SHARD), bf16), pltpu.VMEM((ts, FF_SHARD), bf16), pltpu.VMEM((ts, D_MODEL), bf16)],
        operands=(h, dy, gate, up, g2, wgu, wgu, wd))


def out_bwd(dy, dhparts, x, nw, sh, sc, mix, o, yg, g1, wo):
    s = dy.shape[0]
    ts = _token_block(s)
    nj = dhparts.shape[0]

    ni = s // ts

    def body(dy_ref, dp_ref, x_ref, nw_ref, sh_ref, sc_ref, mix_ref, o_ref, yg_ref, g1_ref, wo_ref,
             dx_ref, dnw_ref, dsh_ref, dsc_ref, do_ref, dyg_ref, dg1_ref, dg2_ref, dwo_ref, acc_sc):
        i = pl.program_id(0)
        g = dy_ref[...]
        _accumulate(i == 0, [dg2_ref], [jnp.sum(g * mix_ref[...], axis=0, keepdims=True)])
        dh = dp_ref[0].astype(f32)
        for j in range(1, nj):
            dh = dh + dp_ref[j].astype(f32)
        _, vjp_mod = jax.vjp(_f_modulate, x_ref[...], nw_ref[...], sh_ref[...], sc_ref[...])
        dx_mod, dnw, dsh, dsc = vjp_mod(dh)
        _accumulate(i == 0, [dnw_ref, dsh_ref, dsc_ref], [dnw, dsh, dsc])
        g = g + dx_mod
        dx_ref[...] = g
        o = o_ref[...]
        wo = wo_ref[...]
        _, vjp = jax.vjp(lambda o_, yg_, g1_, slot: _f_out(o_, yg_, g1_, wo, slot), o, yg_ref[...], g1_ref[...],
                         jnp.zeros(wo.shape, f32))
        do, dyg, dg1, dwo = vjp(g)
        delta = jnp.sum(do * o, axis=-1, keepdims=True)
        high = delta.astype(bf16)
        low = (delta - high.astype(f32)).astype(bf16)
        lane = lax.broadcasted_iota(jnp.int32, (1, 1, HEAD_LANES), 2)
        wide = jnp.concatenate([do.astype(bf16), jnp.zeros(do.shape, bf16)], axis=-1)
        do_ref[...] = jnp.where(lane == SPARE_V, -high, jnp.where(lane == SPARE_V + 1, -low, wide))
        dyg_ref[...] = dyg
        _accumulate(i == 0, [dg1_ref], [dg1])
        _accumulate_then_cast(i == 0, i == ni - 1, [acc_sc], [dwo_ref], [dwo])

    head = pl.BlockSpec((N_HEADS, ts, V_DIM), lambda i: (0, i, 0))
    tok = pl.BlockSpec((ts, D_MODEL), lambda i: (i, 0))
    vec = _const((1, D_MODEL))
    vshape = jax.ShapeDtypeStruct((1, D_MODEL), f32)
    return pl.pallas_call(
        body, name="out_bwd", grid=(ni,), scratch_shapes=[pltpu.VMEM((D_MODEL, D_MODEL), f32)],
        in_specs=[tok, pl.BlockSpec((nj, ts, D_MODEL), lambda i: (0, i, 0)), tok, vec, vec, vec, tok,
                  head, pl.BlockSpec((ts, D_SSD), lambda i: (i, 0)), vec, _const((D_MODEL, D_MODEL))],
        out_specs=[tok, vec, vec, vec, pl.BlockSpec((N_HEADS, ts, HEAD_LANES), lambda i: (0, i, 0)),
                   pl.BlockSpec((ts, D_SSD), lambda i: (i, 0)), vec, vec, _const((D_MODEL, D_MODEL))],
        out_shape=[jax.ShapeDtypeStruct((s, D_MODEL), f32), vshape, vshape, vshape,
                   jax.ShapeDtypeStruct((N_HEADS, s, HEAD_LANES), bf16), jax.ShapeDtypeStruct((s, D_SSD), f32),
                   vshape, vshape, jax.ShapeDtypeStruct((D_MODEL, D_MODEL), bf16)],
    )(dy, dhparts, x, nw, sh, sc, mix, o, yg, g1, wo)


def attn_bwd(qx, k, v, do, job=None):
    s = qx.shape[1]
    t = _token_block(s)
    nb = s // t

    hp = ATTN_HEADS_BWD

    def body(q_ref, k_ref, v_ref, do_ref, dq_ref, dk_ref, dv_ref, dv_sc):
        ki = pl.program_id(1)

        @pl.when(ki == 0)
        def _():
            dq_ref[...] = jnp.zeros(dq_ref.shape, f32)

        dk_ref[...] = jnp.zeros(dk_ref.shape, f32)
        dv_sc[...] = jnp.zeros(dv_sc.shape, f32)

        def step(q0, diagonal):
            half = t // 2
            subs = [(0, half, half), (half, half, t)] if diagonal and half % 128 == 0 else [(0, t, t)]
            chains = [(hh, sub) for hh in range(hp) for sub in subs]

            def products(hh, sub):
                r0, nr, nk = sub
                rows = pl.ds(q0 + r0, nr)
                sc = _scores(q_ref[hh, rows, :], k_ref[hh, :nk, :])
                dps = _scores(do_ref[hh, rows, :], v_ref[hh, :nk, :])
                return (jnp.where(_tril(nr, nk, r0), sc, NEG) if diagonal else sc), dps

            ahead = products(*chains[0])
            for c, (hh, (r0, nr, nk)) in enumerate(chains):
                sc, dps = ahead
                if c + 1 < len(chains):
                    ahead = products(*chains[c + 1])
                rows = pl.ds(q0 + r0, nr)
                p = jnp.exp2(sc)
                ds = (p * dps).astype(bf16)
                dv_sc[hh, :nk, :] += lax.dot_general(p.astype(bf16), do_ref[hh, rows, :], (((0,), (0,)), ((), ())),
                                                     preferred_element_type=f32)
                dk_ref[hh, :nk, :] += lax.dot_general(ds, q_ref[hh, rows, :], (((0,), (0,)), ((), ())),
                                                      preferred_element_type=f32)
                dq_ref[hh, rows, :] += jnp.dot(ds, k_ref[hh, :nk, :], preferred_element_type=f32)

        step(pl.multiple_of(ki * t, t), True)

        def above(qi, carry):
            step(pl.multiple_of(qi * t, t), False)
            return carry

        lax.fori_loop(ki + 1, nb, above, 0)
        real = lax.broadcasted_iota(jnp.int32, (1, 1, HEAD_LANES), 2) < SPARE_Q
        dk_ref[...] = jnp.where(real, dk_ref[...] * LN2, 0.0)
        dv_ref[...] = dv_sc[:, :, :V_DIM]

        @pl.when(ki == nb - 1)
        def _():
            dq_ref[...] = jnp.where(real, dq_ref[...] * LN2, 0.0)

    qspec = pl.BlockSpec((hp, s, HEAD_LANES), lambda h, ki: (h, 0, 0))
    kspec = lambda w: pl.BlockSpec((hp, t, w), lambda h, ki: (h, ki, 0))
    return _call_with_job(
        body, "attn_bwd" if job is None else "attn_bwd_comm", (N_HEADS // hp, nb), job,
        in_specs=[qspec, kspec(HEAD_LANES), kspec(HEAD_LANES), qspec],
        out_specs=[qspec, kspec(HEAD_LANES), kspec(V_DIM)],
        out_shape=[jax.ShapeDtypeStruct((N_HEADS, s, HEAD_LANES), f32), jax.ShapeDtypeStruct((N_HEADS, s, HEAD_LANES), f32),
                   jax.ShapeDtypeStruct((N_HEADS, s, V_DIM), f32)],
        scratch_shapes=[pltpu.VMEM((hp, t, HEAD_LANES), f32)], operands=(qx, k, v, do))


def ssd_bwd(px, pz, plast, states, dyg, params):
    s = px.shape[0]
    nc = s // CHUNK
    per = CHUNK // HALO

    def body(px_ref, halo_ref, pz_ref, pl_ref, st_ref, dyg_ref, cw_ref, cb_ref, dtb_ref, alog_ref, dskip_ref, snw_ref,
             dpx_ref, dpz_ref, dpl_ref, dcw_ref, dcb_ref, ddtb_ref, dalog_ref, ddskip_ref, dsnw_ref, dstate_sc, dhalo_sc):
        t = pl.program_id(0)
        chunk = nc - 1 - t

        @pl.when(t == 0)
        def _():
            dstate_sc[...] = jnp.zeros(dstate_sc.shape, f32)
            dhalo_sc[...] = jnp.zeros(dhalo_sc.shape, f32)

        halo = jnp.where(chunk > 0, halo_ref[...], 0.0)
        xext = jnp.concatenate([halo, px_ref[...]], axis=0)
        _, vjp = jax.vjp(_f_ssd, xext, pz_ref[...], pl_ref[...], st_ref[...], cw_ref[...], cb_ref[...], dtb_ref[...],
                         alog_ref[...], dskip_ref[...], snw_ref[...])
        dxext, dz, dpl, dprev, dcw, dcb, ddtb, dalog, ddskip, dsnw = vjp((dyg_ref[...], dstate_sc[...]))
        dpx_ref[...] = dxext[HALO:]
        dpx_ref[CHUNK - HALO:, :] += dhalo_sc[...]
        dhalo_sc[...] = dxext[:HALO]
        dstate_sc[...] = dprev
        dpz_ref[...] = dz
        dpl_ref[...] = dpl
        _accumulate(t == 0, [dcw_ref, dcb_ref, ddtb_ref, dalog_ref, ddskip_ref, dsnw_ref],
                    [dcw, dcb, ddtb, dalog, ddskip, dsnw])

    rev = lambda w: pl.BlockSpec((CHUNK, w), lambda t: (nc - 1 - t, 0))
    pshapes = [jax.ShapeDtypeStruct((4, D_CONV), f32), jax.ShapeDtypeStruct((1, D_CONV), f32),
               jax.ShapeDtypeStruct((1, 128), f32), jax.ShapeDtypeStruct((1, 128), f32),
               jax.ShapeDtypeStruct((1, 128), f32), jax.ShapeDtypeStruct((1, D_SSD), f32)]
    return pl.pallas_call(
        body, name="ssd_bwd", grid=(nc,),
        in_specs=[rev(D_CONV),
                  pl.BlockSpec((HALO, D_CONV), lambda t: (jnp.maximum((nc - 1 - t) * per - 1, 0), 0)),
                  rev(D_SSD), rev(128),
                  pl.BlockSpec((None, N_HEADS // 2, 2 * SSD_HEAD_DIM, SSD_STATE), lambda t: (nc - 1 - t, 0, 0, 0)),
                  rev(D_SSD)] + _ssd_param_specs(),
        out_specs=[rev(D_CONV), rev(D_SSD), rev(128)] + _ssd_param_specs(),
        out_shape=[jax.ShapeDtypeStruct((s, D_CONV), f32), jax.ShapeDtypeStruct((s, D_SSD), f32),
                   jax.ShapeDtypeStruct((s, 128), f32)] + pshapes,
        scratch_shapes=[pltpu.VMEM((N_HEADS // 2, 2 * SSD_HEAD_DIM, SSD_STATE), f32), pltpu.VMEM((HALO, D_CONV), f32)],
    )(px, px, pz, plast, states, dyg, *params)


def qkv_bwd(pa, plast, cos_t, sin_t, params, dq, dk, dv):
    s = pa.shape[0]
    ts = _token_block(s)

    def body(pa_ref, pl_ref, cos_ref, sin_ref, *rest):
        qaw, kvaw, wq, wk, wv, qnw, knw, kpw = [r[...] for r in rest[:8]]
        dq_ref, dk_ref, dv_ref = rest[8:11]
        dpa_ref, dpl_ref = rest[11:13]
        dprm_refs = list(rest[13:])
        cos_t, sin_t = cos_ref[...], sin_ref[...]

        def stage(pa_, pl_, qaw_, kvaw_, sq, sk, sv, qnw_, knw_, kpw_):
            return _f_qkv(pa_, pl_, cos_t, sin_t, qaw_, kvaw_, wq, wk, wv, qnw_, knw_, kpw_, (sq, sk, sv))

        _, vjp = jax.vjp(stage, pa_ref[...], pl_ref[...], qaw, kvaw, jnp.zeros(wq.shape, f32), jnp.zeros(wk.shape, f32),
                         jnp.zeros(wv.shape, f32), qnw, knw, kpw)
        grads = vjp((dq_ref[...], dk_ref[...], dv_ref[...]))
        dpa_ref[...] = grads[0]
        dpl_ref[...] = grads[1]
        _accumulate(pl.program_id(0) == 0, dprm_refs, list(grads[2:]))

    tok = lambda w: pl.BlockSpec((ts, w), lambda i: (i, 0))
    head = lambda w: pl.BlockSpec((N_HEADS, ts, w), lambda i: (0, i, 0))
    pshapes = [jax.ShapeDtypeStruct((1, Q_RANK), f32), jax.ShapeDtypeStruct((1, KV_RANK), f32),
               jax.ShapeDtypeStruct((N_HEADS, Q_RANK, HEAD_LANES), f32), jax.ShapeDtypeStruct((N_HEADS, KV_RANK, HEAD_LANES), f32),
               jax.ShapeDtypeStruct((N_HEADS, KV_RANK, V_DIM), f32), jax.ShapeDtypeStruct((1, HEAD_LANES), f32),
               jax.ShapeDtypeStruct((1, HEAD_LANES), f32), jax.ShapeDtypeStruct((1, HEAD_LANES), f32)]
    return pl.pallas_call(
        body, name="qkv_bwd", grid=(s // ts,),
        in_specs=[tok(384), tok(128), tok(128), tok(128)] + _qkv_param_specs()
                 + [head(HEAD_LANES), head(HEAD_LANES), head(V_DIM)],
        out_specs=[tok(384), tok(128)] + _qkv_param_specs(),
        out_shape=[jax.ShapeDtypeStruct((s, 384), f32), jax.ShapeDtypeStruct((s, 128), f32)] + pshapes,
    )(pa, plast, cos_t, sin_t, *params, dq, dk, dv)


def proj_bwd(x, nw, sh, sc, w, dpa, dpz, dpx, dpl_k, dpl_dt, dres):
    s = x.shape[0]
    ts = _token_block(s)

    ni = s // ts

    def body(x_ref, nw_ref, sh_ref, sc_ref, w_ref, dpa_ref, dpz_ref, dpx_ref, dplk_ref, dpld_ref, dres_ref,
             dx_ref, dnw_ref, dsh_ref, dsc_ref, dw_ref, acc_sc):
        i = pl.program_id(0)
        g = jnp.concatenate([dpa_ref[...], dpz_ref[...], dpx_ref[...], dplk_ref[...] + dpld_ref[...]], axis=1)
        w = w_ref[...]
        _, vjp = jax.vjp(lambda x_, nw_, sh_, sc_, slot: _f_proj(x_, nw_, sh_, sc_, w, slot), x_ref[...], nw_ref[...],
                         sh_ref[...], sc_ref[...], jnp.zeros(w.shape, f32))
        dx, dnw, dsh, dsc, dw = vjp(g)
        dx_ref[...] = dx + dres_ref[...]
        _accumulate(i == 0, [dnw_ref, dsh_ref, dsc_ref], [dnw, dsh, dsc])
        _accumulate_then_cast(i == 0, i == ni - 1, [acc_sc], [dw_ref], [dw])

    vec = _const((1, D_MODEL))
    vshape = jax.ShapeDtypeStruct((1, D_MODEL), f32)
    tok = lambda w_: pl.BlockSpec((ts, w_), lambda i: (i, 0))
    return pl.pallas_call(
        body, name="proj_bwd", grid=(ni,), scratch_shapes=[pltpu.VMEM((D_PROJ, D_MODEL), f32)],
        in_specs=[tok(D_MODEL), vec, vec, vec, _const((D_PROJ, D_MODEL)), tok(384), tok(512), tok(1024), tok(128), tok(128),
                  tok(D_MODEL)],
        out_specs=[tok(D_MODEL), vec, vec, vec, _const((D_PROJ, D_MODEL))],
        out_shape=[jax.ShapeDtypeStruct((s, D_MODEL), f32), vshape, vshape, vshape,
                   jax.ShapeDtypeStruct((D_PROJ, D_MODEL), bf16)],
    )(x, nw, sh, sc, w, dpa, dpz, dpx, dpl_k, dpl_dt, dres)


def ada_fwd(c_all, w_ada, b_cols):
    def body(c_ref, w_ref, b_ref, out_ref):
        act = jax.nn.silu(c_ref[...])
        for l in range(2):
            out_ref[l] = jnp.dot(act, w_ref[l], precision=lax.Precision.HIGHEST, preferred_element_type=f32) + b_ref[l]

    return pl.pallas_call(body, name="ada_fwd", out_shape=jax.ShapeDtypeStruct((2, N_DEV, 768), f32))(c_all, w_ada, b_cols)


def ada_bwd(c_all, dmod_cols):
    def body(c_ref, d_ref, out_ref):
        out_ref[0] = lax.dot_general(jax.nn.silu(c_ref[...]), d_ref[0], (((0,), (0,)), ((), ())),
                                     precision=lax.Precision.HIGHEST, preferred_element_type=f32)

    return pl.pallas_call(
        body, name="ada_bwd", grid=(2,),
        in_specs=[_const((N_DEV, D_MODEL)), pl.BlockSpec((1, N_DEV, 768), lambda l: (l, 0, 0))],
        out_specs=pl.BlockSpec((1, D_MODEL, 768), lambda l: (l, 0, 0)),
        out_shape=jax.ShapeDtypeStruct((2, D_MODEL, 768), f32),
    )(c_all, dmod_cols)


def _adamw(w, g, m, v):
    m = ADAM_B1 * m + (1.0 - ADAM_B1) * g
    v = ADAM_B2 * v + (1.0 - ADAM_B2) * (g * g)
    m_hat = m / (1.0 - ADAM_B1 ** ADAM_STEP)
    v_hat = v / (1.0 - ADAM_B2 ** ADAM_STEP)
    delta = -ADAM_LR * (m_hat / (jnp.sqrt(v_hat) + ADAM_EPS) + ADAM_WD * w)
    return delta, m, v


def adamw(parts, w, m, v, layer, prev, name):
    n, r, c = parts.shape
    nl = w.shape[0]
    per_elem = 2 * (n * parts.dtype.itemsize + 7 * 4)
    lanes = -(-c // 128) * 128
    tr, tc = r, c
    if per_elem * r * lanes > ADAMW_BLOCK_BYTES:
        fits = [t for t in range(r // 2, 15, -1) if r % t == 0 and t % 16 == 0 and per_elem * t * lanes <= ADAMW_BLOCK_BYTES]
        if fits:
            tr = fits[0]
        else:
            tc = next(t for t in (512, 256, 128) if c % t == 0)

    def body(p_ref, w_ref, m_ref, v_ref, *rest):
        g_ref, d_ref, nm_ref, nv_ref = rest[-4:]
        g = p_ref[0].astype(f32)
        for k in range(1, n):
            g = g + p_ref[k].astype(f32)
        delta, nm, nv = _adamw(w_ref[...], g, m_ref[...], v_ref[...])
        g_ref[...] = g
        d_ref[...] = delta
        nm_ref[...] = nm
        nv_ref[...] = nv

    blk = pl.BlockSpec((None, tr, tc), lambda i, j: (layer, i, j))
    shp = jax.ShapeDtypeStruct((nl, r, c), f32)
    kept = [] if prev is None else list(prev)
    return pl.pallas_call(
        body, name=name, grid=(r // tr, c // tc),
        in_specs=[pl.BlockSpec((n, tr, tc), lambda i, j: (0, i, j)), blk, blk, blk] + [ANY] * len(kept),
        out_specs=[blk] * 4, out_shape=[shp] * 4,
        input_output_aliases={4 + j: j for j in range(len(kept))},
    )(parts, w, m, v, *kept)


def adamw_two_layers(parts, w, m, v, name, job=None):
    n, r, c = parts[0].shape
    per_elem = 2 * (2 * n * parts[0].dtype.itemsize + 7 * 4)
    lanes = -(-c // 128) * 128
    tr = next(t for t in range(r, 15, -1) if r % t == 0 and t % 16 == 0 and per_elem * t * lanes <= ADAMW_BLOCK_BYTES)
    nblk = r // tr

    def body(p0_ref, p1_ref, w_ref, m_ref, v_ref, g_ref, d_ref, nm_ref, nv_ref):
        layer = pl.program_id(0)

        def update(p_ref):
            g = p_ref[0].astype(f32)
            for k in range(1, n):
                g = g + p_ref[k].astype(f32)
            delta, nm, nv = _adamw(w_ref[...], g, m_ref[...], v_ref[...])
            g_ref[...] = g
            d_ref[...] = delta
            nm_ref[...] = nm
            nv_ref[...] = nv

        @pl.when(layer == 0)
        def _():
            update(p0_ref)

        @pl.when(layer == 1)
        def _():
            update(p1_ref)

    blk = pl.BlockSpec((None, tr, c), lambda l, i: (l, i, 0))
    shp = jax.ShapeDtypeStruct((2, r, c), f32)
    return _call_with_job(
        body, name, (2, nblk), job,
        in_specs=[pl.BlockSpec((n, tr, c), lambda l, i: (0, i * (1 - l), 0)),
                  pl.BlockSpec((n, tr, c), lambda l, i: (0, i * l, 0)), blk, blk, blk],
        out_specs=[blk] * 4, out_shape=[shp] * 4, scratch_shapes=[], operands=(parts[0], parts[1], w, m, v),
        relay_at=(1, 0))


def adamw_layers_inside(parts, w, m, v, name):
    n, r, c = parts[0].shape
    nl = w.shape[1]
    tc = next(t for t in (256, 128) if c % t == 0)

    def body(*refs):
        p_refs = refs[:nl]
        w_ref, m_ref, v_ref, g_ref, d_ref, nm_ref, nv_ref = refs[nl:]
        for l in range(nl):
            g = p_refs[l][0].astype(f32)
            for k in range(1, n):
                g = g + p_refs[l][k].astype(f32)
            delta, nm, nv = _adamw(w_ref[:, l, :], g, m_ref[:, l, :], v_ref[:, l, :])
            g_ref[:, l, :] = g
            d_ref[:, l, :] = delta
            nm_ref[:, l, :] = nm
            nv_ref[:, l, :] = nv

    blk = pl.BlockSpec((r, nl, tc), lambda j: (0, 0, j))
    shp = jax.ShapeDtypeStruct((r, nl, c), f32)
    return pl.pallas_call(
        body, name=name, grid=(c // tc,),
        in_specs=[pl.BlockSpec((n, r, tc), lambda j: (0, 0, j))] * nl + [blk] * 3,
        out_specs=[blk] * 4, out_shape=[shp] * 4,
    )(*parts, w, m, v)


def _my_index():
    return 4 * lax.axis_index("x") + 2 * lax.axis_index("y") + lax.axis_index("c")


def _coords(idx):
    return (idx // 4, (idx // 2) % 2, idx % 2)


class CommJob:
    def __init__(self, operands, out_shape, phases, scratch):
        self.operands, self.out_shape, self.phases, self.scratch = operands, out_shape, phases, scratch


def _wait(out, n_blocks, send_sem, recv_sem, send=True, recv=True):
    span = out.at[pl.ds(0, n_blocks)]
    desc = pltpu.make_async_remote_copy(src_ref=span, dst_ref=span, send_sem=send_sem, recv_sem=recv_sem,
                                        device_id=_coords(_my_index()), device_id_type=MESH)
    if recv:
        desc.wait_recv()
    if send:
        desc.wait_send()


def gather_job(shards):
    n = len(shards)

    def places():
        x, y, c = lax.axis_index("x"), lax.axis_index("y"), lax.axis_index("c")
        return (x, y, c), (x, y, 1 - c), [(1 - x, y), (x, 1 - y), (1 - x, 1 - y)]

    def index(p):
        return 4 * p[0] + 2 * p[1] + p[2]

    def start(ins, outs, sems):
        far_send, far_recv, near_send, near_recv, local = sems
        me, sibling, chips = places()
        for k in range(n):
            pltpu.make_async_copy(ins[k], outs[k].at[index(me)], local.at[k]).start()
            for chip in chips:
                pltpu.make_async_remote_copy(src_ref=ins[k], dst_ref=outs[k].at[index(me)], send_sem=far_send.at[k],
                                             recv_sem=far_recv.at[k], device_id=(*chip, me[2]), device_id_type=MESH).start()
            pltpu.make_async_remote_copy(src_ref=ins[k], dst_ref=outs[k].at[index(me)], send_sem=near_send.at[k],
                                         recv_sem=near_recv.at[k], device_id=sibling, device_id_type=MESH).start()

    def relay(ins, outs, sems):
        far_send, far_recv, near_send, near_recv, local = sems
        me, sibling, chips = places()
        for k in range(n):
            _wait(outs[k], 3, far_send.at[k], far_recv.at[k], send=False)
            for chip in chips:
                block = outs[k].at[index((*chip, me[2]))]
                pltpu.make_async_remote_copy(src_ref=block, dst_ref=block, send_sem=near_send.at[k],
                                             recv_sem=near_recv.at[k], device_id=sibling, device_id_type=MESH).start()

    def finish(ins, outs, sems):
        far_send, far_recv, near_send, near_recv, local = sems
        for k in range(n):
            _wait(outs[k], 4, near_send.at[k], near_recv.at[k])
            _wait(outs[k], 3, far_send.at[k], far_recv.at[k], recv=False)
            pltpu.make_async_copy(ins[k], outs[k].at[0], local.at[k]).wait()

    shapes = [jax.ShapeDtypeStruct((N_DEV,) + tuple(a.shape), a.dtype) for a in shards]
    return CommJob(list(shards), shapes, [start, relay, finish], [pltpu.SemaphoreType.DMA((n,))] * 5)


def scatter_job(tensors):
    n = len(tensors)
    flat, where = [], {}
    for k, pieces in enumerate(tensors):
        d = 0
        for piece in pieces:
            for b in range(piece.shape[0]):
                where[k, d] = (len(flat), b)
                d += 1
            flat.append(piece)
        assert d == N_DEV

    def start(ins, outs, sems):
        send_sems, recv_sems, local_sems = sems
        me = _my_index()

        def block(k, d):
            i, b = where[k, d]
            return ins[i].at[b]

        for d in range(N_DEV):
            @pl.when(d != me)
            def _():
                for k in range(n):
                    pltpu.make_async_remote_copy(src_ref=block(k, d), dst_ref=outs[k].at[me], send_sem=send_sems.at[k],
                                                 recv_sem=recv_sems.at[k], device_id=(d // 4, (d // 2) % 2, d % 2),
                                                 device_id_type=MESH).start()

            @pl.when(d == me)
            def _():
                for k in range(n):
                    pltpu.make_async_copy(block(k, d), outs[k].at[d], local_sems.at[k]).start()

    def finish(ins, outs, sems):
        send_sems, recv_sems, local_sems = sems
        for k in range(n):
            _wait(outs[k], N_DEV - 1, send_sems.at[k], recv_sems.at[k])
            i, b = where[k, 0]
            pltpu.make_async_copy(ins[i].at[b], outs[k].at[0], local_sems.at[k]).wait()

    shapes = [jax.ShapeDtypeStruct((N_DEV,) + tuple(p[0].shape[1:]), p[0].dtype) for p in tensors]
    return CommJob(flat, shapes, [start, finish], [pltpu.SemaphoreType.DMA((n,))] * 3)


def merge_jobs(a, b):
    def on(job, off):
        oi, oo, os_ = off
        ni, no, ns = len(job.operands), len(job.out_shape), len(job.scratch)
        return lambda phase: (lambda ins, outs, sems: phase(ins[oi:oi + ni], outs[oo:oo + no], sems[os_:os_ + ns]))

    wrap_a = on(a, (0, 0, 0))
    wrap_b = on(b, (len(a.operands), len(a.out_shape), len(a.scratch)))
    pa, pb = [wrap_a(p) for p in a.phases], [wrap_b(p) for p in b.phases]

    def together(*phases):
        def run(ins, outs, sems):
            for p in phases:
                p(ins, outs, sems)
        return run

    middle = pa[1:-1] + pb[1:-1]
    phases = [together(pa[0], pb[0])] + middle + [together(pa[-1], pb[-1])]
    return CommJob(a.operands + b.operands, a.out_shape + b.out_shape, phases, a.scratch + b.scratch)


def comm_call(job, name):
    ni, no = len(job.operands), len(job.out_shape)

    def body(*refs):
        ins, outs, sems = refs[:ni], refs[ni:ni + no], refs[ni + no:]
        for phase in job.phases:
            phase(ins, outs, sems)

    return pl.pallas_call(body, name=name, in_specs=[ANY] * ni, out_specs=[ANY] * no, out_shape=job.out_shape,
                          scratch_shapes=job.scratch)(*job.operands)


def _carry(job, body, n_in, n_out, at_step):
    ji, jo, js = len(job.operands), len(job.out_shape), len(job.scratch)

    def carrier(*refs):
        a, b = n_in, n_in + ji
        c, d = b + n_out, b + n_out + jo
        e = len(refs) - js
        job_refs = (refs[a:b], refs[c:d], refs[e:])
        n = len(job.phases)

        @pl.when(at_step(0, n))
        def _():
            job.phases[0](*job_refs)

        body(*refs[:a], *refs[b:c], *refs[d:e])

        for i in range(1, n):
            @pl.when(at_step(i, n))
            def _():
                job.phases[i](*job_refs)

    return carrier


def _pad_lanes(v, lo, total=128):
    return jnp.pad(v, (lo, total - lo - v.shape[0]))[None, :]


MIXER_WEIGHTS = ("w_in", "w_q_up", "w_kv_up", "conv_w")
LATE_WEIGHTS = ("w_out", "w_gate_up", "w_down")


def mixer_operands(g, sw):
    w_in = g["w_in"].reshape(D_IN, D_MODEL)
    zero = lambda rows: jnp.zeros((rows, D_MODEL), w_in.dtype)
    w_proj = jnp.concatenate(
        [w_in[:384], w_in[416:928], w_in[928:1952], w_in[1952:1960], zero(56), w_in[384:416], zero(32)], axis=0)
    wq = jnp.pad(g["w_q_up"], ((0, 0), (0, 0), (0, HEAD_LANES - NOPE - ROPE)))
    wk = jnp.pad(g["w_kv_up"][:, :, :NOPE], ((0, 0), (0, 0), (0, HEAD_LANES - NOPE)))
    wv = g["w_kv_up"][:, :, NOPE:]
    qkv = (sw["q_a_norm_w"][None, :], sw["kv_a_norm_w"][None, :], wq, wk, wv,
           _pad_lanes(jnp.concatenate([sw["q_nope_norm_w"], sw["q_pe_norm_w"]]), 0),
           _pad_lanes(sw["k_nope_norm_w"], 0), _pad_lanes(sw["k_pe_norm_w"], NOPE))
    conv_w = g["conv_w"].astype(f32).transpose(1, 0, 2).reshape(4, D_CONV)
    ssd = (conv_w, sw["conv_b"][None, :], _pad_lanes(sw["dt_bias"], 0), _pad_lanes(sw["a_log"], 0),
           _pad_lanes(sw["d_skip"], 0), sw["ssd_norm_w"][None, :])
    return dict(w_proj=w_proj, qkv=qkv, ssd=ssd, n1=sw["norm1_w"][None, :])


def late_operands(g, sw):
    return dict(wo=g["w_out"].reshape(D_MODEL, D_MODEL), wgu=g["w_gate_up"],
                wd=g["w_down"].reshape(N_DEV // 2, FF_SHARD, D_MODEL), n2=sw["norm2_w"][None, :])


def layer_fwd(x, mod, kw, cos_t, sin_t, job=None, late=None, target=None):
    sh1, sc1, g1, sh2, sc2, g2 = [mod[i:i + 1] for i in range(6)]
    pa, pz, px, plast = proj_fwd(x, kw["n1"], sh1, sc1, kw["w_proj"])
    q, k, v = qkv_fwd(pa, plast, cos_t, sin_t, kw["qkv"])
    (o, qx), carried = attn_fwd(q, k, v, job)
    if late is not None:
        kw = {**kw, **late(carried)}
    yg, states = ssd_fwd(px, pz, plast, kw["ssd"])
    x_mid = out_fwd(x, o, yg, g1, kw["wo"])
    x_out, mix, h_mid, gate, up, *loss_part = mlp_fwd(x_mid, kw["n2"], sh2, sc2, g2, kw["wgu"], kw["wd"], target)
    saved = dict(x=x, pa=pa, pz=pz, px=px, plast=plast, qx=qx, k=k, v=v, o=o, yg=yg, states=states, x_mid=x_mid,
                 mix=mix, h_mid=h_mid, gate=gate, up=up)
    return (x_out if target is None else (x_out, loss_part[0])), saved, kw, carried


def layer_bwd_head(dy, mod, kw, sv, job=None):
    _, _, g1, sh2, sc2, g2 = [mod[i:i + 1] for i in range(6)]
    (dhparts, dwg, dwu, dwd), carried = mlp_bwd(sv["h_mid"], dy, sv["gate"], sv["up"], g2, kw["wgu"], kw["wd"], job)
    dmid, dn2, dsh2, dsc2, do, dyg, dg1, dg2, dwo = out_bwd(
        dy, dhparts, sv["x_mid"], kw["n2"], sh2, sc2, sv["mix"], sv["o"], sv["yg"], g1, kw["wo"])
    early = dict(w_out=[dwo.reshape(N_DEV, D_MODEL // N_DEV, D_MODEL)], w_gate_up=[dwg, dwu],
                 w_down=[dwd.reshape(N_DEV, D_FF // N_DEV, D_MODEL)])
    head = dict(dmid=dmid, do=do, dyg=dyg, dn2=dn2, dsh2=dsh2, dsc2=dsc2, dg2=dg2, dg1=dg1)
    return head, early, carried


def layer_bwd_tail(hd, mod, kw, cos_t, sin_t, sv, job=None):
    sh1, sc1 = mod[0:1], mod[1:2]
    (dq, dk, dv), carried = attn_bwd(sv["qx"], sv["k"], sv["v"], hd["do"], job)
    dpx, dpz, dpl_dt, dcw, dcb, ddtb, dalog, ddskip, dsnw = ssd_bwd(sv["px"], sv["pz"], sv["plast"], sv["states"],
                                                                   hd["dyg"], kw["ssd"])
    dpa, dpl_k, dqaw, dkvaw, dwq, dwk, dwv, dqnw, dknw, dkpw = qkv_bwd(sv["pa"], sv["plast"], cos_t, sin_t, kw["qkv"],
                                                                       dq, dk, dv)
    dx, dn1, dsh1, dsc1, dwp = proj_bwd(sv["x"], kw["n1"], sh1, sc1, kw["w_proj"], dpa, dpz, dpx, dpl_k, dpl_dt, hd["dmid"])
    dmod = jnp.concatenate([dsh1, dsc1, hd["dg1"], hd["dsh2"], hd["dsc2"], hd["dg2"]], axis=0)
    dw_in = jnp.concatenate([dwp[:384], dwp[1984:2016], dwp[384:1920], dwp[1920:1928]], axis=0)
    grads = dict(
        norm1_w=dn1[0], norm2_w=hd["dn2"][0], q_a_norm_w=dqaw[0], kv_a_norm_w=dkvaw[0],
        q_nope_norm_w=dqnw[0, :NOPE], q_pe_norm_w=dqnw[0, NOPE:NOPE + ROPE], k_nope_norm_w=dknw[0, :NOPE],
        k_pe_norm_w=dkpw[0, NOPE:NOPE + ROPE], conv_b=dcb[0], dt_bias=ddtb[0, :N_HEADS], a_log=dalog[0, :N_HEADS],
        d_skip=ddskip[0, :N_HEADS], ssd_norm_w=dsnw[0],
        w_in=[dw_in.reshape(N_DEV, D_IN // N_DEV, D_MODEL)],
        w_q_up=[dwq[:, :, :NOPE + ROPE].astype(bf16)],
        w_kv_up=[jnp.concatenate([dwk[:, :, :NOPE], dwv], axis=2).astype(bf16)],
        conv_w=[dcw.reshape(4, N_DEV, D_CONV // N_DEV).transpose(1, 0, 2).astype(bf16)],
    )
    return dx, dmod, grads, carried


def _pack_small(get, last=None):
    flat = jnp.concatenate([get(name).reshape(-1) for name, _ in SMALL])
    flat = jnp.pad(flat, (0, SMALL_ROWS * 128 - flat.shape[0]))
    if last is not None:
        flat = flat.at[-1].set(last)
    return flat.reshape(SMALL_ROWS, 128)


def _unpack_small(packed):
    flat = packed.reshape(-1)
    out, off = {}, 0
    for name, size in SMALL:
        out[name] = flat[off:off + 2 * size].reshape(2, size)
        off += 2 * size
    return out


def kernel(x, c, positions, norm1_w, norm2_w, w_ada, b_ada, w_in, q_a_norm_w, w_q_up, kv_a_norm_w, w_kv_up, q_nope_norm_w, q_pe_norm_w, k_nope_norm_w, k_pe_norm_w, conv_w, conv_b, dt_bias, a_log, d_skip, ssd_norm_w, w_out, w_gate_up, w_down, loss_target, m_norm1_w, m_norm2_w, m_w_ada, m_b_ada, m_w_in, m_q_a_norm_w, m_w_q_up, m_kv_a_norm_w, m_w_kv_up, m_q_nope_norm_w, m_q_pe_norm_w, m_k_nope_norm_w, m_k_pe_norm_w, m_conv_w, m_conv_b, m_dt_bias, m_a_log, m_d_skip, m_ssd_norm_w, m_w_out, m_w_gate_up, m_w_down, v_norm1_w, v_norm2_w, v_w_ada, v_b_ada, v_w_in, v_q_a_norm_w, v_w_q_up, v_kv_a_norm_w, v_w_kv_up, v_q_nope_norm_w, v_q_pe_norm_w, v_k_nope_norm_w, v_k_pe_norm_w, v_conv_w, v_conv_b, v_dt_bias, v_a_log, v_d_skip, v_ssd_norm_w, v_w_out, v_w_gate_up, v_w_down):
    w = dict(norm1_w=norm1_w, norm2_w=norm2_w, w_ada=w_ada, b_ada=b_ada, w_in=w_in, q_a_norm_w=q_a_norm_w, w_q_up=w_q_up,
             kv_a_norm_w=kv_a_norm_w, w_kv_up=w_kv_up, q_nope_norm_w=q_nope_norm_w, q_pe_norm_w=q_pe_norm_w,
             k_nope_norm_w=k_nope_norm_w, k_pe_norm_w=k_pe_norm_w, conv_w=conv_w, conv_b=conv_b, dt_bias=dt_bias,
             a_log=a_log, d_skip=d_skip, ssd_norm_w=ssd_norm_w, w_out=w_out, w_gate_up=w_gate_up, w_down=w_down)
    m = dict(norm1_w=m_norm1_w, norm2_w=m_norm2_w, w_ada=m_w_ada, b_ada=m_b_ada, w_in=m_w_in, q_a_norm_w=m_q_a_norm_w,
             w_q_up=m_w_q_up, kv_a_norm_w=m_kv_a_norm_w, w_kv_up=m_w_kv_up, q_nope_norm_w=m_q_nope_norm_w,
             q_pe_norm_w=m_q_pe_norm_w, k_nope_norm_w=m_k_nope_norm_w, k_pe_norm_w=m_k_pe_norm_w, conv_w=m_conv_w,
             conv_b=m_conv_b, dt_bias=m_dt_bias, a_log=m_a_log, d_skip=m_d_skip, ssd_norm_w=m_ssd_norm_w, w_out=m_w_out,
             w_gate_up=m_w_gate_up, w_down=m_w_down)
    v = dict(norm1_w=v_norm1_w, norm2_w=v_norm2_w, w_ada=v_w_ada, b_ada=v_b_ada, w_in=v_w_in, q_a_norm_w=v_q_a_norm_w,
             w_q_up=v_w_q_up, kv_a_norm_w=v_kv_a_norm_w, w_kv_up=v_w_kv_up, q_nope_norm_w=v_q_nope_norm_w,
             q_pe_norm_w=v_q_pe_norm_w, k_nope_norm_w=v_k_nope_norm_w, k_pe_norm_w=v_k_pe_norm_w, conv_w=v_conv_w,
             conv_b=v_conv_b, dt_bias=v_dt_bias, a_log=v_a_log, d_skip=v_d_skip, ssd_norm_w=v_ssd_norm_w, w_out=v_w_out,
             w_gate_up=v_w_gate_up, w_down=v_w_down)
    me = _my_index()
    seq = x.shape[1]

    def shard(name, l):
        if name == "conv_w":
            return w[name][l]
        if name in TRANSPOSED:
            return jnp.swapaxes(w[name][l], 0, 1).astype(bf16)
        return w[name][l].astype(bf16)

    def shards(names, l):
        return [shard(name, l) for name in names]

    small = [{name: w[name][l] for name, _ in SMALL if name != "b_ada"} for l in range(2)]
    n_late = len(LATE_WEIGHTS)

    first = comm_call(gather_job([c] + shards(MIXER_WEIGHTS, 0)), "gather_first")
    c_all = first[0].reshape(N_DEV, D_MODEL)
    kws = [mixer_operands(dict(zip(MIXER_WEIGHTS, first[1:])), small[0]), None]

    b_cols = lax.dynamic_slice_in_dim(b_ada, me * 768, 768, axis=1)
    mod_cols = ada_fwd(c_all, w_ada, b_cols)
    (mod_all,) = comm_call(gather_job([mod_cols]), "gather_mod")
    mod_me = lax.dynamic_index_in_dim(mod_all, me, axis=2, keepdims=False)
    mods = [mod_me[:, l, :].reshape(6, D_MODEL) for l in range(2)]

    inv_freq = 1.0 / (ROPE_THETA ** (jnp.arange(0, ROPE, 2, dtype=f32) / ROPE))
    inv = _pad_lanes(jnp.concatenate([inv_freq, inv_freq]), NOPE)
    cos_t, sin_t = rope_tables(positions.reshape(seq, 1), inv)

    saved = [None, None]
    h, saved[0], kws[0], got = layer_fwd(
        x[0], mods[0], kws[0], cos_t, sin_t, gather_job(shards(LATE_WEIGHTS, 0) + shards(MIXER_WEIGHTS, 1)),
        lambda got: late_operands(dict(zip(LATE_WEIGHTS, got[:n_late])), small[0]))
    kws[1] = mixer_operands(dict(zip(MIXER_WEIGHTS, got[n_late:])), small[1])
    (dy, loss_part), saved[1], kws[1], _ = layer_fwd(
        h, mods[1], kws[1], cos_t, sin_t, gather_job(shards(LATE_WEIGHTS, 1)),
        lambda got: late_operands(dict(zip(LATE_WEIGHTS, got)), small[1]), loss_target[0])

    early, late = ("w_out", "w_gate_up", "w_down"), ("w_in", "w_q_up", "w_kv_up", "conv_w")
    parts = [{}, {}]
    head, pieces, _ = layer_bwd_head(dy, mods[1], kws[1], saved[1])
    dy, dmod1, grads1, got = layer_bwd_tail(head, mods[1], kws[1], cos_t, sin_t, saved[1], scatter_job([pieces[n] for n in early]))
    parts[1].update(zip(early, got))
    head, pieces, got = layer_bwd_head(dy, mods[0], kws[0], saved[0], scatter_job([grads1[n] for n in late]))
    parts[1].update(zip(late, got))
    dy, dmod0, grads0, got = layer_bwd_tail(head, mods[0], kws[0], cos_t, sin_t, saved[0], scatter_job([pieces[n] for n in early]))
    parts[0].update(zip(early, got))
    grad_x = dy[None]

    small_part = {name: jnp.stack([grads0[name], grads1[name]]) for name, _ in SMALL if name != "b_ada"}
    small_part["b_ada"] = jnp.stack([dmod0.reshape(-1), dmod1.reshape(-1)])
    swap = lambda a: jnp.swapaxes(a, 1, 2)
    updated, last = adamw_two_layers(
        [parts[l]["w_gate_up"] for l in range(2)], swap(w_gate_up), swap(m_w_gate_up), swap(v_w_gate_up), "adamw_w_gate_up_comm",
        merge_jobs(scatter_job([grads0[n] for n in late]),
                   gather_job([_pack_small(lambda n: small_part[n], loss_part[0, 0])])))
    parts[0].update(zip(late, last[:len(late)]))
    small_all = last[len(late)]
    packed = adamw(small_all, _pack_small(lambda n: w[n])[None], _pack_small(lambda n: m[n])[None],
                   _pack_small(lambda n: v[n])[None], 0, None, "adamw_small")
    loss = packed[0][0, -1, -1]
    res = {}
    for key, arr in zip("gdmv", packed):
        for name, val in _unpack_small(arr[0]).items():
            res[key, name] = val

    off = 2 * (1024 + 1024)
    dmod_all = small_all.reshape(N_DEV, -1)[:, off:off + 2 * 6144].reshape(N_DEV, 2, 6144)
    dmod_cols = lax.dynamic_slice_in_dim(dmod_all, me * 768, 768, axis=2).transpose(1, 0, 2)
    g_ada = ada_bwd(c_all, dmod_cols)
    out = None
    for l in range(2):
        out = adamw(g_ada[l][None], w_ada, m_w_ada, v_w_ada, l, out, "adamw_w_ada")
    res.update(zip([(key, "w_ada") for key in "gdmv"], out))

    inside = lambda a: jnp.transpose(a, (2, 0, 1))
    out = adamw_layers_inside([parts[l]["w_in"] for l in range(2)], inside(w_in), inside(m_w_in), inside(v_w_in), "adamw_w_in")
    res.update(zip([(key, "w_in") for key in "gdmv"], [jnp.transpose(a, (1, 2, 0)) for a in out]))
    res.update(zip([(key, "w_gate_up") for key in "gdmv"], [swap(a) for a in updated]))
    for name in BIG:
        if name in ("w_in", "w_gate_up"):
            continue
        view = (lambda a: jnp.swapaxes(a, 1, 2)) if name in TRANSPOSED else (lambda a: a)
        out = None
        for l in range(2):
            out = adamw(parts[l][name], view(w[name]), view(m[name]), view(v[name]), l, out, "adamw_" + name)
        res.update(zip([(key, name) for key in "gdmv"], [view(a) for a in out]))

    return (loss, grad_x, *[res["g", n] for n in WEIGHTS], *[res["d", n] for n in WEIGHTS],
            *[res["m", n] for n in WEIGHTS], *[res["v", n] for n in WEIGHTS])
```

```python
import functools

import jax
import jax.numpy as jnp
from jax import lax
from jax.experimental import pallas as pl
from jax.experimental.pallas import tpu as pltpu

f32 = jnp.float32
bf16 = jnp.bfloat16

N_DEV = 8
D_MODEL = 1024
N_HEADS = 8
HEAD_LANES = 128
NOPE = 64
ROPE = 32
V_DIM = 64
Q_RANK = 256
KV_RANK = 128
D_SSD = 512
D_CONV = 1024
SSD_STATE = 128
SSD_HEAD_DIM = 64
CHUNK = 128
HALO = 8
D_FF = 2816
FF_SHARD = 704
D_IN = 1960
D_PROJ = 2048
EPS = 1e-6
LOG2E = 1.4426950408889634
LN2 = 0.6931471805599453
Q_SCALE = (NOPE + ROPE) ** -0.5 * LOG2E
SPARE_Q = NOPE + ROPE
SPARE_V = V_DIM
ATTN_ROWS_FWD = 256
ATTN_HEADS_FWD = 8
ATTN_HEADS_BWD = 4
MLP_FWD_ROWS = 1024
MLP_BWD_CHUNK = 256
MLP_BWD_ROWS = 1024
ROPE_THETA = 10000.0
NEG = -1e30

ADAM_LR = 0.001
ADAM_B1 = 0.9
ADAM_B2 = 0.999
ADAM_EPS = 1e-08
ADAM_WD = 0.01
ADAM_STEP = 10
ADAMW_BLOCK_BYTES = 36 << 20

MESH = pl.DeviceIdType.MESH
ANY = pl.BlockSpec(memory_space=pl.ANY)

SMALL = (("norm1_w", 1024), ("norm2_w", 1024), ("b_ada", 6144), ("q_a_norm_w", 256), ("kv_a_norm_w", 128),
         ("q_nope_norm_w", 64), ("q_pe_norm_w", 32), ("k_nope_norm_w", 64), ("k_pe_norm_w", 32),
         ("conv_b", 1024), ("dt_bias", 8), ("a_log", 8), ("d_skip", 8), ("ssd_norm_w", 512))
SMALL_ROWS = 168
BIG = ("w_in", "w_q_up", "w_kv_up", "conv_w", "w_out", "w_gate_up", "w_down")
TRANSPOSED = ("w_in", "w_gate_up")
WEIGHTS = ("norm1_w", "norm2_w", "w_ada", "b_ada", "w_in", "q_a_norm_w", "w_q_up", "kv_a_norm_w", "w_kv_up",
           "q_nope_norm_w", "q_pe_norm_w", "k_nope_norm_w", "k_pe_norm_w", "conv_w", "conv_b", "dt_bias",
           "a_log", "d_skip", "ssd_norm_w", "w_out", "w_gate_up", "w_down")


def _dot(a, b, ca, cb):
    return lax.dot_general(a.astype(bf16), b.astype(bf16), (((ca,), (cb,)), ((), ())), preferred_element_type=f32)


@jax.custom_vjp
def mm(a, b):
    return _dot(a, b, 1, 0)


def _mm_fwd(a, b):
    return _dot(a, b, 1, 0), (a, b)


def _mm_bwd(res, g):
    a, b = res
    return _dot(g, b, 1, 1).astype(a.dtype), _dot(a, g, 0, 0).astype(b.dtype)


mm.defvjp(_mm_fwd, _mm_bwd)


@jax.custom_vjp
def _mm_slot(a, w, slot):
    return _dot(a, w, 1, 0)


def _mm_slot_fwd(a, w, slot):
    return _dot(a, w, 1, 0), (a, w)


def _mm_slot_bwd(res, g):
    a, w = res
    return _dot(g, w, 1, 1).astype(a.dtype), None, _dot(a, g, 0, 0)


_mm_slot.defvjp(_mm_slot_fwd, _mm_slot_bwd)


def mmw(a, w, slot=None):
    return _dot(a, w, 1, 0) if slot is None else _mm_slot(a, w, slot)


@jax.custom_vjp
def _mm_slot_t(a, wt, slot):
    return _dot(a, wt, 1, 1)


def _mm_slot_t_fwd(a, wt, slot):
    return _dot(a, wt, 1, 1), (a, wt)


def _mm_slot_t_bwd(res, g):
    a, wt = res
    return _dot(g, wt, 1, 0).astype(a.dtype), None, _dot(g, a, 0, 0)


_mm_slot_t.defvjp(_mm_slot_t_fwd, _mm_slot_t_bwd)


def mmw_t(a, wt, slot=None):
    return _dot(a, wt, 1, 1) if slot is None else _mm_slot_t(a, wt, slot)


@jax.custom_vjp
def mm_nt(a, b):
    return _dot(a, b, 1, 1)


def _mm_nt_fwd(a, b):
    return _dot(a, b, 1, 1), (a, b)


def _mm_nt_bwd(res, g):
    a, b = res
    return _dot(g, b, 1, 0).astype(a.dtype), _dot(g, a, 0, 0).astype(b.dtype)


mm_nt.defvjp(_mm_nt_fwd, _mm_nt_bwd)


@jax.custom_vjp
def mm_tn(a, b):
    return _dot(a, b, 0, 0)


def _mm_tn_fwd(a, b):
    return _dot(a, b, 0, 0), (a, b)


def _mm_tn_bwd(res, g):
    a, b = res
    return _dot(b, g, 1, 1).astype(a.dtype), _dot(a, g, 1, 0).astype(b.dtype)


mm_tn.defvjp(_mm_tn_fwd, _mm_tn_bwd)


def _rms(x, w):
    return x * lax.rsqrt(jnp.mean(x * x, axis=-1, keepdims=True) + EPS) * w


def _const(shape):
    n = len(shape)
    return pl.BlockSpec(shape, lambda *_: (0,) * n)


def _accumulate(first, refs, vals):
    @pl.when(first)
    def _():
        for r, v in zip(refs, vals):
            r[...] = v

    @pl.when(jnp.logical_not(first))
    def _():
        for r, v in zip(refs, vals):
            r[...] += v


def _accumulate_then_cast(first, last, accs, outs, vals):
    _accumulate(first, accs, vals)

    @pl.when(last)
    def _():
        for a, o in zip(accs, outs):
            o[...] = a[...].astype(o.dtype)


def _token_block(s):
    return min(512, s)


def _f_proj(x, nw, sh, sc, w, slot=None):
    h = _rms(x, nw) * (1.0 + sc) + sh
    return mmw_t(h, w, slot)


def _f_qkv(pa, plast, cos_t, sin_t, qaw, kvaw, wq, wk, wv, qnw, knw, kpw, slots=None):
    sq, sk, sv = slots if slots is not None else ([None] * N_HEADS,) * 3
    lane = lax.broadcasted_iota(jnp.int32, (1, HEAD_LANES), 1)
    m_nope = lane < NOPE
    m_pe = (lane >= NOPE) & (lane < NOPE + ROPE)
    rows = pa.shape[0]

    def rope(t):
        half = ROPE // 2
        swapped = jnp.concatenate(
            [jnp.zeros((rows, NOPE), f32), t[:, NOPE + half:NOPE + ROPE], t[:, NOPE:NOPE + half],
             jnp.zeros((rows, HEAD_LANES - NOPE - ROPE), f32)], axis=1)
        return t * cos_t + swapped * sin_t

    qa = _rms(pa[:, :Q_RANK], qaw)
    kva = _rms(pa[:, Q_RANK:Q_RANK + KV_RANK], kvaw)
    kp = jnp.where(m_pe, plast, 0.0)
    kp = kp * lax.rsqrt(jnp.sum(kp * kp, axis=-1, keepdims=True) / ROPE + EPS) * kpw
    k_rot = rope(kp)
    qs, ks, vs = [], [], []
    for h in range(N_HEADS):
        qh = mmw(qa, wq[h], sq[h])
        ss_n = jnp.sum(jnp.where(m_nope, qh * qh, 0.0), axis=-1, keepdims=True) / NOPE
        ss_p = jnp.sum(jnp.where(m_pe, qh * qh, 0.0), axis=-1, keepdims=True) / ROPE
        r = jnp.where(m_nope, lax.rsqrt(ss_n + EPS), lax.rsqrt(ss_p + EPS))
        qs.append(rope(qh * r * qnw) * Q_SCALE)
        kh = mmw(kva, wk[h], sk[h])
        kh = kh * lax.rsqrt(jnp.sum(kh * kh, axis=-1, keepdims=True) / NOPE + EPS) * knw
        ks.append(kh + k_rot)
        vs.append(mmw(kva, wv[h], sv[h]))
    return jnp.stack(qs), jnp.stack(ks), jnp.stack(vs)


def _f_ssd(xext, z, plast, prev, cw, cb, dtb, alog, dskip, snw):
    n = CHUNK
    conv = cb
    for k in range(4):
        conv = conv + cw[k:k + 1] * xext[HALO - 3 + k:HALO - 3 + k + n]
    xc = jax.nn.silu(conv)
    xs, bm, cm = xc[:, :D_SSD], xc[:, D_SSD:D_SSD + 2 * SSD_STATE], xc[:, D_SSD + 2 * SSD_STATE:]
    lane = lax.broadcasted_iota(jnp.int32, (1, 128), 1)
    dt = jax.nn.softplus(jnp.where(lane < N_HEADS, plast, 0.0) + dtb)
    adt = dt * (-jnp.exp(alog))
    row = lax.broadcasted_iota(jnp.int32, (n, n), 0)
    col = lax.broadcasted_iota(jnp.int32, (n, n), 1)
    tri = row >= col
    acs = jnp.dot(tri.astype(f32), adt, precision=lax.Precision.HIGHEST, preferred_element_type=f32)
    acs_t = acs.T
    bgs = [bm[:, g * SSD_STATE:(g + 1) * SSD_STATE] for g in range(2)]
    cgs = [cm[:, g * SSD_STATE:(g + 1) * SSD_STATE] for g in range(2)]
    cb_ts = [mm_nt(cgs[g], bgs[g]) for g in range(2)]
    low = lane < SSD_HEAD_DIM
    low_rows = lax.broadcasted_iota(jnp.int32, (2 * SSD_HEAD_DIM, 1), 0) < SSD_HEAD_DIM

    def both(a0, a1):
        return jnp.where(low, a0, a1)

    pre = []
    for i in range(N_HEADS // 2):
        h0, h1 = 2 * i, 2 * i + 1
        col0, col1 = acs[:, h0:h0 + 1], acs[:, h1:h1 + 1]
        last0, last1 = acs[n - 1:n, h0:h0 + 1], acs[n - 1:n, h1:h1 + 1]
        cb_t = cb_ts[i // 2]
        scores0 = cb_t * jnp.exp(jnp.where(tri, col0 - acs_t[h0:h0 + 1, :], -jnp.inf))
        scores1 = cb_t * jnp.exp(jnp.where(tri, col1 - acs_t[h1:h1 + 1, :], -jnp.inf))
        xp = xs[:, i * 128:(i + 1) * 128]
        xdt = xp * both(dt[:, h0:h0 + 1], dt[:, h1:h1 + 1])
        weighted = xdt * both(jnp.exp(last0 - col0), jnp.exp(last1 - col1))
        chunk_decay = jnp.where(low_rows, jnp.exp(last0), jnp.exp(last1))
        in_decay = both(jnp.exp(col0), jnp.exp(col1))
        skip = both(dskip[:, h0:h0 + 1], dskip[:, h1:h1 + 1]) * xp
        pre.append((scores0, scores1, xdt, weighted, chunk_decay, in_decay, skip))
    prods = []
    for i in range(N_HEADS // 2):
        scores0, scores1, xdt, weighted, _, _, _ = pre[i]
        g = i // 2
        y_diag = mm(scores0, jnp.where(low, xdt, 0.0)) + mm(scores1, jnp.where(low, 0.0, xdt))
        prods.append((y_diag, mm_tn(weighted, bgs[g]), mm_nt(cgs[g], prev[i])))
    ys, news = [], []
    for i in range(N_HEADS // 2):
        y_diag, st, y_off = prods[i]
        _, _, _, _, chunk_decay, in_decay, skip = pre[i]
        news.append(chunk_decay * prev[i] + st)
        ys.append(y_diag + y_off * in_decay + skip)
    y = jnp.concatenate(ys, axis=1)
    yg = y * jax.nn.silu(z)
    half = D_SSD // 2
    outs = []
    for g in range(2):
        t = yg[:, g * half:(g + 1) * half]
        outs.append(t * lax.rsqrt(jnp.mean(t * t, axis=-1, keepdims=True) + EPS))
    return jnp.concatenate(outs, axis=1) * snw, jnp.stack(news)


def _f_out(o, yg, g1, wo, slot=None):
    cat = jnp.concatenate([o[h] for h in range(N_HEADS)] + [yg], axis=1)
    return g1 * mmw(cat, wo, slot)


def _f_modulate(x, nw, sh, sc):
    return _rms(x, nw) * (1.0 + sc) + sh


def proj_fwd(x, nw, sh, sc, w):
    s = x.shape[0]
    ts = _token_block(s)

    def body(x_ref, nw_ref, sh_ref, sc_ref, w_ref, pa_ref, pz_ref, px_ref, pl_ref):
        p = _f_proj(x_ref[...], nw_ref[...], sh_ref[...], sc_ref[...], w_ref[...])
        pa_ref[...] = p[:, :384]
        pz_ref[...] = p[:, 384:896]
        px_ref[...] = p[:, 896:1920]
        pl_ref[...] = p[:, 1920:]

    vec = _const((1, D_MODEL))
    return pl.pallas_call(
        body, name="proj_fwd", grid=(s // ts,),
        in_specs=[pl.BlockSpec((ts, D_MODEL), lambda i: (i, 0)), vec, vec, vec, _const((D_PROJ, D_MODEL))],
        out_specs=[pl.BlockSpec((ts, 384), lambda i: (i, 0)), pl.BlockSpec((ts, 512), lambda i: (i, 0)),
                   pl.BlockSpec((ts, 1024), lambda i: (i, 0)), pl.BlockSpec((ts, 128), lambda i: (i, 0))],
        out_shape=[jax.ShapeDtypeStruct((s, 384), f32), jax.ShapeDtypeStruct((s, 512), f32),
                   jax.ShapeDtypeStruct((s, 1024), f32), jax.ShapeDtypeStruct((s, 128), f32)],
    )(x, nw, sh, sc, w)


def rope_tables(pos, inv, job=None):
    s = pos.shape[0]
    ts = _token_block(s)

    def body(pos_ref, inv_ref, cos_ref, sin_ref):
        ang = pos_ref[...].astype(f32) * inv_ref[...]
        lane = lax.broadcasted_iota(jnp.int32, (1, HEAD_LANES), 1)
        half = ROPE // 2
        cos_ref[...] = jnp.where(lane < NOPE, 1.0, jnp.where(lane < NOPE + ROPE, jnp.cos(ang), 0.0))
        sn = jnp.sin(ang)
        sin_ref[...] = jnp.where((lane >= NOPE) & (lane < NOPE + half), -sn,
                                 jnp.where((lane >= NOPE + half) & (lane < NOPE + ROPE), sn, 0.0))

    steps = s // ts
    return _call_with_job(
        body, "rope_tables" if job is None else "rope_tables_comm", (steps,), job,
        in_specs=[pl.BlockSpec((ts, 1), lambda i: (i, 0)), _const((1, HEAD_LANES))],
        out_specs=[pl.BlockSpec((ts, HEAD_LANES), lambda i: (i, 0))] * 2,
        out_shape=[jax.ShapeDtypeStruct((s, HEAD_LANES), f32)] * 2, scratch_shapes=[], operands=(pos, inv),
        relay_at=(max(steps - 2, 0),))


def _qkv_param_specs():
    return [_const((1, Q_RANK)), _const((1, KV_RANK)), _const((N_HEADS, Q_RANK, HEAD_LANES)),
            _const((N_HEADS, KV_RANK, HEAD_LANES)), _const((N_HEADS, KV_RANK, V_DIM)),
            _const((1, HEAD_LANES)), _const((1, HEAD_LANES)), _const((1, HEAD_LANES))]


def qkv_fwd(pa, plast, cos_t, sin_t, params):
    s = pa.shape[0]
    ts = _token_block(s)

    def body(pa_ref, pl_ref, cos_ref, sin_ref, *rest):
        prm = [r[...] for r in rest[:8]]
        q_ref, k_ref, v_ref = rest[8:]
        q, k, v = _f_qkv(pa_ref[...], pl_ref[...], cos_ref[...], sin_ref[...], *prm)
        q_ref[...] = q.astype(bf16)
        lane = lax.broadcasted_iota(jnp.int32, (1, 1, HEAD_LANES), 2)
        k_ref[...] = jnp.where((lane == SPARE_Q) | (lane == SPARE_Q + 1), 1.0, k).astype(bf16)
        v_ref[...] = jnp.concatenate([v, jnp.ones_like(v)], axis=-1).astype(bf16)

    tok = lambda w: pl.BlockSpec((ts, w), lambda i: (i, 0))
    head = pl.BlockSpec((N_HEADS, ts, HEAD_LANES), lambda i: (0, i, 0))
    return pl.pallas_call(
        body, name="qkv_fwd", grid=(s // ts,),
        in_specs=[tok(384), tok(128), tok(128), tok(128)] + _qkv_param_specs(),
        out_specs=[head] * 3, out_shape=[jax.ShapeDtypeStruct((N_HEADS, s, HEAD_LANES), bf16)] * 3,
    )(pa, plast, cos_t, sin_t, *params)


def _scores(q, k):
    return lax.dot_general(q, k, (((1,), (1,)), ((), ())), preferred_element_type=f32)


def _tril(rows, cols, row_offset):
    row = row_offset + lax.broadcasted_iota(jnp.int32, (rows, cols), 0)
    col = lax.broadcasted_iota(jnp.int32, (rows, cols), 1)
    return row >= col


def _call_with_job(body, name, grid, job, in_specs, out_specs, out_shape, scratch_shapes, operands, relay_at=None):
    if job is None:
        res = pl.pallas_call(body, name=name, grid=grid, in_specs=in_specs, out_specs=out_specs, out_shape=out_shape,
                             scratch_shapes=scratch_shapes)(*operands)
        return res, None

    def at_step(i, n):
        if i == 0:
            want = [0] * len(grid)
        elif i == n - 1:
            want = [g - 1 for g in grid]
        else:
            want = relay_at
        return functools.reduce(jnp.logical_and, [pl.program_id(a) == s for a, s in enumerate(want)])

    carrier = _carry(job, body, len(in_specs), len(out_specs), at_step)
    res = pl.pallas_call(
        carrier, name=name, grid=grid,
        in_specs=list(in_specs) + [ANY] * len(job.operands), out_specs=list(out_specs) + [ANY] * len(job.out_shape),
        out_shape=list(out_shape) + list(job.out_shape), scratch_shapes=list(scratch_shapes) + job.scratch,
    )(*operands, *job.operands)
    return res[:len(out_specs)], res[len(out_specs):]


def attn_fwd(q, k, v, job=None):
    s = q.shape[1]
    t = _token_block(s)
    nb = s // t

    rb = min(ATTN_ROWS_FWD, t)

    hp = ATTN_HEADS_FWD

    def body(q_ref, k_ref, v_ref, o_ref, qx_ref, m_sc, acc_sc):
        qi = pl.program_id(1)
        m_sc[...] = jnp.full(m_sc.shape, NEG, f32)
        acc_sc[...] = jnp.zeros(acc_sc.shape, f32)

        def step(k0, diagonal):
            chains = [(hh, r) for hh in range(hp) for r in range(t // rb)]

            def scores(hh, r):
                nk = (r + 1) * rb if diagonal else t
                sc = _scores(q_ref[hh, pl.ds(r * rb, rb), :], k_ref[hh, pl.ds(k0, nk), :])
                return jnp.where(_tril(rb, nk, r * rb), sc, NEG) if diagonal else sc

            ahead = scores(*chains[0])
            for c, (hh, r) in enumerate(chains):
                sc = ahead
                if c + 1 < len(chains):
                    ahead = scores(*chains[c + 1])
                rows = pl.ds(r * rb, rb)
                keys = pl.ds(k0, (r + 1) * rb if diagonal else t)
                m_prev = m_sc[hh, rows, :1]
                m_new = jnp.maximum(m_prev, jnp.max(sc, axis=-1, keepdims=True))
                p = jnp.exp2(sc - m_new)
                alpha = jnp.exp2(m_prev - m_new)
                acc = alpha * acc_sc[hh, rows, :] + jnp.dot(p.astype(bf16), v_ref[hh, keys, :], preferred_element_type=f32)
                if diagonal:
                    l = acc[:, V_DIM:V_DIM + 1]
                    o_ref[hh, rows, :] = acc[:, :V_DIM] / l
                    lse = m_new + jnp.log2(l)
                    high = lse.astype(bf16)
                    low = (lse - high.astype(f32)).astype(bf16)
                    lane = lax.broadcasted_iota(jnp.int32, (1, HEAD_LANES), 1)
                    qx_ref[hh, rows, :] = jnp.where(lane == SPARE_Q, -high,
                                                    jnp.where(lane == SPARE_Q + 1, -low, q_ref[hh, rows, :]))
                else:
                    acc_sc[hh, rows, :] = acc
                    m_sc[hh, rows, :] = jnp.broadcast_to(m_new, (rb, 128))

        def below(ki, carry):
            step(pl.multiple_of(ki * t, t), False)
            return carry

        lax.fori_loop(0, qi, below, 0)
        step(pl.multiple_of(qi * t, t), True)

    return _call_with_job(
        body, "attn_fwd" if job is None else "attn_fwd_comm", (N_HEADS // hp, nb), job,
        in_specs=[pl.BlockSpec((hp, t, HEAD_LANES), lambda h, qi: (h, qi, 0)),
                  pl.BlockSpec((hp, s, HEAD_LANES), lambda h, qi: (h, 0, 0)),
                  pl.BlockSpec((hp, s, HEAD_LANES), lambda h, qi: (h, 0, 0))],
        out_specs=[pl.BlockSpec((hp, t, V_DIM), lambda h, qi: (h, qi, 0)),
                   pl.BlockSpec((hp, t, HEAD_LANES), lambda h, qi: (h, qi, 0))],
        out_shape=[jax.ShapeDtypeStruct((N_HEADS, s, V_DIM), f32), jax.ShapeDtypeStruct((N_HEADS, s, HEAD_LANES), bf16)],
        scratch_shapes=[pltpu.VMEM((hp, t, 128), f32), pltpu.VMEM((hp, t, HEAD_LANES), f32)],
        operands=(q, k, v), relay_at=(N_HEADS // hp - 1, max(nb - 2, 0)))


def _ssd_param_specs():
    return [_const((4, D_CONV)), _const((1, D_CONV)), _const((1, 128)), _const((1, 128)), _const((1, 128)),
            _const((1, D_SSD))]


def ssd_fwd(px, pz, plast, params):
    s = px.shape[0]
    nc = s // CHUNK

    def body(px_ref, pz_ref, pl_ref, cw_ref, cb_ref, dtb_ref, alog_ref, dskip_ref, snw_ref, yg_ref, st_ref,
             state_sc, halo_sc):
        i = pl.program_id(0)

        @pl.when(i == 0)
        def _():
            state_sc[...] = jnp.zeros(state_sc.shape, f32)
            halo_sc[...] = jnp.zeros(halo_sc.shape, f32)

        x = px_ref[...]
        prev = state_sc[...]
        st_ref[...] = prev
        xext = jnp.concatenate([halo_sc[...], x], axis=0)
        yg, new = _f_ssd(xext, pz_ref[...], pl_ref[...], prev, cw_ref[...], cb_ref[...], dtb_ref[...],
                         alog_ref[...], dskip_ref[...], snw_ref[...])
        yg_ref[...] = yg
        state_sc[...] = new
        halo_sc[...] = x[CHUNK - HALO:]

    tok = lambda w: pl.BlockSpec((CHUNK, w), lambda i: (i, 0))
    return pl.pallas_call(
        body, name="ssd_fwd", grid=(nc,),
        in_specs=[tok(D_CONV), tok(D_SSD), tok(128)] + _ssd_param_specs(),
        out_specs=[tok(D_SSD), pl.BlockSpec((None, N_HEADS // 2, 2 * SSD_HEAD_DIM, SSD_STATE), lambda i: (i, 0, 0, 0))],
        out_shape=[jax.ShapeDtypeStruct((s, D_SSD), f32),
                   jax.ShapeDtypeStruct((nc, N_HEADS // 2, 2 * SSD_HEAD_DIM, SSD_STATE), f32)],
        scratch_shapes=[pltpu.VMEM((N_HEADS // 2, 2 * SSD_HEAD_DIM, SSD_STATE), f32), pltpu.VMEM((HALO, D_CONV), f32)],
    )(px, pz, plast, *params)


def out_fwd(x, o, yg, g1, wo):
    s = x.shape[0]
    ts = _token_block(s)

    def body(x_ref, o_ref, yg_ref, g1_ref, wo_ref, out_ref):
        out_ref[...] = x_ref[...] + _f_out(o_ref[...], yg_ref[...], g1_ref[...], wo_ref[...])

    return pl.pallas_call(
        body, name="out_fwd", grid=(s // ts,),
        in_specs=[pl.BlockSpec((ts, D_MODEL), lambda i: (i, 0)), pl.BlockSpec((N_HEADS, ts, V_DIM), lambda i: (0, i, 0)),
                  pl.BlockSpec((ts, D_SSD), lambda i: (i, 0)), _const((1, D_MODEL)), _const((D_MODEL, D_MODEL))],
        out_specs=pl.BlockSpec((ts, D_MODEL), lambda i: (i, 0)),
        out_shape=jax.ShapeDtypeStruct((s, D_MODEL), f32),
    )(x, o, yg, g1, wo)


def mlp_fwd(x, nw, sh, sc, g2, wgu, wd, target=None):
    s = x.shape[0]
    ts = min(MLP_FWD_ROWS, s)
    nj = N_DEV // 2

    def body(x_ref, nw_ref, sh_ref, sc_ref, g2_ref, wg_ref, wu_ref, wd_ref, *rest):
        if target is None:
            out_ref, mix_ref, h_ref, gate_ref, up_ref = rest
        else:
            t_ref, out_ref, mix_ref, h_ref, gate_ref, up_ref, loss_ref = rest
        j = pl.program_id(1)
        first_block = pl.program_id(0) == 0

        @pl.when(j == 0)
        def _():
            h_ref[...] = _f_modulate(x_ref[...], nw_ref[...], sh_ref[...], sc_ref[...]).astype(bf16)
            mix_ref[...] = jnp.zeros(mix_ref.shape, f32)

        nr = max(ts // 512, 1)
        half = ts // nr
        wg, wu, wd = wg_ref[...], wu_ref[...], wd_ref[...]
        products = lambda r: (mmw_t(h_ref[pl.ds(r * half, half), :], wg), mmw_t(h_ref[pl.ds(r * half, half), :], wu))
        ahead = products(0)
        for r in range(nr):
            gate, up = ahead
            if r + 1 < nr:
                ahead = products(r + 1)
            rows = pl.ds(r * half, half)
            gate_ref[rows, :] = gate.astype(bf16)
            up_ref[rows, :] = up.astype(bf16)
            mix_ref[rows, :] += mmw(jax.nn.silu(gate) * up, wd)

        @pl.when(j == nj - 1)
        def _():
            y = x_ref[...] + g2_ref[...] * mix_ref[...]
            if target is None:
                out_ref[...] = y
            else:
                d = y - t_ref[...]
                out_ref[...] = d * (1.0 / D_MODEL)
                part = 0.5 * jnp.sum(jnp.sum(d * d, axis=-1, keepdims=True) * (1.0 / D_MODEL), axis=0, keepdims=True)
                _accumulate(first_block, [loss_ref], [jnp.broadcast_to(part, (8, 128))])

    vec = _const((1, D_MODEL))
    tok = pl.BlockSpec((ts, D_MODEL), lambda i, j: (i, 0))
    wide = pl.BlockSpec((None, ts, FF_SHARD), lambda i, j: (j, i, 0))
    last = target is not None
    return pl.pallas_call(
        body, name="mlp_fwd_loss" if last else "mlp_fwd", grid=(s // ts, nj),
        in_specs=[tok, vec, vec, vec, vec,
                  pl.BlockSpec((None, FF_SHARD, D_MODEL), lambda i, j: (j, 0, 0)),
                  pl.BlockSpec((None, FF_SHARD, D_MODEL), lambda i, j: (j + nj, 0, 0)),
                  pl.BlockSpec((None, FF_SHARD, D_MODEL), lambda i, j: (j, 0, 0))] + [tok] * last,
        out_specs=[tok] * 3 + [wide] * 2 + [_const((8, 128))] * last,
        out_shape=[jax.ShapeDtypeStruct((s, D_MODEL), f32), jax.ShapeDtypeStruct((s, D_MODEL), f32),
                   jax.ShapeDtypeStruct((s, D_MODEL), bf16)] + [jax.ShapeDtypeStruct((nj, s, FF_SHARD), bf16)] * 2
                  + [jax.ShapeDtypeStruct((8, 128), f32)] * last,
    )(x, nw, sh, sc, g2, wgu, wgu, wd, *([target] if last else []))


def mlp_bwd(h, dy, gate, up, g2, wgu, wd, job=None):
    s = h.shape[0]
    ts = min(MLP_BWD_ROWS, s)
    nj = N_DEV // 2
    ni = s // ts

    rows_per = min(MLP_BWD_CHUNK, ts)

    def body(h_ref, dy_ref, gate_ref, up_ref, g2_ref, wg_ref, wu_ref, wd_ref, dh_ref, dwg_ref, dwu_ref, dwd_ref,
             ag_sc, au_sc, ad_sc, act_sc, dgate_sc, dup_sc, dmix_sc):
        i = pl.program_id(1)
        wg, wu, wd = wg_ref[...], wu_ref[...], wd_ref[...]
        g2 = g2_ref[...]
        for r in range(ts // rows_per):
            rows = pl.ds(r * rows_per, rows_per)
            act, vjp = jax.vjp(lambda g, u: jax.nn.silu(g) * u, gate_ref[rows, :].astype(f32), up_ref[rows, :].astype(f32))
            dmix = (dy_ref[rows, :] * g2).astype(bf16)
            dgate, dup = vjp(_dot(dmix, wd, 1, 1))
            dgate, dup = dgate.astype(bf16), dup.astype(bf16)
            dh_ref[rows, :] = (_dot(dgate, wg, 1, 0) + _dot(dup, wu, 1, 0)).astype(bf16)
            act_sc[rows, :] = act.astype(bf16)
            dgate_sc[rows, :] = dgate
            dup_sc[rows, :] = dup
            dmix_sc[rows, :] = dmix
        h = h_ref[...]
        grads = [_dot(dgate_sc[...], h, 0, 0), _dot(dup_sc[...], h, 0, 0), _dot(act_sc[...], dmix_sc[...], 0, 0)]
        _accumulate_then_cast(i == 0, i == ni - 1, [ag_sc, au_sc, ad_sc], [dwg_ref, dwu_ref, dwd_ref], grads)

    once = pl.Buffered(1)
    wspec = lambda off: pl.BlockSpec((None, FF_SHARD, D_MODEL), lambda j, i: (j + off, 0, 0), pipeline_mode=once)
    dspec = pl.BlockSpec((None, FF_SHARD, D_MODEL), lambda j, i: (j, 0, 0), pipeline_mode=once)
    wide = pl.BlockSpec((None, ts, FF_SHARD), lambda j, i: (j, i, 0))
    return _call_with_job(
        body, "mlp_bwd" if job is None else "mlp_bwd_comm", (nj, ni), job,
        in_specs=[pl.BlockSpec((ts, D_MODEL), lambda j, i: (i, 0)), pl.BlockSpec((ts, D_MODEL), lambda j, i: (i, 0)),
                  wide, wide, _const((1, D_MODEL)), wspec(0), wspec(nj), dspec],
        out_specs=[pl.BlockSpec((None, ts, D_MODEL), lambda j, i: (j, i, 0)), wspec(0), wspec(0), dspec],
        out_shape=[jax.ShapeDtypeStruct((nj, s, D_MODEL), bf16),
                   jax.ShapeDtypeStruct((nj, FF_SHARD, D_MODEL), bf16), jax.ShapeDtypeStruct((nj, FF_SHARD, D_MODEL), bf16),
                   jax.ShapeDtypeStruct((nj, FF_SHARD, D_MODEL), bf16)],
        scratch_shapes=[pltpu.VMEM((FF_SHARD, D_MODEL), f32), pltpu.VMEM((FF_SHARD, D_MODEL), f32),
                        pltpu.VMEM((FF_SHARD, D_MODEL), f32), pltpu.VMEM((ts, FF_SHARD), bf16),
                        pltpu.VMEM((ts, FF_SHARD), bf16), pltpu.VMEM((ts, FF_SHARD), bf16), pltpu.VMEM((ts, D_MODEL), bf16)],
        operands=(h, dy, gate, up, g2, wgu, wgu, wd))


def out_bwd(dy, dhparts, x, nw, sh, sc, mix, o, yg, g1, wo):
    s = dy.shape[0]
    ts = _token_block(s)
    nj = dhparts.shape[0]

    ni = s // ts

    def body(dy_ref, dp_ref, x_ref, nw_ref, sh_ref, sc_ref, mix_ref, o_ref, yg_ref, g1_ref, wo_ref,
             dx_ref, dnw_ref, dsh_ref, dsc_ref, do_ref, dyg_ref, dg1_ref, dg2_ref, dwo_ref, acc_sc):
        i = pl.program_id(0)
        g = dy_ref[...]
        _accumulate(i == 0, [dg2_ref], [jnp.sum(g * mix_ref[...], axis=0, keepdims=True)])
        dh = dp_ref[0].astype(f32)
        for j in range(1, nj):
            dh = dh + dp_ref[j].astype(f32)
        _, vjp_mod = jax.vjp(_f_modulate, x_ref[...], nw_ref[...], sh_ref[...], sc_ref[...])
        dx_mod, dnw, dsh, dsc = vjp_mod(dh)
        _accumulate(i == 0, [dnw_ref, dsh_ref, dsc_ref], [dnw, dsh, dsc])
        g = g + dx_mod
        dx_ref[...] = g
        o = o_ref[...]
        wo = wo_ref[...]
        _, vjp = jax.vjp(lambda o_, yg_, g1_, slot: _f_out(o_, yg_, g1_, wo, slot), o, yg_ref[...], g1_ref[...],
                         jnp.zeros(wo.shape, f32))
        do, dyg, dg1, dwo = vjp(g)
        delta = jnp.sum(do * o, axis=-1, keepdims=True)
        high = delta.astype(bf16)
        low = (delta - high.astype(f32)).astype(bf16)
        lane = lax.broadcasted_iota(jnp.int32, (1, 1, HEAD_LANES), 2)
        wide = jnp.concatenate([do.astype(bf16), jnp.zeros(do.shape, bf16)], axis=-1)
        do_ref[...] = jnp.where(lane == SPARE_V, -high, jnp.where(lane == SPARE_V + 1, -low, wide))
        dyg_ref[...] = dyg
        _accumulate(i == 0, [dg1_ref], [dg1])
        _accumulate_then_cast(i == 0, i == ni - 1, [acc_sc], [dwo_ref], [dwo])

    head = pl.BlockSpec((N_HEADS, ts, V_DIM), lambda i: (0, i, 0))
    tok = pl.BlockSpec((ts, D_MODEL), lambda i: (i, 0))
    vec = _const((1, D_MODEL))
    vshape = jax.ShapeDtypeStruct((1, D_MODEL), f32)
    return pl.pallas_call(
        body, name="out_bwd", grid=(ni,), scratch_shapes=[pltpu.VMEM((D_MODEL, D_MODEL), f32)],
        in_specs=[tok, pl.BlockSpec((nj, ts, D_MODEL), lambda i: (0, i, 0)), tok, vec, vec, vec, tok,
                  head, pl.BlockSpec((ts, D_SSD), lambda i: (i, 0)), vec, _const((D_MODEL, D_MODEL))],
        out_specs=[tok, vec, vec, vec, pl.BlockSpec((N_HEADS, ts, HEAD_LANES), lambda i: (0, i, 0)),
                   pl.BlockSpec((ts, D_SSD), lambda i: (i, 0)), vec, vec, _const((D_MODEL, D_MODEL))],
        out_shape=[jax.ShapeDtypeStruct((s, D_MODEL), f32), vshape, vshape, vshape,
                   jax.ShapeDtypeStruct((N_HEADS, s, HEAD_LANES), bf16), jax.ShapeDtypeStruct((s, D_SSD), f32),
                   vshape, vshape, jax.ShapeDtypeStruct((D_MODEL, D_MODEL), bf16)],
    )(dy, dhparts, x, nw, sh, sc, mix, o, yg, g1, wo)


def attn_bwd(qx, k, v, do, job=None):
    s = qx.shape[1]
    t = _token_block(s)
    nb = s // t

    hp = ATTN_HEADS_BWD

    def body(q_ref, k_ref, v_ref, do_ref, dq_ref, dk_ref, dv_ref, dv_sc):
        ki = pl.program_id(1)

        @pl.when(ki == 0)
        def _():
            dq_ref[...] = jnp.zeros(dq_ref.shape, f32)

        dk_ref[...] = jnp.zeros(dk_ref.shape, f32)
        dv_sc[...] = jnp.zeros(dv_sc.shape, f32)

        def step(q0, diagonal):
            half = t // 2
            subs = [(0, half, half), (half, half, t)] if diagonal and half % 128 == 0 else [(0, t, t)]
            chains = [(hh, sub) for hh in range(hp) for sub in subs]

            def products(hh, sub):
                r0, nr, nk = sub
                rows = pl.ds(q0 + r0, nr)
                sc = _scores(q_ref[hh, rows, :], k_ref[hh, :nk, :])
                dps = _scores(do_ref[hh, rows, :], v_ref[hh, :nk, :])
                return (jnp.where(_tril(nr, nk, r0), sc, NEG) if diagonal else sc), dps

            ahead = products(*chains[0])
            for c, (hh, (r0, nr, nk)) in enumerate(chains):
                sc, dps = ahead
                if c + 1 < len(chains):
                    ahead = products(*chains[c + 1])
                rows = pl.ds(q0 + r0, nr)
                p = jnp.exp2(sc)
                ds = (p * dps).astype(bf16)
                dv_sc[hh, :nk, :] += lax.dot_general(p.astype(bf16), do_ref[hh, rows, :], (((0,), (0,)), ((), ())),
                                                     preferred_element_type=f32)
                dk_ref[hh, :nk, :] += lax.dot_general(ds, q_ref[hh, rows, :], (((0,), (0,)), ((), ())),
                                                      preferred_element_type=f32)
                dq_ref[hh, rows, :] += jnp.dot(ds, k_ref[hh, :nk, :], preferred_element_type=f32)

        step(pl.multiple_of(ki * t, t), True)

        def above(qi, carry):
            step(pl.multiple_of(qi * t, t), False)
            return carry

        lax.fori_loop(ki + 1, nb, above, 0)
        real = lax.broadcasted_iota(jnp.int32, (1, 1, HEAD_LANES), 2) < SPARE_Q
        dk_ref[...] = jnp.where(real, dk_ref[...] * LN2, 0.0)
        dv_ref[...] = dv_sc[:, :, :V_DIM]

        @pl.when(ki == nb - 1)
        def _():
            dq_ref[...] = jnp.where(real, dq_ref[...] * LN2, 0.0)

    qspec = pl.BlockSpec((hp, s, HEAD_LANES), lambda h, ki: (h, 0, 0))
    kspec = lambda w: pl.BlockSpec((hp, t, w), lambda h, ki: (h, ki, 0))
    return _call_with_job(
        body, "attn_bwd" if job is None else "attn_bwd_comm", (N_HEADS // hp, nb), job,
        in_specs=[qspec, kspec(HEAD_LANES), kspec(HEAD_LANES), qspec],
        out_specs=[qspec, kspec(HEAD_LANES), kspec(V_DIM)],
        out_shape=[jax.ShapeDtypeStruct((N_HEADS, s, HEAD_LANES), f32), jax.ShapeDtypeStruct((N_HEADS, s, HEAD_LANES), f32),
                   jax.ShapeDtypeStruct((N_HEADS, s, V_DIM), f32)],
        scratch_shapes=[pltpu.VMEM((hp, t, HEAD_LANES), f32)], operands=(qx, k, v, do))


def ssd_bwd(px, pz, plast, states, dyg, params):
    s = px.shape[0]
    nc = s // CHUNK
    per = CHUNK // HALO

    def body(px_ref, halo_ref, pz_ref, pl_ref, st_ref, dyg_ref, cw_ref, cb_ref, dtb_ref, alog_ref, dskip_ref, snw_ref,
             dpx_ref, dpz_ref, dpl_ref, dcw_ref, dcb_ref, ddtb_ref, dalog_ref, ddskip_ref, dsnw_ref, dstate_sc, dhalo_sc):
        t = pl.program_id(0)
        chunk = nc - 1 - t

        @pl.when(t == 0)
        def _():
            dstate_sc[...] = jnp.zeros(dstate_sc.shape, f32)
            dhalo_sc[...] = jnp.zeros(dhalo_sc.shape, f32)

        halo = jnp.where(chunk > 0, halo_ref[...], 0.0)
        xext = jnp.concatenate([halo, px_ref[...]], axis=0)
        _, vjp = jax.vjp(_f_ssd, xext, pz_ref[...], pl_ref[...], st_ref[...], cw_ref[...], cb_ref[...], dtb_ref[...],
                         alog_ref[...], dskip_ref[...], snw_ref[...])
        dxext, dz, dpl, dprev, dcw, dcb, ddtb, dalog, ddskip, dsnw = vjp((dyg_ref[...], dstate_sc[...]))
        dpx_ref[...] = dxext[HALO:]
        dpx_ref[CHUNK - HALO:, :] += dhalo_sc[...]
        dhalo_sc[...] = dxext[:HALO]
        dstate_sc[...] = dprev
        dpz_ref[...] = dz
        dpl_ref[...] = dpl
        _accumulate(t == 0, [dcw_ref, dcb_ref, ddtb_ref, dalog_ref, ddskip_ref, dsnw_ref],
                    [dcw, dcb, ddtb, dalog, ddskip, dsnw])

    rev = lambda w: pl.BlockSpec((CHUNK, w), lambda t: (nc - 1 - t, 0))
    pshapes = [jax.ShapeDtypeStruct((4, D_CONV), f32), jax.ShapeDtypeStruct((1, D_CONV), f32),
               jax.ShapeDtypeStruct((1, 128), f32), jax.ShapeDtypeStruct((1, 128), f32),
               jax.ShapeDtypeStruct((1, 128), f32), jax.ShapeDtypeStruct((1, D_SSD), f32)]
    return pl.pallas_call(
        body, name="ssd_bwd", grid=(nc,),
        in_specs=[rev(D_CONV),
                  pl.BlockSpec((HALO, D_CONV), lambda t: (jnp.maximum((nc - 1 - t) * per - 1, 0), 0)),
                  rev(D_SSD), rev(128),
                  pl.BlockSpec((None, N_HEADS // 2, 2 * SSD_HEAD_DIM, SSD_STATE), lambda t: (nc - 1 - t, 0, 0, 0)),
                  rev(D_SSD)] + _ssd_param_specs(),
        out_specs=[rev(D_CONV), rev(D_SSD), rev(128)] + _ssd_param_specs(),
        out_shape=[jax.ShapeDtypeStruct((s, D_CONV), f32), jax.ShapeDtypeStruct((s, D_SSD), f32),
                   jax.ShapeDtypeStruct((s, 128), f32)] + pshapes,
        scratch_shapes=[pltpu.VMEM((N_HEADS // 2, 2 * SSD_HEAD_DIM, SSD_STATE), f32), pltpu.VMEM((HALO, D_CONV), f32)],
    )(px, px, pz, plast, states, dyg, *params)


def qkv_bwd(pa, plast, cos_t, sin_t, params, dq, dk, dv):
    s = pa.shape[0]
    ts = _token_block(s)

    def body(pa_ref, pl_ref, cos_ref, sin_ref, *rest):
        qaw, kvaw, wq, wk, wv, qnw, knw, kpw = [r[...] for r in rest[:8]]
        dq_ref, dk_ref, dv_ref = rest[8:11]
        dpa_ref, dpl_ref = rest[11:13]
        dprm_refs = list(rest[13:])
        cos_t, sin_t = cos_ref[...], sin_ref[...]

        def stage(pa_, pl_, qaw_, kvaw_, sq, sk, sv, qnw_, knw_, kpw_):
            return _f_qkv(pa_, pl_, cos_t, sin_t, qaw_, kvaw_, wq, wk, wv, qnw_, knw_, kpw_, (sq, sk, sv))

        _, vjp = jax.vjp(stage, pa_ref[...], pl_ref[...], qaw, kvaw, jnp.zeros(wq.shape, f32), jnp.zeros(wk.shape, f32),
                         jnp.zeros(wv.shape, f32), qnw, knw, kpw)
        grads = vjp((dq_ref[...], dk_ref[...], dv_ref[...]))
        dpa_ref[...] = grads[0]
        dpl_ref[...] = grads[1]
        _accumulate(pl.program_id(0) == 0, dprm_refs, list(grads[2:]))

    tok = lambda w: pl.BlockSpec((ts, w), lambda i: (i, 0))
    head = lambda w: pl.BlockSpec((N_HEADS, ts, w), lambda i: (0, i, 0))
    pshapes = [jax.ShapeDtypeStruct((1, Q_RANK), f32), jax.ShapeDtypeStruct((1, KV_RANK), f32),
               jax.ShapeDtypeStruct((N_HEADS, Q_RANK, HEAD_LANES), f32), jax.ShapeDtypeStruct((N_HEADS, KV_RANK, HEAD_LANES), f32),
               jax.ShapeDtypeStruct((N_HEADS, KV_RANK, V_DIM), f32), jax.ShapeDtypeStruct((1, HEAD_LANES), f32),
               jax.ShapeDtypeStruct((1, HEAD_LANES), f32), jax.ShapeDtypeStruct((1, HEAD_LANES), f32)]
    return pl.pallas_call(
        body, name="qkv_bwd", grid=(s // ts,),
        in_specs=[tok(384), tok(128), tok(128), tok(128)] + _qkv_param_specs()
                 + [head(HEAD_LANES), head(HEAD_LANES), head(V_DIM)],
        out_specs=[tok(384), tok(128)] + _qkv_param_specs(),
        out_shape=[jax.ShapeDtypeStruct((s, 384), f32), jax.ShapeDtypeStruct((s, 128), f32)] + pshapes,
    )(pa, plast, cos_t, sin_t, *params, dq, dk, dv)


def proj_bwd(x, nw, sh, sc, w, dpa, dpz, dpx, dpl_k, dpl_dt, dres):
    s = x.shape[0]
    ts = _token_block(s)

    ni = s // ts

    def body(x_ref, nw_ref, sh_ref, sc_ref, w_ref, dpa_ref, dpz_ref, dpx_ref, dplk_ref, dpld_ref, dres_ref,
             dx_ref, dnw_ref, dsh_ref, dsc_ref, dw_ref, acc_sc):
        i = pl.program_id(0)
        g = jnp.concatenate([dpa_ref[...], dpz_ref[...], dpx_ref[...], dplk_ref[...] + dpld_ref[...]], axis=1)
        w = w_ref[...]
        _, vjp = jax.vjp(lambda x_, nw_, sh_, sc_, slot: _f_proj(x_, nw_, sh_, sc_, w, slot), x_ref[...], nw_ref[...],
                         sh_ref[...], sc_ref[...], jnp.zeros(w.shape, f32))
        dx, dnw, dsh, dsc, dw = vjp(g)
        dx_ref[...] = dx + dres_ref[...]
        _accumulate(i == 0, [dnw_ref, dsh_ref, dsc_ref], [dnw, dsh, dsc])
        _accumulate_then_cast(i == 0, i == ni - 1, [acc_sc], [dw_ref], [dw])

    vec = _const((1, D_MODEL))
    vshape = jax.ShapeDtypeStruct((1, D_MODEL), f32)
    tok = lambda w_: pl.BlockSpec((ts, w_), lambda i: (i, 0))
    return pl.pallas_call(
        body, name="proj_bwd", grid=(ni,), scratch_shapes=[pltpu.VMEM((D_PROJ, D_MODEL), f32)],
        in_specs=[tok(D_MODEL), vec, vec, vec, _const((D_PROJ, D_MODEL)), tok(384), tok(512), tok(1024), tok(128), tok(128),
                  tok(D_MODEL)],
        out_specs=[tok(D_MODEL), vec, vec, vec, _const((D_PROJ, D_MODEL))],
        out_shape=[jax.ShapeDtypeStruct((s, D_MODEL), f32), vshape, vshape, vshape,
                   jax.ShapeDtypeStruct((D_PROJ, D_MODEL), bf16)],
    )(x, nw, sh, sc, w, dpa, dpz, dpx, dpl_k, dpl_dt, dres)


def ada_fwd(c_all, w_ada, b_cols):
    def body(c_ref, w_ref, b_ref, out_ref):
        act = jax.nn.silu(c_ref[...])
        for l in range(2):
            out_ref[l] = jnp.dot(act, w_ref[l], precision=lax.Precision.HIGHEST, preferred_element_type=f32) + b_ref[l]

    return pl.pallas_call(body, name="ada_fwd", out_shape=jax.ShapeDtypeStruct((2, N_DEV, 768), f32))(c_all, w_ada, b_cols)


def ada_bwd(c_all, dmod_cols):
    def body(c_ref, d_ref, out_ref):
        out_ref[0] = lax.dot_general(jax.nn.silu(c_ref[...]), d_ref[0], (((0,), (0,)), ((), ())),
                                     precision=lax.Precision.HIGHEST, preferred_element_type=f32)

    return pl.pallas_call(
        body, name="ada_bwd", grid=(2,),
        in_specs=[_const((N_DEV, D_MODEL)), pl.BlockSpec((1, N_DEV, 768), lambda l: (l, 0, 0))],
        out_specs=pl.BlockSpec((1, D_MODEL, 768), lambda l: (l, 0, 0)),
        out_shape=jax.ShapeDtypeStruct((2, D_MODEL, 768), f32),
    )(c_all, dmod_cols)


def _adamw(w, g, m, v):
    m = ADAM_B1 * m + (1.0 - ADAM_B1) * g
    v = ADAM_B2 * v + (1.0 - ADAM_B2) * (g * g)
    m_hat = m / (1.0 - ADAM_B1 ** ADAM_STEP)
    v_hat = v / (1.0 - ADAM_B2 ** ADAM_STEP)
    delta = -ADAM_LR * (m_hat / (jnp.sqrt(v_hat) + ADAM_EPS) + ADAM_WD * w)
    return delta, m, v


def adamw(parts, w, m, v, layer, prev, name):
    n, r, c = parts.shape
    nl = w.shape[0]
    per_elem = 2 * (n * parts.dtype.itemsize + 7 * 4)
    lanes = -(-c // 128) * 128
    tr, tc = r, c
    if per_elem * r * lanes > ADAMW_BLOCK_BYTES:
        fits = [t for t in range(r // 2, 15, -1) if r % t == 0 and t % 16 == 0 and per_elem * t * lanes <= ADAMW_BLOCK_BYTES]
        if fits:
            tr = fits[0]
        else:
            tc = next(t for t in (512, 256, 128) if c % t == 0)

    def body(p_ref, w_ref, m_ref, v_ref, *rest):
        g_ref, d_ref, nm_ref, nv_ref = rest[-4:]
        g = p_ref[0].astype(f32)
        for k in range(1, n):
            g = g + p_ref[k].astype(f32)
        delta, nm, nv = _adamw(w_ref[...], g, m_ref[...], v_ref[...])
        g_ref[...] = g
        d_ref[...] = delta
        nm_ref[...] = nm
        nv_ref[...] = nv

    blk = pl.BlockSpec((None, tr, tc), lambda i, j: (layer, i, j))
    shp = jax.ShapeDtypeStruct((nl, r, c), f32)
    kept = [] if prev is None else list(prev)
    return pl.pallas_call(
        body, name=name, grid=(r // tr, c // tc),
        in_specs=[pl.BlockSpec((n, tr, tc), lambda i, j: (0, i, j)), blk, blk, blk] + [ANY] * len(kept),
        out_specs=[blk] * 4, out_shape=[shp] * 4,
        input_output_aliases={4 + j: j for j in range(len(kept))},
    )(parts, w, m, v, *kept)


def adamw_two_layers(parts, w, m, v, name, job=None):
    n, r, c = parts[0].shape
    per_elem = 2 * (2 * n * parts[0].dtype.itemsize + 7 * 4)
    lanes = -(-c // 128) * 128
    tr = next(t for t in range(r, 15, -1) if r % t == 0 and t % 16 == 0 and per_elem * t * lanes <= ADAMW_BLOCK_BYTES)
    nblk = r // tr

    def body(p0_ref, p1_ref, w_ref, m_ref, v_ref, g_ref, d_ref, nm_ref, nv_ref):
        layer = pl.program_id(0)

        def update(p_ref):
            g = p_ref[0].astype(f32)
            for k in range(1, n):
                g = g + p_ref[k].astype(f32)
            delta, nm, nv = _adamw(w_ref[...], g, m_ref[...], v_ref[...])
            g_ref[...] = g
            d_ref[...] = delta
            nm_ref[...] = nm
            nv_ref[...] = nv

        @pl.when(layer == 0)
        def _():
            update(p0_ref)

        @pl.when(layer == 1)
        def _():
            update(p1_ref)

    blk = pl.BlockSpec((None, tr, c), lambda l, i: (l, i, 0))
    shp = jax.ShapeDtypeStruct((2, r, c), f32)
    return _call_with_job(
        body, name, (2, nblk), job,
        in_specs=[pl.BlockSpec((n, tr, c), lambda l, i: (0, i * (1 - l), 0)),
                  pl.BlockSpec((n, tr, c), lambda l, i: (0, i * l, 0)), blk, blk, blk],
        out_specs=[blk] * 4, out_shape=[shp] * 4, scratch_shapes=[], operands=(parts[0], parts[1], w, m, v),
        relay_at=(1, 0))


def adamw_layers_inside(parts, w, m, v, name):
    n, r, c = parts[0].shape
    nl = w.shape[1]
    tc = next(t for t in (256, 128) if c % t == 0)

    def body(*refs):
        p_refs = refs[:nl]
        w_ref, m_ref, v_ref, g_ref, d_ref, nm_ref, nv_ref = refs[nl:]
        for l in range(nl):
            g = p_refs[l][0].astype(f32)
            for k in range(1, n):
                g = g + p_refs[l][k].astype(f32)
            delta, nm, nv = _adamw(w_ref[:, l, :], g, m_ref[:, l, :], v_ref[:, l, :])
            g_ref[:, l, :] = g
            d_ref[:, l, :] = delta
            nm_ref[:, l, :] = nm
            nv_ref[:, l, :] = nv

    blk = pl.BlockSpec((r, nl, tc), lambda j: (0, 0, j))
    shp = jax.ShapeDtypeStruct((r, nl, c), f32)
    return pl.pallas_call(
        body, name=name, grid=(c // tc,),
        in_specs=[pl.BlockSpec((n, r, tc), lambda j: (0, 0, j))] * nl + [blk] * 3,
        out_specs=[blk] * 4, out_shape=[shp] * 4,
    )(*parts, w, m, v)


def _my_index():
    return 4 * lax.axis_index("x") + 2 * lax.axis_index("y") + lax.axis_index("c")


def _coords(idx):
    return (idx // 4, (idx // 2) % 2, idx % 2)


class CommJob:
    def __init__(self, operands, out_shape, phases, scratch):
        self.operands, self.out_shape, self.phases, self.scratch = operands, out_shape, phases, scratch


def _wait(out, n_blocks, send_sem, recv_sem, send=True, recv=True):
    span = out.at[pl.ds(0, n_blocks)]
    desc = pltpu.make_async_remote_copy(src_ref=span, dst_ref=span, send_sem=send_sem, recv_sem=recv_sem,
                                        device_id=_coords(_my_index()), device_id_type=MESH)
    if recv:
        desc.wait_recv()
    if send:
        desc.wait_send()


def gather_job(shards):
    n = len(shards)

    def places():
        x, y, c = lax.axis_index("x"), lax.axis_index("y"), lax.axis_index("c")
        return (x, y, c), (x, y, 1 - c), [(1 - x, y), (x, 1 - y), (1 - x, 1 - y)]

    def index(p):
        return 4 * p[0] + 2 * p[1] + p[2]

    def start(ins, outs, sems):
        far_send, far_recv, near_send, near_recv, local = sems
        me, sibling, chips = places()
        for k in range(n):
            pltpu.make_async_copy(ins[k], outs[k].at[index(me)], local.at[k]).start()
            for chip in chips:
                pltpu.make_async_remote_copy(src_ref=ins[k], dst_ref=outs[k].at[index(me)], send_sem=far_send.at[k],
                                             recv_sem=far_recv.at[k], device_id=(*chip, me[2]), device_id_type=MESH).start()
            pltpu.make_async_remote_copy(src_ref=ins[k], dst_ref=outs[k].at[index(me)], send_sem=near_send.at[k],
                                         recv_sem=near_recv.at[k], device_id=sibling, device_id_type=MESH).start()

    def relay(ins, outs, sems):
        far_send, far_recv, near_send, near_recv, local = sems
        me, sibling, chips = places()
        for k in range(n):
            _wait(outs[k], 3, far_send.at[k], far_recv.at[k], send=False)
            for chip in chips:
                block = outs[k].at[index((*chip, me[2]))]
                pltpu.make_async_remote_copy(src_ref=block, dst_ref=block, send_sem=near_send.at[k],
                                             recv_sem=near_recv.at[k], device_id=sibling, device_id_type=MESH).start()

    def finish(ins, outs, sems):
        far_send, far_recv, near_send, near_recv, local = sems
        for k in range(n):
            _wait(outs[k], 4, near_send.at[k], near_recv.at[k])
            _wait(outs[k], 3, far_send.at[k], far_recv.at[k], recv=False)
            pltpu.make_async_copy(ins[k], outs[k].at[0], local.at[k]).wait()

    shapes = [jax.ShapeDtypeStruct((N_DEV,) + tuple(a.shape), a.dtype) for a in shards]
    return CommJob(list(shards), shapes, [start, relay, finish], [pltpu.SemaphoreType.DMA((n,))] * 5)


def scatter_job(tensors):
    n = len(tensors)
    flat, where = [], {}
    for k, pieces in enumerate(tensors):
        d = 0
        for piece in pieces:
            for b in range(piece.shape[0]):
                where[k, d] = (len(flat), b)
                d += 1
            flat.append(piece)
        assert d == N_DEV

    def start(ins, outs, sems):
        send_sems, recv_sems, local_sems = sems
        me = _my_index()

        def block(k, d):
            i, b = where[k, d]
            return ins[i].at[b]

        for d in range(N_DEV):
            @pl.when(d != me)
            def _():
                for k in range(n):
                    pltpu.make_async_remote_copy(src_ref=block(k, d), dst_ref=outs[k].at[me], send_sem=send_sems.at[k],
                                                 recv_sem=recv_sems.at[k], device_id=(d // 4, (d // 2) % 2, d % 2),
                                                 device_id_type=MESH).start()

            @pl.when(d == me)
            def _():
                for k in range(n):
                    pltpu.make_async_copy(block(k, d), outs[k].at[d], local_sems.at[k]).start()

    def finish(ins, outs, sems):
        send_sems, recv_sems, local_sems = sems
        for k in range(n):
            _wait(outs[k], N_DEV - 1, send_sems.at[k], recv_sems.at[k])
            i, b = where[k, 0]
            pltpu.make_async_copy(ins[i].at[b], outs[k].at[0], local_sems.at[k]).wait()

    shapes = [jax.ShapeDtypeStruct((N_DEV,) + tuple(p[0].shape[1:]), p[0].dtype) for p in tensors]
    return CommJob(flat, shapes, [start, finish], [pltpu.SemaphoreType.DMA((n,))] * 3)


def merge_jobs(a, b):
    def on(job, off):
        oi, oo, os_ = off
        ni, no, ns = len(job.operands), len(job.out_shape), len(job.scratch)
        return lambda phase: (lambda ins, outs, sems: phase(ins[oi:oi + ni], outs[oo:oo + no], sems[os_:os_ + ns]))

    wrap_a = on(a, (0, 0, 0))
    wrap_b = on(b, (len(a.operands), len(a.out_shape), len(a.scratch)))
    pa, pb = [wrap_a(p) for p in a.phases], [wrap_b(p) for p in b.phases]

    def together(*phases):
        def run(ins, outs, sems):
            for p in phases:
                p(ins, outs, sems)
        return run

    middle = pa[1:-1] + pb[1:-1]
    phases = [together(pa[0], pb[0])] + middle + [together(pa[-1], pb[-1])]
    return CommJob(a.operands + b.operands, a.out_shape + b.out_shape, phases, a.scratch + b.scratch)


def comm_call(job, name):
    ni, no = len(job.operands), len(job.out_shape)

    def body(*refs):
        ins, outs, sems = refs[:ni], refs[ni:ni + no], refs[ni + no:]
        for phase in job.phases:
            phase(ins, outs, sems)

    return pl.pallas_call(body, name=name, in_specs=[ANY] * ni, out_specs=[ANY] * no, out_shape=job.out_shape,
                          scratch_shapes=job.scratch)(*job.operands)


def _carry(job, body, n_in, n_out, at_step):
    ji, jo, js = len(job.operands), len(job.out_shape), len(job.scratch)

    def carrier(*refs):
        a, b = n_in, n_in + ji
        c, d = b + n_out, b + n_out + jo
        e = len(refs) - js
        job_refs = (refs[a:b], refs[c:d], refs[e:])
        n = len(job.phases)

        @pl.when(at_step(0, n))
        def _():
            job.phases[0](*job_refs)

        body(*refs[:a], *refs[b:c], *refs[d:e])

        for i in range(1, n):
            @pl.when(at_step(i, n))
            def _():
                job.phases[i](*job_refs)

    return carrier


def _pad_lanes(v, lo, total=128):
    return jnp.pad(v, (lo, total - lo - v.shape[0]))[None, :]


MIXER_WEIGHTS = ("w_in", "w_q_up", "w_kv_up", "conv_w")
LATE_WEIGHTS = ("w_out", "w_gate_up", "w_down")


def mixer_operands(g, sw):
    w_in = g["w_in"].reshape(D_IN, D_MODEL)
    zero = lambda rows: jnp.zeros((rows, D_MODEL), w_in.dtype)
    w_proj = jnp.concatenate(
        [w_in[:384], w_in[416:928], w_in[928:1952], w_in[1952:1960], zero(56), w_in[384:416], zero(32)], axis=0)
    wq = jnp.pad(g["w_q_up"], ((0, 0), (0, 0), (0, HEAD_LANES - NOPE - ROPE)))
    wk = jnp.pad(g["w_kv_up"][:, :, :NOPE], ((0, 0), (0, 0), (0, HEAD_LANES - NOPE)))
    wv = g["w_kv_up"][:, :, NOPE:]
    qkv = (sw["q_a_norm_w"][None, :], sw["kv_a_norm_w"][None, :], wq, wk, wv,
           _pad_lanes(jnp.concatenate([sw["q_nope_norm_w"], sw["q_pe_norm_w"]]), 0),
           _pad_lanes(sw["k_nope_norm_w"], 0), _pad_lanes(sw["k_pe_norm_w"], NOPE))
    conv_w = g["conv_w"].astype(f32).transpose(1, 0, 2).reshape(4, D_CONV)
    ssd = (conv_w, sw["conv_b"][None, :], _pad_lanes(sw["dt_bias"], 0), _pad_lanes(sw["a_log"], 0),
           _pad_lanes(sw["d_skip"], 0), sw["ssd_norm_w"][None, :])
    return dict(w_proj=w_proj, qkv=qkv, ssd=ssd, n1=sw["norm1_w"][None, :])


def late_operands(g, sw):
    return dict(wo=g["w_out"].reshape(D_MODEL, D_MODEL), wgu=g["w_gate_up"],
                wd=g["w_down"].reshape(N_DEV // 2, FF_SHARD, D_MODEL), n2=sw["norm2_w"][None, :])


def layer_fwd(x, mod, kw, cos_t, sin_t, job=None, late=None, target=None):
    sh1, sc1, g1, sh2, sc2, g2 = [mod[i:i + 1] for i in range(6)]
    pa, pz, px, plast = proj_fwd(x, kw["n1"], sh1, sc1, kw["w_proj"])
    q, k, v = qkv_fwd(pa, plast, cos_t, sin_t, kw["qkv"])
    (o, qx), carried = attn_fwd(q, k, v, job)
    if late is not None:
        kw = {**kw, **late(carried)}
    yg, states = ssd_fwd(px, pz, plast, kw["ssd"])
    x_mid = out_fwd(x, o, yg, g1, kw["wo"])
    x_out, mix, h_mid, gate, up, *loss_part = mlp_fwd(x_mid, kw["n2"], sh2, sc2, g2, kw["wgu"], kw["wd"], target)
    saved = dict(x=x, pa=pa, pz=pz, px=px, plast=plast, qx=qx, k=k, v=v, o=o, yg=yg, states=states, x_mid=x_mid,
                 mix=mix, h_mid=h_mid, gate=gate, up=up)
    return (x_out if target is None else (x_out, loss_part[0])), saved, kw, carried


def layer_bwd_head(dy, mod, kw, sv, job=None):
    _, _, g1, sh2, sc2, g2 = [mod[i:i + 1] for i in range(6)]
    (dhparts, dwg, dwu, dwd), carried = mlp_bwd(sv["h_mid"], dy, sv["gate"], sv["up"], g2, kw["wgu"], kw["wd"], job)
    dmid, dn2, dsh2, dsc2, do, dyg, dg1, dg2, dwo = out_bwd(
        dy, dhparts, sv["x_mid"], kw["n2"], sh2, sc2, sv["mix"], sv["o"], sv["yg"], g1, kw["wo"])
    early = dict(w_out=[dwo.reshape(N_DEV, D_MODEL // N_DEV, D_MODEL)], w_gate_up=[dwg, dwu],
                 w_down=[dwd.reshape(N_DEV, D_FF // N_DEV, D_MODEL)])
    head = dict(dmid=dmid, do=do, dyg=dyg, dn2=dn2, dsh2=dsh2, dsc2=dsc2, dg2=dg2, dg1=dg1)
    return head, early, carried


def layer_bwd_tail(hd, mod, kw, cos_t, sin_t, sv, job=None):
    sh1, sc1 = mod[0:1], mod[1:2]
    (dq, dk, dv), carried = attn_bwd(sv["qx"], sv["k"], sv["v"], hd["do"], job)
    dpx, dpz, dpl_dt, dcw, dcb, ddtb, dalog, ddskip, dsnw = ssd_bwd(sv["px"], sv["pz"], sv["plast"], sv["states"],
                                                                   hd["dyg"], kw["ssd"])
    dpa, dpl_k, dqaw, dkvaw, dwq, dwk, dwv, dqnw, dknw, dkpw = qkv_bwd(sv["pa"], sv["plast"], cos_t, sin_t, kw["qkv"],
                                                                       dq, dk, dv)
    dx, dn1, dsh1, dsc1, dwp = proj_bwd(sv["x"], kw["n1"], sh1, sc1, kw["w_proj"], dpa, dpz, dpx, dpl_k, dpl_dt, hd["dmid"])
    dmod = jnp.concatenate([dsh1, dsc1, hd["dg1"], hd["dsh2"], hd["dsc2"], hd["dg2"]], axis=0)
    dw_in = jnp.concatenate([dwp[:384], dwp[1984:2016], dwp[384:1920], dwp[1920:1928]], axis=0)
    grads = dict(
        norm1_w=dn1[0], norm2_w=hd["dn2"][0], q_a_norm_w=dqaw[0], kv_a_norm_w=dkvaw[0],
        q_nope_norm_w=dqnw[0, :NOPE], q_pe_norm_w=dqnw[0, NOPE:NOPE + ROPE], k_nope_norm_w=dknw[0, :NOPE],
        k_pe_norm_w=dkpw[0, NOPE:NOPE + ROPE], conv_b=dcb[0], dt_bias=ddtb[0, :N_HEADS], a_log=dalog[0, :N_HEADS],
        d_skip=ddskip[0, :N_HEADS], ssd_norm_w=dsnw[0],
        w_in=[dw_in.reshape(N_DEV, D_IN // N_DEV, D_MODEL)],
        w_q_up=[dwq[:, :, :NOPE + ROPE].astype(bf16)],
        w_kv_up=[jnp.concatenate([dwk[:, :, :NOPE], dwv], axis=2).astype(bf16)],
        conv_w=[dcw.reshape(4, N_DEV, D_CONV // N_DEV).transpose(1, 0, 2).astype(bf16)],
    )
    return dx, dmod, grads, carried


def _pack_small(get, last=None):
    flat = jnp.concatenate([get(name).reshape(-1) for name, _ in SMALL])
    flat = jnp.pad(flat, (0, SMALL_ROWS * 128 - flat.shape[0]))
    if last is not None:
        flat = flat.at[-1].set(last)
    return flat.reshape(SMALL_ROWS, 128)


def _unpack_small(packed):
    flat = packed.reshape(-1)
    out, off = {}, 0
    for name, size in SMALL:
        out[name] = flat[off:off + 2 * size].reshape(2, size)
        off += 2 * size
    return out


def kernel(x, c, positions, norm1_w, norm2_w, w_ada, b_ada, w_in, q_a_norm_w, w_q_up, kv_a_norm_w, w_kv_up, q_nope_norm_w, q_pe_norm_w, k_nope_norm_w, k_pe_norm_w, conv_w, conv_b, dt_bias, a_log, d_skip, ssd_norm_w, w_out, w_gate_up, w_down, loss_target, m_norm1_w, m_norm2_w, m_w_ada, m_b_ada, m_w_in, m_q_a_norm_w, m_w_q_up, m_kv_a_norm_w, m_w_kv_up, m_q_nope_norm_w, m_q_pe_norm_w, m_k_nope_norm_w, m_k_pe_norm_w, m_conv_w, m_conv_b, m_dt_bias, m_a_log, m_d_skip, m_ssd_norm_w, m_w_out, m_w_gate_up, m_w_down, v_norm1_w, v_norm2_w, v_w_ada, v_b_ada, v_w_in, v_q_a_norm_w, v_w_q_up, v_kv_a_norm_w, v_w_kv_up, v_q_nope_norm_w, v_q_pe_norm_w, v_k_nope_norm_w, v_k_pe_norm_w, v_conv_w, v_conv_b, v_dt_bias, v_a_log, v_d_skip, v_ssd_norm_w, v_w_out, v_w_gate_up, v_w_down):
    w = dict(norm1_w=norm1_w, norm2_w=norm2_w, w_ada=w_ada, b_ada=b_ada, w_in=w_in, q_a_norm_w=q_a_norm_w, w_q_up=w_q_up,
             kv_a_norm_w=kv_a_norm_w, w_kv_up=w_kv_up, q_nope_norm_w=q_nope_norm_w, q_pe_norm_w=q_pe_norm_w,
             k_nope_norm_w=k_nope_norm_w, k_pe_norm_w=k_pe_norm_w, conv_w=conv_w, conv_b=conv_b, dt_bias=dt_bias,
             a_log=a_log, d_skip=d_skip, ssd_norm_w=ssd_norm_w, w_out=w_out, w_gate_up=w_gate_up, w_down=w_down)
    m = dict(norm1_w=m_norm1_w, norm2_w=m_norm2_w, w_ada=m_w_ada, b_ada=m_b_ada, w_in=m_w_in, q_a_norm_w=m_q_a_norm_w,
             w_q_up=m_w_q_up, kv_a_norm_w=m_kv_a_norm_w, w_kv_up=m_w_kv_up, q_nope_norm_w=m_q_nope_norm_w,
             q_pe_norm_w=m_q_pe_norm_w, k_nope_norm_w=m_k_nope_norm_w, k_pe_norm_w=m_k_pe_norm_w, conv_w=m_conv_w,
             conv_b=m_conv_b, dt_bias=m_dt_bias, a_log=m_a_log, d_skip=m_d_skip, ssd_norm_w=m_ssd_norm_w, w_out=m_w_out,
             w_gate_up=m_w_gate_up, w_down=m_w_down)
    v = dict(norm1_w=v_norm1_w, norm2_w=v_norm2_w, w_ada=v_w_ada, b_ada=v_b_ada, w_in=v_w_in, q_a_norm_w=v_q_a_norm_w,
             w_q_up=v_w_q_up, kv_a_norm_w=v_kv_a_norm_w, w_kv_up=v_w_kv_up, q_nope_norm_w=v_q_nope_norm_w,
             q_pe_norm_w=v_q_pe_norm_w, k_nope_norm_w=v_k_nope_norm_w, k_pe_norm_w=v_k_pe_norm_w, conv_w=v_conv_w,
             conv_b=v_conv_b, dt_bias=v_dt_bias, a_log=v_a_log, d_skip=v_d_skip, ssd_norm_w=v_ssd_norm_w, w_out=v_w_out,
             w_gate_up=v_w_gate_up, w_down=v_w_down)
    me = _my_index()
    seq = x.shape[1]

    def shard(name, l):
        if name == "conv_w":
            return w[name][l]
        if name in TRANSPOSED:
            return jnp.swapaxes(w[name][l], 0, 1).astype(bf16)
        return w[name][l].astype(bf16)

    def shards(names, l):
        return [shard(name, l) for name in names]

    small = [{name: w[name][l] for name, _ in SMALL if name != "b_ada"} for l in range(2)]
    n_late = len(LATE_WEIGHTS)

    inv_freq = 1.0 / (ROPE_THETA ** (jnp.arange(0, ROPE, 2, dtype=f32) / ROPE))
    inv = _pad_lanes(jnp.concatenate([inv_freq, inv_freq]), NOPE)
    (cos_t, sin_t), first = rope_tables(positions.reshape(seq, 1), inv, gather_job([c] + shards(MIXER_WEIGHTS, 0)))
    c_all = first[0].reshape(N_DEV, D_MODEL)
    kws = [mixer_operands(dict(zip(MIXER_WEIGHTS, first[1:])), small[0]), None]

    b_cols = lax.dynamic_slice_in_dim(b_ada, me * 768, 768, axis=1)
    mod_cols = ada_fwd(c_all, w_ada, b_cols)
    (mod_all,) = comm_call(gather_job([mod_cols]), "gather_mod")
    mod_me = lax.dynamic_index_in_dim(mod_all, me, axis=2, keepdims=False)
    mods = [mod_me[:, l, :].reshape(6, D_MODEL) for l in range(2)]

    saved = [None, None]
    h, saved[0], kws[0], got = layer_fwd(
        x[0], mods[0], kws[0], cos_t, sin_t, gather_job(shards(LATE_WEIGHTS, 0) + shards(MIXER_WEIGHTS, 1)),
        lambda got: late_operands(dict(zip(LATE_WEIGHTS, got[:n_late])), small[0]))
    kws[1] = mixer_operands(dict(zip(MIXER_WEIGHTS, got[n_late:])), small[1])
    (dy, loss_part), saved[1], kws[1], _ = layer_fwd(
        h, mods[1], kws[1], cos_t, sin_t, gather_job(shards(LATE_WEIGHTS, 1)),
        lambda got: late_operands(dict(zip(LATE_WEIGHTS, got)), small[1]), loss_target[0])

    early, late = ("w_out", "w_gate_up", "w_down"), ("w_in", "w_q_up", "w_kv_up", "conv_w")
    parts = [{}, {}]
    head, pieces, _ = layer_bwd_head(dy, mods[1], kws[1], saved[1])
    dy, dmod1, grads1, got = layer_bwd_tail(head, mods[1], kws[1], cos_t, sin_t, saved[1], scatter_job([pieces[n] for n in early]))
    parts[1].update(zip(early, got))
    head, pieces, got = layer_bwd_head(dy, mods[0], kws[0], saved[0], scatter_job([grads1[n] for n in late]))
    parts[1].update(zip(late, got))
    dy, dmod0, grads0, got = layer_bwd_tail(head, mods[0], kws[0], cos_t, sin_t, saved[0], scatter_job([pieces[n] for n in early]))
    parts[0].update(zip(early, got))
    grad_x = dy[None]

    small_part = {name: jnp.stack([grads0[name], grads1[name]]) for name, _ in SMALL if name != "b_ada"}
    small_part["b_ada"] = jnp.stack([dmod0.reshape(-1), dmod1.reshape(-1)])
    swap = lambda a: jnp.swapaxes(a, 1, 2)
    updated, last = adamw_two_layers(
        [parts[l]["w_gate_up"] for l in range(2)], swap(w_gate_up), swap(m_w_gate_up), swap(v_w_gate_up), "adamw_w_gate_up_comm",
        merge_jobs(scatter_job([grads0[n] for n in late]),
                   gather_job([_pack_small(lambda n: small_part[n], loss_part[0, 0])])))
    parts[0].update(zip(late, last[:len(late)]))
    small_all = last[len(late)]
    packed = adamw(small_all, _pack_small(lambda n: w[n])[None], _pack_small(lambda n: m[n])[None],
                   _pack_small(lambda n: v[n])[None], 0, None, "adamw_small")
    loss = packed[0][0, -1, -1]
    res = {}
    for key, arr in zip("gdmv", packed):
        for name, val in _unpack_small(arr[0]).items():
            res[key, name] = val

    off = 2 * (1024 + 1024)
    dmod_all = small_all.reshape(N_DEV, -1)[:, off:off + 2 * 6144].reshape(N_DEV, 2, 6144)
    dmod_cols = lax.dynamic_slice_in_dim(dmod_all, me * 768, 768, axis=2).transpose(1, 0, 2)
    g_ada = ada_bwd(c_all, dmod_cols)
    out = None
    for l in range(2):
        out = adamw(g_ada[l][None], w_ada, m_w_ada, v_w_ada, l, out, "adamw_w_ada")
    res.update(zip([(key, "w_ada") for key in "gdmv"], out))

    inside = lambda a: jnp.transpose(a, (2, 0, 1))
    out = adamw_layers_inside([parts[l]["w_in"] for l in range(2)], inside(w_in), inside(m_w_in), inside(v_w_in), "adamw_w_in")
    res.update(zip([(key, "w_in") for key in "gdmv"], [jnp.transpose(a, (1, 2, 0)) for a in out]))
    res.update(zip([(key, "w_gate_up") for key in "gdmv"], [swap(a) for a in updated]))
    for name in BIG:
        if name in ("w_in", "w_gate_up"):
            continue
        view = (lambda a: jnp.swapaxes(a, 1, 2)) if name in TRANSPOSED else (lambda a: a)
        out = None
        for l in range(2):
            out = adamw(parts[l][name], view(w[name]), view(m[name]), view(v[name]), l, out, "adamw_" + name)
        res.update(zip([(key, name) for key in "gdmv"], [view(a) for a in out]))

    return (loss, grad_x, *[res["g", n] for n in WEIGHTS], *[res["d", n] for n in WEIGHTS],
            *[res["m", n] for n in WEIGHTS], *[res["v", n] for n in WEIGHTS])
```

```python
import functools

import jax
import jax.numpy as jnp
from jax import lax
from jax.experimental import pallas as pl
from jax.experimental.pallas import tpu as pltpu

f32 = jnp.float32
bf16 = jnp.bfloat16

N_DEV = 8
D_MODEL = 1024
N_HEADS = 8
HEAD_LANES = 128
NOPE = 64
ROPE = 32
V_DIM = 64
Q_RANK = 256
KV_RANK = 128
D_SSD = 512
D_CONV = 1024
SSD_STATE = 128
SSD_HEAD_DIM = 64
CHUNK = 128
HALO = 8
D_FF = 2816
FF_SHARD = 704
D_IN = 1960
D_PROJ = 2048
EPS = 1e-6
LOG2E = 1.4426950408889634
LN2 = 0.6931471805599453
Q_SCALE = (NOPE + ROPE) ** -0.5 * LOG2E
SPARE_Q = NOPE + ROPE
SPARE_V = V_DIM
ATTN_BLOCK_FWD = 1024
ATTN_ROWS_FWD = 256
ATTN_HEADS_FWD = 8
ATTN_HEADS_BWD = 4
MLP_FWD_ROWS = 1024
MLP_BWD_CHUNK = 256
MLP_BWD_ROWS = 1024
ROPE_THETA = 10000.0
NEG = -1e30

ADAM_LR = 0.001
ADAM_B1 = 0.9
ADAM_B2 = 0.999
ADAM_EPS = 1e-08
ADAM_WD = 0.01
ADAM_STEP = 10
ADAMW_BLOCK_BYTES = 36 << 20

MESH = pl.DeviceIdType.MESH
ANY = pl.BlockSpec(memory_space=pl.ANY)

SMALL = (("norm1_w", 1024), ("norm2_w", 1024), ("b_ada", 6144), ("q_a_norm_w", 256), ("kv_a_norm_w", 128),
         ("q_nope_norm_w", 64), ("q_pe_norm_w", 32), ("k_nope_norm_w", 64), ("k_pe_norm_w", 32),
         ("conv_b", 1024), ("dt_bias", 8), ("a_log", 8), ("d_skip", 8), ("ssd_norm_w", 512))
SMALL_ROWS = 168
BIG = ("w_in", "w_q_up", "w_kv_up", "conv_w", "w_out", "w_gate_up", "w_down")
TRANSPOSED = ("w_in", "w_gate_up")
WEIGHTS = ("norm1_w", "norm2_w", "w_ada", "b_ada", "w_in", "q_a_norm_w", "w_q_up", "kv_a_norm_w", "w_kv_up",
           "q_nope_norm_w", "q_pe_norm_w", "k_nope_norm_w", "k_pe_norm_w", "conv_w", "conv_b", "dt_bias",
           "a_log", "d_skip", "ssd_norm_w", "w_out", "w_gate_up", "w_down")


def _dot(a, b, ca, cb):
    return lax.dot_general(a.astype(bf16), b.astype(bf16), (((ca,), (cb,)), ((), ())), preferred_element_type=f32)


@jax.custom_vjp
def mm(a, b):
    return _dot(a, b, 1, 0)


def _mm_fwd(a, b):
    return _dot(a, b, 1, 0), (a, b)


def _mm_bwd(res, g):
    a, b = res
    return _dot(g, b, 1, 1).astype(a.dtype), _dot(a, g, 0, 0).astype(b.dtype)


mm.defvjp(_mm_fwd, _mm_bwd)


@jax.custom_vjp
def _mm_slot(a, w, slot):
    return _dot(a, w, 1, 0)


def _mm_slot_fwd(a, w, slot):
    return _dot(a, w, 1, 0), (a, w)


def _mm_slot_bwd(res, g):
    a, w = res
    return _dot(g, w, 1, 1).astype(a.dtype), None, _dot(a, g, 0, 0)


_mm_slot.defvjp(_mm_slot_fwd, _mm_slot_bwd)


def mmw(a, w, slot=None):
    return _dot(a, w, 1, 0) if slot is None else _mm_slot(a, w, slot)


@jax.custom_vjp
def _mm_slot_t(a, wt, slot):
    return _dot(a, wt, 1, 1)


def _mm_slot_t_fwd(a, wt, slot):
    return _dot(a, wt, 1, 1), (a, wt)


def _mm_slot_t_bwd(res, g):
    a, wt = res
    return _dot(g, wt, 1, 0).astype(a.dtype), None, _dot(g, a, 0, 0)


_mm_slot_t.defvjp(_mm_slot_t_fwd, _mm_slot_t_bwd)


def mmw_t(a, wt, slot=None):
    return _dot(a, wt, 1, 1) if slot is None else _mm_slot_t(a, wt, slot)


@jax.custom_vjp
def mm_nt(a, b):
    return _dot(a, b, 1, 1)


def _mm_nt_fwd(a, b):
    return _dot(a, b, 1, 1), (a, b)


def _mm_nt_bwd(res, g):
    a, b = res
    return _dot(g, b, 1, 0).astype(a.dtype), _dot(g, a, 0, 0).astype(b.dtype)


mm_nt.defvjp(_mm_nt_fwd, _mm_nt_bwd)


@jax.custom_vjp
def mm_tn(a, b):
    return _dot(a, b, 0, 0)


def _mm_tn_fwd(a, b):
    return _dot(a, b, 0, 0), (a, b)


def _mm_tn_bwd(res, g):
    a, b = res
    return _dot(b, g, 1, 1).astype(a.dtype), _dot(a, g, 1, 0).astype(b.dtype)


mm_tn.defvjp(_mm_tn_fwd, _mm_tn_bwd)


def _rms(x, w):
    return x * lax.rsqrt(jnp.mean(x * x, axis=-1, keepdims=True) + EPS) * w


def _const(shape):
    n = len(shape)
    return pl.BlockSpec(shape, lambda *_: (0,) * n)


def _accumulate(first, refs, vals):
    @pl.when(first)
    def _():
        for r, v in zip(refs, vals):
            r[...] = v

    @pl.when(jnp.logical_not(first))
    def _():
        for r, v in zip(refs, vals):
            r[...] += v


def _accumulate_then_cast(first, last, accs, outs, vals):
    _accumulate(first, accs, vals)

    @pl.when(last)
    def _():
        for a, o in zip(accs, outs):
            o[...] = a[...].astype(o.dtype)


def _token_block(s):
    return min(512, s)


def _f_proj(x, nw, sh, sc, w, slot=None):
    h = _rms(x, nw) * (1.0 + sc) + sh
    return mmw_t(h, w, slot)


def _f_qkv(pa, plast, cos_t, sin_t, qaw, kvaw, wq, wk, wv, qnw, knw, kpw, slots=None):
    sq, sk, sv = slots if slots is not None else ([None] * N_HEADS,) * 3
    lane = lax.broadcasted_iota(jnp.int32, (1, HEAD_LANES), 1)
    m_nope = lane < NOPE
    m_pe = (lane >= NOPE) & (lane < NOPE + ROPE)
    rows = pa.shape[0]

    def rope(t):
        half = ROPE // 2
        swapped = jnp.concatenate(
            [jnp.zeros((rows, NOPE), f32), t[:, NOPE + half:NOPE + ROPE], t[:, NOPE:NOPE + half],
             jnp.zeros((rows, HEAD_LANES - NOPE - ROPE), f32)], axis=1)
        return t * cos_t + swapped * sin_t

    qa = _rms(pa[:, :Q_RANK], qaw)
    kva = _rms(pa[:, Q_RANK:Q_RANK + KV_RANK], kvaw)
    kp = jnp.where(m_pe, plast, 0.0)
    kp = kp * lax.rsqrt(jnp.sum(kp * kp, axis=-1, keepdims=True) / ROPE + EPS) * kpw
    k_rot = rope(kp)
    qs, ks, vs = [], [], []
    for h in range(N_HEADS):
        qh = mmw(qa, wq[h], sq[h])
        ss_n = jnp.sum(jnp.where(m_nope, qh * qh, 0.0), axis=-1, keepdims=True) / NOPE
        ss_p = jnp.sum(jnp.where(m_pe, qh * qh, 0.0), axis=-1, keepdims=True) / ROPE
        r = jnp.where(m_nope, lax.rsqrt(ss_n + EPS), lax.rsqrt(ss_p + EPS))
        qs.append(rope(qh * r * qnw) * Q_SCALE)
        kh = mmw(kva, wk[h], sk[h])
        kh = kh * lax.rsqrt(jnp.sum(kh * kh, axis=-1, keepdims=True) / NOPE + EPS) * knw
        ks.append(kh + k_rot)
        vs.append(mmw(kva, wv[h], sv[h]))
    return jnp.stack(qs), jnp.stack(ks), jnp.stack(vs)


def _f_ssd(xext, z, plast, prev, cw, cb, dtb, alog, dskip, snw):
    n = CHUNK
    conv = cb
    for k in range(4):
        conv = conv + cw[k:k + 1] * xext[HALO - 3 + k:HALO - 3 + k + n]
    xc = jax.nn.silu(conv)
    xs, bm, cm = xc[:, :D_SSD], xc[:, D_SSD:D_SSD + 2 * SSD_STATE], xc[:, D_SSD + 2 * SSD_STATE:]
    lane = lax.broadcasted_iota(jnp.int32, (1, 128), 1)
    dt = jax.nn.softplus(jnp.where(lane < N_HEADS, plast, 0.0) + dtb)
    adt = dt * (-jnp.exp(alog))
    row = lax.broadcasted_iota(jnp.int32, (n, n), 0)
    col = lax.broadcasted_iota(jnp.int32, (n, n), 1)
    tri = row >= col
    acs = jnp.dot(tri.astype(f32), adt, precision=lax.Precision.HIGHEST, preferred_element_type=f32)
    acs_t = acs.T
    bgs = [bm[:, g * SSD_STATE:(g + 1) * SSD_STATE] for g in range(2)]
    cgs = [cm[:, g * SSD_STATE:(g + 1) * SSD_STATE] for g in range(2)]
    cb_ts = [mm_nt(cgs[g], bgs[g]) for g in range(2)]
    low = lane < SSD_HEAD_DIM
    low_rows = lax.broadcasted_iota(jnp.int32, (2 * SSD_HEAD_DIM, 1), 0) < SSD_HEAD_DIM

    def both(a0, a1):
        return jnp.where(low, a0, a1)

    pre = []
    for i in range(N_HEADS // 2):
        h0, h1 = 2 * i, 2 * i + 1
        col0, col1 = acs[:, h0:h0 + 1], acs[:, h1:h1 + 1]
        last0, last1 = acs[n - 1:n, h0:h0 + 1], acs[n - 1:n, h1:h1 + 1]
        cb_t = cb_ts[i // 2]
        scores0 = cb_t * jnp.exp(jnp.where(tri, col0 - acs_t[h0:h0 + 1, :], -jnp.inf))
        scores1 = cb_t * jnp.exp(jnp.where(tri, col1 - acs_t[h1:h1 + 1, :], -jnp.inf))
        xp = xs[:, i * 128:(i + 1) * 128]
        xdt = xp * both(dt[:, h0:h0 + 1], dt[:, h1:h1 + 1])
        weighted = xdt * both(jnp.exp(last0 - col0), jnp.exp(last1 - col1))
        chunk_decay = jnp.where(low_rows, jnp.exp(last0), jnp.exp(last1))
        in_decay = both(jnp.exp(col0), jnp.exp(col1))
        skip = both(dskip[:, h0:h0 + 1], dskip[:, h1:h1 + 1]) * xp
        pre.append((scores0, scores1, xdt, weighted, chunk_decay, in_decay, skip))
    prods = []
    for i in range(N_HEADS // 2):
        scores0, scores1, xdt, weighted, _, _, _ = pre[i]
        g = i // 2
        y_diag = mm(scores0, jnp.where(low, xdt, 0.0)) + mm(scores1, jnp.where(low, 0.0, xdt))
        prods.append((y_diag, mm_tn(weighted, bgs[g]), mm_nt(cgs[g], prev[i])))
    ys, news = [], []
    for i in range(N_HEADS // 2):
        y_diag, st, y_off = prods[i]
        _, _, _, _, chunk_decay, in_decay, skip = pre[i]
        news.append(chunk_decay * prev[i] + st)
        ys.append(y_diag + y_off * in_decay + skip)
    y = jnp.concatenate(ys, axis=1)
    yg = y * jax.nn.silu(z)
    half = D_SSD // 2
    outs = []
    for g in range(2):
        t = yg[:, g * half:(g + 1) * half]
        outs.append(t * lax.rsqrt(jnp.mean(t * t, axis=-1, keepdims=True) + EPS))
    return jnp.concatenate(outs, axis=1) * snw, jnp.stack(news)


def _f_out(o, yg, g1, wo, slot=None):
    cat = jnp.concatenate([o[h] for h in range(N_HEADS)] + [yg], axis=1)
    return g1 * mmw(cat, wo, slot)


def _f_modulate(x, nw, sh, sc):
    return _rms(x, nw) * (1.0 + sc) + sh


def proj_fwd(x, nw, sh, sc, w):
    s = x.shape[0]
    ts = _token_block(s)

    def body(x_ref, nw_ref, sh_ref, sc_ref, w_ref, pa_ref, pz_ref, px_ref, pl_ref):
        p = _f_proj(x_ref[...], nw_ref[...], sh_ref[...], sc_ref[...], w_ref[...])
        pa_ref[...] = p[:, :384]
        pz_ref[...] = p[:, 384:896]
        px_ref[...] = p[:, 896:1920]
        pl_ref[...] = p[:, 1920:]

    vec = _const((1, D_MODEL))
    return pl.pallas_call(
        body, name="proj_fwd", grid=(s // ts,),
        in_specs=[pl.BlockSpec((ts, D_MODEL), lambda i: (i, 0)), vec, vec, vec, _const((D_PROJ, D_MODEL))],
        out_specs=[pl.BlockSpec((ts, 384), lambda i: (i, 0)), pl.BlockSpec((ts, 512), lambda i: (i, 0)),
                   pl.BlockSpec((ts, 1024), lambda i: (i, 0)), pl.BlockSpec((ts, 128), lambda i: (i, 0))],
        out_shape=[jax.ShapeDtypeStruct((s, 384), f32), jax.ShapeDtypeStruct((s, 512), f32),
                   jax.ShapeDtypeStruct((s, 1024), f32), jax.ShapeDtypeStruct((s, 128), f32)],
    )(x, nw, sh, sc, w)


def rope_tables(pos, inv, job=None):
    s = pos.shape[0]
    ts = _token_block(s)

    def body(pos_ref, inv_ref, cos_ref, sin_ref):
        ang = pos_ref[...].astype(f32) * inv_ref[...]
        lane = lax.broadcasted_iota(jnp.int32, (1, HEAD_LANES), 1)
        half = ROPE // 2
        cos_ref[...] = jnp.where(lane < NOPE, 1.0, jnp.where(lane < NOPE + ROPE, jnp.cos(ang), 0.0))
        sn = jnp.sin(ang)
        sin_ref[...] = jnp.where((lane >= NOPE) & (lane < NOPE + half), -sn,
                                 jnp.where((lane >= NOPE + half) & (lane < NOPE + ROPE), sn, 0.0))

    steps = s // ts
    return _call_with_job(
        body, "rope_tables" if job is None else "rope_tables_comm", (steps,), job,
        in_specs=[pl.BlockSpec((ts, 1), lambda i: (i, 0)), _const((1, HEAD_LANES))],
        out_specs=[pl.BlockSpec((ts, HEAD_LANES), lambda i: (i, 0))] * 2,
        out_shape=[jax.ShapeDtypeStruct((s, HEAD_LANES), f32)] * 2, scratch_shapes=[], operands=(pos, inv),
        relay_at=(max(steps - 2, 0),))


def _qkv_param_specs():
    return [_const((1, Q_RANK)), _const((1, KV_RANK)), _const((N_HEADS, Q_RANK, HEAD_LANES)),
            _const((N_HEADS, KV_RANK, HEAD_LANES)), _const((N_HEADS, KV_RANK, V_DIM)),
            _const((1, HEAD_LANES)), _const((1, HEAD_LANES)), _const((1, HEAD_LANES))]


def qkv_fwd(pa, plast, cos_t, sin_t, params):
    s = pa.shape[0]
    ts = _token_block(s)

    def body(pa_ref, pl_ref, cos_ref, sin_ref, *rest):
        prm = [r[...] for r in rest[:8]]
        q_ref, k_ref, v_ref = rest[8:]
        q, k, v = _f_qkv(pa_ref[...], pl_ref[...], cos_ref[...], sin_ref[...], *prm)
        q_ref[...] = q.astype(bf16)
        lane = lax.broadcasted_iota(jnp.int32, (1, 1, HEAD_LANES), 2)
        k_ref[...] = jnp.where((lane == SPARE_Q) | (lane == SPARE_Q + 1), 1.0, k).astype(bf16)
        v_ref[...] = jnp.concatenate([v, jnp.ones_like(v)], axis=-1).astype(bf16)

    tok = lambda w: pl.BlockSpec((ts, w), lambda i: (i, 0))
    head = pl.BlockSpec((N_HEADS, ts, HEAD_LANES), lambda i: (0, i, 0))
    return pl.pallas_call(
        body, name="qkv_fwd", grid=(s // ts,),
        in_specs=[tok(384), tok(128), tok(128), tok(128)] + _qkv_param_specs(),
        out_specs=[head] * 3, out_shape=[jax.ShapeDtypeStruct((N_HEADS, s, HEAD_LANES), bf16)] * 3,
    )(pa, plast, cos_t, sin_t, *params)


def _scores(q, k):
    return lax.dot_general(q, k, (((1,), (1,)), ((), ())), preferred_element_type=f32)


def _tril(rows, cols, row_offset):
    row = row_offset + lax.broadcasted_iota(jnp.int32, (rows, cols), 0)
    col = lax.broadcasted_iota(jnp.int32, (rows, cols), 1)
    return row >= col


def _call_with_job(body, name, grid, job, in_specs, out_specs, out_shape, scratch_shapes, operands, relay_at=None):
    if job is None:
        res = pl.pallas_call(body, name=name, grid=grid, in_specs=in_specs, out_specs=out_specs, out_shape=out_shape,
                             scratch_shapes=scratch_shapes)(*operands)
        return res, None

    def at_step(i, n):
        if i == 0:
            want = [0] * len(grid)
        elif i == n - 1:
            want = [g - 1 for g in grid]
        else:
            want = relay_at
        return functools.reduce(jnp.logical_and, [pl.program_id(a) == s for a, s in enumerate(want)])

    carrier = _carry(job, body, len(in_specs), len(out_specs), at_step)
    res = pl.pallas_call(
        carrier, name=name, grid=grid,
        in_specs=list(in_specs) + [ANY] * len(job.operands), out_specs=list(out_specs) + [ANY] * len(job.out_shape),
        out_shape=list(out_shape) + list(job.out_shape), scratch_shapes=list(scratch_shapes) + job.scratch,
    )(*operands, *job.operands)
    return res[:len(out_specs)], res[len(out_specs):]


def attn_fwd(q, k, v, job=None):
    s = q.shape[1]
    t = min(ATTN_BLOCK_FWD, s)
    nb = s // t

    rb = min(ATTN_ROWS_FWD, t)

    hp = ATTN_HEADS_FWD

    def body(q_ref, k_ref, v_ref, o_ref, qx_ref, m_sc, acc_sc):
        qi = pl.program_id(1)
        m_sc[...] = jnp.full(m_sc.shape, NEG, f32)
        acc_sc[...] = jnp.zeros(acc_sc.shape, f32)

        def step(k0, diagonal):
            chains = [(hh, r) for hh in range(hp) for r in range(t // rb)]

            def scores(hh, r):
                nk = (r + 1) * rb if diagonal else t
                sc = _scores(q_ref[hh, pl.ds(r * rb, rb), :], k_ref[hh, pl.ds(k0, nk), :])
                return jnp.where(_tril(rb, nk, r * rb), sc, NEG) if diagonal else sc

            ahead = scores(*chains[0])
            for c, (hh, r) in enumerate(chains):
                sc = ahead
                if c + 1 < len(chains):
                    ahead = scores(*chains[c + 1])
                rows = pl.ds(r * rb, rb)
                keys = pl.ds(k0, (r + 1) * rb if diagonal else t)
                m_prev = m_sc[hh, rows, :1]
                m_new = jnp.maximum(m_prev, jnp.max(sc, axis=-1, keepdims=True))
                p = jnp.exp2(sc - m_new)
                alpha = jnp.exp2(m_prev - m_new)
                acc = alpha * acc_sc[hh, rows, :] + jnp.dot(p.astype(bf16), v_ref[hh, keys, :], preferred_element_type=f32)
                if diagonal:
                    l = acc[:, V_DIM:V_DIM + 1]
                    o_ref[hh, rows, :] = acc[:, :V_DIM] / l
                    lse = m_new + jnp.log2(l)
                    high = lse.astype(bf16)
                    low = (lse - high.astype(f32)).astype(bf16)
                    lane = lax.broadcasted_iota(jnp.int32, (1, HEAD_LANES), 1)
                    qx_ref[hh, rows, :] = jnp.where(lane == SPARE_Q, -high,
                                                    jnp.where(lane == SPARE_Q + 1, -low, q_ref[hh, rows, :]))
                else:
                    acc_sc[hh, rows, :] = acc
                    m_sc[hh, rows, :] = jnp.broadcast_to(m_new, (rb, 128))

        def below(ki, carry):
            step(pl.multiple_of(ki * t, t), False)
            return carry

        lax.fori_loop(0, qi, below, 0)
        step(pl.multiple_of(qi * t, t), True)

    return _call_with_job(
        body, "attn_fwd" if job is None else "attn_fwd_comm", (N_HEADS // hp, nb), job,
        in_specs=[pl.BlockSpec((hp, t, HEAD_LANES), lambda h, qi: (h, qi, 0)),
                  pl.BlockSpec((hp, s, HEAD_LANES), lambda h, qi: (h, 0, 0)),
                  pl.BlockSpec((hp, s, HEAD_LANES), lambda h, qi: (h, 0, 0))],
        out_specs=[pl.BlockSpec((hp, t, V_DIM), lambda h, qi: (h, qi, 0)),
                   pl.BlockSpec((hp, t, HEAD_LANES), lambda h, qi: (h, qi, 0))],
        out_shape=[jax.ShapeDtypeStruct((N_HEADS, s, V_DIM), f32), jax.ShapeDtypeStruct((N_HEADS, s, HEAD_LANES), bf16)],
        scratch_shapes=[pltpu.VMEM((hp, t, 128), f32), pltpu.VMEM((hp, t, HEAD_LANES), f32)],
        operands=(q, k, v), relay_at=(N_HEADS // hp - 1, nb - 1))


def _ssd_param_specs():
    return [_const((4, D_CONV)), _const((1, D_CONV)), _const((1, 128)), _const((1, 128)), _const((1, 128)),
            _const((1, D_SSD))]


def ssd_fwd(px, pz, plast, params):
    s = px.shape[0]
    nc = s // CHUNK

    def body(px_ref, pz_ref, pl_ref, cw_ref, cb_ref, dtb_ref, alog_ref, dskip_ref, snw_ref, yg_ref, st_ref,
             state_sc, halo_sc):
        i = pl.program_id(0)

        @pl.when(i == 0)
        def _():
            state_sc[...] = jnp.zeros(state_sc.shape, f32)
            halo_sc[...] = jnp.zeros(halo_sc.shape, f32)

        x = px_ref[...]
        prev = state_sc[...]
        st_ref[...] = prev
        xext = jnp.concatenate([halo_sc[...], x], axis=0)
        yg, new = _f_ssd(xext, pz_ref[...], pl_ref[...], prev, cw_ref[...], cb_ref[...], dtb_ref[...],
                         alog_ref[...], dskip_ref[...], snw_ref[...])
        yg_ref[...] = yg
        state_sc[...] = new
        halo_sc[...] = x[CHUNK - HALO:]

    tok = lambda w: pl.BlockSpec((CHUNK, w), lambda i: (i, 0))
    return pl.pallas_call(
        body, name="ssd_fwd", grid=(nc,),
        in_specs=[tok(D_CONV), tok(D_SSD), tok(128)] + _ssd_param_specs(),
        out_specs=[tok(D_SSD), pl.BlockSpec((None, N_HEADS // 2, 2 * SSD_HEAD_DIM, SSD_STATE), lambda i: (i, 0, 0, 0))],
        out_shape=[jax.ShapeDtypeStruct((s, D_SSD), f32),
                   jax.ShapeDtypeStruct((nc, N_HEADS // 2, 2 * SSD_HEAD_DIM, SSD_STATE), f32)],
        scratch_shapes=[pltpu.VMEM((N_HEADS // 2, 2 * SSD_HEAD_DIM, SSD_STATE), f32), pltpu.VMEM((HALO, D_CONV), f32)],
    )(px, pz, plast, *params)


def out_fwd(x, o, yg, g1, wo):
    s = x.shape[0]
    ts = _token_block(s)

    def body(x_ref, o_ref, yg_ref, g1_ref, wo_ref, out_ref):
        out_ref[...] = x_ref[...] + _f_out(o_ref[...], yg_ref[...], g1_ref[...], wo_ref[...])

    return pl.pallas_call(
        body, name="out_fwd", grid=(s // ts,),
        in_specs=[pl.BlockSpec((ts, D_MODEL), lambda i: (i, 0)), pl.BlockSpec((N_HEADS, ts, V_DIM), lambda i: (0, i, 0)),
                  pl.BlockSpec((ts, D_SSD), lambda i: (i, 0)), _const((1, D_MODEL)), _const((D_MODEL, D_MODEL))],
        out_specs=pl.BlockSpec((ts, D_MODEL), lambda i: (i, 0)),
        out_shape=jax.ShapeDtypeStruct((s, D_MODEL), f32),
    )(x, o, yg, g1, wo)


def mlp_fwd(x, nw, sh, sc, g2, wgu, wd, target=None):
    s = x.shape[0]
    ts = min(MLP_FWD_ROWS, s)
    nj = N_DEV // 2

    def body(x_ref, nw_ref, sh_ref, sc_ref, g2_ref, wg_ref, wu_ref, wd_ref, *rest):
        if target is None:
            out_ref, mix_ref, h_ref, gate_ref, up_ref = rest
        else:
            t_ref, out_ref, mix_ref, h_ref, gate_ref, up_ref, loss_ref = rest
        j = pl.program_id(1)
        first_block = pl.program_id(0) == 0

        @pl.when(j == 0)
        def _():
            h_ref[...] = _f_modulate(x_ref[...], nw_ref[...], sh_ref[...], sc_ref[...]).astype(bf16)
            mix_ref[...] = jnp.zeros(mix_ref.shape, f32)

        nr = max(ts // 512, 1)
        half = ts // nr
        wg, wu, wd = wg_ref[...], wu_ref[...], wd_ref[...]
        products = lambda r: (mmw_t(h_ref[pl.ds(r * half, half), :], wg), mmw_t(h_ref[pl.ds(r * half, half), :], wu))
        ahead = products(0)
        for r in range(nr):
            gate, up = ahead
            if r + 1 < nr:
                ahead = products(r + 1)
            rows = pl.ds(r * half, half)
            gate_ref[rows, :] = gate.astype(bf16)
            up_ref[rows, :] = up.astype(bf16)
            mix_ref[rows, :] += mmw(jax.nn.silu(gate) * up, wd)

        @pl.when(j == nj - 1)
        def _():
            y = x_ref[...] + g2_ref[...] * mix_ref[...]
            if target is None:
                out_ref[...] = y
            else:
                d = y - t_ref[...]
                out_ref[...] = d * (1.0 / D_MODEL)
                part = 0.5 * jnp.sum(jnp.sum(d * d, axis=-1, keepdims=True) * (1.0 / D_MODEL), axis=0, keepdims=True)
                _accumulate(first_block, [loss_ref], [jnp.broadcast_to(part, (8, 128))])

    vec = _const((1, D_MODEL))
    tok = pl.BlockSpec((ts, D_MODEL), lambda i, j: (i, 0))
    wide = pl.BlockSpec((None, ts, FF_SHARD), lambda i, j: (j, i, 0))
    last = target is not None
    return pl.pallas_call(
        body, name="mlp_fwd_loss" if last else "mlp_fwd", grid=(s // ts, nj),
        in_specs=[tok, vec, vec, vec, vec,
                  pl.BlockSpec((None, FF_SHARD, D_MODEL), lambda i, j: (j, 0, 0)),
                  pl.BlockSpec((None, FF_SHARD, D_MODEL), lambda i, j: (j + nj, 0, 0)),
                  pl.BlockSpec((None, FF_SHARD, D_MODEL), lambda i, j: (j, 0, 0))] + [tok] * last,
        out_specs=[tok] * 3 + [wide] * 2 + [_const((8, 128))] * last,
        out_shape=[jax.ShapeDtypeStruct((s, D_MODEL), f32), jax.ShapeDtypeStruct((s, D_MODEL), f32),
                   jax.ShapeDtypeStruct((s, D_MODEL), bf16)] + [jax.ShapeDtypeStruct((nj, s, FF_SHARD), bf16)] * 2
                  + [jax.ShapeDtypeStruct((8, 128), f32)] * last,
    )(x, nw, sh, sc, g2, wgu, wgu, wd, *([target] if last else []))


def mlp_bwd(h, dy, gate, up, g2, wgu, wd, job=None):
    s = h.shape[0]
    ts = min(MLP_BWD_ROWS, s)
    nj = N_DEV // 2
    ni = s // ts

    rows_per = min(MLP_BWD_CHUNK, ts)

    def body(h_ref, dy_ref, gate_ref, up_ref, g2_ref, wg_ref, wu_ref, wd_ref, dh_ref, dwg_ref, dwu_ref, dwd_ref,
             ag_sc, au_sc, ad_sc, act_sc, dgate_sc, dup_sc, dmix_sc):
        i = pl.program_id(1)
        wg, wu, wd = wg_ref[...], wu_ref[...], wd_ref[...]
        g2 = g2_ref[...]
        for r in range(ts // rows_per):
            rows = pl.ds(r * rows_per, rows_per)
            act, vjp = jax.vjp(lambda g, u: jax.nn.silu(g) * u, gate_ref[rows, :].astype(f32), up_ref[rows, :].astype(f32))
            dmix = (dy_ref[rows, :] * g2).astype(bf16)
            dgate, dup = vjp(_dot(dmix, wd, 1, 1))
            dgate, dup = dgate.astype(bf16), dup.astype(bf16)
            dh_ref[rows, :] = (_dot(dgate, wg, 1, 0) + _dot(dup, wu, 1, 0)).astype(bf16)
            act_sc[rows, :] = act.astype(bf16)
            dgate_sc[rows, :] = dgate
            dup_sc[rows, :] = dup
            dmix_sc[rows, :] = dmix
        h = h_ref[...]
        grads = [_dot(dgate_sc[...], h, 0, 0), _dot(dup_sc[...], h, 0, 0), _dot(act_sc[...], dmix_sc[...], 0, 0)]
        _accumulate_then_cast(i == 0, i == ni - 1, [ag_sc, au_sc, ad_sc], [dwg_ref, dwu_ref, dwd_ref], grads)

    once = pl.Buffered(1)
    wspec = lambda off: pl.BlockSpec((None, FF_SHARD, D_MODEL), lambda j, i: (j + off, 0, 0), pipeline_mode=once)
    dspec = pl.BlockSpec((None, FF_SHARD, D_MODEL), lambda j, i: (j, 0, 0), pipeline_mode=once)
    wide = pl.BlockSpec((None, ts, FF_SHARD), lambda j, i: (j, i, 0))
    return _call_with_job(
        body, "mlp_bwd" if job is None else "mlp_bwd_comm", (nj, ni), job,
        in_specs=[pl.BlockSpec((ts, D_MODEL), lambda j, i: (i, 0)), pl.BlockSpec((ts, D_MODEL), lambda j, i: (i, 0)),
                  wide, wide, _const((1, D_MODEL)), wspec(0), wspec(nj), dspec],
        out_specs=[pl.BlockSpec((None, ts, D_MODEL), lambda j, i: (j, i, 0)), wspec(0), wspec(0), dspec],
        out_shape=[jax.ShapeDtypeStruct((nj, s, D_MODEL), bf16),
                   jax.ShapeDtypeStruct((nj, FF_SHARD, D_MODEL), bf16), jax.ShapeDtypeStruct((nj, FF_SHARD, D_MODEL), bf16),
                   jax.ShapeDtypeStruct((nj, FF_SHARD, D_MODEL), bf16)],
        scratch_shapes=[pltpu.VMEM((FF_SHARD, D_MODEL), f32), pltpu.VMEM((FF_SHARD, D_MODEL), f32),
                        pltpu.VMEM((FF_SHARD, D_MODEL), f32), pltpu.VMEM((ts, FF_SHARD), bf16),
                        pltpu.VMEM((ts, FF_SHARD), bf16), pltpu.VMEM((ts, FF_SHARD), bf16), pltpu.VMEM((ts, D_MODEL), bf16)],
        operands=(h, dy, gate, up, g2, wgu, wgu, wd))


def out_bwd(dy, dhparts, x, nw, sh, sc, mix, o, yg, g1, wo):
    s = dy.shape[0]
    ts = _token_block(s)
    nj = dhparts.shape[0]

    ni = s // ts

    def body(dy_ref, dp_ref, x_ref, nw_ref, sh_ref, sc_ref, mix_ref, o_ref, yg_ref, g1_ref, wo_ref,
             dx_ref, dnw_ref, dsh_ref, dsc_ref, do_ref, dyg_ref, dg1_ref, dg2_ref, dwo_ref, acc_sc):
        i = pl.program_id(0)
        g = dy_ref[...]
        _accumulate(i == 0, [dg2_ref], [jnp.sum(g * mix_ref[...], axis=0, keepdims=True)])
        dh = dp_ref[0].astype(f32)
        for j in range(1, nj):
            dh = dh + dp_ref[j].astype(f32)
        _, vjp_mod = jax.vjp(_f_modulate, x_ref[...], nw_ref[...], sh_ref[...], sc_ref[...])
        dx_mod, dnw, dsh, dsc = vjp_mod(dh)
        _accumulate(i == 0, [dnw_ref, dsh_ref, dsc_ref], [dnw, dsh, dsc])
        g = g + dx_mod
        dx_ref[...] = g
        o = o_ref[...]
        wo = wo_ref[...]
        _, vjp = jax.vjp(lambda o_, yg_, g1_, slot: _f_out(o_, yg_, g1_, wo, slot), o, yg_ref[...], g1_ref[...],
                         jnp.zeros(wo.shape, f32))
        do, dyg, dg1, dwo = vjp(g)
        delta = jnp.sum(do * o, axis=-1, keepdims=True)
        high = delta.astype(bf16)
        low = (delta - high.astype(f32)).astype(bf16)
        lane = lax.broadcasted_iota(jnp.int32, (1, 1, HEAD_LANES), 2)
        wide = jnp.concatenate([do.astype(bf16), jnp.zeros(do.shape, bf16)], axis=-1)
        do_ref[...] = jnp.where(lane == SPARE_V, -high, jnp.where(lane == SPARE_V + 1, -low, wide))
        dyg_ref[...] = dyg
        _accumulate(i == 0, [dg1_ref], [dg1])
        _accumulate_then_cast(i == 0, i == ni - 1, [acc_sc], [dwo_ref], [dwo])

    head = pl.BlockSpec((N_HEADS, ts, V_DIM), lambda i: (0, i, 0))
    tok = pl.BlockSpec((ts, D_MODEL), lambda i: (i, 0))
    vec = _const((1, D_MODEL))
    vshape = jax.ShapeDtypeStruct((1, D_MODEL), f32)
    return pl.pallas_call(
        body, name="out_bwd", grid=(ni,), scratch_shapes=[pltpu.VMEM((D_MODEL, D_MODEL), f32)],
        in_specs=[tok, pl.BlockSpec((nj, ts, D_MODEL), lambda i: (0, i, 0)), tok, vec, vec, vec, tok,
                  head, pl.BlockSpec((ts, D_SSD), lambda i: (i, 0)), vec, _const((D_MODEL, D_MODEL))],
        out_specs=[tok, vec, vec, vec, pl.BlockSpec((N_HEADS, ts, HEAD_LANES), lambda i: (0, i, 0)),
                   pl.BlockSpec((ts, D_SSD), lambda i: (i, 0)), vec, vec, _const((D_MODEL, D_MODEL))],
        out_shape=[jax.ShapeDtypeStruct((s, D_MODEL), f32), vshape, vshape, vshape,
                   jax.ShapeDtypeStruct((N_HEADS, s, HEAD_LANES), bf16), jax.ShapeDtypeStruct((s, D_SSD), f32),
                   vshape, vshape, jax.ShapeDtypeStruct((D_MODEL, D_MODEL), bf16)],
    )(dy, dhparts, x, nw, sh, sc, mix, o, yg, g1, wo)


def attn_bwd(qx, k, v, do, job=None):
    s = qx.shape[1]
    t = _token_block(s)
    nb = s // t

    hp = ATTN_HEADS_BWD

    def body(q_ref, k_ref, v_ref, do_ref, dq_ref, dk_ref, dv_ref, dv_sc):
        ki = pl.program_id(1)

        @pl.when(ki == 0)
        def _():
            dq_ref[...] = jnp.zeros(dq_ref.shape, f32)

        dk_ref[...] = jnp.zeros(dk_ref.shape, f32)
        dv_sc[...] = jnp.zeros(dv_sc.shape, f32)

        def step(q0, diagonal):
            half = t // 2
            subs = [(0, half, half), (half, half, t)] if diagonal and half % 128 == 0 else [(0, t, t)]
            chains = [(hh, sub) for hh in range(hp) for sub in subs]

            def products(hh, sub):
                r0, nr, nk = sub
                rows = pl.ds(q0 + r0, nr)
                sc = _scores(q_ref[hh, rows, :], k_ref[hh, :nk, :])
                dps = _scores(do_ref[hh, rows, :], v_ref[hh, :nk, :])
                return (jnp.where(_tril(nr, nk, r0), sc, NEG) if diagonal else sc), dps

            ahead = products(*chains[0])
            for c, (hh, (r0, nr, nk)) in enumerate(chains):
                sc, dps = ahead
                if c + 1 < len(chains):
                    ahead = products(*chains[c + 1])
                rows = pl.ds(q0 + r0, nr)
                p = jnp.exp2(sc)
                ds = (p * dps).astype(bf16)
                dv_sc[hh, :nk, :] += lax.dot_general(p.astype(bf16), do_ref[hh, rows, :], (((0,), (0,)), ((), ())),
                                                     preferred_element_type=f32)
                dk_ref[hh, :nk, :] += lax.dot_general(ds, q_ref[hh, rows, :], (((0,), (0,)), ((), ())),
                                                      preferred_element_type=f32)
                dq_ref[hh, rows, :] += jnp.dot(ds, k_ref[hh, :nk, :], preferred_element_type=f32)

        step(pl.multiple_of(ki * t, t), True)

        def above(qi, carry):
            step(pl.multiple_of(qi * t, t), False)
            return carry

        lax.fori_loop(ki + 1, nb, above, 0)
        real = lax.broadcasted_iota(jnp.int32, (1, 1, HEAD_LANES), 2) < SPARE_Q
        dk_ref[...] = jnp.where(real, dk_ref[...] * LN2, 0.0)
        dv_ref[...] = dv_sc[:, :, :V_DIM]

        @pl.when(ki == nb - 1)
        def _():
            dq_ref[...] = jnp.where(real, dq_ref[...] * LN2, 0.0)

    qspec = pl.BlockSpec((hp, s, HEAD_LANES), lambda h, ki: (h, 0, 0))
    kspec = lambda w: pl.BlockSpec((hp, t, w), lambda h, ki: (h, ki, 0))
    return _call_with_job(
        body, "attn_bwd" if job is None else "attn_bwd_comm", (N_HEADS // hp, nb), job,
        in_specs=[qspec, kspec(HEAD_LANES), kspec(HEAD_LANES), qspec],
        out_specs=[qspec, kspec(HEAD_LANES), kspec(V_DIM)],
        out_shape=[jax.ShapeDtypeStruct((N_HEADS, s, HEAD_LANES), f32), jax.ShapeDtypeStruct((N_HEADS, s, HEAD_LANES), f32),
                   jax.ShapeDtypeStruct((N_HEADS, s, V_DIM), f32)],
        scratch_shapes=[pltpu.VMEM((hp, t, HEAD_LANES), f32)], operands=(qx, k, v, do))


def ssd_bwd(px, pz, plast, states, dyg, params):
    s = px.shape[0]
    nc = s // CHUNK
    per = CHUNK // HALO

    def body(px_ref, halo_ref, pz_ref, pl_ref, st_ref, dyg_ref, cw_ref, cb_ref, dtb_ref, alog_ref, dskip_ref, snw_ref,
             dpx_ref, dpz_ref, dpl_ref, dcw_ref, dcb_ref, ddtb_ref, dalog_ref, ddskip_ref, dsnw_ref, dstate_sc, dhalo_sc):
        t = pl.program_id(0)
        chunk = nc - 1 - t

        @pl.when(t == 0)
        def _():
            dstate_sc[...] = jnp.zeros(dstate_sc.shape, f32)
            dhalo_sc[...] = jnp.zeros(dhalo_sc.shape, f32)

        halo = jnp.where(chunk > 0, halo_ref[...], 0.0)
        xext = jnp.concatenate([halo, px_ref[...]], axis=0)
        _, vjp = jax.vjp(_f_ssd, xext, pz_ref[...], pl_ref[...], st_ref[...], cw_ref[...], cb_ref[...], dtb_ref[...],
                         alog_ref[...], dskip_ref[...], snw_ref[...])
        dxext, dz, dpl, dprev, dcw, dcb, ddtb, dalog, ddskip, dsnw = vjp((dyg_ref[...], dstate_sc[...]))
        dpx_ref[...] = dxext[HALO:]
        dpx_ref[CHUNK - HALO:, :] += dhalo_sc[...]
        dhalo_sc[...] = dxext[:HALO]
        dstate_sc[...] = dprev
        dpz_ref[...] = dz
        dpl_ref[...] = dpl
        _accumulate(t == 0, [dcw_ref, dcb_ref, ddtb_ref, dalog_ref, ddskip_ref, dsnw_ref],
                    [dcw, dcb, ddtb, dalog, ddskip, dsnw])

    rev = lambda w: pl.BlockSpec((CHUNK, w), lambda t: (nc - 1 - t, 0))
    pshapes = [jax.ShapeDtypeStruct((4, D_CONV), f32), jax.ShapeDtypeStruct((1, D_CONV), f32),
               jax.ShapeDtypeStruct((1, 128), f32), jax.ShapeDtypeStruct((1, 128), f32),
               jax.ShapeDtypeStruct((1, 128), f32), jax.ShapeDtypeStruct((1, D_SSD), f32)]
    return pl.pallas_call(
        body, name="ssd_bwd", grid=(nc,),
        in_specs=[rev(D_CONV),
                  pl.BlockSpec((HALO, D_CONV), lambda t: (jnp.maximum((nc - 1 - t) * per - 1, 0), 0)),
                  rev(D_SSD), rev(128),
                  pl.BlockSpec((None, N_HEADS // 2, 2 * SSD_HEAD_DIM, SSD_STATE), lambda t: (nc - 1 - t, 0, 0, 0)),
                  rev(D_SSD)] + _ssd_param_specs(),
        out_specs=[rev(D_CONV), rev(D_SSD), rev(128)] + _ssd_param_specs(),
        out_shape=[jax.ShapeDtypeStruct((s, D_CONV), f32), jax.ShapeDtypeStruct((s, D_SSD), f32),
                   jax.ShapeDtypeStruct((s, 128), f32)] + pshapes,
        scratch_shapes=[pltpu.VMEM((N_HEADS // 2, 2 * SSD_HEAD_DIM, SSD_STATE), f32), pltpu.VMEM((HALO, D_CONV), f32)],
    )(px, px, pz, plast, states, dyg, *params)


def qkv_bwd(pa, plast, cos_t, sin_t, params, dq, dk, dv):
    s = pa.shape[0]
    ts = _token_block(s)

    def body(pa_ref, pl_ref, cos_ref, sin_ref, *rest):
        qaw, kvaw, wq, wk, wv, qnw, knw, kpw = [r[...] for r in rest[:8]]
        dq_ref, dk_ref, dv_ref = rest[8:11]
        dpa_ref, dpl_ref = rest[11:13]
        dprm_refs = list(rest[13:])
        cos_t, sin_t = cos_ref[...], sin_ref[...]

        def stage(pa_, pl_, qaw_, kvaw_, sq, sk, sv, qnw_, knw_, kpw_):
            return _f_qkv(pa_, pl_, cos_t, sin_t, qaw_, kvaw_, wq, wk, wv, qnw_, knw_, kpw_, (sq, sk, sv))

        _, vjp = jax.vjp(stage, pa_ref[...], pl_ref[...], qaw, kvaw, jnp.zeros(wq.shape, f32), jnp.zeros(wk.shape, f32),
                         jnp.zeros(wv.shape, f32), qnw, knw, kpw)
        grads = vjp((dq_ref[...], dk_ref[...], dv_ref[...]))
        dpa_ref[...] = grads[0]
        dpl_ref[...] = grads[1]
        _accumulate(pl.program_id(0) == 0, dprm_refs, list(grads[2:]))

    tok = lambda w: pl.BlockSpec((ts, w), lambda i: (i, 0))
    head = lambda w: pl.BlockSpec((N_HEADS, ts, w), lambda i: (0, i, 0))
    pshapes = [jax.ShapeDtypeStruct((1, Q_RANK), f32), jax.ShapeDtypeStruct((1, KV_RANK), f32),
               jax.ShapeDtypeStruct((N_HEADS, Q_RANK, HEAD_LANES), f32), jax.ShapeDtypeStruct((N_HEADS, KV_RANK, HEAD_LANES), f32),
               jax.ShapeDtypeStruct((N_HEADS, KV_RANK, V_DIM), f32), jax.ShapeDtypeStruct((1, HEAD_LANES), f32),
               jax.ShapeDtypeStruct((1, HEAD_LANES), f32), jax.ShapeDtypeStruct((1, HEAD_LANES), f32)]
    return pl.pallas_call(
        body, name="qkv_bwd", grid=(s // ts,),
        in_specs=[tok(384), tok(128), tok(128), tok(128)] + _qkv_param_specs()
                 + [head(HEAD_LANES), head(HEAD_LANES), head(V_DIM)],
        out_specs=[tok(384), tok(128)] + _qkv_param_specs(),
        out_shape=[jax.ShapeDtypeStruct((s, 384), f32), jax.ShapeDtypeStruct((s, 128), f32)] + pshapes,
    )(pa, plast, cos_t, sin_t, *params, dq, dk, dv)


def proj_bwd(x, nw, sh, sc, w, dpa, dpz, dpx, dpl_k, dpl_dt, dres):
    s = x.shape[0]
    ts = _token_block(s)

    ni = s // ts

    def body(x_ref, nw_ref, sh_ref, sc_ref, w_ref, dpa_ref, dpz_ref, dpx_ref, dplk_ref, dpld_ref, dres_ref,
             dx_ref, dnw_ref, dsh_ref, dsc_ref, dw_ref, acc_sc):
        i = pl.program_id(0)
        g = jnp.concatenate([dpa_ref[...], dpz_ref[...], dpx_ref[...], dplk_ref[...] + dpld_ref[...]], axis=1)
        w = w_ref[...]
        _, vjp = jax.vjp(lambda x_, nw_, sh_, sc_, slot: _f_proj(x_, nw_, sh_, sc_, w, slot), x_ref[...], nw_ref[...],
                         sh_ref[...], sc_ref[...], jnp.zeros(w.shape, f32))
        dx, dnw, dsh, dsc, dw = vjp(g)
        dx_ref[...] = dx + dres_ref[...]
        _accumulate(i == 0, [dnw_ref, dsh_ref, dsc_ref], [dnw, dsh, dsc])
        _accumulate_then_cast(i == 0, i == ni - 1, [acc_sc], [dw_ref], [dw])

    vec = _const((1, D_MODEL))
    vshape = jax.ShapeDtypeStruct((1, D_MODEL), f32)
    tok = lambda w_: pl.BlockSpec((ts, w_), lambda i: (i, 0))
    return pl.pallas_call(
        body, name="proj_bwd", grid=(ni,), scratch_shapes=[pltpu.VMEM((D_PROJ, D_MODEL), f32)],
        in_specs=[tok(D_MODEL), vec, vec, vec, _const((D_PROJ, D_MODEL)), tok(384), tok(512), tok(1024), tok(128), tok(128),
                  tok(D_MODEL)],
        out_specs=[tok(D_MODEL), vec, vec, vec, _const((D_PROJ, D_MODEL))],
        out_shape=[jax.ShapeDtypeStruct((s, D_MODEL), f32), vshape, vshape, vshape,
                   jax.ShapeDtypeStruct((D_PROJ, D_MODEL), bf16)],
    )(x, nw, sh, sc, w, dpa, dpz, dpx, dpl_k, dpl_dt, dres)


def ada_fwd(c_all, w_ada, b_cols):
    def body(c_ref, w_ref, b_ref, out_ref):
        act = jax.nn.silu(c_ref[...])
        for l in range(2):
            out_ref[l] = jnp.dot(act, w_ref[l], precision=lax.Precision.HIGHEST, preferred_element_type=f32) + b_ref[l]

    return pl.pallas_call(body, name="ada_fwd", out_shape=jax.ShapeDtypeStruct((2, N_DEV, 768), f32))(c_all, w_ada, b_cols)


def ada_bwd(c_all, dmod_cols):
    def body(c_ref, d_ref, out_ref):
        out_ref[0] = lax.dot_general(jax.nn.silu(c_ref[...]), d_ref[0], (((0,), (0,)), ((), ())),
                                     precision=lax.Precision.HIGHEST, preferred_element_type=f32)

    return pl.pallas_call(
        body, name="ada_bwd", grid=(2,),
        in_specs=[_const((N_DEV, D_MODEL)), pl.BlockSpec((1, N_DEV, 768), lambda l: (l, 0, 0))],
        out_specs=pl.BlockSpec((1, D_MODEL, 768), lambda l: (l, 0, 0)),
        out_shape=jax.ShapeDtypeStruct((2, D_MODEL, 768), f32),
    )(c_all, dmod_cols)


def _adamw(w, g, m, v):
    m = ADAM_B1 * m + (1.0 - ADAM_B1) * g
    v = ADAM_B2 * v + (1.0 - ADAM_B2) * (g * g)
    m_hat = m / (1.0 - ADAM_B1 ** ADAM_STEP)
    v_hat = v / (1.0 - ADAM_B2 ** ADAM_STEP)
    delta = -ADAM_LR * (m_hat / (jnp.sqrt(v_hat) + ADAM_EPS) + ADAM_WD * w)
    return delta, m, v


def adamw(parts, w, m, v, layer, prev, name):
    n, r, c = parts.shape
    nl = w.shape[0]
    per_elem = 2 * (n * parts.dtype.itemsize + 7 * 4)
    lanes = -(-c // 128) * 128
    tr, tc = r, c
    if per_elem * r * lanes > ADAMW_BLOCK_BYTES:
        fits = [t for t in range(r // 2, 15, -1) if r % t == 0 and t % 16 == 0 and per_elem * t * lanes <= ADAMW_BLOCK_BYTES]
        if fits:
            tr = fits[0]
        else:
            tc = next(t for t in (512, 256, 128) if c % t == 0)

    def body(p_ref, w_ref, m_ref, v_ref, *rest):
        g_ref, d_ref, nm_ref, nv_ref = rest[-4:]
        g = p_ref[0].astype(f32)
        for k in range(1, n):
            g = g + p_ref[k].astype(f32)
        delta, nm, nv = _adamw(w_ref[...], g, m_ref[...], v_ref[...])
        g_ref[...] = g
        d_ref[...] = delta
        nm_ref[...] = nm
        nv_ref[...] = nv

    blk = pl.BlockSpec((None, tr, tc), lambda i, j: (layer, i, j))
    shp = jax.ShapeDtypeStruct((nl, r, c), f32)
    kept = [] if prev is None else list(prev)
    return pl.pallas_call(
        body, name=name, grid=(r // tr, c // tc),
        in_specs=[pl.BlockSpec((n, tr, tc), lambda i, j: (0, i, j)), blk, blk, blk] + [ANY] * len(kept),
        out_specs=[blk] * 4, out_shape=[shp] * 4,
        input_output_aliases={4 + j: j for j in range(len(kept))},
    )(parts, w, m, v, *kept)


def adamw_two_layers(parts, w, m, v, name, job=None):
    n, r, c = parts[0].shape
    per_elem = 2 * (2 * n * parts[0].dtype.itemsize + 7 * 4)
    lanes = -(-c // 128) * 128
    tr = next(t for t in range(r, 15, -1) if r % t == 0 and t % 16 == 0 and per_elem * t * lanes <= ADAMW_BLOCK_BYTES)
    nblk = r // tr

    def body(p0_ref, p1_ref, w_ref, m_ref, v_ref, g_ref, d_ref, nm_ref, nv_ref):
        layer = pl.program_id(0)

        def update(p_ref):
            g = p_ref[0].astype(f32)
            for k in range(1, n):
                g = g + p_ref[k].astype(f32)
            delta, nm, nv = _adamw(w_ref[...], g, m_ref[...], v_ref[...])
            g_ref[...] = g
            d_ref[...] = delta
            nm_ref[...] = nm
            nv_ref[...] = nv

        @pl.when(layer == 0)
        def _():
            update(p0_ref)

        @pl.when(layer == 1)
        def _():
            update(p1_ref)

    blk = pl.BlockSpec((None, tr, c), lambda l, i: (l, i, 0))
    shp = jax.ShapeDtypeStruct((2, r, c), f32)
    return _call_with_job(
        body, name, (2, nblk), job,
        in_specs=[pl.BlockSpec((n, tr, c), lambda l, i: (0, i * (1 - l), 0)),
                  pl.BlockSpec((n, tr, c), lambda l, i: (0, i * l, 0)), blk, blk, blk],
        out_specs=[blk] * 4, out_shape=[shp] * 4, scratch_shapes=[], operands=(parts[0], parts[1], w, m, v),
        relay_at=(1, 0))


def adamw_layers_inside(parts, w, m, v, name):
    n, r, c = parts[0].shape
    nl = w.shape[1]
    tc = next(t for t in (256, 128) if c % t == 0)

    def body(*refs):
        p_refs = refs[:nl]
        w_ref, m_ref, v_ref, g_ref, d_ref, nm_ref, nv_ref = refs[nl:]
        for l in range(nl):
            g = p_refs[l][0].astype(f32)
            for k in range(1, n):
                g = g + p_refs[l][k].astype(f32)
            delta, nm, nv = _adamw(w_ref[:, l, :], g, m_ref[:, l, :], v_ref[:, l, :])
            g_ref[:, l, :] = g
            d_ref[:, l, :] = delta
            nm_ref[:, l, :] = nm
            nv_ref[:, l, :] = nv

    blk = pl.BlockSpec((r, nl, tc), lambda j: (0, 0, j))
    shp = jax.ShapeDtypeStruct((r, nl, c), f32)
    return pl.pallas_call(
        body, name=name, grid=(c // tc,),
        in_specs=[pl.BlockSpec((n, r, tc), lambda j: (0, 0, j))] * nl + [blk] * 3,
        out_specs=[blk] * 4, out_shape=[shp] * 4,
    )(*parts, w, m, v)


def _my_index():
    return 4 * lax.axis_index("x") + 2 * lax.axis_index("y") + lax.axis_index("c")


def _coords(idx):
    return (idx // 4, (idx // 2) % 2, idx % 2)


class CommJob:
    def __init__(self, operands, out_shape, phases, scratch):
        self.operands, self.out_shape, self.phases, self.scratch = operands, out_shape, phases, scratch


def _wait(out, n_blocks, send_sem, recv_sem, send=True, recv=True):
    span = out.at[pl.ds(0, n_blocks)]
    desc = pltpu.make_async_remote_copy(src_ref=span, dst_ref=span, send_sem=send_sem, recv_sem=recv_sem,
                                        device_id=_coords(_my_index()), device_id_type=MESH)
    if recv:
        desc.wait_recv()
    if send:
        desc.wait_send()


def gather_job(shards):
    n = len(shards)

    def places():
        x, y, c = lax.axis_index("x"), lax.axis_index("y"), lax.axis_index("c")
        return (x, y, c), (x, y, 1 - c), [(1 - x, y), (x, 1 - y), (1 - x, 1 - y)]

    def index(p):
        return 4 * p[0] + 2 * p[1] + p[2]

    def start(ins, outs, sems):
        far_send, far_recv, near_send, near_recv, local = sems
        me, sibling, chips = places()
        for k in range(n):
            pltpu.make_async_copy(ins[k], outs[k].at[index(me)], local.at[k]).start()
            for chip in chips:
                pltpu.make_async_remote_copy(src_ref=ins[k], dst_ref=outs[k].at[index(me)], send_sem=far_send.at[k],
                                             recv_sem=far_recv.at[k], device_id=(*chip, me[2]), device_id_type=MESH).start()
            pltpu.make_async_remote_copy(src_ref=ins[k], dst_ref=outs[k].at[index(me)], send_sem=near_send.at[k],
                                         recv_sem=near_recv.at[k], device_id=sibling, device_id_type=MESH).start()

    def relay(ins, outs, sems):
        far_send, far_recv, near_send, near_recv, local = sems
        me, sibling, chips = places()
        for k in range(n):
            _wait(outs[k], 3, far_send.at[k], far_recv.at[k], send=False)
            for chip in chips:
                block = outs[k].at[index((*chip, me[2]))]
                pltpu.make_async_remote_copy(src_ref=block, dst_ref=block, send_sem=near_send.at[k],
                                             recv_sem=near_recv.at[k], device_id=sibling, device_id_type=MESH).start()

    def finish(ins, outs, sems):
        far_send, far_recv, near_send, near_recv, local = sems
        for k in range(n):
            _wait(outs[k], 4, near_send.at[k], near_recv.at[k])
            _wait(outs[k], 3, far_send.at[k], far_recv.at[k], recv=False)
            pltpu.make_async_copy(ins[k], outs[k].at[0], local.at[k]).wait()

    shapes = [jax.ShapeDtypeStruct((N_DEV,) + tuple(a.shape), a.dtype) for a in shards]
    return CommJob(list(shards), shapes, [start, relay, finish], [pltpu.SemaphoreType.DMA((n,))] * 5)


def scatter_job(tensors):
    n = len(tensors)
    flat, where = [], {}
    for k, pieces in enumerate(tensors):
        d = 0
        for piece in pieces:
            for b in range(piece.shape[0]):
                where[k, d] = (len(flat), b)
                d += 1
            flat.append(piece)
        assert d == N_DEV

    def start(ins, outs, sems):
        send_sems, recv_sems, local_sems = sems
        me = _my_index()

        def block(k, d):
            i, b = where[k, d]
            return ins[i].at[b]

        for d in range(N_DEV):
            @pl.when(d != me)
            def _():
                for k in range(n):
                    pltpu.make_async_remote_copy(src_ref=block(k, d), dst_ref=outs[k].at[me], send_sem=send_sems.at[k],
                                                 recv_sem=recv_sems.at[k], device_id=(d // 4, (d // 2) % 2, d % 2),
                                                 device_id_type=MESH).start()

            @pl.when(d == me)
            def _():
                for k in range(n):
                    pltpu.make_async_copy(block(k, d), outs[k].at[d], local_sems.at[k]).start()

    def finish(ins, outs, sems):
        send_sems, recv_sems, local_sems = sems
        for k in range(n):
            _wait(outs[k], N_DEV - 1, send_sems.at[k], recv_sems.at[k])
            i, b = where[k, 0]
            pltpu.make_async_copy(ins[i].at[b], outs[k].at[0], local_sems.at[k]).wait()

    shapes = [jax.ShapeDtypeStruct((N_DEV,) + tuple(p[0].shape[1:]), p[0].dtype) for p in tensors]
    return CommJob(flat, shapes, [start, finish], [pltpu.SemaphoreType.DMA((n,))] * 3)


def merge_jobs(a, b):
    def on(job, off):
        oi, oo, os_ = off
        ni, no, ns = len(job.operands), len(job.out_shape), len(job.scratch)
        return lambda phase: (lambda ins, outs, sems: phase(ins[oi:oi + ni], outs[oo:oo + no], sems[os_:os_ + ns]))

    wrap_a = on(a, (0, 0, 0))
    wrap_b = on(b, (len(a.operands), len(a.out_shape), len(a.scratch)))
    pa, pb = [wrap_a(p) for p in a.phases], [wrap_b(p) for p in b.phases]

    def together(*phases):
        def run(ins, outs, sems):
            for p in phases:
                p(ins, outs, sems)
        return run

    middle = pa[1:-1] + pb[1:-1]
    phases = [together(pa[0], pb[0])] + middle + [together(pa[-1], pb[-1])]
    return CommJob(a.operands + b.operands, a.out_shape + b.out_shape, phases, a.scratch + b.scratch)


def comm_call(job, name):
    ni, no = len(job.operands), len(job.out_shape)

    def body(*refs):
        ins, outs, sems = refs[:ni], refs[ni:ni + no], refs[ni + no:]
        for phase in job.phases:
            phase(ins, outs, sems)

    return pl.pallas_call(body, name=name, in_specs=[ANY] * ni, out_specs=[ANY] * no, out_shape=job.out_shape,
                          scratch_shapes=job.scratch)(*job.operands)


def _carry(job, body, n_in, n_out, at_step):
    ji, jo, js = len(job.operands), len(job.out_shape), len(job.scratch)

    def carrier(*refs):
        a, b = n_in, n_in + ji
        c, d = b + n_out, b + n_out + jo
        e = len(refs) - js
        job_refs = (refs[a:b], refs[c:d], refs[e:])
        n = len(job.phases)

        @pl.when(at_step(0, n))
        def _():
            job.phases[0](*job_refs)

        body(*refs[:a], *refs[b:c], *refs[d:e])

        for i in range(1, n):
            @pl.when(at_step(i, n))
            def _():
                job.phases[i](*job_refs)

    return carrier


def _pad_lanes(v, lo, total=128):
    return jnp.pad(v, (lo, total - lo - v.shape[0]))[None, :]


MIXER_WEIGHTS = ("w_in", "w_q_up", "w_kv_up", "conv_w")
LATE_WEIGHTS = ("w_out", "w_gate_up", "w_down")


def mixer_operands(g, sw):
    w_in = g["w_in"].reshape(D_IN, D_MODEL)
    zero = lambda rows: jnp.zeros((rows, D_MODEL), w_in.dtype)
    w_proj = jnp.concatenate(
        [w_in[:384], w_in[416:928], w_in[928:1952], w_in[1952:1960], zero(56), w_in[384:416], zero(32)], axis=0)
    wq = jnp.pad(g["w_q_up"], ((0, 0), (0, 0), (0, HEAD_LANES - NOPE - ROPE)))
    wk = jnp.pad(g["w_kv_up"][:, :, :NOPE], ((0, 0), (0, 0), (0, HEAD_LANES - NOPE)))
    wv = g["w_kv_up"][:, :, NOPE:]
    qkv = (sw["q_a_norm_w"][None, :], sw["kv_a_norm_w"][None, :], wq, wk, wv,
           _pad_lanes(jnp.concatenate([sw["q_nope_norm_w"], sw["q_pe_norm_w"]]), 0),
           _pad_lanes(sw["k_nope_norm_w"], 0), _pad_lanes(sw["k_pe_norm_w"], NOPE))
    conv_w = g["conv_w"].astype(f32).transpose(1, 0, 2).reshape(4, D_CONV)
    ssd = (conv_w, sw["conv_b"][None, :], _pad_lanes(sw["dt_bias"], 0), _pad_lanes(sw["a_log"], 0),
           _pad_lanes(sw["d_skip"], 0), sw["ssd_norm_w"][None, :])
    return dict(w_proj=w_proj, qkv=qkv, ssd=ssd, n1=sw["norm1_w"][None, :])


def late_operands(g, sw):
    return dict(wo=g["w_out"].reshape(D_MODEL, D_MODEL), wgu=g["w_gate_up"],
                wd=g["w_down"].reshape(N_DEV // 2, FF_SHARD, D_MODEL), n2=sw["norm2_w"][None, :])


def layer_fwd(x, mod, kw, cos_t, sin_t, job=None, late=None, target=None):
    sh1, sc1, g1, sh2, sc2, g2 = [mod[i:i + 1] for i in range(6)]
    pa, pz, px, plast = proj_fwd(x, kw["n1"], sh1, sc1, kw["w_proj"])
    q, k, v = qkv_fwd(pa, plast, cos_t, sin_t, kw["qkv"])
    (o, qx), carried = attn_fwd(q, k, v, job)
    if late is not None:
        kw = {**kw, **late(carried)}
    yg, states = ssd_fwd(px, pz, plast, kw["ssd"])
    x_mid = out_fwd(x, o, yg, g1, kw["wo"])
    x_out, mix, h_mid, gate, up, *loss_part = mlp_fwd(x_mid, kw["n2"], sh2, sc2, g2, kw["wgu"], kw["wd"], target)
    saved = dict(x=x, pa=pa, pz=pz, px=px, plast=plast, qx=qx, k=k, v=v, o=o, yg=yg, states=states, x_mid=x_mid,
                 mix=mix, h_mid=h_mid, gate=gate, up=up)
    return (x_out if target is None else (x_out, loss_part[0])), saved, kw, carried


def layer_bwd_head(dy, mod, kw, sv, job=None):
    _, _, g1, sh2, sc2, g2 = [mod[i:i + 1] for i in range(6)]
    (dhparts, dwg, dwu, dwd), carried = mlp_bwd(sv["h_mid"], dy, sv["gate"], sv["up"], g2, kw["wgu"], kw["wd"], job)
    dmid, dn2, dsh2, dsc2, do, dyg, dg1, dg2, dwo = out_bwd(
        dy, dhparts, sv["x_mid"], kw["n2"], sh2, sc2, sv["mix"], sv["o"], sv["yg"], g1, kw["wo"])
    early = dict(w_out=[dwo.reshape(N_DEV, D_MODEL // N_DEV, D_MODEL)], w_gate_up=[dwg, dwu],
                 w_down=[dwd.reshape(N_DEV, D_FF // N_DEV, D_MODEL)])
    head = dict(dmid=dmid, do=do, dyg=dyg, dn2=dn2, dsh2=dsh2, dsc2=dsc2, dg2=dg2, dg1=dg1)
    return head, early, carried


def layer_bwd_tail(hd, mod, kw, cos_t, sin_t, sv, job=None):
    sh1, sc1 = mod[0:1], mod[1:2]
    (dq, dk, dv), carried = attn_bwd(sv["qx"], sv["k"], sv["v"], hd["do"], job)
    dpx, dpz, dpl_dt, dcw, dcb, ddtb, dalog, ddskip, dsnw = ssd_bwd(sv["px"], sv["pz"], sv["plast"], sv["states"],
                                                                   hd["dyg"], kw["ssd"])
    dpa, dpl_k, dqaw, dkvaw, dwq, dwk, dwv, dqnw, dknw, dkpw = qkv_bwd(sv["pa"], sv["plast"], cos_t, sin_t, kw["qkv"],
                                                                       dq, dk, dv)
    dx, dn1, dsh1, dsc1, dwp = proj_bwd(sv["x"], kw["n1"], sh1, sc1, kw["w_proj"], dpa, dpz, dpx, dpl_k, dpl_dt, hd["dmid"])
    dmod = jnp.concatenate([dsh1, dsc1, hd["dg1"], hd["dsh2"], hd["dsc2"], hd["dg2"]], axis=0)
    dw_in = jnp.concatenate([dwp[:384], dwp[1984:2016], dwp[384:1920], dwp[1920:1928]], axis=0)
    grads = dict(
        norm1_w=dn1[0], norm2_w=hd["dn2"][0], q_a_norm_w=dqaw[0], kv_a_norm_w=dkvaw[0],
        q_nope_norm_w=dqnw[0, :NOPE], q_pe_norm_w=dqnw[0, NOPE:NOPE + ROPE], k_nope_norm_w=dknw[0, :NOPE],
        k_pe_norm_w=dkpw[0, NOPE:NOPE + ROPE], conv_b=dcb[0], dt_bias=ddtb[0, :N_HEADS], a_log=dalog[0, :N_HEADS],
        d_skip=ddskip[0, :N_HEADS], ssd_norm_w=dsnw[0],
        w_in=[dw_in.reshape(N_DEV, D_IN // N_DEV, D_MODEL)],
        w_q_up=[dwq[:, :, :NOPE + ROPE].astype(bf16)],
        w_kv_up=[jnp.concatenate([dwk[:, :, :NOPE], dwv], axis=2).astype(bf16)],
        conv_w=[dcw.reshape(4, N_DEV, D_CONV // N_DEV).transpose(1, 0, 2).astype(bf16)],
    )
    return dx, dmod, grads, carried


def _pack_small(get, last=None):
    flat = jnp.concatenate([get(name).reshape(-1) for name, _ in SMALL])
    flat = jnp.pad(flat, (0, SMALL_ROWS * 128 - flat.shape[0]))
    if last is not None:
        flat = flat.at[-1].set(last)
    return flat.reshape(SMALL_ROWS, 128)


def _unpack_small(packed):
    flat = packed.reshape(-1)
    out, off = {}, 0
    for name, size in SMALL:
        out[name] = flat[off:off + 2 * size].reshape(2, size)
        off += 2 * size
    return out


def kernel(x, c, positions, norm1_w, norm2_w, w_ada, b_ada, w_in, q_a_norm_w, w_q_up, kv_a_norm_w, w_kv_up, q_nope_norm_w, q_pe_norm_w, k_nope_norm_w, k_pe_norm_w, conv_w, conv_b, dt_bias, a_log, d_skip, ssd_norm_w, w_out, w_gate_up, w_down, loss_target, m_norm1_w, m_norm2_w, m_w_ada, m_b_ada, m_w_in, m_q_a_norm_w, m_w_q_up, m_kv_a_norm_w, m_w_kv_up, m_q_nope_norm_w, m_q_pe_norm_w, m_k_nope_norm_w, m_k_pe_norm_w, m_conv_w, m_conv_b, m_dt_bias, m_a_log, m_d_skip, m_ssd_norm_w, m_w_out, m_w_gate_up, m_w_down, v_norm1_w, v_norm2_w, v_w_ada, v_b_ada, v_w_in, v_q_a_norm_w, v_w_q_up, v_kv_a_norm_w, v_w_kv_up, v_q_nope_norm_w, v_q_pe_norm_w, v_k_nope_norm_w, v_k_pe_norm_w, v_conv_w, v_conv_b, v_dt_bias, v_a_log, v_d_skip, v_ssd_norm_w, v_w_out, v_w_gate_up, v_w_down):
    w = dict(norm1_w=norm1_w, norm2_w=norm2_w, w_ada=w_ada, b_ada=b_ada, w_in=w_in, q_a_norm_w=q_a_norm_w, w_q_up=w_q_up,
             kv_a_norm_w=kv_a_norm_w, w_kv_up=w_kv_up, q_nope_norm_w=q_nope_norm_w, q_pe_norm_w=q_pe_norm_w,
             k_nope_norm_w=k_nope_norm_w, k_pe_norm_w=k_pe_norm_w, conv_w=conv_w, conv_b=conv_b, dt_bias=dt_bias,
             a_log=a_log, d_skip=d_skip, ssd_norm_w=ssd_norm_w, w_out=w_out, w_gate_up=w_gate_up, w_down=w_down)
    m = dict(norm1_w=m_norm1_w, norm2_w=m_norm2_w, w_ada=m_w_ada, b_ada=m_b_ada, w_in=m_w_in, q_a_norm_w=m_q_a_norm_w,
             w_q_up=m_w_q_up, kv_a_norm_w=m_kv_a_norm_w, w_kv_up=m_w_kv_up, q_nope_norm_w=m_q_nope_norm_w,
             q_pe_norm_w=m_q_pe_norm_w, k_nope_norm_w=m_k_nope_norm_w, k_pe_norm_w=m_k_pe_norm_w, conv_w=m_conv_w,
             conv_b=m_conv_b, dt_bias=m_dt_bias, a_log=m_a_log, d_skip=m_d_skip, ssd_norm_w=m_ssd_norm_w, w_out=m_w_out,
             w_gate_up=m_w_gate_up, w_down=m_w_down)
    v = dict(norm1_w=v_norm1_w, norm2_w=v_norm2_w, w_ada=v_w_ada, b_ada=v_b_ada, w_in=v_w_in, q_a_norm_w=v_q_a_norm_w,
             w_q_up=v_w_q_up, kv_a_norm_w=v_kv_a_norm_w, w_kv_up=v_w_kv_up, q_nope_norm_w=v_q_nope_norm_w,
             q_pe_norm_w=v_q_pe_norm_w, k_nope_norm_w=v_k_nope_norm_w, k_pe_norm_w=v_k_pe_norm_w, conv_w=v_conv_w,
             conv_b=v_conv_b, dt_bias=v_dt_bias, a_log=v_a_log, d_skip=v_d_skip, ssd_norm_w=v_ssd_norm_w, w_out=v_w_out,
             w_gate_up=v_w_gate_up, w_down=v_w_down)
    me = _my_index()
    seq = x.shape[1]

    def shard(name, l):
        if name == "conv_w":
            return w[name][l]
        if name in TRANSPOSED:
            return jnp.swapaxes(w[name][l], 0, 1).astype(bf16)
        return w[name][l].astype(bf16)

    def shards(names, l):
        return [shard(name, l) for name in names]

    small = [{name: w[name][l] for name, _ in SMALL if name != "b_ada"} for l in range(2)]
    n_late = len(LATE_WEIGHTS)

    inv_freq = 1.0 / (ROPE_THETA ** (jnp.arange(0, ROPE, 2, dtype=f32) / ROPE))
    inv = _pad_lanes(jnp.concatenate([inv_freq, inv_freq]), NOPE)
    (cos_t, sin_t), first = rope_tables(positions.reshape(seq, 1), inv, gather_job([c] + shards(MIXER_WEIGHTS, 0)))
    c_all = first[0].reshape(N_DEV, D_MODEL)
    kws = [mixer_operands(dict(zip(MIXER_WEIGHTS, first[1:])), small[0]), None]

    b_cols = lax.dynamic_slice_in_dim(b_ada, me * 768, 768, axis=1)
    mod_cols = ada_fwd(c_all, w_ada, b_cols)
    (mod_all,) = comm_call(gather_job([mod_cols]), "gather_mod")
    mod_me = lax.dynamic_index_in_dim(mod_all, me, axis=2, keepdims=False)
    mods = [mod_me[:, l, :].reshape(6, D_MODEL) for l in range(2)]

    saved = [None, None]
    h, saved[0], kws[0], got = layer_fwd(
        x[0], mods[0], kws[0], cos_t, sin_t, gather_job(shards(LATE_WEIGHTS, 0) + shards(MIXER_WEIGHTS, 1)),
        lambda got: late_operands(dict(zip(LATE_WEIGHTS, got[:n_late])), small[0]))
    kws[1] = mixer_operands(dict(zip(MIXER_WEIGHTS, got[n_late:])), small[1])
    (dy, loss_part), saved[1], kws[1], _ = layer_fwd(
        h, mods[1], kws[1], cos_t, sin_t, gather_job(shards(LATE_WEIGHTS, 1)),
        lambda got: late_operands(dict(zip(LATE_WEIGHTS, got)), small[1]), loss_target[0])

    early, late = ("w_out", "w_gate_up", "w_down"), ("w_in", "w_q_up", "w_kv_up", "conv_w")
    parts = [{}, {}]
    head, pieces, _ = layer_bwd_head(dy, mods[1], kws[1], saved[1])
    dy, dmod1, grads1, got = layer_bwd_tail(head, mods[1], kws[1], cos_t, sin_t, saved[1], scatter_job([pieces[n] for n in early]))
    parts[1].update(zip(early, got))
    head, pieces, got = layer_bwd_head(dy, mods[0], kws[0], saved[0], scatter_job([grads1[n] for n in late]))
    parts[1].update(zip(late, got))
    dy, dmod0, grads0, got = layer_bwd_tail(head, mods[0], kws[0], cos_t, sin_t, saved[0], scatter_job([pieces[n] for n in early]))
    parts[0].update(zip(early, got))
    grad_x = dy[None]

    small_part = {name: jnp.stack([grads0[name], grads1[name]]) for name, _ in SMALL if name != "b_ada"}
    small_part["b_ada"] = jnp.stack([dmod0.reshape(-1), dmod1.reshape(-1)])
    swap = lambda a: jnp.swapaxes(a, 1, 2)
    updated, last = adamw_two_layers(
        [parts[l]["w_gate_up"] for l in range(2)], swap(w_gate_up), swap(m_w_gate_up), swap(v_w_gate_up), "adamw_w_gate_up_comm",
        merge_jobs(scatter_job([grads0[n] for n in late]),
                   gather_job([_pack_small(lambda n: small_part[n], loss_part[0, 0])])))
    parts[0].update(zip(late, last[:len(late)]))
    small_all = last[len(late)]
    packed = adamw(small_all, _pack_small(lambda n: w[n])[None], _pack_small(lambda n: m[n])[None],
                   _pack_small(lambda n: v[n])[None], 0, None, "adamw_small")
    loss = packed[0][0, -1, -1]
    res = {}
    for key, arr in zip("gdmv", packed):
        for name, val in _unpack_small(arr[0]).items():
            res[key, name] = val

    off = 2 * (1024 + 1024)
    dmod_all = small_all.reshape(N_DEV, -1)[:, off:off + 2 * 6144].reshape(N_DEV, 2, 6144)
    dmod_cols = lax.dynamic_slice_in_dim(dmod_all, me * 768, 768, axis=2).transpose(1, 0, 2)
    g_ada = ada_bwd(c_all, dmod_cols)
    out = None
    for l in range(2):
        out = adamw(g_ada[l][None], w_ada, m_w_ada, v_w_ada, l, out, "adamw_w_ada")
    res.update(zip([(key, "w_ada") for key in "gdmv"], out))

    inside = lambda a: jnp.transpose(a, (2, 0, 1))
    out = adamw_layers_inside([parts[l]["w_in"] for l in range(2)], inside(w_in), inside(m_w_in), inside(v_w_in), "adamw_w_in")
    res.update(zip([(key, "w_in") for key in "gdmv"], [jnp.transpose(a, (1, 2, 0)) for a in out]))
    res.update(zip([(key, "w_gate_up") for key in "gdmv"], [swap(a) for a in updated]))
    for name in BIG:
        if name in ("w_in", "w_gate_up"):
            continue
        view = (lambda a: jnp.swapaxes(a, 1, 2)) if name in TRANSPOSED else (lambda a: a)
        out = None
        for l in range(2):
            out = adamw(parts[l][name], view(w[name]), view(m[name]), view(v[name]), l, out, "adamw_" + name)
        res.update(zip([(key, name) for key in "gdmv"], [view(a) for a in out]))

    return (loss, grad_x, *[res["g", n] for n in WEIGHTS], *[res["d", n] for n in WEIGHTS],
            *[res["m", n] for n in WEIGHTS], *[res["v", n] for n in WEIGHTS])
```

```python
import functools

import jax
import jax.numpy as jnp
from jax import lax
from jax.experimental import pallas as pl
from jax.experimental.pallas import tpu as pltpu

f32 = jnp.float32
bf16 = jnp.bfloat16

N_DEV = 8
D_MODEL = 1024
N_HEADS = 8
HEAD_LANES = 128
NOPE = 64
ROPE = 32
V_DIM = 64
Q_RANK = 256
KV_RANK = 128
D_SSD = 512
D_CONV = 1024
SSD_STATE = 128
SSD_HEAD_DIM = 64
CHUNK = 128
HALO = 8
D_FF = 2816
FF_SHARD = 704
D_IN = 1960
D_PROJ = 2048
EPS = 1e-6
LOG2E = 1.4426950408889634
LN2 = 0.6931471805599453
Q_SCALE = (NOPE + ROPE) ** -0.5 * LOG2E
SPARE_Q = NOPE + ROPE
SPARE_V = V_DIM
ATTN_BLOCK_FWD = 1024
ATTN_ROWS_FWD = 256
ATTN_HEADS_FWD = 8
ATTN_HEADS_BWD = 4
MLP_FWD_ROWS = 1024
MLP_BWD_CHUNK = 256
MLP_BWD_ROWS = 1024
ROPE_THETA = 10000.0
NEG = -1e30

ADAM_LR = 0.001
ADAM_B1 = 0.9
ADAM_B2 = 0.999
ADAM_EPS = 1e-08
ADAM_WD = 0.01
ADAM_STEP = 10
ADAMW_BLOCK_BYTES = 36 << 20

MESH = pl.DeviceIdType.MESH
ANY = pl.BlockSpec(memory_space=pl.ANY)

SMALL = (("norm1_w", 1024), ("norm2_w", 1024), ("b_ada", 6144), ("q_a_norm_w", 256), ("kv_a_norm_w", 128),
         ("q_nope_norm_w", 64), ("q_pe_norm_w", 32), ("k_nope_norm_w", 64), ("k_pe_norm_w", 32),
         ("conv_b", 1024), ("dt_bias", 8), ("a_log", 8), ("d_skip", 8), ("ssd_norm_w", 512))
SMALL_ROWS = 168
BIG = ("w_in", "w_q_up", "w_kv_up", "conv_w", "w_out", "w_gate_up", "w_down")
TRANSPOSED = ("w_in", "w_gate_up")
WEIGHTS = ("norm1_w", "norm2_w", "w_ada", "b_ada", "w_in", "q_a_norm_w", "w_q_up", "kv_a_norm_w", "w_kv_up",
           "q_nope_norm_w", "q_pe_norm_w", "k_nope_norm_w", "k_pe_norm_w", "conv_w", "conv_b", "dt_bias",
           "a_log", "d_skip", "ssd_norm_w", "w_out", "w_gate_up", "w_down")


def _dot(a, b, ca, cb):
    return lax.dot_general(a.astype(bf16), b.astype(bf16), (((ca,), (cb,)), ((), ())), preferred_element_type=f32)


@jax.custom_vjp
def mm(a, b):
    return _dot(a, b, 1, 0)


def _mm_fwd(a, b):
    return _dot(a, b, 1, 0), (a, b)


def _mm_bwd(res, g):
    a, b = res
    return _dot(g, b, 1, 1).astype(a.dtype), _dot(a, g, 0, 0).astype(b.dtype)


mm.defvjp(_mm_fwd, _mm_bwd)


@jax.custom_vjp
def _mm_slot(a, w, slot):
    return _dot(a, w, 1, 0)


def _mm_slot_fwd(a, w, slot):
    return _dot(a, w, 1, 0), (a, w)


def _mm_slot_bwd(res, g):
    a, w = res
    return _dot(g, w, 1, 1).astype(a.dtype), None, _dot(a, g, 0, 0)


_mm_slot.defvjp(_mm_slot_fwd, _mm_slot_bwd)


def mmw(a, w, slot=None):
    return _dot(a, w, 1, 0) if slot is None else _mm_slot(a, w, slot)


@jax.custom_vjp
def _mm_slot_t(a, wt, slot):
    return _dot(a, wt, 1, 1)


def _mm_slot_t_fwd(a, wt, slot):
    return _dot(a, wt, 1, 1), (a, wt)


def _mm_slot_t_bwd(res, g):
    a, wt = res
    return _dot(g, wt, 1, 0).astype(a.dtype), None, _dot(g, a, 0, 0)


_mm_slot_t.defvjp(_mm_slot_t_fwd, _mm_slot_t_bwd)


def mmw_t(a, wt, slot=None):
    return _dot(a, wt, 1, 1) if slot is None else _mm_slot_t(a, wt, slot)


@jax.custom_vjp
def mm_nt(a, b):
    return _dot(a, b, 1, 1)


def _mm_nt_fwd(a, b):
    return _dot(a, b, 1, 1), (a, b)


def _mm_nt_bwd(res, g):
    a, b = res
    return _dot(g, b, 1, 0).astype(a.dtype), _dot(g, a, 0, 0).astype(b.dtype)


mm_nt.defvjp(_mm_nt_fwd, _mm_nt_bwd)


@jax.custom_vjp
def mm_tn(a, b):
    return _dot(a, b, 0, 0)


def _mm_tn_fwd(a, b):
    return _dot(a, b, 0, 0), (a, b)


def _mm_tn_bwd(res, g):
    a, b = res
    return _dot(b, g, 1, 1).astype(a.dtype), _dot(a, g, 1, 0).astype(b.dtype)


mm_tn.defvjp(_mm_tn_fwd, _mm_tn_bwd)


def _rms(x, w):
    return x * lax.rsqrt(jnp.mean(x * x, axis=-1, keepdims=True) + EPS) * w


def _const(shape):
    n = len(shape)
    return pl.BlockSpec(shape, lambda *_: (0,) * n)


def _accumulate(first, refs, vals):
    @pl.when(first)
    def _():
        for r, v in zip(refs, vals):
            r[...] = v

    @pl.when(jnp.logical_not(first))
    def _():
        for r, v in zip(refs, vals):
            r[...] += v


def _accumulate_then_cast(first, last, accs, outs, vals):
    _accumulate(first, accs, vals)

    @pl.when(last)
    def _():
        for a, o in zip(accs, outs):
            o[...] = a[...].astype(o.dtype)


def _token_block(s):
    return min(512, s)


def _f_proj(x, nw, sh, sc, w, slot=None):
    h = _rms(x, nw) * (1.0 + sc) + sh
    return mmw_t(h, w, slot)


def _f_qkv(pa, plast, cos_t, sin_t, qaw, kvaw, wq, wk, wv, qnw, knw, kpw, slots=None):
    sq, sk, sv = slots if slots is not None else ([None] * N_HEADS,) * 3
    lane = lax.broadcasted_iota(jnp.int32, (1, HEAD_LANES), 1)
    m_nope = lane < NOPE
    m_pe = (lane >= NOPE) & (lane < NOPE + ROPE)
    rows = pa.shape[0]

    def rope(t):
        half = ROPE // 2
        swapped = jnp.concatenate(
            [jnp.zeros((rows, NOPE), f32), t[:, NOPE + half:NOPE + ROPE], t[:, NOPE:NOPE + half],
             jnp.zeros((rows, HEAD_LANES - NOPE - ROPE), f32)], axis=1)
        return t * cos_t + swapped * sin_t

    qa = _rms(pa[:, :Q_RANK], qaw)
    kva = _rms(pa[:, Q_RANK:Q_RANK + KV_RANK], kvaw)
    kp = jnp.where(m_pe, plast, 0.0)
    kp = kp * lax.rsqrt(jnp.sum(kp * kp, axis=-1, keepdims=True) / ROPE + EPS) * kpw
    k_rot = rope(kp)
    qs, ks, vs = [], [], []
    for h in range(N_HEADS):
        qh = mmw(qa, wq[h], sq[h])
        ss_n = jnp.sum(jnp.where(m_nope, qh * qh, 0.0), axis=-1, keepdims=True) / NOPE
        ss_p = jnp.sum(jnp.where(m_pe, qh * qh, 0.0), axis=-1, keepdims=True) / ROPE
        r = jnp.where(m_nope, lax.rsqrt(ss_n + EPS), lax.rsqrt(ss_p + EPS))
        qs.append(rope(qh * r * qnw) * Q_SCALE)
        kh = mmw(kva, wk[h], sk[h])
        kh = kh * lax.rsqrt(jnp.sum(kh * kh, axis=-1, keepdims=True) / NOPE + EPS) * knw
        ks.append(kh + k_rot)
        vs.append(mmw(kva, wv[h], sv[h]))
    return jnp.stack(qs), jnp.stack(ks), jnp.stack(vs)


def _f_ssd(xext, z, plast, prev, cw, cb, dtb, alog, dskip, snw):
    n = CHUNK
    conv = cb
    for k in range(4):
        conv = conv + cw[k:k + 1] * xext[HALO - 3 + k:HALO - 3 + k + n]
    xc = jax.nn.silu(conv)
    xs, bm, cm = xc[:, :D_SSD], xc[:, D_SSD:D_SSD + 2 * SSD_STATE], xc[:, D_SSD + 2 * SSD_STATE:]
    lane = lax.broadcasted_iota(jnp.int32, (1, 128), 1)
    dt = jax.nn.softplus(jnp.where(lane < N_HEADS, plast, 0.0) + dtb)
    adt = dt * (-jnp.exp(alog))
    row = lax.broadcasted_iota(jnp.int32, (n, n), 0)
    col = lax.broadcasted_iota(jnp.int32, (n, n), 1)
    tri = row >= col
    acs = jnp.dot(tri.astype(f32), adt, precision=lax.Precision.HIGHEST, preferred_element_type=f32)
    acs_t = acs.T
    bgs = [bm[:, g * SSD_STATE:(g + 1) * SSD_STATE] for g in range(2)]
    cgs = [cm[:, g * SSD_STATE:(g + 1) * SSD_STATE] for g in range(2)]
    cb_ts = [mm_nt(cgs[g], bgs[g]) for g in range(2)]
    low = lane < SSD_HEAD_DIM
    low_rows = lax.broadcasted_iota(jnp.int32, (2 * SSD_HEAD_DIM, 1), 0) < SSD_HEAD_DIM

    def both(a0, a1):
        return jnp.where(low, a0, a1)

    pre = []
    for i in range(N_HEADS // 2):
        h0, h1 = 2 * i, 2 * i + 1
        col0, col1 = acs[:, h0:h0 + 1], acs[:, h1:h1 + 1]
        last0, last1 = acs[n - 1:n, h0:h0 + 1], acs[n - 1:n, h1:h1 + 1]
        cb_t = cb_ts[i // 2]
        scores0 = cb_t * jnp.exp(jnp.where(tri, col0 - acs_t[h0:h0 + 1, :], -jnp.inf))
        scores1 = cb_t * jnp.exp(jnp.where(tri, col1 - acs_t[h1:h1 + 1, :], -jnp.inf))
        xp = xs[:, i * 128:(i + 1) * 128]
        xdt = xp * both(dt[:, h0:h0 + 1], dt[:, h1:h1 + 1])
        weighted = xdt * both(jnp.exp(last0 - col0), jnp.exp(last1 - col1))
        chunk_decay = jnp.where(low_rows, jnp.exp(last0), jnp.exp(last1))
        in_decay = both(jnp.exp(col0), jnp.exp(col1))
        skip = both(dskip[:, h0:h0 + 1], dskip[:, h1:h1 + 1]) * xp
        pre.append((scores0, scores1, xdt, weighted, chunk_decay, in_decay, skip))
    prods = []
    for i in range(N_HEADS // 2):
        scores0, scores1, xdt, weighted, _, _, _ = pre[i]
        g = i // 2
        y_diag = mm(scores0, jnp.where(low, xdt, 0.0)) + mm(scores1, jnp.where(low, 0.0, xdt))
        prods.append((y_diag, mm_tn(weighted, bgs[g]), mm_nt(cgs[g], prev[i])))
    ys, news = [], []
    for i in range(N_HEADS // 2):
        y_diag, st, y_off = prods[i]
        _, _, _, _, chunk_decay, in_decay, skip = pre[i]
        news.append(chunk_decay * prev[i] + st)
        ys.append(y_diag + y_off * in_decay + skip)
    y = jnp.concatenate(ys, axis=1)
    yg = y * jax.nn.silu(z)
    half = D_SSD // 2
    outs = []
    for g in range(2):
        t = yg[:, g * half:(g + 1) * half]
        outs.append(t * lax.rsqrt(jnp.mean(t * t, axis=-1, keepdims=True) + EPS))
    return jnp.concatenate(outs, axis=1) * snw, jnp.stack(news)


def _f_out(o, yg, g1, wo, slot=None):
    cat = jnp.concatenate([o[h] for h in range(N_HEADS)] + [yg], axis=1)
    return g1 * mmw(cat, wo, slot)


def _f_modulate(x, nw, sh, sc):
    return _rms(x, nw) * (1.0 + sc) + sh


def proj_fwd(x, nw, sh, sc, w):
    s = x.shape[0]
    ts = _token_block(s)

    def body(x_ref, nw_ref, sh_ref, sc_ref, w_ref, pa_ref, pz_ref, px_ref, pl_ref):
        p = _f_proj(x_ref[...], nw_ref[...], sh_ref[...], sc_ref[...], w_ref[...])
        pa_ref[...] = p[:, :384]
        pz_ref[...] = p[:, 384:896]
        px_ref[...] = p[:, 896:1920]
        pl_ref[...] = p[:, 1920:]

    vec = _const((1, D_MODEL))
    return pl.pallas_call(
        body, name="proj_fwd", grid=(s // ts,),
        in_specs=[pl.BlockSpec((ts, D_MODEL), lambda i: (i, 0)), vec, vec, vec, _const((D_PROJ, D_MODEL))],
        out_specs=[pl.BlockSpec((ts, 384), lambda i: (i, 0)), pl.BlockSpec((ts, 512), lambda i: (i, 0)),
                   pl.BlockSpec((ts, 1024), lambda i: (i, 0)), pl.BlockSpec((ts, 128), lambda i: (i, 0))],
        out_shape=[jax.ShapeDtypeStruct((s, 384), f32), jax.ShapeDtypeStruct((s, 512), f32),
                   jax.ShapeDtypeStruct((s, 1024), f32), jax.ShapeDtypeStruct((s, 128), f32)],
    )(x, nw, sh, sc, w)


def rope_tables(pos, inv, job=None):
    s = pos.shape[0]
    ts = _token_block(s)

    def body(pos_ref, inv_ref, cos_ref, sin_ref):
        ang = pos_ref[...].astype(f32) * inv_ref[...]
        lane = lax.broadcasted_iota(jnp.int32, (1, HEAD_LANES), 1)
        half = ROPE // 2
        cos_ref[...] = jnp.where(lane < NOPE, 1.0, jnp.where(lane < NOPE + ROPE, jnp.cos(ang), 0.0))
        sn = jnp.sin(ang)
        sin_ref[...] = jnp.where((lane >= NOPE) & (lane < NOPE + half), -sn,
                                 jnp.where((lane >= NOPE + half) & (lane < NOPE + ROPE), sn, 0.0))

    steps = s // ts
    return _call_with_job(
        body, "rope_tables" if job is None else "rope_tables_comm", (steps,), job,
        in_specs=[pl.BlockSpec((ts, 1), lambda i: (i, 0)), _const((1, HEAD_LANES))],
        out_specs=[pl.BlockSpec((ts, HEAD_LANES), lambda i: (i, 0))] * 2,
        out_shape=[jax.ShapeDtypeStruct((s, HEAD_LANES), f32)] * 2, scratch_shapes=[], operands=(pos, inv),
        relay_at=(max(steps - 2, 0),))


def _qkv_param_specs():
    return [_const((1, Q_RANK)), _const((1, KV_RANK)), _const((N_HEADS, Q_RANK, HEAD_LANES)),
            _const((N_HEADS, KV_RANK, HEAD_LANES)), _const((N_HEADS, KV_RANK, V_DIM)),
            _const((1, HEAD_LANES)), _const((1, HEAD_LANES)), _const((1, HEAD_LANES))]


def qkv_fwd(pa, plast, cos_t, sin_t, params):
    s = pa.shape[0]
    ts = _token_block(s)

    def body(pa_ref, pl_ref, cos_ref, sin_ref, *rest):
        prm = [r[...] for r in rest[:8]]
        q_ref, k_ref, v_ref = rest[8:]
        q, k, v = _f_qkv(pa_ref[...], pl_ref[...], cos_ref[...], sin_ref[...], *prm)
        q_ref[...] = q.astype(bf16)
        lane = lax.broadcasted_iota(jnp.int32, (1, 1, HEAD_LANES), 2)
        k_ref[...] = jnp.where((lane == SPARE_Q) | (lane == SPARE_Q + 1), 1.0, k).astype(bf16)
        v_ref[...] = jnp.concatenate([v, jnp.ones_like(v)], axis=-1).astype(bf16)

    tok = lambda w: pl.BlockSpec((ts, w), lambda i: (i, 0))
    head = pl.BlockSpec((N_HEADS, ts, HEAD_LANES), lambda i: (0, i, 0))
    return pl.pallas_call(
        body, name="qkv_fwd", grid=(s // ts,),
        in_specs=[tok(384), tok(128), tok(128), tok(128)] + _qkv_param_specs(),
        out_specs=[head] * 3, out_shape=[jax.ShapeDtypeStruct((N_HEADS, s, HEAD_LANES), bf16)] * 3,
    )(pa, plast, cos_t, sin_t, *params)


def _scores(q, k):
    return lax.dot_general(q, k, (((1,), (1,)), ((), ())), preferred_element_type=f32)


def _tril(rows, cols, row_offset):
    row = row_offset + lax.broadcasted_iota(jnp.int32, (rows, cols), 0)
    col = lax.broadcasted_iota(jnp.int32, (rows, cols), 1)
    return row >= col


def _call_with_job(body, name, grid, job, in_specs, out_specs, out_shape, scratch_shapes, operands, relay_at=None):
    if job is None:
        res = pl.pallas_call(body, name=name, grid=grid, in_specs=in_specs, out_specs=out_specs, out_shape=out_shape,
                             scratch_shapes=scratch_shapes)(*operands)
        return res, None

    def at_step(i, n):
        if i == 0:
            want = [0] * len(grid)
        elif i == n - 1:
            want = [g - 1 for g in grid]
        else:
            want = relay_at
        return functools.reduce(jnp.logical_and, [pl.program_id(a) == s for a, s in enumerate(want)])

    carrier = _carry(job, body, len(in_specs), len(out_specs), at_step)
    res = pl.pallas_call(
        carrier, name=name, grid=grid,
        in_specs=list(in_specs) + [ANY] * len(job.operands), out_specs=list(out_specs) + [ANY] * len(job.out_shape),
        out_shape=list(out_shape) + list(job.out_shape), scratch_shapes=list(scratch_shapes) + job.scratch,
    )(*operands, *job.operands)
    return res[:len(out_specs)], res[len(out_specs):]


def attn_fwd(q, k, v, job=None):
    s = q.shape[1]
    t = min(ATTN_BLOCK_FWD, s)
    nb = s // t

    rb = min(ATTN_ROWS_FWD, t)

    hp = ATTN_HEADS_FWD

    def body(q_ref, k_ref, v_ref, o_ref, qx_ref, m_sc, acc_sc):
        qi = pl.program_id(1)
        m_sc[...] = jnp.full(m_sc.shape, NEG, f32)
        acc_sc[...] = jnp.zeros(acc_sc.shape, f32)

        def step(k0, diagonal):
            chains = [(hh, r) for hh in range(hp) for r in range(t // rb)]

            def scores(hh, r):
                nk = (r + 1) * rb if diagonal else t
                sc = _scores(q_ref[hh, pl.ds(r * rb, rb), :], k_ref[hh, pl.ds(k0, nk), :])
                return jnp.where(_tril(rb, nk, r * rb), sc, NEG) if diagonal else sc

            ahead = scores(*chains[0])
            for c, (hh, r) in enumerate(chains):
                sc = ahead
                if c + 1 < len(chains):
                    ahead = scores(*chains[c + 1])
                rows = pl.ds(r * rb, rb)
                keys = pl.ds(k0, (r + 1) * rb if diagonal else t)
                m_prev = m_sc[hh, rows, :1]
                m_new = jnp.maximum(m_prev, jnp.max(sc, axis=-1, keepdims=True))
                p = jnp.exp2(sc - m_new)
                alpha = jnp.exp2(m_prev - m_new)
                acc = alpha * acc_sc[hh, rows, :] + jnp.dot(p.astype(bf16), v_ref[hh, keys, :], preferred_element_type=f32)
                if diagonal:
                    l = acc[:, V_DIM:V_DIM + 1]
                    o_ref[hh, rows, :] = acc[:, :V_DIM] / l
                    lse = m_new + jnp.log2(l)
                    high = lse.astype(bf16)
                    low = (lse - high.astype(f32)).astype(bf16)
                    lane = lax.broadcasted_iota(jnp.int32, (1, HEAD_LANES), 1)
                    qx_ref[hh, rows, :] = jnp.where(lane == SPARE_Q, -high,
                                                    jnp.where(lane == SPARE_Q + 1, -low, q_ref[hh, rows, :]))
                else:
                    acc_sc[hh, rows, :] = acc
                    m_sc[hh, rows, :] = jnp.broadcast_to(m_new, (rb, 128))

        def below(ki, carry):
            step(pl.multiple_of(ki * t, t), False)
            return carry

        lax.fori_loop(0, qi, below, 0)
        step(pl.multiple_of(qi * t, t), True)

    return _call_with_job(
        body, "attn_fwd" if job is None else "attn_fwd_comm", (N_HEADS // hp, nb), job,
        in_specs=[pl.BlockSpec((hp, t, HEAD_LANES), lambda h, qi: (h, qi, 0)),
                  pl.BlockSpec((hp, s, HEAD_LANES), lambda h, qi: (h, 0, 0)),
                  pl.BlockSpec((hp, s, HEAD_LANES), lambda h, qi: (h, 0, 0))],
        out_specs=[pl.BlockSpec((hp, t, V_DIM), lambda h, qi: (h, qi, 0)),
                   pl.BlockSpec((hp, t, HEAD_LANES), lambda h, qi: (h, qi, 0))],
        out_shape=[jax.ShapeDtypeStruct((N_HEADS, s, V_DIM), f32), jax.ShapeDtypeStruct((N_HEADS, s, HEAD_LANES), bf16)],
        scratch_shapes=[pltpu.VMEM((hp, t, 128), f32), pltpu.VMEM((hp, t, HEAD_LANES), f32)],
        operands=(q, k, v), relay_at=(N_HEADS // hp - 1, nb - 1))


def _ssd_param_specs():
    return [_const((4, D_CONV)), _const((1, D_CONV)), _const((1, 128)), _const((1, 128)), _const((1, 128)),
            _const((1, D_SSD))]


def ssd_fwd(px, pz, plast, params):
    s = px.shape[0]
    nc = s // CHUNK

    def body(px_ref, pz_ref, pl_ref, cw_ref, cb_ref, dtb_ref, alog_ref, dskip_ref, snw_ref, yg_ref, st_ref,
             state_sc, halo_sc):
        i = pl.program_id(0)

        @pl.when(i == 0)
        def _():
            state_sc[...] = jnp.zeros(state_sc.shape, f32)
            halo_sc[...] = jnp.zeros(halo_sc.shape, f32)

        x = px_ref[...]
        prev = state_sc[...]
        st_ref[...] = prev
        xext = jnp.concatenate([halo_sc[...], x], axis=0)
        yg, new = _f_ssd(xext, pz_ref[...], pl_ref[...], prev, cw_ref[...], cb_ref[...], dtb_ref[...],
                         alog_ref[...], dskip_ref[...], snw_ref[...])
        yg_ref[...] = yg
        state_sc[...] = new
        halo_sc[...] = x[CHUNK - HALO:]

    tok = lambda w: pl.BlockSpec((CHUNK, w), lambda i: (i, 0))
    return pl.pallas_call(
        body, name="ssd_fwd", grid=(nc,),
        in_specs=[tok(D_CONV), tok(D_SSD), tok(128)] + _ssd_param_specs(),
        out_specs=[tok(D_SSD), pl.BlockSpec((None, N_HEADS // 2, 2 * SSD_HEAD_DIM, SSD_STATE), lambda i: (i, 0, 0, 0))],
        out_shape=[jax.ShapeDtypeStruct((s, D_SSD), f32),
                   jax.ShapeDtypeStruct((nc, N_HEADS // 2, 2 * SSD_HEAD_DIM, SSD_STATE), f32)],
        scratch_shapes=[pltpu.VMEM((N_HEADS // 2, 2 * SSD_HEAD_DIM, SSD_STATE), f32), pltpu.VMEM((HALO, D_CONV), f32)],
    )(px, pz, plast, *params)


def out_fwd(x, o, yg, g1, wo):
    s = x.shape[0]
    ts = _token_block(s)

    def body(x_ref, o_ref, yg_ref, g1_ref, wo_ref, out_ref):
        out_ref[...] = x_ref[...] + _f_out(o_ref[...], yg_ref[...], g1_ref[...], wo_ref[...])

    return pl.pallas_call(
        body, name="out_fwd", grid=(s // ts,),
        in_specs=[pl.BlockSpec((ts, D_MODEL), lambda i: (i, 0)), pl.BlockSpec((N_HEADS, ts, V_DIM), lambda i: (0, i, 0)),
                  pl.BlockSpec((ts, D_SSD), lambda i: (i, 0)), _const((1, D_MODEL)), _const((D_MODEL, D_MODEL))],
        out_specs=pl.BlockSpec((ts, D_MODEL), lambda i: (i, 0)),
        out_shape=jax.ShapeDtypeStruct((s, D_MODEL), f32),
    )(x, o, yg, g1, wo)


def mlp_fwd(x, nw, sh, sc, g2, wgu, wd, target=None, job=None):
    s = x.shape[0]
    ts = min(MLP_FWD_ROWS, s)
    nj = N_DEV // 2

    def body(x_ref, nw_ref, sh_ref, sc_ref, g2_ref, wg_ref, wu_ref, wd_ref, *rest):
        if target is None:
            out_ref, mix_ref, h_ref, gate_ref, up_ref = rest
        else:
            t_ref, out_ref, mix_ref, h_ref, gate_ref, up_ref, loss_ref = rest
        j = pl.program_id(1)
        first_block = pl.program_id(0) == 0

        @pl.when(j == 0)
        def _():
            h_ref[...] = _f_modulate(x_ref[...], nw_ref[...], sh_ref[...], sc_ref[...]).astype(bf16)
            mix_ref[...] = jnp.zeros(mix_ref.shape, f32)

        nr = max(ts // 512, 1)
        half = ts // nr
        wg, wu, wd = wg_ref[...], wu_ref[...], wd_ref[...]
        products = lambda r: (mmw_t(h_ref[pl.ds(r * half, half), :], wg), mmw_t(h_ref[pl.ds(r * half, half), :], wu))
        ahead = products(0)
        for r in range(nr):
            gate, up = ahead
            if r + 1 < nr:
                ahead = products(r + 1)
            rows = pl.ds(r * half, half)
            gate_ref[rows, :] = gate.astype(bf16)
            up_ref[rows, :] = up.astype(bf16)
            mix_ref[rows, :] += mmw(jax.nn.silu(gate) * up, wd)

        @pl.when(j == nj - 1)
        def _():
            y = x_ref[...] + g2_ref[...] * mix_ref[...]
            if target is None:
                out_ref[...] = y
            else:
                d = y - t_ref[...]
                out_ref[...] = d * (1.0 / D_MODEL)
                part = 0.5 * jnp.sum(jnp.sum(d * d, axis=-1, keepdims=True) * (1.0 / D_MODEL), axis=0, keepdims=True)
                _accumulate(first_block, [loss_ref], [jnp.broadcast_to(part, (8, 128))])

    vec = _const((1, D_MODEL))
    tok = pl.BlockSpec((ts, D_MODEL), lambda i, j: (i, 0))
    wide = pl.BlockSpec((None, ts, FF_SHARD), lambda i, j: (j, i, 0))
    last = target is not None
    name = ("mlp_fwd_loss" if last else "mlp_fwd") + ("" if job is None else "_comm")
    return _call_with_job(
        body, name, (s // ts, nj), job,
        in_specs=[tok, vec, vec, vec, vec,
                  pl.BlockSpec((None, FF_SHARD, D_MODEL), lambda i, j: (j, 0, 0)),
                  pl.BlockSpec((None, FF_SHARD, D_MODEL), lambda i, j: (j + nj, 0, 0)),
                  pl.BlockSpec((None, FF_SHARD, D_MODEL), lambda i, j: (j, 0, 0))] + [tok] * last,
        out_specs=[tok] * 3 + [wide] * 2 + [_const((8, 128))] * last,
        out_shape=[jax.ShapeDtypeStruct((s, D_MODEL), f32), jax.ShapeDtypeStruct((s, D_MODEL), f32),
                   jax.ShapeDtypeStruct((s, D_MODEL), bf16)] + [jax.ShapeDtypeStruct((nj, s, FF_SHARD), bf16)] * 2
                  + [jax.ShapeDtypeStruct((8, 128), f32)] * last,
        scratch_shapes=[], operands=(x, nw, sh, sc, g2, wgu, wgu, wd, *([target] if last else [])),
        relay_at=(s // ts - 1, 0))


def mlp_bwd(h, dy, gate, up, g2, wgu, wd, job=None):
    s = h.shape[0]
    ts = min(MLP_BWD_ROWS, s)
    nj = N_DEV // 2
    ni = s // ts

    rows_per = min(MLP_BWD_CHUNK, ts)

    def body(h_ref, dy_ref, gate_ref, up_ref, g2_ref, wg_ref, wu_ref, wd_ref, dh_ref, dwg_ref, dwu_ref, dwd_ref,
             ag_sc, au_sc, ad_sc, act_sc, dgate_sc, dup_sc, dmix_sc):
        i = pl.program_id(1)
        wg, wu, wd = wg_ref[...], wu_ref[...], wd_ref[...]
        g2 = g2_ref[...]
        for r in range(ts // rows_per):
            rows = pl.ds(r * rows_per, rows_per)
            act, vjp = jax.vjp(lambda g, u: jax.nn.silu(g) * u, gate_ref[rows, :].astype(f32), up_ref[rows, :].astype(f32))
            dmix = (dy_ref[rows, :] * g2).astype(bf16)
            dgate, dup = vjp(_dot(dmix, wd, 1, 1))
            dgate, dup = dgate.astype(bf16), dup.astype(bf16)
            dh_ref[rows, :] = (_dot(dgate, wg, 1, 0) + _dot(dup, wu, 1, 0)).astype(bf16)
            act_sc[rows, :] = act.astype(bf16)
            dgate_sc[rows, :] = dgate
            dup_sc[rows, :] = dup
            dmix_sc[rows, :] = dmix
        h = h_ref[...]
        grads = [_dot(dgate_sc[...], h, 0, 0), _dot(dup_sc[...], h, 0, 0), _dot(act_sc[...], dmix_sc[...], 0, 0)]
        _accumulate_then_cast(i == 0, i == ni - 1, [ag_sc, au_sc, ad_sc], [dwg_ref, dwu_ref, dwd_ref], grads)

    once = pl.Buffered(1)
    wspec = lambda off: pl.BlockSpec((None, FF_SHARD, D_MODEL), lambda j, i: (j + off, 0, 0), pipeline_mode=once)
    dspec = pl.BlockSpec((None, FF_SHARD, D_MODEL), lambda j, i: (j, 0, 0), pipeline_mode=once)
    wide = pl.BlockSpec((None, ts, FF_SHARD), lambda j, i: (j, i, 0))
    return _call_with_job(
        body, "mlp_bwd" if job is None else "mlp_bwd_comm", (nj, ni), job,
        in_specs=[pl.BlockSpec((ts, D_MODEL), lambda j, i: (i, 0)), pl.BlockSpec((ts, D_MODEL), lambda j, i: (i, 0)),
                  wide, wide, _const((1, D_MODEL)), wspec(0), wspec(nj), dspec],
        out_specs=[pl.BlockSpec((None, ts, D_MODEL), lambda j, i: (j, i, 0)), wspec(0), wspec(0), dspec],
        out_shape=[jax.ShapeDtypeStruct((nj, s, D_MODEL), bf16),
                   jax.ShapeDtypeStruct((nj, FF_SHARD, D_MODEL), bf16), jax.ShapeDtypeStruct((nj, FF_SHARD, D_MODEL), bf16),
                   jax.ShapeDtypeStruct((nj, FF_SHARD, D_MODEL), bf16)],
        scratch_shapes=[pltpu.VMEM((FF_SHARD, D_MODEL), f32), pltpu.VMEM((FF_SHARD, D_MODEL), f32),
                        pltpu.VMEM((FF_SHARD, D_MODEL), f32), pltpu.VMEM((ts, FF_SHARD), bf16),
                        pltpu.VMEM((ts, FF_SHARD), bf16), pltpu.VMEM((ts, FF_SHARD), bf16), pltpu.VMEM((ts, D_MODEL), bf16)],
        operands=(h, dy, gate, up, g2, wgu, wgu, wd))


def out_bwd(dy, dhparts, x, nw, sh, sc, mix, o, yg, g1, wo):
    s = dy.shape[0]
    ts = _token_block(s)
    nj = dhparts.shape[0]

    ni = s // ts

    def body(dy_ref, dp_ref, x_ref, nw_ref, sh_ref, sc_ref, mix_ref, o_ref, yg_ref, g1_ref, wo_ref,
             dx_ref, dnw_ref, dsh_ref, dsc_ref, do_ref, dyg_ref, dg1_ref, dg2_ref, dwo_ref, acc_sc):
        i = pl.program_id(0)
        g = dy_ref[...]
        _accumulate(i == 0, [dg2_ref], [jnp.sum(g * mix_ref[...], axis=0, keepdims=True)])
        dh = dp_ref[0].astype(f32)
        for j in range(1, nj):
            dh = dh + dp_ref[j].astype(f32)
        _, vjp_mod = jax.vjp(_f_modulate, x_ref[...], nw_ref[...], sh_ref[...], sc_ref[...])
        dx_mod, dnw, dsh, dsc = vjp_mod(dh)
        _accumulate(i == 0, [dnw_ref, dsh_ref, dsc_ref], [dnw, dsh, dsc])
        g = g + dx_mod
        dx_ref[...] = g
        o = o_ref[...]
        wo = wo_ref[...]
        _, vjp = jax.vjp(lambda o_, yg_, g1_, slot: _f_out(o_, yg_, g1_, wo, slot), o, yg_ref[...], g1_ref[...],
                         jnp.zeros(wo.shape, f32))
        do, dyg, dg1, dwo = vjp(g)
        delta = jnp.sum(do * o, axis=-1, keepdims=True)
        high = delta.astype(bf16)
        low = (delta - high.astype(f32)).astype(bf16)
        lane = lax.broadcasted_iota(jnp.int32, (1, 1, HEAD_LANES), 2)
        wide = jnp.concatenate([do.astype(bf16), jnp.zeros(do.shape, bf16)], axis=-1)
        do_ref[...] = jnp.where(lane == SPARE_V, -high, jnp.where(lane == SPARE_V + 1, -low, wide))
        dyg_ref[...] = dyg
        _accumulate(i == 0, [dg1_ref], [dg1])
        _accumulate_then_cast(i == 0, i == ni - 1, [acc_sc], [dwo_ref], [dwo])

    head = pl.BlockSpec((N_HEADS, ts, V_DIM), lambda i: (0, i, 0))
    tok = pl.BlockSpec((ts, D_MODEL), lambda i: (i, 0))
    vec = _const((1, D_MODEL))
    vshape = jax.ShapeDtypeStruct((1, D_MODEL), f32)
    return pl.pallas_call(
        body, name="out_bwd", grid=(ni,), scratch_shapes=[pltpu.VMEM((D_MODEL, D_MODEL), f32)],
        in_specs=[tok, pl.BlockSpec((nj, ts, D_MODEL), lambda i: (0, i, 0)), tok, vec, vec, vec, tok,
                  head, pl.BlockSpec((ts, D_SSD), lambda i: (i, 0)), vec, _const((D_MODEL, D_MODEL))],
        out_specs=[tok, vec, vec, vec, pl.BlockSpec((N_HEADS, ts, HEAD_LANES), lambda i: (0, i, 0)),
                   pl.BlockSpec((ts, D_SSD), lambda i: (i, 0)), vec, vec, _const((D_MODEL, D_MODEL))],
        out_shape=[jax.ShapeDtypeStruct((s, D_MODEL), f32), vshape, vshape, vshape,
                   jax.ShapeDtypeStruct((N_HEADS, s, HEAD_LANES), bf16), jax.ShapeDtypeStruct((s, D_SSD), f32),
                   vshape, vshape, jax.ShapeDtypeStruct((D_MODEL, D_MODEL), bf16)],
    )(dy, dhparts, x, nw, sh, sc, mix, o, yg, g1, wo)


def attn_bwd(qx, k, v, do, job=None):
    s = qx.shape[1]
    t = _token_block(s)
    nb = s // t

    hp = ATTN_HEADS_BWD

    def body(q_ref, k_ref, v_ref, do_ref, dq_ref, dk_ref, dv_ref, dv_sc):
        ki = pl.program_id(1)

        @pl.when(ki == 0)
        def _():
            dq_ref[...] = jnp.zeros(dq_ref.shape, f32)

        dk_ref[...] = jnp.zeros(dk_ref.shape, f32)
        dv_sc[...] = jnp.zeros(dv_sc.shape, f32)

        def step(q0, diagonal):
            half = t // 2
            subs = [(0, half, half), (half, half, t)] if diagonal and half % 128 == 0 else [(0, t, t)]
            chains = [(hh, sub) for hh in range(hp) for sub in subs]

            def products(hh, sub):
                r0, nr, nk = sub
                rows = pl.ds(q0 + r0, nr)
                sc = _scores(q_ref[hh, rows, :], k_ref[hh, :nk, :])
                dps = _scores(do_ref[hh, rows, :], v_ref[hh, :nk, :])
                return (jnp.where(_tril(nr, nk, r0), sc, NEG) if diagonal else sc), dps

            ahead = products(*chains[0])
            for c, (hh, (r0, nr, nk)) in enumerate(chains):
                sc, dps = ahead
                if c + 1 < len(chains):
                    ahead = products(*chains[c + 1])
                rows = pl.ds(q0 + r0, nr)
                p = jnp.exp2(sc)
                ds = (p * dps).astype(bf16)
                dv_sc[hh, :nk, :] += lax.dot_general(p.astype(bf16), do_ref[hh, rows, :], (((0,), (0,)), ((), ())),
                                                     preferred_element_type=f32)
                dk_ref[hh, :nk, :] += lax.dot_general(ds, q_ref[hh, rows, :], (((0,), (0,)), ((), ())),
                                                      preferred_element_type=f32)
                dq_ref[hh, rows, :] += jnp.dot(ds, k_ref[hh, :nk, :], preferred_element_type=f32)

        step(pl.multiple_of(ki * t, t), True)

        def above(qi, carry):
            step(pl.multiple_of(qi * t, t), False)
            return carry

        lax.fori_loop(ki + 1, nb, above, 0)
        real = lax.broadcasted_iota(jnp.int32, (1, 1, HEAD_LANES), 2) < SPARE_Q
        dk_ref[...] = jnp.where(real, dk_ref[...] * LN2, 0.0)
        dv_ref[...] = dv_sc[:, :, :V_DIM]

        @pl.when(ki == nb - 1)
        def _():
            dq_ref[...] = jnp.where(real, dq_ref[...] * LN2, 0.0)

    qspec = pl.BlockSpec((hp, s, HEAD_LANES), lambda h, ki: (h, 0, 0))
    kspec = lambda w: pl.BlockSpec((hp, t, w), lambda h, ki: (h, ki, 0))
    return _call_with_job(
        body, "attn_bwd" if job is None else "attn_bwd_comm", (N_HEADS // hp, nb), job,
        in_specs=[qspec, kspec(HEAD_LANES), kspec(HEAD_LANES), qspec],
        out_specs=[qspec, kspec(HEAD_LANES), kspec(V_DIM)],
        out_shape=[jax.ShapeDtypeStruct((N_HEADS, s, HEAD_LANES), f32), jax.ShapeDtypeStruct((N_HEADS, s, HEAD_LANES), f32),
                   jax.ShapeDtypeStruct((N_HEADS, s, V_DIM), f32)],
        scratch_shapes=[pltpu.VMEM((hp, t, HEAD_LANES), f32)], operands=(qx, k, v, do))


def ssd_bwd(px, pz, plast, states, dyg, params):
    s = px.shape[0]
    nc = s // CHUNK
    per = CHUNK // HALO

    def body(px_ref, halo_ref, pz_ref, pl_ref, st_ref, dyg_ref, cw_ref, cb_ref, dtb_ref, alog_ref, dskip_ref, snw_ref,
             dpx_ref, dpz_ref, dpl_ref, dcw_ref, dcb_ref, ddtb_ref, dalog_ref, ddskip_ref, dsnw_ref, dstate_sc, dhalo_sc):
        t = pl.program_id(0)
        chunk = nc - 1 - t

        @pl.when(t == 0)
        def _():
            dstate_sc[...] = jnp.zeros(dstate_sc.shape, f32)
            dhalo_sc[...] = jnp.zeros(dhalo_sc.shape, f32)

        halo = jnp.where(chunk > 0, halo_ref[...], 0.0)
        xext = jnp.concatenate([halo, px_ref[...]], axis=0)
        _, vjp = jax.vjp(_f_ssd, xext, pz_ref[...], pl_ref[...], st_ref[...], cw_ref[...], cb_ref[...], dtb_ref[...],
                         alog_ref[...], dskip_ref[...], snw_ref[...])
        dxext, dz, dpl, dprev, dcw, dcb, ddtb, dalog, ddskip, dsnw = vjp((dyg_ref[...], dstate_sc[...]))
        dpx_ref[...] = dxext[HALO:]
        dpx_ref[CHUNK - HALO:, :] += dhalo_sc[...]
        dhalo_sc[...] = dxext[:HALO]
        dstate_sc[...] = dprev
        dpz_ref[...] = dz
        dpl_ref[...] = dpl
        _accumulate(t == 0, [dcw_ref, dcb_ref, ddtb_ref, dalog_ref, ddskip_ref, dsnw_ref],
                    [dcw, dcb, ddtb, dalog, ddskip, dsnw])

    rev = lambda w: pl.BlockSpec((CHUNK, w), lambda t: (nc - 1 - t, 0))
    pshapes = [jax.ShapeDtypeStruct((4, D_CONV), f32), jax.ShapeDtypeStruct((1, D_CONV), f32),
               jax.ShapeDtypeStruct((1, 128), f32), jax.ShapeDtypeStruct((1, 128), f32),
               jax.ShapeDtypeStruct((1, 128), f32), jax.ShapeDtypeStruct((1, D_SSD), f32)]
    return pl.pallas_call(
        body, name="ssd_bwd", grid=(nc,),
        in_specs=[rev(D_CONV),
                  pl.BlockSpec((HALO, D_CONV), lambda t: (jnp.maximum((nc - 1 - t) * per - 1, 0), 0)),
                  rev(D_SSD), rev(128),
                  pl.BlockSpec((None, N_HEADS // 2, 2 * SSD_HEAD_DIM, SSD_STATE), lambda t: (nc - 1 - t, 0, 0, 0)),
                  rev(D_SSD)] + _ssd_param_specs(),
        out_specs=[rev(D_CONV), rev(D_SSD), rev(128)] + _ssd_param_specs(),
        out_shape=[jax.ShapeDtypeStruct((s, D_CONV), f32), jax.ShapeDtypeStruct((s, D_SSD), f32),
                   jax.ShapeDtypeStruct((s, 128), f32)] + pshapes,
        scratch_shapes=[pltpu.VMEM((N_HEADS // 2, 2 * SSD_HEAD_DIM, SSD_STATE), f32), pltpu.VMEM((HALO, D_CONV), f32)],
    )(px, px, pz, plast, states, dyg, *params)


def qkv_bwd(pa, plast, cos_t, sin_t, params, dq, dk, dv):
    s = pa.shape[0]
    ts = _token_block(s)

    def body(pa_ref, pl_ref, cos_ref, sin_ref, *rest):
        qaw, kvaw, wq, wk, wv, qnw, knw, kpw = [r[...] for r in rest[:8]]
        dq_ref, dk_ref, dv_ref = rest[8:11]
        dpa_ref, dpl_ref = rest[11:13]
        dprm_refs = list(rest[13:])
        cos_t, sin_t = cos_ref[...], sin_ref[...]

        def stage(pa_, pl_, qaw_, kvaw_, sq, sk, sv, qnw_, knw_, kpw_):
            return _f_qkv(pa_, pl_, cos_t, sin_t, qaw_, kvaw_, wq, wk, wv, qnw_, knw_, kpw_, (sq, sk, sv))

        _, vjp = jax.vjp(stage, pa_ref[...], pl_ref[...], qaw, kvaw, jnp.zeros(wq.shape, f32), jnp.zeros(wk.shape, f32),
                         jnp.zeros(wv.shape, f32), qnw, knw, kpw)
        grads = vjp((dq_ref[...], dk_ref[...], dv_ref[...]))
        dpa_ref[...] = grads[0]
        dpl_ref[...] = grads[1]
        _accumulate(pl.program_id(0) == 0, dprm_refs, list(grads[2:]))

    tok = lambda w: pl.BlockSpec((ts, w), lambda i: (i, 0))
    head = lambda w: pl.BlockSpec((N_HEADS, ts, w), lambda i: (0, i, 0))
    pshapes = [jax.ShapeDtypeStruct((1, Q_RANK), f32), jax.ShapeDtypeStruct((1, KV_RANK), f32),
               jax.ShapeDtypeStruct((N_HEADS, Q_RANK, HEAD_LANES), f32), jax.ShapeDtypeStruct((N_HEADS, KV_RANK, HEAD_LANES), f32),
               jax.ShapeDtypeStruct((N_HEADS, KV_RANK, V_DIM), f32), jax.ShapeDtypeStruct((1, HEAD_LANES), f32),
               jax.ShapeDtypeStruct((1, HEAD_LANES), f32), jax.ShapeDtypeStruct((1, HEAD_LANES), f32)]
    return pl.pallas_call(
        body, name="qkv_bwd", grid=(s // ts,),
        in_specs=[tok(384), tok(128), tok(128), tok(128)] + _qkv_param_specs()
                 + [head(HEAD_LANES), head(HEAD_LANES), head(V_DIM)],
        out_specs=[tok(384), tok(128)] + _qkv_param_specs(),
        out_shape=[jax.ShapeDtypeStruct((s, 384), f32), jax.ShapeDtypeStruct((s, 128), f32)] + pshapes,
    )(pa, plast, cos_t, sin_t, *params, dq, dk, dv)


def proj_bwd(x, nw, sh, sc, w, dpa, dpz, dpx, dpl_k, dpl_dt, dres):
    s = x.shape[0]
    ts = _token_block(s)

    ni = s // ts

    def body(x_ref, nw_ref, sh_ref, sc_ref, w_ref, dpa_ref, dpz_ref, dpx_ref, dplk_ref, dpld_ref, dres_ref,
             dx_ref, dnw_ref, dsh_ref, dsc_ref, dw_ref, acc_sc):
        i = pl.program_id(0)
        g = jnp.concatenate([dpa_ref[...], dpz_ref[...], dpx_ref[...], dplk_ref[...] + dpld_ref[...]], axis=1)
        w = w_ref[...]
        _, vjp = jax.vjp(lambda x_, nw_, sh_, sc_, slot: _f_proj(x_, nw_, sh_, sc_, w, slot), x_ref[...], nw_ref[...],
                         sh_ref[...], sc_ref[...], jnp.zeros(w.shape, f32))
        dx, dnw, dsh, dsc, dw = vjp(g)
        dx_ref[...] = dx + dres_ref[...]
        _accumulate(i == 0, [dnw_ref, dsh_ref, dsc_ref], [dnw, dsh, dsc])
        _accumulate_then_cast(i == 0, i == ni - 1, [acc_sc], [dw_ref], [dw])

    vec = _const((1, D_MODEL))
    vshape = jax.ShapeDtypeStruct((1, D_MODEL), f32)
    tok = lambda w_: pl.BlockSpec((ts, w_), lambda i: (i, 0))
    return pl.pallas_call(
        body, name="proj_bwd", grid=(ni,), scratch_shapes=[pltpu.VMEM((D_PROJ, D_MODEL), f32)],
        in_specs=[tok(D_MODEL), vec, vec, vec, _const((D_PROJ, D_MODEL)), tok(384), tok(512), tok(1024), tok(128), tok(128),
                  tok(D_MODEL)],
        out_specs=[tok(D_MODEL), vec, vec, vec, _const((D_PROJ, D_MODEL))],
        out_shape=[jax.ShapeDtypeStruct((s, D_MODEL), f32), vshape, vshape, vshape,
                   jax.ShapeDtypeStruct((D_PROJ, D_MODEL), bf16)],
    )(x, nw, sh, sc, w, dpa, dpz, dpx, dpl_k, dpl_dt, dres)


def ada_fwd(c_all, w_ada, b_cols):
    def body(c_ref, w_ref, b_ref, out_ref):
        act = jax.nn.silu(c_ref[...])
        for l in range(2):
            out_ref[l] = jnp.dot(act, w_ref[l], precision=lax.Precision.HIGHEST, preferred_element_type=f32) + b_ref[l]

    return pl.pallas_call(body, name="ada_fwd", out_shape=jax.ShapeDtypeStruct((2, N_DEV, 768), f32))(c_all, w_ada, b_cols)


def ada_bwd(c_all, dmod_cols):
    def body(c_ref, d_ref, out_ref):
        out_ref[0] = lax.dot_general(jax.nn.silu(c_ref[...]), d_ref[0], (((0,), (0,)), ((), ())),
                                     precision=lax.Precision.HIGHEST, preferred_element_type=f32)

    return pl.pallas_call(
        body, name="ada_bwd", grid=(2,),
        in_specs=[_const((N_DEV, D_MODEL)), pl.BlockSpec((1, N_DEV, 768), lambda l: (l, 0, 0))],
        out_specs=pl.BlockSpec((1, D_MODEL, 768), lambda l: (l, 0, 0)),
        out_shape=jax.ShapeDtypeStruct((2, D_MODEL, 768), f32),
    )(c_all, dmod_cols)


def _adamw(w, g, m, v):
    m = ADAM_B1 * m + (1.0 - ADAM_B1) * g
    v = ADAM_B2 * v + (1.0 - ADAM_B2) * (g * g)
    m_hat = m / (1.0 - ADAM_B1 ** ADAM_STEP)
    v_hat = v / (1.0 - ADAM_B2 ** ADAM_STEP)
    delta = -ADAM_LR * (m_hat / (jnp.sqrt(v_hat) + ADAM_EPS) + ADAM_WD * w)
    return delta, m, v


def adamw(parts, w, m, v, layer, prev, name):
    n, r, c = parts.shape
    nl = w.shape[0]
    per_elem = 2 * (n * parts.dtype.itemsize + 7 * 4)
    lanes = -(-c // 128) * 128
    tr, tc = r, c
    if per_elem * r * lanes > ADAMW_BLOCK_BYTES:
        fits = [t for t in range(r // 2, 15, -1) if r % t == 0 and t % 16 == 0 and per_elem * t * lanes <= ADAMW_BLOCK_BYTES]
        if fits:
            tr = fits[0]
        else:
            tc = next(t for t in (512, 256, 128) if c % t == 0)

    def body(p_ref, w_ref, m_ref, v_ref, *rest):
        g_ref, d_ref, nm_ref, nv_ref = rest[-4:]
        g = p_ref[0].astype(f32)
        for k in range(1, n):
            g = g + p_ref[k].astype(f32)
        delta, nm, nv = _adamw(w_ref[...], g, m_ref[...], v_ref[...])
        g_ref[...] = g
        d_ref[...] = delta
        nm_ref[...] = nm
        nv_ref[...] = nv

    blk = pl.BlockSpec((None, tr, tc), lambda i, j: (layer, i, j))
    shp = jax.ShapeDtypeStruct((nl, r, c), f32)
    kept = [] if prev is None else list(prev)
    return pl.pallas_call(
        body, name=name, grid=(r // tr, c // tc),
        in_specs=[pl.BlockSpec((n, tr, tc), lambda i, j: (0, i, j)), blk, blk, blk] + [ANY] * len(kept),
        out_specs=[blk] * 4, out_shape=[shp] * 4,
        input_output_aliases={4 + j: j for j in range(len(kept))},
    )(parts, w, m, v, *kept)


def adamw_two_layers(parts, w, m, v, name, job=None):
    n, r, c = parts[0].shape
    per_elem = 2 * (2 * n * parts[0].dtype.itemsize + 7 * 4)
    lanes = -(-c // 128) * 128
    tr = next(t for t in range(r, 15, -1) if r % t == 0 and t % 16 == 0 and per_elem * t * lanes <= ADAMW_BLOCK_BYTES)
    nblk = r // tr

    def body(p0_ref, p1_ref, w_ref, m_ref, v_ref, g_ref, d_ref, nm_ref, nv_ref):
        layer = pl.program_id(0)

        def update(p_ref):
            g = p_ref[0].astype(f32)
            for k in range(1, n):
                g = g + p_ref[k].astype(f32)
            delta, nm, nv = _adamw(w_ref[...], g, m_ref[...], v_ref[...])
            g_ref[...] = g
            d_ref[...] = delta
            nm_ref[...] = nm
            nv_ref[...] = nv

        @pl.when(layer == 0)
        def _():
            update(p0_ref)

        @pl.when(layer == 1)
        def _():
            update(p1_ref)

    blk = pl.BlockSpec((None, tr, c), lambda l, i: (l, i, 0))
    shp = jax.ShapeDtypeStruct((2, r, c), f32)
    return _call_with_job(
        body, name, (2, nblk), job,
        in_specs=[pl.BlockSpec((n, tr, c), lambda l, i: (0, i * (1 - l), 0)),
                  pl.BlockSpec((n, tr, c), lambda l, i: (0, i * l, 0)), blk, blk, blk],
        out_specs=[blk] * 4, out_shape=[shp] * 4, scratch_shapes=[], operands=(parts[0], parts[1], w, m, v),
        relay_at=(1, 0))


def adamw_layers_inside(parts, w, m, v, name):
    n, r, c = parts[0].shape
    nl = w.shape[1]
    tc = next(t for t in (256, 128) if c % t == 0)

    def body(*refs):
        p_refs = refs[:nl]
        w_ref, m_ref, v_ref, g_ref, d_ref, nm_ref, nv_ref = refs[nl:]
        for l in range(nl):
            g = p_refs[l][0].astype(f32)
            for k in range(1, n):
                g = g + p_refs[l][k].astype(f32)
            delta, nm, nv = _adamw(w_ref[:, l, :], g, m_ref[:, l, :], v_ref[:, l, :])
            g_ref[:, l, :] = g
            d_ref[:, l, :] = delta
            nm_ref[:, l, :] = nm
            nv_ref[:, l, :] = nv

    blk = pl.BlockSpec((r, nl, tc), lambda j: (0, 0, j))
    shp = jax.ShapeDtypeStruct((r, nl, c), f32)
    return pl.pallas_call(
        body, name=name, grid=(c // tc,),
        in_specs=[pl.BlockSpec((n, r, tc), lambda j: (0, 0, j))] * nl + [blk] * 3,
        out_specs=[blk] * 4, out_shape=[shp] * 4,
    )(*parts, w, m, v)


def _my_index():
    return 4 * lax.axis_index("x") + 2 * lax.axis_index("y") + lax.axis_index("c")


def _coords(idx):
    return (idx // 4, (idx // 2) % 2, idx % 2)


class CommJob:
    def __init__(self, operands, out_shape, phases, scratch):
        self.operands, self.out_shape, self.phases, self.scratch = operands, out_shape, phases, scratch


def _wait(out, n_blocks, send_sem, recv_sem, send=True, recv=True):
    span = out.at[pl.ds(0, n_blocks)]
    desc = pltpu.make_async_remote_copy(src_ref=span, dst_ref=span, send_sem=send_sem, recv_sem=recv_sem,
                                        device_id=_coords(_my_index()), device_id_type=MESH)
    if recv:
        desc.wait_recv()
    if send:
        desc.wait_send()


def gather_job(shards):
    n = len(shards)

    def places():
        x, y, c = lax.axis_index("x"), lax.axis_index("y"), lax.axis_index("c")
        return (x, y, c), (x, y, 1 - c), [(1 - x, y), (x, 1 - y), (1 - x, 1 - y)]

    def index(p):
        return 4 * p[0] + 2 * p[1] + p[2]

    def start(ins, outs, sems):
        far_send, far_recv, near_send, near_recv, local = sems
        me, sibling, chips = places()
        for k in range(n):
            pltpu.make_async_copy(ins[k], outs[k].at[index(me)], local.at[k]).start()
            for chip in chips:
                pltpu.make_async_remote_copy(src_ref=ins[k], dst_ref=outs[k].at[index(me)], send_sem=far_send.at[k],
                                             recv_sem=far_recv.at[k], device_id=(*chip, me[2]), device_id_type=MESH).start()
            pltpu.make_async_remote_copy(src_ref=ins[k], dst_ref=outs[k].at[index(me)], send_sem=near_send.at[k],
                                         recv_sem=near_recv.at[k], device_id=sibling, device_id_type=MESH).start()

    def relay(ins, outs, sems):
        far_send, far_recv, near_send, near_recv, local = sems
        me, sibling, chips = places()
        for k in range(n):
            _wait(outs[k], 3, far_send.at[k], far_recv.at[k], send=False)
            for chip in chips:
                block = outs[k].at[index((*chip, me[2]))]
                pltpu.make_async_remote_copy(src_ref=block, dst_ref=block, send_sem=near_send.at[k],
                                             recv_sem=near_recv.at[k], device_id=sibling, device_id_type=MESH).start()

    def finish(ins, outs, sems):
        far_send, far_recv, near_send, near_recv, local = sems
        for k in range(n):
            _wait(outs[k], 4, near_send.at[k], near_recv.at[k])
            _wait(outs[k], 3, far_send.at[k], far_recv.at[k], recv=False)
            pltpu.make_async_copy(ins[k], outs[k].at[0], local.at[k]).wait()

    shapes = [jax.ShapeDtypeStruct((N_DEV,) + tuple(a.shape), a.dtype) for a in shards]
    return CommJob(list(shards), shapes, [start, relay, finish], [pltpu.SemaphoreType.DMA((n,))] * 5)


def scatter_job(tensors):
    n = len(tensors)
    flat, where = [], {}
    for k, pieces in enumerate(tensors):
        d = 0
        for piece in pieces:
            for b in range(piece.shape[0]):
                where[k, d] = (len(flat), b)
                d += 1
            flat.append(piece)
        assert d == N_DEV

    def start(ins, outs, sems):
        send_sems, recv_sems, local_sems = sems
        me = _my_index()

        def block(k, d):
            i, b = where[k, d]
            return ins[i].at[b]

        for d in range(N_DEV):
            @pl.when(d != me)
            def _():
                for k in range(n):
                    pltpu.make_async_remote_copy(src_ref=block(k, d), dst_ref=outs[k].at[me], send_sem=send_sems.at[k],
                                                 recv_sem=recv_sems.at[k], device_id=(d // 4, (d // 2) % 2, d % 2),
                                                 device_id_type=MESH).start()

            @pl.when(d == me)
            def _():
                for k in range(n):
                    pltpu.make_async_copy(block(k, d), outs[k].at[d], local_sems.at[k]).start()

    def finish(ins, outs, sems):
        send_sems, recv_sems, local_sems = sems
        for k in range(n):
            _wait(outs[k], N_DEV - 1, send_sems.at[k], recv_sems.at[k])
            i, b = where[k, 0]
            pltpu.make_async_copy(ins[i].at[b], outs[k].at[0], local_sems.at[k]).wait()

    shapes = [jax.ShapeDtypeStruct((N_DEV,) + tuple(p[0].shape[1:]), p[0].dtype) for p in tensors]
    return CommJob(flat, shapes, [start, finish], [pltpu.SemaphoreType.DMA((n,))] * 3)


def merge_jobs(a, b):
    def on(job, off):
        oi, oo, os_ = off
        ni, no, ns = len(job.operands), len(job.out_shape), len(job.scratch)
        return lambda phase: (lambda ins, outs, sems: phase(ins[oi:oi + ni], outs[oo:oo + no], sems[os_:os_ + ns]))

    wrap_a = on(a, (0, 0, 0))
    wrap_b = on(b, (len(a.operands), len(a.out_shape), len(a.scratch)))
    pa, pb = [wrap_a(p) for p in a.phases], [wrap_b(p) for p in b.phases]

    def together(*phases):
        def run(ins, outs, sems):
            for p in phases:
                p(ins, outs, sems)
        return run

    middle = pa[1:-1] + pb[1:-1]
    phases = [together(pa[0], pb[0])] + middle + [together(pa[-1], pb[-1])]
    return CommJob(a.operands + b.operands, a.out_shape + b.out_shape, phases, a.scratch + b.scratch)


def comm_call(job, name):
    ni, no = len(job.operands), len(job.out_shape)

    def body(*refs):
        ins, outs, sems = refs[:ni], refs[ni:ni + no], refs[ni + no:]
        for phase in job.phases:
            phase(ins, outs, sems)

    return pl.pallas_call(body, name=name, in_specs=[ANY] * ni, out_specs=[ANY] * no, out_shape=job.out_shape,
                          scratch_shapes=job.scratch)(*job.operands)


def _carry(job, body, n_in, n_out, at_step):
    ji, jo, js = len(job.operands), len(job.out_shape), len(job.scratch)

    def carrier(*refs):
        a, b = n_in, n_in + ji
        c, d = b + n_out, b + n_out + jo
        e = len(refs) - js
        job_refs = (refs[a:b], refs[c:d], refs[e:])
        n = len(job.phases)

        @pl.when(at_step(0, n))
        def _():
            job.phases[0](*job_refs)

        body(*refs[:a], *refs[b:c], *refs[d:e])

        for i in range(1, n):
            @pl.when(at_step(i, n))
            def _():
                job.phases[i](*job_refs)

    return carrier


def _pad_lanes(v, lo, total=128):
    return jnp.pad(v, (lo, total - lo - v.shape[0]))[None, :]


MIXER_WEIGHTS = ("w_in", "w_q_up", "w_kv_up", "conv_w")
LATE_WEIGHTS = ("w_out", "w_gate_up", "w_down")


def mixer_operands(g, sw):
    w_in = g["w_in"].reshape(D_IN, D_MODEL)
    zero = lambda rows: jnp.zeros((rows, D_MODEL), w_in.dtype)
    w_proj = jnp.concatenate(
        [w_in[:384], w_in[416:928], w_in[928:1952], w_in[1952:1960], zero(56), w_in[384:416], zero(32)], axis=0)
    wq = jnp.pad(g["w_q_up"], ((0, 0), (0, 0), (0, HEAD_LANES - NOPE - ROPE)))
    wk = jnp.pad(g["w_kv_up"][:, :, :NOPE], ((0, 0), (0, 0), (0, HEAD_LANES - NOPE)))
    wv = g["w_kv_up"][:, :, NOPE:]
    qkv = (sw["q_a_norm_w"][None, :], sw["kv_a_norm_w"][None, :], wq, wk, wv,
           _pad_lanes(jnp.concatenate([sw["q_nope_norm_w"], sw["q_pe_norm_w"]]), 0),
           _pad_lanes(sw["k_nope_norm_w"], 0), _pad_lanes(sw["k_pe_norm_w"], NOPE))
    conv_w = g["conv_w"].astype(f32).transpose(1, 0, 2).reshape(4, D_CONV)
    ssd = (conv_w, sw["conv_b"][None, :], _pad_lanes(sw["dt_bias"], 0), _pad_lanes(sw["a_log"], 0),
           _pad_lanes(sw["d_skip"], 0), sw["ssd_norm_w"][None, :])
    return dict(w_proj=w_proj, qkv=qkv, ssd=ssd, n1=sw["norm1_w"][None, :])


def late_operands(g, sw):
    return dict(wo=g["w_out"].reshape(D_MODEL, D_MODEL), wgu=g["w_gate_up"],
                wd=g["w_down"].reshape(N_DEV // 2, FF_SHARD, D_MODEL), n2=sw["norm2_w"][None, :])


def layer_fwd(x, mod, kw, cos_t, sin_t, job=None, late=None, target=None, mixer_job=None):
    sh1, sc1, g1, sh2, sc2, g2 = [mod[i:i + 1] for i in range(6)]
    pa, pz, px, plast = proj_fwd(x, kw["n1"], sh1, sc1, kw["w_proj"])
    q, k, v = qkv_fwd(pa, plast, cos_t, sin_t, kw["qkv"])
    (o, qx), carried = attn_fwd(q, k, v, job)
    if late is not None:
        kw = {**kw, **late(carried)}
    yg, states = ssd_fwd(px, pz, plast, kw["ssd"])
    x_mid = out_fwd(x, o, yg, g1, kw["wo"])
    (x_out, mix, h_mid, gate, up, *loss_part), carried_mixer = mlp_fwd(
        x_mid, kw["n2"], sh2, sc2, g2, kw["wgu"], kw["wd"], target, mixer_job)
    if mixer_job is not None:
        carried = (carried, carried_mixer)
    saved = dict(x=x, pa=pa, pz=pz, px=px, plast=plast, qx=qx, k=k, v=v, o=o, yg=yg, states=states, x_mid=x_mid,
                 mix=mix, h_mid=h_mid, gate=gate, up=up)
    return (x_out if target is None else (x_out, loss_part[0])), saved, kw, carried


def layer_bwd_head(dy, mod, kw, sv, job=None):
    _, _, g1, sh2, sc2, g2 = [mod[i:i + 1] for i in range(6)]
    (dhparts, dwg, dwu, dwd), carried = mlp_bwd(sv["h_mid"], dy, sv["gate"], sv["up"], g2, kw["wgu"], kw["wd"], job)
    dmid, dn2, dsh2, dsc2, do, dyg, dg1, dg2, dwo = out_bwd(
        dy, dhparts, sv["x_mid"], kw["n2"], sh2, sc2, sv["mix"], sv["o"], sv["yg"], g1, kw["wo"])
    early = dict(w_out=[dwo.reshape(N_DEV, D_MODEL // N_DEV, D_MODEL)], w_gate_up=[dwg, dwu],
                 w_down=[dwd.reshape(N_DEV, D_FF // N_DEV, D_MODEL)])
    head = dict(dmid=dmid, do=do, dyg=dyg, dn2=dn2, dsh2=dsh2, dsc2=dsc2, dg2=dg2, dg1=dg1)
    return head, early, carried


def layer_bwd_tail(hd, mod, kw, cos_t, sin_t, sv, job=None):
    sh1, sc1 = mod[0:1], mod[1:2]
    (dq, dk, dv), carried = attn_bwd(sv["qx"], sv["k"], sv["v"], hd["do"], job)
    dpx, dpz, dpl_dt, dcw, dcb, ddtb, dalog, ddskip, dsnw = ssd_bwd(sv["px"], sv["pz"], sv["plast"], sv["states"],
                                                                   hd["dyg"], kw["ssd"])
    dpa, dpl_k, dqaw, dkvaw, dwq, dwk, dwv, dqnw, dknw, dkpw = qkv_bwd(sv["pa"], sv["plast"], cos_t, sin_t, kw["qkv"],
                                                                       dq, dk, dv)
    dx, dn1, dsh1, dsc1, dwp = proj_bwd(sv["x"], kw["n1"], sh1, sc1, kw["w_proj"], dpa, dpz, dpx, dpl_k, dpl_dt, hd["dmid"])
    dmod = jnp.concatenate([dsh1, dsc1, hd["dg1"], hd["dsh2"], hd["dsc2"], hd["dg2"]], axis=0)
    dw_in = jnp.concatenate([dwp[:384], dwp[1984:2016], dwp[384:1920], dwp[1920:1928]], axis=0)
    grads = dict(
        norm1_w=dn1[0], norm2_w=hd["dn2"][0], q_a_norm_w=dqaw[0], kv_a_norm_w=dkvaw[0],
        q_nope_norm_w=dqnw[0, :NOPE], q_pe_norm_w=dqnw[0, NOPE:NOPE + ROPE], k_nope_norm_w=dknw[0, :NOPE],
        k_pe_norm_w=dkpw[0, NOPE:NOPE + ROPE], conv_b=dcb[0], dt_bias=ddtb[0, :N_HEADS], a_log=dalog[0, :N_HEADS],
        d_skip=ddskip[0, :N_HEADS], ssd_norm_w=dsnw[0],
        w_in=[dw_in.reshape(N_DEV, D_IN // N_DEV, D_MODEL)],
        w_q_up=[dwq[:, :, :NOPE + ROPE].astype(bf16)],
        w_kv_up=[jnp.concatenate([dwk[:, :, :NOPE], dwv], axis=2).astype(bf16)],
        conv_w=[dcw.reshape(4, N_DEV, D_CONV // N_DEV).transpose(1, 0, 2).astype(bf16)],
    )
    return dx, dmod, grads, carried


def _pack_small(get, last=None):
    flat = jnp.concatenate([get(name).reshape(-1) for name, _ in SMALL])
    flat = jnp.pad(flat, (0, SMALL_ROWS * 128 - flat.shape[0]))
    if last is not None:
        flat = flat.at[-1].set(last)
    return flat.reshape(SMALL_ROWS, 128)


def _unpack_small(packed):
    flat = packed.reshape(-1)
    out, off = {}, 0
    for name, size in SMALL:
        out[name] = flat[off:off + 2 * size].reshape(2, size)
        off += 2 * size
    return out


def kernel(x, c, positions, norm1_w, norm2_w, w_ada, b_ada, w_in, q_a_norm_w, w_q_up, kv_a_norm_w, w_kv_up, q_nope_norm_w, q_pe_norm_w, k_nope_norm_w, k_pe_norm_w, conv_w, conv_b, dt_bias, a_log, d_skip, ssd_norm_w, w_out, w_gate_up, w_down, loss_target, m_norm1_w, m_norm2_w, m_w_ada, m_b_ada, m_w_in, m_q_a_norm_w, m_w_q_up, m_kv_a_norm_w, m_w_kv_up, m_q_nope_norm_w, m_q_pe_norm_w, m_k_nope_norm_w, m_k_pe_norm_w, m_conv_w, m_conv_b, m_dt_bias, m_a_log, m_d_skip, m_ssd_norm_w, m_w_out, m_w_gate_up, m_w_down, v_norm1_w, v_norm2_w, v_w_ada, v_b_ada, v_w_in, v_q_a_norm_w, v_w_q_up, v_kv_a_norm_w, v_w_kv_up, v_q_nope_norm_w, v_q_pe_norm_w, v_k_nope_norm_w, v_k_pe_norm_w, v_conv_w, v_conv_b, v_dt_bias, v_a_log, v_d_skip, v_ssd_norm_w, v_w_out, v_w_gate_up, v_w_down):
    w = dict(norm1_w=norm1_w, norm2_w=norm2_w, w_ada=w_ada, b_ada=b_ada, w_in=w_in, q_a_norm_w=q_a_norm_w, w_q_up=w_q_up,
             kv_a_norm_w=kv_a_norm_w, w_kv_up=w_kv_up, q_nope_norm_w=q_nope_norm_w, q_pe_norm_w=q_pe_norm_w,
             k_nope_norm_w=k_nope_norm_w, k_pe_norm_w=k_pe_norm_w, conv_w=conv_w, conv_b=conv_b, dt_bias=dt_bias,
             a_log=a_log, d_skip=d_skip, ssd_norm_w=ssd_norm_w, w_out=w_out, w_gate_up=w_gate_up, w_down=w_down)
    m = dict(norm1_w=m_norm1_w, norm2_w=m_norm2_w, w_ada=m_w_ada, b_ada=m_b_ada, w_in=m_w_in, q_a_norm_w=m_q_a_norm_w,
             w_q_up=m_w_q_up, kv_a_norm_w=m_kv_a_norm_w, w_kv_up=m_w_kv_up, q_nope_norm_w=m_q_nope_norm_w,
             q_pe_norm_w=m_q_pe_norm_w, k_nope_norm_w=m_k_nope_norm_w, k_pe_norm_w=m_k_pe_norm_w, conv_w=m_conv_w,
             conv_b=m_conv_b, dt_bias=m_dt_bias, a_log=m_a_log, d_skip=m_d_skip, ssd_norm_w=m_ssd_norm_w, w_out=m_w_out,
             w_gate_up=m_w_gate_up, w_down=m_w_down)
    v = dict(norm1_w=v_norm1_w, norm2_w=v_norm2_w, w_ada=v_w_ada, b_ada=v_b_ada, w_in=v_w_in, q_a_norm_w=v_q_a_norm_w,
             w_q_up=v_w_q_up, kv_a_norm_w=v_kv_a_norm_w, w_kv_up=v_w_kv_up, q_nope_norm_w=v_q_nope_norm_w,
             q_pe_norm_w=v_q_pe_norm_w, k_nope_norm_w=v_k_nope_norm_w, k_pe_norm_w=v_k_pe_norm_w, conv_w=v_conv_w,
             conv_b=v_conv_b, dt_bias=v_dt_bias, a_log=v_a_log, d_skip=v_d_skip, ssd_norm_w=v_ssd_norm_w, w_out=v_w_out,
             w_gate_up=v_w_gate_up, w_down=v_w_down)
    me = _my_index()
    seq = x.shape[1]

    def shard(name, l):
        if name == "conv_w":
            return w[name][l]
        if name in TRANSPOSED:
            return jnp.swapaxes(w[name][l], 0, 1).astype(bf16)
        return w[name][l].astype(bf16)

    def shards(names, l):
        return [shard(name, l) for name in names]

    small = [{name: w[name][l] for name, _ in SMALL if name != "b_ada"} for l in range(2)]

    inv_freq = 1.0 / (ROPE_THETA ** (jnp.arange(0, ROPE, 2, dtype=f32) / ROPE))
    inv = _pad_lanes(jnp.concatenate([inv_freq, inv_freq]), NOPE)
    (cos_t, sin_t), first = rope_tables(positions.reshape(seq, 1), inv, gather_job([c] + shards(MIXER_WEIGHTS, 0)))
    c_all = first[0].reshape(N_DEV, D_MODEL)
    kws = [mixer_operands(dict(zip(MIXER_WEIGHTS, first[1:])), small[0]), None]

    b_cols = lax.dynamic_slice_in_dim(b_ada, me * 768, 768, axis=1)
    mod_cols = ada_fwd(c_all, w_ada, b_cols)
    (mod_all,) = comm_call(gather_job([mod_cols]), "gather_mod")
    mod_me = lax.dynamic_index_in_dim(mod_all, me, axis=2, keepdims=False)
    mods = [mod_me[:, l, :].reshape(6, D_MODEL) for l in range(2)]

    saved = [None, None]
    h, saved[0], kws[0], (_, got) = layer_fwd(
        x[0], mods[0], kws[0], cos_t, sin_t, gather_job(shards(LATE_WEIGHTS, 0)),
        lambda got: late_operands(dict(zip(LATE_WEIGHTS, got)), small[0]),
        mixer_job=gather_job(shards(MIXER_WEIGHTS, 1)))
    kws[1] = mixer_operands(dict(zip(MIXER_WEIGHTS, got)), small[1])
    (dy, loss_part), saved[1], kws[1], _ = layer_fwd(
        h, mods[1], kws[1], cos_t, sin_t, gather_job(shards(LATE_WEIGHTS, 1)),
        lambda got: late_operands(dict(zip(LATE_WEIGHTS, got)), small[1]), loss_target[0])

    early, late = ("w_out", "w_gate_up", "w_down"), ("w_in", "w_q_up", "w_kv_up", "conv_w")
    parts = [{}, {}]
    head, pieces, _ = layer_bwd_head(dy, mods[1], kws[1], saved[1])
    dy, dmod1, grads1, got = layer_bwd_tail(head, mods[1], kws[1], cos_t, sin_t, saved[1], scatter_job([pieces[n] for n in early]))
    parts[1].update(zip(early, got))
    head, pieces, got = layer_bwd_head(dy, mods[0], kws[0], saved[0], scatter_job([grads1[n] for n in late]))
    parts[1].update(zip(late, got))
    dy, dmod0, grads0, got = layer_bwd_tail(head, mods[0], kws[0], cos_t, sin_t, saved[0], scatter_job([pieces[n] for n in early]))
    parts[0].update(zip(early, got))
    grad_x = dy[None]

    small_part = {name: jnp.stack([grads0[name], grads1[name]]) for name, _ in SMALL if name != "b_ada"}
    small_part["b_ada"] = jnp.stack([dmod0.reshape(-1), dmod1.reshape(-1)])
    swap = lambda a: jnp.swapaxes(a, 1, 2)
    updated, last = adamw_two_layers(
        [parts[l]["w_gate_up"] for l in range(2)], swap(w_gate_up), swap(m_w_gate_up), swap(v_w_gate_up), "adamw_w_gate_up_comm",
        merge_jobs(scatter_job([grads0[n] for n in late]),
                   gather_job([_pack_small(lambda n: small_part[n], loss_part[0, 0])])))
    parts[0].update(zip(late, last[:len(late)]))
    small_all = last[len(late)]
    packed = adamw(small_all, _pack_small(lambda n: w[n])[None], _pack_small(lambda n: m[n])[None],
                   _pack_small(lambda n: v[n])[None], 0, None, "adamw_small")
    loss = packed[0][0, -1, -1]
    res = {}
    for key, arr in zip("gdmv", packed):
        for name, val in _unpack_small(arr[0]).items():
            res[key, name] = val

    off = 2 * (1024 + 1024)
    dmod_all = small_all.reshape(N_DEV, -1)[:, off:off + 2 * 6144].reshape(N_DEV, 2, 6144)
    dmod_cols = lax.dynamic_slice_in_dim(dmod_all, me * 768, 768, axis=2).transpose(1, 0, 2)
    g_ada = ada_bwd(c_all, dmod_cols)
    out = None
    for l in range(2):
        out = adamw(g_ada[l][None], w_ada, m_w_ada, v_w_ada, l, out, "adamw_w_ada")
    res.update(zip([(key, "w_ada") for key in "gdmv"], out))

    inside = lambda a: jnp.transpose(a, (2, 0, 1))
    out = adamw_layers_inside([parts[l]["w_in"] for l in range(2)], inside(w_in), inside(m_w_in), inside(v_w_in), "adamw_w_in")
    res.update(zip([(key, "w_in") for key in "gdmv"], [jnp.transpose(a, (1, 2, 0)) for a in out]))
    res.update(zip([(key, "w_gate_up") for key in "gdmv"], [swap(a) for a in updated]))
    for name in BIG:
        if name in ("w_in", "w_gate_up"):
            continue
        view = (lambda a: jnp.swapaxes(a, 1, 2)) if name in TRANSPOSED else (lambda a: a)
        out = None
        for l in range(2):
            out = adamw(parts[l][name], view(w[name]), view(m[name]), view(v[name]), l, out, "adamw_" + name)
        res.update(zip([(key, name) for key in "gdmv"], [view(a) for a in out]))

    return (loss, grad_x, *[res["g", n] for n in WEIGHTS], *[res["d", n] for n in WEIGHTS],
            *[res["m", n] for n in WEIGHTS], *[res["v", n] for n in WEIGHTS])
```

```python
import functools

import jax
import jax.numpy as jnp
from jax import lax
from jax.experimental import pallas as pl
from jax.experimental.pallas import tpu as pltpu

f32 = jnp.float32
bf16 = jnp.bfloat16

N_DEV = 8
D_MODEL = 1024
N_HEADS = 8
HEAD_LANES = 128
NOPE = 64
ROPE = 32
V_DIM = 64
Q_RANK = 256
KV_RANK = 128
D_SSD = 512
D_CONV = 1024
SSD_STATE = 128
SSD_HEAD_DIM = 64
CHUNK = 128
HALO = 8
D_FF = 2816
FF_SHARD = 704
D_IN = 1960
D_PROJ = 2048
EPS = 1e-6
LOG2E = 1.4426950408889634
LN2 = 0.6931471805599453
Q_SCALE = (NOPE + ROPE) ** -0.5 * LOG2E
SPARE_Q = NOPE + ROPE
SPARE_V = V_DIM
ATTN_BLOCK_FWD = 2048
ATTN_ROWS_FWD = 256
ATTN_HEADS_FWD = 2
ATTN_HEADS_BWD = 4
MLP_FWD_ROWS = 1024
MLP_BWD_CHUNK = 256
MLP_BWD_ROWS = 1024
ROPE_THETA = 10000.0
NEG = -1e30

ADAM_LR = 0.001
ADAM_B1 = 0.9
ADAM_B2 = 0.999
ADAM_EPS = 1e-08
ADAM_WD = 0.01
ADAM_STEP = 10
ADAMW_BLOCK_BYTES = 36 << 20

MESH = pl.DeviceIdType.MESH
ANY = pl.BlockSpec(memory_space=pl.ANY)

SMALL = (("norm1_w", 1024), ("norm2_w", 1024), ("b_ada", 6144), ("q_a_norm_w", 256), ("kv_a_norm_w", 128),
         ("q_nope_norm_w", 64), ("q_pe_norm_w", 32), ("k_nope_norm_w", 64), ("k_pe_norm_w", 32),
         ("conv_b", 1024), ("dt_bias", 8), ("a_log", 8), ("d_skip", 8), ("ssd_norm_w", 512))
SMALL_ROWS = 168
BIG = ("w_in", "w_q_up", "w_kv_up", "conv_w", "w_out", "w_gate_up", "w_down")
TRANSPOSED = ("w_in", "w_gate_up")
WEIGHTS = ("norm1_w", "norm2_w", "w_ada", "b_ada", "w_in", "q_a_norm_w", "w_q_up", "kv_a_norm_w", "w_kv_up",
           "q_nope_norm_w", "q_pe_norm_w", "k_nope_norm_w", "k_pe_norm_w", "conv_w", "conv_b", "dt_bias",
           "a_log", "d_skip", "ssd_norm_w", "w_out", "w_gate_up", "w_down")


def _dot(a, b, ca, cb):
    return lax.dot_general(a.astype(bf16), b.astype(bf16), (((ca,), (cb,)), ((), ())), preferred_element_type=f32)


@jax.custom_vjp
def mm(a, b):
    return _dot(a, b, 1, 0)


def _mm_fwd(a, b):
    return _dot(a, b, 1, 0), (a, b)


def _mm_bwd(res, g):
    a, b = res
    return _dot(g, b, 1, 1).astype(a.dtype), _dot(a, g, 0, 0).astype(b.dtype)


mm.defvjp(_mm_fwd, _mm_bwd)


@jax.custom_vjp
def _mm_slot(a, w, slot):
    return _dot(a, w, 1, 0)


def _mm_slot_fwd(a, w, slot):
    return _dot(a, w, 1, 0), (a, w)


def _mm_slot_bwd(res, g):
    a, w = res
    return _dot(g, w, 1, 1).astype(a.dtype), None, _dot(a, g, 0, 0)


_mm_slot.defvjp(_mm_slot_fwd, _mm_slot_bwd)


def mmw(a, w, slot=None):
    return _dot(a, w, 1, 0) if slot is None else _mm_slot(a, w, slot)


@jax.custom_vjp
def _mm_slot_t(a, wt, slot):
    return _dot(a, wt, 1, 1)


def _mm_slot_t_fwd(a, wt, slot):
    return _dot(a, wt, 1, 1), (a, wt)


def _mm_slot_t_bwd(res, g):
    a, wt = res
    return _dot(g, wt, 1, 0).astype(a.dtype), None, _dot(g, a, 0, 0)


_mm_slot_t.defvjp(_mm_slot_t_fwd, _mm_slot_t_bwd)


def mmw_t(a, wt, slot=None):
    return _dot(a, wt, 1, 1) if slot is None else _mm_slot_t(a, wt, slot)


@jax.custom_vjp
def mm_nt(a, b):
    return _dot(a, b, 1, 1)


def _mm_nt_fwd(a, b):
    return _dot(a, b, 1, 1), (a, b)


def _mm_nt_bwd(res, g):
    a, b = res
    return _dot(g, b, 1, 0).astype(a.dtype), _dot(g, a, 0, 0).astype(b.dtype)


mm_nt.defvjp(_mm_nt_fwd, _mm_nt_bwd)


@jax.custom_vjp
def mm_tn(a, b):
    return _dot(a, b, 0, 0)


def _mm_tn_fwd(a, b):
    return _dot(a, b, 0, 0), (a, b)


def _mm_tn_bwd(res, g):
    a, b = res
    return _dot(b, g, 1, 1).astype(a.dtype), _dot(a, g, 1, 0).astype(b.dtype)


mm_tn.defvjp(_mm_tn_fwd, _mm_tn_bwd)


def _rms(x, w):
    return x * lax.rsqrt(jnp.mean(x * x, axis=-1, keepdims=True) + EPS) * w


def _const(shape):
    n = len(shape)
    return pl.BlockSpec(shape, lambda *_: (0,) * n)


def _accumulate(first, refs, vals):
    @pl.when(first)
    def _():
        for r, v in zip(refs, vals):
            r[...] = v

    @pl.when(jnp.logical_not(first))
    def _():
        for r, v in zip(refs, vals):
            r[...] += v


def _accumulate_then_cast(first, last, accs, outs, vals):
    _accumulate(first, accs, vals)

    @pl.when(last)
    def _():
        for a, o in zip(accs, outs):
            o[...] = a[...].astype(o.dtype)


def _token_block(s):
    return min(512, s)


def _f_proj(x, nw, sh, sc, w, slot=None):
    h = _rms(x, nw) * (1.0 + sc) + sh
    return mmw_t(h, w, slot)


def _f_qkv(pa, plast, cos_t, sin_t, qaw, kvaw, wq, wk, wv, qnw, knw, kpw, slots=None):
    sq, sk, sv = slots if slots is not None else ([None] * N_HEADS,) * 3
    lane = lax.broadcasted_iota(jnp.int32, (1, HEAD_LANES), 1)
    m_nope = lane < NOPE
    m_pe = (lane >= NOPE) & (lane < NOPE + ROPE)
    rows = pa.shape[0]

    def rope(t):
        half = ROPE // 2
        swapped = jnp.concatenate(
            [jnp.zeros((rows, NOPE), f32), t[:, NOPE + half:NOPE + ROPE], t[:, NOPE:NOPE + half],
             jnp.zeros((rows, HEAD_LANES - NOPE - ROPE), f32)], axis=1)
        return t * cos_t + swapped * sin_t

    qa = _rms(pa[:, :Q_RANK], qaw)
    kva = _rms(pa[:, Q_RANK:Q_RANK + KV_RANK], kvaw)
    kp = jnp.where(m_pe, plast, 0.0)
    kp = kp * lax.rsqrt(jnp.sum(kp * kp, axis=-1, keepdims=True) / ROPE + EPS) * kpw
    k_rot = rope(kp)
    qs, ks, vs = [], [], []
    for h in range(N_HEADS):
        qh = mmw(qa, wq[h], sq[h])
        ss_n = jnp.sum(jnp.where(m_nope, qh * qh, 0.0), axis=-1, keepdims=True) / NOPE
        ss_p = jnp.sum(jnp.where(m_pe, qh * qh, 0.0), axis=-1, keepdims=True) / ROPE
        r = jnp.where(m_nope, lax.rsqrt(ss_n + EPS), lax.rsqrt(ss_p + EPS))
        qs.append(rope(qh * r * qnw) * Q_SCALE)
        kh = mmw(kva, wk[h], sk[h])
        kh = kh * lax.rsqrt(jnp.sum(kh * kh, axis=-1, keepdims=True) / NOPE + EPS) * knw
        ks.append(kh + k_rot)
        vs.append(mmw(kva, wv[h], sv[h]))
    return jnp.stack(qs), jnp.stack(ks), jnp.stack(vs)


def _f_ssd(xext, z, plast, prev, cw, cb, dtb, alog, dskip, snw):
    n = CHUNK
    conv = cb
    for k in range(4):
        conv = conv + cw[k:k + 1] * xext[HALO - 3 + k:HALO - 3 + k + n]
    xc = jax.nn.silu(conv)
    xs, bm, cm = xc[:, :D_SSD], xc[:, D_SSD:D_SSD + 2 * SSD_STATE], xc[:, D_SSD + 2 * SSD_STATE:]
    lane = lax.broadcasted_iota(jnp.int32, (1, 128), 1)
    dt = jax.nn.softplus(jnp.where(lane < N_HEADS, plast, 0.0) + dtb)
    adt = dt * (-jnp.exp(alog))
    row = lax.broadcasted_iota(jnp.int32, (n, n), 0)
    col = lax.broadcasted_iota(jnp.int32, (n, n), 1)
    tri = row >= col
    acs = jnp.dot(tri.astype(f32), adt, precision=lax.Precision.HIGHEST, preferred_element_type=f32)
    acs_t = acs.T
    bgs = [bm[:, g * SSD_STATE:(g + 1) * SSD_STATE] for g in range(2)]
    cgs = [cm[:, g * SSD_STATE:(g + 1) * SSD_STATE] for g in range(2)]
    cb_ts = [mm_nt(cgs[g], bgs[g]) for g in range(2)]
    low = lane < SSD_HEAD_DIM
    low_rows = lax.broadcasted_iota(jnp.int32, (2 * SSD_HEAD_DIM, 1), 0) < SSD_HEAD_DIM

    def both(a0, a1):
        return jnp.where(low, a0, a1)

    pre = []
    for i in range(N_HEADS // 2):
        h0, h1 = 2 * i, 2 * i + 1
        col0, col1 = acs[:, h0:h0 + 1], acs[:, h1:h1 + 1]
        last0, last1 = acs[n - 1:n, h0:h0 + 1], acs[n - 1:n, h1:h1 + 1]
        cb_t = cb_ts[i // 2]
        scores0 = cb_t * jnp.exp(jnp.where(tri, col0 - acs_t[h0:h0 + 1, :], -jnp.inf))
        scores1 = cb_t * jnp.exp(jnp.where(tri, col1 - acs_t[h1:h1 + 1, :], -jnp.inf))
        xp = xs[:, i * 128:(i + 1) * 128]
        xdt = xp * both(dt[:, h0:h0 + 1], dt[:, h1:h1 + 1])
        weighted = xdt * both(jnp.exp(last0 - col0), jnp.exp(last1 - col1))
        chunk_decay = jnp.where(low_rows, jnp.exp(last0), jnp.exp(last1))
        in_decay = both(jnp.exp(col0), jnp.exp(col1))
        skip = both(dskip[:, h0:h0 + 1], dskip[:, h1:h1 + 1]) * xp
        pre.append((scores0, scores1, xdt, weighted, chunk_decay, in_decay, skip))
    prods = []
    for i in range(N_HEADS // 2):
        scores0, scores1, xdt, weighted, _, _, _ = pre[i]
        g = i // 2
        y_diag = mm(scores0, jnp.where(low, xdt, 0.0)) + mm(scores1, jnp.where(low, 0.0, xdt))
        prods.append((y_diag, mm_tn(weighted, bgs[g]), mm_nt(cgs[g], prev[i])))
    ys, news = [], []
    for i in range(N_HEADS // 2):
        y_diag, st, y_off = prods[i]
        _, _, _, _, chunk_decay, in_decay, skip = pre[i]
        news.append(chunk_decay * prev[i] + st)
        ys.append(y_diag + y_off * in_decay + skip)
    y = jnp.concatenate(ys, axis=1)
    yg = y * jax.nn.silu(z)
    half = D_SSD // 2
    outs = []
    for g in range(2):
        t = yg[:, g * half:(g + 1) * half]
        outs.append(t * lax.rsqrt(jnp.mean(t * t, axis=-1, keepdims=True) + EPS))
    return jnp.concatenate(outs, axis=1) * snw, jnp.stack(news)


def _f_out(o, yg, g1, wo, slot=None):
    cat = jnp.concatenate([o[h] for h in range(N_HEADS)] + [yg], axis=1)
    return g1 * mmw(cat, wo, slot)


def _f_modulate(x, nw, sh, sc):
    return _rms(x, nw) * (1.0 + sc) + sh


def proj_fwd(x, nw, sh, sc, w):
    s = x.shape[0]
    ts = _token_block(s)

    def body(x_ref, nw_ref, sh_ref, sc_ref, w_ref, pa_ref, pz_ref, px_ref, pl_ref):
        p = _f_proj(x_ref[...], nw_ref[...], sh_ref[...], sc_ref[...], w_ref[...])
        pa_ref[...] = p[:, :384]
        pz_ref[...] = p[:, 384:896]
        px_ref[...] = p[:, 896:1920]
        pl_ref[...] = p[:, 1920:]

    vec = _const((1, D_MODEL))
    return pl.pallas_call(
        body, name="proj_fwd", grid=(s // ts,),
        in_specs=[pl.BlockSpec((ts, D_MODEL), lambda i: (i, 0)), vec, vec, vec, _const((D_PROJ, D_MODEL))],
        out_specs=[pl.BlockSpec((ts, 384), lambda i: (i, 0)), pl.BlockSpec((ts, 512), lambda i: (i, 0)),
                   pl.BlockSpec((ts, 1024), lambda i: (i, 0)), pl.BlockSpec((ts, 128), lambda i: (i, 0))],
        out_shape=[jax.ShapeDtypeStruct((s, 384), f32), jax.ShapeDtypeStruct((s, 512), f32),
                   jax.ShapeDtypeStruct((s, 1024), f32), jax.ShapeDtypeStruct((s, 128), f32)],
    )(x, nw, sh, sc, w)


def rope_tables(pos, inv, job=None):
    s = pos.shape[0]
    ts = _token_block(s)

    def body(pos_ref, inv_ref, cos_ref, sin_ref):
        ang = pos_ref[...].astype(f32) * inv_ref[...]
        lane = lax.broadcasted_iota(jnp.int32, (1, HEAD_LANES), 1)
        half = ROPE // 2
        cos_ref[...] = jnp.where(lane < NOPE, 1.0, jnp.where(lane < NOPE + ROPE, jnp.cos(ang), 0.0))
        sn = jnp.sin(ang)
        sin_ref[...] = jnp.where((lane >= NOPE) & (lane < NOPE + half), -sn,
                                 jnp.where((lane >= NOPE + half) & (lane < NOPE + ROPE), sn, 0.0))

    steps = s // ts
    return _call_with_job(
        body, "rope_tables" if job is None else "rope_tables_comm", (steps,), job,
        in_specs=[pl.BlockSpec((ts, 1), lambda i: (i, 0)), _const((1, HEAD_LANES))],
        out_specs=[pl.BlockSpec((ts, HEAD_LANES), lambda i: (i, 0))] * 2,
        out_shape=[jax.ShapeDtypeStruct((s, HEAD_LANES), f32)] * 2, scratch_shapes=[], operands=(pos, inv),
        relay_at=(max(steps - 2, 0),))


def _qkv_param_specs():
    return [_const((1, Q_RANK)), _const((1, KV_RANK)), _const((N_HEADS, Q_RANK, HEAD_LANES)),
            _const((N_HEADS, KV_RANK, HEAD_LANES)), _const((N_HEADS, KV_RANK, V_DIM)),
            _const((1, HEAD_LANES)), _const((1, HEAD_LANES)), _const((1, HEAD_LANES))]


def qkv_fwd(pa, plast, cos_t, sin_t, params):
    s = pa.shape[0]
    ts = _token_block(s)

    def body(pa_ref, pl_ref, cos_ref, sin_ref, *rest):
        prm = [r[...] for r in rest[:8]]
        q_ref, k_ref, v_ref = rest[8:]
        q, k, v = _f_qkv(pa_ref[...], pl_ref[...], cos_ref[...], sin_ref[...], *prm)
        q_ref[...] = q.astype(bf16)
        lane = lax.broadcasted_iota(jnp.int32, (1, 1, HEAD_LANES), 2)
        k_ref[...] = jnp.where((lane == SPARE_Q) | (lane == SPARE_Q + 1), 1.0, k).astype(bf16)
        v_ref[...] = jnp.concatenate([v, jnp.ones_like(v)], axis=-1).astype(bf16)

    tok = lambda w: pl.BlockSpec((ts, w), lambda i: (i, 0))
    head = pl.BlockSpec((N_HEADS, ts, HEAD_LANES), lambda i: (0, i, 0))
    return pl.pallas_call(
        body, name="qkv_fwd", grid=(s // ts,),
        in_specs=[tok(384), tok(128), tok(128), tok(128)] + _qkv_param_specs(),
        out_specs=[head] * 3, out_shape=[jax.ShapeDtypeStruct((N_HEADS, s, HEAD_LANES), bf16)] * 3,
    )(pa, plast, cos_t, sin_t, *params)


def _scores(q, k):
    return lax.dot_general(q, k, (((1,), (1,)), ((), ())), preferred_element_type=f32)


def _tril(rows, cols, row_offset):
    row = row_offset + lax.broadcasted_iota(jnp.int32, (rows, cols), 0)
    col = lax.broadcasted_iota(jnp.int32, (rows, cols), 1)
    return row >= col


def _call_with_job(body, name, grid, job, in_specs, out_specs, out_shape, scratch_shapes, operands, relay_at=None):
    if job is None:
        res = pl.pallas_call(body, name=name, grid=grid, in_specs=in_specs, out_specs=out_specs, out_shape=out_shape,
                             scratch_shapes=scratch_shapes)(*operands)
        return res, None

    def at_step(i, n):
        if i == 0:
            want = [0] * len(grid)
        elif i == n - 1:
            want = [g - 1 for g in grid]
        else:
            want = relay_at
        return functools.reduce(jnp.logical_and, [pl.program_id(a) == s for a, s in enumerate(want)])

    carrier = _carry(job, body, len(in_specs), len(out_specs), at_step)
    res = pl.pallas_call(
        carrier, name=name, grid=grid,
        in_specs=list(in_specs) + [ANY] * len(job.operands), out_specs=list(out_specs) + [ANY] * len(job.out_shape),
        out_shape=list(out_shape) + list(job.out_shape), scratch_shapes=list(scratch_shapes) + job.scratch,
    )(*operands, *job.operands)
    return res[:len(out_specs)], res[len(out_specs):]


def attn_fwd(q, k, v, job=None):
    s = q.shape[1]
    t = min(ATTN_BLOCK_FWD, s)
    nb = s // t

    rb = min(ATTN_ROWS_FWD, t)

    hp = ATTN_HEADS_FWD

    def body(q_ref, k_ref, v_ref, o_ref, qx_ref, m_sc, acc_sc):
        qi = pl.program_id(1)
        m_sc[...] = jnp.full(m_sc.shape, NEG, f32)
        acc_sc[...] = jnp.zeros(acc_sc.shape, f32)

        def step(k0, diagonal):
            chains = [(hh, r) for hh in range(hp) for r in range(t // rb)]

            def scores(hh, r):
                nk = (r + 1) * rb if diagonal else t
                sc = _scores(q_ref[hh, pl.ds(r * rb, rb), :], k_ref[hh, pl.ds(k0, nk), :])
                return jnp.where(_tril(rb, nk, r * rb), sc, NEG) if diagonal else sc

            ahead = scores(*chains[0])
            for c, (hh, r) in enumerate(chains):
                sc = ahead
                if c + 1 < len(chains):
                    ahead = scores(*chains[c + 1])
                rows = pl.ds(r * rb, rb)
                keys = pl.ds(k0, (r + 1) * rb if diagonal else t)
                m_prev = m_sc[hh, rows, :1]
                m_new = jnp.maximum(m_prev, jnp.max(sc, axis=-1, keepdims=True))
                p = jnp.exp2(sc - m_new)
                alpha = jnp.exp2(m_prev - m_new)
                acc = alpha * acc_sc[hh, rows, :] + jnp.dot(p.astype(bf16), v_ref[hh, keys, :], preferred_element_type=f32)
                if diagonal:
                    l = acc[:, V_DIM:V_DIM + 1]
                    o_ref[hh, rows, :] = acc[:, :V_DIM] / l
                    lse = m_new + jnp.log2(l)
                    high = lse.astype(bf16)
                    low = (lse - high.astype(f32)).astype(bf16)
                    lane = lax.broadcasted_iota(jnp.int32, (1, HEAD_LANES), 1)
                    qx_ref[hh, rows, :] = jnp.where(lane == SPARE_Q, -high,
                                                    jnp.where(lane == SPARE_Q + 1, -low, q_ref[hh, rows, :]))
                else:
                    acc_sc[hh, rows, :] = acc
                    m_sc[hh, rows, :] = jnp.broadcast_to(m_new, (rb, 128))

        def below(ki, carry):
            step(pl.multiple_of(ki * t, t), False)
            return carry

        lax.fori_loop(0, qi, below, 0)
        step(pl.multiple_of(qi * t, t), True)

    return _call_with_job(
        body, "attn_fwd" if job is None else "attn_fwd_comm", (N_HEADS // hp, nb), job,
        in_specs=[pl.BlockSpec((hp, t, HEAD_LANES), lambda h, qi: (h, qi, 0)),
                  pl.BlockSpec((hp, s, HEAD_LANES), lambda h, qi: (h, 0, 0)),
                  pl.BlockSpec((hp, s, HEAD_LANES), lambda h, qi: (h, 0, 0))],
        out_specs=[pl.BlockSpec((hp, t, V_DIM), lambda h, qi: (h, qi, 0)),
                   pl.BlockSpec((hp, t, HEAD_LANES), lambda h, qi: (h, qi, 0))],
        out_shape=[jax.ShapeDtypeStruct((N_HEADS, s, V_DIM), f32), jax.ShapeDtypeStruct((N_HEADS, s, HEAD_LANES), bf16)],
        scratch_shapes=[pltpu.VMEM((hp, t, 128), f32), pltpu.VMEM((hp, t, HEAD_LANES), f32)],
        operands=(q, k, v), relay_at=(N_HEADS // hp - 1, nb - 1))


def _ssd_param_specs():
    return [_const((4, D_CONV)), _const((1, D_CONV)), _const((1, 128)), _const((1, 128)), _const((1, 128)),
            _const((1, D_SSD))]


def ssd_fwd(px, pz, plast, params):
    s = px.shape[0]
    nc = s // CHUNK

    def body(px_ref, pz_ref, pl_ref, cw_ref, cb_ref, dtb_ref, alog_ref, dskip_ref, snw_ref, yg_ref, st_ref,
             state_sc, halo_sc):
        i = pl.program_id(0)

        @pl.when(i == 0)
        def _():
            state_sc[...] = jnp.zeros(state_sc.shape, f32)
            halo_sc[...] = jnp.zeros(halo_sc.shape, f32)

        x = px_ref[...]
        prev = state_sc[...]
        st_ref[...] = prev
        xext = jnp.concatenate([halo_sc[...], x], axis=0)
        yg, new = _f_ssd(xext, pz_ref[...], pl_ref[...], prev, cw_ref[...], cb_ref[...], dtb_ref[...],
                         alog_ref[...], dskip_ref[...], snw_ref[...])
        yg_ref[...] = yg
        state_sc[...] = new
        halo_sc[...] = x[CHUNK - HALO:]

    tok = lambda w: pl.BlockSpec((CHUNK, w), lambda i: (i, 0))
    return pl.pallas_call(
        body, name="ssd_fwd", grid=(nc,),
        in_specs=[tok(D_CONV), tok(D_SSD), tok(128)] + _ssd_param_specs(),
        out_specs=[tok(D_SSD), pl.BlockSpec((None, N_HEADS // 2, 2 * SSD_HEAD_DIM, SSD_STATE), lambda i: (i, 0, 0, 0))],
        out_shape=[jax.ShapeDtypeStruct((s, D_SSD), f32),
                   jax.ShapeDtypeStruct((nc, N_HEADS // 2, 2 * SSD_HEAD_DIM, SSD_STATE), f32)],
        scratch_shapes=[pltpu.VMEM((N_HEADS // 2, 2 * SSD_HEAD_DIM, SSD_STATE), f32), pltpu.VMEM((HALO, D_CONV), f32)],
    )(px, pz, plast, *params)


def out_fwd(x, o, yg, g1, wo):
    s = x.shape[0]
    ts = _token_block(s)

    def body(x_ref, o_ref, yg_ref, g1_ref, wo_ref, out_ref):
        out_ref[...] = x_ref[...] + _f_out(o_ref[...], yg_ref[...], g1_ref[...], wo_ref[...])

    return pl.pallas_call(
        body, name="out_fwd", grid=(s // ts,),
        in_specs=[pl.BlockSpec((ts, D_MODEL), lambda i: (i, 0)), pl.BlockSpec((N_HEADS, ts, V_DIM), lambda i: (0, i, 0)),
                  pl.BlockSpec((ts, D_SSD), lambda i: (i, 0)), _const((1, D_MODEL)), _const((D_MODEL, D_MODEL))],
        out_specs=pl.BlockSpec((ts, D_MODEL), lambda i: (i, 0)),
        out_shape=jax.ShapeDtypeStruct((s, D_MODEL), f32),
    )(x, o, yg, g1, wo)


def mlp_fwd(x, nw, sh, sc, g2, wgu, wd, target=None, job=None):
    s = x.shape[0]
    ts = min(MLP_FWD_ROWS, s)
    nj = N_DEV // 2

    def body(x_ref, nw_ref, sh_ref, sc_ref, g2_ref, wg_ref, wu_ref, wd_ref, *rest):
        if target is None:
            out_ref, mix_ref, h_ref, gate_ref, up_ref = rest
        else:
            t_ref, out_ref, mix_ref, h_ref, gate_ref, up_ref, loss_ref = rest
        j = pl.program_id(1)
        first_block = pl.program_id(0) == 0

        @pl.when(j == 0)
        def _():
            h_ref[...] = _f_modulate(x_ref[...], nw_ref[...], sh_ref[...], sc_ref[...]).astype(bf16)
            mix_ref[...] = jnp.zeros(mix_ref.shape, f32)

        nr = max(ts // 512, 1)
        half = ts // nr
        wg, wu, wd = wg_ref[...], wu_ref[...], wd_ref[...]
        products = lambda r: (mmw_t(h_ref[pl.ds(r * half, half), :], wg), mmw_t(h_ref[pl.ds(r * half, half), :], wu))
        ahead = products(0)
        for r in range(nr):
            gate, up = ahead
            if r + 1 < nr:
                ahead = products(r + 1)
            rows = pl.ds(r * half, half)
            gate_ref[rows, :] = gate.astype(bf16)
            up_ref[rows, :] = up.astype(bf16)
            mix_ref[rows, :] += mmw(jax.nn.silu(gate) * up, wd)

        @pl.when(j == nj - 1)
        def _():
            y = x_ref[...] + g2_ref[...] * mix_ref[...]
            if target is None:
                out_ref[...] = y
            else:
                d = y - t_ref[...]
                out_ref[...] = d * (1.0 / D_MODEL)
                part = 0.5 * jnp.sum(jnp.sum(d * d, axis=-1, keepdims=True) * (1.0 / D_MODEL), axis=0, keepdims=True)
                _accumulate(first_block, [loss_ref], [jnp.broadcast_to(part, (8, 128))])

    vec = _const((1, D_MODEL))
    tok = pl.BlockSpec((ts, D_MODEL), lambda i, j: (i, 0))
    wide = pl.BlockSpec((None, ts, FF_SHARD), lambda i, j: (j, i, 0))
    last = target is not None
    name = ("mlp_fwd_loss" if last else "mlp_fwd") + ("" if job is None else "_comm")
    return _call_with_job(
        body, name, (s // ts, nj), job,
        in_specs=[tok, vec, vec, vec, vec,
                  pl.BlockSpec((None, FF_SHARD, D_MODEL), lambda i, j: (j, 0, 0)),
                  pl.BlockSpec((None, FF_SHARD, D_MODEL), lambda i, j: (j + nj, 0, 0)),
                  pl.BlockSpec((None, FF_SHARD, D_MODEL), lambda i, j: (j, 0, 0))] + [tok] * last,
        out_specs=[tok] * 3 + [wide] * 2 + [_const((8, 128))] * last,
        out_shape=[jax.ShapeDtypeStruct((s, D_MODEL), f32), jax.ShapeDtypeStruct((s, D_MODEL), f32),
                   jax.ShapeDtypeStruct((s, D_MODEL), bf16)] + [jax.ShapeDtypeStruct((nj, s, FF_SHARD), bf16)] * 2
                  + [jax.ShapeDtypeStruct((8, 128), f32)] * last,
        scratch_shapes=[], operands=(x, nw, sh, sc, g2, wgu, wgu, wd, *([target] if last else [])),
        relay_at=(s // ts - 1, 0))


def mlp_bwd(h, dy, gate, up, g2, wgu, wd, job=None):
    s = h.shape[0]
    ts = min(MLP_BWD_ROWS, s)
    nj = N_DEV // 2
    ni = s // ts

    rows_per = min(MLP_BWD_CHUNK, ts)

    def body(h_ref, dy_ref, gate_ref, up_ref, g2_ref, wg_ref, wu_ref, wd_ref, dh_ref, dwg_ref, dwu_ref, dwd_ref,
             ag_sc, au_sc, ad_sc, act_sc, dgate_sc, dup_sc, dmix_sc):
        i = pl.program_id(1)
        wg, wu, wd = wg_ref[...], wu_ref[...], wd_ref[...]
        g2 = g2_ref[...]
        for r in range(ts // rows_per):
            rows = pl.ds(r * rows_per, rows_per)
            act, vjp = jax.vjp(lambda g, u: jax.nn.silu(g) * u, gate_ref[rows, :].astype(f32), up_ref[rows, :].astype(f32))
            dmix = (dy_ref[rows, :] * g2).astype(bf16)
            dgate, dup = vjp(_dot(dmix, wd, 1, 1))
            dgate, dup = dgate.astype(bf16), dup.astype(bf16)
            dh_ref[rows, :] = (_dot(dgate, wg, 1, 0) + _dot(dup, wu, 1, 0)).astype(bf16)
            act_sc[rows, :] = act.astype(bf16)
            dgate_sc[rows, :] = dgate
            dup_sc[rows, :] = dup
            dmix_sc[rows, :] = dmix
        h = h_ref[...]
        grads = [_dot(dgate_sc[...], h, 0, 0), _dot(dup_sc[...], h, 0, 0), _dot(act_sc[...], dmix_sc[...], 0, 0)]
        _accumulate_then_cast(i == 0, i == ni - 1, [ag_sc, au_sc, ad_sc], [dwg_ref, dwu_ref, dwd_ref], grads)

    once = pl.Buffered(1)
    wspec = lambda off: pl.BlockSpec((None, FF_SHARD, D_MODEL), lambda j, i: (j + off, 0, 0), pipeline_mode=once)
    dspec = pl.BlockSpec((None, FF_SHARD, D_MODEL), lambda j, i: (j, 0, 0), pipeline_mode=once)
    wide = pl.BlockSpec((None, ts, FF_SHARD), lambda j, i: (j, i, 0))
    return _call_with_job(
        body, "mlp_bwd" if job is None else "mlp_bwd_comm", (nj, ni), job,
        in_specs=[pl.BlockSpec((ts, D_MODEL), lambda j, i: (i, 0)), pl.BlockSpec((ts, D_MODEL), lambda j, i: (i, 0)),
                  wide, wide, _const((1, D_MODEL)), wspec(0), wspec(nj), dspec],
        out_specs=[pl.BlockSpec((None, ts, D_MODEL), lambda j, i: (j, i, 0)), wspec(0), wspec(0), dspec],
        out_shape=[jax.ShapeDtypeStruct((nj, s, D_MODEL), bf16),
                   jax.ShapeDtypeStruct((nj, FF_SHARD, D_MODEL), bf16), jax.ShapeDtypeStruct((nj, FF_SHARD, D_MODEL), bf16),
                   jax.ShapeDtypeStruct((nj, FF_SHARD, D_MODEL), bf16)],
        scratch_shapes=[pltpu.VMEM((FF_SHARD, D_MODEL), f32), pltpu.VMEM((FF_SHARD, D_MODEL), f32),
                        pltpu.VMEM((FF_SHARD, D_MODEL), f32), pltpu.VMEM((ts, FF_SHARD), bf16),
                        pltpu.VMEM((ts, FF_SHARD), bf16), pltpu.VMEM((ts, FF_SHARD), bf16), pltpu.VMEM((ts, D_MODEL), bf16)],
        operands=(h, dy, gate, up, g2, wgu, wgu, wd))


def out_bwd(dy, dhparts, x, nw, sh, sc, mix, o, yg, g1, wo):
    s = dy.shape[0]
    ts = _token_block(s)
    nj = dhparts.shape[0]

    ni = s // ts

    def body(dy_ref, dp_ref, x_ref, nw_ref, sh_ref, sc_ref, mix_ref, o_ref, yg_ref, g1_ref, wo_ref,
             dx_ref, dnw_ref, dsh_ref, dsc_ref, do_ref, dyg_ref, dg1_ref, dg2_ref, dwo_ref, acc_sc):
        i = pl.program_id(0)
        g = dy_ref[...]
        _accumulate(i == 0, [dg2_ref], [jnp.sum(g * mix_ref[...], axis=0, keepdims=True)])
        dh = dp_ref[0].astype(f32)
        for j in range(1, nj):
            dh = dh + dp_ref[j].astype(f32)
        _, vjp_mod = jax.vjp(_f_modulate, x_ref[...], nw_ref[...], sh_ref[...], sc_ref[...])
        dx_mod, dnw, dsh, dsc = vjp_mod(dh)
        _accumulate(i == 0, [dnw_ref, dsh_ref, dsc_ref], [dnw, dsh, dsc])
        g = g + dx_mod
        dx_ref[...] = g
        o = o_ref[...]
        wo = wo_ref[...]
        _, vjp = jax.vjp(lambda o_, yg_, g1_, slot: _f_out(o_, yg_, g1_, wo, slot), o, yg_ref[...], g1_ref[...],
                         jnp.zeros(wo.shape, f32))
        do, dyg, dg1, dwo = vjp(g)
        delta = jnp.sum(do * o, axis=-1, keepdims=True)
        high = delta.astype(bf16)
        low = (delta - high.astype(f32)).astype(bf16)
        lane = lax.broadcasted_iota(jnp.int32, (1, 1, HEAD_LANES), 2)
        wide = jnp.concatenate([do.astype(bf16), jnp.zeros(do.shape, bf16)], axis=-1)
        do_ref[...] = jnp.where(lane == SPARE_V, -high, jnp.where(lane == SPARE_V + 1, -low, wide))
        dyg_ref[...] = dyg
        _accumulate(i == 0, [dg1_ref], [dg1])
        _accumulate_then_cast(i == 0, i == ni - 1, [acc_sc], [dwo_ref], [dwo])

    head = pl.BlockSpec((N_HEADS, ts, V_DIM), lambda i: (0, i, 0))
    tok = pl.BlockSpec((ts, D_MODEL), lambda i: (i, 0))
    vec = _const((1, D_MODEL))
    vshape = jax.ShapeDtypeStruct((1, D_MODEL), f32)
    return pl.pallas_call(
        body, name="out_bwd", grid=(ni,), scratch_shapes=[pltpu.VMEM((D_MODEL, D_MODEL), f32)],
        in_specs=[tok, pl.BlockSpec((nj, ts, D_MODEL), lambda i: (0, i, 0)), tok, vec, vec, vec, tok,
                  head, pl.BlockSpec((ts, D_SSD), lambda i: (i, 0)), vec, _const((D_MODEL, D_MODEL))],
        out_specs=[tok, vec, vec, vec, pl.BlockSpec((N_HEADS, ts, HEAD_LANES), lambda i: (0, i, 0)),
                   pl.BlockSpec((ts, D_SSD), lambda i: (i, 0)), vec, vec, _const((D_MODEL, D_MODEL))],
        out_shape=[jax.ShapeDtypeStruct((s, D_MODEL), f32), vshape, vshape, vshape,
                   jax.ShapeDtypeStruct((N_HEADS, s, HEAD_LANES), bf16), jax.ShapeDtypeStruct((s, D_SSD), f32),
                   vshape, vshape, jax.ShapeDtypeStruct((D_MODEL, D_MODEL), bf16)],
    )(dy, dhparts, x, nw, sh, sc, mix, o, yg, g1, wo)


def attn_bwd(qx, k, v, do, job=None):
    s = qx.shape[1]
    t = _token_block(s)
    nb = s // t

    hp = ATTN_HEADS_BWD

    def body(q_ref, k_ref, v_ref, do_ref, dq_ref, dk_ref, dv_ref, dv_sc):
        ki = pl.program_id(1)

        @pl.when(ki == 0)
        def _():
            dq_ref[...] = jnp.zeros(dq_ref.shape, f32)

        dk_ref[...] = jnp.zeros(dk_ref.shape, f32)
        dv_sc[...] = jnp.zeros(dv_sc.shape, f32)

        def step(q0, diagonal):
            half = t // 2
            subs = [(0, half, half), (half, half, t)] if diagonal and half % 128 == 0 else [(0, t, t)]
            chains = [(hh, sub) for hh in range(hp) for sub in subs]

            def products(hh, sub):
                r0, nr, nk = sub
                rows = pl.ds(q0 + r0, nr)
                sc = _scores(q_ref[hh, rows, :], k_ref[hh, :nk, :])
                dps = _scores(do_ref[hh, rows, :], v_ref[hh, :nk, :])
                return (jnp.where(_tril(nr, nk, r0), sc, NEG) if diagonal else sc), dps

            ahead = products(*chains[0])
            for c, (hh, (r0, nr, nk)) in enumerate(chains):
                sc, dps = ahead
                if c + 1 < len(chains):
                    ahead = products(*chains[c + 1])
                rows = pl.ds(q0 + r0, nr)
                p = jnp.exp2(sc)
                ds = (p * dps).astype(bf16)
                dv_sc[hh, :nk, :] += lax.dot_general(p.astype(bf16), do_ref[hh, rows, :], (((0,), (0,)), ((), ())),
                                                     preferred_element_type=f32)
                dk_ref[hh, :nk, :] += lax.dot_general(ds, q_ref[hh, rows, :], (((0,), (0,)), ((), ())),
                                                      preferred_element_type=f32)
                dq_ref[hh, rows, :] += jnp.dot(ds, k_ref[hh, :nk, :], preferred_element_type=f32)

        step(pl.multiple_of(ki * t, t), True)

        def above(qi, carry):
            step(pl.multiple_of(qi * t, t), False)
            return carry

        lax.fori_loop(ki + 1, nb, above, 0)
        real = lax.broadcasted_iota(jnp.int32, (1, 1, HEAD_LANES), 2) < SPARE_Q
        dk_ref[...] = jnp.where(real, dk_ref[...] * LN2, 0.0)
        dv_ref[...] = dv_sc[:, :, :V_DIM]

        @pl.when(ki == nb - 1)
        def _():
            dq_ref[...] = jnp.where(real, dq_ref[...] * LN2, 0.0)

    qspec = pl.BlockSpec((hp, s, HEAD_LANES), lambda h, ki: (h, 0, 0))
    kspec = lambda w: pl.BlockSpec((hp, t, w), lambda h, ki: (h, ki, 0))
    return _call_with_job(
        body, "attn_bwd" if job is None else "attn_bwd_comm", (N_HEADS // hp, nb), job,
        in_specs=[qspec, kspec(HEAD_LANES), kspec(HEAD_LANES), qspec],
        out_specs=[qspec, kspec(HEAD_LANES), kspec(V_DIM)],
        out_shape=[jax.ShapeDtypeStruct((N_HEADS, s, HEAD_LANES), f32), jax.ShapeDtypeStruct((N_HEADS, s, HEAD_LANES), f32),
                   jax.ShapeDtypeStruct((N_HEADS, s, V_DIM), f32)],
        scratch_shapes=[pltpu.VMEM((hp, t, HEAD_LANES), f32)], operands=(qx, k, v, do))


def ssd_bwd(px, pz, plast, states, dyg, params):
    s = px.shape[0]
    nc = s // CHUNK
    per = CHUNK // HALO

    def body(px_ref, halo_ref, pz_ref, pl_ref, st_ref, dyg_ref, cw_ref, cb_ref, dtb_ref, alog_ref, dskip_ref, snw_ref,
             dpx_ref, dpz_ref, dpl_ref, dcw_ref, dcb_ref, ddtb_ref, dalog_ref, ddskip_ref, dsnw_ref, dstate_sc, dhalo_sc):
        t = pl.program_id(0)
        chunk = nc - 1 - t

        @pl.when(t == 0)
        def _():
            dstate_sc[...] = jnp.zeros(dstate_sc.shape, f32)
            dhalo_sc[...] = jnp.zeros(dhalo_sc.shape, f32)

        halo = jnp.where(chunk > 0, halo_ref[...], 0.0)
        xext = jnp.concatenate([halo, px_ref[...]], axis=0)
        _, vjp = jax.vjp(_f_ssd, xext, pz_ref[...], pl_ref[...], st_ref[...], cw_ref[...], cb_ref[...], dtb_ref[...],
                         alog_ref[...], dskip_ref[...], snw_ref[...])
        dxext, dz, dpl, dprev, dcw, dcb, ddtb, dalog, ddskip, dsnw = vjp((dyg_ref[...], dstate_sc[...]))
        dpx_ref[...] = dxext[HALO:]
        dpx_ref[CHUNK - HALO:, :] += dhalo_sc[...]
        dhalo_sc[...] = dxext[:HALO]
        dstate_sc[...] = dprev
        dpz_ref[...] = dz
        dpl_ref[...] = dpl
        _accumulate(t == 0, [dcw_ref, dcb_ref, ddtb_ref, dalog_ref, ddskip_ref, dsnw_ref],
                    [dcw, dcb, ddtb, dalog, ddskip, dsnw])

    rev = lambda w: pl.BlockSpec((CHUNK, w), lambda t: (nc - 1 - t, 0))
    pshapes = [jax.ShapeDtypeStruct((4, D_CONV), f32), jax.ShapeDtypeStruct((1, D_CONV), f32),
               jax.ShapeDtypeStruct((1, 128), f32), jax.ShapeDtypeStruct((1, 128), f32),
               jax.ShapeDtypeStruct((1, 128), f32), jax.ShapeDtypeStruct((1, D_SSD), f32)]
    return pl.pallas_call(
        body, name="ssd_bwd", grid=(nc,),
        in_specs=[rev(D_CONV),
                  pl.BlockSpec((HALO, D_CONV), lambda t: (jnp.maximum((nc - 1 - t) * per - 1, 0), 0)),
                  rev(D_SSD), rev(128),
                  pl.BlockSpec((None, N_HEADS // 2, 2 * SSD_HEAD_DIM, SSD_STATE), lambda t: (nc - 1 - t, 0, 0, 0)),
                  rev(D_SSD)] + _ssd_param_specs(),
        out_specs=[rev(D_CONV), rev(D_SSD), rev(128)] + _ssd_param_specs(),
        out_shape=[jax.ShapeDtypeStruct((s, D_CONV), f32), jax.ShapeDtypeStruct((s, D_SSD), f32),
                   jax.ShapeDtypeStruct((s, 128), f32)] + pshapes,
        scratch_shapes=[pltpu.VMEM((N_HEADS // 2, 2 * SSD_HEAD_DIM, SSD_STATE), f32), pltpu.VMEM((HALO, D_CONV), f32)],
    )(px, px, pz, plast, states, dyg, *params)


def qkv_bwd(pa, plast, cos_t, sin_t, params, dq, dk, dv):
    s = pa.shape[0]
    ts = _token_block(s)

    def body(pa_ref, pl_ref, cos_ref, sin_ref, *rest):
        qaw, kvaw, wq, wk, wv, qnw, knw, kpw = [r[...] for r in rest[:8]]
        dq_ref, dk_ref, dv_ref = rest[8:11]
        dpa_ref, dpl_ref = rest[11:13]
        dprm_refs = list(rest[13:])
        cos_t, sin_t = cos_ref[...], sin_ref[...]

        def stage(pa_, pl_, qaw_, kvaw_, sq, sk, sv, qnw_, knw_, kpw_):
            return _f_qkv(pa_, pl_, cos_t, sin_t, qaw_, kvaw_, wq, wk, wv, qnw_, knw_, kpw_, (sq, sk, sv))

        _, vjp = jax.vjp(stage, pa_ref[...], pl_ref[...], qaw, kvaw, jnp.zeros(wq.shape, f32), jnp.zeros(wk.shape, f32),
                         jnp.zeros(wv.shape, f32), qnw, knw, kpw)
        grads = vjp((dq_ref[...], dk_ref[...], dv_ref[...]))
        dpa_ref[...] = grads[0]
        dpl_ref[...] = grads[1]
        _accumulate(pl.program_id(0) == 0, dprm_refs, list(grads[2:]))

    tok = lambda w: pl.BlockSpec((ts, w), lambda i: (i, 0))
    head = lambda w: pl.BlockSpec((N_HEADS, ts, w), lambda i: (0, i, 0))
    pshapes = [jax.ShapeDtypeStruct((1, Q_RANK), f32), jax.ShapeDtypeStruct((1, KV_RANK), f32),
               jax.ShapeDtypeStruct((N_HEADS, Q_RANK, HEAD_LANES), f32), jax.ShapeDtypeStruct((N_HEADS, KV_RANK, HEAD_LANES), f32),
               jax.ShapeDtypeStruct((N_HEADS, KV_RANK, V_DIM), f32), jax.ShapeDtypeStruct((1, HEAD_LANES), f32),
               jax.ShapeDtypeStruct((1, HEAD_LANES), f32), jax.ShapeDtypeStruct((1, HEAD_LANES), f32)]
    return pl.pallas_call(
        body, name="qkv_bwd", grid=(s // ts,),
        in_specs=[tok(384), tok(128), tok(128), tok(128)] + _qkv_param_specs()
                 + [head(HEAD_LANES), head(HEAD_LANES), head(V_DIM)],
        out_specs=[tok(384), tok(128)] + _qkv_param_specs(),
        out_shape=[jax.ShapeDtypeStruct((s, 384), f32), jax.ShapeDtypeStruct((s, 128), f32)] + pshapes,
    )(pa, plast, cos_t, sin_t, *params, dq, dk, dv)


def proj_bwd(x, nw, sh, sc, w, dpa, dpz, dpx, dpl_k, dpl_dt, dres):
    s = x.shape[0]
    ts = _token_block(s)

    ni = s // ts

    def body(x_ref, nw_ref, sh_ref, sc_ref, w_ref, dpa_ref, dpz_ref, dpx_ref, dplk_ref, dpld_ref, dres_ref,
             dx_ref, dnw_ref, dsh_ref, dsc_ref, dw_ref, acc_sc):
        i = pl.program_id(0)
        g = jnp.concatenate([dpa_ref[...], dpz_ref[...], dpx_ref[...], dplk_ref[...] + dpld_ref[...]], axis=1)
        w = w_ref[...]
        _, vjp = jax.vjp(lambda x_, nw_, sh_, sc_, slot: _f_proj(x_, nw_, sh_, sc_, w, slot), x_ref[...], nw_ref[...],
                         sh_ref[...], sc_ref[...], jnp.zeros(w.shape, f32))
        dx, dnw, dsh, dsc, dw = vjp(g)
        dx_ref[...] = dx + dres_ref[...]
        _accumulate(i == 0, [dnw_ref, dsh_ref, dsc_ref], [dnw, dsh, dsc])
        _accumulate_then_cast(i == 0, i == ni - 1, [acc_sc], [dw_ref], [dw])

    vec = _const((1, D_MODEL))
    vshape = jax.ShapeDtypeStruct((1, D_MODEL), f32)
    tok = lambda w_: pl.BlockSpec((ts, w_), lambda i: (i, 0))
    return pl.pallas_call(
        body, name="proj_bwd", grid=(ni,), scratch_shapes=[pltpu.VMEM((D_PROJ, D_MODEL), f32)],
        in_specs=[tok(D_MODEL), vec, vec, vec, _const((D_PROJ, D_MODEL)), tok(384), tok(512), tok(1024), tok(128), tok(128),
                  tok(D_MODEL)],
        out_specs=[tok(D_MODEL), vec, vec, vec, _const((D_PROJ, D_MODEL))],
        out_shape=[jax.ShapeDtypeStruct((s, D_MODEL), f32), vshape, vshape, vshape,
                   jax.ShapeDtypeStruct((D_PROJ, D_MODEL), bf16)],
    )(x, nw, sh, sc, w, dpa, dpz, dpx, dpl_k, dpl_dt, dres)


def ada_fwd(c_all, w_ada, b_cols):
    def body(c_ref, w_ref, b_ref, out_ref):
        act = jax.nn.silu(c_ref[...])
        for l in range(2):
            out_ref[l] = jnp.dot(act, w_ref[l], precision=lax.Precision.HIGHEST, preferred_element_type=f32) + b_ref[l]

    return pl.pallas_call(body, name="ada_fwd", out_shape=jax.ShapeDtypeStruct((2, N_DEV, 768), f32))(c_all, w_ada, b_cols)


def ada_bwd(c_all, dmod_cols):
    def body(c_ref, d_ref, out_ref):
        out_ref[0] = lax.dot_general(jax.nn.silu(c_ref[...]), d_ref[0], (((0,), (0,)), ((), ())),
                                     precision=lax.Precision.HIGHEST, preferred_element_type=f32)

    return pl.pallas_call(
        body, name="ada_bwd", grid=(2,),
        in_specs=[_const((N_DEV, D_MODEL)), pl.BlockSpec((1, N_DEV, 768), lambda l: (l, 0, 0))],
        out_specs=pl.BlockSpec((1, D_MODEL, 768), lambda l: (l, 0, 0)),
        out_shape=jax.ShapeDtypeStruct((2, D_MODEL, 768), f32),
    )(c_all, dmod_cols)


def _adamw(w, g, m, v):
    m = ADAM_B1 * m + (1.0 - ADAM_B1) * g
    v = ADAM_B2 * v + (1.0 - ADAM_B2) * (g * g)
    m_hat = m / (1.0 - ADAM_B1 ** ADAM_STEP)
    v_hat = v / (1.0 - ADAM_B2 ** ADAM_STEP)
    delta = -ADAM_LR * (m_hat / (jnp.sqrt(v_hat) + ADAM_EPS) + ADAM_WD * w)
    return delta, m, v


def adamw(parts, w, m, v, layer, prev, name):
    n, r, c = parts.shape
    nl = w.shape[0]
    per_elem = 2 * (n * parts.dtype.itemsize + 7 * 4)
    lanes = -(-c // 128) * 128
    tr, tc = r, c
    if per_elem * r * lanes > ADAMW_BLOCK_BYTES:
        fits = [t for t in range(r // 2, 15, -1) if r % t == 0 and t % 16 == 0 and per_elem * t * lanes <= ADAMW_BLOCK_BYTES]
        if fits:
            tr = fits[0]
        else:
            tc = next(t for t in (512, 256, 128) if c % t == 0)

    def body(p_ref, w_ref, m_ref, v_ref, *rest):
        g_ref, d_ref, nm_ref, nv_ref = rest[-4:]
        g = p_ref[0].astype(f32)
        for k in range(1, n):
            g = g + p_ref[k].astype(f32)
        delta, nm, nv = _adamw(w_ref[...], g, m_ref[...], v_ref[...])
        g_ref[...] = g
        d_ref[...] = delta
        nm_ref[...] = nm
        nv_ref[...] = nv

    blk = pl.BlockSpec((None, tr, tc), lambda i, j: (layer, i, j))
    shp = jax.ShapeDtypeStruct((nl, r, c), f32)
    kept = [] if prev is None else list(prev)
    return pl.pallas_call(
        body, name=name, grid=(r // tr, c // tc),
        in_specs=[pl.BlockSpec((n, tr, tc), lambda i, j: (0, i, j)), blk, blk, blk] + [ANY] * len(kept),
        out_specs=[blk] * 4, out_shape=[shp] * 4,
        input_output_aliases={4 + j: j for j in range(len(kept))},
    )(parts, w, m, v, *kept)


def adamw_two_layers(parts, w, m, v, name, job=None):
    n, r, c = parts[0].shape
    per_elem = 2 * (2 * n * parts[0].dtype.itemsize + 7 * 4)
    lanes = -(-c // 128) * 128
    tr = next(t for t in range(r, 15, -1) if r % t == 0 and t % 16 == 0 and per_elem * t * lanes <= ADAMW_BLOCK_BYTES)
    nblk = r // tr

    def body(p0_ref, p1_ref, w_ref, m_ref, v_ref, g_ref, d_ref, nm_ref, nv_ref):
        layer = pl.program_id(0)

        def update(p_ref):
            g = p_ref[0].astype(f32)
            for k in range(1, n):
                g = g + p_ref[k].astype(f32)
            delta, nm, nv = _adamw(w_ref[...], g, m_ref[...], v_ref[...])
            g_ref[...] = g
            d_ref[...] = delta
            nm_ref[...] = nm
            nv_ref[...] = nv

        @pl.when(layer == 0)
        def _():
            update(p0_ref)

        @pl.when(layer == 1)
        def _():
            update(p1_ref)

    blk = pl.BlockSpec((None, tr, c), lambda l, i: (l, i, 0))
    shp = jax.ShapeDtypeStruct((2, r, c), f32)
    return _call_with_job(
        body, name, (2, nblk), job,
        in_specs=[pl.BlockSpec((n, tr, c), lambda l, i: (0, i * (1 - l), 0)),
                  pl.BlockSpec((n, tr, c), lambda l, i: (0, i * l, 0)), blk, blk, blk],
        out_specs=[blk] * 4, out_shape=[shp] * 4, scratch_shapes=[], operands=(parts[0], parts[1], w, m, v),
        relay_at=(1, 0))


def adamw_layers_inside(parts, w, m, v, name):
    n, r, c = parts[0].shape
    nl = w.shape[1]
    tc = next(t for t in (256, 128) if c % t == 0)

    def body(*refs):
        p_refs = refs[:nl]
        w_ref, m_ref, v_ref, g_ref, d_ref, nm_ref, nv_ref = refs[nl:]
        for l in range(nl):
            g = p_refs[l][0].astype(f32)
            for k in range(1, n):
                g = g + p_refs[l][k].astype(f32)
            delta, nm, nv = _adamw(w_ref[:, l, :], g, m_ref[:, l, :], v_ref[:, l, :])
            g_ref[:, l, :] = g
            d_ref[:, l, :] = delta
            nm_ref[:, l, :] = nm
            nv_ref[:, l, :] = nv

    blk = pl.BlockSpec((r, nl, tc), lambda j: (0, 0, j))
    shp = jax.ShapeDtypeStruct((r, nl, c), f32)
    return pl.pallas_call(
        body, name=name, grid=(c // tc,),
        in_specs=[pl.BlockSpec((n, r, tc), lambda j: (0, 0, j))] * nl + [blk] * 3,
        out_specs=[blk] * 4, out_shape=[shp] * 4,
    )(*parts, w, m, v)


def _my_index():
    return 4 * lax.axis_index("x") + 2 * lax.axis_index("y") + lax.axis_index("c")


def _coords(idx):
    return (idx // 4, (idx // 2) % 2, idx % 2)


class CommJob:
    def __init__(self, operands, out_shape, phases, scratch):
        self.operands, self.out_shape, self.phases, self.scratch = operands, out_shape, phases, scratch


def _wait(out, n_blocks, send_sem, recv_sem, send=True, recv=True):
    span = out.at[pl.ds(0, n_blocks)]
    desc = pltpu.make_async_remote_copy(src_ref=span, dst_ref=span, send_sem=send_sem, recv_sem=recv_sem,
                                        device_id=_coords(_my_index()), device_id_type=MESH)
    if recv:
        desc.wait_recv()
    if send:
        desc.wait_send()


def gather_job(shards):
    n = len(shards)

    def places():
        x, y, c = lax.axis_index("x"), lax.axis_index("y"), lax.axis_index("c")
        return (x, y, c), (x, y, 1 - c), [(1 - x, y), (x, 1 - y), (1 - x, 1 - y)]

    def index(p):
        return 4 * p[0] + 2 * p[1] + p[2]

    def start(ins, outs, sems):
        far_send, far_recv, near_send, near_recv, local = sems
        me, sibling, chips = places()
        for k in range(n):
            pltpu.make_async_copy(ins[k], outs[k].at[index(me)], local.at[k]).start()
            for chip in chips:
                pltpu.make_async_remote_copy(src_ref=ins[k], dst_ref=outs[k].at[index(me)], send_sem=far_send.at[k],
                                             recv_sem=far_recv.at[k], device_id=(*chip, me[2]), device_id_type=MESH).start()
            pltpu.make_async_remote_copy(src_ref=ins[k], dst_ref=outs[k].at[index(me)], send_sem=near_send.at[k],
                                         recv_sem=near_recv.at[k], device_id=sibling, device_id_type=MESH).start()

    def relay(ins, outs, sems):
        far_send, far_recv, near_send, near_recv, local = sems
        me, sibling, chips = places()
        for k in range(n):
            _wait(outs[k], 3, far_send.at[k], far_recv.at[k], send=False)
            for chip in chips:
                block = outs[k].at[index((*chip, me[2]))]
                pltpu.make_async_remote_copy(src_ref=block, dst_ref=block, send_sem=near_send.at[k],
                                             recv_sem=near_recv.at[k], device_id=sibling, device_id_type=MESH).start()

    def finish(ins, outs, sems):
        far_send, far_recv, near_send, near_recv, local = sems
        for k in range(n):
            _wait(outs[k], 4, near_send.at[k], near_recv.at[k])
            _wait(outs[k], 3, far_send.at[k], far_recv.at[k], recv=False)
            pltpu.make_async_copy(ins[k], outs[k].at[0], local.at[k]).wait()

    shapes = [jax.ShapeDtypeStruct((N_DEV,) + tuple(a.shape), a.dtype) for a in shards]
    return CommJob(list(shards), shapes, [start, relay, finish], [pltpu.SemaphoreType.DMA((n,))] * 5)


def scatter_job(tensors):
    n = len(tensors)
    flat, where = [], {}
    for k, pieces in enumerate(tensors):
        d = 0
        for piece in pieces:
            for b in range(piece.shape[0]):
                where[k, d] = (len(flat), b)
                d += 1
            flat.append(piece)
        assert d == N_DEV

    def start(ins, outs, sems):
        send_sems, recv_sems, local_sems = sems
        me = _my_index()

        def block(k, d):
            i, b = where[k, d]
            return ins[i].at[b]

        for d in range(N_DEV):
            @pl.when(d != me)
            def _():
                for k in range(n):
                    pltpu.make_async_remote_copy(src_ref=block(k, d), dst_ref=outs[k].at[me], send_sem=send_sems.at[k],
                                                 recv_sem=recv_sems.at[k], device_id=(d // 4, (d // 2) % 2, d % 2),
                                                 device_id_type=MESH).start()

            @pl.when(d == me)
            def _():
                for k in range(n):
                    pltpu.make_async_copy(block(k, d), outs[k].at[d], local_sems.at[k]).start()

    def finish(ins, outs, sems):
        send_sems, recv_sems, local_sems = sems
        for k in range(n):
            _wait(outs[k], N_DEV - 1, send_sems.at[k], recv_sems.at[k])
            i, b = where[k, 0]
            pltpu.make_async_copy(ins[i].at[b], outs[k].at[0], local_sems.at[k]).wait()

    shapes = [jax.ShapeDtypeStruct((N_DEV,) + tuple(p[0].shape[1:]), p[0].dtype) for p in tensors]
    return CommJob(flat, shapes, [start, finish], [pltpu.SemaphoreType.DMA((n,))] * 3)


def merge_jobs(a, b):
    def on(job, off):
        oi, oo, os_ = off
        ni, no, ns = len(job.operands), len(job.out_shape), len(job.scratch)
        return lambda phase: (lambda ins, outs, sems: phase(ins[oi:oi + ni], outs[oo:oo + no], sems[os_:os_ + ns]))

    wrap_a = on(a, (0, 0, 0))
    wrap_b = on(b, (len(a.operands), len(a.out_shape), len(a.scratch)))
    pa, pb = [wrap_a(p) for p in a.phases], [wrap_b(p) for p in b.phases]

    def together(*phases):
        def run(ins, outs, sems):
            for p in phases:
                p(ins, outs, sems)
        return run

    middle = pa[1:-1] + pb[1:-1]
    phases = [together(pa[0], pb[0])] + middle + [together(pa[-1], pb[-1])]
    return CommJob(a.operands + b.operands, a.out_shape + b.out_shape, phases, a.scratch + b.scratch)


def comm_call(job, name):
    ni, no = len(job.operands), len(job.out_shape)

    def body(*refs):
        ins, outs, sems = refs[:ni], refs[ni:ni + no], refs[ni + no:]
        for phase in job.phases:
            phase(ins, outs, sems)

    return pl.pallas_call(body, name=name, in_specs=[ANY] * ni, out_specs=[ANY] * no, out_shape=job.out_shape,
                          scratch_shapes=job.scratch)(*job.operands)


def _carry(job, body, n_in, n_out, at_step):
    ji, jo, js = len(job.operands), len(job.out_shape), len(job.scratch)

    def carrier(*refs):
        a, b = n_in, n_in + ji
        c, d = b + n_out, b + n_out + jo
        e = len(refs) - js
        job_refs = (refs[a:b], refs[c:d], refs[e:])
        n = len(job.phases)

        @pl.when(at_step(0, n))
        def _():
            job.phases[0](*job_refs)

        body(*refs[:a], *refs[b:c], *refs[d:e])

        for i in range(1, n):
            @pl.when(at_step(i, n))
            def _():
                job.phases[i](*job_refs)

    return carrier


def _pad_lanes(v, lo, total=128):
    return jnp.pad(v, (lo, total - lo - v.shape[0]))[None, :]


MIXER_WEIGHTS = ("w_in", "w_q_up", "w_kv_up", "conv_w")
LATE_WEIGHTS = ("w_out", "w_gate_up", "w_down")


def mixer_operands(g, sw):
    w_in = g["w_in"].reshape(D_IN, D_MODEL)
    zero = lambda rows: jnp.zeros((rows, D_MODEL), w_in.dtype)
    w_proj = jnp.concatenate(
        [w_in[:384], w_in[416:928], w_in[928:1952], w_in[1952:1960], zero(56), w_in[384:416], zero(32)], axis=0)
    wq = jnp.pad(g["w_q_up"], ((0, 0), (0, 0), (0, HEAD_LANES - NOPE - ROPE)))
    wk = jnp.pad(g["w_kv_up"][:, :, :NOPE], ((0, 0), (0, 0), (0, HEAD_LANES - NOPE)))
    wv = g["w_kv_up"][:, :, NOPE:]
    qkv = (sw["q_a_norm_w"][None, :], sw["kv_a_norm_w"][None, :], wq, wk, wv,
           _pad_lanes(jnp.concatenate([sw["q_nope_norm_w"], sw["q_pe_norm_w"]]), 0),
           _pad_lanes(sw["k_nope_norm_w"], 0), _pad_lanes(sw["k_pe_norm_w"], NOPE))
    conv_w = g["conv_w"].astype(f32).transpose(1, 0, 2).reshape(4, D_CONV)
    ssd = (conv_w, sw["conv_b"][None, :], _pad_lanes(sw["dt_bias"], 0), _pad_lanes(sw["a_log"], 0),
           _pad_lanes(sw["d_skip"], 0), sw["ssd_norm_w"][None, :])
    return dict(w_proj=w_proj, qkv=qkv, ssd=ssd, n1=sw["norm1_w"][None, :])


def late_operands(g, sw):
    return dict(wo=g["w_out"].reshape(D_MODEL, D_MODEL), wgu=g["w_gate_up"],
                wd=g["w_down"].reshape(N_DEV // 2, FF_SHARD, D_MODEL), n2=sw["norm2_w"][None, :])


def layer_fwd(x, mod, kw, cos_t, sin_t, job=None, late=None, target=None, mixer_job=None):
    sh1, sc1, g1, sh2, sc2, g2 = [mod[i:i + 1] for i in range(6)]
    pa, pz, px, plast = proj_fwd(x, kw["n1"], sh1, sc1, kw["w_proj"])
    q, k, v = qkv_fwd(pa, plast, cos_t, sin_t, kw["qkv"])
    (o, qx), carried = attn_fwd(q, k, v, job)
    if late is not None:
        kw = {**kw, **late(carried)}
    yg, states = ssd_fwd(px, pz, plast, kw["ssd"])
    x_mid = out_fwd(x, o, yg, g1, kw["wo"])
    (x_out, mix, h_mid, gate, up, *loss_part), carried_mixer = mlp_fwd(
        x_mid, kw["n2"], sh2, sc2, g2, kw["wgu"], kw["wd"], target, mixer_job)
    if mixer_job is not None:
        carried = (carried, carried_mixer)
    saved = dict(x=x, pa=pa, pz=pz, px=px, plast=plast, qx=qx, k=k, v=v, o=o, yg=yg, states=states, x_mid=x_mid,
                 mix=mix, h_mid=h_mid, gate=gate, up=up)
    return (x_out if target is None else (x_out, loss_part[0])), saved, kw, carried


def layer_bwd_head(dy, mod, kw, sv, job=None):
    _, _, g1, sh2, sc2, g2 = [mod[i:i + 1] for i in range(6)]
    (dhparts, dwg, dwu, dwd), carried = mlp_bwd(sv["h_mid"], dy, sv["gate"], sv["up"], g2, kw["wgu"], kw["wd"], job)
    dmid, dn2, dsh2, dsc2, do, dyg, dg1, dg2, dwo = out_bwd(
        dy, dhparts, sv["x_mid"], kw["n2"], sh2, sc2, sv["mix"], sv["o"], sv["yg"], g1, kw["wo"])
    early = dict(w_out=[dwo.reshape(N_DEV, D_MODEL // N_DEV, D_MODEL)], w_gate_up=[dwg, dwu],
                 w_down=[dwd.reshape(N_DEV, D_FF // N_DEV, D_MODEL)])
    head = dict(dmid=dmid, do=do, dyg=dyg, dn2=dn2, dsh2=dsh2, dsc2=dsc2, dg2=dg2, dg1=dg1)
    return head, early, carried


def layer_bwd_tail(hd, mod, kw, cos_t, sin_t, sv, job=None):
    sh1, sc1 = mod[0:1], mod[1:2]
    (dq, dk, dv), carried = attn_bwd(sv["qx"], sv["k"], sv["v"], hd["do"], job)
    dpx, dpz, dpl_dt, dcw, dcb, ddtb, dalog, ddskip, dsnw = ssd_bwd(sv["px"], sv["pz"], sv["plast"], sv["states"],
                                                                   hd["dyg"], kw["ssd"])
    dpa, dpl_k, dqaw, dkvaw, dwq, dwk, dwv, dqnw, dknw, dkpw = qkv_bwd(sv["pa"], sv["plast"], cos_t, sin_t, kw["qkv"],
                                                                       dq, dk, dv)
    dx, dn1, dsh1, dsc1, dwp = proj_bwd(sv["x"], kw["n1"], sh1, sc1, kw["w_proj"], dpa, dpz, dpx, dpl_k, dpl_dt, hd["dmid"])
    dmod = jnp.concatenate([dsh1, dsc1, hd["dg1"], hd["dsh2"], hd["dsc2"], hd["dg2"]], axis=0)
    dw_in = jnp.concatenate([dwp[:384], dwp[1984:2016], dwp[384:1920], dwp[1920:1928]], axis=0)
    grads = dict(
        norm1_w=dn1[0], norm2_w=hd["dn2"][0], q_a_norm_w=dqaw[0], kv_a_norm_w=dkvaw[0],
        q_nope_norm_w=dqnw[0, :NOPE], q_pe_norm_w=dqnw[0, NOPE:NOPE + ROPE], k_nope_norm_w=dknw[0, :NOPE],
        k_pe_norm_w=dkpw[0, NOPE:NOPE + ROPE], conv_b=dcb[0], dt_bias=ddtb[0, :N_HEADS], a_log=dalog[0, :N_HEADS],
        d_skip=ddskip[0, :N_HEADS], ssd_norm_w=dsnw[0],
        w_in=[dw_in.reshape(N_DEV, D_IN // N_DEV, D_MODEL)],
        w_q_up=[dwq[:, :, :NOPE + ROPE].astype(bf16)],
        w_kv_up=[jnp.concatenate([dwk[:, :, :NOPE], dwv], axis=2).astype(bf16)],
        conv_w=[dcw.reshape(4, N_DEV, D_CONV // N_DEV).transpose(1, 0, 2).astype(bf16)],
    )
    return dx, dmod, grads, carried


def _pack_small(get, last=None):
    flat = jnp.concatenate([get(name).reshape(-1) for name, _ in SMALL])
    flat = jnp.pad(flat, (0, SMALL_ROWS * 128 - flat.shape[0]))
    if last is not None:
        flat = flat.at[-1].set(last)
    return flat.reshape(SMALL_ROWS, 128)


def _unpack_small(packed):
    flat = packed.reshape(-1)
    out, off = {}, 0
    for name, size in SMALL:
        out[name] = flat[off:off + 2 * size].reshape(2, size)
        off += 2 * size
    return out


def kernel(x, c, positions, norm1_w, norm2_w, w_ada, b_ada, w_in, q_a_norm_w, w_q_up, kv_a_norm_w, w_kv_up, q_nope_norm_w, q_pe_norm_w, k_nope_norm_w, k_pe_norm_w, conv_w, conv_b, dt_bias, a_log, d_skip, ssd_norm_w, w_out, w_gate_up, w_down, loss_target, m_norm1_w, m_norm2_w, m_w_ada, m_b_ada, m_w_in, m_q_a_norm_w, m_w_q_up, m_kv_a_norm_w, m_w_kv_up, m_q_nope_norm_w, m_q_pe_norm_w, m_k_nope_norm_w, m_k_pe_norm_w, m_conv_w, m_conv_b, m_dt_bias, m_a_log, m_d_skip, m_ssd_norm_w, m_w_out, m_w_gate_up, m_w_down, v_norm1_w, v_norm2_w, v_w_ada, v_b_ada, v_w_in, v_q_a_norm_w, v_w_q_up, v_kv_a_norm_w, v_w_kv_up, v_q_nope_norm_w, v_q_pe_norm_w, v_k_nope_norm_w, v_k_pe_norm_w, v_conv_w, v_conv_b, v_dt_bias, v_a_log, v_d_skip, v_ssd_norm_w, v_w_out, v_w_gate_up, v_w_down):
    w = dict(norm1_w=norm1_w, norm2_w=norm2_w, w_ada=w_ada, b_ada=b_ada, w_in=w_in, q_a_norm_w=q_a_norm_w, w_q_up=w_q_up,
             kv_a_norm_w=kv_a_norm_w, w_kv_up=w_kv_up, q_nope_norm_w=q_nope_norm_w, q_pe_norm_w=q_pe_norm_w,
             k_nope_norm_w=k_nope_norm_w, k_pe_norm_w=k_pe_norm_w, conv_w=conv_w, conv_b=conv_b, dt_bias=dt_bias,
             a_log=a_log, d_skip=d_skip, ssd_norm_w=ssd_norm_w, w_out=w_out, w_gate_up=w_gate_up, w_down=w_down)
    m = dict(norm1_w=m_norm1_w, norm2_w=m_norm2_w, w_ada=m_w_ada, b_ada=m_b_ada, w_in=m_w_in, q_a_norm_w=m_q_a_norm_w,
             w_q_up=m_w_q_up, kv_a_norm_w=m_kv_a_norm_w, w_kv_up=m_w_kv_up, q_nope_norm_w=m_q_nope_norm_w,
             q_pe_norm_w=m_q_pe_norm_w, k_nope_norm_w=m_k_nope_norm_w, k_pe_norm_w=m_k_pe_norm_w, conv_w=m_conv_w,
             conv_b=m_conv_b, dt_bias=m_dt_bias, a_log=m_a_log, d_skip=m_d_skip, ssd_norm_w=m_ssd_norm_w, w_out=m_w_out,
             w_gate_up=m_w_gate_up, w_down=m_w_down)
    v = dict(norm1_w=v_norm1_w, norm2_w=v_norm2_w, w_ada=v_w_ada, b_ada=v_b_ada, w_in=v_w_in, q_a_norm_w=v_q_a_norm_w,
             w_q_up=v_w_q_up, kv_a_norm_w=v_kv_a_norm_w, w_kv_up=v_w_kv_up, q_nope_norm_w=v_q_nope_norm_w,
             q_pe_norm_w=v_q_pe_norm_w, k_nope_norm_w=v_k_nope_norm_w, k_pe_norm_w=v_k_pe_norm_w, conv_w=v_conv_w,
             conv_b=v_conv_b, dt_bias=v_dt_bias, a_log=v_a_log, d_skip=v_d_skip, ssd_norm_w=v_ssd_norm_w, w_out=v_w_out,
             w_gate_up=v_w_gate_up, w_down=v_w_down)
    me = _my_index()
    seq = x.shape[1]

    def shard(name, l):
        if name == "conv_w":
            return w[name][l]
        if name in TRANSPOSED:
            return jnp.swapaxes(w[name][l], 0, 1).astype(bf16)
        return w[name][l].astype(bf16)

    def shards(names, l):
        return [shard(name, l) for name in names]

    small = [{name: w[name][l] for name, _ in SMALL if name != "b_ada"} for l in range(2)]

    inv_freq = 1.0 / (ROPE_THETA ** (jnp.arange(0, ROPE, 2, dtype=f32) / ROPE))
    inv = _pad_lanes(jnp.concatenate([inv_freq, inv_freq]), NOPE)
    (cos_t, sin_t), first = rope_tables(positions.reshape(seq, 1), inv, gather_job([c] + shards(MIXER_WEIGHTS, 0)))
    c_all = first[0].reshape(N_DEV, D_MODEL)
    kws = [mixer_operands(dict(zip(MIXER_WEIGHTS, first[1:])), small[0]), None]

    b_cols = lax.dynamic_slice_in_dim(b_ada, me * 768, 768, axis=1)
    mod_cols = ada_fwd(c_all, w_ada, b_cols)
    (mod_all,) = comm_call(gather_job([mod_cols]), "gather_mod")
    mod_me = lax.dynamic_index_in_dim(mod_all, me, axis=2, keepdims=False)
    mods = [mod_me[:, l, :].reshape(6, D_MODEL) for l in range(2)]

    saved = [None, None]
    h, saved[0], kws[0], (_, got) = layer_fwd(
        x[0], mods[0], kws[0], cos_t, sin_t, gather_job(shards(LATE_WEIGHTS, 0)),
        lambda got: late_operands(dict(zip(LATE_WEIGHTS, got)), small[0]),
        mixer_job=gather_job(shards(MIXER_WEIGHTS, 1)))
    kws[1] = mixer_operands(dict(zip(MIXER_WEIGHTS, got)), small[1])
    (dy, loss_part), saved[1], kws[1], _ = layer_fwd(
        h, mods[1], kws[1], cos_t, sin_t, gather_job(shards(LATE_WEIGHTS, 1)),
        lambda got: late_operands(dict(zip(LATE_WEIGHTS, got)), small[1]), loss_target[0])

    early, late = ("w_out", "w_gate_up", "w_down"), ("w_in", "w_q_up", "w_kv_up", "conv_w")
    parts = [{}, {}]
    head, pieces, _ = layer_bwd_head(dy, mods[1], kws[1], saved[1])
    dy, dmod1, grads1, got = layer_bwd_tail(head, mods[1], kws[1], cos_t, sin_t, saved[1], scatter_job([pieces[n] for n in early]))
    parts[1].update(zip(early, got))
    head, pieces, got = layer_bwd_head(dy, mods[0], kws[0], saved[0], scatter_job([grads1[n] for n in late]))
    parts[1].update(zip(late, got))
    dy, dmod0, grads0, got = layer_bwd_tail(head, mods[0], kws[0], cos_t, sin_t, saved[0], scatter_job([pieces[n] for n in early]))
    parts[0].update(zip(early, got))
    grad_x = dy[None]

    small_part = {name: jnp.stack([grads0[name], grads1[name]]) for name, _ in SMALL if name != "b_ada"}
    small_part["b_ada"] = jnp.stack([dmod0.reshape(-1), dmod1.reshape(-1)])
    swap = lambda a: jnp.swapaxes(a, 1, 2)
    updated, last = adamw_two_layers(
        [parts[l]["w_gate_up"] for l in range(2)], swap(w_gate_up), swap(m_w_gate_up), swap(v_w_gate_up), "adamw_w_gate_up_comm",
        merge_jobs(scatter_job([grads0[n] for n in late]),
                   gather_job([_pack_small(lambda n: small_part[n], loss_part[0, 0])])))
    parts[0].update(zip(late, last[:len(late)]))
    small_all = last[len(late)]
    packed = adamw(small_all, _pack_small(lambda n: w[n])[None], _pack_small(lambda n: m[n])[None],
                   _pack_small(lambda n: v[n])[None], 0, None, "adamw_small")
    loss = packed[0][0, -1, -1]
    res = {}
    for key, arr in zip("gdmv", packed):
        for name, val in _unpack_small(arr[0]).items():
            res[key, name] = val

    off = 2 * (1024 + 1024)
    dmod_all = small_all.reshape(N_DEV, -1)[:, off:off + 2 * 6144].reshape(N_DEV, 2, 6144)
    dmod_cols = lax.dynamic_slice_in_dim(dmod_all, me * 768, 768, axis=2).transpose(1, 0, 2)
    g_ada = ada_bwd(c_all, dmod_cols)
    out = None
    for l in range(2):
        out = adamw(g_ada[l][None], w_ada, m_w_ada, v_w_ada, l, out, "adamw_w_ada")
    res.update(zip([(key, "w_ada") for key in "gdmv"], out))

    inside = lambda a: jnp.transpose(a, (2, 0, 1))
    out = adamw_layers_inside([parts[l]["w_in"] for l in range(2)], inside(w_in), inside(m_w_in), inside(v_w_in), "adamw_w_in")
    res.update(zip([(key, "w_in") for key in "gdmv"], [jnp.transpose(a, (1, 2, 0)) for a in out]))
    res.update(zip([(key, "w_gate_up") for key in "gdmv"], [swap(a) for a in updated]))
    for name in BIG:
        if name in ("w_in", "w_gate_up"):
            continue
        view = (lambda a: jnp.swapaxes(a, 1, 2)) if name in TRANSPOSED else (lambda a: a)
        out = None
        for l in range(2):
            out = adamw(parts[l][name], view(w[name]), view(m[name]), view(v[name]), l, out, "adamw_" + name)
        res.update(zip([(key, name) for key in "gdmv"], [view(a) for a in out]))

    return (loss, grad_x, *[res["g", n] for n in WEIGHTS], *[res["d", n] for n in WEIGHTS],
            *[res["m", n] for n in WEIGHTS], *[res["v", n] for n in WEIGHTS])
```

```python
import functools

import jax
import jax.numpy as jnp
from jax import lax
from jax.experimental import pallas as pl
from jax.experimental.pallas import tpu as pltpu

f32 = jnp.float32
bf16 = jnp.bfloat16

N_DEV = 8
D_MODEL = 1024
N_HEADS = 8
HEAD_LANES = 128
NOPE = 64
ROPE = 32
V_DIM = 64
Q_RANK = 256
KV_RANK = 128
D_SSD = 512
D_CONV = 1024
SSD_STATE = 128
SSD_HEAD_DIM = 64
CHUNK = 128
HALO = 8
D_FF = 2816
FF_SHARD = 704
D_IN = 1960
D_PROJ = 2048
EPS = 1e-6
LOG2E = 1.4426950408889634
LN2 = 0.6931471805599453
Q_SCALE = (NOPE + ROPE) ** -0.5 * LOG2E
SPARE_Q = NOPE + ROPE
SPARE_V = V_DIM
ATTN_BLOCK_FWD = 4096
ATTN_ROWS_FWD = 256
ATTN_HEADS_FWD = 2
ATTN_HEADS_BWD = 4
MLP_FWD_ROWS = 1024
MLP_BWD_CHUNK = 256
MLP_BWD_ROWS = 1024
ROPE_THETA = 10000.0
NEG = -1e30

ADAM_LR = 0.001
ADAM_B1 = 0.9
ADAM_B2 = 0.999
ADAM_EPS = 1e-08
ADAM_WD = 0.01
ADAM_STEP = 10
ADAMW_BLOCK_BYTES = 36 << 20

MESH = pl.DeviceIdType.MESH
ANY = pl.BlockSpec(memory_space=pl.ANY)

SMALL = (("norm1_w", 1024), ("norm2_w", 1024), ("b_ada", 6144), ("q_a_norm_w", 256), ("kv_a_norm_w", 128),
         ("q_nope_norm_w", 64), ("q_pe_norm_w", 32), ("k_nope_norm_w", 64), ("k_pe_norm_w", 32),
         ("conv_b", 1024), ("dt_bias", 8), ("a_log", 8), ("d_skip", 8), ("ssd_norm_w", 512))
SMALL_ROWS = 168
BIG = ("w_in", "w_q_up", "w_kv_up", "conv_w", "w_out", "w_gate_up", "w_down")
TRANSPOSED = ("w_in", "w_gate_up")
WEIGHTS = ("norm1_w", "norm2_w", "w_ada", "b_ada", "w_in", "q_a_norm_w", "w_q_up", "kv_a_norm_w", "w_kv_up",
           "q_nope_norm_w", "q_pe_norm_w", "k_nope_norm_w", "k_pe_norm_w", "conv_w", "conv_b", "dt_bias",
           "a_log", "d_skip", "ssd_norm_w", "w_out", "w_gate_up", "w_down")


def _dot(a, b, ca, cb):
    return lax.dot_general(a.astype(bf16), b.astype(bf16), (((ca,), (cb,)), ((), ())), preferred_element_type=f32)


@jax.custom_vjp
def mm(a, b):
    return _dot(a, b, 1, 0)


def _mm_fwd(a, b):
    return _dot(a, b, 1, 0), (a, b)


def _mm_bwd(res, g):
    a, b = res
    return _dot(g, b, 1, 1).astype(a.dtype), _dot(a, g, 0, 0).astype(b.dtype)


mm.defvjp(_mm_fwd, _mm_bwd)


@jax.custom_vjp
def _mm_slot(a, w, slot):
    return _dot(a, w, 1, 0)


def _mm_slot_fwd(a, w, slot):
    return _dot(a, w, 1, 0), (a, w)


def _mm_slot_bwd(res, g):
    a, w = res
    return _dot(g, w, 1, 1).astype(a.dtype), None, _dot(a, g, 0, 0)


_mm_slot.defvjp(_mm_slot_fwd, _mm_slot_bwd)


def mmw(a, w, slot=None):
    return _dot(a, w, 1, 0) if slot is None else _mm_slot(a, w, slot)


@jax.custom_vjp
def _mm_slot_t(a, wt, slot):
    return _dot(a, wt, 1, 1)


def _mm_slot_t_fwd(a, wt, slot):
    return _dot(a, wt, 1, 1), (a, wt)


def _mm_slot_t_bwd(res, g):
    a, wt = res
    return _dot(g, wt, 1, 0).astype(a.dtype), None, _dot(g, a, 0, 0)


_mm_slot_t.defvjp(_mm_slot_t_fwd, _mm_slot_t_bwd)


def mmw_t(a, wt, slot=None):
    return _dot(a, wt, 1, 1) if slot is None else _mm_slot_t(a, wt, slot)


@jax.custom_vjp
def mm_nt(a, b):
    return _dot(a, b, 1, 1)


def _mm_nt_fwd(a, b):
    return _dot(a, b, 1, 1), (a, b)


def _mm_nt_bwd(res, g):
    a, b = res
    return _dot(g, b, 1, 0).astype(a.dtype), _dot(g, a, 0, 0).astype(b.dtype)


mm_nt.defvjp(_mm_nt_fwd, _mm_nt_bwd)


@jax.custom_vjp
def mm_tn(a, b):
    return _dot(a, b, 0, 0)


def _mm_tn_fwd(a, b):
    return _dot(a, b, 0, 0), (a, b)


def _mm_tn_bwd(res, g):
    a, b = res
    return _dot(b, g, 1, 1).astype(a.dtype), _dot(a, g, 1, 0).astype(b.dtype)


mm_tn.defvjp(_mm_tn_fwd, _mm_tn_bwd)


def _rms(x, w):
    return x * lax.rsqrt(jnp.mean(x * x, axis=-1, keepdims=True) + EPS) * w


def _const(shape):
    n = len(shape)
    return pl.BlockSpec(shape, lambda *_: (0,) * n)


def _accumulate(first, refs, vals):
    @pl.when(first)
    def _():
        for r, v in zip(refs, vals):
            r[...] = v

    @pl.when(jnp.logical_not(first))
    def _():
        for r, v in zip(refs, vals):
            r[...] += v


def _accumulate_then_cast(first, last, accs, outs, vals):
    _accumulate(first, accs, vals)

    @pl.when(last)
    def _():
        for a, o in zip(accs, outs):
            o[...] = a[...].astype(o.dtype)


def _token_block(s):
    return min(512, s)


def _f_proj(x, nw, sh, sc, w, slot=None):
    h = _rms(x, nw) * (1.0 + sc) + sh
    return mmw_t(h, w, slot)


def _f_qkv(pa, plast, cos_t, sin_t, qaw, kvaw, wq, wk, wv, qnw, knw, kpw, slots=None):
    sq, sk, sv = slots if slots is not None else ([None] * N_HEADS,) * 3
    lane = lax.broadcasted_iota(jnp.int32, (1, HEAD_LANES), 1)
    m_nope = lane < NOPE
    m_pe = (lane >= NOPE) & (lane < NOPE + ROPE)
    rows = pa.shape[0]

    def rope(t):
        half = ROPE // 2
        swapped = jnp.concatenate(
            [jnp.zeros((rows, NOPE), f32), t[:, NOPE + half:NOPE + ROPE], t[:, NOPE:NOPE + half],
             jnp.zeros((rows, HEAD_LANES - NOPE - ROPE), f32)], axis=1)
        return t * cos_t + swapped * sin_t

    qa = _rms(pa[:, :Q_RANK], qaw)
    kva = _rms(pa[:, Q_RANK:Q_RANK + KV_RANK], kvaw)
    kp = jnp.where(m_pe, plast, 0.0)
    kp = kp * lax.rsqrt(jnp.sum(kp * kp, axis=-1, keepdims=True) / ROPE + EPS) * kpw
    k_rot = rope(kp)
    qs, ks, vs = [], [], []
    for h in range(N_HEADS):
        qh = mmw(qa, wq[h], sq[h])
        ss_n = jnp.sum(jnp.where(m_nope, qh * qh, 0.0), axis=-1, keepdims=True) / NOPE
        ss_p = jnp.sum(jnp.where(m_pe, qh * qh, 0.0), axis=-1, keepdims=True) / ROPE
        r = jnp.where(m_nope, lax.rsqrt(ss_n + EPS), lax.rsqrt(ss_p + EPS))
        qs.append(rope(qh * r * qnw) * Q_SCALE)
        kh = mmw(kva, wk[h], sk[h])
        kh = kh * lax.rsqrt(jnp.sum(kh * kh, axis=-1, keepdims=True) / NOPE + EPS) * knw
        ks.append(kh + k_rot)
        vs.append(mmw(kva, wv[h], sv[h]))
    return jnp.stack(qs), jnp.stack(ks), jnp.stack(vs)


def _f_ssd(xext, z, plast, prev, cw, cb, dtb, alog, dskip, snw):
    n = CHUNK
    conv = cb
    for k in range(4):
        conv = conv + cw[k:k + 1] * xext[HALO - 3 + k:HALO - 3 + k + n]
    xc = jax.nn.silu(conv)
    xs, bm, cm = xc[:, :D_SSD], xc[:, D_SSD:D_SSD + 2 * SSD_STATE], xc[:, D_SSD + 2 * SSD_STATE:]
    lane = lax.broadcasted_iota(jnp.int32, (1, 128), 1)
    dt = jax.nn.softplus(jnp.where(lane < N_HEADS, plast, 0.0) + dtb)
    adt = dt * (-jnp.exp(alog))
    row = lax.broadcasted_iota(jnp.int32, (n, n), 0)
    col = lax.broadcasted_iota(jnp.int32, (n, n), 1)
    tri = row >= col
    acs = jnp.dot(tri.astype(f32), adt, precision=lax.Precision.HIGHEST, preferred_element_type=f32)
    acs_t = acs.T
    bgs = [bm[:, g * SSD_STATE:(g + 1) * SSD_STATE] for g in range(2)]
    cgs = [cm[:, g * SSD_STATE:(g + 1) * SSD_STATE] for g in range(2)]
    cb_ts = [mm_nt(cgs[g], bgs[g]) for g in range(2)]
    low = lane < SSD_HEAD_DIM
    low_rows = lax.broadcasted_iota(jnp.int32, (2 * SSD_HEAD_DIM, 1), 0) < SSD_HEAD_DIM

    def both(a0, a1):
        return jnp.where(low, a0, a1)

    pre = []
    for i in range(N_HEADS // 2):
        h0, h1 = 2 * i, 2 * i + 1
        col0, col1 = acs[:, h0:h0 + 1], acs[:, h1:h1 + 1]
        last0, last1 = acs[n - 1:n, h0:h0 + 1], acs[n - 1:n, h1:h1 + 1]
        cb_t = cb_ts[i // 2]
        scores0 = cb_t * jnp.exp(jnp.where(tri, col0 - acs_t[h0:h0 + 1, :], -jnp.inf))
        scores1 = cb_t * jnp.exp(jnp.where(tri, col1 - acs_t[h1:h1 + 1, :], -jnp.inf))
        xp = xs[:, i * 128:(i + 1) * 128]
        xdt = xp * both(dt[:, h0:h0 + 1], dt[:, h1:h1 + 1])
        weighted = xdt * both(jnp.exp(last0 - col0), jnp.exp(last1 - col1))
        chunk_decay = jnp.where(low_rows, jnp.exp(last0), jnp.exp(last1))
        in_decay = both(jnp.exp(col0), jnp.exp(col1))
        skip = both(dskip[:, h0:h0 + 1], dskip[:, h1:h1 + 1]) * xp
        pre.append((scores0, scores1, xdt, weighted, chunk_decay, in_decay, skip))
    prods = []
    for i in range(N_HEADS // 2):
        scores0, scores1, xdt, weighted, _, _, _ = pre[i]
        g = i // 2
        y_diag = mm(scores0, jnp.where(low, xdt, 0.0)) + mm(scores1, jnp.where(low, 0.0, xdt))
        prods.append((y_diag, mm_tn(weighted, bgs[g]), mm_nt(cgs[g], prev[i])))
    ys, news = [], []
    for i in range(N_HEADS // 2):
        y_diag, st, y_off = prods[i]
        _, _, _, _, chunk_decay, in_decay, skip = pre[i]
        news.append(chunk_decay * prev[i] + st)
        ys.append(y_diag + y_off * in_decay + skip)
    y = jnp.concatenate(ys, axis=1)
    yg = y * jax.nn.silu(z)
    half = D_SSD // 2
    outs = []
    for g in range(2):
        t = yg[:, g * half:(g + 1) * half]
        outs.append(t * lax.rsqrt(jnp.mean(t * t, axis=-1, keepdims=True) + EPS))
    return jnp.concatenate(outs, axis=1) * snw, jnp.stack(news)


def _f_out(o, yg, g1, wo, slot=None):
    cat = jnp.concatenate([o[h] for h in range(N_HEADS)] + [yg], axis=1)
    return g1 * mmw(cat, wo, slot)


def _f_modulate(x, nw, sh, sc):
    return _rms(x, nw) * (1.0 + sc) + sh


def proj_fwd(x, nw, sh, sc, w):
    s = x.shape[0]
    ts = _token_block(s)

    def body(x_ref, nw_ref, sh_ref, sc_ref, w_ref, pa_ref, pz_ref, px_ref, pl_ref):
        p = _f_proj(x_ref[...], nw_ref[...], sh_ref[...], sc_ref[...], w_ref[...])
        pa_ref[...] = p[:, :384]
        pz_ref[...] = p[:, 384:896]
        px_ref[...] = p[:, 896:1920]
        pl_ref[...] = p[:, 1920:]

    vec = _const((1, D_MODEL))
    return pl.pallas_call(
        body, name="proj_fwd", grid=(s // ts,),
        in_specs=[pl.BlockSpec((ts, D_MODEL), lambda i: (i, 0)), vec, vec, vec, _const((D_PROJ, D_MODEL))],
        out_specs=[pl.BlockSpec((ts, 384), lambda i: (i, 0)), pl.BlockSpec((ts, 512), lambda i: (i, 0)),
                   pl.BlockSpec((ts, 1024), lambda i: (i, 0)), pl.BlockSpec((ts, 128), lambda i: (i, 0))],
        out_shape=[jax.ShapeDtypeStruct((s, 384), f32), jax.ShapeDtypeStruct((s, 512), f32),
                   jax.ShapeDtypeStruct((s, 1024), f32), jax.ShapeDtypeStruct((s, 128), f32)],
    )(x, nw, sh, sc, w)


def rope_tables(pos, inv, job=None):
    s = pos.shape[0]
    ts = _token_block(s)

    def body(pos_ref, inv_ref, cos_ref, sin_ref):
        ang = pos_ref[...].astype(f32) * inv_ref[...]
        lane = lax.broadcasted_iota(jnp.int32, (1, HEAD_LANES), 1)
        half = ROPE // 2
        cos_ref[...] = jnp.where(lane < NOPE, 1.0, jnp.where(lane < NOPE + ROPE, jnp.cos(ang), 0.0))
        sn = jnp.sin(ang)
        sin_ref[...] = jnp.where((lane >= NOPE) & (lane < NOPE + half), -sn,
                                 jnp.where((lane >= NOPE + half) & (lane < NOPE + ROPE), sn, 0.0))

    steps = s // ts
    return _call_with_job(
        body, "rope_tables" if job is None else "rope_tables_comm", (steps,), job,
        in_specs=[pl.BlockSpec((ts, 1), lambda i: (i, 0)), _const((1, HEAD_LANES))],
        out_specs=[pl.BlockSpec((ts, HEAD_LANES), lambda i: (i, 0))] * 2,
        out_shape=[jax.ShapeDtypeStruct((s, HEAD_LANES), f32)] * 2, scratch_shapes=[], operands=(pos, inv),
        relay_at=(max(steps - 2, 0),))


def _qkv_param_specs():
    return [_const((1, Q_RANK)), _const((1, KV_RANK)), _const((N_HEADS, Q_RANK, HEAD_LANES)),
            _const((N_HEADS, KV_RANK, HEAD_LANES)), _const((N_HEADS, KV_RANK, V_DIM)),
            _const((1, HEAD_LANES)), _const((1, HEAD_LANES)), _const((1, HEAD_LANES))]


def qkv_fwd(pa, plast, cos_t, sin_t, params):
    s = pa.shape[0]
    ts = _token_block(s)

    def body(pa_ref, pl_ref, cos_ref, sin_ref, *rest):
        prm = [r[...] for r in rest[:8]]
        q_ref, k_ref, v_ref = rest[8:]
        q, k, v = _f_qkv(pa_ref[...], pl_ref[...], cos_ref[...], sin_ref[...], *prm)
        q_ref[...] = q.astype(bf16)
        lane = lax.broadcasted_iota(jnp.int32, (1, 1, HEAD_LANES), 2)
        k_ref[...] = jnp.where((lane == SPARE_Q) | (lane == SPARE_Q + 1), 1.0, k).astype(bf16)
        v_ref[...] = jnp.concatenate([v, jnp.ones_like(v)], axis=-1).astype(bf16)

    tok = lambda w: pl.BlockSpec((ts, w), lambda i: (i, 0))
    head = pl.BlockSpec((N_HEADS, ts, HEAD_LANES), lambda i: (0, i, 0))
    return pl.pallas_call(
        body, name="qkv_fwd", grid=(s // ts,),
        in_specs=[tok(384), tok(128), tok(128), tok(128)] + _qkv_param_specs(),
        out_specs=[head] * 3, out_shape=[jax.ShapeDtypeStruct((N_HEADS, s, HEAD_LANES), bf16)] * 3,
    )(pa, plast, cos_t, sin_t, *params)


def _scores(q, k):
    return lax.dot_general(q, k, (((1,), (1,)), ((), ())), preferred_element_type=f32)


def _tril(rows, cols, row_offset):
    row = row_offset + lax.broadcasted_iota(jnp.int32, (rows, cols), 0)
    col = lax.broadcasted_iota(jnp.int32, (rows, cols), 1)
    return row >= col


def _call_with_job(body, name, grid, job, in_specs, out_specs, out_shape, scratch_shapes, operands, relay_at=None):
    if job is None:
        res = pl.pallas_call(body, name=name, grid=grid, in_specs=in_specs, out_specs=out_specs, out_shape=out_shape,
                             scratch_shapes=scratch_shapes)(*operands)
        return res, None

    def at_step(i, n):
        if i == 0:
            want = [0] * len(grid)
        elif i == n - 1:
            want = [g - 1 for g in grid]
        else:
            want = relay_at
        return functools.reduce(jnp.logical_and, [pl.program_id(a) == s for a, s in enumerate(want)])

    carrier = _carry(job, body, len(in_specs), len(out_specs), at_step)
    res = pl.pallas_call(
        carrier, name=name, grid=grid,
        in_specs=list(in_specs) + [ANY] * len(job.operands), out_specs=list(out_specs) + [ANY] * len(job.out_shape),
        out_shape=list(out_shape) + list(job.out_shape), scratch_shapes=list(scratch_shapes) + job.scratch,
    )(*operands, *job.operands)
    return res[:len(out_specs)], res[len(out_specs):]


def attn_fwd(q, k, v, job=None):
    s = q.shape[1]
    t = min(ATTN_BLOCK_FWD, s)
    nb = s // t

    rb = min(ATTN_ROWS_FWD, t)

    hp = ATTN_HEADS_FWD

    def body(q_ref, k_ref, v_ref, o_ref, qx_ref, m_sc, acc_sc):
        qi = pl.program_id(1)
        m_sc[...] = jnp.full(m_sc.shape, NEG, f32)
        acc_sc[...] = jnp.zeros(acc_sc.shape, f32)

        def step(k0, diagonal):
            chains = [(hh, r) for hh in range(hp) for r in range(t // rb)]

            def scores(hh, r):
                nk = (r + 1) * rb if diagonal else t
                sc = _scores(q_ref[hh, pl.ds(r * rb, rb), :], k_ref[hh, pl.ds(k0, nk), :])
                return jnp.where(_tril(rb, nk, r * rb), sc, NEG) if diagonal else sc

            ahead = scores(*chains[0])
            for c, (hh, r) in enumerate(chains):
                sc = ahead
                if c + 1 < len(chains):
                    ahead = scores(*chains[c + 1])
                rows = pl.ds(r * rb, rb)
                keys = pl.ds(k0, (r + 1) * rb if diagonal else t)
                m_prev = m_sc[hh, rows, :1]
                m_new = jnp.maximum(m_prev, jnp.max(sc, axis=-1, keepdims=True))
                p = jnp.exp2(sc - m_new)
                alpha = jnp.exp2(m_prev - m_new)
                acc = alpha * acc_sc[hh, rows, :] + jnp.dot(p.astype(bf16), v_ref[hh, keys, :], preferred_element_type=f32)
                if diagonal:
                    l = acc[:, V_DIM:V_DIM + 1]
                    o_ref[hh, rows, :] = acc[:, :V_DIM] / l
                    lse = m_new + jnp.log2(l)
                    high = lse.astype(bf16)
                    low = (lse - high.astype(f32)).astype(bf16)
                    lane = lax.broadcasted_iota(jnp.int32, (1, HEAD_LANES), 1)
                    qx_ref[hh, rows, :] = jnp.where(lane == SPARE_Q, -high,
                                                    jnp.where(lane == SPARE_Q + 1, -low, q_ref[hh, rows, :]))
                else:
                    acc_sc[hh, rows, :] = acc
                    m_sc[hh, rows, :] = jnp.broadcast_to(m_new, (rb, 128))

        def below(ki, carry):
            step(pl.multiple_of(ki * t, t), False)
            return carry

        lax.fori_loop(0, qi, below, 0)
        step(pl.multiple_of(qi * t, t), True)

    return _call_with_job(
        body, "attn_fwd" if job is None else "attn_fwd_comm", (N_HEADS // hp, nb), job,
        in_specs=[pl.BlockSpec((hp, t, HEAD_LANES), lambda h, qi: (h, qi, 0)),
                  pl.BlockSpec((hp, s, HEAD_LANES), lambda h, qi: (h, 0, 0)),
                  pl.BlockSpec((hp, s, HEAD_LANES), lambda h, qi: (h, 0, 0))],
        out_specs=[pl.BlockSpec((hp, t, V_DIM), lambda h, qi: (h, qi, 0)),
                   pl.BlockSpec((hp, t, HEAD_LANES), lambda h, qi: (h, qi, 0))],
        out_shape=[jax.ShapeDtypeStruct((N_HEADS, s, V_DIM), f32), jax.ShapeDtypeStruct((N_HEADS, s, HEAD_LANES), bf16)],
        scratch_shapes=[pltpu.VMEM((hp, t, 128), f32), pltpu.VMEM((hp, t, HEAD_LANES), f32)],
        operands=(q, k, v), relay_at=(N_HEADS // hp - 1, nb - 1))


def _ssd_param_specs():
    return [_const((4, D_CONV)), _const((1, D_CONV)), _const((1, 128)), _const((1, 128)), _const((1, 128)),
            _const((1, D_SSD))]


def ssd_fwd(px, pz, plast, params):
    s = px.shape[0]
    nc = s // CHUNK

    def body(px_ref, pz_ref, pl_ref, cw_ref, cb_ref, dtb_ref, alog_ref, dskip_ref, snw_ref, yg_ref, st_ref,
             state_sc, halo_sc):
        i = pl.program_id(0)

        @pl.when(i == 0)
        def _():
            state_sc[...] = jnp.zeros(state_sc.shape, f32)
            halo_sc[...] = jnp.zeros(halo_sc.shape, f32)

        x = px_ref[...]
        prev = state_sc[...]
        st_ref[...] = prev
        xext = jnp.concatenate([halo_sc[...], x], axis=0)
        yg, new = _f_ssd(xext, pz_ref[...], pl_ref[...], prev, cw_ref[...], cb_ref[...], dtb_ref[...],
                         alog_ref[...], dskip_ref[...], snw_ref[...])
        yg_ref[...] = yg
        state_sc[...] = new
        halo_sc[...] = x[CHUNK - HALO:]

    tok = lambda w: pl.BlockSpec((CHUNK, w), lambda i: (i, 0))
    return pl.pallas_call(
        body, name="ssd_fwd", grid=(nc,),
        in_specs=[tok(D_CONV), tok(D_SSD), tok(128)] + _ssd_param_specs(),
        out_specs=[tok(D_SSD), pl.BlockSpec((None, N_HEADS // 2, 2 * SSD_HEAD_DIM, SSD_STATE), lambda i: (i, 0, 0, 0))],
        out_shape=[jax.ShapeDtypeStruct((s, D_SSD), f32),
                   jax.ShapeDtypeStruct((nc, N_HEADS // 2, 2 * SSD_HEAD_DIM, SSD_STATE), f32)],
        scratch_shapes=[pltpu.VMEM((N_HEADS // 2, 2 * SSD_HEAD_DIM, SSD_STATE), f32), pltpu.VMEM((HALO, D_CONV), f32)],
    )(px, pz, plast, *params)


def out_fwd(x, o, yg, g1, wo):
    s = x.shape[0]
    ts = _token_block(s)

    def body(x_ref, o_ref, yg_ref, g1_ref, wo_ref, out_ref):
        out_ref[...] = x_ref[...] + _f_out(o_ref[...], yg_ref[...], g1_ref[...], wo_ref[...])

    return pl.pallas_call(
        body, name="out_fwd", grid=(s // ts,),
        in_specs=[pl.BlockSpec((ts, D_MODEL), lambda i: (i, 0)), pl.BlockSpec((N_HEADS, ts, V_DIM), lambda i: (0, i, 0)),
                  pl.BlockSpec((ts, D_SSD), lambda i: (i, 0)), _const((1, D_MODEL)), _const((D_MODEL, D_MODEL))],
        out_specs=pl.BlockSpec((ts, D_MODEL), lambda i: (i, 0)),
        out_shape=jax.ShapeDtypeStruct((s, D_MODEL), f32),
    )(x, o, yg, g1, wo)


def mlp_fwd(x, nw, sh, sc, g2, wgu, wd, target=None, job=None):
    s = x.shape[0]
    ts = min(MLP_FWD_ROWS, s)
    nj = N_DEV // 2

    def body(x_ref, nw_ref, sh_ref, sc_ref, g2_ref, wg_ref, wu_ref, wd_ref, *rest):
        if target is None:
            out_ref, mix_ref, h_ref, gate_ref, up_ref = rest
        else:
            t_ref, out_ref, mix_ref, h_ref, gate_ref, up_ref, loss_ref = rest
        j = pl.program_id(1)
        first_block = pl.program_id(0) == 0

        @pl.when(j == 0)
        def _():
            h_ref[...] = _f_modulate(x_ref[...], nw_ref[...], sh_ref[...], sc_ref[...]).astype(bf16)
            mix_ref[...] = jnp.zeros(mix_ref.shape, f32)

        nr = max(ts // 512, 1)
        half = ts // nr
        wg, wu, wd = wg_ref[...], wu_ref[...], wd_ref[...]
        products = lambda r: (mmw_t(h_ref[pl.ds(r * half, half), :], wg), mmw_t(h_ref[pl.ds(r * half, half), :], wu))
        ahead = products(0)
        for r in range(nr):
            gate, up = ahead
            if r + 1 < nr:
                ahead = products(r + 1)
            rows = pl.ds(r * half, half)
            gate_ref[rows, :] = gate.astype(bf16)
            up_ref[rows, :] = up.astype(bf16)
            mix_ref[rows, :] += mmw(jax.nn.silu(gate) * up, wd)

        @pl.when(j == nj - 1)
        def _():
            y = x_ref[...] + g2_ref[...] * mix_ref[...]
            if target is None:
                out_ref[...] = y
            else:
                d = y - t_ref[...]
                out_ref[...] = d * (1.0 / D_MODEL)
                part = 0.5 * jnp.sum(jnp.sum(d * d, axis=-1, keepdims=True) * (1.0 / D_MODEL), axis=0, keepdims=True)
                _accumulate(first_block, [loss_ref], [jnp.broadcast_to(part, (8, 128))])

    vec = _const((1, D_MODEL))
    tok = pl.BlockSpec((ts, D_MODEL), lambda i, j: (i, 0))
    wide = pl.BlockSpec((None, ts, FF_SHARD), lambda i, j: (j, i, 0))
    last = target is not None
    name = ("mlp_fwd_loss" if last else "mlp_fwd") + ("" if job is None else "_comm")
    return _call_with_job(
        body, name, (s // ts, nj), job,
        in_specs=[tok, vec, vec, vec, vec,
                  pl.BlockSpec((None, FF_SHARD, D_MODEL), lambda i, j: (j, 0, 0)),
                  pl.BlockSpec((None, FF_SHARD, D_MODEL), lambda i, j: (j + nj, 0, 0)),
                  pl.BlockSpec((None, FF_SHARD, D_MODEL), lambda i, j: (j, 0, 0))] + [tok] * last,
        out_specs=[tok] * 3 + [wide] * 2 + [_const((8, 128))] * last,
        out_shape=[jax.ShapeDtypeStruct((s, D_MODEL), f32), jax.ShapeDtypeStruct((s, D_MODEL), f32),
                   jax.ShapeDtypeStruct((s, D_MODEL), bf16)] + [jax.ShapeDtypeStruct((nj, s, FF_SHARD), bf16)] * 2
                  + [jax.ShapeDtypeStruct((8, 128), f32)] * last,
        scratch_shapes=[], operands=(x, nw, sh, sc, g2, wgu, wgu, wd, *([target] if last else [])),
        relay_at=(s // ts - 1, 0))


def mlp_bwd(h, dy, gate, up, g2, wgu, wd, job=None):
    s = h.shape[0]
    ts = min(MLP_BWD_ROWS, s)
    nj = N_DEV // 2
    ni = s // ts

    rows_per = min(MLP_BWD_CHUNK, ts)

    def body(h_ref, dy_ref, gate_ref, up_ref, g2_ref, wg_ref, wu_ref, wd_ref, dh_ref, dwg_ref, dwu_ref, dwd_ref,
             ag_sc, au_sc, ad_sc, act_sc, dgate_sc, dup_sc, dmix_sc):
        i = pl.program_id(1)
        wg, wu, wd = wg_ref[...], wu_ref[...], wd_ref[...]
        g2 = g2_ref[...]
        for r in range(ts // rows_per):
            rows = pl.ds(r * rows_per, rows_per)
            act, vjp = jax.vjp(lambda g, u: jax.nn.silu(g) * u, gate_ref[rows, :].astype(f32), up_ref[rows, :].astype(f32))
            dmix = (dy_ref[rows, :] * g2).astype(bf16)
            dgate, dup = vjp(_dot(dmix, wd, 1, 1))
            dgate, dup = dgate.astype(bf16), dup.astype(bf16)
            dh_ref[rows, :] = (_dot(dgate, wg, 1, 0) + _dot(dup, wu, 1, 0)).astype(bf16)
            act_sc[rows, :] = act.astype(bf16)
            dgate_sc[rows, :] = dgate
            dup_sc[rows, :] = dup
            dmix_sc[rows, :] = dmix
        h = h_ref[...]
        grads = [_dot(dgate_sc[...], h, 0, 0), _dot(dup_sc[...], h, 0, 0), _dot(act_sc[...], dmix_sc[...], 0, 0)]
        _accumulate_then_cast(i == 0, i == ni - 1, [ag_sc, au_sc, ad_sc], [dwg_ref, dwu_ref, dwd_ref], grads)

    once = pl.Buffered(1)
    wspec = lambda off: pl.BlockSpec((None, FF_SHARD, D_MODEL), lambda j, i: (j + off, 0, 0), pipeline_mode=once)
    dspec = pl.BlockSpec((None, FF_SHARD, D_MODEL), lambda j, i: (j, 0, 0), pipeline_mode=once)
    wide = pl.BlockSpec((None, ts, FF_SHARD), lambda j, i: (j, i, 0))
    return _call_with_job(
        body, "mlp_bwd" if job is None else "mlp_bwd_comm", (nj, ni), job,
        in_specs=[pl.BlockSpec((ts, D_MODEL), lambda j, i: (i, 0)), pl.BlockSpec((ts, D_MODEL), lambda j, i: (i, 0)),
                  wide, wide, _const((1, D_MODEL)), wspec(0), wspec(nj), dspec],
        out_specs=[pl.BlockSpec((None, ts, D_MODEL), lambda j, i: (j, i, 0)), wspec(0), wspec(0), dspec],
        out_shape=[jax.ShapeDtypeStruct((nj, s, D_MODEL), bf16),
                   jax.ShapeDtypeStruct((nj, FF_SHARD, D_MODEL), bf16), jax.ShapeDtypeStruct((nj, FF_SHARD, D_MODEL), bf16),
                   jax.ShapeDtypeStruct((nj, FF_SHARD, D_MODEL), bf16)],
        scratch_shapes=[pltpu.VMEM((FF_SHARD, D_MODEL), f32), pltpu.VMEM((FF_SHARD, D_MODEL), f32),
                        pltpu.VMEM((FF_SHARD, D_MODEL), f32), pltpu.VMEM((ts, FF_SHARD), bf16),
                        pltpu.VMEM((ts, FF_SHARD), bf16), pltpu.VMEM((ts, FF_SHARD), bf16), pltpu.VMEM((ts, D_MODEL), bf16)],
        operands=(h, dy, gate, up, g2, wgu, wgu, wd))


def out_bwd(dy, dhparts, x, nw, sh, sc, mix, o, yg, g1, wo):
    s = dy.shape[0]
    ts = _token_block(s)
    nj = dhparts.shape[0]

    ni = s // ts

    def body(dy_ref, dp_ref, x_ref, nw_ref, sh_ref, sc_ref, mix_ref, o_ref, yg_ref, g1_ref, wo_ref,
             dx_ref, dnw_ref, dsh_ref, dsc_ref, do_ref, dyg_ref, dg1_ref, dg2_ref, dwo_ref, acc_sc):
        i = pl.program_id(0)
        g = dy_ref[...]
        _accumulate(i == 0, [dg2_ref], [jnp.sum(g * mix_ref[...], axis=0, keepdims=True)])
        dh = dp_ref[0].astype(f32)
        for j in range(1, nj):
            dh = dh + dp_ref[j].astype(f32)
        _, vjp_mod = jax.vjp(_f_modulate, x_ref[...], nw_ref[...], sh_ref[...], sc_ref[...])
        dx_mod, dnw, dsh, dsc = vjp_mod(dh)
        _accumulate(i == 0, [dnw_ref, dsh_ref, dsc_ref], [dnw, dsh, dsc])
        g = g + dx_mod
        dx_ref[...] = g
        o = o_ref[...]
        wo = wo_ref[...]
        _, vjp = jax.vjp(lambda o_, yg_, g1_, slot: _f_out(o_, yg_, g1_, wo, slot), o, yg_ref[...], g1_ref[...],
                         jnp.zeros(wo.shape, f32))
        do, dyg, dg1, dwo = vjp(g)
        delta = jnp.sum(do * o, axis=-1, keepdims=True)
        high = delta.astype(bf16)
        low = (delta - high.astype(f32)).astype(bf16)
        lane = lax.broadcasted_iota(jnp.int32, (1, 1, HEAD_LANES), 2)
        wide = jnp.concatenate([do.astype(bf16), jnp.zeros(do.shape, bf16)], axis=-1)
        do_ref[...] = jnp.where(lane == SPARE_V, -high, jnp.where(lane == SPARE_V + 1, -low, wide))
        dyg_ref[...] = dyg
        _accumulate(i == 0, [dg1_ref], [dg1])
        _accumulate_then_cast(i == 0, i == ni - 1, [acc_sc], [dwo_ref], [dwo])

    head = pl.BlockSpec((N_HEADS, ts, V_DIM), lambda i: (0, i, 0))
    tok = pl.BlockSpec((ts, D_MODEL), lambda i: (i, 0))
    vec = _const((1, D_MODEL))
    vshape = jax.ShapeDtypeStruct((1, D_MODEL), f32)
    return pl.pallas_call(
        body, name="out_bwd", grid=(ni,), scratch_shapes=[pltpu.VMEM((D_MODEL, D_MODEL), f32)],
        in_specs=[tok, pl.BlockSpec((nj, ts, D_MODEL), lambda i: (0, i, 0)), tok, vec, vec, vec, tok,
                  head, pl.BlockSpec((ts, D_SSD), lambda i: (i, 0)), vec, _const((D_MODEL, D_MODEL))],
        out_specs=[tok, vec, vec, vec, pl.BlockSpec((N_HEADS, ts, HEAD_LANES), lambda i: (0, i, 0)),
                   pl.BlockSpec((ts, D_SSD), lambda i: (i, 0)), vec, vec, _const((D_MODEL, D_MODEL))],
        out_shape=[jax.ShapeDtypeStruct((s, D_MODEL), f32), vshape, vshape, vshape,
                   jax.ShapeDtypeStruct((N_HEADS, s, HEAD_LANES), bf16), jax.ShapeDtypeStruct((s, D_SSD), f32),
                   vshape, vshape, jax.ShapeDtypeStruct((D_MODEL, D_MODEL), bf16)],
    )(dy, dhparts, x, nw, sh, sc, mix, o, yg, g1, wo)


def attn_bwd(qx, k, v, do, job=None):
    s = qx.shape[1]
    t = _token_block(s)
    nb = s // t

    hp = ATTN_HEADS_BWD

    def body(q_ref, k_ref, v_ref, do_ref, dq_ref, dk_ref, dv_ref, dv_sc):
        ki = pl.program_id(1)

        @pl.when(ki == 0)
        def _():
            dq_ref[...] = jnp.zeros(dq_ref.shape, f32)

        dk_ref[...] = jnp.zeros(dk_ref.shape, f32)
        dv_sc[...] = jnp.zeros(dv_sc.shape, f32)

        def step(q0, diagonal):
            half = t // 2
            subs = [(0, half, half), (half, half, t)] if diagonal and half % 128 == 0 else [(0, t, t)]
            chains = [(hh, sub) for hh in range(hp) for sub in subs]

            def products(hh, sub):
                r0, nr, nk = sub
                rows = pl.ds(q0 + r0, nr)
                sc = _scores(q_ref[hh, rows, :], k_ref[hh, :nk, :])
                dps = _scores(do_ref[hh, rows, :], v_ref[hh, :nk, :])
                return (jnp.where(_tril(nr, nk, r0), sc, NEG) if diagonal else sc), dps

            ahead = products(*chains[0])
            for c, (hh, (r0, nr, nk)) in enumerate(chains):
                sc, dps = ahead
                if c + 1 < len(chains):
                    ahead = products(*chains[c + 1])
                rows = pl.ds(q0 + r0, nr)
                p = jnp.exp2(sc)
                ds = (p * dps).astype(bf16)
                dv_sc[hh, :nk, :] += lax.dot_general(p.astype(bf16), do_ref[hh, rows, :], (((0,), (0,)), ((), ())),
                                                     preferred_element_type=f32)
                dk_ref[hh, :nk, :] += lax.dot_general(ds, q_ref[hh, rows, :], (((0,), (0,)), ((), ())),
                                                      preferred_element_type=f32)
                dq_ref[hh, rows, :] += jnp.dot(ds, k_ref[hh, :nk, :], preferred_element_type=f32)

        step(pl.multiple_of(ki * t, t), True)

        def above(qi, carry):
            step(pl.multiple_of(qi * t, t), False)
            return carry

        lax.fori_loop(ki + 1, nb, above, 0)
        real = lax.broadcasted_iota(jnp.int32, (1, 1, HEAD_LANES), 2) < SPARE_Q
        dk_ref[...] = jnp.where(real, dk_ref[...] * LN2, 0.0)
        dv_ref[...] = dv_sc[:, :, :V_DIM]

        @pl.when(ki == nb - 1)
        def _():
            dq_ref[...] = jnp.where(real, dq_ref[...] * LN2, 0.0)

    qspec = pl.BlockSpec((hp, s, HEAD_LANES), lambda h, ki: (h, 0, 0))
    kspec = lambda w: pl.BlockSpec((hp, t, w), lambda h, ki: (h, ki, 0))
    return _call_with_job(
        body, "attn_bwd" if job is None else "attn_bwd_comm", (N_HEADS // hp, nb), job,
        in_specs=[qspec, kspec(HEAD_LANES), kspec(HEAD_LANES), qspec],
        out_specs=[qspec, kspec(HEAD_LANES), kspec(V_DIM)],
        out_shape=[jax.ShapeDtypeStruct((N_HEADS, s, HEAD_LANES), f32), jax.ShapeDtypeStruct((N_HEADS, s, HEAD_LANES), f32),
                   jax.ShapeDtypeStruct((N_HEADS, s, V_DIM), f32)],
        scratch_shapes=[pltpu.VMEM((hp, t, HEAD_LANES), f32)], operands=(qx, k, v, do))


def ssd_bwd(px, pz, plast, states, dyg, params):
    s = px.shape[0]
    nc = s // CHUNK
    per = CHUNK // HALO

    def body(px_ref, halo_ref, pz_ref, pl_ref, st_ref, dyg_ref, cw_ref, cb_ref, dtb_ref, alog_ref, dskip_ref, snw_ref,
             dpx_ref, dpz_ref, dpl_ref, dcw_ref, dcb_ref, ddtb_ref, dalog_ref, ddskip_ref, dsnw_ref, dstate_sc, dhalo_sc):
        t = pl.program_id(0)
        chunk = nc - 1 - t

        @pl.when(t == 0)
        def _():
            dstate_sc[...] = jnp.zeros(dstate_sc.shape, f32)
            dhalo_sc[...] = jnp.zeros(dhalo_sc.shape, f32)

        halo = jnp.where(chunk > 0, halo_ref[...], 0.0)
        xext = jnp.concatenate([halo, px_ref[...]], axis=0)
        _, vjp = jax.vjp(_f_ssd, xext, pz_ref[...], pl_ref[...], st_ref[...], cw_ref[...], cb_ref[...], dtb_ref[...],
                         alog_ref[...], dskip_ref[...], snw_ref[...])
        dxext, dz, dpl, dprev, dcw, dcb, ddtb, dalog, ddskip, dsnw = vjp((dyg_ref[...], dstate_sc[...]))
        dpx_ref[...] = dxext[HALO:]
        dpx_ref[CHUNK - HALO:, :] += dhalo_sc[...]
        dhalo_sc[...] = dxext[:HALO]
        dstate_sc[...] = dprev
        dpz_ref[...] = dz
        dpl_ref[...] = dpl
        _accumulate(t == 0, [dcw_ref, dcb_ref, ddtb_ref, dalog_ref, ddskip_ref, dsnw_ref],
                    [dcw, dcb, ddtb, dalog, ddskip, dsnw])

    rev = lambda w: pl.BlockSpec((CHUNK, w), lambda t: (nc - 1 - t, 0))
    pshapes = [jax.ShapeDtypeStruct((4, D_CONV), f32), jax.ShapeDtypeStruct((1, D_CONV), f32),
               jax.ShapeDtypeStruct((1, 128), f32), jax.ShapeDtypeStruct((1, 128), f32),
               jax.ShapeDtypeStruct((1, 128), f32), jax.ShapeDtypeStruct((1, D_SSD), f32)]
    return pl.pallas_call(
        body, name="ssd_bwd", grid=(nc,),
        in_specs=[rev(D_CONV),
                  pl.BlockSpec((HALO, D_CONV), lambda t: (jnp.maximum((nc - 1 - t) * per - 1, 0), 0)),
                  rev(D_SSD), rev(128),
                  pl.BlockSpec((None, N_HEADS // 2, 2 * SSD_HEAD_DIM, SSD_STATE), lambda t: (nc - 1 - t, 0, 0, 0)),
                  rev(D_SSD)] + _ssd_param_specs(),
        out_specs=[rev(D_CONV), rev(D_SSD), rev(128)] + _ssd_param_specs(),
        out_shape=[jax.ShapeDtypeStruct((s, D_CONV), f32), jax.ShapeDtypeStruct((s, D_SSD), f32),
                   jax.ShapeDtypeStruct((s, 128), f32)] + pshapes,
        scratch_shapes=[pltpu.VMEM((N_HEADS // 2, 2 * SSD_HEAD_DIM, SSD_STATE), f32), pltpu.VMEM((HALO, D_CONV), f32)],
    )(px, px, pz, plast, states, dyg, *params)


def qkv_bwd(pa, plast, cos_t, sin_t, params, dq, dk, dv):
    s = pa.shape[0]
    ts = _token_block(s)

    def body(pa_ref, pl_ref, cos_ref, sin_ref, *rest):
        qaw, kvaw, wq, wk, wv, qnw, knw, kpw = [r[...] for r in rest[:8]]
        dq_ref, dk_ref, dv_ref = rest[8:11]
        dpa_ref, dpl_ref = rest[11:13]
        dprm_refs = list(rest[13:])
        cos_t, sin_t = cos_ref[...], sin_ref[...]

        def stage(pa_, pl_, qaw_, kvaw_, sq, sk, sv, qnw_, knw_, kpw_):
            return _f_qkv(pa_, pl_, cos_t, sin_t, qaw_, kvaw_, wq, wk, wv, qnw_, knw_, kpw_, (sq, sk, sv))

        _, vjp = jax.vjp(stage, pa_ref[...], pl_ref[...], qaw, kvaw, jnp.zeros(wq.shape, f32), jnp.zeros(wk.shape, f32),
                         jnp.zeros(wv.shape, f32), qnw, knw, kpw)
        grads = vjp((dq_ref[...], dk_ref[...], dv_ref[...]))
        dpa_ref[...] = grads[0]
        dpl_ref[...] = grads[1]
        _accumulate(pl.program_id(0) == 0, dprm_refs, list(grads[2:]))

    tok = lambda w: pl.BlockSpec((ts, w), lambda i: (i, 0))
    head = lambda w: pl.BlockSpec((N_HEADS, ts, w), lambda i: (0, i, 0))
    pshapes = [jax.ShapeDtypeStruct((1, Q_RANK), f32), jax.ShapeDtypeStruct((1, KV_RANK), f32),
               jax.ShapeDtypeStruct((N_HEADS, Q_RANK, HEAD_LANES), f32), jax.ShapeDtypeStruct((N_HEADS, KV_RANK, HEAD_LANES), f32),
               jax.ShapeDtypeStruct((N_HEADS, KV_RANK, V_DIM), f32), jax.ShapeDtypeStruct((1, HEAD_LANES), f32),
               jax.ShapeDtypeStruct((1, HEAD_LANES), f32), jax.ShapeDtypeStruct((1, HEAD_LANES), f32)]
    return pl.pallas_call(
        body, name="qkv_bwd", grid=(s // ts,),
        in_specs=[tok(384), tok(128), tok(128), tok(128)] + _qkv_param_specs()
                 + [head(HEAD_LANES), head(HEAD_LANES), head(V_DIM)],
        out_specs=[tok(384), tok(128)] + _qkv_param_specs(),
        out_shape=[jax.ShapeDtypeStruct((s, 384), f32), jax.ShapeDtypeStruct((s, 128), f32)] + pshapes,
    )(pa, plast, cos_t, sin_t, *params, dq, dk, dv)


def proj_bwd(x, nw, sh, sc, w, dpa, dpz, dpx, dpl_k, dpl_dt, dres):
    s = x.shape[0]
    ts = _token_block(s)

    ni = s // ts

    def body(x_ref, nw_ref, sh_ref, sc_ref, w_ref, dpa_ref, dpz_ref, dpx_ref, dplk_ref, dpld_ref, dres_ref,
             dx_ref, dnw_ref, dsh_ref, dsc_ref, dw_ref, acc_sc):
        i = pl.program_id(0)
        g = jnp.concatenate([dpa_ref[...], dpz_ref[...], dpx_ref[...], dplk_ref[...] + dpld_ref[...]], axis=1)
        w = w_ref[...]
        _, vjp = jax.vjp(lambda x_, nw_, sh_, sc_, slot: _f_proj(x_, nw_, sh_, sc_, w, slot), x_ref[...], nw_ref[...],
                         sh_ref[...], sc_ref[...], jnp.zeros(w.shape, f32))
        dx, dnw, dsh, dsc, dw = vjp(g)
        dx_ref[...] = dx + dres_ref[...]
        _accumulate(i == 0, [dnw_ref, dsh_ref, dsc_ref], [dnw, dsh, dsc])
        _accumulate_then_cast(i == 0, i == ni - 1, [acc_sc], [dw_ref], [dw])

    vec = _const((1, D_MODEL))
    vshape = jax.ShapeDtypeStruct((1, D_MODEL), f32)
    tok = lambda w_: pl.BlockSpec((ts, w_), lambda i: (i, 0))
    return pl.pallas_call(
        body, name="proj_bwd", grid=(ni,), scratch_shapes=[pltpu.VMEM((D_PROJ, D_MODEL), f32)],
        in_specs=[tok(D_MODEL), vec, vec, vec, _const((D_PROJ, D_MODEL)), tok(384), tok(512), tok(1024), tok(128), tok(128),
                  tok(D_MODEL)],
        out_specs=[tok(D_MODEL), vec, vec, vec, _const((D_PROJ, D_MODEL))],
        out_shape=[jax.ShapeDtypeStruct((s, D_MODEL), f32), vshape, vshape, vshape,
                   jax.ShapeDtypeStruct((D_PROJ, D_MODEL), bf16)],
    )(x, nw, sh, sc, w, dpa, dpz, dpx, dpl_k, dpl_dt, dres)


def ada_fwd(c_all, w_ada, b_cols):
    def body(c_ref, w_ref, b_ref, out_ref):
        act = jax.nn.silu(c_ref[...])
        for l in range(2):
            out_ref[l] = jnp.dot(act, w_ref[l], precision=lax.Precision.HIGHEST, preferred_element_type=f32) + b_ref[l]

    return pl.pallas_call(body, name="ada_fwd", out_shape=jax.ShapeDtypeStruct((2, N_DEV, 768), f32))(c_all, w_ada, b_cols)


def ada_bwd(c_all, dmod_cols):
    def body(c_ref, d_ref, out_ref):
        out_ref[0] = lax.dot_general(jax.nn.silu(c_ref[...]), d_ref[0], (((0,), (0,)), ((), ())),
                                     precision=lax.Precision.HIGHEST, preferred_element_type=f32)

    return pl.pallas_call(
        body, name="ada_bwd", grid=(2,),
        in_specs=[_const((N_DEV, D_MODEL)), pl.BlockSpec((1, N_DEV, 768), lambda l: (l, 0, 0))],
        out_specs=pl.BlockSpec((1, D_MODEL, 768), lambda l: (l, 0, 0)),
        out_shape=jax.ShapeDtypeStruct((2, D_MODEL, 768), f32),
    )(c_all, dmod_cols)


def _adamw(w, g, m, v):
    m = ADAM_B1 * m + (1.0 - ADAM_B1) * g
    v = ADAM_B2 * v + (1.0 - ADAM_B2) * (g * g)
    m_hat = m / (1.0 - ADAM_B1 ** ADAM_STEP)
    v_hat = v / (1.0 - ADAM_B2 ** ADAM_STEP)
    delta = -ADAM_LR * (m_hat / (jnp.sqrt(v_hat) + ADAM_EPS) + ADAM_WD * w)
    return delta, m, v


def adamw(parts, w, m, v, layer, prev, name):
    n, r, c = parts.shape
    nl = w.shape[0]
    per_elem = 2 * (n * parts.dtype.itemsize + 7 * 4)
    lanes = -(-c // 128) * 128
    tr, tc = r, c
    if per_elem * r * lanes > ADAMW_BLOCK_BYTES:
        fits = [t for t in range(r // 2, 15, -1) if r % t == 0 and t % 16 == 0 and per_elem * t * lanes <= ADAMW_BLOCK_BYTES]
        if fits:
            tr = fits[0]
        else:
            tc = next(t for t in (512, 256, 128) if c % t == 0)

    def body(p_ref, w_ref, m_ref, v_ref, *rest):
        g_ref, d_ref, nm_ref, nv_ref = rest[-4:]
        g = p_ref[0].astype(f32)
        for k in range(1, n):
            g = g + p_ref[k].astype(f32)
        delta, nm, nv = _adamw(w_ref[...], g, m_ref[...], v_ref[...])
        g_ref[...] = g
        d_ref[...] = delta
        nm_ref[...] = nm
        nv_ref[...] = nv

    blk = pl.BlockSpec((None, tr, tc), lambda i, j: (layer, i, j))
    shp = jax.ShapeDtypeStruct((nl, r, c), f32)
    kept = [] if prev is None else list(prev)
    return pl.pallas_call(
        body, name=name, grid=(r // tr, c // tc),
        in_specs=[pl.BlockSpec((n, tr, tc), lambda i, j: (0, i, j)), blk, blk, blk] + [ANY] * len(kept),
        out_specs=[blk] * 4, out_shape=[shp] * 4,
        input_output_aliases={4 + j: j for j in range(len(kept))},
    )(parts, w, m, v, *kept)


def adamw_two_layers(parts, w, m, v, name, job=None):
    n, r, c = parts[0].shape
    per_elem = 2 * (2 * n * parts[0].dtype.itemsize + 7 * 4)
    lanes = -(-c // 128) * 128
    tr = next(t for t in range(r, 15, -1) if r % t == 0 and t % 16 == 0 and per_elem * t * lanes <= ADAMW_BLOCK_BYTES)
    nblk = r // tr

    def body(p0_ref, p1_ref, w_ref, m_ref, v_ref, g_ref, d_ref, nm_ref, nv_ref):
        layer = pl.program_id(0)

        def update(p_ref):
            g = p_ref[0].astype(f32)
            for k in range(1, n):
                g = g + p_ref[k].astype(f32)
            delta, nm, nv = _adamw(w_ref[...], g, m_ref[...], v_ref[...])
            g_ref[...] = g
            d_ref[...] = delta
            nm_ref[...] = nm
            nv_ref[...] = nv

        @pl.when(layer == 0)
        def _():
            update(p0_ref)

        @pl.when(layer == 1)
        def _():
            update(p1_ref)

    blk = pl.BlockSpec((None, tr, c), lambda l, i: (l, i, 0))
    shp = jax.ShapeDtypeStruct((2, r, c), f32)
    return _call_with_job(
        body, name, (2, nblk), job,
        in_specs=[pl.BlockSpec((n, tr, c), lambda l, i: (0, i * (1 - l), 0)),
                  pl.BlockSpec((n, tr, c), lambda l, i: (0, i * l, 0)), blk, blk, blk],
        out_specs=[blk] * 4, out_shape=[shp] * 4, scratch_shapes=[], operands=(parts[0], parts[1], w, m, v),
        relay_at=(1, 0))


def adamw_layers_inside(parts, w, m, v, name):
    n, r, c = parts[0].shape
    nl = w.shape[1]
    tc = next(t for t in (256, 128) if c % t == 0)

    def body(*refs):
        p_refs = refs[:nl]
        w_ref, m_ref, v_ref, g_ref, d_ref, nm_ref, nv_ref = refs[nl:]
        for l in range(nl):
            g = p_refs[l][0].astype(f32)
            for k in range(1, n):
                g = g + p_refs[l][k].astype(f32)
            delta, nm, nv = _adamw(w_ref[:, l, :], g, m_ref[:, l, :], v_ref[:, l, :])
            g_ref[:, l, :] = g
            d_ref[:, l, :] = delta
            nm_ref[:, l, :] = nm
            nv_ref[:, l, :] = nv

    blk = pl.BlockSpec((r, nl, tc), lambda j: (0, 0, j))
    shp = jax.ShapeDtypeStruct((r, nl, c), f32)
    return pl.pallas_call(
        body, name=name, grid=(c // tc,),
        in_specs=[pl.BlockSpec((n, r, tc), lambda j: (0, 0, j))] * nl + [blk] * 3,
        out_specs=[blk] * 4, out_shape=[shp] * 4,
    )(*parts, w, m, v)


def _my_index():
    return 4 * lax.axis_index("x") + 2 * lax.axis_index("y") + lax.axis_index("c")


def _coords(idx):
    return (idx // 4, (idx // 2) % 2, idx % 2)


class CommJob:
    def __init__(self, operands, out_shape, phases, scratch):
        self.operands, self.out_shape, self.phases, self.scratch = operands, out_shape, phases, scratch


def _wait(out, n_blocks, send_sem, recv_sem, send=True, recv=True):
    span = out.at[pl.ds(0, n_blocks)]
    desc = pltpu.make_async_remote_copy(src_ref=span, dst_ref=span, send_sem=send_sem, recv_sem=recv_sem,
                                        device_id=_coords(_my_index()), device_id_type=MESH)
    if recv:
        desc.wait_recv()
    if send:
        desc.wait_send()


def gather_job(shards):
    n = len(shards)

    def places():
        x, y, c = lax.axis_index("x"), lax.axis_index("y"), lax.axis_index("c")
        return (x, y, c), (x, y, 1 - c), [(1 - x, y), (x, 1 - y), (1 - x, 1 - y)]

    def index(p):
        return 4 * p[0] + 2 * p[1] + p[2]

    def start(ins, outs, sems):
        far_send, far_recv, near_send, near_recv, local = sems
        me, sibling, chips = places()
        for k in range(n):
            pltpu.make_async_copy(ins[k], outs[k].at[index(me)], local.at[k]).start()
            for chip in chips:
                pltpu.make_async_remote_copy(src_ref=ins[k], dst_ref=outs[k].at[index(me)], send_sem=far_send.at[k],
                                             recv_sem=far_recv.at[k], device_id=(*chip, me[2]), device_id_type=MESH).start()
            pltpu.make_async_remote_copy(src_ref=ins[k], dst_ref=outs[k].at[index(me)], send_sem=near_send.at[k],
                                         recv_sem=near_recv.at[k], device_id=sibling, device_id_type=MESH).start()

    def relay(ins, outs, sems):
        far_send, far_recv, near_send, near_recv, local = sems
        me, sibling, chips = places()
        for k in range(n):
            _wait(outs[k], 3, far_send.at[k], far_recv.at[k], send=False)
            for chip in chips:
                block = outs[k].at[index((*chip, me[2]))]
                pltpu.make_async_remote_copy(src_ref=block, dst_ref=block, send_sem=near_send.at[k],
                                             recv_sem=near_recv.at[k], device_id=sibling, device_id_type=MESH).start()

    def finish(ins, outs, sems):
        far_send, far_recv, near_send, near_recv, local = sems
        for k in range(n):
            _wait(outs[k], 4, near_send.at[k], near_recv.at[k])
            _wait(outs[k], 3, far_send.at[k], far_recv.at[k], recv=False)
            pltpu.make_async_copy(ins[k], outs[k].at[0], local.at[k]).wait()

    shapes = [jax.ShapeDtypeStruct((N_DEV,) + tuple(a.shape), a.dtype) for a in shards]
    return CommJob(list(shards), shapes, [start, relay, finish], [pltpu.SemaphoreType.DMA((n,))] * 5)


def scatter_job(tensors):
    n = len(tensors)
    flat, where = [], {}
    for k, pieces in enumerate(tensors):
        d = 0
        for piece in pieces:
            for b in range(piece.shape[0]):
                where[k, d] = (len(flat), b)
                d += 1
            flat.append(piece)
        assert d == N_DEV

    def start(ins, outs, sems):
        send_sems, recv_sems, local_sems = sems
        me = _my_index()

        def block(k, d):
            i, b = where[k, d]
            return ins[i].at[b]

        for d in range(N_DEV):
            @pl.when(d != me)
            def _():
                for k in range(n):
                    pltpu.make_async_remote_copy(src_ref=block(k, d), dst_ref=outs[k].at[me], send_sem=send_sems.at[k],
                                                 recv_sem=recv_sems.at[k], device_id=(d // 4, (d // 2) % 2, d % 2),
                                                 device_id_type=MESH).start()

            @pl.when(d == me)
            def _():
                for k in range(n):
                    pltpu.make_async_copy(block(k, d), outs[k].at[d], local_sems.at[k]).start()

    def finish(ins, outs, sems):
        send_sems, recv_sems, local_sems = sems
        for k in range(n):
            _wait(outs[k], N_DEV - 1, send_sems.at[k], recv_sems.at[k])
            i, b = where[k, 0]
            pltpu.make_async_copy(ins[i].at[b], outs[k].at[0], local_sems.at[k]).wait()

    shapes = [jax.ShapeDtypeStruct((N_DEV,) + tuple(p[0].shape[1:]), p[0].dtype) for p in tensors]
    return CommJob(flat, shapes, [start, finish], [pltpu.SemaphoreType.DMA((n,))] * 3)


def merge_jobs(a, b):
    def on(job, off):
        oi, oo, os_ = off
        ni, no, ns = len(job.operands), len(job.out_shape), len(job.scratch)
        return lambda phase: (lambda ins, outs, sems: phase(ins[oi:oi + ni], outs[oo:oo + no], sems[os_:os_ + ns]))

    wrap_a = on(a, (0, 0, 0))
    wrap_b = on(b, (len(a.operands), len(a.out_shape), len(a.scratch)))
    pa, pb = [wrap_a(p) for p in a.phases], [wrap_b(p) for p in b.phases]

    def together(*phases):
        def run(ins, outs, sems):
            for p in phases:
                p(ins, outs, sems)
        return run

    middle = pa[1:-1] + pb[1:-1]
    phases = [together(pa[0], pb[0])] + middle + [together(pa[-1], pb[-1])]
    return CommJob(a.operands + b.operands, a.out_shape + b.out_shape, phases, a.scratch + b.scratch)


def comm_call(job, name):
    ni, no = len(job.operands), len(job.out_shape)

    def body(*refs):
        ins, outs, sems = refs[:ni], refs[ni:ni + no], refs[ni + no:]
        for phase in job.phases:
            phase(ins, outs, sems)

    return pl.pallas_call(body, name=name, in_specs=[ANY] * ni, out_specs=[ANY] * no, out_shape=job.out_shape,
                          scratch_shapes=job.scratch)(*job.operands)


def _carry(job, body, n_in, n_out, at_step):
    ji, jo, js = len(job.operands), len(job.out_shape), len(job.scratch)

    def carrier(*refs):
        a, b = n_in, n_in + ji
        c, d = b + n_out, b + n_out + jo
        e = len(refs) - js
        job_refs = (refs[a:b], refs[c:d], refs[e:])
        n = len(job.phases)

        @pl.when(at_step(0, n))
        def _():
            job.phases[0](*job_refs)

        body(*refs[:a], *refs[b:c], *refs[d:e])

        for i in range(1, n):
            @pl.when(at_step(i, n))
            def _():
                job.phases[i](*job_refs)

    return carrier


def _pad_lanes(v, lo, total=128):
    return jnp.pad(v, (lo, total - lo - v.shape[0]))[None, :]


MIXER_WEIGHTS = ("w_in", "w_q_up", "w_kv_up", "conv_w")
LATE_WEIGHTS = ("w_out", "w_gate_up", "w_down")


def mixer_operands(g, sw):
    w_in = g["w_in"].reshape(D_IN, D_MODEL)
    zero = lambda rows: jnp.zeros((rows, D_MODEL), w_in.dtype)
    w_proj = jnp.concatenate(
        [w_in[:384], w_in[416:928], w_in[928:1952], w_in[1952:1960], zero(56), w_in[384:416], zero(32)], axis=0)
    wq = jnp.pad(g["w_q_up"], ((0, 0), (0, 0), (0, HEAD_LANES - NOPE - ROPE)))
    wk = jnp.pad(g["w_kv_up"][:, :, :NOPE], ((0, 0), (0, 0), (0, HEAD_LANES - NOPE)))
    wv = g["w_kv_up"][:, :, NOPE:]
    qkv = (sw["q_a_norm_w"][None, :], sw["kv_a_norm_w"][None, :], wq, wk, wv,
           _pad_lanes(jnp.concatenate([sw["q_nope_norm_w"], sw["q_pe_norm_w"]]), 0),
           _pad_lanes(sw["k_nope_norm_w"], 0), _pad_lanes(sw["k_pe_norm_w"], NOPE))
    conv_w = g["conv_w"].astype(f32).transpose(1, 0, 2).reshape(4, D_CONV)
    ssd = (conv_w, sw["conv_b"][None, :], _pad_lanes(sw["dt_bias"], 0), _pad_lanes(sw["a_log"], 0),
           _pad_lanes(sw["d_skip"], 0), sw["ssd_norm_w"][None, :])
    return dict(w_proj=w_proj, qkv=qkv, ssd=ssd, n1=sw["norm1_w"][None, :])


def late_operands(g, sw):
    return dict(wo=g["w_out"].reshape(D_MODEL, D_MODEL), wgu=g["w_gate_up"],
                wd=g["w_down"].reshape(N_DEV // 2, FF_SHARD, D_MODEL), n2=sw["norm2_w"][None, :])


def layer_fwd(x, mod, kw, cos_t, sin_t, job=None, late=None, target=None, mixer_job=None):
    sh1, sc1, g1, sh2, sc2, g2 = [mod[i:i + 1] for i in range(6)]
    pa, pz, px, plast = proj_fwd(x, kw["n1"], sh1, sc1, kw["w_proj"])
    q, k, v = qkv_fwd(pa, plast, cos_t, sin_t, kw["qkv"])
    (o, qx), carried = attn_fwd(q, k, v, job)
    if late is not None:
        kw = {**kw, **late(carried)}
    yg, states = ssd_fwd(px, pz, plast, kw["ssd"])
    x_mid = out_fwd(x, o, yg, g1, kw["wo"])
    (x_out, mix, h_mid, gate, up, *loss_part), carried_mixer = mlp_fwd(
        x_mid, kw["n2"], sh2, sc2, g2, kw["wgu"], kw["wd"], target, mixer_job)
    if mixer_job is not None:
        carried = (carried, carried_mixer)
    saved = dict(x=x, pa=pa, pz=pz, px=px, plast=plast, qx=qx, k=k, v=v, o=o, yg=yg, states=states, x_mid=x_mid,
                 mix=mix, h_mid=h_mid, gate=gate, up=up)
    return (x_out if target is None else (x_out, loss_part[0])), saved, kw, carried


def layer_bwd_head(dy, mod, kw, sv, job=None):
    _, _, g1, sh2, sc2, g2 = [mod[i:i + 1] for i in range(6)]
    (dhparts, dwg, dwu, dwd), carried = mlp_bwd(sv["h_mid"], dy, sv["gate"], sv["up"], g2, kw["wgu"], kw["wd"], job)
    dmid, dn2, dsh2, dsc2, do, dyg, dg1, dg2, dwo = out_bwd(
        dy, dhparts, sv["x_mid"], kw["n2"], sh2, sc2, sv["mix"], sv["o"], sv["yg"], g1, kw["wo"])
    early = dict(w_out=[dwo.reshape(N_DEV, D_MODEL // N_DEV, D_MODEL)], w_gate_up=[dwg, dwu],
                 w_down=[dwd.reshape(N_DEV, D_FF // N_DEV, D_MODEL)])
    head = dict(dmid=dmid, do=do, dyg=dyg, dn2=dn2, dsh2=dsh2, dsc2=dsc2, dg2=dg2, dg1=dg1)
    return head, early, carried


def layer_bwd_tail(hd, mod, kw, cos_t, sin_t, sv, job=None):
    sh1, sc1 = mod[0:1], mod[1:2]
    (dq, dk, dv), carried = attn_bwd(sv["qx"], sv["k"], sv["v"], hd["do"], job)
    dpx, dpz, dpl_dt, dcw, dcb, ddtb, dalog, ddskip, dsnw = ssd_bwd(sv["px"], sv["pz"], sv["plast"], sv["states"],
                                                                   hd["dyg"], kw["ssd"])
    dpa, dpl_k, dqaw, dkvaw, dwq, dwk, dwv, dqnw, dknw, dkpw = qkv_bwd(sv["pa"], sv["plast"], cos_t, sin_t, kw["qkv"],
                                                                       dq, dk, dv)
    dx, dn1, dsh1, dsc1, dwp = proj_bwd(sv["x"], kw["n1"], sh1, sc1, kw["w_proj"], dpa, dpz, dpx, dpl_k, dpl_dt, hd["dmid"])
    dmod = jnp.concatenate([dsh1, dsc1, hd["dg1"], hd["dsh2"], hd["dsc2"], hd["dg2"]], axis=0)
    dw_in = jnp.concatenate([dwp[:384], dwp[1984:2016], dwp[384:1920], dwp[1920:1928]], axis=0)
    grads = dict(
        norm1_w=dn1[0], norm2_w=hd["dn2"][0], q_a_norm_w=dqaw[0], kv_a_norm_w=dkvaw[0],
        q_nope_norm_w=dqnw[0, :NOPE], q_pe_norm_w=dqnw[0, NOPE:NOPE + ROPE], k_nope_norm_w=dknw[0, :NOPE],
        k_pe_norm_w=dkpw[0, NOPE:NOPE + ROPE], conv_b=dcb[0], dt_bias=ddtb[0, :N_HEADS], a_log=dalog[0, :N_HEADS],
        d_skip=ddskip[0, :N_HEADS], ssd_norm_w=dsnw[0],
        w_in=[dw_in.reshape(N_DEV, D_IN // N_DEV, D_MODEL)],
        w_q_up=[dwq[:, :, :NOPE + ROPE].astype(bf16)],
        w_kv_up=[jnp.concatenate([dwk[:, :, :NOPE], dwv], axis=2).astype(bf16)],
        conv_w=[dcw.reshape(4, N_DEV, D_CONV // N_DEV).transpose(1, 0, 2).astype(bf16)],
    )
    return dx, dmod, grads, carried


def _pack_small(get, last=None):
    flat = jnp.concatenate([get(name).reshape(-1) for name, _ in SMALL])
    flat = jnp.pad(flat, (0, SMALL_ROWS * 128 - flat.shape[0]))
    if last is not None:
        flat = flat.at[-1].set(last)
    return flat.reshape(SMALL_ROWS, 128)


def _unpack_small(packed):
    flat = packed.reshape(-1)
    out, off = {}, 0
    for name, size in SMALL:
        out[name] = flat[off:off + 2 * size].reshape(2, size)
        off += 2 * size
    return out


def kernel(x, c, positions, norm1_w, norm2_w, w_ada, b_ada, w_in, q_a_norm_w, w_q_up, kv_a_norm_w, w_kv_up, q_nope_norm_w, q_pe_norm_w, k_nope_norm_w, k_pe_norm_w, conv_w, conv_b, dt_bias, a_log, d_skip, ssd_norm_w, w_out, w_gate_up, w_down, loss_target, m_norm1_w, m_norm2_w, m_w_ada, m_b_ada, m_w_in, m_q_a_norm_w, m_w_q_up, m_kv_a_norm_w, m_w_kv_up, m_q_nope_norm_w, m_q_pe_norm_w, m_k_nope_norm_w, m_k_pe_norm_w, m_conv_w, m_conv_b, m_dt_bias, m_a_log, m_d_skip, m_ssd_norm_w, m_w_out, m_w_gate_up, m_w_down, v_norm1_w, v_norm2_w, v_w_ada, v_b_ada, v_w_in, v_q_a_norm_w, v_w_q_up, v_kv_a_norm_w, v_w_kv_up, v_q_nope_norm_w, v_q_pe_norm_w, v_k_nope_norm_w, v_k_pe_norm_w, v_conv_w, v_conv_b, v_dt_bias, v_a_log, v_d_skip, v_ssd_norm_w, v_w_out, v_w_gate_up, v_w_down):
    w = dict(norm1_w=norm1_w, norm2_w=norm2_w, w_ada=w_ada, b_ada=b_ada, w_in=w_in, q_a_norm_w=q_a_norm_w, w_q_up=w_q_up,
             kv_a_norm_w=kv_a_norm_w, w_kv_up=w_kv_up, q_nope_norm_w=q_nope_norm_w, q_pe_norm_w=q_pe_norm_w,
             k_nope_norm_w=k_nope_norm_w, k_pe_norm_w=k_pe_norm_w, conv_w=conv_w, conv_b=conv_b, dt_bias=dt_bias,
             a_log=a_log, d_skip=d_skip, ssd_norm_w=ssd_norm_w, w_out=w_out, w_gate_up=w_gate_up, w_down=w_down)
    m = dict(norm1_w=m_norm1_w, norm2_w=m_norm2_w, w_ada=m_w_ada, b_ada=m_b_ada, w_in=m_w_in, q_a_norm_w=m_q_a_norm_w,
             w_q_up=m_w_q_up, kv_a_norm_w=m_kv_a_norm_w, w_kv_up=m_w_kv_up, q_nope_norm_w=m_q_nope_norm_w,
             q_pe_norm_w=m_q_pe_norm_w, k_nope_norm_w=m_k_nope_norm_w, k_pe_norm_w=m_k_pe_norm_w, conv_w=m_conv_w,
             conv_b=m_conv_b, dt_bias=m_dt_bias, a_log=m_a_log, d_skip=m_d_skip, ssd_norm_w=m_ssd_norm_w, w_out=m_w_out,
             w_gate_up=m_w_gate_up, w_down=m_w_down)
    v = dict(norm1_w=v_norm1_w, norm2_w=v_norm2_w, w_ada=v_w_ada, b_ada=v_b_ada, w_in=v_w_in, q_a_norm_w=v_q_a_norm_w,
             w_q_up=v_w_q_up, kv_a_norm_w=v_kv_a_norm_w, w_kv_up=v_w_kv_up, q_nope_norm_w=v_q_nope_norm_w,
             q_pe_norm_w=v_q_pe_norm_w, k_nope_norm_w=v_k_nope_norm_w, k_pe_norm_w=v_k_pe_norm_w, conv_w=v_conv_w,
             conv_b=v_conv_b, dt_bias=v_dt_bias, a_log=v_a_log, d_skip=v_d_skip, ssd_norm_w=v_ssd_norm_w, w_out=v_w_out,
             w_gate_up=v_w_gate_up, w_down=v_w_down)
    me = _my_index()
    seq = x.shape[1]

    def shard(name, l):
        if name == "conv_w":
            return w[name][l]
        if name in TRANSPOSED:
            return jnp.swapaxes(w[name][l], 0, 1).astype(bf16)
        return w[name][l].astype(bf16)

    def shards(names, l):
        return [shard(name, l) for name in names]

    small = [{name: w[name][l] for name, _ in SMALL if name != "b_ada"} for l in range(2)]

    inv_freq = 1.0 / (ROPE_THETA ** (jnp.arange(0, ROPE, 2, dtype=f32) / ROPE))
    inv = _pad_lanes(jnp.concatenate([inv_freq, inv_freq]), NOPE)
    (cos_t, sin_t), first = rope_tables(positions.reshape(seq, 1), inv, gather_job([c] + shards(MIXER_WEIGHTS, 0)))
    c_all = first[0].reshape(N_DEV, D_MODEL)
    kws = [mixer_operands(dict(zip(MIXER_WEIGHTS, first[1:])), small[0]), None]

    b_cols = lax.dynamic_slice_in_dim(b_ada, me * 768, 768, axis=1)
    mod_cols = ada_fwd(c_all, w_ada, b_cols)
    (mod_all,) = comm_call(gather_job([mod_cols]), "gather_mod")
    mod_me = lax.dynamic_index_in_dim(mod_all, me, axis=2, keepdims=False)
    mods = [mod_me[:, l, :].reshape(6, D_MODEL) for l in range(2)]

    saved = [None, None]
    h, saved[0], kws[0], (_, got) = layer_fwd(
        x[0], mods[0], kws[0], cos_t, sin_t, gather_job(shards(LATE_WEIGHTS, 0)),
        lambda got: late_operands(dict(zip(LATE_WEIGHTS, got)), small[0]),
        mixer_job=gather_job(shards(MIXER_WEIGHTS, 1)))
    kws[1] = mixer_operands(dict(zip(MIXER_WEIGHTS, got)), small[1])
    (dy, loss_part), saved[1], kws[1], _ = layer_fwd(
        h, mods[1], kws[1], cos_t, sin_t, gather_job(shards(LATE_WEIGHTS, 1)),
        lambda got: late_operands(dict(zip(LATE_WEIGHTS, got)), small[1]), loss_target[0])

    early, late = ("w_out", "w_gate_up", "w_down"), ("w_in", "w_q_up", "w_kv_up", "conv_w")
    parts = [{}, {}]
    head, pieces, _ = layer_bwd_head(dy, mods[1], kws[1], saved[1])
    dy, dmod1, grads1, got = layer_bwd_tail(head, mods[1], kws[1], cos_t, sin_t, saved[1], scatter_job([pieces[n] for n in early]))
    parts[1].update(zip(early, got))
    head, pieces, got = layer_bwd_head(dy, mods[0], kws[0], saved[0], scatter_job([grads1[n] for n in late]))
    parts[1].update(zip(late, got))
    dy, dmod0, grads0, got = layer_bwd_tail(head, mods[0], kws[0], cos_t, sin_t, saved[0], scatter_job([pieces[n] for n in early]))
    parts[0].update(zip(early, got))
    grad_x = dy[None]

    small_part = {name: jnp.stack([grads0[name], grads1[name]]) for name, _ in SMALL if name != "b_ada"}
    small_part["b_ada"] = jnp.stack([dmod0.reshape(-1), dmod1.reshape(-1)])
    swap = lambda a: jnp.swapaxes(a, 1, 2)
    updated, last = adamw_two_layers(
        [parts[l]["w_gate_up"] for l in range(2)], swap(w_gate_up), swap(m_w_gate_up), swap(v_w_gate_up), "adamw_w_gate_up_comm",
        merge_jobs(scatter_job([grads0[n] for n in late]),
                   gather_job([_pack_small(lambda n: small_part[n], loss_part[0, 0])])))
    parts[0].update(zip(late, last[:len(late)]))
    small_all = last[len(late)]
    packed = adamw(small_all, _pack_small(lambda n: w[n])[None], _pack_small(lambda n: m[n])[None],
                   _pack_small(lambda n: v[n])[None], 0, None, "adamw_small")
    loss = packed[0][0, -1, -1]
    res = {}
    for key, arr in zip("gdmv", packed):
        for name, val in _unpack_small(arr[0]).items():
            res[key, name] = val

    off = 2 * (1024 + 1024)
    dmod_all = small_all.reshape(N_DEV, -1)[:, off:off + 2 * 6144].reshape(N_DEV, 2, 6144)
    dmod_cols = lax.dynamic_slice_in_dim(dmod_all, me * 768, 768, axis=2).transpose(1, 0, 2)
    g_ada = ada_bwd(c_all, dmod_cols)
    out = None
    for l in range(2):
        out = adamw(g_ada[l][None], w_ada, m_w_ada, v_w_ada, l, out, "adamw_w_ada")
    res.update(zip([(key, "w_ada") for key in "gdmv"], out))

    inside = lambda a: jnp.transpose(a, (2, 0, 1))
    out = adamw_layers_inside([parts[l]["w_in"] for l in range(2)], inside(w_in), inside(m_w_in), inside(v_w_in), "adamw_w_in")
    res.update(zip([(key, "w_in") for key in "gdmv"], [jnp.transpose(a, (1, 2, 0)) for a in out]))
    res.update(zip([(key, "w_gate_up") for key in "gdmv"], [swap(a) for a in updated]))
    for name in BIG:
        if name in ("w_in", "w_gate_up"):
            continue
        view = (lambda a: jnp.swapaxes(a, 1, 2)) if name in TRANSPOSED else (lambda a: a)
        out = None
        for l in range(2):
            out = adamw(parts[l][name], view(w[name]), view(m[name]), view(v[name]), l, out, "adamw_" + name)
        res.update(zip([(key, name) for key in "gdmv"], [view(a) for a in out]))

    return (loss, grad_x, *[res["g", n] for n in WEIGHTS], *[res["d", n] for n in WEIGHTS],
            *[res["m", n] for n in WEIGHTS], *[res["v", n] for n in WEIGHTS])
```
